```python
import jax, jax.numpy as jnp
from jax import lax
import numpy as np

D_MODEL = 1024
BATCH = 8
SEQ = 2048
DEPTH = 1

ATTN_HEADS = 8
HEAD_DIM = 64
D_ATTN = ATTN_HEADS * HEAD_DIM
MOBA_BLOCK = 256
MOBA_TOPK = 3
Q_CHUNK = 32
ROPE_THETA = 10000.0
POOL_WINDOWS = (2, 4, 8, 16)
N_POOL_GROUPS = len(POOL_WINDOWS)
POOL_GROUP_DIM = 128
D_POOL = N_POOL_GROUPS * POOL_GROUP_DIM
N_BRANCHES = 2
D_IN_PROJ = 3 * D_ATTN + D_POOL + N_BRANCHES * D_MODEL
D_FF = 2816
CONV_WIDTH = 3
LN_EPS = 1e-5
DEEPNORM_ALPHA = (2.0 * DEPTH) ** 0.25
DEEPNORM_BETA = (8.0 * DEPTH) ** -0.25
NEG = -1e30

kernel_name = "hybrid_moba_pool_convffn_deepnorm"


def layer_norm(x, g, b):
    xf = x.astype(jnp.float32)
    mu = jnp.mean(xf, axis=-1, keepdims=True)
    var = jnp.mean(jnp.square(xf - mu), axis=-1, keepdims=True)
    y = (xf - mu) * lax.rsqrt(var + LN_EPS) * g.astype(jnp.float32) + b.astype(jnp.float32)
    return y.astype(x.dtype)


def rope_tables(s):
    half = HEAD_DIM // 2
    inv_freq = 1.0 / (ROPE_THETA ** (jnp.arange(half, dtype=jnp.float32) / half))
    ang = jnp.arange(s, dtype=jnp.float32)[:, None] * inv_freq[None, :]
    return jnp.cos(ang), jnp.sin(ang)


def apply_rope(t, cos, sin):
    tf = t.astype(jnp.float32)
    half = HEAD_DIM // 2
    t1, t2 = tf[..., :half], tf[..., half:]
    return jnp.concatenate([t1 * cos - t2 * sin, t2 * cos + t1 * sin], axis=-1).astype(t.dtype)


def moba_attention(q, k, v):
    b, h, s, d = q.shape
    nb = -(-s // MOBA_BLOCK)
    s_pad = nb * MOBA_BLOCK
    pad = ((0, 0), (0, 0), (0, s_pad - s), (0, 0))
    q, k, v = jnp.pad(q, pad), jnp.pad(k, pad), jnp.pad(v, pad)
    k_blocks = k.reshape(b, h, nb, MOBA_BLOCK, d)
    v_blocks = v.reshape(b, h, nb, MOBA_BLOCK, d)
    k_mean = jnp.mean(k_blocks.astype(jnp.float32), axis=3)
    q_block = jnp.arange(s_pad) // MOBA_BLOCK
    gate = jnp.einsum('bhsd,bhnd->bhsn', q.astype(jnp.float32), k_mean)
    past = jnp.arange(nb)[None, :] < q_block[:, None]
    gate = jnp.where(past, gate, NEG)
    n_sel = min(MOBA_TOPK, nb)
    _, sel = lax.top_k(gate, n_sel)
    sel_valid = sel < q_block[:, None]

    nc = s_pad // Q_CHUNK

    def to_chunks(t):
        return jnp.moveaxis(t.reshape(b, h, nc, Q_CHUNK, *t.shape[3:]), 2, 0)

    gather = jax.vmap(jax.vmap(lambda blocks, idx: blocks[idx]))
    scale = HEAD_DIM ** -0.5
    key_off = jnp.arange(MOBA_BLOCK)

    def one_chunk(args):
        c, qc, sc, vc = args
        qpos = c * Q_CHUNK + jnp.arange(Q_CHUNK)
        own = (c * Q_CHUNK) // MOBA_BLOCK
        k_own = lax.dynamic_index_in_dim(k_blocks, own, axis=2, keepdims=False).astype(jnp.float32)
        v_own = lax.dynamic_index_in_dim(v_blocks, own, axis=2, keepdims=False).astype(jnp.float32)
        k_sel = gather(k_blocks, sc).astype(jnp.float32)
        v_sel = gather(v_blocks, sc).astype(jnp.float32)
        qf = qc.astype(jnp.float32) * scale
        s_sel = jnp.einsum('bhqd,bhqnkd->bhqnk', qf, k_sel)
        s_sel = jnp.where(vc[..., None], s_sel, NEG).reshape(b, h, Q_CHUNK, n_sel * MOBA_BLOCK)
        s_own = jnp.einsum('bhqd,bhkd->bhqk', qf, k_own)
        causal = (own * MOBA_BLOCK + key_off)[None, :] <= qpos[:, None]
        s_own = jnp.where(causal, s_own, NEG)
        p = jax.nn.softmax(jnp.concatenate([s_sel, s_own], axis=-1), axis=-1)
        p_sel = p[..., :n_sel * MOBA_BLOCK].reshape(b, h, Q_CHUNK, n_sel, MOBA_BLOCK)
        p_own = p[..., n_sel * MOBA_BLOCK:]
        o = (jnp.einsum('bhqnk,bhqnkd->bhqd', p_sel, v_sel)
             + jnp.einsum('bhqk,bhkd->bhqd', p_own, v_own))
        return o.astype(q.dtype)

    out = lax.map(one_chunk, (jnp.arange(nc), to_chunks(q), to_chunks(sel), to_chunks(sel_valid)))
    out = jnp.moveaxis(out, 0, 2).reshape(b, h, s_pad, d)
    return out[:, :, :s]


def multiscale_pool(u, w_pool, pool_scale):
    b, s, _ = u.shape
    uf = u.astype(jnp.float32)
    cs = jnp.pad(lax.cumsum(uf, axis=1), ((0, 0), (1, 0), (0, 0)))
    t = jnp.arange(s)
    outs = []
    for g, w in enumerate(POOL_WINDOWS):
        sl = slice(g * POOL_GROUP_DIM, (g + 1) * POOL_GROUP_DIM)
        start = jnp.maximum(t + 1 - w, 0)
        count = (t + 1 - start).astype(jnp.float32)
        win_sum = cs[:, 1:, sl] - cs[:, start, sl]
        outs.append(win_sum / count[None, :, None] - uf[:, :, sl])
    pooled = jnp.stack(outs, axis=2)
    mixed = jnp.einsum('bsgc,gcd->bsgd', pooled, w_pool.astype(jnp.float32)).reshape(b, s, D_POOL)
    return (mixed * pool_scale.astype(jnp.float32)).astype(u.dtype)


def conv_ffn(x, w_ffn_gate, w_ffn_up, conv_w, conv_b, w_ffn_down):
    s = x.shape[1]
    a = x @ w_ffn_gate
    u = x @ w_ffn_up
    ap = jnp.pad(a, ((0, 0), (CONV_WIDTH - 1, 0), (0, 0)))
    a = sum(ap[:, i:i + s] * conv_w[i] for i in range(CONV_WIDTH)) + conv_b
    return (jax.nn.gelu(a, approximate=False) * u) @ w_ffn_down


def _fwd_setup_inputs(seed: int = 0) -> dict:
    key = jax.random.key(seed)
    ks = jax.random.split(key, 20)
    L, D = DEPTH, D_MODEL
    f32 = jnp.float32

    def nrm(k, shape, fan_in, gain=1.0):
        return (jax.random.normal(k, shape, f32) * (gain * fan_in ** -0.5)).astype(f32)

    return {
        "x": jax.random.normal(ks[0], (BATCH, SEQ, D), f32),
        "w_in": nrm(ks[1], (L, D, D_IN_PROJ), D),
        "b_gate": 0.02 * jax.random.normal(ks[2], (L, N_BRANCHES * D), f32),
        "w_branch_attn": nrm(ks[3], (L, D_ATTN, D), D_ATTN),
        "w_pool": nrm(ks[4], (L, N_POOL_GROUPS, POOL_GROUP_DIM, POOL_GROUP_DIM), POOL_GROUP_DIM),
        "pool_scale": 1.0 + 0.05 * jax.random.normal(ks[5], (L, D_POOL), f32),
        "w_branch_pool": nrm(ks[6], (L, D_POOL, D), D_POOL),
        "w_out": nrm(ks[7], (L, D, D), D, DEEPNORM_BETA),
        "ln1_g": 1.0 + 0.05 * jax.random.normal(ks[8], (L, D), f32),
        "ln1_b": 0.02 * jax.random.normal(ks[9], (L, D), f32),
        "w_ffn_gate": nrm(ks[10], (L, D, D_FF), D),
        "w_ffn_up": nrm(ks[11], (L, D, D_FF), D),
        "conv_w": nrm(ks[12], (L, CONV_WIDTH, D_FF), CONV_WIDTH),
        "conv_b": 0.02 * jax.random.normal(ks[13], (L, D_FF), f32),
        "w_ffn_down": nrm(ks[14], (L, D_FF, D), D_FF, DEEPNORM_BETA),
        "ln2_g": 1.0 + 0.05 * jax.random.normal(ks[15], (L, D), f32),
        "ln2_b": 0.02 * jax.random.normal(ks[16], (L, D), f32),
    }


def _fwd_reference(x, w_in, b_gate, w_branch_attn, w_pool, pool_scale, w_branch_pool, w_out,
              ln1_g, ln1_b, w_ffn_gate, w_ffn_up, conv_w, conv_b, w_ffn_down, ln2_g, ln2_b):
    b, s, _ = x.shape
    cos, sin = rope_tables(s)
    for l in range(DEPTH):
        z = x @ w_in[l]
        o0, o1, o2, o3 = D_ATTN, 2 * D_ATTN, 3 * D_ATTN, 3 * D_ATTN + D_POOL
        heads = lambda t: t.reshape(b, s, ATTN_HEADS, HEAD_DIM).transpose(0, 2, 1, 3)
        q = apply_rope(heads(z[..., :o0]), cos, sin)
        k = apply_rope(heads(z[..., o0:o1]), cos, sin)
        v = heads(z[..., o1:o2])
        u_pool = z[..., o2:o3]
        gates = jax.nn.sigmoid(z[..., o3:] + b_gate[l])
        g_attn, g_pool = gates[..., :D_MODEL], gates[..., D_MODEL:]
        y_attn = moba_attention(q, k, v).transpose(0, 2, 1, 3).reshape(b, s, D_ATTN)
        y_attn = y_attn @ w_branch_attn[l]
        y_pool = multiscale_pool(u_pool, w_pool[l], pool_scale[l]) @ w_branch_pool[l]
        mix = (g_attn * y_attn + g_pool * y_pool) @ w_out[l]
        x = layer_norm(DEEPNORM_ALPHA * x + mix, ln1_g[l], ln1_b[l])
        ffn = conv_ffn(x, w_ffn_gate[l], w_ffn_up[l], conv_w[l], conv_b[l], w_ffn_down[l])
        x = layer_norm(DEEPNORM_ALPHA * x + ffn, ln2_g[l], ln2_b[l])
    return x


import jax as _jax
import jax.numpy as _jnp

TWIN_FORMAT = 'train_step'
FWD_PARAMS = ['x', 'w_in', 'b_gate', 'w_branch_attn', 'w_pool', 'pool_scale', 'w_branch_pool', 'w_out', 'ln1_g', 'ln1_b', 'w_ffn_gate', 'w_ffn_up', 'conv_w', 'conv_b', 'w_ffn_down', 'ln2_g', 'ln2_b']
TWIN_WEIGHTS = ['w_in', 'b_gate', 'w_branch_attn', 'w_pool', 'pool_scale', 'w_branch_pool', 'w_out', 'ln1_g', 'ln1_b', 'w_ffn_gate', 'w_ffn_up', 'conv_w', 'conv_b', 'w_ffn_down', 'ln2_g', 'ln2_b']
TWIN_DIFF_INPUT = 'x'
TWIN_INPUTS = ['x', 'w_in', 'b_gate', 'w_branch_attn', 'w_pool', 'pool_scale', 'w_branch_pool', 'w_out', 'ln1_g', 'ln1_b', 'w_ffn_gate', 'w_ffn_up', 'conv_w', 'conv_b', 'w_ffn_down', 'ln2_g', 'ln2_b', 'loss_target', 'm_w_in', 'm_b_gate', 'm_w_branch_attn', 'm_w_pool', 'm_pool_scale', 'm_w_branch_pool', 'm_w_out', 'm_ln1_g', 'm_ln1_b', 'm_w_ffn_gate', 'm_w_ffn_up', 'm_conv_w', 'm_conv_b', 'm_w_ffn_down', 'm_ln2_g', 'm_ln2_b', 'v_w_in', 'v_b_gate', 'v_w_branch_attn', 'v_w_pool', 'v_pool_scale', 'v_w_branch_pool', 'v_w_out', 'v_ln1_g', 'v_ln1_b', 'v_w_ffn_gate', 'v_w_ffn_up', 'v_conv_w', 'v_conv_b', 'v_w_ffn_down', 'v_ln2_g', 'v_ln2_b']
TWIN_OUTPUTS = ['loss', 'grad_x', 'grad_w_in', 'grad_b_gate', 'grad_w_branch_attn', 'grad_w_pool', 'grad_pool_scale', 'grad_w_branch_pool', 'grad_w_out', 'grad_ln1_g', 'grad_ln1_b', 'grad_w_ffn_gate', 'grad_w_ffn_up', 'grad_conv_w', 'grad_conv_b', 'grad_w_ffn_down', 'grad_ln2_g', 'grad_ln2_b', 'delta_w_in', 'delta_b_gate', 'delta_w_branch_attn', 'delta_w_pool', 'delta_pool_scale', 'delta_w_branch_pool', 'delta_w_out', 'delta_ln1_g', 'delta_ln1_b', 'delta_w_ffn_gate', 'delta_w_ffn_up', 'delta_conv_w', 'delta_conv_b', 'delta_w_ffn_down', 'delta_ln2_g', 'delta_ln2_b', 'new_m_w_in', 'new_m_b_gate', 'new_m_w_branch_attn', 'new_m_w_pool', 'new_m_pool_scale', 'new_m_w_branch_pool', 'new_m_w_out', 'new_m_ln1_g', 'new_m_ln1_b', 'new_m_w_ffn_gate', 'new_m_w_ffn_up', 'new_m_conv_w', 'new_m_conv_b', 'new_m_w_ffn_down', 'new_m_ln2_g', 'new_m_ln2_b', 'new_v_w_in', 'new_v_b_gate', 'new_v_w_branch_attn', 'new_v_w_pool', 'new_v_pool_scale', 'new_v_w_branch_pool', 'new_v_w_out', 'new_v_ln1_g', 'new_v_ln1_b', 'new_v_w_ffn_gate', 'new_v_w_ffn_up', 'new_v_conv_w', 'new_v_conv_b', 'new_v_w_ffn_down', 'new_v_ln2_g', 'new_v_ln2_b']
TWIN_LEAF_KINDS = {'loss': 'loss', 'grad_x': 'grad_x', 'grad_w_in': 'grad_w', 'grad_b_gate': 'grad_w', 'grad_w_branch_attn': 'grad_w', 'grad_w_pool': 'grad_w', 'grad_pool_scale': 'grad_w', 'grad_w_branch_pool': 'grad_w', 'grad_w_out': 'grad_w', 'grad_ln1_g': 'grad_w', 'grad_ln1_b': 'grad_w', 'grad_w_ffn_gate': 'grad_w', 'grad_w_ffn_up': 'grad_w', 'grad_conv_w': 'grad_w', 'grad_conv_b': 'grad_w', 'grad_w_ffn_down': 'grad_w', 'grad_ln2_g': 'grad_w', 'grad_ln2_b': 'grad_w', 'delta_w_in': 'delta_w', 'delta_b_gate': 'delta_w', 'delta_w_branch_attn': 'delta_w', 'delta_w_pool': 'delta_w', 'delta_pool_scale': 'delta_w', 'delta_w_branch_pool': 'delta_w', 'delta_w_out': 'delta_w', 'delta_ln1_g': 'delta_w', 'delta_ln1_b': 'delta_w', 'delta_w_ffn_gate': 'delta_w', 'delta_w_ffn_up': 'delta_w', 'delta_conv_w': 'delta_w', 'delta_conv_b': 'delta_w', 'delta_w_ffn_down': 'delta_w', 'delta_ln2_g': 'delta_w', 'delta_ln2_b': 'delta_w', 'new_m_w_in': 'new_m', 'new_m_b_gate': 'new_m', 'new_m_w_branch_attn': 'new_m', 'new_m_w_pool': 'new_m', 'new_m_pool_scale': 'new_m', 'new_m_w_branch_pool': 'new_m', 'new_m_w_out': 'new_m', 'new_m_ln1_g': 'new_m', 'new_m_ln1_b': 'new_m', 'new_m_w_ffn_gate': 'new_m', 'new_m_w_ffn_up': 'new_m', 'new_m_conv_w': 'new_m', 'new_m_conv_b': 'new_m', 'new_m_w_ffn_down': 'new_m', 'new_m_ln2_g': 'new_m', 'new_m_ln2_b': 'new_m', 'new_v_w_in': 'new_v', 'new_v_b_gate': 'new_v', 'new_v_w_branch_attn': 'new_v', 'new_v_w_pool': 'new_v', 'new_v_pool_scale': 'new_v', 'new_v_w_branch_pool': 'new_v', 'new_v_w_out': 'new_v', 'new_v_ln1_g': 'new_v', 'new_v_ln1_b': 'new_v', 'new_v_w_ffn_gate': 'new_v', 'new_v_w_ffn_up': 'new_v', 'new_v_conv_w': 'new_v', 'new_v_conv_b': 'new_v', 'new_v_w_ffn_down': 'new_v', 'new_v_ln2_g': 'new_v', 'new_v_ln2_b': 'new_v'}


def _forward(args):
    return _fwd_reference(*[args[k] for k in FWD_PARAMS])


def _output_shape():
    out = _jax.eval_shape(lambda: _forward(_fwd_setup_inputs(0)))
    return out.shape, out.dtype

N_MICROBATCH = 1
ADAM_LR = 0.001
ADAM_B1 = 0.9
ADAM_B2 = 0.999
ADAM_EPS = 1e-08
ADAM_WD = 0.01
ADAM_STEP = 10
PER_EXAMPLE_BATCH_AXIS = {'x': 0, 'loss_target': 0}
SHARED_INPUTS = []
_WEIGHT_DTYPES = {'w_in': _jnp.float32, 'b_gate': _jnp.float32, 'w_branch_attn': _jnp.float32, 'w_pool': _jnp.float32, 'pool_scale': _jnp.float32, 'w_branch_pool': _jnp.float32, 'w_out': _jnp.float32, 'ln1_g': _jnp.float32, 'ln1_b': _jnp.float32, 'w_ffn_gate': _jnp.float32, 'w_ffn_up': _jnp.float32, 'conv_w': _jnp.float32, 'conv_b': _jnp.float32, 'w_ffn_down': _jnp.float32, 'ln2_g': _jnp.float32, 'ln2_b': _jnp.float32}
MOMENT_SCALE = {'w_in': 1.747582e-02, 'b_gate': 8.977378e-03, 'w_branch_attn': 7.490868e-03, 'w_pool': 4.459076e-02, 'pool_scale': 4.152080e-02, 'w_branch_pool': 3.137951e-02, 'w_out': 5.280482e-02, 'ln1_g': 1.407142e+00, 'ln1_b': 2.576509e-01, 'w_ffn_gate': 2.454117e-02, 'w_ffn_up': 2.394940e-02, 'conv_w': 2.446439e-02, 'conv_b': 2.408986e-02, 'w_ffn_down': 6.692937e-02, 'ln2_g': 1.605116e+01, 'ln2_b': 3.858768e-01}


def _to_microbatches(a, axis):
    t = _jnp.moveaxis(a, axis, 0)
    t = t.reshape((N_MICROBATCH, t.shape[0] // N_MICROBATCH) + t.shape[1:])
    return _jnp.moveaxis(t, 1, axis + 1)


def setup_inputs(seed: int = 0) -> dict:
    inp = _fwd_setup_inputs(seed)
    key = _jax.random.fold_in(_jax.random.key(seed), 7919)
    shape, _ = _output_shape()
    out = dict(inp)
    out["loss_target"] = _jax.random.normal(_jax.random.fold_in(key, 0), shape, _jnp.float32)
    for i, name in enumerate(TWIN_WEIGHTS):
        w = inp[name].astype(_jnp.float32)
        if MOMENT_SCALE is None:
            s = _jnp.sqrt(_jnp.mean(_jnp.square(w)) + 1e-30)
        else:
            s = MOMENT_SCALE[name]
        km, kv = _jax.random.split(_jax.random.fold_in(key, i + 1))
        out[name] = w
        out["m_" + name] = s * _jax.random.normal(km, w.shape, _jnp.float32)
        out["v_" + name] = (s * s) * _jax.random.uniform(kv, w.shape, _jnp.float32, 0.5, 1.5)
    if N_MICROBATCH > 1:
        for name, axis in PER_EXAMPLE_BATCH_AXIS.items():
            out[name] = _to_microbatches(out[name], axis)
    return {'x': out['x'], 'w_in': out['w_in'], 'b_gate': out['b_gate'], 'w_branch_attn': out['w_branch_attn'], 'w_pool': out['w_pool'], 'pool_scale': out['pool_scale'], 'w_branch_pool': out['w_branch_pool'], 'w_out': out['w_out'], 'ln1_g': out['ln1_g'], 'ln1_b': out['ln1_b'], 'w_ffn_gate': out['w_ffn_gate'], 'w_ffn_up': out['w_ffn_up'], 'conv_w': out['conv_w'], 'conv_b': out['conv_b'], 'w_ffn_down': out['w_ffn_down'], 'ln2_g': out['ln2_g'], 'ln2_b': out['ln2_b'], 'loss_target': out['loss_target'], 'm_w_in': out['m_w_in'], 'm_b_gate': out['m_b_gate'], 'm_w_branch_attn': out['m_w_branch_attn'], 'm_w_pool': out['m_w_pool'], 'm_pool_scale': out['m_pool_scale'], 'm_w_branch_pool': out['m_w_branch_pool'], 'm_w_out': out['m_w_out'], 'm_ln1_g': out['m_ln1_g'], 'm_ln1_b': out['m_ln1_b'], 'm_w_ffn_gate': out['m_w_ffn_gate'], 'm_w_ffn_up': out['m_w_ffn_up'], 'm_conv_w': out['m_conv_w'], 'm_conv_b': out['m_conv_b'], 'm_w_ffn_down': out['m_w_ffn_down'], 'm_ln2_g': out['m_ln2_g'], 'm_ln2_b': out['m_ln2_b'], 'v_w_in': out['v_w_in'], 'v_b_gate': out['v_b_gate'], 'v_w_branch_attn': out['v_w_branch_attn'], 'v_w_pool': out['v_w_pool'], 'v_pool_scale': out['v_pool_scale'], 'v_w_branch_pool': out['v_w_branch_pool'], 'v_w_out': out['v_w_out'], 'v_ln1_g': out['v_ln1_g'], 'v_ln1_b': out['v_ln1_b'], 'v_w_ffn_gate': out['v_w_ffn_gate'], 'v_w_ffn_up': out['v_w_ffn_up'], 'v_conv_w': out['v_conv_w'], 'v_conv_b': out['v_conv_b'], 'v_w_ffn_down': out['v_w_ffn_down'], 'v_ln2_g': out['v_ln2_g'], 'v_ln2_b': out['v_ln2_b']}


def _loss(weights, diff, rest, loss_target):
    with _jax.named_scope("forward"):
        args = {**rest, TWIN_DIFF_INPUT: diff, **{k: w.astype(_WEIGHT_DTYPES[k]) for k, w in weights.items()}}
        y = _forward(args)
    with _jax.named_scope("loss_head"):
        err = _jnp.square(y.astype(_jnp.float32) - loss_target)
        return 0.5 * _jnp.sum(_jnp.mean(err, axis=-1)) if err.ndim else 0.5 * err


def _adamw(w, g, m, v):
    m = ADAM_B1 * m + (1.0 - ADAM_B1) * g
    v = ADAM_B2 * v + (1.0 - ADAM_B2) * _jnp.square(g)
    m_hat = m / (1.0 - ADAM_B1 ** ADAM_STEP)
    v_hat = v / (1.0 - ADAM_B2 ** ADAM_STEP)
    delta = -ADAM_LR * (m_hat / (_jnp.sqrt(v_hat) + ADAM_EPS) + ADAM_WD * w)
    return delta, m, v


def reference(x, w_in, b_gate, w_branch_attn, w_pool, pool_scale, w_branch_pool, w_out, ln1_g, ln1_b, w_ffn_gate, w_ffn_up, conv_w, conv_b, w_ffn_down, ln2_g, ln2_b, loss_target, m_w_in, m_b_gate, m_w_branch_attn, m_w_pool, m_pool_scale, m_w_branch_pool, m_w_out, m_ln1_g, m_ln1_b, m_w_ffn_gate, m_w_ffn_up, m_conv_w, m_conv_b, m_w_ffn_down, m_ln2_g, m_ln2_b, v_w_in, v_b_gate, v_w_branch_attn, v_w_pool, v_pool_scale, v_w_branch_pool, v_w_out, v_ln1_g, v_ln1_b, v_w_ffn_gate, v_w_ffn_up, v_conv_w, v_conv_b, v_w_ffn_down, v_ln2_g, v_ln2_b):
    given = dict(x=x, w_in=w_in, b_gate=b_gate, w_branch_attn=w_branch_attn, w_pool=w_pool, pool_scale=pool_scale, w_branch_pool=w_branch_pool, w_out=w_out, ln1_g=ln1_g, ln1_b=ln1_b, w_ffn_gate=w_ffn_gate, w_ffn_up=w_ffn_up, conv_w=conv_w, conv_b=conv_b, w_ffn_down=w_ffn_down, ln2_g=ln2_g, ln2_b=ln2_b, loss_target=loss_target, m_w_in=m_w_in, m_b_gate=m_b_gate, m_w_branch_attn=m_w_branch_attn, m_w_pool=m_w_pool, m_pool_scale=m_pool_scale, m_w_branch_pool=m_w_branch_pool, m_w_out=m_w_out, m_ln1_g=m_ln1_g, m_ln1_b=m_ln1_b, m_w_ffn_gate=m_w_ffn_gate, m_w_ffn_up=m_w_ffn_up, m_conv_w=m_conv_w, m_conv_b=m_conv_b, m_w_ffn_down=m_w_ffn_down, m_ln2_g=m_ln2_g, m_ln2_b=m_ln2_b, v_w_in=v_w_in, v_b_gate=v_b_gate, v_w_branch_attn=v_w_branch_attn, v_w_pool=v_w_pool, v_pool_scale=v_pool_scale, v_w_branch_pool=v_w_branch_pool, v_w_out=v_w_out, v_ln1_g=v_ln1_g, v_ln1_b=v_ln1_b, v_w_ffn_gate=v_w_ffn_gate, v_w_ffn_up=v_w_ffn_up, v_conv_w=v_conv_w, v_conv_b=v_conv_b, v_w_ffn_down=v_w_ffn_down, v_ln2_g=v_ln2_g, v_ln2_b=v_ln2_b)
    weights = {n: given[n] for n in TWIN_WEIGHTS}
    shared = {n: given[n] for n in SHARED_INPUTS}
    per_example = {n: given[n] for n in ['x']}
    grad_fn = _jax.value_and_grad(_loss, argnums=(0, 1))

    def one_microbatch(ex, loss_target):
        ex = dict(ex)
        diff = ex.pop(TWIN_DIFF_INPUT)
        return grad_fn(weights, diff, {**shared, **ex}, loss_target)

    if N_MICROBATCH == 1:
        loss, (grad_w, grad_x) = one_microbatch(per_example, given["loss_target"])
    else:
        def body(carry, xs):
            loss_sum, grad_sum = carry
            l_k, (gw_k, gx_k) = one_microbatch(xs[0], xs[1])
            with _jax.named_scope("update"):
                return (loss_sum + l_k, _jax.tree.map(_jnp.add, grad_sum, gw_k)), gx_k

        init = (_jnp.zeros((), _jnp.float32), _jax.tree.map(_jnp.zeros_like, weights))
        (loss, grad_w), grad_x = _jax.lax.scan(body, init, (per_example, given["loss_target"]))
    with _jax.named_scope("update"):
        delta_w, new_m, new_v = {}, {}, {}
        for n in TWIN_WEIGHTS:
            delta_w[n], new_m[n], new_v[n] = _adamw(weights[n], grad_w[n], given["m_" + n], given["v_" + n])
    return (loss, grad_x, *[grad_w[n] for n in TWIN_WEIGHTS], *[delta_w[n] for n in TWIN_WEIGHTS],
            *[new_m[n] for n in TWIN_WEIGHTS], *[new_v[n] for n in TWIN_WEIGHTS])
```

```python
import functools

import jax
import jax.numpy as jnp
from jax import lax
from jax.experimental import pallas as pl
from jax.experimental.pallas import tpu as pltpu

D_MODEL = 1024
HEAD_DIM = 64
D_ATTN = 512
D_POOL = 512
MOBA_BLOCK = 256
MOBA_TOPK = 3
POOL_GROUP = 128
MAX_WINDOW = 16
FF_SHARD = 704
FF_PAD = 768
D_FF_PAD = 4 * FF_PAD
N_CHIPS = 4
LANES = 128
ALPHA = (2.0 * 1) ** 0.25
LN_EPS = 1e-5
NEG = -1e30
SCALE = HEAD_DIM ** -0.5
ADAM_LR, ADAM_B1, ADAM_B2, ADAM_EPS, ADAM_WD, ADAM_STEP = 0.001, 0.9, 0.999, 1e-08, 0.01, 10
VMEM_LIMIT = 56 * 1024 * 1024
MESH = pl.DeviceIdType.MESH

bf16 = jnp.bfloat16
f32 = jnp.float32


def _dot(a, b):
    return jnp.dot(a, b, preferred_element_type=f32)


def _dot_nt(a, b):
    return lax.dot_general(a, b, (((1,), (1,)), ((), ())), preferred_element_type=f32)


def _dot_tn(a, b):
    return lax.dot_general(a, b, (((0,), (0,)), ((), ())), preferred_element_type=f32)


def _call(body, *, name, grid, in_specs, out_specs, out_shape, scratch=()):
    return pl.pallas_call(
        body, name=name, grid=grid, in_specs=in_specs, out_specs=out_specs, out_shape=out_shape,
        scratch_shapes=list(scratch),
        compiler_params=pltpu.CompilerParams(dimension_semantics=("arbitrary",) * len(grid),
                                             vmem_limit_bytes=VMEM_LIMIT))


def _rot_half(t):
    lane = lax.broadcasted_iota(jnp.int32, t.shape, 1)
    first = (lane % HEAD_DIM) < (HEAD_DIM // 2)
    return jnp.where(first, pltpu.roll(t, LANES - HEAD_DIM // 2, 1), pltpu.roll(t, HEAD_DIM // 2, 1))


def _rope(t, cos, sin_signed):
    return t * cos + _rot_half(t) * sin_signed


def _rope_bwd(d, cos, sin_signed):
    return d * cos + _rot_half(d * sin_signed)


def _gelu_parts(a):
    cdf = 0.5 * (1.0 + lax.erf(a * (2.0 ** -0.5)))
    pdf = jnp.exp(-0.5 * a * a) * ((2.0 * jnp.pi) ** -0.5)
    return a * cdf, cdf + a * pdf


def _layer_norm(h, g, b):
    mu = jnp.mean(h, axis=-1, keepdims=True)
    xc = h - mu
    var = jnp.mean(xc * xc, axis=-1, keepdims=True)
    rstd = lax.rsqrt(var + LN_EPS)
    xhat = xc * rstd
    return xhat * g + b, xhat, rstd


def _layer_norm_bwd(dy, xhat, rstd, g):
    dxh = dy * g
    m1 = jnp.mean(dxh, axis=-1, keepdims=True)
    m2 = jnp.mean(dxh * xhat, axis=-1, keepdims=True)
    return rstd * (dxh - m1 - xhat * m2)


def _inproj(x_bf, win_g, cos_t, sin_t, b_gate):
    s = x_bf.shape[0]
    tm, tn = 512, 512

    def body(x_ref, w_ref, cos_ref, sin_ref, b_ref, o_ref):
        j = pl.program_id(1)
        acc = _dot(x_ref[...], w_ref[...])

        @pl.when(j < 2)
        def _():
            for c in range(tn // LANES):
                sl = slice(c * LANES, (c + 1) * LANES)
                o_ref[:, sl] = _rope(acc[:, sl], cos_ref[...], sin_ref[...])

        @pl.when((j >= 2) & (j < 4))
        def _():
            o_ref[...] = acc

        @pl.when(j >= 4)
        def _():
            o_ref[...] = jax.nn.sigmoid(acc + b_ref[...])

    return _call(
        body, name="inproj", grid=(s // tm, 4 * D_MODEL // tn),
        in_specs=[pl.BlockSpec((tm, D_MODEL), lambda i, j: (i, 0)),
                  pl.BlockSpec((None, D_MODEL, tn), lambda i, j: (j // 2, 0, j % 2)),
                  pl.BlockSpec((tm, LANES), lambda i, j: (i, 0)),
                  pl.BlockSpec((tm, LANES), lambda i, j: (i, 0)),
                  pl.BlockSpec((1, tn), lambda i, j: (0, jnp.maximum(j - 4, 0)))],
        out_specs=pl.BlockSpec((tm, tn), lambda i, j: (i, j)),
        out_shape=jax.ShapeDtypeStruct((s, 4 * D_MODEL), f32))(x_bf, win_g, cos_t, sin_t, b_gate)


def _moba_select(q, k_all, i, nb):
    lane = lax.broadcasted_iota(jnp.int32, q.shape, 1)
    k_mean = jnp.mean(k_all.reshape(nb, MOBA_BLOCK, LANES), axis=1)
    n_io = lax.broadcasted_iota(jnp.int32, (MOBA_BLOCK, nb), 1)
    past = n_io < i
    sels = []
    for h in range(2):
        head = (lane < HEAD_DIM) if h == 0 else (lane >= HEAD_DIM)
        qh = jnp.where(head, q, 0.0)
        gate = lax.dot_general(qh, k_mean, (((1,), (1,)), ((), ())), precision=lax.Precision.HIGHEST,
                               preferred_element_type=f32)
        g = jnp.where(past, gate, NEG)
        rank = jnp.zeros((MOBA_BLOCK, nb), f32)
        for m in range(nb):
            gm = g[:, m:m + 1]
            rank = rank + jnp.where((gm > g) | ((gm == g) & (m < n_io)), 1.0, 0.0)
        sels.append(jnp.where(past & (rank < MOBA_TOPK), 1.0, 0.0))
    return sels


def _attn_fwd(z, nb):
    s = z.shape[0]

    def body(q_ref, k_ref, v_ref, o_ref, lse_ref, m_s, l_s, acc_s):
        i = pl.program_id(1)
        lane = lax.broadcasted_iota(jnp.int32, (MOBA_BLOCK, LANES), 1)
        head0 = lane < HEAD_DIM
        heads = (head0, jnp.logical_not(head0))
        q = q_ref[...]
        sels = _moba_select(q, k_ref[...], i, nb)
        qs = q * SCALE
        qb = [jnp.where(hm, qs, 0.0).astype(bf16) for hm in heads]
        row = lax.broadcasted_iota(jnp.int32, (MOBA_BLOCK, MOBA_BLOCK), 0)
        col = lax.broadcasted_iota(jnp.int32, (MOBA_BLOCK, MOBA_BLOCK), 1)

        own = pl.ds(pl.multiple_of(i * MOBA_BLOCK, MOBA_BLOCK), MOBA_BLOCK)
        kj = k_ref[own, :].astype(bf16)
        vj = v_ref[own, :]
        ms, ls, pv = [], [], None
        for h in range(2):
            sc = jnp.where(col <= row, _dot_nt(qb[h], kj), NEG)
            m = jnp.max(sc, axis=1, keepdims=True)
            p = jnp.exp(sc - m)
            ms.append(m)
            ls.append(jnp.sum(p, axis=1, keepdims=True))
            t = _dot(p.astype(bf16), jnp.where(heads[h], vj, 0.0).astype(bf16))
            pv = t if pv is None else pv + t
        m_s[...] = jnp.where(head0, ms[0], ms[1])
        l_s[...] = jnp.where(head0, ls[0], ls[1])
        acc_s[...] = pv

        for j in range(nb - 1):
            @pl.when(j < i)
            def _(j=j):
                kj = k_ref[j * MOBA_BLOCK:(j + 1) * MOBA_BLOCK, :].astype(bf16)
                vj = v_ref[j * MOBA_BLOCK:(j + 1) * MOBA_BLOCK, :]
                m_old_l, l_old_l = m_s[...], l_s[...]
                m_new, alpha, l_new, pv = [], [], [], None
                for h in range(2):
                    c0 = h * HEAD_DIM
                    sc = jnp.where(sels[h][:, j:j + 1] > 0.0, _dot_nt(qb[h], kj), NEG)
                    m_old = m_old_l[:, c0:c0 + 1]
                    mn = jnp.maximum(m_old, jnp.max(sc, axis=1, keepdims=True))
                    a = jnp.exp(m_old - mn)
                    p = jnp.exp(sc - mn)
                    m_new.append(mn)
                    alpha.append(a)
                    l_new.append(a * l_old_l[:, c0:c0 + 1] + jnp.sum(p, axis=1, keepdims=True))
                    t = _dot(p.astype(bf16), jnp.where(heads[h], vj, 0.0).astype(bf16))
                    pv = t if pv is None else pv + t
                m_s[...] = jnp.where(head0, m_new[0], m_new[1])
                l_s[...] = jnp.where(head0, l_new[0], l_new[1])
                acc_s[...] = acc_s[...] * jnp.where(head0, alpha[0], alpha[1]) + pv

        o_ref[...] = (acc_s[...] / l_s[...]).astype(bf16)
        lse_ref[...] = m_s[...] + jnp.log(l_s[...])

    return _call(
        body, name="attn_fwd", grid=(D_ATTN // LANES, nb),
        in_specs=[pl.BlockSpec((MOBA_BLOCK, LANES), lambda hp, i: (i, hp)),
                  pl.BlockSpec((s, LANES), lambda hp, i: (0, 4 + hp)),
                  pl.BlockSpec((s, LANES), lambda hp, i: (0, 8 + hp))],
        out_specs=[pl.BlockSpec((MOBA_BLOCK, LANES), lambda hp, i: (i, hp)),
                   pl.BlockSpec((None, MOBA_BLOCK, LANES), lambda hp, i: (hp, i, 0))],
        out_shape=[jax.ShapeDtypeStruct((s, D_ATTN), bf16),
                   jax.ShapeDtypeStruct((D_ATTN // LANES, s, LANES), f32)],
        scratch=[pltpu.VMEM((MOBA_BLOCK, LANES), f32)] * 3)(z, z, z)


def _window_select(snaps, g):
    return jnp.where(g == 0, snaps[0], jnp.where(g == 1, snaps[1], jnp.where(g == 2, snaps[2], snaps[3])))


def _pool_fwd(z, w_pool, pool_scale):
    s = z.shape[0]

    def body(u_ref, w_ref, sc_ref, pooled_ref, mixed_ref, pm_ref, pad):
        g = pl.program_id(0)
        u = u_ref[...]
        pad[0:MAX_WINDOW, :] = jnp.zeros((MAX_WINDOW, POOL_GROUP), f32)
        pad[MAX_WINDOW:MAX_WINDOW + s, :] = u
        acc = u
        snaps = []
        for d in range(1, MAX_WINDOW):
            acc = acc + pad[MAX_WINDOW - d:MAX_WINDOW - d + s, :]
            if d + 1 in (2, 4, 8, 16):
                snaps.append(acc)
        win = _window_select(snaps, g)
        t = lax.broadcasted_iota(jnp.int32, (s, POOL_GROUP), 0)
        count = jnp.minimum(t + 1, jnp.left_shift(2, g)).astype(f32)
        pooled = (win / count - u).astype(bf16)
        mixed = _dot(pooled, w_ref[...].astype(bf16))
        pooled_ref[...] = pooled
        mixed_ref[...] = mixed
        pm_ref[...] = (mixed * sc_ref[...]).astype(bf16)

    blk = pl.BlockSpec((s, POOL_GROUP), lambda g: (0, g))
    return _call(
        body, name="pool_fwd", grid=(4,),
        in_specs=[pl.BlockSpec((s, POOL_GROUP), lambda g: (0, 12 + g)),
                  pl.BlockSpec((None, POOL_GROUP, POOL_GROUP), lambda g: (g, 0, 0)),
                  pl.BlockSpec((1, POOL_GROUP), lambda g: (0, g))],
        out_specs=[blk, blk, blk],
        out_shape=[jax.ShapeDtypeStruct((s, D_POOL), bf16), jax.ShapeDtypeStruct((s, D_POOL), f32),
                   jax.ShapeDtypeStruct((s, D_POOL), bf16)],
        scratch=[pltpu.VMEM((s + MAX_WINDOW, POOL_GROUP), f32)])(z, w_pool, pool_scale)


def _branch_merge(o_bf, pm_bf, z, wba, wbp):
    s = o_bf.shape[0]
    tm = 256

    def body(o_ref, pm_ref, ga_ref, gp_ref, wba_ref, wbp_ref, m_ref):
        ya = _dot(o_ref[...], wba_ref[...])
        yp = _dot(pm_ref[...], wbp_ref[...])
        m_ref[...] = (ga_ref[...] * ya + gp_ref[...] * yp).astype(bf16)

    full = lambda r, c: pl.BlockSpec((r, c), lambda i: (0, 0))
    return _call(
        body, name="branch_merge", grid=(s // tm,),
        in_specs=[pl.BlockSpec((tm, D_ATTN), lambda i: (i, 0)), pl.BlockSpec((tm, D_POOL), lambda i: (i, 0)),
                  pl.BlockSpec((tm, D_MODEL), lambda i: (i, 2)), pl.BlockSpec((tm, D_MODEL), lambda i: (i, 3)),
                  full(D_ATTN, D_MODEL), full(D_POOL, D_MODEL)],
        out_specs=pl.BlockSpec((tm, D_MODEL), lambda i: (i, 0)),
        out_shape=jax.ShapeDtypeStruct((s, D_MODEL), bf16))(o_bf, pm_bf, z, z, wba, wbp)


def _out_ln1(m_bf, wout, x, ln_g, ln_b):
    s = x.shape[0]
    tm = 256

    def body(m_ref, w_ref, x_ref, g_ref, b_ref, x1_ref, xhat_ref, rstd_ref):
        h = ALPHA * x_ref[...] + _dot(m_ref[...], w_ref[...])
        y, xhat, rstd = _layer_norm(h, g_ref[...], b_ref[...])
        x1_ref[...] = y.astype(bf16)
        xhat_ref[...] = xhat
        rstd_ref[...] = jnp.broadcast_to(rstd, (tm, LANES))

    row = pl.BlockSpec((tm, D_MODEL), lambda i: (i, 0))
    vec = pl.BlockSpec((1, D_MODEL), lambda i: (0, 0))
    return _call(
        body, name="out_ln1", grid=(s // tm,),
        in_specs=[row, pl.BlockSpec((D_MODEL, D_MODEL), lambda i: (0, 0)), row, vec, vec],
        out_specs=[row, row, pl.BlockSpec((tm, LANES), lambda i: (i, 0))],
        out_shape=[jax.ShapeDtypeStruct((s, D_MODEL), bf16), jax.ShapeDtypeStruct((s, D_MODEL), f32),
                   jax.ShapeDtypeStruct((s, LANES), f32)])(m_bf, wout, x, ln_g, ln_b)


FF_TILE = 256
FF_TILES_PER_SHARD = FF_PAD // FF_TILE
CONV_PAD = 8


def _ff_weight_spec(rows):
    return pl.BlockSpec((None, rows, FF_TILE), lambda j: (j // FF_TILES_PER_SHARD, 0, j % FF_TILES_PER_SHARD))


def _conv_taps(pad, s, cw_ref, off=0):
    return (pad[CONV_PAD + off:CONV_PAD + off + s, :] * cw_ref[2:3, :]
            + pad[CONV_PAD - 1 + off:CONV_PAD - 1 + off + s, :] * cw_ref[1:2, :]
            + pad[CONV_PAD - 2 + off:CONV_PAD - 2 + off + s, :] * cw_ref[0:1, :])


def _ffn_up(x1_bf, wg_g, wu_g, convw_g, convb_g):
    s = x1_bf.shape[0]

    def body(x_ref, wg_ref, wu_ref, cw_ref, cb_ref, a_ref, up_ref, hh_ref, pad):
        x1 = x_ref[...]
        a = _dot(x1, wg_ref[...])
        up = _dot(x1, wu_ref[...])
        a_ref[...] = a
        up_ref[...] = up
        pad[0:CONV_PAD, :] = jnp.zeros((CONV_PAD, FF_TILE), f32)
        pad[CONV_PAD:CONV_PAD + s, :] = a
        ac = _conv_taps(pad, s, cw_ref) + cb_ref[...]
        hg, _ = _gelu_parts(ac)
        hh_ref[...] = (hg * up).astype(bf16)

    col = pl.BlockSpec((s, FF_TILE), lambda j: (0, j))
    return _call(
        body, name="ffn_up", grid=(D_FF_PAD // FF_TILE,),
        in_specs=[pl.BlockSpec((s, D_MODEL), lambda j: (0, 0)), _ff_weight_spec(D_MODEL), _ff_weight_spec(D_MODEL),
                  _ff_weight_spec(8), _ff_weight_spec(1)],
        out_specs=[col, col, col],
        out_shape=[jax.ShapeDtypeStruct((s, D_FF_PAD), f32), jax.ShapeDtypeStruct((s, D_FF_PAD), f32),
                   jax.ShapeDtypeStruct((s, D_FF_PAD), bf16)],
        scratch=[pltpu.VMEM((s + CONV_PAD, FF_TILE), f32)])(x1_bf, wg_g, wu_g, convw_g, convb_g)


def _ffn_down_ln2_loss(hh_bf, wd, xhat1, ln1_g, ln1_b, ln2_g, ln2_b, target):
    s = hh_bf.shape[0]
    tm = 256

    def body(hh_ref, w_ref, xh1_ref, g1_ref, b1_ref, g2_ref, b2_ref, t_ref, loss_ref, dh_ref, dhb_ref, dg_ref, db_ref):
        i = pl.program_id(0)
        x1 = xh1_ref[...] * g1_ref[...] + b1_ref[...]
        h = ALPHA * x1 + _dot(hh_ref[...], w_ref[...])
        y, xhat, rstd = _layer_norm(h, g2_ref[...], b2_ref[...])
        err = y - t_ref[...]
        part = 0.5 * jnp.sum(jnp.mean(err * err, axis=-1, keepdims=True), axis=0, keepdims=True)
        dy = err * (1.0 / D_MODEL)

        @pl.when(i == 0)
        def _():
            loss_ref[...] = jnp.zeros_like(loss_ref)
            dg_ref[...] = jnp.zeros_like(dg_ref)
            db_ref[...] = jnp.zeros_like(db_ref)

        loss_ref[...] += jnp.broadcast_to(part, loss_ref.shape)
        dg_ref[...] += jnp.sum(dy * xhat, axis=0, keepdims=True)
        db_ref[...] += jnp.sum(dy, axis=0, keepdims=True)
        dh = _layer_norm_bwd(dy, xhat, rstd, g2_ref[...])
        dh_ref[...] = dh
        dhb_ref[...] = dh.astype(bf16)

    row = pl.BlockSpec((tm, D_MODEL), lambda i: (i, 0))
    vec = pl.BlockSpec((1, D_MODEL), lambda i: (0, 0))
    return _call(
        body, name="ffn_down_ln2_loss", grid=(s // tm,),
        in_specs=[pl.BlockSpec((tm, D_FF_PAD), lambda i: (i, 0)), pl.BlockSpec((D_FF_PAD, D_MODEL), lambda i: (0, 0)),
                  row, vec, vec, vec, vec, row],
        out_specs=[pl.BlockSpec((8, LANES), lambda i: (0, 0)), row, row, vec, vec],
        out_shape=[jax.ShapeDtypeStruct((8, LANES), f32), jax.ShapeDtypeStruct((s, D_MODEL), f32),
                   jax.ShapeDtypeStruct((s, D_MODEL), bf16), jax.ShapeDtypeStruct((1, D_MODEL), f32),
                   jax.ShapeDtypeStruct((1, D_MODEL), f32)])(hh_bf, wd, xhat1, ln1_g, ln1_b, ln2_g, ln2_b, target)


def _ffn_act_bwd(dh2_bf, wd_g, a, up, convw_g, convb_g):
    s = dh2_bf.shape[0]

    def body(dh_ref, wd_ref, a_ref, up_ref, cw_ref, cb_ref, da_ref, dup_ref, dc_ref, pad, dpad):
        dhh = _dot_nt(dh_ref[...], wd_ref[...])
        zeros = jnp.zeros((CONV_PAD, FF_TILE), f32)
        pad[0:CONV_PAD, :] = zeros
        pad[CONV_PAD:CONV_PAD + s, :] = a_ref[...]
        ac = _conv_taps(pad, s, cw_ref) + cb_ref[...]
        hg, dgelu = _gelu_parts(ac)
        up = up_ref[...]
        dup_ref[...] = (dhh * hg).astype(bf16)
        dac = dhh * up * dgelu
        rows = [jnp.sum(dac * pad[CONV_PAD - 2 + i:CONV_PAD - 2 + i + s, :], axis=0, keepdims=True) for i in range(3)]
        rows.append(jnp.sum(dac, axis=0, keepdims=True))
        rows.append(jnp.zeros((4, FF_TILE), f32))
        dc_ref[...] = jnp.concatenate(rows, axis=0)
        dpad[0:s, :] = dac
        dpad[s:s + CONV_PAD, :] = zeros
        da = (dpad[0:s, :] * cw_ref[2:3, :] + dpad[1:1 + s, :] * cw_ref[1:2, :] + dpad[2:2 + s, :] * cw_ref[0:1, :])
        da_ref[...] = da.astype(bf16)

    col = pl.BlockSpec((s, FF_TILE), lambda j: (0, j))
    return _call(
        body, name="ffn_act_bwd", grid=(D_FF_PAD // FF_TILE,),
        in_specs=[pl.BlockSpec((s, D_MODEL), lambda j: (0, 0)),
                  pl.BlockSpec((None, FF_TILE, D_MODEL),
                               lambda j: (j // FF_TILES_PER_SHARD, j % FF_TILES_PER_SHARD, 0)),
                  col, col, _ff_weight_spec(8), _ff_weight_spec(1)],
        out_specs=[col, col, _ff_weight_spec(8)],
        out_shape=[jax.ShapeDtypeStruct((s, D_FF_PAD), bf16), jax.ShapeDtypeStruct((s, D_FF_PAD), bf16),
                   jax.ShapeDtypeStruct((N_CHIPS, 8, FF_PAD), f32)],
        scratch=[pltpu.VMEM((s + CONV_PAD, FF_TILE), f32), pltpu.VMEM((s + CONV_PAD, FF_TILE), f32)],
    )(dh2_bf, wd_g, a, up, convw_g, convb_g)


def _matmul_tn(a, b, n_shards, name, tm=512, tn=256):
    k, m = a.shape
    n = b.shape[1]
    tm, tn = min(tm, m), min(tn, n // n_shards)
    per = n // n_shards // tn

    def body(a_ref, b_ref, o_ref):
        o_ref[...] = _dot_tn(a_ref[...], b_ref[...])

    return _call(
        body, name=name, grid=(m // tm, n // tn),
        in_specs=[pl.BlockSpec((k, tm), lambda i, j: (0, i)), pl.BlockSpec((k, tn), lambda i, j: (0, j))],
        out_specs=pl.BlockSpec((None, tm, tn), lambda i, j: (j // per, i, j % per)),
        out_shape=jax.ShapeDtypeStruct((n_shards, m, n // n_shards), f32))(a, b)


def _ffn_in_bwd_ln1(da_bf, dup_bf, wg_g, wu_g, dh2, xhat1, rstd1, ln1_g):
    s = da_bf.shape[0]
    tm = 256

    def body(da_ref, dup_ref, wg_ref, wu_ref, dh2_ref, xh_ref, rstd_ref, g_ref, dh_ref, dhb_ref, dg_ref, db_ref):
        i = pl.program_id(0)
        dx1 = ALPHA * dh2_ref[...]
        for sh in range(N_CHIPS):
            sl = slice(sh * FF_PAD, (sh + 1) * FF_PAD)
            dx1 = dx1 + _dot_nt(da_ref[:, sl], wg_ref[sh]) + _dot_nt(dup_ref[:, sl], wu_ref[sh])
        xhat = xh_ref[...]

        @pl.when(i == 0)
        def _():
            dg_ref[...] = jnp.zeros_like(dg_ref)
            db_ref[...] = jnp.zeros_like(db_ref)

        dg_ref[...] += jnp.sum(dx1 * xhat, axis=0, keepdims=True)
        db_ref[...] += jnp.sum(dx1, axis=0, keepdims=True)
        dh = _layer_norm_bwd(dx1, xhat, rstd_ref[:, 0:1], g_ref[...])
        dh_ref[...] = dh
        dhb_ref[...] = dh.astype(bf16)

    row = pl.BlockSpec((tm, D_MODEL), lambda i: (i, 0))
    wide = pl.BlockSpec((tm, D_FF_PAD), lambda i: (i, 0))
    wfull = pl.BlockSpec((N_CHIPS, D_MODEL, FF_PAD), lambda i: (0, 0, 0))
    vec = pl.BlockSpec((1, D_MODEL), lambda i: (0, 0))
    return _call(
        body, name="ffn_in_bwd_ln1", grid=(s // tm,),
        in_specs=[wide, wide, wfull, wfull, row, row, pl.BlockSpec((tm, LANES), lambda i: (i, 0)), vec],
        out_specs=[row, row, vec, vec],
        out_shape=[jax.ShapeDtypeStruct((s, D_MODEL), f32), jax.ShapeDtypeStruct((s, D_MODEL), bf16),
                   jax.ShapeDtypeStruct((1, D_MODEL), f32), jax.ShapeDtypeStruct((1, D_MODEL), f32)],
    )(da_bf, dup_bf, wg_g, wu_g, dh2, xhat1, rstd1, ln1_g)


def _merge_bwd(dh1_bf, wout, o_bf, pm_bf, z, wba, wbp):
    s = dh1_bf.shape[0]
    tm = 256

    def body(dh_ref, wout_ref, o_ref, pm_ref, ga_ref, gp_ref, wba_ref, wbp_ref,
             dzg_ref, dya_ref, dyp_ref, do_ref, dpm_ref, dbg_ref):
        i = pl.program_id(0)
        dm = _dot_nt(dh_ref[...], wout_ref[...])
        ya = _dot(o_ref[...], wba_ref[...])
        yp = _dot(pm_ref[...], wbp_ref[...])
        ga, gp = ga_ref[...], gp_ref[...]
        dza = dm * ya * ga * (1.0 - ga)
        dzp = dm * yp * gp * (1.0 - gp)

        @pl.when(i == 0)
        def _():
            dbg_ref[...] = jnp.zeros_like(dbg_ref)

        dbg_ref[:, 0:D_MODEL] += jnp.sum(dza, axis=0, keepdims=True)
        dbg_ref[:, D_MODEL:2 * D_MODEL] += jnp.sum(dzp, axis=0, keepdims=True)
        dzg_ref[:, 0:D_MODEL] = dza.astype(bf16)
        dzg_ref[:, D_MODEL:2 * D_MODEL] = dzp.astype(bf16)
        dya = (dm * ga).astype(bf16)
        dyp = (dm * gp).astype(bf16)
        dya_ref[...] = dya
        dyp_ref[...] = dyp
        do_ref[...] = _dot_nt(dya, wba_ref[...]).astype(bf16)
        dpm_ref[...] = _dot_nt(dyp, wbp_ref[...])

    row = pl.BlockSpec((tm, D_MODEL), lambda i: (i, 0))
    half = pl.BlockSpec((tm, D_ATTN), lambda i: (i, 0))
    full = lambda r, c: pl.BlockSpec((r, c), lambda i: (0, 0))
    return _call(
        body, name="merge_bwd", grid=(s // tm,),
        in_specs=[row, full(D_MODEL, D_MODEL), half, half,
                  pl.BlockSpec((tm, D_MODEL), lambda i: (i, 2)), pl.BlockSpec((tm, D_MODEL), lambda i: (i, 3)),
                  full(D_ATTN, D_MODEL), full(D_POOL, D_MODEL)],
        out_specs=[pl.BlockSpec((tm, 2 * D_MODEL), lambda i: (i, 0)), row, row, half, half,
                   pl.BlockSpec((1, 2 * D_MODEL), lambda i: (0, 0))],
        out_shape=[jax.ShapeDtypeStruct((s, 2 * D_MODEL), bf16), jax.ShapeDtypeStruct((s, D_MODEL), bf16),
                   jax.ShapeDtypeStruct((s, D_MODEL), bf16), jax.ShapeDtypeStruct((s, D_ATTN), bf16),
                   jax.ShapeDtypeStruct((s, D_POOL), f32), jax.ShapeDtypeStruct((1, 2 * D_MODEL), f32)],
    )(dh1_bf, wout, o_bf, pm_bf, z, z, wba, wbp)


def _pool_bwd(dpm, mixed, pooled_bf, w_pool, pool_scale):
    s = dpm.shape[0]

    def body(dpm_ref, mixed_ref, pooled_ref, w_ref, sc_ref, du_ref, dw_ref, dsc_ref, pad):
        g = pl.program_id(0)
        dpm_v = dpm_ref[...]
        dsc_ref[...] = jnp.sum(dpm_v * mixed_ref[...], axis=0, keepdims=True)
        dmixed = (dpm_v * sc_ref[...]).astype(bf16)
        dw_ref[...] = _dot_tn(pooled_ref[...], dmixed)
        dpooled = _dot_nt(dmixed, w_ref[...].astype(bf16))
        t = lax.broadcasted_iota(jnp.int32, (s, POOL_GROUP), 0)
        count = jnp.minimum(t + 1, jnp.left_shift(2, g)).astype(f32)
        r = dpooled / count
        pad[0:s, :] = r
        pad[s:s + MAX_WINDOW, :] = jnp.zeros((MAX_WINDOW, POOL_GROUP), f32)
        acc = r
        snaps = []
        for d in range(1, MAX_WINDOW):
            acc = acc + pad[d:d + s, :]
            if d + 1 in (2, 4, 8, 16):
                snaps.append(acc)
        du_ref[...] = (_window_select(snaps, g) - dpooled).astype(bf16)

    blk = pl.BlockSpec((s, POOL_GROUP), lambda g: (0, g))
    return _call(
        body, name="pool_bwd", grid=(4,),
        in_specs=[blk, blk, blk, pl.BlockSpec((None, POOL_GROUP, POOL_GROUP), lambda g: (g, 0, 0)),
                  pl.BlockSpec((1, POOL_GROUP), lambda g: (0, g))],
        out_specs=[blk, pl.BlockSpec((None, POOL_GROUP, POOL_GROUP), lambda g: (g, 0, 0)),
                   pl.BlockSpec((1, POOL_GROUP), lambda g: (0, g))],
        out_shape=[jax.ShapeDtypeStruct((s, D_POOL), bf16), jax.ShapeDtypeStruct((4, POOL_GROUP, POOL_GROUP), f32),
                   jax.ShapeDtypeStruct((1, D_POOL), f32)],
        scratch=[pltpu.VMEM((s + MAX_WINDOW, POOL_GROUP), f32)])(dpm, mixed, pooled_bf, w_pool, pool_scale)


def _attn_bwd(z, o_bf, lse, do_bf, cos_t, sin_t, nb):
    s = z.shape[0]

    def body(q_ref, k_ref, v_ref, o_ref, lse_ref, do_ref, cq_ref, sq_ref, cf_ref, sf_ref,
             dq_ref, dk_ref, dv_ref, dq_s, dk_s, dv_s):
        i = pl.program_id(1)
        lane = lax.broadcasted_iota(jnp.int32, (MOBA_BLOCK, LANES), 1)
        head0 = lane < HEAD_DIM
        heads = (head0, jnp.logical_not(head0))

        @pl.when(i == 0)
        def _():
            dk_s[...] = jnp.zeros_like(dk_s)
            dv_s[...] = jnp.zeros_like(dv_s)

        q = q_ref[...]
        sels = _moba_select(q, k_ref[...], i, nb)
        qs = q * SCALE
        qb = [jnp.where(hm, qs, 0.0).astype(bf16) for hm in heads]
        qraw = [jnp.where(hm, q, 0.0).astype(bf16) for hm in heads]
        do = do_ref[...].astype(f32)
        prod = do * o_ref[...].astype(f32)
        delta = [jnp.sum(jnp.where(hm, prod, 0.0), axis=1, keepdims=True) for hm in heads]
        dob = [jnp.where(hm, do, 0.0).astype(bf16) for hm in heads]
        lse_l = lse_ref[...]
        lse_h = [lse_l[:, 0:1], lse_l[:, HEAD_DIM:HEAD_DIM + 1]]
        row = lax.broadcasted_iota(jnp.int32, (MOBA_BLOCK, MOBA_BLOCK), 0)
        col = lax.broadcasted_iota(jnp.int32, (MOBA_BLOCK, MOBA_BLOCK), 1)
        dq_s[...] = jnp.zeros_like(dq_s)

        def block(rows, allow):
            kj = k_ref[rows, :].astype(bf16)
            vj = v_ref[rows, :].astype(bf16)
            dq, dk, dv = None, None, None
            for h in range(2):
                p = jnp.where(allow[h], jnp.exp(_dot_nt(qb[h], kj) - lse_h[h]), 0.0)
                dp = _dot_nt(dob[h], vj)
                ds = (p * (dp - delta[h]) * SCALE).astype(bf16)
                tv = _dot_tn(p.astype(bf16), dob[h])
                tq = _dot(ds, jnp.where(heads[h], kj, jnp.zeros_like(kj)))
                tk = _dot_tn(ds, qraw[h])
                dq, dk, dv = (tq, tk, tv) if dq is None else (dq + tq, dk + tk, dv + tv)
            dq_s[...] += dq
            dk_s[rows, :] += dk
            dv_s[rows, :] += dv

        block(pl.ds(pl.multiple_of(i * MOBA_BLOCK, MOBA_BLOCK), MOBA_BLOCK), [col <= row, col <= row])
        for j in range(nb - 1):
            @pl.when(j < i)
            def _(j=j):
                block(slice(j * MOBA_BLOCK, (j + 1) * MOBA_BLOCK), [sels[h][:, j:j + 1] > 0.0 for h in range(2)])

        dq_ref[...] = _rope_bwd(dq_s[...], cq_ref[...], sq_ref[...]).astype(bf16)

        @pl.when(i == nb - 1)
        def _():
            dk_ref[...] = _rope_bwd(dk_s[...], cf_ref[...], sf_ref[...]).astype(bf16)
            dv_ref[...] = dv_s[...].astype(bf16)

    qblk = pl.BlockSpec((MOBA_BLOCK, LANES), lambda hp, i: (i, hp))
    tq = pl.BlockSpec((MOBA_BLOCK, LANES), lambda hp, i: (i, 0))
    tf = pl.BlockSpec((s, LANES), lambda hp, i: (0, 0))
    colblk = pl.BlockSpec((s, LANES), lambda hp, i: (0, hp))
    return _call(
        body, name="attn_bwd", grid=(D_ATTN // LANES, nb),
        in_specs=[qblk, pl.BlockSpec((s, LANES), lambda hp, i: (0, 4 + hp)),
                  pl.BlockSpec((s, LANES), lambda hp, i: (0, 8 + hp)), qblk,
                  pl.BlockSpec((None, MOBA_BLOCK, LANES), lambda hp, i: (hp, i, 0)), qblk, tq, tq, tf, tf],
        out_specs=[qblk, colblk, colblk],
        out_shape=[jax.ShapeDtypeStruct((s, D_ATTN), bf16)] * 3,
        scratch=[pltpu.VMEM((MOBA_BLOCK, LANES), f32), pltpu.VMEM((s, LANES), f32), pltpu.VMEM((s, LANES), f32)],
    )(z, z, z, o_bf, lse, do_bf, cos_t, sin_t, cos_t, sin_t)


def _inproj_dx(dz_bf, win_g, dh1):
    s = dz_bf.shape[0]
    tm = 256

    def body(dz_ref, w_ref, dh_ref, gx_ref):
        acc = ALPHA * dh_ref[...]
        for sh in range(N_CHIPS):
            acc = acc + _dot_nt(dz_ref[:, sh * D_MODEL:(sh + 1) * D_MODEL], w_ref[sh])
        gx_ref[...] = acc

    row = pl.BlockSpec((tm, D_MODEL), lambda i: (i, 0))
    return _call(
        body, name="inproj_dx", grid=(s // tm,),
        in_specs=[pl.BlockSpec((tm, 4 * D_MODEL), lambda i: (i, 0)),
                  pl.BlockSpec((N_CHIPS, D_MODEL, D_MODEL), lambda i: (0, 0, 0)), row],
        out_specs=row, out_shape=jax.ShapeDtypeStruct((s, D_MODEL), f32))(dz_bf, win_g, dh1)


ANY = pl.BlockSpec(memory_space=pl.ANY)


def _chip_index():
    return 2 * lax.axis_index("x") + lax.axis_index("y")


def _peer(k):
    x, y, c = lax.axis_index("x"), lax.axis_index("y"), lax.axis_index("c")
    return (x ^ (k >> 1), y ^ (k & 1), c)


def _all_gather_shards(shards):
    n = len(shards)

    def body(*refs):
        src, dst = refs[:n], refs[n:2 * n]
        send_sems, recv_sems, local_sems = refs[2 * n:]
        p = _chip_index()
        local = [pltpu.make_async_copy(src[a], dst[a].at[p], local_sems.at[a]) for a in range(n)]
        for cp in local:
            cp.start()
        remote = []
        for k in range(1, N_CHIPS):
            for a in range(n):
                cp = pltpu.make_async_remote_copy(
                    src_ref=src[a], dst_ref=dst[a].at[p], send_sem=send_sems.at[k - 1, a],
                    recv_sem=recv_sems.at[k - 1, a], device_id=_peer(k), device_id_type=MESH)
                cp.start()
                remote.append((k, a, cp))
        for k, a, cp in remote:
            pltpu.make_async_remote_copy(
                src_ref=src[a], dst_ref=dst[a].at[p ^ k], send_sem=send_sems.at[k - 1, a],
                recv_sem=recv_sems.at[k - 1, a], device_id=_peer(k), device_id_type=MESH).wait_recv()
        for k, a, cp in remote:
            cp.wait_send()
        for cp in local:
            cp.wait()

    return pl.pallas_call(
        body, name="all_gather_weights",
        in_specs=[ANY] * n, out_specs=[ANY] * n,
        out_shape=[jax.ShapeDtypeStruct((N_CHIPS,) + a.shape, a.dtype) for a in shards],
        scratch_shapes=[pltpu.SemaphoreType.DMA((N_CHIPS - 1, n)), pltpu.SemaphoreType.DMA((N_CHIPS - 1, n)),
                        pltpu.SemaphoreType.DMA((n,))],
        compiler_params=pltpu.CompilerParams(has_side_effects=True))(*shards)


def _exchange_slices(grads, whole):
    n = len(grads)

    def body(*refs):
        src, dst = refs[:n], refs[n:2 * n]
        send_sems, recv_sems, local_sems = refs[2 * n:]
        p = _chip_index()
        pick = lambda a, q: src[a].at[0] if whole[a] else src[a].at[q]
        local = [pltpu.make_async_copy(pick(a, p), dst[a].at[0], local_sems.at[a]) for a in range(n)]
        for cp in local:
            cp.start()
        remote = []
        for k in range(1, N_CHIPS):
            for a in range(n):
                cp = pltpu.make_async_remote_copy(
                    src_ref=pick(a, p ^ k), dst_ref=dst[a].at[k], send_sem=send_sems.at[k - 1, a],
                    recv_sem=recv_sems.at[k - 1, a], device_id=_peer(k), device_id_type=MESH)
                cp.start()
                remote.append(cp)
        for cp in remote:
            cp.wait_recv()
        for cp in remote:
            cp.wait_send()
        for cp in local:
            cp.wait()

    return pl.pallas_call(
        body, name="exchange_grad_slices",
        in_specs=[ANY] * n, out_specs=[ANY] * n,
        out_shape=[jax.ShapeDtypeStruct((N_CHIPS,) + g.shape[1:], g.dtype) for g in grads],
        scratch_shapes=[pltpu.SemaphoreType.DMA((N_CHIPS - 1, n)), pltpu.SemaphoreType.DMA((N_CHIPS - 1, n)),
                        pltpu.SemaphoreType.DMA((n,))],
        compiler_params=pltpu.CompilerParams(has_side_effects=True))(*grads)


def _sibling_swap(arrays):
    n = len(arrays)

    def body(*refs):
        src, dst = refs[:n], refs[n:2 * n]
        send_sems, recv_sems = refs[2 * n:]
        sibling = (lax.axis_index("x"), lax.axis_index("y"), 1 - lax.axis_index("c"))
        copies = [pltpu.make_async_remote_copy(src_ref=src[a], dst_ref=dst[a], send_sem=send_sems.at[a],
                                               recv_sem=recv_sems.at[a], device_id=sibling, device_id_type=MESH)
                  for a in range(n)]
        for cp in copies:
            cp.start()
        for cp in copies:
            cp.wait_recv()
        for cp in copies:
            cp.wait_send()

    return pl.pallas_call(
        body, name="sibling_swap",
        in_specs=[ANY] * n, out_specs=[ANY] * n,
        out_shape=[jax.ShapeDtypeStruct(a.shape, a.dtype) for a in arrays],
        scratch_shapes=[pltpu.SemaphoreType.DMA((n,)), pltpu.SemaphoreType.DMA((n,))],
        compiler_params=pltpu.CompilerParams(has_side_effects=True))(*arrays)


def _sum_slots(r, name):
    _, rows, cols = r.shape
    tm = 256 if rows % 256 == 0 else rows

    def body(r_ref, o_ref):
        o_ref[...] = (r_ref[0] + r_ref[1]) + (r_ref[2] + r_ref[3])

    return _call(
        body, name=name, grid=(rows // tm,),
        in_specs=[pl.BlockSpec((N_CHIPS, tm, cols), lambda i: (0, i, 0))],
        out_specs=pl.BlockSpec((tm, cols), lambda i: (i, 0)),
        out_shape=jax.ShapeDtypeStruct((rows, cols), f32))(r)


def _adamw(w, m, v, g_mine, g_sib, name, tm):
    rows, cols = w.shape
    gcols = g_mine.shape[1]
    if tm == rows:
        gspec = pl.BlockSpec(g_mine.shape, lambda i: (0, 0))
    else:
        gspec = pl.BlockSpec((tm, gcols), lambda i: (i, 0))

    def body(w_ref, m_ref, v_ref, ga_ref, gb_ref, g_ref, d_ref, nm_ref, nv_ref):
        g = (ga_ref[...] + gb_ref[...])[0:tm, 0:cols]
        w_, m_, v_ = w_ref[...], m_ref[...], v_ref[...]
        m_ = ADAM_B1 * m_ + (1.0 - ADAM_B1) * g
        v_ = ADAM_B2 * v_ + (1.0 - ADAM_B2) * jnp.square(g)
        m_hat = m_ / (1.0 - ADAM_B1 ** ADAM_STEP)
        v_hat = v_ / (1.0 - ADAM_B2 ** ADAM_STEP)
        g_ref[...] = g
        d_ref[...] = -ADAM_LR * (m_hat / (jnp.sqrt(v_hat) + ADAM_EPS) + ADAM_WD * w_)
        nm_ref[...] = m_
        nv_ref[...] = v_

    blk = pl.BlockSpec((tm, cols), lambda i: (i, 0))
    return _call(
        body, name=name, grid=(rows // tm,),
        in_specs=[blk, blk, blk, gspec, gspec], out_specs=[blk] * 4,
        out_shape=[jax.ShapeDtypeStruct((rows, cols), f32)] * 4)(w, m, v, g_mine, g_sib)


def _rope_tables(s):
    half = HEAD_DIM // 2
    inv_freq = 1.0 / (10000.0 ** (jnp.arange(half, dtype=f32) / half))
    ang = jnp.arange(s, dtype=f32)[:, None] * inv_freq[None, :]
    cos, sin = jnp.cos(ang), jnp.sin(ang)
    return jnp.tile(cos, (1, LANES // half)), jnp.tile(jnp.concatenate([-sin, sin], axis=1), (1, LANES // HEAD_DIM))


def _local_step(x, target, gw, b_gate, w_pool, pool_scale, ln1_g, ln1_b, convb_g, ln2_g, ln2_b):
    s = x.shape[0]
    nb = s // MOBA_BLOCK
    win_g, wba_g, wbp_g, wout_g, wg_g, wu_g, wd_g, convw_g = gw
    wba = wba_g.transpose(1, 0, 2).reshape(D_ATTN, D_MODEL)
    wbp = wbp_g.transpose(1, 0, 2).reshape(D_POOL, D_MODEL)
    wout = wout_g.reshape(D_MODEL, D_MODEL)
    wd = wd_g.reshape(D_FF_PAD, D_MODEL)
    cos_t, sin_t = _rope_tables(s)
    x_bf = x.astype(bf16)

    z = _inproj(x_bf, win_g, cos_t, sin_t, b_gate)
    o_bf, lse = _attn_fwd(z, nb)
    pooled_bf, mixed, pm_bf = _pool_fwd(z, w_pool, pool_scale)
    m_bf = _branch_merge(o_bf, pm_bf, z, wba, wbp)
    x1_bf, xhat1, rstd1 = _out_ln1(m_bf, wout, x, ln1_g, ln1_b)
    a, up, hh_bf = _ffn_up(x1_bf, wg_g, wu_g, convw_g, convb_g)
    loss, dh2, dh2_bf, d_ln2_g, d_ln2_b = _ffn_down_ln2_loss(hh_bf, wd, xhat1, ln1_g, ln1_b, ln2_g, ln2_b, target)

    da_bf, dup_bf, dconv = _ffn_act_bwd(dh2_bf, wd_g, a, up, convw_g, convb_g)
    d_wg = _matmul_tn(x1_bf, da_bf, N_CHIPS, "dw_ffn_gate")
    d_wu = _matmul_tn(x1_bf, dup_bf, N_CHIPS, "dw_ffn_up")
    d_wd = _matmul_tn(hh_bf, dh2_bf, 1, "dw_ffn_down").reshape(N_CHIPS, FF_PAD, D_MODEL)
    dh1, dh1_bf, d_ln1_g, d_ln1_b = _ffn_in_bwd_ln1(da_bf, dup_bf, wg_g, wu_g, dh2, xhat1, rstd1, ln1_g)
    d_wout = _matmul_tn(m_bf, dh1_bf, 1, "dw_out").reshape(N_CHIPS, D_MODEL // N_CHIPS, D_MODEL)
    dzg_bf, dya_bf, dyp_bf, do_bf, dpm, d_bgate = _merge_bwd(dh1_bf, wout, o_bf, pm_bf, z, wba, wbp)
    d_wba = _matmul_tn(o_bf, dya_bf, N_CHIPS, "dw_branch_attn")
    d_wbp = _matmul_tn(pm_bf, dyp_bf, N_CHIPS, "dw_branch_pool")
    du_bf, d_wpool, d_pscale = _pool_bwd(dpm, mixed, pooled_bf, w_pool, pool_scale)
    dq_bf, dk_bf, dv_bf = _attn_bwd(z, o_bf, lse, do_bf, cos_t, sin_t, nb)
    dz_bf = jnp.concatenate([dq_bf, dk_bf, dv_bf, du_bf, dzg_bf], axis=1)
    d_win = _matmul_tn(x_bf, dz_bf, N_CHIPS, "dw_in")
    grad_x = _inproj_dx(dz_bf, win_g, dh1)

    sharded = [d_win, d_wba, d_wbp, d_wout, d_wg, d_wu, d_wd, dconv]
    small = [d_bgate, d_wpool, d_pscale, d_ln1_g, d_ln1_b, d_ln2_g, d_ln2_b, dconv[:, 3, :]]
    return loss[0, 0], grad_x, sharded, small


SMALL_ROWS = 592


def _pack_small(parts):
    flat = jnp.concatenate([p.reshape(-1) for p in parts])
    return jnp.pad(flat, (0, SMALL_ROWS * LANES - flat.shape[0])).reshape(SMALL_ROWS, LANES)


def _unpack_small(packed, shapes):
    flat = packed.reshape(-1)
    out, off = [], 0
    for shp in shapes:
        n = 1
        for d in shp:
            n *= d
        out.append(flat[off:off + n].reshape(shp))
        off += n
    return out


def _pad_conv_b(cb):
    return jnp.pad(cb.reshape(N_CHIPS, FF_SHARD), ((0, 0), (0, FF_PAD - FF_SHARD)))


def kernel(x, w_in, b_gate, w_branch_attn, w_pool, pool_scale, w_branch_pool, w_out, ln1_g, ln1_b, w_ffn_gate, w_ffn_up, conv_w, conv_b, w_ffn_down, ln2_g, ln2_b, loss_target, m_w_in, m_b_gate, m_w_branch_attn, m_w_pool, m_pool_scale, m_w_branch_pool, m_w_out, m_ln1_g, m_ln1_b, m_w_ffn_gate, m_w_ffn_up, m_conv_w, m_conv_b, m_w_ffn_down, m_ln2_g, m_ln2_b, v_w_in, v_b_gate, v_w_branch_attn, v_w_pool, v_pool_scale, v_w_branch_pool, v_w_out, v_ln1_g, v_ln1_b, v_w_ffn_gate, v_w_ffn_up, v_conv_w, v_conv_b, v_w_ffn_down, v_ln2_g, v_ln2_b):
    pad_cols = lambda w: jnp.pad(w, ((0, 0), (0, FF_PAD - FF_SHARD)))
    shards = [w_in[0].astype(bf16), w_branch_attn[0].astype(bf16), w_branch_pool[0].astype(bf16),
              w_out[0].astype(bf16), pad_cols(w_ffn_gate[0].astype(bf16)), pad_cols(w_ffn_up[0].astype(bf16)),
              jnp.pad(w_ffn_down[0].astype(bf16), ((0, FF_PAD - FF_SHARD), (0, 0))),
              jnp.pad(conv_w[0], ((0, 5), (0, FF_PAD - FF_SHARD)))]
    gathered = _all_gather_shards(shards)

    convb_g = _pad_conv_b(conv_b).reshape(N_CHIPS, 1, FF_PAD)
    loss, grad_x, sharded, small = _local_step(
        x[0], loss_target[0], gathered, b_gate, w_pool[0], pool_scale, ln1_g, ln1_b, convb_g, ln2_g, ln2_b)
    loss = lax.psum(loss, ("x", "y", "c"))

    small_packed = _pack_small(small)[None]
    slots = _exchange_slices(sharded + [small_packed], [False] * len(sharded) + [True])
    names = ["w_in", "w_branch_attn", "w_branch_pool", "w_out", "w_ffn_gate", "w_ffn_up", "w_ffn_down", "conv_w", "small"]
    mine = [_sum_slots(r, "sum_" + nm) for r, nm in zip(slots, names)]
    theirs = _sibling_swap(mine)

    weights = [w_in[0], w_branch_attn[0], w_branch_pool[0], w_out[0], w_ffn_gate[0], w_ffn_up[0], w_ffn_down[0], conv_w[0]]
    m_in = [m_w_in[0], m_w_branch_attn[0], m_w_branch_pool[0], m_w_out[0], m_w_ffn_gate[0], m_w_ffn_up[0], m_w_ffn_down[0], m_conv_w[0]]
    v_in = [v_w_in[0], v_w_branch_attn[0], v_w_branch_pool[0], v_w_out[0], v_w_ffn_gate[0], v_w_ffn_up[0], v_w_ffn_down[0], v_conv_w[0]]
    tiles = [256, 256, 256, 128, 256, 256, 88, 3]
    res = {}
    for i, nm in enumerate(names[:-1]):
        res[nm] = [r[None] for r in _adamw(weights[i], m_in[i], v_in[i], mine[i], theirs[i], "adamw_" + nm, tiles[i])]

    small_names = ["b_gate", "w_pool", "pool_scale", "ln1_g", "ln1_b", "ln2_g", "ln2_b", "conv_b"]
    small_w = [b_gate, w_pool, pool_scale, ln1_g, ln1_b, ln2_g, ln2_b, _pad_conv_b(conv_b)]
    small_m = [m_b_gate, m_w_pool, m_pool_scale, m_ln1_g, m_ln1_b, m_ln2_g, m_ln2_b, _pad_conv_b(m_conv_b)]
    small_v = [v_b_gate, v_w_pool, v_pool_scale, v_ln1_g, v_ln1_b, v_ln2_g, v_ln2_b, _pad_conv_b(v_conv_b)]
    packed = _adamw(_pack_small(small_w), _pack_small(small_m), _pack_small(small_v), mine[-1], theirs[-1],
                    "adamw_small", SMALL_ROWS)
    shapes = [w.shape for w in small_w]
    for kind in range(4):
        for nm, val in zip(small_names, _unpack_small(packed[kind], shapes)):
            if nm == "conv_b":
                val = val[:, :FF_SHARD].reshape(1, 4 * FF_SHARD)
            res.setdefault(nm, [None] * 4)[kind] = val

    order = ["w_in", "b_gate", "w_branch_attn", "w_pool", "pool_scale", "w_branch_pool", "w_out", "ln1_g", "ln1_b",
             "w_ffn_gate", "w_ffn_up", "conv_w", "conv_b", "w_ffn_down", "ln2_g", "ln2_b"]
    outs = [loss, grad_x[None]]
    for kind in range(4):
        outs += [res[nm][kind] for nm in order]
    return tuple(outs)
```

```python
import functools

import jax
import jax.numpy as jnp
from jax import lax
from jax.experimental import pallas as pl
from jax.experimental.pallas import tpu as pltpu

D_MODEL = 1024
HEAD_DIM = 64
D_ATTN = 512
D_POOL = 512
MOBA_BLOCK = 256
MOBA_TOPK = 3
POOL_GROUP = 128
MAX_WINDOW = 16
FF_SHARD = 704
FF_PAD = 768
D_FF_PAD = 4 * FF_PAD
N_CHIPS = 4
LANES = 128
ALPHA = (2.0 * 1) ** 0.25
LN_EPS = 1e-5
NEG = -1e30
SCALE = HEAD_DIM ** -0.5
ADAM_LR, ADAM_B1, ADAM_B2, ADAM_EPS, ADAM_WD, ADAM_STEP = 0.001, 0.9, 0.999, 1e-08, 0.01, 10
VMEM_LIMIT = 56 * 1024 * 1024
MESH = pl.DeviceIdType.MESH

bf16 = jnp.bfloat16
f32 = jnp.float32


def _dot(a, b):
    return jnp.dot(a, b, preferred_element_type=f32)


def _dot_nt(a, b):
    return lax.dot_general(a, b, (((1,), (1,)), ((), ())), preferred_element_type=f32)


def _dot_tn(a, b):
    return lax.dot_general(a, b, (((0,), (0,)), ((), ())), preferred_element_type=f32)


def _call(body, *, name, grid, in_specs, out_specs, out_shape, scratch=()):
    return pl.pallas_call(
        body, name=name, grid=grid, in_specs=in_specs, out_specs=out_specs, out_shape=out_shape,
        scratch_shapes=list(scratch),
        compiler_params=pltpu.CompilerParams(dimension_semantics=("arbitrary",) * len(grid),
                                             vmem_limit_bytes=VMEM_LIMIT))


def _rot_half(t):
    lane = lax.broadcasted_iota(jnp.int32, t.shape, 1)
    first = (lane % HEAD_DIM) < (HEAD_DIM // 2)
    return jnp.where(first, pltpu.roll(t, LANES - HEAD_DIM // 2, 1), pltpu.roll(t, HEAD_DIM // 2, 1))


def _rope(t, cos, sin_signed):
    return t * cos + _rot_half(t) * sin_signed


def _rope_bwd(d, cos, sin_signed):
    return d * cos + _rot_half(d * sin_signed)


def _gelu_parts(a):
    cdf = 0.5 * (1.0 + lax.erf(a * (2.0 ** -0.5)))
    pdf = jnp.exp(-0.5 * a * a) * ((2.0 * jnp.pi) ** -0.5)
    return a * cdf, cdf + a * pdf


def _layer_norm(h, g, b):
    mu = jnp.mean(h, axis=-1, keepdims=True)
    xc = h - mu
    var = jnp.mean(xc * xc, axis=-1, keepdims=True)
    rstd = lax.rsqrt(var + LN_EPS)
    xhat = xc * rstd
    return xhat * g + b, xhat, rstd


def _layer_norm_bwd(dy, xhat, rstd, g):
    dxh = dy * g
    m1 = jnp.mean(dxh, axis=-1, keepdims=True)
    m2 = jnp.mean(dxh * xhat, axis=-1, keepdims=True)
    return rstd * (dxh - m1 - xhat * m2)


def _inproj(x_bf, win_g, cos_t, sin_t, b_gate):
    s = x_bf.shape[0]
    tm, tn = 512, 512

    def body(x_ref, w_ref, cos_ref, sin_ref, b_ref, o_ref):
        j = pl.program_id(1)
        acc = _dot(x_ref[...], w_ref[...])

        @pl.when(j < 2)
        def _():
            for c in range(tn // LANES):
                sl = slice(c * LANES, (c + 1) * LANES)
                o_ref[:, sl] = _rope(acc[:, sl], cos_ref[...], sin_ref[...])

        @pl.when((j >= 2) & (j < 4))
        def _():
            o_ref[...] = acc

        @pl.when(j >= 4)
        def _():
            o_ref[...] = jax.nn.sigmoid(acc + b_ref[...])

    return _call(
        body, name="inproj", grid=(s // tm, 4 * D_MODEL // tn),
        in_specs=[pl.BlockSpec((tm, D_MODEL), lambda i, j: (i, 0)),
                  pl.BlockSpec((None, D_MODEL, tn), lambda i, j: (j // 2, 0, j % 2)),
                  pl.BlockSpec((tm, LANES), lambda i, j: (i, 0)),
                  pl.BlockSpec((tm, LANES), lambda i, j: (i, 0)),
                  pl.BlockSpec((1, tn), lambda i, j: (0, jnp.maximum(j - 4, 0)))],
        out_specs=pl.BlockSpec((tm, tn), lambda i, j: (i, j)),
        out_shape=jax.ShapeDtypeStruct((s, 4 * D_MODEL), f32))(x_bf, win_g, cos_t, sin_t, b_gate)


def _moba_select(q, k_all, i, nb):
    lane = lax.broadcasted_iota(jnp.int32, q.shape, 1)
    k_mean = jnp.mean(k_all.reshape(nb, MOBA_BLOCK, LANES), axis=1).astype(bf16)
    n_io = lax.broadcasted_iota(jnp.int32, (MOBA_BLOCK, nb), 1)
    past = n_io < i
    sels = []
    for h in range(2):
        head = (lane < HEAD_DIM) if h == 0 else (lane >= HEAD_DIM)
        gate = _dot_nt(jnp.where(head, q, 0.0).astype(bf16), k_mean)
        g = jnp.where(past, gate, NEG)
        rank = jnp.zeros((MOBA_BLOCK, nb), f32)
        for m in range(nb):
            gm = g[:, m:m + 1]
            rank = rank + jnp.where((gm > g) | ((gm == g) & (m < n_io)), 1.0, 0.0)
        sels.append(jnp.where(past & (rank < MOBA_TOPK), 1.0, 0.0))
    return sels


def _attn_fwd(z, nb):
    s = z.shape[0]

    def body(q_ref, k_ref, v_ref, o_ref, lse_ref, m_s, l_s, acc_s):
        i = pl.program_id(1)
        lane = lax.broadcasted_iota(jnp.int32, (MOBA_BLOCK, LANES), 1)
        head0 = lane < HEAD_DIM
        heads = (head0, jnp.logical_not(head0))
        q = q_ref[...]
        sels = _moba_select(q, k_ref[...], i, nb)
        qs = q * SCALE
        qb = [jnp.where(hm, qs, 0.0).astype(bf16) for hm in heads]
        row = lax.broadcasted_iota(jnp.int32, (MOBA_BLOCK, MOBA_BLOCK), 0)
        col = lax.broadcasted_iota(jnp.int32, (MOBA_BLOCK, MOBA_BLOCK), 1)

        own = pl.ds(pl.multiple_of(i * MOBA_BLOCK, MOBA_BLOCK), MOBA_BLOCK)
        kj = k_ref[own, :].astype(bf16)
        vj = v_ref[own, :]
        ms, ls, pv = [], [], None
        for h in range(2):
            sc = jnp.where(col <= row, _dot_nt(qb[h], kj), NEG)
            m = jnp.max(sc, axis=1, keepdims=True)
            p = jnp.exp(sc - m)
            ms.append(m)
            ls.append(jnp.sum(p, axis=1, keepdims=True))
            t = _dot(p.astype(bf16), jnp.where(heads[h], vj, 0.0).astype(bf16))
            pv = t if pv is None else pv + t
        m_s[...] = jnp.where(head0, ms[0], ms[1])
        l_s[...] = jnp.where(head0, ls[0], ls[1])
        acc_s[...] = pv

        for j in range(nb - 1):
            @pl.when(j < i)
            def _(j=j):
                kj = k_ref[j * MOBA_BLOCK:(j + 1) * MOBA_BLOCK, :].astype(bf16)
                vj = v_ref[j * MOBA_BLOCK:(j + 1) * MOBA_BLOCK, :]
                m_old_l, l_old_l = m_s[...], l_s[...]
                m_new, alpha, l_new, pv = [], [], [], None
                for h in range(2):
                    c0 = h * HEAD_DIM
                    sc = jnp.where(sels[h][:, j:j + 1] > 0.0, _dot_nt(qb[h], kj), NEG)
                    m_old = m_old_l[:, c0:c0 + 1]
                    mn = jnp.maximum(m_old, jnp.max(sc, axis=1, keepdims=True))
                    a = jnp.exp(m_old - mn)
                    p = jnp.exp(sc - mn)
                    m_new.append(mn)
                    alpha.append(a)
                    l_new.append(a * l_old_l[:, c0:c0 + 1] + jnp.sum(p, axis=1, keepdims=True))
                    t = _dot(p.astype(bf16), jnp.where(heads[h], vj, 0.0).astype(bf16))
                    pv = t if pv is None else pv + t
                m_s[...] = jnp.where(head0, m_new[0], m_new[1])
                l_s[...] = jnp.where(head0, l_new[0], l_new[1])
                acc_s[...] = acc_s[...] * jnp.where(head0, alpha[0], alpha[1]) + pv

        o_ref[...] = (acc_s[...] / l_s[...]).astype(bf16)
        lse_ref[...] = m_s[...] + jnp.log(l_s[...])

    return _call(
        body, name="attn_fwd", grid=(D_ATTN // LANES, nb),
        in_specs=[pl.BlockSpec((MOBA_BLOCK, LANES), lambda hp, i: (i, hp)),
                  pl.BlockSpec((s, LANES), lambda hp, i: (0, 4 + hp)),
                  pl.BlockSpec((s, LANES), lambda hp, i: (0, 8 + hp))],
        out_specs=[pl.BlockSpec((MOBA_BLOCK, LANES), lambda hp, i: (i, hp)),
                   pl.BlockSpec((None, MOBA_BLOCK, LANES), lambda hp, i: (hp, i, 0))],
        out_shape=[jax.ShapeDtypeStruct((s, D_ATTN), bf16),
                   jax.ShapeDtypeStruct((D_ATTN // LANES, s, LANES), f32)],
        scratch=[pltpu.VMEM((MOBA_BLOCK, LANES), f32)] * 3)(z, z, z)


def _window_select(snaps, g):
    return jnp.where(g == 0, snaps[0], jnp.where(g == 1, snaps[1], jnp.where(g == 2, snaps[2], snaps[3])))


def _pool_fwd(z, w_pool, pool_scale):
    s = z.shape[0]

    def body(u_ref, w_ref, sc_ref, pooled_ref, mixed_ref, pm_ref, pad):
        g = pl.program_id(0)
        u = u_ref[...]
        pad[0:MAX_WINDOW, :] = jnp.zeros((MAX_WINDOW, POOL_GROUP), f32)
        pad[MAX_WINDOW:MAX_WINDOW + s, :] = u
        acc = u
        snaps = []
        for d in range(1, MAX_WINDOW):
            acc = acc + pad[MAX_WINDOW - d:MAX_WINDOW - d + s, :]
            if d + 1 in (2, 4, 8, 16):
                snaps.append(acc)
        win = _window_select(snaps, g)
        t = lax.broadcasted_iota(jnp.int32, (s, POOL_GROUP), 0)
        count = jnp.minimum(t + 1, jnp.left_shift(2, g)).astype(f32)
        pooled = (win / count - u).astype(bf16)
        mixed = _dot(pooled, w_ref[...].astype(bf16))
        pooled_ref[...] = pooled
        mixed_ref[...] = mixed
        pm_ref[...] = (mixed * sc_ref[...]).astype(bf16)

    blk = pl.BlockSpec((s, POOL_GROUP), lambda g: (0, g))
    return _call(
        body, name="pool_fwd", grid=(4,),
        in_specs=[pl.BlockSpec((s, POOL_GROUP), lambda g: (0, 12 + g)),
                  pl.BlockSpec((None, POOL_GROUP, POOL_GROUP), lambda g: (g, 0, 0)),
                  pl.BlockSpec((1, POOL_GROUP), lambda g: (0, g))],
        out_specs=[blk, blk, blk],
        out_shape=[jax.ShapeDtypeStruct((s, D_POOL), bf16), jax.ShapeDtypeStruct((s, D_POOL), f32),
                   jax.ShapeDtypeStruct((s, D_POOL), bf16)],
        scratch=[pltpu.VMEM((s + MAX_WINDOW, POOL_GROUP), f32)])(z, w_pool, pool_scale)


def _branch_merge(o_bf, pm_bf, z, wba, wbp):
    s = o_bf.shape[0]
    tm = 256

    def body(o_ref, pm_ref, ga_ref, gp_ref, wba_ref, wbp_ref, m_ref):
        ya = _dot(o_ref[...], wba_ref[...])
        yp = _dot(pm_ref[...], wbp_ref[...])
        m_ref[...] = (ga_ref[...] * ya + gp_ref[...] * yp).astype(bf16)

    full = lambda r, c: pl.BlockSpec((r, c), lambda i: (0, 0))
    return _call(
        body, name="branch_merge", grid=(s // tm,),
        in_specs=[pl.BlockSpec((tm, D_ATTN), lambda i: (i, 0)), pl.BlockSpec((tm, D_POOL), lambda i: (i, 0)),
                  pl.BlockSpec((tm, D_MODEL), lambda i: (i, 2)), pl.BlockSpec((tm, D_MODEL), lambda i: (i, 3)),
                  full(D_ATTN, D_MODEL), full(D_POOL, D_MODEL)],
        out_specs=pl.BlockSpec((tm, D_MODEL), lambda i: (i, 0)),
        out_shape=jax.ShapeDtypeStruct((s, D_MODEL), bf16))(o_bf, pm_bf, z, z, wba, wbp)


def _out_ln1(m_bf, wout, x, ln_g, ln_b):
    s = x.shape[0]
    tm = 256

    def body(m_ref, w_ref, x_ref, g_ref, b_ref, x1_ref, xhat_ref, rstd_ref):
        h = ALPHA * x_ref[...] + _dot(m_ref[...], w_ref[...])
        y, xhat, rstd = _layer_norm(h, g_ref[...], b_ref[...])
        x1_ref[...] = y.astype(bf16)
        xhat_ref[...] = xhat
        rstd_ref[...] = jnp.broadcast_to(rstd, (tm, LANES))

    row = pl.BlockSpec((tm, D_MODEL), lambda i: (i, 0))
    vec = pl.BlockSpec((1, D_MODEL), lambda i: (0, 0))
    return _call(
        body, name="out_ln1", grid=(s // tm,),
        in_specs=[row, pl.BlockSpec((D_MODEL, D_MODEL), lambda i: (0, 0)), row, vec, vec],
        out_specs=[row, row, pl.BlockSpec((tm, LANES), lambda i: (i, 0))],
        out_shape=[jax.ShapeDtypeStruct((s, D_MODEL), bf16), jax.ShapeDtypeStruct((s, D_MODEL), f32),
                   jax.ShapeDtypeStruct((s, LANES), f32)])(m_bf, wout, x, ln_g, ln_b)


FF_TILE = 256
FF_TILES_PER_SHARD = FF_PAD // FF_TILE
CONV_PAD = 8
CONV_ROWS = 16


def _ff_weight_spec(rows):
    return pl.BlockSpec((None, rows, FF_TILE), lambda j: (j // FF_TILES_PER_SHARD, 0, j % FF_TILES_PER_SHARD))


def _conv_taps(pad, s, cw_ref, off=0):
    return (pad[CONV_PAD + off:CONV_PAD + off + s, :] * cw_ref[2:3, :]
            + pad[CONV_PAD - 1 + off:CONV_PAD - 1 + off + s, :] * cw_ref[1:2, :]
            + pad[CONV_PAD - 2 + off:CONV_PAD - 2 + off + s, :] * cw_ref[0:1, :])


def _ffn_up(x1_bf, wg_g, wu_g, convw_g, convb_g):
    s = x1_bf.shape[0]

    def body(x_ref, wg_ref, wu_ref, cw_ref, cb_ref, a_ref, up_ref, hh_ref, pad):
        x1 = x_ref[...]
        a = _dot(x1, wg_ref[...])
        up = _dot(x1, wu_ref[...])
        a_ref[...] = a
        up_ref[...] = up
        pad[0:CONV_PAD, :] = jnp.zeros((CONV_PAD, FF_TILE), f32)
        pad[CONV_PAD:CONV_PAD + s, :] = a
        ac = _conv_taps(pad, s, cw_ref) + cb_ref[...]
        hg, _ = _gelu_parts(ac)
        hh_ref[...] = (hg * up).astype(bf16)

    col = pl.BlockSpec((s, FF_TILE), lambda j: (0, j))
    return _call(
        body, name="ffn_up", grid=(D_FF_PAD // FF_TILE,),
        in_specs=[pl.BlockSpec((s, D_MODEL), lambda j: (0, 0)), _ff_weight_spec(D_MODEL), _ff_weight_spec(D_MODEL),
                  _ff_weight_spec(8), _ff_weight_spec(1)],
        out_specs=[col, col, col],
        out_shape=[jax.ShapeDtypeStruct((s, D_FF_PAD), f32), jax.ShapeDtypeStruct((s, D_FF_PAD), f32),
                   jax.ShapeDtypeStruct((s, D_FF_PAD), bf16)],
        scratch=[pltpu.VMEM((s + CONV_PAD, FF_TILE), f32)])(x1_bf, wg_g, wu_g, convw_g, convb_g)


def _ffn_down_ln2_loss(hh_bf, wd, xhat1, ln1_g, ln1_b, ln2_g, ln2_b, target):
    s = hh_bf.shape[0]
    tm = 256

    def body(hh_ref, w_ref, xh1_ref, g1_ref, b1_ref, g2_ref, b2_ref, t_ref, loss_ref, dh_ref, dhb_ref, dg_ref, db_ref):
        i = pl.program_id(0)
        x1 = xh1_ref[...] * g1_ref[...] + b1_ref[...]
        h = ALPHA * x1 + _dot(hh_ref[...], w_ref[...])
        y, xhat, rstd = _layer_norm(h, g2_ref[...], b2_ref[...])
        err = y - t_ref[...]
        part = 0.5 * jnp.sum(jnp.mean(err * err, axis=-1, keepdims=True), axis=0, keepdims=True)
        dy = err * (1.0 / D_MODEL)

        @pl.when(i == 0)
        def _():
            loss_ref[...] = jnp.zeros_like(loss_ref)
            dg_ref[...] = jnp.zeros_like(dg_ref)
            db_ref[...] = jnp.zeros_like(db_ref)

        loss_ref[...] += jnp.broadcast_to(part, loss_ref.shape)
        dg_ref[...] += jnp.sum(dy * xhat, axis=0, keepdims=True)
        db_ref[...] += jnp.sum(dy, axis=0, keepdims=True)
        dh = _layer_norm_bwd(dy, xhat, rstd, g2_ref[...])
        dh_ref[...] = dh
        dhb_ref[...] = dh.astype(bf16)

    row = pl.BlockSpec((tm, D_MODEL), lambda i: (i, 0))
    vec = pl.BlockSpec((1, D_MODEL), lambda i: (0, 0))
    return _call(
        body, name="ffn_down_ln2_loss", grid=(s // tm,),
        in_specs=[pl.BlockSpec((tm, D_FF_PAD), lambda i: (i, 0)), pl.BlockSpec((D_FF_PAD, D_MODEL), lambda i: (0, 0)),
                  row, vec, vec, vec, vec, row],
        out_specs=[pl.BlockSpec((8, LANES), lambda i: (0, 0)), row, row, vec, vec],
        out_shape=[jax.ShapeDtypeStruct((8, LANES), f32), jax.ShapeDtypeStruct((s, D_MODEL), f32),
                   jax.ShapeDtypeStruct((s, D_MODEL), bf16), jax.ShapeDtypeStruct((1, D_MODEL), f32),
                   jax.ShapeDtypeStruct((1, D_MODEL), f32)])(hh_bf, wd, xhat1, ln1_g, ln1_b, ln2_g, ln2_b, target)


def _ffn_act_bwd(dh2_bf, wd_g, a, up, convw_g, convb_g):
    s = dh2_bf.shape[0]

    def body(dh_ref, wd_ref, a_ref, up_ref, cw_ref, cb_ref, da_ref, dup_ref, dc_ref, pad, dpad):
        dhh = _dot_nt(dh_ref[...], wd_ref[...])
        zeros = jnp.zeros((CONV_PAD, FF_TILE), f32)
        pad[0:CONV_PAD, :] = zeros
        pad[CONV_PAD:CONV_PAD + s, :] = a_ref[...]
        ac = _conv_taps(pad, s, cw_ref) + cb_ref[...]
        hg, dgelu = _gelu_parts(ac)
        up = up_ref[...]
        dup_ref[...] = (dhh * hg).astype(bf16)
        dac = dhh * up * dgelu
        rows = [jnp.sum(dac * pad[CONV_PAD - 2 + i:CONV_PAD - 2 + i + s, :], axis=0, keepdims=True) for i in range(3)]
        rows.append(jnp.sum(dac, axis=0, keepdims=True))
        rows.append(jnp.zeros((CONV_ROWS - 4, FF_TILE), f32))
        dc_ref[...] = jnp.concatenate(rows, axis=0)
        dpad[0:s, :] = dac
        dpad[s:s + CONV_PAD, :] = zeros
        da = (dpad[0:s, :] * cw_ref[2:3, :] + dpad[1:1 + s, :] * cw_ref[1:2, :] + dpad[2:2 + s, :] * cw_ref[0:1, :])
        da_ref[...] = da.astype(bf16)

    col = pl.BlockSpec((s, FF_TILE), lambda j: (0, j))
    return _call(
        body, name="ffn_act_bwd", grid=(D_FF_PAD // FF_TILE,),
        in_specs=[pl.BlockSpec((s, D_MODEL), lambda j: (0, 0)),
                  pl.BlockSpec((None, FF_TILE, D_MODEL),
                               lambda j: (j // FF_TILES_PER_SHARD, j % FF_TILES_PER_SHARD, 0)),
                  col, col, _ff_weight_spec(8), _ff_weight_spec(1)],
        out_specs=[col, col, _ff_weight_spec(CONV_ROWS)],
        out_shape=[jax.ShapeDtypeStruct((s, D_FF_PAD), bf16), jax.ShapeDtypeStruct((s, D_FF_PAD), bf16),
                   jax.ShapeDtypeStruct((N_CHIPS, CONV_ROWS, FF_PAD), f32)],
        scratch=[pltpu.VMEM((s + CONV_PAD, FF_TILE), f32), pltpu.VMEM((s + CONV_PAD, FF_TILE), f32)],
    )(dh2_bf, wd_g, a, up, convw_g, convb_g)


def _matmul_tn(a, b, n_shards, name, tm=512, tn=256):
    k, m = a.shape
    n = b.shape[1]
    tm, tn = min(tm, m), min(tn, n // n_shards)
    per = n // n_shards // tn

    def body(a_ref, b_ref, o_ref):
        o_ref[...] = _dot_tn(a_ref[...], b_ref[...]).astype(bf16)

    return _call(
        body, name=name, grid=(m // tm, n // tn),
        in_specs=[pl.BlockSpec((k, tm), lambda i, j: (0, i)), pl.BlockSpec((k, tn), lambda i, j: (0, j))],
        out_specs=pl.BlockSpec((None, tm, tn), lambda i, j: (j // per, i, j % per)),
        out_shape=jax.ShapeDtypeStruct((n_shards, m, n // n_shards), bf16))(a, b)


def _ffn_in_bwd_ln1(da_bf, dup_bf, wg_g, wu_g, dh2, xhat1, rstd1, ln1_g):
    s = da_bf.shape[0]
    tm = 256

    def body(da_ref, dup_ref, wg_ref, wu_ref, dh2_ref, xh_ref, rstd_ref, g_ref, dh_ref, dhb_ref, dg_ref, db_ref):
        i = pl.program_id(0)
        dx1 = ALPHA * dh2_ref[...]
        for sh in range(N_CHIPS):
            sl = slice(sh * FF_PAD, (sh + 1) * FF_PAD)
            dx1 = dx1 + _dot_nt(da_ref[:, sl], wg_ref[sh]) + _dot_nt(dup_ref[:, sl], wu_ref[sh])
        xhat = xh_ref[...]

        @pl.when(i == 0)
        def _():
            dg_ref[...] = jnp.zeros_like(dg_ref)
            db_ref[...] = jnp.zeros_like(db_ref)

        dg_ref[...] += jnp.sum(dx1 * xhat, axis=0, keepdims=True)
        db_ref[...] += jnp.sum(dx1, axis=0, keepdims=True)
        dh = _layer_norm_bwd(dx1, xhat, rstd_ref[:, 0:1], g_ref[...])
        dh_ref[...] = dh
        dhb_ref[...] = dh.astype(bf16)

    row = pl.BlockSpec((tm, D_MODEL), lambda i: (i, 0))
    wide = pl.BlockSpec((tm, D_FF_PAD), lambda i: (i, 0))
    wfull = pl.BlockSpec((N_CHIPS, D_MODEL, FF_PAD), lambda i: (0, 0, 0))
    vec = pl.BlockSpec((1, D_MODEL), lambda i: (0, 0))
    return _call(
        body, name="ffn_in_bwd_ln1", grid=(s // tm,),
        in_specs=[wide, wide, wfull, wfull, row, row, pl.BlockSpec((tm, LANES), lambda i: (i, 0)), vec],
        out_specs=[row, row, vec, vec],
        out_shape=[jax.ShapeDtypeStruct((s, D_MODEL), f32), jax.ShapeDtypeStruct((s, D_MODEL), bf16),
                   jax.ShapeDtypeStruct((1, D_MODEL), f32), jax.ShapeDtypeStruct((1, D_MODEL), f32)],
    )(da_bf, dup_bf, wg_g, wu_g, dh2, xhat1, rstd1, ln1_g)


def _merge_bwd(dh1_bf, wout, o_bf, pm_bf, z, wba, wbp):
    s = dh1_bf.shape[0]
    tm = 256

    def body(dh_ref, wout_ref, o_ref, pm_ref, ga_ref, gp_ref, wba_ref, wbp_ref,
             dzg_ref, dya_ref, dyp_ref, do_ref, dpm_ref, dbg_ref):
        i = pl.program_id(0)
        dm = _dot_nt(dh_ref[...], wout_ref[...])
        ya = _dot(o_ref[...], wba_ref[...])
        yp = _dot(pm_ref[...], wbp_ref[...])
        ga, gp = ga_ref[...], gp_ref[...]
        dza = dm * ya * ga * (1.0 - ga)
        dzp = dm * yp * gp * (1.0 - gp)

        @pl.when(i == 0)
        def _():
            dbg_ref[...] = jnp.zeros_like(dbg_ref)

        dbg_ref[:, 0:D_MODEL] += jnp.sum(dza, axis=0, keepdims=True)
        dbg_ref[:, D_MODEL:2 * D_MODEL] += jnp.sum(dzp, axis=0, keepdims=True)
        dzg_ref[:, 0:D_MODEL] = dza.astype(bf16)
        dzg_ref[:, D_MODEL:2 * D_MODEL] = dzp.astype(bf16)
        dya = (dm * ga).astype(bf16)
        dyp = (dm * gp).astype(bf16)
        dya_ref[...] = dya
        dyp_ref[...] = dyp
        do_ref[...] = _dot_nt(dya, wba_ref[...]).astype(bf16)
        dpm_ref[...] = _dot_nt(dyp, wbp_ref[...])

    row = pl.BlockSpec((tm, D_MODEL), lambda i: (i, 0))
    half = pl.BlockSpec((tm, D_ATTN), lambda i: (i, 0))
    full = lambda r, c: pl.BlockSpec((r, c), lambda i: (0, 0))
    return _call(
        body, name="merge_bwd", grid=(s // tm,),
        in_specs=[row, full(D_MODEL, D_MODEL), half, half,
                  pl.BlockSpec((tm, D_MODEL), lambda i: (i, 2)), pl.BlockSpec((tm, D_MODEL), lambda i: (i, 3)),
                  full(D_ATTN, D_MODEL), full(D_POOL, D_MODEL)],
        out_specs=[pl.BlockSpec((tm, 2 * D_MODEL), lambda i: (i, 0)), row, row, half, half,
                   pl.BlockSpec((1, 2 * D_MODEL), lambda i: (0, 0))],
        out_shape=[jax.ShapeDtypeStruct((s, 2 * D_MODEL), bf16), jax.ShapeDtypeStruct((s, D_MODEL), bf16),
                   jax.ShapeDtypeStruct((s, D_MODEL), bf16), jax.ShapeDtypeStruct((s, D_ATTN), bf16),
                   jax.ShapeDtypeStruct((s, D_POOL), f32), jax.ShapeDtypeStruct((1, 2 * D_MODEL), f32)],
    )(dh1_bf, wout, o_bf, pm_bf, z, z, wba, wbp)


def _pool_bwd(dpm, mixed, pooled_bf, w_pool, pool_scale):
    s = dpm.shape[0]

    def body(dpm_ref, mixed_ref, pooled_ref, w_ref, sc_ref, du_ref, dw_ref, dsc_ref, pad):
        g = pl.program_id(0)
        dpm_v = dpm_ref[...]
        dsc_ref[...] = jnp.sum(dpm_v * mixed_ref[...], axis=0, keepdims=True)
        dmixed = (dpm_v * sc_ref[...]).astype(bf16)
        dw_ref[...] = _dot_tn(pooled_ref[...], dmixed)
        dpooled = _dot_nt(dmixed, w_ref[...].astype(bf16))
        t = lax.broadcasted_iota(jnp.int32, (s, POOL_GROUP), 0)
        count = jnp.minimum(t + 1, jnp.left_shift(2, g)).astype(f32)
        r = dpooled / count
        pad[0:s, :] = r
        pad[s:s + MAX_WINDOW, :] = jnp.zeros((MAX_WINDOW, POOL_GROUP), f32)
        acc = r
        snaps = []
        for d in range(1, MAX_WINDOW):
            acc = acc + pad[d:d + s, :]
            if d + 1 in (2, 4, 8, 16):
                snaps.append(acc)
        du_ref[...] = (_window_select(snaps, g) - dpooled).astype(bf16)

    blk = pl.BlockSpec((s, POOL_GROUP), lambda g: (0, g))
    return _call(
        body, name="pool_bwd", grid=(4,),
        in_specs=[blk, blk, blk, pl.BlockSpec((None, POOL_GROUP, POOL_GROUP), lambda g: (g, 0, 0)),
                  pl.BlockSpec((1, POOL_GROUP), lambda g: (0, g))],
        out_specs=[blk, pl.BlockSpec((None, POOL_GROUP, POOL_GROUP), lambda g: (g, 0, 0)),
                   pl.BlockSpec((1, POOL_GROUP), lambda g: (0, g))],
        out_shape=[jax.ShapeDtypeStruct((s, D_POOL), bf16), jax.ShapeDtypeStruct((4, POOL_GROUP, POOL_GROUP), f32),
                   jax.ShapeDtypeStruct((1, D_POOL), f32)],
        scratch=[pltpu.VMEM((s + MAX_WINDOW, POOL_GROUP), f32)])(dpm, mixed, pooled_bf, w_pool, pool_scale)


def _attn_bwd(z, o_bf, lse, do_bf, cos_t, sin_t, nb):
    s = z.shape[0]

    def body(q_ref, k_ref, v_ref, o_ref, lse_ref, do_ref, cq_ref, sq_ref, cf_ref, sf_ref,
             dq_ref, dk_ref, dv_ref, dq_s, dk_s, dv_s):
        i = pl.program_id(1)
        lane = lax.broadcasted_iota(jnp.int32, (MOBA_BLOCK, LANES), 1)
        head0 = lane < HEAD_DIM
        heads = (head0, jnp.logical_not(head0))

        @pl.when(i == 0)
        def _():
            dk_s[...] = jnp.zeros_like(dk_s)
            dv_s[...] = jnp.zeros_like(dv_s)

        q = q_ref[...]
        sels = _moba_select(q, k_ref[...], i, nb)
        qs = q * SCALE
        qb = [jnp.where(hm, qs, 0.0).astype(bf16) for hm in heads]
        qraw = [jnp.where(hm, q, 0.0).astype(bf16) for hm in heads]
        do = do_ref[...].astype(f32)
        prod = do * o_ref[...].astype(f32)
        delta = [jnp.sum(jnp.where(hm, prod, 0.0), axis=1, keepdims=True) for hm in heads]
        dob = [jnp.where(hm, do, 0.0).astype(bf16) for hm in heads]
        lse_l = lse_ref[...]
        lse_h = [lse_l[:, 0:1], lse_l[:, HEAD_DIM:HEAD_DIM + 1]]
        row = lax.broadcasted_iota(jnp.int32, (MOBA_BLOCK, MOBA_BLOCK), 0)
        col = lax.broadcasted_iota(jnp.int32, (MOBA_BLOCK, MOBA_BLOCK), 1)
        dq_s[...] = jnp.zeros_like(dq_s)

        def block(rows, allow):
            kj = k_ref[rows, :].astype(bf16)
            vj = v_ref[rows, :].astype(bf16)
            dq, dk, dv = None, None, None
            for h in range(2):
                p = jnp.where(allow[h], jnp.exp(_dot_nt(qb[h], kj) - lse_h[h]), 0.0)
                dp = _dot_nt(dob[h], vj)
                ds = (p * (dp - delta[h]) * SCALE).astype(bf16)
                tv = _dot_tn(p.astype(bf16), dob[h])
                tq = _dot(ds, jnp.where(heads[h], kj, jnp.zeros_like(kj)))
                tk = _dot_tn(ds, qraw[h])
                dq, dk, dv = (tq, tk, tv) if dq is None else (dq + tq, dk + tk, dv + tv)
            dq_s[...] += dq
            dk_s[rows, :] += dk
            dv_s[rows, :] += dv

        block(pl.ds(pl.multiple_of(i * MOBA_BLOCK, MOBA_BLOCK), MOBA_BLOCK), [col <= row, col <= row])
        for j in range(nb - 1):
            @pl.when(j < i)
            def _(j=j):
                block(slice(j * MOBA_BLOCK, (j + 1) * MOBA_BLOCK), [sels[h][:, j:j + 1] > 0.0 for h in range(2)])

        dq_ref[...] = _rope_bwd(dq_s[...], cq_ref[...], sq_ref[...]).astype(bf16)

        @pl.when(i == nb - 1)
        def _():
            dk_ref[...] = _rope_bwd(dk_s[...], cf_ref[...], sf_ref[...]).astype(bf16)
            dv_ref[...] = dv_s[...].astype(bf16)

    qblk = pl.BlockSpec((MOBA_BLOCK, LANES), lambda hp, i: (i, hp))
    tq = pl.BlockSpec((MOBA_BLOCK, LANES), lambda hp, i: (i, 0))
    tf = pl.BlockSpec((s, LANES), lambda hp, i: (0, 0))
    colblk = pl.BlockSpec((s, LANES), lambda hp, i: (0, hp))
    return _call(
        body, name="attn_bwd", grid=(D_ATTN // LANES, nb),
        in_specs=[qblk, pl.BlockSpec((s, LANES), lambda hp, i: (0, 4 + hp)),
                  pl.BlockSpec((s, LANES), lambda hp, i: (0, 8 + hp)), qblk,
                  pl.BlockSpec((None, MOBA_BLOCK, LANES), lambda hp, i: (hp, i, 0)), qblk, tq, tq, tf, tf],
        out_specs=[qblk, colblk, colblk],
        out_shape=[jax.ShapeDtypeStruct((s, D_ATTN), bf16)] * 3,
        scratch=[pltpu.VMEM((MOBA_BLOCK, LANES), f32), pltpu.VMEM((s, LANES), f32), pltpu.VMEM((s, LANES), f32)],
    )(z, z, z, o_bf, lse, do_bf, cos_t, sin_t, cos_t, sin_t)


def _inproj_dx(dz_bf, win_g, dh1):
    s = dz_bf.shape[0]
    tm = 256

    def body(dz_ref, w_ref, dh_ref, gx_ref):
        acc = ALPHA * dh_ref[...]
        for sh in range(N_CHIPS):
            acc = acc + _dot_nt(dz_ref[:, sh * D_MODEL:(sh + 1) * D_MODEL], w_ref[sh])
        gx_ref[...] = acc

    row = pl.BlockSpec((tm, D_MODEL), lambda i: (i, 0))
    return _call(
        body, name="inproj_dx", grid=(s // tm,),
        in_specs=[pl.BlockSpec((tm, 4 * D_MODEL), lambda i: (i, 0)),
                  pl.BlockSpec((N_CHIPS, D_MODEL, D_MODEL), lambda i: (0, 0, 0)), row],
        out_specs=row, out_shape=jax.ShapeDtypeStruct((s, D_MODEL), f32))(dz_bf, win_g, dh1)


ANY = pl.BlockSpec(memory_space=pl.ANY)


def _chip_index():
    return 2 * lax.axis_index("x") + lax.axis_index("y")


def _peer(k):
    x, y, c = lax.axis_index("x"), lax.axis_index("y"), lax.axis_index("c")
    return (x ^ (k >> 1), y ^ (k & 1), c)


def _sibling():
    return (lax.axis_index("x"), lax.axis_index("y"), 1 - lax.axis_index("c"))


def _all_gather_shards(shards):
    n = len(shards)

    def body(*refs):
        src, dst = refs[:n], refs[n:2 * n]
        send_sems, recv_sems, fsend_sems, frecv_sems, local_sems = refs[2 * n:]
        p, c = _chip_index(), lax.axis_index("c")
        half = lambda a, cc: pl.ds(cc * (shards[a].shape[0] // 2), shards[a].shape[0] // 2)
        local = [pltpu.make_async_copy(src[a], dst[a].at[p], local_sems.at[a]) for a in range(n)]
        for cp in local:
            cp.start()
        sends = []
        for a in range(n):
            for k in range(1, N_CHIPS):
                cp = pltpu.make_async_remote_copy(
                    src_ref=src[a].at[half(a, c)], dst_ref=dst[a].at[p, half(a, c)], send_sem=send_sems.at[k - 1, a],
                    recv_sem=recv_sems.at[k - 1, a], device_id=_peer(k), device_id_type=MESH)
                cp.start()
                sends.append(cp)
        for a in range(n):
            for k in range(1, N_CHIPS):
                got = dst[a].at[p ^ k, half(a, c)]
                pltpu.make_async_remote_copy(
                    src_ref=src[a].at[half(a, c)], dst_ref=got, send_sem=send_sems.at[k - 1, a],
                    recv_sem=recv_sems.at[k - 1, a], device_id=_peer(k), device_id_type=MESH).wait_recv()
                cp = pltpu.make_async_remote_copy(
                    src_ref=got, dst_ref=got, send_sem=fsend_sems.at[k - 1, a], recv_sem=frecv_sems.at[k - 1, a],
                    device_id=_sibling(), device_id_type=MESH)
                cp.start()
                sends.append(cp)
        for a in range(n):
            for k in range(1, N_CHIPS):
                theirs = dst[a].at[p ^ k, half(a, 1 - c)]
                pltpu.make_async_remote_copy(
                    src_ref=theirs, dst_ref=theirs, send_sem=fsend_sems.at[k - 1, a], recv_sem=frecv_sems.at[k - 1, a],
                    device_id=_sibling(), device_id_type=MESH).wait_recv()
        for cp in sends:
            cp.wait_send()
        for cp in local:
            cp.wait()

    sem = pltpu.SemaphoreType.DMA((N_CHIPS - 1, n))
    return pl.pallas_call(
        body, name="all_gather_weights",
        in_specs=[ANY] * n, out_specs=[ANY] * n,
        out_shape=[jax.ShapeDtypeStruct((N_CHIPS,) + a.shape, a.dtype) for a in shards],
        scratch_shapes=[sem, sem, sem, sem, pltpu.SemaphoreType.DMA((n,))])(*shards)


def _presum_swap(grads):
    n = len(grads)

    def body(*refs):
        src, own, got = refs[:n], refs[n:2 * n], refs[2 * n:3 * n]
        send_sems, recv_sems, local_sems = refs[3 * n:]
        c = lax.axis_index("c")
        half = lambda a, cc: pl.ds(cc * (grads[a].shape[1] // 2), grads[a].shape[1] // 2)
        local = [pltpu.make_async_copy(src[a].at[:, half(a, c)], own[a], local_sems.at[a]) for a in range(n)]
        remote = [pltpu.make_async_remote_copy(src_ref=src[a].at[:, half(a, 1 - c)], dst_ref=got[a],
                                               send_sem=send_sems.at[a], recv_sem=recv_sems.at[a],
                                               device_id=_sibling(), device_id_type=MESH) for a in range(n)]
        for cp in remote + local:
            cp.start()
        for cp in remote:
            cp.wait_recv()
        for cp in remote:
            cp.wait_send()
        for cp in local:
            cp.wait()

    halves = [jax.ShapeDtypeStruct((g.shape[0], g.shape[1] // 2, g.shape[2]), g.dtype) for g in grads]
    sem = pltpu.SemaphoreType.DMA((n,))
    out = pl.pallas_call(
        body, name="presum_swap", in_specs=[ANY] * n, out_specs=[ANY] * (2 * n), out_shape=halves + halves,
        scratch_shapes=[sem, sem, sem])(*grads)
    return out[:n], out[n:]


def _add_pair(a, b, name):
    lead, rows, cols = a.shape
    tm = 256 if rows % 256 == 0 else rows

    def body(a_ref, b_ref, o_ref):
        o_ref[...] = (a_ref[...].astype(f32) + b_ref[...].astype(f32)).astype(o_ref.dtype)

    blk = pl.BlockSpec((lead, tm, cols), lambda i: (0, i, 0))
    return _call(body, name=name, grid=(rows // tm,), in_specs=[blk, blk], out_specs=blk,
                 out_shape=jax.ShapeDtypeStruct(a.shape, a.dtype))(a, b)


def _sibling_fill(halves):
    n = len(halves)

    def body(*refs):
        src, dst = refs[:n], refs[n:2 * n]
        send_sems, recv_sems, local_sems = refs[2 * n:]
        c = lax.axis_index("c")
        place = lambda a, cc: dst[a].at[pl.ds(cc * halves[a].shape[0], halves[a].shape[0])]
        local = [pltpu.make_async_copy(src[a], place(a, c), local_sems.at[a]) for a in range(n)]
        remote = [pltpu.make_async_remote_copy(src_ref=src[a], dst_ref=place(a, c), send_sem=send_sems.at[a],
                                               recv_sem=recv_sems.at[a], device_id=_sibling(), device_id_type=MESH)
                  for a in range(n)]
        for cp in remote + local:
            cp.start()
        for a in range(n):
            pltpu.make_async_remote_copy(src_ref=src[a], dst_ref=place(a, 1 - c), send_sem=send_sems.at[a],
                                         recv_sem=recv_sems.at[a], device_id=_sibling(), device_id_type=MESH).wait_recv()
        for cp in remote:
            cp.wait_send()
        for cp in local:
            cp.wait()

    sem = pltpu.SemaphoreType.DMA((n,))
    return pl.pallas_call(
        body, name="sibling_fill", in_specs=[ANY] * n, out_specs=[ANY] * n,
        out_shape=[jax.ShapeDtypeStruct((2 * h.shape[0], h.shape[1]), h.dtype) for h in halves],
        scratch_shapes=[sem, sem, sem])(*halves)


def _exchange_slices(grads, whole):
    n = len(grads)

    def body(*refs):
        src, dst = refs[:n], refs[n:2 * n]
        send_sems, recv_sems, local_sems = refs[2 * n:]
        p = _chip_index()
        pick = lambda a, q: src[a].at[0] if whole[a] else src[a].at[q]
        local = [pltpu.make_async_copy(pick(a, p), dst[a].at[0], local_sems.at[a]) for a in range(n)]
        for cp in local:
            cp.start()
        remote = []
        for k in range(1, N_CHIPS):
            for a in range(n):
                cp = pltpu.make_async_remote_copy(
                    src_ref=pick(a, p ^ k), dst_ref=dst[a].at[k], send_sem=send_sems.at[k - 1, a],
                    recv_sem=recv_sems.at[k - 1, a], device_id=_peer(k), device_id_type=MESH)
                cp.start()
                remote.append(cp)
        for cp in remote:
            cp.wait_recv()
        for cp in remote:
            cp.wait_send()
        for cp in local:
            cp.wait()

    return pl.pallas_call(
        body, name="exchange_grad_slices",
        in_specs=[ANY] * n, out_specs=[ANY] * n,
        out_shape=[jax.ShapeDtypeStruct((N_CHIPS,) + g.shape[1:], g.dtype) for g in grads],
        scratch_shapes=[pltpu.SemaphoreType.DMA((N_CHIPS - 1, n)), pltpu.SemaphoreType.DMA((N_CHIPS - 1, n)),
                        pltpu.SemaphoreType.DMA((n,))])(*grads)


def _sum_slots(r, name):
    _, rows, cols = r.shape
    tm = 256 if rows % 256 == 0 else rows

    def body(r_ref, o_ref):
        r0, r1, r2, r3 = [r_ref[k].astype(f32) for k in range(N_CHIPS)]
        o_ref[...] = (r0 + r1) + (r2 + r3)

    return _call(
        body, name=name, grid=(rows // tm,),
        in_specs=[pl.BlockSpec((N_CHIPS, tm, cols), lambda i: (0, i, 0))],
        out_specs=pl.BlockSpec((tm, cols), lambda i: (i, 0)),
        out_shape=jax.ShapeDtypeStruct((rows, cols), f32))(r)


def _adamw(w, m, v, grad, name, tm):
    rows, cols = w.shape
    gcols = grad.shape[1]
    if tm == rows:
        gspec = pl.BlockSpec(grad.shape, lambda i: (0, 0))
    else:
        gspec = pl.BlockSpec((tm, gcols), lambda i: (i, 0))

    def body(w_ref, m_ref, v_ref, gin_ref, g_ref, d_ref, nm_ref, nv_ref):
        g = gin_ref[0:tm, 0:cols]
        w_, m_, v_ = w_ref[...], m_ref[...], v_ref[...]
        m_ = ADAM_B1 * m_ + (1.0 - ADAM_B1) * g
        v_ = ADAM_B2 * v_ + (1.0 - ADAM_B2) * jnp.square(g)
        m_hat = m_ / (1.0 - ADAM_B1 ** ADAM_STEP)
        v_hat = v_ / (1.0 - ADAM_B2 ** ADAM_STEP)
        g_ref[...] = g
        d_ref[...] = -ADAM_LR * (m_hat / (jnp.sqrt(v_hat) + ADAM_EPS) + ADAM_WD * w_)
        nm_ref[...] = m_
        nv_ref[...] = v_

    blk = pl.BlockSpec((tm, cols), lambda i: (i, 0))
    return _call(
        body, name=name, grid=(rows // tm,),
        in_specs=[blk, blk, blk, gspec], out_specs=[blk] * 4,
        out_shape=[jax.ShapeDtypeStruct((rows, cols), f32)] * 4)(w, m, v, grad)


def _rope_tables(s):
    half = HEAD_DIM // 2
    inv_freq = 1.0 / (10000.0 ** (jnp.arange(half, dtype=f32) / half))
    ang = jnp.arange(s, dtype=f32)[:, None] * inv_freq[None, :]
    cos, sin = jnp.cos(ang), jnp.sin(ang)
    return jnp.tile(cos, (1, LANES // half)), jnp.tile(jnp.concatenate([-sin, sin], axis=1), (1, LANES // HEAD_DIM))


def _local_step(x, target, gw, b_gate, w_pool, pool_scale, ln1_g, ln1_b, convb_g, ln2_g, ln2_b):
    s = x.shape[0]
    nb = s // MOBA_BLOCK
    win_g, wba_g, wbp_g, wout_g, wg_g, wu_g, wd_g, convw_g = gw
    wba = wba_g.transpose(1, 0, 2).reshape(D_ATTN, D_MODEL)
    wbp = wbp_g.transpose(1, 0, 2).reshape(D_POOL, D_MODEL)
    wout = wout_g.reshape(D_MODEL, D_MODEL)
    wd = wd_g.reshape(D_FF_PAD, D_MODEL)
    cos_t, sin_t = _rope_tables(s)
    x_bf = x.astype(bf16)

    z = _inproj(x_bf, win_g, cos_t, sin_t, b_gate)
    o_bf, lse = _attn_fwd(z, nb)
    pooled_bf, mixed, pm_bf = _pool_fwd(z, w_pool, pool_scale)
    m_bf = _branch_merge(o_bf, pm_bf, z, wba, wbp)
    x1_bf, xhat1, rstd1 = _out_ln1(m_bf, wout, x, ln1_g, ln1_b)
    a, up, hh_bf = _ffn_up(x1_bf, wg_g, wu_g, convw_g, convb_g)
    loss, dh2, dh2_bf, d_ln2_g, d_ln2_b = _ffn_down_ln2_loss(hh_bf, wd, xhat1, ln1_g, ln1_b, ln2_g, ln2_b, target)

    da_bf, dup_bf, dconv = _ffn_act_bwd(dh2_bf, wd_g, a, up, convw_g, convb_g)
    d_wg = _matmul_tn(x1_bf, da_bf, N_CHIPS, "dw_ffn_gate")
    d_wu = _matmul_tn(x1_bf, dup_bf, N_CHIPS, "dw_ffn_up")
    d_wd = _matmul_tn(hh_bf, dh2_bf, 1, "dw_ffn_down").reshape(N_CHIPS, FF_PAD, D_MODEL)
    dh1, dh1_bf, d_ln1_g, d_ln1_b = _ffn_in_bwd_ln1(da_bf, dup_bf, wg_g, wu_g, dh2, xhat1, rstd1, ln1_g)
    d_wout = _matmul_tn(m_bf, dh1_bf, 1, "dw_out").reshape(N_CHIPS, D_MODEL // N_CHIPS, D_MODEL)
    dzg_bf, dya_bf, dyp_bf, do_bf, dpm, d_bgate = _merge_bwd(dh1_bf, wout, o_bf, pm_bf, z, wba, wbp)
    d_wba = _matmul_tn(o_bf, dya_bf, N_CHIPS, "dw_branch_attn")
    d_wbp = _matmul_tn(pm_bf, dyp_bf, N_CHIPS, "dw_branch_pool")
    du_bf, d_wpool, d_pscale = _pool_bwd(dpm, mixed, pooled_bf, w_pool, pool_scale)
    dq_bf, dk_bf, dv_bf = _attn_bwd(z, o_bf, lse, do_bf, cos_t, sin_t, nb)
    dz_bf = jnp.concatenate([dq_bf, dk_bf, dv_bf, du_bf, dzg_bf], axis=1)
    d_win = _matmul_tn(x_bf, dz_bf, N_CHIPS, "dw_in")
    grad_x = _inproj_dx(dz_bf, win_g, dh1)

    sharded = [d_win, d_wba, d_wbp, d_wout, d_wg, d_wu, d_wd, dconv]
    small = [d_bgate, d_wpool, d_pscale, d_ln1_g, d_ln1_b, d_ln2_g, d_ln2_b, dconv[:, 3, :]]
    return loss[0, 0], grad_x, sharded, small


SMALL_ROWS = 592


def _pack_small(parts):
    flat = jnp.concatenate([p.reshape(-1) for p in parts])
    return jnp.pad(flat, (0, SMALL_ROWS * LANES - flat.shape[0])).reshape(SMALL_ROWS, LANES)


def _unpack_small(packed, shapes):
    flat = packed.reshape(-1)
    out, off = [], 0
    for shp in shapes:
        n = 1
        for d in shp:
            n *= d
        out.append(flat[off:off + n].reshape(shp))
        off += n
    return out


def _pad_conv_b(cb):
    return jnp.pad(cb.reshape(N_CHIPS, FF_SHARD), ((0, 0), (0, FF_PAD - FF_SHARD)))


def kernel(x, w_in, b_gate, w_branch_attn, w_pool, pool_scale, w_branch_pool, w_out, ln1_g, ln1_b, w_ffn_gate, w_ffn_up, conv_w, conv_b, w_ffn_down, ln2_g, ln2_b, loss_target, m_w_in, m_b_gate, m_w_branch_attn, m_w_pool, m_pool_scale, m_w_branch_pool, m_w_out, m_ln1_g, m_ln1_b, m_w_ffn_gate, m_w_ffn_up, m_conv_w, m_conv_b, m_w_ffn_down, m_ln2_g, m_ln2_b, v_w_in, v_b_gate, v_w_branch_attn, v_w_pool, v_pool_scale, v_w_branch_pool, v_w_out, v_ln1_g, v_ln1_b, v_w_ffn_gate, v_w_ffn_up, v_conv_w, v_conv_b, v_w_ffn_down, v_ln2_g, v_ln2_b):
    pad_cols = lambda w: jnp.pad(w, ((0, 0), (0, FF_PAD - FF_SHARD)))
    shards = [w_in[0].astype(bf16), w_branch_attn[0].astype(bf16), w_branch_pool[0].astype(bf16),
              w_out[0].astype(bf16), pad_cols(w_ffn_gate[0].astype(bf16)), pad_cols(w_ffn_up[0].astype(bf16)),
              jnp.pad(w_ffn_down[0].astype(bf16), ((0, FF_PAD - FF_SHARD), (0, 0))),
              jnp.pad(conv_w[0], ((0, CONV_ROWS - 3), (0, FF_PAD - FF_SHARD)))]
    gathered = _all_gather_shards(shards)

    convb_g = _pad_conv_b(conv_b).reshape(N_CHIPS, 1, FF_PAD)
    loss, grad_x, sharded, small = _local_step(
        x[0], loss_target[0], gathered, b_gate, w_pool[0], pool_scale, ln1_g, ln1_b, convb_g, ln2_g, ln2_b)
    loss = lax.psum(loss, ("x", "y", "c"))

    names = ["w_in", "w_branch_attn", "w_branch_pool", "w_out", "w_ffn_gate", "w_ffn_up", "w_ffn_down", "conv_w", "small"]
    own, got = _presum_swap(sharded + [_pack_small(small)[None]])
    chip_sums = [_add_pair(a, b, "presum_" + nm) for a, b, nm in zip(own, got, names)]
    slots = _exchange_slices(chip_sums, [False] * len(sharded) + [True])
    grads = _sibling_fill([_sum_slots(r, "sum_" + nm) for r, nm in zip(slots, names)])

    weights = [w_in[0], w_branch_attn[0], w_branch_pool[0], w_out[0], w_ffn_gate[0], w_ffn_up[0], w_ffn_down[0], conv_w[0]]
    m_in = [m_w_in[0], m_w_branch_attn[0], m_w_branch_pool[0], m_w_out[0], m_w_ffn_gate[0], m_w_ffn_up[0], m_w_ffn_down[0], m_conv_w[0]]
    v_in = [v_w_in[0], v_w_branch_attn[0], v_w_branch_pool[0], v_w_out[0], v_w_ffn_gate[0], v_w_ffn_up[0], v_w_ffn_down[0], v_conv_w[0]]
    tiles = [256, 256, 256, 128, 256, 256, 88, 3]
    res = {}
    for i, nm in enumerate(names[:-1]):
        res[nm] = [r[None] for r in _adamw(weights[i], m_in[i], v_in[i], grads[i], "adamw_" + nm, tiles[i])]

    small_names = ["b_gate", "w_pool", "pool_scale", "ln1_g", "ln1_b", "ln2_g", "ln2_b", "conv_b"]
    small_w = [b_gate, w_pool, pool_scale, ln1_g, ln1_b, ln2_g, ln2_b, _pad_conv_b(conv_b)]
    small_m = [m_b_gate, m_w_pool, m_pool_scale, m_ln1_g, m_ln1_b, m_ln2_g, m_ln2_b, _pad_conv_b(m_conv_b)]
    small_v = [v_b_gate, v_w_pool, v_pool_scale, v_ln1_g, v_ln1_b, v_ln2_g, v_ln2_b, _pad_conv_b(v_conv_b)]
    packed = _adamw(_pack_small(small_w), _pack_small(small_m), _pack_small(small_v), grads[-1],
                    "adamw_small", SMALL_ROWS)
    shapes = [w.shape for w in small_w]
    for kind in range(4):
        for nm, val in zip(small_names, _unpack_small(packed[kind], shapes)):
            if nm == "conv_b":
                val = val[:, :FF_SHARD].reshape(1, 4 * FF_SHARD)
            res.setdefault(nm, [None] * 4)[kind] = val

    order = ["w_in", "b_gate", "w_branch_attn", "w_pool", "pool_scale", "w_branch_pool", "w_out", "ln1_g", "ln1_b",
             "w_ffn_gate", "w_ffn_up", "conv_w", "conv_b", "w_ffn_down", "ln2_g", "ln2_b"]
    outs = [loss, grad_x[None]]
    for kind in range(4):
        outs += [res[nm][kind] for nm in order]
    return tuple(outs)
```

```python
import functools

import jax
import jax.numpy as jnp
from jax import lax
from jax.experimental import pallas as pl
from jax.experimental.pallas import tpu as pltpu

D_MODEL = 1024
HEAD_DIM = 64
D_ATTN = 512
D_POOL = 512
MOBA_BLOCK = 256
MOBA_TOPK = 3
POOL_GROUP = 128
MAX_WINDOW = 16
FF_SHARD = 704
FF_PAD = 768
D_FF_PAD = 4 * FF_PAD
N_CHIPS = 4
LANES = 128
ALPHA = (2.0 * 1) ** 0.25
LN_EPS = 1e-5
NEG = -1e30
SCALE = HEAD_DIM ** -0.5
ADAM_LR, ADAM_B1, ADAM_B2, ADAM_EPS, ADAM_WD, ADAM_STEP = 0.001, 0.9, 0.999, 1e-08, 0.01, 10
VMEM_LIMIT = 56 * 1024 * 1024
MESH = pl.DeviceIdType.MESH

bf16 = jnp.bfloat16
f32 = jnp.float32


def _dot(a, b):
    return jnp.dot(a, b, preferred_element_type=f32)


def _dot_nt(a, b):
    return lax.dot_general(a, b, (((1,), (1,)), ((), ())), preferred_element_type=f32)


def _dot_tn(a, b):
    return lax.dot_general(a, b, (((0,), (0,)), ((), ())), preferred_element_type=f32)


def _call(body, *, name, grid, in_specs, out_specs, out_shape, scratch=()):
    return pl.pallas_call(
        body, name=name, grid=grid, in_specs=in_specs, out_specs=out_specs, out_shape=out_shape,
        scratch_shapes=list(scratch),
        compiler_params=pltpu.CompilerParams(dimension_semantics=("arbitrary",) * len(grid),
                                             vmem_limit_bytes=VMEM_LIMIT))


def _rot_half(t):
    lane = lax.broadcasted_iota(jnp.int32, t.shape, 1)
    first = (lane % HEAD_DIM) < (HEAD_DIM // 2)
    return jnp.where(first, pltpu.roll(t, LANES - HEAD_DIM // 2, 1), pltpu.roll(t, HEAD_DIM // 2, 1))


def _rope(t, cos, sin_signed):
    return t * cos + _rot_half(t) * sin_signed


def _rope_bwd(d, cos, sin_signed):
    return d * cos + _rot_half(d * sin_signed)


def _gelu_parts(a):
    cdf = 0.5 * (1.0 + lax.erf(a * (2.0 ** -0.5)))
    pdf = jnp.exp(-0.5 * a * a) * ((2.0 * jnp.pi) ** -0.5)
    return a * cdf, cdf + a * pdf


def _layer_norm(h, g, b):
    mu = jnp.mean(h, axis=-1, keepdims=True)
    xc = h - mu
    var = jnp.mean(xc * xc, axis=-1, keepdims=True)
    rstd = lax.rsqrt(var + LN_EPS)
    xhat = xc * rstd
    return xhat * g + b, xhat, rstd


def _layer_norm_bwd(dy, xhat, rstd, g):
    dxh = dy * g
    m1 = jnp.mean(dxh, axis=-1, keepdims=True)
    m2 = jnp.mean(dxh * xhat, axis=-1, keepdims=True)
    return rstd * (dxh - m1 - xhat * m2)


def _inproj(x_bf, win_g, cos_t, sin_t, b_gate):
    s = x_bf.shape[0]
    tm, tn = 512, 512

    def body(x_ref, w_ref, cos_ref, sin_ref, b_ref, o_ref):
        j = pl.program_id(1)
        acc = _dot(x_ref[...], w_ref[...])

        @pl.when(j < 2)
        def _():
            for c in range(tn // LANES):
                sl = slice(c * LANES, (c + 1) * LANES)
                o_ref[:, sl] = _rope(acc[:, sl], cos_ref[...], sin_ref[...])

        @pl.when((j >= 2) & (j < 4))
        def _():
            o_ref[...] = acc

        @pl.when(j >= 4)
        def _():
            o_ref[...] = jax.nn.sigmoid(acc + b_ref[...])

    return _call(
        body, name="inproj", grid=(s // tm, 4 * D_MODEL // tn),
        in_specs=[pl.BlockSpec((tm, D_MODEL), lambda i, j: (i, 0)),
                  pl.BlockSpec((None, D_MODEL, tn), lambda i, j: (j // 2, 0, j % 2)),
                  pl.BlockSpec((tm, LANES), lambda i, j: (i, 0)),
                  pl.BlockSpec((tm, LANES), lambda i, j: (i, 0)),
                  pl.BlockSpec((1, tn), lambda i, j: (0, jnp.maximum(j - 4, 0)))],
        out_specs=pl.BlockSpec((tm, tn), lambda i, j: (i, j)),
        out_shape=jax.ShapeDtypeStruct((s, 4 * D_MODEL), f32))(x_bf, win_g, cos_t, sin_t, b_gate)


def _moba_select(q, k_all, i, nb):
    lane = lax.broadcasted_iota(jnp.int32, q.shape, 1)
    k_mean = jnp.mean(k_all.reshape(nb, MOBA_BLOCK, LANES), axis=1).astype(bf16)
    n_io = lax.broadcasted_iota(jnp.int32, (MOBA_BLOCK, nb), 1)
    past = n_io < i
    sels = []
    for h in range(2):
        head = (lane < HEAD_DIM) if h == 0 else (lane >= HEAD_DIM)
        gate = _dot_nt(jnp.where(head, q, 0.0).astype(bf16), k_mean)
        g = jnp.where(past, gate, NEG)
        rank = jnp.zeros((MOBA_BLOCK, nb), f32)
        for m in range(nb):
            gm = g[:, m:m + 1]
            rank = rank + jnp.where((gm > g) | ((gm == g) & (m < n_io)), 1.0, 0.0)
        sels.append(jnp.where(past & (rank < MOBA_TOPK), 1.0, 0.0))
    return sels


def _attn_fwd(z, nb):
    s = z.shape[0]

    def body(q_ref, k_ref, v_ref, o_ref, lse_ref, m_s, l_s, acc_s):
        i = pl.program_id(1)
        lane = lax.broadcasted_iota(jnp.int32, (MOBA_BLOCK, LANES), 1)
        head0 = lane < HEAD_DIM
        heads = (head0, jnp.logical_not(head0))
        q = q_ref[...]
        sels = _moba_select(q, k_ref[...], i, nb)
        qs = q * SCALE
        qb = [jnp.where(hm, qs, 0.0).astype(bf16) for hm in heads]
        row = lax.broadcasted_iota(jnp.int32, (MOBA_BLOCK, MOBA_BLOCK), 0)
        col = lax.broadcasted_iota(jnp.int32, (MOBA_BLOCK, MOBA_BLOCK), 1)

        own = pl.ds(pl.multiple_of(i * MOBA_BLOCK, MOBA_BLOCK), MOBA_BLOCK)
        kj = k_ref[own, :].astype(bf16)
        vj = v_ref[own, :]
        ms, ls, pv = [], [], None
        for h in range(2):
            sc = jnp.where(col <= row, _dot_nt(qb[h], kj), NEG)
            m = jnp.max(sc, axis=1, keepdims=True)
            p = jnp.exp(sc - m)
            ms.append(m)
            ls.append(jnp.sum(p, axis=1, keepdims=True))
            t = _dot(p.astype(bf16), jnp.where(heads[h], vj, 0.0).astype(bf16))
            pv = t if pv is None else pv + t
        m_s[...] = jnp.where(head0, ms[0], ms[1])
        l_s[...] = jnp.where(head0, ls[0], ls[1])
        acc_s[...] = pv

        for j in range(nb - 1):
            @pl.when(j < i)
            def _(j=j):
                kj = k_ref[j * MOBA_BLOCK:(j + 1) * MOBA_BLOCK, :].astype(bf16)
                vj = v_ref[j * MOBA_BLOCK:(j + 1) * MOBA_BLOCK, :]
                m_old_l, l_old_l = m_s[...], l_s[...]
                m_new, alpha, l_new, pv = [], [], [], None
                for h in range(2):
                    c0 = h * HEAD_DIM
                    sc = jnp.where(sels[h][:, j:j + 1] > 0.0, _dot_nt(qb[h], kj), NEG)
                    m_old = m_old_l[:, c0:c0 + 1]
                    mn = jnp.maximum(m_old, jnp.max(sc, axis=1, keepdims=True))
                    a = jnp.exp(m_old - mn)
                    p = jnp.exp(sc - mn)
                    m_new.append(mn)
                    alpha.append(a)
                    l_new.append(a * l_old_l[:, c0:c0 + 1] + jnp.sum(p, axis=1, keepdims=True))
                    t = _dot(p.astype(bf16), jnp.where(heads[h], vj, 0.0).astype(bf16))
                    pv = t if pv is None else pv + t
                m_s[...] = jnp.where(head0, m_new[0], m_new[1])
                l_s[...] = jnp.where(head0, l_new[0], l_new[1])
                acc_s[...] = acc_s[...] * jnp.where(head0, alpha[0], alpha[1]) + pv

        o_ref[...] = (acc_s[...] / l_s[...]).astype(bf16)
        lse_ref[...] = m_s[...] + jnp.log(l_s[...])

    return _call(
        body, name="attn_fwd", grid=(D_ATTN // LANES, nb),
        in_specs=[pl.BlockSpec((MOBA_BLOCK, LANES), lambda hp, i: (i, hp)),
                  pl.BlockSpec((s, LANES), lambda hp, i: (0, 4 + hp)),
                  pl.BlockSpec((s, LANES), lambda hp, i: (0, 8 + hp))],
        out_specs=[pl.BlockSpec((MOBA_BLOCK, LANES), lambda hp, i: (i, hp)),
                   pl.BlockSpec((None, MOBA_BLOCK, LANES), lambda hp, i: (hp, i, 0))],
        out_shape=[jax.ShapeDtypeStruct((s, D_ATTN), bf16),
                   jax.ShapeDtypeStruct((D_ATTN // LANES, s, LANES), f32)],
        scratch=[pltpu.VMEM((MOBA_BLOCK, LANES), f32)] * 3)(z, z, z)


def _window_select(snaps, g):
    return jnp.where(g == 0, snaps[0], jnp.where(g == 1, snaps[1], jnp.where(g == 2, snaps[2], snaps[3])))


def _pool_fwd(z, w_pool, pool_scale):
    s = z.shape[0]

    def body(u_ref, w_ref, sc_ref, pooled_ref, mixed_ref, pm_ref, pad):
        g = pl.program_id(0)
        u = u_ref[...]
        pad[0:MAX_WINDOW, :] = jnp.zeros((MAX_WINDOW, POOL_GROUP), f32)
        pad[MAX_WINDOW:MAX_WINDOW + s, :] = u
        acc = u
        snaps = []
        for d in range(1, MAX_WINDOW):
            acc = acc + pad[MAX_WINDOW - d:MAX_WINDOW - d + s, :]
            if d + 1 in (2, 4, 8, 16):
                snaps.append(acc)
        win = _window_select(snaps, g)
        t = lax.broadcasted_iota(jnp.int32, (s, POOL_GROUP), 0)
        count = jnp.minimum(t + 1, jnp.left_shift(2, g)).astype(f32)
        pooled = (win / count - u).astype(bf16)
        mixed = _dot(pooled, w_ref[...].astype(bf16))
        pooled_ref[...] = pooled
        mixed_ref[...] = mixed
        pm_ref[...] = (mixed * sc_ref[...]).astype(bf16)

    blk = pl.BlockSpec((s, POOL_GROUP), lambda g: (0, g))
    return _call(
        body, name="pool_fwd", grid=(4,),
        in_specs=[pl.BlockSpec((s, POOL_GROUP), lambda g: (0, 12 + g)),
                  pl.BlockSpec((None, POOL_GROUP, POOL_GROUP), lambda g: (g, 0, 0)),
                  pl.BlockSpec((1, POOL_GROUP), lambda g: (0, g))],
        out_specs=[blk, blk, blk],
        out_shape=[jax.ShapeDtypeStruct((s, D_POOL), bf16), jax.ShapeDtypeStruct((s, D_POOL), f32),
                   jax.ShapeDtypeStruct((s, D_POOL), bf16)],
        scratch=[pltpu.VMEM((s + MAX_WINDOW, POOL_GROUP), f32)])(z, w_pool, pool_scale)


def _branch_merge(o_bf, pm_bf, z, wba, wbp):
    s = o_bf.shape[0]
    tm = 256

    def body(o_ref, pm_ref, ga_ref, gp_ref, wba_ref, wbp_ref, m_ref):
        ya = _dot(o_ref[...], wba_ref[...])
        yp = _dot(pm_ref[...], wbp_ref[...])
        m_ref[...] = (ga_ref[...] * ya + gp_ref[...] * yp).astype(bf16)

    full = lambda r, c: pl.BlockSpec((r, c), lambda i: (0, 0))
    return _call(
        body, name="branch_merge", grid=(s // tm,),
        in_specs=[pl.BlockSpec((tm, D_ATTN), lambda i: (i, 0)), pl.BlockSpec((tm, D_POOL), lambda i: (i, 0)),
                  pl.BlockSpec((tm, D_MODEL), lambda i: (i, 2)), pl.BlockSpec((tm, D_MODEL), lambda i: (i, 3)),
                  full(D_ATTN, D_MODEL), full(D_POOL, D_MODEL)],
        out_specs=pl.BlockSpec((tm, D_MODEL), lambda i: (i, 0)),
        out_shape=jax.ShapeDtypeStruct((s, D_MODEL), bf16))(o_bf, pm_bf, z, z, wba, wbp)


def _out_ln1(m_bf, wout, x, ln_g, ln_b):
    s = x.shape[0]
    tm = 256

    def body(m_ref, w_ref, x_ref, g_ref, b_ref, x1_ref, xhat_ref, rstd_ref):
        h = ALPHA * x_ref[...] + _dot(m_ref[...], w_ref[...])
        y, xhat, rstd = _layer_norm(h, g_ref[...], b_ref[...])
        x1_ref[...] = y.astype(bf16)
        xhat_ref[...] = xhat
        rstd_ref[...] = jnp.broadcast_to(rstd, (tm, LANES))

    row = pl.BlockSpec((tm, D_MODEL), lambda i: (i, 0))
    vec = pl.BlockSpec((1, D_MODEL), lambda i: (0, 0))
    return _call(
        body, name="out_ln1", grid=(s // tm,),
        in_specs=[row, pl.BlockSpec((D_MODEL, D_MODEL), lambda i: (0, 0)), row, vec, vec],
        out_specs=[row, row, pl.BlockSpec((tm, LANES), lambda i: (i, 0))],
        out_shape=[jax.ShapeDtypeStruct((s, D_MODEL), bf16), jax.ShapeDtypeStruct((s, D_MODEL), f32),
                   jax.ShapeDtypeStruct((s, LANES), f32)])(m_bf, wout, x, ln_g, ln_b)


FF_TILE = 256
FF_TILES_PER_SHARD = FF_PAD // FF_TILE
CONV_PAD = 8
CONV_ROWS = 16


def _ff_weight_spec(rows):
    return pl.BlockSpec((None, rows, FF_TILE), lambda j: (j // FF_TILES_PER_SHARD, 0, j % FF_TILES_PER_SHARD))


def _conv_taps(pad, s, cw_ref, off=0):
    return (pad[CONV_PAD + off:CONV_PAD + off + s, :] * cw_ref[2:3, :]
            + pad[CONV_PAD - 1 + off:CONV_PAD - 1 + off + s, :] * cw_ref[1:2, :]
            + pad[CONV_PAD - 2 + off:CONV_PAD - 2 + off + s, :] * cw_ref[0:1, :])


def _ffn_up(x1_bf, wg_g, wu_g, convw_g, convb_g):
    s = x1_bf.shape[0]

    def body(x_ref, wg_ref, wu_ref, cw_ref, cb_ref, a_ref, up_ref, hh_ref, pad):
        x1 = x_ref[...]
        a = _dot(x1, wg_ref[...])
        up = _dot(x1, wu_ref[...])
        a_ref[...] = a
        up_ref[...] = up
        pad[0:CONV_PAD, :] = jnp.zeros((CONV_PAD, FF_TILE), f32)
        pad[CONV_PAD:CONV_PAD + s, :] = a
        ac = _conv_taps(pad, s, cw_ref) + cb_ref[...]
        hg, _ = _gelu_parts(ac)
        hh_ref[...] = (hg * up).astype(bf16)

    col = pl.BlockSpec((s, FF_TILE), lambda j: (0, j))
    return _call(
        body, name="ffn_up", grid=(D_FF_PAD // FF_TILE,),
        in_specs=[pl.BlockSpec((s, D_MODEL), lambda j: (0, 0)), _ff_weight_spec(D_MODEL), _ff_weight_spec(D_MODEL),
                  _ff_weight_spec(8), _ff_weight_spec(1)],
        out_specs=[col, col, col],
        out_shape=[jax.ShapeDtypeStruct((s, D_FF_PAD), f32), jax.ShapeDtypeStruct((s, D_FF_PAD), f32),
                   jax.ShapeDtypeStruct((s, D_FF_PAD), bf16)],
        scratch=[pltpu.VMEM((s + CONV_PAD, FF_TILE), f32)])(x1_bf, wg_g, wu_g, convw_g, convb_g)


def _ffn_down_ln2_loss(hh_bf, wd, xhat1, ln1_g, ln1_b, ln2_g, ln2_b, target):
    s = hh_bf.shape[0]
    tm = 256

    def body(hh_ref, w_ref, xh1_ref, g1_ref, b1_ref, g2_ref, b2_ref, t_ref, loss_ref, dh_ref, dhb_ref, dg_ref, db_ref):
        i = pl.program_id(0)
        x1 = xh1_ref[...] * g1_ref[...] + b1_ref[...]
        h = ALPHA * x1 + _dot(hh_ref[...], w_ref[...])
        y, xhat, rstd = _layer_norm(h, g2_ref[...], b2_ref[...])
        err = y - t_ref[...]
        part = 0.5 * jnp.sum(jnp.mean(err * err, axis=-1, keepdims=True), axis=0, keepdims=True)
        dy = err * (1.0 / D_MODEL)

        @pl.when(i == 0)
        def _():
            loss_ref[...] = jnp.zeros_like(loss_ref)
            dg_ref[...] = jnp.zeros_like(dg_ref)
            db_ref[...] = jnp.zeros_like(db_ref)

        loss_ref[...] += jnp.broadcast_to(part, loss_ref.shape)
        dg_ref[...] += jnp.sum(dy * xhat, axis=0, keepdims=True)
        db_ref[...] += jnp.sum(dy, axis=0, keepdims=True)
        dh = _layer_norm_bwd(dy, xhat, rstd, g2_ref[...])
        dh_ref[...] = dh
        dhb_ref[...] = dh.astype(bf16)

    row = pl.BlockSpec((tm, D_MODEL), lambda i: (i, 0))
    vec = pl.BlockSpec((1, D_MODEL), lambda i: (0, 0))
    return _call(
        body, name="ffn_down_ln2_loss", grid=(s // tm,),
        in_specs=[pl.BlockSpec((tm, D_FF_PAD), lambda i: (i, 0)), pl.BlockSpec((D_FF_PAD, D_MODEL), lambda i: (0, 0)),
                  row, vec, vec, vec, vec, row],
        out_specs=[pl.BlockSpec((8, LANES), lambda i: (0, 0)), row, row, vec, vec],
        out_shape=[jax.ShapeDtypeStruct((8, LANES), f32), jax.ShapeDtypeStruct((s, D_MODEL), f32),
                   jax.ShapeDtypeStruct((s, D_MODEL), bf16), jax.ShapeDtypeStruct((1, D_MODEL), f32),
                   jax.ShapeDtypeStruct((1, D_MODEL), f32)])(hh_bf, wd, xhat1, ln1_g, ln1_b, ln2_g, ln2_b, target)


def _ffn_act_bwd(dh2_bf, wd_g, a, up, convw_g, convb_g):
    s = dh2_bf.shape[0]

    def body(dh_ref, wd_ref, a_ref, up_ref, cw_ref, cb_ref, da_ref, dup_ref, dc_ref, pad, dpad):
        dhh = _dot_nt(dh_ref[...], wd_ref[...])
        zeros = jnp.zeros((CONV_PAD, FF_TILE), f32)
        pad[0:CONV_PAD, :] = zeros
        pad[CONV_PAD:CONV_PAD + s, :] = a_ref[...]
        ac = _conv_taps(pad, s, cw_ref) + cb_ref[...]
        hg, dgelu = _gelu_parts(ac)
        up = up_ref[...]
        dup_ref[...] = (dhh * hg).astype(bf16)
        dac = dhh * up * dgelu
        rows = [jnp.sum(dac * pad[CONV_PAD - 2 + i:CONV_PAD - 2 + i + s, :], axis=0, keepdims=True) for i in range(3)]
        rows.append(jnp.sum(dac, axis=0, keepdims=True))
        rows.append(jnp.zeros((CONV_ROWS - 4, FF_TILE), f32))
        dc_ref[...] = jnp.concatenate(rows, axis=0)
        dpad[0:s, :] = dac
        dpad[s:s + CONV_PAD, :] = zeros
        da = (dpad[0:s, :] * cw_ref[2:3, :] + dpad[1:1 + s, :] * cw_ref[1:2, :] + dpad[2:2 + s, :] * cw_ref[0:1, :])
        da_ref[...] = da.astype(bf16)

    col = pl.BlockSpec((s, FF_TILE), lambda j: (0, j))
    return _call(
        body, name="ffn_act_bwd", grid=(D_FF_PAD // FF_TILE,),
        in_specs=[pl.BlockSpec((s, D_MODEL), lambda j: (0, 0)),
                  pl.BlockSpec((None, FF_TILE, D_MODEL),
                               lambda j: (j // FF_TILES_PER_SHARD, j % FF_TILES_PER_SHARD, 0)),
                  col, col, _ff_weight_spec(8), _ff_weight_spec(1)],
        out_specs=[col, col, _ff_weight_spec(CONV_ROWS)],
        out_shape=[jax.ShapeDtypeStruct((s, D_FF_PAD), bf16), jax.ShapeDtypeStruct((s, D_FF_PAD), bf16),
                   jax.ShapeDtypeStruct((N_CHIPS, CONV_ROWS, FF_PAD), f32)],
        scratch=[pltpu.VMEM((s + CONV_PAD, FF_TILE), f32), pltpu.VMEM((s + CONV_PAD, FF_TILE), f32)],
    )(dh2_bf, wd_g, a, up, convw_g, convb_g)


def _matmul_tn(a, b, n_shards, name, tm=512, tn=256):
    k, m = a.shape
    n = b.shape[1]
    tm, tn = min(tm, m), min(tn, n // n_shards)
    per = n // n_shards // tn

    def body(a_ref, b_ref, o_ref):
        o_ref[...] = _dot_tn(a_ref[...], b_ref[...]).astype(bf16)

    return _call(
        body, name=name, grid=(m // tm, n // tn),
        in_specs=[pl.BlockSpec((k, tm), lambda i, j: (0, i)), pl.BlockSpec((k, tn), lambda i, j: (0, j))],
        out_specs=pl.BlockSpec((None, tm, tn), lambda i, j: (j // per, i, j % per)),
        out_shape=jax.ShapeDtypeStruct((n_shards, m, n // n_shards), bf16))(a, b)


def _ffn_in_bwd_ln1(da_bf, dup_bf, wg_g, wu_g, dh2, xhat1, rstd1, ln1_g):
    s = da_bf.shape[0]
    tm = 256

    def body(da_ref, dup_ref, wg_ref, wu_ref, dh2_ref, xh_ref, rstd_ref, g_ref, dh_ref, dhb_ref, dg_ref, db_ref):
        i = pl.program_id(0)
        dx1 = ALPHA * dh2_ref[...]
        for sh in range(N_CHIPS):
            sl = slice(sh * FF_PAD, (sh + 1) * FF_PAD)
            dx1 = dx1 + _dot_nt(da_ref[:, sl], wg_ref[sh]) + _dot_nt(dup_ref[:, sl], wu_ref[sh])
        xhat = xh_ref[...]

        @pl.when(i == 0)
        def _():
            dg_ref[...] = jnp.zeros_like(dg_ref)
            db_ref[...] = jnp.zeros_like(db_ref)

        dg_ref[...] += jnp.sum(dx1 * xhat, axis=0, keepdims=True)
        db_ref[...] += jnp.sum(dx1, axis=0, keepdims=True)
        dh = _layer_norm_bwd(dx1, xhat, rstd_ref[:, 0:1], g_ref[...])
        dh_ref[...] = dh
        dhb_ref[...] = dh.astype(bf16)

    row = pl.BlockSpec((tm, D_MODEL), lambda i: (i, 0))
    wide = pl.BlockSpec((tm, D_FF_PAD), lambda i: (i, 0))
    wfull = pl.BlockSpec((N_CHIPS, D_MODEL, FF_PAD), lambda i: (0, 0, 0))
    vec = pl.BlockSpec((1, D_MODEL), lambda i: (0, 0))
    return _call(
        body, name="ffn_in_bwd_ln1", grid=(s // tm,),
        in_specs=[wide, wide, wfull, wfull, row, row, pl.BlockSpec((tm, LANES), lambda i: (i, 0)), vec],
        out_specs=[row, row, vec, vec],
        out_shape=[jax.ShapeDtypeStruct((s, D_MODEL), f32), jax.ShapeDtypeStruct((s, D_MODEL), bf16),
                   jax.ShapeDtypeStruct((1, D_MODEL), f32), jax.ShapeDtypeStruct((1, D_MODEL), f32)],
    )(da_bf, dup_bf, wg_g, wu_g, dh2, xhat1, rstd1, ln1_g)


def _merge_bwd(dh1_bf, wout, o_bf, pm_bf, z, wba, wbp):
    s = dh1_bf.shape[0]
    tm = 256

    def body(dh_ref, wout_ref, o_ref, pm_ref, ga_ref, gp_ref, wba_ref, wbp_ref,
             dzg_ref, dya_ref, dyp_ref, do_ref, dpm_ref, dbg_ref):
        i = pl.program_id(0)
        dm = _dot_nt(dh_ref[...], wout_ref[...])
        ya = _dot(o_ref[...], wba_ref[...])
        yp = _dot(pm_ref[...], wbp_ref[...])
        ga, gp = ga_ref[...], gp_ref[...]
        dza = dm * ya * ga * (1.0 - ga)
        dzp = dm * yp * gp * (1.0 - gp)

        @pl.when(i == 0)
        def _():
            dbg_ref[...] = jnp.zeros_like(dbg_ref)

        dbg_ref[:, 0:D_MODEL] += jnp.sum(dza, axis=0, keepdims=True)
        dbg_ref[:, D_MODEL:2 * D_MODEL] += jnp.sum(dzp, axis=0, keepdims=True)
        dzg_ref[:, 0:D_MODEL] = dza.astype(bf16)
        dzg_ref[:, D_MODEL:2 * D_MODEL] = dzp.astype(bf16)
        dya = (dm * ga).astype(bf16)
        dyp = (dm * gp).astype(bf16)
        dya_ref[...] = dya
        dyp_ref[...] = dyp
        do_ref[...] = _dot_nt(dya, wba_ref[...]).astype(bf16)
        dpm_ref[...] = _dot_nt(dyp, wbp_ref[...])

    row = pl.BlockSpec((tm, D_MODEL), lambda i: (i, 0))
    half = pl.BlockSpec((tm, D_ATTN), lambda i: (i, 0))
    full = lambda r, c: pl.BlockSpec((r, c), lambda i: (0, 0))
    return _call(
        body, name="merge_bwd", grid=(s // tm,),
        in_specs=[row, full(D_MODEL, D_MODEL), half, half,
                  pl.BlockSpec((tm, D_MODEL), lambda i: (i, 2)), pl.BlockSpec((tm, D_MODEL), lambda i: (i, 3)),
                  full(D_ATTN, D_MODEL), full(D_POOL, D_MODEL)],
        out_specs=[pl.BlockSpec((tm, 2 * D_MODEL), lambda i: (i, 0)), row, row, half, half,
                   pl.BlockSpec((1, 2 * D_MODEL), lambda i: (0, 0))],
        out_shape=[jax.ShapeDtypeStruct((s, 2 * D_MODEL), bf16), jax.ShapeDtypeStruct((s, D_MODEL), bf16),
                   jax.ShapeDtypeStruct((s, D_MODEL), bf16), jax.ShapeDtypeStruct((s, D_ATTN), bf16),
                   jax.ShapeDtypeStruct((s, D_POOL), f32), jax.ShapeDtypeStruct((1, 2 * D_MODEL), f32)],
    )(dh1_bf, wout, o_bf, pm_bf, z, z, wba, wbp)


def _pool_bwd(dpm, mixed, pooled_bf, w_pool, pool_scale):
    s = dpm.shape[0]

    def body(dpm_ref, mixed_ref, pooled_ref, w_ref, sc_ref, du_ref, dw_ref, dsc_ref, pad):
        g = pl.program_id(0)
        dpm_v = dpm_ref[...]
        dsc_ref[...] = jnp.sum(dpm_v * mixed_ref[...], axis=0, keepdims=True)
        dmixed = (dpm_v * sc_ref[...]).astype(bf16)
        dw_ref[...] = _dot_tn(pooled_ref[...], dmixed)
        dpooled = _dot_nt(dmixed, w_ref[...].astype(bf16))
        t = lax.broadcasted_iota(jnp.int32, (s, POOL_GROUP), 0)
        count = jnp.minimum(t + 1, jnp.left_shift(2, g)).astype(f32)
        r = dpooled / count
        pad[0:s, :] = r
        pad[s:s + MAX_WINDOW, :] = jnp.zeros((MAX_WINDOW, POOL_GROUP), f32)
        acc = r
        snaps = []
        for d in range(1, MAX_WINDOW):
            acc = acc + pad[d:d + s, :]
            if d + 1 in (2, 4, 8, 16):
                snaps.append(acc)
        du_ref[...] = (_window_select(snaps, g) - dpooled).astype(bf16)

    blk = pl.BlockSpec((s, POOL_GROUP), lambda g: (0, g))
    return _call(
        body, name="pool_bwd", grid=(4,),
        in_specs=[blk, blk, blk, pl.BlockSpec((None, POOL_GROUP, POOL_GROUP), lambda g: (g, 0, 0)),
                  pl.BlockSpec((1, POOL_GROUP), lambda g: (0, g))],
        out_specs=[blk, pl.BlockSpec((None, POOL_GROUP, POOL_GROUP), lambda g: (g, 0, 0)),
                   pl.BlockSpec((1, POOL_GROUP), lambda g: (0, g))],
        out_shape=[jax.ShapeDtypeStruct((s, D_POOL), bf16), jax.ShapeDtypeStruct((4, POOL_GROUP, POOL_GROUP), f32),
                   jax.ShapeDtypeStruct((1, D_POOL), f32)],
        scratch=[pltpu.VMEM((s + MAX_WINDOW, POOL_GROUP), f32)])(dpm, mixed, pooled_bf, w_pool, pool_scale)


def _attn_bwd(z, o_bf, lse, do_bf, cos_t, sin_t, nb):
    s = z.shape[0]

    def body(q_ref, k_ref, v_ref, o_ref, lse_ref, do_ref, cq_ref, sq_ref, cf_ref, sf_ref,
             dq_ref, dk_ref, dv_ref, dq_s, dk_s, dv_s):
        i = pl.program_id(1)
        lane = lax.broadcasted_iota(jnp.int32, (MOBA_BLOCK, LANES), 1)
        head0 = lane < HEAD_DIM
        heads = (head0, jnp.logical_not(head0))

        @pl.when(i == 0)
        def _():
            dk_s[...] = jnp.zeros_like(dk_s)
            dv_s[...] = jnp.zeros_like(dv_s)

        q = q_ref[...]
        sels = _moba_select(q, k_ref[...], i, nb)
        qs = q * SCALE
        qb = [jnp.where(hm, qs, 0.0).astype(bf16) for hm in heads]
        qraw = [jnp.where(hm, q, 0.0).astype(bf16) for hm in heads]
        do = do_ref[...].astype(f32)
        prod = do * o_ref[...].astype(f32)
        delta = [jnp.sum(jnp.where(hm, prod, 0.0), axis=1, keepdims=True) for hm in heads]
        dob = [jnp.where(hm, do, 0.0).astype(bf16) for hm in heads]
        lse_l = lse_ref[...]
        lse_h = [lse_l[:, 0:1], lse_l[:, HEAD_DIM:HEAD_DIM + 1]]
        row = lax.broadcasted_iota(jnp.int32, (MOBA_BLOCK, MOBA_BLOCK), 0)
        col = lax.broadcasted_iota(jnp.int32, (MOBA_BLOCK, MOBA_BLOCK), 1)
        dq_s[...] = jnp.zeros_like(dq_s)

        def block(rows, allow):
            kj = k_ref[rows, :].astype(bf16)
            vj = v_ref[rows, :].astype(bf16)
            dq, dk, dv = None, None, None
            for h in range(2):
                p = jnp.where(allow[h], jnp.exp(_dot_nt(qb[h], kj) - lse_h[h]), 0.0)
                dp = _dot_nt(dob[h], vj)
                ds = (p * (dp - delta[h]) * SCALE).astype(bf16)
                tv = _dot_tn(p.astype(bf16), dob[h])
                tq = _dot(ds, jnp.where(heads[h], kj, jnp.zeros_like(kj)))
                tk = _dot_tn(ds, qraw[h])
                dq, dk, dv = (tq, tk, tv) if dq is None else (dq + tq, dk + tk, dv + tv)
            dq_s[...] += dq
            dk_s[rows, :] += dk
            dv_s[rows, :] += dv

        block(pl.ds(pl.multiple_of(i * MOBA_BLOCK, MOBA_BLOCK), MOBA_BLOCK), [col <= row, col <= row])
        for j in range(nb - 1):
            @pl.when(j < i)
            def _(j=j):
                block(slice(j * MOBA_BLOCK, (j + 1) * MOBA_BLOCK), [sels[h][:, j:j + 1] > 0.0 for h in range(2)])

        dq_ref[...] = _rope_bwd(dq_s[...], cq_ref[...], sq_ref[...]).astype(bf16)

        @pl.when(i == nb - 1)
        def _():
            dk_ref[...] = _rope_bwd(dk_s[...], cf_ref[...], sf_ref[...]).astype(bf16)
            dv_ref[...] = dv_s[...].astype(bf16)

    qblk = pl.BlockSpec((MOBA_BLOCK, LANES), lambda hp, i: (i, hp))
    tq = pl.BlockSpec((MOBA_BLOCK, LANES), lambda hp, i: (i, 0))
    tf = pl.BlockSpec((s, LANES), lambda hp, i: (0, 0))
    colblk = pl.BlockSpec((s, LANES), lambda hp, i: (0, hp))
    return _call(
        body, name="attn_bwd", grid=(D_ATTN // LANES, nb),
        in_specs=[qblk, pl.BlockSpec((s, LANES), lambda hp, i: (0, 4 + hp)),
                  pl.BlockSpec((s, LANES), lambda hp, i: (0, 8 + hp)), qblk,
                  pl.BlockSpec((None, MOBA_BLOCK, LANES), lambda hp, i: (hp, i, 0)), qblk, tq, tq, tf, tf],
        out_specs=[qblk, colblk, colblk],
        out_shape=[jax.ShapeDtypeStruct((s, D_ATTN), bf16)] * 3,
        scratch=[pltpu.VMEM((MOBA_BLOCK, LANES), f32), pltpu.VMEM((s, LANES), f32), pltpu.VMEM((s, LANES), f32)],
    )(z, z, z, o_bf, lse, do_bf, cos_t, sin_t, cos_t, sin_t)


def _inproj_dx(dz_bf, win_g, dh1):
    s = dz_bf.shape[0]
    tm = 256

    def body(dz_ref, w_ref, dh_ref, gx_ref):
        acc = ALPHA * dh_ref[...]
        for sh in range(N_CHIPS):
            acc = acc + _dot_nt(dz_ref[:, sh * D_MODEL:(sh + 1) * D_MODEL], w_ref[sh])
        gx_ref[...] = acc

    row = pl.BlockSpec((tm, D_MODEL), lambda i: (i, 0))
    return _call(
        body, name="inproj_dx", grid=(s // tm,),
        in_specs=[pl.BlockSpec((tm, 4 * D_MODEL), lambda i: (i, 0)),
                  pl.BlockSpec((N_CHIPS, D_MODEL, D_MODEL), lambda i: (0, 0, 0)), row],
        out_specs=row, out_shape=jax.ShapeDtypeStruct((s, D_MODEL), f32))(dz_bf, win_g, dh1)


ANY = pl.BlockSpec(memory_space=pl.ANY)


def _chip_index():
    return 2 * lax.axis_index("x") + lax.axis_index("y")


def _peer(k):
    x, y, c = lax.axis_index("x"), lax.axis_index("y"), lax.axis_index("c")
    return (x ^ (k >> 1), y ^ (k & 1), c)


def _sibling():
    return (lax.axis_index("x"), lax.axis_index("y"), 1 - lax.axis_index("c"))


def _all_gather_shards(shards):
    n = len(shards)
    split = [a.reshape(2, a.shape[0] // 2, a.shape[1]) for a in shards]

    def body(*refs):
        src, dst = refs[:n], refs[n:2 * n]
        send_sems, recv_sems, fsend_sems, frecv_sems, local_sems = refs[2 * n:]
        p, c = _chip_index(), lax.axis_index("c")
        local = [pltpu.make_async_copy(src[a], dst[a].at[p], local_sems.at[a]) for a in range(n)]
        for cp in local:
            cp.start()
        sends = []
        for a in range(n):
            for k in range(1, N_CHIPS):
                cp = pltpu.make_async_remote_copy(
                    src_ref=src[a].at[c], dst_ref=dst[a].at[p, c], send_sem=send_sems.at[k - 1, a],
                    recv_sem=recv_sems.at[k - 1, a], device_id=_peer(k), device_id_type=MESH)
                cp.start()
                sends.append(cp)
        for a in range(n):
            for k in range(1, N_CHIPS):
                got = dst[a].at[p ^ k, c]
                pltpu.make_async_remote_copy(
                    src_ref=src[a].at[c], dst_ref=got, send_sem=send_sems.at[k - 1, a],
                    recv_sem=recv_sems.at[k - 1, a], device_id=_peer(k), device_id_type=MESH).wait_recv()
                cp = pltpu.make_async_remote_copy(
                    src_ref=got, dst_ref=got, send_sem=fsend_sems.at[k - 1, a], recv_sem=frecv_sems.at[k - 1, a],
                    device_id=_sibling(), device_id_type=MESH)
                cp.start()
                sends.append(cp)
        for a in range(n):
            for k in range(1, N_CHIPS):
                theirs = dst[a].at[p ^ k, 1 - c]
                pltpu.make_async_remote_copy(
                    src_ref=theirs, dst_ref=theirs, send_sem=fsend_sems.at[k - 1, a], recv_sem=frecv_sems.at[k - 1, a],
                    device_id=_sibling(), device_id_type=MESH).wait_recv()
        for cp in sends:
            cp.wait_send()
        for cp in local:
            cp.wait()

    sem = pltpu.SemaphoreType.DMA((N_CHIPS - 1, n))
    out = pl.pallas_call(
        body, name="all_gather_weights",
        in_specs=[ANY] * n, out_specs=[ANY] * n,
        out_shape=[jax.ShapeDtypeStruct((N_CHIPS,) + a.shape, a.dtype) for a in split],
        scratch_shapes=[sem, sem, sem, sem, pltpu.SemaphoreType.DMA((n,))])(*split)
    return [o.reshape((N_CHIPS,) + a.shape) for o, a in zip(out, shards)]


def _presum_swap(grads):
    n = len(grads)
    split = [g.reshape(g.shape[0], 2, g.shape[1] // 2, g.shape[2]) for g in grads]

    def body(*refs):
        src, own, got = refs[:n], refs[n:2 * n], refs[2 * n:3 * n]
        send_sems, recv_sems, local_sems = refs[3 * n:]
        c = lax.axis_index("c")
        local = [pltpu.make_async_copy(src[a].at[:, c], own[a], local_sems.at[a]) for a in range(n)]
        remote = [pltpu.make_async_remote_copy(src_ref=src[a].at[:, 1 - c], dst_ref=got[a],
                                               send_sem=send_sems.at[a], recv_sem=recv_sems.at[a],
                                               device_id=_sibling(), device_id_type=MESH) for a in range(n)]
        for cp in remote + local:
            cp.start()
        for cp in remote:
            cp.wait_recv()
        for cp in remote:
            cp.wait_send()
        for cp in local:
            cp.wait()

    halves = [jax.ShapeDtypeStruct((g.shape[0], g.shape[1] // 2, g.shape[2]), g.dtype) for g in grads]
    sem = pltpu.SemaphoreType.DMA((n,))
    out = pl.pallas_call(
        body, name="presum_swap", in_specs=[ANY] * n, out_specs=[ANY] * (2 * n), out_shape=halves + halves,
        scratch_shapes=[sem, sem, sem])(*split)
    return out[:n], out[n:]


def _add_pair(a, b, name):
    lead, rows, cols = a.shape
    tm = 256 if rows % 256 == 0 else rows

    def body(a_ref, b_ref, o_ref):
        o_ref[...] = (a_ref[...].astype(f32) + b_ref[...].astype(f32)).astype(o_ref.dtype)

    blk = pl.BlockSpec((lead, tm, cols), lambda i: (0, i, 0))
    return _call(body, name=name, grid=(rows // tm,), in_specs=[blk, blk], out_specs=blk,
                 out_shape=jax.ShapeDtypeStruct(a.shape, a.dtype))(a, b)


def _sibling_fill(halves):
    n = len(halves)

    def body(*refs):
        src, dst = refs[:n], refs[n:2 * n]
        send_sems, recv_sems, local_sems = refs[2 * n:]
        c = lax.axis_index("c")
        place = lambda a, cc: dst[a].at[cc]
        local = [pltpu.make_async_copy(src[a], place(a, c), local_sems.at[a]) for a in range(n)]
        remote = [pltpu.make_async_remote_copy(src_ref=src[a], dst_ref=place(a, c), send_sem=send_sems.at[a],
                                               recv_sem=recv_sems.at[a], device_id=_sibling(), device_id_type=MESH)
                  for a in range(n)]
        for cp in remote + local:
            cp.start()
        for a in range(n):
            pltpu.make_async_remote_copy(src_ref=src[a], dst_ref=place(a, 1 - c), send_sem=send_sems.at[a],
                                         recv_sem=recv_sems.at[a], device_id=_sibling(), device_id_type=MESH).wait_recv()
        for cp in remote:
            cp.wait_send()
        for cp in local:
            cp.wait()

    sem = pltpu.SemaphoreType.DMA((n,))
    out = pl.pallas_call(
        body, name="sibling_fill", in_specs=[ANY] * n, out_specs=[ANY] * n,
        out_shape=[jax.ShapeDtypeStruct((2,) + h.shape, h.dtype) for h in halves],
        scratch_shapes=[sem, sem, sem])(*halves)
    return [o.reshape(2 * h.shape[0], h.shape[1]) for o, h in zip(out, halves)]


def _exchange_slices(grads, whole):
    n = len(grads)

    def body(*refs):
        src, dst = refs[:n], refs[n:2 * n]
        send_sems, recv_sems, local_sems = refs[2 * n:]
        p = _chip_index()
        pick = lambda a, q: src[a].at[0] if whole[a] else src[a].at[q]
        local = [pltpu.make_async_copy(pick(a, p), dst[a].at[0], local_sems.at[a]) for a in range(n)]
        for cp in local:
            cp.start()
        remote = []
        for k in range(1, N_CHIPS):
            for a in range(n):
                cp = pltpu.make_async_remote_copy(
                    src_ref=pick(a, p ^ k), dst_ref=dst[a].at[k], send_sem=send_sems.at[k - 1, a],
                    recv_sem=recv_sems.at[k - 1, a], device_id=_peer(k), device_id_type=MESH)
                cp.start()
                remote.append(cp)
        for cp in remote:
            cp.wait_recv()
        for cp in remote:
            cp.wait_send()
        for cp in local:
            cp.wait()

    return pl.pallas_call(
        body, name="exchange_grad_slices",
        in_specs=[ANY] * n, out_specs=[ANY] * n,
        out_shape=[jax.ShapeDtypeStruct((N_CHIPS,) + g.shape[1:], g.dtype) for g in grads],
        scratch_shapes=[pltpu.SemaphoreType.DMA((N_CHIPS - 1, n)), pltpu.SemaphoreType.DMA((N_CHIPS - 1, n)),
                        pltpu.SemaphoreType.DMA((n,))])(*grads)


def _sum_slots(r, name):
    _, rows, cols = r.shape
    tm = 256 if rows % 256 == 0 else rows

    def body(r_ref, o_ref):
        r0, r1, r2, r3 = [r_ref[k].astype(f32) for k in range(N_CHIPS)]
        o_ref[...] = (r0 + r1) + (r2 + r3)

    return _call(
        body, name=name, grid=(rows // tm,),
        in_specs=[pl.BlockSpec((N_CHIPS, tm, cols), lambda i: (0, i, 0))],
        out_specs=pl.BlockSpec((tm, cols), lambda i: (i, 0)),
        out_shape=jax.ShapeDtypeStruct((rows, cols), f32))(r)


def _adamw(w, m, v, grad, name, tm):
    rows, cols = w.shape
    gcols = grad.shape[1]
    if tm == rows:
        gspec = pl.BlockSpec(grad.shape, lambda i: (0, 0))
    else:
        gspec = pl.BlockSpec((tm, gcols), lambda i: (i, 0))

    def body(w_ref, m_ref, v_ref, gin_ref, g_ref, d_ref, nm_ref, nv_ref):
        g = gin_ref[0:tm, 0:cols]
        w_, m_, v_ = w_ref[...], m_ref[...], v_ref[...]
        m_ = ADAM_B1 * m_ + (1.0 - ADAM_B1) * g
        v_ = ADAM_B2 * v_ + (1.0 - ADAM_B2) * jnp.square(g)
        m_hat = m_ / (1.0 - ADAM_B1 ** ADAM_STEP)
        v_hat = v_ / (1.0 - ADAM_B2 ** ADAM_STEP)
        g_ref[...] = g
        d_ref[...] = -ADAM_LR * (m_hat / (jnp.sqrt(v_hat) + ADAM_EPS) + ADAM_WD * w_)
        nm_ref[...] = m_
        nv_ref[...] = v_

    blk = pl.BlockSpec((tm, cols), lambda i: (i, 0))
    return _call(
        body, name=name, grid=(rows // tm,),
        in_specs=[blk, blk, blk, gspec], out_specs=[blk] * 4,
        out_shape=[jax.ShapeDtypeStruct((rows, cols), f32)] * 4)(w, m, v, grad)


def _rope_tables(s):
    half = HEAD_DIM // 2
    inv_freq = 1.0 / (10000.0 ** (jnp.arange(half, dtype=f32) / half))
    ang = jnp.arange(s, dtype=f32)[:, None] * inv_freq[None, :]
    cos, sin = jnp.cos(ang), jnp.sin(ang)
    return jnp.tile(cos, (1, LANES // half)), jnp.tile(jnp.concatenate([-sin, sin], axis=1), (1, LANES // HEAD_DIM))


def _local_step(x, target, gw, b_gate, w_pool, pool_scale, ln1_g, ln1_b, convb_g, ln2_g, ln2_b):
    s = x.shape[0]
    nb = s // MOBA_BLOCK
    win_g, wba_g, wbp_g, wout_g, wg_g, wu_g, wd_g, convw_g = gw
    wba = wba_g.transpose(1, 0, 2).reshape(D_ATTN, D_MODEL)
    wbp = wbp_g.transpose(1, 0, 2).reshape(D_POOL, D_MODEL)
    wout = wout_g.reshape(D_MODEL, D_MODEL)
    wd = wd_g.reshape(D_FF_PAD, D_MODEL)
    cos_t, sin_t = _rope_tables(s)
    x_bf = x.astype(bf16)

    z = _inproj(x_bf, win_g, cos_t, sin_t, b_gate)
    o_bf, lse = _attn_fwd(z, nb)
    pooled_bf, mixed, pm_bf = _pool_fwd(z, w_pool, pool_scale)
    m_bf = _branch_merge(o_bf, pm_bf, z, wba, wbp)
    x1_bf, xhat1, rstd1 = _out_ln1(m_bf, wout, x, ln1_g, ln1_b)
    a, up, hh_bf = _ffn_up(x1_bf, wg_g, wu_g, convw_g, convb_g)
    loss, dh2, dh2_bf, d_ln2_g, d_ln2_b = _ffn_down_ln2_loss(hh_bf, wd, xhat1, ln1_g, ln1_b, ln2_g, ln2_b, target)

    da_bf, dup_bf, dconv = _ffn_act_bwd(dh2_bf, wd_g, a, up, convw_g, convb_g)
    d_wg = _matmul_tn(x1_bf, da_bf, N_CHIPS, "dw_ffn_gate")
    d_wu = _matmul_tn(x1_bf, dup_bf, N_CHIPS, "dw_ffn_up")
    d_wd = _matmul_tn(hh_bf, dh2_bf, 1, "dw_ffn_down").reshape(N_CHIPS, FF_PAD, D_MODEL)
    dh1, dh1_bf, d_ln1_g, d_ln1_b = _ffn_in_bwd_ln1(da_bf, dup_bf, wg_g, wu_g, dh2, xhat1, rstd1, ln1_g)
    d_wout = _matmul_tn(m_bf, dh1_bf, 1, "dw_out").reshape(N_CHIPS, D_MODEL // N_CHIPS, D_MODEL)
    dzg_bf, dya_bf, dyp_bf, do_bf, dpm, d_bgate = _merge_bwd(dh1_bf, wout, o_bf, pm_bf, z, wba, wbp)
    d_wba = _matmul_tn(o_bf, dya_bf, N_CHIPS, "dw_branch_attn")
    d_wbp = _matmul_tn(pm_bf, dyp_bf, N_CHIPS, "dw_branch_pool")
    du_bf, d_wpool, d_pscale = _pool_bwd(dpm, mixed, pooled_bf, w_pool, pool_scale)
    dq_bf, dk_bf, dv_bf = _attn_bwd(z, o_bf, lse, do_bf, cos_t, sin_t, nb)
    dz_bf = jnp.concatenate([dq_bf, dk_bf, dv_bf, du_bf, dzg_bf], axis=1)
    d_win = _matmul_tn(x_bf, dz_bf, N_CHIPS, "dw_in")
    grad_x = _inproj_dx(dz_bf, win_g, dh1)

    sharded = [d_win, d_wba, d_wbp, d_wout, d_wg, d_wu, d_wd, dconv]
    small = [d_bgate, d_wpool, d_pscale, d_ln1_g, d_ln1_b, d_ln2_g, d_ln2_b, dconv[:, 3, :]]
    return loss[0, 0], grad_x, sharded, small


SMALL_ROWS = 592


def _pack_small(parts):
    flat = jnp.concatenate([p.reshape(-1) for p in parts])
    return jnp.pad(flat, (0, SMALL_ROWS * LANES - flat.shape[0])).reshape(SMALL_ROWS, LANES)


def _unpack_small(packed, shapes):
    flat = packed.reshape(-1)
    out, off = [], 0
    for shp in shapes:
        n = 1
        for d in shp:
            n *= d
        out.append(flat[off:off + n].reshape(shp))
        off += n
    return out


def _pad_conv_b(cb):
    return jnp.pad(cb.reshape(N_CHIPS, FF_SHARD), ((0, 0), (0, FF_PAD - FF_SHARD)))


def kernel(x, w_in, b_gate, w_branch_attn, w_pool, pool_scale, w_branch_pool, w_out, ln1_g, ln1_b, w_ffn_gate, w_ffn_up, conv_w, conv_b, w_ffn_down, ln2_g, ln2_b, loss_target, m_w_in, m_b_gate, m_w_branch_attn, m_w_pool, m_pool_scale, m_w_branch_pool, m_w_out, m_ln1_g, m_ln1_b, m_w_ffn_gate, m_w_ffn_up, m_conv_w, m_conv_b, m_w_ffn_down, m_ln2_g, m_ln2_b, v_w_in, v_b_gate, v_w_branch_attn, v_w_pool, v_pool_scale, v_w_branch_pool, v_w_out, v_ln1_g, v_ln1_b, v_w_ffn_gate, v_w_ffn_up, v_conv_w, v_conv_b, v_w_ffn_down, v_ln2_g, v_ln2_b):
    pad_cols = lambda w: jnp.pad(w, ((0, 0), (0, FF_PAD - FF_SHARD)))
    shards = [w_in[0].astype(bf16), w_branch_attn[0].astype(bf16), w_branch_pool[0].astype(bf16),
              w_out[0].astype(bf16), pad_cols(w_ffn_gate[0].astype(bf16)), pad_cols(w_ffn_up[0].astype(bf16)),
              jnp.pad(w_ffn_down[0].astype(bf16), ((0, FF_PAD - FF_SHARD), (0, 0))),
              jnp.pad(conv_w[0], ((0, CONV_ROWS - 3), (0, FF_PAD - FF_SHARD)))]
    gathered = _all_gather_shards(shards)

    convb_g = _pad_conv_b(conv_b).reshape(N_CHIPS, 1, FF_PAD)
    loss, grad_x, sharded, small = _local_step(
        x[0], loss_target[0], gathered, b_gate, w_pool[0], pool_scale, ln1_g, ln1_b, convb_g, ln2_g, ln2_b)
    loss = lax.psum(loss, ("x", "y", "c"))

    names = ["w_in", "w_branch_attn", "w_branch_pool", "w_out", "w_ffn_gate", "w_ffn_up", "w_ffn_down", "conv_w", "small"]
    own, got = _presum_swap(sharded + [_pack_small(small)[None]])
    chip_sums = [_add_pair(a, b, "presum_" + nm) for a, b, nm in zip(own, got, names)]
    slots = _exchange_slices(chip_sums, [False] * len(sharded) + [True])
    grads = _sibling_fill([_sum_slots(r, "sum_" + nm) for r, nm in zip(slots, names)])

    weights = [w_in[0], w_branch_attn[0], w_branch_pool[0], w_out[0], w_ffn_gate[0], w_ffn_up[0], w_ffn_down[0], conv_w[0]]
    m_in = [m_w_in[0], m_w_branch_attn[0], m_w_branch_pool[0], m_w_out[0], m_w_ffn_gate[0], m_w_ffn_up[0], m_w_ffn_down[0], m_conv_w[0]]
    v_in = [v_w_in[0], v_w_branch_attn[0], v_w_branch_pool[0], v_w_out[0], v_w_ffn_gate[0], v_w_ffn_up[0], v_w_ffn_down[0], v_conv_w[0]]
    tiles = [256, 256, 256, 128, 256, 256, 88, 3]
    res = {}
    for i, nm in enumerate(names[:-1]):
        res[nm] = [r[None] for r in _adamw(weights[i], m_in[i], v_in[i], grads[i], "adamw_" + nm, tiles[i])]

    small_names = ["b_gate", "w_pool", "pool_scale", "ln1_g", "ln1_b", "ln2_g", "ln2_b", "conv_b"]
    small_w = [b_gate, w_pool, pool_scale, ln1_g, ln1_b, ln2_g, ln2_b, _pad_conv_b(conv_b)]
    small_m = [m_b_gate, m_w_pool, m_pool_scale, m_ln1_g, m_ln1_b, m_ln2_g, m_ln2_b, _pad_conv_b(m_conv_b)]
    small_v = [v_b_gate, v_w_pool, v_pool_scale, v_ln1_g, v_ln1_b, v_ln2_g, v_ln2_b, _pad_conv_b(v_conv_b)]
    packed = _adamw(_pack_small(small_w), _pack_small(small_m), _pack_small(small_v), grads[-1],
                    "adamw_small", SMALL_ROWS)
    shapes = [w.shape for w in small_w]
    for kind in range(4):
        for nm, val in zip(small_names, _unpack_small(packed[kind], shapes)):
            if nm == "conv_b":
                val = val[:, :FF_SHARD].reshape(1, 4 * FF_SHARD)
            res.setdefault(nm, [None] * 4)[kind] = val

    order = ["w_in", "b_gate", "w_branch_attn", "w_pool", "pool_scale", "w_branch_pool", "w_out", "ln1_g", "ln1_b",
             "w_ffn_gate", "w_ffn_up", "conv_w", "conv_b", "w_ffn_down", "ln2_g", "ln2_b"]
    outs = [loss, grad_x[None]]
    for kind in range(4):
        outs += [res[nm][kind] for nm in order]
    return tuple(outs)
```

```python
import functools

import jax
import jax.numpy as jnp
from jax import lax
from jax.experimental import pallas as pl
from jax.experimental.pallas import tpu as pltpu

D_MODEL = 1024
HEAD_DIM = 64
D_ATTN = 512
D_POOL = 512
MOBA_BLOCK = 256
MOBA_TOPK = 3
POOL_GROUP = 128
MAX_WINDOW = 16
FF_SHARD = 704
FF_PAD = 768
D_FF_PAD = 4 * FF_PAD
N_CHIPS = 4
LANES = 128
ALPHA = (2.0 * 1) ** 0.25
LN_EPS = 1e-5
NEG = -1e30
SCALE = HEAD_DIM ** -0.5
ADAM_LR, ADAM_B1, ADAM_B2, ADAM_EPS, ADAM_WD, ADAM_STEP = 0.001, 0.9, 0.999, 1e-08, 0.01, 10
VMEM_LIMIT = 56 * 1024 * 1024
MESH = pl.DeviceIdType.MESH

bf16 = jnp.bfloat16
f32 = jnp.float32


def _dot(a, b):
    return jnp.dot(a, b, preferred_element_type=f32)


def _dot_nt(a, b):
    return lax.dot_general(a, b, (((1,), (1,)), ((), ())), preferred_element_type=f32)


def _dot_tn(a, b):
    return lax.dot_general(a, b, (((0,), (0,)), ((), ())), preferred_element_type=f32)


def _call(body, *, name, grid, in_specs, out_specs, out_shape, scratch=()):
    return pl.pallas_call(
        body, name=name, grid=grid, in_specs=in_specs, out_specs=out_specs, out_shape=out_shape,
        scratch_shapes=list(scratch),
        compiler_params=pltpu.CompilerParams(dimension_semantics=("arbitrary",) * len(grid),
                                             vmem_limit_bytes=VMEM_LIMIT))


def _rot_half(t):
    lane = lax.broadcasted_iota(jnp.int32, t.shape, 1)
    first = (lane % HEAD_DIM) < (HEAD_DIM // 2)
    return jnp.where(first, pltpu.roll(t, LANES - HEAD_DIM // 2, 1), pltpu.roll(t, HEAD_DIM // 2, 1))


def _rope(t, cos, sin_signed):
    return t * cos + _rot_half(t) * sin_signed


def _rope_bwd(d, cos, sin_signed):
    return d * cos + _rot_half(d * sin_signed)


def _gelu_parts(a):
    cdf = 0.5 * (1.0 + lax.erf(a * (2.0 ** -0.5)))
    pdf = jnp.exp(-0.5 * a * a) * ((2.0 * jnp.pi) ** -0.5)
    return a * cdf, cdf + a * pdf


def _layer_norm(h, g, b):
    mu = jnp.mean(h, axis=-1, keepdims=True)
    xc = h - mu
    var = jnp.mean(xc * xc, axis=-1, keepdims=True)
    rstd = lax.rsqrt(var + LN_EPS)
    xhat = xc * rstd
    return xhat * g + b, xhat, rstd


def _layer_norm_bwd(dy, xhat, rstd, g):
    dxh = dy * g
    m1 = jnp.mean(dxh, axis=-1, keepdims=True)
    m2 = jnp.mean(dxh * xhat, axis=-1, keepdims=True)
    return rstd * (dxh - m1 - xhat * m2)


def _inproj(x_bf, win_g, cos_t, sin_t, b_gate):
    s = x_bf.shape[0]
    tm, tn = 512, 512

    def body(x_ref, w_ref, cos_ref, sin_ref, b_ref, o_ref):
        j = pl.program_id(1)
        acc = _dot(x_ref[...], w_ref[...])

        @pl.when(j < 2)
        def _():
            for c in range(tn // LANES):
                sl = slice(c * LANES, (c + 1) * LANES)
                o_ref[:, sl] = _rope(acc[:, sl], cos_ref[...], sin_ref[...])

        @pl.when((j >= 2) & (j < 4))
        def _():
            o_ref[...] = acc

        @pl.when(j >= 4)
        def _():
            o_ref[...] = jax.nn.sigmoid(acc + b_ref[...])

    return _call(
        body, name="inproj", grid=(s // tm, 4 * D_MODEL // tn),
        in_specs=[pl.BlockSpec((tm, D_MODEL), lambda i, j: (i, 0)),
                  pl.BlockSpec((None, D_MODEL, tn), lambda i, j: (j // 2, 0, j % 2)),
                  pl.BlockSpec((tm, LANES), lambda i, j: (i, 0)),
                  pl.BlockSpec((tm, LANES), lambda i, j: (i, 0)),
                  pl.BlockSpec((1, tn), lambda i, j: (0, jnp.maximum(j - 4, 0)))],
        out_specs=pl.BlockSpec((tm, tn), lambda i, j: (i, j)),
        out_shape=jax.ShapeDtypeStruct((s, 4 * D_MODEL), f32))(x_bf, win_g, cos_t, sin_t, b_gate)


def _moba_select(q, k_all, i, nb):
    lane = lax.broadcasted_iota(jnp.int32, q.shape, 1)
    k_mean = jnp.mean(k_all.reshape(nb, MOBA_BLOCK, LANES), axis=1).astype(bf16)
    n_io = lax.broadcasted_iota(jnp.int32, (MOBA_BLOCK, nb), 1)
    past = n_io < i
    sels = []
    for h in range(2):
        head = (lane < HEAD_DIM) if h == 0 else (lane >= HEAD_DIM)
        gate = _dot_nt(jnp.where(head, q, 0.0).astype(bf16), k_mean)
        g = jnp.where(past, gate, NEG)
        rank = jnp.zeros((MOBA_BLOCK, nb), f32)
        for m in range(nb):
            gm = g[:, m:m + 1]
            rank = rank + jnp.where((gm > g) | ((gm == g) & (m < n_io)), 1.0, 0.0)
        sels.append(jnp.where(past & (rank < MOBA_TOPK), 1.0, 0.0))
    return sels


def _attn_fwd(z, nb):
    s = z.shape[0]

    def body(q_ref, k_ref, v_ref, o_ref, lse_ref, m_s, l_s, acc_s):
        i = pl.program_id(1)
        lane = lax.broadcasted_iota(jnp.int32, (MOBA_BLOCK, LANES), 1)
        head0 = lane < HEAD_DIM
        heads = (head0, jnp.logical_not(head0))
        q = q_ref[...]
        sels = _moba_select(q, k_ref[...], i, nb)
        qs = q * SCALE
        qb = [jnp.where(hm, qs, 0.0).astype(bf16) for hm in heads]
        row = lax.broadcasted_iota(jnp.int32, (MOBA_BLOCK, MOBA_BLOCK), 0)
        col = lax.broadcasted_iota(jnp.int32, (MOBA_BLOCK, MOBA_BLOCK), 1)

        own = pl.ds(pl.multiple_of(i * MOBA_BLOCK, MOBA_BLOCK), MOBA_BLOCK)
        kj = k_ref[own, :].astype(bf16)
        vj = v_ref[own, :]
        ms, ls, pv = [], [], None
        for h in range(2):
            sc = jnp.where(col <= row, _dot_nt(qb[h], kj), NEG)
            m = jnp.max(sc, axis=1, keepdims=True)
            p = jnp.exp(sc - m)
            ms.append(m)
            ls.append(jnp.sum(p, axis=1, keepdims=True))
            t = _dot(p.astype(bf16), jnp.where(heads[h], vj, 0.0).astype(bf16))
            pv = t if pv is None else pv + t
        m_s[...] = jnp.where(head0, ms[0], ms[1])
        l_s[...] = jnp.where(head0, ls[0], ls[1])
        acc_s[...] = pv

        for j in range(nb - 1):
            @pl.when(j < i)
            def _(j=j):
                kj = k_ref[j * MOBA_BLOCK:(j + 1) * MOBA_BLOCK, :].astype(bf16)
                vj = v_ref[j * MOBA_BLOCK:(j + 1) * MOBA_BLOCK, :]
                m_old_l, l_old_l = m_s[...], l_s[...]
                m_new, alpha, l_new, pv = [], [], [], None
                for h in range(2):
                    c0 = h * HEAD_DIM
                    sc = jnp.where(sels[h][:, j:j + 1] > 0.0, _dot_nt(qb[h], kj), NEG)
                    m_old = m_old_l[:, c0:c0 + 1]
                    mn = jnp.maximum(m_old, jnp.max(sc, axis=1, keepdims=True))
                    a = jnp.exp(m_old - mn)
                    p = jnp.exp(sc - mn)
                    m_new.append(mn)
                    alpha.append(a)
                    l_new.append(a * l_old_l[:, c0:c0 + 1] + jnp.sum(p, axis=1, keepdims=True))
                    t = _dot(p.astype(bf16), jnp.where(heads[h], vj, 0.0).astype(bf16))
                    pv = t if pv is None else pv + t
                m_s[...] = jnp.where(head0, m_new[0], m_new[1])
                l_s[...] = jnp.where(head0, l_new[0], l_new[1])
                acc_s[...] = acc_s[...] * jnp.where(head0, alpha[0], alpha[1]) + pv

        o_ref[...] = (acc_s[...] / l_s[...]).astype(bf16)
        lse_ref[...] = m_s[...] + jnp.log(l_s[...])

    return _call(
        body, name="attn_fwd", grid=(D_ATTN // LANES, nb),
        in_specs=[pl.BlockSpec((MOBA_BLOCK, LANES), lambda hp, i: (i, hp)),
                  pl.BlockSpec((s, LANES), lambda hp, i: (0, 4 + hp)),
                  pl.BlockSpec((s, LANES), lambda hp, i: (0, 8 + hp))],
        out_specs=[pl.BlockSpec((MOBA_BLOCK, LANES), lambda hp, i: (i, hp)),
                   pl.BlockSpec((None, MOBA_BLOCK, LANES), lambda hp, i: (hp, i, 0))],
        out_shape=[jax.ShapeDtypeStruct((s, D_ATTN), bf16),
                   jax.ShapeDtypeStruct((D_ATTN // LANES, s, LANES), f32)],
        scratch=[pltpu.VMEM((MOBA_BLOCK, LANES), f32)] * 3)(z, z, z)


def _window_select(snaps, g):
    return jnp.where(g == 0, snaps[0], jnp.where(g == 1, snaps[1], jnp.where(g == 2, snaps[2], snaps[3])))


def _pool_fwd(z, w_pool, pool_scale):
    s = z.shape[0]

    def body(u_ref, w_ref, sc_ref, pooled_ref, mixed_ref, pm_ref, pad):
        g = pl.program_id(0)
        u = u_ref[...]
        pad[0:MAX_WINDOW, :] = jnp.zeros((MAX_WINDOW, POOL_GROUP), f32)
        pad[MAX_WINDOW:MAX_WINDOW + s, :] = u
        acc = u
        snaps = []
        for d in range(1, MAX_WINDOW):
            acc = acc + pad[MAX_WINDOW - d:MAX_WINDOW - d + s, :]
            if d + 1 in (2, 4, 8, 16):
                snaps.append(acc)
        win = _window_select(snaps, g)
        t = lax.broadcasted_iota(jnp.int32, (s, POOL_GROUP), 0)
        count = jnp.minimum(t + 1, jnp.left_shift(2, g)).astype(f32)
        pooled = (win / count - u).astype(bf16)
        mixed = _dot(pooled, w_ref[...].astype(bf16))
        pooled_ref[...] = pooled
        mixed_ref[...] = mixed
        pm_ref[...] = (mixed * sc_ref[...]).astype(bf16)

    blk = pl.BlockSpec((s, POOL_GROUP), lambda g: (0, g))
    return _call(
        body, name="pool_fwd", grid=(4,),
        in_specs=[pl.BlockSpec((s, POOL_GROUP), lambda g: (0, 12 + g)),
                  pl.BlockSpec((None, POOL_GROUP, POOL_GROUP), lambda g: (g, 0, 0)),
                  pl.BlockSpec((1, POOL_GROUP), lambda g: (0, g))],
        out_specs=[blk, blk, blk],
        out_shape=[jax.ShapeDtypeStruct((s, D_POOL), bf16), jax.ShapeDtypeStruct((s, D_POOL), f32),
                   jax.ShapeDtypeStruct((s, D_POOL), bf16)],
        scratch=[pltpu.VMEM((s + MAX_WINDOW, POOL_GROUP), f32)])(z, w_pool, pool_scale)


def _branch_merge(o_bf, pm_bf, z, wba, wbp):
    s = o_bf.shape[0]
    tm = 256

    def body(o_ref, pm_ref, ga_ref, gp_ref, wba_ref, wbp_ref, m_ref):
        ya = _dot(o_ref[...], wba_ref[...])
        yp = _dot(pm_ref[...], wbp_ref[...])
        m_ref[...] = (ga_ref[...] * ya + gp_ref[...] * yp).astype(bf16)

    full = lambda r, c: pl.BlockSpec((r, c), lambda i: (0, 0))
    return _call(
        body, name="branch_merge", grid=(s // tm,),
        in_specs=[pl.BlockSpec((tm, D_ATTN), lambda i: (i, 0)), pl.BlockSpec((tm, D_POOL), lambda i: (i, 0)),
                  pl.BlockSpec((tm, D_MODEL), lambda i: (i, 2)), pl.BlockSpec((tm, D_MODEL), lambda i: (i, 3)),
                  full(D_ATTN, D_MODEL), full(D_POOL, D_MODEL)],
        out_specs=pl.BlockSpec((tm, D_MODEL), lambda i: (i, 0)),
        out_shape=jax.ShapeDtypeStruct((s, D_MODEL), bf16))(o_bf, pm_bf, z, z, wba, wbp)


def _out_ln1(m_bf, wout, x, ln_g, ln_b):
    s = x.shape[0]
    tm = 256

    def body(m_ref, w_ref, x_ref, g_ref, b_ref, x1_ref, xhat_ref, rstd_ref):
        h = ALPHA * x_ref[...] + _dot(m_ref[...], w_ref[...])
        y, xhat, rstd = _layer_norm(h, g_ref[...], b_ref[...])
        x1_ref[...] = y.astype(bf16)
        xhat_ref[...] = xhat
        rstd_ref[...] = jnp.broadcast_to(rstd, (tm, LANES))

    row = pl.BlockSpec((tm, D_MODEL), lambda i: (i, 0))
    vec = pl.BlockSpec((1, D_MODEL), lambda i: (0, 0))
    return _call(
        body, name="out_ln1", grid=(s // tm,),
        in_specs=[row, pl.BlockSpec((D_MODEL, D_MODEL), lambda i: (0, 0)), row, vec, vec],
        out_specs=[row, row, pl.BlockSpec((tm, LANES), lambda i: (i, 0))],
        out_shape=[jax.ShapeDtypeStruct((s, D_MODEL), bf16), jax.ShapeDtypeStruct((s, D_MODEL), f32),
                   jax.ShapeDtypeStruct((s, LANES), f32)])(m_bf, wout, x, ln_g, ln_b)


FF_TILE = 256
FF_TILES_PER_SHARD = FF_PAD // FF_TILE
CONV_PAD = 8
CONV_ROWS = 16


def _ff_weight_spec(rows):
    return pl.BlockSpec((None, rows, FF_TILE), lambda j: (j // FF_TILES_PER_SHARD, 0, j % FF_TILES_PER_SHARD))


def _conv_taps(pad, s, cw_ref, off=0):
    return (pad[CONV_PAD + off:CONV_PAD + off + s, :] * cw_ref[2:3, :]
            + pad[CONV_PAD - 1 + off:CONV_PAD - 1 + off + s, :] * cw_ref[1:2, :]
            + pad[CONV_PAD - 2 + off:CONV_PAD - 2 + off + s, :] * cw_ref[0:1, :])


def _ffn_up(x1_bf, wg_g, wu_g, convw_g, convb_g):
    s = x1_bf.shape[0]

    def body(x_ref, wg_ref, wu_ref, cw_ref, cb_ref, a_ref, up_ref, hh_ref, pad):
        x1 = x_ref[...]
        a = _dot(x1, wg_ref[...])
        up = _dot(x1, wu_ref[...])
        a_ref[...] = a
        up_ref[...] = up
        pad[0:CONV_PAD, :] = jnp.zeros((CONV_PAD, FF_TILE), f32)
        pad[CONV_PAD:CONV_PAD + s, :] = a
        ac = _conv_taps(pad, s, cw_ref) + cb_ref[...]
        hg, _ = _gelu_parts(ac)
        hh_ref[...] = (hg * up).astype(bf16)

    col = pl.BlockSpec((s, FF_TILE), lambda j: (0, j))
    return _call(
        body, name="ffn_up", grid=(D_FF_PAD // FF_TILE,),
        in_specs=[pl.BlockSpec((s, D_MODEL), lambda j: (0, 0)), _ff_weight_spec(D_MODEL), _ff_weight_spec(D_MODEL),
                  _ff_weight_spec(8), _ff_weight_spec(1)],
        out_specs=[col, col, col],
        out_shape=[jax.ShapeDtypeStruct((s, D_FF_PAD), f32), jax.ShapeDtypeStruct((s, D_FF_PAD), f32),
                   jax.ShapeDtypeStruct((s, D_FF_PAD), bf16)],
        scratch=[pltpu.VMEM((s + CONV_PAD, FF_TILE), f32)])(x1_bf, wg_g, wu_g, convw_g, convb_g)


def _ffn_down_ln2_loss(hh_bf, wd, xhat1, ln1_g, ln1_b, ln2_g, ln2_b, target):
    s = hh_bf.shape[0]
    tm = 256

    def body(hh_ref, w_ref, xh1_ref, g1_ref, b1_ref, g2_ref, b2_ref, t_ref, loss_ref, dh_ref, dhb_ref, dg_ref, db_ref):
        i = pl.program_id(0)
        x1 = xh1_ref[...] * g1_ref[...] + b1_ref[...]
        h = ALPHA * x1 + _dot(hh_ref[...], w_ref[...])
        y, xhat, rstd = _layer_norm(h, g2_ref[...], b2_ref[...])
        err = y - t_ref[...]
        part = 0.5 * jnp.sum(jnp.mean(err * err, axis=-1, keepdims=True), axis=0, keepdims=True)
        dy = err * (1.0 / D_MODEL)

        @pl.when(i == 0)
        def _():
            loss_ref[...] = jnp.zeros_like(loss_ref)
            dg_ref[...] = jnp.zeros_like(dg_ref)
            db_ref[...] = jnp.zeros_like(db_ref)

        loss_ref[...] += jnp.broadcast_to(part, loss_ref.shape)
        dg_ref[...] += jnp.sum(dy * xhat, axis=0, keepdims=True)
        db_ref[...] += jnp.sum(dy, axis=0, keepdims=True)
        dh = _layer_norm_bwd(dy, xhat, rstd, g2_ref[...])
        dh_ref[...] = dh
        dhb_ref[...] = dh.astype(bf16)

    row = pl.BlockSpec((tm, D_MODEL), lambda i: (i, 0))
    vec = pl.BlockSpec((1, D_MODEL), lambda i: (0, 0))
    return _call(
        body, name="ffn_down_ln2_loss", grid=(s // tm,),
        in_specs=[pl.BlockSpec((tm, D_FF_PAD), lambda i: (i, 0)), pl.BlockSpec((D_FF_PAD, D_MODEL), lambda i: (0, 0)),
                  row, vec, vec, vec, vec, row],
        out_specs=[pl.BlockSpec((8, LANES), lambda i: (0, 0)), row, row, vec, vec],
        out_shape=[jax.ShapeDtypeStruct((8, LANES), f32), jax.ShapeDtypeStruct((s, D_MODEL), f32),
                   jax.ShapeDtypeStruct((s, D_MODEL), bf16), jax.ShapeDtypeStruct((1, D_MODEL), f32),
                   jax.ShapeDtypeStruct((1, D_MODEL), f32)])(hh_bf, wd, xhat1, ln1_g, ln1_b, ln2_g, ln2_b, target)


def _ffn_act_bwd(dh2_bf, wd_g, a, up, convw_g, convb_g):
    s = dh2_bf.shape[0]

    def body(dh_ref, wd_ref, a_ref, up_ref, cw_ref, cb_ref, da_ref, dup_ref, dc_ref, pad, dpad):
        dhh = _dot_nt(dh_ref[...], wd_ref[...])
        zeros = jnp.zeros((CONV_PAD, FF_TILE), f32)
        pad[0:CONV_PAD, :] = zeros
        pad[CONV_PAD:CONV_PAD + s, :] = a_ref[...]
        ac = _conv_taps(pad, s, cw_ref) + cb_ref[...]
        hg, dgelu = _gelu_parts(ac)
        up = up_ref[...]
        dup_ref[...] = (dhh * hg).astype(bf16)
        dac = dhh * up * dgelu
        rows = [jnp.sum(dac * pad[CONV_PAD - 2 + i:CONV_PAD - 2 + i + s, :], axis=0, keepdims=True) for i in range(3)]
        rows.append(jnp.sum(dac, axis=0, keepdims=True))
        rows.append(jnp.zeros((CONV_ROWS - 4, FF_TILE), f32))
        dc_ref[...] = jnp.concatenate(rows, axis=0)
        dpad[0:s, :] = dac
        dpad[s:s + CONV_PAD, :] = zeros
        da = (dpad[0:s, :] * cw_ref[2:3, :] + dpad[1:1 + s, :] * cw_ref[1:2, :] + dpad[2:2 + s, :] * cw_ref[0:1, :])
        da_ref[...] = da.astype(bf16)

    col = pl.BlockSpec((s, FF_TILE), lambda j: (0, j))
    return _call(
        body, name="ffn_act_bwd", grid=(D_FF_PAD // FF_TILE,),
        in_specs=[pl.BlockSpec((s, D_MODEL), lambda j: (0, 0)),
                  pl.BlockSpec((None, FF_TILE, D_MODEL),
                               lambda j: (j // FF_TILES_PER_SHARD, j % FF_TILES_PER_SHARD, 0)),
                  col, col, _ff_weight_spec(8), _ff_weight_spec(1)],
        out_specs=[col, col, _ff_weight_spec(CONV_ROWS)],
        out_shape=[jax.ShapeDtypeStruct((s, D_FF_PAD), bf16), jax.ShapeDtypeStruct((s, D_FF_PAD), bf16),
                   jax.ShapeDtypeStruct((N_CHIPS, CONV_ROWS, FF_PAD), f32)],
        scratch=[pltpu.VMEM((s + CONV_PAD, FF_TILE), f32), pltpu.VMEM((s + CONV_PAD, FF_TILE), f32)],
    )(dh2_bf, wd_g, a, up, convw_g, convb_g)


def _matmul_tn(a, b, n_shards, name, tm=512, tn=256):
    k, m = a.shape
    n = b.shape[1]
    tm, tn = min(tm, m), min(tn, n // n_shards)
    per = n // n_shards // tn

    def body(a_ref, b_ref, o_ref):
        o_ref[...] = _dot_tn(a_ref[...], b_ref[...]).astype(bf16)

    return _call(
        body, name=name, grid=(m // tm, n // tn),
        in_specs=[pl.BlockSpec((k, tm), lambda i, j: (0, i)), pl.BlockSpec((k, tn), lambda i, j: (0, j))],
        out_specs=pl.BlockSpec((None, tm, tn), lambda i, j: (j // per, i, j % per)),
        out_shape=jax.ShapeDtypeStruct((n_shards, m, n // n_shards), bf16))(a, b)


def _ffn_in_bwd_ln1(da_bf, dup_bf, wg_g, wu_g, dh2, xhat1, rstd1, ln1_g):
    s = da_bf.shape[0]
    tm = 256

    def body(da_ref, dup_ref, wg_ref, wu_ref, dh2_ref, xh_ref, rstd_ref, g_ref, dh_ref, dhb_ref, dg_ref, db_ref):
        i = pl.program_id(0)
        dx1 = ALPHA * dh2_ref[...]
        for sh in range(N_CHIPS):
            sl = slice(sh * FF_PAD, (sh + 1) * FF_PAD)
            dx1 = dx1 + _dot_nt(da_ref[:, sl], wg_ref[sh]) + _dot_nt(dup_ref[:, sl], wu_ref[sh])
        xhat = xh_ref[...]

        @pl.when(i == 0)
        def _():
            dg_ref[...] = jnp.zeros_like(dg_ref)
            db_ref[...] = jnp.zeros_like(db_ref)

        dg_ref[...] += jnp.sum(dx1 * xhat, axis=0, keepdims=True)
        db_ref[...] += jnp.sum(dx1, axis=0, keepdims=True)
        dh = _layer_norm_bwd(dx1, xhat, rstd_ref[:, 0:1], g_ref[...])
        dh_ref[...] = dh
        dhb_ref[...] = dh.astype(bf16)

    row = pl.BlockSpec((tm, D_MODEL), lambda i: (i, 0))
    wide = pl.BlockSpec((tm, D_FF_PAD), lambda i: (i, 0))
    wfull = pl.BlockSpec((N_CHIPS, D_MODEL, FF_PAD), lambda i: (0, 0, 0))
    vec = pl.BlockSpec((1, D_MODEL), lambda i: (0, 0))
    return _call(
        body, name="ffn_in_bwd_ln1", grid=(s // tm,),
        in_specs=[wide, wide, wfull, wfull, row, row, pl.BlockSpec((tm, LANES), lambda i: (i, 0)), vec],
        out_specs=[row, row, vec, vec],
        out_shape=[jax.ShapeDtypeStruct((s, D_MODEL), f32), jax.ShapeDtypeStruct((s, D_MODEL), bf16),
                   jax.ShapeDtypeStruct((1, D_MODEL), f32), jax.ShapeDtypeStruct((1, D_MODEL), f32)],
    )(da_bf, dup_bf, wg_g, wu_g, dh2, xhat1, rstd1, ln1_g)


def _merge_bwd(dh1_bf, wout, o_bf, pm_bf, z, wba, wbp):
    s = dh1_bf.shape[0]
    tm = 256

    def body(dh_ref, wout_ref, o_ref, pm_ref, ga_ref, gp_ref, wba_ref, wbp_ref,
             dzg_ref, dya_ref, dyp_ref, do_ref, dpm_ref, dbg_ref):
        i = pl.program_id(0)
        dm = _dot_nt(dh_ref[...], wout_ref[...])
        ya = _dot(o_ref[...], wba_ref[...])
        yp = _dot(pm_ref[...], wbp_ref[...])
        ga, gp = ga_ref[...], gp_ref[...]
        dza = dm * ya * ga * (1.0 - ga)
        dzp = dm * yp * gp * (1.0 - gp)

        @pl.when(i == 0)
        def _():
            dbg_ref[...] = jnp.zeros_like(dbg_ref)

        dbg_ref[:, 0:D_MODEL] += jnp.sum(dza, axis=0, keepdims=True)
        dbg_ref[:, D_MODEL:2 * D_MODEL] += jnp.sum(dzp, axis=0, keepdims=True)
        dzg_ref[:, 0:D_MODEL] = dza.astype(bf16)
        dzg_ref[:, D_MODEL:2 * D_MODEL] = dzp.astype(bf16)
        dya = (dm * ga).astype(bf16)
        dyp = (dm * gp).astype(bf16)
        dya_ref[...] = dya
        dyp_ref[...] = dyp
        do_ref[...] = _dot_nt(dya, wba_ref[...]).astype(bf16)
        dpm_ref[...] = _dot_nt(dyp, wbp_ref[...])

    row = pl.BlockSpec((tm, D_MODEL), lambda i: (i, 0))
    half = pl.BlockSpec((tm, D_ATTN), lambda i: (i, 0))
    full = lambda r, c: pl.BlockSpec((r, c), lambda i: (0, 0))
    return _call(
        body, name="merge_bwd", grid=(s // tm,),
        in_specs=[row, full(D_MODEL, D_MODEL), half, half,
                  pl.BlockSpec((tm, D_MODEL), lambda i: (i, 2)), pl.BlockSpec((tm, D_MODEL), lambda i: (i, 3)),
                  full(D_ATTN, D_MODEL), full(D_POOL, D_MODEL)],
        out_specs=[pl.BlockSpec((tm, 2 * D_MODEL), lambda i: (i, 0)), row, row, half, half,
                   pl.BlockSpec((1, 2 * D_MODEL), lambda i: (0, 0))],
        out_shape=[jax.ShapeDtypeStruct((s, 2 * D_MODEL), bf16), jax.ShapeDtypeStruct((s, D_MODEL), bf16),
                   jax.ShapeDtypeStruct((s, D_MODEL), bf16), jax.ShapeDtypeStruct((s, D_ATTN), bf16),
                   jax.ShapeDtypeStruct((s, D_POOL), f32), jax.ShapeDtypeStruct((1, 2 * D_MODEL), f32)],
    )(dh1_bf, wout, o_bf, pm_bf, z, z, wba, wbp)


def _pool_bwd(dpm, mixed, pooled_bf, w_pool, pool_scale):
    s = dpm.shape[0]

    def body(dpm_ref, mixed_ref, pooled_ref, w_ref, sc_ref, du_ref, dw_ref, dsc_ref, pad):
        g = pl.program_id(0)
        dpm_v = dpm_ref[...]
        dsc_ref[...] = jnp.sum(dpm_v * mixed_ref[...], axis=0, keepdims=True)
        dmixed = (dpm_v * sc_ref[...]).astype(bf16)
        dw_ref[...] = _dot_tn(pooled_ref[...], dmixed)
        dpooled = _dot_nt(dmixed, w_ref[...].astype(bf16))
        t = lax.broadcasted_iota(jnp.int32, (s, POOL_GROUP), 0)
        count = jnp.minimum(t + 1, jnp.left_shift(2, g)).astype(f32)
        r = dpooled / count
        pad[0:s, :] = r
        pad[s:s + MAX_WINDOW, :] = jnp.zeros((MAX_WINDOW, POOL_GROUP), f32)
        acc = r
        snaps = []
        for d in range(1, MAX_WINDOW):
            acc = acc + pad[d:d + s, :]
            if d + 1 in (2, 4, 8, 16):
                snaps.append(acc)
        du_ref[...] = (_window_select(snaps, g) - dpooled).astype(bf16)

    blk = pl.BlockSpec((s, POOL_GROUP), lambda g: (0, g))
    return _call(
        body, name="pool_bwd", grid=(4,),
        in_specs=[blk, blk, blk, pl.BlockSpec((None, POOL_GROUP, POOL_GROUP), lambda g: (g, 0, 0)),
                  pl.BlockSpec((1, POOL_GROUP), lambda g: (0, g))],
        out_specs=[blk, pl.BlockSpec((None, POOL_GROUP, POOL_GROUP), lambda g: (g, 0, 0)),
                   pl.BlockSpec((1, POOL_GROUP), lambda g: (0, g))],
        out_shape=[jax.ShapeDtypeStruct((s, D_POOL), bf16), jax.ShapeDtypeStruct((4, POOL_GROUP, POOL_GROUP), f32),
                   jax.ShapeDtypeStruct((1, D_POOL), f32)],
        scratch=[pltpu.VMEM((s + MAX_WINDOW, POOL_GROUP), f32)])(dpm, mixed, pooled_bf, w_pool, pool_scale)


def _attn_bwd(z, o_bf, lse, do_bf, cos_t, sin_t, nb):
    s = z.shape[0]

    def body(q_ref, k_ref, v_ref, o_ref, lse_ref, do_ref, cq_ref, sq_ref, cf_ref, sf_ref,
             dq_ref, dk_ref, dv_ref, dq_s, dk_s, dv_s):
        i = pl.program_id(1)
        lane = lax.broadcasted_iota(jnp.int32, (MOBA_BLOCK, LANES), 1)
        head0 = lane < HEAD_DIM
        heads = (head0, jnp.logical_not(head0))

        @pl.when(i == 0)
        def _():
            dk_s[...] = jnp.zeros_like(dk_s)
            dv_s[...] = jnp.zeros_like(dv_s)

        q = q_ref[...]
        sels = _moba_select(q, k_ref[...], i, nb)
        qs = q * SCALE
        qb = [jnp.where(hm, qs, 0.0).astype(bf16) for hm in heads]
        qraw = [jnp.where(hm, q, 0.0).astype(bf16) for hm in heads]
        do = do_ref[...].astype(f32)
        prod = do * o_ref[...].astype(f32)
        delta = [jnp.sum(jnp.where(hm, prod, 0.0), axis=1, keepdims=True) for hm in heads]
        dob = [jnp.where(hm, do, 0.0).astype(bf16) for hm in heads]
        lse_l = lse_ref[...]
        lse_h = [lse_l[:, 0:1], lse_l[:, HEAD_DIM:HEAD_DIM + 1]]
        row = lax.broadcasted_iota(jnp.int32, (MOBA_BLOCK, MOBA_BLOCK), 0)
        col = lax.broadcasted_iota(jnp.int32, (MOBA_BLOCK, MOBA_BLOCK), 1)
        dq_s[...] = jnp.zeros_like(dq_s)

        def block(rows, allow):
            kj = k_ref[rows, :].astype(bf16)
            vj = v_ref[rows, :].astype(bf16)
            dq, dk, dv = None, None, None
            for h in range(2):
                p = jnp.where(allow[h], jnp.exp(_dot_nt(qb[h], kj) - lse_h[h]), 0.0)
                dp = _dot_nt(dob[h], vj)
                ds = (p * (dp - delta[h]) * SCALE).astype(bf16)
                tv = _dot_tn(p.astype(bf16), dob[h])
                tq = _dot(ds, jnp.where(heads[h], kj, jnp.zeros_like(kj)))
                tk = _dot_tn(ds, qraw[h])
                dq, dk, dv = (tq, tk, tv) if dq is None else (dq + tq, dk + tk, dv + tv)
            dq_s[...] += dq
            dk_s[rows, :] += dk
            dv_s[rows, :] += dv

        block(pl.ds(pl.multiple_of(i * MOBA_BLOCK, MOBA_BLOCK), MOBA_BLOCK), [col <= row, col <= row])
        for j in range(nb - 1):
            @pl.when(j < i)
            def _(j=j):
                block(slice(j * MOBA_BLOCK, (j + 1) * MOBA_BLOCK), [sels[h][:, j:j + 1] > 0.0 for h in range(2)])

        dq_ref[...] = _rope_bwd(dq_s[...], cq_ref[...], sq_ref[...]).astype(bf16)

        @pl.when(i == nb - 1)
        def _():
            dk_ref[...] = _rope_bwd(dk_s[...], cf_ref[...], sf_ref[...]).astype(bf16)
            dv_ref[...] = dv_s[...].astype(bf16)

    qblk = pl.BlockSpec((MOBA_BLOCK, LANES), lambda hp, i: (i, hp))
    tq = pl.BlockSpec((MOBA_BLOCK, LANES), lambda hp, i: (i, 0))
    tf = pl.BlockSpec((s, LANES), lambda hp, i: (0, 0))
    colblk = pl.BlockSpec((s, LANES), lambda hp, i: (0, hp))
    return _call(
        body, name="attn_bwd", grid=(D_ATTN // LANES, nb),
        in_specs=[qblk, pl.BlockSpec((s, LANES), lambda hp, i: (0, 4 + hp)),
                  pl.BlockSpec((s, LANES), lambda hp, i: (0, 8 + hp)), qblk,
                  pl.BlockSpec((None, MOBA_BLOCK, LANES), lambda hp, i: (hp, i, 0)), qblk, tq, tq, tf, tf],
        out_specs=[qblk, colblk, colblk],
        out_shape=[jax.ShapeDtypeStruct((s, D_ATTN), bf16)] * 3,
        scratch=[pltpu.VMEM((MOBA_BLOCK, LANES), f32), pltpu.VMEM((s, LANES), f32), pltpu.VMEM((s, LANES), f32)],
    )(z, z, z, o_bf, lse, do_bf, cos_t, sin_t, cos_t, sin_t)


def _inproj_dx(dz_bf, win_g, dh1):
    s = dz_bf.shape[0]
    tm = 256

    def body(dz_ref, w_ref, dh_ref, gx_ref):
        acc = ALPHA * dh_ref[...]
        for sh in range(N_CHIPS):
            acc = acc + _dot_nt(dz_ref[:, sh * D_MODEL:(sh + 1) * D_MODEL], w_ref[sh])
        gx_ref[...] = acc

    row = pl.BlockSpec((tm, D_MODEL), lambda i: (i, 0))
    return _call(
        body, name="inproj_dx", grid=(s // tm,),
        in_specs=[pl.BlockSpec((tm, 4 * D_MODEL), lambda i: (i, 0)),
                  pl.BlockSpec((N_CHIPS, D_MODEL, D_MODEL), lambda i: (0, 0, 0)), row],
        out_specs=row, out_shape=jax.ShapeDtypeStruct((s, D_MODEL), f32))(dz_bf, win_g, dh1)


ANY = pl.BlockSpec(memory_space=pl.ANY)


def _chip_index():
    return 2 * lax.axis_index("x") + lax.axis_index("y")


def _peer(k):
    x, y, c = lax.axis_index("x"), lax.axis_index("y"), lax.axis_index("c")
    return (x ^ (k >> 1), y ^ (k & 1), c)


def _sibling():
    return (lax.axis_index("x"), lax.axis_index("y"), 1 - lax.axis_index("c"))


def _all_gather_shards(shards):
    n = len(shards)
    split = [a.reshape(2, a.shape[0] // 2, a.shape[1]) for a in shards]

    def body(*refs):
        src, dst = refs[:n], refs[n:2 * n]
        send_sems, recv_sems, fsend_sems, frecv_sems, local_sems, own_sems = refs[2 * n:]
        p, c = _chip_index(), lax.axis_index("c")
        local = [pltpu.make_async_remote_copy(src_ref=src[a], dst_ref=dst[a].at[p], send_sem=local_sems.at[a],
                                              recv_sem=own_sems.at[a], device_id=_sibling(), device_id_type=MESH)
                 for a in range(n)]
        for cp in local:
            cp.start()
        sends = []
        for a in range(n):
            for k in range(1, N_CHIPS):
                cp = pltpu.make_async_remote_copy(
                    src_ref=src[a].at[c], dst_ref=dst[a].at[p, c], send_sem=send_sems.at[k - 1, a],
                    recv_sem=recv_sems.at[k - 1, a], device_id=_peer(k), device_id_type=MESH)
                cp.start()
                sends.append(cp)
        for a in range(n):
            for k in range(1, N_CHIPS):
                got = dst[a].at[p ^ k, c]
                pltpu.make_async_remote_copy(
                    src_ref=src[a].at[c], dst_ref=got, send_sem=send_sems.at[k - 1, a],
                    recv_sem=recv_sems.at[k - 1, a], device_id=_peer(k), device_id_type=MESH).wait_recv()
                cp = pltpu.make_async_remote_copy(
                    src_ref=got, dst_ref=got, send_sem=fsend_sems.at[k - 1, a], recv_sem=frecv_sems.at[k - 1, a],
                    device_id=_sibling(), device_id_type=MESH)
                cp.start()
                sends.append(cp)
        for a in range(n):
            for k in range(1, N_CHIPS):
                theirs = dst[a].at[p ^ k, 1 - c]
                pltpu.make_async_remote_copy(
                    src_ref=theirs, dst_ref=theirs, send_sem=fsend_sems.at[k - 1, a], recv_sem=frecv_sems.at[k - 1, a],
                    device_id=_sibling(), device_id_type=MESH).wait_recv()
        for cp in sends:
            cp.wait_send()
        for cp in local:
            cp.wait()

    sem, one = pltpu.SemaphoreType.DMA((N_CHIPS - 1, n)), pltpu.SemaphoreType.DMA((n,))
    out = pl.pallas_call(
        body, name="all_gather_weights",
        in_specs=[ANY] * n, out_specs=[ANY] * n,
        out_shape=[jax.ShapeDtypeStruct((N_CHIPS,) + a.shape, a.dtype) for a in split],
        scratch_shapes=[sem, sem, sem, sem, one, one])(*split)
    return [o.reshape((N_CHIPS,) + a.shape) for o, a in zip(out, shards)]


def _split_halves(g):
    return g.reshape(g.shape[0], 2, g.shape[1] // 2, g.shape[2])


def _ew_tile(rows):
    return 256 if rows % 256 == 0 else rows


def _prefetch_call(body, *, name, grid, in_specs, out_specs, out_shape, **kw):
    return pl.pallas_call(
        body, name=name, out_shape=out_shape,
        grid_spec=pltpu.PrefetchScalarGridSpec(num_scalar_prefetch=1, grid=grid, in_specs=in_specs, out_specs=out_specs),
        compiler_params=pltpu.CompilerParams(dimension_semantics=("arbitrary",) * len(grid),
                                             vmem_limit_bytes=VMEM_LIMIT), **kw)


def _presum_swap(split):
    n = len(split)

    def body(*refs):
        src, got = refs[:n], refs[n:2 * n]
        send_sems, recv_sems = refs[2 * n:]
        c = lax.axis_index("c")
        remote = [pltpu.make_async_remote_copy(src_ref=src[a].at[:, 1 - c], dst_ref=got[a],
                                               send_sem=send_sems.at[a], recv_sem=recv_sems.at[a],
                                               device_id=_sibling(), device_id_type=MESH) for a in range(n)]
        for cp in remote:
            cp.start()
        for cp in remote:
            cp.wait_recv()
        for cp in remote:
            cp.wait_send()

    sem = pltpu.SemaphoreType.DMA((n,))
    return pl.pallas_call(
        body, name="presum_swap", in_specs=[ANY] * n, out_specs=[ANY] * n,
        out_shape=[jax.ShapeDtypeStruct((g.shape[0],) + g.shape[2:], g.dtype) for g in split],
        scratch_shapes=[sem, sem])(*split)


def _add_pair(mine, theirs, place, name):
    lead, _, rows, cols = mine.shape
    tm = _ew_tile(rows)

    def body(place_ref, a_ref, b_ref, o_ref):
        o_ref[...] = (a_ref[...].astype(f32) + b_ref[...].astype(f32)).astype(o_ref.dtype)

    blk = pl.BlockSpec((lead, tm, cols), lambda i, pc: (0, i, 0))
    return _prefetch_call(
        body, name=name, grid=(rows // tm,),
        in_specs=[pl.BlockSpec((lead, None, tm, cols), lambda i, pc: (0, pc[1], i, 0)), blk], out_specs=blk,
        out_shape=jax.ShapeDtypeStruct(theirs.shape, theirs.dtype))(place, mine, theirs)


def _exchange_slices(sums, whole):
    n = len(sums)

    def body(*refs):
        src, dst = refs[:n], refs[n:2 * n]
        send_sems, recv_sems = refs[2 * n:]
        p = _chip_index()
        remote = []
        for k in range(1, N_CHIPS):
            for a in range(n):
                cp = pltpu.make_async_remote_copy(
                    src_ref=src[a].at[0] if whole[a] else src[a].at[p ^ k], dst_ref=dst[a].at[k - 1],
                    send_sem=send_sems.at[k - 1, a], recv_sem=recv_sems.at[k - 1, a], device_id=_peer(k),
                    device_id_type=MESH)
                cp.start()
                remote.append(cp)
        for cp in remote:
            cp.wait_recv()
        for cp in remote:
            cp.wait_send()

    sem = pltpu.SemaphoreType.DMA((N_CHIPS - 1, n))
    return pl.pallas_call(
        body, name="exchange_grad_slices", in_specs=[ANY] * n, out_specs=[ANY] * n,
        out_shape=[jax.ShapeDtypeStruct((N_CHIPS - 1,) + g.shape[1:], g.dtype) for g in sums],
        scratch_shapes=[sem, sem])(*sums)


def _sum_slots(own, recv, place, whole, name):
    _, rows, cols = own.shape
    tm = _ew_tile(rows)

    def body(place_ref, own_ref, r_ref, o_ref):
        r0, r1, r2 = [r_ref[k].astype(f32) for k in range(N_CHIPS - 1)]
        o_ref[...] = (own_ref[...].astype(f32) + r0) + (r1 + r2)

    return _prefetch_call(
        body, name=name, grid=(rows // tm,),
        in_specs=[pl.BlockSpec((None, tm, cols), lambda i, pc: (0 if whole else pc[0], i, 0)),
                  pl.BlockSpec((N_CHIPS - 1, tm, cols), lambda i, pc: (0, i, 0))],
        out_specs=pl.BlockSpec((None, tm, cols), lambda i, pc: (pc[1], i, 0)),
        out_shape=jax.ShapeDtypeStruct((2, rows, cols), f32))(place, own, recv)


def _sibling_fill(bufs):
    n = len(bufs)

    def body(*refs):
        src, dst = refs[:n], refs[n:2 * n]
        send_sems, recv_sems = refs[2 * n:]
        c = lax.axis_index("c")
        remote = [pltpu.make_async_remote_copy(src_ref=src[a].at[c], dst_ref=dst[a].at[c], send_sem=send_sems.at[a],
                                               recv_sem=recv_sems.at[a], device_id=_sibling(), device_id_type=MESH)
                  for a in range(n)]
        for cp in remote:
            cp.start()
        for a in range(n):
            pltpu.make_async_remote_copy(src_ref=src[a].at[c], dst_ref=dst[a].at[1 - c], send_sem=send_sems.at[a],
                                         recv_sem=recv_sems.at[a], device_id=_sibling(), device_id_type=MESH).wait_recv()
        for cp in remote:
            cp.wait_send()

    sem = pltpu.SemaphoreType.DMA((n,))
    out = pl.pallas_call(
        body, name="sibling_fill", in_specs=[ANY] * n, out_specs=[ANY] * n,
        out_shape=[jax.ShapeDtypeStruct(b.shape, b.dtype) for b in bufs],
        input_output_aliases={a: a for a in range(n)}, scratch_shapes=[sem, sem])(*bufs)
    return [o.reshape(2 * b.shape[1], b.shape[2]) for o, b in zip(out, bufs)]


def _adamw(w, m, v, grad, name, tm):
    rows, cols = w.shape
    gcols = grad.shape[1]
    if tm == rows:
        gspec = pl.BlockSpec(grad.shape, lambda i: (0, 0))
    else:
        gspec = pl.BlockSpec((tm, gcols), lambda i: (i, 0))

    def body(w_ref, m_ref, v_ref, gin_ref, g_ref, d_ref, nm_ref, nv_ref):
        g = gin_ref[0:tm, 0:cols]
        w_, m_, v_ = w_ref[...], m_ref[...], v_ref[...]
        m_ = ADAM_B1 * m_ + (1.0 - ADAM_B1) * g
        v_ = ADAM_B2 * v_ + (1.0 - ADAM_B2) * jnp.square(g)
        m_hat = m_ / (1.0 - ADAM_B1 ** ADAM_STEP)
        v_hat = v_ / (1.0 - ADAM_B2 ** ADAM_STEP)
        g_ref[...] = g
        d_ref[...] = -ADAM_LR * (m_hat / (jnp.sqrt(v_hat) + ADAM_EPS) + ADAM_WD * w_)
        nm_ref[...] = m_
        nv_ref[...] = v_

    blk = pl.BlockSpec((tm, cols), lambda i: (i, 0))
    return _call(
        body, name=name, grid=(rows // tm,),
        in_specs=[blk, blk, blk, gspec], out_specs=[blk] * 4,
        out_shape=[jax.ShapeDtypeStruct((rows, cols), f32)] * 4)(w, m, v, grad)


def _rope_tables(s):
    half = HEAD_DIM // 2
    inv_freq = 1.0 / (10000.0 ** (jnp.arange(half, dtype=f32) / half))
    ang = jnp.arange(s, dtype=f32)[:, None] * inv_freq[None, :]
    cos, sin = jnp.cos(ang), jnp.sin(ang)
    return jnp.tile(cos, (1, LANES // half)), jnp.tile(jnp.concatenate([-sin, sin], axis=1), (1, LANES // HEAD_DIM))


def _local_step(x, target, gw, b_gate, w_pool, pool_scale, ln1_g, ln1_b, convb_g, ln2_g, ln2_b):
    s = x.shape[0]
    nb = s // MOBA_BLOCK
    win_g, wba_g, wbp_g, wout_g, wg_g, wu_g, wd_g, convw_g = gw
    wba = wba_g.transpose(1, 0, 2).reshape(D_ATTN, D_MODEL)
    wbp = wbp_g.transpose(1, 0, 2).reshape(D_POOL, D_MODEL)
    wout = wout_g.reshape(D_MODEL, D_MODEL)
    wd = wd_g.reshape(D_FF_PAD, D_MODEL)
    cos_t, sin_t = _rope_tables(s)
    x_bf = x.astype(bf16)

    z = _inproj(x_bf, win_g, cos_t, sin_t, b_gate)
    o_bf, lse = _attn_fwd(z, nb)
    pooled_bf, mixed, pm_bf = _pool_fwd(z, w_pool, pool_scale)
    m_bf = _branch_merge(o_bf, pm_bf, z, wba, wbp)
    x1_bf, xhat1, rstd1 = _out_ln1(m_bf, wout, x, ln1_g, ln1_b)
    a, up, hh_bf = _ffn_up(x1_bf, wg_g, wu_g, convw_g, convb_g)
    loss, dh2, dh2_bf, d_ln2_g, d_ln2_b = _ffn_down_ln2_loss(hh_bf, wd, xhat1, ln1_g, ln1_b, ln2_g, ln2_b, target)

    da_bf, dup_bf, dconv = _ffn_act_bwd(dh2_bf, wd_g, a, up, convw_g, convb_g)
    d_wg = _matmul_tn(x1_bf, da_bf, N_CHIPS, "dw_ffn_gate")
    d_wu = _matmul_tn(x1_bf, dup_bf, N_CHIPS, "dw_ffn_up")
    d_wd = _matmul_tn(hh_bf, dh2_bf, 1, "dw_ffn_down").reshape(N_CHIPS, FF_PAD, D_MODEL)
    dh1, dh1_bf, d_ln1_g, d_ln1_b = _ffn_in_bwd_ln1(da_bf, dup_bf, wg_g, wu_g, dh2, xhat1, rstd1, ln1_g)
    d_wout = _matmul_tn(m_bf, dh1_bf, 1, "dw_out").reshape(N_CHIPS, D_MODEL // N_CHIPS, D_MODEL)
    dzg_bf, dya_bf, dyp_bf, do_bf, dpm, d_bgate = _merge_bwd(dh1_bf, wout, o_bf, pm_bf, z, wba, wbp)
    d_wba = _matmul_tn(o_bf, dya_bf, N_CHIPS, "dw_branch_attn")
    d_wbp = _matmul_tn(pm_bf, dyp_bf, N_CHIPS, "dw_branch_pool")
    du_bf, d_wpool, d_pscale = _pool_bwd(dpm, mixed, pooled_bf, w_pool, pool_scale)
    dq_bf, dk_bf, dv_bf = _attn_bwd(z, o_bf, lse, do_bf, cos_t, sin_t, nb)
    dz_bf = jnp.concatenate([dq_bf, dk_bf, dv_bf, du_bf, dzg_bf], axis=1)
    d_win = _matmul_tn(x_bf, dz_bf, N_CHIPS, "dw_in")
    grad_x = _inproj_dx(dz_bf, win_g, dh1)

    sharded = [d_win, d_wba, d_wbp, d_wout, d_wg, d_wu, d_wd, dconv]
    small = [d_bgate, d_wpool, d_pscale, d_ln1_g, d_ln1_b, d_ln2_g, d_ln2_b, dconv[:, 3, :]]
    return loss[0, 0], grad_x, sharded, small


SMALL_ROWS = 592


def _pack_small(parts):
    flat = jnp.concatenate([p.reshape(-1) for p in parts])
    return jnp.pad(flat, (0, SMALL_ROWS * LANES - flat.shape[0])).reshape(SMALL_ROWS, LANES)


def _unpack_small(packed, shapes):
    flat = packed.reshape(-1)
    out, off = [], 0
    for shp in shapes:
        n = 1
        for d in shp:
            n *= d
        out.append(flat[off:off + n].reshape(shp))
        off += n
    return out


def _pad_conv_b(cb):
    return jnp.pad(cb.reshape(N_CHIPS, FF_SHARD), ((0, 0), (0, FF_PAD - FF_SHARD)))


def kernel(x, w_in, b_gate, w_branch_attn, w_pool, pool_scale, w_branch_pool, w_out, ln1_g, ln1_b, w_ffn_gate, w_ffn_up, conv_w, conv_b, w_ffn_down, ln2_g, ln2_b, loss_target, m_w_in, m_b_gate, m_w_branch_attn, m_w_pool, m_pool_scale, m_w_branch_pool, m_w_out, m_ln1_g, m_ln1_b, m_w_ffn_gate, m_w_ffn_up, m_conv_w, m_conv_b, m_w_ffn_down, m_ln2_g, m_ln2_b, v_w_in, v_b_gate, v_w_branch_attn, v_w_pool, v_pool_scale, v_w_branch_pool, v_w_out, v_ln1_g, v_ln1_b, v_w_ffn_gate, v_w_ffn_up, v_conv_w, v_conv_b, v_w_ffn_down, v_ln2_g, v_ln2_b):
    pad_cols = lambda w: jnp.pad(w, ((0, 0), (0, FF_PAD - FF_SHARD)))
    shards = [w_in[0].astype(bf16), w_branch_attn[0].astype(bf16), w_branch_pool[0].astype(bf16),
              w_out[0].astype(bf16), pad_cols(w_ffn_gate[0].astype(bf16)), pad_cols(w_ffn_up[0].astype(bf16)),
              jnp.pad(w_ffn_down[0].astype(bf16), ((0, FF_PAD - FF_SHARD), (0, 0))),
              jnp.pad(conv_w[0], ((0, CONV_ROWS - 3), (0, FF_PAD - FF_SHARD)))]
    gathered = _all_gather_shards(shards)

    convb_g = _pad_conv_b(conv_b).reshape(N_CHIPS, 1, FF_PAD)
    loss, grad_x, sharded, small = _local_step(
        x[0], loss_target[0], gathered, b_gate, w_pool[0], pool_scale, ln1_g, ln1_b, convb_g, ln2_g, ln2_b)
    loss = lax.psum(loss, ("x", "y", "c"))

    names = ["w_in", "w_branch_attn", "w_branch_pool", "w_out", "w_ffn_gate", "w_ffn_up", "w_ffn_down", "conv_w", "small"]
    whole = [False] * len(sharded) + [True]
    place = jnp.stack([2 * lax.axis_index("x") + lax.axis_index("y"), lax.axis_index("c")]).astype(jnp.int32)
    split = [_split_halves(g) for g in sharded + [_pack_small(small)[None]]]
    got = _presum_swap(split)
    chip_sums = [_add_pair(a, b, place, "presum_" + nm) for a, b, nm in zip(split, got, names)]
    slots = _exchange_slices(chip_sums, whole)
    grads = _sibling_fill([_sum_slots(s, r, place, w, "sum_" + nm)
                           for s, r, w, nm in zip(chip_sums, slots, whole, names)])

    weights = [w_in[0], w_branch_attn[0], w_branch_pool[0], w_out[0], w_ffn_gate[0], w_ffn_up[0], w_ffn_down[0], conv_w[0]]
    m_in = [m_w_in[0], m_w_branch_attn[0], m_w_branch_pool[0], m_w_out[0], m_w_ffn_gate[0], m_w_ffn_up[0], m_w_ffn_down[0], m_conv_w[0]]
    v_in = [v_w_in[0], v_w_branch_attn[0], v_w_branch_pool[0], v_w_out[0], v_w_ffn_gate[0], v_w_ffn_up[0], v_w_ffn_down[0], v_conv_w[0]]
    tiles = [256, 256, 256, 128, 256, 256, 88, 3]
    res = {}
    for i, nm in enumerate(names[:-1]):
        res[nm] = [r[None] for r in _adamw(weights[i], m_in[i], v_in[i], grads[i], "adamw_" + nm, tiles[i])]

    small_names = ["b_gate", "w_pool", "pool_scale", "ln1_g", "ln1_b", "ln2_g", "ln2_b", "conv_b"]
    small_w = [b_gate, w_pool, pool_scale, ln1_g, ln1_b, ln2_g, ln2_b, _pad_conv_b(conv_b)]
    small_m = [m_b_gate, m_w_pool, m_pool_scale, m_ln1_g, m_ln1_b, m_ln2_g, m_ln2_b, _pad_conv_b(m_conv_b)]
    small_v = [v_b_gate, v_w_pool, v_pool_scale, v_ln1_g, v_ln1_b, v_ln2_g, v_ln2_b, _pad_conv_b(v_conv_b)]
    packed = _adamw(_pack_small(small_w), _pack_small(small_m), _pack_small(small_v), grads[-1],
                    "adamw_small", SMALL_ROWS)
    shapes = [w.shape for w in small_w]
    for kind in range(4):
        for nm, val in zip(small_names, _unpack_small(packed[kind], shapes)):
            if nm == "conv_b":
                val = val[:, :FF_SHARD].reshape(1, 4 * FF_SHARD)
            res.setdefault(nm, [None] * 4)[kind] = val

    order = ["w_in", "b_gate", "w_branch_attn", "w_pool", "pool_scale", "w_branch_pool", "w_out", "ln1_g", "ln1_b",
             "w_ffn_gate", "w_ffn_up", "conv_w", "conv_b", "w_ffn_down", "ln2_g", "ln2_b"]
    outs = [loss, grad_x[None]]
    for kind in range(4):
        outs += [res[nm][kind] for nm in order]
    return tuple(outs)
```

```python
import functools

import jax
import jax.numpy as jnp
from jax import lax
from jax.experimental import pallas as pl
from jax.experimental.pallas import tpu as pltpu

D_MODEL = 1024
HEAD_DIM = 64
D_ATTN = 512
D_POOL = 512
MOBA_BLOCK = 256
MOBA_TOPK = 3
POOL_GROUP = 128
MAX_WINDOW = 16
FF_SHARD = 704
FF_PAD = 768
D_FF_PAD = 4 * FF_PAD
N_CHIPS = 4
LANES = 128
ALPHA = (2.0 * 1) ** 0.25
LN_EPS = 1e-5
NEG = -1e30
SCALE = HEAD_DIM ** -0.5
ADAM_LR, ADAM_B1, ADAM_B2, ADAM_EPS, ADAM_WD, ADAM_STEP = 0.001, 0.9, 0.999, 1e-08, 0.01, 10
VMEM_LIMIT = 56 * 1024 * 1024
MESH = pl.DeviceIdType.MESH

bf16 = jnp.bfloat16
f32 = jnp.float32


def _dot(a, b):
    return jnp.dot(a, b, preferred_element_type=f32)


def _dot_nt(a, b):
    return lax.dot_general(a, b, (((1,), (1,)), ((), ())), preferred_element_type=f32)


def _dot_tn(a, b):
    return lax.dot_general(a, b, (((0,), (0,)), ((), ())), preferred_element_type=f32)


def _call(body, *, name, grid, in_specs, out_specs, out_shape, scratch=()):
    return pl.pallas_call(
        body, name=name, grid=grid, in_specs=in_specs, out_specs=out_specs, out_shape=out_shape,
        scratch_shapes=list(scratch),
        compiler_params=pltpu.CompilerParams(dimension_semantics=("arbitrary",) * len(grid),
                                             vmem_limit_bytes=VMEM_LIMIT))


def _rot_half(t):
    lane = lax.broadcasted_iota(jnp.int32, t.shape, 1)
    first = (lane % HEAD_DIM) < (HEAD_DIM // 2)
    return jnp.where(first, pltpu.roll(t, LANES - HEAD_DIM // 2, 1), pltpu.roll(t, HEAD_DIM // 2, 1))


def _rope(t, cos, sin_signed):
    return t * cos + _rot_half(t) * sin_signed


def _rope_bwd(d, cos, sin_signed):
    return d * cos + _rot_half(d * sin_signed)


def _gelu_parts(a):
    cdf = 0.5 * (1.0 + lax.erf(a * (2.0 ** -0.5)))
    pdf = jnp.exp(-0.5 * a * a) * ((2.0 * jnp.pi) ** -0.5)
    return a * cdf, cdf + a * pdf


def _layer_norm(h, g, b):
    mu = jnp.mean(h, axis=-1, keepdims=True)
    xc = h - mu
    var = jnp.mean(xc * xc, axis=-1, keepdims=True)
    rstd = lax.rsqrt(var + LN_EPS)
    xhat = xc * rstd
    return xhat * g + b, xhat, rstd


def _layer_norm_bwd(dy, xhat, rstd, g):
    dxh = dy * g
    m1 = jnp.mean(dxh, axis=-1, keepdims=True)
    m2 = jnp.mean(dxh * xhat, axis=-1, keepdims=True)
    return rstd * (dxh - m1 - xhat * m2)


def _inproj(x_bf, win_g, cos_t, sin_t, b_gate, cargo):
    s = x_bf.shape[0]
    tm, tn = 512, 512

    def body(x_ref, w_ref, cos_ref, sin_ref, b_ref, o_ref):
        j = pl.program_id(1)
        acc = _dot(x_ref[...], w_ref[...])

        @pl.when(j < 2)
        def _():
            for c in range(tn // LANES):
                sl = slice(c * LANES, (c + 1) * LANES)
                o_ref[:, sl] = _rope(acc[:, sl], cos_ref[...], sin_ref[...])

        @pl.when((j >= 2) & (j < 4))
        def _():
            o_ref[...] = acc

        @pl.when(j >= 4)
        def _():
            o_ref[...] = jax.nn.sigmoid(acc + b_ref[...])

    (z,), carried = _cargo_call(
        body, cargo, name="inproj", grid=(s // tm, 4 * D_MODEL // tn),
        in_specs=[pl.BlockSpec((tm, D_MODEL), lambda i, j: (i, 0)),
                  pl.BlockSpec((None, D_MODEL, tn), lambda i, j: (j // 2, 0, j % 2)),
                  pl.BlockSpec((tm, LANES), lambda i, j: (i, 0)),
                  pl.BlockSpec((tm, LANES), lambda i, j: (i, 0)),
                  pl.BlockSpec((1, tn), lambda i, j: (0, jnp.maximum(j - 4, 0)))],
        out_specs=[pl.BlockSpec((tm, tn), lambda i, j: (i, j))],
        out_shape=[jax.ShapeDtypeStruct((s, 4 * D_MODEL), f32)])(x_bf, win_g, cos_t, sin_t, b_gate)
    return z, carried


def _moba_select(q, k_all, i, nb):
    lane = lax.broadcasted_iota(jnp.int32, q.shape, 1)
    k_mean = jnp.mean(k_all.reshape(nb, MOBA_BLOCK, LANES), axis=1).astype(bf16)
    n_io = lax.broadcasted_iota(jnp.int32, (MOBA_BLOCK, nb), 1)
    past = n_io < i
    sels = []
    for h in range(2):
        head = (lane < HEAD_DIM) if h == 0 else (lane >= HEAD_DIM)
        gate = _dot_nt(jnp.where(head, q, 0.0).astype(bf16), k_mean)
        g = jnp.where(past, gate, NEG)
        rank = jnp.zeros((MOBA_BLOCK, nb), f32)
        for m in range(nb):
            gm = g[:, m:m + 1]
            rank = rank + jnp.where((gm > g) | ((gm == g) & (m < n_io)), 1.0, 0.0)
        sels.append(jnp.where(past & (rank < MOBA_TOPK), 1.0, 0.0))
    return sels


def _attn_fwd(z, nb, cargo):
    s = z.shape[0]

    def body(q_ref, k_ref, v_ref, o_ref, lse_ref, m_s, l_s, acc_s):
        i = pl.program_id(1)
        lane = lax.broadcasted_iota(jnp.int32, (MOBA_BLOCK, LANES), 1)
        head0 = lane < HEAD_DIM
        heads = (head0, jnp.logical_not(head0))
        q = q_ref[...]
        sels = _moba_select(q, k_ref[...], i, nb)
        qs = q * SCALE
        qb = [jnp.where(hm, qs, 0.0).astype(bf16) for hm in heads]
        row = lax.broadcasted_iota(jnp.int32, (MOBA_BLOCK, MOBA_BLOCK), 0)
        col = lax.broadcasted_iota(jnp.int32, (MOBA_BLOCK, MOBA_BLOCK), 1)

        own = pl.ds(pl.multiple_of(i * MOBA_BLOCK, MOBA_BLOCK), MOBA_BLOCK)
        kj = k_ref[own, :].astype(bf16)
        vj = v_ref[own, :]
        ms, ls, pv = [], [], None
        for h in range(2):
            sc = jnp.where(col <= row, _dot_nt(qb[h], kj), NEG)
            m = jnp.max(sc, axis=1, keepdims=True)
            p = jnp.exp(sc - m)
            ms.append(m)
            ls.append(jnp.sum(p, axis=1, keepdims=True))
            t = _dot(p.astype(bf16), jnp.where(heads[h], vj, 0.0).astype(bf16))
            pv = t if pv is None else pv + t
        m_s[...] = jnp.where(head0, ms[0], ms[1])
        l_s[...] = jnp.where(head0, ls[0], ls[1])
        acc_s[...] = pv

        for j in range(nb - 1):
            @pl.when(j < i)
            def _(j=j):
                kj = k_ref[j * MOBA_BLOCK:(j + 1) * MOBA_BLOCK, :].astype(bf16)
                vj = v_ref[j * MOBA_BLOCK:(j + 1) * MOBA_BLOCK, :]
                m_old_l, l_old_l = m_s[...], l_s[...]
                m_new, alpha, l_new, pv = [], [], [], None
                for h in range(2):
                    c0 = h * HEAD_DIM
                    sc = jnp.where(sels[h][:, j:j + 1] > 0.0, _dot_nt(qb[h], kj), NEG)
                    m_old = m_old_l[:, c0:c0 + 1]
                    mn = jnp.maximum(m_old, jnp.max(sc, axis=1, keepdims=True))
                    a = jnp.exp(m_old - mn)
                    p = jnp.exp(sc - mn)
                    m_new.append(mn)
                    alpha.append(a)
                    l_new.append(a * l_old_l[:, c0:c0 + 1] + jnp.sum(p, axis=1, keepdims=True))
                    t = _dot(p.astype(bf16), jnp.where(heads[h], vj, 0.0).astype(bf16))
                    pv = t if pv is None else pv + t
                m_s[...] = jnp.where(head0, m_new[0], m_new[1])
                l_s[...] = jnp.where(head0, l_new[0], l_new[1])
                acc_s[...] = acc_s[...] * jnp.where(head0, alpha[0], alpha[1]) + pv

        o_ref[...] = (acc_s[...] / l_s[...]).astype(bf16)
        lse_ref[...] = m_s[...] + jnp.log(l_s[...])

    return _cargo_call(
        body, cargo, name="attn_fwd", grid=(D_ATTN // LANES, nb),
        in_specs=[pl.BlockSpec((MOBA_BLOCK, LANES), lambda hp, i: (i, hp)),
                  pl.BlockSpec((s, LANES), lambda hp, i: (0, 4 + hp)),
                  pl.BlockSpec((s, LANES), lambda hp, i: (0, 8 + hp))],
        out_specs=[pl.BlockSpec((MOBA_BLOCK, LANES), lambda hp, i: (i, hp)),
                   pl.BlockSpec((None, MOBA_BLOCK, LANES), lambda hp, i: (hp, i, 0))],
        out_shape=[jax.ShapeDtypeStruct((s, D_ATTN), bf16),
                   jax.ShapeDtypeStruct((D_ATTN // LANES, s, LANES), f32)],
        scratch=[pltpu.VMEM((MOBA_BLOCK, LANES), f32)] * 3)(z, z, z)


def _window_select(snaps, g):
    return jnp.where(g == 0, snaps[0], jnp.where(g == 1, snaps[1], jnp.where(g == 2, snaps[2], snaps[3])))


def _pool_fwd(z, w_pool, pool_scale):
    s = z.shape[0]

    def body(u_ref, w_ref, sc_ref, pooled_ref, mixed_ref, pm_ref, pad):
        g = pl.program_id(0)
        u = u_ref[...]
        pad[0:MAX_WINDOW, :] = jnp.zeros((MAX_WINDOW, POOL_GROUP), f32)
        pad[MAX_WINDOW:MAX_WINDOW + s, :] = u
        acc = u
        snaps = []
        for d in range(1, MAX_WINDOW):
            acc = acc + pad[MAX_WINDOW - d:MAX_WINDOW - d + s, :]
            if d + 1 in (2, 4, 8, 16):
                snaps.append(acc)
        win = _window_select(snaps, g)
        t = lax.broadcasted_iota(jnp.int32, (s, POOL_GROUP), 0)
        count = jnp.minimum(t + 1, jnp.left_shift(2, g)).astype(f32)
        pooled = (win / count - u).astype(bf16)
        mixed = _dot(pooled, w_ref[...].astype(bf16))
        pooled_ref[...] = pooled
        mixed_ref[...] = mixed
        pm_ref[...] = (mixed * sc_ref[...]).astype(bf16)

    blk = pl.BlockSpec((s, POOL_GROUP), lambda g: (0, g))
    return _call(
        body, name="pool_fwd", grid=(4,),
        in_specs=[pl.BlockSpec((s, POOL_GROUP), lambda g: (0, 12 + g)),
                  pl.BlockSpec((None, POOL_GROUP, POOL_GROUP), lambda g: (g, 0, 0)),
                  pl.BlockSpec((1, POOL_GROUP), lambda g: (0, g))],
        out_specs=[blk, blk, blk],
        out_shape=[jax.ShapeDtypeStruct((s, D_POOL), bf16), jax.ShapeDtypeStruct((s, D_POOL), f32),
                   jax.ShapeDtypeStruct((s, D_POOL), bf16)],
        scratch=[pltpu.VMEM((s + MAX_WINDOW, POOL_GROUP), f32)])(z, w_pool, pool_scale)


def _branch_merge(o_bf, pm_bf, z, wba, wbp):
    s = o_bf.shape[0]
    tm = 256

    def body(o_ref, pm_ref, ga_ref, gp_ref, wba_ref, wbp_ref, m_ref):
        ya = _dot(o_ref[...], wba_ref[...])
        yp = _dot(pm_ref[...], wbp_ref[...])
        m_ref[...] = (ga_ref[...] * ya + gp_ref[...] * yp).astype(bf16)

    full = lambda r, c: pl.BlockSpec((r, c), lambda i: (0, 0))
    return _call(
        body, name="branch_merge", grid=(s // tm,),
        in_specs=[pl.BlockSpec((tm, D_ATTN), lambda i: (i, 0)), pl.BlockSpec((tm, D_POOL), lambda i: (i, 0)),
                  pl.BlockSpec((tm, D_MODEL), lambda i: (i, 2)), pl.BlockSpec((tm, D_MODEL), lambda i: (i, 3)),
                  full(D_ATTN, D_MODEL), full(D_POOL, D_MODEL)],
        out_specs=pl.BlockSpec((tm, D_MODEL), lambda i: (i, 0)),
        out_shape=jax.ShapeDtypeStruct((s, D_MODEL), bf16))(o_bf, pm_bf, z, z, wba, wbp)


def _out_ln1(m_bf, wout, x, ln_g, ln_b):
    s = x.shape[0]
    tm = 256

    def body(m_ref, w_ref, x_ref, g_ref, b_ref, x1_ref, xhat_ref, rstd_ref):
        h = ALPHA * x_ref[...] + _dot(m_ref[...], w_ref[...])
        y, xhat, rstd = _layer_norm(h, g_ref[...], b_ref[...])
        x1_ref[...] = y.astype(bf16)
        xhat_ref[...] = xhat
        rstd_ref[...] = jnp.broadcast_to(rstd, (tm, LANES))

    row = pl.BlockSpec((tm, D_MODEL), lambda i: (i, 0))
    vec = pl.BlockSpec((1, D_MODEL), lambda i: (0, 0))
    return _call(
        body, name="out_ln1", grid=(s // tm,),
        in_specs=[row, pl.BlockSpec((D_MODEL, D_MODEL), lambda i: (0, 0)), row, vec, vec],
        out_specs=[row, row, pl.BlockSpec((tm, LANES), lambda i: (i, 0))],
        out_shape=[jax.ShapeDtypeStruct((s, D_MODEL), bf16), jax.ShapeDtypeStruct((s, D_MODEL), f32),
                   jax.ShapeDtypeStruct((s, LANES), f32)])(m_bf, wout, x, ln_g, ln_b)


FF_TILE = 256
FF_TILES_PER_SHARD = FF_PAD // FF_TILE
CONV_PAD = 8
CONV_ROWS = 16


def _ff_weight_spec(rows):
    return pl.BlockSpec((None, rows, FF_TILE), lambda j: (j // FF_TILES_PER_SHARD, 0, j % FF_TILES_PER_SHARD))


def _conv_taps(pad, s, cw_ref, off=0):
    return (pad[CONV_PAD + off:CONV_PAD + off + s, :] * cw_ref[2:3, :]
            + pad[CONV_PAD - 1 + off:CONV_PAD - 1 + off + s, :] * cw_ref[1:2, :]
            + pad[CONV_PAD - 2 + off:CONV_PAD - 2 + off + s, :] * cw_ref[0:1, :])


def _ffn_up(x1_bf, wg_g, wu_g, convw_g, convb_g):
    s = x1_bf.shape[0]

    def body(x_ref, wg_ref, wu_ref, cw_ref, cb_ref, a_ref, up_ref, hh_ref, pad):
        x1 = x_ref[...]
        a = _dot(x1, wg_ref[...])
        up = _dot(x1, wu_ref[...])
        a_ref[...] = a
        up_ref[...] = up
        pad[0:CONV_PAD, :] = jnp.zeros((CONV_PAD, FF_TILE), f32)
        pad[CONV_PAD:CONV_PAD + s, :] = a
        ac = _conv_taps(pad, s, cw_ref) + cb_ref[...]
        hg, _ = _gelu_parts(ac)
        hh_ref[...] = (hg * up).astype(bf16)

    col = pl.BlockSpec((s, FF_TILE), lambda j: (0, j))
    return _call(
        body, name="ffn_up", grid=(D_FF_PAD // FF_TILE,),
        in_specs=[pl.BlockSpec((s, D_MODEL), lambda j: (0, 0)), _ff_weight_spec(D_MODEL), _ff_weight_spec(D_MODEL),
                  _ff_weight_spec(8), _ff_weight_spec(1)],
        out_specs=[col, col, col],
        out_shape=[jax.ShapeDtypeStruct((s, D_FF_PAD), f32), jax.ShapeDtypeStruct((s, D_FF_PAD), f32),
                   jax.ShapeDtypeStruct((s, D_FF_PAD), bf16)],
        scratch=[pltpu.VMEM((s + CONV_PAD, FF_TILE), f32)])(x1_bf, wg_g, wu_g, convw_g, convb_g)


def _ffn_down_ln2_loss(hh_bf, wd, xhat1, ln1_g, ln1_b, ln2_g, ln2_b, target):
    s = hh_bf.shape[0]
    tm = 256

    def body(hh_ref, w_ref, xh1_ref, g1_ref, b1_ref, g2_ref, b2_ref, t_ref, loss_ref, dh_ref, dhb_ref, dg_ref, db_ref):
        i = pl.program_id(0)
        x1 = xh1_ref[...] * g1_ref[...] + b1_ref[...]
        h = ALPHA * x1 + _dot(hh_ref[...], w_ref[...])
        y, xhat, rstd = _layer_norm(h, g2_ref[...], b2_ref[...])
        err = y - t_ref[...]
        part = 0.5 * jnp.sum(jnp.mean(err * err, axis=-1, keepdims=True), axis=0, keepdims=True)
        dy = err * (1.0 / D_MODEL)

        @pl.when(i == 0)
        def _():
            loss_ref[...] = jnp.zeros_like(loss_ref)
            dg_ref[...] = jnp.zeros_like(dg_ref)
            db_ref[...] = jnp.zeros_like(db_ref)

        loss_ref[...] += jnp.broadcast_to(part, loss_ref.shape)
        dg_ref[...] += jnp.sum(dy * xhat, axis=0, keepdims=True)
        db_ref[...] += jnp.sum(dy, axis=0, keepdims=True)
        dh = _layer_norm_bwd(dy, xhat, rstd, g2_ref[...])
        dh_ref[...] = dh
        dhb_ref[...] = dh.astype(bf16)

    row = pl.BlockSpec((tm, D_MODEL), lambda i: (i, 0))
    vec = pl.BlockSpec((1, D_MODEL), lambda i: (0, 0))
    return _call(
        body, name="ffn_down_ln2_loss", grid=(s // tm,),
        in_specs=[pl.BlockSpec((tm, D_FF_PAD), lambda i: (i, 0)), pl.BlockSpec((D_FF_PAD, D_MODEL), lambda i: (0, 0)),
                  row, vec, vec, vec, vec, row],
        out_specs=[pl.BlockSpec((8, LANES), lambda i: (0, 0)), row, row, vec, vec],
        out_shape=[jax.ShapeDtypeStruct((8, LANES), f32), jax.ShapeDtypeStruct((s, D_MODEL), f32),
                   jax.ShapeDtypeStruct((s, D_MODEL), bf16), jax.ShapeDtypeStruct((1, D_MODEL), f32),
                   jax.ShapeDtypeStruct((1, D_MODEL), f32)])(hh_bf, wd, xhat1, ln1_g, ln1_b, ln2_g, ln2_b, target)


def _ffn_act_bwd(dh2_bf, wd_g, a, up, convw_g, convb_g):
    s = dh2_bf.shape[0]

    def body(dh_ref, wd_ref, a_ref, up_ref, cw_ref, cb_ref, da_ref, dup_ref, dc_ref, pad, dpad):
        dhh = _dot_nt(dh_ref[...], wd_ref[...])
        zeros = jnp.zeros((CONV_PAD, FF_TILE), f32)
        pad[0:CONV_PAD, :] = zeros
        pad[CONV_PAD:CONV_PAD + s, :] = a_ref[...]
        ac = _conv_taps(pad, s, cw_ref) + cb_ref[...]
        hg, dgelu = _gelu_parts(ac)
        up = up_ref[...]
        dup_ref[...] = (dhh * hg).astype(bf16)
        dac = dhh * up * dgelu
        rows = [jnp.sum(dac * pad[CONV_PAD - 2 + i:CONV_PAD - 2 + i + s, :], axis=0, keepdims=True) for i in range(3)]
        rows.append(jnp.sum(dac, axis=0, keepdims=True))
        rows.append(jnp.zeros((CONV_ROWS - 4, FF_TILE), f32))
        dc_ref[...] = jnp.concatenate(rows, axis=0)
        dpad[0:s, :] = dac
        dpad[s:s + CONV_PAD, :] = zeros
        da = (dpad[0:s, :] * cw_ref[2:3, :] + dpad[1:1 + s, :] * cw_ref[1:2, :] + dpad[2:2 + s, :] * cw_ref[0:1, :])
        da_ref[...] = da.astype(bf16)

    col = pl.BlockSpec((s, FF_TILE), lambda j: (0, j))
    return _call(
        body, name="ffn_act_bwd", grid=(D_FF_PAD // FF_TILE,),
        in_specs=[pl.BlockSpec((s, D_MODEL), lambda j: (0, 0)),
                  pl.BlockSpec((None, FF_TILE, D_MODEL),
                               lambda j: (j // FF_TILES_PER_SHARD, j % FF_TILES_PER_SHARD, 0)),
                  col, col, _ff_weight_spec(8), _ff_weight_spec(1)],
        out_specs=[col, col, _ff_weight_spec(CONV_ROWS)],
        out_shape=[jax.ShapeDtypeStruct((s, D_FF_PAD), bf16), jax.ShapeDtypeStruct((s, D_FF_PAD), bf16),
                   jax.ShapeDtypeStruct((N_CHIPS, CONV_ROWS, FF_PAD), f32)],
        scratch=[pltpu.VMEM((s + CONV_PAD, FF_TILE), f32), pltpu.VMEM((s + CONV_PAD, FF_TILE), f32)],
    )(dh2_bf, wd_g, a, up, convw_g, convb_g)


def _matmul_tn(a, b, n_shards, name, tm=512, tn=256):
    k, m = a.shape
    n = b.shape[1]
    tm, tn = min(tm, m), min(tn, n // n_shards)
    per = n // n_shards // tn

    def body(a_ref, b_ref, o_ref):
        o_ref[...] = _dot_tn(a_ref[...], b_ref[...]).astype(bf16)

    return _call(
        body, name=name, grid=(m // tm, n // tn),
        in_specs=[pl.BlockSpec((k, tm), lambda i, j: (0, i)), pl.BlockSpec((k, tn), lambda i, j: (0, j))],
        out_specs=pl.BlockSpec((None, tm, tn), lambda i, j: (j // per, i, j % per)),
        out_shape=jax.ShapeDtypeStruct((n_shards, m, n // n_shards), bf16))(a, b)


def _ffn_in_bwd_ln1(da_bf, dup_bf, wg_g, wu_g, dh2, xhat1, rstd1, ln1_g):
    s = da_bf.shape[0]
    tm = 256

    def body(da_ref, dup_ref, wg_ref, wu_ref, dh2_ref, xh_ref, rstd_ref, g_ref, dh_ref, dhb_ref, dg_ref, db_ref):
        i = pl.program_id(0)
        dx1 = ALPHA * dh2_ref[...]
        for sh in range(N_CHIPS):
            sl = slice(sh * FF_PAD, (sh + 1) * FF_PAD)
            dx1 = dx1 + _dot_nt(da_ref[:, sl], wg_ref[sh]) + _dot_nt(dup_ref[:, sl], wu_ref[sh])
        xhat = xh_ref[...]

        @pl.when(i == 0)
        def _():
            dg_ref[...] = jnp.zeros_like(dg_ref)
            db_ref[...] = jnp.zeros_like(db_ref)

        dg_ref[...] += jnp.sum(dx1 * xhat, axis=0, keepdims=True)
        db_ref[...] += jnp.sum(dx1, axis=0, keepdims=True)
        dh = _layer_norm_bwd(dx1, xhat, rstd_ref[:, 0:1], g_ref[...])
        dh_ref[...] = dh
        dhb_ref[...] = dh.astype(bf16)

    row = pl.BlockSpec((tm, D_MODEL), lambda i: (i, 0))
    wide = pl.BlockSpec((tm, D_FF_PAD), lambda i: (i, 0))
    wfull = pl.BlockSpec((N_CHIPS, D_MODEL, FF_PAD), lambda i: (0, 0, 0))
    vec = pl.BlockSpec((1, D_MODEL), lambda i: (0, 0))
    return _call(
        body, name="ffn_in_bwd_ln1", grid=(s // tm,),
        in_specs=[wide, wide, wfull, wfull, row, row, pl.BlockSpec((tm, LANES), lambda i: (i, 0)), vec],
        out_specs=[row, row, vec, vec],
        out_shape=[jax.ShapeDtypeStruct((s, D_MODEL), f32), jax.ShapeDtypeStruct((s, D_MODEL), bf16),
                   jax.ShapeDtypeStruct((1, D_MODEL), f32), jax.ShapeDtypeStruct((1, D_MODEL), f32)],
    )(da_bf, dup_bf, wg_g, wu_g, dh2, xhat1, rstd1, ln1_g)


def _merge_bwd(dh1_bf, wout, o_bf, pm_bf, z, wba, wbp):
    s = dh1_bf.shape[0]
    tm = 256

    def body(dh_ref, wout_ref, o_ref, pm_ref, ga_ref, gp_ref, wba_ref, wbp_ref,
             dzg_ref, dya_ref, dyp_ref, do_ref, dpm_ref, dbg_ref):
        i = pl.program_id(0)
        dm = _dot_nt(dh_ref[...], wout_ref[...])
        ya = _dot(o_ref[...], wba_ref[...])
        yp = _dot(pm_ref[...], wbp_ref[...])
        ga, gp = ga_ref[...], gp_ref[...]
        dza = dm * ya * ga * (1.0 - ga)
        dzp = dm * yp * gp * (1.0 - gp)

        @pl.when(i == 0)
        def _():
            dbg_ref[...] = jnp.zeros_like(dbg_ref)

        dbg_ref[:, 0:D_MODEL] += jnp.sum(dza, axis=0, keepdims=True)
        dbg_ref[:, D_MODEL:2 * D_MODEL] += jnp.sum(dzp, axis=0, keepdims=True)
        dzg_ref[:, 0:D_MODEL] = dza.astype(bf16)
        dzg_ref[:, D_MODEL:2 * D_MODEL] = dzp.astype(bf16)
        dya = (dm * ga).astype(bf16)
        dyp = (dm * gp).astype(bf16)
        dya_ref[...] = dya
        dyp_ref[...] = dyp
        do_ref[...] = _dot_nt(dya, wba_ref[...]).astype(bf16)
        dpm_ref[...] = _dot_nt(dyp, wbp_ref[...])

    row = pl.BlockSpec((tm, D_MODEL), lambda i: (i, 0))
    half = pl.BlockSpec((tm, D_ATTN), lambda i: (i, 0))
    full = lambda r, c: pl.BlockSpec((r, c), lambda i: (0, 0))
    return _call(
        body, name="merge_bwd", grid=(s // tm,),
        in_specs=[row, full(D_MODEL, D_MODEL), half, half,
                  pl.BlockSpec((tm, D_MODEL), lambda i: (i, 2)), pl.BlockSpec((tm, D_MODEL), lambda i: (i, 3)),
                  full(D_ATTN, D_MODEL), full(D_POOL, D_MODEL)],
        out_specs=[pl.BlockSpec((tm, 2 * D_MODEL), lambda i: (i, 0)), row, row, half, half,
                   pl.BlockSpec((1, 2 * D_MODEL), lambda i: (0, 0))],
        out_shape=[jax.ShapeDtypeStruct((s, 2 * D_MODEL), bf16), jax.ShapeDtypeStruct((s, D_MODEL), bf16),
                   jax.ShapeDtypeStruct((s, D_MODEL), bf16), jax.ShapeDtypeStruct((s, D_ATTN), bf16),
                   jax.ShapeDtypeStruct((s, D_POOL), f32), jax.ShapeDtypeStruct((1, 2 * D_MODEL), f32)],
    )(dh1_bf, wout, o_bf, pm_bf, z, z, wba, wbp)


def _pool_bwd(dpm, mixed, pooled_bf, w_pool, pool_scale):
    s = dpm.shape[0]

    def body(dpm_ref, mixed_ref, pooled_ref, w_ref, sc_ref, du_ref, dw_ref, dsc_ref, pad):
        g = pl.program_id(0)
        dpm_v = dpm_ref[...]
        dsc_ref[...] = jnp.sum(dpm_v * mixed_ref[...], axis=0, keepdims=True)
        dmixed = (dpm_v * sc_ref[...]).astype(bf16)
        dw_ref[...] = _dot_tn(pooled_ref[...], dmixed)
        dpooled = _dot_nt(dmixed, w_ref[...].astype(bf16))
        t = lax.broadcasted_iota(jnp.int32, (s, POOL_GROUP), 0)
        count = jnp.minimum(t + 1, jnp.left_shift(2, g)).astype(f32)
        r = dpooled / count
        pad[0:s, :] = r
        pad[s:s + MAX_WINDOW, :] = jnp.zeros((MAX_WINDOW, POOL_GROUP), f32)
        acc = r
        snaps = []
        for d in range(1, MAX_WINDOW):
            acc = acc + pad[d:d + s, :]
            if d + 1 in (2, 4, 8, 16):
                snaps.append(acc)
        du_ref[...] = (_window_select(snaps, g) - dpooled).astype(bf16)

    blk = pl.BlockSpec((s, POOL_GROUP), lambda g: (0, g))
    return _call(
        body, name="pool_bwd", grid=(4,),
        in_specs=[blk, blk, blk, pl.BlockSpec((None, POOL_GROUP, POOL_GROUP), lambda g: (g, 0, 0)),
                  pl.BlockSpec((1, POOL_GROUP), lambda g: (0, g))],
        out_specs=[blk, pl.BlockSpec((None, POOL_GROUP, POOL_GROUP), lambda g: (g, 0, 0)),
                   pl.BlockSpec((1, POOL_GROUP), lambda g: (0, g))],
        out_shape=[jax.ShapeDtypeStruct((s, D_POOL), bf16), jax.ShapeDtypeStruct((4, POOL_GROUP, POOL_GROUP), f32),
                   jax.ShapeDtypeStruct((1, D_POOL), f32)],
        scratch=[pltpu.VMEM((s + MAX_WINDOW, POOL_GROUP), f32)])(dpm, mixed, pooled_bf, w_pool, pool_scale)


def _attn_bwd(z, o_bf, lse, do_bf, cos_t, sin_t, nb, cargo):
    s = z.shape[0]

    def body(q_ref, k_ref, v_ref, o_ref, lse_ref, do_ref, cq_ref, sq_ref, cf_ref, sf_ref,
             dq_ref, dk_ref, dv_ref, dq_s, dk_s, dv_s):
        i = pl.program_id(1)
        lane = lax.broadcasted_iota(jnp.int32, (MOBA_BLOCK, LANES), 1)
        head0 = lane < HEAD_DIM
        heads = (head0, jnp.logical_not(head0))

        @pl.when(i == 0)
        def _():
            dk_s[...] = jnp.zeros_like(dk_s)
            dv_s[...] = jnp.zeros_like(dv_s)

        q = q_ref[...]
        sels = _moba_select(q, k_ref[...], i, nb)
        qs = q * SCALE
        qb = [jnp.where(hm, qs, 0.0).astype(bf16) for hm in heads]
        qraw = [jnp.where(hm, q, 0.0).astype(bf16) for hm in heads]
        do = do_ref[...].astype(f32)
        prod = do * o_ref[...].astype(f32)
        delta = [jnp.sum(jnp.where(hm, prod, 0.0), axis=1, keepdims=True) for hm in heads]
        dob = [jnp.where(hm, do, 0.0).astype(bf16) for hm in heads]
        lse_l = lse_ref[...]
        lse_h = [lse_l[:, 0:1], lse_l[:, HEAD_DIM:HEAD_DIM + 1]]
        row = lax.broadcasted_iota(jnp.int32, (MOBA_BLOCK, MOBA_BLOCK), 0)
        col = lax.broadcasted_iota(jnp.int32, (MOBA_BLOCK, MOBA_BLOCK), 1)
        dq_s[...] = jnp.zeros_like(dq_s)

        def block(rows, allow):
            kj = k_ref[rows, :].astype(bf16)
            vj = v_ref[rows, :].astype(bf16)
            dq, dk, dv = None, None, None
            for h in range(2):
                p = jnp.where(allow[h], jnp.exp(_dot_nt(qb[h], kj) - lse_h[h]), 0.0)
                dp = _dot_nt(dob[h], vj)
                ds = (p * (dp - delta[h]) * SCALE).astype(bf16)
                tv = _dot_tn(p.astype(bf16), dob[h])
                tq = _dot(ds, jnp.where(heads[h], kj, jnp.zeros_like(kj)))
                tk = _dot_tn(ds, qraw[h])
                dq, dk, dv = (tq, tk, tv) if dq is None else (dq + tq, dk + tk, dv + tv)
            dq_s[...] += dq
            dk_s[rows, :] += dk
            dv_s[rows, :] += dv

        block(pl.ds(pl.multiple_of(i * MOBA_BLOCK, MOBA_BLOCK), MOBA_BLOCK), [col <= row, col <= row])
        for j in range(nb - 1):
            @pl.when(j < i)
            def _(j=j):
                block(slice(j * MOBA_BLOCK, (j + 1) * MOBA_BLOCK), [sels[h][:, j:j + 1] > 0.0 for h in range(2)])

        dq_ref[...] = _rope_bwd(dq_s[...], cq_ref[...], sq_ref[...]).astype(bf16)

        @pl.when(i == nb - 1)
        def _():
            dk_ref[...] = _rope_bwd(dk_s[...], cf_ref[...], sf_ref[...]).astype(bf16)
            dv_ref[...] = dv_s[...].astype(bf16)

    qblk = pl.BlockSpec((MOBA_BLOCK, LANES), lambda hp, i: (i, hp))
    tq = pl.BlockSpec((MOBA_BLOCK, LANES), lambda hp, i: (i, 0))
    tf = pl.BlockSpec((s, LANES), lambda hp, i: (0, 0))
    colblk = pl.BlockSpec((s, LANES), lambda hp, i: (0, hp))
    return _cargo_call(
        body, cargo, name="attn_bwd", grid=(D_ATTN // LANES, nb),
        in_specs=[qblk, pl.BlockSpec((s, LANES), lambda hp, i: (0, 4 + hp)),
                  pl.BlockSpec((s, LANES), lambda hp, i: (0, 8 + hp)), qblk,
                  pl.BlockSpec((None, MOBA_BLOCK, LANES), lambda hp, i: (hp, i, 0)), qblk, tq, tq, tf, tf],
        out_specs=[qblk, colblk, colblk],
        out_shape=[jax.ShapeDtypeStruct((s, D_ATTN), bf16)] * 3,
        scratch=[pltpu.VMEM((MOBA_BLOCK, LANES), f32), pltpu.VMEM((s, LANES), f32), pltpu.VMEM((s, LANES), f32)],
    )(z, z, z, o_bf, lse, do_bf, cos_t, sin_t, cos_t, sin_t)


def _inproj_dx(dz_bf, win_g, dh1):
    s = dz_bf.shape[0]
    tm = 256

    def body(dz_ref, w_ref, dh_ref, gx_ref):
        acc = ALPHA * dh_ref[...]
        for sh in range(N_CHIPS):
            acc = acc + _dot_nt(dz_ref[:, sh * D_MODEL:(sh + 1) * D_MODEL], w_ref[sh])
        gx_ref[...] = acc

    row = pl.BlockSpec((tm, D_MODEL), lambda i: (i, 0))
    return _call(
        body, name="inproj_dx", grid=(s // tm,),
        in_specs=[pl.BlockSpec((tm, 4 * D_MODEL), lambda i: (i, 0)),
                  pl.BlockSpec((N_CHIPS, D_MODEL, D_MODEL), lambda i: (0, 0, 0)), row],
        out_specs=row, out_shape=jax.ShapeDtypeStruct((s, D_MODEL), f32))(dz_bf, win_g, dh1)


ANY = pl.BlockSpec(memory_space=pl.ANY)


def _chip_index():
    return 2 * lax.axis_index("x") + lax.axis_index("y")


def _peer(k):
    x, y, c = lax.axis_index("x"), lax.axis_index("y"), lax.axis_index("c")
    return (x ^ (k >> 1), y ^ (k & 1), c)


def _sibling():
    return (lax.axis_index("x"), lax.axis_index("y"), 1 - lax.axis_index("c"))


class _GatherCargo:
    def __init__(self, shards, pass_on):
        self.shards, self.pass_on, n = shards, pass_on, len(shards)
        self.inputs = [a.reshape(2, a.shape[0] // 2, a.shape[1]) for a in shards]
        self.out_shape = [jax.ShapeDtypeStruct((N_CHIPS,) + a.shape, a.dtype) for a in self.inputs]
        sem, one = pltpu.SemaphoreType.DMA((N_CHIPS - 1, n)), pltpu.SemaphoreType.DMA((n,))
        self.sems = [sem, sem, sem, sem, one, one]

    def _copies(self, src, dst, sems, a):
        send_sems, recv_sems, fsend_sems, frecv_sems, local_sems, own_sems = sems
        p, c = _chip_index(), lax.axis_index("c")
        own = pltpu.make_async_remote_copy(src_ref=src[a], dst_ref=dst[a].at[p], send_sem=local_sems.at[a],
                                           recv_sem=own_sems.at[a], device_id=_sibling(), device_id_type=MESH)
        out, arrive, onward, landed = [], [], [], []
        for k in range(1, N_CHIPS):
            ici = dict(send_sem=send_sems.at[k - 1, a], recv_sem=recv_sems.at[k - 1, a], device_id=_peer(k),
                       device_id_type=MESH)
            d2d = dict(send_sem=fsend_sems.at[k - 1, a], recv_sem=frecv_sems.at[k - 1, a], device_id=_sibling(),
                       device_id_type=MESH)
            got, theirs = dst[a].at[p ^ k, c], dst[a].at[p ^ k, 1 - c]
            out.append(pltpu.make_async_remote_copy(src_ref=src[a].at[c], dst_ref=dst[a].at[p, c], **ici))
            arrive.append(pltpu.make_async_remote_copy(src_ref=src[a].at[c], dst_ref=got, **ici))
            onward.append(pltpu.make_async_remote_copy(src_ref=got, dst_ref=got, **d2d))
            landed.append(pltpu.make_async_remote_copy(src_ref=theirs, dst_ref=theirs, **d2d))
        return own, out, arrive, onward, landed

    def stages(self, steps):
        n = len(self.shards)

        def start(src, dst, sems):
            for a in range(n):
                own, out, _, _, _ = self._copies(src, dst, sems, a)
                own.start()
                for cp in out:
                    cp.start()

        def pass_on(a):
            def act(src, dst, sems):
                _, _, arrive, onward, _ = self._copies(src, dst, sems, a)
                for k in range(N_CHIPS - 1):
                    arrive[k].wait_recv()
                    onward[k].start()
            return act

        def finish(src, dst, sems):
            for a in range(n):
                own, out, _, onward, landed = self._copies(src, dst, sems, a)
                for cp in landed:
                    cp.wait_recv()
                for cp in out + onward:
                    cp.wait_send()
                own.wait()

        mids = [(min(steps - 1, int(self.pass_on[a] * steps)), pass_on(a)) for a in range(n)]
        return [(0, start)] + mids + [(steps - 1, finish)]

    def results(self, outs):
        return [o.reshape((N_CHIPS,) + a.shape) for o, a in zip(outs, self.shards)]


class _ExchangeCargo:
    def __init__(self, sums, whole):
        self.inputs, self.whole, n = sums, whole, len(sums)
        self.out_shape = [jax.ShapeDtypeStruct((N_CHIPS - 1,) + g.shape[1:], g.dtype) for g in sums]
        sem = pltpu.SemaphoreType.DMA((N_CHIPS - 1, n))
        self.sems = [sem, sem]

    def _copies(self, src, dst, sems):
        send_sems, recv_sems = sems
        p = _chip_index()
        return [pltpu.make_async_remote_copy(
            src_ref=src[a].at[0] if self.whole[a] else src[a].at[p ^ k], dst_ref=dst[a].at[k - 1],
            send_sem=send_sems.at[k - 1, a], recv_sem=recv_sems.at[k - 1, a], device_id=_peer(k), device_id_type=MESH)
            for k in range(1, N_CHIPS) for a in range(len(self.inputs))]

    def stages(self, steps):
        def start(src, dst, sems):
            for cp in self._copies(src, dst, sems):
                cp.start()

        def finish(src, dst, sems):
            copies = self._copies(src, dst, sems)
            for cp in copies:
                cp.wait_recv()
            for cp in copies:
                cp.wait_send()

        return [(0, start), (steps - 1, finish)]

    def results(self, outs):
        return list(outs)


def _cargo_call(body, cargo, *, name, grid, in_specs, out_specs, out_shape, scratch=()):
    n_in, n_out, n_scr = len(in_specs), len(out_specs), len(scratch)
    c_in, c_out = len(cargo.inputs), len(cargo.out_shape)
    steps = 1
    for g in grid:
        steps *= g
    stages = cargo.stages(steps)

    def wrapped(*refs):
        ins, refs = refs[:n_in], refs[n_in:]
        cin, refs = refs[:c_in], refs[c_in:]
        outs, refs = refs[:n_out], refs[n_out:]
        cout, refs = refs[:c_out], refs[c_out:]
        scr, sems = refs[:n_scr], refs[n_scr:]
        if not grid:
            for _, act in stages:
                act(cin, cout, sems)
            return
        step = 0
        for d in range(len(grid)):
            step = step * grid[d] + pl.program_id(d)
        pl.when(step == 0)(functools.partial(stages[0][1], cin, cout, sems))
        body(*ins, *outs, *scr)
        for at, act in stages[1:]:
            pl.when(step == at)(functools.partial(act, cin, cout, sems))

    params = dict(vmem_limit_bytes=VMEM_LIMIT)
    if grid:
        params["dimension_semantics"] = ("arbitrary",) * len(grid)
    res = pl.pallas_call(
        wrapped, name=name, grid=grid, in_specs=list(in_specs) + [ANY] * c_in, out_specs=list(out_specs) + [ANY] * c_out,
        out_shape=list(out_shape) + cargo.out_shape, scratch_shapes=list(scratch) + cargo.sems,
        compiler_params=pltpu.CompilerParams(**params))
    return lambda *args: (lambda r: (r[:n_out], cargo.results(r[n_out:])))(res(*args, *cargo.inputs))


def _split_halves(g):
    return g.reshape(g.shape[0], 2, g.shape[1] // 2, g.shape[2])


def _ew_tile(rows):
    return 256 if rows % 256 == 0 else rows


def _prefetch_call(body, *, name, grid, in_specs, out_specs, out_shape, **kw):
    return pl.pallas_call(
        body, name=name, out_shape=out_shape,
        grid_spec=pltpu.PrefetchScalarGridSpec(num_scalar_prefetch=1, grid=grid, in_specs=in_specs, out_specs=out_specs),
        compiler_params=pltpu.CompilerParams(dimension_semantics=("arbitrary",) * len(grid),
                                             vmem_limit_bytes=VMEM_LIMIT), **kw)


def _presum_swap(split, name):
    n = len(split)

    def body(*refs):
        src, got = refs[:n], refs[n:2 * n]
        send_sems, recv_sems = refs[2 * n:]
        c = lax.axis_index("c")
        remote = [pltpu.make_async_remote_copy(src_ref=src[a].at[:, 1 - c], dst_ref=got[a],
                                               send_sem=send_sems.at[a], recv_sem=recv_sems.at[a],
                                               device_id=_sibling(), device_id_type=MESH) for a in range(n)]
        for cp in remote:
            cp.start()
        for cp in remote:
            cp.wait_recv()
        for cp in remote:
            cp.wait_send()

    sem = pltpu.SemaphoreType.DMA((n,))
    return pl.pallas_call(
        body, name=name, in_specs=[ANY] * n, out_specs=[ANY] * n,
        out_shape=[jax.ShapeDtypeStruct((g.shape[0],) + g.shape[2:], g.dtype) for g in split],
        scratch_shapes=[sem, sem])(*split)


def _add_pair(mine, theirs, place, name):
    lead, _, rows, cols = mine.shape
    tm = _ew_tile(rows)

    def body(place_ref, a_ref, b_ref, o_ref):
        o_ref[...] = (a_ref[...].astype(f32) + b_ref[...].astype(f32)).astype(o_ref.dtype)

    blk = pl.BlockSpec((lead, tm, cols), lambda i, pc: (0, i, 0))
    return _prefetch_call(
        body, name=name, grid=(rows // tm,),
        in_specs=[pl.BlockSpec((lead, None, tm, cols), lambda i, pc: (0, pc[1], i, 0)), blk], out_specs=blk,
        out_shape=jax.ShapeDtypeStruct(theirs.shape, theirs.dtype))(place, mine, theirs)


class _NoCargo:
    def __init__(self, results=()):
        self.inputs, self.out_shape, self.sems, self._results = [], [], [], list(results)

    def stages(self, steps):
        return [(0, lambda src, dst, sems: None), (steps - 1, lambda src, dst, sems: None)]

    def results(self, outs):
        return self._results


def _cargo_alone(cargo, name):
    return _cargo_call(None, cargo, name=name, grid=(), in_specs=[], out_specs=[], out_shape=[])()[1]


def _sum_slots(own, recv, place, whole, name):
    _, rows, cols = own.shape
    tm = _ew_tile(rows)

    def body(place_ref, own_ref, r_ref, o_ref):
        r0, r1, r2 = [r_ref[k].astype(f32) for k in range(N_CHIPS - 1)]
        o_ref[...] = (own_ref[...].astype(f32) + r0) + (r1 + r2)

    return _prefetch_call(
        body, name=name, grid=(rows // tm,),
        in_specs=[pl.BlockSpec((None, tm, cols), lambda i, pc: (0 if whole else pc[0], i, 0)),
                  pl.BlockSpec((N_CHIPS - 1, tm, cols), lambda i, pc: (0, i, 0))],
        out_specs=pl.BlockSpec((None, tm, cols), lambda i, pc: (pc[1], i, 0)),
        out_shape=jax.ShapeDtypeStruct((2, rows, cols), f32))(place, own, recv)


def _sibling_fill(bufs):
    n = len(bufs)

    def body(*refs):
        src, dst = refs[:n], refs[n:2 * n]
        send_sems, recv_sems = refs[2 * n:]
        c = lax.axis_index("c")
        remote = [pltpu.make_async_remote_copy(src_ref=src[a].at[c], dst_ref=dst[a].at[c], send_sem=send_sems.at[a],
                                               recv_sem=recv_sems.at[a], device_id=_sibling(), device_id_type=MESH)
                  for a in range(n)]
        for cp in remote:
            cp.start()
        for a in range(n):
            pltpu.make_async_remote_copy(src_ref=src[a].at[c], dst_ref=dst[a].at[1 - c], send_sem=send_sems.at[a],
                                         recv_sem=recv_sems.at[a], device_id=_sibling(), device_id_type=MESH).wait_recv()
        for cp in remote:
            cp.wait_send()

    sem = pltpu.SemaphoreType.DMA((n,))
    out = pl.pallas_call(
        body, name="sibling_fill", in_specs=[ANY] * n, out_specs=[ANY] * n,
        out_shape=[jax.ShapeDtypeStruct(b.shape, b.dtype) for b in bufs],
        input_output_aliases={a: a for a in range(n)}, scratch_shapes=[sem, sem])(*bufs)
    return [o.reshape(2 * b.shape[1], b.shape[2]) for o, b in zip(out, bufs)]


def _adamw(w, m, v, grad, name, tm):
    rows, cols = w.shape
    gcols = grad.shape[1]
    if tm == rows:
        gspec = pl.BlockSpec(grad.shape, lambda i: (0, 0))
    else:
        gspec = pl.BlockSpec((tm, gcols), lambda i: (i, 0))

    def body(w_ref, m_ref, v_ref, gin_ref, g_ref, d_ref, nm_ref, nv_ref):
        g = gin_ref[0:tm, 0:cols]
        w_, m_, v_ = w_ref[...], m_ref[...], v_ref[...]
        m_ = ADAM_B1 * m_ + (1.0 - ADAM_B1) * g
        v_ = ADAM_B2 * v_ + (1.0 - ADAM_B2) * jnp.square(g)
        m_hat = m_ / (1.0 - ADAM_B1 ** ADAM_STEP)
        v_hat = v_ / (1.0 - ADAM_B2 ** ADAM_STEP)
        g_ref[...] = g
        d_ref[...] = -ADAM_LR * (m_hat / (jnp.sqrt(v_hat) + ADAM_EPS) + ADAM_WD * w_)
        nm_ref[...] = m_
        nv_ref[...] = v_

    blk = pl.BlockSpec((tm, cols), lambda i: (i, 0))
    return _call(
        body, name=name, grid=(rows // tm,),
        in_specs=[blk, blk, blk, gspec], out_specs=[blk] * 4,
        out_shape=[jax.ShapeDtypeStruct((rows, cols), f32)] * 4)(w, m, v, grad)


def _rope_tables(s):
    half = HEAD_DIM // 2
    inv_freq = 1.0 / (10000.0 ** (jnp.arange(half, dtype=f32) / half))
    ang = jnp.arange(s, dtype=f32)[:, None] * inv_freq[None, :]
    cos, sin = jnp.cos(ang), jnp.sin(ang)
    return jnp.tile(cos, (1, LANES // half)), jnp.tile(jnp.concatenate([-sin, sin], axis=1), (1, LANES // HEAD_DIM))


def _local_step(x, target, win_g, branch_cargo, ffn_cargo, ffn_exchange, b_gate, w_pool, pool_scale, ln1_g, ln1_b,
                convb_g, ln2_g, ln2_b):
    s = x.shape[0]
    nb = s // MOBA_BLOCK
    cos_t, sin_t = _rope_tables(s)
    x_bf = x.astype(bf16)

    z, (wba_g, wbp_g, wout_g) = _inproj(x_bf, win_g, cos_t, sin_t, b_gate, branch_cargo)
    wba = wba_g.transpose(1, 0, 2).reshape(D_ATTN, D_MODEL)
    wbp = wbp_g.transpose(1, 0, 2).reshape(D_POOL, D_MODEL)
    wout = wout_g.reshape(D_MODEL, D_MODEL)
    (o_bf, lse), (wg_g, wu_g, wd_g, convw_g) = _attn_fwd(z, nb, ffn_cargo)
    wd = wd_g.reshape(D_FF_PAD, D_MODEL)
    pooled_bf, mixed, pm_bf = _pool_fwd(z, w_pool, pool_scale)
    m_bf = _branch_merge(o_bf, pm_bf, z, wba, wbp)
    x1_bf, xhat1, rstd1 = _out_ln1(m_bf, wout, x, ln1_g, ln1_b)
    a, up, hh_bf = _ffn_up(x1_bf, wg_g, wu_g, convw_g, convb_g)
    loss, dh2, dh2_bf, d_ln2_g, d_ln2_b = _ffn_down_ln2_loss(hh_bf, wd, xhat1, ln1_g, ln1_b, ln2_g, ln2_b, target)

    da_bf, dup_bf, dconv = _ffn_act_bwd(dh2_bf, wd_g, a, up, convw_g, convb_g)
    d_wg = _matmul_tn(x1_bf, da_bf, N_CHIPS, "dw_ffn_gate")
    d_wu = _matmul_tn(x1_bf, dup_bf, N_CHIPS, "dw_ffn_up")
    d_wd = _matmul_tn(hh_bf, dh2_bf, 1, "dw_ffn_down").reshape(N_CHIPS, FF_PAD, D_MODEL)
    dh1, dh1_bf, d_ln1_g, d_ln1_b = _ffn_in_bwd_ln1(da_bf, dup_bf, wg_g, wu_g, dh2, xhat1, rstd1, ln1_g)
    d_wout = _matmul_tn(m_bf, dh1_bf, 1, "dw_out").reshape(N_CHIPS, D_MODEL // N_CHIPS, D_MODEL)
    dzg_bf, dya_bf, dyp_bf, do_bf, dpm, d_bgate = _merge_bwd(dh1_bf, wout, o_bf, pm_bf, z, wba, wbp)
    d_wba = _matmul_tn(o_bf, dya_bf, N_CHIPS, "dw_branch_attn")
    d_wbp = _matmul_tn(pm_bf, dyp_bf, N_CHIPS, "dw_branch_pool")
    du_bf, d_wpool, d_pscale = _pool_bwd(dpm, mixed, pooled_bf, w_pool, pool_scale)
    (dq_bf, dk_bf, dv_bf), brought = _attn_bwd(z, o_bf, lse, do_bf, cos_t, sin_t, nb,
                                               ffn_exchange([d_wg, d_wu, d_wd, dconv]))
    dz_bf = jnp.concatenate([dq_bf, dk_bf, dv_bf, du_bf, dzg_bf], axis=1)
    d_win = _matmul_tn(x_bf, dz_bf, N_CHIPS, "dw_in")
    grad_x = _inproj_dx(dz_bf, win_g, dh1)

    small = [d_bgate, d_wpool, d_pscale, d_ln1_g, d_ln1_b, d_ln2_g, d_ln2_b, dconv[:, 3, :]]
    return loss[0, 0], grad_x, [d_win, d_wba, d_wbp, d_wout], small, brought


SMALL_ROWS = 592


def _pack_small(parts):
    flat = jnp.concatenate([p.reshape(-1) for p in parts])
    return jnp.pad(flat, (0, SMALL_ROWS * LANES - flat.shape[0])).reshape(SMALL_ROWS, LANES)


def _unpack_small(packed, shapes):
    flat = packed.reshape(-1)
    out, off = [], 0
    for shp in shapes:
        n = 1
        for d in shp:
            n *= d
        out.append(flat[off:off + n].reshape(shp))
        off += n
    return out


def _pad_conv_b(cb):
    return jnp.pad(cb.reshape(N_CHIPS, FF_SHARD), ((0, 0), (0, FF_PAD - FF_SHARD)))


def kernel(x, w_in, b_gate, w_branch_attn, w_pool, pool_scale, w_branch_pool, w_out, ln1_g, ln1_b, w_ffn_gate, w_ffn_up, conv_w, conv_b, w_ffn_down, ln2_g, ln2_b, loss_target, m_w_in, m_b_gate, m_w_branch_attn, m_w_pool, m_pool_scale, m_w_branch_pool, m_w_out, m_ln1_g, m_ln1_b, m_w_ffn_gate, m_w_ffn_up, m_conv_w, m_conv_b, m_w_ffn_down, m_ln2_g, m_ln2_b, v_w_in, v_b_gate, v_w_branch_attn, v_w_pool, v_pool_scale, v_w_branch_pool, v_w_out, v_ln1_g, v_ln1_b, v_w_ffn_gate, v_w_ffn_up, v_conv_w, v_conv_b, v_w_ffn_down, v_ln2_g, v_ln2_b):
    pad_cols = lambda w: jnp.pad(w, ((0, 0), (0, FF_PAD - FF_SHARD)))
    shards = [w_in[0].astype(bf16), w_branch_attn[0].astype(bf16), w_branch_pool[0].astype(bf16),
              w_out[0].astype(bf16), pad_cols(w_ffn_gate[0].astype(bf16)), pad_cols(w_ffn_up[0].astype(bf16)),
              jnp.pad(w_ffn_down[0].astype(bf16), ((0, FF_PAD - FF_SHARD), (0, 0))),
              jnp.pad(conv_w[0], ((0, CONV_ROWS - 3), (0, FF_PAD - FF_SHARD)))]
    (win_g,) = _cargo_alone(_GatherCargo(shards[:1], [0.0]), "gather_w_in")
    branch_cargo = _GatherCargo(shards[1:4], [0.3, 0.4, 0.6])
    ffn_cargo = _GatherCargo(shards[4:], [0.35, 0.62, 0.88, 0.9])

    place = jnp.stack([2 * lax.axis_index("x") + lax.axis_index("y"), lax.axis_index("c")]).astype(jnp.int32)

    def chip_sums(grads, names, group):
        split = [_split_halves(g) for g in grads]
        got = _presum_swap(split, "presum_swap_" + group)
        return [_add_pair(a, b, place, "presum_" + nm) for a, b, nm in zip(split, got, names)]

    ffn_names, rest_names = ["w_ffn_gate", "w_ffn_up", "w_ffn_down", "conv_w"], ["w_in", "w_branch_attn", "w_branch_pool", "w_out", "small"]
    ffn_sums = []

    def ffn_exchange(grads):
        ffn_sums.extend(chip_sums(grads, ffn_names, "ffn"))
        return _ExchangeCargo(ffn_sums, [False] * 4)

    convb_g = _pad_conv_b(conv_b).reshape(N_CHIPS, 1, FF_PAD)
    loss, grad_x, rest, small, ffn_slots = _local_step(
        x[0], loss_target[0], win_g, branch_cargo, ffn_cargo, ffn_exchange, b_gate, w_pool[0], pool_scale, ln1_g, ln1_b,
        convb_g, ln2_g, ln2_b)
    loss = lax.psum(loss, ("x", "y", "c"))

    rest_whole = [False] * 4 + [True]
    rest_sums = chip_sums(rest + [_pack_small(small)[None]], rest_names, "rest")
    rest_slots = _cargo_alone(_ExchangeCargo(rest_sums, rest_whole), "exchange_rest")
    names = rest_names[:4] + ffn_names + rest_names[4:]
    sums, slots = rest_sums[:4] + ffn_sums + rest_sums[4:], rest_slots[:4] + ffn_slots + rest_slots[4:]
    whole = [False] * 8 + [True]
    grads = _sibling_fill([_sum_slots(s, r, place, w, "sum_" + nm) for s, r, w, nm in zip(sums, slots, whole, names)])

    weights = [w_in[0], w_branch_attn[0], w_branch_pool[0], w_out[0], w_ffn_gate[0], w_ffn_up[0], w_ffn_down[0], conv_w[0]]
    m_in = [m_w_in[0], m_w_branch_attn[0], m_w_branch_pool[0], m_w_out[0], m_w_ffn_gate[0], m_w_ffn_up[0], m_w_ffn_down[0], m_conv_w[0]]
    v_in = [v_w_in[0], v_w_branch_attn[0], v_w_branch_pool[0], v_w_out[0], v_w_ffn_gate[0], v_w_ffn_up[0], v_w_ffn_down[0], v_conv_w[0]]
    tiles = [256, 256, 256, 128, 256, 256, 88, 3]
    res = {}
    for i, nm in enumerate(names[:-1]):
        res[nm] = [r[None] for r in _adamw(weights[i], m_in[i], v_in[i], grads[i], "adamw_" + nm, tiles[i])]

    small_names = ["b_gate", "w_pool", "pool_scale", "ln1_g", "ln1_b", "ln2_g", "ln2_b", "conv_b"]
    small_w = [b_gate, w_pool, pool_scale, ln1_g, ln1_b, ln2_g, ln2_b, _pad_conv_b(conv_b)]
    small_m = [m_b_gate, m_w_pool, m_pool_scale, m_ln1_g, m_ln1_b, m_ln2_g, m_ln2_b, _pad_conv_b(m_conv_b)]
    small_v = [v_b_gate, v_w_pool, v_pool_scale, v_ln1_g, v_ln1_b, v_ln2_g, v_ln2_b, _pad_conv_b(v_conv_b)]
    packed = _adamw(_pack_small(small_w), _pack_small(small_m), _pack_small(small_v), grads[-1],
                    "adamw_small", SMALL_ROWS)
    shapes = [w.shape for w in small_w]
    for kind in range(4):
        for nm, val in zip(small_names, _unpack_small(packed[kind], shapes)):
            if nm == "conv_b":
                val = val[:, :FF_SHARD].reshape(1, 4 * FF_SHARD)
            res.setdefault(nm, [None] * 4)[kind] = val

    order = ["w_in", "b_gate", "w_branch_attn", "w_pool", "pool_scale", "w_branch_pool", "w_out", "ln1_g", "ln1_b",
             "w_ffn_gate", "w_ffn_up", "conv_w", "conv_b", "w_ffn_down", "ln2_g", "ln2_b"]
    outs = [loss, grad_x[None]]
    for kind in range(4):
        outs += [res[nm][kind] for nm in order]
    return tuple(outs)
```

```python
import functools

import jax
import jax.numpy as jnp
from jax import lax
from jax.experimental import pallas as pl
from jax.experimental.pallas import tpu as pltpu

D_MODEL = 1024
HEAD_DIM = 64
D_ATTN = 512
D_POOL = 512
MOBA_BLOCK = 256
MOBA_TOPK = 3
POOL_GROUP = 128
MAX_WINDOW = 16
FF_SHARD = 704
FF_PAD = 768
D_FF_PAD = 4 * FF_PAD
N_CHIPS = 4
LANES = 128
ALPHA = (2.0 * 1) ** 0.25
LN_EPS = 1e-5
NEG = -1e30
SCALE = HEAD_DIM ** -0.5
ADAM_LR, ADAM_B1, ADAM_B2, ADAM_EPS, ADAM_WD, ADAM_STEP = 0.001, 0.9, 0.999, 1e-08, 0.01, 10
VMEM_LIMIT = 56 * 1024 * 1024
MESH = pl.DeviceIdType.MESH

bf16 = jnp.bfloat16
f32 = jnp.float32


def _dot(a, b):
    return jnp.dot(a, b, preferred_element_type=f32)


def _dot_nt(a, b):
    return lax.dot_general(a, b, (((1,), (1,)), ((), ())), preferred_element_type=f32)


def _dot_tn(a, b):
    return lax.dot_general(a, b, (((0,), (0,)), ((), ())), preferred_element_type=f32)


def _call(body, *, name, grid, in_specs, out_specs, out_shape, scratch=()):
    return pl.pallas_call(
        body, name=name, grid=grid, in_specs=in_specs, out_specs=out_specs, out_shape=out_shape,
        scratch_shapes=list(scratch),
        compiler_params=pltpu.CompilerParams(dimension_semantics=("arbitrary",) * len(grid),
                                             vmem_limit_bytes=VMEM_LIMIT))


def _rot_half(t):
    lane = lax.broadcasted_iota(jnp.int32, t.shape, 1)
    first = (lane % HEAD_DIM) < (HEAD_DIM // 2)
    return jnp.where(first, pltpu.roll(t, LANES - HEAD_DIM // 2, 1), pltpu.roll(t, HEAD_DIM // 2, 1))


def _rope(t, cos, sin_signed):
    return t * cos + _rot_half(t) * sin_signed


def _rope_bwd(d, cos, sin_signed):
    return d * cos + _rot_half(d * sin_signed)


def _gelu_parts(a):
    cdf = 0.5 * (1.0 + lax.erf(a * (2.0 ** -0.5)))
    pdf = jnp.exp(-0.5 * a * a) * ((2.0 * jnp.pi) ** -0.5)
    return a * cdf, cdf + a * pdf


def _layer_norm(h, g, b):
    mu = jnp.mean(h, axis=-1, keepdims=True)
    xc = h - mu
    var = jnp.mean(xc * xc, axis=-1, keepdims=True)
    rstd = lax.rsqrt(var + LN_EPS)
    xhat = xc * rstd
    return xhat * g + b, xhat, rstd


def _layer_norm_bwd(dy, xhat, rstd, g):
    dxh = dy * g
    m1 = jnp.mean(dxh, axis=-1, keepdims=True)
    m2 = jnp.mean(dxh * xhat, axis=-1, keepdims=True)
    return rstd * (dxh - m1 - xhat * m2)


def _inproj(x_bf, win_g, cos_t, sin_t, b_gate, cargo):
    s = x_bf.shape[0]
    tm, tn = 512, 512

    def body(x_ref, w_ref, cos_ref, sin_ref, b_ref, o_ref):
        j = pl.program_id(1)
        acc = _dot(x_ref[...], w_ref[...])

        @pl.when(j < 2)
        def _():
            for c in range(tn // LANES):
                sl = slice(c * LANES, (c + 1) * LANES)
                o_ref[:, sl] = _rope(acc[:, sl], cos_ref[...], sin_ref[...])

        @pl.when((j >= 2) & (j < 4))
        def _():
            o_ref[...] = acc

        @pl.when(j >= 4)
        def _():
            o_ref[...] = jax.nn.sigmoid(acc + b_ref[...])

    (z,), carried = _cargo_call(
        body, cargo, name="inproj", grid=(s // tm, 4 * D_MODEL // tn),
        in_specs=[pl.BlockSpec((tm, D_MODEL), lambda i, j: (i, 0)),
                  pl.BlockSpec((None, D_MODEL, tn), lambda i, j: (j // 2, 0, j % 2)),
                  pl.BlockSpec((tm, LANES), lambda i, j: (i, 0)),
                  pl.BlockSpec((tm, LANES), lambda i, j: (i, 0)),
                  pl.BlockSpec((1, tn), lambda i, j: (0, jnp.maximum(j - 4, 0)))],
        out_specs=[pl.BlockSpec((tm, tn), lambda i, j: (i, j))],
        out_shape=[jax.ShapeDtypeStruct((s, 4 * D_MODEL), f32)])(x_bf, win_g, cos_t, sin_t, b_gate)
    return z, carried


def _moba_select(q, k_all, i, nb):
    lane = lax.broadcasted_iota(jnp.int32, q.shape, 1)
    k_mean = jnp.mean(k_all.reshape(nb, MOBA_BLOCK, LANES), axis=1).astype(bf16)
    n_io = lax.broadcasted_iota(jnp.int32, (MOBA_BLOCK, nb), 1)
    past = n_io < i
    sels = []
    for h in range(2):
        head = (lane < HEAD_DIM) if h == 0 else (lane >= HEAD_DIM)
        gate = _dot_nt(jnp.where(head, q, 0.0).astype(bf16), k_mean)
        g = jnp.where(past, gate, NEG)
        rank = jnp.zeros((MOBA_BLOCK, nb), f32)
        for m in range(nb):
            gm = g[:, m:m + 1]
            rank = rank + jnp.where((gm > g) | ((gm == g) & (m < n_io)), 1.0, 0.0)
        sels.append(jnp.where(past & (rank < MOBA_TOPK), 1.0, 0.0))
    return sels


def _attn_fwd(z, nb, cargo):
    s = z.shape[0]

    def body(q_ref, k_ref, v_ref, o_ref, lse_ref, m_s, l_s, acc_s):
        i = pl.program_id(1)
        lane = lax.broadcasted_iota(jnp.int32, (MOBA_BLOCK, LANES), 1)
        head0 = lane < HEAD_DIM
        heads = (head0, jnp.logical_not(head0))
        q = q_ref[...]
        sels = _moba_select(q, k_ref[...], i, nb)
        qs = q * SCALE
        qb = [jnp.where(hm, qs, 0.0).astype(bf16) for hm in heads]
        row = lax.broadcasted_iota(jnp.int32, (MOBA_BLOCK, MOBA_BLOCK), 0)
        col = lax.broadcasted_iota(jnp.int32, (MOBA_BLOCK, MOBA_BLOCK), 1)

        own = pl.ds(pl.multiple_of(i * MOBA_BLOCK, MOBA_BLOCK), MOBA_BLOCK)
        kj = k_ref[own, :].astype(bf16)
        vj = v_ref[own, :]
        ms, ls, pv = [], [], None
        for h in range(2):
            sc = jnp.where(col <= row, _dot_nt(qb[h], kj), NEG)
            m = jnp.max(sc, axis=1, keepdims=True)
            p = jnp.exp(sc - m)
            ms.append(m)
            ls.append(jnp.sum(p, axis=1, keepdims=True))
            t = _dot(p.astype(bf16), jnp.where(heads[h], vj, 0.0).astype(bf16))
            pv = t if pv is None else pv + t
        m_s[...] = jnp.where(head0, ms[0], ms[1])
        l_s[...] = jnp.where(head0, ls[0], ls[1])
        acc_s[...] = pv

        for j in range(nb - 1):
            @pl.when(j < i)
            def _(j=j):
                kj = k_ref[j * MOBA_BLOCK:(j + 1) * MOBA_BLOCK, :].astype(bf16)
                vj = v_ref[j * MOBA_BLOCK:(j + 1) * MOBA_BLOCK, :]
                m_old_l, l_old_l = m_s[...], l_s[...]
                m_new, alpha, l_new, pv = [], [], [], None
                for h in range(2):
                    c0 = h * HEAD_DIM
                    sc = jnp.where(sels[h][:, j:j + 1] > 0.0, _dot_nt(qb[h], kj), NEG)
                    m_old = m_old_l[:, c0:c0 + 1]
                    mn = jnp.maximum(m_old, jnp.max(sc, axis=1, keepdims=True))
                    a = jnp.exp(m_old - mn)
                    p = jnp.exp(sc - mn)
                    m_new.append(mn)
                    alpha.append(a)
                    l_new.append(a * l_old_l[:, c0:c0 + 1] + jnp.sum(p, axis=1, keepdims=True))
                    t = _dot(p.astype(bf16), jnp.where(heads[h], vj, 0.0).astype(bf16))
                    pv = t if pv is None else pv + t
                m_s[...] = jnp.where(head0, m_new[0], m_new[1])
                l_s[...] = jnp.where(head0, l_new[0], l_new[1])
                acc_s[...] = acc_s[...] * jnp.where(head0, alpha[0], alpha[1]) + pv

        o_ref[...] = (acc_s[...] / l_s[...]).astype(bf16)
        lse_ref[...] = m_s[...] + jnp.log(l_s[...])

    return _cargo_call(
        body, cargo, name="attn_fwd", grid=(D_ATTN // LANES, nb),
        in_specs=[pl.BlockSpec((MOBA_BLOCK, LANES), lambda hp, i: (i, hp)),
                  pl.BlockSpec((s, LANES), lambda hp, i: (0, 4 + hp)),
                  pl.BlockSpec((s, LANES), lambda hp, i: (0, 8 + hp))],
        out_specs=[pl.BlockSpec((MOBA_BLOCK, LANES), lambda hp, i: (i, hp)),
                   pl.BlockSpec((None, MOBA_BLOCK, LANES), lambda hp, i: (hp, i, 0))],
        out_shape=[jax.ShapeDtypeStruct((s, D_ATTN), bf16),
                   jax.ShapeDtypeStruct((D_ATTN // LANES, s, LANES), f32)],
        scratch=[pltpu.VMEM((MOBA_BLOCK, LANES), f32)] * 3)(z, z, z)


def _window_select(snaps, g):
    return jnp.where(g == 0, snaps[0], jnp.where(g == 1, snaps[1], jnp.where(g == 2, snaps[2], snaps[3])))


def _pool_fwd(z, w_pool, pool_scale):
    s = z.shape[0]

    def body(u_ref, w_ref, sc_ref, pooled_ref, mixed_ref, pm_ref, pad):
        g = pl.program_id(0)
        u = u_ref[...]
        pad[0:MAX_WINDOW, :] = jnp.zeros((MAX_WINDOW, POOL_GROUP), f32)
        pad[MAX_WINDOW:MAX_WINDOW + s, :] = u
        acc = u
        snaps = []
        for d in range(1, MAX_WINDOW):
            acc = acc + pad[MAX_WINDOW - d:MAX_WINDOW - d + s, :]
            if d + 1 in (2, 4, 8, 16):
                snaps.append(acc)
        win = _window_select(snaps, g)
        t = lax.broadcasted_iota(jnp.int32, (s, POOL_GROUP), 0)
        count = jnp.minimum(t + 1, jnp.left_shift(2, g)).astype(f32)
        pooled = (win / count - u).astype(bf16)
        mixed = _dot(pooled, w_ref[...].astype(bf16))
        pooled_ref[...] = pooled
        mixed_ref[...] = mixed
        pm_ref[...] = (mixed * sc_ref[...]).astype(bf16)

    blk = pl.BlockSpec((s, POOL_GROUP), lambda g: (0, g))
    return _call(
        body, name="pool_fwd", grid=(4,),
        in_specs=[pl.BlockSpec((s, POOL_GROUP), lambda g: (0, 12 + g)),
                  pl.BlockSpec((None, POOL_GROUP, POOL_GROUP), lambda g: (g, 0, 0)),
                  pl.BlockSpec((1, POOL_GROUP), lambda g: (0, g))],
        out_specs=[blk, blk, blk],
        out_shape=[jax.ShapeDtypeStruct((s, D_POOL), bf16), jax.ShapeDtypeStruct((s, D_POOL), f32),
                   jax.ShapeDtypeStruct((s, D_POOL), bf16)],
        scratch=[pltpu.VMEM((s + MAX_WINDOW, POOL_GROUP), f32)])(z, w_pool, pool_scale)


def _branch_merge(o_bf, pm_bf, z, wba, wbp):
    s = o_bf.shape[0]
    tm = 256

    def body(o_ref, pm_ref, ga_ref, gp_ref, wba_ref, wbp_ref, m_ref):
        ya = _dot(o_ref[...], wba_ref[...])
        yp = _dot(pm_ref[...], wbp_ref[...])
        m_ref[...] = (ga_ref[...] * ya + gp_ref[...] * yp).astype(bf16)

    full = lambda r, c: pl.BlockSpec((r, c), lambda i: (0, 0))
    return _call(
        body, name="branch_merge", grid=(s // tm,),
        in_specs=[pl.BlockSpec((tm, D_ATTN), lambda i: (i, 0)), pl.BlockSpec((tm, D_POOL), lambda i: (i, 0)),
                  pl.BlockSpec((tm, D_MODEL), lambda i: (i, 2)), pl.BlockSpec((tm, D_MODEL), lambda i: (i, 3)),
                  full(D_ATTN, D_MODEL), full(D_POOL, D_MODEL)],
        out_specs=pl.BlockSpec((tm, D_MODEL), lambda i: (i, 0)),
        out_shape=jax.ShapeDtypeStruct((s, D_MODEL), bf16))(o_bf, pm_bf, z, z, wba, wbp)


def _out_ln1(m_bf, wout, x, ln_g, ln_b):
    s = x.shape[0]
    tm = 256

    def body(m_ref, w_ref, x_ref, g_ref, b_ref, x1_ref, xhat_ref, rstd_ref):
        h = ALPHA * x_ref[...] + _dot(m_ref[...], w_ref[...])
        y, xhat, rstd = _layer_norm(h, g_ref[...], b_ref[...])
        x1_ref[...] = y.astype(bf16)
        xhat_ref[...] = xhat
        rstd_ref[...] = jnp.broadcast_to(rstd, (tm, LANES))

    row = pl.BlockSpec((tm, D_MODEL), lambda i: (i, 0))
    vec = pl.BlockSpec((1, D_MODEL), lambda i: (0, 0))
    return _call(
        body, name="out_ln1", grid=(s // tm,),
        in_specs=[row, pl.BlockSpec((D_MODEL, D_MODEL), lambda i: (0, 0)), row, vec, vec],
        out_specs=[row, row, pl.BlockSpec((tm, LANES), lambda i: (i, 0))],
        out_shape=[jax.ShapeDtypeStruct((s, D_MODEL), bf16), jax.ShapeDtypeStruct((s, D_MODEL), f32),
                   jax.ShapeDtypeStruct((s, LANES), f32)])(m_bf, wout, x, ln_g, ln_b)


FF_TILE = 256
FF_TILES_PER_SHARD = FF_PAD // FF_TILE
CONV_PAD = 8
CONV_ROWS = 16


def _ff_weight_spec(rows):
    return pl.BlockSpec((None, rows, FF_TILE), lambda j: (j // FF_TILES_PER_SHARD, 0, j % FF_TILES_PER_SHARD))


def _ff_rows_spec():
    return pl.BlockSpec((None, FF_TILE, D_MODEL), lambda j: (j // FF_TILES_PER_SHARD, j % FF_TILES_PER_SHARD, 0))


def _conv_taps(pad, s, cw_ref, off=0):
    return (pad[CONV_PAD + off:CONV_PAD + off + s, :] * cw_ref[2:3, :]
            + pad[CONV_PAD - 1 + off:CONV_PAD - 1 + off + s, :] * cw_ref[1:2, :]
            + pad[CONV_PAD - 2 + off:CONV_PAD - 2 + off + s, :] * cw_ref[0:1, :])


def _ffn_up(x1_bf, wg_g, wu_g, convw_g, convb_g):
    s = x1_bf.shape[0]

    def body(x_ref, wg_ref, wu_ref, cw_ref, cb_ref, a_ref, up_ref, hh_ref, pad):
        x1 = x_ref[...]
        a = _dot_nt(x1, wg_ref[...])
        up = _dot_nt(x1, wu_ref[...])
        a_ref[...] = a
        up_ref[...] = up
        pad[0:CONV_PAD, :] = jnp.zeros((CONV_PAD, FF_TILE), f32)
        pad[CONV_PAD:CONV_PAD + s, :] = a
        ac = _conv_taps(pad, s, cw_ref) + cb_ref[...]
        hg, _ = _gelu_parts(ac)
        hh_ref[...] = (hg * up).astype(bf16)

    col = pl.BlockSpec((s, FF_TILE), lambda j: (0, j))
    return _call(
        body, name="ffn_up", grid=(D_FF_PAD // FF_TILE,),
        in_specs=[pl.BlockSpec((s, D_MODEL), lambda j: (0, 0)), _ff_rows_spec(), _ff_rows_spec(),
                  _ff_weight_spec(8), _ff_weight_spec(1)],
        out_specs=[col, col, col],
        out_shape=[jax.ShapeDtypeStruct((s, D_FF_PAD), f32), jax.ShapeDtypeStruct((s, D_FF_PAD), f32),
                   jax.ShapeDtypeStruct((s, D_FF_PAD), bf16)],
        scratch=[pltpu.VMEM((s + CONV_PAD, FF_TILE), f32)])(x1_bf, wg_g, wu_g, convw_g, convb_g)


def _ffn_down_ln2_loss(hh_bf, wd, xhat1, ln1_g, ln1_b, ln2_g, ln2_b, target):
    s = hh_bf.shape[0]
    tm = 256

    def body(hh_ref, w_ref, xh1_ref, g1_ref, b1_ref, g2_ref, b2_ref, t_ref, loss_ref, dh_ref, dhb_ref, dg_ref, db_ref):
        i = pl.program_id(0)
        x1 = xh1_ref[...] * g1_ref[...] + b1_ref[...]
        h = ALPHA * x1 + _dot(hh_ref[...], w_ref[...])
        y, xhat, rstd = _layer_norm(h, g2_ref[...], b2_ref[...])
        err = y - t_ref[...]
        part = 0.5 * jnp.sum(jnp.mean(err * err, axis=-1, keepdims=True), axis=0, keepdims=True)
        dy = err * (1.0 / D_MODEL)

        @pl.when(i == 0)
        def _():
            loss_ref[...] = jnp.zeros_like(loss_ref)
            dg_ref[...] = jnp.zeros_like(dg_ref)
            db_ref[...] = jnp.zeros_like(db_ref)

        loss_ref[...] += jnp.broadcast_to(part, loss_ref.shape)
        dg_ref[...] += jnp.sum(dy * xhat, axis=0, keepdims=True)
        db_ref[...] += jnp.sum(dy, axis=0, keepdims=True)
        dh = _layer_norm_bwd(dy, xhat, rstd, g2_ref[...])
        dh_ref[...] = dh
        dhb_ref[...] = dh.astype(bf16)

    row = pl.BlockSpec((tm, D_MODEL), lambda i: (i, 0))
    vec = pl.BlockSpec((1, D_MODEL), lambda i: (0, 0))
    return _call(
        body, name="ffn_down_ln2_loss", grid=(s // tm,),
        in_specs=[pl.BlockSpec((tm, D_FF_PAD), lambda i: (i, 0)), pl.BlockSpec((D_FF_PAD, D_MODEL), lambda i: (0, 0)),
                  row, vec, vec, vec, vec, row],
        out_specs=[pl.BlockSpec((8, LANES), lambda i: (0, 0)), row, row, vec, vec],
        out_shape=[jax.ShapeDtypeStruct((8, LANES), f32), jax.ShapeDtypeStruct((s, D_MODEL), f32),
                   jax.ShapeDtypeStruct((s, D_MODEL), bf16), jax.ShapeDtypeStruct((1, D_MODEL), f32),
                   jax.ShapeDtypeStruct((1, D_MODEL), f32)])(hh_bf, wd, xhat1, ln1_g, ln1_b, ln2_g, ln2_b, target)


def _ffn_act_bwd(dh2_bf, wd_g, a, up, convw_g, convb_g):
    s = dh2_bf.shape[0]

    def body(dh_ref, wd_ref, a_ref, up_ref, cw_ref, cb_ref, da_ref, dup_ref, dc_ref, pad, dpad):
        dhh = _dot_nt(dh_ref[...], wd_ref[...])
        zeros = jnp.zeros((CONV_PAD, FF_TILE), f32)
        pad[0:CONV_PAD, :] = zeros
        pad[CONV_PAD:CONV_PAD + s, :] = a_ref[...]
        ac = _conv_taps(pad, s, cw_ref) + cb_ref[...]
        hg, dgelu = _gelu_parts(ac)
        up = up_ref[...]
        dup_ref[...] = (dhh * hg).astype(bf16)
        dac = dhh * up * dgelu
        rows = [jnp.sum(dac * pad[CONV_PAD - 2 + i:CONV_PAD - 2 + i + s, :], axis=0, keepdims=True) for i in range(3)]
        rows.append(jnp.sum(dac, axis=0, keepdims=True))
        rows.append(jnp.zeros((CONV_ROWS - 4, FF_TILE), f32))
        dc_ref[...] = jnp.concatenate(rows, axis=0)
        dpad[0:s, :] = dac
        dpad[s:s + CONV_PAD, :] = zeros
        da = (dpad[0:s, :] * cw_ref[2:3, :] + dpad[1:1 + s, :] * cw_ref[1:2, :] + dpad[2:2 + s, :] * cw_ref[0:1, :])
        da_ref[...] = da.astype(bf16)

    col = pl.BlockSpec((s, FF_TILE), lambda j: (0, j))
    return _call(
        body, name="ffn_act_bwd", grid=(D_FF_PAD // FF_TILE,),
        in_specs=[pl.BlockSpec((s, D_MODEL), lambda j: (0, 0)), _ff_rows_spec(),
                  col, col, _ff_weight_spec(8), _ff_weight_spec(1)],
        out_specs=[col, col, _ff_weight_spec(CONV_ROWS)],
        out_shape=[jax.ShapeDtypeStruct((s, D_FF_PAD), bf16), jax.ShapeDtypeStruct((s, D_FF_PAD), bf16),
                   jax.ShapeDtypeStruct((N_CHIPS, CONV_ROWS, FF_PAD), f32)],
        scratch=[pltpu.VMEM((s + CONV_PAD, FF_TILE), f32), pltpu.VMEM((s + CONV_PAD, FF_TILE), f32)],
    )(dh2_bf, wd_g, a, up, convw_g, convb_g)


def _matmul_tn(a, b, n_shards, name, tm=512, tn=256):
    k, m = a.shape
    n = b.shape[1]
    tm, tn = min(tm, m), min(tn, n // n_shards)
    per = n // n_shards // tn

    def body(a_ref, b_ref, o_ref):
        o_ref[...] = _dot_tn(a_ref[...], b_ref[...]).astype(bf16)

    return _call(
        body, name=name, grid=(m // tm, n // tn),
        in_specs=[pl.BlockSpec((k, tm), lambda i, j: (0, i)), pl.BlockSpec((k, tn), lambda i, j: (0, j))],
        out_specs=pl.BlockSpec((None, tm, tn), lambda i, j: (j // per, i, j % per)),
        out_shape=jax.ShapeDtypeStruct((n_shards, m, n // n_shards), bf16))(a, b)


def _ffn_in_bwd_ln1(da_bf, dup_bf, wg_g, wu_g, dh2, xhat1, rstd1, ln1_g):
    s = da_bf.shape[0]
    tm = 256

    def body(da_ref, dup_ref, wg_ref, wu_ref, dh2_ref, xh_ref, rstd_ref, g_ref, dh_ref, dhb_ref, dg_ref, db_ref):
        i = pl.program_id(0)
        dx1 = ALPHA * dh2_ref[...]
        for sh in range(N_CHIPS):
            sl = slice(sh * FF_PAD, (sh + 1) * FF_PAD)
            dx1 = dx1 + _dot(da_ref[:, sl], wg_ref[sh]) + _dot(dup_ref[:, sl], wu_ref[sh])
        xhat = xh_ref[...]

        @pl.when(i == 0)
        def _():
            dg_ref[...] = jnp.zeros_like(dg_ref)
            db_ref[...] = jnp.zeros_like(db_ref)

        dg_ref[...] += jnp.sum(dx1 * xhat, axis=0, keepdims=True)
        db_ref[...] += jnp.sum(dx1, axis=0, keepdims=True)
        dh = _layer_norm_bwd(dx1, xhat, rstd_ref[:, 0:1], g_ref[...])
        dh_ref[...] = dh
        dhb_ref[...] = dh.astype(bf16)

    row = pl.BlockSpec((tm, D_MODEL), lambda i: (i, 0))
    wide = pl.BlockSpec((tm, D_FF_PAD), lambda i: (i, 0))
    wfull = pl.BlockSpec((N_CHIPS, FF_PAD, D_MODEL), lambda i: (0, 0, 0))
    vec = pl.BlockSpec((1, D_MODEL), lambda i: (0, 0))
    return _call(
        body, name="ffn_in_bwd_ln1", grid=(s // tm,),
        in_specs=[wide, wide, wfull, wfull, row, row, pl.BlockSpec((tm, LANES), lambda i: (i, 0)), vec],
        out_specs=[row, row, vec, vec],
        out_shape=[jax.ShapeDtypeStruct((s, D_MODEL), f32), jax.ShapeDtypeStruct((s, D_MODEL), bf16),
                   jax.ShapeDtypeStruct((1, D_MODEL), f32), jax.ShapeDtypeStruct((1, D_MODEL), f32)],
    )(da_bf, dup_bf, wg_g, wu_g, dh2, xhat1, rstd1, ln1_g)


def _merge_bwd(dh1_bf, wout, o_bf, pm_bf, z, wba, wbp):
    s = dh1_bf.shape[0]
    tm = 256

    def body(dh_ref, wout_ref, o_ref, pm_ref, ga_ref, gp_ref, wba_ref, wbp_ref,
             dzg_ref, dya_ref, dyp_ref, do_ref, dpm_ref, dbg_ref):
        i = pl.program_id(0)
        dm = _dot_nt(dh_ref[...], wout_ref[...])
        ya = _dot(o_ref[...], wba_ref[...])
        yp = _dot(pm_ref[...], wbp_ref[...])
        ga, gp = ga_ref[...], gp_ref[...]
        dza = dm * ya * ga * (1.0 - ga)
        dzp = dm * yp * gp * (1.0 - gp)

        @pl.when(i == 0)
        def _():
            dbg_ref[...] = jnp.zeros_like(dbg_ref)

        dbg_ref[:, 0:D_MODEL] += jnp.sum(dza, axis=0, keepdims=True)
        dbg_ref[:, D_MODEL:2 * D_MODEL] += jnp.sum(dzp, axis=0, keepdims=True)
        dzg_ref[:, 0:D_MODEL] = dza.astype(bf16)
        dzg_ref[:, D_MODEL:2 * D_MODEL] = dzp.astype(bf16)
        dya = (dm * ga).astype(bf16)
        dyp = (dm * gp).astype(bf16)
        dya_ref[...] = dya
        dyp_ref[...] = dyp
        do_ref[...] = _dot_nt(dya, wba_ref[...]).astype(bf16)
        dpm_ref[...] = _dot_nt(dyp, wbp_ref[...])

    row = pl.BlockSpec((tm, D_MODEL), lambda i: (i, 0))
    half = pl.BlockSpec((tm, D_ATTN), lambda i: (i, 0))
    full = lambda r, c: pl.BlockSpec((r, c), lambda i: (0, 0))
    return _call(
        body, name="merge_bwd", grid=(s // tm,),
        in_specs=[row, full(D_MODEL, D_MODEL), half, half,
                  pl.BlockSpec((tm, D_MODEL), lambda i: (i, 2)), pl.BlockSpec((tm, D_MODEL), lambda i: (i, 3)),
                  full(D_ATTN, D_MODEL), full(D_POOL, D_MODEL)],
        out_specs=[pl.BlockSpec((tm, 2 * D_MODEL), lambda i: (i, 0)), row, row, half, half,
                   pl.BlockSpec((1, 2 * D_MODEL), lambda i: (0, 0))],
        out_shape=[jax.ShapeDtypeStruct((s, 2 * D_MODEL), bf16), jax.ShapeDtypeStruct((s, D_MODEL), bf16),
                   jax.ShapeDtypeStruct((s, D_MODEL), bf16), jax.ShapeDtypeStruct((s, D_ATTN), bf16),
                   jax.ShapeDtypeStruct((s, D_POOL), f32), jax.ShapeDtypeStruct((1, 2 * D_MODEL), f32)],
    )(dh1_bf, wout, o_bf, pm_bf, z, z, wba, wbp)


def _pool_bwd(dpm, mixed, pooled_bf, w_pool, pool_scale):
    s = dpm.shape[0]

    def body(dpm_ref, mixed_ref, pooled_ref, w_ref, sc_ref, du_ref, dw_ref, dsc_ref, pad):
        g = pl.program_id(0)
        dpm_v = dpm_ref[...]
        dsc_ref[...] = jnp.sum(dpm_v * mixed_ref[...], axis=0, keepdims=True)
        dmixed = (dpm_v * sc_ref[...]).astype(bf16)
        dw_ref[...] = _dot_tn(pooled_ref[...], dmixed)
        dpooled = _dot_nt(dmixed, w_ref[...].astype(bf16))
        t = lax.broadcasted_iota(jnp.int32, (s, POOL_GROUP), 0)
        count = jnp.minimum(t + 1, jnp.left_shift(2, g)).astype(f32)
        r = dpooled / count
        pad[0:s, :] = r
        pad[s:s + MAX_WINDOW, :] = jnp.zeros((MAX_WINDOW, POOL_GROUP), f32)
        acc = r
        snaps = []
        for d in range(1, MAX_WINDOW):
            acc = acc + pad[d:d + s, :]
            if d + 1 in (2, 4, 8, 16):
                snaps.append(acc)
        du_ref[...] = (_window_select(snaps, g) - dpooled).astype(bf16)

    blk = pl.BlockSpec((s, POOL_GROUP), lambda g: (0, g))
    return _call(
        body, name="pool_bwd", grid=(4,),
        in_specs=[blk, blk, blk, pl.BlockSpec((None, POOL_GROUP, POOL_GROUP), lambda g: (g, 0, 0)),
                  pl.BlockSpec((1, POOL_GROUP), lambda g: (0, g))],
        out_specs=[blk, pl.BlockSpec((None, POOL_GROUP, POOL_GROUP), lambda g: (g, 0, 0)),
                   pl.BlockSpec((1, POOL_GROUP), lambda g: (0, g))],
        out_shape=[jax.ShapeDtypeStruct((s, D_POOL), bf16), jax.ShapeDtypeStruct((4, POOL_GROUP, POOL_GROUP), f32),
                   jax.ShapeDtypeStruct((1, D_POOL), f32)],
        scratch=[pltpu.VMEM((s + MAX_WINDOW, POOL_GROUP), f32)])(dpm, mixed, pooled_bf, w_pool, pool_scale)


def _attn_bwd(z, o_bf, lse, do_bf, cos_t, sin_t, nb, cargo):
    s = z.shape[0]

    def body(q_ref, k_ref, v_ref, o_ref, lse_ref, do_ref, cq_ref, sq_ref, cf_ref, sf_ref,
             dq_ref, dk_ref, dv_ref, dq_s, dk_s, dv_s):
        i = pl.program_id(1)
        lane = lax.broadcasted_iota(jnp.int32, (MOBA_BLOCK, LANES), 1)
        head0 = lane < HEAD_DIM
        heads = (head0, jnp.logical_not(head0))

        @pl.when(i == 0)
        def _():
            dk_s[...] = jnp.zeros_like(dk_s)
            dv_s[...] = jnp.zeros_like(dv_s)

        q = q_ref[...]
        sels = _moba_select(q, k_ref[...], i, nb)
        qs = q * SCALE
        qb = [jnp.where(hm, qs, 0.0).astype(bf16) for hm in heads]
        qraw = [jnp.where(hm, q, 0.0).astype(bf16) for hm in heads]
        do = do_ref[...].astype(f32)
        prod = do * o_ref[...].astype(f32)
        delta = [jnp.sum(jnp.where(hm, prod, 0.0), axis=1, keepdims=True) for hm in heads]
        dob = [jnp.where(hm, do, 0.0).astype(bf16) for hm in heads]
        lse_l = lse_ref[...]
        lse_h = [lse_l[:, 0:1], lse_l[:, HEAD_DIM:HEAD_DIM + 1]]
        row = lax.broadcasted_iota(jnp.int32, (MOBA_BLOCK, MOBA_BLOCK), 0)
        col = lax.broadcasted_iota(jnp.int32, (MOBA_BLOCK, MOBA_BLOCK), 1)
        dq_s[...] = jnp.zeros_like(dq_s)

        def block(rows, allow):
            kj = k_ref[rows, :].astype(bf16)
            vj = v_ref[rows, :].astype(bf16)
            dq, dk, dv = None, None, None
            for h in range(2):
                p = jnp.where(allow[h], jnp.exp(_dot_nt(qb[h], kj) - lse_h[h]), 0.0)
                dp = _dot_nt(dob[h], vj)
                ds = (p * (dp - delta[h]) * SCALE).astype(bf16)
                tv = _dot_tn(p.astype(bf16), dob[h])
                tq = _dot(ds, jnp.where(heads[h], kj, jnp.zeros_like(kj)))
                tk = _dot_tn(ds, qraw[h])
                dq, dk, dv = (tq, tk, tv) if dq is None else (dq + tq, dk + tk, dv + tv)
            dq_s[...] += dq
            dk_s[rows, :] += dk
            dv_s[rows, :] += dv

        block(pl.ds(pl.multiple_of(i * MOBA_BLOCK, MOBA_BLOCK), MOBA_BLOCK), [col <= row, col <= row])
        for j in range(nb - 1):
            @pl.when(j < i)
            def _(j=j):
                block(slice(j * MOBA_BLOCK, (j + 1) * MOBA_BLOCK), [sels[h][:, j:j + 1] > 0.0 for h in range(2)])

        dq_ref[...] = _rope_bwd(dq_s[...], cq_ref[...], sq_ref[...]).astype(bf16)

        @pl.when(i == nb - 1)
        def _():
            dk_ref[...] = _rope_bwd(dk_s[...], cf_ref[...], sf_ref[...]).astype(bf16)
            dv_ref[...] = dv_s[...].astype(bf16)

    qblk = pl.BlockSpec((MOBA_BLOCK, LANES), lambda hp, i: (i, hp))
    tq = pl.BlockSpec((MOBA_BLOCK, LANES), lambda hp, i: (i, 0))
    tf = pl.BlockSpec((s, LANES), lambda hp, i: (0, 0))
    colblk = pl.BlockSpec((s, LANES), lambda hp, i: (0, hp))
    return _cargo_call(
        body, cargo, name="attn_bwd", grid=(D_ATTN // LANES, nb),
        in_specs=[qblk, pl.BlockSpec((s, LANES), lambda hp, i: (0, 4 + hp)),
                  pl.BlockSpec((s, LANES), lambda hp, i: (0, 8 + hp)), qblk,
                  pl.BlockSpec((None, MOBA_BLOCK, LANES), lambda hp, i: (hp, i, 0)), qblk, tq, tq, tf, tf],
        out_specs=[qblk, colblk, colblk],
        out_shape=[jax.ShapeDtypeStruct((s, D_ATTN), bf16)] * 3,
        scratch=[pltpu.VMEM((MOBA_BLOCK, LANES), f32), pltpu.VMEM((s, LANES), f32), pltpu.VMEM((s, LANES), f32)],
    )(z, z, z, o_bf, lse, do_bf, cos_t, sin_t, cos_t, sin_t)


def _inproj_dx(dz_bf, win_g, dh1):
    s = dz_bf.shape[0]
    tm = 256

    def body(dz_ref, w_ref, dh_ref, gx_ref):
        acc = ALPHA * dh_ref[...]
        for sh in range(N_CHIPS):
            acc = acc + _dot_nt(dz_ref[:, sh * D_MODEL:(sh + 1) * D_MODEL], w_ref[sh])
        gx_ref[...] = acc

    row = pl.BlockSpec((tm, D_MODEL), lambda i: (i, 0))
    return _call(
        body, name="inproj_dx", grid=(s // tm,),
        in_specs=[pl.BlockSpec((tm, 4 * D_MODEL), lambda i: (i, 0)),
                  pl.BlockSpec((N_CHIPS, D_MODEL, D_MODEL), lambda i: (0, 0, 0)), row],
        out_specs=row, out_shape=jax.ShapeDtypeStruct((s, D_MODEL), f32))(dz_bf, win_g, dh1)


ANY = pl.BlockSpec(memory_space=pl.ANY)


def _chip_index():
    return 2 * lax.axis_index("x") + lax.axis_index("y")


def _peer(k):
    x, y, c = lax.axis_index("x"), lax.axis_index("y"), lax.axis_index("c")
    return (x ^ (k >> 1), y ^ (k & 1), c)


def _sibling():
    return (lax.axis_index("x"), lax.axis_index("y"), 1 - lax.axis_index("c"))


class _GatherCargo:
    def __init__(self, shards, pass_on):
        self.shards, self.pass_on, n = shards, pass_on, len(shards)
        self.inputs = [a.reshape(2, a.shape[0] // 2, a.shape[1]) for a in shards]
        self.out_shape = [jax.ShapeDtypeStruct((N_CHIPS,) + a.shape, a.dtype) for a in self.inputs]
        sem, one = pltpu.SemaphoreType.DMA((N_CHIPS - 1, n)), pltpu.SemaphoreType.DMA((n,))
        self.sems = [sem, sem, sem, sem, one, one]

    def _copies(self, src, dst, sems, a):
        send_sems, recv_sems, fsend_sems, frecv_sems, local_sems, own_sems = sems
        p, c = _chip_index(), lax.axis_index("c")
        own = pltpu.make_async_remote_copy(src_ref=src[a], dst_ref=dst[a].at[p], send_sem=local_sems.at[a],
                                           recv_sem=own_sems.at[a], device_id=_sibling(), device_id_type=MESH)
        out, arrive, onward, landed = [], [], [], []
        for k in range(1, N_CHIPS):
            ici = dict(send_sem=send_sems.at[k - 1, a], recv_sem=recv_sems.at[k - 1, a], device_id=_peer(k),
                       device_id_type=MESH)
            d2d = dict(send_sem=fsend_sems.at[k - 1, a], recv_sem=frecv_sems.at[k - 1, a], device_id=_sibling(),
                       device_id_type=MESH)
            got, theirs = dst[a].at[p ^ k, c], dst[a].at[p ^ k, 1 - c]
            out.append(pltpu.make_async_remote_copy(src_ref=src[a].at[c], dst_ref=dst[a].at[p, c], **ici))
            arrive.append(pltpu.make_async_remote_copy(src_ref=src[a].at[c], dst_ref=got, **ici))
            onward.append(pltpu.make_async_remote_copy(src_ref=got, dst_ref=got, **d2d))
            landed.append(pltpu.make_async_remote_copy(src_ref=theirs, dst_ref=theirs, **d2d))
        return own, out, arrive, onward, landed

    def stages(self, steps):
        n = len(self.shards)

        def start(src, dst, sems):
            for a in range(n):
                own, out, _, _, _ = self._copies(src, dst, sems, a)
                own.start()
                for cp in out:
                    cp.start()

        def pass_on(a):
            def act(src, dst, sems):
                _, _, arrive, onward, _ = self._copies(src, dst, sems, a)
                for k in range(N_CHIPS - 1):
                    arrive[k].wait_recv()
                    onward[k].start()
            return act

        def finish(src, dst, sems):
            for a in range(n):
                own, out, _, onward, landed = self._copies(src, dst, sems, a)
                for cp in landed:
                    cp.wait_recv()
                for cp in out + onward:
                    cp.wait_send()
                own.wait()

        mids = [(min(steps - 1, int(self.pass_on[a] * steps)), pass_on(a)) for a in range(n)]
        return [(0, start)] + mids + [(steps - 1, finish)]

    def results(self, outs):
        return [o.reshape((N_CHIPS,) + a.shape) for o, a in zip(outs, self.shards)]


class _ExchangeCargo:
    def __init__(self, sums, whole):
        self.inputs, self.whole, n = sums, whole, len(sums)
        self.out_shape = [jax.ShapeDtypeStruct((N_CHIPS - 1,) + g.shape[1:], g.dtype) for g in sums]
        sem = pltpu.SemaphoreType.DMA((N_CHIPS - 1, n))
        self.sems = [sem, sem]

    def _copies(self, src, dst, sems):
        send_sems, recv_sems = sems
        p = _chip_index()
        return [pltpu.make_async_remote_copy(
            src_ref=src[a].at[0] if self.whole[a] else src[a].at[p ^ k], dst_ref=dst[a].at[k - 1],
            send_sem=send_sems.at[k - 1, a], recv_sem=recv_sems.at[k - 1, a], device_id=_peer(k), device_id_type=MESH)
            for k in range(1, N_CHIPS) for a in range(len(self.inputs))]

    def stages(self, steps):
        def start(src, dst, sems):
            for cp in self._copies(src, dst, sems):
                cp.start()

        def finish(src, dst, sems):
            copies = self._copies(src, dst, sems)
            for cp in copies:
                cp.wait_recv()
            for cp in copies:
                cp.wait_send()

        return [(0, start), (steps - 1, finish)]

    def results(self, outs):
        return list(outs)


def _cargo_call(body, cargo, *, name, grid, in_specs, out_specs, out_shape, scratch=()):
    n_in, n_out, n_scr = len(in_specs), len(out_specs), len(scratch)
    c_in, c_out = len(cargo.inputs), len(cargo.out_shape)
    steps = 1
    for g in grid:
        steps *= g
    stages = cargo.stages(steps)

    def wrapped(*refs):
        ins, refs = refs[:n_in], refs[n_in:]
        cin, refs = refs[:c_in], refs[c_in:]
        outs, refs = refs[:n_out], refs[n_out:]
        cout, refs = refs[:c_out], refs[c_out:]
        scr, sems = refs[:n_scr], refs[n_scr:]
        if not grid:
            for _, act in stages:
                act(cin, cout, sems)
            return
        step = 0
        for d in range(len(grid)):
            step = step * grid[d] + pl.program_id(d)
        pl.when(step == 0)(functools.partial(stages[0][1], cin, cout, sems))
        body(*ins, *outs, *scr)
        for at, act in stages[1:]:
            pl.when(step == at)(functools.partial(act, cin, cout, sems))

    params = dict(vmem_limit_bytes=VMEM_LIMIT)
    if grid:
        params["dimension_semantics"] = ("arbitrary",) * len(grid)
    res = pl.pallas_call(
        wrapped, name=name, grid=grid, in_specs=list(in_specs) + [ANY] * c_in, out_specs=list(out_specs) + [ANY] * c_out,
        out_shape=list(out_shape) + cargo.out_shape, scratch_shapes=list(scratch) + cargo.sems,
        compiler_params=pltpu.CompilerParams(**params))
    return lambda *args: (lambda r: (r[:n_out], cargo.results(r[n_out:])))(res(*args, *cargo.inputs))


def _split_halves(g):
    return g.reshape(g.shape[0], 2, g.shape[1] // 2, g.shape[2])


def _ew_tile(rows):
    return 256 if rows % 256 == 0 else rows


def _prefetch_call(body, *, name, grid, in_specs, out_specs, out_shape, **kw):
    return pl.pallas_call(
        body, name=name, out_shape=out_shape,
        grid_spec=pltpu.PrefetchScalarGridSpec(num_scalar_prefetch=1, grid=grid, in_specs=in_specs, out_specs=out_specs),
        compiler_params=pltpu.CompilerParams(dimension_semantics=("arbitrary",) * len(grid),
                                             vmem_limit_bytes=VMEM_LIMIT), **kw)


def _presum_swap(split, name):
    n = len(split)

    def body(*refs):
        src, got = refs[:n], refs[n:2 * n]
        send_sems, recv_sems = refs[2 * n:]
        c = lax.axis_index("c")
        remote = [pltpu.make_async_remote_copy(src_ref=src[a].at[:, 1 - c], dst_ref=got[a],
                                               send_sem=send_sems.at[a], recv_sem=recv_sems.at[a],
                                               device_id=_sibling(), device_id_type=MESH) for a in range(n)]
        for cp in remote:
            cp.start()
        for cp in remote:
            cp.wait_recv()
        for cp in remote:
            cp.wait_send()

    sem = pltpu.SemaphoreType.DMA((n,))
    return pl.pallas_call(
        body, name=name, in_specs=[ANY] * n, out_specs=[ANY] * n,
        out_shape=[jax.ShapeDtypeStruct((g.shape[0],) + g.shape[2:], g.dtype) for g in split],
        scratch_shapes=[sem, sem])(*split)


def _add_pair(mine, theirs, place, name):
    lead, _, rows, cols = mine.shape
    tm = _ew_tile(rows)

    def body(place_ref, a_ref, b_ref, o_ref):
        o_ref[...] = (a_ref[...].astype(f32) + b_ref[...].astype(f32)).astype(o_ref.dtype)

    blk = pl.BlockSpec((lead, tm, cols), lambda i, pc: (0, i, 0))
    return _prefetch_call(
        body, name=name, grid=(rows // tm,),
        in_specs=[pl.BlockSpec((lead, None, tm, cols), lambda i, pc: (0, pc[1], i, 0)), blk], out_specs=blk,
        out_shape=jax.ShapeDtypeStruct(theirs.shape, theirs.dtype))(place, mine, theirs)


class _NoCargo:
    def __init__(self, results=()):
        self.inputs, self.out_shape, self.sems, self._results = [], [], [], list(results)

    def stages(self, steps):
        return [(0, lambda src, dst, sems: None), (steps - 1, lambda src, dst, sems: None)]

    def results(self, outs):
        return self._results


def _cargo_alone(cargo, name):
    return _cargo_call(None, cargo, name=name, grid=(), in_specs=[], out_specs=[], out_shape=[])()[1]


def _sum_slots(own, recv, place, whole, name):
    _, rows, cols = own.shape
    tm = _ew_tile(rows)

    def body(place_ref, own_ref, r_ref, o_ref):
        r0, r1, r2 = [r_ref[k].astype(f32) for k in range(N_CHIPS - 1)]
        o_ref[...] = (own_ref[...].astype(f32) + r0) + (r1 + r2)

    return _prefetch_call(
        body, name=name, grid=(rows // tm,),
        in_specs=[pl.BlockSpec((None, tm, cols), lambda i, pc: (0 if whole else pc[0], i, 0)),
                  pl.BlockSpec((N_CHIPS - 1, tm, cols), lambda i, pc: (0, i, 0))],
        out_specs=pl.BlockSpec((None, tm, cols), lambda i, pc: (pc[1], i, 0)),
        out_shape=jax.ShapeDtypeStruct((2, rows, cols), f32))(place, own, recv)


def _sibling_fill(bufs):
    n = len(bufs)

    def body(*refs):
        src, dst = refs[:n], refs[n:2 * n]
        send_sems, recv_sems = refs[2 * n:]
        c = lax.axis_index("c")
        remote = [pltpu.make_async_remote_copy(src_ref=src[a].at[c], dst_ref=dst[a].at[c], send_sem=send_sems.at[a],
                                               recv_sem=recv_sems.at[a], device_id=_sibling(), device_id_type=MESH)
                  for a in range(n)]
        for cp in remote:
            cp.start()
        for a in range(n):
            pltpu.make_async_remote_copy(src_ref=src[a].at[c], dst_ref=dst[a].at[1 - c], send_sem=send_sems.at[a],
                                         recv_sem=recv_sems.at[a], device_id=_sibling(), device_id_type=MESH).wait_recv()
        for cp in remote:
            cp.wait_send()

    sem = pltpu.SemaphoreType.DMA((n,))
    out = pl.pallas_call(
        body, name="sibling_fill", in_specs=[ANY] * n, out_specs=[ANY] * n,
        out_shape=[jax.ShapeDtypeStruct(b.shape, b.dtype) for b in bufs],
        input_output_aliases={a: a for a in range(n)}, scratch_shapes=[sem, sem])(*bufs)
    return [o.reshape(2 * b.shape[1], b.shape[2]) for o, b in zip(out, bufs)]


def _adamw(w, m, v, grad, name, tm):
    rows, cols = w.shape
    gcols = grad.shape[1]
    if tm == rows:
        gspec = pl.BlockSpec(grad.shape, lambda i: (0, 0))
    else:
        gspec = pl.BlockSpec((tm, gcols), lambda i: (i, 0))

    def body(w_ref, m_ref, v_ref, gin_ref, g_ref, d_ref, nm_ref, nv_ref):
        g = gin_ref[0:tm, 0:cols]
        w_, m_, v_ = w_ref[...], m_ref[...], v_ref[...]
        m_ = ADAM_B1 * m_ + (1.0 - ADAM_B1) * g
        v_ = ADAM_B2 * v_ + (1.0 - ADAM_B2) * jnp.square(g)
        m_hat = m_ / (1.0 - ADAM_B1 ** ADAM_STEP)
        v_hat = v_ / (1.0 - ADAM_B2 ** ADAM_STEP)
        g_ref[...] = g
        d_ref[...] = -ADAM_LR * (m_hat / (jnp.sqrt(v_hat) + ADAM_EPS) + ADAM_WD * w_)
        nm_ref[...] = m_
        nv_ref[...] = v_

    blk = pl.BlockSpec((tm, cols), lambda i: (i, 0))
    return _call(
        body, name=name, grid=(rows // tm,),
        in_specs=[blk, blk, blk, gspec], out_specs=[blk] * 4,
        out_shape=[jax.ShapeDtypeStruct((rows, cols), f32)] * 4)(w, m, v, grad)


def _rope_tables(s):
    half = HEAD_DIM // 2
    inv_freq = 1.0 / (10000.0 ** (jnp.arange(half, dtype=f32) / half))
    ang = jnp.arange(s, dtype=f32)[:, None] * inv_freq[None, :]
    cos, sin = jnp.cos(ang), jnp.sin(ang)
    return jnp.tile(cos, (1, LANES // half)), jnp.tile(jnp.concatenate([-sin, sin], axis=1), (1, LANES // HEAD_DIM))


def _local_step(x, target, win_g, branch_cargo, ffn_cargo, ffn_exchange, b_gate, w_pool, pool_scale, ln1_g, ln1_b,
                convb_g, ln2_g, ln2_b):
    s = x.shape[0]
    nb = s // MOBA_BLOCK
    cos_t, sin_t = _rope_tables(s)
    x_bf = x.astype(bf16)

    z, (wba_g, wbp_g, wout_g) = _inproj(x_bf, win_g, cos_t, sin_t, b_gate, branch_cargo)
    wba = wba_g.transpose(1, 0, 2).reshape(D_ATTN, D_MODEL)
    wbp = wbp_g.transpose(1, 0, 2).reshape(D_POOL, D_MODEL)
    wout = wout_g.reshape(D_MODEL, D_MODEL)
    (o_bf, lse), (wg_g, wu_g, wd_g, convw_g) = _attn_fwd(z, nb, ffn_cargo)
    wd = wd_g.reshape(D_FF_PAD, D_MODEL)
    pooled_bf, mixed, pm_bf = _pool_fwd(z, w_pool, pool_scale)
    m_bf = _branch_merge(o_bf, pm_bf, z, wba, wbp)
    x1_bf, xhat1, rstd1 = _out_ln1(m_bf, wout, x, ln1_g, ln1_b)
    a, up, hh_bf = _ffn_up(x1_bf, wg_g, wu_g, convw_g, convb_g)
    loss, dh2, dh2_bf, d_ln2_g, d_ln2_b = _ffn_down_ln2_loss(hh_bf, wd, xhat1, ln1_g, ln1_b, ln2_g, ln2_b, target)

    da_bf, dup_bf, dconv = _ffn_act_bwd(dh2_bf, wd_g, a, up, convw_g, convb_g)
    d_wg = _matmul_tn(da_bf, x1_bf, 1, "dw_ffn_gate").reshape(N_CHIPS, FF_PAD, D_MODEL)
    d_wu = _matmul_tn(dup_bf, x1_bf, 1, "dw_ffn_up").reshape(N_CHIPS, FF_PAD, D_MODEL)
    d_wd = _matmul_tn(hh_bf, dh2_bf, 1, "dw_ffn_down").reshape(N_CHIPS, FF_PAD, D_MODEL)
    dh1, dh1_bf, d_ln1_g, d_ln1_b = _ffn_in_bwd_ln1(da_bf, dup_bf, wg_g, wu_g, dh2, xhat1, rstd1, ln1_g)
    d_wout = _matmul_tn(m_bf, dh1_bf, 1, "dw_out").reshape(N_CHIPS, D_MODEL // N_CHIPS, D_MODEL)
    dzg_bf, dya_bf, dyp_bf, do_bf, dpm, d_bgate = _merge_bwd(dh1_bf, wout, o_bf, pm_bf, z, wba, wbp)
    d_wba = _matmul_tn(o_bf, dya_bf, N_CHIPS, "dw_branch_attn")
    d_wbp = _matmul_tn(pm_bf, dyp_bf, N_CHIPS, "dw_branch_pool")
    du_bf, d_wpool, d_pscale = _pool_bwd(dpm, mixed, pooled_bf, w_pool, pool_scale)
    (dq_bf, dk_bf, dv_bf), brought = _attn_bwd(z, o_bf, lse, do_bf, cos_t, sin_t, nb,
                                               ffn_exchange([d_wg, d_wu, d_wd, dconv]))
    dz_bf = jnp.concatenate([dq_bf, dk_bf, dv_bf, du_bf, dzg_bf], axis=1)
    d_win = _matmul_tn(x_bf, dz_bf, N_CHIPS, "dw_in")
    grad_x = _inproj_dx(dz_bf, win_g, dh1)

    small = [d_bgate, d_wpool, d_pscale, d_ln1_g, d_ln1_b, d_ln2_g, d_ln2_b, dconv[:, 3, :], loss[0:1, :]]
    return grad_x, [d_win, d_wba, d_wbp, d_wout], small, brought


SMALL_ROWS = 592


def _pack_small(parts):
    flat = jnp.concatenate([p.reshape(-1) for p in parts])
    return jnp.pad(flat, (0, SMALL_ROWS * LANES - flat.shape[0])).reshape(SMALL_ROWS, LANES)


def _unpack_small(packed, shapes):
    flat = packed.reshape(-1)
    out, off = [], 0
    for shp in shapes:
        n = 1
        for d in shp:
            n *= d
        out.append(flat[off:off + n].reshape(shp))
        off += n
    return out


def _pad_conv_b(cb):
    return jnp.pad(cb.reshape(N_CHIPS, FF_SHARD), ((0, 0), (0, FF_PAD - FF_SHARD)))


def kernel(x, w_in, b_gate, w_branch_attn, w_pool, pool_scale, w_branch_pool, w_out, ln1_g, ln1_b, w_ffn_gate, w_ffn_up, conv_w, conv_b, w_ffn_down, ln2_g, ln2_b, loss_target, m_w_in, m_b_gate, m_w_branch_attn, m_w_pool, m_pool_scale, m_w_branch_pool, m_w_out, m_ln1_g, m_ln1_b, m_w_ffn_gate, m_w_ffn_up, m_conv_w, m_conv_b, m_w_ffn_down, m_ln2_g, m_ln2_b, v_w_in, v_b_gate, v_w_branch_attn, v_w_pool, v_pool_scale, v_w_branch_pool, v_w_out, v_ln1_g, v_ln1_b, v_w_ffn_gate, v_w_ffn_up, v_conv_w, v_conv_b, v_w_ffn_down, v_ln2_g, v_ln2_b):
    pad_rows = lambda w: jnp.pad(w.astype(bf16), ((0, FF_PAD - FF_SHARD), (0, 0)))
    shards = [w_in[0].astype(bf16), w_branch_attn[0].astype(bf16), w_branch_pool[0].astype(bf16),
              w_out[0].astype(bf16), pad_rows(w_ffn_gate[0].T), pad_rows(w_ffn_up[0].T), pad_rows(w_ffn_down[0]),
              jnp.pad(conv_w[0], ((0, CONV_ROWS - 3), (0, FF_PAD - FF_SHARD)))]
    (win_g,) = _cargo_alone(_GatherCargo(shards[:1], [0.0]), "gather_w_in")
    branch_cargo = _GatherCargo(shards[1:4], [0.3, 0.4, 0.6])
    ffn_cargo = _GatherCargo(shards[4:], [0.35, 0.62, 0.88, 0.9])

    place = jnp.stack([2 * lax.axis_index("x") + lax.axis_index("y"), lax.axis_index("c")]).astype(jnp.int32)

    def chip_sums(grads, names, group):
        split = [_split_halves(g) for g in grads]
        got = _presum_swap(split, "presum_swap_" + group)
        return [_add_pair(a, b, place, "presum_" + nm) for a, b, nm in zip(split, got, names)]

    ffn_names, rest_names = ["w_ffn_gate", "w_ffn_up", "w_ffn_down", "conv_w"], ["w_in", "w_branch_attn", "w_branch_pool", "w_out", "small"]
    ffn_sums = []

    def ffn_exchange(grads):
        ffn_sums.extend(chip_sums(grads, ffn_names, "ffn"))
        return _ExchangeCargo(ffn_sums, [False] * 4)

    convb_g = _pad_conv_b(conv_b).reshape(N_CHIPS, 1, FF_PAD)
    grad_x, rest, small, ffn_slots = _local_step(
        x[0], loss_target[0], win_g, branch_cargo, ffn_cargo, ffn_exchange, b_gate, w_pool[0], pool_scale, ln1_g, ln1_b,
        convb_g, ln2_g, ln2_b)

    rest_whole = [False] * 4 + [True]
    rest_sums = chip_sums(rest + [_pack_small(small)[None]], rest_names, "rest")
    rest_slots = _cargo_alone(_ExchangeCargo(rest_sums, rest_whole), "exchange_rest")
    names = rest_names[:4] + ffn_names + rest_names[4:]
    sums, slots = rest_sums[:4] + ffn_sums + rest_sums[4:], rest_slots[:4] + ffn_slots + rest_slots[4:]
    whole = [False] * 8 + [True]
    grads = _sibling_fill([_sum_slots(s, r, place, w, "sum_" + nm) for s, r, w, nm in zip(sums, slots, whole, names)])

    weights = [w_in[0], w_branch_attn[0], w_branch_pool[0], w_out[0], w_ffn_gate[0].T, w_ffn_up[0].T, w_ffn_down[0], conv_w[0]]
    m_in = [m_w_in[0], m_w_branch_attn[0], m_w_branch_pool[0], m_w_out[0], m_w_ffn_gate[0].T, m_w_ffn_up[0].T, m_w_ffn_down[0], m_conv_w[0]]
    v_in = [v_w_in[0], v_w_branch_attn[0], v_w_branch_pool[0], v_w_out[0], v_w_ffn_gate[0].T, v_w_ffn_up[0].T, v_w_ffn_down[0], v_conv_w[0]]
    tiles = [256, 256, 256, 128, 88, 88, 88, 3]
    res = {}
    for i, nm in enumerate(names[:-1]):
        outs4 = _adamw(weights[i], m_in[i], v_in[i], grads[i], "adamw_" + nm, tiles[i])
        res[nm] = [(r.T if nm in ("w_ffn_gate", "w_ffn_up") else r)[None] for r in outs4]
    loss = grads[-1][SMALL_ROWS - 4, 0]

    small_names = ["b_gate", "w_pool", "pool_scale", "ln1_g", "ln1_b", "ln2_g", "ln2_b", "conv_b"]
    small_w = [b_gate, w_pool, pool_scale, ln1_g, ln1_b, ln2_g, ln2_b, _pad_conv_b(conv_b)]
    small_m = [m_b_gate, m_w_pool, m_pool_scale, m_ln1_g, m_ln1_b, m_ln2_g, m_ln2_b, _pad_conv_b(m_conv_b)]
    small_v = [v_b_gate, v_w_pool, v_pool_scale, v_ln1_g, v_ln1_b, v_ln2_g, v_ln2_b, _pad_conv_b(v_conv_b)]
    packed = _adamw(_pack_small(small_w), _pack_small(small_m), _pack_small(small_v), grads[-1],
                    "adamw_small", SMALL_ROWS)
    shapes = [w.shape for w in small_w]
    for kind in range(4):
        for nm, val in zip(small_names, _unpack_small(packed[kind], shapes)):
            if nm == "conv_b":
                val = val[:, :FF_SHARD].reshape(1, 4 * FF_SHARD)
            res.setdefault(nm, [None] * 4)[kind] = val

    order = ["w_in", "b_gate", "w_branch_attn", "w_pool", "pool_scale", "w_branch_pool", "w_out", "ln1_g", "ln1_b",
             "w_ffn_gate", "w_ffn_up", "conv_w", "conv_b", "w_ffn_down", "ln2_g", "ln2_b"]
    outs = [loss, grad_x[None]]
    for kind in range(4):
        outs += [res[nm][kind] for nm in order]
    return tuple(outs)
```

```python
import functools

import jax
import jax.numpy as jnp
from jax import lax
from jax.experimental import pallas as pl
from jax.experimental.pallas import tpu as pltpu

D_MODEL = 1024
HEAD_DIM = 64
D_ATTN = 512
D_POOL = 512
MOBA_BLOCK = 256
MOBA_TOPK = 3
POOL_GROUP = 128
MAX_WINDOW = 16
FF_SHARD = 704
FF_PAD = 768
D_FF_PAD = 4 * FF_PAD
N_CHIPS = 4
LANES = 128
ALPHA = (2.0 * 1) ** 0.25
LN_EPS = 1e-5
NEG = -1e30
SCALE = HEAD_DIM ** -0.5
ADAM_LR, ADAM_B1, ADAM_B2, ADAM_EPS, ADAM_WD, ADAM_STEP = 0.001, 0.9, 0.999, 1e-08, 0.01, 10
VMEM_LIMIT = 56 * 1024 * 1024
MESH = pl.DeviceIdType.MESH

bf16 = jnp.bfloat16
f32 = jnp.float32


def _dot(a, b):
    return jnp.dot(a, b, preferred_element_type=f32)


def _dot_nt(a, b):
    return lax.dot_general(a, b, (((1,), (1,)), ((), ())), preferred_element_type=f32)


def _dot_tn(a, b):
    return lax.dot_general(a, b, (((0,), (0,)), ((), ())), preferred_element_type=f32)


def _call(body, *, name, grid, in_specs, out_specs, out_shape, scratch=()):
    return pl.pallas_call(
        body, name=name, grid=grid, in_specs=in_specs, out_specs=out_specs, out_shape=out_shape,
        scratch_shapes=list(scratch),
        compiler_params=pltpu.CompilerParams(dimension_semantics=("arbitrary",) * len(grid),
                                             vmem_limit_bytes=VMEM_LIMIT))


def _rot_half(t):
    lane = lax.broadcasted_iota(jnp.int32, t.shape, 1)
    first = (lane % HEAD_DIM) < (HEAD_DIM // 2)
    return jnp.where(first, pltpu.roll(t, LANES - HEAD_DIM // 2, 1), pltpu.roll(t, HEAD_DIM // 2, 1))


def _rope(t, cos, sin_signed):
    return t * cos + _rot_half(t) * sin_signed


def _rope_bwd(d, cos, sin_signed):
    return d * cos + _rot_half(d * sin_signed)


def _gelu_parts(a):
    cdf = 0.5 * (1.0 + lax.erf(a * (2.0 ** -0.5)))
    pdf = jnp.exp(-0.5 * a * a) * ((2.0 * jnp.pi) ** -0.5)
    return a * cdf, cdf + a * pdf


def _layer_norm(h, g, b):
    mu = jnp.mean(h, axis=-1, keepdims=True)
    xc = h - mu
    var = jnp.mean(xc * xc, axis=-1, keepdims=True)
    rstd = lax.rsqrt(var + LN_EPS)
    xhat = xc * rstd
    return xhat * g + b, xhat, rstd


def _layer_norm_bwd(dy, xhat, rstd, g):
    dxh = dy * g
    m1 = jnp.mean(dxh, axis=-1, keepdims=True)
    m2 = jnp.mean(dxh * xhat, axis=-1, keepdims=True)
    return rstd * (dxh - m1 - xhat * m2)


def _inproj(x_bf, win_g, cos_t, sin_t, b_gate, cargo):
    s = x_bf.shape[0]
    tm, tn = 512, 512

    def body(x_ref, w_ref, cos_ref, sin_ref, b_ref, o_ref):
        j = pl.program_id(1)
        acc = _dot(x_ref[...], w_ref[...])

        @pl.when(j < 2)
        def _():
            for c in range(tn // LANES):
                sl = slice(c * LANES, (c + 1) * LANES)
                o_ref[:, sl] = _rope(acc[:, sl], cos_ref[...], sin_ref[...])

        @pl.when((j >= 2) & (j < 4))
        def _():
            o_ref[...] = acc

        @pl.when(j >= 4)
        def _():
            o_ref[...] = jax.nn.sigmoid(acc + b_ref[...])

    (z,), carried = _cargo_call(
        body, cargo, name="inproj", grid=(s // tm, 4 * D_MODEL // tn),
        in_specs=[pl.BlockSpec((tm, D_MODEL), lambda i, j: (i, 0)),
                  pl.BlockSpec((None, D_MODEL, tn), lambda i, j: (j // 2, 0, j % 2)),
                  pl.BlockSpec((tm, LANES), lambda i, j: (i, 0)),
                  pl.BlockSpec((tm, LANES), lambda i, j: (i, 0)),
                  pl.BlockSpec((1, tn), lambda i, j: (0, jnp.maximum(j - 4, 0)))],
        out_specs=[pl.BlockSpec((tm, tn), lambda i, j: (i, j))],
        out_shape=[jax.ShapeDtypeStruct((s, 4 * D_MODEL), f32)])(x_bf, win_g, cos_t, sin_t, b_gate)
    return z, carried


STAT_ROWS = 8


def _pair_rows(v0, v1, fill):
    r = lax.broadcasted_iota(jnp.int32, (STAT_ROWS, v0.shape[1]), 0)
    return jnp.where(r == 0, v0, jnp.where(r == 1, v1, fill))


def _head_lanes(shape):
    lane = lax.broadcasted_iota(jnp.int32, shape, 1)
    return lane < HEAD_DIM, lane >= HEAD_DIM


def _head_rows(shape):
    row = lax.broadcasted_iota(jnp.int32, shape, 0)
    return row < HEAD_DIM, row >= HEAD_DIM


def _moba_select(q_bf, k_all, i, nb):
    k_mean = jnp.mean(k_all.reshape(nb, MOBA_BLOCK, LANES), axis=1)
    n_io = lax.broadcasted_iota(jnp.int32, (nb, MOBA_BLOCK), 0)
    past = n_io < i
    sels = []
    for head in _head_lanes((nb, LANES)):
        gate = _dot_nt(jnp.where(head, k_mean, 0.0).astype(bf16), q_bf)
        g = jnp.where(past, gate, NEG)
        rank = jnp.zeros((nb, MOBA_BLOCK), f32)
        for m in range(nb):
            gm = g[m:m + 1, :]
            rank = rank + jnp.where((gm > g) | ((gm == g) & (m < n_io)), 1.0, 0.0)
        sels.append(jnp.where(past & (rank < MOBA_TOPK), 1.0, 0.0))
    return sels


def _attn_fwd(z, nb, cargo):
    s = z.shape[0]

    def body(q_ref, k_ref, v_ref, o_ref, lse_ref, vt_s, m_s, l_s, acc_s):
        i = pl.program_id(1)

        @pl.when(i == 0)
        def _():
            for j in range(nb):
                vt_s[j] = v_ref[j * MOBA_BLOCK:(j + 1) * MOBA_BLOCK, :].T.astype(bf16)

        q = q_ref[...]
        sels = _moba_select(q.astype(bf16), k_ref[...], i, nb)
        qs = (q * SCALE).astype(bf16)
        k_heads = _head_lanes((MOBA_BLOCK, LANES))
        o_heads = _head_rows((LANES, MOBA_BLOCK))
        key = lax.broadcasted_iota(jnp.int32, (MOBA_BLOCK, MOBA_BLOCK), 0)
        qry = lax.broadcasted_iota(jnp.int32, (MOBA_BLOCK, MOBA_BLOCK), 1)

        def block(kj, vtj, allow, m_old, l_old):
            m_new, alpha, l_new, pv = [], [], [], None
            for h in range(2):
                sc = jnp.where(allow[h], _dot_nt(jnp.where(k_heads[h], kj, 0.0).astype(bf16), qs), NEG)
                mx = jnp.max(sc, axis=0, keepdims=True)
                mn = mx if m_old is None else jnp.maximum(m_old[h], mx)
                p = jnp.exp(sc - mn)
                lsum = jnp.sum(p, axis=0, keepdims=True)
                if m_old is None:
                    a = None
                else:
                    a = jnp.exp(m_old[h] - mn)
                    lsum = a * l_old[h] + lsum
                t = _dot(jnp.where(o_heads[h], vtj, jnp.zeros_like(vtj)), p.astype(bf16))
                pv = t if pv is None else pv + t
                m_new.append(mn)
                alpha.append(a)
                l_new.append(lsum)
            return m_new, alpha, l_new, pv

        own = pl.ds(pl.multiple_of(i * MOBA_BLOCK, MOBA_BLOCK), MOBA_BLOCK)
        m_new, _, l_new, pv = block(k_ref[own, :], vt_s[i], [key <= qry] * 2, None, None)
        m_s[...] = _pair_rows(m_new[0], m_new[1], 0.0)
        l_s[...] = _pair_rows(l_new[0], l_new[1], 1.0)
        acc_s[...] = pv

        for j in range(nb - 1):
            @pl.when(j < i)
            def _(j=j):
                m_old, l_old = m_s[...], l_s[...]
                m_new, alpha, l_new, pv = block(
                    k_ref[j * MOBA_BLOCK:(j + 1) * MOBA_BLOCK, :], vt_s[j], [sels[h][j:j + 1, :] > 0.0 for h in range(2)],
                    [m_old[0:1, :], m_old[1:2, :]], [l_old[0:1, :], l_old[1:2, :]])
                m_s[...] = _pair_rows(m_new[0], m_new[1], 0.0)
                l_s[...] = _pair_rows(l_new[0], l_new[1], 1.0)
                acc_s[...] = acc_s[...] * jnp.where(o_heads[0], alpha[0], alpha[1]) + pv

        l_fin = l_s[...]
        o_t = acc_s[...] / jnp.where(o_heads[0], l_fin[0:1, :], l_fin[1:2, :])
        o_ref[...] = o_t.T.astype(bf16)
        lse_ref[...] = m_s[...] + jnp.log(l_fin)

    stat = pltpu.VMEM((STAT_ROWS, MOBA_BLOCK), f32)
    return _cargo_call(
        body, cargo, name="attn_fwd", grid=(D_ATTN // LANES, nb),
        in_specs=[pl.BlockSpec((MOBA_BLOCK, LANES), lambda hp, i: (i, hp)),
                  pl.BlockSpec((s, LANES), lambda hp, i: (0, 4 + hp)),
                  pl.BlockSpec((s, LANES), lambda hp, i: (0, 8 + hp))],
        out_specs=[pl.BlockSpec((MOBA_BLOCK, LANES), lambda hp, i: (i, hp)),
                   pl.BlockSpec((None, STAT_ROWS, MOBA_BLOCK), lambda hp, i: (hp, 0, i))],
        out_shape=[jax.ShapeDtypeStruct((s, D_ATTN), bf16),
                   jax.ShapeDtypeStruct((D_ATTN // LANES, STAT_ROWS, s), f32)],
        scratch=[pltpu.VMEM((nb, LANES, MOBA_BLOCK), bf16), stat, stat,
                 pltpu.VMEM((LANES, MOBA_BLOCK), f32)])(z, z, z)


def _window_select(snaps, g):
    return jnp.where(g == 0, snaps[0], jnp.where(g == 1, snaps[1], jnp.where(g == 2, snaps[2], snaps[3])))


def _pool_fwd(z, w_pool, pool_scale):
    s = z.shape[0]

    def body(u_ref, w_ref, sc_ref, pooled_ref, mixed_ref, pm_ref, pad):
        g = pl.program_id(0)
        u = u_ref[...]
        pad[0:MAX_WINDOW, :] = jnp.zeros((MAX_WINDOW, POOL_GROUP), f32)
        pad[MAX_WINDOW:MAX_WINDOW + s, :] = u
        acc = u
        snaps = []
        for d in range(1, MAX_WINDOW):
            acc = acc + pad[MAX_WINDOW - d:MAX_WINDOW - d + s, :]
            if d + 1 in (2, 4, 8, 16):
                snaps.append(acc)
        win = _window_select(snaps, g)
        t = lax.broadcasted_iota(jnp.int32, (s, POOL_GROUP), 0)
        count = jnp.minimum(t + 1, jnp.left_shift(2, g)).astype(f32)
        pooled = (win / count - u).astype(bf16)
        mixed = _dot(pooled, w_ref[...].astype(bf16))
        pooled_ref[...] = pooled
        mixed_ref[...] = mixed
        pm_ref[...] = (mixed * sc_ref[...]).astype(bf16)

    blk = pl.BlockSpec((s, POOL_GROUP), lambda g: (0, g))
    return _call(
        body, name="pool_fwd", grid=(4,),
        in_specs=[pl.BlockSpec((s, POOL_GROUP), lambda g: (0, 12 + g)),
                  pl.BlockSpec((None, POOL_GROUP, POOL_GROUP), lambda g: (g, 0, 0)),
                  pl.BlockSpec((1, POOL_GROUP), lambda g: (0, g))],
        out_specs=[blk, blk, blk],
        out_shape=[jax.ShapeDtypeStruct((s, D_POOL), bf16), jax.ShapeDtypeStruct((s, D_POOL), f32),
                   jax.ShapeDtypeStruct((s, D_POOL), bf16)],
        scratch=[pltpu.VMEM((s + MAX_WINDOW, POOL_GROUP), f32)])(z, w_pool, pool_scale)


def _branch_merge(o_bf, pm_bf, z, wba, wbp):
    s = o_bf.shape[0]
    tm = 256

    def body(o_ref, pm_ref, ga_ref, gp_ref, wba_ref, wbp_ref, m_ref):
        ya = _dot(o_ref[...], wba_ref[...])
        yp = _dot(pm_ref[...], wbp_ref[...])
        m_ref[...] = (ga_ref[...] * ya + gp_ref[...] * yp).astype(bf16)

    full = lambda r, c: pl.BlockSpec((r, c), lambda i: (0, 0))
    return _call(
        body, name="branch_merge", grid=(s // tm,),
        in_specs=[pl.BlockSpec((tm, D_ATTN), lambda i: (i, 0)), pl.BlockSpec((tm, D_POOL), lambda i: (i, 0)),
                  pl.BlockSpec((tm, D_MODEL), lambda i: (i, 2)), pl.BlockSpec((tm, D_MODEL), lambda i: (i, 3)),
                  full(D_ATTN, D_MODEL), full(D_POOL, D_MODEL)],
        out_specs=pl.BlockSpec((tm, D_MODEL), lambda i: (i, 0)),
        out_shape=jax.ShapeDtypeStruct((s, D_MODEL), bf16))(o_bf, pm_bf, z, z, wba, wbp)


def _out_ln1(m_bf, wout, x, ln_g, ln_b):
    s = x.shape[0]
    tm = 256

    def body(m_ref, w_ref, x_ref, g_ref, b_ref, x1_ref, xhat_ref, rstd_ref):
        h = ALPHA * x_ref[...] + _dot(m_ref[...], w_ref[...])
        y, xhat, rstd = _layer_norm(h, g_ref[...], b_ref[...])
        x1_ref[...] = y.astype(bf16)
        xhat_ref[...] = xhat
        rstd_ref[...] = jnp.broadcast_to(rstd, (tm, LANES))

    row = pl.BlockSpec((tm, D_MODEL), lambda i: (i, 0))
    vec = pl.BlockSpec((1, D_MODEL), lambda i: (0, 0))
    return _call(
        body, name="out_ln1", grid=(s // tm,),
        in_specs=[row, pl.BlockSpec((D_MODEL, D_MODEL), lambda i: (0, 0)), row, vec, vec],
        out_specs=[row, row, pl.BlockSpec((tm, LANES), lambda i: (i, 0))],
        out_shape=[jax.ShapeDtypeStruct((s, D_MODEL), bf16), jax.ShapeDtypeStruct((s, D_MODEL), f32),
                   jax.ShapeDtypeStruct((s, LANES), f32)])(m_bf, wout, x, ln_g, ln_b)


FF_TILE = 256
FF_TILES_PER_SHARD = FF_PAD // FF_TILE
CONV_PAD = 8
CONV_ROWS = 16


def _ff_weight_spec(rows):
    return pl.BlockSpec((None, rows, FF_TILE), lambda j: (j // FF_TILES_PER_SHARD, 0, j % FF_TILES_PER_SHARD))


def _ff_rows_spec():
    return pl.BlockSpec((None, FF_TILE, D_MODEL), lambda j: (j // FF_TILES_PER_SHARD, j % FF_TILES_PER_SHARD, 0))


def _conv_taps(pad, s, cw_ref, off=0):
    return (pad[CONV_PAD + off:CONV_PAD + off + s, :] * cw_ref[2:3, :]
            + pad[CONV_PAD - 1 + off:CONV_PAD - 1 + off + s, :] * cw_ref[1:2, :]
            + pad[CONV_PAD - 2 + off:CONV_PAD - 2 + off + s, :] * cw_ref[0:1, :])


def _ffn_up(x1_bf, wg_g, wu_g, convw_g, convb_g):
    s = x1_bf.shape[0]

    def body(x_ref, wg_ref, wu_ref, cw_ref, cb_ref, a_ref, up_ref, hh_ref, pad):
        x1 = x_ref[...]
        a = _dot_nt(x1, wg_ref[...])
        up = _dot_nt(x1, wu_ref[...])
        a_ref[...] = a
        up_ref[...] = up
        pad[0:CONV_PAD, :] = jnp.zeros((CONV_PAD, FF_TILE), f32)
        pad[CONV_PAD:CONV_PAD + s, :] = a
        ac = _conv_taps(pad, s, cw_ref) + cb_ref[...]
        hg, _ = _gelu_parts(ac)
        hh_ref[...] = (hg * up).astype(bf16)

    col = pl.BlockSpec((s, FF_TILE), lambda j: (0, j))
    return _call(
        body, name="ffn_up", grid=(D_FF_PAD // FF_TILE,),
        in_specs=[pl.BlockSpec((s, D_MODEL), lambda j: (0, 0)), _ff_rows_spec(), _ff_rows_spec(),
                  _ff_weight_spec(8), _ff_weight_spec(1)],
        out_specs=[col, col, col],
        out_shape=[jax.ShapeDtypeStruct((s, D_FF_PAD), f32), jax.ShapeDtypeStruct((s, D_FF_PAD), f32),
                   jax.ShapeDtypeStruct((s, D_FF_PAD), bf16)],
        scratch=[pltpu.VMEM((s + CONV_PAD, FF_TILE), f32)])(x1_bf, wg_g, wu_g, convw_g, convb_g)


def _ffn_down_ln2_loss(hh_bf, wd, xhat1, ln1_g, ln1_b, ln2_g, ln2_b, target):
    s = hh_bf.shape[0]
    tm = 256

    def body(hh_ref, w_ref, xh1_ref, g1_ref, b1_ref, g2_ref, b2_ref, t_ref, loss_ref, dh_ref, dhb_ref, dg_ref, db_ref):
        i = pl.program_id(0)
        x1 = xh1_ref[...] * g1_ref[...] + b1_ref[...]
        h = ALPHA * x1 + _dot(hh_ref[...], w_ref[...])
        y, xhat, rstd = _layer_norm(h, g2_ref[...], b2_ref[...])
        err = y - t_ref[...]
        part = 0.5 * jnp.sum(jnp.mean(err * err, axis=-1, keepdims=True), axis=0, keepdims=True)
        dy = err * (1.0 / D_MODEL)

        @pl.when(i == 0)
        def _():
            loss_ref[...] = jnp.zeros_like(loss_ref)
            dg_ref[...] = jnp.zeros_like(dg_ref)
            db_ref[...] = jnp.zeros_like(db_ref)

        loss_ref[...] += jnp.broadcast_to(part, loss_ref.shape)
        dg_ref[...] += jnp.sum(dy * xhat, axis=0, keepdims=True)
        db_ref[...] += jnp.sum(dy, axis=0, keepdims=True)
        dh = _layer_norm_bwd(dy, xhat, rstd, g2_ref[...])
        dh_ref[...] = dh
        dhb_ref[...] = dh.astype(bf16)

    row = pl.BlockSpec((tm, D_MODEL), lambda i: (i, 0))
    vec = pl.BlockSpec((1, D_MODEL), lambda i: (0, 0))
    return _call(
        body, name="ffn_down_ln2_loss", grid=(s // tm,),
        in_specs=[pl.BlockSpec((tm, D_FF_PAD), lambda i: (i, 0)), pl.BlockSpec((D_FF_PAD, D_MODEL), lambda i: (0, 0)),
                  row, vec, vec, vec, vec, row],
        out_specs=[pl.BlockSpec((8, LANES), lambda i: (0, 0)), row, row, vec, vec],
        out_shape=[jax.ShapeDtypeStruct((8, LANES), f32), jax.ShapeDtypeStruct((s, D_MODEL), f32),
                   jax.ShapeDtypeStruct((s, D_MODEL), bf16), jax.ShapeDtypeStruct((1, D_MODEL), f32),
                   jax.ShapeDtypeStruct((1, D_MODEL), f32)])(hh_bf, wd, xhat1, ln1_g, ln1_b, ln2_g, ln2_b, target)


def _ffn_act_bwd(dh2_bf, wd_g, a, up, convw_g, convb_g):
    s = dh2_bf.shape[0]

    def body(dh_ref, wd_ref, a_ref, up_ref, cw_ref, cb_ref, da_ref, dup_ref, dc_ref, pad, dpad):
        dhh = _dot_nt(dh_ref[...], wd_ref[...])
        zeros = jnp.zeros((CONV_PAD, FF_TILE), f32)
        pad[0:CONV_PAD, :] = zeros
        pad[CONV_PAD:CONV_PAD + s, :] = a_ref[...]
        ac = _conv_taps(pad, s, cw_ref) + cb_ref[...]
        hg, dgelu = _gelu_parts(ac)
        up = up_ref[...]
        dup_ref[...] = (dhh * hg).astype(bf16)
        dac = dhh * up * dgelu
        rows = [jnp.sum(dac * pad[CONV_PAD - 2 + i:CONV_PAD - 2 + i + s, :], axis=0, keepdims=True) for i in range(3)]
        rows.append(jnp.sum(dac, axis=0, keepdims=True))
        rows.append(jnp.zeros((CONV_ROWS - 4, FF_TILE), f32))
        dc_ref[...] = jnp.concatenate(rows, axis=0)
        dpad[0:s, :] = dac
        dpad[s:s + CONV_PAD, :] = zeros
        da = (dpad[0:s, :] * cw_ref[2:3, :] + dpad[1:1 + s, :] * cw_ref[1:2, :] + dpad[2:2 + s, :] * cw_ref[0:1, :])
        da_ref[...] = da.astype(bf16)

    col = pl.BlockSpec((s, FF_TILE), lambda j: (0, j))
    return _call(
        body, name="ffn_act_bwd", grid=(D_FF_PAD // FF_TILE,),
        in_specs=[pl.BlockSpec((s, D_MODEL), lambda j: (0, 0)), _ff_rows_spec(),
                  col, col, _ff_weight_spec(8), _ff_weight_spec(1)],
        out_specs=[col, col, _ff_weight_spec(CONV_ROWS)],
        out_shape=[jax.ShapeDtypeStruct((s, D_FF_PAD), bf16), jax.ShapeDtypeStruct((s, D_FF_PAD), bf16),
                   jax.ShapeDtypeStruct((N_CHIPS, CONV_ROWS, FF_PAD), f32)],
        scratch=[pltpu.VMEM((s + CONV_PAD, FF_TILE), f32), pltpu.VMEM((s + CONV_PAD, FF_TILE), f32)],
    )(dh2_bf, wd_g, a, up, convw_g, convb_g)


def _matmul_tn(a, b, n_shards, name, tm=512, tn=256):
    k, m = a.shape
    n = b.shape[1]
    tm, tn = min(tm, m), min(tn, n // n_shards)
    per = n // n_shards // tn

    def body(a_ref, b_ref, o_ref):
        o_ref[...] = _dot_tn(a_ref[...], b_ref[...]).astype(bf16)

    return _call(
        body, name=name, grid=(m // tm, n // tn),
        in_specs=[pl.BlockSpec((k, tm), lambda i, j: (0, i)), pl.BlockSpec((k, tn), lambda i, j: (0, j))],
        out_specs=pl.BlockSpec((None, tm, tn), lambda i, j: (j // per, i, j % per)),
        out_shape=jax.ShapeDtypeStruct((n_shards, m, n // n_shards), bf16))(a, b)


def _ffn_in_bwd_ln1(da_bf, dup_bf, wg_g, wu_g, dh2, xhat1, rstd1, ln1_g):
    s = da_bf.shape[0]
    tm = 256

    def body(da_ref, dup_ref, wg_ref, wu_ref, dh2_ref, xh_ref, rstd_ref, g_ref, dh_ref, dhb_ref, dg_ref, db_ref):
        i = pl.program_id(0)
        dx1 = ALPHA * dh2_ref[...]
        for sh in range(N_CHIPS):
            sl = slice(sh * FF_PAD, (sh + 1) * FF_PAD)
            dx1 = dx1 + _dot(da_ref[:, sl], wg_ref[sh]) + _dot(dup_ref[:, sl], wu_ref[sh])
        xhat = xh_ref[...]

        @pl.when(i == 0)
        def _():
            dg_ref[...] = jnp.zeros_like(dg_ref)
            db_ref[...] = jnp.zeros_like(db_ref)

        dg_ref[...] += jnp.sum(dx1 * xhat, axis=0, keepdims=True)
        db_ref[...] += jnp.sum(dx1, axis=0, keepdims=True)
        dh = _layer_norm_bwd(dx1, xhat, rstd_ref[:, 0:1], g_ref[...])
        dh_ref[...] = dh
        dhb_ref[...] = dh.astype(bf16)

    row = pl.BlockSpec((tm, D_MODEL), lambda i: (i, 0))
    wide = pl.BlockSpec((tm, D_FF_PAD), lambda i: (i, 0))
    wfull = pl.BlockSpec((N_CHIPS, FF_PAD, D_MODEL), lambda i: (0, 0, 0))
    vec = pl.BlockSpec((1, D_MODEL), lambda i: (0, 0))
    return _call(
        body, name="ffn_in_bwd_ln1", grid=(s // tm,),
        in_specs=[wide, wide, wfull, wfull, row, row, pl.BlockSpec((tm, LANES), lambda i: (i, 0)), vec],
        out_specs=[row, row, vec, vec],
        out_shape=[jax.ShapeDtypeStruct((s, D_MODEL), f32), jax.ShapeDtypeStruct((s, D_MODEL), bf16),
                   jax.ShapeDtypeStruct((1, D_MODEL), f32), jax.ShapeDtypeStruct((1, D_MODEL), f32)],
    )(da_bf, dup_bf, wg_g, wu_g, dh2, xhat1, rstd1, ln1_g)


def _merge_bwd(dh1_bf, wout, o_bf, pm_bf, z, wba, wbp):
    s = dh1_bf.shape[0]
    tm = 256

    def body(dh_ref, wout_ref, o_ref, pm_ref, ga_ref, gp_ref, wba_ref, wbp_ref,
             dzg_ref, dya_ref, dyp_ref, do_ref, dpm_ref, dbg_ref):
        i = pl.program_id(0)
        dm = _dot_nt(dh_ref[...], wout_ref[...])
        ya = _dot(o_ref[...], wba_ref[...])
        yp = _dot(pm_ref[...], wbp_ref[...])
        ga, gp = ga_ref[...], gp_ref[...]
        dza = dm * ya * ga * (1.0 - ga)
        dzp = dm * yp * gp * (1.0 - gp)

        @pl.when(i == 0)
        def _():
            dbg_ref[...] = jnp.zeros_like(dbg_ref)

        dbg_ref[:, 0:D_MODEL] += jnp.sum(dza, axis=0, keepdims=True)
        dbg_ref[:, D_MODEL:2 * D_MODEL] += jnp.sum(dzp, axis=0, keepdims=True)
        dzg_ref[:, 0:D_MODEL] = dza.astype(bf16)
        dzg_ref[:, D_MODEL:2 * D_MODEL] = dzp.astype(bf16)
        dya = (dm * ga).astype(bf16)
        dyp = (dm * gp).astype(bf16)
        dya_ref[...] = dya
        dyp_ref[...] = dyp
        do_ref[...] = _dot_nt(dya, wba_ref[...]).astype(bf16)
        dpm_ref[...] = _dot_nt(dyp, wbp_ref[...])

    row = pl.BlockSpec((tm, D_MODEL), lambda i: (i, 0))
    half = pl.BlockSpec((tm, D_ATTN), lambda i: (i, 0))
    full = lambda r, c: pl.BlockSpec((r, c), lambda i: (0, 0))
    return _call(
        body, name="merge_bwd", grid=(s // tm,),
        in_specs=[row, full(D_MODEL, D_MODEL), half, half,
                  pl.BlockSpec((tm, D_MODEL), lambda i: (i, 2)), pl.BlockSpec((tm, D_MODEL), lambda i: (i, 3)),
                  full(D_ATTN, D_MODEL), full(D_POOL, D_MODEL)],
        out_specs=[pl.BlockSpec((tm, 2 * D_MODEL), lambda i: (i, 0)), row, row, half, half,
                   pl.BlockSpec((1, 2 * D_MODEL), lambda i: (0, 0))],
        out_shape=[jax.ShapeDtypeStruct((s, 2 * D_MODEL), bf16), jax.ShapeDtypeStruct((s, D_MODEL), bf16),
                   jax.ShapeDtypeStruct((s, D_MODEL), bf16), jax.ShapeDtypeStruct((s, D_ATTN), bf16),
                   jax.ShapeDtypeStruct((s, D_POOL), f32), jax.ShapeDtypeStruct((1, 2 * D_MODEL), f32)],
    )(dh1_bf, wout, o_bf, pm_bf, z, z, wba, wbp)


def _pool_bwd(dpm, mixed, pooled_bf, w_pool, pool_scale):
    s = dpm.shape[0]

    def body(dpm_ref, mixed_ref, pooled_ref, w_ref, sc_ref, du_ref, dw_ref, dsc_ref, pad):
        g = pl.program_id(0)
        dpm_v = dpm_ref[...]
        dsc_ref[...] = jnp.sum(dpm_v * mixed_ref[...], axis=0, keepdims=True)
        dmixed = (dpm_v * sc_ref[...]).astype(bf16)
        dw_ref[...] = _dot_tn(pooled_ref[...], dmixed)
        dpooled = _dot_nt(dmixed, w_ref[...].astype(bf16))
        t = lax.broadcasted_iota(jnp.int32, (s, POOL_GROUP), 0)
        count = jnp.minimum(t + 1, jnp.left_shift(2, g)).astype(f32)
        r = dpooled / count
        pad[0:s, :] = r
        pad[s:s + MAX_WINDOW, :] = jnp.zeros((MAX_WINDOW, POOL_GROUP), f32)
        acc = r
        snaps = []
        for d in range(1, MAX_WINDOW):
            acc = acc + pad[d:d + s, :]
            if d + 1 in (2, 4, 8, 16):
                snaps.append(acc)
        du_ref[...] = (_window_select(snaps, g) - dpooled).astype(bf16)

    blk = pl.BlockSpec((s, POOL_GROUP), lambda g: (0, g))
    return _call(
        body, name="pool_bwd", grid=(4,),
        in_specs=[blk, blk, blk, pl.BlockSpec((None, POOL_GROUP, POOL_GROUP), lambda g: (g, 0, 0)),
                  pl.BlockSpec((1, POOL_GROUP), lambda g: (0, g))],
        out_specs=[blk, pl.BlockSpec((None, POOL_GROUP, POOL_GROUP), lambda g: (g, 0, 0)),
                   pl.BlockSpec((1, POOL_GROUP), lambda g: (0, g))],
        out_shape=[jax.ShapeDtypeStruct((s, D_POOL), bf16), jax.ShapeDtypeStruct((4, POOL_GROUP, POOL_GROUP), f32),
                   jax.ShapeDtypeStruct((1, D_POOL), f32)],
        scratch=[pltpu.VMEM((s + MAX_WINDOW, POOL_GROUP), f32)])(dpm, mixed, pooled_bf, w_pool, pool_scale)


def _attn_bwd(z, o_bf, lse, do_bf, cos_t, sin_t, nb, cargo):
    s = z.shape[0]

    def body(q_ref, k_ref, v_ref, o_ref, lse_ref, do_ref, cq_ref, sq_ref, cf_ref, sf_ref,
             dq_ref, dk_ref, dv_ref, kt_s, dqt_s, dk_s, dv_s):
        i = pl.program_id(1)

        @pl.when(i == 0)
        def _():
            dk_s[...] = jnp.zeros_like(dk_s)
            dv_s[...] = jnp.zeros_like(dv_s)
            for j in range(nb):
                kt_s[j] = k_ref[j * MOBA_BLOCK:(j + 1) * MOBA_BLOCK, :].T.astype(bf16)

        q = q_ref[...]
        sels = _moba_select(q.astype(bf16), k_ref[...], i, nb)
        qs = (q * SCALE).astype(bf16)
        k_heads = _head_lanes((MOBA_BLOCK, LANES))
        t_heads = _head_rows((LANES, MOBA_BLOCK))
        qraw = [jnp.where(hm, q, 0.0).astype(bf16) for hm in k_heads]
        do = do_ref[...].astype(f32)
        do_bf = do_ref[...]
        dob = [jnp.where(hm, do, 0.0).astype(bf16) for hm in k_heads]
        pick = _head_lanes((STAT_ROWS, LANES))
        r8 = lax.broadcasted_iota(jnp.int32, (STAT_ROWS, LANES), 0)
        head_pick = jnp.where(((r8 == 0) & pick[0]) | ((r8 == 1) & pick[1]), 1.0, 0.0)
        delta8 = lax.dot_general(head_pick, do * o_ref[...].astype(f32), (((1,), (1,)), ((), ())),
                                 precision=lax.Precision.HIGHEST, preferred_element_type=f32)
        lse8 = lse_ref[...]
        delta, lse_h = [delta8[0:1, :], delta8[1:2, :]], [lse8[0:1, :], lse8[1:2, :]]
        key = lax.broadcasted_iota(jnp.int32, (MOBA_BLOCK, MOBA_BLOCK), 0)
        qry = lax.broadcasted_iota(jnp.int32, (MOBA_BLOCK, MOBA_BLOCK), 1)
        dqt_s[...] = jnp.zeros_like(dqt_s)

        def block(rows, ktj, allow):
            kj, vj = k_ref[rows, :], v_ref[rows, :]
            dqt, dk, dv = None, None, None
            for h in range(2):
                sc = _dot_nt(jnp.where(k_heads[h], kj, 0.0).astype(bf16), qs)
                p = jnp.where(allow[h], jnp.exp(sc - lse_h[h]), 0.0)
                dp = _dot_nt(jnp.where(k_heads[h], vj, 0.0).astype(bf16), do_bf)
                ds = (p * (dp - delta[h]) * SCALE).astype(bf16)
                tv = _dot(p.astype(bf16), dob[h])
                tk = _dot(ds, qraw[h])
                tq = _dot(jnp.where(t_heads[h], ktj, jnp.zeros_like(ktj)), ds)
                dqt, dk, dv = (tq, tk, tv) if dqt is None else (dqt + tq, dk + tk, dv + tv)
            dqt_s[...] += dqt
            dk_s[rows, :] += dk
            dv_s[rows, :] += dv

        block(pl.ds(pl.multiple_of(i * MOBA_BLOCK, MOBA_BLOCK), MOBA_BLOCK), kt_s[i], [key <= qry] * 2)
        for j in range(nb - 1):
            @pl.when(j < i)
            def _(j=j):
                block(slice(j * MOBA_BLOCK, (j + 1) * MOBA_BLOCK), kt_s[j], [sels[h][j:j + 1, :] > 0.0 for h in range(2)])

        dq_ref[...] = _rope_bwd(dqt_s[...].T, cq_ref[...], sq_ref[...]).astype(bf16)

        @pl.when(i == nb - 1)
        def _():
            dk_ref[...] = _rope_bwd(dk_s[...], cf_ref[...], sf_ref[...]).astype(bf16)
            dv_ref[...] = dv_s[...].astype(bf16)

    qblk = pl.BlockSpec((MOBA_BLOCK, LANES), lambda hp, i: (i, hp))
    tq = pl.BlockSpec((MOBA_BLOCK, LANES), lambda hp, i: (i, 0))
    tf = pl.BlockSpec((s, LANES), lambda hp, i: (0, 0))
    colblk = pl.BlockSpec((s, LANES), lambda hp, i: (0, hp))
    return _cargo_call(
        body, cargo, name="attn_bwd", grid=(D_ATTN // LANES, nb),
        in_specs=[qblk, pl.BlockSpec((s, LANES), lambda hp, i: (0, 4 + hp)),
                  pl.BlockSpec((s, LANES), lambda hp, i: (0, 8 + hp)), qblk,
                  pl.BlockSpec((None, STAT_ROWS, MOBA_BLOCK), lambda hp, i: (hp, 0, i)), qblk, tq, tq, tf, tf],
        out_specs=[qblk, colblk, colblk],
        out_shape=[jax.ShapeDtypeStruct((s, D_ATTN), bf16)] * 3,
        scratch=[pltpu.VMEM((nb, LANES, MOBA_BLOCK), bf16), pltpu.VMEM((LANES, MOBA_BLOCK), f32),
                 pltpu.VMEM((s, LANES), f32), pltpu.VMEM((s, LANES), f32)],
    )(z, z, z, o_bf, lse, do_bf, cos_t, sin_t, cos_t, sin_t)


def _inproj_dx(dz_bf, win_g, dh1):
    s = dz_bf.shape[0]
    tm = 256

    def body(dz_ref, w_ref, dh_ref, gx_ref):
        acc = ALPHA * dh_ref[...]
        for sh in range(N_CHIPS):
            acc = acc + _dot_nt(dz_ref[:, sh * D_MODEL:(sh + 1) * D_MODEL], w_ref[sh])
        gx_ref[...] = acc

    row = pl.BlockSpec((tm, D_MODEL), lambda i: (i, 0))
    return _call(
        body, name="inproj_dx", grid=(s // tm,),
        in_specs=[pl.BlockSpec((tm, 4 * D_MODEL), lambda i: (i, 0)),
                  pl.BlockSpec((N_CHIPS, D_MODEL, D_MODEL), lambda i: (0, 0, 0)), row],
        out_specs=row, out_shape=jax.ShapeDtypeStruct((s, D_MODEL), f32))(dz_bf, win_g, dh1)


ANY = pl.BlockSpec(memory_space=pl.ANY)


def _chip_index():
    return 2 * lax.axis_index("x") + lax.axis_index("y")


def _peer(k):
    x, y, c = lax.axis_index("x"), lax.axis_index("y"), lax.axis_index("c")
    return (x ^ (k >> 1), y ^ (k & 1), c)


def _sibling():
    return (lax.axis_index("x"), lax.axis_index("y"), 1 - lax.axis_index("c"))


class _GatherCargo:
    def __init__(self, shards, pass_on):
        self.shards, self.pass_on, n = shards, pass_on, len(shards)
        self.inputs = [a.reshape(2, a.shape[0] // 2, a.shape[1]) for a in shards]
        self.out_shape = [jax.ShapeDtypeStruct((N_CHIPS,) + a.shape, a.dtype) for a in self.inputs]
        sem, one = pltpu.SemaphoreType.DMA((N_CHIPS - 1, n)), pltpu.SemaphoreType.DMA((n,))
        self.sems = [sem, sem, sem, sem, one, one]

    def _copies(self, src, dst, sems, a):
        send_sems, recv_sems, fsend_sems, frecv_sems, local_sems, own_sems = sems
        p, c = _chip_index(), lax.axis_index("c")
        own = pltpu.make_async_remote_copy(src_ref=src[a], dst_ref=dst[a].at[p], send_sem=local_sems.at[a],
                                           recv_sem=own_sems.at[a], device_id=_sibling(), device_id_type=MESH)
        out, arrive, onward, landed = [], [], [], []
        for k in range(1, N_CHIPS):
            ici = dict(send_sem=send_sems.at[k - 1, a], recv_sem=recv_sems.at[k - 1, a], device_id=_peer(k),
                       device_id_type=MESH)
            d2d = dict(send_sem=fsend_sems.at[k - 1, a], recv_sem=frecv_sems.at[k - 1, a], device_id=_sibling(),
                       device_id_type=MESH)
            got, theirs = dst[a].at[p ^ k, c], dst[a].at[p ^ k, 1 - c]
            out.append(pltpu.make_async_remote_copy(src_ref=src[a].at[c], dst_ref=dst[a].at[p, c], **ici))
            arrive.append(pltpu.make_async_remote_copy(src_ref=src[a].at[c], dst_ref=got, **ici))
            onward.append(pltpu.make_async_remote_copy(src_ref=got, dst_ref=got, **d2d))
            landed.append(pltpu.make_async_remote_copy(src_ref=theirs, dst_ref=theirs, **d2d))
        return own, out, arrive, onward, landed

    def stages(self, steps):
        n = len(self.shards)

        def start(src, dst, sems):
            for a in range(n):
                own, out, _, _, _ = self._copies(src, dst, sems, a)
                own.start()
                for cp in out:
                    cp.start()

        def pass_on(a):
            def act(src, dst, sems):
                _, _, arrive, onward, _ = self._copies(src, dst, sems, a)
                for k in range(N_CHIPS - 1):
                    arrive[k].wait_recv()
                    onward[k].start()
            return act

        def finish(src, dst, sems):
            for a in range(n):
                own, out, _, onward, landed = self._copies(src, dst, sems, a)
                for cp in landed:
                    cp.wait_recv()
                for cp in out + onward:
                    cp.wait_send()
                own.wait()

        mids = [(min(steps - 1, int(self.pass_on[a] * steps)), pass_on(a)) for a in range(n)]
        return [(0, start)] + mids + [(steps - 1, finish)]

    def results(self, outs):
        return [o.reshape((N_CHIPS,) + a.shape) for o, a in zip(outs, self.shards)]


class _ExchangeCargo:
    def __init__(self, sums, whole):
        self.inputs, self.whole, n = sums, whole, len(sums)
        self.out_shape = [jax.ShapeDtypeStruct((N_CHIPS - 1,) + g.shape[1:], g.dtype) for g in sums]
        sem = pltpu.SemaphoreType.DMA((N_CHIPS - 1, n))
        self.sems = [sem, sem]

    def _copies(self, src, dst, sems):
        send_sems, recv_sems = sems
        p = _chip_index()
        return [pltpu.make_async_remote_copy(
            src_ref=src[a].at[0] if self.whole[a] else src[a].at[p ^ k], dst_ref=dst[a].at[k - 1],
            send_sem=send_sems.at[k - 1, a], recv_sem=recv_sems.at[k - 1, a], device_id=_peer(k), device_id_type=MESH)
            for k in range(1, N_CHIPS) for a in range(len(self.inputs))]

    def stages(self, steps):
        def start(src, dst, sems):
            for cp in self._copies(src, dst, sems):
                cp.start()

        def finish(src, dst, sems):
            copies = self._copies(src, dst, sems)
            for cp in copies:
                cp.wait_recv()
            for cp in copies:
                cp.wait_send()

        return [(0, start), (steps - 1, finish)]

    def results(self, outs):
        return list(outs)


def _cargo_call(body, cargo, *, name, grid, in_specs, out_specs, out_shape, scratch=()):
    n_in, n_out, n_scr = len(in_specs), len(out_specs), len(scratch)
    c_in, c_out = len(cargo.inputs), len(cargo.out_shape)
    steps = 1
    for g in grid:
        steps *= g
    stages = cargo.stages(steps)

    def wrapped(*refs):
        ins, refs = refs[:n_in], refs[n_in:]
        cin, refs = refs[:c_in], refs[c_in:]
        outs, refs = refs[:n_out], refs[n_out:]
        cout, refs = refs[:c_out], refs[c_out:]
        scr, sems = refs[:n_scr], refs[n_scr:]
        if not grid:
            for _, act in stages:
                act(cin, cout, sems)
            return
        step = 0
        for d in range(len(grid)):
            step = step * grid[d] + pl.program_id(d)
        pl.when(step == 0)(functools.partial(stages[0][1], cin, cout, sems))
        body(*ins, *outs, *scr)
        for at, act in stages[1:]:
            pl.when(step == at)(functools.partial(act, cin, cout, sems))

    params = dict(vmem_limit_bytes=VMEM_LIMIT)
    if grid:
        params["dimension_semantics"] = ("arbitrary",) * len(grid)
    res = pl.pallas_call(
        wrapped, name=name, grid=grid, in_specs=list(in_specs) + [ANY] * c_in, out_specs=list(out_specs) + [ANY] * c_out,
        out_shape=list(out_shape) + cargo.out_shape, scratch_shapes=list(scratch) + cargo.sems,
        compiler_params=pltpu.CompilerParams(**params))
    return lambda *args: (lambda r: (r[:n_out], cargo.results(r[n_out:])))(res(*args, *cargo.inputs))


def _split_halves(g):
    return g.reshape(g.shape[0], 2, g.shape[1] // 2, g.shape[2])


def _ew_tile(rows):
    return 256 if rows % 256 == 0 else rows


def _prefetch_call(body, *, name, grid, in_specs, out_specs, out_shape, **kw):
    return pl.pallas_call(
        body, name=name, out_shape=out_shape,
        grid_spec=pltpu.PrefetchScalarGridSpec(num_scalar_prefetch=1, grid=grid, in_specs=in_specs, out_specs=out_specs),
        compiler_params=pltpu.CompilerParams(dimension_semantics=("arbitrary",) * len(grid),
                                             vmem_limit_bytes=VMEM_LIMIT), **kw)


def _presum_swap(split, name):
    n = len(split)

    def body(*refs):
        src, got = refs[:n], refs[n:2 * n]
        send_sems, recv_sems = refs[2 * n:]
        c = lax.axis_index("c")
        remote = [pltpu.make_async_remote_copy(src_ref=src[a].at[:, 1 - c], dst_ref=got[a],
                                               send_sem=send_sems.at[a], recv_sem=recv_sems.at[a],
                                               device_id=_sibling(), device_id_type=MESH) for a in range(n)]
        for cp in remote:
            cp.start()
        for cp in remote:
            cp.wait_recv()
        for cp in remote:
            cp.wait_send()

    sem = pltpu.SemaphoreType.DMA((n,))
    return pl.pallas_call(
        body, name=name, in_specs=[ANY] * n, out_specs=[ANY] * n,
        out_shape=[jax.ShapeDtypeStruct((g.shape[0],) + g.shape[2:], g.dtype) for g in split],
        scratch_shapes=[sem, sem])(*split)


def _add_pair(mine, theirs, place, name):
    lead, _, rows, cols = mine.shape
    tm = _ew_tile(rows)

    def body(place_ref, a_ref, b_ref, o_ref):
        o_ref[...] = (a_ref[...].astype(f32) + b_ref[...].astype(f32)).astype(o_ref.dtype)

    blk = pl.BlockSpec((lead, tm, cols), lambda i, pc: (0, i, 0))
    return _prefetch_call(
        body, name=name, grid=(rows // tm,),
        in_specs=[pl.BlockSpec((lead, None, tm, cols), lambda i, pc: (0, pc[1], i, 0)), blk], out_specs=blk,
        out_shape=jax.ShapeDtypeStruct(theirs.shape, theirs.dtype))(place, mine, theirs)


class _NoCargo:
    def __init__(self, results=()):
        self.inputs, self.out_shape, self.sems, self._results = [], [], [], list(results)

    def stages(self, steps):
        return [(0, lambda src, dst, sems: None), (steps - 1, lambda src, dst, sems: None)]

    def results(self, outs):
        return self._results


def _cargo_alone(cargo, name):
    return _cargo_call(None, cargo, name=name, grid=(), in_specs=[], out_specs=[], out_shape=[])()[1]


def _sum_slots(own, recv, place, whole, name):
    _, rows, cols = own.shape
    tm = _ew_tile(rows)

    def body(place_ref, own_ref, r_ref, o_ref):
        r0, r1, r2 = [r_ref[k].astype(f32) for k in range(N_CHIPS - 1)]
        o_ref[...] = (own_ref[...].astype(f32) + r0) + (r1 + r2)

    return _prefetch_call(
        body, name=name, grid=(rows // tm,),
        in_specs=[pl.BlockSpec((None, tm, cols), lambda i, pc: (0 if whole else pc[0], i, 0)),
                  pl.BlockSpec((N_CHIPS - 1, tm, cols), lambda i, pc: (0, i, 0))],
        out_specs=pl.BlockSpec((None, tm, cols), lambda i, pc: (pc[1], i, 0)),
        out_shape=jax.ShapeDtypeStruct((2, rows, cols), f32))(place, own, recv)


def _sibling_fill(bufs):
    n = len(bufs)

    def body(*refs):
        src, dst = refs[:n], refs[n:2 * n]
        send_sems, recv_sems = refs[2 * n:]
        c = lax.axis_index("c")
        remote = [pltpu.make_async_remote_copy(src_ref=src[a].at[c], dst_ref=dst[a].at[c], send_sem=send_sems.at[a],
                                               recv_sem=recv_sems.at[a], device_id=_sibling(), device_id_type=MESH)
                  for a in range(n)]
        for cp in remote:
            cp.start()
        for a in range(n):
            pltpu.make_async_remote_copy(src_ref=src[a].at[c], dst_ref=dst[a].at[1 - c], send_sem=send_sems.at[a],
                                         recv_sem=recv_sems.at[a], device_id=_sibling(), device_id_type=MESH).wait_recv()
        for cp in remote:
            cp.wait_send()

    sem = pltpu.SemaphoreType.DMA((n,))
    out = pl.pallas_call(
        body, name="sibling_fill", in_specs=[ANY] * n, out_specs=[ANY] * n,
        out_shape=[jax.ShapeDtypeStruct(b.shape, b.dtype) for b in bufs],
        input_output_aliases={a: a for a in range(n)}, scratch_shapes=[sem, sem])(*bufs)
    return [o.reshape(2 * b.shape[1], b.shape[2]) for o, b in zip(out, bufs)]


def _adamw(w, m, v, grad, name, tm):
    rows, cols = w.shape
    gcols = grad.shape[1]
    if tm == rows:
        gspec = pl.BlockSpec(grad.shape, lambda i: (0, 0))
    else:
        gspec = pl.BlockSpec((tm, gcols), lambda i: (i, 0))

    def body(w_ref, m_ref, v_ref, gin_ref, g_ref, d_ref, nm_ref, nv_ref):
        g = gin_ref[0:tm, 0:cols]
        w_, m_, v_ = w_ref[...], m_ref[...], v_ref[...]
        m_ = ADAM_B1 * m_ + (1.0 - ADAM_B1) * g
        v_ = ADAM_B2 * v_ + (1.0 - ADAM_B2) * jnp.square(g)
        m_hat = m_ / (1.0 - ADAM_B1 ** ADAM_STEP)
        v_hat = v_ / (1.0 - ADAM_B2 ** ADAM_STEP)
        g_ref[...] = g
        d_ref[...] = -ADAM_LR * (m_hat / (jnp.sqrt(v_hat) + ADAM_EPS) + ADAM_WD * w_)
        nm_ref[...] = m_
        nv_ref[...] = v_

    blk = pl.BlockSpec((tm, cols), lambda i: (i, 0))
    return _call(
        body, name=name, grid=(rows // tm,),
        in_specs=[blk, blk, blk, gspec], out_specs=[blk] * 4,
        out_shape=[jax.ShapeDtypeStruct((rows, cols), f32)] * 4)(w, m, v, grad)


def _rope_tables(s):
    half = HEAD_DIM // 2
    inv_freq = 1.0 / (10000.0 ** (jnp.arange(half, dtype=f32) / half))
    ang = jnp.arange(s, dtype=f32)[:, None] * inv_freq[None, :]
    cos, sin = jnp.cos(ang), jnp.sin(ang)
    return jnp.tile(cos, (1, LANES // half)), jnp.tile(jnp.concatenate([-sin, sin], axis=1), (1, LANES // HEAD_DIM))


def _local_step(x, target, win_g, branch_cargo, ffn_cargo, ffn_exchange, b_gate, w_pool, pool_scale, ln1_g, ln1_b,
                convb_g, ln2_g, ln2_b):
    s = x.shape[0]
    nb = s // MOBA_BLOCK
    cos_t, sin_t = _rope_tables(s)
    x_bf = x.astype(bf16)

    z, (wba_g, wbp_g, wout_g) = _inproj(x_bf, win_g, cos_t, sin_t, b_gate, branch_cargo)
    wba = wba_g.transpose(1, 0, 2).reshape(D_ATTN, D_MODEL)
    wbp = wbp_g.transpose(1, 0, 2).reshape(D_POOL, D_MODEL)
    wout = wout_g.reshape(D_MODEL, D_MODEL)
    (o_bf, lse), (wg_g, wu_g, wd_g, convw_g) = _attn_fwd(z, nb, ffn_cargo)
    wd = wd_g.reshape(D_FF_PAD, D_MODEL)
    pooled_bf, mixed, pm_bf = _pool_fwd(z, w_pool, pool_scale)
    m_bf = _branch_merge(o_bf, pm_bf, z, wba, wbp)
    x1_bf, xhat1, rstd1 = _out_ln1(m_bf, wout, x, ln1_g, ln1_b)
    a, up, hh_bf = _ffn_up(x1_bf, wg_g, wu_g, convw_g, convb_g)
    loss, dh2, dh2_bf, d_ln2_g, d_ln2_b = _ffn_down_ln2_loss(hh_bf, wd, xhat1, ln1_g, ln1_b, ln2_g, ln2_b, target)

    da_bf, dup_bf, dconv = _ffn_act_bwd(dh2_bf, wd_g, a, up, convw_g, convb_g)
    d_wg = _matmul_tn(da_bf, x1_bf, 1, "dw_ffn_gate").reshape(N_CHIPS, FF_PAD, D_MODEL)
    d_wu = _matmul_tn(dup_bf, x1_bf, 1, "dw_ffn_up").reshape(N_CHIPS, FF_PAD, D_MODEL)
    d_wd = _matmul_tn(hh_bf, dh2_bf, 1, "dw_ffn_down").reshape(N_CHIPS, FF_PAD, D_MODEL)
    dh1, dh1_bf, d_ln1_g, d_ln1_b = _ffn_in_bwd_ln1(da_bf, dup_bf, wg_g, wu_g, dh2, xhat1, rstd1, ln1_g)
    d_wout = _matmul_tn(m_bf, dh1_bf, 1, "dw_out").reshape(N_CHIPS, D_MODEL // N_CHIPS, D_MODEL)
    dzg_bf, dya_bf, dyp_bf, do_bf, dpm, d_bgate = _merge_bwd(dh1_bf, wout, o_bf, pm_bf, z, wba, wbp)
    d_wba = _matmul_tn(o_bf, dya_bf, N_CHIPS, "dw_branch_attn")
    d_wbp = _matmul_tn(pm_bf, dyp_bf, N_CHIPS, "dw_branch_pool")
    du_bf, d_wpool, d_pscale = _pool_bwd(dpm, mixed, pooled_bf, w_pool, pool_scale)
    (dq_bf, dk_bf, dv_bf), brought = _attn_bwd(z, o_bf, lse, do_bf, cos_t, sin_t, nb,
                                               ffn_exchange([d_wg, d_wu, d_wd, dconv]))
    dz_bf = jnp.concatenate([dq_bf, dk_bf, dv_bf, du_bf, dzg_bf], axis=1)
    d_win = _matmul_tn(x_bf, dz_bf, N_CHIPS, "dw_in")
    grad_x = _inproj_dx(dz_bf, win_g, dh1)

    small = [d_bgate, d_wpool, d_pscale, d_ln1_g, d_ln1_b, d_ln2_g, d_ln2_b, dconv[:, 3, :], loss[0:1, :]]
    return grad_x, [d_win, d_wba, d_wbp, d_wout], small, brought


SMALL_ROWS = 592


def _pack_small(parts):
    flat = jnp.concatenate([p.reshape(-1) for p in parts])
    return jnp.pad(flat, (0, SMALL_ROWS * LANES - flat.shape[0])).reshape(SMALL_ROWS, LANES)


def _unpack_small(packed, shapes):
    flat = packed.reshape(-1)
    out, off = [], 0
    for shp in shapes:
        n = 1
        for d in shp:
            n *= d
        out.append(flat[off:off + n].reshape(shp))
        off += n
    return out


def _pad_conv_b(cb):
    return jnp.pad(cb.reshape(N_CHIPS, FF_SHARD), ((0, 0), (0, FF_PAD - FF_SHARD)))


def kernel(x, w_in, b_gate, w_branch_attn, w_pool, pool_scale, w_branch_pool, w_out, ln1_g, ln1_b, w_ffn_gate, w_ffn_up, conv_w, conv_b, w_ffn_down, ln2_g, ln2_b, loss_target, m_w_in, m_b_gate, m_w_branch_attn, m_w_pool, m_pool_scale, m_w_branch_pool, m_w_out, m_ln1_g, m_ln1_b, m_w_ffn_gate, m_w_ffn_up, m_conv_w, m_conv_b, m_w_ffn_down, m_ln2_g, m_ln2_b, v_w_in, v_b_gate, v_w_branch_attn, v_w_pool, v_pool_scale, v_w_branch_pool, v_w_out, v_ln1_g, v_ln1_b, v_w_ffn_gate, v_w_ffn_up, v_conv_w, v_conv_b, v_w_ffn_down, v_ln2_g, v_ln2_b):
    pad_rows = lambda w: jnp.pad(w.astype(bf16), ((0, FF_PAD - FF_SHARD), (0, 0)))
    shards = [w_in[0].astype(bf16), w_branch_attn[0].astype(bf16), w_branch_pool[0].astype(bf16),
              w_out[0].astype(bf16), pad_rows(w_ffn_gate[0].T), pad_rows(w_ffn_up[0].T), pad_rows(w_ffn_down[0]),
              jnp.pad(conv_w[0], ((0, CONV_ROWS - 3), (0, FF_PAD - FF_SHARD)))]
    (win_g,) = _cargo_alone(_GatherCargo(shards[:1], [0.0]), "gather_w_in")
    branch_cargo = _GatherCargo(shards[1:4], [0.3, 0.4, 0.6])
    ffn_cargo = _GatherCargo(shards[4:], [0.35, 0.62, 0.88, 0.9])

    place = jnp.stack([2 * lax.axis_index("x") + lax.axis_index("y"), lax.axis_index("c")]).astype(jnp.int32)

    def chip_sums(grads, names, group):
        split = [_split_halves(g) for g in grads]
        got = _presum_swap(split, "presum_swap_" + group)
        return [_add_pair(a, b, place, "presum_" + nm) for a, b, nm in zip(split, got, names)]

    ffn_names, rest_names = ["w_ffn_gate", "w_ffn_up", "w_ffn_down", "conv_w"], ["w_in", "w_branch_attn", "w_branch_pool", "w_out", "small"]
    ffn_sums = []

    def ffn_exchange(grads):
        ffn_sums.extend(chip_sums(grads, ffn_names, "ffn"))
        return _ExchangeCargo(ffn_sums, [False] * 4)

    convb_g = _pad_conv_b(conv_b).reshape(N_CHIPS, 1, FF_PAD)
    grad_x, rest, small, ffn_slots = _local_step(
        x[0], loss_target[0], win_g, branch_cargo, ffn_cargo, ffn_exchange, b_gate, w_pool[0], pool_scale, ln1_g, ln1_b,
        convb_g, ln2_g, ln2_b)

    rest_whole = [False] * 4 + [True]
    rest_sums = chip_sums(rest + [_pack_small(small)[None]], rest_names, "rest")
    rest_slots = _cargo_alone(_ExchangeCargo(rest_sums, rest_whole), "exchange_rest")
    names = rest_names[:4] + ffn_names + rest_names[4:]
    sums, slots = rest_sums[:4] + ffn_sums + rest_sums[4:], rest_slots[:4] + ffn_slots + rest_slots[4:]
    whole = [False] * 8 + [True]
    grads = _sibling_fill([_sum_slots(s, r, place, w, "sum_" + nm) for s, r, w, nm in zip(sums, slots, whole, names)])

    weights = [w_in[0], w_branch_attn[0], w_branch_pool[0], w_out[0], w_ffn_gate[0].T, w_ffn_up[0].T, w_ffn_down[0], conv_w[0]]
    m_in = [m_w_in[0], m_w_branch_attn[0], m_w_branch_pool[0], m_w_out[0], m_w_ffn_gate[0].T, m_w_ffn_up[0].T, m_w_ffn_down[0], m_conv_w[0]]
    v_in = [v_w_in[0], v_w_branch_attn[0], v_w_branch_pool[0], v_w_out[0], v_w_ffn_gate[0].T, v_w_ffn_up[0].T, v_w_ffn_down[0], v_conv_w[0]]
    tiles = [256, 256, 256, 128, 88, 88, 88, 3]
    res = {}
    for i, nm in enumerate(names[:-1]):
        outs4 = _adamw(weights[i], m_in[i], v_in[i], grads[i], "adamw_" + nm, tiles[i])
        res[nm] = [(r.T if nm in ("w_ffn_gate", "w_ffn_up") else r)[None] for r in outs4]
    loss = grads[-1][SMALL_ROWS - 4, 0]

    small_names = ["b_gate", "w_pool", "pool_scale", "ln1_g", "ln1_b", "ln2_g", "ln2_b", "conv_b"]
    small_w = [b_gate, w_pool, pool_scale, ln1_g, ln1_b, ln2_g, ln2_b, _pad_conv_b(conv_b)]
    small_m = [m_b_gate, m_w_pool, m_pool_scale, m_ln1_g, m_ln1_b, m_ln2_g, m_ln2_b, _pad_conv_b(m_conv_b)]
    small_v = [v_b_gate, v_w_pool, v_pool_scale, v_ln1_g, v_ln1_b, v_ln2_g, v_ln2_b, _pad_conv_b(v_conv_b)]
    packed = _adamw(_pack_small(small_w), _pack_small(small_m), _pack_small(small_v), grads[-1],
                    "adamw_small", SMALL_ROWS)
    shapes = [w.shape for w in small_w]
    for kind in range(4):
        for nm, val in zip(small_names, _unpack_small(packed[kind], shapes)):
            if nm == "conv_b":
                val = val[:, :FF_SHARD].reshape(1, 4 * FF_SHARD)
            res.setdefault(nm, [None] * 4)[kind] = val

    order = ["w_in", "b_gate", "w_branch_attn", "w_pool", "pool_scale", "w_branch_pool", "w_out", "ln1_g", "ln1_b",
             "w_ffn_gate", "w_ffn_up", "conv_w", "conv_b", "w_ffn_down", "ln2_g", "ln2_b"]
    outs = [loss, grad_x[None]]
    for kind in range(4):
        outs += [res[nm][kind] for nm in order]
    return tuple(outs)
```

```python
import functools

import jax
import jax.numpy as jnp
from jax import lax
from jax.experimental import pallas as pl
from jax.experimental.pallas import tpu as pltpu

D_MODEL = 1024
HEAD_DIM = 64
D_ATTN = 512
D_POOL = 512
MOBA_BLOCK = 256
MOBA_TOPK = 3
POOL_GROUP = 128
MAX_WINDOW = 16
FF_SHARD = 704
FF_PAD = 768
D_FF_PAD = 4 * FF_PAD
N_CHIPS = 4
LANES = 128
ALPHA = (2.0 * 1) ** 0.25
LN_EPS = 1e-5
NEG = -1e30
SCALE = HEAD_DIM ** -0.5
ADAM_LR, ADAM_B1, ADAM_B2, ADAM_EPS, ADAM_WD, ADAM_STEP = 0.001, 0.9, 0.999, 1e-08, 0.01, 10
VMEM_LIMIT = 56 * 1024 * 1024
MESH = pl.DeviceIdType.MESH

bf16 = jnp.bfloat16
f32 = jnp.float32


def _dot(a, b):
    return jnp.dot(a, b, preferred_element_type=f32)


def _dot_nt(a, b):
    return lax.dot_general(a, b, (((1,), (1,)), ((), ())), preferred_element_type=f32)


def _dot_tn(a, b):
    return lax.dot_general(a, b, (((0,), (0,)), ((), ())), preferred_element_type=f32)


def _call(body, *, name, grid, in_specs, out_specs, out_shape, scratch=(), cargo=None):
    if cargo is not None:
        return _cargo_call(body, cargo, name=name, grid=grid, in_specs=in_specs, out_specs=out_specs,
                           out_shape=out_shape, scratch=scratch)
    return pl.pallas_call(
        body, name=name, grid=grid, in_specs=in_specs, out_specs=out_specs, out_shape=out_shape,
        scratch_shapes=list(scratch),
        compiler_params=pltpu.CompilerParams(dimension_semantics=("arbitrary",) * len(grid),
                                             vmem_limit_bytes=VMEM_LIMIT))


def _rot_half(t):
    lane = lax.broadcasted_iota(jnp.int32, t.shape, 1)
    first = (lane % HEAD_DIM) < (HEAD_DIM // 2)
    return jnp.where(first, pltpu.roll(t, LANES - HEAD_DIM // 2, 1), pltpu.roll(t, HEAD_DIM // 2, 1))


def _rope(t, cos, sin_signed):
    return t * cos + _rot_half(t) * sin_signed


def _rope_bwd(d, cos, sin_signed):
    return d * cos + _rot_half(d * sin_signed)


def _gelu_parts(a):
    cdf = 0.5 * (1.0 + lax.erf(a * (2.0 ** -0.5)))
    pdf = jnp.exp(-0.5 * a * a) * ((2.0 * jnp.pi) ** -0.5)
    return a * cdf, cdf + a * pdf


def _layer_norm(h, g, b):
    mu = jnp.mean(h, axis=-1, keepdims=True)
    xc = h - mu
    var = jnp.mean(xc * xc, axis=-1, keepdims=True)
    rstd = lax.rsqrt(var + LN_EPS)
    xhat = xc * rstd
    return xhat * g + b, xhat, rstd


def _layer_norm_bwd(dy, xhat, rstd, g):
    dxh = dy * g
    m1 = jnp.mean(dxh, axis=-1, keepdims=True)
    m2 = jnp.mean(dxh * xhat, axis=-1, keepdims=True)
    return rstd * (dxh - m1 - xhat * m2)


def _inproj(x_bf, win_g, cos_t, sin_t, b_gate, cargo):
    s = x_bf.shape[0]
    tm, tn = 512, 512

    def body(x_ref, w_ref, cos_ref, sin_ref, b_ref, o_ref):
        j = pl.program_id(1)
        acc = _dot(x_ref[...], w_ref[...])

        @pl.when(j < 2)
        def _():
            for c in range(tn // LANES):
                sl = slice(c * LANES, (c + 1) * LANES)
                o_ref[:, sl] = _rope(acc[:, sl], cos_ref[...], sin_ref[...])

        @pl.when((j >= 2) & (j < 4))
        def _():
            o_ref[...] = acc

        @pl.when(j >= 4)
        def _():
            o_ref[...] = jax.nn.sigmoid(acc + b_ref[...])

    (z,), carried = _cargo_call(
        body, cargo, name="inproj", grid=(s // tm, 4 * D_MODEL // tn),
        in_specs=[pl.BlockSpec((tm, D_MODEL), lambda i, j: (i, 0)),
                  pl.BlockSpec((None, D_MODEL, tn), lambda i, j: (j // 2, 0, j % 2)),
                  pl.BlockSpec((tm, LANES), lambda i, j: (i, 0)),
                  pl.BlockSpec((tm, LANES), lambda i, j: (i, 0)),
                  pl.BlockSpec((1, tn), lambda i, j: (0, jnp.maximum(j - 4, 0)))],
        out_specs=[pl.BlockSpec((tm, tn), lambda i, j: (i, j))],
        out_shape=[jax.ShapeDtypeStruct((s, 4 * D_MODEL), f32)])(x_bf, win_g, cos_t, sin_t, b_gate)
    return z, carried


STAT_ROWS = 8


def _pair_rows(v0, v1, fill):
    r = lax.broadcasted_iota(jnp.int32, (STAT_ROWS, v0.shape[1]), 0)
    return jnp.where(r == 0, v0, jnp.where(r == 1, v1, fill))


def _head_lanes(shape):
    lane = lax.broadcasted_iota(jnp.int32, shape, 1)
    return lane < HEAD_DIM, lane >= HEAD_DIM


def _head_rows(shape):
    row = lax.broadcasted_iota(jnp.int32, shape, 0)
    return row < HEAD_DIM, row >= HEAD_DIM


def _moba_select(q_bf, k_all, i, nb):
    k_mean = jnp.mean(k_all.reshape(nb, MOBA_BLOCK, LANES), axis=1)
    n_io = lax.broadcasted_iota(jnp.int32, (nb, MOBA_BLOCK), 0)
    past = n_io < i
    sels = []
    for head in _head_lanes((nb, LANES)):
        gate = _dot_nt(jnp.where(head, k_mean, 0.0).astype(bf16), q_bf)
        g = jnp.where(past, gate, NEG)
        rank = jnp.zeros((nb, MOBA_BLOCK), f32)
        for m in range(nb):
            gm = g[m:m + 1, :]
            rank = rank + jnp.where((gm > g) | ((gm == g) & (m < n_io)), 1.0, 0.0)
        sels.append(jnp.where(past & (rank < MOBA_TOPK), 1.0, 0.0))
    return sels


def _attn_fwd(z, nb, cargo):
    s = z.shape[0]

    def body(q_ref, k_ref, v_ref, o_ref, lse_ref, vt_s, m_s, l_s, acc_s):
        i = pl.program_id(1)

        @pl.when(i == 0)
        def _():
            for j in range(nb):
                vt_s[j] = v_ref[j * MOBA_BLOCK:(j + 1) * MOBA_BLOCK, :].T.astype(bf16)

        q = q_ref[...]
        sels = _moba_select(q.astype(bf16), k_ref[...], i, nb)
        qs = (q * SCALE).astype(bf16)
        k_heads = _head_lanes((MOBA_BLOCK, LANES))
        o_heads = _head_rows((LANES, MOBA_BLOCK))
        key = lax.broadcasted_iota(jnp.int32, (MOBA_BLOCK, MOBA_BLOCK), 0)
        qry = lax.broadcasted_iota(jnp.int32, (MOBA_BLOCK, MOBA_BLOCK), 1)

        def block(kj, vtj, allow, m_old, l_old):
            m_new, alpha, l_new, pv = [], [], [], None
            for h in range(2):
                sc = jnp.where(allow[h], _dot_nt(jnp.where(k_heads[h], kj, 0.0).astype(bf16), qs), NEG)
                mx = jnp.max(sc, axis=0, keepdims=True)
                mn = mx if m_old is None else jnp.maximum(m_old[h], mx)
                p = jnp.exp(sc - mn)
                lsum = jnp.sum(p, axis=0, keepdims=True)
                if m_old is None:
                    a = None
                else:
                    a = jnp.exp(m_old[h] - mn)
                    lsum = a * l_old[h] + lsum
                t = _dot(jnp.where(o_heads[h], vtj, jnp.zeros_like(vtj)), p.astype(bf16))
                pv = t if pv is None else pv + t
                m_new.append(mn)
                alpha.append(a)
                l_new.append(lsum)
            return m_new, alpha, l_new, pv

        own = pl.ds(pl.multiple_of(i * MOBA_BLOCK, MOBA_BLOCK), MOBA_BLOCK)
        m_new, _, l_new, pv = block(k_ref[own, :], vt_s[i], [key <= qry] * 2, None, None)
        m_s[...] = _pair_rows(m_new[0], m_new[1], 0.0)
        l_s[...] = _pair_rows(l_new[0], l_new[1], 1.0)
        acc_s[...] = pv

        for j in range(nb - 1):
            @pl.when(j < i)
            def _(j=j):
                m_old, l_old = m_s[...], l_s[...]
                m_new, alpha, l_new, pv = block(
                    k_ref[j * MOBA_BLOCK:(j + 1) * MOBA_BLOCK, :], vt_s[j], [sels[h][j:j + 1, :] > 0.0 for h in range(2)],
                    [m_old[0:1, :], m_old[1:2, :]], [l_old[0:1, :], l_old[1:2, :]])
                m_s[...] = _pair_rows(m_new[0], m_new[1], 0.0)
                l_s[...] = _pair_rows(l_new[0], l_new[1], 1.0)
                acc_s[...] = acc_s[...] * jnp.where(o_heads[0], alpha[0], alpha[1]) + pv

        l_fin = l_s[...]
        o_t = acc_s[...] / jnp.where(o_heads[0], l_fin[0:1, :], l_fin[1:2, :])
        o_ref[...] = o_t.T.astype(bf16)
        lse_ref[...] = m_s[...] + jnp.log(l_fin)

    stat = pltpu.VMEM((STAT_ROWS, MOBA_BLOCK), f32)
    return _cargo_call(
        body, cargo, name="attn_fwd", grid=(D_ATTN // LANES, nb),
        in_specs=[pl.BlockSpec((MOBA_BLOCK, LANES), lambda hp, i: (i, hp)),
                  pl.BlockSpec((s, LANES), lambda hp, i: (0, 4 + hp)),
                  pl.BlockSpec((s, LANES), lambda hp, i: (0, 8 + hp))],
        out_specs=[pl.BlockSpec((MOBA_BLOCK, LANES), lambda hp, i: (i, hp)),
                   pl.BlockSpec((None, STAT_ROWS, MOBA_BLOCK), lambda hp, i: (hp, 0, i))],
        out_shape=[jax.ShapeDtypeStruct((s, D_ATTN), bf16),
                   jax.ShapeDtypeStruct((D_ATTN // LANES, STAT_ROWS, s), f32)],
        scratch=[pltpu.VMEM((nb, LANES, MOBA_BLOCK), bf16), stat, stat,
                 pltpu.VMEM((LANES, MOBA_BLOCK), f32)])(z, z, z)


def _window_select(snaps, g):
    return jnp.where(g == 0, snaps[0], jnp.where(g == 1, snaps[1], jnp.where(g == 2, snaps[2], snaps[3])))


def _pool_fwd(z, w_pool, pool_scale):
    s = z.shape[0]

    def body(u_ref, w_ref, sc_ref, pooled_ref, mixed_ref, pm_ref, pad):
        g = pl.program_id(0)
        u = u_ref[...]
        pad[0:MAX_WINDOW, :] = jnp.zeros((MAX_WINDOW, POOL_GROUP), f32)
        pad[MAX_WINDOW:MAX_WINDOW + s, :] = u
        acc = u
        snaps = []
        for d in range(1, MAX_WINDOW):
            acc = acc + pad[MAX_WINDOW - d:MAX_WINDOW - d + s, :]
            if d + 1 in (2, 4, 8, 16):
                snaps.append(acc)
        win = _window_select(snaps, g)
        t = lax.broadcasted_iota(jnp.int32, (s, POOL_GROUP), 0)
        count = jnp.minimum(t + 1, jnp.left_shift(2, g)).astype(f32)
        pooled = (win / count - u).astype(bf16)
        mixed = _dot(pooled, w_ref[...].astype(bf16))
        pooled_ref[...] = pooled
        mixed_ref[...] = mixed
        pm_ref[...] = (mixed * sc_ref[...]).astype(bf16)

    blk = pl.BlockSpec((s, POOL_GROUP), lambda g: (0, g))
    return _call(
        body, name="pool_fwd", grid=(4,),
        in_specs=[pl.BlockSpec((s, POOL_GROUP), lambda g: (0, 12 + g)),
                  pl.BlockSpec((None, POOL_GROUP, POOL_GROUP), lambda g: (g, 0, 0)),
                  pl.BlockSpec((1, POOL_GROUP), lambda g: (0, g))],
        out_specs=[blk, blk, blk],
        out_shape=[jax.ShapeDtypeStruct((s, D_POOL), bf16), jax.ShapeDtypeStruct((s, D_POOL), f32),
                   jax.ShapeDtypeStruct((s, D_POOL), bf16)],
        scratch=[pltpu.VMEM((s + MAX_WINDOW, POOL_GROUP), f32)])(z, w_pool, pool_scale)


def _branch_merge(o_bf, pm_bf, z, wba, wbp):
    s = o_bf.shape[0]
    tm = 256

    def body(o_ref, pm_ref, ga_ref, gp_ref, wba_ref, wbp_ref, m_ref):
        ya = _dot(o_ref[...], wba_ref[...])
        yp = _dot(pm_ref[...], wbp_ref[...])
        m_ref[...] = (ga_ref[...] * ya + gp_ref[...] * yp).astype(bf16)

    full = lambda r, c: pl.BlockSpec((r, c), lambda i: (0, 0))
    return _call(
        body, name="branch_merge", grid=(s // tm,),
        in_specs=[pl.BlockSpec((tm, D_ATTN), lambda i: (i, 0)), pl.BlockSpec((tm, D_POOL), lambda i: (i, 0)),
                  pl.BlockSpec((tm, D_MODEL), lambda i: (i, 2)), pl.BlockSpec((tm, D_MODEL), lambda i: (i, 3)),
                  full(D_ATTN, D_MODEL), full(D_POOL, D_MODEL)],
        out_specs=pl.BlockSpec((tm, D_MODEL), lambda i: (i, 0)),
        out_shape=jax.ShapeDtypeStruct((s, D_MODEL), bf16))(o_bf, pm_bf, z, z, wba, wbp)


def _out_ln1(m_bf, wout, x, ln_g, ln_b):
    s = x.shape[0]
    tm = 256

    def body(m_ref, w_ref, x_ref, g_ref, b_ref, x1_ref, xhat_ref, rstd_ref):
        h = ALPHA * x_ref[...] + _dot(m_ref[...], w_ref[...])
        y, xhat, rstd = _layer_norm(h, g_ref[...], b_ref[...])
        x1_ref[...] = y.astype(bf16)
        xhat_ref[...] = xhat
        rstd_ref[...] = jnp.broadcast_to(rstd, (tm, LANES))

    row = pl.BlockSpec((tm, D_MODEL), lambda i: (i, 0))
    vec = pl.BlockSpec((1, D_MODEL), lambda i: (0, 0))
    return _call(
        body, name="out_ln1", grid=(s // tm,),
        in_specs=[row, pl.BlockSpec((D_MODEL, D_MODEL), lambda i: (0, 0)), row, vec, vec],
        out_specs=[row, row, pl.BlockSpec((tm, LANES), lambda i: (i, 0))],
        out_shape=[jax.ShapeDtypeStruct((s, D_MODEL), bf16), jax.ShapeDtypeStruct((s, D_MODEL), f32),
                   jax.ShapeDtypeStruct((s, LANES), f32)])(m_bf, wout, x, ln_g, ln_b)


FF_TILE = 256
FF_TILES_PER_SHARD = FF_PAD // FF_TILE
CONV_PAD = 8
CONV_ROWS = 16


def _ff_weight_spec(rows):
    return pl.BlockSpec((None, rows, FF_TILE), lambda j: (j // FF_TILES_PER_SHARD, 0, j % FF_TILES_PER_SHARD))


def _ff_rows_spec():
    return pl.BlockSpec((None, FF_TILE, D_MODEL), lambda j: (j // FF_TILES_PER_SHARD, j % FF_TILES_PER_SHARD, 0))


def _conv_taps(pad, s, cw_ref, off=0):
    return (pad[CONV_PAD + off:CONV_PAD + off + s, :] * cw_ref[2:3, :]
            + pad[CONV_PAD - 1 + off:CONV_PAD - 1 + off + s, :] * cw_ref[1:2, :]
            + pad[CONV_PAD - 2 + off:CONV_PAD - 2 + off + s, :] * cw_ref[0:1, :])


def _ffn_up(x1_bf, wg_g, wu_g, convw_g, convb_g, cargo):
    s = x1_bf.shape[0]

    def body(x_ref, wg_ref, wu_ref, cw_ref, cb_ref, a_ref, up_ref, hh_ref, pad):
        x1 = x_ref[...]
        a = _dot_nt(x1, wg_ref[...])
        up = _dot_nt(x1, wu_ref[...])
        a_ref[...] = a
        up_ref[...] = up
        pad[0:CONV_PAD, :] = jnp.zeros((CONV_PAD, FF_TILE), f32)
        pad[CONV_PAD:CONV_PAD + s, :] = a
        ac = _conv_taps(pad, s, cw_ref) + cb_ref[...]
        hg, _ = _gelu_parts(ac)
        hh_ref[...] = (hg * up).astype(bf16)

    col = pl.BlockSpec((s, FF_TILE), lambda j: (0, j))
    return _call(
        body, name="ffn_up", grid=(D_FF_PAD // FF_TILE,),
        in_specs=[pl.BlockSpec((s, D_MODEL), lambda j: (0, 0)), _ff_rows_spec(), _ff_rows_spec(),
                  _ff_weight_spec(8), _ff_weight_spec(1)],
        out_specs=[col, col, col],
        out_shape=[jax.ShapeDtypeStruct((s, D_FF_PAD), f32), jax.ShapeDtypeStruct((s, D_FF_PAD), f32),
                   jax.ShapeDtypeStruct((s, D_FF_PAD), bf16)],
        scratch=[pltpu.VMEM((s + CONV_PAD, FF_TILE), f32)], cargo=cargo)(x1_bf, wg_g, wu_g, convw_g, convb_g)


def _ffn_down_ln2_loss(hh_bf, wd, xhat1, ln1_g, ln1_b, ln2_g, ln2_b, target):
    s = hh_bf.shape[0]
    tm = 256

    def body(hh_ref, w_ref, xh1_ref, g1_ref, b1_ref, g2_ref, b2_ref, t_ref, loss_ref, dh_ref, dhb_ref, dg_ref, db_ref):
        i = pl.program_id(0)
        x1 = xh1_ref[...] * g1_ref[...] + b1_ref[...]
        h = ALPHA * x1 + _dot(hh_ref[...], w_ref[...])
        y, xhat, rstd = _layer_norm(h, g2_ref[...], b2_ref[...])
        err = y - t_ref[...]
        part = 0.5 * jnp.sum(jnp.mean(err * err, axis=-1, keepdims=True), axis=0, keepdims=True)
        dy = err * (1.0 / D_MODEL)

        @pl.when(i == 0)
        def _():
            loss_ref[...] = jnp.zeros_like(loss_ref)
            dg_ref[...] = jnp.zeros_like(dg_ref)
            db_ref[...] = jnp.zeros_like(db_ref)

        loss_ref[...] += jnp.broadcast_to(part, loss_ref.shape)
        dg_ref[...] += jnp.sum(dy * xhat, axis=0, keepdims=True)
        db_ref[...] += jnp.sum(dy, axis=0, keepdims=True)
        dh = _layer_norm_bwd(dy, xhat, rstd, g2_ref[...])
        dh_ref[...] = dh
        dhb_ref[...] = dh.astype(bf16)

    row = pl.BlockSpec((tm, D_MODEL), lambda i: (i, 0))
    vec = pl.BlockSpec((1, D_MODEL), lambda i: (0, 0))
    return _call(
        body, name="ffn_down_ln2_loss", grid=(s // tm,),
        in_specs=[pl.BlockSpec((tm, D_FF_PAD), lambda i: (i, 0)), pl.BlockSpec((D_FF_PAD, D_MODEL), lambda i: (0, 0)),
                  row, vec, vec, vec, vec, row],
        out_specs=[pl.BlockSpec((8, LANES), lambda i: (0, 0)), row, row, vec, vec],
        out_shape=[jax.ShapeDtypeStruct((8, LANES), f32), jax.ShapeDtypeStruct((s, D_MODEL), f32),
                   jax.ShapeDtypeStruct((s, D_MODEL), bf16), jax.ShapeDtypeStruct((1, D_MODEL), f32),
                   jax.ShapeDtypeStruct((1, D_MODEL), f32)])(hh_bf, wd, xhat1, ln1_g, ln1_b, ln2_g, ln2_b, target)


def _ffn_act_bwd(dh2_bf, wd_g, a, up, convw_g, convb_g):
    s = dh2_bf.shape[0]

    def body(dh_ref, wd_ref, a_ref, up_ref, cw_ref, cb_ref, da_ref, dup_ref, dc_ref, pad, dpad):
        dhh = _dot_nt(dh_ref[...], wd_ref[...])
        zeros = jnp.zeros((CONV_PAD, FF_TILE), f32)
        pad[0:CONV_PAD, :] = zeros
        pad[CONV_PAD:CONV_PAD + s, :] = a_ref[...]
        ac = _conv_taps(pad, s, cw_ref) + cb_ref[...]
        hg, dgelu = _gelu_parts(ac)
        up = up_ref[...]
        dup_ref[...] = (dhh * hg).astype(bf16)
        dac = dhh * up * dgelu
        rows = [jnp.sum(dac * pad[CONV_PAD - 2 + i:CONV_PAD - 2 + i + s, :], axis=0, keepdims=True) for i in range(3)]
        rows.append(jnp.sum(dac, axis=0, keepdims=True))
        rows.append(jnp.zeros((CONV_ROWS - 4, FF_TILE), f32))
        dc_ref[...] = jnp.concatenate(rows, axis=0)
        dpad[0:s, :] = dac
        dpad[s:s + CONV_PAD, :] = zeros
        da = (dpad[0:s, :] * cw_ref[2:3, :] + dpad[1:1 + s, :] * cw_ref[1:2, :] + dpad[2:2 + s, :] * cw_ref[0:1, :])
        da_ref[...] = da.astype(bf16)

    col = pl.BlockSpec((s, FF_TILE), lambda j: (0, j))
    return _call(
        body, name="ffn_act_bwd", grid=(D_FF_PAD // FF_TILE,),
        in_specs=[pl.BlockSpec((s, D_MODEL), lambda j: (0, 0)), _ff_rows_spec(),
                  col, col, _ff_weight_spec(8), _ff_weight_spec(1)],
        out_specs=[col, col, _ff_weight_spec(CONV_ROWS)],
        out_shape=[jax.ShapeDtypeStruct((s, D_FF_PAD), bf16), jax.ShapeDtypeStruct((s, D_FF_PAD), bf16),
                   jax.ShapeDtypeStruct((N_CHIPS, CONV_ROWS, FF_PAD), f32)],
        scratch=[pltpu.VMEM((s + CONV_PAD, FF_TILE), f32), pltpu.VMEM((s + CONV_PAD, FF_TILE), f32)],
    )(dh2_bf, wd_g, a, up, convw_g, convb_g)


def _matmul_tn(a, b, n_shards, name, tm=512, tn=256, cargo=None):
    k, m = a.shape
    n = b.shape[1]
    tm, tn = min(tm, m), min(tn, n // n_shards)
    per = n // n_shards // tn

    def body(a_ref, b_ref, o_ref):
        o_ref[...] = _dot_tn(a_ref[...], b_ref[...]).astype(bf16)

    res = _call(
        body, name=name, grid=(m // tm, n // tn),
        in_specs=[pl.BlockSpec((k, tm), lambda i, j: (0, i)), pl.BlockSpec((k, tn), lambda i, j: (0, j))],
        out_specs=[pl.BlockSpec((None, tm, tn), lambda i, j: (j // per, i, j % per))],
        out_shape=[jax.ShapeDtypeStruct((n_shards, m, n // n_shards), bf16)], cargo=cargo)(a, b)
    return res[0] if cargo is None else (res[0][0], res[1])


def _ffn_in_bwd_ln1(da_bf, dup_bf, wg_g, wu_g, dh2, xhat1, rstd1, ln1_g, cargo):
    s = da_bf.shape[0]
    tm = 256

    def body(da_ref, dup_ref, wg_ref, wu_ref, dh2_ref, xh_ref, rstd_ref, g_ref, dh_ref, dhb_ref, dg_ref, db_ref):
        i = pl.program_id(0)
        dx1 = ALPHA * dh2_ref[...]
        for sh in range(N_CHIPS):
            sl = slice(sh * FF_PAD, (sh + 1) * FF_PAD)
            dx1 = dx1 + _dot(da_ref[:, sl], wg_ref[sh]) + _dot(dup_ref[:, sl], wu_ref[sh])
        xhat = xh_ref[...]

        @pl.when(i == 0)
        def _():
            dg_ref[...] = jnp.zeros_like(dg_ref)
            db_ref[...] = jnp.zeros_like(db_ref)

        dg_ref[...] += jnp.sum(dx1 * xhat, axis=0, keepdims=True)
        db_ref[...] += jnp.sum(dx1, axis=0, keepdims=True)
        dh = _layer_norm_bwd(dx1, xhat, rstd_ref[:, 0:1], g_ref[...])
        dh_ref[...] = dh
        dhb_ref[...] = dh.astype(bf16)

    row = pl.BlockSpec((tm, D_MODEL), lambda i: (i, 0))
    wide = pl.BlockSpec((tm, D_FF_PAD), lambda i: (i, 0))
    wfull = pl.BlockSpec((N_CHIPS, FF_PAD, D_MODEL), lambda i: (0, 0, 0))
    vec = pl.BlockSpec((1, D_MODEL), lambda i: (0, 0))
    return _call(
        body, name="ffn_in_bwd_ln1", grid=(s // tm,),
        in_specs=[wide, wide, wfull, wfull, row, row, pl.BlockSpec((tm, LANES), lambda i: (i, 0)), vec],
        out_specs=[row, row, vec, vec],
        out_shape=[jax.ShapeDtypeStruct((s, D_MODEL), f32), jax.ShapeDtypeStruct((s, D_MODEL), bf16),
                   jax.ShapeDtypeStruct((1, D_MODEL), f32), jax.ShapeDtypeStruct((1, D_MODEL), f32)],
        cargo=cargo)(da_bf, dup_bf, wg_g, wu_g, dh2, xhat1, rstd1, ln1_g)


def _merge_bwd(dh1_bf, wout, o_bf, pm_bf, z, wba, wbp):
    s = dh1_bf.shape[0]
    tm = 256

    def body(dh_ref, wout_ref, o_ref, pm_ref, ga_ref, gp_ref, wba_ref, wbp_ref,
             dzg_ref, dya_ref, dyp_ref, do_ref, dpm_ref, dbg_ref):
        i = pl.program_id(0)
        dm = _dot_nt(dh_ref[...], wout_ref[...])
        ya = _dot(o_ref[...], wba_ref[...])
        yp = _dot(pm_ref[...], wbp_ref[...])
        ga, gp = ga_ref[...], gp_ref[...]
        dza = dm * ya * ga * (1.0 - ga)
        dzp = dm * yp * gp * (1.0 - gp)

        @pl.when(i == 0)
        def _():
            dbg_ref[...] = jnp.zeros_like(dbg_ref)

        dbg_ref[:, 0:D_MODEL] += jnp.sum(dza, axis=0, keepdims=True)
        dbg_ref[:, D_MODEL:2 * D_MODEL] += jnp.sum(dzp, axis=0, keepdims=True)
        dzg_ref[:, 0:D_MODEL] = dza.astype(bf16)
        dzg_ref[:, D_MODEL:2 * D_MODEL] = dzp.astype(bf16)
        dya = (dm * ga).astype(bf16)
        dyp = (dm * gp).astype(bf16)
        dya_ref[...] = dya
        dyp_ref[...] = dyp
        do_ref[...] = _dot_nt(dya, wba_ref[...]).astype(bf16)
        dpm_ref[...] = _dot_nt(dyp, wbp_ref[...])

    row = pl.BlockSpec((tm, D_MODEL), lambda i: (i, 0))
    half = pl.BlockSpec((tm, D_ATTN), lambda i: (i, 0))
    full = lambda r, c: pl.BlockSpec((r, c), lambda i: (0, 0))
    return _call(
        body, name="merge_bwd", grid=(s // tm,),
        in_specs=[row, full(D_MODEL, D_MODEL), half, half,
                  pl.BlockSpec((tm, D_MODEL), lambda i: (i, 2)), pl.BlockSpec((tm, D_MODEL), lambda i: (i, 3)),
                  full(D_ATTN, D_MODEL), full(D_POOL, D_MODEL)],
        out_specs=[pl.BlockSpec((tm, 2 * D_MODEL), lambda i: (i, 0)), row, row, half, half,
                   pl.BlockSpec((1, 2 * D_MODEL), lambda i: (0, 0))],
        out_shape=[jax.ShapeDtypeStruct((s, 2 * D_MODEL), bf16), jax.ShapeDtypeStruct((s, D_MODEL), bf16),
                   jax.ShapeDtypeStruct((s, D_MODEL), bf16), jax.ShapeDtypeStruct((s, D_ATTN), bf16),
                   jax.ShapeDtypeStruct((s, D_POOL), f32), jax.ShapeDtypeStruct((1, 2 * D_MODEL), f32)],
    )(dh1_bf, wout, o_bf, pm_bf, z, z, wba, wbp)


def _pool_bwd(dpm, mixed, pooled_bf, w_pool, pool_scale):
    s = dpm.shape[0]

    def body(dpm_ref, mixed_ref, pooled_ref, w_ref, sc_ref, du_ref, dw_ref, dsc_ref, pad):
        g = pl.program_id(0)
        dpm_v = dpm_ref[...]
        dsc_ref[...] = jnp.sum(dpm_v * mixed_ref[...], axis=0, keepdims=True)
        dmixed = (dpm_v * sc_ref[...]).astype(bf16)
        dw_ref[...] = _dot_tn(pooled_ref[...], dmixed)
        dpooled = _dot_nt(dmixed, w_ref[...].astype(bf16))
        t = lax.broadcasted_iota(jnp.int32, (s, POOL_GROUP), 0)
        count = jnp.minimum(t + 1, jnp.left_shift(2, g)).astype(f32)
        r = dpooled / count
        pad[0:s, :] = r
        pad[s:s + MAX_WINDOW, :] = jnp.zeros((MAX_WINDOW, POOL_GROUP), f32)
        acc = r
        snaps = []
        for d in range(1, MAX_WINDOW):
            acc = acc + pad[d:d + s, :]
            if d + 1 in (2, 4, 8, 16):
                snaps.append(acc)
        du_ref[...] = (_window_select(snaps, g) - dpooled).astype(bf16)

    blk = pl.BlockSpec((s, POOL_GROUP), lambda g: (0, g))
    return _call(
        body, name="pool_bwd", grid=(4,),
        in_specs=[blk, blk, blk, pl.BlockSpec((None, POOL_GROUP, POOL_GROUP), lambda g: (g, 0, 0)),
                  pl.BlockSpec((1, POOL_GROUP), lambda g: (0, g))],
        out_specs=[blk, pl.BlockSpec((None, POOL_GROUP, POOL_GROUP), lambda g: (g, 0, 0)),
                   pl.BlockSpec((1, POOL_GROUP), lambda g: (0, g))],
        out_shape=[jax.ShapeDtypeStruct((s, D_POOL), bf16), jax.ShapeDtypeStruct((4, POOL_GROUP, POOL_GROUP), f32),
                   jax.ShapeDtypeStruct((1, D_POOL), f32)],
        scratch=[pltpu.VMEM((s + MAX_WINDOW, POOL_GROUP), f32)])(dpm, mixed, pooled_bf, w_pool, pool_scale)


def _attn_bwd(z, o_bf, lse, do_bf, cos_t, sin_t, nb, cargo):
    s = z.shape[0]

    def body(q_ref, k_ref, v_ref, o_ref, lse_ref, do_ref, cq_ref, sq_ref, cf_ref, sf_ref,
             dq_ref, dk_ref, dv_ref, kt_s, dqt_s, dk_s, dv_s):
        i = pl.program_id(1)

        @pl.when(i == 0)
        def _():
            dk_s[...] = jnp.zeros_like(dk_s)
            dv_s[...] = jnp.zeros_like(dv_s)
            for j in range(nb):
                kt_s[j] = k_ref[j * MOBA_BLOCK:(j + 1) * MOBA_BLOCK, :].T.astype(bf16)

        q = q_ref[...]
        sels = _moba_select(q.astype(bf16), k_ref[...], i, nb)
        qs = (q * SCALE).astype(bf16)
        k_heads = _head_lanes((MOBA_BLOCK, LANES))
        t_heads = _head_rows((LANES, MOBA_BLOCK))
        qraw = [jnp.where(hm, q, 0.0).astype(bf16) for hm in k_heads]
        do = do_ref[...].astype(f32)
        do_bf = do_ref[...]
        dob = [jnp.where(hm, do, 0.0).astype(bf16) for hm in k_heads]
        pick = _head_lanes((STAT_ROWS, LANES))
        r8 = lax.broadcasted_iota(jnp.int32, (STAT_ROWS, LANES), 0)
        head_pick = jnp.where(((r8 == 0) & pick[0]) | ((r8 == 1) & pick[1]), 1.0, 0.0)
        delta8 = lax.dot_general(head_pick, do * o_ref[...].astype(f32), (((1,), (1,)), ((), ())),
                                 precision=lax.Precision.HIGHEST, preferred_element_type=f32)
        lse8 = lse_ref[...]
        delta, lse_h = [delta8[0:1, :], delta8[1:2, :]], [lse8[0:1, :], lse8[1:2, :]]
        key = lax.broadcasted_iota(jnp.int32, (MOBA_BLOCK, MOBA_BLOCK), 0)
        qry = lax.broadcasted_iota(jnp.int32, (MOBA_BLOCK, MOBA_BLOCK), 1)
        dqt_s[...] = jnp.zeros_like(dqt_s)

        def block(rows, ktj, allow):
            kj, vj = k_ref[rows, :], v_ref[rows, :]
            dqt, dk, dv = None, None, None
            for h in range(2):
                sc = _dot_nt(jnp.where(k_heads[h], kj, 0.0).astype(bf16), qs)
                p = jnp.where(allow[h], jnp.exp(sc - lse_h[h]), 0.0)
                dp = _dot_nt(jnp.where(k_heads[h], vj, 0.0).astype(bf16), do_bf)
                ds = (p * (dp - delta[h]) * SCALE).astype(bf16)
                tv = _dot(p.astype(bf16), dob[h])
                tk = _dot(ds, qraw[h])
                tq = _dot(jnp.where(t_heads[h], ktj, jnp.zeros_like(ktj)), ds)
                dqt, dk, dv = (tq, tk, tv) if dqt is None else (dqt + tq, dk + tk, dv + tv)
            dqt_s[...] += dqt
            dk_s[rows, :] += dk
            dv_s[rows, :] += dv

        block(pl.ds(pl.multiple_of(i * MOBA_BLOCK, MOBA_BLOCK), MOBA_BLOCK), kt_s[i], [key <= qry] * 2)
        for j in range(nb - 1):
            @pl.when(j < i)
            def _(j=j):
                block(slice(j * MOBA_BLOCK, (j + 1) * MOBA_BLOCK), kt_s[j], [sels[h][j:j + 1, :] > 0.0 for h in range(2)])

        dq_ref[...] = _rope_bwd(dqt_s[...].T, cq_ref[...], sq_ref[...]).astype(bf16)

        @pl.when(i == nb - 1)
        def _():
            dk_ref[...] = _rope_bwd(dk_s[...], cf_ref[...], sf_ref[...]).astype(bf16)
            dv_ref[...] = dv_s[...].astype(bf16)

    qblk = pl.BlockSpec((MOBA_BLOCK, LANES), lambda hp, i: (i, hp))
    tq = pl.BlockSpec((MOBA_BLOCK, LANES), lambda hp, i: (i, 0))
    tf = pl.BlockSpec((s, LANES), lambda hp, i: (0, 0))
    colblk = pl.BlockSpec((s, LANES), lambda hp, i: (0, hp))
    return _cargo_call(
        body, cargo, name="attn_bwd", grid=(D_ATTN // LANES, nb),
        in_specs=[qblk, pl.BlockSpec((s, LANES), lambda hp, i: (0, 4 + hp)),
                  pl.BlockSpec((s, LANES), lambda hp, i: (0, 8 + hp)), qblk,
                  pl.BlockSpec((None, STAT_ROWS, MOBA_BLOCK), lambda hp, i: (hp, 0, i)), qblk, tq, tq, tf, tf],
        out_specs=[qblk, colblk, colblk],
        out_shape=[jax.ShapeDtypeStruct((s, D_ATTN), bf16)] * 3,
        scratch=[pltpu.VMEM((nb, LANES, MOBA_BLOCK), bf16), pltpu.VMEM((LANES, MOBA_BLOCK), f32),
                 pltpu.VMEM((s, LANES), f32), pltpu.VMEM((s, LANES), f32)],
    )(z, z, z, o_bf, lse, do_bf, cos_t, sin_t, cos_t, sin_t)


def _inproj_dx(dz_bf, win_g, dh1, cargo):
    s = dz_bf.shape[0]
    tm = 256

    def body(dz_ref, w_ref, dh_ref, gx_ref):
        acc = ALPHA * dh_ref[...]
        for sh in range(N_CHIPS):
            acc = acc + _dot_nt(dz_ref[:, sh * D_MODEL:(sh + 1) * D_MODEL], w_ref[sh])
        gx_ref[...] = acc

    row = pl.BlockSpec((tm, D_MODEL), lambda i: (i, 0))
    return _call(
        body, name="inproj_dx", grid=(s // tm,),
        in_specs=[pl.BlockSpec((tm, 4 * D_MODEL), lambda i: (i, 0)),
                  pl.BlockSpec((N_CHIPS, D_MODEL, D_MODEL), lambda i: (0, 0, 0)), row],
        out_specs=[row], out_shape=[jax.ShapeDtypeStruct((s, D_MODEL), f32)], cargo=cargo)(dz_bf, win_g, dh1)


ANY = pl.BlockSpec(memory_space=pl.ANY)


def _chip_index():
    return 2 * lax.axis_index("x") + lax.axis_index("y")


def _peer(k):
    x, y, c = lax.axis_index("x"), lax.axis_index("y"), lax.axis_index("c")
    return (x ^ (k >> 1), y ^ (k & 1), c)


def _sibling():
    return (lax.axis_index("x"), lax.axis_index("y"), 1 - lax.axis_index("c"))


class _GatherCargo:
    def __init__(self, shards, pass_on):
        self.shards, self.pass_on, n = shards, pass_on, len(shards)
        self.inputs = [a.reshape(2, a.shape[0] // 2, a.shape[1]) for a in shards]
        self.out_shape = [jax.ShapeDtypeStruct((N_CHIPS,) + a.shape, a.dtype) for a in self.inputs]
        sem, one = pltpu.SemaphoreType.DMA((N_CHIPS - 1, n)), pltpu.SemaphoreType.DMA((n,))
        self.sems = [sem, sem, sem, sem, one, one]

    def _copies(self, src, dst, sems, a):
        send_sems, recv_sems, fsend_sems, frecv_sems, local_sems, own_sems = sems
        p, c = _chip_index(), lax.axis_index("c")
        own = pltpu.make_async_remote_copy(src_ref=src[a], dst_ref=dst[a].at[p], send_sem=local_sems.at[a],
                                           recv_sem=own_sems.at[a], device_id=_sibling(), device_id_type=MESH)
        out, arrive, onward, landed = [], [], [], []
        for k in range(1, N_CHIPS):
            ici = dict(send_sem=send_sems.at[k - 1, a], recv_sem=recv_sems.at[k - 1, a], device_id=_peer(k),
                       device_id_type=MESH)
            d2d = dict(send_sem=fsend_sems.at[k - 1, a], recv_sem=frecv_sems.at[k - 1, a], device_id=_sibling(),
                       device_id_type=MESH)
            got, theirs = dst[a].at[p ^ k, c], dst[a].at[p ^ k, 1 - c]
            out.append(pltpu.make_async_remote_copy(src_ref=src[a].at[c], dst_ref=dst[a].at[p, c], **ici))
            arrive.append(pltpu.make_async_remote_copy(src_ref=src[a].at[c], dst_ref=got, **ici))
            onward.append(pltpu.make_async_remote_copy(src_ref=got, dst_ref=got, **d2d))
            landed.append(pltpu.make_async_remote_copy(src_ref=theirs, dst_ref=theirs, **d2d))
        return own, out, arrive, onward, landed

    def stages(self, steps):
        n = len(self.shards)

        def start(src, dst, sems):
            for a in range(n):
                own, out, _, _, _ = self._copies(src, dst, sems, a)
                own.start()
                for cp in out:
                    cp.start()

        def pass_on(a):
            def act(src, dst, sems):
                _, _, arrive, onward, _ = self._copies(src, dst, sems, a)
                for k in range(N_CHIPS - 1):
                    arrive[k].wait_recv()
                    onward[k].start()
            return act

        def finish(src, dst, sems):
            for a in range(n):
                own, out, _, onward, landed = self._copies(src, dst, sems, a)
                for cp in landed:
                    cp.wait_recv()
                for cp in out + onward:
                    cp.wait_send()
                own.wait()

        mids = [(min(steps - 1, int(self.pass_on[a] * steps)), pass_on(a)) for a in range(n)]
        return [(0, start)] + mids + [(steps - 1, finish)]

    def results(self, outs):
        return [o.reshape((N_CHIPS,) + a.shape) for o, a in zip(outs, self.shards)]


class _ExchangeCargo:
    def __init__(self, sums, whole):
        self.inputs, self.whole, n = sums, whole, len(sums)
        self.out_shape = [jax.ShapeDtypeStruct((N_CHIPS - 1,) + g.shape[1:], g.dtype) for g in sums]
        sem = pltpu.SemaphoreType.DMA((N_CHIPS - 1, n))
        self.sems = [sem, sem]

    def _copies(self, src, dst, sems):
        send_sems, recv_sems = sems
        p = _chip_index()
        return [pltpu.make_async_remote_copy(
            src_ref=src[a].at[0] if self.whole[a] else src[a].at[p ^ k], dst_ref=dst[a].at[k - 1],
            send_sem=send_sems.at[k - 1, a], recv_sem=recv_sems.at[k - 1, a], device_id=_peer(k), device_id_type=MESH)
            for k in range(1, N_CHIPS) for a in range(len(self.inputs))]

    def stages(self, steps):
        def start(src, dst, sems):
            for cp in self._copies(src, dst, sems):
                cp.start()

        def finish(src, dst, sems):
            copies = self._copies(src, dst, sems)
            for cp in copies:
                cp.wait_recv()
            for cp in copies:
                cp.wait_send()

        return [(0, start), (steps - 1, finish)]

    def results(self, outs):
        return list(outs)


def _cargo_call(body, cargo, *, name, grid, in_specs, out_specs, out_shape, scratch=()):
    n_in, n_out, n_scr = len(in_specs), len(out_specs), len(scratch)
    c_in, c_out = len(cargo.inputs), len(cargo.out_shape)
    steps = 1
    for g in grid:
        steps *= g
    stages = cargo.stages(steps)

    def wrapped(*refs):
        ins, refs = refs[:n_in], refs[n_in:]
        cin, refs = refs[:c_in], refs[c_in:]
        outs, refs = refs[:n_out], refs[n_out:]
        cout, refs = refs[:c_out], refs[c_out:]
        scr, sems = refs[:n_scr], refs[n_scr:]
        if not grid:
            for _, act in stages:
                act(cin, cout, sems)
            return
        step = 0
        for d in range(len(grid)):
            step = step * grid[d] + pl.program_id(d)
        pl.when(step == 0)(functools.partial(stages[0][1], cin, cout, sems))
        body(*ins, *outs, *scr)
        for at, act in stages[1:]:
            pl.when(step == at)(functools.partial(act, cin, cout, sems))

    params = dict(vmem_limit_bytes=VMEM_LIMIT)
    if grid:
        params["dimension_semantics"] = ("arbitrary",) * len(grid)
    res = pl.pallas_call(
        wrapped, name=name, grid=grid, in_specs=list(in_specs) + [ANY] * c_in, out_specs=list(out_specs) + [ANY] * c_out,
        out_shape=list(out_shape) + cargo.out_shape, scratch_shapes=list(scratch) + cargo.sems,
        compiler_params=pltpu.CompilerParams(**params))
    return lambda *args: (lambda r: (r[:n_out], cargo.results(r[n_out:])))(res(*args, *cargo.inputs))


def _split_halves(g):
    return g.reshape(g.shape[0], 2, g.shape[1] // 2, g.shape[2])


def _ew_tile(rows):
    return 256 if rows % 256 == 0 else rows


def _prefetch_call(body, *, name, grid, in_specs, out_specs, out_shape, **kw):
    return pl.pallas_call(
        body, name=name, out_shape=out_shape,
        grid_spec=pltpu.PrefetchScalarGridSpec(num_scalar_prefetch=1, grid=grid, in_specs=in_specs, out_specs=out_specs),
        compiler_params=pltpu.CompilerParams(dimension_semantics=("arbitrary",) * len(grid),
                                             vmem_limit_bytes=VMEM_LIMIT), **kw)


def _presum_swap(split, name):
    n = len(split)

    def body(*refs):
        src, got = refs[:n], refs[n:2 * n]
        send_sems, recv_sems = refs[2 * n:]
        c = lax.axis_index("c")
        remote = [pltpu.make_async_remote_copy(src_ref=src[a].at[:, 1 - c], dst_ref=got[a],
                                               send_sem=send_sems.at[a], recv_sem=recv_sems.at[a],
                                               device_id=_sibling(), device_id_type=MESH) for a in range(n)]
        for cp in remote:
            cp.start()
        for cp in remote:
            cp.wait_recv()
        for cp in remote:
            cp.wait_send()

    sem = pltpu.SemaphoreType.DMA((n,))
    return pl.pallas_call(
        body, name=name, in_specs=[ANY] * n, out_specs=[ANY] * n,
        out_shape=[jax.ShapeDtypeStruct((g.shape[0],) + g.shape[2:], g.dtype) for g in split],
        scratch_shapes=[sem, sem])(*split)


def _add_pair(mine, theirs, place, name):
    lead, _, rows, cols = mine.shape
    tm = _ew_tile(rows)

    def body(place_ref, a_ref, b_ref, o_ref):
        o_ref[...] = (a_ref[...].astype(f32) + b_ref[...].astype(f32)).astype(o_ref.dtype)

    blk = pl.BlockSpec((lead, tm, cols), lambda i, pc: (0, i, 0))
    return _prefetch_call(
        body, name=name, grid=(rows // tm,),
        in_specs=[pl.BlockSpec((lead, None, tm, cols), lambda i, pc: (0, pc[1], i, 0)), blk], out_specs=blk,
        out_shape=jax.ShapeDtypeStruct(theirs.shape, theirs.dtype))(place, mine, theirs)


class _NoCargo:
    def __init__(self, results=()):
        self.inputs, self.out_shape, self.sems, self._results = [], [], [], list(results)

    def stages(self, steps):
        return [(0, lambda src, dst, sems: None), (steps - 1, lambda src, dst, sems: None)]

    def results(self, outs):
        return self._results


def _cargo_alone(cargo, name):
    return _cargo_call(None, cargo, name=name, grid=(), in_specs=[], out_specs=[], out_shape=[])()[1]


def _sum_slots(own, recv, place, whole, name):
    _, rows, cols = own.shape
    tm = _ew_tile(rows)

    def body(place_ref, own_ref, r_ref, o_ref):
        r0, r1, r2 = [r_ref[k].astype(f32) for k in range(N_CHIPS - 1)]
        o_ref[...] = (own_ref[...].astype(f32) + r0) + (r1 + r2)

    return _prefetch_call(
        body, name=name, grid=(rows // tm,),
        in_specs=[pl.BlockSpec((None, tm, cols), lambda i, pc: (0 if whole else pc[0], i, 0)),
                  pl.BlockSpec((N_CHIPS - 1, tm, cols), lambda i, pc: (0, i, 0))],
        out_specs=pl.BlockSpec((None, tm, cols), lambda i, pc: (pc[1], i, 0)),
        out_shape=jax.ShapeDtypeStruct((2, rows, cols), f32))(place, own, recv)


def _sibling_fill(bufs):
    n = len(bufs)

    def body(*refs):
        src, dst = refs[:n], refs[n:2 * n]
        send_sems, recv_sems = refs[2 * n:]
        c = lax.axis_index("c")
        remote = [pltpu.make_async_remote_copy(src_ref=src[a].at[c], dst_ref=dst[a].at[c], send_sem=send_sems.at[a],
                                               recv_sem=recv_sems.at[a], device_id=_sibling(), device_id_type=MESH)
                  for a in range(n)]
        for cp in remote:
            cp.start()
        for a in range(n):
            pltpu.make_async_remote_copy(src_ref=src[a].at[c], dst_ref=dst[a].at[1 - c], send_sem=send_sems.at[a],
                                         recv_sem=recv_sems.at[a], device_id=_sibling(), device_id_type=MESH).wait_recv()
        for cp in remote:
            cp.wait_send()

    sem = pltpu.SemaphoreType.DMA((n,))
    out = pl.pallas_call(
        body, name="sibling_fill", in_specs=[ANY] * n, out_specs=[ANY] * n,
        out_shape=[jax.ShapeDtypeStruct(b.shape, b.dtype) for b in bufs],
        input_output_aliases={a: a for a in range(n)}, scratch_shapes=[sem, sem])(*bufs)
    return [o.reshape(2 * b.shape[1], b.shape[2]) for o, b in zip(out, bufs)]


def _adamw(w, m, v, grad, name, tm):
    rows, cols = w.shape
    gcols = grad.shape[1]
    if tm == rows:
        gspec = pl.BlockSpec(grad.shape, lambda i: (0, 0))
    else:
        gspec = pl.BlockSpec((tm, gcols), lambda i: (i, 0))

    def body(w_ref, m_ref, v_ref, gin_ref, g_ref, d_ref, nm_ref, nv_ref):
        g = gin_ref[0:tm, 0:cols]
        w_, m_, v_ = w_ref[...], m_ref[...], v_ref[...]
        m_ = ADAM_B1 * m_ + (1.0 - ADAM_B1) * g
        v_ = ADAM_B2 * v_ + (1.0 - ADAM_B2) * jnp.square(g)
        m_hat = m_ / (1.0 - ADAM_B1 ** ADAM_STEP)
        v_hat = v_ / (1.0 - ADAM_B2 ** ADAM_STEP)
        g_ref[...] = g
        d_ref[...] = -ADAM_LR * (m_hat / (jnp.sqrt(v_hat) + ADAM_EPS) + ADAM_WD * w_)
        nm_ref[...] = m_
        nv_ref[...] = v_

    blk = pl.BlockSpec((tm, cols), lambda i: (i, 0))
    return _call(
        body, name=name, grid=(rows // tm,),
        in_specs=[blk, blk, blk, gspec], out_specs=[blk] * 4,
        out_shape=[jax.ShapeDtypeStruct((rows, cols), f32)] * 4)(w, m, v, grad)


def _rope_tables(s):
    half = HEAD_DIM // 2
    inv_freq = 1.0 / (10000.0 ** (jnp.arange(half, dtype=f32) / half))
    ang = jnp.arange(s, dtype=f32)[:, None] * inv_freq[None, :]
    cos, sin = jnp.cos(ang), jnp.sin(ang)
    return jnp.tile(cos, (1, LANES // half)), jnp.tile(jnp.concatenate([-sin, sin], axis=1), (1, LANES // HEAD_DIM))


def _local_step(x, target, win_g, gather, exchange, b_gate, w_pool, pool_scale, ln1_g, ln1_b, convb_g, ln2_g, ln2_b):
    s = x.shape[0]
    nb = s // MOBA_BLOCK
    cos_t, sin_t = _rope_tables(s)
    x_bf = x.astype(bf16)

    z, (wba_g, wbp_g, wout_g, wg_g, convw_g) = _inproj(x_bf, win_g, cos_t, sin_t, b_gate, gather["inproj"])
    wba = wba_g.transpose(1, 0, 2).reshape(D_ATTN, D_MODEL)
    wbp = wbp_g.transpose(1, 0, 2).reshape(D_POOL, D_MODEL)
    wout = wout_g.reshape(D_MODEL, D_MODEL)
    (o_bf, lse), (wu_g,) = _attn_fwd(z, nb, gather["attn_fwd"])
    pooled_bf, mixed, pm_bf = _pool_fwd(z, w_pool, pool_scale)
    m_bf = _branch_merge(o_bf, pm_bf, z, wba, wbp)
    x1_bf, xhat1, rstd1 = _out_ln1(m_bf, wout, x, ln1_g, ln1_b)
    (a, up, hh_bf), (wd_g,) = _ffn_up(x1_bf, wg_g, wu_g, convw_g, convb_g, gather["ffn_up"])
    wd = wd_g.reshape(D_FF_PAD, D_MODEL)
    loss, dh2, dh2_bf, d_ln2_g, d_ln2_b = _ffn_down_ln2_loss(hh_bf, wd, xhat1, ln1_g, ln1_b, ln2_g, ln2_b, target)

    da_bf, dup_bf, dconv = _ffn_act_bwd(dh2_bf, wd_g, a, up, convw_g, convb_g)
    d_wd = _matmul_tn(hh_bf, dh2_bf, 1, "dw_ffn_down").reshape(N_CHIPS, FF_PAD, D_MODEL)
    d_wg = _matmul_tn(da_bf, x1_bf, 1, "dw_ffn_gate").reshape(N_CHIPS, FF_PAD, D_MODEL)
    d_wu = _matmul_tn(dup_bf, x1_bf, 1, "dw_ffn_up").reshape(N_CHIPS, FF_PAD, D_MODEL)
    (dh1, dh1_bf, d_ln1_g, d_ln1_b), got_down = _ffn_in_bwd_ln1(
        da_bf, dup_bf, wg_g, wu_g, dh2, xhat1, rstd1, ln1_g, exchange("ffn_in_bwd_ln1", [("w_ffn_down", d_wd)]))
    d_wout = _matmul_tn(m_bf, dh1_bf, 1, "dw_out").reshape(N_CHIPS, D_MODEL // N_CHIPS, D_MODEL)
    dzg_bf, dya_bf, dyp_bf, do_bf, dpm, d_bgate = _merge_bwd(dh1_bf, wout, o_bf, pm_bf, z, wba, wbp)
    d_wba = _matmul_tn(o_bf, dya_bf, N_CHIPS, "dw_branch_attn")
    d_wbp = _matmul_tn(pm_bf, dyp_bf, N_CHIPS, "dw_branch_pool")
    du_bf, d_wpool, d_pscale = _pool_bwd(dpm, mixed, pooled_bf, w_pool, pool_scale)
    small = _pack_small([d_bgate, d_wpool, d_pscale, d_ln1_g, d_ln1_b, d_ln2_g, d_ln2_b, dconv[:, 3, :],
                         loss[0:1, :]])[None]
    (dq_bf, dk_bf, dv_bf), got_ffn = _attn_bwd(
        z, o_bf, lse, do_bf, cos_t, sin_t, nb,
        exchange("attn_bwd", [("w_ffn_gate", d_wg), ("w_ffn_up", d_wu), ("conv_w", dconv), ("small", small)]))
    dz_bf = jnp.concatenate([dq_bf, dk_bf, dv_bf, du_bf, dzg_bf], axis=1)
    d_win, got_branch = _matmul_tn(
        x_bf, dz_bf, N_CHIPS, "dw_in",
        cargo=exchange("dw_in", [("w_out", d_wout), ("w_branch_attn", d_wba), ("w_branch_pool", d_wbp)]))
    (grad_x,), got_in = _inproj_dx(dz_bf, win_g, dh1, exchange("inproj_dx", [("w_in", d_win)]))
    return grad_x, got_down + got_ffn + got_branch + got_in


SMALL_ROWS = 592


def _pack_small(parts):
    flat = jnp.concatenate([p.reshape(-1) for p in parts])
    return jnp.pad(flat, (0, SMALL_ROWS * LANES - flat.shape[0])).reshape(SMALL_ROWS, LANES)


def _unpack_small(packed, shapes):
    flat = packed.reshape(-1)
    out, off = [], 0
    for shp in shapes:
        n = 1
        for d in shp:
            n *= d
        out.append(flat[off:off + n].reshape(shp))
        off += n
    return out


def _pad_conv_b(cb):
    return jnp.pad(cb.reshape(N_CHIPS, FF_SHARD), ((0, 0), (0, FF_PAD - FF_SHARD)))


def kernel(x, w_in, b_gate, w_branch_attn, w_pool, pool_scale, w_branch_pool, w_out, ln1_g, ln1_b, w_ffn_gate, w_ffn_up, conv_w, conv_b, w_ffn_down, ln2_g, ln2_b, loss_target, m_w_in, m_b_gate, m_w_branch_attn, m_w_pool, m_pool_scale, m_w_branch_pool, m_w_out, m_ln1_g, m_ln1_b, m_w_ffn_gate, m_w_ffn_up, m_conv_w, m_conv_b, m_w_ffn_down, m_ln2_g, m_ln2_b, v_w_in, v_b_gate, v_w_branch_attn, v_w_pool, v_pool_scale, v_w_branch_pool, v_w_out, v_ln1_g, v_ln1_b, v_w_ffn_gate, v_w_ffn_up, v_conv_w, v_conv_b, v_w_ffn_down, v_ln2_g, v_ln2_b):
    pad_rows = lambda w: jnp.pad(w.astype(bf16), ((0, FF_PAD - FF_SHARD), (0, 0)))
    shards = [w_in[0].astype(bf16), w_branch_attn[0].astype(bf16), w_branch_pool[0].astype(bf16),
              w_out[0].astype(bf16), pad_rows(w_ffn_gate[0].T), pad_rows(w_ffn_up[0].T), pad_rows(w_ffn_down[0]),
              jnp.pad(conv_w[0], ((0, CONV_ROWS - 3), (0, FF_PAD - FF_SHARD)))]
    (win_g,) = _cargo_alone(_GatherCargo(shards[:1], [0.0]), "gather_w_in")
    gather = {"inproj": _GatherCargo(shards[1:5] + shards[7:], [0.2, 0.3, 0.5, 0.97, 0.97]),
              "attn_fwd": _GatherCargo(shards[5:6], [0.5]), "ffn_up": _GatherCargo(shards[6:7], [0.8])}

    place = jnp.stack([2 * lax.axis_index("x") + lax.axis_index("y"), lax.axis_index("c")]).astype(jnp.int32)
    chip_sums = {}

    def exchange(carrier, named):
        split = [_split_halves(g) for _, g in named]
        got = _presum_swap(split, "presum_swap_" + carrier)
        sums = [_add_pair(a, b, place, "presum_" + nm) for a, b, (nm, _) in zip(split, got, named)]
        chip_sums.update({nm: s for (nm, _), s in zip(named, sums)})
        return _ExchangeCargo(sums, [nm == "small" for nm, _ in named])

    convb_g = _pad_conv_b(conv_b).reshape(N_CHIPS, 1, FF_PAD)
    grad_x, brought = _local_step(
        x[0], loss_target[0], win_g, gather, exchange, b_gate, w_pool[0], pool_scale, ln1_g, ln1_b, convb_g, ln2_g, ln2_b)
    slots = dict(zip(chip_sums, brought))
    names = ["w_in", "w_branch_attn", "w_branch_pool", "w_out", "w_ffn_gate", "w_ffn_up", "w_ffn_down", "conv_w", "small"]
    grads = _sibling_fill([_sum_slots(chip_sums[nm], slots[nm], place, nm == "small", "sum_" + nm) for nm in names])

    weights = [w_in[0], w_branch_attn[0], w_branch_pool[0], w_out[0], w_ffn_gate[0].T, w_ffn_up[0].T, w_ffn_down[0], conv_w[0]]
    m_in = [m_w_in[0], m_w_branch_attn[0], m_w_branch_pool[0], m_w_out[0], m_w_ffn_gate[0].T, m_w_ffn_up[0].T, m_w_ffn_down[0], m_conv_w[0]]
    v_in = [v_w_in[0], v_w_branch_attn[0], v_w_branch_pool[0], v_w_out[0], v_w_ffn_gate[0].T, v_w_ffn_up[0].T, v_w_ffn_down[0], v_conv_w[0]]
    tiles = [256, 256, 256, 128, 88, 88, 88, 3]
    res = {}
    for i, nm in enumerate(names[:-1]):
        outs4 = _adamw(weights[i], m_in[i], v_in[i], grads[i], "adamw_" + nm, tiles[i])
        res[nm] = [(r.T if nm in ("w_ffn_gate", "w_ffn_up") else r)[None] for r in outs4]
    loss = grads[-1][SMALL_ROWS - 4, 0]

    small_names = ["b_gate", "w_pool", "pool_scale", "ln1_g", "ln1_b", "ln2_g", "ln2_b", "conv_b"]
    small_w = [b_gate, w_pool, pool_scale, ln1_g, ln1_b, ln2_g, ln2_b, _pad_conv_b(conv_b)]
    small_m = [m_b_gate, m_w_pool, m_pool_scale, m_ln1_g, m_ln1_b, m_ln2_g, m_ln2_b, _pad_conv_b(m_conv_b)]
    small_v = [v_b_gate, v_w_pool, v_pool_scale, v_ln1_g, v_ln1_b, v_ln2_g, v_ln2_b, _pad_conv_b(v_conv_b)]
    packed = _adamw(_pack_small(small_w), _pack_small(small_m), _pack_small(small_v), grads[-1],
                    "adamw_small", SMALL_ROWS)
    shapes = [w.shape for w in small_w]
    for kind in range(4):
        for nm, val in zip(small_names, _unpack_small(packed[kind], shapes)):
            if nm == "conv_b":
                val = val[:, :FF_SHARD].reshape(1, 4 * FF_SHARD)
            res.setdefault(nm, [None] * 4)[kind] = val

    order = ["w_in", "b_gate", "w_branch_attn", "w_pool", "pool_scale", "w_branch_pool", "w_out", "ln1_g", "ln1_b",
             "w_ffn_gate", "w_ffn_up", "conv_w", "conv_b", "w_ffn_down", "ln2_g", "ln2_b"]
    outs = [loss, grad_x[None]]
    for kind in range(4):
        outs += [res[nm][kind] for nm in order]
    return tuple(outs)
```

```python
import functools

import jax
import jax.numpy as jnp
from jax import lax
from jax.experimental import pallas as pl
from jax.experimental.pallas import tpu as pltpu

D_MODEL = 1024
HEAD_DIM = 64
D_ATTN = 512
D_POOL = 512
MOBA_BLOCK = 256
MOBA_TOPK = 3
POOL_GROUP = 128
MAX_WINDOW = 16
FF_SHARD = 704
FF_PAD = 768
D_FF_PAD = 4 * FF_PAD
N_CHIPS = 4
LANES = 128
ALPHA = (2.0 * 1) ** 0.25
LN_EPS = 1e-5
NEG = -1e30
SCALE = HEAD_DIM ** -0.5
ADAM_LR, ADAM_B1, ADAM_B2, ADAM_EPS, ADAM_WD, ADAM_STEP = 0.001, 0.9, 0.999, 1e-08, 0.01, 10
VMEM_LIMIT = 56 * 1024 * 1024
MESH = pl.DeviceIdType.MESH

bf16 = jnp.bfloat16
f32 = jnp.float32


def _dot(a, b):
    return jnp.dot(a, b, preferred_element_type=f32)


def _dot_nt(a, b):
    return lax.dot_general(a, b, (((1,), (1,)), ((), ())), preferred_element_type=f32)


def _dot_tn(a, b):
    return lax.dot_general(a, b, (((0,), (0,)), ((), ())), preferred_element_type=f32)


def _call(body, *, name, grid, in_specs, out_specs, out_shape, scratch=(), cargo=None):
    if cargo is not None:
        return _cargo_call(body, cargo, name=name, grid=grid, in_specs=in_specs, out_specs=out_specs,
                           out_shape=out_shape, scratch=scratch)
    return pl.pallas_call(
        body, name=name, grid=grid, in_specs=in_specs, out_specs=out_specs, out_shape=out_shape,
        scratch_shapes=list(scratch),
        compiler_params=pltpu.CompilerParams(dimension_semantics=("arbitrary",) * len(grid),
                                             vmem_limit_bytes=VMEM_LIMIT))


def _rot_half(t):
    lane = lax.broadcasted_iota(jnp.int32, t.shape, 1)
    first = (lane % HEAD_DIM) < (HEAD_DIM // 2)
    return jnp.where(first, pltpu.roll(t, LANES - HEAD_DIM // 2, 1), pltpu.roll(t, HEAD_DIM // 2, 1))


def _rope(t, cos, sin_signed):
    return t * cos + _rot_half(t) * sin_signed


def _rope_bwd(d, cos, sin_signed):
    return d * cos + _rot_half(d * sin_signed)


def _gelu_parts(a):
    cdf = 0.5 * (1.0 + lax.erf(a * (2.0 ** -0.5)))
    pdf = jnp.exp(-0.5 * a * a) * ((2.0 * jnp.pi) ** -0.5)
    return a * cdf, cdf + a * pdf


def _layer_norm(h, g, b):
    mu = jnp.mean(h, axis=-1, keepdims=True)
    xc = h - mu
    var = jnp.mean(xc * xc, axis=-1, keepdims=True)
    rstd = lax.rsqrt(var + LN_EPS)
    xhat = xc * rstd
    return xhat * g + b, xhat, rstd


def _layer_norm_bwd(dy, xhat, rstd, g):
    dxh = dy * g
    m1 = jnp.mean(dxh, axis=-1, keepdims=True)
    m2 = jnp.mean(dxh * xhat, axis=-1, keepdims=True)
    return rstd * (dxh - m1 - xhat * m2)


def _inproj(x_bf, win_g, cos_t, sin_t, b_gate, cargo):
    s = x_bf.shape[0]
    tm, tn = 1024, 512

    def body(x_ref, w_ref, cos_ref, sin_ref, b_ref, o_ref):
        j = pl.program_id(1)
        acc = _dot(x_ref[...], w_ref[...])

        @pl.when(j < 2)
        def _():
            for c in range(tn // LANES):
                sl = slice(c * LANES, (c + 1) * LANES)
                o_ref[:, sl] = _rope(acc[:, sl], cos_ref[...], sin_ref[...])

        @pl.when((j >= 2) & (j < 4))
        def _():
            o_ref[...] = acc

        @pl.when(j >= 4)
        def _():
            o_ref[...] = jax.nn.sigmoid(acc + b_ref[...])

    (z,), carried = _cargo_call(
        body, cargo, name="inproj", grid=(s // tm, 4 * D_MODEL // tn),
        in_specs=[pl.BlockSpec((tm, D_MODEL), lambda i, j: (i, 0)),
                  pl.BlockSpec((None, D_MODEL, tn), lambda i, j: (j // 2, 0, j % 2)),
                  pl.BlockSpec((tm, LANES), lambda i, j: (i, 0)),
                  pl.BlockSpec((tm, LANES), lambda i, j: (i, 0)),
                  pl.BlockSpec((1, tn), lambda i, j: (0, jnp.maximum(j - 4, 0)))],
        out_specs=[pl.BlockSpec((tm, tn), lambda i, j: (i, j))],
        out_shape=[jax.ShapeDtypeStruct((s, 4 * D_MODEL), f32)])(x_bf, win_g, cos_t, sin_t, b_gate)
    return z, carried


STAT_ROWS = 8


def _pair_rows(v0, v1, fill):
    r = lax.broadcasted_iota(jnp.int32, (STAT_ROWS, v0.shape[1]), 0)
    return jnp.where(r == 0, v0, jnp.where(r == 1, v1, fill))


def _head_lanes(shape):
    lane = lax.broadcasted_iota(jnp.int32, shape, 1)
    return lane < HEAD_DIM, lane >= HEAD_DIM


def _head_rows(shape):
    row = lax.broadcasted_iota(jnp.int32, shape, 0)
    return row < HEAD_DIM, row >= HEAD_DIM


def _moba_select(q_bf, k_all, i, nb):
    k_mean = jnp.mean(k_all.reshape(nb, MOBA_BLOCK, LANES), axis=1)
    n_io = lax.broadcasted_iota(jnp.int32, (nb, MOBA_BLOCK), 0)
    past = n_io < i
    sels = []
    for head in _head_lanes((nb, LANES)):
        gate = _dot_nt(jnp.where(head, k_mean, 0.0).astype(bf16), q_bf)
        g = jnp.where(past, gate, NEG)
        rank = jnp.zeros((nb, MOBA_BLOCK), f32)
        for m in range(nb):
            gm = g[m:m + 1, :]
            rank = rank + jnp.where((gm > g) | ((gm == g) & (m < n_io)), 1.0, 0.0)
        sels.append(jnp.where(past & (rank < MOBA_TOPK), 1.0, 0.0))
    return sels


def _attn_fwd(z, nb, cargo):
    s = z.shape[0]

    def body(q_ref, k_ref, v_ref, o_ref, lse_ref, vt_s, m_s, l_s, acc_s):
        i = pl.program_id(1)

        @pl.when(i == 0)
        def _():
            for j in range(nb):
                vt_s[j] = v_ref[j * MOBA_BLOCK:(j + 1) * MOBA_BLOCK, :].T.astype(bf16)

        q = q_ref[...]
        sels = _moba_select(q.astype(bf16), k_ref[...], i, nb)
        qs = (q * SCALE).astype(bf16)
        k_heads = _head_lanes((MOBA_BLOCK, LANES))
        o_heads = _head_rows((LANES, MOBA_BLOCK))
        key = lax.broadcasted_iota(jnp.int32, (MOBA_BLOCK, MOBA_BLOCK), 0)
        qry = lax.broadcasted_iota(jnp.int32, (MOBA_BLOCK, MOBA_BLOCK), 1)

        def block(kj, vtj, allow, m_old, l_old):
            m_new, alpha, l_new, pv = [], [], [], None
            for h in range(2):
                sc = jnp.where(allow[h], _dot_nt(jnp.where(k_heads[h], kj, 0.0).astype(bf16), qs), NEG)
                mx = jnp.max(sc, axis=0, keepdims=True)
                mn = mx if m_old is None else jnp.maximum(m_old[h], mx)
                p = jnp.exp(sc - mn)
                lsum = jnp.sum(p, axis=0, keepdims=True)
                if m_old is None:
                    a = None
                else:
                    a = jnp.exp(m_old[h] - mn)
                    lsum = a * l_old[h] + lsum
                t = _dot(jnp.where(o_heads[h], vtj, jnp.zeros_like(vtj)), p.astype(bf16))
                pv = t if pv is None else pv + t
                m_new.append(mn)
                alpha.append(a)
                l_new.append(lsum)
            return m_new, alpha, l_new, pv

        own = pl.ds(pl.multiple_of(i * MOBA_BLOCK, MOBA_BLOCK), MOBA_BLOCK)
        m_new, _, l_new, pv = block(k_ref[own, :], vt_s[i], [key <= qry] * 2, None, None)
        m_s[...] = _pair_rows(m_new[0], m_new[1], 0.0)
        l_s[...] = _pair_rows(l_new[0], l_new[1], 1.0)
        acc_s[...] = pv

        for j in range(nb - 1):
            @pl.when(j < i)
            def _(j=j):
                m_old, l_old = m_s[...], l_s[...]
                m_new, alpha, l_new, pv = block(
                    k_ref[j * MOBA_BLOCK:(j + 1) * MOBA_BLOCK, :], vt_s[j], [sels[h][j:j + 1, :] > 0.0 for h in range(2)],
                    [m_old[0:1, :], m_old[1:2, :]], [l_old[0:1, :], l_old[1:2, :]])
                m_s[...] = _pair_rows(m_new[0], m_new[1], 0.0)
                l_s[...] = _pair_rows(l_new[0], l_new[1], 1.0)
                acc_s[...] = acc_s[...] * jnp.where(o_heads[0], alpha[0], alpha[1]) + pv

        l_fin = l_s[...]
        o_t = acc_s[...] / jnp.where(o_heads[0], l_fin[0:1, :], l_fin[1:2, :])
        o_ref[...] = o_t.T.astype(bf16)
        lse_ref[...] = m_s[...] + jnp.log(l_fin)

    stat = pltpu.VMEM((STAT_ROWS, MOBA_BLOCK), f32)
    return _cargo_call(
        body, cargo, name="attn_fwd", grid=(D_ATTN // LANES, nb),
        in_specs=[pl.BlockSpec((MOBA_BLOCK, LANES), lambda hp, i: (i, hp)),
                  pl.BlockSpec((s, LANES), lambda hp, i: (0, 4 + hp)),
                  pl.BlockSpec((s, LANES), lambda hp, i: (0, 8 + hp))],
        out_specs=[pl.BlockSpec((MOBA_BLOCK, LANES), lambda hp, i: (i, hp)),
                   pl.BlockSpec((None, STAT_ROWS, MOBA_BLOCK), lambda hp, i: (hp, 0, i))],
        out_shape=[jax.ShapeDtypeStruct((s, D_ATTN), bf16),
                   jax.ShapeDtypeStruct((D_ATTN // LANES, STAT_ROWS, s), f32)],
        scratch=[pltpu.VMEM((nb, LANES, MOBA_BLOCK), bf16), stat, stat,
                 pltpu.VMEM((LANES, MOBA_BLOCK), f32)])(z, z, z)


def _window_select(snaps, g):
    return jnp.where(g == 0, snaps[0], jnp.where(g == 1, snaps[1], jnp.where(g == 2, snaps[2], snaps[3])))


def _pool_fwd(z, w_pool, pool_scale):
    s = z.shape[0]

    def body(u_ref, w_ref, sc_ref, pooled_ref, mixed_ref, pm_ref, pad):
        g = pl.program_id(0)
        u = u_ref[...]
        pad[0:MAX_WINDOW, :] = jnp.zeros((MAX_WINDOW, POOL_GROUP), f32)
        pad[MAX_WINDOW:MAX_WINDOW + s, :] = u
        acc = u
        snaps = []
        for d in range(1, MAX_WINDOW):
            acc = acc + pad[MAX_WINDOW - d:MAX_WINDOW - d + s, :]
            if d + 1 in (2, 4, 8, 16):
                snaps.append(acc)
        win = _window_select(snaps, g)
        t = lax.broadcasted_iota(jnp.int32, (s, POOL_GROUP), 0)
        count = jnp.minimum(t + 1, jnp.left_shift(2, g)).astype(f32)
        pooled = (win / count - u).astype(bf16)
        mixed = _dot(pooled, w_ref[...].astype(bf16))
        pooled_ref[...] = pooled
        mixed_ref[...] = mixed
        pm_ref[...] = (mixed * sc_ref[...]).astype(bf16)

    blk = pl.BlockSpec((s, POOL_GROUP), lambda g: (0, g))
    return _call(
        body, name="pool_fwd", grid=(4,),
        in_specs=[pl.BlockSpec((s, POOL_GROUP), lambda g: (0, 12 + g)),
                  pl.BlockSpec((None, POOL_GROUP, POOL_GROUP), lambda g: (g, 0, 0)),
                  pl.BlockSpec((1, POOL_GROUP), lambda g: (0, g))],
        out_specs=[blk, blk, blk],
        out_shape=[jax.ShapeDtypeStruct((s, D_POOL), bf16), jax.ShapeDtypeStruct((s, D_POOL), f32),
                   jax.ShapeDtypeStruct((s, D_POOL), bf16)],
        scratch=[pltpu.VMEM((s + MAX_WINDOW, POOL_GROUP), f32)])(z, w_pool, pool_scale)


def _branch_merge(o_bf, pm_bf, z, wba, wbp):
    s = o_bf.shape[0]
    tm = 256

    def body(o_ref, pm_ref, ga_ref, gp_ref, wba_ref, wbp_ref, m_ref):
        ya = _dot(o_ref[...], wba_ref[...])
        yp = _dot(pm_ref[...], wbp_ref[...])
        m_ref[...] = (ga_ref[...] * ya + gp_ref[...] * yp).astype(bf16)

    full = lambda r, c: pl.BlockSpec((r, c), lambda i: (0, 0))
    return _call(
        body, name="branch_merge", grid=(s // tm,),
        in_specs=[pl.BlockSpec((tm, D_ATTN), lambda i: (i, 0)), pl.BlockSpec((tm, D_POOL), lambda i: (i, 0)),
                  pl.BlockSpec((tm, D_MODEL), lambda i: (i, 2)), pl.BlockSpec((tm, D_MODEL), lambda i: (i, 3)),
                  full(D_ATTN, D_MODEL), full(D_POOL, D_MODEL)],
        out_specs=pl.BlockSpec((tm, D_MODEL), lambda i: (i, 0)),
        out_shape=jax.ShapeDtypeStruct((s, D_MODEL), bf16))(o_bf, pm_bf, z, z, wba, wbp)


def _out_ln1(m_bf, wout, x, ln_g, ln_b):
    s = x.shape[0]
    tm = 256

    def body(m_ref, w_ref, x_ref, g_ref, b_ref, x1_ref, xhat_ref, rstd_ref):
        h = ALPHA * x_ref[...] + _dot(m_ref[...], w_ref[...])
        y, xhat, rstd = _layer_norm(h, g_ref[...], b_ref[...])
        x1_ref[...] = y.astype(bf16)
        xhat_ref[...] = xhat
        rstd_ref[...] = jnp.broadcast_to(rstd, (tm, LANES))

    row = pl.BlockSpec((tm, D_MODEL), lambda i: (i, 0))
    vec = pl.BlockSpec((1, D_MODEL), lambda i: (0, 0))
    return _call(
        body, name="out_ln1", grid=(s // tm,),
        in_specs=[row, pl.BlockSpec((D_MODEL, D_MODEL), lambda i: (0, 0)), row, vec, vec],
        out_specs=[row, row, pl.BlockSpec((tm, LANES), lambda i: (i, 0))],
        out_shape=[jax.ShapeDtypeStruct((s, D_MODEL), bf16), jax.ShapeDtypeStruct((s, D_MODEL), f32),
                   jax.ShapeDtypeStruct((s, LANES), f32)])(m_bf, wout, x, ln_g, ln_b)


FF_TILE = 256
FF_TILES_PER_SHARD = FF_PAD // FF_TILE
CONV_PAD = 8
CONV_ROWS = 16


def _ff_weight_spec(rows):
    return pl.BlockSpec((None, rows, FF_TILE), lambda j: (j // FF_TILES_PER_SHARD, 0, j % FF_TILES_PER_SHARD))


def _ff_rows_spec():
    return pl.BlockSpec((None, FF_TILE, D_MODEL), lambda j: (j // FF_TILES_PER_SHARD, j % FF_TILES_PER_SHARD, 0))


def _conv_taps(pad, s, cw_ref, off=0):
    return (pad[CONV_PAD + off:CONV_PAD + off + s, :] * cw_ref[2:3, :]
            + pad[CONV_PAD - 1 + off:CONV_PAD - 1 + off + s, :] * cw_ref[1:2, :]
            + pad[CONV_PAD - 2 + off:CONV_PAD - 2 + off + s, :] * cw_ref[0:1, :])


def _ffn_up(x1_bf, wg_g, wu_g, convw_g, convb_g, cargo):
    s = x1_bf.shape[0]

    def body(x_ref, wg_ref, wu_ref, cw_ref, cb_ref, a_ref, up_ref, hh_ref, pad):
        x1 = x_ref[...]
        a = _dot_nt(x1, wg_ref[...])
        up = _dot_nt(x1, wu_ref[...])
        a_ref[...] = a
        up_ref[...] = up
        pad[0:CONV_PAD, :] = jnp.zeros((CONV_PAD, FF_TILE), f32)
        pad[CONV_PAD:CONV_PAD + s, :] = a
        ac = _conv_taps(pad, s, cw_ref) + cb_ref[...]
        hg, _ = _gelu_parts(ac)
        hh_ref[...] = (hg * up).astype(bf16)

    col = pl.BlockSpec((s, FF_TILE), lambda j: (0, j))
    return _call(
        body, name="ffn_up", grid=(D_FF_PAD // FF_TILE,),
        in_specs=[pl.BlockSpec((s, D_MODEL), lambda j: (0, 0)), _ff_rows_spec(), _ff_rows_spec(),
                  _ff_weight_spec(8), _ff_weight_spec(1)],
        out_specs=[col, col, col],
        out_shape=[jax.ShapeDtypeStruct((s, D_FF_PAD), f32), jax.ShapeDtypeStruct((s, D_FF_PAD), f32),
                   jax.ShapeDtypeStruct((s, D_FF_PAD), bf16)],
        scratch=[pltpu.VMEM((s + CONV_PAD, FF_TILE), f32)], cargo=cargo)(x1_bf, wg_g, wu_g, convw_g, convb_g)


def _ffn_down_ln2_loss(hh_bf, wd, xhat1, ln1_g, ln1_b, ln2_g, ln2_b, target):
    s = hh_bf.shape[0]
    tm = 256

    def body(hh_ref, w_ref, xh1_ref, g1_ref, b1_ref, g2_ref, b2_ref, t_ref, loss_ref, dh_ref, dhb_ref, dg_ref, db_ref):
        i = pl.program_id(0)
        x1 = xh1_ref[...] * g1_ref[...] + b1_ref[...]
        h = ALPHA * x1 + _dot(hh_ref[...], w_ref[...])
        y, xhat, rstd = _layer_norm(h, g2_ref[...], b2_ref[...])
        err = y - t_ref[...]
        part = 0.5 * jnp.sum(jnp.mean(err * err, axis=-1, keepdims=True), axis=0, keepdims=True)
        dy = err * (1.0 / D_MODEL)

        @pl.when(i == 0)
        def _():
            loss_ref[...] = jnp.zeros_like(loss_ref)
            dg_ref[...] = jnp.zeros_like(dg_ref)
            db_ref[...] = jnp.zeros_like(db_ref)

        loss_ref[...] += jnp.broadcast_to(part, loss_ref.shape)
        dg_ref[...] += jnp.sum(dy * xhat, axis=0, keepdims=True)
        db_ref[...] += jnp.sum(dy, axis=0, keepdims=True)
        dh = _layer_norm_bwd(dy, xhat, rstd, g2_ref[...])
        dh_ref[...] = dh
        dhb_ref[...] = dh.astype(bf16)

    row = pl.BlockSpec((tm, D_MODEL), lambda i: (i, 0))
    vec = pl.BlockSpec((1, D_MODEL), lambda i: (0, 0))
    return _call(
        body, name="ffn_down_ln2_loss", grid=(s // tm,),
        in_specs=[pl.BlockSpec((tm, D_FF_PAD), lambda i: (i, 0)), pl.BlockSpec((D_FF_PAD, D_MODEL), lambda i: (0, 0)),
                  row, vec, vec, vec, vec, row],
        out_specs=[pl.BlockSpec((8, LANES), lambda i: (0, 0)), row, row, vec, vec],
        out_shape=[jax.ShapeDtypeStruct((8, LANES), f32), jax.ShapeDtypeStruct((s, D_MODEL), f32),
                   jax.ShapeDtypeStruct((s, D_MODEL), bf16), jax.ShapeDtypeStruct((1, D_MODEL), f32),
                   jax.ShapeDtypeStruct((1, D_MODEL), f32)])(hh_bf, wd, xhat1, ln1_g, ln1_b, ln2_g, ln2_b, target)


def _ffn_act_bwd(dh2_bf, wd_g, a, up, convw_g, convb_g):
    s = dh2_bf.shape[0]

    def body(dh_ref, wd_ref, a_ref, up_ref, cw_ref, cb_ref, da_ref, dup_ref, dc_ref, pad, dpad):
        dhh = _dot_nt(dh_ref[...], wd_ref[...])
        zeros = jnp.zeros((CONV_PAD, FF_TILE), f32)
        pad[0:CONV_PAD, :] = zeros
        pad[CONV_PAD:CONV_PAD + s, :] = a_ref[...]
        ac = _conv_taps(pad, s, cw_ref) + cb_ref[...]
        hg, dgelu = _gelu_parts(ac)
        up = up_ref[...]
        dup_ref[...] = (dhh * hg).astype(bf16)
        dac = dhh * up * dgelu
        rows = [jnp.sum(dac * pad[CONV_PAD - 2 + i:CONV_PAD - 2 + i + s, :], axis=0, keepdims=True) for i in range(3)]
        rows.append(jnp.sum(dac, axis=0, keepdims=True))
        rows.append(jnp.zeros((CONV_ROWS - 4, FF_TILE), f32))
        dc_ref[...] = jnp.concatenate(rows, axis=0)
        dpad[0:s, :] = dac
        dpad[s:s + CONV_PAD, :] = zeros
        da = (dpad[0:s, :] * cw_ref[2:3, :] + dpad[1:1 + s, :] * cw_ref[1:2, :] + dpad[2:2 + s, :] * cw_ref[0:1, :])
        da_ref[...] = da.astype(bf16)

    col = pl.BlockSpec((s, FF_TILE), lambda j: (0, j))
    return _call(
        body, name="ffn_act_bwd", grid=(D_FF_PAD // FF_TILE,),
        in_specs=[pl.BlockSpec((s, D_MODEL), lambda j: (0, 0)), _ff_rows_spec(),
                  col, col, _ff_weight_spec(8), _ff_weight_spec(1)],
        out_specs=[col, col, _ff_weight_spec(CONV_ROWS)],
        out_shape=[jax.ShapeDtypeStruct((s, D_FF_PAD), bf16), jax.ShapeDtypeStruct((s, D_FF_PAD), bf16),
                   jax.ShapeDtypeStruct((N_CHIPS, CONV_ROWS, FF_PAD), f32)],
        scratch=[pltpu.VMEM((s + CONV_PAD, FF_TILE), f32), pltpu.VMEM((s + CONV_PAD, FF_TILE), f32)],
    )(dh2_bf, wd_g, a, up, convw_g, convb_g)


def _matmul_tn(a, b, n_shards, name, tm=512, tn=1024, cargo=None):
    k, m = a.shape
    n = b.shape[1]
    tm, tn = min(tm, m), min(tn, n // n_shards)
    per = n // n_shards // tn

    def body(a_ref, b_ref, o_ref, at_s):
        @pl.when(pl.program_id(1) == 0)
        def _():
            at_s[...] = a_ref[...].T

        o_ref[...] = _dot(at_s[...], b_ref[...]).astype(bf16)

    res = _call(
        body, name=name, grid=(m // tm, n // tn),
        in_specs=[pl.BlockSpec((k, tm), lambda i, j: (0, i)), pl.BlockSpec((k, tn), lambda i, j: (0, j))],
        out_specs=[pl.BlockSpec((None, tm, tn), lambda i, j: (j // per, i, j % per))],
        out_shape=[jax.ShapeDtypeStruct((n_shards, m, n // n_shards), bf16)], scratch=[pltpu.VMEM((tm, k), bf16)],
        cargo=cargo)(a, b)
    return res[0] if cargo is None else (res[0][0], res[1])


def _ffn_in_bwd_ln1(da_bf, dup_bf, wg_g, wu_g, dh2, xhat1, rstd1, ln1_g, cargo):
    s = da_bf.shape[0]
    tm = 256

    def body(da_ref, dup_ref, wg_ref, wu_ref, dh2_ref, xh_ref, rstd_ref, g_ref, dh_ref, dhb_ref, dg_ref, db_ref):
        i = pl.program_id(0)
        dx1 = ALPHA * dh2_ref[...]
        for sh in range(N_CHIPS):
            sl = slice(sh * FF_PAD, (sh + 1) * FF_PAD)
            dx1 = dx1 + _dot(da_ref[:, sl], wg_ref[sh]) + _dot(dup_ref[:, sl], wu_ref[sh])
        xhat = xh_ref[...]

        @pl.when(i == 0)
        def _():
            dg_ref[...] = jnp.zeros_like(dg_ref)
            db_ref[...] = jnp.zeros_like(db_ref)

        dg_ref[...] += jnp.sum(dx1 * xhat, axis=0, keepdims=True)
        db_ref[...] += jnp.sum(dx1, axis=0, keepdims=True)
        dh = _layer_norm_bwd(dx1, xhat, rstd_ref[:, 0:1], g_ref[...])
        dh_ref[...] = dh
        dhb_ref[...] = dh.astype(bf16)

    row = pl.BlockSpec((tm, D_MODEL), lambda i: (i, 0))
    wide = pl.BlockSpec((tm, D_FF_PAD), lambda i: (i, 0))
    wfull = pl.BlockSpec((N_CHIPS, FF_PAD, D_MODEL), lambda i: (0, 0, 0))
    vec = pl.BlockSpec((1, D_MODEL), lambda i: (0, 0))
    return _call(
        body, name="ffn_in_bwd_ln1", grid=(s // tm,),
        in_specs=[wide, wide, wfull, wfull, row, row, pl.BlockSpec((tm, LANES), lambda i: (i, 0)), vec],
        out_specs=[row, row, vec, vec],
        out_shape=[jax.ShapeDtypeStruct((s, D_MODEL), f32), jax.ShapeDtypeStruct((s, D_MODEL), bf16),
                   jax.ShapeDtypeStruct((1, D_MODEL), f32), jax.ShapeDtypeStruct((1, D_MODEL), f32)],
        cargo=cargo)(da_bf, dup_bf, wg_g, wu_g, dh2, xhat1, rstd1, ln1_g)


def _merge_bwd(dh1_bf, wout, o_bf, pm_bf, z, wba, wbp):
    s = dh1_bf.shape[0]
    tm = 256

    def body(dh_ref, wout_ref, o_ref, pm_ref, ga_ref, gp_ref, wba_ref, wbp_ref,
             dzg_ref, dya_ref, dyp_ref, do_ref, dpm_ref, dbg_ref):
        i = pl.program_id(0)
        dm = _dot_nt(dh_ref[...], wout_ref[...])
        ya = _dot(o_ref[...], wba_ref[...])
        yp = _dot(pm_ref[...], wbp_ref[...])
        ga, gp = ga_ref[...], gp_ref[...]
        dza = dm * ya * ga * (1.0 - ga)
        dzp = dm * yp * gp * (1.0 - gp)

        @pl.when(i == 0)
        def _():
            dbg_ref[...] = jnp.zeros_like(dbg_ref)

        dbg_ref[:, 0:D_MODEL] += jnp.sum(dza, axis=0, keepdims=True)
        dbg_ref[:, D_MODEL:2 * D_MODEL] += jnp.sum(dzp, axis=0, keepdims=True)
        dzg_ref[:, 0:D_MODEL] = dza.astype(bf16)
        dzg_ref[:, D_MODEL:2 * D_MODEL] = dzp.astype(bf16)
        dya = (dm * ga).astype(bf16)
        dyp = (dm * gp).astype(bf16)
        dya_ref[...] = dya
        dyp_ref[...] = dyp
        do_ref[...] = _dot_nt(dya, wba_ref[...]).astype(bf16)
        dpm_ref[...] = _dot_nt(dyp, wbp_ref[...])

    row = pl.BlockSpec((tm, D_MODEL), lambda i: (i, 0))
    half = pl.BlockSpec((tm, D_ATTN), lambda i: (i, 0))
    full = lambda r, c: pl.BlockSpec((r, c), lambda i: (0, 0))
    return _call(
        body, name="merge_bwd", grid=(s // tm,),
        in_specs=[row, full(D_MODEL, D_MODEL), half, half,
                  pl.BlockSpec((tm, D_MODEL), lambda i: (i, 2)), pl.BlockSpec((tm, D_MODEL), lambda i: (i, 3)),
                  full(D_ATTN, D_MODEL), full(D_POOL, D_MODEL)],
        out_specs=[pl.BlockSpec((tm, 2 * D_MODEL), lambda i: (i, 0)), row, row, half, half,
                   pl.BlockSpec((1, 2 * D_MODEL), lambda i: (0, 0))],
        out_shape=[jax.ShapeDtypeStruct((s, 2 * D_MODEL), bf16), jax.ShapeDtypeStruct((s, D_MODEL), bf16),
                   jax.ShapeDtypeStruct((s, D_MODEL), bf16), jax.ShapeDtypeStruct((s, D_ATTN), bf16),
                   jax.ShapeDtypeStruct((s, D_POOL), f32), jax.ShapeDtypeStruct((1, 2 * D_MODEL), f32)],
    )(dh1_bf, wout, o_bf, pm_bf, z, z, wba, wbp)


def _pool_bwd(dpm, mixed, pooled_bf, w_pool, pool_scale):
    s = dpm.shape[0]

    def body(dpm_ref, mixed_ref, pooled_ref, w_ref, sc_ref, du_ref, dw_ref, dsc_ref, pad):
        g = pl.program_id(0)
        dpm_v = dpm_ref[...]
        dsc_ref[...] = jnp.sum(dpm_v * mixed_ref[...], axis=0, keepdims=True)
        dmixed = (dpm_v * sc_ref[...]).astype(bf16)
        dw_ref[...] = _dot_tn(pooled_ref[...], dmixed)
        dpooled = _dot_nt(dmixed, w_ref[...].astype(bf16))
        t = lax.broadcasted_iota(jnp.int32, (s, POOL_GROUP), 0)
        count = jnp.minimum(t + 1, jnp.left_shift(2, g)).astype(f32)
        r = dpooled / count
        pad[0:s, :] = r
        pad[s:s + MAX_WINDOW, :] = jnp.zeros((MAX_WINDOW, POOL_GROUP), f32)
        acc = r
        snaps = []
        for d in range(1, MAX_WINDOW):
            acc = acc + pad[d:d + s, :]
            if d + 1 in (2, 4, 8, 16):
                snaps.append(acc)
        du_ref[...] = (_window_select(snaps, g) - dpooled).astype(bf16)

    blk = pl.BlockSpec((s, POOL_GROUP), lambda g: (0, g))
    return _call(
        body, name="pool_bwd", grid=(4,),
        in_specs=[blk, blk, blk, pl.BlockSpec((None, POOL_GROUP, POOL_GROUP), lambda g: (g, 0, 0)),
                  pl.BlockSpec((1, POOL_GROUP), lambda g: (0, g))],
        out_specs=[blk, pl.BlockSpec((None, POOL_GROUP, POOL_GROUP), lambda g: (g, 0, 0)),
                   pl.BlockSpec((1, POOL_GROUP), lambda g: (0, g))],
        out_shape=[jax.ShapeDtypeStruct((s, D_POOL), bf16), jax.ShapeDtypeStruct((4, POOL_GROUP, POOL_GROUP), f32),
                   jax.ShapeDtypeStruct((1, D_POOL), f32)],
        scratch=[pltpu.VMEM((s + MAX_WINDOW, POOL_GROUP), f32)])(dpm, mixed, pooled_bf, w_pool, pool_scale)


def _attn_bwd(z, o_bf, lse, do_bf, cos_t, sin_t, nb, cargo):
    s = z.shape[0]

    def body(q_ref, k_ref, v_ref, o_ref, lse_ref, do_ref, cq_ref, sq_ref, cf_ref, sf_ref,
             dq_ref, dk_ref, dv_ref, kt_s, dqt_s, dk_s, dv_s):
        i = pl.program_id(1)

        @pl.when(i == 0)
        def _():
            dk_s[...] = jnp.zeros_like(dk_s)
            dv_s[...] = jnp.zeros_like(dv_s)
            for j in range(nb):
                kt_s[j] = k_ref[j * MOBA_BLOCK:(j + 1) * MOBA_BLOCK, :].T.astype(bf16)

        q = q_ref[...]
        sels = _moba_select(q.astype(bf16), k_ref[...], i, nb)
        qs = (q * SCALE).astype(bf16)
        k_heads = _head_lanes((MOBA_BLOCK, LANES))
        t_heads = _head_rows((LANES, MOBA_BLOCK))
        qraw = [jnp.where(hm, q, 0.0).astype(bf16) for hm in k_heads]
        do = do_ref[...].astype(f32)
        do_bf = do_ref[...]
        dob = [jnp.where(hm, do, 0.0).astype(bf16) for hm in k_heads]
        pick = _head_lanes((STAT_ROWS, LANES))
        r8 = lax.broadcasted_iota(jnp.int32, (STAT_ROWS, LANES), 0)
        head_pick = jnp.where(((r8 == 0) & pick[0]) | ((r8 == 1) & pick[1]), 1.0, 0.0)
        delta8 = lax.dot_general(head_pick, do * o_ref[...].astype(f32), (((1,), (1,)), ((), ())),
                                 precision=lax.Precision.HIGHEST, preferred_element_type=f32)
        lse8 = lse_ref[...]
        delta, lse_h = [delta8[0:1, :], delta8[1:2, :]], [lse8[0:1, :], lse8[1:2, :]]
        key = lax.broadcasted_iota(jnp.int32, (MOBA_BLOCK, MOBA_BLOCK), 0)
        qry = lax.broadcasted_iota(jnp.int32, (MOBA_BLOCK, MOBA_BLOCK), 1)
        dqt_s[...] = jnp.zeros_like(dqt_s)

        def block(rows, ktj, allow):
            kj, vj = k_ref[rows, :], v_ref[rows, :]
            dqt, dk, dv = None, None, None
            for h in range(2):
                sc = _dot_nt(jnp.where(k_heads[h], kj, 0.0).astype(bf16), qs)
                p = jnp.where(allow[h], jnp.exp(sc - lse_h[h]), 0.0)
                dp = _dot_nt(jnp.where(k_heads[h], vj, 0.0).astype(bf16), do_bf)
                ds = (p * (dp - delta[h]) * SCALE).astype(bf16)
                tv = _dot(p.astype(bf16), dob[h])
                tk = _dot(ds, qraw[h])
                tq = _dot(jnp.where(t_heads[h], ktj, jnp.zeros_like(ktj)), ds)
                dqt, dk, dv = (tq, tk, tv) if dqt is None else (dqt + tq, dk + tk, dv + tv)
            dqt_s[...] += dqt
            dk_s[rows, :] += dk
            dv_s[rows, :] += dv

        block(pl.ds(pl.multiple_of(i * MOBA_BLOCK, MOBA_BLOCK), MOBA_BLOCK), kt_s[i], [key <= qry] * 2)
        for j in range(nb - 1):
            @pl.when(j < i)
            def _(j=j):
                block(slice(j * MOBA_BLOCK, (j + 1) * MOBA_BLOCK), kt_s[j], [sels[h][j:j + 1, :] > 0.0 for h in range(2)])

        dq_ref[...] = _rope_bwd(dqt_s[...].T, cq_ref[...], sq_ref[...]).astype(bf16)

        @pl.when(i == nb - 1)
        def _():
            dk_ref[...] = _rope_bwd(dk_s[...], cf_ref[...], sf_ref[...]).astype(bf16)
            dv_ref[...] = dv_s[...].astype(bf16)

    qblk = pl.BlockSpec((MOBA_BLOCK, LANES), lambda hp, i: (i, hp))
    tq = pl.BlockSpec((MOBA_BLOCK, LANES), lambda hp, i: (i, 0))
    tf = pl.BlockSpec((s, LANES), lambda hp, i: (0, 0))
    colblk = pl.BlockSpec((s, LANES), lambda hp, i: (0, hp))
    return _cargo_call(
        body, cargo, name="attn_bwd", grid=(D_ATTN // LANES, nb),
        in_specs=[qblk, pl.BlockSpec((s, LANES), lambda hp, i: (0, 4 + hp)),
                  pl.BlockSpec((s, LANES), lambda hp, i: (0, 8 + hp)), qblk,
                  pl.BlockSpec((None, STAT_ROWS, MOBA_BLOCK), lambda hp, i: (hp, 0, i)), qblk, tq, tq, tf, tf],
        out_specs=[qblk, colblk, colblk],
        out_shape=[jax.ShapeDtypeStruct((s, D_ATTN), bf16)] * 3,
        scratch=[pltpu.VMEM((nb, LANES, MOBA_BLOCK), bf16), pltpu.VMEM((LANES, MOBA_BLOCK), f32),
                 pltpu.VMEM((s, LANES), f32), pltpu.VMEM((s, LANES), f32)],
    )(z, z, z, o_bf, lse, do_bf, cos_t, sin_t, cos_t, sin_t)


def _inproj_dx(dz_bf, win_g, dh1, cargo):
    s = dz_bf.shape[0]
    tm = 256

    def body(dz_ref, w_ref, dh_ref, gx_ref):
        acc = ALPHA * dh_ref[...]
        for sh in range(N_CHIPS):
            acc = acc + _dot_nt(dz_ref[:, sh * D_MODEL:(sh + 1) * D_MODEL], w_ref[sh])
        gx_ref[...] = acc

    row = pl.BlockSpec((tm, D_MODEL), lambda i: (i, 0))
    return _call(
        body, name="inproj_dx", grid=(s // tm,),
        in_specs=[pl.BlockSpec((tm, 4 * D_MODEL), lambda i: (i, 0)),
                  pl.BlockSpec((N_CHIPS, D_MODEL, D_MODEL), lambda i: (0, 0, 0)), row],
        out_specs=[row], out_shape=[jax.ShapeDtypeStruct((s, D_MODEL), f32)], cargo=cargo)(dz_bf, win_g, dh1)


ANY = pl.BlockSpec(memory_space=pl.ANY)


def _chip_index():
    return 2 * lax.axis_index("x") + lax.axis_index("y")


def _peer(k):
    x, y, c = lax.axis_index("x"), lax.axis_index("y"), lax.axis_index("c")
    return (x ^ (k >> 1), y ^ (k & 1), c)


def _sibling():
    return (lax.axis_index("x"), lax.axis_index("y"), 1 - lax.axis_index("c"))


class _GatherCargo:
    def __init__(self, shards, pass_on):
        self.shards, self.pass_on, n = shards, pass_on, len(shards)
        self.inputs = [a.reshape(2, a.shape[0] // 2, a.shape[1]) for a in shards]
        self.out_shape = [jax.ShapeDtypeStruct((N_CHIPS,) + a.shape, a.dtype) for a in self.inputs]
        sem, one = pltpu.SemaphoreType.DMA((N_CHIPS - 1, n)), pltpu.SemaphoreType.DMA((n,))
        self.sems = [sem, sem, sem, sem, one, one]

    def _copies(self, src, dst, sems, a):
        send_sems, recv_sems, fsend_sems, frecv_sems, local_sems, own_sems = sems
        p, c = _chip_index(), lax.axis_index("c")
        own = pltpu.make_async_remote_copy(src_ref=src[a], dst_ref=dst[a].at[p], send_sem=local_sems.at[a],
                                           recv_sem=own_sems.at[a], device_id=_sibling(), device_id_type=MESH)
        out, arrive, onward, landed = [], [], [], []
        for k in range(1, N_CHIPS):
            ici = dict(send_sem=send_sems.at[k - 1, a], recv_sem=recv_sems.at[k - 1, a], device_id=_peer(k),
                       device_id_type=MESH)
            d2d = dict(send_sem=fsend_sems.at[k - 1, a], recv_sem=frecv_sems.at[k - 1, a], device_id=_sibling(),
                       device_id_type=MESH)
            got, theirs = dst[a].at[p ^ k, c], dst[a].at[p ^ k, 1 - c]
            out.append(pltpu.make_async_remote_copy(src_ref=src[a].at[c], dst_ref=dst[a].at[p, c], **ici))
            arrive.append(pltpu.make_async_remote_copy(src_ref=src[a].at[c], dst_ref=got, **ici))
            onward.append(pltpu.make_async_remote_copy(src_ref=got, dst_ref=got, **d2d))
            landed.append(pltpu.make_async_remote_copy(src_ref=theirs, dst_ref=theirs, **d2d))
        return own, out, arrive, onward, landed

    def stages(self, steps):
        n = len(self.shards)

        def start(src, dst, sems):
            for a in range(n):
                own, out, _, _, _ = self._copies(src, dst, sems, a)
                own.start()
                for cp in out:
                    cp.start()

        def pass_on(a):
            def act(src, dst, sems):
                _, _, arrive, onward, _ = self._copies(src, dst, sems, a)
                for k in range(N_CHIPS - 1):
                    arrive[k].wait_recv()
                    onward[k].start()
            return act

        def finish(src, dst, sems):
            for a in range(n):
                own, out, _, onward, landed = self._copies(src, dst, sems, a)
                for cp in landed:
                    cp.wait_recv()
                for cp in out + onward:
                    cp.wait_send()
                own.wait()

        mids = [(min(steps - 1, int(self.pass_on[a] * steps)), pass_on(a)) for a in range(n)]
        return [(0, start)] + mids + [(steps - 1, finish)]

    def results(self, outs):
        return [o.reshape((N_CHIPS,) + a.shape) for o, a in zip(outs, self.shards)]


class _ExchangeCargo:
    def __init__(self, sums, whole):
        self.inputs, self.whole, n = sums, whole, len(sums)
        self.out_shape = [jax.ShapeDtypeStruct((N_CHIPS - 1,) + g.shape[1:], g.dtype) for g in sums]
        sem = pltpu.SemaphoreType.DMA((N_CHIPS - 1, n))
        self.sems = [sem, sem]

    def _copies(self, src, dst, sems):
        send_sems, recv_sems = sems
        p = _chip_index()
        return [pltpu.make_async_remote_copy(
            src_ref=src[a].at[0] if self.whole[a] else src[a].at[p ^ k], dst_ref=dst[a].at[k - 1],
            send_sem=send_sems.at[k - 1, a], recv_sem=recv_sems.at[k - 1, a], device_id=_peer(k), device_id_type=MESH)
            for k in range(1, N_CHIPS) for a in range(len(self.inputs))]

    def stages(self, steps):
        def start(src, dst, sems):
            for cp in self._copies(src, dst, sems):
                cp.start()

        def finish(src, dst, sems):
            copies = self._copies(src, dst, sems)
            for cp in copies:
                cp.wait_recv()
            for cp in copies:
                cp.wait_send()

        return [(0, start), (steps - 1, finish)]

    def results(self, outs):
        return list(outs)


def _cargo_call(body, cargo, *, name, grid, in_specs, out_specs, out_shape, scratch=()):
    n_in, n_out, n_scr = len(in_specs), len(out_specs), len(scratch)
    c_in, c_out = len(cargo.inputs), len(cargo.out_shape)
    steps = 1
    for g in grid:
        steps *= g
    stages = cargo.stages(steps)

    def wrapped(*refs):
        ins, refs = refs[:n_in], refs[n_in:]
        cin, refs = refs[:c_in], refs[c_in:]
        outs, refs = refs[:n_out], refs[n_out:]
        cout, refs = refs[:c_out], refs[c_out:]
        scr, sems = refs[:n_scr], refs[n_scr:]
        if not grid:
            for _, act in stages:
                act(cin, cout, sems)
            return
        step = 0
        for d in range(len(grid)):
            step = step * grid[d] + pl.program_id(d)
        pl.when(step == 0)(functools.partial(stages[0][1], cin, cout, sems))
        body(*ins, *outs, *scr)
        for at, act in stages[1:]:
            pl.when(step == at)(functools.partial(act, cin, cout, sems))

    params = dict(vmem_limit_bytes=VMEM_LIMIT)
    if grid:
        params["dimension_semantics"] = ("arbitrary",) * len(grid)
    res = pl.pallas_call(
        wrapped, name=name, grid=grid, in_specs=list(in_specs) + [ANY] * c_in, out_specs=list(out_specs) + [ANY] * c_out,
        out_shape=list(out_shape) + cargo.out_shape, scratch_shapes=list(scratch) + cargo.sems,
        compiler_params=pltpu.CompilerParams(**params))
    return lambda *args: (lambda r: (r[:n_out], cargo.results(r[n_out:])))(res(*args, *cargo.inputs))


def _split_halves(g):
    return g.reshape(g.shape[0], 2, g.shape[1] // 2, g.shape[2])


def _ew_tile(rows):
    return 256 if rows % 256 == 0 else rows


def _prefetch_call(body, *, name, grid, in_specs, out_specs, out_shape, **kw):
    return pl.pallas_call(
        body, name=name, out_shape=out_shape,
        grid_spec=pltpu.PrefetchScalarGridSpec(num_scalar_prefetch=1, grid=grid, in_specs=in_specs, out_specs=out_specs),
        compiler_params=pltpu.CompilerParams(dimension_semantics=("arbitrary",) * len(grid),
                                             vmem_limit_bytes=VMEM_LIMIT), **kw)


def _presum_swap(split, name):
    n = len(split)

    def body(*refs):
        src, got = refs[:n], refs[n:2 * n]
        send_sems, recv_sems = refs[2 * n:]
        c = lax.axis_index("c")
        remote = [pltpu.make_async_remote_copy(src_ref=src[a].at[:, 1 - c], dst_ref=got[a],
                                               send_sem=send_sems.at[a], recv_sem=recv_sems.at[a],
                                               device_id=_sibling(), device_id_type=MESH) for a in range(n)]
        for cp in remote:
            cp.start()
        for cp in remote:
            cp.wait_recv()
        for cp in remote:
            cp.wait_send()

    sem = pltpu.SemaphoreType.DMA((n,))
    return pl.pallas_call(
        body, name=name, in_specs=[ANY] * n, out_specs=[ANY] * n,
        out_shape=[jax.ShapeDtypeStruct((g.shape[0],) + g.shape[2:], g.dtype) for g in split],
        scratch_shapes=[sem, sem])(*split)


def _add_pair(mine, theirs, place, name):
    lead, _, rows, cols = mine.shape
    tm = _ew_tile(rows)

    def body(place_ref, a_ref, b_ref, o_ref):
        o_ref[...] = (a_ref[...].astype(f32) + b_ref[...].astype(f32)).astype(o_ref.dtype)

    blk = pl.BlockSpec((lead, tm, cols), lambda i, pc: (0, i, 0))
    return _prefetch_call(
        body, name=name, grid=(rows // tm,),
        in_specs=[pl.BlockSpec((lead, None, tm, cols), lambda i, pc: (0, pc[1], i, 0)), blk], out_specs=blk,
        out_shape=jax.ShapeDtypeStruct(theirs.shape, theirs.dtype))(place, mine, theirs)


class _NoCargo:
    def __init__(self, results=()):
        self.inputs, self.out_shape, self.sems, self._results = [], [], [], list(results)

    def stages(self, steps):
        return [(0, lambda src, dst, sems: None), (steps - 1, lambda src, dst, sems: None)]

    def results(self, outs):
        return self._results


def _cargo_alone(cargo, name):
    return _cargo_call(None, cargo, name=name, grid=(), in_specs=[], out_specs=[], out_shape=[])()[1]


def _sum_slots(own, recv, place, whole, name):
    _, rows, cols = own.shape
    tm = _ew_tile(rows)

    def body(place_ref, own_ref, r_ref, o_ref):
        r0, r1, r2 = [r_ref[k].astype(f32) for k in range(N_CHIPS - 1)]
        o_ref[...] = (own_ref[...].astype(f32) + r0) + (r1 + r2)

    return _prefetch_call(
        body, name=name, grid=(rows // tm,),
        in_specs=[pl.BlockSpec((None, tm, cols), lambda i, pc: (0 if whole else pc[0], i, 0)),
                  pl.BlockSpec((N_CHIPS - 1, tm, cols), lambda i, pc: (0, i, 0))],
        out_specs=pl.BlockSpec((None, tm, cols), lambda i, pc: (pc[1], i, 0)),
        out_shape=jax.ShapeDtypeStruct((2, rows, cols), f32))(place, own, recv)


def _sibling_fill(bufs):
    n = len(bufs)

    def body(*refs):
        src, dst = refs[:n], refs[n:2 * n]
        send_sems, recv_sems = refs[2 * n:]
        c = lax.axis_index("c")
        remote = [pltpu.make_async_remote_copy(src_ref=src[a].at[c], dst_ref=dst[a].at[c], send_sem=send_sems.at[a],
                                               recv_sem=recv_sems.at[a], device_id=_sibling(), device_id_type=MESH)
                  for a in range(n)]
        for cp in remote:
            cp.start()
        for a in range(n):
            pltpu.make_async_remote_copy(src_ref=src[a].at[c], dst_ref=dst[a].at[1 - c], send_sem=send_sems.at[a],
                                         recv_sem=recv_sems.at[a], device_id=_sibling(), device_id_type=MESH).wait_recv()
        for cp in remote:
            cp.wait_send()

    sem = pltpu.SemaphoreType.DMA((n,))
    out = pl.pallas_call(
        body, name="sibling_fill", in_specs=[ANY] * n, out_specs=[ANY] * n,
        out_shape=[jax.ShapeDtypeStruct(b.shape, b.dtype) for b in bufs],
        input_output_aliases={a: a for a in range(n)}, scratch_shapes=[sem, sem])(*bufs)
    return [o.reshape(2 * b.shape[1], b.shape[2]) for o, b in zip(out, bufs)]


def _adamw(w, m, v, grad, name, tm):
    rows, cols = w.shape
    gcols = grad.shape[1]
    if tm == rows:
        gspec = pl.BlockSpec(grad.shape, lambda i: (0, 0))
    else:
        gspec = pl.BlockSpec((tm, gcols), lambda i: (i, 0))

    def body(w_ref, m_ref, v_ref, gin_ref, g_ref, d_ref, nm_ref, nv_ref):
        g = gin_ref[0:tm, 0:cols]
        w_, m_, v_ = w_ref[...], m_ref[...], v_ref[...]
        m_ = ADAM_B1 * m_ + (1.0 - ADAM_B1) * g
        v_ = ADAM_B2 * v_ + (1.0 - ADAM_B2) * jnp.square(g)
        m_hat = m_ / (1.0 - ADAM_B1 ** ADAM_STEP)
        v_hat = v_ / (1.0 - ADAM_B2 ** ADAM_STEP)
        g_ref[...] = g
        d_ref[...] = -ADAM_LR * (m_hat / (jnp.sqrt(v_hat) + ADAM_EPS) + ADAM_WD * w_)
        nm_ref[...] = m_
        nv_ref[...] = v_

    blk = pl.BlockSpec((tm, cols), lambda i: (i, 0))
    return _call(
        body, name=name, grid=(rows // tm,),
        in_specs=[blk, blk, blk, gspec], out_specs=[blk] * 4,
        out_shape=[jax.ShapeDtypeStruct((rows, cols), f32)] * 4)(w, m, v, grad)


def _rope_tables(s):
    half = HEAD_DIM // 2
    inv_freq = 1.0 / (10000.0 ** (jnp.arange(half, dtype=f32) / half))
    ang = jnp.arange(s, dtype=f32)[:, None] * inv_freq[None, :]
    cos, sin = jnp.cos(ang), jnp.sin(ang)
    return jnp.tile(cos, (1, LANES // half)), jnp.tile(jnp.concatenate([-sin, sin], axis=1), (1, LANES // HEAD_DIM))


def _local_step(x, target, win_g, gather, exchange, b_gate, w_pool, pool_scale, ln1_g, ln1_b, convb_g, ln2_g, ln2_b):
    s = x.shape[0]
    nb = s // MOBA_BLOCK
    cos_t, sin_t = _rope_tables(s)
    x_bf = x.astype(bf16)

    z, (wba_g, wbp_g, wout_g) = _inproj(x_bf, win_g, cos_t, sin_t, b_gate, gather["inproj"])
    wba = wba_g.transpose(1, 0, 2).reshape(D_ATTN, D_MODEL)
    wbp = wbp_g.transpose(1, 0, 2).reshape(D_POOL, D_MODEL)
    wout = wout_g.reshape(D_MODEL, D_MODEL)
    (o_bf, lse), (wg_g, wu_g, convw_g) = _attn_fwd(z, nb, gather["attn_fwd"])
    pooled_bf, mixed, pm_bf = _pool_fwd(z, w_pool, pool_scale)
    m_bf = _branch_merge(o_bf, pm_bf, z, wba, wbp)
    x1_bf, xhat1, rstd1 = _out_ln1(m_bf, wout, x, ln1_g, ln1_b)
    (a, up, hh_bf), (wd_g,) = _ffn_up(x1_bf, wg_g, wu_g, convw_g, convb_g, gather["ffn_up"])
    wd = wd_g.reshape(D_FF_PAD, D_MODEL)
    loss, dh2, dh2_bf, d_ln2_g, d_ln2_b = _ffn_down_ln2_loss(hh_bf, wd, xhat1, ln1_g, ln1_b, ln2_g, ln2_b, target)

    da_bf, dup_bf, dconv = _ffn_act_bwd(dh2_bf, wd_g, a, up, convw_g, convb_g)
    d_wd = _matmul_tn(hh_bf, dh2_bf, 1, "dw_ffn_down").reshape(N_CHIPS, FF_PAD, D_MODEL)
    d_wg = _matmul_tn(da_bf, x1_bf, 1, "dw_ffn_gate").reshape(N_CHIPS, FF_PAD, D_MODEL)
    d_wu = _matmul_tn(dup_bf, x1_bf, 1, "dw_ffn_up").reshape(N_CHIPS, FF_PAD, D_MODEL)
    (dh1, dh1_bf, d_ln1_g, d_ln1_b), got_down = _ffn_in_bwd_ln1(
        da_bf, dup_bf, wg_g, wu_g, dh2, xhat1, rstd1, ln1_g, exchange("ffn_in_bwd_ln1", [("w_ffn_down", d_wd)]))
    d_wout = _matmul_tn(m_bf, dh1_bf, 1, "dw_out").reshape(N_CHIPS, D_MODEL // N_CHIPS, D_MODEL)
    dzg_bf, dya_bf, dyp_bf, do_bf, dpm, d_bgate = _merge_bwd(dh1_bf, wout, o_bf, pm_bf, z, wba, wbp)
    d_wba = _matmul_tn(o_bf, dya_bf, N_CHIPS, "dw_branch_attn")
    d_wbp = _matmul_tn(pm_bf, dyp_bf, N_CHIPS, "dw_branch_pool")
    du_bf, d_wpool, d_pscale = _pool_bwd(dpm, mixed, pooled_bf, w_pool, pool_scale)
    small = _pack_small([d_bgate, d_wpool, d_pscale, d_ln1_g, d_ln1_b, d_ln2_g, d_ln2_b, dconv[:, 3, :],
                         loss[0:1, :]])[None]
    (dq_bf, dk_bf, dv_bf), got_ffn = _attn_bwd(
        z, o_bf, lse, do_bf, cos_t, sin_t, nb,
        exchange("attn_bwd", [("w_ffn_gate", d_wg), ("w_ffn_up", d_wu), ("conv_w", dconv), ("small", small)]))
    dz_bf = jnp.concatenate([dq_bf, dk_bf, dv_bf, du_bf, dzg_bf], axis=1)
    d_win, got_branch = _matmul_tn(
        x_bf, dz_bf, N_CHIPS, "dw_in",
        cargo=exchange("dw_in", [("w_out", d_wout), ("w_branch_attn", d_wba), ("w_branch_pool", d_wbp)]))
    (grad_x,), got_in = _inproj_dx(dz_bf, win_g, dh1, exchange("inproj_dx", [("w_in", d_win)]))
    return grad_x, got_down + got_ffn + got_branch + got_in


SMALL_ROWS = 592


def _pack_small(parts):
    flat = jnp.concatenate([p.reshape(-1) for p in parts])
    return jnp.pad(flat, (0, SMALL_ROWS * LANES - flat.shape[0])).reshape(SMALL_ROWS, LANES)


def _unpack_small(packed, shapes):
    flat = packed.reshape(-1)
    out, off = [], 0
    for shp in shapes:
        n = 1
        for d in shp:
            n *= d
        out.append(flat[off:off + n].reshape(shp))
        off += n
    return out


def _pad_conv_b(cb):
    return jnp.pad(cb.reshape(N_CHIPS, FF_SHARD), ((0, 0), (0, FF_PAD - FF_SHARD)))


def kernel(x, w_in, b_gate, w_branch_attn, w_pool, pool_scale, w_branch_pool, w_out, ln1_g, ln1_b, w_ffn_gate, w_ffn_up, conv_w, conv_b, w_ffn_down, ln2_g, ln2_b, loss_target, m_w_in, m_b_gate, m_w_branch_attn, m_w_pool, m_pool_scale, m_w_branch_pool, m_w_out, m_ln1_g, m_ln1_b, m_w_ffn_gate, m_w_ffn_up, m_conv_w, m_conv_b, m_w_ffn_down, m_ln2_g, m_ln2_b, v_w_in, v_b_gate, v_w_branch_attn, v_w_pool, v_pool_scale, v_w_branch_pool, v_w_out, v_ln1_g, v_ln1_b, v_w_ffn_gate, v_w_ffn_up, v_conv_w, v_conv_b, v_w_ffn_down, v_ln2_g, v_ln2_b):
    pad_rows = lambda w: jnp.pad(w.astype(bf16), ((0, FF_PAD - FF_SHARD), (0, 0)))
    shards = [w_in[0].astype(bf16), w_branch_attn[0].astype(bf16), w_branch_pool[0].astype(bf16),
              w_out[0].astype(bf16), pad_rows(w_ffn_gate[0].T), pad_rows(w_ffn_up[0].T), pad_rows(w_ffn_down[0]),
              jnp.pad(conv_w[0], ((0, CONV_ROWS - 3), (0, FF_PAD - FF_SHARD)))]
    (win_g,) = _cargo_alone(_GatherCargo(shards[:1], [0.0]), "gather_w_in")
    gather = {"inproj": _GatherCargo(shards[1:4], [0.3, 0.5, 0.75]),
              "attn_fwd": _GatherCargo(shards[4:6] + shards[7:], [0.45, 0.88, 0.9]),
              "ffn_up": _GatherCargo(shards[6:7], [0.8])}

    place = jnp.stack([2 * lax.axis_index("x") + lax.axis_index("y"), lax.axis_index("c")]).astype(jnp.int32)
    chip_sums = {}

    def exchange(carrier, named):
        split = [_split_halves(g) for _, g in named]
        got = _presum_swap(split, "presum_swap_" + carrier)
        sums = [_add_pair(a, b, place, "presum_" + nm) for a, b, (nm, _) in zip(split, got, named)]
        chip_sums.update({nm: s for (nm, _), s in zip(named, sums)})
        return _ExchangeCargo(sums, [nm == "small" for nm, _ in named])

    convb_g = _pad_conv_b(conv_b).reshape(N_CHIPS, 1, FF_PAD)
    grad_x, brought = _local_step(
        x[0], loss_target[0], win_g, gather, exchange, b_gate, w_pool[0], pool_scale, ln1_g, ln1_b, convb_g, ln2_g, ln2_b)
    slots = dict(zip(chip_sums, brought))
    names = ["w_in", "w_branch_attn", "w_branch_pool", "w_out", "w_ffn_gate", "w_ffn_up", "w_ffn_down", "conv_w", "small"]
    grads = _sibling_fill([_sum_slots(chip_sums[nm], slots[nm], place, nm == "small", "sum_" + nm) for nm in names])

    weights = [w_in[0], w_branch_attn[0], w_branch_pool[0], w_out[0], w_ffn_gate[0].T, w_ffn_up[0].T, w_ffn_down[0], conv_w[0]]
    m_in = [m_w_in[0], m_w_branch_attn[0], m_w_branch_pool[0], m_w_out[0], m_w_ffn_gate[0].T, m_w_ffn_up[0].T, m_w_ffn_down[0], m_conv_w[0]]
    v_in = [v_w_in[0], v_w_branch_attn[0], v_w_branch_pool[0], v_w_out[0], v_w_ffn_gate[0].T, v_w_ffn_up[0].T, v_w_ffn_down[0], v_conv_w[0]]
    tiles = [256, 256, 256, 128, 88, 88, 88, 3]
    res = {}
    for i, nm in enumerate(names[:-1]):
        outs4 = _adamw(weights[i], m_in[i], v_in[i], grads[i], "adamw_" + nm, tiles[i])
        res[nm] = [(r.T if nm in ("w_ffn_gate", "w_ffn_up") else r)[None] for r in outs4]
    loss = grads[-1][SMALL_ROWS - 4, 0]

    small_names = ["b_gate", "w_pool", "pool_scale", "ln1_g", "ln1_b", "ln2_g", "ln2_b", "conv_b"]
    small_w = [b_gate, w_pool, pool_scale, ln1_g, ln1_b, ln2_g, ln2_b, _pad_conv_b(conv_b)]
    small_m = [m_b_gate, m_w_pool, m_pool_scale, m_ln1_g, m_ln1_b, m_ln2_g, m_ln2_b, _pad_conv_b(m_conv_b)]
    small_v = [v_b_gate, v_w_pool, v_pool_scale, v_ln1_g, v_ln1_b, v_ln2_g, v_ln2_b, _pad_conv_b(v_conv_b)]
    packed = _adamw(_pack_small(small_w), _pack_small(small_m), _pack_small(small_v), grads[-1],
                    "adamw_small", SMALL_ROWS)
    shapes = [w.shape for w in small_w]
    for kind in range(4):
        for nm, val in zip(small_names, _unpack_small(packed[kind], shapes)):
            if nm == "conv_b":
                val = val[:, :FF_SHARD].reshape(1, 4 * FF_SHARD)
            res.setdefault(nm, [None] * 4)[kind] = val

    order = ["w_in", "b_gate", "w_branch_attn", "w_pool", "pool_scale", "w_branch_pool", "w_out", "ln1_g", "ln1_b",
             "w_ffn_gate", "w_ffn_up", "conv_w", "conv_b", "w_ffn_down", "ln2_g", "ln2_b"]
    outs = [loss, grad_x[None]]
    for kind in range(4):
        outs += [res[nm][kind] for nm in order]
    return tuple(outs)
```

```python
import functools

import jax
import jax.numpy as jnp
from jax import lax
from jax.experimental import pallas as pl
from jax.experimental.pallas import tpu as pltpu

D_MODEL = 1024
HEAD_DIM = 64
D_ATTN = 512
D_POOL = 512
MOBA_BLOCK = 256
MOBA_TOPK = 3
POOL_GROUP = 128
MAX_WINDOW = 16
FF_SHARD = 704
FF_PAD = 768
D_FF_PAD = 4 * FF_PAD
N_CHIPS = 4
LANES = 128
ALPHA = (2.0 * 1) ** 0.25
LN_EPS = 1e-5
NEG = -1e30
SCALE = HEAD_DIM ** -0.5
ADAM_LR, ADAM_B1, ADAM_B2, ADAM_EPS, ADAM_WD, ADAM_STEP = 0.001, 0.9, 0.999, 1e-08, 0.01, 10
VMEM_LIMIT = 56 * 1024 * 1024
MESH = pl.DeviceIdType.MESH

bf16 = jnp.bfloat16
f32 = jnp.float32


def _dot(a, b):
    return jnp.dot(a, b, preferred_element_type=f32)


def _dot_nt(a, b):
    return lax.dot_general(a, b, (((1,), (1,)), ((), ())), preferred_element_type=f32)


def _dot_tn(a, b):
    return lax.dot_general(a, b, (((0,), (0,)), ((), ())), preferred_element_type=f32)


def _call(body, *, name, grid, in_specs, out_specs, out_shape, scratch=(), cargo=None):
    if cargo is not None:
        return _cargo_call(body, cargo, name=name, grid=grid, in_specs=in_specs, out_specs=out_specs,
                           out_shape=out_shape, scratch=scratch)
    return pl.pallas_call(
        body, name=name, grid=grid, in_specs=in_specs, out_specs=out_specs, out_shape=out_shape,
        scratch_shapes=list(scratch),
        compiler_params=pltpu.CompilerParams(dimension_semantics=("arbitrary",) * len(grid),
                                             vmem_limit_bytes=VMEM_LIMIT))


def _rot_half(t):
    lane = lax.broadcasted_iota(jnp.int32, t.shape, 1)
    first = (lane % HEAD_DIM) < (HEAD_DIM // 2)
    return jnp.where(first, pltpu.roll(t, LANES - HEAD_DIM // 2, 1), pltpu.roll(t, HEAD_DIM // 2, 1))


def _rope(t, cos, sin_signed):
    return t * cos + _rot_half(t) * sin_signed


def _rope_bwd(d, cos, sin_signed):
    return d * cos + _rot_half(d * sin_signed)


def _gelu_parts(a):
    cdf = 0.5 * (1.0 + lax.erf(a * (2.0 ** -0.5)))
    pdf = jnp.exp(-0.5 * a * a) * ((2.0 * jnp.pi) ** -0.5)
    return a * cdf, cdf + a * pdf


def _layer_norm(h, g, b):
    mu = jnp.mean(h, axis=-1, keepdims=True)
    xc = h - mu
    var = jnp.mean(xc * xc, axis=-1, keepdims=True)
    rstd = lax.rsqrt(var + LN_EPS)
    xhat = xc * rstd
    return xhat * g + b, xhat, rstd


def _layer_norm_bwd(dy, xhat, rstd, g):
    dxh = dy * g
    m1 = jnp.mean(dxh, axis=-1, keepdims=True)
    m2 = jnp.mean(dxh * xhat, axis=-1, keepdims=True)
    return rstd * (dxh - m1 - xhat * m2)


def _inproj(x_bf, win_g, cos_t, sin_t, b_gate, cargo):
    s = x_bf.shape[0]
    tm, tn = 1024, 512

    def body(x_ref, w_ref, cos_ref, sin_ref, b_ref, o_ref):
        j = pl.program_id(1)
        acc = _dot(x_ref[...], w_ref[...])

        @pl.when(j < 2)
        def _():
            for c in range(tn // LANES):
                sl = slice(c * LANES, (c + 1) * LANES)
                o_ref[:, sl] = _rope(acc[:, sl], cos_ref[...], sin_ref[...])

        @pl.when((j >= 2) & (j < 4))
        def _():
            o_ref[...] = acc

        @pl.when(j >= 4)
        def _():
            o_ref[...] = jax.nn.sigmoid(acc + b_ref[...])

    (z,), carried = _cargo_call(
        body, cargo, name="inproj", grid=(s // tm, 4 * D_MODEL // tn),
        in_specs=[pl.BlockSpec((tm, D_MODEL), lambda i, j: (i, 0)),
                  pl.BlockSpec((None, D_MODEL, tn), lambda i, j: (j // 2, 0, j % 2)),
                  pl.BlockSpec((tm, LANES), lambda i, j: (i, 0)),
                  pl.BlockSpec((tm, LANES), lambda i, j: (i, 0)),
                  pl.BlockSpec((1, tn), lambda i, j: (0, jnp.maximum(j - 4, 0)))],
        out_specs=[pl.BlockSpec((tm, tn), lambda i, j: (i, j))],
        out_shape=[jax.ShapeDtypeStruct((s, 4 * D_MODEL), f32)])(x_bf, win_g, cos_t, sin_t, b_gate)
    return z, carried


STAT_ROWS = 8


def _pair_rows(v0, v1, fill):
    r = lax.broadcasted_iota(jnp.int32, (STAT_ROWS, v0.shape[1]), 0)
    return jnp.where(r == 0, v0, jnp.where(r == 1, v1, fill))


def _head_lanes(shape):
    lane = lax.broadcasted_iota(jnp.int32, shape, 1)
    return lane < HEAD_DIM, lane >= HEAD_DIM


def _head_rows(shape):
    row = lax.broadcasted_iota(jnp.int32, shape, 0)
    return row < HEAD_DIM, row >= HEAD_DIM


def _moba_select(q_bf, k_all, i, nb):
    k_mean = jnp.mean(k_all.reshape(nb, MOBA_BLOCK, LANES), axis=1)
    n_io = lax.broadcasted_iota(jnp.int32, (nb, MOBA_BLOCK), 0)
    past = n_io < i
    sels = []
    for head in _head_lanes((nb, LANES)):
        gate = _dot_nt(jnp.where(head, k_mean, 0.0).astype(bf16), q_bf)
        g = jnp.where(past, gate, NEG)
        rank = jnp.zeros((nb, MOBA_BLOCK), f32)
        for m in range(nb):
            gm = g[m:m + 1, :]
            rank = rank + jnp.where((gm > g) | ((gm == g) & (m < n_io)), 1.0, 0.0)
        sels.append(jnp.where(past & (rank < MOBA_TOPK), 1.0, 0.0))
    return sels


def _attn_fwd(z, nb, cargo):
    s = z.shape[0]

    def body(q_ref, k_ref, v_ref, o_ref, lse_ref, vt_s):
        i = pl.program_id(1)

        @pl.when(i == 0)
        def _():
            for j in range(nb):
                vt_s[j] = v_ref[j * MOBA_BLOCK:(j + 1) * MOBA_BLOCK, :].T.astype(bf16)

        k_heads = _head_lanes((MOBA_BLOCK, LANES))
        o_heads = _head_rows((LANES, MOBA_BLOCK))

        def block(j, qs, allow, m_old, l_old):
            kj, vtj = k_ref[j * MOBA_BLOCK:(j + 1) * MOBA_BLOCK, :], vt_s[j]
            m_new, alpha, l_new, pv = [], [], [], None
            for h in range(2):
                sc = jnp.where(allow[h], _dot_nt(jnp.where(k_heads[h], kj, 0.0).astype(bf16), qs), NEG)
                mx = jnp.max(sc, axis=0, keepdims=True)
                mn = mx if m_old is None else jnp.maximum(m_old[h], mx)
                p = jnp.exp(sc - mn)
                lsum = jnp.sum(p, axis=0, keepdims=True)
                if m_old is None:
                    a = None
                else:
                    a = jnp.exp(m_old[h] - mn)
                    lsum = a * l_old[h] + lsum
                t = _dot(jnp.where(o_heads[h], vtj, jnp.zeros_like(vtj)), p.astype(bf16))
                pv = t if pv is None else pv + t
                m_new.append(mn)
                alpha.append(a)
                l_new.append(lsum)
            return m_new, alpha, l_new, pv

        def query_block(c):
            q = q_ref[...]
            sels = _moba_select(q.astype(bf16), k_ref[...], c, nb)
            qs = (q * SCALE).astype(bf16)
            key = lax.broadcasted_iota(jnp.int32, (MOBA_BLOCK, MOBA_BLOCK), 0)
            qry = lax.broadcasted_iota(jnp.int32, (MOBA_BLOCK, MOBA_BLOCK), 1)
            m, _, l, acc = block(c, qs, [key <= qry] * 2, None, None)
            for j in range(c):
                m, alpha, l, pv = block(j, qs, [sels[h][j:j + 1, :] > 0.0 for h in range(2)], m, l)
                acc = acc * jnp.where(o_heads[0], alpha[0], alpha[1]) + pv
            o_ref[...] = (acc / jnp.where(o_heads[0], l[0], l[1])).T.astype(bf16)
            lse_ref[...] = _pair_rows(m[0] + jnp.log(l[0]), m[1] + jnp.log(l[1]), 0.0)

        for c in range(nb):
            pl.when(i == c)(functools.partial(query_block, c))

    return _cargo_call(
        body, cargo, name="attn_fwd", grid=(D_ATTN // LANES, nb),
        in_specs=[pl.BlockSpec((MOBA_BLOCK, LANES), lambda hp, i: (i, hp)),
                  pl.BlockSpec((s, LANES), lambda hp, i: (0, 4 + hp)),
                  pl.BlockSpec((s, LANES), lambda hp, i: (0, 8 + hp))],
        out_specs=[pl.BlockSpec((MOBA_BLOCK, LANES), lambda hp, i: (i, hp)),
                   pl.BlockSpec((None, STAT_ROWS, MOBA_BLOCK), lambda hp, i: (hp, 0, i))],
        out_shape=[jax.ShapeDtypeStruct((s, D_ATTN), bf16),
                   jax.ShapeDtypeStruct((D_ATTN // LANES, STAT_ROWS, s), f32)],
        scratch=[pltpu.VMEM((nb, LANES, MOBA_BLOCK), bf16)])(z, z, z)


def _window_select(snaps, g):
    return jnp.where(g == 0, snaps[0], jnp.where(g == 1, snaps[1], jnp.where(g == 2, snaps[2], snaps[3])))


def _pool_fwd(z, w_pool, pool_scale):
    s = z.shape[0]

    def body(u_ref, w_ref, sc_ref, pooled_ref, mixed_ref, pm_ref, pad):
        g = pl.program_id(0)
        u = u_ref[...]
        pad[0:MAX_WINDOW, :] = jnp.zeros((MAX_WINDOW, POOL_GROUP), f32)
        pad[MAX_WINDOW:MAX_WINDOW + s, :] = u
        acc = u
        snaps = []
        for d in range(1, MAX_WINDOW):
            acc = acc + pad[MAX_WINDOW - d:MAX_WINDOW - d + s, :]
            if d + 1 in (2, 4, 8, 16):
                snaps.append(acc)
        win = _window_select(snaps, g)
        t = lax.broadcasted_iota(jnp.int32, (s, POOL_GROUP), 0)
        count = jnp.minimum(t + 1, jnp.left_shift(2, g)).astype(f32)
        pooled = (win / count - u).astype(bf16)
        mixed = _dot(pooled, w_ref[...].astype(bf16))
        pooled_ref[...] = pooled
        mixed_ref[...] = mixed
        pm_ref[...] = (mixed * sc_ref[...]).astype(bf16)

    blk = pl.BlockSpec((s, POOL_GROUP), lambda g: (0, g))
    return _call(
        body, name="pool_fwd", grid=(4,),
        in_specs=[pl.BlockSpec((s, POOL_GROUP), lambda g: (0, 12 + g)),
                  pl.BlockSpec((None, POOL_GROUP, POOL_GROUP), lambda g: (g, 0, 0)),
                  pl.BlockSpec((1, POOL_GROUP), lambda g: (0, g))],
        out_specs=[blk, blk, blk],
        out_shape=[jax.ShapeDtypeStruct((s, D_POOL), bf16), jax.ShapeDtypeStruct((s, D_POOL), f32),
                   jax.ShapeDtypeStruct((s, D_POOL), bf16)],
        scratch=[pltpu.VMEM((s + MAX_WINDOW, POOL_GROUP), f32)])(z, w_pool, pool_scale)


def _branch_merge(o_bf, pm_bf, z, wba, wbp):
    s = o_bf.shape[0]
    tm = 256

    def body(o_ref, pm_ref, ga_ref, gp_ref, wba_ref, wbp_ref, m_ref):
        ya = _dot(o_ref[...], wba_ref[...])
        yp = _dot(pm_ref[...], wbp_ref[...])
        m_ref[...] = (ga_ref[...] * ya + gp_ref[...] * yp).astype(bf16)

    full = lambda r, c: pl.BlockSpec((r, c), lambda i: (0, 0))
    return _call(
        body, name="branch_merge", grid=(s // tm,),
        in_specs=[pl.BlockSpec((tm, D_ATTN), lambda i: (i, 0)), pl.BlockSpec((tm, D_POOL), lambda i: (i, 0)),
                  pl.BlockSpec((tm, D_MODEL), lambda i: (i, 2)), pl.BlockSpec((tm, D_MODEL), lambda i: (i, 3)),
                  full(D_ATTN, D_MODEL), full(D_POOL, D_MODEL)],
        out_specs=pl.BlockSpec((tm, D_MODEL), lambda i: (i, 0)),
        out_shape=jax.ShapeDtypeStruct((s, D_MODEL), bf16))(o_bf, pm_bf, z, z, wba, wbp)


def _out_ln1(m_bf, wout, x, ln_g, ln_b):
    s = x.shape[0]
    tm = 256

    def body(m_ref, w_ref, x_ref, g_ref, b_ref, x1_ref, xhat_ref, rstd_ref):
        h = ALPHA * x_ref[...] + _dot(m_ref[...], w_ref[...])
        y, xhat, rstd = _layer_norm(h, g_ref[...], b_ref[...])
        x1_ref[...] = y.astype(bf16)
        xhat_ref[...] = xhat
        rstd_ref[...] = jnp.broadcast_to(rstd, (tm, LANES))

    row = pl.BlockSpec((tm, D_MODEL), lambda i: (i, 0))
    vec = pl.BlockSpec((1, D_MODEL), lambda i: (0, 0))
    return _call(
        body, name="out_ln1", grid=(s // tm,),
        in_specs=[row, pl.BlockSpec((D_MODEL, D_MODEL), lambda i: (0, 0)), row, vec, vec],
        out_specs=[row, row, pl.BlockSpec((tm, LANES), lambda i: (i, 0))],
        out_shape=[jax.ShapeDtypeStruct((s, D_MODEL), bf16), jax.ShapeDtypeStruct((s, D_MODEL), f32),
                   jax.ShapeDtypeStruct((s, LANES), f32)])(m_bf, wout, x, ln_g, ln_b)


FF_TILE = 256
FF_TILES_PER_SHARD = FF_PAD // FF_TILE
CONV_PAD = 8
CONV_ROWS = 16


def _ff_weight_spec(rows):
    return pl.BlockSpec((None, rows, FF_TILE), lambda j: (j // FF_TILES_PER_SHARD, 0, j % FF_TILES_PER_SHARD))


def _ff_rows_spec():
    return pl.BlockSpec((None, FF_TILE, D_MODEL), lambda j: (j // FF_TILES_PER_SHARD, j % FF_TILES_PER_SHARD, 0))


def _conv_taps(pad, s, cw_ref, off=0):
    return (pad[CONV_PAD + off:CONV_PAD + off + s, :] * cw_ref[2:3, :]
            + pad[CONV_PAD - 1 + off:CONV_PAD - 1 + off + s, :] * cw_ref[1:2, :]
            + pad[CONV_PAD - 2 + off:CONV_PAD - 2 + off + s, :] * cw_ref[0:1, :])


def _ffn_up(x1_bf, wg_g, wu_g, convw_g, convb_g, cargo):
    s = x1_bf.shape[0]

    def body(x_ref, wg_ref, wu_ref, cw_ref, cb_ref, a_ref, up_ref, hh_ref, pad):
        x1 = x_ref[...]
        a = _dot_nt(x1, wg_ref[...])
        up = _dot_nt(x1, wu_ref[...])
        a_ref[...] = a
        up_ref[...] = up
        pad[0:CONV_PAD, :] = jnp.zeros((CONV_PAD, FF_TILE), f32)
        pad[CONV_PAD:CONV_PAD + s, :] = a
        ac = _conv_taps(pad, s, cw_ref) + cb_ref[...]
        hg, _ = _gelu_parts(ac)
        hh_ref[...] = (hg * up).astype(bf16)

    col = pl.BlockSpec((s, FF_TILE), lambda j: (0, j))
    return _call(
        body, name="ffn_up", grid=(D_FF_PAD // FF_TILE,),
        in_specs=[pl.BlockSpec((s, D_MODEL), lambda j: (0, 0)), _ff_rows_spec(), _ff_rows_spec(),
                  _ff_weight_spec(8), _ff_weight_spec(1)],
        out_specs=[col, col, col],
        out_shape=[jax.ShapeDtypeStruct((s, D_FF_PAD), f32), jax.ShapeDtypeStruct((s, D_FF_PAD), f32),
                   jax.ShapeDtypeStruct((s, D_FF_PAD), bf16)],
        scratch=[pltpu.VMEM((s + CONV_PAD, FF_TILE), f32)], cargo=cargo)(x1_bf, wg_g, wu_g, convw_g, convb_g)


def _ffn_down_ln2_loss(hh_bf, wd, xhat1, ln1_g, ln1_b, ln2_g, ln2_b, target):
    s = hh_bf.shape[0]
    tm = 256

    def body(hh_ref, w_ref, xh1_ref, g1_ref, b1_ref, g2_ref, b2_ref, t_ref, loss_ref, dh_ref, dhb_ref, dg_ref, db_ref):
        i = pl.program_id(0)
        x1 = xh1_ref[...] * g1_ref[...] + b1_ref[...]
        h = ALPHA * x1 + _dot(hh_ref[...], w_ref[...])
        y, xhat, rstd = _layer_norm(h, g2_ref[...], b2_ref[...])
        err = y - t_ref[...]
        part = 0.5 * jnp.sum(jnp.mean(err * err, axis=-1, keepdims=True), axis=0, keepdims=True)
        dy = err * (1.0 / D_MODEL)

        @pl.when(i == 0)
        def _():
            loss_ref[...] = jnp.zeros_like(loss_ref)
            dg_ref[...] = jnp.zeros_like(dg_ref)
            db_ref[...] = jnp.zeros_like(db_ref)

        loss_ref[...] += jnp.broadcast_to(part, loss_ref.shape)
        dg_ref[...] += jnp.sum(dy * xhat, axis=0, keepdims=True)
        db_ref[...] += jnp.sum(dy, axis=0, keepdims=True)
        dh = _layer_norm_bwd(dy, xhat, rstd, g2_ref[...])
        dh_ref[...] = dh
        dhb_ref[...] = dh.astype(bf16)

    row = pl.BlockSpec((tm, D_MODEL), lambda i: (i, 0))
    vec = pl.BlockSpec((1, D_MODEL), lambda i: (0, 0))
    return _call(
        body, name="ffn_down_ln2_loss", grid=(s // tm,),
        in_specs=[pl.BlockSpec((tm, D_FF_PAD), lambda i: (i, 0)), pl.BlockSpec((D_FF_PAD, D_MODEL), lambda i: (0, 0)),
                  row, vec, vec, vec, vec, row],
        out_specs=[pl.BlockSpec((8, LANES), lambda i: (0, 0)), row, row, vec, vec],
        out_shape=[jax.ShapeDtypeStruct((8, LANES), f32), jax.ShapeDtypeStruct((s, D_MODEL), f32),
                   jax.ShapeDtypeStruct((s, D_MODEL), bf16), jax.ShapeDtypeStruct((1, D_MODEL), f32),
                   jax.ShapeDtypeStruct((1, D_MODEL), f32)])(hh_bf, wd, xhat1, ln1_g, ln1_b, ln2_g, ln2_b, target)


def _ffn_act_bwd(dh2_bf, wd_g, a, up, convw_g, convb_g):
    s = dh2_bf.shape[0]

    def body(dh_ref, wd_ref, a_ref, up_ref, cw_ref, cb_ref, da_ref, dup_ref, dc_ref, pad, dpad):
        dhh = _dot_nt(dh_ref[...], wd_ref[...])
        zeros = jnp.zeros((CONV_PAD, FF_TILE), f32)
        pad[0:CONV_PAD, :] = zeros
        pad[CONV_PAD:CONV_PAD + s, :] = a_ref[...]
        ac = _conv_taps(pad, s, cw_ref) + cb_ref[...]
        hg, dgelu = _gelu_parts(ac)
        up = up_ref[...]
        dup_ref[...] = (dhh * hg).astype(bf16)
        dac = dhh * up * dgelu
        rows = [jnp.sum(dac * pad[CONV_PAD - 2 + i:CONV_PAD - 2 + i + s, :], axis=0, keepdims=True) for i in range(3)]
        rows.append(jnp.sum(dac, axis=0, keepdims=True))
        rows.append(jnp.zeros((CONV_ROWS - 4, FF_TILE), f32))
        dc_ref[...] = jnp.concatenate(rows, axis=0)
        dpad[0:s, :] = dac
        dpad[s:s + CONV_PAD, :] = zeros
        da = (dpad[0:s, :] * cw_ref[2:3, :] + dpad[1:1 + s, :] * cw_ref[1:2, :] + dpad[2:2 + s, :] * cw_ref[0:1, :])
        da_ref[...] = da.astype(bf16)

    col = pl.BlockSpec((s, FF_TILE), lambda j: (0, j))
    return _call(
        body, name="ffn_act_bwd", grid=(D_FF_PAD // FF_TILE,),
        in_specs=[pl.BlockSpec((s, D_MODEL), lambda j: (0, 0)), _ff_rows_spec(),
                  col, col, _ff_weight_spec(8), _ff_weight_spec(1)],
        out_specs=[col, col, _ff_weight_spec(CONV_ROWS)],
        out_shape=[jax.ShapeDtypeStruct((s, D_FF_PAD), bf16), jax.ShapeDtypeStruct((s, D_FF_PAD), bf16),
                   jax.ShapeDtypeStruct((N_CHIPS, CONV_ROWS, FF_PAD), f32)],
        scratch=[pltpu.VMEM((s + CONV_PAD, FF_TILE), f32), pltpu.VMEM((s + CONV_PAD, FF_TILE), f32)],
    )(dh2_bf, wd_g, a, up, convw_g, convb_g)


def _matmul_tn(a, b, n_shards, name, tm=512, tn=1024, cargo=None):
    k, m = a.shape
    n = b.shape[1]
    tm, tn = min(tm, m), min(tn, n // n_shards)
    per = n // n_shards // tn

    def body(a_ref, b_ref, o_ref, at_s):
        @pl.when(pl.program_id(1) == 0)
        def _():
            at_s[...] = a_ref[...].T

        o_ref[...] = _dot(at_s[...], b_ref[...]).astype(bf16)

    res = _call(
        body, name=name, grid=(m // tm, n // tn),
        in_specs=[pl.BlockSpec((k, tm), lambda i, j: (0, i)), pl.BlockSpec((k, tn), lambda i, j: (0, j))],
        out_specs=[pl.BlockSpec((None, tm, tn), lambda i, j: (j // per, i, j % per))],
        out_shape=[jax.ShapeDtypeStruct((n_shards, m, n // n_shards), bf16)], scratch=[pltpu.VMEM((tm, k), bf16)],
        cargo=cargo)(a, b)
    return res[0] if cargo is None else (res[0][0], res[1])


def _ffn_in_bwd_ln1(da_bf, dup_bf, wg_g, wu_g, dh2, xhat1, rstd1, ln1_g, cargo):
    s = da_bf.shape[0]
    tm = 256

    def body(da_ref, dup_ref, wg_ref, wu_ref, dh2_ref, xh_ref, rstd_ref, g_ref, dh_ref, dhb_ref, dg_ref, db_ref):
        i = pl.program_id(0)
        dx1 = ALPHA * dh2_ref[...]
        for sh in range(N_CHIPS):
            sl = slice(sh * FF_PAD, (sh + 1) * FF_PAD)
            dx1 = dx1 + _dot(da_ref[:, sl], wg_ref[sh]) + _dot(dup_ref[:, sl], wu_ref[sh])
        xhat = xh_ref[...]

        @pl.when(i == 0)
        def _():
            dg_ref[...] = jnp.zeros_like(dg_ref)
            db_ref[...] = jnp.zeros_like(db_ref)

        dg_ref[...] += jnp.sum(dx1 * xhat, axis=0, keepdims=True)
        db_ref[...] += jnp.sum(dx1, axis=0, keepdims=True)
        dh = _layer_norm_bwd(dx1, xhat, rstd_ref[:, 0:1], g_ref[...])
        dh_ref[...] = dh
        dhb_ref[...] = dh.astype(bf16)

    row = pl.BlockSpec((tm, D_MODEL), lambda i: (i, 0))
    wide = pl.BlockSpec((tm, D_FF_PAD), lambda i: (i, 0))
    wfull = pl.BlockSpec((N_CHIPS, FF_PAD, D_MODEL), lambda i: (0, 0, 0))
    vec = pl.BlockSpec((1, D_MODEL), lambda i: (0, 0))
    return _call(
        body, name="ffn_in_bwd_ln1", grid=(s // tm,),
        in_specs=[wide, wide, wfull, wfull, row, row, pl.BlockSpec((tm, LANES), lambda i: (i, 0)), vec],
        out_specs=[row, row, vec, vec],
        out_shape=[jax.ShapeDtypeStruct((s, D_MODEL), f32), jax.ShapeDtypeStruct((s, D_MODEL), bf16),
                   jax.ShapeDtypeStruct((1, D_MODEL), f32), jax.ShapeDtypeStruct((1, D_MODEL), f32)],
        cargo=cargo)(da_bf, dup_bf, wg_g, wu_g, dh2, xhat1, rstd1, ln1_g)


def _merge_bwd(dh1_bf, wout, o_bf, pm_bf, z, wba, wbp):
    s = dh1_bf.shape[0]
    tm = 256

    def body(dh_ref, wout_ref, o_ref, pm_ref, ga_ref, gp_ref, wba_ref, wbp_ref,
             dzg_ref, dya_ref, dyp_ref, do_ref, dpm_ref, dbg_ref):
        i = pl.program_id(0)
        dm = _dot_nt(dh_ref[...], wout_ref[...])
        ya = _dot(o_ref[...], wba_ref[...])
        yp = _dot(pm_ref[...], wbp_ref[...])
        ga, gp = ga_ref[...], gp_ref[...]
        dza = dm * ya * ga * (1.0 - ga)
        dzp = dm * yp * gp * (1.0 - gp)

        @pl.when(i == 0)
        def _():
            dbg_ref[...] = jnp.zeros_like(dbg_ref)

        dbg_ref[:, 0:D_MODEL] += jnp.sum(dza, axis=0, keepdims=True)
        dbg_ref[:, D_MODEL:2 * D_MODEL] += jnp.sum(dzp, axis=0, keepdims=True)
        dzg_ref[:, 0:D_MODEL] = dza.astype(bf16)
        dzg_ref[:, D_MODEL:2 * D_MODEL] = dzp.astype(bf16)
        dya = (dm * ga).astype(bf16)
        dyp = (dm * gp).astype(bf16)
        dya_ref[...] = dya
        dyp_ref[...] = dyp
        do_ref[...] = _dot_nt(dya, wba_ref[...]).astype(bf16)
        dpm_ref[...] = _dot_nt(dyp, wbp_ref[...])

    row = pl.BlockSpec((tm, D_MODEL), lambda i: (i, 0))
    half = pl.BlockSpec((tm, D_ATTN), lambda i: (i, 0))
    full = lambda r, c: pl.BlockSpec((r, c), lambda i: (0, 0))
    return _call(
        body, name="merge_bwd", grid=(s // tm,),
        in_specs=[row, full(D_MODEL, D_MODEL), half, half,
                  pl.BlockSpec((tm, D_MODEL), lambda i: (i, 2)), pl.BlockSpec((tm, D_MODEL), lambda i: (i, 3)),
                  full(D_ATTN, D_MODEL), full(D_POOL, D_MODEL)],
        out_specs=[pl.BlockSpec((tm, 2 * D_MODEL), lambda i: (i, 0)), row, row, half, half,
                   pl.BlockSpec((1, 2 * D_MODEL), lambda i: (0, 0))],
        out_shape=[jax.ShapeDtypeStruct((s, 2 * D_MODEL), bf16), jax.ShapeDtypeStruct((s, D_MODEL), bf16),
                   jax.ShapeDtypeStruct((s, D_MODEL), bf16), jax.ShapeDtypeStruct((s, D_ATTN), bf16),
                   jax.ShapeDtypeStruct((s, D_POOL), f32), jax.ShapeDtypeStruct((1, 2 * D_MODEL), f32)],
    )(dh1_bf, wout, o_bf, pm_bf, z, z, wba, wbp)


def _pool_bwd(dpm, mixed, pooled_bf, w_pool, pool_scale):
    s = dpm.shape[0]

    def body(dpm_ref, mixed_ref, pooled_ref, w_ref, sc_ref, du_ref, dw_ref, dsc_ref, pad):
        g = pl.program_id(0)
        dpm_v = dpm_ref[...]
        dsc_ref[...] = jnp.sum(dpm_v * mixed_ref[...], axis=0, keepdims=True)
        dmixed = (dpm_v * sc_ref[...]).astype(bf16)
        dw_ref[...] = _dot_tn(pooled_ref[...], dmixed)
        dpooled = _dot_nt(dmixed, w_ref[...].astype(bf16))
        t = lax.broadcasted_iota(jnp.int32, (s, POOL_GROUP), 0)
        count = jnp.minimum(t + 1, jnp.left_shift(2, g)).astype(f32)
        r = dpooled / count
        pad[0:s, :] = r
        pad[s:s + MAX_WINDOW, :] = jnp.zeros((MAX_WINDOW, POOL_GROUP), f32)
        acc = r
        snaps = []
        for d in range(1, MAX_WINDOW):
            acc = acc + pad[d:d + s, :]
            if d + 1 in (2, 4, 8, 16):
                snaps.append(acc)
        du_ref[...] = (_window_select(snaps, g) - dpooled).astype(bf16)

    blk = pl.BlockSpec((s, POOL_GROUP), lambda g: (0, g))
    return _call(
        body, name="pool_bwd", grid=(4,),
        in_specs=[blk, blk, blk, pl.BlockSpec((None, POOL_GROUP, POOL_GROUP), lambda g: (g, 0, 0)),
                  pl.BlockSpec((1, POOL_GROUP), lambda g: (0, g))],
        out_specs=[blk, pl.BlockSpec((None, POOL_GROUP, POOL_GROUP), lambda g: (g, 0, 0)),
                   pl.BlockSpec((1, POOL_GROUP), lambda g: (0, g))],
        out_shape=[jax.ShapeDtypeStruct((s, D_POOL), bf16), jax.ShapeDtypeStruct((4, POOL_GROUP, POOL_GROUP), f32),
                   jax.ShapeDtypeStruct((1, D_POOL), f32)],
        scratch=[pltpu.VMEM((s + MAX_WINDOW, POOL_GROUP), f32)])(dpm, mixed, pooled_bf, w_pool, pool_scale)


def _attn_bwd(z, o_bf, lse, do_bf, cos_t, sin_t, nb, cargo):
    s = z.shape[0]

    def body(q_ref, k_ref, v_ref, o_ref, lse_ref, do_ref, cq_ref, sq_ref, cf_ref, sf_ref,
             dq_ref, dk_ref, dv_ref, kt_s, dk_s, dv_s):
        i = pl.program_id(1)

        @pl.when(i == 0)
        def _():
            dk_s[...] = jnp.zeros_like(dk_s)
            dv_s[...] = jnp.zeros_like(dv_s)
            for j in range(nb):
                kt_s[j] = k_ref[j * MOBA_BLOCK:(j + 1) * MOBA_BLOCK, :].T.astype(bf16)

        k_heads = _head_lanes((MOBA_BLOCK, LANES))
        t_heads = _head_rows((LANES, MOBA_BLOCK))

        def query_block(c):
            q = q_ref[...]
            sels = _moba_select(q.astype(bf16), k_ref[...], c, nb)
            qs = (q * SCALE).astype(bf16)
            qraw = [jnp.where(hm, q, 0.0).astype(bf16) for hm in k_heads]
            do = do_ref[...].astype(f32)
            do_bf = do_ref[...]
            dob = [jnp.where(hm, do, 0.0).astype(bf16) for hm in k_heads]
            pick = _head_lanes((STAT_ROWS, LANES))
            r8 = lax.broadcasted_iota(jnp.int32, (STAT_ROWS, LANES), 0)
            head_pick = jnp.where(((r8 == 0) & pick[0]) | ((r8 == 1) & pick[1]), 1.0, 0.0)
            delta8 = lax.dot_general(head_pick, do * o_ref[...].astype(f32), (((1,), (1,)), ((), ())),
                                     precision=lax.Precision.HIGHEST, preferred_element_type=f32)
            lse8 = lse_ref[...]
            delta, lse_h = [delta8[0:1, :], delta8[1:2, :]], [lse8[0:1, :], lse8[1:2, :]]
            key = lax.broadcasted_iota(jnp.int32, (MOBA_BLOCK, MOBA_BLOCK), 0)
            qry = lax.broadcasted_iota(jnp.int32, (MOBA_BLOCK, MOBA_BLOCK), 1)

            def block(j, allow):
                rows = slice(j * MOBA_BLOCK, (j + 1) * MOBA_BLOCK)
                kj, vj, ktj = k_ref[rows, :], v_ref[rows, :], kt_s[j]
                dqt, dk, dv = None, None, None
                for h in range(2):
                    sc = _dot_nt(jnp.where(k_heads[h], kj, 0.0).astype(bf16), qs)
                    p = jnp.where(allow[h], jnp.exp(sc - lse_h[h]), 0.0)
                    dp = _dot_nt(jnp.where(k_heads[h], vj, 0.0).astype(bf16), do_bf)
                    ds = (p * (dp - delta[h]) * SCALE).astype(bf16)
                    tv = _dot(p.astype(bf16), dob[h])
                    tk = _dot(ds, qraw[h])
                    tq = _dot(jnp.where(t_heads[h], ktj, jnp.zeros_like(ktj)), ds)
                    dqt, dk, dv = (tq, tk, tv) if dqt is None else (dqt + tq, dk + tk, dv + tv)
                dk_s[rows, :] += dk
                dv_s[rows, :] += dv
                return dqt

            dqt = block(c, [key <= qry] * 2)
            for j in range(c):
                dqt = dqt + block(j, [sels[h][j:j + 1, :] > 0.0 for h in range(2)])
            dq_ref[...] = _rope_bwd(dqt.T, cq_ref[...], sq_ref[...]).astype(bf16)

        for c in range(nb):
            pl.when(i == c)(functools.partial(query_block, c))

        @pl.when(i == nb - 1)
        def _():
            dk_ref[...] = _rope_bwd(dk_s[...], cf_ref[...], sf_ref[...]).astype(bf16)
            dv_ref[...] = dv_s[...].astype(bf16)

    qblk = pl.BlockSpec((MOBA_BLOCK, LANES), lambda hp, i: (i, hp))
    tq = pl.BlockSpec((MOBA_BLOCK, LANES), lambda hp, i: (i, 0))
    tf = pl.BlockSpec((s, LANES), lambda hp, i: (0, 0))
    colblk = pl.BlockSpec((s, LANES), lambda hp, i: (0, hp))
    return _cargo_call(
        body, cargo, name="attn_bwd", grid=(D_ATTN // LANES, nb),
        in_specs=[qblk, pl.BlockSpec((s, LANES), lambda hp, i: (0, 4 + hp)),
                  pl.BlockSpec((s, LANES), lambda hp, i: (0, 8 + hp)), qblk,
                  pl.BlockSpec((None, STAT_ROWS, MOBA_BLOCK), lambda hp, i: (hp, 0, i)), qblk, tq, tq, tf, tf],
        out_specs=[qblk, colblk, colblk],
        out_shape=[jax.ShapeDtypeStruct((s, D_ATTN), bf16)] * 3,
        scratch=[pltpu.VMEM((nb, LANES, MOBA_BLOCK), bf16), pltpu.VMEM((s, LANES), f32), pltpu.VMEM((s, LANES), f32)],
    )(z, z, z, o_bf, lse, do_bf, cos_t, sin_t, cos_t, sin_t)


def _inproj_dx(dz_bf, win_g, dh1, cargo):
    s = dz_bf.shape[0]
    tm = 256

    def body(dz_ref, w_ref, dh_ref, gx_ref):
        acc = ALPHA * dh_ref[...]
        for sh in range(N_CHIPS):
            acc = acc + _dot_nt(dz_ref[:, sh * D_MODEL:(sh + 1) * D_MODEL], w_ref[sh])
        gx_ref[...] = acc

    row = pl.BlockSpec((tm, D_MODEL), lambda i: (i, 0))
    return _call(
        body, name="inproj_dx", grid=(s // tm,),
        in_specs=[pl.BlockSpec((tm, 4 * D_MODEL), lambda i: (i, 0)),
                  pl.BlockSpec((N_CHIPS, D_MODEL, D_MODEL), lambda i: (0, 0, 0)), row],
        out_specs=[row], out_shape=[jax.ShapeDtypeStruct((s, D_MODEL), f32)], cargo=cargo)(dz_bf, win_g, dh1)


ANY = pl.BlockSpec(memory_space=pl.ANY)


def _chip_index():
    return 2 * lax.axis_index("x") + lax.axis_index("y")


def _peer(k):
    x, y, c = lax.axis_index("x"), lax.axis_index("y"), lax.axis_index("c")
    return (x ^ (k >> 1), y ^ (k & 1), c)


def _sibling():
    return (lax.axis_index("x"), lax.axis_index("y"), 1 - lax.axis_index("c"))


class _GatherCargo:
    def __init__(self, shards, pass_on):
        self.shards, self.pass_on, n = shards, pass_on, len(shards)
        self.inputs = [a.reshape(2, a.shape[0] // 2, a.shape[1]) for a in shards]
        self.out_shape = [jax.ShapeDtypeStruct((N_CHIPS,) + a.shape, a.dtype) for a in self.inputs]
        sem, one = pltpu.SemaphoreType.DMA((N_CHIPS - 1, n)), pltpu.SemaphoreType.DMA((n,))
        self.sems = [sem, sem, sem, sem, one, one]

    def _copies(self, src, dst, sems, a):
        send_sems, recv_sems, fsend_sems, frecv_sems, local_sems, own_sems = sems
        p, c = _chip_index(), lax.axis_index("c")
        own = pltpu.make_async_remote_copy(src_ref=src[a], dst_ref=dst[a].at[p], send_sem=local_sems.at[a],
                                           recv_sem=own_sems.at[a], device_id=_sibling(), device_id_type=MESH)
        out, arrive, onward, landed = [], [], [], []
        for k in range(1, N_CHIPS):
            ici = dict(send_sem=send_sems.at[k - 1, a], recv_sem=recv_sems.at[k - 1, a], device_id=_peer(k),
                       device_id_type=MESH)
            d2d = dict(send_sem=fsend_sems.at[k - 1, a], recv_sem=frecv_sems.at[k - 1, a], device_id=_sibling(),
                       device_id_type=MESH)
            got, theirs = dst[a].at[p ^ k, c], dst[a].at[p ^ k, 1 - c]
            out.append(pltpu.make_async_remote_copy(src_ref=src[a].at[c], dst_ref=dst[a].at[p, c], **ici))
            arrive.append(pltpu.make_async_remote_copy(src_ref=src[a].at[c], dst_ref=got, **ici))
            onward.append(pltpu.make_async_remote_copy(src_ref=got, dst_ref=got, **d2d))
            landed.append(pltpu.make_async_remote_copy(src_ref=theirs, dst_ref=theirs, **d2d))
        return own, out, arrive, onward, landed

    def stages(self, steps):
        n = len(self.shards)

        def start(src, dst, sems):
            for a in range(n):
                own, out, _, _, _ = self._copies(src, dst, sems, a)
                own.start()
                for cp in out:
                    cp.start()

        def pass_on(a):
            def act(src, dst, sems):
                _, _, arrive, onward, _ = self._copies(src, dst, sems, a)
                for k in range(N_CHIPS - 1):
                    arrive[k].wait_recv()
                    onward[k].start()
            return act

        def finish(src, dst, sems):
            for a in range(n):
                own, out, _, onward, landed = self._copies(src, dst, sems, a)
                for cp in landed:
                    cp.wait_recv()
                for cp in out + onward:
                    cp.wait_send()
                own.wait()

        mids = [(min(steps - 1, int(self.pass_on[a] * steps)), pass_on(a)) for a in range(n)]
        return [(0, start)] + mids + [(steps - 1, finish)]

    def results(self, outs):
        return [o.reshape((N_CHIPS,) + a.shape) for o, a in zip(outs, self.shards)]


class _ExchangeCargo:
    def __init__(self, sums, whole):
        self.inputs, self.whole, n = sums, whole, len(sums)
        self.out_shape = [jax.ShapeDtypeStruct((N_CHIPS - 1,) + g.shape[1:], g.dtype) for g in sums]
        sem = pltpu.SemaphoreType.DMA((N_CHIPS - 1, n))
        self.sems = [sem, sem]

    def _copies(self, src, dst, sems):
        send_sems, recv_sems = sems
        p = _chip_index()
        return [pltpu.make_async_remote_copy(
            src_ref=src[a].at[0] if self.whole[a] else src[a].at[p ^ k], dst_ref=dst[a].at[k - 1],
            send_sem=send_sems.at[k - 1, a], recv_sem=recv_sems.at[k - 1, a], device_id=_peer(k), device_id_type=MESH)
            for k in range(1, N_CHIPS) for a in range(len(self.inputs))]

    def stages(self, steps):
        def start(src, dst, sems):
            for cp in self._copies(src, dst, sems):
                cp.start()

        def finish(src, dst, sems):
            copies = self._copies(src, dst, sems)
            for cp in copies:
                cp.wait_recv()
            for cp in copies:
                cp.wait_send()

        return [(0, start), (steps - 1, finish)]

    def results(self, outs):
        return list(outs)


def _cargo_call(body, cargo, *, name, grid, in_specs, out_specs, out_shape, scratch=()):
    n_in, n_out, n_scr = len(in_specs), len(out_specs), len(scratch)
    c_in, c_out = len(cargo.inputs), len(cargo.out_shape)
    steps = 1
    for g in grid:
        steps *= g
    stages = cargo.stages(steps)

    def wrapped(*refs):
        ins, refs = refs[:n_in], refs[n_in:]
        cin, refs = refs[:c_in], refs[c_in:]
        outs, refs = refs[:n_out], refs[n_out:]
        cout, refs = refs[:c_out], refs[c_out:]
        scr, sems = refs[:n_scr], refs[n_scr:]
        if not grid:
            for _, act in stages:
                act(cin, cout, sems)
            return
        step = 0
        for d in range(len(grid)):
            step = step * grid[d] + pl.program_id(d)
        pl.when(step == 0)(functools.partial(stages[0][1], cin, cout, sems))
        body(*ins, *outs, *scr)
        for at, act in stages[1:]:
            pl.when(step == at)(functools.partial(act, cin, cout, sems))

    params = dict(vmem_limit_bytes=VMEM_LIMIT)
    if grid:
        params["dimension_semantics"] = ("arbitrary",) * len(grid)
    res = pl.pallas_call(
        wrapped, name=name, grid=grid, in_specs=list(in_specs) + [ANY] * c_in, out_specs=list(out_specs) + [ANY] * c_out,
        out_shape=list(out_shape) + cargo.out_shape, scratch_shapes=list(scratch) + cargo.sems,
        compiler_params=pltpu.CompilerParams(**params))
    return lambda *args: (lambda r: (r[:n_out], cargo.results(r[n_out:])))(res(*args, *cargo.inputs))


def _split_halves(g):
    return g.reshape(g.shape[0], 2, g.shape[1] // 2, g.shape[2])


def _ew_tile(rows):
    return 256 if rows % 256 == 0 else rows


def _prefetch_call(body, *, name, grid, in_specs, out_specs, out_shape, **kw):
    return pl.pallas_call(
        body, name=name, out_shape=out_shape,
        grid_spec=pltpu.PrefetchScalarGridSpec(num_scalar_prefetch=1, grid=grid, in_specs=in_specs, out_specs=out_specs),
        compiler_params=pltpu.CompilerParams(dimension_semantics=("arbitrary",) * len(grid),
                                             vmem_limit_bytes=VMEM_LIMIT), **kw)


def _presum_swap(split, name):
    n = len(split)

    def body(*refs):
        src, got = refs[:n], refs[n:2 * n]
        send_sems, recv_sems = refs[2 * n:]
        c = lax.axis_index("c")
        remote = [pltpu.make_async_remote_copy(src_ref=src[a].at[:, 1 - c], dst_ref=got[a],
                                               send_sem=send_sems.at[a], recv_sem=recv_sems.at[a],
                                               device_id=_sibling(), device_id_type=MESH) for a in range(n)]
        for cp in remote:
            cp.start()
        for cp in remote:
            cp.wait_recv()
        for cp in remote:
            cp.wait_send()

    sem = pltpu.SemaphoreType.DMA((n,))
    return pl.pallas_call(
        body, name=name, in_specs=[ANY] * n, out_specs=[ANY] * n,
        out_shape=[jax.ShapeDtypeStruct((g.shape[0],) + g.shape[2:], g.dtype) for g in split],
        scratch_shapes=[sem, sem])(*split)


def _add_pair(mine, theirs, place, name):
    lead, _, rows, cols = mine.shape
    tm = _ew_tile(rows)

    def body(place_ref, a_ref, b_ref, o_ref):
        o_ref[...] = (a_ref[...].astype(f32) + b_ref[...].astype(f32)).astype(o_ref.dtype)

    blk = pl.BlockSpec((lead, tm, cols), lambda i, pc: (0, i, 0))
    return _prefetch_call(
        body, name=name, grid=(rows // tm,),
        in_specs=[pl.BlockSpec((lead, None, tm, cols), lambda i, pc: (0, pc[1], i, 0)), blk], out_specs=blk,
        out_shape=jax.ShapeDtypeStruct(theirs.shape, theirs.dtype))(place, mine, theirs)


class _NoCargo:
    def __init__(self, results=()):
        self.inputs, self.out_shape, self.sems, self._results = [], [], [], list(results)

    def stages(self, steps):
        return [(0, lambda src, dst, sems: None), (steps - 1, lambda src, dst, sems: None)]

    def results(self, outs):
        return self._results


def _cargo_alone(cargo, name):
    return _cargo_call(None, cargo, name=name, grid=(), in_specs=[], out_specs=[], out_shape=[])()[1]


def _sum_slots(own, recv, place, whole, name):
    _, rows, cols = own.shape
    tm = _ew_tile(rows)

    def body(place_ref, own_ref, r_ref, o_ref):
        r0, r1, r2 = [r_ref[k].astype(f32) for k in range(N_CHIPS - 1)]
        o_ref[...] = (own_ref[...].astype(f32) + r0) + (r1 + r2)

    return _prefetch_call(
        body, name=name, grid=(rows // tm,),
        in_specs=[pl.BlockSpec((None, tm, cols), lambda i, pc: (0 if whole else pc[0], i, 0)),
                  pl.BlockSpec((N_CHIPS - 1, tm, cols), lambda i, pc: (0, i, 0))],
        out_specs=pl.BlockSpec((None, tm, cols), lambda i, pc: (pc[1], i, 0)),
        out_shape=jax.ShapeDtypeStruct((2, rows, cols), f32))(place, own, recv)


def _sibling_fill(bufs):
    n = len(bufs)

    def body(*refs):
        src, dst = refs[:n], refs[n:2 * n]
        send_sems, recv_sems = refs[2 * n:]
        c = lax.axis_index("c")
        remote = [pltpu.make_async_remote_copy(src_ref=src[a].at[c], dst_ref=dst[a].at[c], send_sem=send_sems.at[a],
                                               recv_sem=recv_sems.at[a], device_id=_sibling(), device_id_type=MESH)
                  for a in range(n)]
        for cp in remote:
            cp.start()
        for a in range(n):
            pltpu.make_async_remote_copy(src_ref=src[a].at[c], dst_ref=dst[a].at[1 - c], send_sem=send_sems.at[a],
                                         recv_sem=recv_sems.at[a], device_id=_sibling(), device_id_type=MESH).wait_recv()
        for cp in remote:
            cp.wait_send()

    sem = pltpu.SemaphoreType.DMA((n,))
    out = pl.pallas_call(
        body, name="sibling_fill", in_specs=[ANY] * n, out_specs=[ANY] * n,
        out_shape=[jax.ShapeDtypeStruct(b.shape, b.dtype) for b in bufs],
        input_output_aliases={a: a for a in range(n)}, scratch_shapes=[sem, sem])(*bufs)
    return [o.reshape(2 * b.shape[1], b.shape[2]) for o, b in zip(out, bufs)]


def _adamw(w, m, v, grad, name, tm):
    rows, cols = w.shape
    gcols = grad.shape[1]
    if tm == rows:
        gspec = pl.BlockSpec(grad.shape, lambda i: (0, 0))
    else:
        gspec = pl.BlockSpec((tm, gcols), lambda i: (i, 0))

    def body(w_ref, m_ref, v_ref, gin_ref, g_ref, d_ref, nm_ref, nv_ref):
        g = gin_ref[0:tm, 0:cols]
        w_, m_, v_ = w_ref[...], m_ref[...], v_ref[...]
        m_ = ADAM_B1 * m_ + (1.0 - ADAM_B1) * g
        v_ = ADAM_B2 * v_ + (1.0 - ADAM_B2) * jnp.square(g)
        m_hat = m_ / (1.0 - ADAM_B1 ** ADAM_STEP)
        v_hat = v_ / (1.0 - ADAM_B2 ** ADAM_STEP)
        g_ref[...] = g
        d_ref[...] = -ADAM_LR * (m_hat / (jnp.sqrt(v_hat) + ADAM_EPS) + ADAM_WD * w_)
        nm_ref[...] = m_
        nv_ref[...] = v_

    blk = pl.BlockSpec((tm, cols), lambda i: (i, 0))
    return _call(
        body, name=name, grid=(rows // tm,),
        in_specs=[blk, blk, blk, gspec], out_specs=[blk] * 4,
        out_shape=[jax.ShapeDtypeStruct((rows, cols), f32)] * 4)(w, m, v, grad)


def _rope_tables(s):
    half = HEAD_DIM // 2
    inv_freq = 1.0 / (10000.0 ** (jnp.arange(half, dtype=f32) / half))
    ang = jnp.arange(s, dtype=f32)[:, None] * inv_freq[None, :]
    cos, sin = jnp.cos(ang), jnp.sin(ang)
    return jnp.tile(cos, (1, LANES // half)), jnp.tile(jnp.concatenate([-sin, sin], axis=1), (1, LANES // HEAD_DIM))


def _local_step(x, target, win_g, gather, exchange, b_gate, w_pool, pool_scale, ln1_g, ln1_b, convb_g, ln2_g, ln2_b):
    s = x.shape[0]
    nb = s // MOBA_BLOCK
    cos_t, sin_t = _rope_tables(s)
    x_bf = x.astype(bf16)

    z, (wba_g, wbp_g, wout_g) = _inproj(x_bf, win_g, cos_t, sin_t, b_gate, gather["inproj"])
    wba = wba_g.transpose(1, 0, 2).reshape(D_ATTN, D_MODEL)
    wbp = wbp_g.transpose(1, 0, 2).reshape(D_POOL, D_MODEL)
    wout = wout_g.reshape(D_MODEL, D_MODEL)
    (o_bf, lse), (wg_g, wu_g, convw_g) = _attn_fwd(z, nb, gather["attn_fwd"])
    pooled_bf, mixed, pm_bf = _pool_fwd(z, w_pool, pool_scale)
    m_bf = _branch_merge(o_bf, pm_bf, z, wba, wbp)
    x1_bf, xhat1, rstd1 = _out_ln1(m_bf, wout, x, ln1_g, ln1_b)
    (a, up, hh_bf), (wd_g,) = _ffn_up(x1_bf, wg_g, wu_g, convw_g, convb_g, gather["ffn_up"])
    wd = wd_g.reshape(D_FF_PAD, D_MODEL)
    loss, dh2, dh2_bf, d_ln2_g, d_ln2_b = _ffn_down_ln2_loss(hh_bf, wd, xhat1, ln1_g, ln1_b, ln2_g, ln2_b, target)

    da_bf, dup_bf, dconv = _ffn_act_bwd(dh2_bf, wd_g, a, up, convw_g, convb_g)
    d_wd = _matmul_tn(hh_bf, dh2_bf, 1, "dw_ffn_down").reshape(N_CHIPS, FF_PAD, D_MODEL)
    d_wg = _matmul_tn(da_bf, x1_bf, 1, "dw_ffn_gate").reshape(N_CHIPS, FF_PAD, D_MODEL)
    d_wu = _matmul_tn(dup_bf, x1_bf, 1, "dw_ffn_up").reshape(N_CHIPS, FF_PAD, D_MODEL)
    (dh1, dh1_bf, d_ln1_g, d_ln1_b), got_down = _ffn_in_bwd_ln1(
        da_bf, dup_bf, wg_g, wu_g, dh2, xhat1, rstd1, ln1_g, exchange("ffn_in_bwd_ln1", [("w_ffn_down", d_wd)]))
    d_wout = _matmul_tn(m_bf, dh1_bf, 1, "dw_out").reshape(N_CHIPS, D_MODEL // N_CHIPS, D_MODEL)
    dzg_bf, dya_bf, dyp_bf, do_bf, dpm, d_bgate = _merge_bwd(dh1_bf, wout, o_bf, pm_bf, z, wba, wbp)
    d_wba = _matmul_tn(o_bf, dya_bf, N_CHIPS, "dw_branch_attn")
    d_wbp = _matmul_tn(pm_bf, dyp_bf, N_CHIPS, "dw_branch_pool")
    du_bf, d_wpool, d_pscale = _pool_bwd(dpm, mixed, pooled_bf, w_pool, pool_scale)
    small = _pack_small([d_bgate, d_wpool, d_pscale, d_ln1_g, d_ln1_b, d_ln2_g, d_ln2_b, dconv[:, 3, :],
                         loss[0:1, :]])[None]
    (dq_bf, dk_bf, dv_bf), got_ffn = _attn_bwd(
        z, o_bf, lse, do_bf, cos_t, sin_t, nb,
        exchange("attn_bwd", [("w_ffn_gate", d_wg), ("w_ffn_up", d_wu), ("conv_w", dconv), ("small", small)]))
    dz_bf = jnp.concatenate([dq_bf, dk_bf, dv_bf, du_bf, dzg_bf], axis=1)
    d_win, got_branch = _matmul_tn(
        x_bf, dz_bf, N_CHIPS, "dw_in",
        cargo=exchange("dw_in", [("w_out", d_wout), ("w_branch_attn", d_wba), ("w_branch_pool", d_wbp)]))
    (grad_x,), got_in = _inproj_dx(dz_bf, win_g, dh1, exchange("inproj_dx", [("w_in", d_win)]))
    return grad_x, got_down + got_ffn + got_branch + got_in


SMALL_ROWS = 592


def _pack_small(parts):
    flat = jnp.concatenate([p.reshape(-1) for p in parts])
    return jnp.pad(flat, (0, SMALL_ROWS * LANES - flat.shape[0])).reshape(SMALL_ROWS, LANES)


def _unpack_small(packed, shapes):
    flat = packed.reshape(-1)
    out, off = [], 0
    for shp in shapes:
        n = 1
        for d in shp:
            n *= d
        out.append(flat[off:off + n].reshape(shp))
        off += n
    return out


def _pad_conv_b(cb):
    return jnp.pad(cb.reshape(N_CHIPS, FF_SHARD), ((0, 0), (0, FF_PAD - FF_SHARD)))


def kernel(x, w_in, b_gate, w_branch_attn, w_pool, pool_scale, w_branch_pool, w_out, ln1_g, ln1_b, w_ffn_gate, w_ffn_up, conv_w, conv_b, w_ffn_down, ln2_g, ln2_b, loss_target, m_w_in, m_b_gate, m_w_branch_attn, m_w_pool, m_pool_scale, m_w_branch_pool, m_w_out, m_ln1_g, m_ln1_b, m_w_ffn_gate, m_w_ffn_up, m_conv_w, m_conv_b, m_w_ffn_down, m_ln2_g, m_ln2_b, v_w_in, v_b_gate, v_w_branch_attn, v_w_pool, v_pool_scale, v_w_branch_pool, v_w_out, v_ln1_g, v_ln1_b, v_w_ffn_gate, v_w_ffn_up, v_conv_w, v_conv_b, v_w_ffn_down, v_ln2_g, v_ln2_b):
    pad_rows = lambda w: jnp.pad(w.astype(bf16), ((0, FF_PAD - FF_SHARD), (0, 0)))
    shards = [w_in[0].astype(bf16), w_branch_attn[0].astype(bf16), w_branch_pool[0].astype(bf16),
              w_out[0].astype(bf16), pad_rows(w_ffn_gate[0].T), pad_rows(w_ffn_up[0].T), pad_rows(w_ffn_down[0]),
              jnp.pad(conv_w[0], ((0, CONV_ROWS - 3), (0, FF_PAD - FF_SHARD)))]
    (win_g,) = _cargo_alone(_GatherCargo(shards[:1], [0.0]), "gather_w_in")
    gather = {"inproj": _GatherCargo(shards[1:4], [0.3, 0.5, 0.75]),
              "attn_fwd": _GatherCargo(shards[4:6] + shards[7:], [0.45, 0.88, 0.9]),
              "ffn_up": _GatherCargo(shards[6:7], [0.8])}

    place = jnp.stack([2 * lax.axis_index("x") + lax.axis_index("y"), lax.axis_index("c")]).astype(jnp.int32)
    chip_sums = {}

    def exchange(carrier, named):
        split = [_split_halves(g) for _, g in named]
        got = _presum_swap(split, "presum_swap_" + carrier)
        sums = [_add_pair(a, b, place, "presum_" + nm) for a, b, (nm, _) in zip(split, got, named)]
        chip_sums.update({nm: s for (nm, _), s in zip(named, sums)})
        return _ExchangeCargo(sums, [nm == "small" for nm, _ in named])

    convb_g = _pad_conv_b(conv_b).reshape(N_CHIPS, 1, FF_PAD)
    grad_x, brought = _local_step(
        x[0], loss_target[0], win_g, gather, exchange, b_gate, w_pool[0], pool_scale, ln1_g, ln1_b, convb_g, ln2_g, ln2_b)
    slots = dict(zip(chip_sums, brought))
    names = ["w_in", "w_branch_attn", "w_branch_pool", "w_out", "w_ffn_gate", "w_ffn_up", "w_ffn_down", "conv_w", "small"]
    grads = _sibling_fill([_sum_slots(chip_sums[nm], slots[nm], place, nm == "small", "sum_" + nm) for nm in names])

    weights = [w_in[0], w_branch_attn[0], w_branch_pool[0], w_out[0], w_ffn_gate[0].T, w_ffn_up[0].T, w_ffn_down[0], conv_w[0]]
    m_in = [m_w_in[0], m_w_branch_attn[0], m_w_branch_pool[0], m_w_out[0], m_w_ffn_gate[0].T, m_w_ffn_up[0].T, m_w_ffn_down[0], m_conv_w[0]]
    v_in = [v_w_in[0], v_w_branch_attn[0], v_w_branch_pool[0], v_w_out[0], v_w_ffn_gate[0].T, v_w_ffn_up[0].T, v_w_ffn_down[0], v_conv_w[0]]
    tiles = [256, 256, 256, 128, 88, 88, 88, 3]
    res = {}
    for i, nm in enumerate(names[:-1]):
        outs4 = _adamw(weights[i], m_in[i], v_in[i], grads[i], "adamw_" + nm, tiles[i])
        res[nm] = [(r.T if nm in ("w_ffn_gate", "w_ffn_up") else r)[None] for r in outs4]
    loss = grads[-1][SMALL_ROWS - 4, 0]

    small_names = ["b_gate", "w_pool", "pool_scale", "ln1_g", "ln1_b", "ln2_g", "ln2_b", "conv_b"]
    small_w = [b_gate, w_pool, pool_scale, ln1_g, ln1_b, ln2_g, ln2_b, _pad_conv_b(conv_b)]
    small_m = [m_b_gate, m_w_pool, m_pool_scale, m_ln1_g, m_ln1_b, m_ln2_g, m_ln2_b, _pad_conv_b(m_conv_b)]
    small_v = [v_b_gate, v_w_pool, v_pool_scale, v_ln1_g, v_ln1_b, v_ln2_g, v_ln2_b, _pad_conv_b(v_conv_b)]
    packed = _adamw(_pack_small(small_w), _pack_small(small_m), _pack_small(small_v), grads[-1],
                    "adamw_small", SMALL_ROWS)
    shapes = [w.shape for w in small_w]
    for kind in range(4):
        for nm, val in zip(small_names, _unpack_small(packed[kind], shapes)):
            if nm == "conv_b":
                val = val[:, :FF_SHARD].reshape(1, 4 * FF_SHARD)
            res.setdefault(nm, [None] * 4)[kind] = val

    order = ["w_in", "b_gate", "w_branch_attn", "w_pool", "pool_scale", "w_branch_pool", "w_out", "ln1_g", "ln1_b",
             "w_ffn_gate", "w_ffn_up", "conv_w", "conv_b", "w_ffn_down", "ln2_g", "ln2_b"]
    outs = [loss, grad_x[None]]
    for kind in range(4):
        outs += [res[nm][kind] for nm in order]
    return tuple(outs)
```

```python
import functools

import jax
import jax.numpy as jnp
from jax import lax
from jax.experimental import pallas as pl
from jax.experimental.pallas import tpu as pltpu

D_MODEL = 1024
HEAD_DIM = 64
D_ATTN = 512
D_POOL = 512
MOBA_BLOCK = 256
MOBA_TOPK = 3
POOL_GROUP = 128
MAX_WINDOW = 16
FF_SHARD = 704
FF_PAD = 768
D_FF_PAD = 4 * FF_PAD
N_CHIPS = 4
LANES = 128
ALPHA = (2.0 * 1) ** 0.25
LN_EPS = 1e-5
NEG = -1e30
SCALE = HEAD_DIM ** -0.5
ADAM_LR, ADAM_B1, ADAM_B2, ADAM_EPS, ADAM_WD, ADAM_STEP = 0.001, 0.9, 0.999, 1e-08, 0.01, 10
VMEM_LIMIT = 56 * 1024 * 1024
MESH = pl.DeviceIdType.MESH

bf16 = jnp.bfloat16
f32 = jnp.float32


def _dot(a, b):
    return jnp.dot(a, b, preferred_element_type=f32)


def _dot_nt(a, b):
    return lax.dot_general(a, b, (((1,), (1,)), ((), ())), preferred_element_type=f32)


def _dot_tn(a, b):
    return lax.dot_general(a, b, (((0,), (0,)), ((), ())), preferred_element_type=f32)


def _call(body, *, name, grid, in_specs, out_specs, out_shape, scratch=(), cargo=None):
    if cargo is not None:
        return _cargo_call(body, cargo, name=name, grid=grid, in_specs=in_specs, out_specs=out_specs,
                           out_shape=out_shape, scratch=scratch)
    return pl.pallas_call(
        body, name=name, grid=grid, in_specs=in_specs, out_specs=out_specs, out_shape=out_shape,
        scratch_shapes=list(scratch),
        compiler_params=pltpu.CompilerParams(dimension_semantics=("arbitrary",) * len(grid),
                                             vmem_limit_bytes=VMEM_LIMIT))


def _rot_half(t):
    lane = lax.broadcasted_iota(jnp.int32, t.shape, 1)
    first = (lane % HEAD_DIM) < (HEAD_DIM // 2)
    return jnp.where(first, pltpu.roll(t, LANES - HEAD_DIM // 2, 1), pltpu.roll(t, HEAD_DIM // 2, 1))


def _rope(t, cos, sin_signed):
    return t * cos + _rot_half(t) * sin_signed


def _rope_bwd(d, cos, sin_signed):
    return d * cos + _rot_half(d * sin_signed)


def _gelu_parts(a):
    cdf = 0.5 * (1.0 + lax.erf(a * (2.0 ** -0.5)))
    pdf = jnp.exp(-0.5 * a * a) * ((2.0 * jnp.pi) ** -0.5)
    return a * cdf, cdf + a * pdf


def _layer_norm(h, g, b):
    mu = jnp.mean(h, axis=-1, keepdims=True)
    xc = h - mu
    var = jnp.mean(xc * xc, axis=-1, keepdims=True)
    rstd = lax.rsqrt(var + LN_EPS)
    xhat = xc * rstd
    return xhat * g + b, xhat, rstd


def _layer_norm_bwd(dy, xhat, rstd, g):
    dxh = dy * g
    m1 = jnp.mean(dxh, axis=-1, keepdims=True)
    m2 = jnp.mean(dxh * xhat, axis=-1, keepdims=True)
    return rstd * (dxh - m1 - xhat * m2)


def _inproj(x_bf, win_g, cos_t, sin_t, b_gate, cargo):
    s = x_bf.shape[0]
    tm, tn = 1024, 512

    def body(x_ref, w_ref, cos_ref, sin_ref, b_ref, o_ref):
        j = pl.program_id(1)
        acc = _dot(x_ref[...], w_ref[...])

        @pl.when(j < 2)
        def _():
            for c in range(tn // LANES):
                sl = slice(c * LANES, (c + 1) * LANES)
                o_ref[:, sl] = _rope(acc[:, sl], cos_ref[...], sin_ref[...])

        @pl.when((j >= 2) & (j < 4))
        def _():
            o_ref[...] = acc

        @pl.when(j >= 4)
        def _():
            o_ref[...] = jax.nn.sigmoid(acc + b_ref[...])

    (z,), carried = _cargo_call(
        body, cargo, name="inproj", grid=(s // tm, 4 * D_MODEL // tn),
        in_specs=[pl.BlockSpec((tm, D_MODEL), lambda i, j: (i, 0)),
                  pl.BlockSpec((None, D_MODEL, tn), lambda i, j: (j // 2, 0, j % 2)),
                  pl.BlockSpec((tm, LANES), lambda i, j: (i, 0)),
                  pl.BlockSpec((tm, LANES), lambda i, j: (i, 0)),
                  pl.BlockSpec((1, tn), lambda i, j: (0, jnp.maximum(j - 4, 0)))],
        out_specs=[pl.BlockSpec((tm, tn), lambda i, j: (i, j))],
        out_shape=[jax.ShapeDtypeStruct((s, 4 * D_MODEL), f32)])(x_bf, win_g, cos_t, sin_t, b_gate)
    return z, carried


STAT_ROWS = 8


def _pair_rows(v0, v1, fill):
    r = lax.broadcasted_iota(jnp.int32, (STAT_ROWS, v0.shape[1]), 0)
    return jnp.where(r == 0, v0, jnp.where(r == 1, v1, fill))


def _head_lanes(shape):
    lane = lax.broadcasted_iota(jnp.int32, shape, 1)
    return lane < HEAD_DIM, lane >= HEAD_DIM


def _head_rows(shape):
    row = lax.broadcasted_iota(jnp.int32, shape, 0)
    return row < HEAD_DIM, row >= HEAD_DIM


def _moba_select(q_bf, k_all, i, nb):
    k_mean = jnp.mean(k_all.reshape(nb, MOBA_BLOCK, LANES), axis=1)
    n_io = lax.broadcasted_iota(jnp.int32, (nb, MOBA_BLOCK), 0)
    past = n_io < i
    sels = []
    for head in _head_lanes((nb, LANES)):
        gate = _dot_nt(jnp.where(head, k_mean, 0.0).astype(bf16), q_bf)
        g = jnp.where(past, gate, NEG)
        rank = jnp.zeros((nb, MOBA_BLOCK), f32)
        for m in range(nb):
            gm = g[m:m + 1, :]
            rank = rank + jnp.where((gm > g) | ((gm == g) & (m < n_io)), 1.0, 0.0)
        sels.append(jnp.where(past & (rank < MOBA_TOPK), 1.0, 0.0))
    return sels


def _attn_fwd(z, nb, cargo):
    s = z.shape[0]

    def body(q_ref, k_ref, v_ref, o_ref, lse_ref, vt_s):
        i = pl.program_id(1)

        @pl.when(i == 0)
        def _():
            for j in range(nb):
                vt_s[j] = v_ref[j * MOBA_BLOCK:(j + 1) * MOBA_BLOCK, :].T.astype(bf16)

        k_heads = _head_lanes((MOBA_BLOCK, LANES))
        o_heads = _head_rows((LANES, MOBA_BLOCK))

        def block(j, qs, allow, m_old, l_old):
            kj, vtj = k_ref[j * MOBA_BLOCK:(j + 1) * MOBA_BLOCK, :], vt_s[j]
            m_new, alpha, l_new, pv = [], [], [], None
            for h in range(2):
                sc = jnp.where(allow[h], _dot_nt(jnp.where(k_heads[h], kj, 0.0).astype(bf16), qs), NEG)
                mx = jnp.max(sc, axis=0, keepdims=True)
                mn = mx if m_old is None else jnp.maximum(m_old[h], mx)
                p = jnp.exp(sc - mn)
                lsum = jnp.sum(p, axis=0, keepdims=True)
                if m_old is None:
                    a = None
                else:
                    a = jnp.exp(m_old[h] - mn)
                    lsum = a * l_old[h] + lsum
                t = _dot(jnp.where(o_heads[h], vtj, jnp.zeros_like(vtj)), p.astype(bf16))
                pv = t if pv is None else pv + t
                m_new.append(mn)
                alpha.append(a)
                l_new.append(lsum)
            return m_new, alpha, l_new, pv

        def query_block(c):
            q = q_ref[...]
            sels = _moba_select(q.astype(bf16), k_ref[...], c, nb)
            qs = (q * SCALE).astype(bf16)
            key = lax.broadcasted_iota(jnp.int32, (MOBA_BLOCK, MOBA_BLOCK), 0)
            qry = lax.broadcasted_iota(jnp.int32, (MOBA_BLOCK, MOBA_BLOCK), 1)
            m, _, l, acc = block(c, qs, [key <= qry] * 2, None, None)
            for j in range(c):
                m, alpha, l, pv = block(j, qs, [sels[h][j:j + 1, :] > 0.0 for h in range(2)], m, l)
                acc = acc * jnp.where(o_heads[0], alpha[0], alpha[1]) + pv
            o_ref[...] = (acc / jnp.where(o_heads[0], l[0], l[1])).T.astype(bf16)
            lse_ref[...] = _pair_rows(m[0] + jnp.log(l[0]), m[1] + jnp.log(l[1]), 0.0)

        for c in range(nb):
            pl.when(i == c)(functools.partial(query_block, c))

    return _cargo_call(
        body, cargo, name="attn_fwd", grid=(D_ATTN // LANES, nb),
        in_specs=[pl.BlockSpec((MOBA_BLOCK, LANES), lambda hp, i: (i, hp)),
                  pl.BlockSpec((s, LANES), lambda hp, i: (0, 4 + hp)),
                  pl.BlockSpec((s, LANES), lambda hp, i: (0, 8 + hp))],
        out_specs=[pl.BlockSpec((MOBA_BLOCK, LANES), lambda hp, i: (i, hp)),
                   pl.BlockSpec((None, STAT_ROWS, MOBA_BLOCK), lambda hp, i: (hp, 0, i))],
        out_shape=[jax.ShapeDtypeStruct((s, D_ATTN), bf16),
                   jax.ShapeDtypeStruct((D_ATTN // LANES, STAT_ROWS, s), f32)],
        scratch=[pltpu.VMEM((nb, LANES, MOBA_BLOCK), bf16)])(z, z, z)


def _window_select(snaps, g):
    return jnp.where(g == 0, snaps[0], jnp.where(g == 1, snaps[1], jnp.where(g == 2, snaps[2], snaps[3])))


def _pool_fwd(z, w_pool, pool_scale):
    s = z.shape[0]

    def body(u_ref, w_ref, sc_ref, pooled_ref, mixed_ref, pm_ref, pad):
        g = pl.program_id(0)
        u = u_ref[...]
        pad[0:MAX_WINDOW, :] = jnp.zeros((MAX_WINDOW, POOL_GROUP), f32)
        pad[MAX_WINDOW:MAX_WINDOW + s, :] = u
        acc = u
        snaps = []
        for d in range(1, MAX_WINDOW):
            acc = acc + pad[MAX_WINDOW - d:MAX_WINDOW - d + s, :]
            if d + 1 in (2, 4, 8, 16):
                snaps.append(acc)
        win = _window_select(snaps, g)
        t = lax.broadcasted_iota(jnp.int32, (s, POOL_GROUP), 0)
        count = jnp.minimum(t + 1, jnp.left_shift(2, g)).astype(f32)
        pooled = (win / count - u).astype(bf16)
        mixed = _dot(pooled, w_ref[...].astype(bf16))
        pooled_ref[...] = pooled
        mixed_ref[...] = mixed
        pm_ref[...] = (mixed * sc_ref[...]).astype(bf16)

    blk = pl.BlockSpec((s, POOL_GROUP), lambda g: (0, g))
    return _call(
        body, name="pool_fwd", grid=(4,),
        in_specs=[pl.BlockSpec((s, POOL_GROUP), lambda g: (0, 12 + g)),
                  pl.BlockSpec((None, POOL_GROUP, POOL_GROUP), lambda g: (g, 0, 0)),
                  pl.BlockSpec((1, POOL_GROUP), lambda g: (0, g))],
        out_specs=[blk, blk, blk],
        out_shape=[jax.ShapeDtypeStruct((s, D_POOL), bf16), jax.ShapeDtypeStruct((s, D_POOL), f32),
                   jax.ShapeDtypeStruct((s, D_POOL), bf16)],
        scratch=[pltpu.VMEM((s + MAX_WINDOW, POOL_GROUP), f32)])(z, w_pool, pool_scale)


def _branch_merge(o_bf, pm_bf, z, wba, wbp):
    s = o_bf.shape[0]
    tm = 256

    def body(o_ref, pm_ref, ga_ref, gp_ref, wba_ref, wbp_ref, m_ref):
        ya = _dot(o_ref[...], wba_ref[...])
        yp = _dot(pm_ref[...], wbp_ref[...])
        m_ref[...] = (ga_ref[...] * ya + gp_ref[...] * yp).astype(bf16)

    full = lambda r, c: pl.BlockSpec((r, c), lambda i: (0, 0))
    return _call(
        body, name="branch_merge", grid=(s // tm,),
        in_specs=[pl.BlockSpec((tm, D_ATTN), lambda i: (i, 0)), pl.BlockSpec((tm, D_POOL), lambda i: (i, 0)),
                  pl.BlockSpec((tm, D_MODEL), lambda i: (i, 2)), pl.BlockSpec((tm, D_MODEL), lambda i: (i, 3)),
                  full(D_ATTN, D_MODEL), full(D_POOL, D_MODEL)],
        out_specs=pl.BlockSpec((tm, D_MODEL), lambda i: (i, 0)),
        out_shape=jax.ShapeDtypeStruct((s, D_MODEL), bf16))(o_bf, pm_bf, z, z, wba, wbp)


def _out_ln1(m_bf, wout, x, ln_g, ln_b):
    s = x.shape[0]
    tm = 256

    def body(m_ref, w_ref, x_ref, g_ref, b_ref, x1_ref, xhat_ref, rstd_ref):
        h = ALPHA * x_ref[...] + _dot(m_ref[...], w_ref[...])
        y, xhat, rstd = _layer_norm(h, g_ref[...], b_ref[...])
        x1_ref[...] = y.astype(bf16)
        xhat_ref[...] = xhat
        rstd_ref[...] = jnp.broadcast_to(rstd, (tm, LANES))

    row = pl.BlockSpec((tm, D_MODEL), lambda i: (i, 0))
    vec = pl.BlockSpec((1, D_MODEL), lambda i: (0, 0))
    return _call(
        body, name="out_ln1", grid=(s // tm,),
        in_specs=[row, pl.BlockSpec((D_MODEL, D_MODEL), lambda i: (0, 0)), row, vec, vec],
        out_specs=[row, row, pl.BlockSpec((tm, LANES), lambda i: (i, 0))],
        out_shape=[jax.ShapeDtypeStruct((s, D_MODEL), bf16), jax.ShapeDtypeStruct((s, D_MODEL), f32),
                   jax.ShapeDtypeStruct((s, LANES), f32)])(m_bf, wout, x, ln_g, ln_b)


FF_TILE = 256
FF_TILES_PER_SHARD = FF_PAD // FF_TILE
CONV_PAD = 8
CONV_ROWS = 16


def _ff_weight_spec(rows):
    return pl.BlockSpec((None, rows, FF_TILE), lambda j: (j // FF_TILES_PER_SHARD, 0, j % FF_TILES_PER_SHARD))


def _ff_rows_spec():
    return pl.BlockSpec((None, FF_TILE, D_MODEL), lambda j: (j // FF_TILES_PER_SHARD, j % FF_TILES_PER_SHARD, 0))


def _row_shifts(x, shifts):
    s = x.shape[0]
    padded = jnp.concatenate([x, jnp.zeros((CONV_PAD, x.shape[1]), x.dtype)], axis=0)
    return [pltpu.roll(padded, d % (s + CONV_PAD), 0)[0:s] for d in shifts]


def _ffn_up(x1_bf, wg_g, wu_g, convw_g, convb_g, cargo):
    s = x1_bf.shape[0]

    def body(x_ref, wg_ref, wu_ref, cw_ref, cb_ref, a_ref, up_ref, ac_ref, hh_ref):
        x1 = x_ref[...]
        a = _dot_nt(x1, wg_ref[...])
        up = _dot_nt(x1, wu_ref[...])
        a_ref[...] = a
        up_ref[...] = up
        a1, a2 = _row_shifts(a, (1, 2))
        ac = a * cw_ref[2:3, :] + a1 * cw_ref[1:2, :] + a2 * cw_ref[0:1, :] + cb_ref[...]
        ac_ref[...] = ac
        hg, _ = _gelu_parts(ac)
        hh_ref[...] = (hg * up).astype(bf16)

    col = pl.BlockSpec((s, FF_TILE), lambda j: (0, j))
    wide = jax.ShapeDtypeStruct((s, D_FF_PAD), f32)
    return _call(
        body, name="ffn_up", grid=(D_FF_PAD // FF_TILE,),
        in_specs=[pl.BlockSpec((s, D_MODEL), lambda j: (0, 0)), _ff_rows_spec(), _ff_rows_spec(),
                  _ff_weight_spec(8), _ff_weight_spec(1)],
        out_specs=[col, col, col, col],
        out_shape=[wide, wide, wide, jax.ShapeDtypeStruct((s, D_FF_PAD), bf16)],
        cargo=cargo)(x1_bf, wg_g, wu_g, convw_g, convb_g)


def _ffn_down_ln2_loss(hh_bf, wd, xhat1, ln1_g, ln1_b, ln2_g, ln2_b, target):
    s = hh_bf.shape[0]
    tm = 256

    def body(hh_ref, w_ref, xh1_ref, g1_ref, b1_ref, g2_ref, b2_ref, t_ref, loss_ref, dh_ref, dhb_ref, dg_ref, db_ref):
        i = pl.program_id(0)
        x1 = xh1_ref[...] * g1_ref[...] + b1_ref[...]
        h = ALPHA * x1 + _dot(hh_ref[...], w_ref[...])
        y, xhat, rstd = _layer_norm(h, g2_ref[...], b2_ref[...])
        err = y - t_ref[...]
        part = 0.5 * jnp.sum(jnp.mean(err * err, axis=-1, keepdims=True), axis=0, keepdims=True)
        dy = err * (1.0 / D_MODEL)

        @pl.when(i == 0)
        def _():
            loss_ref[...] = jnp.zeros_like(loss_ref)
            dg_ref[...] = jnp.zeros_like(dg_ref)
            db_ref[...] = jnp.zeros_like(db_ref)

        loss_ref[...] += jnp.broadcast_to(part, loss_ref.shape)
        dg_ref[...] += jnp.sum(dy * xhat, axis=0, keepdims=True)
        db_ref[...] += jnp.sum(dy, axis=0, keepdims=True)
        dh = _layer_norm_bwd(dy, xhat, rstd, g2_ref[...])
        dh_ref[...] = dh
        dhb_ref[...] = dh.astype(bf16)

    row = pl.BlockSpec((tm, D_MODEL), lambda i: (i, 0))
    vec = pl.BlockSpec((1, D_MODEL), lambda i: (0, 0))
    return _call(
        body, name="ffn_down_ln2_loss", grid=(s // tm,),
        in_specs=[pl.BlockSpec((tm, D_FF_PAD), lambda i: (i, 0)), pl.BlockSpec((D_FF_PAD, D_MODEL), lambda i: (0, 0)),
                  row, vec, vec, vec, vec, row],
        out_specs=[pl.BlockSpec((8, LANES), lambda i: (0, 0)), row, row, vec, vec],
        out_shape=[jax.ShapeDtypeStruct((8, LANES), f32), jax.ShapeDtypeStruct((s, D_MODEL), f32),
                   jax.ShapeDtypeStruct((s, D_MODEL), bf16), jax.ShapeDtypeStruct((1, D_MODEL), f32),
                   jax.ShapeDtypeStruct((1, D_MODEL), f32)])(hh_bf, wd, xhat1, ln1_g, ln1_b, ln2_g, ln2_b, target)


def _ffn_act_bwd(dh2_bf, wd_g, ac, a, up, convw_g):
    s = dh2_bf.shape[0]

    def body(dh_ref, wd_ref, ac_ref, a_ref, up_ref, cw_ref, da_ref, dup_ref, dc_ref):
        dhh = _dot_nt(dh_ref[...], wd_ref[...])
        hg, dgelu = _gelu_parts(ac_ref[...])
        dup_ref[...] = (dhh * hg).astype(bf16)
        dac = dhh * up_ref[...] * dgelu
        dac1, dac2 = _row_shifts(dac, (-1, -2))
        a = a_ref[...]
        rows = [jnp.sum(d * a, axis=0, keepdims=True) for d in (dac2, dac1, dac)]
        rows.append(jnp.sum(dac, axis=0, keepdims=True))
        rows.append(jnp.zeros((CONV_ROWS - 4, FF_TILE), f32))
        dc_ref[...] = jnp.concatenate(rows, axis=0)
        da_ref[...] = (dac * cw_ref[2:3, :] + dac1 * cw_ref[1:2, :] + dac2 * cw_ref[0:1, :]).astype(bf16)

    col = pl.BlockSpec((s, FF_TILE), lambda j: (0, j))
    return _call(
        body, name="ffn_act_bwd", grid=(D_FF_PAD // FF_TILE,),
        in_specs=[pl.BlockSpec((s, D_MODEL), lambda j: (0, 0)), _ff_rows_spec(), col, col, col, _ff_weight_spec(8)],
        out_specs=[col, col, _ff_weight_spec(CONV_ROWS)],
        out_shape=[jax.ShapeDtypeStruct((s, D_FF_PAD), bf16), jax.ShapeDtypeStruct((s, D_FF_PAD), bf16),
                   jax.ShapeDtypeStruct((N_CHIPS, CONV_ROWS, FF_PAD), f32)],
    )(dh2_bf, wd_g, ac, a, up, convw_g)


def _matmul_tn(a, b, n_shards, name, tm=512, tn=1024, cargo=None):
    k, m = a.shape
    n = b.shape[1]
    tm, tn = min(tm, m), min(tn, n // n_shards)
    per = n // n_shards // tn

    def body(a_ref, b_ref, o_ref, at_s):
        @pl.when(pl.program_id(1) == 0)
        def _():
            at_s[...] = a_ref[...].T

        o_ref[...] = _dot(at_s[...], b_ref[...]).astype(bf16)

    res = _call(
        body, name=name, grid=(m // tm, n // tn),
        in_specs=[pl.BlockSpec((k, tm), lambda i, j: (0, i)), pl.BlockSpec((k, tn), lambda i, j: (0, j))],
        out_specs=[pl.BlockSpec((None, tm, tn), lambda i, j: (j // per, i, j % per))],
        out_shape=[jax.ShapeDtypeStruct((n_shards, m, n // n_shards), bf16)], scratch=[pltpu.VMEM((tm, k), bf16)],
        cargo=cargo)(a, b)
    return res[0] if cargo is None else (res[0][0], res[1])


def _ffn_in_bwd_ln1(da_bf, dup_bf, wg_g, wu_g, dh2, xhat1, rstd1, ln1_g, cargo):
    s = da_bf.shape[0]
    tm = 256

    def body(da_ref, dup_ref, wg_ref, wu_ref, dh2_ref, xh_ref, rstd_ref, g_ref, dh_ref, dhb_ref, dg_ref, db_ref):
        i = pl.program_id(0)
        dx1 = ALPHA * dh2_ref[...]
        for sh in range(N_CHIPS):
            sl = slice(sh * FF_PAD, (sh + 1) * FF_PAD)
            dx1 = dx1 + _dot(da_ref[:, sl], wg_ref[sh]) + _dot(dup_ref[:, sl], wu_ref[sh])
        xhat = xh_ref[...]

        @pl.when(i == 0)
        def _():
            dg_ref[...] = jnp.zeros_like(dg_ref)
            db_ref[...] = jnp.zeros_like(db_ref)

        dg_ref[...] += jnp.sum(dx1 * xhat, axis=0, keepdims=True)
        db_ref[...] += jnp.sum(dx1, axis=0, keepdims=True)
        dh = _layer_norm_bwd(dx1, xhat, rstd_ref[:, 0:1], g_ref[...])
        dh_ref[...] = dh
        dhb_ref[...] = dh.astype(bf16)

    row = pl.BlockSpec((tm, D_MODEL), lambda i: (i, 0))
    wide = pl.BlockSpec((tm, D_FF_PAD), lambda i: (i, 0))
    wfull = pl.BlockSpec((N_CHIPS, FF_PAD, D_MODEL), lambda i: (0, 0, 0))
    vec = pl.BlockSpec((1, D_MODEL), lambda i: (0, 0))
    return _call(
        body, name="ffn_in_bwd_ln1", grid=(s // tm,),
        in_specs=[wide, wide, wfull, wfull, row, row, pl.BlockSpec((tm, LANES), lambda i: (i, 0)), vec],
        out_specs=[row, row, vec, vec],
        out_shape=[jax.ShapeDtypeStruct((s, D_MODEL), f32), jax.ShapeDtypeStruct((s, D_MODEL), bf16),
                   jax.ShapeDtypeStruct((1, D_MODEL), f32), jax.ShapeDtypeStruct((1, D_MODEL), f32)],
        cargo=cargo)(da_bf, dup_bf, wg_g, wu_g, dh2, xhat1, rstd1, ln1_g)


def _merge_bwd(dh1_bf, wout, o_bf, pm_bf, z, wba, wbp):
    s = dh1_bf.shape[0]
    tm = 256

    def body(dh_ref, wout_ref, o_ref, pm_ref, ga_ref, gp_ref, wba_ref, wbp_ref,
             dzg_ref, dya_ref, dyp_ref, do_ref, dpm_ref, dbg_ref):
        i = pl.program_id(0)
        dm = _dot_nt(dh_ref[...], wout_ref[...])
        ya = _dot(o_ref[...], wba_ref[...])
        yp = _dot(pm_ref[...], wbp_ref[...])
        ga, gp = ga_ref[...], gp_ref[...]
        dza = dm * ya * ga * (1.0 - ga)
        dzp = dm * yp * gp * (1.0 - gp)

        @pl.when(i == 0)
        def _():
            dbg_ref[...] = jnp.zeros_like(dbg_ref)

        dbg_ref[:, 0:D_MODEL] += jnp.sum(dza, axis=0, keepdims=True)
        dbg_ref[:, D_MODEL:2 * D_MODEL] += jnp.sum(dzp, axis=0, keepdims=True)
        dzg_ref[:, 0:D_MODEL] = dza.astype(bf16)
        dzg_ref[:, D_MODEL:2 * D_MODEL] = dzp.astype(bf16)
        dya = (dm * ga).astype(bf16)
        dyp = (dm * gp).astype(bf16)
        dya_ref[...] = dya
        dyp_ref[...] = dyp
        do_ref[...] = _dot_nt(dya, wba_ref[...]).astype(bf16)
        dpm_ref[...] = _dot_nt(dyp, wbp_ref[...])

    row = pl.BlockSpec((tm, D_MODEL), lambda i: (i, 0))
    half = pl.BlockSpec((tm, D_ATTN), lambda i: (i, 0))
    full = lambda r, c: pl.BlockSpec((r, c), lambda i: (0, 0))
    return _call(
        body, name="merge_bwd", grid=(s // tm,),
        in_specs=[row, full(D_MODEL, D_MODEL), half, half,
                  pl.BlockSpec((tm, D_MODEL), lambda i: (i, 2)), pl.BlockSpec((tm, D_MODEL), lambda i: (i, 3)),
                  full(D_ATTN, D_MODEL), full(D_POOL, D_MODEL)],
        out_specs=[pl.BlockSpec((tm, 2 * D_MODEL), lambda i: (i, 0)), row, row, half, half,
                   pl.BlockSpec((1, 2 * D_MODEL), lambda i: (0, 0))],
        out_shape=[jax.ShapeDtypeStruct((s, 2 * D_MODEL), bf16), jax.ShapeDtypeStruct((s, D_MODEL), bf16),
                   jax.ShapeDtypeStruct((s, D_MODEL), bf16), jax.ShapeDtypeStruct((s, D_ATTN), bf16),
                   jax.ShapeDtypeStruct((s, D_POOL), f32), jax.ShapeDtypeStruct((1, 2 * D_MODEL), f32)],
    )(dh1_bf, wout, o_bf, pm_bf, z, z, wba, wbp)


def _pool_bwd(dpm, mixed, pooled_bf, w_pool, pool_scale):
    s = dpm.shape[0]

    def body(dpm_ref, mixed_ref, pooled_ref, w_ref, sc_ref, du_ref, dw_ref, dsc_ref, pad):
        g = pl.program_id(0)
        dpm_v = dpm_ref[...]
        dsc_ref[...] = jnp.sum(dpm_v * mixed_ref[...], axis=0, keepdims=True)
        dmixed = (dpm_v * sc_ref[...]).astype(bf16)
        dw_ref[...] = _dot_tn(pooled_ref[...], dmixed)
        dpooled = _dot_nt(dmixed, w_ref[...].astype(bf16))
        t = lax.broadcasted_iota(jnp.int32, (s, POOL_GROUP), 0)
        count = jnp.minimum(t + 1, jnp.left_shift(2, g)).astype(f32)
        r = dpooled / count
        pad[0:s, :] = r
        pad[s:s + MAX_WINDOW, :] = jnp.zeros((MAX_WINDOW, POOL_GROUP), f32)
        acc = r
        snaps = []
        for d in range(1, MAX_WINDOW):
            acc = acc + pad[d:d + s, :]
            if d + 1 in (2, 4, 8, 16):
                snaps.append(acc)
        du_ref[...] = (_window_select(snaps, g) - dpooled).astype(bf16)

    blk = pl.BlockSpec((s, POOL_GROUP), lambda g: (0, g))
    return _call(
        body, name="pool_bwd", grid=(4,),
        in_specs=[blk, blk, blk, pl.BlockSpec((None, POOL_GROUP, POOL_GROUP), lambda g: (g, 0, 0)),
                  pl.BlockSpec((1, POOL_GROUP), lambda g: (0, g))],
        out_specs=[blk, pl.BlockSpec((None, POOL_GROUP, POOL_GROUP), lambda g: (g, 0, 0)),
                   pl.BlockSpec((1, POOL_GROUP), lambda g: (0, g))],
        out_shape=[jax.ShapeDtypeStruct((s, D_POOL), bf16), jax.ShapeDtypeStruct((4, POOL_GROUP, POOL_GROUP), f32),
                   jax.ShapeDtypeStruct((1, D_POOL), f32)],
        scratch=[pltpu.VMEM((s + MAX_WINDOW, POOL_GROUP), f32)])(dpm, mixed, pooled_bf, w_pool, pool_scale)


def _attn_bwd(z, o_bf, lse, do_bf, cos_t, sin_t, nb, cargo):
    s = z.shape[0]

    def body(q_ref, k_ref, v_ref, o_ref, lse_ref, do_ref, cq_ref, sq_ref, cf_ref, sf_ref,
             dq_ref, dk_ref, dv_ref, kt_s, dk_s, dv_s):
        i = pl.program_id(1)

        @pl.when(i == 0)
        def _():
            dk_s[...] = jnp.zeros_like(dk_s)
            dv_s[...] = jnp.zeros_like(dv_s)
            for j in range(nb):
                kt_s[j] = k_ref[j * MOBA_BLOCK:(j + 1) * MOBA_BLOCK, :].T.astype(bf16)

        k_heads = _head_lanes((MOBA_BLOCK, LANES))
        t_heads = _head_rows((LANES, MOBA_BLOCK))

        def query_block(c):
            q = q_ref[...]
            sels = _moba_select(q.astype(bf16), k_ref[...], c, nb)
            qs = (q * SCALE).astype(bf16)
            qraw = [jnp.where(hm, q, 0.0).astype(bf16) for hm in k_heads]
            do = do_ref[...].astype(f32)
            do_bf = do_ref[...]
            dob = [jnp.where(hm, do, 0.0).astype(bf16) for hm in k_heads]
            pick = _head_lanes((STAT_ROWS, LANES))
            r8 = lax.broadcasted_iota(jnp.int32, (STAT_ROWS, LANES), 0)
            head_pick = jnp.where(((r8 == 0) & pick[0]) | ((r8 == 1) & pick[1]), 1.0, 0.0)
            delta8 = lax.dot_general(head_pick, do * o_ref[...].astype(f32), (((1,), (1,)), ((), ())),
                                     precision=lax.Precision.HIGHEST, preferred_element_type=f32)
            lse8 = lse_ref[...]
            delta, lse_h = [delta8[0:1, :], delta8[1:2, :]], [lse8[0:1, :], lse8[1:2, :]]
            key = lax.broadcasted_iota(jnp.int32, (MOBA_BLOCK, MOBA_BLOCK), 0)
            qry = lax.broadcasted_iota(jnp.int32, (MOBA_BLOCK, MOBA_BLOCK), 1)

            def block(j, allow):
                rows = slice(j * MOBA_BLOCK, (j + 1) * MOBA_BLOCK)
                kj, vj, ktj = k_ref[rows, :], v_ref[rows, :], kt_s[j]
                dqt, dk, dv = None, None, None
                for h in range(2):
                    sc = _dot_nt(jnp.where(k_heads[h], kj, 0.0).astype(bf16), qs)
                    p = jnp.where(allow[h], jnp.exp(sc - lse_h[h]), 0.0)
                    dp = _dot_nt(jnp.where(k_heads[h], vj, 0.0).astype(bf16), do_bf)
                    ds = (p * (dp - delta[h]) * SCALE).astype(bf16)
                    tv = _dot(p.astype(bf16), dob[h])
                    tk = _dot(ds, qraw[h])
                    tq = _dot(jnp.where(t_heads[h], ktj, jnp.zeros_like(ktj)), ds)
                    dqt, dk, dv = (tq, tk, tv) if dqt is None else (dqt + tq, dk + tk, dv + tv)
                dk_s[rows, :] += dk
                dv_s[rows, :] += dv
                return dqt

            dqt = block(c, [key <= qry] * 2)
            for j in range(c):
                dqt = dqt + block(j, [sels[h][j:j + 1, :] > 0.0 for h in range(2)])
            dq_ref[...] = _rope_bwd(dqt.T, cq_ref[...], sq_ref[...]).astype(bf16)

        for c in range(nb):
            pl.when(i == c)(functools.partial(query_block, c))

        @pl.when(i == nb - 1)
        def _():
            dk_ref[...] = _rope_bwd(dk_s[...], cf_ref[...], sf_ref[...]).astype(bf16)
            dv_ref[...] = dv_s[...].astype(bf16)

    qblk = pl.BlockSpec((MOBA_BLOCK, LANES), lambda hp, i: (i, hp))
    tq = pl.BlockSpec((MOBA_BLOCK, LANES), lambda hp, i: (i, 0))
    tf = pl.BlockSpec((s, LANES), lambda hp, i: (0, 0))
    colblk = pl.BlockSpec((s, LANES), lambda hp, i: (0, hp))
    return _cargo_call(
        body, cargo, name="attn_bwd", grid=(D_ATTN // LANES, nb),
        in_specs=[qblk, pl.BlockSpec((s, LANES), lambda hp, i: (0, 4 + hp)),
                  pl.BlockSpec((s, LANES), lambda hp, i: (0, 8 + hp)), qblk,
                  pl.BlockSpec((None, STAT_ROWS, MOBA_BLOCK), lambda hp, i: (hp, 0, i)), qblk, tq, tq, tf, tf],
        out_specs=[qblk, colblk, colblk],
        out_shape=[jax.ShapeDtypeStruct((s, D_ATTN), bf16)] * 3,
        scratch=[pltpu.VMEM((nb, LANES, MOBA_BLOCK), bf16), pltpu.VMEM((s, LANES), f32), pltpu.VMEM((s, LANES), f32)],
    )(z, z, z, o_bf, lse, do_bf, cos_t, sin_t, cos_t, sin_t)


def _inproj_dx(dz_bf, win_g, dh1, cargo):
    s = dz_bf.shape[0]
    tm = 256

    def body(dz_ref, w_ref, dh_ref, gx_ref):
        acc = ALPHA * dh_ref[...]
        for sh in range(N_CHIPS):
            acc = acc + _dot_nt(dz_ref[:, sh * D_MODEL:(sh + 1) * D_MODEL], w_ref[sh])
        gx_ref[...] = acc

    row = pl.BlockSpec((tm, D_MODEL), lambda i: (i, 0))
    return _call(
        body, name="inproj_dx", grid=(s // tm,),
        in_specs=[pl.BlockSpec((tm, 4 * D_MODEL), lambda i: (i, 0)),
                  pl.BlockSpec((N_CHIPS, D_MODEL, D_MODEL), lambda i: (0, 0, 0)), row],
        out_specs=[row], out_shape=[jax.ShapeDtypeStruct((s, D_MODEL), f32)], cargo=cargo)(dz_bf, win_g, dh1)


ANY = pl.BlockSpec(memory_space=pl.ANY)


def _chip_index():
    return 2 * lax.axis_index("x") + lax.axis_index("y")


def _peer(k):
    x, y, c = lax.axis_index("x"), lax.axis_index("y"), lax.axis_index("c")
    return (x ^ (k >> 1), y ^ (k & 1), c)


def _sibling():
    return (lax.axis_index("x"), lax.axis_index("y"), 1 - lax.axis_index("c"))


class _GatherCargo:
    def __init__(self, shards, pass_on):
        self.shards, self.pass_on, n = shards, pass_on, len(shards)
        self.inputs = [a.reshape(2, a.shape[0] // 2, a.shape[1]) for a in shards]
        self.out_shape = [jax.ShapeDtypeStruct((N_CHIPS,) + a.shape, a.dtype) for a in self.inputs]
        sem, one = pltpu.SemaphoreType.DMA((N_CHIPS - 1, n)), pltpu.SemaphoreType.DMA((n,))
        self.sems = [sem, sem, sem, sem, one, one]

    def _copies(self, src, dst, sems, a):
        send_sems, recv_sems, fsend_sems, frecv_sems, local_sems, own_sems = sems
        p, c = _chip_index(), lax.axis_index("c")
        own = pltpu.make_async_remote_copy(src_ref=src[a], dst_ref=dst[a].at[p], send_sem=local_sems.at[a],
                                           recv_sem=own_sems.at[a], device_id=_sibling(), device_id_type=MESH)
        out, arrive, onward, landed = [], [], [], []
        for k in range(1, N_CHIPS):
            ici = dict(send_sem=send_sems.at[k - 1, a], recv_sem=recv_sems.at[k - 1, a], device_id=_peer(k),
                       device_id_type=MESH)
            d2d = dict(send_sem=fsend_sems.at[k - 1, a], recv_sem=frecv_sems.at[k - 1, a], device_id=_sibling(),
                       device_id_type=MESH)
            got, theirs = dst[a].at[p ^ k, c], dst[a].at[p ^ k, 1 - c]
            out.append(pltpu.make_async_remote_copy(src_ref=src[a].at[c], dst_ref=dst[a].at[p, c], **ici))
            arrive.append(pltpu.make_async_remote_copy(src_ref=src[a].at[c], dst_ref=got, **ici))
            onward.append(pltpu.make_async_remote_copy(src_ref=got, dst_ref=got, **d2d))
            landed.append(pltpu.make_async_remote_copy(src_ref=theirs, dst_ref=theirs, **d2d))
        return own, out, arrive, onward, landed

    def stages(self, steps):
        n = len(self.shards)

        def start(src, dst, sems):
            for a in range(n):
                own, out, _, _, _ = self._copies(src, dst, sems, a)
                own.start()
                for cp in out:
                    cp.start()

        def pass_on(a):
            def act(src, dst, sems):
                _, _, arrive, onward, _ = self._copies(src, dst, sems, a)
                for k in range(N_CHIPS - 1):
                    arrive[k].wait_recv()
                    onward[k].start()
            return act

        def finish(src, dst, sems):
            for a in range(n):
                own, out, _, onward, landed = self._copies(src, dst, sems, a)
                for cp in landed:
                    cp.wait_recv()
                for cp in out + onward:
                    cp.wait_send()
                own.wait()

        mids = [(min(steps - 1, int(self.pass_on[a] * steps)), pass_on(a)) for a in range(n)]
        return [(0, start)] + mids + [(steps - 1, finish)]

    def results(self, outs):
        return [o.reshape((N_CHIPS,) + a.shape) for o, a in zip(outs, self.shards)]


class _ExchangeCargo:
    def __init__(self, sums, whole):
        self.inputs, self.whole, n = sums, whole, len(sums)
        self.out_shape = [jax.ShapeDtypeStruct((N_CHIPS - 1,) + g.shape[1:], g.dtype) for g in sums]
        sem = pltpu.SemaphoreType.DMA((N_CHIPS - 1, n))
        self.sems = [sem, sem]

    def _copies(self, src, dst, sems):
        send_sems, recv_sems = sems
        p = _chip_index()
        return [pltpu.make_async_remote_copy(
            src_ref=src[a].at[0] if self.whole[a] else src[a].at[p ^ k], dst_ref=dst[a].at[k - 1],
            send_sem=send_sems.at[k - 1, a], recv_sem=recv_sems.at[k - 1, a], device_id=_peer(k), device_id_type=MESH)
            for k in range(1, N_CHIPS) for a in range(len(self.inputs))]

    def stages(self, steps):
        def start(src, dst, sems):
            for cp in self._copies(src, dst, sems):
                cp.start()

        def finish(src, dst, sems):
            copies = self._copies(src, dst, sems)
            for cp in copies:
                cp.wait_recv()
            for cp in copies:
                cp.wait_send()

        return [(0, start), (steps - 1, finish)]

    def results(self, outs):
        return list(outs)


def _cargo_call(body, cargo, *, name, grid, in_specs, out_specs, out_shape, scratch=()):
    n_in, n_out, n_scr = len(in_specs), len(out_specs), len(scratch)
    c_in, c_out = len(cargo.inputs), len(cargo.out_shape)
    steps = 1
    for g in grid:
        steps *= g
    stages = cargo.stages(steps)

    def wrapped(*refs):
        ins, refs = refs[:n_in], refs[n_in:]
        cin, refs = refs[:c_in], refs[c_in:]
        outs, refs = refs[:n_out], refs[n_out:]
        cout, refs = refs[:c_out], refs[c_out:]
        scr, sems = refs[:n_scr], refs[n_scr:]
        if not grid:
            for _, act in stages:
                act(cin, cout, sems)
            return
        step = 0
        for d in range(len(grid)):
            step = step * grid[d] + pl.program_id(d)
        pl.when(step == 0)(functools.partial(stages[0][1], cin, cout, sems))
        body(*ins, *outs, *scr)
        for at, act in stages[1:]:
            pl.when(step == at)(functools.partial(act, cin, cout, sems))

    params = dict(vmem_limit_bytes=VMEM_LIMIT)
    if grid:
        params["dimension_semantics"] = ("arbitrary",) * len(grid)
    res = pl.pallas_call(
        wrapped, name=name, grid=grid, in_specs=list(in_specs) + [ANY] * c_in, out_specs=list(out_specs) + [ANY] * c_out,
        out_shape=list(out_shape) + cargo.out_shape, scratch_shapes=list(scratch) + cargo.sems,
        compiler_params=pltpu.CompilerParams(**params))
    return lambda *args: (lambda r: (r[:n_out], cargo.results(r[n_out:])))(res(*args, *cargo.inputs))


def _split_halves(g):
    return g.reshape(g.shape[0], 2, g.shape[1] // 2, g.shape[2])


def _ew_tile(rows):
    return 256 if rows % 256 == 0 else rows


def _prefetch_call(body, *, name, grid, in_specs, out_specs, out_shape, **kw):
    return pl.pallas_call(
        body, name=name, out_shape=out_shape,
        grid_spec=pltpu.PrefetchScalarGridSpec(num_scalar_prefetch=1, grid=grid, in_specs=in_specs, out_specs=out_specs),
        compiler_params=pltpu.CompilerParams(dimension_semantics=("arbitrary",) * len(grid),
                                             vmem_limit_bytes=VMEM_LIMIT), **kw)


def _presum_swap(split, name):
    n = len(split)

    def body(*refs):
        src, got = refs[:n], refs[n:2 * n]
        send_sems, recv_sems = refs[2 * n:]
        c = lax.axis_index("c")
        remote = [pltpu.make_async_remote_copy(src_ref=src[a].at[:, 1 - c], dst_ref=got[a],
                                               send_sem=send_sems.at[a], recv_sem=recv_sems.at[a],
                                               device_id=_sibling(), device_id_type=MESH) for a in range(n)]
        for cp in remote:
            cp.start()
        for cp in remote:
            cp.wait_recv()
        for cp in remote:
            cp.wait_send()

    sem = pltpu.SemaphoreType.DMA((n,))
    return pl.pallas_call(
        body, name=name, in_specs=[ANY] * n, out_specs=[ANY] * n,
        out_shape=[jax.ShapeDtypeStruct((g.shape[0],) + g.shape[2:], g.dtype) for g in split],
        scratch_shapes=[sem, sem])(*split)


def _add_pair(mine, theirs, place, name):
    lead, _, rows, cols = mine.shape
    tm = _ew_tile(rows)

    def body(place_ref, a_ref, b_ref, o_ref):
        o_ref[...] = (a_ref[...].astype(f32) + b_ref[...].astype(f32)).astype(o_ref.dtype)

    blk = pl.BlockSpec((lead, tm, cols), lambda i, pc: (0, i, 0))
    return _prefetch_call(
        body, name=name, grid=(rows // tm,),
        in_specs=[pl.BlockSpec((lead, None, tm, cols), lambda i, pc: (0, pc[1], i, 0)), blk], out_specs=blk,
        out_shape=jax.ShapeDtypeStruct(theirs.shape, theirs.dtype))(place, mine, theirs)


class _NoCargo:
    def __init__(self, results=()):
        self.inputs, self.out_shape, self.sems, self._results = [], [], [], list(results)

    def stages(self, steps):
        return [(0, lambda src, dst, sems: None), (steps - 1, lambda src, dst, sems: None)]

    def results(self, outs):
        return self._results


def _cargo_alone(cargo, name):
    return _cargo_call(None, cargo, name=name, grid=(), in_specs=[], out_specs=[], out_shape=[])()[1]


def _sum_slots(own, recv, place, whole, name):
    _, rows, cols = own.shape
    tm = _ew_tile(rows)

    def body(place_ref, own_ref, r_ref, o_ref):
        r0, r1, r2 = [r_ref[k].astype(f32) for k in range(N_CHIPS - 1)]
        o_ref[...] = (own_ref[...].astype(f32) + r0) + (r1 + r2)

    return _prefetch_call(
        body, name=name, grid=(rows // tm,),
        in_specs=[pl.BlockSpec((None, tm, cols), lambda i, pc: (0 if whole else pc[0], i, 0)),
                  pl.BlockSpec((N_CHIPS - 1, tm, cols), lambda i, pc: (0, i, 0))],
        out_specs=pl.BlockSpec((None, tm, cols), lambda i, pc: (pc[1], i, 0)),
        out_shape=jax.ShapeDtypeStruct((2, rows, cols), f32))(place, own, recv)


def _sibling_fill(bufs):
    n = len(bufs)

    def body(*refs):
        src, dst = refs[:n], refs[n:2 * n]
        send_sems, recv_sems = refs[2 * n:]
        c = lax.axis_index("c")
        remote = [pltpu.make_async_remote_copy(src_ref=src[a].at[c], dst_ref=dst[a].at[c], send_sem=send_sems.at[a],
                                               recv_sem=recv_sems.at[a], device_id=_sibling(), device_id_type=MESH)
                  for a in range(n)]
        for cp in remote:
            cp.start()
        for a in range(n):
            pltpu.make_async_remote_copy(src_ref=src[a].at[c], dst_ref=dst[a].at[1 - c], send_sem=send_sems.at[a],
                                         recv_sem=recv_sems.at[a], device_id=_sibling(), device_id_type=MESH).wait_recv()
        for cp in remote:
            cp.wait_send()

    sem = pltpu.SemaphoreType.DMA((n,))
    out = pl.pallas_call(
        body, name="sibling_fill", in_specs=[ANY] * n, out_specs=[ANY] * n,
        out_shape=[jax.ShapeDtypeStruct(b.shape, b.dtype) for b in bufs],
        input_output_aliases={a: a for a in range(n)}, scratch_shapes=[sem, sem])(*bufs)
    return [o.reshape(2 * b.shape[1], b.shape[2]) for o, b in zip(out, bufs)]


def _adamw(w, m, v, grad, name, tm):
    rows, cols = w.shape
    gcols = grad.shape[1]
    if tm == rows:
        gspec = pl.BlockSpec(grad.shape, lambda i: (0, 0))
    else:
        gspec = pl.BlockSpec((tm, gcols), lambda i: (i, 0))

    def body(w_ref, m_ref, v_ref, gin_ref, g_ref, d_ref, nm_ref, nv_ref):
        g = gin_ref[0:tm, 0:cols]
        w_, m_, v_ = w_ref[...], m_ref[...], v_ref[...]
        m_ = ADAM_B1 * m_ + (1.0 - ADAM_B1) * g
        v_ = ADAM_B2 * v_ + (1.0 - ADAM_B2) * jnp.square(g)
        m_hat = m_ / (1.0 - ADAM_B1 ** ADAM_STEP)
        v_hat = v_ / (1.0 - ADAM_B2 ** ADAM_STEP)
        g_ref[...] = g
        d_ref[...] = -ADAM_LR * (m_hat / (jnp.sqrt(v_hat) + ADAM_EPS) + ADAM_WD * w_)
        nm_ref[...] = m_
        nv_ref[...] = v_

    blk = pl.BlockSpec((tm, cols), lambda i: (i, 0))
    return _call(
        body, name=name, grid=(rows // tm,),
        in_specs=[blk, blk, blk, gspec], out_specs=[blk] * 4,
        out_shape=[jax.ShapeDtypeStruct((rows, cols), f32)] * 4)(w, m, v, grad)


def _rope_tables(s):
    half = HEAD_DIM // 2
    inv_freq = 1.0 / (10000.0 ** (jnp.arange(half, dtype=f32) / half))
    ang = jnp.arange(s, dtype=f32)[:, None] * inv_freq[None, :]
    cos, sin = jnp.cos(ang), jnp.sin(ang)
    return jnp.tile(cos, (1, LANES // half)), jnp.tile(jnp.concatenate([-sin, sin], axis=1), (1, LANES // HEAD_DIM))


def _local_step(x, target, win_g, gather, exchange, b_gate, w_pool, pool_scale, ln1_g, ln1_b, convb_g, ln2_g, ln2_b):
    s = x.shape[0]
    nb = s // MOBA_BLOCK
    cos_t, sin_t = _rope_tables(s)
    x_bf = x.astype(bf16)

    z, (wba_g, wbp_g, wout_g) = _inproj(x_bf, win_g, cos_t, sin_t, b_gate, gather["inproj"])
    wba = wba_g.transpose(1, 0, 2).reshape(D_ATTN, D_MODEL)
    wbp = wbp_g.transpose(1, 0, 2).reshape(D_POOL, D_MODEL)
    wout = wout_g.reshape(D_MODEL, D_MODEL)
    (o_bf, lse), (wg_g, wu_g, convw_g) = _attn_fwd(z, nb, gather["attn_fwd"])
    pooled_bf, mixed, pm_bf = _pool_fwd(z, w_pool, pool_scale)
    m_bf = _branch_merge(o_bf, pm_bf, z, wba, wbp)
    x1_bf, xhat1, rstd1 = _out_ln1(m_bf, wout, x, ln1_g, ln1_b)
    (a, up, ac, hh_bf), (wd_g,) = _ffn_up(x1_bf, wg_g, wu_g, convw_g, convb_g, gather["ffn_up"])
    wd = wd_g.reshape(D_FF_PAD, D_MODEL)
    loss, dh2, dh2_bf, d_ln2_g, d_ln2_b = _ffn_down_ln2_loss(hh_bf, wd, xhat1, ln1_g, ln1_b, ln2_g, ln2_b, target)

    da_bf, dup_bf, dconv = _ffn_act_bwd(dh2_bf, wd_g, ac, a, up, convw_g)
    d_wd = _matmul_tn(hh_bf, dh2_bf, 1, "dw_ffn_down").reshape(N_CHIPS, FF_PAD, D_MODEL)
    d_wg = _matmul_tn(da_bf, x1_bf, 1, "dw_ffn_gate").reshape(N_CHIPS, FF_PAD, D_MODEL)
    d_wu = _matmul_tn(dup_bf, x1_bf, 1, "dw_ffn_up").reshape(N_CHIPS, FF_PAD, D_MODEL)
    (dh1, dh1_bf, d_ln1_g, d_ln1_b), got_down = _ffn_in_bwd_ln1(
        da_bf, dup_bf, wg_g, wu_g, dh2, xhat1, rstd1, ln1_g, exchange("ffn_in_bwd_ln1", [("w_ffn_down", d_wd)]))
    d_wout = _matmul_tn(m_bf, dh1_bf, 1, "dw_out").reshape(N_CHIPS, D_MODEL // N_CHIPS, D_MODEL)
    dzg_bf, dya_bf, dyp_bf, do_bf, dpm, d_bgate = _merge_bwd(dh1_bf, wout, o_bf, pm_bf, z, wba, wbp)
    d_wba = _matmul_tn(o_bf, dya_bf, N_CHIPS, "dw_branch_attn")
    d_wbp = _matmul_tn(pm_bf, dyp_bf, N_CHIPS, "dw_branch_pool")
    du_bf, d_wpool, d_pscale = _pool_bwd(dpm, mixed, pooled_bf, w_pool, pool_scale)
    small = _pack_small([d_bgate, d_wpool, d_pscale, d_ln1_g, d_ln1_b, d_ln2_g, d_ln2_b, dconv[:, 3, :],
                         loss[0:1, :]])[None]
    (dq_bf, dk_bf, dv_bf), got_ffn = _attn_bwd(
        z, o_bf, lse, do_bf, cos_t, sin_t, nb,
        exchange("attn_bwd", [("w_ffn_gate", d_wg), ("w_ffn_up", d_wu), ("conv_w", dconv), ("small", small)]))
    dz_bf = jnp.concatenate([dq_bf, dk_bf, dv_bf, du_bf, dzg_bf], axis=1)
    d_win, got_branch = _matmul_tn(
        x_bf, dz_bf, N_CHIPS, "dw_in",
        cargo=exchange("dw_in", [("w_out", d_wout), ("w_branch_attn", d_wba), ("w_branch_pool", d_wbp)]))
    (grad_x,), got_in = _inproj_dx(dz_bf, win_g, dh1, exchange("inproj_dx", [("w_in", d_win)]))
    return grad_x, got_down + got_ffn + got_branch + got_in


SMALL_ROWS = 592


def _pack_small(parts):
    flat = jnp.concatenate([p.reshape(-1) for p in parts])
    return jnp.pad(flat, (0, SMALL_ROWS * LANES - flat.shape[0])).reshape(SMALL_ROWS, LANES)


def _unpack_small(packed, shapes):
    flat = packed.reshape(-1)
    out, off = [], 0
    for shp in shapes:
        n = 1
        for d in shp:
            n *= d
        out.append(flat[off:off + n].reshape(shp))
        off += n
    return out


def _pad_conv_b(cb):
    return jnp.pad(cb.reshape(N_CHIPS, FF_SHARD), ((0, 0), (0, FF_PAD - FF_SHARD)))


def kernel(x, w_in, b_gate, w_branch_attn, w_pool, pool_scale, w_branch_pool, w_out, ln1_g, ln1_b, w_ffn_gate, w_ffn_up, conv_w, conv_b, w_ffn_down, ln2_g, ln2_b, loss_target, m_w_in, m_b_gate, m_w_branch_attn, m_w_pool, m_pool_scale, m_w_branch_pool, m_w_out, m_ln1_g, m_ln1_b, m_w_ffn_gate, m_w_ffn_up, m_conv_w, m_conv_b, m_w_ffn_down, m_ln2_g, m_ln2_b, v_w_in, v_b_gate, v_w_branch_attn, v_w_pool, v_pool_scale, v_w_branch_pool, v_w_out, v_ln1_g, v_ln1_b, v_w_ffn_gate, v_w_ffn_up, v_conv_w, v_conv_b, v_w_ffn_down, v_ln2_g, v_ln2_b):
    pad_rows = lambda w: jnp.pad(w.astype(bf16), ((0, FF_PAD - FF_SHARD), (0, 0)))
    shards = [w_in[0].astype(bf16), w_branch_attn[0].astype(bf16), w_branch_pool[0].astype(bf16),
              w_out[0].astype(bf16), pad_rows(w_ffn_gate[0].T), pad_rows(w_ffn_up[0].T), pad_rows(w_ffn_down[0]),
              jnp.pad(conv_w[0], ((0, CONV_ROWS - 3), (0, FF_PAD - FF_SHARD)))]
    (win_g,) = _cargo_alone(_GatherCargo(shards[:1], [0.0]), "gather_w_in")
    gather = {"inproj": _GatherCargo(shards[1:4], [0.3, 0.5, 0.75]),
              "attn_fwd": _GatherCargo(shards[4:6] + shards[7:], [0.45, 0.88, 0.9]),
              "ffn_up": _GatherCargo(shards[6:7], [0.8])}

    place = jnp.stack([2 * lax.axis_index("x") + lax.axis_index("y"), lax.axis_index("c")]).astype(jnp.int32)
    chip_sums = {}

    def exchange(carrier, named):
        split = [_split_halves(g) for _, g in named]
        got = _presum_swap(split, "presum_swap_" + carrier)
        sums = [_add_pair(a, b, place, "presum_" + nm) for a, b, (nm, _) in zip(split, got, named)]
        chip_sums.update({nm: s for (nm, _), s in zip(named, sums)})
        return _ExchangeCargo(sums, [nm == "small" for nm, _ in named])

    convb_g = _pad_conv_b(conv_b).reshape(N_CHIPS, 1, FF_PAD)
    grad_x, brought = _local_step(
        x[0], loss_target[0], win_g, gather, exchange, b_gate, w_pool[0], pool_scale, ln1_g, ln1_b, convb_g, ln2_g, ln2_b)
    slots = dict(zip(chip_sums, brought))
    names = ["w_in", "w_branch_attn", "w_branch_pool", "w_out", "w_ffn_gate", "w_ffn_up", "w_ffn_down", "conv_w", "small"]
    grads = _sibling_fill([_sum_slots(chip_sums[nm], slots[nm], place, nm == "small", "sum_" + nm) for nm in names])

    weights = [w_in[0], w_branch_attn[0], w_branch_pool[0], w_out[0], w_ffn_gate[0].T, w_ffn_up[0].T, w_ffn_down[0], conv_w[0]]
    m_in = [m_w_in[0], m_w_branch_attn[0], m_w_branch_pool[0], m_w_out[0], m_w_ffn_gate[0].T, m_w_ffn_up[0].T, m_w_ffn_down[0], m_conv_w[0]]
    v_in = [v_w_in[0], v_w_branch_attn[0], v_w_branch_pool[0], v_w_out[0], v_w_ffn_gate[0].T, v_w_ffn_up[0].T, v_w_ffn_down[0], v_conv_w[0]]
    tiles = [256, 256, 256, 128, 88, 88, 88, 3]
    res = {}
    for i, nm in enumerate(names[:-1]):
        outs4 = _adamw(weights[i], m_in[i], v_in[i], grads[i], "adamw_" + nm, tiles[i])
        res[nm] = [(r.T if nm in ("w_ffn_gate", "w_ffn_up") else r)[None] for r in outs4]
    loss = grads[-1][SMALL_ROWS - 4, 0]

    small_names = ["b_gate", "w_pool", "pool_scale", "ln1_g", "ln1_b", "ln2_g", "ln2_b", "conv_b"]
    small_w = [b_gate, w_pool, pool_scale, ln1_g, ln1_b, ln2_g, ln2_b, _pad_conv_b(conv_b)]
    small_m = [m_b_gate, m_w_pool, m_pool_scale, m_ln1_g, m_ln1_b, m_ln2_g, m_ln2_b, _pad_conv_b(m_conv_b)]
    small_v = [v_b_gate, v_w_pool, v_pool_scale, v_ln1_g, v_ln1_b, v_ln2_g, v_ln2_b, _pad_conv_b(v_conv_b)]
    packed = _adamw(_pack_small(small_w), _pack_small(small_m), _pack_small(small_v), grads[-1],
                    "adamw_small", SMALL_ROWS)
    shapes = [w.shape for w in small_w]
    for kind in range(4):
        for nm, val in zip(small_names, _unpack_small(packed[kind], shapes)):
            if nm == "conv_b":
                val = val[:, :FF_SHARD].reshape(1, 4 * FF_SHARD)
            res.setdefault(nm, [None] * 4)[kind] = val

    order = ["w_in", "b_gate", "w_branch_attn", "w_pool", "pool_scale", "w_branch_pool", "w_out", "ln1_g", "ln1_b",
             "w_ffn_gate", "w_ffn_up", "conv_w", "conv_b", "w_ffn_down", "ln2_g", "ln2_b"]
    outs = [loss, grad_x[None]]
    for kind in range(4):
        outs += [res[nm][kind] for nm in order]
    return tuple(outs)
```

```python
import functools

import jax
import jax.numpy as jnp
from jax import lax
from jax.experimental import pallas as pl
from jax.experimental.pallas import tpu as pltpu

D_MODEL = 1024
HEAD_DIM = 64
D_ATTN = 512
D_POOL = 512
MOBA_BLOCK = 256
MOBA_TOPK = 3
POOL_GROUP = 128
MAX_WINDOW = 16
FF_SHARD = 704
FF_PAD = 768
D_FF_PAD = 4 * FF_PAD
N_CHIPS = 4
LANES = 128
ALPHA = (2.0 * 1) ** 0.25
LN_EPS = 1e-5
NEG = -1e30
SCALE = HEAD_DIM ** -0.5
ADAM_LR, ADAM_B1, ADAM_B2, ADAM_EPS, ADAM_WD, ADAM_STEP = 0.001, 0.9, 0.999, 1e-08, 0.01, 10
VMEM_LIMIT = 56 * 1024 * 1024
MESH = pl.DeviceIdType.MESH

bf16 = jnp.bfloat16
f32 = jnp.float32


def _dot(a, b):
    return jnp.dot(a, b, preferred_element_type=f32)


def _dot_nt(a, b):
    return lax.dot_general(a, b, (((1,), (1,)), ((), ())), preferred_element_type=f32)


def _dot_tn(a, b):
    return lax.dot_general(a, b, (((0,), (0,)), ((), ())), preferred_element_type=f32)


def _call(body, *, name, grid, in_specs, out_specs, out_shape, scratch=(), cargo=None):
    if cargo is not None:
        return _cargo_call(body, cargo, name=name, grid=grid, in_specs=in_specs, out_specs=out_specs,
                           out_shape=out_shape, scratch=scratch)
    return pl.pallas_call(
        body, name=name, grid=grid, in_specs=in_specs, out_specs=out_specs, out_shape=out_shape,
        scratch_shapes=list(scratch),
        compiler_params=pltpu.CompilerParams(dimension_semantics=("arbitrary",) * len(grid),
                                             vmem_limit_bytes=VMEM_LIMIT))


def _rot_half(t):
    lane = lax.broadcasted_iota(jnp.int32, t.shape, 1)
    first = (lane % HEAD_DIM) < (HEAD_DIM // 2)
    return jnp.where(first, pltpu.roll(t, LANES - HEAD_DIM // 2, 1), pltpu.roll(t, HEAD_DIM // 2, 1))


def _rope(t, cos, sin_signed):
    return t * cos + _rot_half(t) * sin_signed


def _rope_bwd(d, cos, sin_signed):
    return d * cos + _rot_half(d * sin_signed)


def _gelu_parts(a):
    cdf = 0.5 * (1.0 + lax.erf(a * (2.0 ** -0.5)))
    pdf = jnp.exp(-0.5 * a * a) * ((2.0 * jnp.pi) ** -0.5)
    return a * cdf, cdf + a * pdf


def _layer_norm(h, g, b):
    mu = jnp.mean(h, axis=-1, keepdims=True)
    xc = h - mu
    var = jnp.mean(xc * xc, axis=-1, keepdims=True)
    rstd = lax.rsqrt(var + LN_EPS)
    xhat = xc * rstd
    return xhat * g + b, xhat, rstd


def _layer_norm_bwd(dy, xhat, rstd, g):
    dxh = dy * g
    m1 = jnp.mean(dxh, axis=-1, keepdims=True)
    m2 = jnp.mean(dxh * xhat, axis=-1, keepdims=True)
    return rstd * (dxh - m1 - xhat * m2)


def _inproj(x_bf, win_g, cos_t, sin_t, b_gate, cargo):
    s = x_bf.shape[0]
    tm, tn = 1024, 512

    def body(x_ref, w_ref, cos_ref, sin_ref, b_ref, o_ref):
        j = pl.program_id(1)
        acc = _dot(x_ref[...], w_ref[...])

        @pl.when(j < 2)
        def _():
            for c in range(tn // LANES):
                sl = slice(c * LANES, (c + 1) * LANES)
                o_ref[:, sl] = _rope(acc[:, sl], cos_ref[...], sin_ref[...])

        @pl.when((j >= 2) & (j < 4))
        def _():
            o_ref[...] = acc

        @pl.when(j >= 4)
        def _():
            o_ref[...] = jax.nn.sigmoid(acc + b_ref[...])

    (z,), carried = _cargo_call(
        body, cargo, name="inproj", grid=(s // tm, 4 * D_MODEL // tn),
        in_specs=[pl.BlockSpec((tm, D_MODEL), lambda i, j: (i, 0)),
                  pl.BlockSpec((None, D_MODEL, tn), lambda i, j: (j // 2, 0, j % 2)),
                  pl.BlockSpec((tm, LANES), lambda i, j: (i, 0)),
                  pl.BlockSpec((tm, LANES), lambda i, j: (i, 0)),
                  pl.BlockSpec((1, tn), lambda i, j: (0, jnp.maximum(j - 4, 0)))],
        out_specs=[pl.BlockSpec((tm, tn), lambda i, j: (i, j))],
        out_shape=[jax.ShapeDtypeStruct((s, 4 * D_MODEL), f32)])(x_bf, win_g, cos_t, sin_t, b_gate)
    return z, carried


STAT_ROWS = 8


def _pair_rows(v0, v1, fill):
    r = lax.broadcasted_iota(jnp.int32, (STAT_ROWS, v0.shape[1]), 0)
    return jnp.where(r == 0, v0, jnp.where(r == 1, v1, fill))


def _head_lanes(shape):
    lane = lax.broadcasted_iota(jnp.int32, shape, 1)
    return lane < HEAD_DIM, lane >= HEAD_DIM


def _head_rows(shape):
    row = lax.broadcasted_iota(jnp.int32, shape, 0)
    return row < HEAD_DIM, row >= HEAD_DIM


def _moba_select(q_bf, k_all, i, nb):
    k_mean = jnp.mean(k_all.reshape(nb, MOBA_BLOCK, LANES), axis=1)
    n_io = lax.broadcasted_iota(jnp.int32, (nb, MOBA_BLOCK), 0)
    past = n_io < i
    sels = []
    for head in _head_lanes((nb, LANES)):
        gate = _dot_nt(jnp.where(head, k_mean, 0.0).astype(bf16), q_bf)
        g = jnp.where(past, gate, NEG)
        rank = jnp.zeros((nb, MOBA_BLOCK), f32)
        for m in range(nb):
            gm = g[m:m + 1, :]
            rank = rank + jnp.where((gm > g) | ((gm == g) & (m < n_io)), 1.0, 0.0)
        sels.append(jnp.where(past & (rank < MOBA_TOPK), 1.0, 0.0))
    return sels


def _attn_fwd(z, nb, cargo):
    s = z.shape[0]

    def body(q_ref, k_ref, v_ref, o_ref, lse_ref, vt_s):
        i = pl.program_id(1)

        @pl.when(i == 0)
        def _():
            for j in range(nb):
                vt_s[j] = v_ref[j * MOBA_BLOCK:(j + 1) * MOBA_BLOCK, :].T.astype(bf16)

        k_heads = _head_lanes((MOBA_BLOCK, LANES))
        o_heads = _head_rows((LANES, MOBA_BLOCK))

        def block(j, qs, allow, m_old, l_old):
            kj, vtj = k_ref[j * MOBA_BLOCK:(j + 1) * MOBA_BLOCK, :], vt_s[j]
            m_new, alpha, l_new, pv = [], [], [], None
            for h in range(2):
                sc = jnp.where(allow[h], _dot_nt(jnp.where(k_heads[h], kj, 0.0).astype(bf16), qs), NEG)
                mx = jnp.max(sc, axis=0, keepdims=True)
                mn = mx if m_old is None else jnp.maximum(m_old[h], mx)
                p = jnp.exp(sc - mn)
                lsum = jnp.sum(p, axis=0, keepdims=True)
                if m_old is None:
                    a = None
                else:
                    a = jnp.exp(m_old[h] - mn)
                    lsum = a * l_old[h] + lsum
                t = _dot(jnp.where(o_heads[h], vtj, jnp.zeros_like(vtj)), p.astype(bf16))
                pv = t if pv is None else pv + t
                m_new.append(mn)
                alpha.append(a)
                l_new.append(lsum)
            return m_new, alpha, l_new, pv

        def query_block(c):
            q = q_ref[...]
            sels = _moba_select(q.astype(bf16), k_ref[...], c, nb)
            qs = (q * SCALE).astype(bf16)
            key = lax.broadcasted_iota(jnp.int32, (MOBA_BLOCK, MOBA_BLOCK), 0)
            qry = lax.broadcasted_iota(jnp.int32, (MOBA_BLOCK, MOBA_BLOCK), 1)
            m, _, l, acc = block(c, qs, [key <= qry] * 2, None, None)
            for j in range(c):
                m, alpha, l, pv = block(j, qs, [sels[h][j:j + 1, :] > 0.0 for h in range(2)], m, l)
                acc = acc * jnp.where(o_heads[0], alpha[0], alpha[1]) + pv
            o_ref[...] = (acc / jnp.where(o_heads[0], l[0], l[1])).T.astype(bf16)
            lse_ref[...] = _pair_rows(m[0] + jnp.log(l[0]), m[1] + jnp.log(l[1]), 0.0)

        for c in range(nb):
            pl.when(i == c)(functools.partial(query_block, c))

    return _cargo_call(
        body, cargo, name="attn_fwd", grid=(D_ATTN // LANES, nb),
        in_specs=[pl.BlockSpec((MOBA_BLOCK, LANES), lambda hp, i: (i, hp)),
                  pl.BlockSpec((s, LANES), lambda hp, i: (0, 4 + hp)),
                  pl.BlockSpec((s, LANES), lambda hp, i: (0, 8 + hp))],
        out_specs=[pl.BlockSpec((MOBA_BLOCK, LANES), lambda hp, i: (i, hp)),
                   pl.BlockSpec((None, STAT_ROWS, MOBA_BLOCK), lambda hp, i: (hp, 0, i))],
        out_shape=[jax.ShapeDtypeStruct((s, D_ATTN), bf16),
                   jax.ShapeDtypeStruct((D_ATTN // LANES, STAT_ROWS, s), f32)],
        scratch=[pltpu.VMEM((nb, LANES, MOBA_BLOCK), bf16)])(z, z, z)


def _window_select(snaps, g):
    return jnp.where(g == 0, snaps[0], jnp.where(g == 1, snaps[1], jnp.where(g == 2, snaps[2], snaps[3])))


def _pool_fwd(z, w_pool, pool_scale):
    s = z.shape[0]

    def body(u_ref, w_ref, sc_ref, pooled_ref, mixed_ref, pm_ref, pad):
        g = pl.program_id(0)
        u = u_ref[...]
        pad[0:MAX_WINDOW, :] = jnp.zeros((MAX_WINDOW, POOL_GROUP), f32)
        pad[MAX_WINDOW:MAX_WINDOW + s, :] = u
        acc = u
        snaps = []
        for d in range(1, MAX_WINDOW):
            acc = acc + pad[MAX_WINDOW - d:MAX_WINDOW - d + s, :]
            if d + 1 in (2, 4, 8, 16):
                snaps.append(acc)
        win = _window_select(snaps, g)
        t = lax.broadcasted_iota(jnp.int32, (s, POOL_GROUP), 0)
        count = jnp.minimum(t + 1, jnp.left_shift(2, g)).astype(f32)
        pooled = (win / count - u).astype(bf16)
        mixed = _dot(pooled, w_ref[...].astype(bf16))
        pooled_ref[...] = pooled
        mixed_ref[...] = mixed
        pm_ref[...] = (mixed * sc_ref[...]).astype(bf16)

    blk = pl.BlockSpec((s, POOL_GROUP), lambda g: (0, g))
    return _call(
        body, name="pool_fwd", grid=(4,),
        in_specs=[pl.BlockSpec((s, POOL_GROUP), lambda g: (0, 12 + g)),
                  pl.BlockSpec((None, POOL_GROUP, POOL_GROUP), lambda g: (g, 0, 0)),
                  pl.BlockSpec((1, POOL_GROUP), lambda g: (0, g))],
        out_specs=[blk, blk, blk],
        out_shape=[jax.ShapeDtypeStruct((s, D_POOL), bf16), jax.ShapeDtypeStruct((s, D_POOL), f32),
                   jax.ShapeDtypeStruct((s, D_POOL), bf16)],
        scratch=[pltpu.VMEM((s + MAX_WINDOW, POOL_GROUP), f32)])(z, w_pool, pool_scale)


def _branch_merge(o_bf, pm_bf, z, wba, wbp):
    s = o_bf.shape[0]
    tm = 512

    def body(o_ref, pm_ref, ga_ref, gp_ref, wba_ref, wbp_ref, m_ref):
        ya = _dot(o_ref[...], wba_ref[...])
        yp = _dot(pm_ref[...], wbp_ref[...])
        m_ref[...] = (ga_ref[...] * ya + gp_ref[...] * yp).astype(bf16)

    full = lambda r, c: pl.BlockSpec((r, c), lambda i: (0, 0))
    return _call(
        body, name="branch_merge", grid=(s // tm,),
        in_specs=[pl.BlockSpec((tm, D_ATTN), lambda i: (i, 0)), pl.BlockSpec((tm, D_POOL), lambda i: (i, 0)),
                  pl.BlockSpec((tm, D_MODEL), lambda i: (i, 2)), pl.BlockSpec((tm, D_MODEL), lambda i: (i, 3)),
                  full(D_ATTN, D_MODEL), full(D_POOL, D_MODEL)],
        out_specs=pl.BlockSpec((tm, D_MODEL), lambda i: (i, 0)),
        out_shape=jax.ShapeDtypeStruct((s, D_MODEL), bf16))(o_bf, pm_bf, z, z, wba, wbp)


def _out_ln1(m_bf, wout, x, ln_g, ln_b):
    s = x.shape[0]
    tm = 512

    def body(m_ref, w_ref, x_ref, g_ref, b_ref, x1_ref, xhat_ref, rstd_ref):
        h = ALPHA * x_ref[...] + _dot(m_ref[...], w_ref[...])
        y, xhat, rstd = _layer_norm(h, g_ref[...], b_ref[...])
        x1_ref[...] = y.astype(bf16)
        xhat_ref[...] = xhat
        rstd_ref[...] = jnp.broadcast_to(rstd, (tm, LANES))

    row = pl.BlockSpec((tm, D_MODEL), lambda i: (i, 0))
    vec = pl.BlockSpec((1, D_MODEL), lambda i: (0, 0))
    return _call(
        body, name="out_ln1", grid=(s // tm,),
        in_specs=[row, pl.BlockSpec((D_MODEL, D_MODEL), lambda i: (0, 0)), row, vec, vec],
        out_specs=[row, row, pl.BlockSpec((tm, LANES), lambda i: (i, 0))],
        out_shape=[jax.ShapeDtypeStruct((s, D_MODEL), bf16), jax.ShapeDtypeStruct((s, D_MODEL), f32),
                   jax.ShapeDtypeStruct((s, LANES), f32)])(m_bf, wout, x, ln_g, ln_b)


FF_TILE = 256
FF_TILES_PER_SHARD = FF_PAD // FF_TILE
CONV_PAD = 8
CONV_ROWS = 16
GATE_SPLIT = 512


def _ff_weight_spec(rows):
    return pl.BlockSpec((None, rows, FF_TILE), lambda j: (j // FF_TILES_PER_SHARD, 0, j % FF_TILES_PER_SHARD))


def _ff_rows_spec():
    return pl.BlockSpec((None, FF_TILE, D_MODEL), lambda j: (j // FF_TILES_PER_SHARD, j % FF_TILES_PER_SHARD, 0))


def _row_shifts(x, shifts):
    s = x.shape[0]
    padded = jnp.concatenate([x, jnp.zeros((CONV_PAD, x.shape[1]), x.dtype)], axis=0)
    return [pltpu.roll(padded, d % (s + CONV_PAD), 0)[0:s] for d in shifts]


def _ffn_up(x1_bf, wg_g, wu_g, convw_g, convb_g, cargo):
    s = x1_bf.shape[0]

    def body(x_ref, wg_ref, wu_ref, cw_ref, cb_ref, a_ref, up_ref, ac_ref, hh_ref):
        x1 = x_ref[...]
        a = _dot_nt(x1, wg_ref[...])
        up = _dot_nt(x1, wu_ref[...])
        a_ref[...] = a
        up_ref[...] = up
        a1, a2 = _row_shifts(a, (1, 2))
        ac = a * cw_ref[2:3, :] + a1 * cw_ref[1:2, :] + a2 * cw_ref[0:1, :] + cb_ref[...]
        ac_ref[...] = ac
        hg, _ = _gelu_parts(ac)
        hh_ref[...] = (hg * up).astype(bf16)

    col = pl.BlockSpec((s, FF_TILE), lambda j: (0, j))
    wide = jax.ShapeDtypeStruct((s, D_FF_PAD), f32)
    return _call(
        body, name="ffn_up", grid=(D_FF_PAD // FF_TILE,),
        in_specs=[pl.BlockSpec((s, D_MODEL), lambda j: (0, 0)), _ff_rows_spec(), _ff_rows_spec(),
                  _ff_weight_spec(8), _ff_weight_spec(1)],
        out_specs=[col, col, col, col],
        out_shape=[wide, wide, wide, jax.ShapeDtypeStruct((s, D_FF_PAD), bf16)],
        cargo=cargo)(x1_bf, wg_g, wu_g, convw_g, convb_g)


def _ffn_down_ln2_loss(hh_bf, wd, xhat1, ln1_g, ln1_b, ln2_g, ln2_b, target):
    s = hh_bf.shape[0]
    tm = 512

    def body(hh_ref, w_ref, xh1_ref, g1_ref, b1_ref, g2_ref, b2_ref, t_ref, loss_ref, dh_ref, dhb_ref, dg_ref, db_ref):
        i = pl.program_id(0)
        x1 = xh1_ref[...] * g1_ref[...] + b1_ref[...]
        h = ALPHA * x1 + _dot(hh_ref[...], w_ref[...])
        y, xhat, rstd = _layer_norm(h, g2_ref[...], b2_ref[...])
        err = y - t_ref[...]
        part = 0.5 * jnp.sum(jnp.mean(err * err, axis=-1, keepdims=True), axis=0, keepdims=True)
        dy = err * (1.0 / D_MODEL)

        @pl.when(i == 0)
        def _():
            loss_ref[...] = jnp.zeros_like(loss_ref)
            dg_ref[...] = jnp.zeros_like(dg_ref)
            db_ref[...] = jnp.zeros_like(db_ref)

        loss_ref[...] += jnp.broadcast_to(part, loss_ref.shape)
        dg_ref[...] += jnp.sum(dy * xhat, axis=0, keepdims=True)
        db_ref[...] += jnp.sum(dy, axis=0, keepdims=True)
        dh = _layer_norm_bwd(dy, xhat, rstd, g2_ref[...])
        dh_ref[...] = dh
        dhb_ref[...] = dh.astype(bf16)

    row = pl.BlockSpec((tm, D_MODEL), lambda i: (i, 0))
    vec = pl.BlockSpec((1, D_MODEL), lambda i: (0, 0))
    return _call(
        body, name="ffn_down_ln2_loss", grid=(s // tm,),
        in_specs=[pl.BlockSpec((tm, D_FF_PAD), lambda i: (i, 0)),
                  pl.BlockSpec((D_FF_PAD, D_MODEL), lambda i: (0, 0), pipeline_mode=pl.Buffered(1)),
                  row, vec, vec, vec, vec, row],
        out_specs=[pl.BlockSpec((8, LANES), lambda i: (0, 0)), row, row, vec, vec],
        out_shape=[jax.ShapeDtypeStruct((8, LANES), f32), jax.ShapeDtypeStruct((s, D_MODEL), f32),
                   jax.ShapeDtypeStruct((s, D_MODEL), bf16), jax.ShapeDtypeStruct((1, D_MODEL), f32),
                   jax.ShapeDtypeStruct((1, D_MODEL), f32)])(hh_bf, wd, xhat1, ln1_g, ln1_b, ln2_g, ln2_b, target)


def _ffn_act_bwd(dh2_bf, wd_g, ac, a, up, convw_g):
    s = dh2_bf.shape[0]

    def body(dh_ref, wd_ref, ac_ref, a_ref, up_ref, cw_ref, da_ref, dup_ref, dc_ref):
        dhh = _dot_nt(dh_ref[...], wd_ref[...])
        hg, dgelu = _gelu_parts(ac_ref[...])
        dup_ref[...] = (dhh * hg).astype(bf16)
        dac = dhh * up_ref[...] * dgelu
        dac1, dac2 = _row_shifts(dac, (-1, -2))
        a = a_ref[...]
        rows = [jnp.sum(d * a, axis=0, keepdims=True) for d in (dac2, dac1, dac)]
        rows.append(jnp.sum(dac, axis=0, keepdims=True))
        rows.append(jnp.zeros((CONV_ROWS - 4, FF_TILE), f32))
        dc_ref[...] = jnp.concatenate(rows, axis=0)
        da_ref[...] = (dac * cw_ref[2:3, :] + dac1 * cw_ref[1:2, :] + dac2 * cw_ref[0:1, :]).astype(bf16)

    col = pl.BlockSpec((s, FF_TILE), lambda j: (0, j))
    return _call(
        body, name="ffn_act_bwd", grid=(D_FF_PAD // FF_TILE,),
        in_specs=[pl.BlockSpec((s, D_MODEL), lambda j: (0, 0)), _ff_rows_spec(), col, col, col, _ff_weight_spec(8)],
        out_specs=[col, col, _ff_weight_spec(CONV_ROWS)],
        out_shape=[jax.ShapeDtypeStruct((s, D_FF_PAD), bf16), jax.ShapeDtypeStruct((s, D_FF_PAD), bf16),
                   jax.ShapeDtypeStruct((N_CHIPS, CONV_ROWS, FF_PAD), f32)],
    )(dh2_bf, wd_g, ac, a, up, convw_g)


def _matmul_tn(a, b, n_shards, name, tm=512, tn=1024, cargo=None):
    k, m = a.shape
    n = b.shape[1]
    tm, tn = min(tm, m), min(tn, n // n_shards)
    per = n // n_shards // tn

    def body(a_ref, b_ref, o_ref, at_s):
        @pl.when(pl.program_id(1) == 0)
        def _():
            at_s[...] = a_ref[...].T

        o_ref[...] = _dot(at_s[...], b_ref[...]).astype(bf16)

    res = _call(
        body, name=name, grid=(m // tm, n // tn),
        in_specs=[pl.BlockSpec((k, tm), lambda i, j: (0, i)), pl.BlockSpec((k, tn), lambda i, j: (0, j))],
        out_specs=[pl.BlockSpec((None, tm, tn), lambda i, j: (j // per, i, j % per))],
        out_shape=[jax.ShapeDtypeStruct((n_shards, m, n // n_shards), bf16)], scratch=[pltpu.VMEM((tm, k), bf16)],
        cargo=cargo)(a, b)
    return res[0] if cargo is None else (res[0][0], res[1])


def _ffn_in_bwd_ln1(da_bf, dup_bf, wg_g, wu_g, dh2, xhat1, rstd1, ln1_g, cargo):
    s = da_bf.shape[0]
    tm = 512

    def body(da_ref, dup_ref, wg_ref, wu_ref, dh2_ref, xh_ref, rstd_ref, g_ref, dh_ref, dhb_ref, dg_ref, db_ref):
        i = pl.program_id(0)
        dx1 = ALPHA * dh2_ref[...]
        for sh in range(N_CHIPS):
            sl = slice(sh * FF_PAD, (sh + 1) * FF_PAD)
            dx1 = dx1 + _dot(da_ref[:, sl], wg_ref[sh]) + _dot(dup_ref[:, sl], wu_ref[sh])
        xhat = xh_ref[...]

        @pl.when(i == 0)
        def _():
            dg_ref[...] = jnp.zeros_like(dg_ref)
            db_ref[...] = jnp.zeros_like(db_ref)

        dg_ref[...] += jnp.sum(dx1 * xhat, axis=0, keepdims=True)
        db_ref[...] += jnp.sum(dx1, axis=0, keepdims=True)
        dh = _layer_norm_bwd(dx1, xhat, rstd_ref[:, 0:1], g_ref[...])
        dh_ref[...] = dh
        dhb_ref[...] = dh.astype(bf16)

    row = pl.BlockSpec((tm, D_MODEL), lambda i: (i, 0))
    wide = pl.BlockSpec((tm, D_FF_PAD), lambda i: (i, 0))
    wfull = pl.BlockSpec((N_CHIPS, FF_PAD, D_MODEL), lambda i: (0, 0, 0), pipeline_mode=pl.Buffered(1))
    vec = pl.BlockSpec((1, D_MODEL), lambda i: (0, 0))
    return _call(
        body, name="ffn_in_bwd_ln1", grid=(s // tm,),
        in_specs=[wide, wide, wfull, wfull, row, row, pl.BlockSpec((tm, LANES), lambda i: (i, 0)), vec],
        out_specs=[row, row, vec, vec],
        out_shape=[jax.ShapeDtypeStruct((s, D_MODEL), f32), jax.ShapeDtypeStruct((s, D_MODEL), bf16),
                   jax.ShapeDtypeStruct((1, D_MODEL), f32), jax.ShapeDtypeStruct((1, D_MODEL), f32)],
        cargo=cargo)(da_bf, dup_bf, wg_g, wu_g, dh2, xhat1, rstd1, ln1_g)


def _merge_bwd(dh1_bf, wout, o_bf, pm_bf, z, wba, wbp):
    s = dh1_bf.shape[0]
    tm = 256

    def body(dh_ref, wout_ref, o_ref, pm_ref, ga_ref, gp_ref, wba_ref, wbp_ref,
             dzg_ref, dya_ref, dyp_ref, do_ref, dpm_ref, dbg_ref):
        i = pl.program_id(0)
        dm = _dot_nt(dh_ref[...], wout_ref[...])
        ya = _dot(o_ref[...], wba_ref[...])
        yp = _dot(pm_ref[...], wbp_ref[...])
        ga, gp = ga_ref[...], gp_ref[...]
        dza = dm * ya * ga * (1.0 - ga)
        dzp = dm * yp * gp * (1.0 - gp)

        @pl.when(i == 0)
        def _():
            dbg_ref[...] = jnp.zeros_like(dbg_ref)

        dbg_ref[:, 0:D_MODEL] += jnp.sum(dza, axis=0, keepdims=True)
        dbg_ref[:, D_MODEL:2 * D_MODEL] += jnp.sum(dzp, axis=0, keepdims=True)
        dzg_ref[:, 0:D_MODEL] = dza.astype(bf16)
        dzg_ref[:, D_MODEL:2 * D_MODEL] = dzp.astype(bf16)
        dya = (dm * ga).astype(bf16)
        dyp = (dm * gp).astype(bf16)
        dya_ref[...] = dya
        dyp_ref[...] = dyp
        do_ref[...] = _dot_nt(dya, wba_ref[...]).astype(bf16)
        dpm_ref[...] = _dot_nt(dyp, wbp_ref[...])

    row = pl.BlockSpec((tm, D_MODEL), lambda i: (i, 0))
    half = pl.BlockSpec((tm, D_ATTN), lambda i: (i, 0))
    full = lambda r, c: pl.BlockSpec((r, c), lambda i: (0, 0))
    return _call(
        body, name="merge_bwd", grid=(s // tm,),
        in_specs=[row, full(D_MODEL, D_MODEL), half, half,
                  pl.BlockSpec((tm, D_MODEL), lambda i: (i, 2)), pl.BlockSpec((tm, D_MODEL), lambda i: (i, 3)),
                  full(D_ATTN, D_MODEL), full(D_POOL, D_MODEL)],
        out_specs=[pl.BlockSpec((tm, 2 * D_MODEL), lambda i: (i, 0)), row, row, half, half,
                   pl.BlockSpec((1, 2 * D_MODEL), lambda i: (0, 0))],
        out_shape=[jax.ShapeDtypeStruct((s, 2 * D_MODEL), bf16), jax.ShapeDtypeStruct((s, D_MODEL), bf16),
                   jax.ShapeDtypeStruct((s, D_MODEL), bf16), jax.ShapeDtypeStruct((s, D_ATTN), bf16),
                   jax.ShapeDtypeStruct((s, D_POOL), f32), jax.ShapeDtypeStruct((1, 2 * D_MODEL), f32)],
    )(dh1_bf, wout, o_bf, pm_bf, z, z, wba, wbp)


def _pool_bwd(dpm, mixed, pooled_bf, w_pool, pool_scale):
    s = dpm.shape[0]

    def body(dpm_ref, mixed_ref, pooled_ref, w_ref, sc_ref, du_ref, dw_ref, dsc_ref, pad):
        g = pl.program_id(0)
        dpm_v = dpm_ref[...]
        dsc_ref[...] = jnp.sum(dpm_v * mixed_ref[...], axis=0, keepdims=True)
        dmixed = (dpm_v * sc_ref[...]).astype(bf16)
        dw_ref[...] = _dot_tn(pooled_ref[...], dmixed)
        dpooled = _dot_nt(dmixed, w_ref[...].astype(bf16))
        t = lax.broadcasted_iota(jnp.int32, (s, POOL_GROUP), 0)
        count = jnp.minimum(t + 1, jnp.left_shift(2, g)).astype(f32)
        r = dpooled / count
        pad[0:s, :] = r
        pad[s:s + MAX_WINDOW, :] = jnp.zeros((MAX_WINDOW, POOL_GROUP), f32)
        acc = r
        snaps = []
        for d in range(1, MAX_WINDOW):
            acc = acc + pad[d:d + s, :]
            if d + 1 in (2, 4, 8, 16):
                snaps.append(acc)
        du_ref[...] = (_window_select(snaps, g) - dpooled).astype(bf16)

    blk = pl.BlockSpec((s, POOL_GROUP), lambda g: (0, g))
    return _call(
        body, name="pool_bwd", grid=(4,),
        in_specs=[blk, blk, blk, pl.BlockSpec((None, POOL_GROUP, POOL_GROUP), lambda g: (g, 0, 0)),
                  pl.BlockSpec((1, POOL_GROUP), lambda g: (0, g))],
        out_specs=[blk, pl.BlockSpec((None, POOL_GROUP, POOL_GROUP), lambda g: (g, 0, 0)),
                   pl.BlockSpec((1, POOL_GROUP), lambda g: (0, g))],
        out_shape=[jax.ShapeDtypeStruct((s, D_POOL), bf16), jax.ShapeDtypeStruct((4, POOL_GROUP, POOL_GROUP), f32),
                   jax.ShapeDtypeStruct((1, D_POOL), f32)],
        scratch=[pltpu.VMEM((s + MAX_WINDOW, POOL_GROUP), f32)])(dpm, mixed, pooled_bf, w_pool, pool_scale)


def _attn_bwd(z, o_bf, lse, do_bf, cos_t, sin_t, nb, cargo):
    s = z.shape[0]

    def body(q_ref, k_ref, v_ref, o_ref, lse_ref, do_ref, cq_ref, sq_ref, cf_ref, sf_ref,
             dq_ref, dk_ref, dv_ref, kt_s, dk_s, dv_s):
        i = pl.program_id(1)

        @pl.when(i == 0)
        def _():
            dk_s[...] = jnp.zeros_like(dk_s)
            dv_s[...] = jnp.zeros_like(dv_s)
            for j in range(nb):
                kt_s[j] = k_ref[j * MOBA_BLOCK:(j + 1) * MOBA_BLOCK, :].T.astype(bf16)

        k_heads = _head_lanes((MOBA_BLOCK, LANES))
        t_heads = _head_rows((LANES, MOBA_BLOCK))

        def query_block(c):
            q = q_ref[...]
            sels = _moba_select(q.astype(bf16), k_ref[...], c, nb)
            qs = (q * SCALE).astype(bf16)
            qraw = [jnp.where(hm, q, 0.0).astype(bf16) for hm in k_heads]
            do = do_ref[...].astype(f32)
            do_bf = do_ref[...]
            dob = [jnp.where(hm, do, 0.0).astype(bf16) for hm in k_heads]
            pick = _head_lanes((STAT_ROWS, LANES))
            r8 = lax.broadcasted_iota(jnp.int32, (STAT_ROWS, LANES), 0)
            head_pick = jnp.where(((r8 == 0) & pick[0]) | ((r8 == 1) & pick[1]), 1.0, 0.0)
            delta8 = lax.dot_general(head_pick, do * o_ref[...].astype(f32), (((1,), (1,)), ((), ())),
                                     precision=lax.Precision.HIGHEST, preferred_element_type=f32)
            lse8 = lse_ref[...]
            delta, lse_h = [delta8[0:1, :], delta8[1:2, :]], [lse8[0:1, :], lse8[1:2, :]]
            key = lax.broadcasted_iota(jnp.int32, (MOBA_BLOCK, MOBA_BLOCK), 0)
            qry = lax.broadcasted_iota(jnp.int32, (MOBA_BLOCK, MOBA_BLOCK), 1)

            def block(j, allow):
                rows = slice(j * MOBA_BLOCK, (j + 1) * MOBA_BLOCK)
                kj, vj, ktj = k_ref[rows, :], v_ref[rows, :], kt_s[j]
                dqt, dk, dv = None, None, None
                for h in range(2):
                    sc = _dot_nt(jnp.where(k_heads[h], kj, 0.0).astype(bf16), qs)
                    p = jnp.where(allow[h], jnp.exp(sc - lse_h[h]), 0.0)
                    dp = _dot_nt(jnp.where(k_heads[h], vj, 0.0).astype(bf16), do_bf)
                    ds = (p * (dp - delta[h]) * SCALE).astype(bf16)
                    tv = _dot(p.astype(bf16), dob[h])
                    tk = _dot(ds, qraw[h])
                    tq = _dot(jnp.where(t_heads[h], ktj, jnp.zeros_like(ktj)), ds)
                    dqt, dk, dv = (tq, tk, tv) if dqt is None else (dqt + tq, dk + tk, dv + tv)
                dk_s[rows, :] += dk
                dv_s[rows, :] += dv
                return dqt

            dqt = block(c, [key <= qry] * 2)
            for j in range(c):
                dqt = dqt + block(j, [sels[h][j:j + 1, :] > 0.0 for h in range(2)])
            dq_ref[...] = _rope_bwd(dqt.T, cq_ref[...], sq_ref[...]).astype(bf16)

        for c in range(nb):
            pl.when(i == c)(functools.partial(query_block, c))

        @pl.when(i == nb - 1)
        def _():
            dk_ref[...] = _rope_bwd(dk_s[...], cf_ref[...], sf_ref[...]).astype(bf16)
            dv_ref[...] = dv_s[...].astype(bf16)

    qblk = pl.BlockSpec((MOBA_BLOCK, LANES), lambda hp, i: (i, hp))
    tq = pl.BlockSpec((MOBA_BLOCK, LANES), lambda hp, i: (i, 0))
    tf = pl.BlockSpec((s, LANES), lambda hp, i: (0, 0))
    colblk = pl.BlockSpec((s, LANES), lambda hp, i: (0, hp))
    return _cargo_call(
        body, cargo, name="attn_bwd", grid=(D_ATTN // LANES, nb),
        in_specs=[qblk, pl.BlockSpec((s, LANES), lambda hp, i: (0, 4 + hp)),
                  pl.BlockSpec((s, LANES), lambda hp, i: (0, 8 + hp)), qblk,
                  pl.BlockSpec((None, STAT_ROWS, MOBA_BLOCK), lambda hp, i: (hp, 0, i)), qblk, tq, tq, tf, tf],
        out_specs=[qblk, colblk, colblk],
        out_shape=[jax.ShapeDtypeStruct((s, D_ATTN), bf16)] * 3,
        scratch=[pltpu.VMEM((nb, LANES, MOBA_BLOCK), bf16), pltpu.VMEM((s, LANES), f32), pltpu.VMEM((s, LANES), f32)],
    )(z, z, z, o_bf, lse, do_bf, cos_t, sin_t, cos_t, sin_t)


def _inproj_dx(dz_bf, win_g, dh1, cargo):
    s = dz_bf.shape[0]
    tm = 512

    def body(dz_ref, w_ref, dh_ref, gx_ref):
        acc = ALPHA * dh_ref[...]
        for sh in range(N_CHIPS):
            acc = acc + _dot_nt(dz_ref[:, sh * D_MODEL:(sh + 1) * D_MODEL], w_ref[sh])
        gx_ref[...] = acc

    row = pl.BlockSpec((tm, D_MODEL), lambda i: (i, 0))
    return _call(
        body, name="inproj_dx", grid=(s // tm,),
        in_specs=[pl.BlockSpec((tm, 4 * D_MODEL), lambda i: (i, 0)),
                  pl.BlockSpec((N_CHIPS, D_MODEL, D_MODEL), lambda i: (0, 0, 0), pipeline_mode=pl.Buffered(1)), row],
        out_specs=[row], out_shape=[jax.ShapeDtypeStruct((s, D_MODEL), f32)], cargo=cargo)(dz_bf, win_g, dh1)


ANY = pl.BlockSpec(memory_space=pl.ANY)


def _chip_index():
    return 2 * lax.axis_index("x") + lax.axis_index("y")


def _peer(k):
    x, y, c = lax.axis_index("x"), lax.axis_index("y"), lax.axis_index("c")
    return (x ^ (k >> 1), y ^ (k & 1), c)


def _sibling():
    return (lax.axis_index("x"), lax.axis_index("y"), 1 - lax.axis_index("c"))


class _GatherCargo:
    def __init__(self, shards, pass_on):
        self.shards, self.pass_on, n = shards, pass_on, len(shards)
        self.inputs = [a.reshape(2, a.shape[0] // 2, a.shape[1]) for a in shards]
        self.out_shape = [jax.ShapeDtypeStruct((N_CHIPS,) + a.shape, a.dtype) for a in self.inputs]
        sem, one = pltpu.SemaphoreType.DMA((N_CHIPS - 1, n)), pltpu.SemaphoreType.DMA((n,))
        self.sems = [sem, sem, sem, sem, one, one]

    def _copies(self, src, dst, sems, a):
        send_sems, recv_sems, fsend_sems, frecv_sems, local_sems, own_sems = sems
        p, c = _chip_index(), lax.axis_index("c")
        own = pltpu.make_async_remote_copy(src_ref=src[a], dst_ref=dst[a].at[p], send_sem=local_sems.at[a],
                                           recv_sem=own_sems.at[a], device_id=_sibling(), device_id_type=MESH)
        out, arrive, onward, landed = [], [], [], []
        for k in range(1, N_CHIPS):
            ici = dict(send_sem=send_sems.at[k - 1, a], recv_sem=recv_sems.at[k - 1, a], device_id=_peer(k),
                       device_id_type=MESH)
            d2d = dict(send_sem=fsend_sems.at[k - 1, a], recv_sem=frecv_sems.at[k - 1, a], device_id=_sibling(),
                       device_id_type=MESH)
            got, theirs = dst[a].at[p ^ k, c], dst[a].at[p ^ k, 1 - c]
            out.append(pltpu.make_async_remote_copy(src_ref=src[a].at[c], dst_ref=dst[a].at[p, c], **ici))
            arrive.append(pltpu.make_async_remote_copy(src_ref=src[a].at[c], dst_ref=got, **ici))
            onward.append(pltpu.make_async_remote_copy(src_ref=got, dst_ref=got, **d2d))
            landed.append(pltpu.make_async_remote_copy(src_ref=theirs, dst_ref=theirs, **d2d))
        return own, out, arrive, onward, landed

    def stages(self, steps):
        n = len(self.shards)

        def start(src, dst, sems):
            for a in range(n):
                own, out, _, _, _ = self._copies(src, dst, sems, a)
                own.start()
                for cp in out:
                    cp.start()

        def pass_on(a):
            def act(src, dst, sems):
                _, _, arrive, onward, _ = self._copies(src, dst, sems, a)
                for k in range(N_CHIPS - 1):
                    arrive[k].wait_recv()
                    onward[k].start()
            return act

        def finish(src, dst, sems):
            for a in range(n):
                own, out, _, onward, landed = self._copies(src, dst, sems, a)
                for cp in landed:
                    cp.wait_recv()
                for cp in out + onward:
                    cp.wait_send()
                own.wait()

        mids = [(min(steps - 1, int(self.pass_on[a] * steps)), pass_on(a)) for a in range(n)]
        return [(0, start)] + mids + [(steps - 1, finish)]

    def results(self, outs):
        return [o.reshape((N_CHIPS,) + a.shape) for o, a in zip(outs, self.shards)]


class _ExchangeCargo:
    def __init__(self, sums, whole):
        self.inputs, self.whole, n = sums, whole, len(sums)
        self.out_shape = [jax.ShapeDtypeStruct((N_CHIPS - 1,) + g.shape[1:], g.dtype) for g in sums]
        sem = pltpu.SemaphoreType.DMA((N_CHIPS - 1, n))
        self.sems = [sem, sem]

    def _copies(self, src, dst, sems):
        send_sems, recv_sems = sems
        p = _chip_index()
        return [pltpu.make_async_remote_copy(
            src_ref=src[a].at[0] if self.whole[a] else src[a].at[p ^ k], dst_ref=dst[a].at[k - 1],
            send_sem=send_sems.at[k - 1, a], recv_sem=recv_sems.at[k - 1, a], device_id=_peer(k), device_id_type=MESH)
            for k in range(1, N_CHIPS) for a in range(len(self.inputs))]

    def stages(self, steps):
        def start(src, dst, sems):
            for cp in self._copies(src, dst, sems):
                cp.start()

        def finish(src, dst, sems):
            copies = self._copies(src, dst, sems)
            for cp in copies:
                cp.wait_recv()
            for cp in copies:
                cp.wait_send()

        return [(0, start), (steps - 1, finish)]

    def results(self, outs):
        return list(outs)


def _cargo_call(body, cargo, *, name, grid, in_specs, out_specs, out_shape, scratch=()):
    n_in, n_out, n_scr = len(in_specs), len(out_specs), len(scratch)
    c_in, c_out = len(cargo.inputs), len(cargo.out_shape)
    steps = 1
    for g in grid:
        steps *= g
    stages = cargo.stages(steps)

    def wrapped(*refs):
        ins, refs = refs[:n_in], refs[n_in:]
        cin, refs = refs[:c_in], refs[c_in:]
        outs, refs = refs[:n_out], refs[n_out:]
        cout, refs = refs[:c_out], refs[c_out:]
        scr, sems = refs[:n_scr], refs[n_scr:]
        if not grid:
            for _, act in stages:
                act(cin, cout, sems)
            return
        step = 0
        for d in range(len(grid)):
            step = step * grid[d] + pl.program_id(d)
        pl.when(step == 0)(functools.partial(stages[0][1], cin, cout, sems))
        body(*ins, *outs, *scr)
        for at, act in stages[1:]:
            pl.when(step == at)(functools.partial(act, cin, cout, sems))

    params = dict(vmem_limit_bytes=VMEM_LIMIT)
    if grid:
        params["dimension_semantics"] = ("arbitrary",) * len(grid)
    res = pl.pallas_call(
        wrapped, name=name, grid=grid, in_specs=list(in_specs) + [ANY] * c_in, out_specs=list(out_specs) + [ANY] * c_out,
        out_shape=list(out_shape) + cargo.out_shape, scratch_shapes=list(scratch) + cargo.sems,
        compiler_params=pltpu.CompilerParams(**params))
    return lambda *args: (lambda r: (r[:n_out], cargo.results(r[n_out:])))(res(*args, *cargo.inputs))


def _split_halves(g):
    return g.reshape(g.shape[0], 2, g.shape[1] // 2, g.shape[2])


def _ew_tile(rows):
    return 256 if rows % 256 == 0 else rows


def _prefetch_call(body, *, name, grid, in_specs, out_specs, out_shape, **kw):
    return pl.pallas_call(
        body, name=name, out_shape=out_shape,
        grid_spec=pltpu.PrefetchScalarGridSpec(num_scalar_prefetch=1, grid=grid, in_specs=in_specs, out_specs=out_specs),
        compiler_params=pltpu.CompilerParams(dimension_semantics=("arbitrary",) * len(grid),
                                             vmem_limit_bytes=VMEM_LIMIT), **kw)


def _presum_swap(split, name):
    n = len(split)

    def body(*refs):
        src, got = refs[:n], refs[n:2 * n]
        send_sems, recv_sems = refs[2 * n:]
        c = lax.axis_index("c")
        remote = [pltpu.make_async_remote_copy(src_ref=src[a].at[:, 1 - c], dst_ref=got[a],
                                               send_sem=send_sems.at[a], recv_sem=recv_sems.at[a],
                                               device_id=_sibling(), device_id_type=MESH) for a in range(n)]
        for cp in remote:
            cp.start()
        for cp in remote:
            cp.wait_recv()
        for cp in remote:
            cp.wait_send()

    sem = pltpu.SemaphoreType.DMA((n,))
    return pl.pallas_call(
        body, name=name, in_specs=[ANY] * n, out_specs=[ANY] * n,
        out_shape=[jax.ShapeDtypeStruct((g.shape[0],) + g.shape[2:], g.dtype) for g in split],
        scratch_shapes=[sem, sem])(*split)


def _add_pair(mine, theirs, place, name):
    lead, _, rows, cols = mine.shape
    tm = _ew_tile(rows)

    def body(place_ref, a_ref, b_ref, o_ref):
        o_ref[...] = (a_ref[...].astype(f32) + b_ref[...].astype(f32)).astype(o_ref.dtype)

    blk = pl.BlockSpec((lead, tm, cols), lambda i, pc: (0, i, 0))
    return _prefetch_call(
        body, name=name, grid=(rows // tm,),
        in_specs=[pl.BlockSpec((lead, None, tm, cols), lambda i, pc: (0, pc[1], i, 0)), blk], out_specs=blk,
        out_shape=jax.ShapeDtypeStruct(theirs.shape, theirs.dtype))(place, mine, theirs)


class _NoCargo:
    def __init__(self, results=()):
        self.inputs, self.out_shape, self.sems, self._results = [], [], [], list(results)

    def stages(self, steps):
        return [(0, lambda src, dst, sems: None), (steps - 1, lambda src, dst, sems: None)]

    def results(self, outs):
        return self._results


def _cargo_alone(cargo, name):
    return _cargo_call(None, cargo, name=name, grid=(), in_specs=[], out_specs=[], out_shape=[])()[1]


def _sum_slots(own, recv, place, whole, name):
    _, rows, cols = own.shape
    tm = _ew_tile(rows)

    def body(place_ref, own_ref, r_ref, o_ref):
        r0, r1, r2 = [r_ref[k].astype(f32) for k in range(N_CHIPS - 1)]
        o_ref[...] = (own_ref[...].astype(f32) + r0) + (r1 + r2)

    return _prefetch_call(
        body, name=name, grid=(rows // tm,),
        in_specs=[pl.BlockSpec((None, tm, cols), lambda i, pc: (0 if whole else pc[0], i, 0)),
                  pl.BlockSpec((N_CHIPS - 1, tm, cols), lambda i, pc: (0, i, 0))],
        out_specs=pl.BlockSpec((None, tm, cols), lambda i, pc: (pc[1], i, 0)),
        out_shape=jax.ShapeDtypeStruct((2, rows, cols), f32))(place, own, recv)


def _sibling_fill(bufs):
    n = len(bufs)

    def body(*refs):
        src, dst = refs[:n], refs[n:2 * n]
        send_sems, recv_sems = refs[2 * n:]
        c = lax.axis_index("c")
        remote = [pltpu.make_async_remote_copy(src_ref=src[a].at[c], dst_ref=dst[a].at[c], send_sem=send_sems.at[a],
                                               recv_sem=recv_sems.at[a], device_id=_sibling(), device_id_type=MESH)
                  for a in range(n)]
        for cp in remote:
            cp.start()
        for a in range(n):
            pltpu.make_async_remote_copy(src_ref=src[a].at[c], dst_ref=dst[a].at[1 - c], send_sem=send_sems.at[a],
                                         recv_sem=recv_sems.at[a], device_id=_sibling(), device_id_type=MESH).wait_recv()
        for cp in remote:
            cp.wait_send()

    sem = pltpu.SemaphoreType.DMA((n,))
    out = pl.pallas_call(
        body, name="sibling_fill", in_specs=[ANY] * n, out_specs=[ANY] * n,
        out_shape=[jax.ShapeDtypeStruct(b.shape, b.dtype) for b in bufs],
        input_output_aliases={a: a for a in range(n)}, scratch_shapes=[sem, sem])(*bufs)
    return [o.reshape(2 * b.shape[1], b.shape[2]) for o, b in zip(out, bufs)]


def _adamw(w, m, v, grad, name, tm):
    rows, cols = w.shape
    gcols = grad.shape[1]
    if tm == rows:
        gspec = pl.BlockSpec(grad.shape, lambda i: (0, 0))
    else:
        gspec = pl.BlockSpec((tm, gcols), lambda i: (i, 0))

    def body(w_ref, m_ref, v_ref, gin_ref, g_ref, d_ref, nm_ref, nv_ref):
        g = gin_ref[0:tm, 0:cols]
        w_, m_, v_ = w_ref[...], m_ref[...], v_ref[...]
        m_ = ADAM_B1 * m_ + (1.0 - ADAM_B1) * g
        v_ = ADAM_B2 * v_ + (1.0 - ADAM_B2) * jnp.square(g)
        m_hat = m_ / (1.0 - ADAM_B1 ** ADAM_STEP)
        v_hat = v_ / (1.0 - ADAM_B2 ** ADAM_STEP)
        g_ref[...] = g
        d_ref[...] = -ADAM_LR * (m_hat / (jnp.sqrt(v_hat) + ADAM_EPS) + ADAM_WD * w_)
        nm_ref[...] = m_
        nv_ref[...] = v_

    blk = pl.BlockSpec((tm, cols), lambda i: (i, 0))
    return _call(
        body, name=name, grid=(rows // tm,),
        in_specs=[blk, blk, blk, gspec], out_specs=[blk] * 4,
        out_shape=[jax.ShapeDtypeStruct((rows, cols), f32)] * 4)(w, m, v, grad)


def _rope_tables(s):
    half = HEAD_DIM // 2
    inv_freq = 1.0 / (10000.0 ** (jnp.arange(half, dtype=f32) / half))
    ang = jnp.arange(s, dtype=f32)[:, None] * inv_freq[None, :]
    cos, sin = jnp.cos(ang), jnp.sin(ang)
    return jnp.tile(cos, (1, LANES // half)), jnp.tile(jnp.concatenate([-sin, sin], axis=1), (1, LANES // HEAD_DIM))


def _local_step(x, target, win_g, gather, exchange, b_gate, w_pool, pool_scale, ln1_g, ln1_b, convb_g, ln2_g, ln2_b):
    s = x.shape[0]
    nb = s // MOBA_BLOCK
    cos_t, sin_t = _rope_tables(s)
    x_bf = x.astype(bf16)

    z, (wba_g, wbp_g, wout_g, gate_tail) = _inproj(x_bf, win_g, cos_t, sin_t, b_gate, gather["inproj"])
    wba = wba_g.transpose(1, 0, 2).reshape(D_ATTN, D_MODEL)
    wbp = wbp_g.transpose(1, 0, 2).reshape(D_POOL, D_MODEL)
    wout = wout_g.reshape(D_MODEL, D_MODEL)
    (o_bf, lse), (gate_head, wu_g, convw_g) = _attn_fwd(z, nb, gather["attn_fwd"])
    wg_g = jnp.concatenate([gate_head, gate_tail], axis=1)
    pooled_bf, mixed, pm_bf = _pool_fwd(z, w_pool, pool_scale)
    m_bf = _branch_merge(o_bf, pm_bf, z, wba, wbp)
    x1_bf, xhat1, rstd1 = _out_ln1(m_bf, wout, x, ln1_g, ln1_b)
    (a, up, ac, hh_bf), (wd_g,) = _ffn_up(x1_bf, wg_g, wu_g, convw_g, convb_g, gather["ffn_up"])
    wd = wd_g.reshape(D_FF_PAD, D_MODEL)
    loss, dh2, dh2_bf, d_ln2_g, d_ln2_b = _ffn_down_ln2_loss(hh_bf, wd, xhat1, ln1_g, ln1_b, ln2_g, ln2_b, target)

    da_bf, dup_bf, dconv = _ffn_act_bwd(dh2_bf, wd_g, ac, a, up, convw_g)
    d_wd = _matmul_tn(hh_bf, dh2_bf, 1, "dw_ffn_down").reshape(N_CHIPS, FF_PAD, D_MODEL)
    d_wg = _matmul_tn(da_bf, x1_bf, 1, "dw_ffn_gate").reshape(N_CHIPS, FF_PAD, D_MODEL)
    d_wu = _matmul_tn(dup_bf, x1_bf, 1, "dw_ffn_up").reshape(N_CHIPS, FF_PAD, D_MODEL)
    (dh1, dh1_bf, d_ln1_g, d_ln1_b), got_down = _ffn_in_bwd_ln1(
        da_bf, dup_bf, wg_g, wu_g, dh2, xhat1, rstd1, ln1_g, exchange("ffn_in_bwd_ln1", [("w_ffn_down", d_wd)]))
    d_wout = _matmul_tn(m_bf, dh1_bf, 1, "dw_out").reshape(N_CHIPS, D_MODEL // N_CHIPS, D_MODEL)
    dzg_bf, dya_bf, dyp_bf, do_bf, dpm, d_bgate = _merge_bwd(dh1_bf, wout, o_bf, pm_bf, z, wba, wbp)
    d_wba = _matmul_tn(o_bf, dya_bf, N_CHIPS, "dw_branch_attn")
    d_wbp = _matmul_tn(pm_bf, dyp_bf, N_CHIPS, "dw_branch_pool")
    du_bf, d_wpool, d_pscale = _pool_bwd(dpm, mixed, pooled_bf, w_pool, pool_scale)
    small = _pack_small([d_bgate, d_wpool, d_pscale, d_ln1_g, d_ln1_b, d_ln2_g, d_ln2_b, dconv[:, 3, :],
                         loss[0:1, :]])[None]
    (dq_bf, dk_bf, dv_bf), got_ffn = _attn_bwd(
        z, o_bf, lse, do_bf, cos_t, sin_t, nb,
        exchange("attn_bwd", [("w_ffn_gate", d_wg), ("w_ffn_up", d_wu), ("conv_w", dconv), ("small", small)]))
    dz_bf = jnp.concatenate([dq_bf, dk_bf, dv_bf, du_bf, dzg_bf], axis=1)
    d_win, got_branch = _matmul_tn(
        x_bf, dz_bf, N_CHIPS, "dw_in",
        cargo=exchange("dw_in", [("w_out", d_wout), ("w_branch_attn", d_wba), ("w_branch_pool", d_wbp)]))
    (grad_x,), got_in = _inproj_dx(dz_bf, win_g, dh1, exchange("inproj_dx", [("w_in", d_win)]))
    return grad_x, got_down + got_ffn + got_branch + got_in


SMALL_ROWS = 592


def _pack_small(parts):
    flat = jnp.concatenate([p.reshape(-1) for p in parts])
    return jnp.pad(flat, (0, SMALL_ROWS * LANES - flat.shape[0])).reshape(SMALL_ROWS, LANES)


def _unpack_small(packed, shapes):
    flat = packed.reshape(-1)
    out, off = [], 0
    for shp in shapes:
        n = 1
        for d in shp:
            n *= d
        out.append(flat[off:off + n].reshape(shp))
        off += n
    return out


def _pad_conv_b(cb):
    return jnp.pad(cb.reshape(N_CHIPS, FF_SHARD), ((0, 0), (0, FF_PAD - FF_SHARD)))


def kernel(x, w_in, b_gate, w_branch_attn, w_pool, pool_scale, w_branch_pool, w_out, ln1_g, ln1_b, w_ffn_gate, w_ffn_up, conv_w, conv_b, w_ffn_down, ln2_g, ln2_b, loss_target, m_w_in, m_b_gate, m_w_branch_attn, m_w_pool, m_pool_scale, m_w_branch_pool, m_w_out, m_ln1_g, m_ln1_b, m_w_ffn_gate, m_w_ffn_up, m_conv_w, m_conv_b, m_w_ffn_down, m_ln2_g, m_ln2_b, v_w_in, v_b_gate, v_w_branch_attn, v_w_pool, v_pool_scale, v_w_branch_pool, v_w_out, v_ln1_g, v_ln1_b, v_w_ffn_gate, v_w_ffn_up, v_conv_w, v_conv_b, v_w_ffn_down, v_ln2_g, v_ln2_b):
    pad_rows = lambda w: jnp.pad(w.astype(bf16), ((0, FF_PAD - FF_SHARD), (0, 0)))
    shards = [w_in[0].astype(bf16), w_branch_attn[0].astype(bf16), w_branch_pool[0].astype(bf16),
              w_out[0].astype(bf16), pad_rows(w_ffn_gate[0].T), pad_rows(w_ffn_up[0].T), pad_rows(w_ffn_down[0]),
              jnp.pad(conv_w[0], ((0, CONV_ROWS - 3), (0, FF_PAD - FF_SHARD)))]
    (win_g,) = _cargo_alone(_GatherCargo(shards[:1], [0.0]), "gather_w_in")
    gate_head, gate_tail = shards[4][:GATE_SPLIT], shards[4][GATE_SPLIT:]
    gather = {"inproj": _GatherCargo(shards[1:4] + [gate_tail], [0.25, 0.35, 0.55, 0.9]),
              "attn_fwd": _GatherCargo([gate_head, shards[5], shards[7]], [0.4, 0.9, 0.92]),
              "ffn_up": _GatherCargo(shards[6:7], [0.8])}

    place = jnp.stack([2 * lax.axis_index("x") + lax.axis_index("y"), lax.axis_index("c")]).astype(jnp.int32)
    chip_sums = {}

    def exchange(carrier, named):
        split = [_split_halves(g) for _, g in named]
        got = _presum_swap(split, "presum_swap_" + carrier)
        sums = [_add_pair(a, b, place, "presum_" + nm) for a, b, (nm, _) in zip(split, got, named)]
        chip_sums.update({nm: s for (nm, _), s in zip(named, sums)})
        return _ExchangeCargo(sums, [nm == "small" for nm, _ in named])

    convb_g = _pad_conv_b(conv_b).reshape(N_CHIPS, 1, FF_PAD)
    grad_x, brought = _local_step(
        x[0], loss_target[0], win_g, gather, exchange, b_gate, w_pool[0], pool_scale, ln1_g, ln1_b, convb_g, ln2_g, ln2_b)
    slots = dict(zip(chip_sums, brought))
    names = ["w_in", "w_branch_attn", "w_branch_pool", "w_out", "w_ffn_gate", "w_ffn_up", "w_ffn_down", "conv_w", "small"]
    grads = _sibling_fill([_sum_slots(chip_sums[nm], slots[nm], place, nm == "small", "sum_" + nm) for nm in names])

    weights = [w_in[0], w_branch_attn[0], w_branch_pool[0], w_out[0], w_ffn_gate[0].T, w_ffn_up[0].T, w_ffn_down[0], conv_w[0]]
    m_in = [m_w_in[0], m_w_branch_attn[0], m_w_branch_pool[0], m_w_out[0], m_w_ffn_gate[0].T, m_w_ffn_up[0].T, m_w_ffn_down[0], m_conv_w[0]]
    v_in = [v_w_in[0], v_w_branch_attn[0], v_w_branch_pool[0], v_w_out[0], v_w_ffn_gate[0].T, v_w_ffn_up[0].T, v_w_ffn_down[0], v_conv_w[0]]
    tiles = [256, 256, 256, 128, 88, 88, 88, 3]
    res = {}
    for i, nm in enumerate(names[:-1]):
        outs4 = _adamw(weights[i], m_in[i], v_in[i], grads[i], "adamw_" + nm, tiles[i])
        res[nm] = [(r.T if nm in ("w_ffn_gate", "w_ffn_up") else r)[None] for r in outs4]
    loss = grads[-1][SMALL_ROWS - 4, 0]

    small_names = ["b_gate", "w_pool", "pool_scale", "ln1_g", "ln1_b", "ln2_g", "ln2_b", "conv_b"]
    small_w = [b_gate, w_pool, pool_scale, ln1_g, ln1_b, ln2_g, ln2_b, _pad_conv_b(conv_b)]
    small_m = [m_b_gate, m_w_pool, m_pool_scale, m_ln1_g, m_ln1_b, m_ln2_g, m_ln2_b, _pad_conv_b(m_conv_b)]
    small_v = [v_b_gate, v_w_pool, v_pool_scale, v_ln1_g, v_ln1_b, v_ln2_g, v_ln2_b, _pad_conv_b(v_conv_b)]
    packed = _adamw(_pack_small(small_w), _pack_small(small_m), _pack_small(small_v), grads[-1],
                    "adamw_small", SMALL_ROWS)
    shapes = [w.shape for w in small_w]
    for kind in range(4):
        for nm, val in zip(small_names, _unpack_small(packed[kind], shapes)):
            if nm == "conv_b":
                val = val[:, :FF_SHARD].reshape(1, 4 * FF_SHARD)
            res.setdefault(nm, [None] * 4)[kind] = val

    order = ["w_in", "b_gate", "w_branch_attn", "w_pool", "pool_scale", "w_branch_pool", "w_out", "ln1_g", "ln1_b",
             "w_ffn_gate", "w_ffn_up", "conv_w", "conv_b", "w_ffn_down", "ln2_g", "ln2_b"]
    outs = [loss, grad_x[None]]
    for kind in range(4):
        outs += [res[nm][kind] for nm in order]
    return tuple(outs)
```

```python
import functools

import jax
import jax.numpy as jnp
from jax import lax
from jax.experimental import pallas as pl
from jax.experimental.pallas import tpu as pltpu

D_MODEL = 1024
HEAD_DIM = 64
D_ATTN = 512
D_POOL = 512
MOBA_BLOCK = 256
MOBA_TOPK = 3
POOL_GROUP = 128
MAX_WINDOW = 16
FF_SHARD = 704
FF_PAD = 768
D_FF_PAD = 4 * FF_PAD
N_CHIPS = 4
LANES = 128
ALPHA = (2.0 * 1) ** 0.25
LN_EPS = 1e-5
NEG = -1e30
SCALE = HEAD_DIM ** -0.5
ADAM_LR, ADAM_B1, ADAM_B2, ADAM_EPS, ADAM_WD, ADAM_STEP = 0.001, 0.9, 0.999, 1e-08, 0.01, 10
VMEM_LIMIT = 56 * 1024 * 1024
MESH = pl.DeviceIdType.MESH

bf16 = jnp.bfloat16
f32 = jnp.float32


def _dot(a, b):
    return jnp.dot(a, b, preferred_element_type=f32)


def _dot_nt(a, b):
    return lax.dot_general(a, b, (((1,), (1,)), ((), ())), preferred_element_type=f32)


def _dot_tn(a, b):
    return lax.dot_general(a, b, (((0,), (0,)), ((), ())), preferred_element_type=f32)


def _call(body, *, name, grid, in_specs, out_specs, out_shape, scratch=(), cargo=None):
    if cargo is not None:
        return _cargo_call(body, cargo, name=name, grid=grid, in_specs=in_specs, out_specs=out_specs,
                           out_shape=out_shape, scratch=scratch)
    return pl.pallas_call(
        body, name=name, grid=grid, in_specs=in_specs, out_specs=out_specs, out_shape=out_shape,
        scratch_shapes=list(scratch),
        compiler_params=pltpu.CompilerParams(dimension_semantics=("arbitrary",) * len(grid),
                                             vmem_limit_bytes=VMEM_LIMIT))


def _rot_half(t):
    lane = lax.broadcasted_iota(jnp.int32, t.shape, 1)
    first = (lane % HEAD_DIM) < (HEAD_DIM // 2)
    return jnp.where(first, pltpu.roll(t, LANES - HEAD_DIM // 2, 1), pltpu.roll(t, HEAD_DIM // 2, 1))


def _rope(t, cos, sin_signed):
    return t * cos + _rot_half(t) * sin_signed


def _rope_bwd(d, cos, sin_signed):
    return d * cos + _rot_half(d * sin_signed)


def _gelu_parts(a):
    cdf = 0.5 * (1.0 + lax.erf(a * (2.0 ** -0.5)))
    pdf = jnp.exp(-0.5 * a * a) * ((2.0 * jnp.pi) ** -0.5)
    return a * cdf, cdf + a * pdf


def _layer_norm(h, g, b):
    mu = jnp.mean(h, axis=-1, keepdims=True)
    xc = h - mu
    var = jnp.mean(xc * xc, axis=-1, keepdims=True)
    rstd = lax.rsqrt(var + LN_EPS)
    xhat = xc * rstd
    return xhat * g + b, xhat, rstd


def _layer_norm_bwd(dy, xhat, rstd, g):
    dxh = dy * g
    m1 = jnp.mean(dxh, axis=-1, keepdims=True)
    m2 = jnp.mean(dxh * xhat, axis=-1, keepdims=True)
    return rstd * (dxh - m1 - xhat * m2)


def _inproj(x_bf, win_cargo, cos_t, sin_t, b_gate, place, cargo):
    s = x_bf.shape[0]
    tm, tn = 1024, 512
    halves = (2, D_MODEL // 2, tn)

    def body(place_ref, x_ref, cos_ref, sin_ref, b_ref, o_ref, w_tile, w_sem, cargo_refs):
        order, t, i = pl.program_id(0), pl.program_id(1), pl.program_id(2)
        shard = place_ref[0] ^ order
        col = 2 * shard + t
        mine, everyones = win_cargo.weights(cargo_refs[0], cargo_refs[1])

        @pl.when(i == 0)
        def _():
            for tt in range(D_MODEL // tn):
                cols = slice(tt * tn, (tt + 1) * tn)

                def fetch(src):
                    cp = pltpu.make_async_copy(src, w_tile, w_sem)
                    cp.start()
                    cp.wait()

                pl.when((t == tt) & (order == 0))(lambda: fetch(mine.at[:, :, cols]))
                pl.when((t == tt) & (order > 0))(lambda: fetch(everyones.at[shard, :, :, cols]))

        acc = _dot(x_ref[...], w_tile[...].reshape(D_MODEL, tn))

        @pl.when(col < 2)
        def _():
            for c in range(tn // LANES):
                sl = slice(c * LANES, (c + 1) * LANES)
                o_ref[:, sl] = _rope(acc[:, sl], cos_ref[...], sin_ref[...])

        @pl.when((col >= 2) & (col < 4))
        def _():
            o_ref[...] = acc

        @pl.when(col >= 4)
        def _():
            o_ref[...] = jax.nn.sigmoid(acc + b_ref[...])

    column = lambda order, t, pc: 2 * (pc[0] ^ order) + t
    (z,), (win_g, carried) = _cargo_call(
        body, _MultiCargo([win_cargo, cargo]), name="inproj", grid=(N_CHIPS, D_MODEL // tn, s // tm),
        in_specs=[pl.BlockSpec((tm, D_MODEL), lambda o, t, i, pc: (i, 0)),
                  pl.BlockSpec((tm, LANES), lambda o, t, i, pc: (i, 0)),
                  pl.BlockSpec((tm, LANES), lambda o, t, i, pc: (i, 0)),
                  pl.BlockSpec((1, tn), lambda o, t, i, pc: (0, jnp.maximum(column(o, t, pc) - 4, 0)))],
        out_specs=[pl.BlockSpec((tm, tn), lambda o, t, i, pc: (i, column(o, t, pc)))],
        out_shape=[jax.ShapeDtypeStruct((s, 4 * D_MODEL), f32)],
        scratch=[pltpu.VMEM(halves, bf16), pltpu.SemaphoreType.DMA], prefetch=place, body_sees_cargo=True,
    )(x_bf, cos_t, sin_t, b_gate)
    return z, win_g[0], carried


STAT_ROWS = 8


def _pair_rows(v0, v1, fill):
    r = lax.broadcasted_iota(jnp.int32, (STAT_ROWS, v0.shape[1]), 0)
    return jnp.where(r == 0, v0, jnp.where(r == 1, v1, fill))


def _head_lanes(shape):
    lane = lax.broadcasted_iota(jnp.int32, shape, 1)
    return lane < HEAD_DIM, lane >= HEAD_DIM


def _head_rows(shape):
    row = lax.broadcasted_iota(jnp.int32, shape, 0)
    return row < HEAD_DIM, row >= HEAD_DIM


def _moba_select(q_bf, k_all, i, nb):
    k_mean = jnp.mean(k_all.reshape(nb, MOBA_BLOCK, LANES), axis=1)
    n_io = lax.broadcasted_iota(jnp.int32, (nb, MOBA_BLOCK), 0)
    past = n_io < i
    sels = []
    for head in _head_lanes((nb, LANES)):
        gate = _dot_nt(jnp.where(head, k_mean, 0.0).astype(bf16), q_bf)
        g = jnp.where(past, gate, NEG)
        rank = jnp.zeros((nb, MOBA_BLOCK), f32)
        for m in range(nb):
            gm = g[m:m + 1, :]
            rank = rank + jnp.where((gm > g) | ((gm == g) & (m < n_io)), 1.0, 0.0)
        sels.append(jnp.where(past & (rank < MOBA_TOPK), 1.0, 0.0))
    return sels


def _attn_fwd(z, nb, cargo):
    s = z.shape[0]

    def body(q_ref, k_ref, v_ref, o_ref, lse_ref, vt_s):
        i = pl.program_id(1)

        @pl.when(i == 0)
        def _():
            for j in range(nb):
                vt_s[j] = v_ref[j * MOBA_BLOCK:(j + 1) * MOBA_BLOCK, :].T.astype(bf16)

        k_heads = _head_lanes((MOBA_BLOCK, LANES))
        o_heads = _head_rows((LANES, MOBA_BLOCK))

        def block(j, qs, allow, m_old, l_old):
            kj, vtj = k_ref[j * MOBA_BLOCK:(j + 1) * MOBA_BLOCK, :], vt_s[j]
            m_new, alpha, l_new, pv = [], [], [], None
            for h in range(2):
                sc = jnp.where(allow[h], _dot_nt(jnp.where(k_heads[h], kj, 0.0).astype(bf16), qs), NEG)
                mx = jnp.max(sc, axis=0, keepdims=True)
                mn = mx if m_old is None else jnp.maximum(m_old[h], mx)
                p = jnp.exp(sc - mn)
                lsum = jnp.sum(p, axis=0, keepdims=True)
                if m_old is None:
                    a = None
                else:
                    a = jnp.exp(m_old[h] - mn)
                    lsum = a * l_old[h] + lsum
                t = _dot(jnp.where(o_heads[h], vtj, jnp.zeros_like(vtj)), p.astype(bf16))
                pv = t if pv is None else pv + t
                m_new.append(mn)
                alpha.append(a)
                l_new.append(lsum)
            return m_new, alpha, l_new, pv

        def query_block(c):
            q = q_ref[...]
            sels = _moba_select(q.astype(bf16), k_ref[...], c, nb)
            qs = (q * SCALE).astype(bf16)
            key = lax.broadcasted_iota(jnp.int32, (MOBA_BLOCK, MOBA_BLOCK), 0)
            qry = lax.broadcasted_iota(jnp.int32, (MOBA_BLOCK, MOBA_BLOCK), 1)
            m, _, l, acc = block(c, qs, [key <= qry] * 2, None, None)
            for j in range(c):
                m, alpha, l, pv = block(j, qs, [sels[h][j:j + 1, :] > 0.0 for h in range(2)], m, l)
                acc = acc * jnp.where(o_heads[0], alpha[0], alpha[1]) + pv
            o_ref[...] = (acc / jnp.where(o_heads[0], l[0], l[1])).T.astype(bf16)
            lse_ref[...] = _pair_rows(m[0] + jnp.log(l[0]), m[1] + jnp.log(l[1]), 0.0)

        for c in range(nb):
            pl.when(i == c)(functools.partial(query_block, c))

    return _cargo_call(
        body, cargo, name="attn_fwd", grid=(D_ATTN // LANES, nb),
        in_specs=[pl.BlockSpec((MOBA_BLOCK, LANES), lambda hp, i: (i, hp)),
                  pl.BlockSpec((s, LANES), lambda hp, i: (0, 4 + hp)),
                  pl.BlockSpec((s, LANES), lambda hp, i: (0, 8 + hp))],
        out_specs=[pl.BlockSpec((MOBA_BLOCK, LANES), lambda hp, i: (i, hp)),
                   pl.BlockSpec((None, STAT_ROWS, MOBA_BLOCK), lambda hp, i: (hp, 0, i))],
        out_shape=[jax.ShapeDtypeStruct((s, D_ATTN), bf16),
                   jax.ShapeDtypeStruct((D_ATTN // LANES, STAT_ROWS, s), f32)],
        scratch=[pltpu.VMEM((nb, LANES, MOBA_BLOCK), bf16)])(z, z, z)


def _window_select(snaps, g):
    return jnp.where(g == 0, snaps[0], jnp.where(g == 1, snaps[1], jnp.where(g == 2, snaps[2], snaps[3])))


def _pool_fwd(z, w_pool, pool_scale):
    s = z.shape[0]

    def body(u_ref, w_ref, sc_ref, pooled_ref, mixed_ref, pm_ref, pad):
        g = pl.program_id(0)
        u = u_ref[...]
        pad[0:MAX_WINDOW, :] = jnp.zeros((MAX_WINDOW, POOL_GROUP), f32)
        pad[MAX_WINDOW:MAX_WINDOW + s, :] = u
        acc = u
        snaps = []
        for d in range(1, MAX_WINDOW):
            acc = acc + pad[MAX_WINDOW - d:MAX_WINDOW - d + s, :]
            if d + 1 in (2, 4, 8, 16):
                snaps.append(acc)
        win = _window_select(snaps, g)
        t = lax.broadcasted_iota(jnp.int32, (s, POOL_GROUP), 0)
        count = jnp.minimum(t + 1, jnp.left_shift(2, g)).astype(f32)
        pooled = (win / count - u).astype(bf16)
        mixed = _dot(pooled, w_ref[...].astype(bf16))
        pooled_ref[...] = pooled
        mixed_ref[...] = mixed
        pm_ref[...] = (mixed * sc_ref[...]).astype(bf16)

    blk = pl.BlockSpec((s, POOL_GROUP), lambda g: (0, g))
    return _call(
        body, name="pool_fwd", grid=(4,),
        in_specs=[pl.BlockSpec((s, POOL_GROUP), lambda g: (0, 12 + g)),
                  pl.BlockSpec((None, POOL_GROUP, POOL_GROUP), lambda g: (g, 0, 0)),
                  pl.BlockSpec((1, POOL_GROUP), lambda g: (0, g))],
        out_specs=[blk, blk, blk],
        out_shape=[jax.ShapeDtypeStruct((s, D_POOL), bf16), jax.ShapeDtypeStruct((s, D_POOL), f32),
                   jax.ShapeDtypeStruct((s, D_POOL), bf16)],
        scratch=[pltpu.VMEM((s + MAX_WINDOW, POOL_GROUP), f32)])(z, w_pool, pool_scale)


def _branch_merge(o_bf, pm_bf, z, wba, wbp):
    s = o_bf.shape[0]
    tm = 512

    def body(o_ref, pm_ref, ga_ref, gp_ref, wba_ref, wbp_ref, m_ref):
        ya = _dot(o_ref[...], wba_ref[...])
        yp = _dot(pm_ref[...], wbp_ref[...])
        m_ref[...] = (ga_ref[...] * ya + gp_ref[...] * yp).astype(bf16)

    full = lambda r, c: pl.BlockSpec((r, c), lambda i: (0, 0))
    return _call(
        body, name="branch_merge", grid=(s // tm,),
        in_specs=[pl.BlockSpec((tm, D_ATTN), lambda i: (i, 0)), pl.BlockSpec((tm, D_POOL), lambda i: (i, 0)),
                  pl.BlockSpec((tm, D_MODEL), lambda i: (i, 2)), pl.BlockSpec((tm, D_MODEL), lambda i: (i, 3)),
                  full(D_ATTN, D_MODEL), full(D_POOL, D_MODEL)],
        out_specs=pl.BlockSpec((tm, D_MODEL), lambda i: (i, 0)),
        out_shape=jax.ShapeDtypeStruct((s, D_MODEL), bf16))(o_bf, pm_bf, z, z, wba, wbp)


def _out_ln1(m_bf, wout, x, ln_g, ln_b):
    s = x.shape[0]
    tm = 512

    def body(m_ref, w_ref, x_ref, g_ref, b_ref, x1_ref, xhat_ref, rstd_ref):
        h = ALPHA * x_ref[...] + _dot(m_ref[...], w_ref[...])
        y, xhat, rstd = _layer_norm(h, g_ref[...], b_ref[...])
        x1_ref[...] = y.astype(bf16)
        xhat_ref[...] = xhat
        rstd_ref[...] = jnp.broadcast_to(rstd, (tm, LANES))

    row = pl.BlockSpec((tm, D_MODEL), lambda i: (i, 0))
    vec = pl.BlockSpec((1, D_MODEL), lambda i: (0, 0))
    return _call(
        body, name="out_ln1", grid=(s // tm,),
        in_specs=[row, pl.BlockSpec((D_MODEL, D_MODEL), lambda i: (0, 0)), row, vec, vec],
        out_specs=[row, row, pl.BlockSpec((tm, LANES), lambda i: (i, 0))],
        out_shape=[jax.ShapeDtypeStruct((s, D_MODEL), bf16), jax.ShapeDtypeStruct((s, D_MODEL), f32),
                   jax.ShapeDtypeStruct((s, LANES), f32)])(m_bf, wout, x, ln_g, ln_b)


FF_TILE = 256
FF_TILES_PER_SHARD = FF_PAD // FF_TILE
CONV_PAD = 8
CONV_ROWS = 16
GATE_SPLIT = 512


def _ff_weight_spec(rows):
    return pl.BlockSpec((None, rows, FF_TILE), lambda j: (j // FF_TILES_PER_SHARD, 0, j % FF_TILES_PER_SHARD))


def _ff_rows_spec():
    return pl.BlockSpec((None, FF_TILE, D_MODEL), lambda j: (j // FF_TILES_PER_SHARD, j % FF_TILES_PER_SHARD, 0))


def _row_shifts(x, shifts):
    s = x.shape[0]
    padded = jnp.concatenate([x, jnp.zeros((CONV_PAD, x.shape[1]), x.dtype)], axis=0)
    return [pltpu.roll(padded, d % (s + CONV_PAD), 0)[0:s] for d in shifts]


def _ffn_up(x1_bf, wg_g, wu_g, convw_g, convb_g, cargo):
    s = x1_bf.shape[0]

    def body(x_ref, wg_ref, wu_ref, cw_ref, cb_ref, a_ref, up_ref, ac_ref, hh_ref):
        x1 = x_ref[...]
        a = _dot_nt(x1, wg_ref[...])
        up = _dot_nt(x1, wu_ref[...])
        a_ref[...] = a
        up_ref[...] = up
        a1, a2 = _row_shifts(a, (1, 2))
        ac = a * cw_ref[2:3, :] + a1 * cw_ref[1:2, :] + a2 * cw_ref[0:1, :] + cb_ref[...]
        ac_ref[...] = ac
        hg, _ = _gelu_parts(ac)
        hh_ref[...] = (hg * up).astype(bf16)

    col = pl.BlockSpec((s, FF_TILE), lambda j: (0, j))
    wide = jax.ShapeDtypeStruct((s, D_FF_PAD), f32)
    return _call(
        body, name="ffn_up", grid=(D_FF_PAD // FF_TILE,),
        in_specs=[pl.BlockSpec((s, D_MODEL), lambda j: (0, 0)), _ff_rows_spec(), _ff_rows_spec(),
                  _ff_weight_spec(8), _ff_weight_spec(1)],
        out_specs=[col, col, col, col],
        out_shape=[wide, wide, wide, jax.ShapeDtypeStruct((s, D_FF_PAD), bf16)],
        cargo=cargo)(x1_bf, wg_g, wu_g, convw_g, convb_g)


def _ffn_down_ln2_loss(hh_bf, wd, xhat1, ln1_g, ln1_b, ln2_g, ln2_b, target):
    s = hh_bf.shape[0]
    tm = 256

    def body(hh_ref, w_ref, xh1_ref, g1_ref, b1_ref, g2_ref, b2_ref, t_ref, loss_ref, dh_ref, dhb_ref, dg_ref, db_ref):
        i = pl.program_id(0)
        x1 = xh1_ref[...] * g1_ref[...] + b1_ref[...]
        h = ALPHA * x1 + _dot(hh_ref[...], w_ref[...])
        y, xhat, rstd = _layer_norm(h, g2_ref[...], b2_ref[...])
        err = y - t_ref[...]
        part = 0.5 * jnp.sum(jnp.mean(err * err, axis=-1, keepdims=True), axis=0, keepdims=True)
        dy = err * (1.0 / D_MODEL)

        @pl.when(i == 0)
        def _():
            loss_ref[...] = jnp.zeros_like(loss_ref)
            dg_ref[...] = jnp.zeros_like(dg_ref)
            db_ref[...] = jnp.zeros_like(db_ref)

        loss_ref[...] += jnp.broadcast_to(part, loss_ref.shape)
        dg_ref[...] += jnp.sum(dy * xhat, axis=0, keepdims=True)
        db_ref[...] += jnp.sum(dy, axis=0, keepdims=True)
        dh = _layer_norm_bwd(dy, xhat, rstd, g2_ref[...])
        dh_ref[...] = dh
        dhb_ref[...] = dh.astype(bf16)

    row = pl.BlockSpec((tm, D_MODEL), lambda i: (i, 0))
    vec = pl.BlockSpec((1, D_MODEL), lambda i: (0, 0))
    return _call(
        body, name="ffn_down_ln2_loss", grid=(s // tm,),
        in_specs=[pl.BlockSpec((tm, D_FF_PAD), lambda i: (i, 0)),
                  pl.BlockSpec((D_FF_PAD, D_MODEL), lambda i: (0, 0)),
                  row, vec, vec, vec, vec, row],
        out_specs=[pl.BlockSpec((8, LANES), lambda i: (0, 0)), row, row, vec, vec],
        out_shape=[jax.ShapeDtypeStruct((8, LANES), f32), jax.ShapeDtypeStruct((s, D_MODEL), f32),
                   jax.ShapeDtypeStruct((s, D_MODEL), bf16), jax.ShapeDtypeStruct((1, D_MODEL), f32),
                   jax.ShapeDtypeStruct((1, D_MODEL), f32)])(hh_bf, wd, xhat1, ln1_g, ln1_b, ln2_g, ln2_b, target)


def _ffn_act_bwd(dh2_bf, wd_g, ac, a, up, convw_g):
    s = dh2_bf.shape[0]

    def body(dh_ref, wd_ref, ac_ref, a_ref, up_ref, cw_ref, da_ref, dup_ref, dc_ref):
        dhh = _dot_nt(dh_ref[...], wd_ref[...])
        hg, dgelu = _gelu_parts(ac_ref[...])
        dup_ref[...] = (dhh * hg).astype(bf16)
        dac = dhh * up_ref[...] * dgelu
        dac1, dac2 = _row_shifts(dac, (-1, -2))
        a = a_ref[...]
        rows = [jnp.sum(d * a, axis=0, keepdims=True) for d in (dac2, dac1, dac)]
        rows.append(jnp.sum(dac, axis=0, keepdims=True))
        rows.append(jnp.zeros((CONV_ROWS - 4, FF_TILE), f32))
        dc_ref[...] = jnp.concatenate(rows, axis=0)
        da_ref[...] = (dac * cw_ref[2:3, :] + dac1 * cw_ref[1:2, :] + dac2 * cw_ref[0:1, :]).astype(bf16)

    col = pl.BlockSpec((s, FF_TILE), lambda j: (0, j))
    return _call(
        body, name="ffn_act_bwd", grid=(D_FF_PAD // FF_TILE,),
        in_specs=[pl.BlockSpec((s, D_MODEL), lambda j: (0, 0)), _ff_rows_spec(), col, col, col, _ff_weight_spec(8)],
        out_specs=[col, col, _ff_weight_spec(CONV_ROWS)],
        out_shape=[jax.ShapeDtypeStruct((s, D_FF_PAD), bf16), jax.ShapeDtypeStruct((s, D_FF_PAD), bf16),
                   jax.ShapeDtypeStruct((N_CHIPS, CONV_ROWS, FF_PAD), f32)],
    )(dh2_bf, wd_g, ac, a, up, convw_g)


def _matmul_tn(a, b, n_shards, name, tm=512, tn=1024, cargo=None):
    k, m = a.shape
    n = b.shape[1]
    tm, tn = min(tm, m), min(tn, n // n_shards)
    per = n // n_shards // tn

    def body(a_ref, b_ref, o_ref, at_s):
        @pl.when(pl.program_id(1) == 0)
        def _():
            at_s[...] = a_ref[...].T

        o_ref[...] = _dot(at_s[...], b_ref[...]).astype(bf16)

    res = _call(
        body, name=name, grid=(m // tm, n // tn),
        in_specs=[pl.BlockSpec((k, tm), lambda i, j: (0, i)), pl.BlockSpec((k, tn), lambda i, j: (0, j))],
        out_specs=[pl.BlockSpec((None, tm, tn), lambda i, j: (j // per, i, j % per))],
        out_shape=[jax.ShapeDtypeStruct((n_shards, m, n // n_shards), bf16)], scratch=[pltpu.VMEM((tm, k), bf16)],
        cargo=cargo)(a, b)
    return res[0] if cargo is None else (res[0][0], res[1])


def _ffn_in_bwd_ln1(da_bf, dup_bf, wg_g, wu_g, dh2, xhat1, rstd1, ln1_g, cargo):
    s = da_bf.shape[0]
    tm = 256

    def body(da_ref, dup_ref, wg_ref, wu_ref, dh2_ref, xh_ref, rstd_ref, g_ref, dh_ref, dhb_ref, dg_ref, db_ref):
        i = pl.program_id(0)
        dx1 = ALPHA * dh2_ref[...]
        for sh in range(N_CHIPS):
            sl = slice(sh * FF_PAD, (sh + 1) * FF_PAD)
            dx1 = dx1 + _dot(da_ref[:, sl], wg_ref[sh]) + _dot(dup_ref[:, sl], wu_ref[sh])
        xhat = xh_ref[...]

        @pl.when(i == 0)
        def _():
            dg_ref[...] = jnp.zeros_like(dg_ref)
            db_ref[...] = jnp.zeros_like(db_ref)

        dg_ref[...] += jnp.sum(dx1 * xhat, axis=0, keepdims=True)
        db_ref[...] += jnp.sum(dx1, axis=0, keepdims=True)
        dh = _layer_norm_bwd(dx1, xhat, rstd_ref[:, 0:1], g_ref[...])
        dh_ref[...] = dh
        dhb_ref[...] = dh.astype(bf16)

    row = pl.BlockSpec((tm, D_MODEL), lambda i: (i, 0))
    wide = pl.BlockSpec((tm, D_FF_PAD), lambda i: (i, 0))
    wfull = pl.BlockSpec((N_CHIPS, FF_PAD, D_MODEL), lambda i: (0, 0, 0))
    vec = pl.BlockSpec((1, D_MODEL), lambda i: (0, 0))
    return _call(
        body, name="ffn_in_bwd_ln1", grid=(s // tm,),
        in_specs=[wide, wide, wfull, wfull, row, row, pl.BlockSpec((tm, LANES), lambda i: (i, 0)), vec],
        out_specs=[row, row, vec, vec],
        out_shape=[jax.ShapeDtypeStruct((s, D_MODEL), f32), jax.ShapeDtypeStruct((s, D_MODEL), bf16),
                   jax.ShapeDtypeStruct((1, D_MODEL), f32), jax.ShapeDtypeStruct((1, D_MODEL), f32)],
        cargo=cargo)(da_bf, dup_bf, wg_g, wu_g, dh2, xhat1, rstd1, ln1_g)


def _merge_bwd(dh1_bf, wout, o_bf, pm_bf, z, wba, wbp):
    s = dh1_bf.shape[0]
    tm = 256

    def body(dh_ref, wout_ref, o_ref, pm_ref, ga_ref, gp_ref, wba_ref, wbp_ref,
             dzg_ref, dya_ref, dyp_ref, do_ref, dpm_ref, dbg_ref):
        i = pl.program_id(0)
        dm = _dot_nt(dh_ref[...], wout_ref[...])
        ya = _dot(o_ref[...], wba_ref[...])
        yp = _dot(pm_ref[...], wbp_ref[...])
        ga, gp = ga_ref[...], gp_ref[...]
        dza = dm * ya * ga * (1.0 - ga)
        dzp = dm * yp * gp * (1.0 - gp)

        @pl.when(i == 0)
        def _():
            dbg_ref[...] = jnp.zeros_like(dbg_ref)

        dbg_ref[:, 0:D_MODEL] += jnp.sum(dza, axis=0, keepdims=True)
        dbg_ref[:, D_MODEL:2 * D_MODEL] += jnp.sum(dzp, axis=0, keepdims=True)
        dzg_ref[:, 0:D_MODEL] = dza.astype(bf16)
        dzg_ref[:, D_MODEL:2 * D_MODEL] = dzp.astype(bf16)
        dya = (dm * ga).astype(bf16)
        dyp = (dm * gp).astype(bf16)
        dya_ref[...] = dya
        dyp_ref[...] = dyp
        do_ref[...] = _dot_nt(dya, wba_ref[...]).astype(bf16)
        dpm_ref[...] = _dot_nt(dyp, wbp_ref[...])

    row = pl.BlockSpec((tm, D_MODEL), lambda i: (i, 0))
    half = pl.BlockSpec((tm, D_ATTN), lambda i: (i, 0))
    full = lambda r, c: pl.BlockSpec((r, c), lambda i: (0, 0))
    return _call(
        body, name="merge_bwd", grid=(s // tm,),
        in_specs=[row, full(D_MODEL, D_MODEL), half, half,
                  pl.BlockSpec((tm, D_MODEL), lambda i: (i, 2)), pl.BlockSpec((tm, D_MODEL), lambda i: (i, 3)),
                  full(D_ATTN, D_MODEL), full(D_POOL, D_MODEL)],
        out_specs=[pl.BlockSpec((tm, 2 * D_MODEL), lambda i: (i, 0)), row, row, half, half,
                   pl.BlockSpec((1, 2 * D_MODEL), lambda i: (0, 0))],
        out_shape=[jax.ShapeDtypeStruct((s, 2 * D_MODEL), bf16), jax.ShapeDtypeStruct((s, D_MODEL), bf16),
                   jax.ShapeDtypeStruct((s, D_MODEL), bf16), jax.ShapeDtypeStruct((s, D_ATTN), bf16),
                   jax.ShapeDtypeStruct((s, D_POOL), f32), jax.ShapeDtypeStruct((1, 2 * D_MODEL), f32)],
    )(dh1_bf, wout, o_bf, pm_bf, z, z, wba, wbp)


def _pool_bwd(dpm, mixed, pooled_bf, w_pool, pool_scale):
    s = dpm.shape[0]

    def body(dpm_ref, mixed_ref, pooled_ref, w_ref, sc_ref, du_ref, dw_ref, dsc_ref, pad):
        g = pl.program_id(0)
        dpm_v = dpm_ref[...]
        dsc_ref[...] = jnp.sum(dpm_v * mixed_ref[...], axis=0, keepdims=True)
        dmixed = (dpm_v * sc_ref[...]).astype(bf16)
        dw_ref[...] = _dot_tn(pooled_ref[...], dmixed)
        dpooled = _dot_nt(dmixed, w_ref[...].astype(bf16))
        t = lax.broadcasted_iota(jnp.int32, (s, POOL_GROUP), 0)
        count = jnp.minimum(t + 1, jnp.left_shift(2, g)).astype(f32)
        r = dpooled / count
        pad[0:s, :] = r
        pad[s:s + MAX_WINDOW, :] = jnp.zeros((MAX_WINDOW, POOL_GROUP), f32)
        acc = r
        snaps = []
        for d in range(1, MAX_WINDOW):
            acc = acc + pad[d:d + s, :]
            if d + 1 in (2, 4, 8, 16):
                snaps.append(acc)
        du_ref[...] = (_window_select(snaps, g) - dpooled).astype(bf16)

    blk = pl.BlockSpec((s, POOL_GROUP), lambda g: (0, g))
    return _call(
        body, name="pool_bwd", grid=(4,),
        in_specs=[blk, blk, blk, pl.BlockSpec((None, POOL_GROUP, POOL_GROUP), lambda g: (g, 0, 0)),
                  pl.BlockSpec((1, POOL_GROUP), lambda g: (0, g))],
        out_specs=[blk, pl.BlockSpec((None, POOL_GROUP, POOL_GROUP), lambda g: (g, 0, 0)),
                   pl.BlockSpec((1, POOL_GROUP), lambda g: (0, g))],
        out_shape=[jax.ShapeDtypeStruct((s, D_POOL), bf16), jax.ShapeDtypeStruct((4, POOL_GROUP, POOL_GROUP), f32),
                   jax.ShapeDtypeStruct((1, D_POOL), f32)],
        scratch=[pltpu.VMEM((s + MAX_WINDOW, POOL_GROUP), f32)])(dpm, mixed, pooled_bf, w_pool, pool_scale)


def _attn_bwd(z, o_bf, lse, do_bf, cos_t, sin_t, nb, cargo):
    s = z.shape[0]

    def body(q_ref, k_ref, v_ref, o_ref, lse_ref, do_ref, cq_ref, sq_ref, cf_ref, sf_ref,
             dq_ref, dk_ref, dv_ref, kt_s, dk_s, dv_s):
        i = pl.program_id(1)

        @pl.when(i == 0)
        def _():
            dk_s[...] = jnp.zeros_like(dk_s)
            dv_s[...] = jnp.zeros_like(dv_s)
            for j in range(nb):
                kt_s[j] = k_ref[j * MOBA_BLOCK:(j + 1) * MOBA_BLOCK, :].T.astype(bf16)

        k_heads = _head_lanes((MOBA_BLOCK, LANES))
        t_heads = _head_rows((LANES, MOBA_BLOCK))

        def query_block(c):
            q = q_ref[...]
            sels = _moba_select(q.astype(bf16), k_ref[...], c, nb)
            qs = (q * SCALE).astype(bf16)
            qraw = [jnp.where(hm, q, 0.0).astype(bf16) for hm in k_heads]
            do = do_ref[...].astype(f32)
            do_bf = do_ref[...]
            dob = [jnp.where(hm, do, 0.0).astype(bf16) for hm in k_heads]
            pick = _head_lanes((STAT_ROWS, LANES))
            r8 = lax.broadcasted_iota(jnp.int32, (STAT_ROWS, LANES), 0)
            head_pick = jnp.where(((r8 == 0) & pick[0]) | ((r8 == 1) & pick[1]), 1.0, 0.0)
            delta8 = lax.dot_general(head_pick, do * o_ref[...].astype(f32), (((1,), (1,)), ((), ())),
                                     precision=lax.Precision.HIGHEST, preferred_element_type=f32)
            lse8 = lse_ref[...]
            delta, lse_h = [delta8[0:1, :], delta8[1:2, :]], [lse8[0:1, :], lse8[1:2, :]]
            key = lax.broadcasted_iota(jnp.int32, (MOBA_BLOCK, MOBA_BLOCK), 0)
            qry = lax.broadcasted_iota(jnp.int32, (MOBA_BLOCK, MOBA_BLOCK), 1)

            def block(j, allow):
                rows = slice(j * MOBA_BLOCK, (j + 1) * MOBA_BLOCK)
                kj, vj, ktj = k_ref[rows, :], v_ref[rows, :], kt_s[j]
                dqt, dk, dv = None, None, None
                for h in range(2):
                    sc = _dot_nt(jnp.where(k_heads[h], kj, 0.0).astype(bf16), qs)
                    p = jnp.where(allow[h], jnp.exp(sc - lse_h[h]), 0.0)
                    dp = _dot_nt(jnp.where(k_heads[h], vj, 0.0).astype(bf16), do_bf)
                    ds = (p * (dp - delta[h]) * SCALE).astype(bf16)
                    tv = _dot(p.astype(bf16), dob[h])
                    tk = _dot(ds, qraw[h])
                    tq = _dot(jnp.where(t_heads[h], ktj, jnp.zeros_like(ktj)), ds)
                    dqt, dk, dv = (tq, tk, tv) if dqt is None else (dqt + tq, dk + tk, dv + tv)
                dk_s[rows, :] += dk
                dv_s[rows, :] += dv
                return dqt

            dqt = block(c, [key <= qry] * 2)
            for j in range(c):
                dqt = dqt + block(j, [sels[h][j:j + 1, :] > 0.0 for h in range(2)])
            dq_ref[...] = _rope_bwd(dqt.T, cq_ref[...], sq_ref[...]).astype(bf16)

        for c in range(nb):
            pl.when(i == c)(functools.partial(query_block, c))

        @pl.when(i == nb - 1)
        def _():
            dk_ref[...] = _rope_bwd(dk_s[...], cf_ref[...], sf_ref[...]).astype(bf16)
            dv_ref[...] = dv_s[...].astype(bf16)

    qblk = pl.BlockSpec((MOBA_BLOCK, LANES), lambda hp, i: (i, hp))
    tq = pl.BlockSpec((MOBA_BLOCK, LANES), lambda hp, i: (i, 0))
    tf = pl.BlockSpec((s, LANES), lambda hp, i: (0, 0))
    colblk = pl.BlockSpec((s, LANES), lambda hp, i: (0, hp))
    return _cargo_call(
        body, cargo, name="attn_bwd", grid=(D_ATTN // LANES, nb),
        in_specs=[qblk, pl.BlockSpec((s, LANES), lambda hp, i: (0, 4 + hp)),
                  pl.BlockSpec((s, LANES), lambda hp, i: (0, 8 + hp)), qblk,
                  pl.BlockSpec((None, STAT_ROWS, MOBA_BLOCK), lambda hp, i: (hp, 0, i)), qblk, tq, tq, tf, tf],
        out_specs=[qblk, colblk, colblk],
        out_shape=[jax.ShapeDtypeStruct((s, D_ATTN), bf16)] * 3,
        scratch=[pltpu.VMEM((nb, LANES, MOBA_BLOCK), bf16), pltpu.VMEM((s, LANES), f32), pltpu.VMEM((s, LANES), f32)],
    )(z, z, z, o_bf, lse, do_bf, cos_t, sin_t, cos_t, sin_t)


def _inproj_dx(dz_bf, win_g, dh1, cargo):
    s = dz_bf.shape[0]
    tm = 256

    def body(dz_ref, w_ref, dh_ref, gx_ref):
        acc = ALPHA * dh_ref[...]
        for sh in range(N_CHIPS):
            acc = acc + _dot_nt(dz_ref[:, sh * D_MODEL:(sh + 1) * D_MODEL], w_ref[sh])
        gx_ref[...] = acc

    row = pl.BlockSpec((tm, D_MODEL), lambda i: (i, 0))
    return _call(
        body, name="inproj_dx", grid=(s // tm,),
        in_specs=[pl.BlockSpec((tm, 4 * D_MODEL), lambda i: (i, 0)),
                  pl.BlockSpec((N_CHIPS, D_MODEL, D_MODEL), lambda i: (0, 0, 0)), row],
        out_specs=[row], out_shape=[jax.ShapeDtypeStruct((s, D_MODEL), f32)], cargo=cargo)(dz_bf, win_g, dh1)


ANY = pl.BlockSpec(memory_space=pl.ANY)


def _chip_index():
    return 2 * lax.axis_index("x") + lax.axis_index("y")


def _peer(k):
    x, y, c = lax.axis_index("x"), lax.axis_index("y"), lax.axis_index("c")
    return (x ^ (k >> 1), y ^ (k & 1), c)


def _sibling():
    return (lax.axis_index("x"), lax.axis_index("y"), 1 - lax.axis_index("c"))


class _GatherCargo:
    def __init__(self, shards, pass_on):
        self.shards, self.pass_on, n = shards, pass_on, len(shards)
        self.inputs = [a.reshape(2, a.shape[0] // 2, a.shape[1]) for a in shards]
        self.out_shape = [jax.ShapeDtypeStruct((N_CHIPS,) + a.shape, a.dtype) for a in self.inputs]
        sem, one = pltpu.SemaphoreType.DMA((N_CHIPS - 1, n)), pltpu.SemaphoreType.DMA((n,))
        self.sems = [sem, sem, sem, sem, one, one]

    def _copies(self, src, dst, sems, a):
        send_sems, recv_sems, fsend_sems, frecv_sems, local_sems, own_sems = sems
        p, c = _chip_index(), lax.axis_index("c")
        own = pltpu.make_async_remote_copy(src_ref=src[a], dst_ref=dst[a].at[p], send_sem=local_sems.at[a],
                                           recv_sem=own_sems.at[a], device_id=_sibling(), device_id_type=MESH)
        out, arrive, onward, landed = [], [], [], []
        for k in range(1, N_CHIPS):
            ici = dict(send_sem=send_sems.at[k - 1, a], recv_sem=recv_sems.at[k - 1, a], device_id=_peer(k),
                       device_id_type=MESH)
            d2d = dict(send_sem=fsend_sems.at[k - 1, a], recv_sem=frecv_sems.at[k - 1, a], device_id=_sibling(),
                       device_id_type=MESH)
            got, theirs = dst[a].at[p ^ k, c], dst[a].at[p ^ k, 1 - c]
            out.append(pltpu.make_async_remote_copy(src_ref=src[a].at[c], dst_ref=dst[a].at[p, c], **ici))
            arrive.append(pltpu.make_async_remote_copy(src_ref=src[a].at[c], dst_ref=got, **ici))
            onward.append(pltpu.make_async_remote_copy(src_ref=got, dst_ref=got, **d2d))
            landed.append(pltpu.make_async_remote_copy(src_ref=theirs, dst_ref=theirs, **d2d))
        return own, out, arrive, onward, landed

    def stages(self, steps):
        n = len(self.shards)

        def start(src, dst, sems):
            for a in range(n):
                own, out, _, _, _ = self._copies(src, dst, sems, a)
                own.start()
                for cp in out:
                    cp.start()

        def pass_on(a):
            def act(src, dst, sems):
                _, _, arrive, onward, _ = self._copies(src, dst, sems, a)
                for k in range(N_CHIPS - 1):
                    arrive[k].wait_recv()
                    onward[k].start()
            return act

        def finish(src, dst, sems):
            for a in range(n):
                own, out, _, onward, landed = self._copies(src, dst, sems, a)
                for cp in landed:
                    cp.wait_recv()
                for cp in out + onward:
                    cp.wait_send()
                own.wait()

        mids = [(min(steps - 1, int(self.pass_on[a] * steps)), pass_on(a)) for a in range(n)]
        return [(0, start)] + mids + [(steps - 1, finish)]

    def results(self, outs):
        return [o.reshape((N_CHIPS,) + a.shape) for o, a in zip(outs, self.shards)]


class _WinGatherCargo:
    def __init__(self, shard):
        self.shard = shard
        self.inputs = [shard.reshape(2, shard.shape[0] // 2, shard.shape[1])]
        self.out_shape = [jax.ShapeDtypeStruct((N_CHIPS,) + self.inputs[0].shape, shard.dtype)]
        sem = pltpu.SemaphoreType.DMA((N_CHIPS - 1,))
        self.sems = [sem, sem, sem, sem, pltpu.SemaphoreType.DMA((2,))]

    def weights(self, src, dst):
        return src[0], dst[0]

    def _copies(self, src, dst, sems):
        send_sems, recv_sems, fsend_sems, frecv_sems, own_sems = sems
        p, c = _chip_index(), lax.axis_index("c")
        own = pltpu.make_async_remote_copy(src_ref=src[0], dst_ref=dst[0].at[p], send_sem=own_sems.at[0],
                                           recv_sem=own_sems.at[1], device_id=_sibling(), device_id_type=MESH)
        out, arrive, onward, landed = [], [], [], []
        for k in range(1, N_CHIPS):
            ici = dict(send_sem=send_sems.at[k - 1], recv_sem=recv_sems.at[k - 1], device_id=_peer(k), device_id_type=MESH)
            d2d = dict(send_sem=fsend_sems.at[k - 1], recv_sem=frecv_sems.at[k - 1], device_id=_sibling(),
                       device_id_type=MESH)
            got, theirs = dst[0].at[p ^ k, c], dst[0].at[p ^ k, 1 - c]
            out.append(pltpu.make_async_remote_copy(src_ref=src[0].at[c], dst_ref=dst[0].at[p, c], **ici))
            arrive.append(pltpu.make_async_remote_copy(src_ref=src[0].at[c], dst_ref=got, **ici))
            onward.append(pltpu.make_async_remote_copy(src_ref=got, dst_ref=got, **d2d))
            landed.append(pltpu.make_async_remote_copy(src_ref=theirs, dst_ref=theirs, **d2d))
        return own, out, arrive, onward, landed

    def stages(self, steps):
        def start(src, dst, sems):
            own, out, _, _, _ = self._copies(src, dst, sems)
            own.start()
            for cp in out:
                cp.start()

        def arrival(k):
            def act(src, dst, sems):
                _, _, arrive, onward, landed = self._copies(src, dst, sems)
                arrive[k - 1].wait_recv()
                onward[k - 1].start()
                landed[k - 1].wait_recv()
            return act

        def finish(src, dst, sems):
            own, out, _, onward, _ = self._copies(src, dst, sems)
            for cp in out + onward:
                cp.wait_send()
            own.wait()

        return ([(0, start)] + [(k * steps // N_CHIPS, arrival(k)) for k in range(1, N_CHIPS)] + [(steps - 1, finish)])

    def results(self, outs):
        return [outs[0].reshape((N_CHIPS,) + self.shard.shape)]


class _ExchangeCargo:
    def __init__(self, sums, whole):
        self.inputs, self.whole, n = sums, whole, len(sums)
        self.out_shape = [jax.ShapeDtypeStruct((N_CHIPS - 1,) + g.shape[1:], g.dtype) for g in sums]
        sem = pltpu.SemaphoreType.DMA((N_CHIPS - 1, n))
        self.sems = [sem, sem]

    def _copies(self, src, dst, sems):
        send_sems, recv_sems = sems
        p = _chip_index()
        return [pltpu.make_async_remote_copy(
            src_ref=src[a].at[0] if self.whole[a] else src[a].at[p ^ k], dst_ref=dst[a].at[k - 1],
            send_sem=send_sems.at[k - 1, a], recv_sem=recv_sems.at[k - 1, a], device_id=_peer(k), device_id_type=MESH)
            for k in range(1, N_CHIPS) for a in range(len(self.inputs))]

    def stages(self, steps):
        def start(src, dst, sems):
            for cp in self._copies(src, dst, sems):
                cp.start()

        def finish(src, dst, sems):
            copies = self._copies(src, dst, sems)
            for cp in copies:
                cp.wait_recv()
            for cp in copies:
                cp.wait_send()

        return [(0, start), (steps - 1, finish)]

    def results(self, outs):
        return list(outs)


class _MultiCargo:
    def __init__(self, cargos):
        self.cargos = cargos
        self.inputs = [a for cg in cargos for a in cg.inputs]
        self.out_shape = [o for cg in cargos for o in cg.out_shape]
        self.sems = [s for cg in cargos for s in cg.sems]

    def _parts(self, refs, count):
        out, off = [], 0
        for cg in self.cargos:
            out.append(refs[off:off + count(cg)])
            off += count(cg)
        return out

    def stages(self, steps):
        merged = []
        for n, cg in enumerate(self.cargos):
            for at, act in cg.stages(steps):
                def part(src, dst, sems, n=n, act=act):
                    act(self._parts(src, lambda g: len(g.inputs))[n], self._parts(dst, lambda g: len(g.out_shape))[n],
                        self._parts(sems, lambda g: len(g.sems))[n])
                merged.append((at, part))
        starts = [m for m in merged if m[0] == 0]
        rest = sorted([m for m in merged if m[0] != 0], key=lambda m: m[0])
        return starts + rest

    def results(self, outs):
        return [cg.results(part) for cg, part in zip(self.cargos, self._parts(list(outs), lambda g: len(g.out_shape)))]


def _cargo_call(body, cargo, *, name, grid, in_specs, out_specs, out_shape, scratch=(), prefetch=None,
                body_sees_cargo=False):
    n_in, n_out, n_scr = len(in_specs), len(out_specs), len(scratch)
    c_in, c_out = len(cargo.inputs), len(cargo.out_shape)
    steps = 1
    for g in grid:
        steps *= g
    stages = cargo.stages(steps)

    def wrapped(*refs):
        head, refs = (refs[:1], refs[1:]) if prefetch is not None else ((), refs)
        ins, refs = refs[:n_in], refs[n_in:]
        cin, refs = refs[:c_in], refs[c_in:]
        outs, refs = refs[:n_out], refs[n_out:]
        cout, refs = refs[:c_out], refs[c_out:]
        scr, sems = refs[:n_scr], refs[n_scr:]
        if not grid:
            for _, act in stages:
                act(cin, cout, sems)
            return
        step = 0
        for d in range(len(grid)):
            step = step * grid[d] + pl.program_id(d)
        for at, act in stages:
            if at < steps - 1:
                pl.when(step == at)(functools.partial(act, cin, cout, sems))
        body(*head, *ins, *outs, *scr, *(((cin, cout, sems),) if body_sees_cargo else ()))
        for at, act in stages:
            if at == steps - 1:
                pl.when(step == at)(functools.partial(act, cin, cout, sems))

    params = dict(vmem_limit_bytes=VMEM_LIMIT)
    if grid:
        params["dimension_semantics"] = ("arbitrary",) * len(grid)
    specs = dict(grid=grid, in_specs=list(in_specs) + [ANY] * c_in, out_specs=list(out_specs) + [ANY] * c_out,
                 scratch_shapes=list(scratch) + cargo.sems)
    if prefetch is not None:
        specs = dict(grid_spec=pltpu.PrefetchScalarGridSpec(num_scalar_prefetch=1, **specs))
    res = pl.pallas_call(
        wrapped, name=name, out_shape=list(out_shape) + cargo.out_shape,
        compiler_params=pltpu.CompilerParams(**params), **specs)
    head = () if prefetch is None else (prefetch,)
    return lambda *args: (lambda r: (r[:n_out], cargo.results(r[n_out:])))(res(*head, *args, *cargo.inputs))


def _split_halves(g):
    return g.reshape(g.shape[0], 2, g.shape[1] // 2, g.shape[2])


def _ew_tile(rows):
    return 256 if rows % 256 == 0 else rows


def _prefetch_call(body, *, name, grid, in_specs, out_specs, out_shape, **kw):
    return pl.pallas_call(
        body, name=name, out_shape=out_shape,
        grid_spec=pltpu.PrefetchScalarGridSpec(num_scalar_prefetch=1, grid=grid, in_specs=in_specs, out_specs=out_specs),
        compiler_params=pltpu.CompilerParams(dimension_semantics=("arbitrary",) * len(grid),
                                             vmem_limit_bytes=VMEM_LIMIT), **kw)


def _presum_swap(split, name):
    n = len(split)

    def body(*refs):
        src, got = refs[:n], refs[n:2 * n]
        send_sems, recv_sems = refs[2 * n:]
        c = lax.axis_index("c")
        remote = [pltpu.make_async_remote_copy(src_ref=src[a].at[:, 1 - c], dst_ref=got[a],
                                               send_sem=send_sems.at[a], recv_sem=recv_sems.at[a],
                                               device_id=_sibling(), device_id_type=MESH) for a in range(n)]
        for cp in remote:
            cp.start()
        for cp in remote:
            cp.wait_recv()
        for cp in remote:
            cp.wait_send()

    sem = pltpu.SemaphoreType.DMA((n,))
    return pl.pallas_call(
        body, name=name, in_specs=[ANY] * n, out_specs=[ANY] * n,
        out_shape=[jax.ShapeDtypeStruct((g.shape[0],) + g.shape[2:], g.dtype) for g in split],
        scratch_shapes=[sem, sem])(*split)


def _add_pair(mine, theirs, place, name):
    lead, _, rows, cols = mine.shape
    tm = _ew_tile(rows)

    def body(place_ref, a_ref, b_ref, o_ref):
        o_ref[...] = (a_ref[...].astype(f32) + b_ref[...].astype(f32)).astype(o_ref.dtype)

    blk = pl.BlockSpec((lead, tm, cols), lambda i, pc: (0, i, 0))
    return _prefetch_call(
        body, name=name, grid=(rows // tm,),
        in_specs=[pl.BlockSpec((lead, None, tm, cols), lambda i, pc: (0, pc[1], i, 0)), blk], out_specs=blk,
        out_shape=jax.ShapeDtypeStruct(theirs.shape, theirs.dtype))(place, mine, theirs)


class _NoCargo:
    def __init__(self, results=()):
        self.inputs, self.out_shape, self.sems, self._results = [], [], [], list(results)

    def stages(self, steps):
        return [(0, lambda src, dst, sems: None), (steps - 1, lambda src, dst, sems: None)]

    def results(self, outs):
        return self._results


def _cargo_alone(cargo, name):
    return _cargo_call(None, cargo, name=name, grid=(), in_specs=[], out_specs=[], out_shape=[])()[1]


def _sum_slots(own, recv, place, whole, name):
    _, rows, cols = own.shape
    tm = _ew_tile(rows)

    def body(place_ref, own_ref, r_ref, o_ref):
        r0, r1, r2 = [r_ref[k].astype(f32) for k in range(N_CHIPS - 1)]
        o_ref[...] = (own_ref[...].astype(f32) + r0) + (r1 + r2)

    return _prefetch_call(
        body, name=name, grid=(rows // tm,),
        in_specs=[pl.BlockSpec((None, tm, cols), lambda i, pc: (0 if whole else pc[0], i, 0)),
                  pl.BlockSpec((N_CHIPS - 1, tm, cols), lambda i, pc: (0, i, 0))],
        out_specs=pl.BlockSpec((None, tm, cols), lambda i, pc: (pc[1], i, 0)),
        out_shape=jax.ShapeDtypeStruct((2, rows, cols), f32))(place, own, recv)


def _sibling_fill(bufs):
    n = len(bufs)

    def body(*refs):
        src, dst = refs[:n], refs[n:2 * n]
        send_sems, recv_sems = refs[2 * n:]
        c = lax.axis_index("c")
        remote = [pltpu.make_async_remote_copy(src_ref=src[a].at[c], dst_ref=dst[a].at[c], send_sem=send_sems.at[a],
                                               recv_sem=recv_sems.at[a], device_id=_sibling(), device_id_type=MESH)
                  for a in range(n)]
        for cp in remote:
            cp.start()
        for a in range(n):
            pltpu.make_async_remote_copy(src_ref=src[a].at[c], dst_ref=dst[a].at[1 - c], send_sem=send_sems.at[a],
                                         recv_sem=recv_sems.at[a], device_id=_sibling(), device_id_type=MESH).wait_recv()
        for cp in remote:
            cp.wait_send()

    sem = pltpu.SemaphoreType.DMA((n,))
    out = pl.pallas_call(
        body, name="sibling_fill", in_specs=[ANY] * n, out_specs=[ANY] * n,
        out_shape=[jax.ShapeDtypeStruct(b.shape, b.dtype) for b in bufs],
        input_output_aliases={a: a for a in range(n)}, scratch_shapes=[sem, sem])(*bufs)
    return [o.reshape(2 * b.shape[1], b.shape[2]) for o, b in zip(out, bufs)]


def _adamw(w, m, v, grad, name, tm):
    rows, cols = w.shape
    gcols = grad.shape[1]
    if tm == rows:
        gspec = pl.BlockSpec(grad.shape, lambda i: (0, 0))
    else:
        gspec = pl.BlockSpec((tm, gcols), lambda i: (i, 0))

    def body(w_ref, m_ref, v_ref, gin_ref, g_ref, d_ref, nm_ref, nv_ref):
        g = gin_ref[0:tm, 0:cols]
        w_, m_, v_ = w_ref[...], m_ref[...], v_ref[...]
        m_ = ADAM_B1 * m_ + (1.0 - ADAM_B1) * g
        v_ = ADAM_B2 * v_ + (1.0 - ADAM_B2) * jnp.square(g)
        m_hat = m_ / (1.0 - ADAM_B1 ** ADAM_STEP)
        v_hat = v_ / (1.0 - ADAM_B2 ** ADAM_STEP)
        g_ref[...] = g
        d_ref[...] = -ADAM_LR * (m_hat / (jnp.sqrt(v_hat) + ADAM_EPS) + ADAM_WD * w_)
        nm_ref[...] = m_
        nv_ref[...] = v_

    blk = pl.BlockSpec((tm, cols), lambda i: (i, 0))
    return _call(
        body, name=name, grid=(rows // tm,),
        in_specs=[blk, blk, blk, gspec], out_specs=[blk] * 4,
        out_shape=[jax.ShapeDtypeStruct((rows, cols), f32)] * 4)(w, m, v, grad)


def _rope_tables(s):
    half = HEAD_DIM // 2
    inv_freq = 1.0 / (10000.0 ** (jnp.arange(half, dtype=f32) / half))
    ang = jnp.arange(s, dtype=f32)[:, None] * inv_freq[None, :]
    cos, sin = jnp.cos(ang), jnp.sin(ang)
    return jnp.tile(cos, (1, LANES // half)), jnp.tile(jnp.concatenate([-sin, sin], axis=1), (1, LANES // HEAD_DIM))


def _local_step(x, target, place, gather, exchange, b_gate, w_pool, pool_scale, ln1_g, ln1_b, convb_g, ln2_g, ln2_b):
    s = x.shape[0]
    nb = s // MOBA_BLOCK
    cos_t, sin_t = _rope_tables(s)
    x_bf = x.astype(bf16)

    z, win_g, (wba_g, wbp_g, wout_g, gate_tail) = _inproj(x_bf, gather["w_in"], cos_t, sin_t, b_gate, place,
                                                          gather["inproj"])
    wba = wba_g.transpose(1, 0, 2).reshape(D_ATTN, D_MODEL)
    wbp = wbp_g.transpose(1, 0, 2).reshape(D_POOL, D_MODEL)
    wout = wout_g.reshape(D_MODEL, D_MODEL)
    (o_bf, lse), (gate_head, wu_g, convw_g) = _attn_fwd(z, nb, gather["attn_fwd"])
    wg_g = jnp.concatenate([gate_head, gate_tail], axis=1)
    pooled_bf, mixed, pm_bf = _pool_fwd(z, w_pool, pool_scale)
    m_bf = _branch_merge(o_bf, pm_bf, z, wba, wbp)
    x1_bf, xhat1, rstd1 = _out_ln1(m_bf, wout, x, ln1_g, ln1_b)
    (a, up, ac, hh_bf), (wd_g,) = _ffn_up(x1_bf, wg_g, wu_g, convw_g, convb_g, gather["ffn_up"])
    wd = wd_g.reshape(D_FF_PAD, D_MODEL)
    loss, dh2, dh2_bf, d_ln2_g, d_ln2_b = _ffn_down_ln2_loss(hh_bf, wd, xhat1, ln1_g, ln1_b, ln2_g, ln2_b, target)

    da_bf, dup_bf, dconv = _ffn_act_bwd(dh2_bf, wd_g, ac, a, up, convw_g)
    d_wd = _matmul_tn(hh_bf, dh2_bf, 1, "dw_ffn_down").reshape(N_CHIPS, FF_PAD, D_MODEL)
    d_wg = _matmul_tn(da_bf, x1_bf, 1, "dw_ffn_gate").reshape(N_CHIPS, FF_PAD, D_MODEL)
    d_wu = _matmul_tn(dup_bf, x1_bf, 1, "dw_ffn_up").reshape(N_CHIPS, FF_PAD, D_MODEL)
    (dh1, dh1_bf, d_ln1_g, d_ln1_b), got_down = _ffn_in_bwd_ln1(
        da_bf, dup_bf, wg_g, wu_g, dh2, xhat1, rstd1, ln1_g, exchange("ffn_in_bwd_ln1", [("w_ffn_down", d_wd)]))
    d_wout = _matmul_tn(m_bf, dh1_bf, 1, "dw_out").reshape(N_CHIPS, D_MODEL // N_CHIPS, D_MODEL)
    dzg_bf, dya_bf, dyp_bf, do_bf, dpm, d_bgate = _merge_bwd(dh1_bf, wout, o_bf, pm_bf, z, wba, wbp)
    d_wba = _matmul_tn(o_bf, dya_bf, N_CHIPS, "dw_branch_attn")
    d_wbp = _matmul_tn(pm_bf, dyp_bf, N_CHIPS, "dw_branch_pool")
    du_bf, d_wpool, d_pscale = _pool_bwd(dpm, mixed, pooled_bf, w_pool, pool_scale)
    small = _pack_small([d_bgate, d_wpool, d_pscale, d_ln1_g, d_ln1_b, d_ln2_g, d_ln2_b, dconv[:, 3, :],
                         loss[0:1, :]])[None]
    (dq_bf, dk_bf, dv_bf), got_ffn = _attn_bwd(
        z, o_bf, lse, do_bf, cos_t, sin_t, nb,
        exchange("attn_bwd", [("w_ffn_gate", d_wg), ("w_ffn_up", d_wu), ("conv_w", dconv), ("small", small)]))
    dz_bf = jnp.concatenate([dq_bf, dk_bf, dv_bf, du_bf, dzg_bf], axis=1)
    d_win, got_branch = _matmul_tn(
        x_bf, dz_bf, N_CHIPS, "dw_in",
        cargo=exchange("dw_in", [("w_out", d_wout), ("w_branch_attn", d_wba), ("w_branch_pool", d_wbp)]))
    (grad_x,), got_in = _inproj_dx(dz_bf, win_g, dh1, exchange("inproj_dx", [("w_in", d_win)]))
    return grad_x, got_down + got_ffn + got_branch + got_in


SMALL_ROWS = 592


def _pack_small(parts):
    flat = jnp.concatenate([p.reshape(-1) for p in parts])
    return jnp.pad(flat, (0, SMALL_ROWS * LANES - flat.shape[0])).reshape(SMALL_ROWS, LANES)


def _unpack_small(packed, shapes):
    flat = packed.reshape(-1)
    out, off = [], 0
    for shp in shapes:
        n = 1
        for d in shp:
            n *= d
        out.append(flat[off:off + n].reshape(shp))
        off += n
    return out


def _pad_conv_b(cb):
    return jnp.pad(cb.reshape(N_CHIPS, FF_SHARD), ((0, 0), (0, FF_PAD - FF_SHARD)))


def kernel(x, w_in, b_gate, w_branch_attn, w_pool, pool_scale, w_branch_pool, w_out, ln1_g, ln1_b, w_ffn_gate, w_ffn_up, conv_w, conv_b, w_ffn_down, ln2_g, ln2_b, loss_target, m_w_in, m_b_gate, m_w_branch_attn, m_w_pool, m_pool_scale, m_w_branch_pool, m_w_out, m_ln1_g, m_ln1_b, m_w_ffn_gate, m_w_ffn_up, m_conv_w, m_conv_b, m_w_ffn_down, m_ln2_g, m_ln2_b, v_w_in, v_b_gate, v_w_branch_attn, v_w_pool, v_pool_scale, v_w_branch_pool, v_w_out, v_ln1_g, v_ln1_b, v_w_ffn_gate, v_w_ffn_up, v_conv_w, v_conv_b, v_w_ffn_down, v_ln2_g, v_ln2_b):
    pad_rows = lambda w: jnp.pad(w.astype(bf16), ((0, FF_PAD - FF_SHARD), (0, 0)))
    shards = [w_in[0].astype(bf16), w_branch_attn[0].astype(bf16), w_branch_pool[0].astype(bf16),
              w_out[0].astype(bf16), pad_rows(w_ffn_gate[0].T), pad_rows(w_ffn_up[0].T), pad_rows(w_ffn_down[0]),
              jnp.pad(conv_w[0], ((0, CONV_ROWS - 3), (0, FF_PAD - FF_SHARD)))]
    gate_head, gate_tail = shards[4][:GATE_SPLIT], shards[4][GATE_SPLIT:]
    gather = {"w_in": _WinGatherCargo(shards[0]),
              "inproj": _GatherCargo(shards[1:4] + [gate_tail], [0.35, 0.45, 0.6, 0.9]),
              "attn_fwd": _GatherCargo([gate_head, shards[5], shards[7]], [0.4, 0.9, 0.92]),
              "ffn_up": _GatherCargo(shards[6:7], [0.8])}

    place = jnp.stack([2 * lax.axis_index("x") + lax.axis_index("y"), lax.axis_index("c")]).astype(jnp.int32)
    chip_sums = {}

    def exchange(carrier, named):
        split = [_split_halves(g) for _, g in named]
        got = _presum_swap(split, "presum_swap_" + carrier)
        sums = [_add_pair(a, b, place, "presum_" + nm) for a, b, (nm, _) in zip(split, got, named)]
        chip_sums.update({nm: s for (nm, _), s in zip(named, sums)})
        return _ExchangeCargo(sums, [nm == "small" for nm, _ in named])

    convb_g = _pad_conv_b(conv_b).reshape(N_CHIPS, 1, FF_PAD)
    grad_x, brought = _local_step(
        x[0], loss_target[0], place, gather, exchange, b_gate, w_pool[0], pool_scale, ln1_g, ln1_b, convb_g, ln2_g, ln2_b)
    slots = dict(zip(chip_sums, brought))
    names = ["w_in", "w_branch_attn", "w_branch_pool", "w_out", "w_ffn_gate", "w_ffn_up", "w_ffn_down", "conv_w", "small"]
    grads = _sibling_fill([_sum_slots(chip_sums[nm], slots[nm], place, nm == "small", "sum_" + nm) for nm in names])

    weights = [w_in[0], w_branch_attn[0], w_branch_pool[0], w_out[0], w_ffn_gate[0].T, w_ffn_up[0].T, w_ffn_down[0], conv_w[0]]
    m_in = [m_w_in[0], m_w_branch_attn[0], m_w_branch_pool[0], m_w_out[0], m_w_ffn_gate[0].T, m_w_ffn_up[0].T, m_w_ffn_down[0], m_conv_w[0]]
    v_in = [v_w_in[0], v_w_branch_attn[0], v_w_branch_pool[0], v_w_out[0], v_w_ffn_gate[0].T, v_w_ffn_up[0].T, v_w_ffn_down[0], v_conv_w[0]]
    tiles = [256, 256, 256, 128, 88, 88, 88, 3]
    res = {}
    for i, nm in enumerate(names[:-1]):
        outs4 = _adamw(weights[i], m_in[i], v_in[i], grads[i], "adamw_" + nm, tiles[i])
        res[nm] = [(r.T if nm in ("w_ffn_gate", "w_ffn_up") else r)[None] for r in outs4]
    loss = grads[-1][SMALL_ROWS - 4, 0]

    small_names = ["b_gate", "w_pool", "pool_scale", "ln1_g", "ln1_b", "ln2_g", "ln2_b", "conv_b"]
    small_w = [b_gate, w_pool, pool_scale, ln1_g, ln1_b, ln2_g, ln2_b, _pad_conv_b(conv_b)]
    small_m = [m_b_gate, m_w_pool, m_pool_scale, m_ln1_g, m_ln1_b, m_ln2_g, m_ln2_b, _pad_conv_b(m_conv_b)]
    small_v = [v_b_gate, v_w_pool, v_pool_scale, v_ln1_g, v_ln1_b, v_ln2_g, v_ln2_b, _pad_conv_b(v_conv_b)]
    packed = _adamw(_pack_small(small_w), _pack_small(small_m), _pack_small(small_v), grads[-1],
                    "adamw_small", SMALL_ROWS)
    shapes = [w.shape for w in small_w]
    for kind in range(4):
        for nm, val in zip(small_names, _unpack_small(packed[kind], shapes)):
            if nm == "conv_b":
                val = val[:, :FF_SHARD].reshape(1, 4 * FF_SHARD)
            res.setdefault(nm, [None] * 4)[kind] = val

    order = ["w_in", "b_gate", "w_branch_attn", "w_pool", "pool_scale", "w_branch_pool", "w_out", "ln1_g", "ln1_b",
             "w_ffn_gate", "w_ffn_up", "conv_w", "conv_b", "w_ffn_down", "ln2_g", "ln2_b"]
    outs = [loss, grad_x[None]]
    for kind in range(4):
        outs += [res[nm][kind] for nm in order]
    return tuple(outs)
```

```python
import functools

import jax
import jax.numpy as jnp
from jax import lax
from jax.experimental import pallas as pl
from jax.experimental.pallas import tpu as pltpu

D_MODEL = 1024
HEAD_DIM = 64
D_ATTN = 512
D_POOL = 512
MOBA_BLOCK = 256
MOBA_TOPK = 3
POOL_GROUP = 128
MAX_WINDOW = 16
FF_SHARD = 704
FF_PAD = 768
D_FF_PAD = 4 * FF_PAD
N_CHIPS = 4
LANES = 128
ALPHA = (2.0 * 1) ** 0.25
LN_EPS = 1e-5
NEG = -1e30
SCALE = HEAD_DIM ** -0.5
ADAM_LR, ADAM_B1, ADAM_B2, ADAM_EPS, ADAM_WD, ADAM_STEP = 0.001, 0.9, 0.999, 1e-08, 0.01, 10
VMEM_LIMIT = 56 * 1024 * 1024
MESH = pl.DeviceIdType.MESH

bf16 = jnp.bfloat16
f32 = jnp.float32


def _dot(a, b):
    return jnp.dot(a, b, preferred_element_type=f32)


def _dot_nt(a, b):
    return lax.dot_general(a, b, (((1,), (1,)), ((), ())), preferred_element_type=f32)


def _dot_tn(a, b):
    return lax.dot_general(a, b, (((0,), (0,)), ((), ())), preferred_element_type=f32)


def _call(body, *, name, grid, in_specs, out_specs, out_shape, scratch=(), cargo=None):
    if cargo is not None:
        return _cargo_call(body, cargo, name=name, grid=grid, in_specs=in_specs, out_specs=out_specs,
                           out_shape=out_shape, scratch=scratch)
    return pl.pallas_call(
        body, name=name, grid=grid, in_specs=in_specs, out_specs=out_specs, out_shape=out_shape,
        scratch_shapes=list(scratch),
        compiler_params=pltpu.CompilerParams(dimension_semantics=("arbitrary",) * len(grid),
                                             vmem_limit_bytes=VMEM_LIMIT))


def _rot_half(t):
    lane = lax.broadcasted_iota(jnp.int32, t.shape, 1)
    first = (lane % HEAD_DIM) < (HEAD_DIM // 2)
    return jnp.where(first, pltpu.roll(t, LANES - HEAD_DIM // 2, 1), pltpu.roll(t, HEAD_DIM // 2, 1))


def _rope(t, cos, sin_signed):
    return t * cos + _rot_half(t) * sin_signed


def _rope_bwd(d, cos, sin_signed):
    return d * cos + _rot_half(d * sin_signed)


def _gelu_parts(a):
    cdf = 0.5 * (1.0 + lax.erf(a * (2.0 ** -0.5)))
    pdf = jnp.exp(-0.5 * a * a) * ((2.0 * jnp.pi) ** -0.5)
    return a * cdf, cdf + a * pdf


def _layer_norm(h, g, b):
    mu = jnp.mean(h, axis=-1, keepdims=True)
    xc = h - mu
    var = jnp.mean(xc * xc, axis=-1, keepdims=True)
    rstd = lax.rsqrt(var + LN_EPS)
    xhat = xc * rstd
    return xhat * g + b, xhat, rstd


def _layer_norm_bwd(dy, xhat, rstd, g):
    dxh = dy * g
    m1 = jnp.mean(dxh, axis=-1, keepdims=True)
    m2 = jnp.mean(dxh * xhat, axis=-1, keepdims=True)
    return rstd * (dxh - m1 - xhat * m2)


def _inproj(x_bf, win_cargo, cos_t, sin_t, b_gate, place, cargo):
    s = x_bf.shape[0]
    tm, tn = 1024, 512
    halves = (2, D_MODEL // 2, tn)

    def body(place_ref, x_ref, cos_ref, sin_ref, b_ref, o_ref, w_tile, w_sem, cargo_refs):
        order, t, i = pl.program_id(0), pl.program_id(1), pl.program_id(2)
        shard = place_ref[0] ^ order
        col = 2 * shard + t
        mine, everyones = win_cargo.weights(cargo_refs[0], cargo_refs[1])

        @pl.when(i == 0)
        def _():
            for tt in range(D_MODEL // tn):
                cols = slice(tt * tn, (tt + 1) * tn)

                def fetch(src):
                    cp = pltpu.make_async_copy(src, w_tile, w_sem)
                    cp.start()
                    cp.wait()

                pl.when((t == tt) & (order == 0))(lambda: fetch(mine.at[:, :, cols]))
                pl.when((t == tt) & (order > 0))(lambda: fetch(everyones.at[shard, :, :, cols]))

        acc = _dot(x_ref[...], w_tile[...].reshape(D_MODEL, tn))

        @pl.when(col < 2)
        def _():
            for c in range(tn // LANES):
                sl = slice(c * LANES, (c + 1) * LANES)
                o_ref[:, sl] = _rope(acc[:, sl], cos_ref[...], sin_ref[...])

        @pl.when((col >= 2) & (col < 4))
        def _():
            o_ref[...] = acc

        @pl.when(col >= 4)
        def _():
            o_ref[...] = jax.nn.sigmoid(acc + b_ref[...])

    column = lambda order, t, pc: 2 * (pc[0] ^ order) + t
    (z,), (win_g, carried) = _cargo_call(
        body, _MultiCargo([win_cargo, cargo]), name="inproj", grid=(N_CHIPS, D_MODEL // tn, s // tm),
        in_specs=[pl.BlockSpec((tm, D_MODEL), lambda o, t, i, pc: (i, 0)),
                  pl.BlockSpec((tm, LANES), lambda o, t, i, pc: (i, 0)),
                  pl.BlockSpec((tm, LANES), lambda o, t, i, pc: (i, 0)),
                  pl.BlockSpec((1, tn), lambda o, t, i, pc: (0, jnp.maximum(column(o, t, pc) - 4, 0)))],
        out_specs=[pl.BlockSpec((tm, tn), lambda o, t, i, pc: (i, column(o, t, pc)))],
        out_shape=[jax.ShapeDtypeStruct((s, 4 * D_MODEL), f32)],
        scratch=[pltpu.VMEM(halves, bf16), pltpu.SemaphoreType.DMA], prefetch=place, body_sees_cargo=True,
    )(x_bf, cos_t, sin_t, b_gate)
    return z, win_g[0], carried


STAT_ROWS = 8


def _pair_rows(v0, v1, fill):
    r = lax.broadcasted_iota(jnp.int32, (STAT_ROWS, v0.shape[1]), 0)
    return jnp.where(r == 0, v0, jnp.where(r == 1, v1, fill))


def _head_lanes(shape):
    lane = lax.broadcasted_iota(jnp.int32, shape, 1)
    return lane < HEAD_DIM, lane >= HEAD_DIM


def _head_rows(shape):
    row = lax.broadcasted_iota(jnp.int32, shape, 0)
    return row < HEAD_DIM, row >= HEAD_DIM


def _moba_select(q_bf, k_all, i, nb):
    k_mean = jnp.mean(k_all.reshape(nb, MOBA_BLOCK, LANES), axis=1)
    n_io = lax.broadcasted_iota(jnp.int32, (nb, MOBA_BLOCK), 0)
    past = n_io < i
    sels = []
    for head in _head_lanes((nb, LANES)):
        gate = _dot_nt(jnp.where(head, k_mean, 0.0).astype(bf16), q_bf)
        g = jnp.where(past, gate, NEG)
        rank = jnp.zeros((nb, MOBA_BLOCK), f32)
        for m in range(nb):
            gm = g[m:m + 1, :]
            rank = rank + jnp.where((gm > g) | ((gm == g) & (m < n_io)), 1.0, 0.0)
        sels.append(jnp.where(past & (rank < MOBA_TOPK), 1.0, 0.0))
    return sels


def _attn_fwd(z, nb, cargo):
    s = z.shape[0]

    def body(q_ref, k_ref, v_ref, o_ref, lse_ref, vt_s):
        i = pl.program_id(1)

        @pl.when(i == 0)
        def _():
            for j in range(nb):
                vt_s[j] = v_ref[j * MOBA_BLOCK:(j + 1) * MOBA_BLOCK, :].T.astype(bf16)

        k_heads = _head_lanes((MOBA_BLOCK, LANES))
        o_heads = _head_rows((LANES, MOBA_BLOCK))

        def block(j, qs, allow, m_old, l_old):
            kj, vtj = k_ref[j * MOBA_BLOCK:(j + 1) * MOBA_BLOCK, :], vt_s[j]
            m_new, alpha, l_new, pv = [], [], [], None
            for h in range(2):
                sc = jnp.where(allow[h], _dot_nt(jnp.where(k_heads[h], kj, 0.0).astype(bf16), qs), NEG)
                mx = jnp.max(sc, axis=0, keepdims=True)
                mn = mx if m_old is None else jnp.maximum(m_old[h], mx)
                p = jnp.exp(sc - mn)
                lsum = jnp.sum(p, axis=0, keepdims=True)
                if m_old is None:
                    a = None
                else:
                    a = jnp.exp(m_old[h] - mn)
                    lsum = a * l_old[h] + lsum
                t = _dot(jnp.where(o_heads[h], vtj, jnp.zeros_like(vtj)), p.astype(bf16))
                pv = t if pv is None else pv + t
                m_new.append(mn)
                alpha.append(a)
                l_new.append(lsum)
            return m_new, alpha, l_new, pv

        def query_block(c):
            q = q_ref[...]
            sels = _moba_select(q.astype(bf16), k_ref[...], c, nb)
            qs = (q * SCALE).astype(bf16)
            key = lax.broadcasted_iota(jnp.int32, (MOBA_BLOCK, MOBA_BLOCK), 0)
            qry = lax.broadcasted_iota(jnp.int32, (MOBA_BLOCK, MOBA_BLOCK), 1)
            m, _, l, acc = block(c, qs, [key <= qry] * 2, None, None)
            for j in range(c):
                m, alpha, l, pv = block(j, qs, [sels[h][j:j + 1, :] > 0.0 for h in range(2)], m, l)
                acc = acc * jnp.where(o_heads[0], alpha[0], alpha[1]) + pv
            o_ref[...] = (acc / jnp.where(o_heads[0], l[0], l[1])).T.astype(bf16)
            lse_ref[...] = _pair_rows(m[0] + jnp.log(l[0]), m[1] + jnp.log(l[1]), 0.0)

        for c in range(nb):
            pl.when(i == c)(functools.partial(query_block, c))

    return _cargo_call(
        body, cargo, name="attn_fwd", grid=(D_ATTN // LANES, nb),
        in_specs=[pl.BlockSpec((MOBA_BLOCK, LANES), lambda hp, i: (i, hp)),
                  pl.BlockSpec((s, LANES), lambda hp, i: (0, 4 + hp)),
                  pl.BlockSpec((s, LANES), lambda hp, i: (0, 8 + hp))],
        out_specs=[pl.BlockSpec((MOBA_BLOCK, LANES), lambda hp, i: (i, hp)),
                   pl.BlockSpec((None, STAT_ROWS, MOBA_BLOCK), lambda hp, i: (hp, 0, i))],
        out_shape=[jax.ShapeDtypeStruct((s, D_ATTN), bf16),
                   jax.ShapeDtypeStruct((D_ATTN // LANES, STAT_ROWS, s), f32)],
        scratch=[pltpu.VMEM((nb, LANES, MOBA_BLOCK), bf16)])(z, z, z)


def _window_select(snaps, g):
    return jnp.where(g == 0, snaps[0], jnp.where(g == 1, snaps[1], jnp.where(g == 2, snaps[2], snaps[3])))


def _pool_fwd(z, w_pool, pool_scale, cargo):
    s = z.shape[0]

    def body(u_ref, w_ref, sc_ref, pooled_ref, mixed_ref, pm_ref, pad):
        g = pl.program_id(0)
        u = u_ref[...]
        pad[0:MAX_WINDOW, :] = jnp.zeros((MAX_WINDOW, POOL_GROUP), f32)
        pad[MAX_WINDOW:MAX_WINDOW + s, :] = u
        acc = u
        snaps = []
        for d in range(1, MAX_WINDOW):
            acc = acc + pad[MAX_WINDOW - d:MAX_WINDOW - d + s, :]
            if d + 1 in (2, 4, 8, 16):
                snaps.append(acc)
        win = _window_select(snaps, g)
        t = lax.broadcasted_iota(jnp.int32, (s, POOL_GROUP), 0)
        count = jnp.minimum(t + 1, jnp.left_shift(2, g)).astype(f32)
        pooled = (win / count - u).astype(bf16)
        mixed = _dot(pooled, w_ref[...].astype(bf16))
        pooled_ref[...] = pooled
        mixed_ref[...] = mixed
        pm_ref[...] = (mixed * sc_ref[...]).astype(bf16)

    blk = pl.BlockSpec((s, POOL_GROUP), lambda g: (0, g))
    return _call(
        body, name="pool_fwd", grid=(4,),
        in_specs=[pl.BlockSpec((s, POOL_GROUP), lambda g: (0, 12 + g)),
                  pl.BlockSpec((None, POOL_GROUP, POOL_GROUP), lambda g: (g, 0, 0)),
                  pl.BlockSpec((1, POOL_GROUP), lambda g: (0, g))],
        out_specs=[blk, blk, blk],
        out_shape=[jax.ShapeDtypeStruct((s, D_POOL), bf16), jax.ShapeDtypeStruct((s, D_POOL), f32),
                   jax.ShapeDtypeStruct((s, D_POOL), bf16)],
        scratch=[pltpu.VMEM((s + MAX_WINDOW, POOL_GROUP), f32)], cargo=cargo)(z, w_pool, pool_scale)


def _branch_merge(o_bf, pm_bf, z, wba, wbp, cargo):
    s = o_bf.shape[0]
    tm = 512

    def body(o_ref, pm_ref, ga_ref, gp_ref, wba_ref, wbp_ref, m_ref):
        ya = _dot(o_ref[...], wba_ref[...])
        yp = _dot(pm_ref[...], wbp_ref[...])
        m_ref[...] = (ga_ref[...] * ya + gp_ref[...] * yp).astype(bf16)

    full = lambda r, c: pl.BlockSpec((r, c), lambda i: (0, 0))
    return _call(
        body, name="branch_merge", grid=(s // tm,),
        in_specs=[pl.BlockSpec((tm, D_ATTN), lambda i: (i, 0)), pl.BlockSpec((tm, D_POOL), lambda i: (i, 0)),
                  pl.BlockSpec((tm, D_MODEL), lambda i: (i, 2)), pl.BlockSpec((tm, D_MODEL), lambda i: (i, 3)),
                  full(D_ATTN, D_MODEL), full(D_POOL, D_MODEL)],
        out_specs=[pl.BlockSpec((tm, D_MODEL), lambda i: (i, 0))],
        out_shape=[jax.ShapeDtypeStruct((s, D_MODEL), bf16)], cargo=cargo)(o_bf, pm_bf, z, z, wba, wbp)


def _out_ln1(m_bf, wout, x, ln_g, ln_b, cargo):
    s = x.shape[0]
    tm = 512

    def body(m_ref, w_ref, x_ref, g_ref, b_ref, x1_ref, xhat_ref, rstd_ref):
        h = ALPHA * x_ref[...] + _dot(m_ref[...], w_ref[...])
        y, xhat, rstd = _layer_norm(h, g_ref[...], b_ref[...])
        x1_ref[...] = y.astype(bf16)
        xhat_ref[...] = xhat
        rstd_ref[...] = jnp.broadcast_to(rstd, (tm, LANES))

    row = pl.BlockSpec((tm, D_MODEL), lambda i: (i, 0))
    vec = pl.BlockSpec((1, D_MODEL), lambda i: (0, 0))
    return _call(
        body, name="out_ln1", grid=(s // tm,),
        in_specs=[row, pl.BlockSpec((D_MODEL, D_MODEL), lambda i: (0, 0)), row, vec, vec],
        out_specs=[row, row, pl.BlockSpec((tm, LANES), lambda i: (i, 0))],
        out_shape=[jax.ShapeDtypeStruct((s, D_MODEL), bf16), jax.ShapeDtypeStruct((s, D_MODEL), f32),
                   jax.ShapeDtypeStruct((s, LANES), f32)], cargo=cargo)(m_bf, wout, x, ln_g, ln_b)


FF_TILE = 256
FF_TILES_PER_SHARD = FF_PAD // FF_TILE
CONV_PAD = 8
CONV_ROWS = 16

def _ff_weight_spec(rows):
    return pl.BlockSpec((None, rows, FF_TILE), lambda j: (j // FF_TILES_PER_SHARD, 0, j % FF_TILES_PER_SHARD))


def _ff_rows_spec():
    return pl.BlockSpec((None, FF_TILE, D_MODEL), lambda j: (j // FF_TILES_PER_SHARD, j % FF_TILES_PER_SHARD, 0))


def _row_shifts(x, shifts):
    s = x.shape[0]
    padded = jnp.concatenate([x, jnp.zeros((CONV_PAD, x.shape[1]), x.dtype)], axis=0)
    return [pltpu.roll(padded, d % (s + CONV_PAD), 0)[0:s] for d in shifts]


def _ffn_up(x1_bf, wg_g, wu_g, convw_g, convb_g, cargo):
    s = x1_bf.shape[0]

    def body(x_ref, wg_ref, wu_ref, cw_ref, cb_ref, a_ref, up_ref, ac_ref, hh_ref):
        x1 = x_ref[...]
        a = _dot_nt(x1, wg_ref[...])
        up = _dot_nt(x1, wu_ref[...])
        a_ref[...] = a
        up_ref[...] = up
        a1, a2 = _row_shifts(a, (1, 2))
        ac = a * cw_ref[2:3, :] + a1 * cw_ref[1:2, :] + a2 * cw_ref[0:1, :] + cb_ref[...]
        ac_ref[...] = ac
        hg, _ = _gelu_parts(ac)
        hh_ref[...] = (hg * up).astype(bf16)

    col = pl.BlockSpec((s, FF_TILE), lambda j: (0, j))
    wide = jax.ShapeDtypeStruct((s, D_FF_PAD), f32)
    return _call(
        body, name="ffn_up", grid=(D_FF_PAD // FF_TILE,),
        in_specs=[pl.BlockSpec((s, D_MODEL), lambda j: (0, 0)), _ff_rows_spec(), _ff_rows_spec(),
                  _ff_weight_spec(8), _ff_weight_spec(1)],
        out_specs=[col, col, col, col],
        out_shape=[wide, wide, wide, jax.ShapeDtypeStruct((s, D_FF_PAD), bf16)],
        cargo=cargo)(x1_bf, wg_g, wu_g, convw_g, convb_g)


def _ffn_down_ln2_loss(hh_bf, wd, xhat1, ln1_g, ln1_b, ln2_g, ln2_b, target):
    s = hh_bf.shape[0]
    tm = 256

    def body(hh_ref, w_ref, xh1_ref, g1_ref, b1_ref, g2_ref, b2_ref, t_ref, loss_ref, dh_ref, dhb_ref, dg_ref, db_ref):
        i = pl.program_id(0)
        x1 = xh1_ref[...] * g1_ref[...] + b1_ref[...]
        h = ALPHA * x1 + _dot(hh_ref[...], w_ref[...])
        y, xhat, rstd = _layer_norm(h, g2_ref[...], b2_ref[...])
        err = y - t_ref[...]
        part = 0.5 * jnp.sum(jnp.mean(err * err, axis=-1, keepdims=True), axis=0, keepdims=True)
        dy = err * (1.0 / D_MODEL)

        @pl.when(i == 0)
        def _():
            loss_ref[...] = jnp.zeros_like(loss_ref)
            dg_ref[...] = jnp.zeros_like(dg_ref)
            db_ref[...] = jnp.zeros_like(db_ref)

        loss_ref[...] += jnp.broadcast_to(part, loss_ref.shape)
        dg_ref[...] += jnp.sum(dy * xhat, axis=0, keepdims=True)
        db_ref[...] += jnp.sum(dy, axis=0, keepdims=True)
        dh = _layer_norm_bwd(dy, xhat, rstd, g2_ref[...])
        dh_ref[...] = dh
        dhb_ref[...] = dh.astype(bf16)

    row = pl.BlockSpec((tm, D_MODEL), lambda i: (i, 0))
    vec = pl.BlockSpec((1, D_MODEL), lambda i: (0, 0))
    return _call(
        body, name="ffn_down_ln2_loss", grid=(s // tm,),
        in_specs=[pl.BlockSpec((tm, D_FF_PAD), lambda i: (i, 0)),
                  pl.BlockSpec((D_FF_PAD, D_MODEL), lambda i: (0, 0)),
                  row, vec, vec, vec, vec, row],
        out_specs=[pl.BlockSpec((8, LANES), lambda i: (0, 0)), row, row, vec, vec],
        out_shape=[jax.ShapeDtypeStruct((8, LANES), f32), jax.ShapeDtypeStruct((s, D_MODEL), f32),
                   jax.ShapeDtypeStruct((s, D_MODEL), bf16), jax.ShapeDtypeStruct((1, D_MODEL), f32),
                   jax.ShapeDtypeStruct((1, D_MODEL), f32)])(hh_bf, wd, xhat1, ln1_g, ln1_b, ln2_g, ln2_b, target)


def _ffn_act_bwd(dh2_bf, wd_g, ac, a, up, convw_g):
    s = dh2_bf.shape[0]

    def body(dh_ref, wd_ref, ac_ref, a_ref, up_ref, cw_ref, da_ref, dup_ref, dc_ref):
        dhh = _dot_nt(dh_ref[...], wd_ref[...])
        hg, dgelu = _gelu_parts(ac_ref[...])
        dup_ref[...] = (dhh * hg).astype(bf16)
        dac = dhh * up_ref[...] * dgelu
        dac1, dac2 = _row_shifts(dac, (-1, -2))
        a = a_ref[...]
        rows = [jnp.sum(d * a, axis=0, keepdims=True) for d in (dac2, dac1, dac)]
        rows.append(jnp.sum(dac, axis=0, keepdims=True))
        rows.append(jnp.zeros((CONV_ROWS - 4, FF_TILE), f32))
        dc_ref[...] = jnp.concatenate(rows, axis=0)
        da_ref[...] = (dac * cw_ref[2:3, :] + dac1 * cw_ref[1:2, :] + dac2 * cw_ref[0:1, :]).astype(bf16)

    col = pl.BlockSpec((s, FF_TILE), lambda j: (0, j))
    return _call(
        body, name="ffn_act_bwd", grid=(D_FF_PAD // FF_TILE,),
        in_specs=[pl.BlockSpec((s, D_MODEL), lambda j: (0, 0)), _ff_rows_spec(), col, col, col, _ff_weight_spec(8)],
        out_specs=[col, col, _ff_weight_spec(CONV_ROWS)],
        out_shape=[jax.ShapeDtypeStruct((s, D_FF_PAD), bf16), jax.ShapeDtypeStruct((s, D_FF_PAD), bf16),
                   jax.ShapeDtypeStruct((N_CHIPS, CONV_ROWS, FF_PAD), f32)],
    )(dh2_bf, wd_g, ac, a, up, convw_g)


def _matmul_tn(a, b, n_shards, name, tm=512, tn=1024, cargo=None):
    k, m = a.shape
    n = b.shape[1]
    tm, tn = min(tm, m), min(tn, n // n_shards)
    per = n // n_shards // tn

    def body(a_ref, b_ref, o_ref, at_s):
        @pl.when(pl.program_id(1) == 0)
        def _():
            at_s[...] = a_ref[...].T

        o_ref[...] = _dot(at_s[...], b_ref[...]).astype(bf16)

    res = _call(
        body, name=name, grid=(m // tm, n // tn),
        in_specs=[pl.BlockSpec((k, tm), lambda i, j: (0, i)), pl.BlockSpec((k, tn), lambda i, j: (0, j))],
        out_specs=[pl.BlockSpec((None, tm, tn), lambda i, j: (j // per, i, j % per))],
        out_shape=[jax.ShapeDtypeStruct((n_shards, m, n // n_shards), bf16)], scratch=[pltpu.VMEM((tm, k), bf16)],
        cargo=cargo)(a, b)
    return res[0] if cargo is None else (res[0][0], res[1])


def _ffn_in_bwd_ln1(da_bf, dup_bf, wg_g, wu_g, dh2, xhat1, rstd1, ln1_g, cargo):
    s = da_bf.shape[0]
    tm = 256

    def body(da_ref, dup_ref, wg_ref, wu_ref, dh2_ref, xh_ref, rstd_ref, g_ref, dh_ref, dhb_ref, dg_ref, db_ref):
        i = pl.program_id(0)
        dx1 = ALPHA * dh2_ref[...]
        for sh in range(N_CHIPS):
            sl = slice(sh * FF_PAD, (sh + 1) * FF_PAD)
            dx1 = dx1 + _dot(da_ref[:, sl], wg_ref[sh]) + _dot(dup_ref[:, sl], wu_ref[sh])
        xhat = xh_ref[...]

        @pl.when(i == 0)
        def _():
            dg_ref[...] = jnp.zeros_like(dg_ref)
            db_ref[...] = jnp.zeros_like(db_ref)

        dg_ref[...] += jnp.sum(dx1 * xhat, axis=0, keepdims=True)
        db_ref[...] += jnp.sum(dx1, axis=0, keepdims=True)
        dh = _layer_norm_bwd(dx1, xhat, rstd_ref[:, 0:1], g_ref[...])
        dh_ref[...] = dh
        dhb_ref[...] = dh.astype(bf16)

    row = pl.BlockSpec((tm, D_MODEL), lambda i: (i, 0))
    wide = pl.BlockSpec((tm, D_FF_PAD), lambda i: (i, 0))
    wfull = pl.BlockSpec((N_CHIPS, FF_PAD, D_MODEL), lambda i: (0, 0, 0))
    vec = pl.BlockSpec((1, D_MODEL), lambda i: (0, 0))
    return _call(
        body, name="ffn_in_bwd_ln1", grid=(s // tm,),
        in_specs=[wide, wide, wfull, wfull, row, row, pl.BlockSpec((tm, LANES), lambda i: (i, 0)), vec],
        out_specs=[row, row, vec, vec],
        out_shape=[jax.ShapeDtypeStruct((s, D_MODEL), f32), jax.ShapeDtypeStruct((s, D_MODEL), bf16),
                   jax.ShapeDtypeStruct((1, D_MODEL), f32), jax.ShapeDtypeStruct((1, D_MODEL), f32)],
        cargo=cargo)(da_bf, dup_bf, wg_g, wu_g, dh2, xhat1, rstd1, ln1_g)


def _merge_bwd(dh1_bf, wout, o_bf, pm_bf, z, wba, wbp):
    s = dh1_bf.shape[0]
    tm = 256

    def body(dh_ref, wout_ref, o_ref, pm_ref, ga_ref, gp_ref, wba_ref, wbp_ref,
             dzg_ref, dya_ref, dyp_ref, do_ref, dpm_ref, dbg_ref):
        i = pl.program_id(0)
        dm = _dot_nt(dh_ref[...], wout_ref[...])
        ya = _dot(o_ref[...], wba_ref[...])
        yp = _dot(pm_ref[...], wbp_ref[...])
        ga, gp = ga_ref[...], gp_ref[...]
        dza = dm * ya * ga * (1.0 - ga)
        dzp = dm * yp * gp * (1.0 - gp)

        @pl.when(i == 0)
        def _():
            dbg_ref[...] = jnp.zeros_like(dbg_ref)

        dbg_ref[:, 0:D_MODEL] += jnp.sum(dza, axis=0, keepdims=True)
        dbg_ref[:, D_MODEL:2 * D_MODEL] += jnp.sum(dzp, axis=0, keepdims=True)
        dzg_ref[:, 0:D_MODEL] = dza.astype(bf16)
        dzg_ref[:, D_MODEL:2 * D_MODEL] = dzp.astype(bf16)
        dya = (dm * ga).astype(bf16)
        dyp = (dm * gp).astype(bf16)
        dya_ref[...] = dya
        dyp_ref[...] = dyp
        do_ref[...] = _dot_nt(dya, wba_ref[...]).astype(bf16)
        dpm_ref[...] = _dot_nt(dyp, wbp_ref[...])

    row = pl.BlockSpec((tm, D_MODEL), lambda i: (i, 0))
    half = pl.BlockSpec((tm, D_ATTN), lambda i: (i, 0))
    full = lambda r, c: pl.BlockSpec((r, c), lambda i: (0, 0))
    return _call(
        body, name="merge_bwd", grid=(s // tm,),
        in_specs=[row, full(D_MODEL, D_MODEL), half, half,
                  pl.BlockSpec((tm, D_MODEL), lambda i: (i, 2)), pl.BlockSpec((tm, D_MODEL), lambda i: (i, 3)),
                  full(D_ATTN, D_MODEL), full(D_POOL, D_MODEL)],
        out_specs=[pl.BlockSpec((tm, 2 * D_MODEL), lambda i: (i, 0)), row, row, half, half,
                   pl.BlockSpec((1, 2 * D_MODEL), lambda i: (0, 0))],
        out_shape=[jax.ShapeDtypeStruct((s, 2 * D_MODEL), bf16), jax.ShapeDtypeStruct((s, D_MODEL), bf16),
                   jax.ShapeDtypeStruct((s, D_MODEL), bf16), jax.ShapeDtypeStruct((s, D_ATTN), bf16),
                   jax.ShapeDtypeStruct((s, D_POOL), f32), jax.ShapeDtypeStruct((1, 2 * D_MODEL), f32)],
    )(dh1_bf, wout, o_bf, pm_bf, z, z, wba, wbp)


def _pool_bwd(dpm, mixed, pooled_bf, w_pool, pool_scale):
    s = dpm.shape[0]

    def body(dpm_ref, mixed_ref, pooled_ref, w_ref, sc_ref, du_ref, dw_ref, dsc_ref, pad):
        g = pl.program_id(0)
        dpm_v = dpm_ref[...]
        dsc_ref[...] = jnp.sum(dpm_v * mixed_ref[...], axis=0, keepdims=True)
        dmixed = (dpm_v * sc_ref[...]).astype(bf16)
        dw_ref[...] = _dot_tn(pooled_ref[...], dmixed)
        dpooled = _dot_nt(dmixed, w_ref[...].astype(bf16))
        t = lax.broadcasted_iota(jnp.int32, (s, POOL_GROUP), 0)
        count = jnp.minimum(t + 1, jnp.left_shift(2, g)).astype(f32)
        r = dpooled / count
        pad[0:s, :] = r
        pad[s:s + MAX_WINDOW, :] = jnp.zeros((MAX_WINDOW, POOL_GROUP), f32)
        acc = r
        snaps = []
        for d in range(1, MAX_WINDOW):
            acc = acc + pad[d:d + s, :]
            if d + 1 in (2, 4, 8, 16):
                snaps.append(acc)
        du_ref[...] = (_window_select(snaps, g) - dpooled).astype(bf16)

    blk = pl.BlockSpec((s, POOL_GROUP), lambda g: (0, g))
    return _call(
        body, name="pool_bwd", grid=(4,),
        in_specs=[blk, blk, blk, pl.BlockSpec((None, POOL_GROUP, POOL_GROUP), lambda g: (g, 0, 0)),
                  pl.BlockSpec((1, POOL_GROUP), lambda g: (0, g))],
        out_specs=[blk, pl.BlockSpec((None, POOL_GROUP, POOL_GROUP), lambda g: (g, 0, 0)),
                   pl.BlockSpec((1, POOL_GROUP), lambda g: (0, g))],
        out_shape=[jax.ShapeDtypeStruct((s, D_POOL), bf16), jax.ShapeDtypeStruct((4, POOL_GROUP, POOL_GROUP), f32),
                   jax.ShapeDtypeStruct((1, D_POOL), f32)],
        scratch=[pltpu.VMEM((s + MAX_WINDOW, POOL_GROUP), f32)])(dpm, mixed, pooled_bf, w_pool, pool_scale)


def _attn_bwd(z, o_bf, lse, do_bf, cos_t, sin_t, nb, cargo):
    s = z.shape[0]

    def body(q_ref, k_ref, v_ref, o_ref, lse_ref, do_ref, cq_ref, sq_ref, cf_ref, sf_ref,
             dq_ref, dk_ref, dv_ref, kt_s, dk_s, dv_s):
        i = pl.program_id(1)

        @pl.when(i == 0)
        def _():
            dk_s[...] = jnp.zeros_like(dk_s)
            dv_s[...] = jnp.zeros_like(dv_s)
            for j in range(nb):
                kt_s[j] = k_ref[j * MOBA_BLOCK:(j + 1) * MOBA_BLOCK, :].T.astype(bf16)

        k_heads = _head_lanes((MOBA_BLOCK, LANES))
        t_heads = _head_rows((LANES, MOBA_BLOCK))

        def query_block(c):
            q = q_ref[...]
            sels = _moba_select(q.astype(bf16), k_ref[...], c, nb)
            qs = (q * SCALE).astype(bf16)
            qraw = [jnp.where(hm, q, 0.0).astype(bf16) for hm in k_heads]
            do = do_ref[...].astype(f32)
            do_bf = do_ref[...]
            dob = [jnp.where(hm, do, 0.0).astype(bf16) for hm in k_heads]
            pick = _head_lanes((STAT_ROWS, LANES))
            r8 = lax.broadcasted_iota(jnp.int32, (STAT_ROWS, LANES), 0)
            head_pick = jnp.where(((r8 == 0) & pick[0]) | ((r8 == 1) & pick[1]), 1.0, 0.0)
            delta8 = lax.dot_general(head_pick, do * o_ref[...].astype(f32), (((1,), (1,)), ((), ())),
                                     precision=lax.Precision.HIGHEST, preferred_element_type=f32)
            lse8 = lse_ref[...]
            delta, lse_h = [delta8[0:1, :], delta8[1:2, :]], [lse8[0:1, :], lse8[1:2, :]]
            key = lax.broadcasted_iota(jnp.int32, (MOBA_BLOCK, MOBA_BLOCK), 0)
            qry = lax.broadcasted_iota(jnp.int32, (MOBA_BLOCK, MOBA_BLOCK), 1)

            def block(j, allow):
                rows = slice(j * MOBA_BLOCK, (j + 1) * MOBA_BLOCK)
                kj, vj, ktj = k_ref[rows, :], v_ref[rows, :], kt_s[j]
                dqt, dk, dv = None, None, None
                for h in range(2):
                    sc = _dot_nt(jnp.where(k_heads[h], kj, 0.0).astype(bf16), qs)
                    p = jnp.where(allow[h], jnp.exp(sc - lse_h[h]), 0.0)
                    dp = _dot_nt(jnp.where(k_heads[h], vj, 0.0).astype(bf16), do_bf)
                    ds = (p * (dp - delta[h]) * SCALE).astype(bf16)
                    tv = _dot(p.astype(bf16), dob[h])
                    tk = _dot(ds, qraw[h])
                    tq = _dot(jnp.where(t_heads[h], ktj, jnp.zeros_like(ktj)), ds)
                    dqt, dk, dv = (tq, tk, tv) if dqt is None else (dqt + tq, dk + tk, dv + tv)
                dk_s[rows, :] += dk
                dv_s[rows, :] += dv
                return dqt

            dqt = block(c, [key <= qry] * 2)
            for j in range(c):
                dqt = dqt + block(j, [sels[h][j:j + 1, :] > 0.0 for h in range(2)])
            dq_ref[...] = _rope_bwd(dqt.T, cq_ref[...], sq_ref[...]).astype(bf16)

        for c in range(nb):
            pl.when(i == c)(functools.partial(query_block, c))

        @pl.when(i == nb - 1)
        def _():
            dk_ref[...] = _rope_bwd(dk_s[...], cf_ref[...], sf_ref[...]).astype(bf16)
            dv_ref[...] = dv_s[...].astype(bf16)

    qblk = pl.BlockSpec((MOBA_BLOCK, LANES), lambda hp, i: (i, hp))
    tq = pl.BlockSpec((MOBA_BLOCK, LANES), lambda hp, i: (i, 0))
    tf = pl.BlockSpec((s, LANES), lambda hp, i: (0, 0))
    colblk = pl.BlockSpec((s, LANES), lambda hp, i: (0, hp))
    return _cargo_call(
        body, cargo, name="attn_bwd", grid=(D_ATTN // LANES, nb),
        in_specs=[qblk, pl.BlockSpec((s, LANES), lambda hp, i: (0, 4 + hp)),
                  pl.BlockSpec((s, LANES), lambda hp, i: (0, 8 + hp)), qblk,
                  pl.BlockSpec((None, STAT_ROWS, MOBA_BLOCK), lambda hp, i: (hp, 0, i)), qblk, tq, tq, tf, tf],
        out_specs=[qblk, colblk, colblk],
        out_shape=[jax.ShapeDtypeStruct((s, D_ATTN), bf16)] * 3,
        scratch=[pltpu.VMEM((nb, LANES, MOBA_BLOCK), bf16), pltpu.VMEM((s, LANES), f32), pltpu.VMEM((s, LANES), f32)],
    )(z, z, z, o_bf, lse, do_bf, cos_t, sin_t, cos_t, sin_t)


def _inproj_dx(dz_bf, win_g, dh1, cargo):
    s = dz_bf.shape[0]
    tm = 256

    def body(dz_ref, w_ref, dh_ref, gx_ref):
        acc = ALPHA * dh_ref[...]
        for sh in range(N_CHIPS):
            acc = acc + _dot_nt(dz_ref[:, sh * D_MODEL:(sh + 1) * D_MODEL], w_ref[sh])
        gx_ref[...] = acc

    row = pl.BlockSpec((tm, D_MODEL), lambda i: (i, 0))
    return _call(
        body, name="inproj_dx", grid=(s // tm,),
        in_specs=[pl.BlockSpec((tm, 4 * D_MODEL), lambda i: (i, 0)),
                  pl.BlockSpec((N_CHIPS, D_MODEL, D_MODEL), lambda i: (0, 0, 0)), row],
        out_specs=[row], out_shape=[jax.ShapeDtypeStruct((s, D_MODEL), f32)], cargo=cargo)(dz_bf, win_g, dh1)


ANY = pl.BlockSpec(memory_space=pl.ANY)


def _chip_index():
    return 2 * lax.axis_index("x") + lax.axis_index("y")


def _peer(k):
    x, y, c = lax.axis_index("x"), lax.axis_index("y"), lax.axis_index("c")
    return (x ^ (k >> 1), y ^ (k & 1), c)


def _sibling():
    return (lax.axis_index("x"), lax.axis_index("y"), 1 - lax.axis_index("c"))


class _GatherCargo:
    def __init__(self, shards, pass_on):
        self.shards, self.pass_on, n = shards, pass_on, len(shards)
        self.inputs = [a.reshape(2, a.shape[0] // 2, a.shape[1]) for a in shards]
        self.out_shape = [jax.ShapeDtypeStruct((N_CHIPS,) + a.shape, a.dtype) for a in self.inputs]
        sem, one = pltpu.SemaphoreType.DMA((N_CHIPS - 1, n)), pltpu.SemaphoreType.DMA((n,))
        self.sems = [sem, sem, sem, sem, one, one]

    def _copies(self, src, dst, sems, a):
        send_sems, recv_sems, fsend_sems, frecv_sems, local_sems, own_sems = sems
        p, c = _chip_index(), lax.axis_index("c")
        own = pltpu.make_async_remote_copy(src_ref=src[a], dst_ref=dst[a].at[p], send_sem=local_sems.at[a],
                                           recv_sem=own_sems.at[a], device_id=_sibling(), device_id_type=MESH)
        out, arrive, onward, landed = [], [], [], []
        for k in range(1, N_CHIPS):
            ici = dict(send_sem=send_sems.at[k - 1, a], recv_sem=recv_sems.at[k - 1, a], device_id=_peer(k),
                       device_id_type=MESH)
            d2d = dict(send_sem=fsend_sems.at[k - 1, a], recv_sem=frecv_sems.at[k - 1, a], device_id=_sibling(),
                       device_id_type=MESH)
            got, theirs = dst[a].at[p ^ k, c], dst[a].at[p ^ k, 1 - c]
            out.append(pltpu.make_async_remote_copy(src_ref=src[a].at[c], dst_ref=dst[a].at[p, c], **ici))
            arrive.append(pltpu.make_async_remote_copy(src_ref=src[a].at[c], dst_ref=got, **ici))
            onward.append(pltpu.make_async_remote_copy(src_ref=got, dst_ref=got, **d2d))
            landed.append(pltpu.make_async_remote_copy(src_ref=theirs, dst_ref=theirs, **d2d))
        return own, out, arrive, onward, landed

    def stages(self, steps):
        n = len(self.shards)

        def start(src, dst, sems):
            for a in range(n):
                own, out, _, _, _ = self._copies(src, dst, sems, a)
                own.start()
                for cp in out:
                    cp.start()

        def pass_on(a):
            def act(src, dst, sems):
                _, _, arrive, onward, _ = self._copies(src, dst, sems, a)
                for k in range(N_CHIPS - 1):
                    arrive[k].wait_recv()
                    onward[k].start()
            return act

        def finish(src, dst, sems):
            for a in range(n):
                own, out, _, onward, landed = self._copies(src, dst, sems, a)
                for cp in landed:
                    cp.wait_recv()
                for cp in out + onward:
                    cp.wait_send()
                own.wait()

        mids = [(min(steps - 1, int(self.pass_on[a] * steps)), pass_on(a)) for a in range(n)]
        return [(0, start)] + mids + [(steps - 1, finish)]

    def results(self, outs):
        return [o.reshape((N_CHIPS,) + a.shape) for o, a in zip(outs, self.shards)]


class _WinGatherCargo:
    def __init__(self, shard):
        self.shard = shard
        self.inputs = [shard.reshape(2, shard.shape[0] // 2, shard.shape[1])]
        self.out_shape = [jax.ShapeDtypeStruct((N_CHIPS,) + self.inputs[0].shape, shard.dtype)]
        sem = pltpu.SemaphoreType.DMA((N_CHIPS - 1,))
        self.sems = [sem, sem, sem, sem, pltpu.SemaphoreType.DMA((2,))]

    def weights(self, src, dst):
        return src[0], dst[0]

    def _copies(self, src, dst, sems):
        send_sems, recv_sems, fsend_sems, frecv_sems, own_sems = sems
        p, c = _chip_index(), lax.axis_index("c")
        own = pltpu.make_async_remote_copy(src_ref=src[0], dst_ref=dst[0].at[p], send_sem=own_sems.at[0],
                                           recv_sem=own_sems.at[1], device_id=_sibling(), device_id_type=MESH)
        out, arrive, onward, landed = [], [], [], []
        for k in range(1, N_CHIPS):
            ici = dict(send_sem=send_sems.at[k - 1], recv_sem=recv_sems.at[k - 1], device_id=_peer(k), device_id_type=MESH)
            d2d = dict(send_sem=fsend_sems.at[k - 1], recv_sem=frecv_sems.at[k - 1], device_id=_sibling(),
                       device_id_type=MESH)
            got, theirs = dst[0].at[p ^ k, c], dst[0].at[p ^ k, 1 - c]
            out.append(pltpu.make_async_remote_copy(src_ref=src[0].at[c], dst_ref=dst[0].at[p, c], **ici))
            arrive.append(pltpu.make_async_remote_copy(src_ref=src[0].at[c], dst_ref=got, **ici))
            onward.append(pltpu.make_async_remote_copy(src_ref=got, dst_ref=got, **d2d))
            landed.append(pltpu.make_async_remote_copy(src_ref=theirs, dst_ref=theirs, **d2d))
        return own, out, arrive, onward, landed

    def stages(self, steps):
        def start(src, dst, sems):
            own, out, _, _, _ = self._copies(src, dst, sems)
            own.start()
            for cp in out:
                cp.start()

        def arrival(k):
            def act(src, dst, sems):
                _, _, arrive, onward, landed = self._copies(src, dst, sems)
                arrive[k - 1].wait_recv()
                onward[k - 1].start()
                landed[k - 1].wait_recv()
            return act

        def finish(src, dst, sems):
            own, out, _, onward, _ = self._copies(src, dst, sems)
            for cp in out + onward:
                cp.wait_send()
            own.wait()

        return ([(0, start)] + [(k * steps // N_CHIPS, arrival(k)) for k in range(1, N_CHIPS)] + [(steps - 1, finish)])

    def results(self, outs):
        return [outs[0].reshape((N_CHIPS,) + self.shard.shape)]


class _ExchangeCargo:
    def __init__(self, sums, whole):
        self.inputs, self.whole, n = sums, whole, len(sums)
        self.out_shape = [jax.ShapeDtypeStruct((N_CHIPS - 1,) + g.shape[1:], g.dtype) for g in sums]
        sem = pltpu.SemaphoreType.DMA((N_CHIPS - 1, n))
        self.sems = [sem, sem]

    def _copies(self, src, dst, sems):
        send_sems, recv_sems = sems
        p = _chip_index()
        return [pltpu.make_async_remote_copy(
            src_ref=src[a].at[0] if self.whole[a] else src[a].at[p ^ k], dst_ref=dst[a].at[k - 1],
            send_sem=send_sems.at[k - 1, a], recv_sem=recv_sems.at[k - 1, a], device_id=_peer(k), device_id_type=MESH)
            for k in range(1, N_CHIPS) for a in range(len(self.inputs))]

    def stages(self, steps):
        def start(src, dst, sems):
            for cp in self._copies(src, dst, sems):
                cp.start()

        def finish(src, dst, sems):
            copies = self._copies(src, dst, sems)
            for cp in copies:
                cp.wait_recv()
            for cp in copies:
                cp.wait_send()

        return [(0, start), (steps - 1, finish)]

    def results(self, outs):
        return list(outs)


class _MultiCargo:
    def __init__(self, cargos):
        self.cargos = cargos
        self.inputs = [a for cg in cargos for a in cg.inputs]
        self.out_shape = [o for cg in cargos for o in cg.out_shape]
        self.sems = [s for cg in cargos for s in cg.sems]

    def _parts(self, refs, count):
        out, off = [], 0
        for cg in self.cargos:
            out.append(refs[off:off + count(cg)])
            off += count(cg)
        return out

    def stages(self, steps):
        merged = []
        for n, cg in enumerate(self.cargos):
            for at, act in cg.stages(steps):
                def part(src, dst, sems, n=n, act=act):
                    act(self._parts(src, lambda g: len(g.inputs))[n], self._parts(dst, lambda g: len(g.out_shape))[n],
                        self._parts(sems, lambda g: len(g.sems))[n])
                merged.append((at, part))
        starts = [m for m in merged if m[0] == 0]
        rest = sorted([m for m in merged if m[0] != 0], key=lambda m: m[0])
        return starts + rest

    def results(self, outs):
        return [cg.results(part) for cg, part in zip(self.cargos, self._parts(list(outs), lambda g: len(g.out_shape)))]


def _cargo_call(body, cargo, *, name, grid, in_specs, out_specs, out_shape, scratch=(), prefetch=None,
                body_sees_cargo=False):
    n_in, n_out, n_scr = len(in_specs), len(out_specs), len(scratch)
    c_in, c_out = len(cargo.inputs), len(cargo.out_shape)
    steps = 1
    for g in grid:
        steps *= g
    stages = cargo.stages(steps)

    def wrapped(*refs):
        head, refs = (refs[:1], refs[1:]) if prefetch is not None else ((), refs)
        ins, refs = refs[:n_in], refs[n_in:]
        cin, refs = refs[:c_in], refs[c_in:]
        outs, refs = refs[:n_out], refs[n_out:]
        cout, refs = refs[:c_out], refs[c_out:]
        scr, sems = refs[:n_scr], refs[n_scr:]
        if not grid:
            for _, act in stages:
                act(cin, cout, sems)
            return
        step = 0
        for d in range(len(grid)):
            step = step * grid[d] + pl.program_id(d)
        for at, act in stages:
            if at < steps - 1:
                pl.when(step == at)(functools.partial(act, cin, cout, sems))
        body(*head, *ins, *outs, *scr, *(((cin, cout, sems),) if body_sees_cargo else ()))
        for at, act in stages:
            if at == steps - 1:
                pl.when(step == at)(functools.partial(act, cin, cout, sems))

    params = dict(vmem_limit_bytes=VMEM_LIMIT)
    if grid:
        params["dimension_semantics"] = ("arbitrary",) * len(grid)
    specs = dict(grid=grid, in_specs=list(in_specs) + [ANY] * c_in, out_specs=list(out_specs) + [ANY] * c_out,
                 scratch_shapes=list(scratch) + cargo.sems)
    if prefetch is not None:
        specs = dict(grid_spec=pltpu.PrefetchScalarGridSpec(num_scalar_prefetch=1, **specs))
    res = pl.pallas_call(
        wrapped, name=name, out_shape=list(out_shape) + cargo.out_shape,
        compiler_params=pltpu.CompilerParams(**params), **specs)
    head = () if prefetch is None else (prefetch,)
    return lambda *args: (lambda r: (r[:n_out], cargo.results(r[n_out:])))(res(*head, *args, *cargo.inputs))


def _split_halves(g):
    return g.reshape(g.shape[0], 2, g.shape[1] // 2, g.shape[2])


def _ew_tile(rows):
    return 256 if rows % 256 == 0 else rows


def _prefetch_call(body, *, name, grid, in_specs, out_specs, out_shape, **kw):
    return pl.pallas_call(
        body, name=name, out_shape=out_shape,
        grid_spec=pltpu.PrefetchScalarGridSpec(num_scalar_prefetch=1, grid=grid, in_specs=in_specs, out_specs=out_specs),
        compiler_params=pltpu.CompilerParams(dimension_semantics=("arbitrary",) * len(grid),
                                             vmem_limit_bytes=VMEM_LIMIT), **kw)


def _presum_swap(split, name):
    n = len(split)

    def body(*refs):
        src, got = refs[:n], refs[n:2 * n]
        send_sems, recv_sems = refs[2 * n:]
        c = lax.axis_index("c")
        remote = [pltpu.make_async_remote_copy(src_ref=src[a].at[:, 1 - c], dst_ref=got[a],
                                               send_sem=send_sems.at[a], recv_sem=recv_sems.at[a],
                                               device_id=_sibling(), device_id_type=MESH) for a in range(n)]
        for cp in remote:
            cp.start()
        for cp in remote:
            cp.wait_recv()
        for cp in remote:
            cp.wait_send()

    sem = pltpu.SemaphoreType.DMA((n,))
    return pl.pallas_call(
        body, name=name, in_specs=[ANY] * n, out_specs=[ANY] * n,
        out_shape=[jax.ShapeDtypeStruct((g.shape[0],) + g.shape[2:], g.dtype) for g in split],
        scratch_shapes=[sem, sem])(*split)


def _add_pair(mine, theirs, place, name):
    lead, _, rows, cols = mine.shape
    tm = _ew_tile(rows)

    def body(place_ref, a_ref, b_ref, o_ref):
        o_ref[...] = (a_ref[...].astype(f32) + b_ref[...].astype(f32)).astype(o_ref.dtype)

    blk = pl.BlockSpec((lead, tm, cols), lambda i, pc: (0, i, 0))
    return _prefetch_call(
        body, name=name, grid=(rows // tm,),
        in_specs=[pl.BlockSpec((lead, None, tm, cols), lambda i, pc: (0, pc[1], i, 0)), blk], out_specs=blk,
        out_shape=jax.ShapeDtypeStruct(theirs.shape, theirs.dtype))(place, mine, theirs)


class _NoCargo:
    def __init__(self, results=()):
        self.inputs, self.out_shape, self.sems, self._results = [], [], [], list(results)

    def stages(self, steps):
        return [(0, lambda src, dst, sems: None), (steps - 1, lambda src, dst, sems: None)]

    def results(self, outs):
        return self._results


def _cargo_alone(cargo, name):
    return _cargo_call(None, cargo, name=name, grid=(), in_specs=[], out_specs=[], out_shape=[])()[1]


def _sum_slots(own, recv, place, whole, name):
    _, rows, cols = own.shape
    tm = _ew_tile(rows)

    def body(place_ref, own_ref, r_ref, o_ref):
        r0, r1, r2 = [r_ref[k].astype(f32) for k in range(N_CHIPS - 1)]
        o_ref[...] = (own_ref[...].astype(f32) + r0) + (r1 + r2)

    return _prefetch_call(
        body, name=name, grid=(rows // tm,),
        in_specs=[pl.BlockSpec((None, tm, cols), lambda i, pc: (0 if whole else pc[0], i, 0)),
                  pl.BlockSpec((N_CHIPS - 1, tm, cols), lambda i, pc: (0, i, 0))],
        out_specs=pl.BlockSpec((None, tm, cols), lambda i, pc: (pc[1], i, 0)),
        out_shape=jax.ShapeDtypeStruct((2, rows, cols), f32))(place, own, recv)


def _sibling_fill(bufs):
    n = len(bufs)

    def body(*refs):
        src, dst = refs[:n], refs[n:2 * n]
        send_sems, recv_sems = refs[2 * n:]
        c = lax.axis_index("c")
        remote = [pltpu.make_async_remote_copy(src_ref=src[a].at[c], dst_ref=dst[a].at[c], send_sem=send_sems.at[a],
                                               recv_sem=recv_sems.at[a], device_id=_sibling(), device_id_type=MESH)
                  for a in range(n)]
        for cp in remote:
            cp.start()
        for a in range(n):
            pltpu.make_async_remote_copy(src_ref=src[a].at[c], dst_ref=dst[a].at[1 - c], send_sem=send_sems.at[a],
                                         recv_sem=recv_sems.at[a], device_id=_sibling(), device_id_type=MESH).wait_recv()
        for cp in remote:
            cp.wait_send()

    sem = pltpu.SemaphoreType.DMA((n,))
    out = pl.pallas_call(
        body, name="sibling_fill", in_specs=[ANY] * n, out_specs=[ANY] * n,
        out_shape=[jax.ShapeDtypeStruct(b.shape, b.dtype) for b in bufs],
        input_output_aliases={a: a for a in range(n)}, scratch_shapes=[sem, sem])(*bufs)
    return [o.reshape(2 * b.shape[1], b.shape[2]) for o, b in zip(out, bufs)]


def _adamw(w, m, v, grad, name, tm):
    rows, cols = w.shape
    gcols = grad.shape[1]
    if tm == rows:
        gspec = pl.BlockSpec(grad.shape, lambda i: (0, 0))
    else:
        gspec = pl.BlockSpec((tm, gcols), lambda i: (i, 0))

    def body(w_ref, m_ref, v_ref, gin_ref, g_ref, d_ref, nm_ref, nv_ref):
        g = gin_ref[0:tm, 0:cols]
        w_, m_, v_ = w_ref[...], m_ref[...], v_ref[...]
        m_ = ADAM_B1 * m_ + (1.0 - ADAM_B1) * g
        v_ = ADAM_B2 * v_ + (1.0 - ADAM_B2) * jnp.square(g)
        m_hat = m_ / (1.0 - ADAM_B1 ** ADAM_STEP)
        v_hat = v_ / (1.0 - ADAM_B2 ** ADAM_STEP)
        g_ref[...] = g
        d_ref[...] = -ADAM_LR * (m_hat / (jnp.sqrt(v_hat) + ADAM_EPS) + ADAM_WD * w_)
        nm_ref[...] = m_
        nv_ref[...] = v_

    blk = pl.BlockSpec((tm, cols), lambda i: (i, 0))
    return _call(
        body, name=name, grid=(rows // tm,),
        in_specs=[blk, blk, blk, gspec], out_specs=[blk] * 4,
        out_shape=[jax.ShapeDtypeStruct((rows, cols), f32)] * 4)(w, m, v, grad)


def _rope_tables(s):
    half = HEAD_DIM // 2
    inv_freq = 1.0 / (10000.0 ** (jnp.arange(half, dtype=f32) / half))
    ang = jnp.arange(s, dtype=f32)[:, None] * inv_freq[None, :]
    cos, sin = jnp.cos(ang), jnp.sin(ang)
    return jnp.tile(cos, (1, LANES // half)), jnp.tile(jnp.concatenate([-sin, sin], axis=1), (1, LANES // HEAD_DIM))


def _local_step(x, target, place, gather, exchange, b_gate, w_pool, pool_scale, ln1_g, ln1_b, convb_g, ln2_g, ln2_b):
    s = x.shape[0]
    nb = s // MOBA_BLOCK
    cos_t, sin_t = _rope_tables(s)
    x_bf = x.astype(bf16)

    z, win_g, _ = _inproj(x_bf, gather["w_in"], cos_t, sin_t, b_gate, place, _NoCargo())
    (o_bf, lse), (wba_g, wbp_g, wout_g, wg_g, convw_g) = _attn_fwd(z, nb, gather["attn_fwd"])
    wba = wba_g.transpose(1, 0, 2).reshape(D_ATTN, D_MODEL)
    wbp = wbp_g.transpose(1, 0, 2).reshape(D_POOL, D_MODEL)
    wout = wout_g.reshape(D_MODEL, D_MODEL)
    (pooled_bf, mixed, pm_bf), up_head = _pool_fwd(z, w_pool, pool_scale, gather["pool_fwd"])
    (m_bf,), up_mid = _branch_merge(o_bf, pm_bf, z, wba, wbp, gather["branch_merge"])
    (x1_bf, xhat1, rstd1), up_tail = _out_ln1(m_bf, wout, x, ln1_g, ln1_b, gather["out_ln1"])
    wu_g = jnp.concatenate(up_head + up_mid + up_tail, axis=1)
    (a, up, ac, hh_bf), (wd_g,) = _ffn_up(x1_bf, wg_g, wu_g, convw_g, convb_g, gather["ffn_up"])
    wd = wd_g.reshape(D_FF_PAD, D_MODEL)
    loss, dh2, dh2_bf, d_ln2_g, d_ln2_b = _ffn_down_ln2_loss(hh_bf, wd, xhat1, ln1_g, ln1_b, ln2_g, ln2_b, target)

    da_bf, dup_bf, dconv = _ffn_act_bwd(dh2_bf, wd_g, ac, a, up, convw_g)
    d_wd = _matmul_tn(hh_bf, dh2_bf, 1, "dw_ffn_down").reshape(N_CHIPS, FF_PAD, D_MODEL)
    d_wg = _matmul_tn(da_bf, x1_bf, 1, "dw_ffn_gate").reshape(N_CHIPS, FF_PAD, D_MODEL)
    d_wu = _matmul_tn(dup_bf, x1_bf, 1, "dw_ffn_up").reshape(N_CHIPS, FF_PAD, D_MODEL)
    (dh1, dh1_bf, d_ln1_g, d_ln1_b), got_down = _ffn_in_bwd_ln1(
        da_bf, dup_bf, wg_g, wu_g, dh2, xhat1, rstd1, ln1_g, exchange("ffn_in_bwd_ln1", [("w_ffn_down", d_wd)]))
    d_wout = _matmul_tn(m_bf, dh1_bf, 1, "dw_out").reshape(N_CHIPS, D_MODEL // N_CHIPS, D_MODEL)
    dzg_bf, dya_bf, dyp_bf, do_bf, dpm, d_bgate = _merge_bwd(dh1_bf, wout, o_bf, pm_bf, z, wba, wbp)
    d_wba = _matmul_tn(o_bf, dya_bf, N_CHIPS, "dw_branch_attn")
    d_wbp = _matmul_tn(pm_bf, dyp_bf, N_CHIPS, "dw_branch_pool")
    du_bf, d_wpool, d_pscale = _pool_bwd(dpm, mixed, pooled_bf, w_pool, pool_scale)
    small = _pack_small([d_bgate, d_wpool, d_pscale, d_ln1_g, d_ln1_b, d_ln2_g, d_ln2_b, dconv[:, 3, :],
                         loss[0:1, :]])[None]
    (dq_bf, dk_bf, dv_bf), got_ffn = _attn_bwd(
        z, o_bf, lse, do_bf, cos_t, sin_t, nb,
        exchange("attn_bwd", [("w_ffn_gate", d_wg), ("w_ffn_up", d_wu), ("conv_w", dconv), ("small", small)]))
    dz_bf = jnp.concatenate([dq_bf, dk_bf, dv_bf, du_bf, dzg_bf], axis=1)
    d_win, got_branch = _matmul_tn(
        x_bf, dz_bf, N_CHIPS, "dw_in",
        cargo=exchange("dw_in", [("w_out", d_wout), ("w_branch_attn", d_wba), ("w_branch_pool", d_wbp)]))
    (grad_x,), got_in = _inproj_dx(dz_bf, win_g, dh1, exchange("inproj_dx", [("w_in", d_win)]))
    return grad_x, got_down + got_ffn + got_branch + got_in


SMALL_ROWS = 592


def _pack_small(parts):
    flat = jnp.concatenate([p.reshape(-1) for p in parts])
    return jnp.pad(flat, (0, SMALL_ROWS * LANES - flat.shape[0])).reshape(SMALL_ROWS, LANES)


def _unpack_small(packed, shapes):
    flat = packed.reshape(-1)
    out, off = [], 0
    for shp in shapes:
        n = 1
        for d in shp:
            n *= d
        out.append(flat[off:off + n].reshape(shp))
        off += n
    return out


def _pad_conv_b(cb):
    return jnp.pad(cb.reshape(N_CHIPS, FF_SHARD), ((0, 0), (0, FF_PAD - FF_SHARD)))


def kernel(x, w_in, b_gate, w_branch_attn, w_pool, pool_scale, w_branch_pool, w_out, ln1_g, ln1_b, w_ffn_gate, w_ffn_up, conv_w, conv_b, w_ffn_down, ln2_g, ln2_b, loss_target, m_w_in, m_b_gate, m_w_branch_attn, m_w_pool, m_pool_scale, m_w_branch_pool, m_w_out, m_ln1_g, m_ln1_b, m_w_ffn_gate, m_w_ffn_up, m_conv_w, m_conv_b, m_w_ffn_down, m_ln2_g, m_ln2_b, v_w_in, v_b_gate, v_w_branch_attn, v_w_pool, v_pool_scale, v_w_branch_pool, v_w_out, v_ln1_g, v_ln1_b, v_w_ffn_gate, v_w_ffn_up, v_conv_w, v_conv_b, v_w_ffn_down, v_ln2_g, v_ln2_b):
    pad_rows = lambda w: jnp.pad(w.astype(bf16), ((0, FF_PAD - FF_SHARD), (0, 0)))
    shards = [w_in[0].astype(bf16), w_branch_attn[0].astype(bf16), w_branch_pool[0].astype(bf16),
              w_out[0].astype(bf16), pad_rows(w_ffn_gate[0].T), pad_rows(w_ffn_up[0].T), pad_rows(w_ffn_down[0]),
              jnp.pad(conv_w[0], ((0, CONV_ROWS - 3), (0, FF_PAD - FF_SHARD)))]
    up_pieces = [shards[5][r:r + FF_TILE] for r in range(0, FF_PAD, FF_TILE)]
    gather = {"w_in": _WinGatherCargo(shards[0]),
              "attn_fwd": _GatherCargo(shards[1:5] + shards[7:], [0.15, 0.25, 0.45, 0.95, 0.95]),
              "pool_fwd": _GatherCargo(up_pieces[0:1], [0.75]), "branch_merge": _GatherCargo(up_pieces[1:2], [0.75]),
              "out_ln1": _GatherCargo(up_pieces[2:3], [0.75]), "ffn_up": _GatherCargo(shards[6:7], [0.8])}

    place = jnp.stack([2 * lax.axis_index("x") + lax.axis_index("y"), lax.axis_index("c")]).astype(jnp.int32)
    chip_sums = {}

    def exchange(carrier, named):
        split = [_split_halves(g) for _, g in named]
        got = _presum_swap(split, "presum_swap_" + carrier)
        sums = [_add_pair(a, b, place, "presum_" + nm) for a, b, (nm, _) in zip(split, got, named)]
        chip_sums.update({nm: s for (nm, _), s in zip(named, sums)})
        return _ExchangeCargo(sums, [nm == "small" for nm, _ in named])

    convb_g = _pad_conv_b(conv_b).reshape(N_CHIPS, 1, FF_PAD)
    grad_x, brought = _local_step(
        x[0], loss_target[0], place, gather, exchange, b_gate, w_pool[0], pool_scale, ln1_g, ln1_b, convb_g, ln2_g, ln2_b)
    slots = dict(zip(chip_sums, brought))
    names = ["w_in", "w_branch_attn", "w_branch_pool", "w_out", "w_ffn_gate", "w_ffn_up", "w_ffn_down", "conv_w", "small"]
    grads = _sibling_fill([_sum_slots(chip_sums[nm], slots[nm], place, nm == "small", "sum_" + nm) for nm in names])

    weights = [w_in[0], w_branch_attn[0], w_branch_pool[0], w_out[0], w_ffn_gate[0].T, w_ffn_up[0].T, w_ffn_down[0], conv_w[0]]
    m_in = [m_w_in[0], m_w_branch_attn[0], m_w_branch_pool[0], m_w_out[0], m_w_ffn_gate[0].T, m_w_ffn_up[0].T, m_w_ffn_down[0], m_conv_w[0]]
    v_in = [v_w_in[0], v_w_branch_attn[0], v_w_branch_pool[0], v_w_out[0], v_w_ffn_gate[0].T, v_w_ffn_up[0].T, v_w_ffn_down[0], v_conv_w[0]]
    tiles = [256, 256, 256, 128, 88, 88, 88, 3]
    res = {}
    for i, nm in enumerate(names[:-1]):
        outs4 = _adamw(weights[i], m_in[i], v_in[i], grads[i], "adamw_" + nm, tiles[i])
        res[nm] = [(r.T if nm in ("w_ffn_gate", "w_ffn_up") else r)[None] for r in outs4]
    loss = grads[-1][SMALL_ROWS - 4, 0]

    small_names = ["b_gate", "w_pool", "pool_scale", "ln1_g", "ln1_b", "ln2_g", "ln2_b", "conv_b"]
    small_w = [b_gate, w_pool, pool_scale, ln1_g, ln1_b, ln2_g, ln2_b, _pad_conv_b(conv_b)]
    small_m = [m_b_gate, m_w_pool, m_pool_scale, m_ln1_g, m_ln1_b, m_ln2_g, m_ln2_b, _pad_conv_b(m_conv_b)]
    small_v = [v_b_gate, v_w_pool, v_pool_scale, v_ln1_g, v_ln1_b, v_ln2_g, v_ln2_b, _pad_conv_b(v_conv_b)]
    packed = _adamw(_pack_small(small_w), _pack_small(small_m), _pack_small(small_v), grads[-1],
                    "adamw_small", SMALL_ROWS)
    shapes = [w.shape for w in small_w]
    for kind in range(4):
        for nm, val in zip(small_names, _unpack_small(packed[kind], shapes)):
            if nm == "conv_b":
                val = val[:, :FF_SHARD].reshape(1, 4 * FF_SHARD)
            res.setdefault(nm, [None] * 4)[kind] = val

    order = ["w_in", "b_gate", "w_branch_attn", "w_pool", "pool_scale", "w_branch_pool", "w_out", "ln1_g", "ln1_b",
             "w_ffn_gate", "w_ffn_up", "conv_w", "conv_b", "w_ffn_down", "ln2_g", "ln2_b"]
    outs = [loss, grad_x[None]]
    for kind in range(4):
        outs += [res[nm][kind] for nm in order]
    return tuple(outs)
```

```python
import functools

import jax
import jax.numpy as jnp
from jax import lax
from jax.experimental import pallas as pl
from jax.experimental.pallas import tpu as pltpu

D_MODEL = 1024
HEAD_DIM = 64
D_ATTN = 512
D_POOL = 512
MOBA_BLOCK = 256
MOBA_TOPK = 3
POOL_GROUP = 128
MAX_WINDOW = 16
FF_SHARD = 704
FF_PAD = 768
D_FF_PAD = 4 * FF_PAD
N_CHIPS = 4
LANES = 128
ALPHA = (2.0 * 1) ** 0.25
LN_EPS = 1e-5
NEG = -1e30
SCALE = HEAD_DIM ** -0.5
ADAM_LR, ADAM_B1, ADAM_B2, ADAM_EPS, ADAM_WD, ADAM_STEP = 0.001, 0.9, 0.999, 1e-08, 0.01, 10
VMEM_LIMIT = 56 * 1024 * 1024
MESH = pl.DeviceIdType.MESH

bf16 = jnp.bfloat16
f32 = jnp.float32


def _dot(a, b):
    return jnp.dot(a, b, preferred_element_type=f32)


def _dot_nt(a, b):
    return lax.dot_general(a, b, (((1,), (1,)), ((), ())), preferred_element_type=f32)


def _dot_tn(a, b):
    return lax.dot_general(a, b, (((0,), (0,)), ((), ())), preferred_element_type=f32)


def _call(body, *, name, grid, in_specs, out_specs, out_shape, scratch=(), cargo=None):
    if cargo is not None:
        return _cargo_call(body, cargo, name=name, grid=grid, in_specs=in_specs, out_specs=out_specs,
                           out_shape=out_shape, scratch=scratch)
    return pl.pallas_call(
        body, name=name, grid=grid, in_specs=in_specs, out_specs=out_specs, out_shape=out_shape,
        scratch_shapes=list(scratch),
        compiler_params=pltpu.CompilerParams(dimension_semantics=("arbitrary",) * len(grid),
                                             vmem_limit_bytes=VMEM_LIMIT))


def _rot_half(t):
    lane = lax.broadcasted_iota(jnp.int32, t.shape, 1)
    first = (lane % HEAD_DIM) < (HEAD_DIM // 2)
    return jnp.where(first, pltpu.roll(t, LANES - HEAD_DIM // 2, 1), pltpu.roll(t, HEAD_DIM // 2, 1))


def _rope(t, cos, sin_signed):
    return t * cos + _rot_half(t) * sin_signed


def _rope_bwd(d, cos, sin_signed):
    return d * cos + _rot_half(d * sin_signed)


def _gelu_parts(a):
    cdf = 0.5 * (1.0 + lax.erf(a * (2.0 ** -0.5)))
    pdf = jnp.exp(-0.5 * a * a) * ((2.0 * jnp.pi) ** -0.5)
    return a * cdf, cdf + a * pdf


def _layer_norm(h, g, b):
    mu = jnp.mean(h, axis=-1, keepdims=True)
    xc = h - mu
    var = jnp.mean(xc * xc, axis=-1, keepdims=True)
    rstd = lax.rsqrt(var + LN_EPS)
    xhat = xc * rstd
    return xhat * g + b, xhat, rstd


def _layer_norm_bwd(dy, xhat, rstd, g):
    dxh = dy * g
    m1 = jnp.mean(dxh, axis=-1, keepdims=True)
    m2 = jnp.mean(dxh * xhat, axis=-1, keepdims=True)
    return rstd * (dxh - m1 - xhat * m2)


def _inproj(x_bf, win_g, cos_t, sin_t, b_gate, cargo):
    s = x_bf.shape[0]
    tm, tn = 1024, 512

    def body(x_ref, w_ref, cos_ref, sin_ref, b_ref, o_ref):
        j = pl.program_id(1)
        acc = _dot(x_ref[...], w_ref[...])

        @pl.when(j < 2)
        def _():
            for c in range(tn // LANES):
                sl = slice(c * LANES, (c + 1) * LANES)
                o_ref[:, sl] = _rope(acc[:, sl], cos_ref[...], sin_ref[...])

        @pl.when((j >= 2) & (j < 4))
        def _():
            o_ref[...] = acc

        @pl.when(j >= 4)
        def _():
            o_ref[...] = jax.nn.sigmoid(acc + b_ref[...])

    (z,), carried = _cargo_call(
        body, cargo, name="inproj", grid=(s // tm, 4 * D_MODEL // tn),
        in_specs=[pl.BlockSpec((tm, D_MODEL), lambda i, j: (i, 0)),
                  pl.BlockSpec((None, D_MODEL, tn), lambda i, j: (j // 2, 0, j % 2)),
                  pl.BlockSpec((tm, LANES), lambda i, j: (i, 0)),
                  pl.BlockSpec((tm, LANES), lambda i, j: (i, 0)),
                  pl.BlockSpec((1, tn), lambda i, j: (0, jnp.maximum(j - 4, 0)))],
        out_specs=[pl.BlockSpec((tm, tn), lambda i, j: (i, j))],
        out_shape=[jax.ShapeDtypeStruct((s, 4 * D_MODEL), f32)])(x_bf, win_g, cos_t, sin_t, b_gate)
    return z, carried


STAT_ROWS = 8


def _pair_rows(v0, v1, fill):
    r = lax.broadcasted_iota(jnp.int32, (STAT_ROWS, v0.shape[1]), 0)
    return jnp.where(r == 0, v0, jnp.where(r == 1, v1, fill))


def _head_lanes(shape):
    lane = lax.broadcasted_iota(jnp.int32, shape, 1)
    return lane < HEAD_DIM, lane >= HEAD_DIM


def _head_rows(shape):
    row = lax.broadcasted_iota(jnp.int32, shape, 0)
    return row < HEAD_DIM, row >= HEAD_DIM


def _moba_select(q_bf, k_all, i, nb):
    k_mean = jnp.mean(k_all.reshape(nb, MOBA_BLOCK, LANES), axis=1)
    n_io = lax.broadcasted_iota(jnp.int32, (nb, MOBA_BLOCK), 0)
    past = n_io < i
    sels = []
    for head in _head_lanes((nb, LANES)):
        gate = _dot_nt(jnp.where(head, k_mean, 0.0).astype(bf16), q_bf)
        g = jnp.where(past, gate, NEG)
        rank = jnp.zeros((nb, MOBA_BLOCK), f32)
        for m in range(nb):
            gm = g[m:m + 1, :]
            rank = rank + jnp.where((gm > g) | ((gm == g) & (m < n_io)), 1.0, 0.0)
        sels.append(jnp.where(past & (rank < MOBA_TOPK), 1.0, 0.0))
    return sels


def _attn_fwd(z, nb, cargo):
    s = z.shape[0]

    def body(q_ref, k_ref, v_ref, o_ref, lse_ref, vt_s):
        i = pl.program_id(1)

        @pl.when(i == 0)
        def _():
            for j in range(nb):
                vt_s[j] = v_ref[j * MOBA_BLOCK:(j + 1) * MOBA_BLOCK, :].T.astype(bf16)

        k_heads = _head_lanes((MOBA_BLOCK, LANES))
        o_heads = _head_rows((LANES, MOBA_BLOCK))

        def block(j, qs, allow, m_old, l_old):
            kj, vtj = k_ref[j * MOBA_BLOCK:(j + 1) * MOBA_BLOCK, :], vt_s[j]
            m_new, alpha, l_new, pv = [], [], [], None
            for h in range(2):
                sc = jnp.where(allow[h], _dot_nt(jnp.where(k_heads[h], kj, 0.0).astype(bf16), qs), NEG)
                mx = jnp.max(sc, axis=0, keepdims=True)
                mn = mx if m_old is None else jnp.maximum(m_old[h], mx)
                p = jnp.exp(sc - mn)
                lsum = jnp.sum(p, axis=0, keepdims=True)
                if m_old is None:
                    a = None
                else:
                    a = jnp.exp(m_old[h] - mn)
                    lsum = a * l_old[h] + lsum
                t = _dot(jnp.where(o_heads[h], vtj, jnp.zeros_like(vtj)), p.astype(bf16))
                pv = t if pv is None else pv + t
                m_new.append(mn)
                alpha.append(a)
                l_new.append(lsum)
            return m_new, alpha, l_new, pv

        def query_block(c):
            q = q_ref[...]
            sels = _moba_select(q.astype(bf16), k_ref[...], c, nb)
            qs = (q * SCALE).astype(bf16)
            key = lax.broadcasted_iota(jnp.int32, (MOBA_BLOCK, MOBA_BLOCK), 0)
            qry = lax.broadcasted_iota(jnp.int32, (MOBA_BLOCK, MOBA_BLOCK), 1)
            m, _, l, acc = block(c, qs, [key <= qry] * 2, None, None)
            for j in range(c):
                m, alpha, l, pv = block(j, qs, [sels[h][j:j + 1, :] > 0.0 for h in range(2)], m, l)
                acc = acc * jnp.where(o_heads[0], alpha[0], alpha[1]) + pv
            o_ref[...] = (acc / jnp.where(o_heads[0], l[0], l[1])).T.astype(bf16)
            lse_ref[...] = _pair_rows(m[0] + jnp.log(l[0]), m[1] + jnp.log(l[1]), 0.0)

        for c in range(nb):
            pl.when(i == c)(functools.partial(query_block, c))

    return _cargo_call(
        body, cargo, name="attn_fwd", grid=(D_ATTN // LANES, nb),
        in_specs=[pl.BlockSpec((MOBA_BLOCK, LANES), lambda hp, i: (i, hp)),
                  pl.BlockSpec((s, LANES), lambda hp, i: (0, 4 + hp)),
                  pl.BlockSpec((s, LANES), lambda hp, i: (0, 8 + hp))],
        out_specs=[pl.BlockSpec((MOBA_BLOCK, LANES), lambda hp, i: (i, hp)),
                   pl.BlockSpec((None, STAT_ROWS, MOBA_BLOCK), lambda hp, i: (hp, 0, i))],
        out_shape=[jax.ShapeDtypeStruct((s, D_ATTN), bf16),
                   jax.ShapeDtypeStruct((D_ATTN // LANES, STAT_ROWS, s), f32)],
        scratch=[pltpu.VMEM((nb, LANES, MOBA_BLOCK), bf16)])(z, z, z)


def _window_select(snaps, g):
    return jnp.where(g == 0, snaps[0], jnp.where(g == 1, snaps[1], jnp.where(g == 2, snaps[2], snaps[3])))


def _pool_fwd(z, w_pool, pool_scale):
    s = z.shape[0]

    def body(u_ref, w_ref, sc_ref, pooled_ref, mixed_ref, pm_ref, pad):
        g = pl.program_id(0)
        u = u_ref[...]
        pad[0:MAX_WINDOW, :] = jnp.zeros((MAX_WINDOW, POOL_GROUP), f32)
        pad[MAX_WINDOW:MAX_WINDOW + s, :] = u
        acc = u
        snaps = []
        for d in range(1, MAX_WINDOW):
            acc = acc + pad[MAX_WINDOW - d:MAX_WINDOW - d + s, :]
            if d + 1 in (2, 4, 8, 16):
                snaps.append(acc)
        win = _window_select(snaps, g)
        t = lax.broadcasted_iota(jnp.int32, (s, POOL_GROUP), 0)
        count = jnp.minimum(t + 1, jnp.left_shift(2, g)).astype(f32)
        pooled = (win / count - u).astype(bf16)
        mixed = _dot(pooled, w_ref[...].astype(bf16))
        pooled_ref[...] = pooled
        mixed_ref[...] = mixed
        pm_ref[...] = (mixed * sc_ref[...]).astype(bf16)

    blk = pl.BlockSpec((s, POOL_GROUP), lambda g: (0, g))
    return _call(
        body, name="pool_fwd", grid=(4,),
        in_specs=[pl.BlockSpec((s, POOL_GROUP), lambda g: (0, 12 + g)),
                  pl.BlockSpec((None, POOL_GROUP, POOL_GROUP), lambda g: (g, 0, 0)),
                  pl.BlockSpec((1, POOL_GROUP), lambda g: (0, g))],
        out_specs=[blk, blk, blk],
        out_shape=[jax.ShapeDtypeStruct((s, D_POOL), bf16), jax.ShapeDtypeStruct((s, D_POOL), f32),
                   jax.ShapeDtypeStruct((s, D_POOL), bf16)],
        scratch=[pltpu.VMEM((s + MAX_WINDOW, POOL_GROUP), f32)])(z, w_pool, pool_scale)


def _branch_merge(o_bf, pm_bf, z, wba, wbp):
    s = o_bf.shape[0]
    tm = 512

    def body(o_ref, pm_ref, ga_ref, gp_ref, wba_ref, wbp_ref, m_ref):
        ya = _dot(o_ref[...], wba_ref[...])
        yp = _dot(pm_ref[...], wbp_ref[...])
        m_ref[...] = (ga_ref[...] * ya + gp_ref[...] * yp).astype(bf16)

    full = lambda r, c: pl.BlockSpec((r, c), lambda i: (0, 0))
    return _call(
        body, name="branch_merge", grid=(s // tm,),
        in_specs=[pl.BlockSpec((tm, D_ATTN), lambda i: (i, 0)), pl.BlockSpec((tm, D_POOL), lambda i: (i, 0)),
                  pl.BlockSpec((tm, D_MODEL), lambda i: (i, 2)), pl.BlockSpec((tm, D_MODEL), lambda i: (i, 3)),
                  full(D_ATTN, D_MODEL), full(D_POOL, D_MODEL)],
        out_specs=pl.BlockSpec((tm, D_MODEL), lambda i: (i, 0)),
        out_shape=jax.ShapeDtypeStruct((s, D_MODEL), bf16))(o_bf, pm_bf, z, z, wba, wbp)


def _out_ln1(m_bf, wout, x, ln_g, ln_b):
    s = x.shape[0]
    tm = 512

    def body(m_ref, w_ref, x_ref, g_ref, b_ref, x1_ref, xhat_ref, rstd_ref):
        h = ALPHA * x_ref[...] + _dot(m_ref[...], w_ref[...])
        y, xhat, rstd = _layer_norm(h, g_ref[...], b_ref[...])
        x1_ref[...] = y.astype(bf16)
        xhat_ref[...] = xhat
        rstd_ref[...] = jnp.broadcast_to(rstd, (tm, LANES))

    row = pl.BlockSpec((tm, D_MODEL), lambda i: (i, 0))
    vec = pl.BlockSpec((1, D_MODEL), lambda i: (0, 0))
    return _call(
        body, name="out_ln1", grid=(s // tm,),
        in_specs=[row, pl.BlockSpec((D_MODEL, D_MODEL), lambda i: (0, 0)), row, vec, vec],
        out_specs=[row, row, pl.BlockSpec((tm, LANES), lambda i: (i, 0))],
        out_shape=[jax.ShapeDtypeStruct((s, D_MODEL), bf16), jax.ShapeDtypeStruct((s, D_MODEL), f32),
                   jax.ShapeDtypeStruct((s, LANES), f32)])(m_bf, wout, x, ln_g, ln_b)


FF_TILE = 256
FF_TILES_PER_SHARD = FF_PAD // FF_TILE
CONV_PAD = 8
CONV_ROWS = 16

def _ff_weight_spec(rows):
    return pl.BlockSpec((None, rows, FF_TILE), lambda j: (j // FF_TILES_PER_SHARD, 0, j % FF_TILES_PER_SHARD))


def _ff_rows_spec():
    return pl.BlockSpec((None, FF_TILE, D_MODEL), lambda j: (j // FF_TILES_PER_SHARD, j % FF_TILES_PER_SHARD, 0))


def _row_shifts(x, shifts):
    s = x.shape[0]
    padded = jnp.concatenate([x, jnp.zeros((CONV_PAD, x.shape[1]), x.dtype)], axis=0)
    return [pltpu.roll(padded, d % (s + CONV_PAD), 0)[0:s] for d in shifts]


def _ffn_up(x1_bf, wg_g, wu_g, convw_g, convb_g, cargo):
    s = x1_bf.shape[0]

    def body(x_ref, wg_ref, wu_ref, cw_ref, cb_ref, a_ref, up_ref, ac_ref, hh_ref):
        x1 = x_ref[...]
        a = _dot_nt(x1, wg_ref[...])
        up = _dot_nt(x1, wu_ref[...])
        a_ref[...] = a
        up_ref[...] = up
        a1, a2 = _row_shifts(a, (1, 2))
        ac = a * cw_ref[2:3, :] + a1 * cw_ref[1:2, :] + a2 * cw_ref[0:1, :] + cb_ref[...]
        ac_ref[...] = ac
        hg, _ = _gelu_parts(ac)
        hh_ref[...] = (hg * up).astype(bf16)

    col = pl.BlockSpec((s, FF_TILE), lambda j: (0, j))
    wide = jax.ShapeDtypeStruct((s, D_FF_PAD), f32)
    return _call(
        body, name="ffn_up", grid=(D_FF_PAD // FF_TILE,),
        in_specs=[pl.BlockSpec((s, D_MODEL), lambda j: (0, 0)), _ff_rows_spec(), _ff_rows_spec(),
                  _ff_weight_spec(8), _ff_weight_spec(1)],
        out_specs=[col, col, col, col],
        out_shape=[wide, wide, wide, jax.ShapeDtypeStruct((s, D_FF_PAD), bf16)],
        cargo=cargo)(x1_bf, wg_g, wu_g, convw_g, convb_g)


def _ffn_down_ln2_loss(hh_bf, wd, xhat1, ln1_g, ln1_b, ln2_g, ln2_b, target):
    s = hh_bf.shape[0]
    tm = 256

    def body(hh_ref, w_ref, xh1_ref, g1_ref, b1_ref, g2_ref, b2_ref, t_ref, loss_ref, dh_ref, dhb_ref, dg_ref, db_ref):
        i = pl.program_id(0)
        x1 = xh1_ref[...] * g1_ref[...] + b1_ref[...]
        h = ALPHA * x1 + _dot(hh_ref[...], w_ref[...])
        y, xhat, rstd = _layer_norm(h, g2_ref[...], b2_ref[...])
        err = y - t_ref[...]
        part = 0.5 * jnp.sum(jnp.mean(err * err, axis=-1, keepdims=True), axis=0, keepdims=True)
        dy = err * (1.0 / D_MODEL)

        @pl.when(i == 0)
        def _():
            loss_ref[...] = jnp.zeros_like(loss_ref)
            dg_ref[...] = jnp.zeros_like(dg_ref)
            db_ref[...] = jnp.zeros_like(db_ref)

        loss_ref[...] += jnp.broadcast_to(part, loss_ref.shape)
        dg_ref[...] += jnp.sum(dy * xhat, axis=0, keepdims=True)
        db_ref[...] += jnp.sum(dy, axis=0, keepdims=True)
        dh = _layer_norm_bwd(dy, xhat, rstd, g2_ref[...])
        dh_ref[...] = dh
        dhb_ref[...] = dh.astype(bf16)

    row = pl.BlockSpec((tm, D_MODEL), lambda i: (i, 0))
    vec = pl.BlockSpec((1, D_MODEL), lambda i: (0, 0))
    return _call(
        body, name="ffn_down_ln2_loss", grid=(s // tm,),
        in_specs=[pl.BlockSpec((tm, D_FF_PAD), lambda i: (i, 0)),
                  pl.BlockSpec((D_FF_PAD, D_MODEL), lambda i: (0, 0)),
                  row, vec, vec, vec, vec, row],
        out_specs=[pl.BlockSpec((8, LANES), lambda i: (0, 0)), row, row, vec, vec],
        out_shape=[jax.ShapeDtypeStruct((8, LANES), f32), jax.ShapeDtypeStruct((s, D_MODEL), f32),
                   jax.ShapeDtypeStruct((s, D_MODEL), bf16), jax.ShapeDtypeStruct((1, D_MODEL), f32),
                   jax.ShapeDtypeStruct((1, D_MODEL), f32)])(hh_bf, wd, xhat1, ln1_g, ln1_b, ln2_g, ln2_b, target)


def _ffn_act_bwd(dh2_bf, wd_g, ac, a, up, convw_g):
    s = dh2_bf.shape[0]

    def body(dh_ref, wd_ref, ac_ref, a_ref, up_ref, cw_ref, da_ref, dup_ref, dc_ref):
        dhh = _dot_nt(dh_ref[...], wd_ref[...])
        hg, dgelu = _gelu_parts(ac_ref[...])
        dup_ref[...] = (dhh * hg).astype(bf16)
        dac = dhh * up_ref[...] * dgelu
        dac1, dac2 = _row_shifts(dac, (-1, -2))
        a = a_ref[...]
        rows = [jnp.sum(d * a, axis=0, keepdims=True) for d in (dac2, dac1, dac)]
        rows.append(jnp.sum(dac, axis=0, keepdims=True))
        rows.append(jnp.zeros((CONV_ROWS - 4, FF_TILE), f32))
        dc_ref[...] = jnp.concatenate(rows, axis=0)
        da_ref[...] = (dac * cw_ref[2:3, :] + dac1 * cw_ref[1:2, :] + dac2 * cw_ref[0:1, :]).astype(bf16)

    col = pl.BlockSpec((s, FF_TILE), lambda j: (0, j))
    return _call(
        body, name="ffn_act_bwd", grid=(D_FF_PAD // FF_TILE,),
        in_specs=[pl.BlockSpec((s, D_MODEL), lambda j: (0, 0)), _ff_rows_spec(), col, col, col, _ff_weight_spec(8)],
        out_specs=[col, col, _ff_weight_spec(CONV_ROWS)],
        out_shape=[jax.ShapeDtypeStruct((s, D_FF_PAD), bf16), jax.ShapeDtypeStruct((s, D_FF_PAD), bf16),
                   jax.ShapeDtypeStruct((N_CHIPS, CONV_ROWS, FF_PAD), f32)],
    )(dh2_bf, wd_g, ac, a, up, convw_g)


def _matmul_tn(a, b, n_shards, name, tm=512, tn=1024, cargo=None):
    k, m = a.shape
    n = b.shape[1]
    tm, tn = min(tm, m), min(tn, n // n_shards)
    per = n // n_shards // tn

    def body(a_ref, b_ref, o_ref, at_s):
        @pl.when(pl.program_id(1) == 0)
        def _():
            at_s[...] = a_ref[...].T

        o_ref[...] = _dot(at_s[...], b_ref[...]).astype(bf16)

    res = _call(
        body, name=name, grid=(m // tm, n // tn),
        in_specs=[pl.BlockSpec((k, tm), lambda i, j: (0, i)), pl.BlockSpec((k, tn), lambda i, j: (0, j))],
        out_specs=[pl.BlockSpec((None, tm, tn), lambda i, j: (j // per, i, j % per))],
        out_shape=[jax.ShapeDtypeStruct((n_shards, m, n // n_shards), bf16)], scratch=[pltpu.VMEM((tm, k), bf16)],
        cargo=cargo)(a, b)
    return res[0] if cargo is None else (res[0][0], res[1])


def _ffn_in_bwd_ln1(da_bf, dup_bf, wg_g, wu_g, dh2, xhat1, rstd1, ln1_g, cargo):
    s = da_bf.shape[0]
    tm = 256

    def body(da_ref, dup_ref, wg_ref, wu_ref, dh2_ref, xh_ref, rstd_ref, g_ref, dh_ref, dhb_ref, dg_ref, db_ref):
        i = pl.program_id(0)
        dx1 = ALPHA * dh2_ref[...]
        for sh in range(N_CHIPS):
            sl = slice(sh * FF_PAD, (sh + 1) * FF_PAD)
            dx1 = dx1 + _dot(da_ref[:, sl], wg_ref[sh]) + _dot(dup_ref[:, sl], wu_ref[sh])
        xhat = xh_ref[...]

        @pl.when(i == 0)
        def _():
            dg_ref[...] = jnp.zeros_like(dg_ref)
            db_ref[...] = jnp.zeros_like(db_ref)

        dg_ref[...] += jnp.sum(dx1 * xhat, axis=0, keepdims=True)
        db_ref[...] += jnp.sum(dx1, axis=0, keepdims=True)
        dh = _layer_norm_bwd(dx1, xhat, rstd_ref[:, 0:1], g_ref[...])
        dh_ref[...] = dh
        dhb_ref[...] = dh.astype(bf16)

    row = pl.BlockSpec((tm, D_MODEL), lambda i: (i, 0))
    wide = pl.BlockSpec((tm, D_FF_PAD), lambda i: (i, 0))
    wfull = pl.BlockSpec((N_CHIPS, FF_PAD, D_MODEL), lambda i: (0, 0, 0))
    vec = pl.BlockSpec((1, D_MODEL), lambda i: (0, 0))
    return _call(
        body, name="ffn_in_bwd_ln1", grid=(s // tm,),
        in_specs=[wide, wide, wfull, wfull, row, row, pl.BlockSpec((tm, LANES), lambda i: (i, 0)), vec],
        out_specs=[row, row, vec, vec],
        out_shape=[jax.ShapeDtypeStruct((s, D_MODEL), f32), jax.ShapeDtypeStruct((s, D_MODEL), bf16),
                   jax.ShapeDtypeStruct((1, D_MODEL), f32), jax.ShapeDtypeStruct((1, D_MODEL), f32)],
        cargo=cargo)(da_bf, dup_bf, wg_g, wu_g, dh2, xhat1, rstd1, ln1_g)


def _merge_bwd(dh1_bf, wout, o_bf, pm_bf, z, wba, wbp):
    s = dh1_bf.shape[0]
    tm = 256

    def body(dh_ref, wout_ref, o_ref, pm_ref, ga_ref, gp_ref, wba_ref, wbp_ref,
             dzg_ref, dya_ref, dyp_ref, do_ref, dpm_ref, dbg_ref):
        i = pl.program_id(0)
        dm = _dot_nt(dh_ref[...], wout_ref[...])
        ya = _dot(o_ref[...], wba_ref[...])
        yp = _dot(pm_ref[...], wbp_ref[...])
        ga, gp = ga_ref[...], gp_ref[...]
        dza = dm * ya * ga * (1.0 - ga)
        dzp = dm * yp * gp * (1.0 - gp)

        @pl.when(i == 0)
        def _():
            dbg_ref[...] = jnp.zeros_like(dbg_ref)

        dbg_ref[:, 0:D_MODEL] += jnp.sum(dza, axis=0, keepdims=True)
        dbg_ref[:, D_MODEL:2 * D_MODEL] += jnp.sum(dzp, axis=0, keepdims=True)
        dzg_ref[:, 0:D_MODEL] = dza.astype(bf16)
        dzg_ref[:, D_MODEL:2 * D_MODEL] = dzp.astype(bf16)
        dya = (dm * ga).astype(bf16)
        dyp = (dm * gp).astype(bf16)
        dya_ref[...] = dya
        dyp_ref[...] = dyp
        do_ref[...] = _dot_nt(dya, wba_ref[...]).astype(bf16)
        dpm_ref[...] = _dot_nt(dyp, wbp_ref[...])

    row = pl.BlockSpec((tm, D_MODEL), lambda i: (i, 0))
    half = pl.BlockSpec((tm, D_ATTN), lambda i: (i, 0))
    full = lambda r, c: pl.BlockSpec((r, c), lambda i: (0, 0))
    return _call(
        body, name="merge_bwd", grid=(s // tm,),
        in_specs=[row, full(D_MODEL, D_MODEL), half, half,
                  pl.BlockSpec((tm, D_MODEL), lambda i: (i, 2)), pl.BlockSpec((tm, D_MODEL), lambda i: (i, 3)),
                  full(D_ATTN, D_MODEL), full(D_POOL, D_MODEL)],
        out_specs=[pl.BlockSpec((tm, 2 * D_MODEL), lambda i: (i, 0)), row, row, half, half,
                   pl.BlockSpec((1, 2 * D_MODEL), lambda i: (0, 0))],
        out_shape=[jax.ShapeDtypeStruct((s, 2 * D_MODEL), bf16), jax.ShapeDtypeStruct((s, D_MODEL), bf16),
                   jax.ShapeDtypeStruct((s, D_MODEL), bf16), jax.ShapeDtypeStruct((s, D_ATTN), bf16),
                   jax.ShapeDtypeStruct((s, D_POOL), f32), jax.ShapeDtypeStruct((1, 2 * D_MODEL), f32)],
    )(dh1_bf, wout, o_bf, pm_bf, z, z, wba, wbp)


def _pool_bwd(dpm, mixed, pooled_bf, w_pool, pool_scale):
    s = dpm.shape[0]

    def body(dpm_ref, mixed_ref, pooled_ref, w_ref, sc_ref, du_ref, dw_ref, dsc_ref, pad):
        g = pl.program_id(0)
        dpm_v = dpm_ref[...]
        dsc_ref[...] = jnp.sum(dpm_v * mixed_ref[...], axis=0, keepdims=True)
        dmixed = (dpm_v * sc_ref[...]).astype(bf16)
        dw_ref[...] = _dot_tn(pooled_ref[...], dmixed)
        dpooled = _dot_nt(dmixed, w_ref[...].astype(bf16))
        t = lax.broadcasted_iota(jnp.int32, (s, POOL_GROUP), 0)
        count = jnp.minimum(t + 1, jnp.left_shift(2, g)).astype(f32)
        r = dpooled / count
        pad[0:s, :] = r
        pad[s:s + MAX_WINDOW, :] = jnp.zeros((MAX_WINDOW, POOL_GROUP), f32)
        acc = r
        snaps = []
        for d in range(1, MAX_WINDOW):
            acc = acc + pad[d:d + s, :]
            if d + 1 in (2, 4, 8, 16):
                snaps.append(acc)
        du_ref[...] = (_window_select(snaps, g) - dpooled).astype(bf16)

    blk = pl.BlockSpec((s, POOL_GROUP), lambda g: (0, g))
    return _call(
        body, name="pool_bwd", grid=(4,),
        in_specs=[blk, blk, blk, pl.BlockSpec((None, POOL_GROUP, POOL_GROUP), lambda g: (g, 0, 0)),
                  pl.BlockSpec((1, POOL_GROUP), lambda g: (0, g))],
        out_specs=[blk, pl.BlockSpec((None, POOL_GROUP, POOL_GROUP), lambda g: (g, 0, 0)),
                   pl.BlockSpec((1, POOL_GROUP), lambda g: (0, g))],
        out_shape=[jax.ShapeDtypeStruct((s, D_POOL), bf16), jax.ShapeDtypeStruct((4, POOL_GROUP, POOL_GROUP), f32),
                   jax.ShapeDtypeStruct((1, D_POOL), f32)],
        scratch=[pltpu.VMEM((s + MAX_WINDOW, POOL_GROUP), f32)])(dpm, mixed, pooled_bf, w_pool, pool_scale)


def _attn_bwd(z, o_bf, lse, do_bf, cos_t, sin_t, nb, cargo):
    s = z.shape[0]

    def body(q_ref, k_ref, v_ref, o_ref, lse_ref, do_ref, cq_ref, sq_ref, cf_ref, sf_ref,
             dq_ref, dk_ref, dv_ref, kt_s, dk_s, dv_s):
        i = pl.program_id(1)

        @pl.when(i == 0)
        def _():
            dk_s[...] = jnp.zeros_like(dk_s)
            dv_s[...] = jnp.zeros_like(dv_s)
            for j in range(nb):
                kt_s[j] = k_ref[j * MOBA_BLOCK:(j + 1) * MOBA_BLOCK, :].T.astype(bf16)

        k_heads = _head_lanes((MOBA_BLOCK, LANES))
        t_heads = _head_rows((LANES, MOBA_BLOCK))

        def query_block(c):
            q = q_ref[...]
            sels = _moba_select(q.astype(bf16), k_ref[...], c, nb)
            qs = (q * SCALE).astype(bf16)
            qraw = [jnp.where(hm, q, 0.0).astype(bf16) for hm in k_heads]
            do = do_ref[...].astype(f32)
            do_bf = do_ref[...]
            dob = [jnp.where(hm, do, 0.0).astype(bf16) for hm in k_heads]
            pick = _head_lanes((STAT_ROWS, LANES))
            r8 = lax.broadcasted_iota(jnp.int32, (STAT_ROWS, LANES), 0)
            head_pick = jnp.where(((r8 == 0) & pick[0]) | ((r8 == 1) & pick[1]), 1.0, 0.0)
            delta8 = lax.dot_general(head_pick, do * o_ref[...].astype(f32), (((1,), (1,)), ((), ())),
                                     precision=lax.Precision.HIGHEST, preferred_element_type=f32)
            lse8 = lse_ref[...]
            delta, lse_h = [delta8[0:1, :], delta8[1:2, :]], [lse8[0:1, :], lse8[1:2, :]]
            key = lax.broadcasted_iota(jnp.int32, (MOBA_BLOCK, MOBA_BLOCK), 0)
            qry = lax.broadcasted_iota(jnp.int32, (MOBA_BLOCK, MOBA_BLOCK), 1)

            def block(j, allow):
                rows = slice(j * MOBA_BLOCK, (j + 1) * MOBA_BLOCK)
                kj, vj, ktj = k_ref[rows, :], v_ref[rows, :], kt_s[j]
                dqt, dk, dv = None, None, None
                for h in range(2):
                    sc = _dot_nt(jnp.where(k_heads[h], kj, 0.0).astype(bf16), qs)
                    p = jnp.where(allow[h], jnp.exp(sc - lse_h[h]), 0.0)
                    dp = _dot_nt(jnp.where(k_heads[h], vj, 0.0).astype(bf16), do_bf)
                    ds = (p * (dp - delta[h]) * SCALE).astype(bf16)
                    tv = _dot(p.astype(bf16), dob[h])
                    tk = _dot(ds, qraw[h])
                    tq = _dot(jnp.where(t_heads[h], ktj, jnp.zeros_like(ktj)), ds)
                    dqt, dk, dv = (tq, tk, tv) if dqt is None else (dqt + tq, dk + tk, dv + tv)
                dk_s[rows, :] += dk
                dv_s[rows, :] += dv
                return dqt

            dqt = block(c, [key <= qry] * 2)
            for j in range(c):
                dqt = dqt + block(j, [sels[h][j:j + 1, :] > 0.0 for h in range(2)])
            dq_ref[...] = _rope_bwd(dqt.T, cq_ref[...], sq_ref[...]).astype(bf16)

        for c in range(nb):
            pl.when(i == c)(functools.partial(query_block, c))

        @pl.when(i == nb - 1)
        def _():
            dk_ref[...] = _rope_bwd(dk_s[...], cf_ref[...], sf_ref[...]).astype(bf16)
            dv_ref[...] = dv_s[...].astype(bf16)

    qblk = pl.BlockSpec((MOBA_BLOCK, LANES), lambda hp, i: (i, hp))
    tq = pl.BlockSpec((MOBA_BLOCK, LANES), lambda hp, i: (i, 0))
    tf = pl.BlockSpec((s, LANES), lambda hp, i: (0, 0))
    colblk = pl.BlockSpec((s, LANES), lambda hp, i: (0, hp))
    return _cargo_call(
        body, cargo, name="attn_bwd", grid=(D_ATTN // LANES, nb),
        in_specs=[qblk, pl.BlockSpec((s, LANES), lambda hp, i: (0, 4 + hp)),
                  pl.BlockSpec((s, LANES), lambda hp, i: (0, 8 + hp)), qblk,
                  pl.BlockSpec((None, STAT_ROWS, MOBA_BLOCK), lambda hp, i: (hp, 0, i)), qblk, tq, tq, tf, tf],
        out_specs=[qblk, colblk, colblk],
        out_shape=[jax.ShapeDtypeStruct((s, D_ATTN), bf16)] * 3,
        scratch=[pltpu.VMEM((nb, LANES, MOBA_BLOCK), bf16), pltpu.VMEM((s, LANES), f32), pltpu.VMEM((s, LANES), f32)],
    )(z, z, z, o_bf, lse, do_bf, cos_t, sin_t, cos_t, sin_t)


def _inproj_dx(dz_bf, win_g, dh1, cargo):
    s = dz_bf.shape[0]
    tm = 256

    def body(dz_ref, w_ref, dh_ref, gx_ref):
        acc = ALPHA * dh_ref[...]
        for sh in range(N_CHIPS):
            acc = acc + _dot_nt(dz_ref[:, sh * D_MODEL:(sh + 1) * D_MODEL], w_ref[sh])
        gx_ref[...] = acc

    row = pl.BlockSpec((tm, D_MODEL), lambda i: (i, 0))
    return _call(
        body, name="inproj_dx", grid=(s // tm,),
        in_specs=[pl.BlockSpec((tm, 4 * D_MODEL), lambda i: (i, 0)),
                  pl.BlockSpec((N_CHIPS, D_MODEL, D_MODEL), lambda i: (0, 0, 0)), row],
        out_specs=[row], out_shape=[jax.ShapeDtypeStruct((s, D_MODEL), f32)], cargo=cargo)(dz_bf, win_g, dh1)


ANY = pl.BlockSpec(memory_space=pl.ANY)


def _chip_index():
    return 2 * lax.axis_index("x") + lax.axis_index("y")


def _peer(k):
    x, y, c = lax.axis_index("x"), lax.axis_index("y"), lax.axis_index("c")
    return (x ^ (k >> 1), y ^ (k & 1), c)


def _sibling():
    return (lax.axis_index("x"), lax.axis_index("y"), 1 - lax.axis_index("c"))


class _GatherCargo:
    def __init__(self, shards, pass_on):
        self.shards, self.pass_on, n = shards, pass_on, len(shards)
        self.inputs = [a.reshape(2, a.shape[0] // 2, a.shape[1]) for a in shards]
        self.out_shape = [jax.ShapeDtypeStruct((N_CHIPS,) + a.shape, a.dtype) for a in self.inputs]
        sem, one = pltpu.SemaphoreType.DMA((N_CHIPS - 1, n)), pltpu.SemaphoreType.DMA((n,))
        self.sems = [sem, sem, sem, sem, one, one]

    def _copies(self, src, dst, sems, a):
        send_sems, recv_sems, fsend_sems, frecv_sems, local_sems, own_sems = sems
        p, c = _chip_index(), lax.axis_index("c")
        own = pltpu.make_async_remote_copy(src_ref=src[a], dst_ref=dst[a].at[p], send_sem=local_sems.at[a],
                                           recv_sem=own_sems.at[a], device_id=_sibling(), device_id_type=MESH)
        out, arrive, onward, landed = [], [], [], []
        for k in range(1, N_CHIPS):
            ici = dict(send_sem=send_sems.at[k - 1, a], recv_sem=recv_sems.at[k - 1, a], device_id=_peer(k),
                       device_id_type=MESH)
            d2d = dict(send_sem=fsend_sems.at[k - 1, a], recv_sem=frecv_sems.at[k - 1, a], device_id=_sibling(),
                       device_id_type=MESH)
            got, theirs = dst[a].at[p ^ k, c], dst[a].at[p ^ k, 1 - c]
            out.append(pltpu.make_async_remote_copy(src_ref=src[a].at[c], dst_ref=dst[a].at[p, c], **ici))
            arrive.append(pltpu.make_async_remote_copy(src_ref=src[a].at[c], dst_ref=got, **ici))
            onward.append(pltpu.make_async_remote_copy(src_ref=got, dst_ref=got, **d2d))
            landed.append(pltpu.make_async_remote_copy(src_ref=theirs, dst_ref=theirs, **d2d))
        return own, out, arrive, onward, landed

    def stages(self, steps):
        n = len(self.shards)

        def start(src, dst, sems):
            for a in range(n):
                own, out, _, _, _ = self._copies(src, dst, sems, a)
                own.start()
                for cp in out:
                    cp.start()

        def pass_on(a):
            def act(src, dst, sems):
                _, _, arrive, onward, _ = self._copies(src, dst, sems, a)
                for k in range(N_CHIPS - 1):
                    arrive[k].wait_recv()
                    onward[k].start()
            return act

        def finish(src, dst, sems):
            for a in range(n):
                own, out, _, onward, landed = self._copies(src, dst, sems, a)
                for cp in landed:
                    cp.wait_recv()
                for cp in out + onward:
                    cp.wait_send()
                own.wait()

        mids = [(min(steps - 1, int(self.pass_on[a] * steps)), pass_on(a)) for a in range(n)]
        return [(0, start)] + mids + [(steps - 1, finish)]

    def results(self, outs):
        return [o.reshape((N_CHIPS,) + a.shape) for o, a in zip(outs, self.shards)]


class _ExchangeCargo:
    def __init__(self, sums, whole):
        self.inputs, self.whole, n = sums, whole, len(sums)
        self.out_shape = [jax.ShapeDtypeStruct((N_CHIPS - 1,) + g.shape[1:], g.dtype) for g in sums]
        sem = pltpu.SemaphoreType.DMA((N_CHIPS - 1, n))
        self.sems = [sem, sem]

    def _copies(self, src, dst, sems):
        send_sems, recv_sems = sems
        p = _chip_index()
        return [pltpu.make_async_remote_copy(
            src_ref=src[a].at[0] if self.whole[a] else src[a].at[p ^ k], dst_ref=dst[a].at[k - 1],
            send_sem=send_sems.at[k - 1, a], recv_sem=recv_sems.at[k - 1, a], device_id=_peer(k), device_id_type=MESH)
            for k in range(1, N_CHIPS) for a in range(len(self.inputs))]

    def stages(self, steps):
        def start(src, dst, sems):
            for cp in self._copies(src, dst, sems):
                cp.start()

        def finish(src, dst, sems):
            copies = self._copies(src, dst, sems)
            for cp in copies:
                cp.wait_recv()
            for cp in copies:
                cp.wait_send()

        return [(0, start), (steps - 1, finish)]

    def results(self, outs):
        return list(outs)


def _cargo_call(body, cargo, *, name, grid, in_specs, out_specs, out_shape, scratch=()):
    n_in, n_out, n_scr = len(in_specs), len(out_specs), len(scratch)
    c_in, c_out = len(cargo.inputs), len(cargo.out_shape)
    steps = 1
    for g in grid:
        steps *= g
    stages = cargo.stages(steps)

    def wrapped(*refs):
        ins, refs = refs[:n_in], refs[n_in:]
        cin, refs = refs[:c_in], refs[c_in:]
        outs, refs = refs[:n_out], refs[n_out:]
        cout, refs = refs[:c_out], refs[c_out:]
        scr, sems = refs[:n_scr], refs[n_scr:]
        if not grid:
            for _, act in stages:
                act(cin, cout, sems)
            return
        step = 0
        for d in range(len(grid)):
            step = step * grid[d] + pl.program_id(d)
        pl.when(step == 0)(functools.partial(stages[0][1], cin, cout, sems))
        body(*ins, *outs, *scr)
        for at, act in stages[1:]:
            pl.when(step == at)(functools.partial(act, cin, cout, sems))

    params = dict(vmem_limit_bytes=VMEM_LIMIT)
    if grid:
        params["dimension_semantics"] = ("arbitrary",) * len(grid)
    res = pl.pallas_call(
        wrapped, name=name, grid=grid, in_specs=list(in_specs) + [ANY] * c_in, out_specs=list(out_specs) + [ANY] * c_out,
        out_shape=list(out_shape) + cargo.out_shape, scratch_shapes=list(scratch) + cargo.sems,
        compiler_params=pltpu.CompilerParams(**params))
    return lambda *args: (lambda r: (r[:n_out], cargo.results(r[n_out:])))(res(*args, *cargo.inputs))


def _split_halves(g):
    return g.reshape(g.shape[0], 2, g.shape[1] // 2, g.shape[2])


def _ew_tile(rows):
    return 128 if rows % 128 == 0 else rows


def _batched_call(items, name, place=None):
    offs, total = [], 0
    for it in items:
        offs.append(total)
        total += it[0]
    in_specs, out_specs, out_shape, args, spans = [], [], [], [], []
    for (n, ins, outs, shapes, operands, _), off in zip(items, offs):
        tile = lambda t, off=off, n=n: jnp.clip(t - off, 0, n - 1)
        i_s, o_s = ins(tile), outs(tile)
        spans.append((len(i_s), len(o_s)))
        in_specs += i_s
        out_specs += o_s
        out_shape += shapes
        args += operands
    n_in = len(in_specs)

    def body(*refs):
        if place is not None:
            refs = refs[1:]
        t = pl.program_id(0)
        i0, o0 = 0, n_in
        for (n, _, _, _, _, compute), off, (ni, no) in zip(items, offs, spans):
            pl.when((t >= off) & (t < off + n))(functools.partial(compute, refs[i0:i0 + ni], refs[o0:o0 + no]))
            i0, o0 = i0 + ni, o0 + no

    params = pltpu.CompilerParams(dimension_semantics=("arbitrary",), vmem_limit_bytes=VMEM_LIMIT)
    if place is None:
        res = pl.pallas_call(body, name=name, grid=(total,), in_specs=in_specs, out_specs=out_specs,
                             out_shape=out_shape, compiler_params=params)(*args)
    else:
        res = pl.pallas_call(
            body, name=name, out_shape=out_shape, compiler_params=params,
            grid_spec=pltpu.PrefetchScalarGridSpec(num_scalar_prefetch=1, grid=(total,), in_specs=in_specs,
                                                   out_specs=out_specs))(place, *args)
    out, o0 = [], 0
    for _, no in spans:
        out.append(res[o0:o0 + no])
        o0 += no
    return out


def _presum_swap(split, name):
    n = len(split)

    def body(*refs):
        src, got = refs[:n], refs[n:2 * n]
        send_sems, recv_sems = refs[2 * n:]
        c = lax.axis_index("c")
        remote = [pltpu.make_async_remote_copy(src_ref=src[a].at[:, 1 - c], dst_ref=got[a],
                                               send_sem=send_sems.at[a], recv_sem=recv_sems.at[a],
                                               device_id=_sibling(), device_id_type=MESH) for a in range(n)]
        for cp in remote:
            cp.start()
        for cp in remote:
            cp.wait_recv()
        for cp in remote:
            cp.wait_send()

    sem = pltpu.SemaphoreType.DMA((n,))
    return pl.pallas_call(
        body, name=name, in_specs=[ANY] * n, out_specs=[ANY] * n,
        out_shape=[jax.ShapeDtypeStruct((g.shape[0],) + g.shape[2:], g.dtype) for g in split],
        scratch_shapes=[sem, sem])(*split)


def _add_pairs(pairs, place, name):
    def item(mine, theirs):
        lead, _, rows, cols = mine.shape
        tm = _ew_tile(rows)

        def compute(ins, outs):
            outs[0][...] = (ins[0][...].astype(f32) + ins[1][...].astype(f32)).astype(outs[0].dtype)

        blk = lambda tile: pl.BlockSpec((lead, tm, cols), lambda t, pc: (0, tile(t), 0))
        mine_half = lambda tile: pl.BlockSpec((lead, None, tm, cols), lambda t, pc: (0, pc[1], tile(t), 0))
        return (rows // tm, lambda tile: [mine_half(tile), blk(tile)], lambda tile: [blk(tile)],
                [jax.ShapeDtypeStruct(theirs.shape, theirs.dtype)], [mine, theirs], compute)

    return [out[0] for out in _batched_call([item(a, b) for a, b in pairs], name, place)]


class _NoCargo:
    def __init__(self, results=()):
        self.inputs, self.out_shape, self.sems, self._results = [], [], [], list(results)

    def stages(self, steps):
        return [(0, lambda src, dst, sems: None), (steps - 1, lambda src, dst, sems: None)]

    def results(self, outs):
        return self._results


def _cargo_alone(cargo, name):
    return _cargo_call(None, cargo, name=name, grid=(), in_specs=[], out_specs=[], out_shape=[])()[1]


def _sum_slots(jobs, place, name):
    def item(own, recv, whole):
        _, rows, cols = own.shape
        tm = _ew_tile(rows)

        def compute(ins, outs):
            r0, r1, r2 = [ins[1][k].astype(f32) for k in range(N_CHIPS - 1)]
            outs[0][...] = (ins[0][...].astype(f32) + r0) + (r1 + r2)

        ins = lambda tile: [pl.BlockSpec((None, tm, cols), lambda t, pc: (0 if whole else pc[0], tile(t), 0)),
                            pl.BlockSpec((N_CHIPS - 1, tm, cols), lambda t, pc: (0, tile(t), 0))]
        outs = lambda tile: [pl.BlockSpec((None, tm, cols), lambda t, pc: (pc[1], tile(t), 0))]
        return rows // tm, ins, outs, [jax.ShapeDtypeStruct((2, rows, cols), f32)], [own, recv], compute

    return [out[0] for out in _batched_call([item(*job) for job in jobs], name, place)]


def _sibling_fill(bufs):
    n = len(bufs)

    def body(*refs):
        src, dst = refs[:n], refs[n:2 * n]
        send_sems, recv_sems = refs[2 * n:]
        c = lax.axis_index("c")
        remote = [pltpu.make_async_remote_copy(src_ref=src[a].at[c], dst_ref=dst[a].at[c], send_sem=send_sems.at[a],
                                               recv_sem=recv_sems.at[a], device_id=_sibling(), device_id_type=MESH)
                  for a in range(n)]
        for cp in remote:
            cp.start()
        for a in range(n):
            pltpu.make_async_remote_copy(src_ref=src[a].at[c], dst_ref=dst[a].at[1 - c], send_sem=send_sems.at[a],
                                         recv_sem=recv_sems.at[a], device_id=_sibling(), device_id_type=MESH).wait_recv()
        for cp in remote:
            cp.wait_send()

    sem = pltpu.SemaphoreType.DMA((n,))
    out = pl.pallas_call(
        body, name="sibling_fill", in_specs=[ANY] * n, out_specs=[ANY] * n,
        out_shape=[jax.ShapeDtypeStruct(b.shape, b.dtype) for b in bufs],
        input_output_aliases={a: a for a in range(n)}, scratch_shapes=[sem, sem])(*bufs)
    return [o.reshape(2 * b.shape[1], b.shape[2]) for o, b in zip(out, bufs)]


def _adamw(jobs, name):
    def item(w, m, v, grad, tm):
        rows, cols = w.shape
        gcols = grad.shape[1]

        def compute(ins, outs):
            w_, m_, v_ = ins[0][...], ins[1][...], ins[2][...]
            g = ins[3][0:tm, 0:cols]
            m_ = ADAM_B1 * m_ + (1.0 - ADAM_B1) * g
            v_ = ADAM_B2 * v_ + (1.0 - ADAM_B2) * jnp.square(g)
            m_hat = m_ / (1.0 - ADAM_B1 ** ADAM_STEP)
            v_hat = v_ / (1.0 - ADAM_B2 ** ADAM_STEP)
            outs[0][...] = g
            outs[1][...] = -ADAM_LR * (m_hat / (jnp.sqrt(v_hat) + ADAM_EPS) + ADAM_WD * w_)
            outs[2][...] = m_
            outs[3][...] = v_

        blk = lambda tile: pl.BlockSpec((tm, cols), lambda t: (tile(t), 0))
        if tm == rows:
            gspec = lambda tile: pl.BlockSpec(grad.shape, lambda t: (0, 0))
        else:
            gspec = lambda tile: pl.BlockSpec((tm, gcols), lambda t: (tile(t), 0))
        return (rows // tm, lambda tile: [blk(tile)] * 3 + [gspec(tile)], lambda tile: [blk(tile)] * 4,
                [jax.ShapeDtypeStruct((rows, cols), f32)] * 4, [w, m, v, grad], compute)

    return _batched_call([item(*job) for job in jobs], name)


def _rope_tables(s):
    half = HEAD_DIM // 2
    inv_freq = 1.0 / (10000.0 ** (jnp.arange(half, dtype=f32) / half))
    ang = jnp.arange(s, dtype=f32)[:, None] * inv_freq[None, :]
    cos, sin = jnp.cos(ang), jnp.sin(ang)
    return jnp.tile(cos, (1, LANES // half)), jnp.tile(jnp.concatenate([-sin, sin], axis=1), (1, LANES // HEAD_DIM))


def _local_step(x, target, win_g, gather, exchange, b_gate, w_pool, pool_scale, ln1_g, ln1_b, convb_g, ln2_g, ln2_b):
    s = x.shape[0]
    nb = s // MOBA_BLOCK
    cos_t, sin_t = _rope_tables(s)
    x_bf = x.astype(bf16)

    z, (wba_g, wbp_g, wout_g) = _inproj(x_bf, win_g, cos_t, sin_t, b_gate, gather["inproj"])
    wba = wba_g.transpose(1, 0, 2).reshape(D_ATTN, D_MODEL)
    wbp = wbp_g.transpose(1, 0, 2).reshape(D_POOL, D_MODEL)
    wout = wout_g.reshape(D_MODEL, D_MODEL)
    (o_bf, lse), (wg_g, wu_g, convw_g) = _attn_fwd(z, nb, gather["attn_fwd"])
    pooled_bf, mixed, pm_bf = _pool_fwd(z, w_pool, pool_scale)
    m_bf = _branch_merge(o_bf, pm_bf, z, wba, wbp)
    x1_bf, xhat1, rstd1 = _out_ln1(m_bf, wout, x, ln1_g, ln1_b)
    (a, up, ac, hh_bf), (wd_g,) = _ffn_up(x1_bf, wg_g, wu_g, convw_g, convb_g, gather["ffn_up"])
    wd = wd_g.reshape(D_FF_PAD, D_MODEL)
    loss, dh2, dh2_bf, d_ln2_g, d_ln2_b = _ffn_down_ln2_loss(hh_bf, wd, xhat1, ln1_g, ln1_b, ln2_g, ln2_b, target)

    da_bf, dup_bf, dconv = _ffn_act_bwd(dh2_bf, wd_g, ac, a, up, convw_g)
    d_wd = _matmul_tn(hh_bf, dh2_bf, 1, "dw_ffn_down").reshape(N_CHIPS, FF_PAD, D_MODEL)
    d_wg = _matmul_tn(da_bf, x1_bf, 1, "dw_ffn_gate").reshape(N_CHIPS, FF_PAD, D_MODEL)
    d_wu = _matmul_tn(dup_bf, x1_bf, 1, "dw_ffn_up").reshape(N_CHIPS, FF_PAD, D_MODEL)
    (dh1, dh1_bf, d_ln1_g, d_ln1_b), got_down = _ffn_in_bwd_ln1(
        da_bf, dup_bf, wg_g, wu_g, dh2, xhat1, rstd1, ln1_g, exchange("ffn_in_bwd_ln1", [("w_ffn_down", d_wd)]))
    d_wout = _matmul_tn(m_bf, dh1_bf, 1, "dw_out").reshape(N_CHIPS, D_MODEL // N_CHIPS, D_MODEL)
    dzg_bf, dya_bf, dyp_bf, do_bf, dpm, d_bgate = _merge_bwd(dh1_bf, wout, o_bf, pm_bf, z, wba, wbp)
    d_wba = _matmul_tn(o_bf, dya_bf, N_CHIPS, "dw_branch_attn")
    d_wbp = _matmul_tn(pm_bf, dyp_bf, N_CHIPS, "dw_branch_pool")
    du_bf, d_wpool, d_pscale = _pool_bwd(dpm, mixed, pooled_bf, w_pool, pool_scale)
    small = _pack_small([d_bgate, d_wpool, d_pscale, d_ln1_g, d_ln1_b, d_ln2_g, d_ln2_b, dconv[:, 3, :],
                         loss[0:1, :]])[None]
    (dq_bf, dk_bf, dv_bf), got_ffn = _attn_bwd(
        z, o_bf, lse, do_bf, cos_t, sin_t, nb,
        exchange("attn_bwd", [("w_ffn_gate", d_wg), ("w_ffn_up", d_wu), ("conv_w", dconv), ("small", small)]))
    dz_bf = jnp.concatenate([dq_bf, dk_bf, dv_bf, du_bf, dzg_bf], axis=1)
    d_win, got_branch = _matmul_tn(
        x_bf, dz_bf, N_CHIPS, "dw_in",
        cargo=exchange("dw_in", [("w_out", d_wout), ("w_branch_attn", d_wba), ("w_branch_pool", d_wbp)]))
    (grad_x,), got_in = _inproj_dx(dz_bf, win_g, dh1, exchange("inproj_dx", [("w_in", d_win)]))
    return grad_x, got_down + got_ffn + got_branch + got_in


SMALL_ROWS = 592


def _pack_small(parts):
    flat = jnp.concatenate([p.reshape(-1) for p in parts])
    return jnp.pad(flat, (0, SMALL_ROWS * LANES - flat.shape[0])).reshape(SMALL_ROWS, LANES)


def _unpack_small(packed, shapes):
    flat = packed.reshape(-1)
    out, off = [], 0
    for shp in shapes:
        n = 1
        for d in shp:
            n *= d
        out.append(flat[off:off + n].reshape(shp))
        off += n
    return out


def _pad_conv_b(cb):
    return jnp.pad(cb.reshape(N_CHIPS, FF_SHARD), ((0, 0), (0, FF_PAD - FF_SHARD)))


def kernel(x, w_in, b_gate, w_branch_attn, w_pool, pool_scale, w_branch_pool, w_out, ln1_g, ln1_b, w_ffn_gate, w_ffn_up, conv_w, conv_b, w_ffn_down, ln2_g, ln2_b, loss_target, m_w_in, m_b_gate, m_w_branch_attn, m_w_pool, m_pool_scale, m_w_branch_pool, m_w_out, m_ln1_g, m_ln1_b, m_w_ffn_gate, m_w_ffn_up, m_conv_w, m_conv_b, m_w_ffn_down, m_ln2_g, m_ln2_b, v_w_in, v_b_gate, v_w_branch_attn, v_w_pool, v_pool_scale, v_w_branch_pool, v_w_out, v_ln1_g, v_ln1_b, v_w_ffn_gate, v_w_ffn_up, v_conv_w, v_conv_b, v_w_ffn_down, v_ln2_g, v_ln2_b):
    pad_rows = lambda w: jnp.pad(w.astype(bf16), ((0, FF_PAD - FF_SHARD), (0, 0)))
    shards = [w_in[0].astype(bf16), w_branch_attn[0].astype(bf16), w_branch_pool[0].astype(bf16),
              w_out[0].astype(bf16), pad_rows(w_ffn_gate[0].T), pad_rows(w_ffn_up[0].T), pad_rows(w_ffn_down[0]),
              jnp.pad(conv_w[0], ((0, CONV_ROWS - 3), (0, FF_PAD - FF_SHARD)))]
    (win_g,) = _cargo_alone(_GatherCargo(shards[:1], [0.0]), "gather_w_in")
    gather = {"inproj": _GatherCargo(shards[1:4], [0.3, 0.5, 0.75]),
              "attn_fwd": _GatherCargo(shards[4:6] + shards[7:], [0.45, 0.88, 0.9]),
              "ffn_up": _GatherCargo(shards[6:7], [0.8])}

    place = jnp.stack([2 * lax.axis_index("x") + lax.axis_index("y"), lax.axis_index("c")]).astype(jnp.int32)
    chip_sums = {}

    def exchange(carrier, named):
        split = [_split_halves(g) for _, g in named]
        got = _presum_swap(split, "presum_swap_" + carrier)
        sums = _add_pairs(list(zip(split, got)), place, "presum_" + carrier)
        chip_sums.update({nm: s for (nm, _), s in zip(named, sums)})
        return _ExchangeCargo(sums, [nm == "small" for nm, _ in named])

    convb_g = _pad_conv_b(conv_b).reshape(N_CHIPS, 1, FF_PAD)
    grad_x, brought = _local_step(
        x[0], loss_target[0], win_g, gather, exchange, b_gate, w_pool[0], pool_scale, ln1_g, ln1_b, convb_g, ln2_g, ln2_b)
    slots = dict(zip(chip_sums, brought))
    names = ["w_in", "w_branch_attn", "w_branch_pool", "w_out", "w_ffn_gate", "w_ffn_up", "w_ffn_down", "conv_w", "small"]
    grads = _sibling_fill(_sum_slots([(chip_sums[nm], slots[nm], nm == "small") for nm in names], place, "sum_slots"))

    small_names = ["b_gate", "w_pool", "pool_scale", "ln1_g", "ln1_b", "ln2_g", "ln2_b", "conv_b"]
    small_w = [b_gate, w_pool, pool_scale, ln1_g, ln1_b, ln2_g, ln2_b, _pad_conv_b(conv_b)]
    small_m = [m_b_gate, m_w_pool, m_pool_scale, m_ln1_g, m_ln1_b, m_ln2_g, m_ln2_b, _pad_conv_b(m_conv_b)]
    small_v = [v_b_gate, v_w_pool, v_pool_scale, v_ln1_g, v_ln1_b, v_ln2_g, v_ln2_b, _pad_conv_b(v_conv_b)]
    weights = [w_in[0], w_branch_attn[0], w_branch_pool[0], w_out[0], w_ffn_gate[0].T, w_ffn_up[0].T, w_ffn_down[0],
               conv_w[0], _pack_small(small_w)]
    m_in = [m_w_in[0], m_w_branch_attn[0], m_w_branch_pool[0], m_w_out[0], m_w_ffn_gate[0].T, m_w_ffn_up[0].T,
            m_w_ffn_down[0], m_conv_w[0], _pack_small(small_m)]
    v_in = [v_w_in[0], v_w_branch_attn[0], v_w_branch_pool[0], v_w_out[0], v_w_ffn_gate[0].T, v_w_ffn_up[0].T,
            v_w_ffn_down[0], v_conv_w[0], _pack_small(small_v)]
    tiles = [64, 128, 128, 64, 88, 88, 88, 3, SMALL_ROWS]
    updated = _adamw(list(zip(weights, m_in, v_in, grads, tiles)), "adamw")
    res = {nm: [(r.T if nm in ("w_ffn_gate", "w_ffn_up") else r)[None] for r in outs4]
           for nm, outs4 in zip(names[:-1], updated)}
    packed = updated[-1]
    loss = grads[-1][SMALL_ROWS - 4, 0]
    shapes = [w.shape for w in small_w]
    for kind in range(4):
        for nm, val in zip(small_names, _unpack_small(packed[kind], shapes)):
            if nm == "conv_b":
                val = val[:, :FF_SHARD].reshape(1, 4 * FF_SHARD)
            res.setdefault(nm, [None] * 4)[kind] = val

    order = ["w_in", "b_gate", "w_branch_attn", "w_pool", "pool_scale", "w_branch_pool", "w_out", "ln1_g", "ln1_b",
             "w_ffn_gate", "w_ffn_up", "conv_w", "conv_b", "w_ffn_down", "ln2_g", "ln2_b"]
    outs = [loss, grad_x[None]]
    for kind in range(4):
        outs += [res[nm][kind] for nm in order]
    return tuple(outs)
```

```python
import functools

import jax
import jax.numpy as jnp
from jax import lax
from jax.experimental import pallas as pl
from jax.experimental.pallas import tpu as pltpu

D_MODEL = 1024
HEAD_DIM = 64
D_ATTN = 512
D_POOL = 512
MOBA_BLOCK = 256
MOBA_TOPK = 3
POOL_GROUP = 128
MAX_WINDOW = 16
FF_SHARD = 704
FF_PAD = 768
D_FF_PAD = 4 * FF_PAD
N_CHIPS = 4
LANES = 128
ALPHA = (2.0 * 1) ** 0.25
LN_EPS = 1e-5
NEG = -1e30
SCALE = HEAD_DIM ** -0.5
ADAM_LR, ADAM_B1, ADAM_B2, ADAM_EPS, ADAM_WD, ADAM_STEP = 0.001, 0.9, 0.999, 1e-08, 0.01, 10
VMEM_LIMIT = 56 * 1024 * 1024
MESH = pl.DeviceIdType.MESH

bf16 = jnp.bfloat16
f32 = jnp.float32


def _dot(a, b):
    return jnp.dot(a, b, preferred_element_type=f32)


def _dot_nt(a, b):
    return lax.dot_general(a, b, (((1,), (1,)), ((), ())), preferred_element_type=f32)


def _dot_tn(a, b):
    return lax.dot_general(a, b, (((0,), (0,)), ((), ())), preferred_element_type=f32)


def _call(body, *, name, grid, in_specs, out_specs, out_shape, scratch=(), cargo=None):
    if cargo is not None:
        return _cargo_call(body, cargo, name=name, grid=grid, in_specs=in_specs, out_specs=out_specs,
                           out_shape=out_shape, scratch=scratch)
    return pl.pallas_call(
        body, name=name, grid=grid, in_specs=in_specs, out_specs=out_specs, out_shape=out_shape,
        scratch_shapes=list(scratch),
        compiler_params=pltpu.CompilerParams(dimension_semantics=("arbitrary",) * len(grid),
                                             vmem_limit_bytes=VMEM_LIMIT))


def _rot_half(t):
    lane = lax.broadcasted_iota(jnp.int32, t.shape, 1)
    first = (lane % HEAD_DIM) < (HEAD_DIM // 2)
    return jnp.where(first, pltpu.roll(t, LANES - HEAD_DIM // 2, 1), pltpu.roll(t, HEAD_DIM // 2, 1))


def _rope(t, cos, sin_signed):
    return t * cos + _rot_half(t) * sin_signed


def _rope_bwd(d, cos, sin_signed):
    return d * cos + _rot_half(d * sin_signed)


def _gelu_parts(a):
    cdf = 0.5 * (1.0 + lax.erf(a * (2.0 ** -0.5)))
    pdf = jnp.exp(-0.5 * a * a) * ((2.0 * jnp.pi) ** -0.5)
    return a * cdf, cdf + a * pdf


def _layer_norm(h, g, b):
    mu = jnp.mean(h, axis=-1, keepdims=True)
    xc = h - mu
    var = jnp.mean(xc * xc, axis=-1, keepdims=True)
    rstd = lax.rsqrt(var + LN_EPS)
    xhat = xc * rstd
    return xhat * g + b, xhat, rstd


def _layer_norm_bwd(dy, xhat, rstd, g):
    dxh = dy * g
    m1 = jnp.mean(dxh, axis=-1, keepdims=True)
    m2 = jnp.mean(dxh * xhat, axis=-1, keepdims=True)
    return rstd * (dxh - m1 - xhat * m2)


def _inproj(x_bf, win_g, cos_t, sin_t, b_gate, cargo):
    s = x_bf.shape[0]
    tm, tn = 1024, 512

    def body(x_ref, w_ref, cos_ref, sin_ref, b_ref, o_ref):
        j = pl.program_id(1)
        acc = _dot(x_ref[...], w_ref[...])

        @pl.when(j < 2)
        def _():
            for c in range(tn // LANES):
                sl = slice(c * LANES, (c + 1) * LANES)
                o_ref[:, sl] = _rope(acc[:, sl], cos_ref[...], sin_ref[...])

        @pl.when((j >= 2) & (j < 4))
        def _():
            o_ref[...] = acc

        @pl.when(j >= 4)
        def _():
            o_ref[...] = jax.nn.sigmoid(acc + b_ref[...])

    (z,), carried = _cargo_call(
        body, cargo, name="inproj", grid=(s // tm, 4 * D_MODEL // tn),
        in_specs=[pl.BlockSpec((tm, D_MODEL), lambda i, j: (i, 0)),
                  pl.BlockSpec((None, D_MODEL, tn), lambda i, j: (j // 2, 0, j % 2)),
                  pl.BlockSpec((tm, LANES), lambda i, j: (i, 0)),
                  pl.BlockSpec((tm, LANES), lambda i, j: (i, 0)),
                  pl.BlockSpec((1, tn), lambda i, j: (0, jnp.maximum(j - 4, 0)))],
        out_specs=[pl.BlockSpec((tm, tn), lambda i, j: (i, j))],
        out_shape=[jax.ShapeDtypeStruct((s, 4 * D_MODEL), f32)])(x_bf, win_g, cos_t, sin_t, b_gate)
    return z, carried


STAT_ROWS = 8


def _pair_rows(v0, v1, fill):
    r = lax.broadcasted_iota(jnp.int32, (STAT_ROWS, v0.shape[1]), 0)
    return jnp.where(r == 0, v0, jnp.where(r == 1, v1, fill))


def _head_lanes(shape):
    lane = lax.broadcasted_iota(jnp.int32, shape, 1)
    return lane < HEAD_DIM, lane >= HEAD_DIM


def _head_rows(shape):
    row = lax.broadcasted_iota(jnp.int32, shape, 0)
    return row < HEAD_DIM, row >= HEAD_DIM


def _moba_select(q_bf, k_all, i, nb):
    k_mean = jnp.mean(k_all.reshape(nb, MOBA_BLOCK, LANES), axis=1)
    n_io = lax.broadcasted_iota(jnp.int32, (nb, MOBA_BLOCK), 0)
    past = n_io < i
    sels = []
    for head in _head_lanes((nb, LANES)):
        gate = _dot_nt(jnp.where(head, k_mean, 0.0).astype(bf16), q_bf)
        g = jnp.where(past, gate, NEG)
        rank = jnp.zeros((nb, MOBA_BLOCK), f32)
        for m in range(nb):
            gm = g[m:m + 1, :]
            rank = rank + jnp.where((gm > g) | ((gm == g) & (m < n_io)), 1.0, 0.0)
        sels.append(jnp.where(past & (rank < MOBA_TOPK), 1.0, 0.0))
    return sels


def _attn_fwd(z, nb, cargo):
    s = z.shape[0]

    def body(q_ref, k_ref, v_ref, o_ref, lse_ref, vt_s):
        i = pl.program_id(1)

        @pl.when(i == 0)
        def _():
            for j in range(nb):
                vt_s[j] = v_ref[j * MOBA_BLOCK:(j + 1) * MOBA_BLOCK, :].T.astype(bf16)

        k_heads = _head_lanes((MOBA_BLOCK, LANES))
        o_heads = _head_rows((LANES, MOBA_BLOCK))

        def block(j, qs, allow, m_old, l_old):
            kj, vtj = k_ref[j * MOBA_BLOCK:(j + 1) * MOBA_BLOCK, :], vt_s[j]
            m_new, alpha, l_new, pv = [], [], [], None
            for h in range(2):
                sc = jnp.where(allow[h], _dot_nt(jnp.where(k_heads[h], kj, 0.0).astype(bf16), qs), NEG)
                mx = jnp.max(sc, axis=0, keepdims=True)
                mn = mx if m_old is None else jnp.maximum(m_old[h], mx)
                p = jnp.exp(sc - mn)
                lsum = jnp.sum(p, axis=0, keepdims=True)
                if m_old is None:
                    a = None
                else:
                    a = jnp.exp(m_old[h] - mn)
                    lsum = a * l_old[h] + lsum
                t = _dot(jnp.where(o_heads[h], vtj, jnp.zeros_like(vtj)), p.astype(bf16))
                pv = t if pv is None else pv + t
                m_new.append(mn)
                alpha.append(a)
                l_new.append(lsum)
            return m_new, alpha, l_new, pv

        def query_block(c):
            q = q_ref[...]
            sels = _moba_select(q.astype(bf16), k_ref[...], c, nb)
            qs = (q * SCALE).astype(bf16)
            key = lax.broadcasted_iota(jnp.int32, (MOBA_BLOCK, MOBA_BLOCK), 0)
            qry = lax.broadcasted_iota(jnp.int32, (MOBA_BLOCK, MOBA_BLOCK), 1)
            m, _, l, acc = block(c, qs, [key <= qry] * 2, None, None)
            for j in range(c):
                m, alpha, l, pv = block(j, qs, [sels[h][j:j + 1, :] > 0.0 for h in range(2)], m, l)
                acc = acc * jnp.where(o_heads[0], alpha[0], alpha[1]) + pv
            o_ref[...] = (acc / jnp.where(o_heads[0], l[0], l[1])).T.astype(bf16)
            lse_ref[...] = _pair_rows(m[0] + jnp.log(l[0]), m[1] + jnp.log(l[1]), 0.0)

        for c in range(nb):
            pl.when(i == c)(functools.partial(query_block, c))

    return _cargo_call(
        body, cargo, name="attn_fwd", grid=(D_ATTN // LANES, nb),
        in_specs=[pl.BlockSpec((MOBA_BLOCK, LANES), lambda hp, i: (i, hp)),
                  pl.BlockSpec((s, LANES), lambda hp, i: (0, 4 + hp)),
                  pl.BlockSpec((s, LANES), lambda hp, i: (0, 8 + hp))],
        out_specs=[pl.BlockSpec((MOBA_BLOCK, LANES), lambda hp, i: (i, hp)),
                   pl.BlockSpec((None, STAT_ROWS, MOBA_BLOCK), lambda hp, i: (hp, 0, i))],
        out_shape=[jax.ShapeDtypeStruct((s, D_ATTN), bf16),
                   jax.ShapeDtypeStruct((D_ATTN // LANES, STAT_ROWS, s), f32)],
        scratch=[pltpu.VMEM((nb, LANES, MOBA_BLOCK), bf16)])(z, z, z)


def _window_select(snaps, g):
    return jnp.where(g == 0, snaps[0], jnp.where(g == 1, snaps[1], jnp.where(g == 2, snaps[2], snaps[3])))


def _pool_fwd(z, w_pool, pool_scale):
    s = z.shape[0]

    def body(u_ref, w_ref, sc_ref, pooled_ref, mixed_ref, pm_ref, pad):
        g = pl.program_id(0)
        u = u_ref[...]
        pad[0:MAX_WINDOW, :] = jnp.zeros((MAX_WINDOW, POOL_GROUP), f32)
        pad[MAX_WINDOW:MAX_WINDOW + s, :] = u
        acc = u
        snaps = []
        for d in range(1, MAX_WINDOW):
            acc = acc + pad[MAX_WINDOW - d:MAX_WINDOW - d + s, :]
            if d + 1 in (2, 4, 8, 16):
                snaps.append(acc)
        win = _window_select(snaps, g)
        t = lax.broadcasted_iota(jnp.int32, (s, POOL_GROUP), 0)
        count = jnp.minimum(t + 1, jnp.left_shift(2, g)).astype(f32)
        pooled = (win / count - u).astype(bf16)
        mixed = _dot(pooled, w_ref[...].astype(bf16))
        pooled_ref[...] = pooled
        mixed_ref[...] = mixed
        pm_ref[...] = (mixed * sc_ref[...]).astype(bf16)

    blk = pl.BlockSpec((s, POOL_GROUP), lambda g: (0, g))
    return _call(
        body, name="pool_fwd", grid=(4,),
        in_specs=[pl.BlockSpec((s, POOL_GROUP), lambda g: (0, 12 + g)),
                  pl.BlockSpec((None, POOL_GROUP, POOL_GROUP), lambda g: (g, 0, 0)),
                  pl.BlockSpec((1, POOL_GROUP), lambda g: (0, g))],
        out_specs=[blk, blk, blk],
        out_shape=[jax.ShapeDtypeStruct((s, D_POOL), bf16), jax.ShapeDtypeStruct((s, D_POOL), f32),
                   jax.ShapeDtypeStruct((s, D_POOL), bf16)],
        scratch=[pltpu.VMEM((s + MAX_WINDOW, POOL_GROUP), f32)])(z, w_pool, pool_scale)


def _branch_merge(o_bf, pm_bf, z, wba, wbp):
    s = o_bf.shape[0]
    tm = 512

    def body(o_ref, pm_ref, ga_ref, gp_ref, wba_ref, wbp_ref, m_ref):
        ya = _dot(o_ref[...], wba_ref[...])
        yp = _dot(pm_ref[...], wbp_ref[...])
        m_ref[...] = (ga_ref[...] * ya + gp_ref[...] * yp).astype(bf16)

    full = lambda r, c: pl.BlockSpec((r, c), lambda i: (0, 0))
    return _call(
        body, name="branch_merge", grid=(s // tm,),
        in_specs=[pl.BlockSpec((tm, D_ATTN), lambda i: (i, 0)), pl.BlockSpec((tm, D_POOL), lambda i: (i, 0)),
                  pl.BlockSpec((tm, D_MODEL), lambda i: (i, 2)), pl.BlockSpec((tm, D_MODEL), lambda i: (i, 3)),
                  full(D_ATTN, D_MODEL), full(D_POOL, D_MODEL)],
        out_specs=pl.BlockSpec((tm, D_MODEL), lambda i: (i, 0)),
        out_shape=jax.ShapeDtypeStruct((s, D_MODEL), bf16))(o_bf, pm_bf, z, z, wba, wbp)


def _out_ln1(m_bf, wout, x, ln_g, ln_b):
    s = x.shape[0]
    tm = 512

    def body(m_ref, w_ref, x_ref, g_ref, b_ref, x1_ref, xhat_ref, rstd_ref):
        h = ALPHA * x_ref[...] + _dot(m_ref[...], w_ref[...])
        y, xhat, rstd = _layer_norm(h, g_ref[...], b_ref[...])
        x1_ref[...] = y.astype(bf16)
        xhat_ref[...] = xhat
        rstd_ref[...] = jnp.broadcast_to(rstd, (tm, LANES))

    row = pl.BlockSpec((tm, D_MODEL), lambda i: (i, 0))
    vec = pl.BlockSpec((1, D_MODEL), lambda i: (0, 0))
    return _call(
        body, name="out_ln1", grid=(s // tm,),
        in_specs=[row, pl.BlockSpec((D_MODEL, D_MODEL), lambda i: (0, 0)), row, vec, vec],
        out_specs=[row, row, pl.BlockSpec((tm, LANES), lambda i: (i, 0))],
        out_shape=[jax.ShapeDtypeStruct((s, D_MODEL), bf16), jax.ShapeDtypeStruct((s, D_MODEL), f32),
                   jax.ShapeDtypeStruct((s, LANES), f32)])(m_bf, wout, x, ln_g, ln_b)


FF_TILE = 256
FF_TILES_PER_SHARD = FF_PAD // FF_TILE
CONV_PAD = 8
CONV_ROWS = 16

def _ff_weight_spec(rows):
    return pl.BlockSpec((None, rows, FF_TILE), lambda j: (j // FF_TILES_PER_SHARD, 0, j % FF_TILES_PER_SHARD))


def _ff_rows_spec():
    return pl.BlockSpec((None, FF_TILE, D_MODEL), lambda j: (j // FF_TILES_PER_SHARD, j % FF_TILES_PER_SHARD, 0))


def _row_shifts(x, shifts):
    s = x.shape[0]
    padded = jnp.concatenate([x, jnp.zeros((CONV_PAD, x.shape[1]), x.dtype)], axis=0)
    return [pltpu.roll(padded, d % (s + CONV_PAD), 0)[0:s] for d in shifts]


def _ffn_up(x1_bf, wg_g, wu_g, convw_g, convb_g, cargo):
    s = x1_bf.shape[0]

    def body(x_ref, wg_ref, wu_ref, cw_ref, cb_ref, a_ref, up_ref, ac_ref, hh_ref):
        x1 = x_ref[...]
        a = _dot_nt(x1, wg_ref[...])
        up = _dot_nt(x1, wu_ref[...])
        a_ref[...] = a
        up_ref[...] = up
        a1, a2 = _row_shifts(a, (1, 2))
        ac = a * cw_ref[2:3, :] + a1 * cw_ref[1:2, :] + a2 * cw_ref[0:1, :] + cb_ref[...]
        ac_ref[...] = ac
        hg, _ = _gelu_parts(ac)
        hh_ref[...] = (hg * up).astype(bf16)

    col = pl.BlockSpec((s, FF_TILE), lambda j: (0, j))
    wide = jax.ShapeDtypeStruct((s, D_FF_PAD), f32)
    return _call(
        body, name="ffn_up", grid=(D_FF_PAD // FF_TILE,),
        in_specs=[pl.BlockSpec((s, D_MODEL), lambda j: (0, 0)), _ff_rows_spec(), _ff_rows_spec(),
                  _ff_weight_spec(8), _ff_weight_spec(1)],
        out_specs=[col, col, col, col],
        out_shape=[wide, wide, wide, jax.ShapeDtypeStruct((s, D_FF_PAD), bf16)],
        cargo=cargo)(x1_bf, wg_g, wu_g, convw_g, convb_g)


def _ffn_down_ln2_loss(hh_bf, wd, xhat1, ln1_g, ln1_b, ln2_g, ln2_b, target):
    s = hh_bf.shape[0]
    tm = 256

    def body(hh_ref, w_ref, xh1_ref, g1_ref, b1_ref, g2_ref, b2_ref, t_ref, loss_ref, dh_ref, dhb_ref, dg_ref, db_ref):
        i = pl.program_id(0)
        x1 = xh1_ref[...] * g1_ref[...] + b1_ref[...]
        h = ALPHA * x1 + _dot(hh_ref[...], w_ref[...])
        y, xhat, rstd = _layer_norm(h, g2_ref[...], b2_ref[...])
        err = y - t_ref[...]
        part = 0.5 * jnp.sum(jnp.mean(err * err, axis=-1, keepdims=True), axis=0, keepdims=True)
        dy = err * (1.0 / D_MODEL)

        @pl.when(i == 0)
        def _():
            loss_ref[...] = jnp.zeros_like(loss_ref)
            dg_ref[...] = jnp.zeros_like(dg_ref)
            db_ref[...] = jnp.zeros_like(db_ref)

        loss_ref[...] += jnp.broadcast_to(part, loss_ref.shape)
        dg_ref[...] += jnp.sum(dy * xhat, axis=0, keepdims=True)
        db_ref[...] += jnp.sum(dy, axis=0, keepdims=True)
        dh = _layer_norm_bwd(dy, xhat, rstd, g2_ref[...])
        dh_ref[...] = dh
        dhb_ref[...] = dh.astype(bf16)

    row = pl.BlockSpec((tm, D_MODEL), lambda i: (i, 0))
    vec = pl.BlockSpec((1, D_MODEL), lambda i: (0, 0))
    return _call(
        body, name="ffn_down_ln2_loss", grid=(s // tm,),
        in_specs=[pl.BlockSpec((tm, D_FF_PAD), lambda i: (i, 0)),
                  pl.BlockSpec((D_FF_PAD, D_MODEL), lambda i: (0, 0)),
                  row, vec, vec, vec, vec, row],
        out_specs=[pl.BlockSpec((8, LANES), lambda i: (0, 0)), row, row, vec, vec],
        out_shape=[jax.ShapeDtypeStruct((8, LANES), f32), jax.ShapeDtypeStruct((s, D_MODEL), f32),
                   jax.ShapeDtypeStruct((s, D_MODEL), bf16), jax.ShapeDtypeStruct((1, D_MODEL), f32),
                   jax.ShapeDtypeStruct((1, D_MODEL), f32)])(hh_bf, wd, xhat1, ln1_g, ln1_b, ln2_g, ln2_b, target)


def _ffn_act_bwd(dh2_bf, wd_g, ac, a, up, convw_g):
    s = dh2_bf.shape[0]

    def body(dh_ref, wd_ref, ac_ref, a_ref, up_ref, cw_ref, da_ref, dup_ref, dc_ref):
        dhh = _dot_nt(dh_ref[...], wd_ref[...])
        hg, dgelu = _gelu_parts(ac_ref[...])
        dup_ref[...] = (dhh * hg).astype(bf16)
        dac = dhh * up_ref[...] * dgelu
        dac1, dac2 = _row_shifts(dac, (-1, -2))
        a = a_ref[...]
        rows = [jnp.sum(d * a, axis=0, keepdims=True) for d in (dac2, dac1, dac)]
        rows.append(jnp.sum(dac, axis=0, keepdims=True))
        rows.append(jnp.zeros((CONV_ROWS - 4, FF_TILE), f32))
        dc_ref[...] = jnp.concatenate(rows, axis=0)
        da_ref[...] = (dac * cw_ref[2:3, :] + dac1 * cw_ref[1:2, :] + dac2 * cw_ref[0:1, :]).astype(bf16)

    col = pl.BlockSpec((s, FF_TILE), lambda j: (0, j))
    return _call(
        body, name="ffn_act_bwd", grid=(D_FF_PAD // FF_TILE,),
        in_specs=[pl.BlockSpec((s, D_MODEL), lambda j: (0, 0)), _ff_rows_spec(), col, col, col, _ff_weight_spec(8)],
        out_specs=[col, col, _ff_weight_spec(CONV_ROWS)],
        out_shape=[jax.ShapeDtypeStruct((s, D_FF_PAD), bf16), jax.ShapeDtypeStruct((s, D_FF_PAD), bf16),
                   jax.ShapeDtypeStruct((N_CHIPS, CONV_ROWS, FF_PAD), f32)],
    )(dh2_bf, wd_g, ac, a, up, convw_g)


def _matmul_tn(a, b, n_shards, name, tm=512, tn=1024, cargo=None):
    k, m = a.shape
    n = b.shape[1]
    tm, tn = min(tm, m), min(tn, n // n_shards)
    per = n // n_shards // tn

    def body(a_ref, b_ref, o_ref, at_s):
        @pl.when(pl.program_id(1) == 0)
        def _():
            at_s[...] = a_ref[...].T

        o_ref[...] = _dot(at_s[...], b_ref[...]).astype(bf16)

    res = _call(
        body, name=name, grid=(m // tm, n // tn),
        in_specs=[pl.BlockSpec((k, tm), lambda i, j: (0, i)), pl.BlockSpec((k, tn), lambda i, j: (0, j))],
        out_specs=[pl.BlockSpec((None, tm, tn), lambda i, j: (j // per, i, j % per))],
        out_shape=[jax.ShapeDtypeStruct((n_shards, m, n // n_shards), bf16)], scratch=[pltpu.VMEM((tm, k), bf16)],
        cargo=cargo)(a, b)
    return res[0] if cargo is None else (res[0][0], res[1])


def _ffn_in_bwd_ln1(da_bf, dup_bf, wg_g, wu_g, dh2, xhat1, rstd1, ln1_g, cargo):
    s = da_bf.shape[0]
    tm = 256

    def body(da_ref, dup_ref, wg_ref, wu_ref, dh2_ref, xh_ref, rstd_ref, g_ref, dh_ref, dhb_ref, dg_ref, db_ref):
        i = pl.program_id(0)
        dx1 = ALPHA * dh2_ref[...]
        for sh in range(N_CHIPS):
            sl = slice(sh * FF_PAD, (sh + 1) * FF_PAD)
            dx1 = dx1 + _dot(da_ref[:, sl], wg_ref[sh]) + _dot(dup_ref[:, sl], wu_ref[sh])
        xhat = xh_ref[...]

        @pl.when(i == 0)
        def _():
            dg_ref[...] = jnp.zeros_like(dg_ref)
            db_ref[...] = jnp.zeros_like(db_ref)

        dg_ref[...] += jnp.sum(dx1 * xhat, axis=0, keepdims=True)
        db_ref[...] += jnp.sum(dx1, axis=0, keepdims=True)
        dh = _layer_norm_bwd(dx1, xhat, rstd_ref[:, 0:1], g_ref[...])
        dh_ref[...] = dh
        dhb_ref[...] = dh.astype(bf16)

    row = pl.BlockSpec((tm, D_MODEL), lambda i: (i, 0))
    wide = pl.BlockSpec((tm, D_FF_PAD), lambda i: (i, 0))
    wfull = pl.BlockSpec((N_CHIPS, FF_PAD, D_MODEL), lambda i: (0, 0, 0))
    vec = pl.BlockSpec((1, D_MODEL), lambda i: (0, 0))
    return _call(
        body, name="ffn_in_bwd_ln1", grid=(s // tm,),
        in_specs=[wide, wide, wfull, wfull, row, row, pl.BlockSpec((tm, LANES), lambda i: (i, 0)), vec],
        out_specs=[row, row, vec, vec],
        out_shape=[jax.ShapeDtypeStruct((s, D_MODEL), f32), jax.ShapeDtypeStruct((s, D_MODEL), bf16),
                   jax.ShapeDtypeStruct((1, D_MODEL), f32), jax.ShapeDtypeStruct((1, D_MODEL), f32)],
        cargo=cargo)(da_bf, dup_bf, wg_g, wu_g, dh2, xhat1, rstd1, ln1_g)


def _merge_bwd(dh1_bf, wout, o_bf, pm_bf, z, wba, wbp):
    s = dh1_bf.shape[0]
    tm = 256

    def body(dh_ref, wout_ref, o_ref, pm_ref, ga_ref, gp_ref, wba_ref, wbp_ref,
             dzg_ref, dya_ref, dyp_ref, do_ref, dpm_ref, dbg_ref):
        i = pl.program_id(0)
        dm = _dot_nt(dh_ref[...], wout_ref[...])
        ya = _dot(o_ref[...], wba_ref[...])
        yp = _dot(pm_ref[...], wbp_ref[...])
        ga, gp = ga_ref[...], gp_ref[...]
        dza = dm * ya * ga * (1.0 - ga)
        dzp = dm * yp * gp * (1.0 - gp)

        @pl.when(i == 0)
        def _():
            dbg_ref[...] = jnp.zeros_like(dbg_ref)

        dbg_ref[:, 0:D_MODEL] += jnp.sum(dza, axis=0, keepdims=True)
        dbg_ref[:, D_MODEL:2 * D_MODEL] += jnp.sum(dzp, axis=0, keepdims=True)
        dzg_ref[:, 0:D_MODEL] = dza.astype(bf16)
        dzg_ref[:, D_MODEL:2 * D_MODEL] = dzp.astype(bf16)
        dya = (dm * ga).astype(bf16)
        dyp = (dm * gp).astype(bf16)
        dya_ref[...] = dya
        dyp_ref[...] = dyp
        do_ref[...] = _dot_nt(dya, wba_ref[...]).astype(bf16)
        dpm_ref[...] = _dot_nt(dyp, wbp_ref[...])

    row = pl.BlockSpec((tm, D_MODEL), lambda i: (i, 0))
    half = pl.BlockSpec((tm, D_ATTN), lambda i: (i, 0))
    full = lambda r, c: pl.BlockSpec((r, c), lambda i: (0, 0))
    return _call(
        body, name="merge_bwd", grid=(s // tm,),
        in_specs=[row, full(D_MODEL, D_MODEL), half, half,
                  pl.BlockSpec((tm, D_MODEL), lambda i: (i, 2)), pl.BlockSpec((tm, D_MODEL), lambda i: (i, 3)),
                  full(D_ATTN, D_MODEL), full(D_POOL, D_MODEL)],
        out_specs=[pl.BlockSpec((tm, 2 * D_MODEL), lambda i: (i, 0)), row, row, half, half,
                   pl.BlockSpec((1, 2 * D_MODEL), lambda i: (0, 0))],
        out_shape=[jax.ShapeDtypeStruct((s, 2 * D_MODEL), bf16), jax.ShapeDtypeStruct((s, D_MODEL), bf16),
                   jax.ShapeDtypeStruct((s, D_MODEL), bf16), jax.ShapeDtypeStruct((s, D_ATTN), bf16),
                   jax.ShapeDtypeStruct((s, D_POOL), f32), jax.ShapeDtypeStruct((1, 2 * D_MODEL), f32)],
    )(dh1_bf, wout, o_bf, pm_bf, z, z, wba, wbp)


def _pool_bwd(dpm, mixed, pooled_bf, w_pool, pool_scale):
    s = dpm.shape[0]

    def body(dpm_ref, mixed_ref, pooled_ref, w_ref, sc_ref, du_ref, dw_ref, dsc_ref, pad):
        g = pl.program_id(0)
        dpm_v = dpm_ref[...]
        dsc_ref[...] = jnp.sum(dpm_v * mixed_ref[...], axis=0, keepdims=True)
        dmixed = (dpm_v * sc_ref[...]).astype(bf16)
        dw_ref[...] = _dot_tn(pooled_ref[...], dmixed)
        dpooled = _dot_nt(dmixed, w_ref[...].astype(bf16))
        t = lax.broadcasted_iota(jnp.int32, (s, POOL_GROUP), 0)
        count = jnp.minimum(t + 1, jnp.left_shift(2, g)).astype(f32)
        r = dpooled / count
        pad[0:s, :] = r
        pad[s:s + MAX_WINDOW, :] = jnp.zeros((MAX_WINDOW, POOL_GROUP), f32)
        acc = r
        snaps = []
        for d in range(1, MAX_WINDOW):
            acc = acc + pad[d:d + s, :]
            if d + 1 in (2, 4, 8, 16):
                snaps.append(acc)
        du_ref[...] = (_window_select(snaps, g) - dpooled).astype(bf16)

    blk = pl.BlockSpec((s, POOL_GROUP), lambda g: (0, g))
    return _call(
        body, name="pool_bwd", grid=(4,),
        in_specs=[blk, blk, blk, pl.BlockSpec((None, POOL_GROUP, POOL_GROUP), lambda g: (g, 0, 0)),
                  pl.BlockSpec((1, POOL_GROUP), lambda g: (0, g))],
        out_specs=[blk, pl.BlockSpec((None, POOL_GROUP, POOL_GROUP), lambda g: (g, 0, 0)),
                   pl.BlockSpec((1, POOL_GROUP), lambda g: (0, g))],
        out_shape=[jax.ShapeDtypeStruct((s, D_POOL), bf16), jax.ShapeDtypeStruct((4, POOL_GROUP, POOL_GROUP), f32),
                   jax.ShapeDtypeStruct((1, D_POOL), f32)],
        scratch=[pltpu.VMEM((s + MAX_WINDOW, POOL_GROUP), f32)])(dpm, mixed, pooled_bf, w_pool, pool_scale)


def _attn_bwd(z, o_bf, lse, do_bf, cos_t, sin_t, nb, cargo):
    s = z.shape[0]

    def body(q_ref, k_ref, v_ref, o_ref, lse_ref, do_ref, cq_ref, sq_ref, cf_ref, sf_ref,
             dq_ref, dk_ref, dv_ref, kt_s, dk_s, dv_s):
        i = pl.program_id(1)

        @pl.when(i == 0)
        def _():
            dk_s[...] = jnp.zeros_like(dk_s)
            dv_s[...] = jnp.zeros_like(dv_s)
            for j in range(nb):
                kt_s[j] = k_ref[j * MOBA_BLOCK:(j + 1) * MOBA_BLOCK, :].T.astype(bf16)

        k_heads = _head_lanes((MOBA_BLOCK, LANES))
        t_heads = _head_rows((LANES, MOBA_BLOCK))

        def query_block(c):
            q = q_ref[...]
            sels = _moba_select(q.astype(bf16), k_ref[...], c, nb)
            qs = (q * SCALE).astype(bf16)
            qraw = [jnp.where(hm, q, 0.0).astype(bf16) for hm in k_heads]
            do = do_ref[...].astype(f32)
            do_bf = do_ref[...]
            dob = [jnp.where(hm, do, 0.0).astype(bf16) for hm in k_heads]
            pick = _head_lanes((STAT_ROWS, LANES))
            r8 = lax.broadcasted_iota(jnp.int32, (STAT_ROWS, LANES), 0)
            head_pick = jnp.where(((r8 == 0) & pick[0]) | ((r8 == 1) & pick[1]), 1.0, 0.0)
            delta8 = lax.dot_general(head_pick, do * o_ref[...].astype(f32), (((1,), (1,)), ((), ())),
                                     precision=lax.Precision.HIGHEST, preferred_element_type=f32)
            lse8 = lse_ref[...]
            delta, lse_h = [delta8[0:1, :], delta8[1:2, :]], [lse8[0:1, :], lse8[1:2, :]]
            key = lax.broadcasted_iota(jnp.int32, (MOBA_BLOCK, MOBA_BLOCK), 0)
            qry = lax.broadcasted_iota(jnp.int32, (MOBA_BLOCK, MOBA_BLOCK), 1)

            def block(j, allow):
                rows = slice(j * MOBA_BLOCK, (j + 1) * MOBA_BLOCK)
                kj, vj, ktj = k_ref[rows, :], v_ref[rows, :], kt_s[j]
                dqt, dk, dv = None, None, None
                for h in range(2):
                    sc = _dot_nt(jnp.where(k_heads[h], kj, 0.0).astype(bf16), qs)
                    p = jnp.where(allow[h], jnp.exp(sc - lse_h[h]), 0.0)
                    dp = _dot_nt(jnp.where(k_heads[h], vj, 0.0).astype(bf16), do_bf)
                    ds = (p * (dp - delta[h]) * SCALE).astype(bf16)
                    tv = _dot(p.astype(bf16), dob[h])
                    tk = _dot(ds, qraw[h])
                    tq = _dot(jnp.where(t_heads[h], ktj, jnp.zeros_like(ktj)), ds)
                    dqt, dk, dv = (tq, tk, tv) if dqt is None else (dqt + tq, dk + tk, dv + tv)
                dk_s[rows, :] += dk
                dv_s[rows, :] += dv
                return dqt

            dqt = block(c, [key <= qry] * 2)
            for j in range(c):
                dqt = dqt + block(j, [sels[h][j:j + 1, :] > 0.0 for h in range(2)])
            dq_ref[...] = _rope_bwd(dqt.T, cq_ref[...], sq_ref[...]).astype(bf16)

        for c in range(nb):
            pl.when(i == c)(functools.partial(query_block, c))

        @pl.when(i == nb - 1)
        def _():
            dk_ref[...] = _rope_bwd(dk_s[...], cf_ref[...], sf_ref[...]).astype(bf16)
            dv_ref[...] = dv_s[...].astype(bf16)

    qblk = pl.BlockSpec((MOBA_BLOCK, LANES), lambda hp, i: (i, hp))
    tq = pl.BlockSpec((MOBA_BLOCK, LANES), lambda hp, i: (i, 0))
    tf = pl.BlockSpec((s, LANES), lambda hp, i: (0, 0))
    colblk = pl.BlockSpec((s, LANES), lambda hp, i: (0, hp))
    return _cargo_call(
        body, cargo, name="attn_bwd", grid=(D_ATTN // LANES, nb),
        in_specs=[qblk, pl.BlockSpec((s, LANES), lambda hp, i: (0, 4 + hp)),
                  pl.BlockSpec((s, LANES), lambda hp, i: (0, 8 + hp)), qblk,
                  pl.BlockSpec((None, STAT_ROWS, MOBA_BLOCK), lambda hp, i: (hp, 0, i)), qblk, tq, tq, tf, tf],
        out_specs=[qblk, colblk, colblk],
        out_shape=[jax.ShapeDtypeStruct((s, D_ATTN), bf16)] * 3,
        scratch=[pltpu.VMEM((nb, LANES, MOBA_BLOCK), bf16), pltpu.VMEM((s, LANES), f32), pltpu.VMEM((s, LANES), f32)],
    )(z, z, z, o_bf, lse, do_bf, cos_t, sin_t, cos_t, sin_t)


def _inproj_dx(dz_bf, win_g, dh1, cargo):
    s = dz_bf.shape[0]
    tm = 256

    def body(dz_ref, w_ref, dh_ref, gx_ref):
        acc = ALPHA * dh_ref[...]
        for sh in range(N_CHIPS):
            acc = acc + _dot_nt(dz_ref[:, sh * D_MODEL:(sh + 1) * D_MODEL], w_ref[sh])
        gx_ref[...] = acc

    row = pl.BlockSpec((tm, D_MODEL), lambda i: (i, 0))
    return _call(
        body, name="inproj_dx", grid=(s // tm,),
        in_specs=[pl.BlockSpec((tm, 4 * D_MODEL), lambda i: (i, 0)),
                  pl.BlockSpec((N_CHIPS, D_MODEL, D_MODEL), lambda i: (0, 0, 0)), row],
        out_specs=[row], out_shape=[jax.ShapeDtypeStruct((s, D_MODEL), f32)], cargo=cargo)(dz_bf, win_g, dh1)


ANY = pl.BlockSpec(memory_space=pl.ANY)


def _chip_index():
    return 2 * lax.axis_index("x") + lax.axis_index("y")


def _peer(k):
    x, y, c = lax.axis_index("x"), lax.axis_index("y"), lax.axis_index("c")
    return (x ^ (k >> 1), y ^ (k & 1), c)


def _sibling():
    return (lax.axis_index("x"), lax.axis_index("y"), 1 - lax.axis_index("c"))


class _GatherCargo:
    def __init__(self, shards, pass_on):
        self.shards, self.pass_on, n = shards, pass_on, len(shards)
        self.inputs = [a.reshape(2, a.shape[0] // 2, a.shape[1]) for a in shards]
        self.out_shape = [jax.ShapeDtypeStruct((N_CHIPS,) + a.shape, a.dtype) for a in self.inputs]
        sem, one = pltpu.SemaphoreType.DMA((N_CHIPS - 1, n)), pltpu.SemaphoreType.DMA((n,))
        self.sems = [sem, sem, sem, sem, one, one]

    def _copies(self, src, dst, sems, a):
        send_sems, recv_sems, fsend_sems, frecv_sems, local_sems, own_sems = sems
        p, c = _chip_index(), lax.axis_index("c")
        own = pltpu.make_async_remote_copy(src_ref=src[a], dst_ref=dst[a].at[p], send_sem=local_sems.at[a],
                                           recv_sem=own_sems.at[a], device_id=_sibling(), device_id_type=MESH)
        out, arrive, onward, landed = [], [], [], []
        for k in range(1, N_CHIPS):
            ici = dict(send_sem=send_sems.at[k - 1, a], recv_sem=recv_sems.at[k - 1, a], device_id=_peer(k),
                       device_id_type=MESH)
            d2d = dict(send_sem=fsend_sems.at[k - 1, a], recv_sem=frecv_sems.at[k - 1, a], device_id=_sibling(),
                       device_id_type=MESH)
            got, theirs = dst[a].at[p ^ k, c], dst[a].at[p ^ k, 1 - c]
            out.append(pltpu.make_async_remote_copy(src_ref=src[a].at[c], dst_ref=dst[a].at[p, c], **ici))
            arrive.append(pltpu.make_async_remote_copy(src_ref=src[a].at[c], dst_ref=got, **ici))
            onward.append(pltpu.make_async_remote_copy(src_ref=got, dst_ref=got, **d2d))
            landed.append(pltpu.make_async_remote_copy(src_ref=theirs, dst_ref=theirs, **d2d))
        return own, out, arrive, onward, landed

    def stages(self, steps):
        n = len(self.shards)

        def start(src, dst, sems):
            for a in range(n):
                own, out, _, _, _ = self._copies(src, dst, sems, a)
                own.start()
                for cp in out:
                    cp.start()

        def pass_on(a):
            def act(src, dst, sems):
                _, _, arrive, onward, _ = self._copies(src, dst, sems, a)
                for k in range(N_CHIPS - 1):
                    arrive[k].wait_recv()
                    onward[k].start()
            return act

        def finish(src, dst, sems):
            for a in range(n):
                own, out, _, onward, landed = self._copies(src, dst, sems, a)
                for cp in landed:
                    cp.wait_recv()
                for cp in out + onward:
                    cp.wait_send()
                own.wait()

        mids = [(min(steps - 1, int(self.pass_on[a] * steps)), pass_on(a)) for a in range(n)]
        return [(0, start)] + mids + [(steps - 1, finish)]

    def results(self, outs):
        return [o.reshape((N_CHIPS,) + a.shape) for o, a in zip(outs, self.shards)]


class _ExchangeCargo:
    def __init__(self, sums, whole):
        self.inputs, self.whole, n = sums, whole, len(sums)
        self.out_shape = [jax.ShapeDtypeStruct((N_CHIPS - 1,) + g.shape[1:], g.dtype) for g in sums]
        sem = pltpu.SemaphoreType.DMA((N_CHIPS - 1, n))
        self.sems = [sem, sem]

    def _copies(self, src, dst, sems):
        send_sems, recv_sems = sems
        p = _chip_index()
        return [pltpu.make_async_remote_copy(
            src_ref=src[a].at[0] if self.whole[a] else src[a].at[p ^ k], dst_ref=dst[a].at[k - 1],
            send_sem=send_sems.at[k - 1, a], recv_sem=recv_sems.at[k - 1, a], device_id=_peer(k), device_id_type=MESH)
            for k in range(1, N_CHIPS) for a in range(len(self.inputs))]

    def stages(self, steps):
        def start(src, dst, sems):
            for cp in self._copies(src, dst, sems):
                cp.start()

        def finish(src, dst, sems):
            copies = self._copies(src, dst, sems)
            for cp in copies:
                cp.wait_recv()
            for cp in copies:
                cp.wait_send()

        return [(0, start), (steps - 1, finish)]

    def results(self, outs):
        return list(outs)


def _cargo_call(body, cargo, *, name, grid, in_specs, out_specs, out_shape, scratch=()):
    n_in, n_out, n_scr = len(in_specs), len(out_specs), len(scratch)
    c_in, c_out = len(cargo.inputs), len(cargo.out_shape)
    steps = 1
    for g in grid:
        steps *= g
    stages = cargo.stages(steps)

    def wrapped(*refs):
        ins, refs = refs[:n_in], refs[n_in:]
        cin, refs = refs[:c_in], refs[c_in:]
        outs, refs = refs[:n_out], refs[n_out:]
        cout, refs = refs[:c_out], refs[c_out:]
        scr, sems = refs[:n_scr], refs[n_scr:]
        if not grid:
            for _, act in stages:
                act(cin, cout, sems)
            return
        step = 0
        for d in range(len(grid)):
            step = step * grid[d] + pl.program_id(d)
        pl.when(step == 0)(functools.partial(stages[0][1], cin, cout, sems))
        body(*ins, *outs, *scr)
        for at, act in stages[1:]:
            pl.when(step == at)(functools.partial(act, cin, cout, sems))

    params = dict(vmem_limit_bytes=VMEM_LIMIT)
    if grid:
        params["dimension_semantics"] = ("arbitrary",) * len(grid)
    res = pl.pallas_call(
        wrapped, name=name, grid=grid, in_specs=list(in_specs) + [ANY] * c_in, out_specs=list(out_specs) + [ANY] * c_out,
        out_shape=list(out_shape) + cargo.out_shape, scratch_shapes=list(scratch) + cargo.sems,
        compiler_params=pltpu.CompilerParams(**params))
    return lambda *args: (lambda r: (r[:n_out], cargo.results(r[n_out:])))(res(*args, *cargo.inputs))


def _split_halves(g):
    return g.reshape(g.shape[0], 2, g.shape[1] // 2, g.shape[2])


def _ew_tile(rows):
    return 128 if rows % 128 == 0 else rows


def _batched_call(items, name, place=None):
    offs, total = [], 0
    for it in items:
        offs.append(total)
        total += it[0]
    in_specs, out_specs, out_shape, args, spans = [], [], [], [], []
    for (n, ins, outs, shapes, operands, _), off in zip(items, offs):
        tile = lambda t, off=off, n=n: jnp.clip(t - off, 0, n - 1)
        i_s, o_s = ins(tile), outs(tile)
        spans.append((len(i_s), len(o_s)))
        in_specs += i_s
        out_specs += o_s
        out_shape += shapes
        args += operands
    n_in = len(in_specs)

    def body(*refs):
        if place is not None:
            refs = refs[1:]
        t = pl.program_id(0)
        i0, o0 = 0, n_in
        for (n, _, _, _, _, compute), off, (ni, no) in zip(items, offs, spans):
            pl.when((t >= off) & (t < off + n))(functools.partial(compute, refs[i0:i0 + ni], refs[o0:o0 + no]))
            i0, o0 = i0 + ni, o0 + no

    params = pltpu.CompilerParams(dimension_semantics=("arbitrary",), vmem_limit_bytes=VMEM_LIMIT)
    if place is None:
        res = pl.pallas_call(body, name=name, grid=(total,), in_specs=in_specs, out_specs=out_specs,
                             out_shape=out_shape, compiler_params=params)(*args)
    else:
        res = pl.pallas_call(
            body, name=name, out_shape=out_shape, compiler_params=params,
            grid_spec=pltpu.PrefetchScalarGridSpec(num_scalar_prefetch=1, grid=(total,), in_specs=in_specs,
                                                   out_specs=out_specs))(place, *args)
    out, o0 = [], 0
    for _, no in spans:
        out.append(res[o0:o0 + no])
        o0 += no
    return out


def _presum_swap(split, name):
    n = len(split)

    def body(*refs):
        src, got = refs[:n], refs[n:2 * n]
        send_sems, recv_sems = refs[2 * n:]
        c = lax.axis_index("c")
        remote = [pltpu.make_async_remote_copy(src_ref=src[a].at[:, 1 - c], dst_ref=got[a],
                                               send_sem=send_sems.at[a], recv_sem=recv_sems.at[a],
                                               device_id=_sibling(), device_id_type=MESH) for a in range(n)]
        for cp in remote:
            cp.start()
        for cp in remote:
            cp.wait_recv()
        for cp in remote:
            cp.wait_send()

    sem = pltpu.SemaphoreType.DMA((n,))
    return pl.pallas_call(
        body, name=name, in_specs=[ANY] * n, out_specs=[ANY] * n,
        out_shape=[jax.ShapeDtypeStruct((g.shape[0],) + g.shape[2:], g.dtype) for g in split],
        scratch_shapes=[sem, sem])(*split)


def _add_pairs(pairs, place, name):
    def item(mine, theirs):
        lead, _, rows, cols = mine.shape
        tm = _ew_tile(rows)

        def compute(ins, outs):
            outs[0][...] = (ins[0][...].astype(f32) + ins[1][...].astype(f32)).astype(outs[0].dtype)

        blk = lambda tile: pl.BlockSpec((lead, tm, cols), lambda t, pc: (0, tile(t), 0))
        mine_half = lambda tile: pl.BlockSpec((lead, None, tm, cols), lambda t, pc: (0, pc[1], tile(t), 0))
        return (rows // tm, lambda tile: [mine_half(tile), blk(tile)], lambda tile: [blk(tile)],
                [jax.ShapeDtypeStruct(theirs.shape, theirs.dtype)], [mine, theirs], compute)

    return [out[0] for out in _batched_call([item(a, b) for a, b in pairs], name, place)]


class _NoCargo:
    def __init__(self, results=()):
        self.inputs, self.out_shape, self.sems, self._results = [], [], [], list(results)

    def stages(self, steps):
        return [(0, lambda src, dst, sems: None), (steps - 1, lambda src, dst, sems: None)]

    def results(self, outs):
        return self._results


def _cargo_alone(cargo, name):
    return _cargo_call(None, cargo, name=name, grid=(), in_specs=[], out_specs=[], out_shape=[])()[1]


def _sum_slots(jobs, place, name):
    def item(own, recv, whole):
        _, rows, cols = own.shape
        tm = _ew_tile(rows)

        def compute(ins, outs):
            r0, r1, r2 = [ins[1][k].astype(f32) for k in range(N_CHIPS - 1)]
            outs[0][...] = (ins[0][...].astype(f32) + r0) + (r1 + r2)

        ins = lambda tile: [pl.BlockSpec((None, tm, cols), lambda t, pc: (0 if whole else pc[0], tile(t), 0)),
                            pl.BlockSpec((N_CHIPS - 1, tm, cols), lambda t, pc: (0, tile(t), 0))]
        outs = lambda tile: [pl.BlockSpec((None, tm, cols), lambda t, pc: (pc[1], tile(t), 0))]
        return rows // tm, ins, outs, [jax.ShapeDtypeStruct((2, rows, cols), f32)], [own, recv], compute

    return [out[0] for out in _batched_call([item(*job) for job in jobs], name, place)]


def _sibling_fill(bufs):
    n = len(bufs)

    def body(*refs):
        src, dst = refs[:n], refs[n:2 * n]
        send_sems, recv_sems = refs[2 * n:]
        c = lax.axis_index("c")
        remote = [pltpu.make_async_remote_copy(src_ref=src[a].at[c], dst_ref=dst[a].at[c], send_sem=send_sems.at[a],
                                               recv_sem=recv_sems.at[a], device_id=_sibling(), device_id_type=MESH)
                  for a in range(n)]
        for cp in remote:
            cp.start()
        for a in range(n):
            pltpu.make_async_remote_copy(src_ref=src[a].at[c], dst_ref=dst[a].at[1 - c], send_sem=send_sems.at[a],
                                         recv_sem=recv_sems.at[a], device_id=_sibling(), device_id_type=MESH).wait_recv()
        for cp in remote:
            cp.wait_send()

    sem = pltpu.SemaphoreType.DMA((n,))
    out = pl.pallas_call(
        body, name="sibling_fill", in_specs=[ANY] * n, out_specs=[ANY] * n,
        out_shape=[jax.ShapeDtypeStruct(b.shape, b.dtype) for b in bufs],
        input_output_aliases={a: a for a in range(n)}, scratch_shapes=[sem, sem])(*bufs)
    return [o.reshape(2 * b.shape[1], b.shape[2]) for o, b in zip(out, bufs)]


def _adamw(jobs, name):
    def item(w, m, v, grad, tm):
        rows, cols = w.shape
        gcols = grad.shape[1]

        def compute(ins, outs):
            w_, m_, v_ = ins[0][...], ins[1][...], ins[2][...]
            g = ins[3][0:tm, 0:cols]
            m_ = ADAM_B1 * m_ + (1.0 - ADAM_B1) * g
            v_ = ADAM_B2 * v_ + (1.0 - ADAM_B2) * jnp.square(g)
            m_hat = m_ / (1.0 - ADAM_B1 ** ADAM_STEP)
            v_hat = v_ / (1.0 - ADAM_B2 ** ADAM_STEP)
            outs[0][...] = g
            outs[1][...] = -ADAM_LR * (m_hat / (jnp.sqrt(v_hat) + ADAM_EPS) + ADAM_WD * w_)
            outs[2][...] = m_
            outs[3][...] = v_

        blk = lambda tile: pl.BlockSpec((tm, cols), lambda t: (tile(t), 0))
        if tm == rows:
            gspec = lambda tile: pl.BlockSpec(grad.shape, lambda t: (0, 0))
        else:
            gspec = lambda tile: pl.BlockSpec((tm, gcols), lambda t: (tile(t), 0))
        return (rows // tm, lambda tile: [blk(tile)] * 3 + [gspec(tile)], lambda tile: [blk(tile)] * 4,
                [jax.ShapeDtypeStruct((rows, cols), f32)] * 4, [w, m, v, grad], compute)

    return _batched_call([item(*job) for job in jobs], name)


def _rope_tables(s):
    half = HEAD_DIM // 2
    inv_freq = 1.0 / (10000.0 ** (jnp.arange(half, dtype=f32) / half))
    ang = jnp.arange(s, dtype=f32)[:, None] * inv_freq[None, :]
    cos, sin = jnp.cos(ang), jnp.sin(ang)
    return jnp.tile(cos, (1, LANES // half)), jnp.tile(jnp.concatenate([-sin, sin], axis=1), (1, LANES // HEAD_DIM))


def _local_step(x, target, win_g, gather, exchange, b_gate, w_pool, pool_scale, ln1_g, ln1_b, convb_g, ln2_g, ln2_b):
    s = x.shape[0]
    nb = s // MOBA_BLOCK
    cos_t, sin_t = _rope_tables(s)
    x_bf = x.astype(bf16)

    z, (wba_g, wbp_g, wout_g) = _inproj(x_bf, win_g, cos_t, sin_t, b_gate, gather["inproj"])
    wba = wba_g.transpose(1, 0, 2).reshape(D_ATTN, D_MODEL)
    wbp = wbp_g.transpose(1, 0, 2).reshape(D_POOL, D_MODEL)
    wout = wout_g.reshape(D_MODEL, D_MODEL)
    (o_bf, lse), (wg_g, wu_g, convw_g) = _attn_fwd(z, nb, gather["attn_fwd"])
    pooled_bf, mixed, pm_bf = _pool_fwd(z, w_pool, pool_scale)
    m_bf = _branch_merge(o_bf, pm_bf, z, wba, wbp)
    x1_bf, xhat1, rstd1 = _out_ln1(m_bf, wout, x, ln1_g, ln1_b)
    (a, up, ac, hh_bf), (wd_g,) = _ffn_up(x1_bf, wg_g, wu_g, convw_g, convb_g, gather["ffn_up"])
    wd = wd_g.reshape(D_FF_PAD, D_MODEL)
    loss, dh2, dh2_bf, d_ln2_g, d_ln2_b = _ffn_down_ln2_loss(hh_bf, wd, xhat1, ln1_g, ln1_b, ln2_g, ln2_b, target)

    da_bf, dup_bf, dconv = _ffn_act_bwd(dh2_bf, wd_g, ac, a, up, convw_g)
    d_wd = _matmul_tn(hh_bf, dh2_bf, 1, "dw_ffn_down").reshape(N_CHIPS, FF_PAD, D_MODEL)
    d_wg = _matmul_tn(da_bf, x1_bf, 1, "dw_ffn_gate").reshape(N_CHIPS, FF_PAD, D_MODEL)
    d_wu = _matmul_tn(dup_bf, x1_bf, 1, "dw_ffn_up").reshape(N_CHIPS, FF_PAD, D_MODEL)
    (dh1, dh1_bf, d_ln1_g, d_ln1_b), got_down = _ffn_in_bwd_ln1(
        da_bf, dup_bf, wg_g, wu_g, dh2, xhat1, rstd1, ln1_g, exchange("ffn_in_bwd_ln1", [("w_ffn_down", d_wd)]))
    d_wout = _matmul_tn(m_bf, dh1_bf, 1, "dw_out").reshape(N_CHIPS, D_MODEL // N_CHIPS, D_MODEL)
    dzg_bf, dya_bf, dyp_bf, do_bf, dpm, d_bgate = _merge_bwd(dh1_bf, wout, o_bf, pm_bf, z, wba, wbp)
    d_wba = _matmul_tn(o_bf, dya_bf, N_CHIPS, "dw_branch_attn")
    d_wbp = _matmul_tn(pm_bf, dyp_bf, N_CHIPS, "dw_branch_pool")
    du_bf, d_wpool, d_pscale = _pool_bwd(dpm, mixed, pooled_bf, w_pool, pool_scale)
    small = _pack_small([d_bgate, d_wpool, d_pscale, d_ln1_g, d_ln1_b, d_ln2_g, d_ln2_b, dconv[:, 3, :],
                         loss[0:1, :]])[None]
    (dq_bf, dk_bf, dv_bf), got_ffn = _attn_bwd(
        z, o_bf, lse, do_bf, cos_t, sin_t, nb,
        exchange("attn_bwd", [("w_ffn_gate", d_wg), ("w_ffn_up", d_wu), ("conv_w", dconv), ("small", small)]))
    dz_bf = jnp.concatenate([dq_bf, dk_bf, dv_bf, du_bf, dzg_bf], axis=1)
    d_win, got_branch = _matmul_tn(
        x_bf, dz_bf, N_CHIPS, "dw_in",
        cargo=exchange("dw_in", [("w_out", d_wout), ("w_branch_attn", d_wba), ("w_branch_pool", d_wbp)]))
    (grad_x,), got_in = _inproj_dx(dz_bf, win_g, dh1, exchange("inproj_dx", [("w_in", d_win)]))
    return grad_x, got_down + got_ffn + got_branch + got_in


SMALL_ROWS = 592


def _pack_small(parts):
    flat = jnp.concatenate([p.reshape(-1) for p in parts])
    return jnp.pad(flat, (0, SMALL_ROWS * LANES - flat.shape[0])).reshape(SMALL_ROWS, LANES)


def _unpack_small(packed, shapes):
    flat = packed.reshape(-1)
    out, off = [], 0
    for shp in shapes:
        n = 1
        for d in shp:
            n *= d
        out.append(flat[off:off + n].reshape(shp))
        off += n
    return out


def _pad_conv_b(cb):
    return jnp.pad(cb.reshape(N_CHIPS, FF_SHARD), ((0, 0), (0, FF_PAD - FF_SHARD)))


def kernel(x, w_in, b_gate, w_branch_attn, w_pool, pool_scale, w_branch_pool, w_out, ln1_g, ln1_b, w_ffn_gate, w_ffn_up, conv_w, conv_b, w_ffn_down, ln2_g, ln2_b, loss_target, m_w_in, m_b_gate, m_w_branch_attn, m_w_pool, m_pool_scale, m_w_branch_pool, m_w_out, m_ln1_g, m_ln1_b, m_w_ffn_gate, m_w_ffn_up, m_conv_w, m_conv_b, m_w_ffn_down, m_ln2_g, m_ln2_b, v_w_in, v_b_gate, v_w_branch_attn, v_w_pool, v_pool_scale, v_w_branch_pool, v_w_out, v_ln1_g, v_ln1_b, v_w_ffn_gate, v_w_ffn_up, v_conv_w, v_conv_b, v_w_ffn_down, v_ln2_g, v_ln2_b):
    pad_rows = lambda w: jnp.pad(w.astype(bf16), ((0, FF_PAD - FF_SHARD), (0, 0)))
    shards = [w_in[0].astype(bf16), w_branch_attn[0].astype(bf16), w_branch_pool[0].astype(bf16),
              w_out[0].astype(bf16), pad_rows(w_ffn_gate[0].T), pad_rows(w_ffn_up[0].T), pad_rows(w_ffn_down[0]),
              jnp.pad(conv_w[0], ((0, CONV_ROWS - 3), (0, FF_PAD - FF_SHARD)))]
    (win_g,) = _cargo_alone(_GatherCargo(shards[:1], [0.0]), "gather_w_in")
    gather = {"inproj": _GatherCargo(shards[1:4], [0.3, 0.5, 0.75]),
              "attn_fwd": _GatherCargo(shards[4:6] + shards[7:], [0.45, 0.88, 0.9]),
              "ffn_up": _GatherCargo(shards[6:7], [0.8])}

    place = jnp.stack([2 * lax.axis_index("x") + lax.axis_index("y"), lax.axis_index("c")]).astype(jnp.int32)
    chip_sums = {}

    def exchange(carrier, named):
        split = [_split_halves(g) for _, g in named]
        got = _presum_swap(split, "presum_swap_" + carrier)
        sums = _add_pairs(list(zip(split, got)), place, "presum_" + carrier)
        chip_sums.update({nm: s for (nm, _), s in zip(named, sums)})
        return _ExchangeCargo(sums, [nm == "small" for nm, _ in named])

    convb_g = _pad_conv_b(conv_b).reshape(N_CHIPS, 1, FF_PAD)
    grad_x, brought = _local_step(
        x[0], loss_target[0], win_g, gather, exchange, b_gate, w_pool[0], pool_scale, ln1_g, ln1_b, convb_g, ln2_g, ln2_b)
    slots = dict(zip(chip_sums, brought))
    names = ["w_in", "w_branch_attn", "w_branch_pool", "w_out", "w_ffn_gate", "w_ffn_up", "w_ffn_down", "conv_w", "small"]
    grads = _sibling_fill(_sum_slots([(chip_sums[nm], slots[nm], nm == "small") for nm in names], place, "sum_slots"))

    small_names = ["b_gate", "w_pool", "pool_scale", "ln1_g", "ln1_b", "ln2_g", "ln2_b", "conv_b"]
    small_w = [b_gate, w_pool, pool_scale, ln1_g, ln1_b, ln2_g, ln2_b, _pad_conv_b(conv_b)]
    small_m = [m_b_gate, m_w_pool, m_pool_scale, m_ln1_g, m_ln1_b, m_ln2_g, m_ln2_b, _pad_conv_b(m_conv_b)]
    small_v = [v_b_gate, v_w_pool, v_pool_scale, v_ln1_g, v_ln1_b, v_ln2_g, v_ln2_b, _pad_conv_b(v_conv_b)]
    weights = [w_in[0], w_branch_attn[0], w_branch_pool[0], w_out[0], w_ffn_gate[0].T, w_ffn_up[0].T, w_ffn_down[0],
               conv_w[0], _pack_small(small_w)]
    m_in = [m_w_in[0], m_w_branch_attn[0], m_w_branch_pool[0], m_w_out[0], m_w_ffn_gate[0].T, m_w_ffn_up[0].T,
            m_w_ffn_down[0], m_conv_w[0], _pack_small(small_m)]
    v_in = [v_w_in[0], v_w_branch_attn[0], v_w_branch_pool[0], v_w_out[0], v_w_ffn_gate[0].T, v_w_ffn_up[0].T,
            v_w_ffn_down[0], v_conv_w[0], _pack_small(small_v)]
    tiles = [256, 256, 256, 128, 176, 176, 176, 3, SMALL_ROWS]
    jobs = list(zip(weights, m_in, v_in, grads, tiles))
    ffn = slice(4, 7)
    updated = _adamw(jobs[:4] + jobs[7:], "adamw_rest")
    updated = updated[:4] + _adamw(jobs[ffn], "adamw_ffn") + updated[4:]
    res = {nm: [(r.T if nm in ("w_ffn_gate", "w_ffn_up") else r)[None] for r in outs4]
           for nm, outs4 in zip(names[:-1], updated)}
    packed = updated[-1]
    loss = grads[-1][SMALL_ROWS - 4, 0]
    shapes = [w.shape for w in small_w]
    for kind in range(4):
        for nm, val in zip(small_names, _unpack_small(packed[kind], shapes)):
            if nm == "conv_b":
                val = val[:, :FF_SHARD].reshape(1, 4 * FF_SHARD)
            res.setdefault(nm, [None] * 4)[kind] = val

    order = ["w_in", "b_gate", "w_branch_attn", "w_pool", "pool_scale", "w_branch_pool", "w_out", "ln1_g", "ln1_b",
             "w_ffn_gate", "w_ffn_up", "conv_w", "conv_b", "w_ffn_down", "ln2_g", "ln2_b"]
    outs = [loss, grad_x[None]]
    for kind in range(4):
        outs += [res[nm][kind] for nm in order]
    return tuple(outs)
```

```python
import functools

import jax
import jax.numpy as jnp
from jax import lax
from jax.experimental import pallas as pl
from jax.experimental.pallas import tpu as pltpu

D_MODEL = 1024
HEAD_DIM = 64
D_ATTN = 512
D_POOL = 512
MOBA_BLOCK = 256
MOBA_TOPK = 3
POOL_GROUP = 128
MAX_WINDOW = 16
FF_SHARD = 704
FF_PAD = 768
D_FF_PAD = 4 * FF_PAD
N_CHIPS = 4
LANES = 128
ALPHA = (2.0 * 1) ** 0.25
LN_EPS = 1e-5
NEG = -1e30
SCALE = HEAD_DIM ** -0.5
ADAM_LR, ADAM_B1, ADAM_B2, ADAM_EPS, ADAM_WD, ADAM_STEP = 0.001, 0.9, 0.999, 1e-08, 0.01, 10
VMEM_LIMIT = 56 * 1024 * 1024
MESH = pl.DeviceIdType.MESH

bf16 = jnp.bfloat16
f32 = jnp.float32


def _dot(a, b):
    return jnp.dot(a, b, preferred_element_type=f32)


def _dot_nt(a, b):
    return lax.dot_general(a, b, (((1,), (1,)), ((), ())), preferred_element_type=f32)


def _dot_tn(a, b):
    return lax.dot_general(a, b, (((0,), (0,)), ((), ())), preferred_element_type=f32)


def _call(body, *, name, grid, in_specs, out_specs, out_shape, scratch=(), cargo=None):
    if cargo is not None:
        return _cargo_call(body, cargo, name=name, grid=grid, in_specs=in_specs, out_specs=out_specs,
                           out_shape=out_shape, scratch=scratch)
    return pl.pallas_call(
        body, name=name, grid=grid, in_specs=in_specs, out_specs=out_specs, out_shape=out_shape,
        scratch_shapes=list(scratch),
        compiler_params=pltpu.CompilerParams(dimension_semantics=("arbitrary",) * len(grid),
                                             vmem_limit_bytes=VMEM_LIMIT))


def _rot_half(t):
    lane = lax.broadcasted_iota(jnp.int32, t.shape, 1)
    first = (lane % HEAD_DIM) < (HEAD_DIM // 2)
    return jnp.where(first, pltpu.roll(t, LANES - HEAD_DIM // 2, 1), pltpu.roll(t, HEAD_DIM // 2, 1))


def _rope(t, cos, sin_signed):
    return t * cos + _rot_half(t) * sin_signed


def _rope_bwd(d, cos, sin_signed):
    return d * cos + _rot_half(d * sin_signed)


def _gelu_parts(a):
    cdf = 0.5 * (1.0 + lax.erf(a * (2.0 ** -0.5)))
    pdf = jnp.exp(-0.5 * a * a) * ((2.0 * jnp.pi) ** -0.5)
    return a * cdf, cdf + a * pdf


def _layer_norm(h, g, b):
    mu = jnp.mean(h, axis=-1, keepdims=True)
    xc = h - mu
    var = jnp.mean(xc * xc, axis=-1, keepdims=True)
    rstd = lax.rsqrt(var + LN_EPS)
    xhat = xc * rstd
    return xhat * g + b, xhat, rstd


def _layer_norm_bwd(dy, xhat, rstd, g):
    dxh = dy * g
    m1 = jnp.mean(dxh, axis=-1, keepdims=True)
    m2 = jnp.mean(dxh * xhat, axis=-1, keepdims=True)
    return rstd * (dxh - m1 - xhat * m2)


def _inproj(x_bf, win_g, cos_t, sin_t, b_gate, cargo):
    s = x_bf.shape[0]
    tm, tn = 1024, 512

    def body(x_ref, w_ref, cos_ref, sin_ref, b_ref, o_ref):
        j = pl.program_id(1)
        acc = _dot(x_ref[...], w_ref[...])

        @pl.when(j < 2)
        def _():
            for c in range(tn // LANES):
                sl = slice(c * LANES, (c + 1) * LANES)
                o_ref[:, sl] = _rope(acc[:, sl], cos_ref[...], sin_ref[...])

        @pl.when((j >= 2) & (j < 4))
        def _():
            o_ref[...] = acc

        @pl.when(j >= 4)
        def _():
            o_ref[...] = jax.nn.sigmoid(acc + b_ref[...])

    (z,), carried = _cargo_call(
        body, cargo, name="inproj", grid=(s // tm, 4 * D_MODEL // tn),
        in_specs=[pl.BlockSpec((tm, D_MODEL), lambda i, j: (i, 0)),
                  pl.BlockSpec((None, D_MODEL, tn), lambda i, j: (j // 2, 0, j % 2)),
                  pl.BlockSpec((tm, LANES), lambda i, j: (i, 0)),
                  pl.BlockSpec((tm, LANES), lambda i, j: (i, 0)),
                  pl.BlockSpec((1, tn), lambda i, j: (0, jnp.maximum(j - 4, 0)))],
        out_specs=[pl.BlockSpec((tm, tn), lambda i, j: (i, j))],
        out_shape=[jax.ShapeDtypeStruct((s, 4 * D_MODEL), f32)])(x_bf, win_g, cos_t, sin_t, b_gate)
    return z, carried


STAT_ROWS = 8


def _pair_rows(v0, v1, fill):
    r = lax.broadcasted_iota(jnp.int32, (STAT_ROWS, v0.shape[1]), 0)
    return jnp.where(r == 0, v0, jnp.where(r == 1, v1, fill))


def _head_lanes(shape):
    lane = lax.broadcasted_iota(jnp.int32, shape, 1)
    return lane < HEAD_DIM, lane >= HEAD_DIM


def _head_rows(shape):
    row = lax.broadcasted_iota(jnp.int32, shape, 0)
    return row < HEAD_DIM, row >= HEAD_DIM


def _moba_select(q_bf, k_all, i, nb):
    k_mean = jnp.mean(k_all.reshape(nb, MOBA_BLOCK, LANES), axis=1)
    n_io = lax.broadcasted_iota(jnp.int32, (nb, MOBA_BLOCK), 0)
    past = n_io < i
    sels = []
    for head in _head_lanes((nb, LANES)):
        gate = _dot_nt(jnp.where(head, k_mean, 0.0).astype(bf16), q_bf)
        g = jnp.where(past, gate, NEG)
        rank = jnp.zeros((nb, MOBA_BLOCK), f32)
        for m in range(nb):
            gm = g[m:m + 1, :]
            rank = rank + jnp.where((gm > g) | ((gm == g) & (m < n_io)), 1.0, 0.0)
        sels.append(jnp.where(past & (rank < MOBA_TOPK), 1.0, 0.0))
    return sels


def _attn_fwd(z, nb, cargo):
    s = z.shape[0]

    def body(q_ref, k_ref, v_ref, o_ref, lse_ref, vt_s):
        i = pl.program_id(1)

        @pl.when(i == 0)
        def _():
            for j in range(nb):
                vt_s[j] = v_ref[j * MOBA_BLOCK:(j + 1) * MOBA_BLOCK, :].T.astype(bf16)

        k_heads = _head_lanes((MOBA_BLOCK, LANES))
        o_heads = _head_rows((LANES, MOBA_BLOCK))

        def block(j, qs, allow, m_old, l_old):
            kj, vtj = k_ref[j * MOBA_BLOCK:(j + 1) * MOBA_BLOCK, :], vt_s[j]
            m_new, alpha, l_new, pv = [], [], [], None
            for h in range(2):
                sc = jnp.where(allow[h], _dot_nt(jnp.where(k_heads[h], kj, 0.0).astype(bf16), qs), NEG)
                mx = jnp.max(sc, axis=0, keepdims=True)
                mn = mx if m_old is None else jnp.maximum(m_old[h], mx)
                p = jnp.exp(sc - mn)
                lsum = jnp.sum(p, axis=0, keepdims=True)
                if m_old is None:
                    a = None
                else:
                    a = jnp.exp(m_old[h] - mn)
                    lsum = a * l_old[h] + lsum
                t = _dot(jnp.where(o_heads[h], vtj, jnp.zeros_like(vtj)), p.astype(bf16))
                pv = t if pv is None else pv + t
                m_new.append(mn)
                alpha.append(a)
                l_new.append(lsum)
            return m_new, alpha, l_new, pv

        def query_block(c):
            q = q_ref[...]
            sels = _moba_select(q.astype(bf16), k_ref[...], c, nb)
            qs = (q * SCALE).astype(bf16)
            key = lax.broadcasted_iota(jnp.int32, (MOBA_BLOCK, MOBA_BLOCK), 0)
            qry = lax.broadcasted_iota(jnp.int32, (MOBA_BLOCK, MOBA_BLOCK), 1)
            m, _, l, acc = block(c, qs, [key <= qry] * 2, None, None)
            for j in range(c):
                m, alpha, l, pv = block(j, qs, [sels[h][j:j + 1, :] > 0.0 for h in range(2)], m, l)
                acc = acc * jnp.where(o_heads[0], alpha[0], alpha[1]) + pv
            o_ref[...] = (acc / jnp.where(o_heads[0], l[0], l[1])).T.astype(bf16)
            lse_ref[...] = _pair_rows(m[0] + jnp.log(l[0]), m[1] + jnp.log(l[1]), 0.0)

        for c in range(nb):
            pl.when(i == c)(functools.partial(query_block, c))

    return _cargo_call(
        body, cargo, name="attn_fwd", grid=(D_ATTN // LANES, nb),
        in_specs=[pl.BlockSpec((MOBA_BLOCK, LANES), lambda hp, i: (i, hp)),
                  pl.BlockSpec((s, LANES), lambda hp, i: (0, 4 + hp)),
                  pl.BlockSpec((s, LANES), lambda hp, i: (0, 8 + hp))],
        out_specs=[pl.BlockSpec((MOBA_BLOCK, LANES), lambda hp, i: (i, hp)),
                   pl.BlockSpec((None, STAT_ROWS, MOBA_BLOCK), lambda hp, i: (hp, 0, i))],
        out_shape=[jax.ShapeDtypeStruct((s, D_ATTN), bf16),
                   jax.ShapeDtypeStruct((D_ATTN // LANES, STAT_ROWS, s), f32)],
        scratch=[pltpu.VMEM((nb, LANES, MOBA_BLOCK), bf16)])(z, z, z)


def _window_select(snaps, g):
    return jnp.where(g == 0, snaps[0], jnp.where(g == 1, snaps[1], jnp.where(g == 2, snaps[2], snaps[3])))


def _pool_fwd(z, w_pool, pool_scale):
    s = z.shape[0]

    def body(u_ref, w_ref, sc_ref, pooled_ref, mixed_ref, pm_ref, pad):
        g = pl.program_id(0)
        u = u_ref[...]
        pad[0:MAX_WINDOW, :] = jnp.zeros((MAX_WINDOW, POOL_GROUP), f32)
        pad[MAX_WINDOW:MAX_WINDOW + s, :] = u
        acc = u
        snaps = []
        for d in range(1, MAX_WINDOW):
            acc = acc + pad[MAX_WINDOW - d:MAX_WINDOW - d + s, :]
            if d + 1 in (2, 4, 8, 16):
                snaps.append(acc)
        win = _window_select(snaps, g)
        t = lax.broadcasted_iota(jnp.int32, (s, POOL_GROUP), 0)
        count = jnp.minimum(t + 1, jnp.left_shift(2, g)).astype(f32)
        pooled = (win / count - u).astype(bf16)
        mixed = _dot(pooled, w_ref[...].astype(bf16))
        pooled_ref[...] = pooled
        mixed_ref[...] = mixed
        pm_ref[...] = (mixed * sc_ref[...]).astype(bf16)

    blk = pl.BlockSpec((s, POOL_GROUP), lambda g: (0, g))
    return _call(
        body, name="pool_fwd", grid=(4,),
        in_specs=[pl.BlockSpec((s, POOL_GROUP), lambda g: (0, 12 + g)),
                  pl.BlockSpec((None, POOL_GROUP, POOL_GROUP), lambda g: (g, 0, 0)),
                  pl.BlockSpec((1, POOL_GROUP), lambda g: (0, g))],
        out_specs=[blk, blk, blk],
        out_shape=[jax.ShapeDtypeStruct((s, D_POOL), bf16), jax.ShapeDtypeStruct((s, D_POOL), f32),
                   jax.ShapeDtypeStruct((s, D_POOL), bf16)],
        scratch=[pltpu.VMEM((s + MAX_WINDOW, POOL_GROUP), f32)])(z, w_pool, pool_scale)


def _branch_merge(o_bf, pm_bf, z, wba, wbp):
    s = o_bf.shape[0]
    tm = 512

    def body(o_ref, pm_ref, ga_ref, gp_ref, wba_ref, wbp_ref, m_ref):
        ya = _dot(o_ref[...], wba_ref[...])
        yp = _dot(pm_ref[...], wbp_ref[...])
        m_ref[...] = (ga_ref[...] * ya + gp_ref[...] * yp).astype(bf16)

    full = lambda r, c: pl.BlockSpec((r, c), lambda i: (0, 0))
    return _call(
        body, name="branch_merge", grid=(s // tm,),
        in_specs=[pl.BlockSpec((tm, D_ATTN), lambda i: (i, 0)), pl.BlockSpec((tm, D_POOL), lambda i: (i, 0)),
                  pl.BlockSpec((tm, D_MODEL), lambda i: (i, 2)), pl.BlockSpec((tm, D_MODEL), lambda i: (i, 3)),
                  full(D_ATTN, D_MODEL), full(D_POOL, D_MODEL)],
        out_specs=pl.BlockSpec((tm, D_MODEL), lambda i: (i, 0)),
        out_shape=jax.ShapeDtypeStruct((s, D_MODEL), bf16))(o_bf, pm_bf, z, z, wba, wbp)


def _out_ln1(m_bf, wout, x, ln_g, ln_b):
    s = x.shape[0]
    tm = 512

    def body(m_ref, w_ref, x_ref, g_ref, b_ref, x1_ref, xhat_ref, rstd_ref):
        h = ALPHA * x_ref[...] + _dot(m_ref[...], w_ref[...])
        y, xhat, rstd = _layer_norm(h, g_ref[...], b_ref[...])
        x1_ref[...] = y.astype(bf16)
        xhat_ref[...] = xhat
        rstd_ref[...] = jnp.broadcast_to(rstd, (tm, LANES))

    row = pl.BlockSpec((tm, D_MODEL), lambda i: (i, 0))
    vec = pl.BlockSpec((1, D_MODEL), lambda i: (0, 0))
    return _call(
        body, name="out_ln1", grid=(s // tm,),
        in_specs=[row, pl.BlockSpec((D_MODEL, D_MODEL), lambda i: (0, 0)), row, vec, vec],
        out_specs=[row, row, pl.BlockSpec((tm, LANES), lambda i: (i, 0))],
        out_shape=[jax.ShapeDtypeStruct((s, D_MODEL), bf16), jax.ShapeDtypeStruct((s, D_MODEL), f32),
                   jax.ShapeDtypeStruct((s, LANES), f32)])(m_bf, wout, x, ln_g, ln_b)


FF_TILE = 256
FF_TILES_PER_SHARD = FF_PAD // FF_TILE
CONV_PAD = 8
CONV_ROWS = 16

def _ff_weight_spec(rows):
    return pl.BlockSpec((None, rows, FF_TILE), lambda j: (j // FF_TILES_PER_SHARD, 0, j % FF_TILES_PER_SHARD))


def _ff_rows_spec():
    return pl.BlockSpec((None, FF_TILE, D_MODEL), lambda j: (j // FF_TILES_PER_SHARD, j % FF_TILES_PER_SHARD, 0))


def _row_shifts(x, shifts):
    s = x.shape[0]
    padded = jnp.concatenate([x, jnp.zeros((CONV_PAD, x.shape[1]), x.dtype)], axis=0)
    return [pltpu.roll(padded, d % (s + CONV_PAD), 0)[0:s] for d in shifts]


def _ffn_up(x1_bf, wg_g, wu_g, convw_g, convb_g, cargo):
    s = x1_bf.shape[0]

    def body(x_ref, wg_ref, wu_ref, cw_ref, cb_ref, a_ref, up_ref, ac_ref, hh_ref):
        x1 = x_ref[...]
        a = _dot_nt(x1, wg_ref[...])
        up = _dot_nt(x1, wu_ref[...])
        a_ref[...] = a
        up_ref[...] = up
        a1, a2 = _row_shifts(a, (1, 2))
        ac = a * cw_ref[2:3, :] + a1 * cw_ref[1:2, :] + a2 * cw_ref[0:1, :] + cb_ref[...]
        ac_ref[...] = ac
        hg, _ = _gelu_parts(ac)
        hh_ref[...] = (hg * up).astype(bf16)

    col = pl.BlockSpec((s, FF_TILE), lambda j: (0, j))
    wide = jax.ShapeDtypeStruct((s, D_FF_PAD), f32)
    return _call(
        body, name="ffn_up", grid=(D_FF_PAD // FF_TILE,),
        in_specs=[pl.BlockSpec((s, D_MODEL), lambda j: (0, 0)), _ff_rows_spec(), _ff_rows_spec(),
                  _ff_weight_spec(8), _ff_weight_spec(1)],
        out_specs=[col, col, col, col],
        out_shape=[wide, wide, wide, jax.ShapeDtypeStruct((s, D_FF_PAD), bf16)],
        cargo=cargo)(x1_bf, wg_g, wu_g, convw_g, convb_g)


def _ffn_down_ln2_loss(hh_bf, wd, xhat1, ln1_g, ln1_b, ln2_g, ln2_b, target):
    s = hh_bf.shape[0]
    tm = 256

    def body(hh_ref, w_ref, xh1_ref, g1_ref, b1_ref, g2_ref, b2_ref, t_ref, loss_ref, dh_ref, dhb_ref, dg_ref, db_ref):
        i = pl.program_id(0)
        x1 = xh1_ref[...] * g1_ref[...] + b1_ref[...]
        h = ALPHA * x1 + _dot(hh_ref[...], w_ref[...])
        y, xhat, rstd = _layer_norm(h, g2_ref[...], b2_ref[...])
        err = y - t_ref[...]
        part = 0.5 * jnp.sum(jnp.mean(err * err, axis=-1, keepdims=True), axis=0, keepdims=True)
        dy = err * (1.0 / D_MODEL)

        @pl.when(i == 0)
        def _():
            loss_ref[...] = jnp.zeros_like(loss_ref)
            dg_ref[...] = jnp.zeros_like(dg_ref)
            db_ref[...] = jnp.zeros_like(db_ref)

        loss_ref[...] += jnp.broadcast_to(part, loss_ref.shape)
        dg_ref[...] += jnp.sum(dy * xhat, axis=0, keepdims=True)
        db_ref[...] += jnp.sum(dy, axis=0, keepdims=True)
        dh = _layer_norm_bwd(dy, xhat, rstd, g2_ref[...])
        dh_ref[...] = dh
        dhb_ref[...] = dh.astype(bf16)

    row = pl.BlockSpec((tm, D_MODEL), lambda i: (i, 0))
    vec = pl.BlockSpec((1, D_MODEL), lambda i: (0, 0))
    return _call(
        body, name="ffn_down_ln2_loss", grid=(s // tm,),
        in_specs=[pl.BlockSpec((tm, D_FF_PAD), lambda i: (i, 0)),
                  pl.BlockSpec((D_FF_PAD, D_MODEL), lambda i: (0, 0)),
                  row, vec, vec, vec, vec, row],
        out_specs=[pl.BlockSpec((8, LANES), lambda i: (0, 0)), row, row, vec, vec],
        out_shape=[jax.ShapeDtypeStruct((8, LANES), f32), jax.ShapeDtypeStruct((s, D_MODEL), f32),
                   jax.ShapeDtypeStruct((s, D_MODEL), bf16), jax.ShapeDtypeStruct((1, D_MODEL), f32),
                   jax.ShapeDtypeStruct((1, D_MODEL), f32)])(hh_bf, wd, xhat1, ln1_g, ln1_b, ln2_g, ln2_b, target)


def _ffn_act_bwd(dh2_bf, wd_g, ac, a, up, convw_g):
    s = dh2_bf.shape[0]

    def body(dh_ref, wd_ref, ac_ref, a_ref, up_ref, cw_ref, da_ref, dup_ref, dc_ref):
        dhh = _dot_nt(dh_ref[...], wd_ref[...])
        hg, dgelu = _gelu_parts(ac_ref[...])
        dup_ref[...] = (dhh * hg).astype(bf16)
        dac = dhh * up_ref[...] * dgelu
        dac1, dac2 = _row_shifts(dac, (-1, -2))
        a = a_ref[...]
        rows = [jnp.sum(d * a, axis=0, keepdims=True) for d in (dac2, dac1, dac)]
        rows.append(jnp.sum(dac, axis=0, keepdims=True))
        rows.append(jnp.zeros((CONV_ROWS - 4, FF_TILE), f32))
        dc_ref[...] = jnp.concatenate(rows, axis=0)
        da_ref[...] = (dac * cw_ref[2:3, :] + dac1 * cw_ref[1:2, :] + dac2 * cw_ref[0:1, :]).astype(bf16)

    col = pl.BlockSpec((s, FF_TILE), lambda j: (0, j))
    return _call(
        body, name="ffn_act_bwd", grid=(D_FF_PAD // FF_TILE,),
        in_specs=[pl.BlockSpec((s, D_MODEL), lambda j: (0, 0)), _ff_rows_spec(), col, col, col, _ff_weight_spec(8)],
        out_specs=[col, col, _ff_weight_spec(CONV_ROWS)],
        out_shape=[jax.ShapeDtypeStruct((s, D_FF_PAD), bf16), jax.ShapeDtypeStruct((s, D_FF_PAD), bf16),
                   jax.ShapeDtypeStruct((N_CHIPS, CONV_ROWS, FF_PAD), f32)],
    )(dh2_bf, wd_g, ac, a, up, convw_g)


def _matmul_tn(a, b, n_shards, name, tm=512, tn=1024, cargo=None):
    k, m = a.shape
    n = b.shape[1]
    tm, tn = min(tm, m), min(tn, n // n_shards)
    per = n // n_shards // tn

    def body(a_ref, b_ref, o_ref, at_s):
        @pl.when(pl.program_id(1) == 0)
        def _():
            at_s[...] = a_ref[...].T

        o_ref[...] = _dot(at_s[...], b_ref[...]).astype(bf16)

    res = _call(
        body, name=name, grid=(m // tm, n // tn),
        in_specs=[pl.BlockSpec((k, tm), lambda i, j: (0, i)), pl.BlockSpec((k, tn), lambda i, j: (0, j))],
        out_specs=[pl.BlockSpec((None, tm, tn), lambda i, j: (j // per, i, j % per))],
        out_shape=[jax.ShapeDtypeStruct((n_shards, m, n // n_shards), bf16)], scratch=[pltpu.VMEM((tm, k), bf16)],
        cargo=cargo)(a, b)
    return res[0] if cargo is None else (res[0][0], res[1])


def _ffn_in_bwd_ln1(da_bf, dup_bf, wg_g, wu_g, dh2, xhat1, rstd1, ln1_g, cargo):
    s = da_bf.shape[0]
    tm = 256

    def body(da_ref, dup_ref, wg_ref, wu_ref, dh2_ref, xh_ref, rstd_ref, g_ref, dh_ref, dhb_ref, dg_ref, db_ref):
        i = pl.program_id(0)
        dx1 = ALPHA * dh2_ref[...]
        for sh in range(N_CHIPS):
            sl = slice(sh * FF_PAD, (sh + 1) * FF_PAD)
            dx1 = dx1 + _dot(da_ref[:, sl], wg_ref[sh]) + _dot(dup_ref[:, sl], wu_ref[sh])
        xhat = xh_ref[...]

        @pl.when(i == 0)
        def _():
            dg_ref[...] = jnp.zeros_like(dg_ref)
            db_ref[...] = jnp.zeros_like(db_ref)

        dg_ref[...] += jnp.sum(dx1 * xhat, axis=0, keepdims=True)
        db_ref[...] += jnp.sum(dx1, axis=0, keepdims=True)
        dh = _layer_norm_bwd(dx1, xhat, rstd_ref[:, 0:1], g_ref[...])
        dh_ref[...] = dh
        dhb_ref[...] = dh.astype(bf16)

    row = pl.BlockSpec((tm, D_MODEL), lambda i: (i, 0))
    wide = pl.BlockSpec((tm, D_FF_PAD), lambda i: (i, 0))
    wfull = pl.BlockSpec((N_CHIPS, FF_PAD, D_MODEL), lambda i: (0, 0, 0))
    vec = pl.BlockSpec((1, D_MODEL), lambda i: (0, 0))
    return _call(
        body, name="ffn_in_bwd_ln1", grid=(s // tm,),
        in_specs=[wide, wide, wfull, wfull, row, row, pl.BlockSpec((tm, LANES), lambda i: (i, 0)), vec],
        out_specs=[row, row, vec, vec],
        out_shape=[jax.ShapeDtypeStruct((s, D_MODEL), f32), jax.ShapeDtypeStruct((s, D_MODEL), bf16),
                   jax.ShapeDtypeStruct((1, D_MODEL), f32), jax.ShapeDtypeStruct((1, D_MODEL), f32)],
        cargo=cargo)(da_bf, dup_bf, wg_g, wu_g, dh2, xhat1, rstd1, ln1_g)


def _merge_bwd(dh1_bf, wout, o_bf, pm_bf, z, wba, wbp):
    s = dh1_bf.shape[0]
    tm = 256

    def body(dh_ref, wout_ref, o_ref, pm_ref, ga_ref, gp_ref, wba_ref, wbp_ref,
             dzg_ref, dya_ref, dyp_ref, do_ref, dpm_ref, dbg_ref):
        i = pl.program_id(0)
        dm = _dot_nt(dh_ref[...], wout_ref[...])
        ya = _dot(o_ref[...], wba_ref[...])
        yp = _dot(pm_ref[...], wbp_ref[...])
        ga, gp = ga_ref[...], gp_ref[...]
        dza = dm * ya * ga * (1.0 - ga)
        dzp = dm * yp * gp * (1.0 - gp)

        @pl.when(i == 0)
        def _():
            dbg_ref[...] = jnp.zeros_like(dbg_ref)

        dbg_ref[:, 0:D_MODEL] += jnp.sum(dza, axis=0, keepdims=True)
        dbg_ref[:, D_MODEL:2 * D_MODEL] += jnp.sum(dzp, axis=0, keepdims=True)
        dzg_ref[:, 0:D_MODEL] = dza.astype(bf16)
        dzg_ref[:, D_MODEL:2 * D_MODEL] = dzp.astype(bf16)
        dya = (dm * ga).astype(bf16)
        dyp = (dm * gp).astype(bf16)
        dya_ref[...] = dya
        dyp_ref[...] = dyp
        do_ref[...] = _dot_nt(dya, wba_ref[...]).astype(bf16)
        dpm_ref[...] = _dot_nt(dyp, wbp_ref[...])

    row = pl.BlockSpec((tm, D_MODEL), lambda i: (i, 0))
    half = pl.BlockSpec((tm, D_ATTN), lambda i: (i, 0))
    full = lambda r, c: pl.BlockSpec((r, c), lambda i: (0, 0))
    return _call(
        body, name="merge_bwd", grid=(s // tm,),
        in_specs=[row, full(D_MODEL, D_MODEL), half, half,
                  pl.BlockSpec((tm, D_MODEL), lambda i: (i, 2)), pl.BlockSpec((tm, D_MODEL), lambda i: (i, 3)),
                  full(D_ATTN, D_MODEL), full(D_POOL, D_MODEL)],
        out_specs=[pl.BlockSpec((tm, 2 * D_MODEL), lambda i: (i, 0)), row, row, half, half,
                   pl.BlockSpec((1, 2 * D_MODEL), lambda i: (0, 0))],
        out_shape=[jax.ShapeDtypeStruct((s, 2 * D_MODEL), bf16), jax.ShapeDtypeStruct((s, D_MODEL), bf16),
                   jax.ShapeDtypeStruct((s, D_MODEL), bf16), jax.ShapeDtypeStruct((s, D_ATTN), bf16),
                   jax.ShapeDtypeStruct((s, D_POOL), f32), jax.ShapeDtypeStruct((1, 2 * D_MODEL), f32)],
    )(dh1_bf, wout, o_bf, pm_bf, z, z, wba, wbp)


def _pool_bwd(dpm, mixed, pooled_bf, w_pool, pool_scale):
    s = dpm.shape[0]

    def body(dpm_ref, mixed_ref, pooled_ref, w_ref, sc_ref, du_ref, dw_ref, dsc_ref, pad):
        g = pl.program_id(0)
        dpm_v = dpm_ref[...]
        dsc_ref[...] = jnp.sum(dpm_v * mixed_ref[...], axis=0, keepdims=True)
        dmixed = (dpm_v * sc_ref[...]).astype(bf16)
        dw_ref[...] = _dot_tn(pooled_ref[...], dmixed)
        dpooled = _dot_nt(dmixed, w_ref[...].astype(bf16))
        t = lax.broadcasted_iota(jnp.int32, (s, POOL_GROUP), 0)
        count = jnp.minimum(t + 1, jnp.left_shift(2, g)).astype(f32)
        r = dpooled / count
        pad[0:s, :] = r
        pad[s:s + MAX_WINDOW, :] = jnp.zeros((MAX_WINDOW, POOL_GROUP), f32)
        acc = r
        snaps = []
        for d in range(1, MAX_WINDOW):
            acc = acc + pad[d:d + s, :]
            if d + 1 in (2, 4, 8, 16):
                snaps.append(acc)
        du_ref[...] = (_window_select(snaps, g) - dpooled).astype(bf16)

    blk = pl.BlockSpec((s, POOL_GROUP), lambda g: (0, g))
    return _call(
        body, name="pool_bwd", grid=(4,),
        in_specs=[blk, blk, blk, pl.BlockSpec((None, POOL_GROUP, POOL_GROUP), lambda g: (g, 0, 0)),
                  pl.BlockSpec((1, POOL_GROUP), lambda g: (0, g))],
        out_specs=[blk, pl.BlockSpec((None, POOL_GROUP, POOL_GROUP), lambda g: (g, 0, 0)),
                   pl.BlockSpec((1, POOL_GROUP), lambda g: (0, g))],
        out_shape=[jax.ShapeDtypeStruct((s, D_POOL), bf16), jax.ShapeDtypeStruct((4, POOL_GROUP, POOL_GROUP), f32),
                   jax.ShapeDtypeStruct((1, D_POOL), f32)],
        scratch=[pltpu.VMEM((s + MAX_WINDOW, POOL_GROUP), f32)])(dpm, mixed, pooled_bf, w_pool, pool_scale)


def _attn_bwd(z, o_bf, lse, do_bf, cos_t, sin_t, nb, cargo):
    s = z.shape[0]

    def body(q_ref, k_ref, v_ref, o_ref, lse_ref, do_ref, cq_ref, sq_ref, cf_ref, sf_ref,
             dq_ref, dk_ref, dv_ref, kt_s, dk_s, dv_s):
        i = pl.program_id(1)

        @pl.when(i == 0)
        def _():
            dk_s[...] = jnp.zeros_like(dk_s)
            dv_s[...] = jnp.zeros_like(dv_s)
            for j in range(nb):
                kt_s[j] = k_ref[j * MOBA_BLOCK:(j + 1) * MOBA_BLOCK, :].T.astype(bf16)

        k_heads = _head_lanes((MOBA_BLOCK, LANES))
        t_heads = _head_rows((LANES, MOBA_BLOCK))

        def query_block(c):
            q = q_ref[...]
            sels = _moba_select(q.astype(bf16), k_ref[...], c, nb)
            qs = (q * SCALE).astype(bf16)
            qraw = [jnp.where(hm, q, 0.0).astype(bf16) for hm in k_heads]
            do = do_ref[...].astype(f32)
            do_bf = do_ref[...]
            dob = [jnp.where(hm, do, 0.0).astype(bf16) for hm in k_heads]
            pick = _head_lanes((STAT_ROWS, LANES))
            r8 = lax.broadcasted_iota(jnp.int32, (STAT_ROWS, LANES), 0)
            head_pick = jnp.where(((r8 == 0) & pick[0]) | ((r8 == 1) & pick[1]), 1.0, 0.0)
            delta8 = lax.dot_general(head_pick, do * o_ref[...].astype(f32), (((1,), (1,)), ((), ())),
                                     precision=lax.Precision.HIGHEST, preferred_element_type=f32)
            lse8 = lse_ref[...]
            delta, lse_h = [delta8[0:1, :], delta8[1:2, :]], [lse8[0:1, :], lse8[1:2, :]]
            key = lax.broadcasted_iota(jnp.int32, (MOBA_BLOCK, MOBA_BLOCK), 0)
            qry = lax.broadcasted_iota(jnp.int32, (MOBA_BLOCK, MOBA_BLOCK), 1)

            def block(j, allow):
                rows = slice(j * MOBA_BLOCK, (j + 1) * MOBA_BLOCK)
                kj, vj, ktj = k_ref[rows, :], v_ref[rows, :], kt_s[j]
                dqt, dk, dv = None, None, None
                for h in range(2):
                    sc = _dot_nt(jnp.where(k_heads[h], kj, 0.0).astype(bf16), qs)
                    p = jnp.where(allow[h], jnp.exp(sc - lse_h[h]), 0.0)
                    dp = _dot_nt(jnp.where(k_heads[h], vj, 0.0).astype(bf16), do_bf)
                    ds = (p * (dp - delta[h]) * SCALE).astype(bf16)
                    tv = _dot(p.astype(bf16), dob[h])
                    tk = _dot(ds, qraw[h])
                    tq = _dot(jnp.where(t_heads[h], ktj, jnp.zeros_like(ktj)), ds)
                    dqt, dk, dv = (tq, tk, tv) if dqt is None else (dqt + tq, dk + tk, dv + tv)
                dk_s[rows, :] += dk
                dv_s[rows, :] += dv
                return dqt

            dqt = block(c, [key <= qry] * 2)
            for j in range(c):
                dqt = dqt + block(j, [sels[h][j:j + 1, :] > 0.0 for h in range(2)])
            dq_ref[...] = _rope_bwd(dqt.T, cq_ref[...], sq_ref[...]).astype(bf16)

        for c in range(nb):
            pl.when(i == c)(functools.partial(query_block, c))

        @pl.when(i == nb - 1)
        def _():
            dk_ref[...] = _rope_bwd(dk_s[...], cf_ref[...], sf_ref[...]).astype(bf16)
            dv_ref[...] = dv_s[...].astype(bf16)

    qblk = pl.BlockSpec((MOBA_BLOCK, LANES), lambda hp, i: (i, hp))
    tq = pl.BlockSpec((MOBA_BLOCK, LANES), lambda hp, i: (i, 0))
    tf = pl.BlockSpec((s, LANES), lambda hp, i: (0, 0))
    colblk = pl.BlockSpec((s, LANES), lambda hp, i: (0, hp))
    return _cargo_call(
        body, cargo, name="attn_bwd", grid=(D_ATTN // LANES, nb),
        in_specs=[qblk, pl.BlockSpec((s, LANES), lambda hp, i: (0, 4 + hp)),
                  pl.BlockSpec((s, LANES), lambda hp, i: (0, 8 + hp)), qblk,
                  pl.BlockSpec((None, STAT_ROWS, MOBA_BLOCK), lambda hp, i: (hp, 0, i)), qblk, tq, tq, tf, tf],
        out_specs=[qblk, colblk, colblk],
        out_shape=[jax.ShapeDtypeStruct((s, D_ATTN), bf16)] * 3,
        scratch=[pltpu.VMEM((nb, LANES, MOBA_BLOCK), bf16), pltpu.VMEM((s, LANES), f32), pltpu.VMEM((s, LANES), f32)],
    )(z, z, z, o_bf, lse, do_bf, cos_t, sin_t, cos_t, sin_t)


def _inproj_dx(dz_bf, win_g, dh1, cargo):
    s = dz_bf.shape[0]
    tm = 256

    def body(dz_ref, w_ref, dh_ref, gx_ref):
        acc = ALPHA * dh_ref[...]
        for sh in range(N_CHIPS):
            acc = acc + _dot_nt(dz_ref[:, sh * D_MODEL:(sh + 1) * D_MODEL], w_ref[sh])
        gx_ref[...] = acc

    row = pl.BlockSpec((tm, D_MODEL), lambda i: (i, 0))
    return _call(
        body, name="inproj_dx", grid=(s // tm,),
        in_specs=[pl.BlockSpec((tm, 4 * D_MODEL), lambda i: (i, 0)),
                  pl.BlockSpec((N_CHIPS, D_MODEL, D_MODEL), lambda i: (0, 0, 0)), row],
        out_specs=[row], out_shape=[jax.ShapeDtypeStruct((s, D_MODEL), f32)], cargo=cargo)(dz_bf, win_g, dh1)


ANY = pl.BlockSpec(memory_space=pl.ANY)


def _chip_index():
    return 2 * lax.axis_index("x") + lax.axis_index("y")


def _peer(k):
    x, y, c = lax.axis_index("x"), lax.axis_index("y"), lax.axis_index("c")
    return (x ^ (k >> 1), y ^ (k & 1), c)


def _sibling():
    return (lax.axis_index("x"), lax.axis_index("y"), 1 - lax.axis_index("c"))


class _GatherCargo:
    def __init__(self, shards, pass_on):
        self.shards, self.pass_on, n = shards, pass_on, len(shards)
        self.inputs = [a.reshape(2, a.shape[0] // 2, a.shape[1]) for a in shards]
        self.out_shape = [jax.ShapeDtypeStruct((N_CHIPS,) + a.shape, a.dtype) for a in self.inputs]
        sem, one = pltpu.SemaphoreType.DMA((N_CHIPS - 1, n)), pltpu.SemaphoreType.DMA((n,))
        self.sems = [sem, sem, sem, sem, one, one]

    def _copies(self, src, dst, sems, a):
        send_sems, recv_sems, fsend_sems, frecv_sems, local_sems, own_sems = sems
        p, c = _chip_index(), lax.axis_index("c")
        own = pltpu.make_async_remote_copy(src_ref=src[a], dst_ref=dst[a].at[p], send_sem=local_sems.at[a],
                                           recv_sem=own_sems.at[a], device_id=_sibling(), device_id_type=MESH)
        out, arrive, onward, landed = [], [], [], []
        for k in range(1, N_CHIPS):
            ici = dict(send_sem=send_sems.at[k - 1, a], recv_sem=recv_sems.at[k - 1, a], device_id=_peer(k),
                       device_id_type=MESH)
            d2d = dict(send_sem=fsend_sems.at[k - 1, a], recv_sem=frecv_sems.at[k - 1, a], device_id=_sibling(),
                       device_id_type=MESH)
            got, theirs = dst[a].at[p ^ k, c], dst[a].at[p ^ k, 1 - c]
            out.append(pltpu.make_async_remote_copy(src_ref=src[a].at[c], dst_ref=dst[a].at[p, c], **ici))
            arrive.append(pltpu.make_async_remote_copy(src_ref=src[a].at[c], dst_ref=got, **ici))
            onward.append(pltpu.make_async_remote_copy(src_ref=got, dst_ref=got, **d2d))
            landed.append(pltpu.make_async_remote_copy(src_ref=theirs, dst_ref=theirs, **d2d))
        return own, out, arrive, onward, landed

    def stages(self, steps):
        n = len(self.shards)

        def start(src, dst, sems):
            for a in range(n):
                own, out, _, _, _ = self._copies(src, dst, sems, a)
                own.start()
                for cp in out:
                    cp.start()

        def pass_on(a):
            def act(src, dst, sems):
                _, _, arrive, onward, _ = self._copies(src, dst, sems, a)
                for k in range(N_CHIPS - 1):
                    arrive[k].wait_recv()
                    onward[k].start()
            return act

        def finish(src, dst, sems):
            for a in range(n):
                own, out, _, onward, landed = self._copies(src, dst, sems, a)
                for cp in landed:
                    cp.wait_recv()
                for cp in out + onward:
                    cp.wait_send()
                own.wait()

        mids = [(min(steps - 1, int(self.pass_on[a] * steps)), pass_on(a)) for a in range(n)]
        return [(0, start)] + mids + [(steps - 1, finish)]

    def results(self, outs):
        return [o.reshape((N_CHIPS,) + a.shape) for o, a in zip(outs, self.shards)]


class _ExchangeCargo:
    def __init__(self, sums, whole):
        self.inputs, self.whole, n = sums, whole, len(sums)
        self.out_shape = [jax.ShapeDtypeStruct((N_CHIPS - 1,) + g.shape[1:], g.dtype) for g in sums]
        sem = pltpu.SemaphoreType.DMA((N_CHIPS - 1, n))
        self.sems = [sem, sem]

    def _copies(self, src, dst, sems):
        send_sems, recv_sems = sems
        p = _chip_index()
        return [pltpu.make_async_remote_copy(
            src_ref=src[a].at[0] if self.whole[a] else src[a].at[p ^ k], dst_ref=dst[a].at[k - 1],
            send_sem=send_sems.at[k - 1, a], recv_sem=recv_sems.at[k - 1, a], device_id=_peer(k), device_id_type=MESH)
            for k in range(1, N_CHIPS) for a in range(len(self.inputs))]

    def stages(self, steps):
        def start(src, dst, sems):
            for cp in self._copies(src, dst, sems):
                cp.start()

        def finish(src, dst, sems):
            copies = self._copies(src, dst, sems)
            for cp in copies:
                cp.wait_recv()
            for cp in copies:
                cp.wait_send()

        return [(0, start), (steps - 1, finish)]

    def results(self, outs):
        return list(outs)


def _cargo_call(body, cargo, *, name, grid, in_specs, out_specs, out_shape, scratch=()):
    n_in, n_out, n_scr = len(in_specs), len(out_specs), len(scratch)
    c_in, c_out = len(cargo.inputs), len(cargo.out_shape)
    steps = 1
    for g in grid:
        steps *= g
    stages = cargo.stages(steps)

    def wrapped(*refs):
        ins, refs = refs[:n_in], refs[n_in:]
        cin, refs = refs[:c_in], refs[c_in:]
        outs, refs = refs[:n_out], refs[n_out:]
        cout, refs = refs[:c_out], refs[c_out:]
        scr, sems = refs[:n_scr], refs[n_scr:]
        if not grid:
            for _, act in stages:
                act(cin, cout, sems)
            return
        step = 0
        for d in range(len(grid)):
            step = step * grid[d] + pl.program_id(d)
        pl.when(step == 0)(functools.partial(stages[0][1], cin, cout, sems))
        body(*ins, *outs, *scr)
        for at, act in stages[1:]:
            pl.when(step == at)(functools.partial(act, cin, cout, sems))

    params = dict(vmem_limit_bytes=VMEM_LIMIT)
    if grid:
        params["dimension_semantics"] = ("arbitrary",) * len(grid)
    res = pl.pallas_call(
        wrapped, name=name, grid=grid, in_specs=list(in_specs) + [ANY] * c_in, out_specs=list(out_specs) + [ANY] * c_out,
        out_shape=list(out_shape) + cargo.out_shape, scratch_shapes=list(scratch) + cargo.sems,
        compiler_params=pltpu.CompilerParams(**params))
    return lambda *args: (lambda r: (r[:n_out], cargo.results(r[n_out:])))(res(*args, *cargo.inputs))


def _split_halves(g):
    return g.reshape(g.shape[0], 2, g.shape[1] // 2, g.shape[2])


def _ew_tile(rows):
    return rows if rows <= 384 else 256


def _batched_call(items, name, place=None):
    offs, total = [], 0
    for it in items:
        offs.append(total)
        total += it[0]
    in_specs, out_specs, out_shape, args, spans = [], [], [], [], []
    for (n, ins, outs, shapes, operands, _), off in zip(items, offs):
        tile = lambda t, off=off, n=n: jnp.clip(t - off, 0, n - 1)
        i_s, o_s = ins(tile), outs(tile)
        spans.append((len(i_s), len(o_s)))
        in_specs += i_s
        out_specs += o_s
        out_shape += shapes
        args += operands
    n_in = len(in_specs)

    def body(*refs):
        if place is not None:
            refs = refs[1:]
        t = pl.program_id(0)
        i0, o0 = 0, n_in
        for (n, _, _, _, _, compute), off, (ni, no) in zip(items, offs, spans):
            pl.when((t >= off) & (t < off + n))(functools.partial(compute, refs[i0:i0 + ni], refs[o0:o0 + no]))
            i0, o0 = i0 + ni, o0 + no

    params = pltpu.CompilerParams(dimension_semantics=("arbitrary",), vmem_limit_bytes=VMEM_LIMIT)
    if place is None:
        res = pl.pallas_call(body, name=name, grid=(total,), in_specs=in_specs, out_specs=out_specs,
                             out_shape=out_shape, compiler_params=params)(*args)
    else:
        res = pl.pallas_call(
            body, name=name, out_shape=out_shape, compiler_params=params,
            grid_spec=pltpu.PrefetchScalarGridSpec(num_scalar_prefetch=1, grid=(total,), in_specs=in_specs,
                                                   out_specs=out_specs))(place, *args)
    out, o0 = [], 0
    for _, no in spans:
        out.append(res[o0:o0 + no])
        o0 += no
    return out


def _presum_swap(split, name):
    n = len(split)

    def body(*refs):
        src, got = refs[:n], refs[n:2 * n]
        send_sems, recv_sems = refs[2 * n:]
        c = lax.axis_index("c")
        remote = [pltpu.make_async_remote_copy(src_ref=src[a].at[:, 1 - c], dst_ref=got[a],
                                               send_sem=send_sems.at[a], recv_sem=recv_sems.at[a],
                                               device_id=_sibling(), device_id_type=MESH) for a in range(n)]
        for cp in remote:
            cp.start()
        for cp in remote:
            cp.wait_recv()
        for cp in remote:
            cp.wait_send()

    sem = pltpu.SemaphoreType.DMA((n,))
    return pl.pallas_call(
        body, name=name, in_specs=[ANY] * n, out_specs=[ANY] * n,
        out_shape=[jax.ShapeDtypeStruct((g.shape[0],) + g.shape[2:], g.dtype) for g in split],
        scratch_shapes=[sem, sem])(*split)


def _add_pairs(pairs, place, name):
    def item(mine, theirs):
        lead, _, rows, cols = mine.shape
        tm = _ew_tile(rows)

        def compute(ins, outs):
            outs[0][...] = (ins[0][...].astype(f32) + ins[1][...].astype(f32)).astype(outs[0].dtype)

        blk = lambda tile: pl.BlockSpec((lead, tm, cols), lambda t, pc: (0, tile(t), 0))
        mine_half = lambda tile: pl.BlockSpec((lead, None, tm, cols), lambda t, pc: (0, pc[1], tile(t), 0))
        return (rows // tm, lambda tile: [mine_half(tile), blk(tile)], lambda tile: [blk(tile)],
                [jax.ShapeDtypeStruct(theirs.shape, theirs.dtype)], [mine, theirs], compute)

    return [out[0] for out in _batched_call([item(a, b) for a, b in pairs], name, place)]


class _NoCargo:
    def __init__(self, results=()):
        self.inputs, self.out_shape, self.sems, self._results = [], [], [], list(results)

    def stages(self, steps):
        return [(0, lambda src, dst, sems: None), (steps - 1, lambda src, dst, sems: None)]

    def results(self, outs):
        return self._results


def _cargo_alone(cargo, name):
    return _cargo_call(None, cargo, name=name, grid=(), in_specs=[], out_specs=[], out_shape=[])()[1]


def _sum_slots(jobs, place, name):
    def item(own, recv, whole):
        _, rows, cols = own.shape
        tm = _ew_tile(rows)

        def compute(ins, outs):
            r0, r1, r2 = [ins[1][k].astype(f32) for k in range(N_CHIPS - 1)]
            outs[0][...] = (ins[0][...].astype(f32) + r0) + (r1 + r2)

        ins = lambda tile: [pl.BlockSpec((None, tm, cols), lambda t, pc: (0 if whole else pc[0], tile(t), 0)),
                            pl.BlockSpec((N_CHIPS - 1, tm, cols), lambda t, pc: (0, tile(t), 0))]
        outs = lambda tile: [pl.BlockSpec((None, tm, cols), lambda t, pc: (pc[1], tile(t), 0))]
        return rows // tm, ins, outs, [jax.ShapeDtypeStruct((2, rows, cols), f32)], [own, recv], compute

    return [out[0] for out in _batched_call([item(*job) for job in jobs], name, place)]


def _sibling_fill(bufs):
    n = len(bufs)

    def body(*refs):
        src, dst = refs[:n], refs[n:2 * n]
        send_sems, recv_sems = refs[2 * n:]
        c = lax.axis_index("c")
        remote = [pltpu.make_async_remote_copy(src_ref=src[a].at[c], dst_ref=dst[a].at[c], send_sem=send_sems.at[a],
                                               recv_sem=recv_sems.at[a], device_id=_sibling(), device_id_type=MESH)
                  for a in range(n)]
        for cp in remote:
            cp.start()
        for a in range(n):
            pltpu.make_async_remote_copy(src_ref=src[a].at[c], dst_ref=dst[a].at[1 - c], send_sem=send_sems.at[a],
                                         recv_sem=recv_sems.at[a], device_id=_sibling(), device_id_type=MESH).wait_recv()
        for cp in remote:
            cp.wait_send()

    sem = pltpu.SemaphoreType.DMA((n,))
    out = pl.pallas_call(
        body, name="sibling_fill", in_specs=[ANY] * n, out_specs=[ANY] * n,
        out_shape=[jax.ShapeDtypeStruct(b.shape, b.dtype) for b in bufs],
        input_output_aliases={a: a for a in range(n)}, scratch_shapes=[sem, sem])(*bufs)
    return [o.reshape(2 * b.shape[1], b.shape[2]) for o, b in zip(out, bufs)]


def _adamw(jobs, name):
    def item(w, m, v, grad, tm):
        rows, cols = w.shape
        gcols = grad.shape[1]

        def compute(ins, outs):
            w_, m_, v_ = ins[0][...], ins[1][...], ins[2][...]
            g = ins[3][0:tm, 0:cols]
            m_ = ADAM_B1 * m_ + (1.0 - ADAM_B1) * g
            v_ = ADAM_B2 * v_ + (1.0 - ADAM_B2) * jnp.square(g)
            m_hat = m_ / (1.0 - ADAM_B1 ** ADAM_STEP)
            v_hat = v_ / (1.0 - ADAM_B2 ** ADAM_STEP)
            outs[0][...] = g
            outs[1][...] = -ADAM_LR * (m_hat / (jnp.sqrt(v_hat) + ADAM_EPS) + ADAM_WD * w_)
            outs[2][...] = m_
            outs[3][...] = v_

        blk = lambda tile: pl.BlockSpec((tm, cols), lambda t: (tile(t), 0))
        if tm == rows:
            gspec = lambda tile: pl.BlockSpec(grad.shape, lambda t: (0, 0))
        else:
            gspec = lambda tile: pl.BlockSpec((tm, gcols), lambda t: (tile(t), 0))
        return (rows // tm, lambda tile: [blk(tile)] * 3 + [gspec(tile)], lambda tile: [blk(tile)] * 4,
                [jax.ShapeDtypeStruct((rows, cols), f32)] * 4, [w, m, v, grad], compute)

    return _batched_call([item(*job) for job in jobs], name)


def _rope_tables(s):
    half = HEAD_DIM // 2
    inv_freq = 1.0 / (10000.0 ** (jnp.arange(half, dtype=f32) / half))
    ang = jnp.arange(s, dtype=f32)[:, None] * inv_freq[None, :]
    cos, sin = jnp.cos(ang), jnp.sin(ang)
    return jnp.tile(cos, (1, LANES // half)), jnp.tile(jnp.concatenate([-sin, sin], axis=1), (1, LANES // HEAD_DIM))


def _local_step(x, target, win_g, gather, exchange, b_gate, w_pool, pool_scale, ln1_g, ln1_b, convb_g, ln2_g, ln2_b):
    s = x.shape[0]
    nb = s // MOBA_BLOCK
    cos_t, sin_t = _rope_tables(s)
    x_bf = x.astype(bf16)

    z, (wba_g, wbp_g, wout_g) = _inproj(x_bf, win_g, cos_t, sin_t, b_gate, gather["inproj"])
    wba = wba_g.transpose(1, 0, 2).reshape(D_ATTN, D_MODEL)
    wbp = wbp_g.transpose(1, 0, 2).reshape(D_POOL, D_MODEL)
    wout = wout_g.reshape(D_MODEL, D_MODEL)
    (o_bf, lse), (wg_g, wu_g, convw_g) = _attn_fwd(z, nb, gather["attn_fwd"])
    pooled_bf, mixed, pm_bf = _pool_fwd(z, w_pool, pool_scale)
    m_bf = _branch_merge(o_bf, pm_bf, z, wba, wbp)
    x1_bf, xhat1, rstd1 = _out_ln1(m_bf, wout, x, ln1_g, ln1_b)
    (a, up, ac, hh_bf), (wd_g,) = _ffn_up(x1_bf, wg_g, wu_g, convw_g, convb_g, gather["ffn_up"])
    wd = wd_g.reshape(D_FF_PAD, D_MODEL)
    loss, dh2, dh2_bf, d_ln2_g, d_ln2_b = _ffn_down_ln2_loss(hh_bf, wd, xhat1, ln1_g, ln1_b, ln2_g, ln2_b, target)

    da_bf, dup_bf, dconv = _ffn_act_bwd(dh2_bf, wd_g, ac, a, up, convw_g)
    d_wd = _matmul_tn(hh_bf, dh2_bf, 1, "dw_ffn_down").reshape(N_CHIPS, FF_PAD, D_MODEL)
    d_wg = _matmul_tn(da_bf, x1_bf, 1, "dw_ffn_gate").reshape(N_CHIPS, FF_PAD, D_MODEL)
    d_wu = _matmul_tn(dup_bf, x1_bf, 1, "dw_ffn_up").reshape(N_CHIPS, FF_PAD, D_MODEL)
    (dh1, dh1_bf, d_ln1_g, d_ln1_b), got_down = _ffn_in_bwd_ln1(
        da_bf, dup_bf, wg_g, wu_g, dh2, xhat1, rstd1, ln1_g, exchange("ffn_in_bwd_ln1", [("w_ffn_down", d_wd)]))
    d_wout = _matmul_tn(m_bf, dh1_bf, 1, "dw_out").reshape(N_CHIPS, D_MODEL // N_CHIPS, D_MODEL)
    dzg_bf, dya_bf, dyp_bf, do_bf, dpm, d_bgate = _merge_bwd(dh1_bf, wout, o_bf, pm_bf, z, wba, wbp)
    d_wba = _matmul_tn(o_bf, dya_bf, N_CHIPS, "dw_branch_attn")
    d_wbp = _matmul_tn(pm_bf, dyp_bf, N_CHIPS, "dw_branch_pool")
    du_bf, d_wpool, d_pscale = _pool_bwd(dpm, mixed, pooled_bf, w_pool, pool_scale)
    small = _pack_small([d_bgate, d_wpool, d_pscale, d_ln1_g, d_ln1_b, d_ln2_g, d_ln2_b, dconv[:, 3, :],
                         loss[0:1, :]])[None]
    (dq_bf, dk_bf, dv_bf), got_ffn = _attn_bwd(
        z, o_bf, lse, do_bf, cos_t, sin_t, nb,
        exchange("attn_bwd", [("w_ffn_gate", d_wg), ("w_ffn_up", d_wu), ("conv_w", dconv), ("small", small)]))
    dz_bf = jnp.concatenate([dq_bf, dk_bf, dv_bf, du_bf, dzg_bf], axis=1)
    d_win, got_branch = _matmul_tn(
        x_bf, dz_bf, N_CHIPS, "dw_in",
        cargo=exchange("dw_in", [("w_out", d_wout), ("w_branch_attn", d_wba), ("w_branch_pool", d_wbp)]))
    (grad_x,), got_in = _inproj_dx(dz_bf, win_g, dh1, exchange("inproj_dx", [("w_in", d_win)]))
    return grad_x, got_down + got_ffn + got_branch + got_in


SMALL_ROWS = 592


def _pack_small(parts):
    flat = jnp.concatenate([p.reshape(-1) for p in parts])
    return jnp.pad(flat, (0, SMALL_ROWS * LANES - flat.shape[0])).reshape(SMALL_ROWS, LANES)


def _unpack_small(packed, shapes):
    flat = packed.reshape(-1)
    out, off = [], 0
    for shp in shapes:
        n = 1
        for d in shp:
            n *= d
        out.append(flat[off:off + n].reshape(shp))
        off += n
    return out


def _pad_conv_b(cb):
    return jnp.pad(cb.reshape(N_CHIPS, FF_SHARD), ((0, 0), (0, FF_PAD - FF_SHARD)))


def kernel(x, w_in, b_gate, w_branch_attn, w_pool, pool_scale, w_branch_pool, w_out, ln1_g, ln1_b, w_ffn_gate, w_ffn_up, conv_w, conv_b, w_ffn_down, ln2_g, ln2_b, loss_target, m_w_in, m_b_gate, m_w_branch_attn, m_w_pool, m_pool_scale, m_w_branch_pool, m_w_out, m_ln1_g, m_ln1_b, m_w_ffn_gate, m_w_ffn_up, m_conv_w, m_conv_b, m_w_ffn_down, m_ln2_g, m_ln2_b, v_w_in, v_b_gate, v_w_branch_attn, v_w_pool, v_pool_scale, v_w_branch_pool, v_w_out, v_ln1_g, v_ln1_b, v_w_ffn_gate, v_w_ffn_up, v_conv_w, v_conv_b, v_w_ffn_down, v_ln2_g, v_ln2_b):
    pad_rows = lambda w: jnp.pad(w.astype(bf16), ((0, FF_PAD - FF_SHARD), (0, 0)))
    shards = [w_in[0].astype(bf16), w_branch_attn[0].astype(bf16), w_branch_pool[0].astype(bf16),
              w_out[0].astype(bf16), pad_rows(w_ffn_gate[0].T), pad_rows(w_ffn_up[0].T), pad_rows(w_ffn_down[0]),
              jnp.pad(conv_w[0], ((0, CONV_ROWS - 3), (0, FF_PAD - FF_SHARD)))]
    (win_g,) = _cargo_alone(_GatherCargo(shards[:1], [0.0]), "gather_w_in")
    gather = {"inproj": _GatherCargo(shards[1:4], [0.3, 0.5, 0.75]),
              "attn_fwd": _GatherCargo(shards[4:6] + shards[7:], [0.45, 0.88, 0.9]),
              "ffn_up": _GatherCargo(shards[6:7], [0.8])}

    place = jnp.stack([2 * lax.axis_index("x") + lax.axis_index("y"), lax.axis_index("c")]).astype(jnp.int32)
    chip_sums = {}

    def exchange(carrier, named):
        split = [_split_halves(g) for _, g in named]
        got = _presum_swap(split, "presum_swap_" + carrier)
        sums = _add_pairs(list(zip(split, got)), place, "presum_" + carrier)
        chip_sums.update({nm: s for (nm, _), s in zip(named, sums)})
        return _ExchangeCargo(sums, [nm == "small" for nm, _ in named])

    convb_g = _pad_conv_b(conv_b).reshape(N_CHIPS, 1, FF_PAD)
    grad_x, brought = _local_step(
        x[0], loss_target[0], win_g, gather, exchange, b_gate, w_pool[0], pool_scale, ln1_g, ln1_b, convb_g, ln2_g, ln2_b)
    slots = dict(zip(chip_sums, brought))
    names = ["w_in", "w_branch_attn", "w_branch_pool", "w_out", "w_ffn_gate", "w_ffn_up", "w_ffn_down", "conv_w", "small"]
    grads = _sibling_fill(_sum_slots([(chip_sums[nm], slots[nm], nm == "small") for nm in names], place, "sum_slots"))

    small_names = ["b_gate", "w_pool", "pool_scale", "ln1_g", "ln1_b", "ln2_g", "ln2_b", "conv_b"]
    small_w = [b_gate, w_pool, pool_scale, ln1_g, ln1_b, ln2_g, ln2_b, _pad_conv_b(conv_b)]
    small_m = [m_b_gate, m_w_pool, m_pool_scale, m_ln1_g, m_ln1_b, m_ln2_g, m_ln2_b, _pad_conv_b(m_conv_b)]
    small_v = [v_b_gate, v_w_pool, v_pool_scale, v_ln1_g, v_ln1_b, v_ln2_g, v_ln2_b, _pad_conv_b(v_conv_b)]
    weights = [w_in[0], w_branch_attn[0], w_branch_pool[0], w_out[0], w_ffn_gate[0].T, w_ffn_up[0].T, w_ffn_down[0],
               conv_w[0], _pack_small(small_w)]
    m_in = [m_w_in[0], m_w_branch_attn[0], m_w_branch_pool[0], m_w_out[0], m_w_ffn_gate[0].T, m_w_ffn_up[0].T,
            m_w_ffn_down[0], m_conv_w[0], _pack_small(small_m)]
    v_in = [v_w_in[0], v_w_branch_attn[0], v_w_branch_pool[0], v_w_out[0], v_w_ffn_gate[0].T, v_w_ffn_up[0].T,
            v_w_ffn_down[0], v_conv_w[0], _pack_small(small_v)]
    tiles = [256, 256, 256, 128, 176, 176, 176, 3, SMALL_ROWS]
    jobs = list(zip(weights, m_in, v_in, grads, tiles))
    ffn = slice(4, 7)
    updated = _adamw(jobs[:4] + jobs[7:], "adamw_rest")
    updated = updated[:4] + _adamw(jobs[ffn], "adamw_ffn") + updated[4:]
    res = {nm: [(r.T if nm in ("w_ffn_gate", "w_ffn_up") else r)[None] for r in outs4]
           for nm, outs4 in zip(names[:-1], updated)}
    packed = updated[-1]
    loss = grads[-1][SMALL_ROWS - 4, 0]
    shapes = [w.shape for w in small_w]
    for kind in range(4):
        for nm, val in zip(small_names, _unpack_small(packed[kind], shapes)):
            if nm == "conv_b":
                val = val[:, :FF_SHARD].reshape(1, 4 * FF_SHARD)
            res.setdefault(nm, [None] * 4)[kind] = val

    order = ["w_in", "b_gate", "w_branch_attn", "w_pool", "pool_scale", "w_branch_pool", "w_out", "ln1_g", "ln1_b",
             "w_ffn_gate", "w_ffn_up", "conv_w", "conv_b", "w_ffn_down", "ln2_g", "ln2_b"]
    outs = [loss, grad_x[None]]
    for kind in range(4):
        outs += [res[nm][kind] for nm in order]
    return tuple(outs)
```

```python
import functools

import jax
import jax.numpy as jnp
from jax import lax
from jax.experimental import pallas as pl
from jax.experimental.pallas import tpu as pltpu

D_MODEL = 1024
HEAD_DIM = 64
D_ATTN = 512
D_POOL = 512
MOBA_BLOCK = 256
MOBA_TOPK = 3
POOL_GROUP = 128
MAX_WINDOW = 16
FF_SHARD = 704
FF_PAD = 768
D_FF_PAD = 4 * FF_PAD
N_CHIPS = 4
LANES = 128
ALPHA = (2.0 * 1) ** 0.25
LN_EPS = 1e-5
NEG = -1e30
SCALE = HEAD_DIM ** -0.5
ADAM_LR, ADAM_B1, ADAM_B2, ADAM_EPS, ADAM_WD, ADAM_STEP = 0.001, 0.9, 0.999, 1e-08, 0.01, 10
VMEM_LIMIT = 56 * 1024 * 1024
MESH = pl.DeviceIdType.MESH

bf16 = jnp.bfloat16
f32 = jnp.float32


def _dot(a, b):
    return jnp.dot(a, b, preferred_element_type=f32)


def _dot_nt(a, b):
    return lax.dot_general(a, b, (((1,), (1,)), ((), ())), preferred_element_type=f32)


def _dot_tn(a, b):
    return lax.dot_general(a, b, (((0,), (0,)), ((), ())), preferred_element_type=f32)


def _call(body, *, name, grid, in_specs, out_specs, out_shape, scratch=(), cargo=None):
    if cargo is not None:
        return _cargo_call(body, cargo, name=name, grid=grid, in_specs=in_specs, out_specs=out_specs,
                           out_shape=out_shape, scratch=scratch)
    return pl.pallas_call(
        body, name=name, grid=grid, in_specs=in_specs, out_specs=out_specs, out_shape=out_shape,
        scratch_shapes=list(scratch),
        compiler_params=pltpu.CompilerParams(dimension_semantics=("arbitrary",) * len(grid),
                                             vmem_limit_bytes=VMEM_LIMIT))


def _rot_half(t):
    lane = lax.broadcasted_iota(jnp.int32, t.shape, 1)
    first = (lane % HEAD_DIM) < (HEAD_DIM // 2)
    return jnp.where(first, pltpu.roll(t, LANES - HEAD_DIM // 2, 1), pltpu.roll(t, HEAD_DIM // 2, 1))


def _rope(t, cos, sin_signed):
    return t * cos + _rot_half(t) * sin_signed


def _rope_bwd(d, cos, sin_signed):
    return d * cos + _rot_half(d * sin_signed)


def _gelu_parts(a):
    cdf = 0.5 * (1.0 + lax.erf(a * (2.0 ** -0.5)))
    pdf = jnp.exp(-0.5 * a * a) * ((2.0 * jnp.pi) ** -0.5)
    return a * cdf, cdf + a * pdf


def _layer_norm(h, g, b):
    mu = jnp.mean(h, axis=-1, keepdims=True)
    xc = h - mu
    var = jnp.mean(xc * xc, axis=-1, keepdims=True)
    rstd = lax.rsqrt(var + LN_EPS)
    xhat = xc * rstd
    return xhat * g + b, xhat, rstd


def _layer_norm_bwd(dy, xhat, rstd, g):
    dxh = dy * g
    m1 = jnp.mean(dxh, axis=-1, keepdims=True)
    m2 = jnp.mean(dxh * xhat, axis=-1, keepdims=True)
    return rstd * (dxh - m1 - xhat * m2)


def _inproj(x_bf, win_g, cos_t, sin_t, b_gate, cargo):
    s = x_bf.shape[0]
    tm, tn = 1024, 512

    def body(x_ref, w_ref, cos_ref, sin_ref, b_ref, o_ref):
        j = pl.program_id(1)
        acc = _dot(x_ref[...], w_ref[...])

        @pl.when(j < 2)
        def _():
            for c in range(tn // LANES):
                sl = slice(c * LANES, (c + 1) * LANES)
                o_ref[:, sl] = _rope(acc[:, sl], cos_ref[...], sin_ref[...])

        @pl.when((j >= 2) & (j < 4))
        def _():
            o_ref[...] = acc

        @pl.when(j >= 4)
        def _():
            o_ref[...] = jax.nn.sigmoid(acc + b_ref[...])

    (z,), carried = _cargo_call(
        body, cargo, name="inproj", grid=(s // tm, 4 * D_MODEL // tn),
        in_specs=[pl.BlockSpec((tm, D_MODEL), lambda i, j: (i, 0)),
                  pl.BlockSpec((None, D_MODEL, tn), lambda i, j: (j // 2, 0, j % 2)),
                  pl.BlockSpec((tm, LANES), lambda i, j: (i, 0)),
                  pl.BlockSpec((tm, LANES), lambda i, j: (i, 0)),
                  pl.BlockSpec((1, tn), lambda i, j: (0, jnp.maximum(j - 4, 0)))],
        out_specs=[pl.BlockSpec((tm, tn), lambda i, j: (i, j))],
        out_shape=[jax.ShapeDtypeStruct((s, 4 * D_MODEL), f32)])(x_bf, win_g, cos_t, sin_t, b_gate)
    return z, carried


STAT_ROWS = 8


def _pair_rows(v0, v1, fill):
    r = lax.broadcasted_iota(jnp.int32, (STAT_ROWS, v0.shape[1]), 0)
    return jnp.where(r == 0, v0, jnp.where(r == 1, v1, fill))


def _head_lanes(shape):
    lane = lax.broadcasted_iota(jnp.int32, shape, 1)
    return lane < HEAD_DIM, lane >= HEAD_DIM


def _head_rows(shape):
    row = lax.broadcasted_iota(jnp.int32, shape, 0)
    return row < HEAD_DIM, row >= HEAD_DIM


def _moba_select(q_bf, k_all, i, nb):
    k_mean = jnp.mean(k_all.reshape(nb, MOBA_BLOCK, LANES), axis=1)
    n_io = lax.broadcasted_iota(jnp.int32, (nb, MOBA_BLOCK), 0)
    past = n_io < i
    sels = []
    for head in _head_lanes((nb, LANES)):
        gate = _dot_nt(jnp.where(head, k_mean, 0.0).astype(bf16), q_bf)
        g = jnp.where(past, gate, NEG)
        rank = jnp.zeros((nb, MOBA_BLOCK), f32)
        for m in range(nb):
            gm = g[m:m + 1, :]
            rank = rank + jnp.where((gm > g) | ((gm == g) & (m < n_io)), 1.0, 0.0)
        sels.append(jnp.where(past & (rank < MOBA_TOPK), 1.0, 0.0))
    return sels


def _attn_fwd(z, nb, cargo):
    s = z.shape[0]

    def body(q_ref, k_ref, v_ref, o_ref, lse_ref, vt_s):
        i = pl.program_id(1)

        @pl.when(i == 0)
        def _():
            for j in range(nb):
                vt_s[j] = v_ref[j * MOBA_BLOCK:(j + 1) * MOBA_BLOCK, :].T.astype(bf16)

        k_heads = _head_lanes((MOBA_BLOCK, LANES))
        o_heads = _head_rows((LANES, MOBA_BLOCK))

        def block(j, qs, allow, m_old, l_old):
            kj, vtj = k_ref[j * MOBA_BLOCK:(j + 1) * MOBA_BLOCK, :], vt_s[j]
            m_new, alpha, l_new, pv = [], [], [], None
            for h in range(2):
                sc = jnp.where(allow[h], _dot_nt(jnp.where(k_heads[h], kj, 0.0).astype(bf16), qs), NEG)
                mx = jnp.max(sc, axis=0, keepdims=True)
                mn = mx if m_old is None else jnp.maximum(m_old[h], mx)
                p = jnp.exp(sc - mn)
                lsum = jnp.sum(p, axis=0, keepdims=True)
                if m_old is None:
                    a = None
                else:
                    a = jnp.exp(m_old[h] - mn)
                    lsum = a * l_old[h] + lsum
                t = _dot(jnp.where(o_heads[h], vtj, jnp.zeros_like(vtj)), p.astype(bf16))
                pv = t if pv is None else pv + t
                m_new.append(mn)
                alpha.append(a)
                l_new.append(lsum)
            return m_new, alpha, l_new, pv

        def query_block(c):
            q = q_ref[...]
            sels = _moba_select(q.astype(bf16), k_ref[...], c, nb)
            qs = (q * SCALE).astype(bf16)
            key = lax.broadcasted_iota(jnp.int32, (MOBA_BLOCK, MOBA_BLOCK), 0)
            qry = lax.broadcasted_iota(jnp.int32, (MOBA_BLOCK, MOBA_BLOCK), 1)
            m, _, l, acc = block(c, qs, [key <= qry] * 2, None, None)
            for j in range(c):
                m, alpha, l, pv = block(j, qs, [sels[h][j:j + 1, :] > 0.0 for h in range(2)], m, l)
                acc = acc * jnp.where(o_heads[0], alpha[0], alpha[1]) + pv
            o_ref[...] = (acc / jnp.where(o_heads[0], l[0], l[1])).T.astype(bf16)
            lse_ref[...] = _pair_rows(m[0] + jnp.log(l[0]), m[1] + jnp.log(l[1]), 0.0)

        for c in range(nb):
            pl.when(i == c)(functools.partial(query_block, c))

    return _cargo_call(
        body, cargo, name="attn_fwd", grid=(D_ATTN // LANES, nb),
        in_specs=[pl.BlockSpec((MOBA_BLOCK, LANES), lambda hp, i: (i, hp)),
                  pl.BlockSpec((s, LANES), lambda hp, i: (0, 4 + hp)),
                  pl.BlockSpec((s, LANES), lambda hp, i: (0, 8 + hp))],
        out_specs=[pl.BlockSpec((MOBA_BLOCK, LANES), lambda hp, i: (i, hp)),
                   pl.BlockSpec((None, STAT_ROWS, MOBA_BLOCK), lambda hp, i: (hp, 0, i))],
        out_shape=[jax.ShapeDtypeStruct((s, D_ATTN), bf16),
                   jax.ShapeDtypeStruct((D_ATTN // LANES, STAT_ROWS, s), f32)],
        scratch=[pltpu.VMEM((nb, LANES, MOBA_BLOCK), bf16)])(z, z, z)


def _window_select(snaps, g):
    return jnp.where(g == 0, snaps[0], jnp.where(g == 1, snaps[1], jnp.where(g == 2, snaps[2], snaps[3])))


def _pool_fwd(z, w_pool, pool_scale):
    s = z.shape[0]

    def body(u_ref, w_ref, sc_ref, pooled_ref, mixed_ref, pm_ref, pad):
        g = pl.program_id(0)
        u = u_ref[...]
        pad[0:MAX_WINDOW, :] = jnp.zeros((MAX_WINDOW, POOL_GROUP), f32)
        pad[MAX_WINDOW:MAX_WINDOW + s, :] = u
        acc = u
        snaps = []
        for d in range(1, MAX_WINDOW):
            acc = acc + pad[MAX_WINDOW - d:MAX_WINDOW - d + s, :]
            if d + 1 in (2, 4, 8, 16):
                snaps.append(acc)
        win = _window_select(snaps, g)
        t = lax.broadcasted_iota(jnp.int32, (s, POOL_GROUP), 0)
        count = jnp.minimum(t + 1, jnp.left_shift(2, g)).astype(f32)
        pooled = (win / count - u).astype(bf16)
        mixed = _dot(pooled, w_ref[...].astype(bf16))
        pooled_ref[...] = pooled
        mixed_ref[...] = mixed
        pm_ref[...] = (mixed * sc_ref[...]).astype(bf16)

    blk = pl.BlockSpec((s, POOL_GROUP), lambda g: (0, g))
    return _call(
        body, name="pool_fwd", grid=(4,),
        in_specs=[pl.BlockSpec((s, POOL_GROUP), lambda g: (0, 12 + g)),
                  pl.BlockSpec((None, POOL_GROUP, POOL_GROUP), lambda g: (g, 0, 0)),
                  pl.BlockSpec((1, POOL_GROUP), lambda g: (0, g))],
        out_specs=[blk, blk, blk],
        out_shape=[jax.ShapeDtypeStruct((s, D_POOL), bf16), jax.ShapeDtypeStruct((s, D_POOL), f32),
                   jax.ShapeDtypeStruct((s, D_POOL), bf16)],
        scratch=[pltpu.VMEM((s + MAX_WINDOW, POOL_GROUP), f32)])(z, w_pool, pool_scale)


def _branch_merge(o_bf, pm_bf, z, wba, wbp):
    s = o_bf.shape[0]
    tm = 512

    def body(o_ref, pm_ref, ga_ref, gp_ref, wba_ref, wbp_ref, m_ref):
        ya = _dot(o_ref[...], wba_ref[...])
        yp = _dot(pm_ref[...], wbp_ref[...])
        m_ref[...] = (ga_ref[...] * ya + gp_ref[...] * yp).astype(bf16)

    full = lambda r, c: pl.BlockSpec((r, c), lambda i: (0, 0))
    return _call(
        body, name="branch_merge", grid=(s // tm,),
        in_specs=[pl.BlockSpec((tm, D_ATTN), lambda i: (i, 0)), pl.BlockSpec((tm, D_POOL), lambda i: (i, 0)),
                  pl.BlockSpec((tm, D_MODEL), lambda i: (i, 2)), pl.BlockSpec((tm, D_MODEL), lambda i: (i, 3)),
                  full(D_ATTN, D_MODEL), full(D_POOL, D_MODEL)],
        out_specs=pl.BlockSpec((tm, D_MODEL), lambda i: (i, 0)),
        out_shape=jax.ShapeDtypeStruct((s, D_MODEL), bf16))(o_bf, pm_bf, z, z, wba, wbp)


def _out_ln1(m_bf, wout, x, ln_g, ln_b):
    s = x.shape[0]
    tm = 512

    def body(m_ref, w_ref, x_ref, g_ref, b_ref, x1_ref, xhat_ref, rstd_ref):
        h = ALPHA * x_ref[...] + _dot(m_ref[...], w_ref[...])
        y, xhat, rstd = _layer_norm(h, g_ref[...], b_ref[...])
        x1_ref[...] = y.astype(bf16)
        xhat_ref[...] = xhat
        rstd_ref[...] = jnp.broadcast_to(rstd, (tm, LANES))

    row = pl.BlockSpec((tm, D_MODEL), lambda i: (i, 0))
    vec = pl.BlockSpec((1, D_MODEL), lambda i: (0, 0))
    return _call(
        body, name="out_ln1", grid=(s // tm,),
        in_specs=[row, pl.BlockSpec((D_MODEL, D_MODEL), lambda i: (0, 0)), row, vec, vec],
        out_specs=[row, row, pl.BlockSpec((tm, LANES), lambda i: (i, 0))],
        out_shape=[jax.ShapeDtypeStruct((s, D_MODEL), bf16), jax.ShapeDtypeStruct((s, D_MODEL), f32),
                   jax.ShapeDtypeStruct((s, LANES), f32)])(m_bf, wout, x, ln_g, ln_b)


FF_TILE = 256
FF_TILES_PER_SHARD = FF_PAD // FF_TILE
CONV_PAD = 8
CONV_ROWS = 16

def _ff_weight_spec(rows):
    return pl.BlockSpec((None, rows, FF_TILE), lambda j: (j // FF_TILES_PER_SHARD, 0, j % FF_TILES_PER_SHARD))


def _ff_rows_spec():
    return pl.BlockSpec((None, FF_TILE, D_MODEL), lambda j: (j // FF_TILES_PER_SHARD, j % FF_TILES_PER_SHARD, 0))


def _row_shifts(x, shifts):
    s = x.shape[0]
    padded = jnp.concatenate([x, jnp.zeros((CONV_PAD, x.shape[1]), x.dtype)], axis=0)
    return [pltpu.roll(padded, d % (s + CONV_PAD), 0)[0:s] for d in shifts]


def _ffn_up(x1_bf, wg_g, wu_g, convw_g, convb_g, cargo):
    s = x1_bf.shape[0]

    def body(x_ref, wg_ref, wu_ref, cw_ref, cb_ref, a_ref, up_ref, ac_ref, hh_ref):
        x1 = x_ref[...]
        a = _dot_nt(x1, wg_ref[...])
        up = _dot_nt(x1, wu_ref[...])
        a_ref[...] = a
        up_ref[...] = up
        a1, a2 = _row_shifts(a, (1, 2))
        ac = a * cw_ref[2:3, :] + a1 * cw_ref[1:2, :] + a2 * cw_ref[0:1, :] + cb_ref[...]
        ac_ref[...] = ac
        hg, _ = _gelu_parts(ac)
        hh_ref[...] = (hg * up).astype(bf16)

    col = pl.BlockSpec((s, FF_TILE), lambda j: (0, j))
    wide = jax.ShapeDtypeStruct((s, D_FF_PAD), f32)
    return _call(
        body, name="ffn_up", grid=(D_FF_PAD // FF_TILE,),
        in_specs=[pl.BlockSpec((s, D_MODEL), lambda j: (0, 0)), _ff_rows_spec(), _ff_rows_spec(),
                  _ff_weight_spec(8), _ff_weight_spec(1)],
        out_specs=[col, col, col, col],
        out_shape=[wide, wide, wide, jax.ShapeDtypeStruct((s, D_FF_PAD), bf16)],
        cargo=cargo)(x1_bf, wg_g, wu_g, convw_g, convb_g)


def _ffn_down_ln2_loss(hh_bf, wd, xhat1, ln1_g, ln1_b, ln2_g, ln2_b, target):
    s = hh_bf.shape[0]
    tm = 256

    def body(hh_ref, w_ref, xh1_ref, g1_ref, b1_ref, g2_ref, b2_ref, t_ref, loss_ref, dh_ref, dhb_ref, dg_ref, db_ref):
        i = pl.program_id(0)
        x1 = xh1_ref[...] * g1_ref[...] + b1_ref[...]
        h = ALPHA * x1 + _dot(hh_ref[...], w_ref[...])
        y, xhat, rstd = _layer_norm(h, g2_ref[...], b2_ref[...])
        err = y - t_ref[...]
        part = 0.5 * jnp.sum(jnp.mean(err * err, axis=-1, keepdims=True), axis=0, keepdims=True)
        dy = err * (1.0 / D_MODEL)

        @pl.when(i == 0)
        def _():
            loss_ref[...] = jnp.zeros_like(loss_ref)
            dg_ref[...] = jnp.zeros_like(dg_ref)
            db_ref[...] = jnp.zeros_like(db_ref)

        loss_ref[...] += jnp.broadcast_to(part, loss_ref.shape)
        dg_ref[...] += jnp.sum(dy * xhat, axis=0, keepdims=True)
        db_ref[...] += jnp.sum(dy, axis=0, keepdims=True)
        dh = _layer_norm_bwd(dy, xhat, rstd, g2_ref[...])
        dh_ref[...] = dh
        dhb_ref[...] = dh.astype(bf16)

    row = pl.BlockSpec((tm, D_MODEL), lambda i: (i, 0))
    vec = pl.BlockSpec((1, D_MODEL), lambda i: (0, 0))
    return _call(
        body, name="ffn_down_ln2_loss", grid=(s // tm,),
        in_specs=[pl.BlockSpec((tm, D_FF_PAD), lambda i: (i, 0)),
                  pl.BlockSpec((D_FF_PAD, D_MODEL), lambda i: (0, 0)),
                  row, vec, vec, vec, vec, row],
        out_specs=[pl.BlockSpec((8, LANES), lambda i: (0, 0)), row, row, vec, vec],
        out_shape=[jax.ShapeDtypeStruct((8, LANES), f32), jax.ShapeDtypeStruct((s, D_MODEL), f32),
                   jax.ShapeDtypeStruct((s, D_MODEL), bf16), jax.ShapeDtypeStruct((1, D_MODEL), f32),
                   jax.ShapeDtypeStruct((1, D_MODEL), f32)])(hh_bf, wd, xhat1, ln1_g, ln1_b, ln2_g, ln2_b, target)


def _ffn_act_bwd(dh2_bf, wd_g, ac, a, up, convw_g):
    s = dh2_bf.shape[0]

    def body(dh_ref, wd_ref, ac_ref, a_ref, up_ref, cw_ref, da_ref, dup_ref, dc_ref):
        dhh = _dot_nt(dh_ref[...], wd_ref[...])
        hg, dgelu = _gelu_parts(ac_ref[...])
        dup_ref[...] = (dhh * hg).astype(bf16)
        dac = dhh * up_ref[...] * dgelu
        dac1, dac2 = _row_shifts(dac, (-1, -2))
        a = a_ref[...]
        rows = [jnp.sum(d * a, axis=0, keepdims=True) for d in (dac2, dac1, dac)]
        rows.append(jnp.sum(dac, axis=0, keepdims=True))
        rows.append(jnp.zeros((CONV_ROWS - 4, FF_TILE), f32))
        dc_ref[...] = jnp.concatenate(rows, axis=0)
        da_ref[...] = (dac * cw_ref[2:3, :] + dac1 * cw_ref[1:2, :] + dac2 * cw_ref[0:1, :]).astype(bf16)

    col = pl.BlockSpec((s, FF_TILE), lambda j: (0, j))
    return _call(
        body, name="ffn_act_bwd", grid=(D_FF_PAD // FF_TILE,),
        in_specs=[pl.BlockSpec((s, D_MODEL), lambda j: (0, 0)), _ff_rows_spec(), col, col, col, _ff_weight_spec(8)],
        out_specs=[col, col, _ff_weight_spec(CONV_ROWS)],
        out_shape=[jax.ShapeDtypeStruct((s, D_FF_PAD), bf16), jax.ShapeDtypeStruct((s, D_FF_PAD), bf16),
                   jax.ShapeDtypeStruct((N_CHIPS, CONV_ROWS, FF_PAD), f32)],
    )(dh2_bf, wd_g, ac, a, up, convw_g)


def _matmul_tn(a, b, n_shards, name, tm=512, tn=1024, cargo=None):
    k, m = a.shape
    n = b.shape[1]
    tm, tn = min(tm, m), min(tn, n // n_shards)
    per = n // n_shards // tn

    def body(a_ref, b_ref, o_ref, at_s):
        @pl.when(pl.program_id(1) == 0)
        def _():
            at_s[...] = a_ref[...].T

        o_ref[...] = _dot(at_s[...], b_ref[...]).astype(bf16)

    res = _call(
        body, name=name, grid=(m // tm, n // tn),
        in_specs=[pl.BlockSpec((k, tm), lambda i, j: (0, i)), pl.BlockSpec((k, tn), lambda i, j: (0, j))],
        out_specs=[pl.BlockSpec((None, tm, tn), lambda i, j: (j // per, i, j % per))],
        out_shape=[jax.ShapeDtypeStruct((n_shards, m, n // n_shards), bf16)], scratch=[pltpu.VMEM((tm, k), bf16)],
        cargo=cargo)(a, b)
    return res[0] if cargo is None else (res[0][0], res[1])


def _ffn_in_bwd_ln1(da_bf, dup_bf, wg_g, wu_g, dh2, xhat1, rstd1, ln1_g, cargo):
    s = da_bf.shape[0]
    tm = 256

    def body(da_ref, dup_ref, wg_ref, wu_ref, dh2_ref, xh_ref, rstd_ref, g_ref, dh_ref, dhb_ref, dg_ref, db_ref):
        i = pl.program_id(0)
        dx1 = ALPHA * dh2_ref[...]
        for sh in range(N_CHIPS):
            sl = slice(sh * FF_PAD, (sh + 1) * FF_PAD)
            dx1 = dx1 + _dot(da_ref[:, sl], wg_ref[sh]) + _dot(dup_ref[:, sl], wu_ref[sh])
        xhat = xh_ref[...]

        @pl.when(i == 0)
        def _():
            dg_ref[...] = jnp.zeros_like(dg_ref)
            db_ref[...] = jnp.zeros_like(db_ref)

        dg_ref[...] += jnp.sum(dx1 * xhat, axis=0, keepdims=True)
        db_ref[...] += jnp.sum(dx1, axis=0, keepdims=True)
        dh = _layer_norm_bwd(dx1, xhat, rstd_ref[:, 0:1], g_ref[...])
        dh_ref[...] = dh
        dhb_ref[...] = dh.astype(bf16)

    row = pl.BlockSpec((tm, D_MODEL), lambda i: (i, 0))
    wide = pl.BlockSpec((tm, D_FF_PAD), lambda i: (i, 0))
    wfull = pl.BlockSpec((N_CHIPS, FF_PAD, D_MODEL), lambda i: (0, 0, 0))
    vec = pl.BlockSpec((1, D_MODEL), lambda i: (0, 0))
    return _call(
        body, name="ffn_in_bwd_ln1", grid=(s // tm,),
        in_specs=[wide, wide, wfull, wfull, row, row, pl.BlockSpec((tm, LANES), lambda i: (i, 0)), vec],
        out_specs=[row, row, vec, vec],
        out_shape=[jax.ShapeDtypeStruct((s, D_MODEL), f32), jax.ShapeDtypeStruct((s, D_MODEL), bf16),
                   jax.ShapeDtypeStruct((1, D_MODEL), f32), jax.ShapeDtypeStruct((1, D_MODEL), f32)],
        cargo=cargo)(da_bf, dup_bf, wg_g, wu_g, dh2, xhat1, rstd1, ln1_g)


def _merge_bwd(dh1_bf, wout, o_bf, pm_bf, z, wba, wbp):
    s = dh1_bf.shape[0]
    tm = 256

    def body(dh_ref, wout_ref, o_ref, pm_ref, ga_ref, gp_ref, wba_ref, wbp_ref,
             dzg_ref, dya_ref, dyp_ref, do_ref, dpm_ref, dbg_ref):
        i = pl.program_id(0)
        dm = _dot_nt(dh_ref[...], wout_ref[...])
        ya = _dot(o_ref[...], wba_ref[...])
        yp = _dot(pm_ref[...], wbp_ref[...])
        ga, gp = ga_ref[...], gp_ref[...]
        dza = dm * ya * ga * (1.0 - ga)
        dzp = dm * yp * gp * (1.0 - gp)

        @pl.when(i == 0)
        def _():
            dbg_ref[...] = jnp.zeros_like(dbg_ref)

        dbg_ref[:, 0:D_MODEL] += jnp.sum(dza, axis=0, keepdims=True)
        dbg_ref[:, D_MODEL:2 * D_MODEL] += jnp.sum(dzp, axis=0, keepdims=True)
        dzg_ref[:, 0:D_MODEL] = dza.astype(bf16)
        dzg_ref[:, D_MODEL:2 * D_MODEL] = dzp.astype(bf16)
        dya = (dm * ga).astype(bf16)
        dyp = (dm * gp).astype(bf16)
        dya_ref[...] = dya
        dyp_ref[...] = dyp
        do_ref[...] = _dot_nt(dya, wba_ref[...]).astype(bf16)
        dpm_ref[...] = _dot_nt(dyp, wbp_ref[...])

    row = pl.BlockSpec((tm, D_MODEL), lambda i: (i, 0))
    half = pl.BlockSpec((tm, D_ATTN), lambda i: (i, 0))
    full = lambda r, c: pl.BlockSpec((r, c), lambda i: (0, 0))
    return _call(
        body, name="merge_bwd", grid=(s // tm,),
        in_specs=[row, full(D_MODEL, D_MODEL), half, half,
                  pl.BlockSpec((tm, D_MODEL), lambda i: (i, 2)), pl.BlockSpec((tm, D_MODEL), lambda i: (i, 3)),
                  full(D_ATTN, D_MODEL), full(D_POOL, D_MODEL)],
        out_specs=[pl.BlockSpec((tm, 2 * D_MODEL), lambda i: (i, 0)), row, row, half, half,
                   pl.BlockSpec((1, 2 * D_MODEL), lambda i: (0, 0))],
        out_shape=[jax.ShapeDtypeStruct((s, 2 * D_MODEL), bf16), jax.ShapeDtypeStruct((s, D_MODEL), bf16),
                   jax.ShapeDtypeStruct((s, D_MODEL), bf16), jax.ShapeDtypeStruct((s, D_ATTN), bf16),
                   jax.ShapeDtypeStruct((s, D_POOL), f32), jax.ShapeDtypeStruct((1, 2 * D_MODEL), f32)],
    )(dh1_bf, wout, o_bf, pm_bf, z, z, wba, wbp)


def _pool_bwd(dpm, mixed, pooled_bf, w_pool, pool_scale):
    s = dpm.shape[0]

    def body(dpm_ref, mixed_ref, pooled_ref, w_ref, sc_ref, du_ref, dw_ref, dsc_ref, pad):
        g = pl.program_id(0)
        dpm_v = dpm_ref[...]
        dsc_ref[...] = jnp.sum(dpm_v * mixed_ref[...], axis=0, keepdims=True)
        dmixed = (dpm_v * sc_ref[...]).astype(bf16)
        dw_ref[...] = _dot_tn(pooled_ref[...], dmixed)
        dpooled = _dot_nt(dmixed, w_ref[...].astype(bf16))
        t = lax.broadcasted_iota(jnp.int32, (s, POOL_GROUP), 0)
        count = jnp.minimum(t + 1, jnp.left_shift(2, g)).astype(f32)
        r = dpooled / count
        pad[0:s, :] = r
        pad[s:s + MAX_WINDOW, :] = jnp.zeros((MAX_WINDOW, POOL_GROUP), f32)
        acc = r
        snaps = []
        for d in range(1, MAX_WINDOW):
            acc = acc + pad[d:d + s, :]
            if d + 1 in (2, 4, 8, 16):
                snaps.append(acc)
        du_ref[...] = (_window_select(snaps, g) - dpooled).astype(bf16)

    blk = pl.BlockSpec((s, POOL_GROUP), lambda g: (0, g))
    return _call(
        body, name="pool_bwd", grid=(4,),
        in_specs=[blk, blk, blk, pl.BlockSpec((None, POOL_GROUP, POOL_GROUP), lambda g: (g, 0, 0)),
                  pl.BlockSpec((1, POOL_GROUP), lambda g: (0, g))],
        out_specs=[blk, pl.BlockSpec((None, POOL_GROUP, POOL_GROUP), lambda g: (g, 0, 0)),
                   pl.BlockSpec((1, POOL_GROUP), lambda g: (0, g))],
        out_shape=[jax.ShapeDtypeStruct((s, D_POOL), bf16), jax.ShapeDtypeStruct((4, POOL_GROUP, POOL_GROUP), f32),
                   jax.ShapeDtypeStruct((1, D_POOL), f32)],
        scratch=[pltpu.VMEM((s + MAX_WINDOW, POOL_GROUP), f32)])(dpm, mixed, pooled_bf, w_pool, pool_scale)


def _attn_bwd(z, o_bf, lse, do_bf, cos_t, sin_t, nb, cargo):
    s = z.shape[0]

    def body(q_ref, k_ref, v_ref, o_ref, lse_ref, do_ref, cq_ref, sq_ref, cf_ref, sf_ref,
             dq_ref, dk_ref, dv_ref, kt_s, dk_s, dv_s):
        i = pl.program_id(1)

        @pl.when(i == 0)
        def _():
            dk_s[...] = jnp.zeros_like(dk_s)
            dv_s[...] = jnp.zeros_like(dv_s)
            for j in range(nb):
                kt_s[j] = k_ref[j * MOBA_BLOCK:(j + 1) * MOBA_BLOCK, :].T.astype(bf16)

        k_heads = _head_lanes((MOBA_BLOCK, LANES))
        t_heads = _head_rows((LANES, MOBA_BLOCK))

        def query_block(c):
            q = q_ref[...]
            sels = _moba_select(q.astype(bf16), k_ref[...], c, nb)
            qs = (q * SCALE).astype(bf16)
            qraw = [jnp.where(hm, q, 0.0).astype(bf16) for hm in k_heads]
            do = do_ref[...].astype(f32)
            do_bf = do_ref[...]
            dob = [jnp.where(hm, do, 0.0).astype(bf16) for hm in k_heads]
            pick = _head_lanes((STAT_ROWS, LANES))
            r8 = lax.broadcasted_iota(jnp.int32, (STAT_ROWS, LANES), 0)
            head_pick = jnp.where(((r8 == 0) & pick[0]) | ((r8 == 1) & pick[1]), 1.0, 0.0)
            delta8 = lax.dot_general(head_pick, do * o_ref[...].astype(f32), (((1,), (1,)), ((), ())),
                                     precision=lax.Precision.HIGHEST, preferred_element_type=f32)
            lse8 = lse_ref[...]
            delta, lse_h = [delta8[0:1, :], delta8[1:2, :]], [lse8[0:1, :], lse8[1:2, :]]
            key = lax.broadcasted_iota(jnp.int32, (MOBA_BLOCK, MOBA_BLOCK), 0)
            qry = lax.broadcasted_iota(jnp.int32, (MOBA_BLOCK, MOBA_BLOCK), 1)

            def block(j, allow):
                rows = slice(j * MOBA_BLOCK, (j + 1) * MOBA_BLOCK)
                kj, vj, ktj = k_ref[rows, :], v_ref[rows, :], kt_s[j]
                dqt, dk, dv = None, None, None
                for h in range(2):
                    sc = _dot_nt(jnp.where(k_heads[h], kj, 0.0).astype(bf16), qs)
                    p = jnp.where(allow[h], jnp.exp(sc - lse_h[h]), 0.0)
                    dp = _dot_nt(jnp.where(k_heads[h], vj, 0.0).astype(bf16), do_bf)
                    ds = (p * (dp - delta[h]) * SCALE).astype(bf16)
                    tv = _dot(p.astype(bf16), dob[h])
                    tk = _dot(ds, qraw[h])
                    tq = _dot(jnp.where(t_heads[h], ktj, jnp.zeros_like(ktj)), ds)
                    dqt, dk, dv = (tq, tk, tv) if dqt is None else (dqt + tq, dk + tk, dv + tv)
                dk_s[rows, :] += dk
                dv_s[rows, :] += dv
                return dqt

            dqt = block(c, [key <= qry] * 2)
            for j in range(c):
                dqt = dqt + block(j, [sels[h][j:j + 1, :] > 0.0 for h in range(2)])
            dq_ref[...] = _rope_bwd(dqt.T, cq_ref[...], sq_ref[...]).astype(bf16)

        for c in range(nb):
            pl.when(i == c)(functools.partial(query_block, c))

        @pl.when(i == nb - 1)
        def _():
            dk_ref[...] = _rope_bwd(dk_s[...], cf_ref[...], sf_ref[...]).astype(bf16)
            dv_ref[...] = dv_s[...].astype(bf16)

    qblk = pl.BlockSpec((MOBA_BLOCK, LANES), lambda hp, i: (i, hp))
    tq = pl.BlockSpec((MOBA_BLOCK, LANES), lambda hp, i: (i, 0))
    tf = pl.BlockSpec((s, LANES), lambda hp, i: (0, 0))
    colblk = pl.BlockSpec((s, LANES), lambda hp, i: (0, hp))
    return _cargo_call(
        body, cargo, name="attn_bwd", grid=(D_ATTN // LANES, nb),
        in_specs=[qblk, pl.BlockSpec((s, LANES), lambda hp, i: (0, 4 + hp)),
                  pl.BlockSpec((s, LANES), lambda hp, i: (0, 8 + hp)), qblk,
                  pl.BlockSpec((None, STAT_ROWS, MOBA_BLOCK), lambda hp, i: (hp, 0, i)), qblk, tq, tq, tf, tf],
        out_specs=[qblk, colblk, colblk],
        out_shape=[jax.ShapeDtypeStruct((s, D_ATTN), bf16)] * 3,
        scratch=[pltpu.VMEM((nb, LANES, MOBA_BLOCK), bf16), pltpu.VMEM((s, LANES), f32), pltpu.VMEM((s, LANES), f32)],
    )(z, z, z, o_bf, lse, do_bf, cos_t, sin_t, cos_t, sin_t)


def _inproj_dx(dz_bf, win_g, dh1, cargo):
    s = dz_bf.shape[0]
    tm = 256

    def body(dz_ref, w_ref, dh_ref, gx_ref):
        acc = ALPHA * dh_ref[...]
        for sh in range(N_CHIPS):
            acc = acc + _dot_nt(dz_ref[:, sh * D_MODEL:(sh + 1) * D_MODEL], w_ref[sh])
        gx_ref[...] = acc

    row = pl.BlockSpec((tm, D_MODEL), lambda i: (i, 0))
    return _call(
        body, name="inproj_dx", grid=(s // tm,),
        in_specs=[pl.BlockSpec((tm, 4 * D_MODEL), lambda i: (i, 0)),
                  pl.BlockSpec((N_CHIPS, D_MODEL, D_MODEL), lambda i: (0, 0, 0)), row],
        out_specs=[row], out_shape=[jax.ShapeDtypeStruct((s, D_MODEL), f32)], cargo=cargo)(dz_bf, win_g, dh1)


ANY = pl.BlockSpec(memory_space=pl.ANY)


def _chip_index():
    return 2 * lax.axis_index("x") + lax.axis_index("y")


def _peer(k):
    x, y, c = lax.axis_index("x"), lax.axis_index("y"), lax.axis_index("c")
    return (x ^ (k >> 1), y ^ (k & 1), c)


def _sibling():
    return (lax.axis_index("x"), lax.axis_index("y"), 1 - lax.axis_index("c"))


class _GatherCargo:
    def __init__(self, shards, pass_on):
        self.shards, self.pass_on, n = shards, pass_on, len(shards)
        self.inputs = [a.reshape(2, a.shape[0] // 2, a.shape[1]) for a in shards]
        self.out_shape = [jax.ShapeDtypeStruct((N_CHIPS,) + a.shape, a.dtype) for a in self.inputs]
        sem, one = pltpu.SemaphoreType.DMA((N_CHIPS - 1, n)), pltpu.SemaphoreType.DMA((n,))
        self.sems = [sem, sem, sem, sem, one, one]

    def _copies(self, src, dst, sems, a):
        send_sems, recv_sems, fsend_sems, frecv_sems, local_sems, own_sems = sems
        p, c = _chip_index(), lax.axis_index("c")
        own = pltpu.make_async_remote_copy(src_ref=src[a], dst_ref=dst[a].at[p], send_sem=local_sems.at[a],
                                           recv_sem=own_sems.at[a], device_id=_sibling(), device_id_type=MESH)
        out, arrive, onward, landed = [], [], [], []
        for k in range(1, N_CHIPS):
            ici = dict(send_sem=send_sems.at[k - 1, a], recv_sem=recv_sems.at[k - 1, a], device_id=_peer(k),
                       device_id_type=MESH)
            d2d = dict(send_sem=fsend_sems.at[k - 1, a], recv_sem=frecv_sems.at[k - 1, a], device_id=_sibling(),
                       device_id_type=MESH)
            got, theirs = dst[a].at[p ^ k, c], dst[a].at[p ^ k, 1 - c]
            out.append(pltpu.make_async_remote_copy(src_ref=src[a].at[c], dst_ref=dst[a].at[p, c], **ici))
            arrive.append(pltpu.make_async_remote_copy(src_ref=src[a].at[c], dst_ref=got, **ici))
            onward.append(pltpu.make_async_remote_copy(src_ref=got, dst_ref=got, **d2d))
            landed.append(pltpu.make_async_remote_copy(src_ref=theirs, dst_ref=theirs, **d2d))
        return own, out, arrive, onward, landed

    def stages(self, steps):
        n = len(self.shards)

        def start(src, dst, sems):
            for a in range(n):
                own, out, _, _, _ = self._copies(src, dst, sems, a)
                own.start()
                for cp in out:
                    cp.start()

        def pass_on(a):
            def act(src, dst, sems):
                _, _, arrive, onward, _ = self._copies(src, dst, sems, a)
                for k in range(N_CHIPS - 1):
                    arrive[k].wait_recv()
                    onward[k].start()
            return act

        def finish(src, dst, sems):
            for a in range(n):
                own, out, _, onward, landed = self._copies(src, dst, sems, a)
                for cp in landed:
                    cp.wait_recv()
                for cp in out + onward:
                    cp.wait_send()
                own.wait()

        mids = [(min(steps - 1, int(self.pass_on[a] * steps)), pass_on(a)) for a in range(n)]
        return [(0, start)] + mids + [(steps - 1, finish)]

    def results(self, outs):
        return [o.reshape((N_CHIPS,) + a.shape) for o, a in zip(outs, self.shards)]


class _ExchangeCargo:
    def __init__(self, sums, whole):
        self.inputs, self.whole, n = sums, whole, len(sums)
        self.out_shape = [jax.ShapeDtypeStruct((N_CHIPS - 1,) + g.shape[1:], g.dtype) for g in sums]
        sem = pltpu.SemaphoreType.DMA((N_CHIPS - 1, n))
        self.sems = [sem, sem]

    def _copies(self, src, dst, sems):
        send_sems, recv_sems = sems
        p = _chip_index()
        return [pltpu.make_async_remote_copy(
            src_ref=src[a].at[0] if self.whole[a] else src[a].at[p ^ k], dst_ref=dst[a].at[k - 1],
            send_sem=send_sems.at[k - 1, a], recv_sem=recv_sems.at[k - 1, a], device_id=_peer(k), device_id_type=MESH)
            for k in range(1, N_CHIPS) for a in range(len(self.inputs))]

    def stages(self, steps):
        def start(src, dst, sems):
            for cp in self._copies(src, dst, sems):
                cp.start()

        def finish(src, dst, sems):
            copies = self._copies(src, dst, sems)
            for cp in copies:
                cp.wait_recv()
            for cp in copies:
                cp.wait_send()

        return [(0, start), (steps - 1, finish)]

    def results(self, outs):
        return list(outs)


class _SwapCargo:
    def __init__(self, split):
        self.inputs, n = split, len(split)
        self.out_shape = [jax.ShapeDtypeStruct((g.shape[0],) + g.shape[2:], g.dtype) for g in split]
        sem = pltpu.SemaphoreType.DMA((n,))
        self.sems = [sem, sem]

    def _copies(self, src, dst, sems):
        c = lax.axis_index("c")
        return [pltpu.make_async_remote_copy(src_ref=src[a].at[:, 1 - c], dst_ref=dst[a], send_sem=sems[0].at[a],
                                             recv_sem=sems[1].at[a], device_id=_sibling(), device_id_type=MESH)
                for a in range(len(self.inputs))]

    def stages(self, steps):
        def start(src, dst, sems):
            for cp in self._copies(src, dst, sems):
                cp.start()

        def finish(src, dst, sems):
            copies = self._copies(src, dst, sems)
            for cp in copies:
                cp.wait_recv()
            for cp in copies:
                cp.wait_send()

        return [(0, start), (steps - 1, finish)]

    def results(self, outs):
        return list(outs)


class _MultiCargo:
    def __init__(self, cargos):
        self.cargos = cargos
        self.inputs = [a for cg in cargos for a in cg.inputs]
        self.out_shape = [o for cg in cargos for o in cg.out_shape]
        self.sems = [s for cg in cargos for s in cg.sems]

    def _parts(self, refs, count):
        out, off = [], 0
        for cg in self.cargos:
            out.append(refs[off:off + count(cg)])
            off += count(cg)
        return out

    def stages(self, steps):
        merged = []
        for n, cg in enumerate(self.cargos):
            for at, act in cg.stages(steps):
                def part(src, dst, sems, n=n, act=act):
                    act(self._parts(src, lambda g: len(g.inputs))[n], self._parts(dst, lambda g: len(g.out_shape))[n],
                        self._parts(sems, lambda g: len(g.sems))[n])
                merged.append((at, part))
        starts = [m for m in merged if m[0] == 0]
        return starts + sorted([m for m in merged if m[0] != 0], key=lambda m: m[0])

    def results(self, outs):
        return [cg.results(part) for cg, part in zip(self.cargos, self._parts(list(outs), lambda g: len(g.out_shape)))]


def _cargo_call(body, cargo, *, name, grid, in_specs, out_specs, out_shape, scratch=()):
    n_in, n_out, n_scr = len(in_specs), len(out_specs), len(scratch)
    c_in, c_out = len(cargo.inputs), len(cargo.out_shape)
    steps = 1
    for g in grid:
        steps *= g
    stages = cargo.stages(steps)

    def wrapped(*refs):
        ins, refs = refs[:n_in], refs[n_in:]
        cin, refs = refs[:c_in], refs[c_in:]
        outs, refs = refs[:n_out], refs[n_out:]
        cout, refs = refs[:c_out], refs[c_out:]
        scr, sems = refs[:n_scr], refs[n_scr:]
        if not grid:
            for _, act in stages:
                act(cin, cout, sems)
            return
        step = 0
        for d in range(len(grid)):
            step = step * grid[d] + pl.program_id(d)
        for at, act in stages:
            if at == 0:
                pl.when(step == 0)(functools.partial(act, cin, cout, sems))
        body(*ins, *outs, *scr)
        for at, act in stages:
            if at > 0:
                pl.when(step == at)(functools.partial(act, cin, cout, sems))

    params = dict(vmem_limit_bytes=VMEM_LIMIT)
    if grid:
        params["dimension_semantics"] = ("arbitrary",) * len(grid)
    res = pl.pallas_call(
        wrapped, name=name, grid=grid, in_specs=list(in_specs) + [ANY] * c_in, out_specs=list(out_specs) + [ANY] * c_out,
        out_shape=list(out_shape) + cargo.out_shape, scratch_shapes=list(scratch) + cargo.sems,
        compiler_params=pltpu.CompilerParams(**params))
    return lambda *args: (lambda r: (r[:n_out], cargo.results(r[n_out:])))(res(*args, *cargo.inputs))


def _split_halves(g):
    return g.reshape(g.shape[0], 2, g.shape[1] // 2, g.shape[2])


def _ew_tile(rows):
    return rows if rows <= 384 else 256


def _batched_call(items, name, place=None):
    offs, total = [], 0
    for it in items:
        offs.append(total)
        total += it[0]
    in_specs, out_specs, out_shape, args, spans = [], [], [], [], []
    for (n, ins, outs, shapes, operands, _), off in zip(items, offs):
        tile = lambda t, off=off, n=n: jnp.clip(t - off, 0, n - 1)
        i_s, o_s = ins(tile), outs(tile)
        spans.append((len(i_s), len(o_s)))
        in_specs += i_s
        out_specs += o_s
        out_shape += shapes
        args += operands
    n_in = len(in_specs)

    def body(*refs):
        if place is not None:
            refs = refs[1:]
        t = pl.program_id(0)
        i0, o0 = 0, n_in
        for (n, _, _, _, _, compute), off, (ni, no) in zip(items, offs, spans):
            pl.when((t >= off) & (t < off + n))(functools.partial(compute, refs[i0:i0 + ni], refs[o0:o0 + no]))
            i0, o0 = i0 + ni, o0 + no

    params = pltpu.CompilerParams(dimension_semantics=("arbitrary",), vmem_limit_bytes=VMEM_LIMIT)
    if place is None:
        res = pl.pallas_call(body, name=name, grid=(total,), in_specs=in_specs, out_specs=out_specs,
                             out_shape=out_shape, compiler_params=params)(*args)
    else:
        res = pl.pallas_call(
            body, name=name, out_shape=out_shape, compiler_params=params,
            grid_spec=pltpu.PrefetchScalarGridSpec(num_scalar_prefetch=1, grid=(total,), in_specs=in_specs,
                                                   out_specs=out_specs))(place, *args)
    out, o0 = [], 0
    for _, no in spans:
        out.append(res[o0:o0 + no])
        o0 += no
    return out


def _add_pairs(pairs, place, name):
    def item(mine, theirs):
        lead, _, rows, cols = mine.shape
        tm = _ew_tile(rows)

        def compute(ins, outs):
            outs[0][...] = (ins[0][...].astype(f32) + ins[1][...].astype(f32)).astype(outs[0].dtype)

        blk = lambda tile: pl.BlockSpec((lead, tm, cols), lambda t, pc: (0, tile(t), 0))
        mine_half = lambda tile: pl.BlockSpec((lead, None, tm, cols), lambda t, pc: (0, pc[1], tile(t), 0))
        return (rows // tm, lambda tile: [mine_half(tile), blk(tile)], lambda tile: [blk(tile)],
                [jax.ShapeDtypeStruct(theirs.shape, theirs.dtype)], [mine, theirs], compute)

    return [out[0] for out in _batched_call([item(a, b) for a, b in pairs], name, place)]


class _NoCargo:
    def __init__(self, results=()):
        self.inputs, self.out_shape, self.sems, self._results = [], [], [], list(results)

    def stages(self, steps):
        return [(0, lambda src, dst, sems: None), (steps - 1, lambda src, dst, sems: None)]

    def results(self, outs):
        return self._results


def _cargo_alone(cargo, name):
    return _cargo_call(None, cargo, name=name, grid=(), in_specs=[], out_specs=[], out_shape=[])()[1]


def _sum_slots(jobs, place, name):
    def item(own, recv, whole):
        _, rows, cols = own.shape
        tm = _ew_tile(rows)

        def compute(ins, outs):
            r0, r1, r2 = [ins[1][k].astype(f32) for k in range(N_CHIPS - 1)]
            outs[0][...] = (ins[0][...].astype(f32) + r0) + (r1 + r2)

        ins = lambda tile: [pl.BlockSpec((None, tm, cols), lambda t, pc: (0 if whole else pc[0], tile(t), 0)),
                            pl.BlockSpec((N_CHIPS - 1, tm, cols), lambda t, pc: (0, tile(t), 0))]
        outs = lambda tile: [pl.BlockSpec((None, tm, cols), lambda t, pc: (pc[1], tile(t), 0))]
        return rows // tm, ins, outs, [jax.ShapeDtypeStruct((2, rows, cols), f32)], [own, recv], compute

    return [out[0] for out in _batched_call([item(*job) for job in jobs], name, place)]


def _sibling_fill(bufs):
    n = len(bufs)

    def body(*refs):
        src, dst = refs[:n], refs[n:2 * n]
        send_sems, recv_sems = refs[2 * n:]
        c = lax.axis_index("c")
        remote = [pltpu.make_async_remote_copy(src_ref=src[a].at[c], dst_ref=dst[a].at[c], send_sem=send_sems.at[a],
                                               recv_sem=recv_sems.at[a], device_id=_sibling(), device_id_type=MESH)
                  for a in range(n)]
        for cp in remote:
            cp.start()
        for a in range(n):
            pltpu.make_async_remote_copy(src_ref=src[a].at[c], dst_ref=dst[a].at[1 - c], send_sem=send_sems.at[a],
                                         recv_sem=recv_sems.at[a], device_id=_sibling(), device_id_type=MESH).wait_recv()
        for cp in remote:
            cp.wait_send()

    sem = pltpu.SemaphoreType.DMA((n,))
    out = pl.pallas_call(
        body, name="sibling_fill", in_specs=[ANY] * n, out_specs=[ANY] * n,
        out_shape=[jax.ShapeDtypeStruct(b.shape, b.dtype) for b in bufs],
        input_output_aliases={a: a for a in range(n)}, scratch_shapes=[sem, sem])(*bufs)
    return [o.reshape(2 * b.shape[1], b.shape[2]) for o, b in zip(out, bufs)]


class _GradReducer:
    def __init__(self, place):
        self.place, self.split, self.sums, self.slots = place, {}, {}, {}

    def swap(self, named):
        self.split.update({nm: _split_halves(g) for nm, g in named})
        return _SwapCargo([self.split[nm] for nm, _ in named])

    def summed(self, names, got, tag):
        sums = _add_pairs([(self.split[nm], g) for nm, g in zip(names, got)], self.place, "presum_" + tag)
        self.sums.update(zip(names, sums))

    def exchange(self, names):
        return _ExchangeCargo([self.sums[nm] for nm in names], [nm == "small" for nm in names])

    def arrived(self, names, slots):
        self.slots.update(zip(names, slots))

    def reduce(self, names):
        jobs = [(self.sums[nm], self.slots[nm], nm == "small") for nm in names]
        return _sibling_fill(_sum_slots(jobs, self.place, "sum_slots"))


def _adamw(jobs, name):
    def item(w, m, v, grad, tm):
        rows, cols = w.shape
        gcols = grad.shape[1]

        def compute(ins, outs):
            w_, m_, v_ = ins[0][...], ins[1][...], ins[2][...]
            g = ins[3][0:tm, 0:cols]
            m_ = ADAM_B1 * m_ + (1.0 - ADAM_B1) * g
            v_ = ADAM_B2 * v_ + (1.0 - ADAM_B2) * jnp.square(g)
            m_hat = m_ / (1.0 - ADAM_B1 ** ADAM_STEP)
            v_hat = v_ / (1.0 - ADAM_B2 ** ADAM_STEP)
            outs[0][...] = g
            outs[1][...] = -ADAM_LR * (m_hat / (jnp.sqrt(v_hat) + ADAM_EPS) + ADAM_WD * w_)
            outs[2][...] = m_
            outs[3][...] = v_

        blk = lambda tile: pl.BlockSpec((tm, cols), lambda t: (tile(t), 0))
        if tm == rows:
            gspec = lambda tile: pl.BlockSpec(grad.shape, lambda t: (0, 0))
        else:
            gspec = lambda tile: pl.BlockSpec((tm, gcols), lambda t: (tile(t), 0))
        return (rows // tm, lambda tile: [blk(tile)] * 3 + [gspec(tile)], lambda tile: [blk(tile)] * 4,
                [jax.ShapeDtypeStruct((rows, cols), f32)] * 4, [w, m, v, grad], compute)

    return _batched_call([item(*job) for job in jobs], name)


def _rope_tables(s):
    half = HEAD_DIM // 2
    inv_freq = 1.0 / (10000.0 ** (jnp.arange(half, dtype=f32) / half))
    ang = jnp.arange(s, dtype=f32)[:, None] * inv_freq[None, :]
    cos, sin = jnp.cos(ang), jnp.sin(ang)
    return jnp.tile(cos, (1, LANES // half)), jnp.tile(jnp.concatenate([-sin, sin], axis=1), (1, LANES // HEAD_DIM))


def _local_step(x, target, win_g, gather, comm, b_gate, w_pool, pool_scale, ln1_g, ln1_b, convb_g, ln2_g, ln2_b):
    s = x.shape[0]
    nb = s // MOBA_BLOCK
    cos_t, sin_t = _rope_tables(s)
    x_bf = x.astype(bf16)

    z, (wba_g, wbp_g, wout_g) = _inproj(x_bf, win_g, cos_t, sin_t, b_gate, gather["inproj"])
    wba = wba_g.transpose(1, 0, 2).reshape(D_ATTN, D_MODEL)
    wbp = wbp_g.transpose(1, 0, 2).reshape(D_POOL, D_MODEL)
    wout = wout_g.reshape(D_MODEL, D_MODEL)
    (o_bf, lse), (wg_g, wu_g, convw_g) = _attn_fwd(z, nb, gather["attn_fwd"])
    pooled_bf, mixed, pm_bf = _pool_fwd(z, w_pool, pool_scale)
    m_bf = _branch_merge(o_bf, pm_bf, z, wba, wbp)
    x1_bf, xhat1, rstd1 = _out_ln1(m_bf, wout, x, ln1_g, ln1_b)
    (a, up, ac, hh_bf), (wd_g,) = _ffn_up(x1_bf, wg_g, wu_g, convw_g, convb_g, gather["ffn_up"])
    wd = wd_g.reshape(D_FF_PAD, D_MODEL)
    loss, dh2, dh2_bf, d_ln2_g, d_ln2_b = _ffn_down_ln2_loss(hh_bf, wd, xhat1, ln1_g, ln1_b, ln2_g, ln2_b, target)

    da_bf, dup_bf, dconv = _ffn_act_bwd(dh2_bf, wd_g, ac, a, up, convw_g)
    ffn_shape = (N_CHIPS, FF_PAD, D_MODEL)
    d_wd = _matmul_tn(hh_bf, dh2_bf, 1, "dw_ffn_down").reshape(ffn_shape)
    d_wg, got = _matmul_tn(da_bf, x1_bf, 1, "dw_ffn_gate", cargo=comm.swap([("w_ffn_down", d_wd)]))
    comm.summed(["w_ffn_down"], got, "ffn_down")
    d_wg = d_wg.reshape(ffn_shape)
    d_wu = _matmul_tn(dup_bf, x1_bf, 1, "dw_ffn_up").reshape(ffn_shape)
    gate_up = ["w_ffn_gate", "w_ffn_up", "conv_w"]
    (dh1, dh1_bf, d_ln1_g, d_ln1_b), (slots, got) = _ffn_in_bwd_ln1(
        da_bf, dup_bf, wg_g, wu_g, dh2, xhat1, rstd1, ln1_g,
        _MultiCargo([comm.exchange(["w_ffn_down"]), comm.swap(list(zip(gate_up, [d_wg, d_wu, dconv])))]))
    comm.arrived(["w_ffn_down"], slots)
    comm.summed(gate_up, got, "ffn_gate_up")
    d_wout = _matmul_tn(m_bf, dh1_bf, 1, "dw_out").reshape(N_CHIPS, D_MODEL // N_CHIPS, D_MODEL)
    dzg_bf, dya_bf, dyp_bf, do_bf, dpm, d_bgate = _merge_bwd(dh1_bf, wout, o_bf, pm_bf, z, wba, wbp)
    d_wba = _matmul_tn(o_bf, dya_bf, N_CHIPS, "dw_branch_attn")
    d_wbp = _matmul_tn(pm_bf, dyp_bf, N_CHIPS, "dw_branch_pool")
    du_bf, d_wpool, d_pscale = _pool_bwd(dpm, mixed, pooled_bf, w_pool, pool_scale)
    small = _pack_small([d_bgate, d_wpool, d_pscale, d_ln1_g, d_ln1_b, d_ln2_g, d_ln2_b, dconv[:, 3, :],
                         loss[0:1, :]])[None]
    branch = ["w_out", "w_branch_attn", "w_branch_pool", "small"]
    (dq_bf, dk_bf, dv_bf), (slots, got) = _attn_bwd(
        z, o_bf, lse, do_bf, cos_t, sin_t, nb,
        _MultiCargo([comm.exchange(gate_up), comm.swap(list(zip(branch, [d_wout, d_wba, d_wbp, small])))]))
    comm.arrived(gate_up, slots)
    comm.summed(branch, got, "branch")
    dz_bf = jnp.concatenate([dq_bf, dk_bf, dv_bf, du_bf, dzg_bf], axis=1)
    d_win, slots = _matmul_tn(x_bf, dz_bf, N_CHIPS, "dw_in", cargo=comm.exchange(branch))
    comm.arrived(branch, slots)
    comm.summed(["w_in"], _cargo_alone(comm.swap([("w_in", d_win)]), "swap_w_in"), "w_in")
    (grad_x,), slots = _inproj_dx(dz_bf, win_g, dh1, comm.exchange(["w_in"]))
    comm.arrived(["w_in"], slots)
    return grad_x


SMALL_ROWS = 592


def _pack_small(parts):
    flat = jnp.concatenate([p.reshape(-1) for p in parts])
    return jnp.pad(flat, (0, SMALL_ROWS * LANES - flat.shape[0])).reshape(SMALL_ROWS, LANES)


def _unpack_small(packed, shapes):
    flat = packed.reshape(-1)
    out, off = [], 0
    for shp in shapes:
        n = 1
        for d in shp:
            n *= d
        out.append(flat[off:off + n].reshape(shp))
        off += n
    return out


def _pad_conv_b(cb):
    return jnp.pad(cb.reshape(N_CHIPS, FF_SHARD), ((0, 0), (0, FF_PAD - FF_SHARD)))


def kernel(x, w_in, b_gate, w_branch_attn, w_pool, pool_scale, w_branch_pool, w_out, ln1_g, ln1_b, w_ffn_gate, w_ffn_up, conv_w, conv_b, w_ffn_down, ln2_g, ln2_b, loss_target, m_w_in, m_b_gate, m_w_branch_attn, m_w_pool, m_pool_scale, m_w_branch_pool, m_w_out, m_ln1_g, m_ln1_b, m_w_ffn_gate, m_w_ffn_up, m_conv_w, m_conv_b, m_w_ffn_down, m_ln2_g, m_ln2_b, v_w_in, v_b_gate, v_w_branch_attn, v_w_pool, v_pool_scale, v_w_branch_pool, v_w_out, v_ln1_g, v_ln1_b, v_w_ffn_gate, v_w_ffn_up, v_conv_w, v_conv_b, v_w_ffn_down, v_ln2_g, v_ln2_b):
    pad_rows = lambda w: jnp.pad(w.astype(bf16), ((0, FF_PAD - FF_SHARD), (0, 0)))
    shards = [w_in[0].astype(bf16), w_branch_attn[0].astype(bf16), w_branch_pool[0].astype(bf16),
              w_out[0].astype(bf16), pad_rows(w_ffn_gate[0].T), pad_rows(w_ffn_up[0].T), pad_rows(w_ffn_down[0]),
              jnp.pad(conv_w[0], ((0, CONV_ROWS - 3), (0, FF_PAD - FF_SHARD)))]
    (win_g,) = _cargo_alone(_GatherCargo(shards[:1], [0.0]), "gather_w_in")
    gather = {"inproj": _GatherCargo(shards[1:4], [0.3, 0.5, 0.75]),
              "attn_fwd": _GatherCargo(shards[4:6] + shards[7:], [0.45, 0.88, 0.9]),
              "ffn_up": _GatherCargo(shards[6:7], [0.8])}

    place = jnp.stack([2 * lax.axis_index("x") + lax.axis_index("y"), lax.axis_index("c")]).astype(jnp.int32)
    comm = _GradReducer(place)
    convb_g = _pad_conv_b(conv_b).reshape(N_CHIPS, 1, FF_PAD)
    grad_x = _local_step(
        x[0], loss_target[0], win_g, gather, comm, b_gate, w_pool[0], pool_scale, ln1_g, ln1_b, convb_g, ln2_g, ln2_b)
    names = ["w_in", "w_branch_attn", "w_branch_pool", "w_out", "w_ffn_gate", "w_ffn_up", "w_ffn_down", "conv_w", "small"]
    grads = comm.reduce(names)

    small_names = ["b_gate", "w_pool", "pool_scale", "ln1_g", "ln1_b", "ln2_g", "ln2_b", "conv_b"]
    small_w = [b_gate, w_pool, pool_scale, ln1_g, ln1_b, ln2_g, ln2_b, _pad_conv_b(conv_b)]
    small_m = [m_b_gate, m_w_pool, m_pool_scale, m_ln1_g, m_ln1_b, m_ln2_g, m_ln2_b, _pad_conv_b(m_conv_b)]
    small_v = [v_b_gate, v_w_pool, v_pool_scale, v_ln1_g, v_ln1_b, v_ln2_g, v_ln2_b, _pad_conv_b(v_conv_b)]
    weights = [w_in[0], w_branch_attn[0], w_branch_pool[0], w_out[0], w_ffn_gate[0].T, w_ffn_up[0].T, w_ffn_down[0],
               conv_w[0], _pack_small(small_w)]
    m_in = [m_w_in[0], m_w_branch_attn[0], m_w_branch_pool[0], m_w_out[0], m_w_ffn_gate[0].T, m_w_ffn_up[0].T,
            m_w_ffn_down[0], m_conv_w[0], _pack_small(small_m)]
    v_in = [v_w_in[0], v_w_branch_attn[0], v_w_branch_pool[0], v_w_out[0], v_w_ffn_gate[0].T, v_w_ffn_up[0].T,
            v_w_ffn_down[0], v_conv_w[0], _pack_small(small_v)]
    tiles = [256, 256, 256, 128, 176, 176, 176, 3, SMALL_ROWS]
    jobs = list(zip(weights, m_in, v_in, grads, tiles))
    ffn = slice(4, 7)
    updated = _adamw(jobs[:4] + jobs[7:], "adamw_rest")
    updated = updated[:4] + _adamw(jobs[ffn], "adamw_ffn") + updated[4:]
    res = {nm: [(r.T if nm in ("w_ffn_gate", "w_ffn_up") else r)[None] for r in outs4]
           for nm, outs4 in zip(names[:-1], updated)}
    packed = updated[-1]
    loss = grads[-1][SMALL_ROWS - 4, 0]
    shapes = [w.shape for w in small_w]
    for kind in range(4):
        for nm, val in zip(small_names, _unpack_small(packed[kind], shapes)):
            if nm == "conv_b":
                val = val[:, :FF_SHARD].reshape(1, 4 * FF_SHARD)
            res.setdefault(nm, [None] * 4)[kind] = val

    order = ["w_in", "b_gate", "w_branch_attn", "w_pool", "pool_scale", "w_branch_pool", "w_out", "ln1_g", "ln1_b",
             "w_ffn_gate", "w_ffn_up", "conv_w", "conv_b", "w_ffn_down", "ln2_g", "ln2_b"]
    outs = [loss, grad_x[None]]
    for kind in range(4):
        outs += [res[nm][kind] for nm in order]
    return tuple(outs)
```

```python
import functools

import jax
import jax.numpy as jnp
from jax import lax
from jax.experimental import pallas as pl
from jax.experimental.pallas import tpu as pltpu

D_MODEL = 1024
HEAD_DIM = 64
D_ATTN = 512
D_POOL = 512
MOBA_BLOCK = 256
MOBA_TOPK = 3
POOL_GROUP = 128
MAX_WINDOW = 16
FF_SHARD = 704
FF_PAD = 768
D_FF_PAD = 4 * FF_PAD
N_CHIPS = 4
LANES = 128
ALPHA = (2.0 * 1) ** 0.25
LN_EPS = 1e-5
NEG = -1e30
SCALE = HEAD_DIM ** -0.5
ADAM_LR, ADAM_B1, ADAM_B2, ADAM_EPS, ADAM_WD, ADAM_STEP = 0.001, 0.9, 0.999, 1e-08, 0.01, 10
VMEM_LIMIT = 56 * 1024 * 1024
MESH = pl.DeviceIdType.MESH

bf16 = jnp.bfloat16
f32 = jnp.float32


def _dot(a, b):
    return jnp.dot(a, b, preferred_element_type=f32)


def _dot_nt(a, b):
    return lax.dot_general(a, b, (((1,), (1,)), ((), ())), preferred_element_type=f32)


def _dot_tn(a, b):
    return lax.dot_general(a, b, (((0,), (0,)), ((), ())), preferred_element_type=f32)


def _call(body, *, name, grid, in_specs, out_specs, out_shape, scratch=(), cargo=None):
    if cargo is not None:
        return _cargo_call(body, cargo, name=name, grid=grid, in_specs=in_specs, out_specs=out_specs,
                           out_shape=out_shape, scratch=scratch)
    return pl.pallas_call(
        body, name=name, grid=grid, in_specs=in_specs, out_specs=out_specs, out_shape=out_shape,
        scratch_shapes=list(scratch),
        compiler_params=pltpu.CompilerParams(dimension_semantics=("arbitrary",) * len(grid),
                                             vmem_limit_bytes=VMEM_LIMIT))


def _rot_half(t):
    lane = lax.broadcasted_iota(jnp.int32, t.shape, 1)
    first = (lane % HEAD_DIM) < (HEAD_DIM // 2)
    return jnp.where(first, pltpu.roll(t, LANES - HEAD_DIM // 2, 1), pltpu.roll(t, HEAD_DIM // 2, 1))


def _rope(t, cos, sin_signed):
    return t * cos + _rot_half(t) * sin_signed


def _rope_bwd(d, cos, sin_signed):
    return d * cos + _rot_half(d * sin_signed)


def _gelu_parts(a):
    cdf = 0.5 * (1.0 + lax.erf(a * (2.0 ** -0.5)))
    pdf = jnp.exp(-0.5 * a * a) * ((2.0 * jnp.pi) ** -0.5)
    return a * cdf, cdf + a * pdf


def _layer_norm(h, g, b):
    mu = jnp.mean(h, axis=-1, keepdims=True)
    xc = h - mu
    var = jnp.mean(xc * xc, axis=-1, keepdims=True)
    rstd = lax.rsqrt(var + LN_EPS)
    xhat = xc * rstd
    return xhat * g + b, xhat, rstd


def _layer_norm_bwd(dy, xhat, rstd, g):
    dxh = dy * g
    m1 = jnp.mean(dxh, axis=-1, keepdims=True)
    m2 = jnp.mean(dxh * xhat, axis=-1, keepdims=True)
    return rstd * (dxh - m1 - xhat * m2)


def _inproj(x_bf, win_g, cos_t, sin_t, b_gate, cargo):
    s = x_bf.shape[0]
    tm, tn = 1024, 512

    def body(x_ref, w_ref, cos_ref, sin_ref, b_ref, o_ref):
        j = pl.program_id(1)
        acc = _dot(x_ref[...], w_ref[...])

        @pl.when(j < 2)
        def _():
            for c in range(tn // LANES):
                sl = slice(c * LANES, (c + 1) * LANES)
                o_ref[:, sl] = _rope(acc[:, sl], cos_ref[...], sin_ref[...])

        @pl.when((j >= 2) & (j < 4))
        def _():
            o_ref[...] = acc

        @pl.when(j >= 4)
        def _():
            o_ref[...] = jax.nn.sigmoid(acc + b_ref[...])

    (z,), carried = _cargo_call(
        body, cargo, name="inproj", grid=(s // tm, 4 * D_MODEL // tn),
        in_specs=[pl.BlockSpec((tm, D_MODEL), lambda i, j: (i, 0)),
                  pl.BlockSpec((None, D_MODEL, tn), lambda i, j: (j // 2, 0, j % 2)),
                  pl.BlockSpec((tm, LANES), lambda i, j: (i, 0)),
                  pl.BlockSpec((tm, LANES), lambda i, j: (i, 0)),
                  pl.BlockSpec((1, tn), lambda i, j: (0, jnp.maximum(j - 4, 0)))],
        out_specs=[pl.BlockSpec((tm, tn), lambda i, j: (i, j))],
        out_shape=[jax.ShapeDtypeStruct((s, 4 * D_MODEL), f32)])(x_bf, win_g, cos_t, sin_t, b_gate)
    return z, carried


STAT_ROWS = 8


def _pair_rows(v0, v1, fill):
    r = lax.broadcasted_iota(jnp.int32, (STAT_ROWS, v0.shape[1]), 0)
    return jnp.where(r == 0, v0, jnp.where(r == 1, v1, fill))


def _head_lanes(shape):
    lane = lax.broadcasted_iota(jnp.int32, shape, 1)
    return lane < HEAD_DIM, lane >= HEAD_DIM


def _head_rows(shape):
    row = lax.broadcasted_iota(jnp.int32, shape, 0)
    return row < HEAD_DIM, row >= HEAD_DIM


def _moba_select(q_bf, k_all, i, nb):
    k_mean = jnp.mean(k_all.reshape(nb, MOBA_BLOCK, LANES), axis=1)
    n_io = lax.broadcasted_iota(jnp.int32, (nb, MOBA_BLOCK), 0)
    past = n_io < i
    sels = []
    for head in _head_lanes((nb, LANES)):
        gate = _dot_nt(jnp.where(head, k_mean, 0.0).astype(bf16), q_bf)
        g = jnp.where(past, gate, NEG)
        rank = jnp.zeros((nb, MOBA_BLOCK), f32)
        for m in range(nb):
            gm = g[m:m + 1, :]
            rank = rank + jnp.where((gm > g) | ((gm == g) & (m < n_io)), 1.0, 0.0)
        sels.append(jnp.where(past & (rank < MOBA_TOPK), 1.0, 0.0))
    return sels


def _attn_fwd(z, nb, w_pool, pool_scale, cargo):
    s = z.shape[0]

    def body(q_ref, k_ref, v_ref, u_ref, wp_ref, sc_ref, o_ref, lse_ref, pooled_ref, mixed_ref, pm_ref, vt_s, pad):
        pair, i = pl.program_id(0), pl.program_id(1)

        @pl.when(i == 0)
        def _():
            for j in range(nb):
                vt_s[j] = v_ref[j * MOBA_BLOCK:(j + 1) * MOBA_BLOCK, :].T.astype(bf16)
            _pool_group(pair, u_ref, wp_ref, sc_ref, pooled_ref, mixed_ref, pm_ref, pad)

        k_heads = _head_lanes((MOBA_BLOCK, LANES))
        o_heads = _head_rows((LANES, MOBA_BLOCK))

        def block(j, qs, allow, m_old, l_old):
            kj, vtj = k_ref[j * MOBA_BLOCK:(j + 1) * MOBA_BLOCK, :], vt_s[j]
            m_new, alpha, l_new, pv = [], [], [], None
            for h in range(2):
                sc = jnp.where(allow[h], _dot_nt(jnp.where(k_heads[h], kj, 0.0).astype(bf16), qs), NEG)
                mx = jnp.max(sc, axis=0, keepdims=True)
                mn = mx if m_old is None else jnp.maximum(m_old[h], mx)
                p = jnp.exp(sc - mn)
                lsum = jnp.sum(p, axis=0, keepdims=True)
                if m_old is None:
                    a = None
                else:
                    a = jnp.exp(m_old[h] - mn)
                    lsum = a * l_old[h] + lsum
                t = _dot(jnp.where(o_heads[h], vtj, jnp.zeros_like(vtj)), p.astype(bf16))
                pv = t if pv is None else pv + t
                m_new.append(mn)
                alpha.append(a)
                l_new.append(lsum)
            return m_new, alpha, l_new, pv

        def query_block(c):
            q = q_ref[...]
            sels = _moba_select(q.astype(bf16), k_ref[...], c, nb)
            qs = (q * SCALE).astype(bf16)
            key = lax.broadcasted_iota(jnp.int32, (MOBA_BLOCK, MOBA_BLOCK), 0)
            qry = lax.broadcasted_iota(jnp.int32, (MOBA_BLOCK, MOBA_BLOCK), 1)
            m, _, l, acc = block(c, qs, [key <= qry] * 2, None, None)
            for j in range(c):
                m, alpha, l, pv = block(j, qs, [sels[h][j:j + 1, :] > 0.0 for h in range(2)], m, l)
                acc = acc * jnp.where(o_heads[0], alpha[0], alpha[1]) + pv
            o_ref[...] = (acc / jnp.where(o_heads[0], l[0], l[1])).T.astype(bf16)
            lse_ref[...] = _pair_rows(m[0] + jnp.log(l[0]), m[1] + jnp.log(l[1]), 0.0)

        for c in range(nb):
            pl.when(i == c)(functools.partial(query_block, c))

    group = pl.BlockSpec((s, POOL_GROUP), lambda hp, i: (0, hp))
    return _cargo_call(
        body, cargo, name="attn_fwd", grid=(D_ATTN // LANES, nb),
        in_specs=[pl.BlockSpec((MOBA_BLOCK, LANES), lambda hp, i: (i, hp)),
                  pl.BlockSpec((s, LANES), lambda hp, i: (0, 4 + hp)),
                  pl.BlockSpec((s, LANES), lambda hp, i: (0, 8 + hp)),
                  pl.BlockSpec((s, POOL_GROUP), lambda hp, i: (0, 12 + hp)),
                  pl.BlockSpec((None, POOL_GROUP, POOL_GROUP), lambda hp, i: (hp, 0, 0)),
                  pl.BlockSpec((1, POOL_GROUP), lambda hp, i: (0, hp))],
        out_specs=[pl.BlockSpec((MOBA_BLOCK, LANES), lambda hp, i: (i, hp)),
                   pl.BlockSpec((None, STAT_ROWS, MOBA_BLOCK), lambda hp, i: (hp, 0, i)), group, group, group],
        out_shape=[jax.ShapeDtypeStruct((s, D_ATTN), bf16),
                   jax.ShapeDtypeStruct((D_ATTN // LANES, STAT_ROWS, s), f32),
                   jax.ShapeDtypeStruct((s, D_POOL), bf16), jax.ShapeDtypeStruct((s, D_POOL), f32),
                   jax.ShapeDtypeStruct((s, D_POOL), bf16)],
        scratch=[pltpu.VMEM((nb, LANES, MOBA_BLOCK), bf16), pltpu.VMEM((s + MAX_WINDOW, POOL_GROUP), f32)],
    )(z, z, z, z, w_pool, pool_scale)


def _window_select(snaps, g):
    return jnp.where(g == 0, snaps[0], jnp.where(g == 1, snaps[1], jnp.where(g == 2, snaps[2], snaps[3])))


def _pool_group(g, u_ref, w_ref, sc_ref, pooled_ref, mixed_ref, pm_ref, pad):
    s = u_ref.shape[0]
    u = u_ref[...]
    pad[0:MAX_WINDOW, :] = jnp.zeros((MAX_WINDOW, POOL_GROUP), f32)
    pad[MAX_WINDOW:MAX_WINDOW + s, :] = u
    acc = u
    snaps = []
    for d in range(1, MAX_WINDOW):
        acc = acc + pad[MAX_WINDOW - d:MAX_WINDOW - d + s, :]
        if d + 1 in (2, 4, 8, 16):
            snaps.append(acc)
    win = _window_select(snaps, g)
    t = lax.broadcasted_iota(jnp.int32, (s, POOL_GROUP), 0)
    count = jnp.minimum(t + 1, jnp.left_shift(2, g)).astype(f32)
    pooled = (win / count - u).astype(bf16)
    mixed = _dot(pooled, w_ref[...].astype(bf16))
    pooled_ref[...] = pooled
    mixed_ref[...] = mixed
    pm_ref[...] = (mixed * sc_ref[...]).astype(bf16)


def _branch_merge(o_bf, pm_bf, z, wba, wbp):
    s = o_bf.shape[0]
    tm = 512

    def body(o_ref, pm_ref, ga_ref, gp_ref, wba_ref, wbp_ref, m_ref):
        ya = _dot(o_ref[...], wba_ref[...])
        yp = _dot(pm_ref[...], wbp_ref[...])
        m_ref[...] = (ga_ref[...] * ya + gp_ref[...] * yp).astype(bf16)

    full = lambda r, c: pl.BlockSpec((r, c), lambda i: (0, 0))
    return _call(
        body, name="branch_merge", grid=(s // tm,),
        in_specs=[pl.BlockSpec((tm, D_ATTN), lambda i: (i, 0)), pl.BlockSpec((tm, D_POOL), lambda i: (i, 0)),
                  pl.BlockSpec((tm, D_MODEL), lambda i: (i, 2)), pl.BlockSpec((tm, D_MODEL), lambda i: (i, 3)),
                  full(D_ATTN, D_MODEL), full(D_POOL, D_MODEL)],
        out_specs=pl.BlockSpec((tm, D_MODEL), lambda i: (i, 0)),
        out_shape=jax.ShapeDtypeStruct((s, D_MODEL), bf16))(o_bf, pm_bf, z, z, wba, wbp)


def _out_ln1(m_bf, wout, x, ln_g, ln_b):
    s = x.shape[0]
    tm = 512

    def body(m_ref, w_ref, x_ref, g_ref, b_ref, x1_ref, xhat_ref, rstd_ref):
        h = ALPHA * x_ref[...] + _dot(m_ref[...], w_ref[...])
        y, xhat, rstd = _layer_norm(h, g_ref[...], b_ref[...])
        x1_ref[...] = y.astype(bf16)
        xhat_ref[...] = xhat
        rstd_ref[...] = jnp.broadcast_to(rstd, (tm, LANES))

    row = pl.BlockSpec((tm, D_MODEL), lambda i: (i, 0))
    vec = pl.BlockSpec((1, D_MODEL), lambda i: (0, 0))
    return _call(
        body, name="out_ln1", grid=(s // tm,),
        in_specs=[row, pl.BlockSpec((D_MODEL, D_MODEL), lambda i: (0, 0)), row, vec, vec],
        out_specs=[row, row, pl.BlockSpec((tm, LANES), lambda i: (i, 0))],
        out_shape=[jax.ShapeDtypeStruct((s, D_MODEL), bf16), jax.ShapeDtypeStruct((s, D_MODEL), f32),
                   jax.ShapeDtypeStruct((s, LANES), f32)])(m_bf, wout, x, ln_g, ln_b)


FF_TILE = 256
FF_TILES_PER_SHARD = FF_PAD // FF_TILE
CONV_PAD = 8
CONV_ROWS = 16

def _ff_weight_spec(rows):
    return pl.BlockSpec((None, rows, FF_TILE), lambda j: (j // FF_TILES_PER_SHARD, 0, j % FF_TILES_PER_SHARD))


def _ff_rows_spec():
    return pl.BlockSpec((None, FF_TILE, D_MODEL), lambda j: (j // FF_TILES_PER_SHARD, j % FF_TILES_PER_SHARD, 0))


def _row_shifts(x, shifts):
    s = x.shape[0]
    padded = jnp.concatenate([x, jnp.zeros((CONV_PAD, x.shape[1]), x.dtype)], axis=0)
    return [pltpu.roll(padded, d % (s + CONV_PAD), 0)[0:s] for d in shifts]


def _ffn_up(x1_bf, wg_g, wu_g, convw_g, convb_g, cargo):
    s = x1_bf.shape[0]

    def body(x_ref, wg_ref, wu_ref, cw_ref, cb_ref, a_ref, up_ref, ac_ref, hh_ref):
        x1 = x_ref[...]
        a = _dot_nt(x1, wg_ref[...])
        up = _dot_nt(x1, wu_ref[...])
        a_ref[...] = a
        up_ref[...] = up
        a1, a2 = _row_shifts(a, (1, 2))
        ac = a * cw_ref[2:3, :] + a1 * cw_ref[1:2, :] + a2 * cw_ref[0:1, :] + cb_ref[...]
        ac_ref[...] = ac
        hg, _ = _gelu_parts(ac)
        hh_ref[...] = (hg * up).astype(bf16)

    col = pl.BlockSpec((s, FF_TILE), lambda j: (0, j))
    wide = jax.ShapeDtypeStruct((s, D_FF_PAD), f32)
    return _call(
        body, name="ffn_up", grid=(D_FF_PAD // FF_TILE,),
        in_specs=[pl.BlockSpec((s, D_MODEL), lambda j: (0, 0)), _ff_rows_spec(), _ff_rows_spec(),
                  _ff_weight_spec(8), _ff_weight_spec(1)],
        out_specs=[col, col, col, col],
        out_shape=[wide, wide, wide, jax.ShapeDtypeStruct((s, D_FF_PAD), bf16)],
        cargo=cargo)(x1_bf, wg_g, wu_g, convw_g, convb_g)


def _ffn_down_ln2_loss(hh_bf, wd, xhat1, ln1_g, ln1_b, ln2_g, ln2_b, target):
    s = hh_bf.shape[0]
    tm = 256

    def body(hh_ref, w_ref, xh1_ref, g1_ref, b1_ref, g2_ref, b2_ref, t_ref, loss_ref, dh_ref, dhb_ref, dg_ref, db_ref):
        i = pl.program_id(0)
        x1 = xh1_ref[...] * g1_ref[...] + b1_ref[...]
        h = ALPHA * x1 + _dot(hh_ref[...], w_ref[...])
        y, xhat, rstd = _layer_norm(h, g2_ref[...], b2_ref[...])
        err = y - t_ref[...]
        part = 0.5 * jnp.sum(jnp.mean(err * err, axis=-1, keepdims=True), axis=0, keepdims=True)
        dy = err * (1.0 / D_MODEL)

        @pl.when(i == 0)
        def _():
            loss_ref[...] = jnp.zeros_like(loss_ref)
            dg_ref[...] = jnp.zeros_like(dg_ref)
            db_ref[...] = jnp.zeros_like(db_ref)

        loss_ref[...] += jnp.broadcast_to(part, loss_ref.shape)
        dg_ref[...] += jnp.sum(dy * xhat, axis=0, keepdims=True)
        db_ref[...] += jnp.sum(dy, axis=0, keepdims=True)
        dh = _layer_norm_bwd(dy, xhat, rstd, g2_ref[...])
        dh_ref[...] = dh
        dhb_ref[...] = dh.astype(bf16)

    row = pl.BlockSpec((tm, D_MODEL), lambda i: (i, 0))
    vec = pl.BlockSpec((1, D_MODEL), lambda i: (0, 0))
    return _call(
        body, name="ffn_down_ln2_loss", grid=(s // tm,),
        in_specs=[pl.BlockSpec((tm, D_FF_PAD), lambda i: (i, 0)),
                  pl.BlockSpec((D_FF_PAD, D_MODEL), lambda i: (0, 0)),
                  row, vec, vec, vec, vec, row],
        out_specs=[pl.BlockSpec((8, LANES), lambda i: (0, 0)), row, row, vec, vec],
        out_shape=[jax.ShapeDtypeStruct((8, LANES), f32), jax.ShapeDtypeStruct((s, D_MODEL), f32),
                   jax.ShapeDtypeStruct((s, D_MODEL), bf16), jax.ShapeDtypeStruct((1, D_MODEL), f32),
                   jax.ShapeDtypeStruct((1, D_MODEL), f32)])(hh_bf, wd, xhat1, ln1_g, ln1_b, ln2_g, ln2_b, target)


def _ffn_act_bwd(dh2_bf, wd_g, ac, a, up, convw_g):
    s = dh2_bf.shape[0]

    def body(dh_ref, wd_ref, ac_ref, a_ref, up_ref, cw_ref, da_ref, dup_ref, dc_ref):
        dhh = _dot_nt(dh_ref[...], wd_ref[...])
        hg, dgelu = _gelu_parts(ac_ref[...])
        dup_ref[...] = (dhh * hg).astype(bf16)
        dac = dhh * up_ref[...] * dgelu
        dac1, dac2 = _row_shifts(dac, (-1, -2))
        a = a_ref[...]
        rows = [jnp.sum(d * a, axis=0, keepdims=True) for d in (dac2, dac1, dac)]
        rows.append(jnp.sum(dac, axis=0, keepdims=True))
        rows.append(jnp.zeros((CONV_ROWS - 4, FF_TILE), f32))
        dc_ref[...] = jnp.concatenate(rows, axis=0)
        da_ref[...] = (dac * cw_ref[2:3, :] + dac1 * cw_ref[1:2, :] + dac2 * cw_ref[0:1, :]).astype(bf16)

    col = pl.BlockSpec((s, FF_TILE), lambda j: (0, j))
    return _call(
        body, name="ffn_act_bwd", grid=(D_FF_PAD // FF_TILE,),
        in_specs=[pl.BlockSpec((s, D_MODEL), lambda j: (0, 0)), _ff_rows_spec(), col, col, col, _ff_weight_spec(8)],
        out_specs=[col, col, _ff_weight_spec(CONV_ROWS)],
        out_shape=[jax.ShapeDtypeStruct((s, D_FF_PAD), bf16), jax.ShapeDtypeStruct((s, D_FF_PAD), bf16),
                   jax.ShapeDtypeStruct((N_CHIPS, CONV_ROWS, FF_PAD), f32)],
    )(dh2_bf, wd_g, ac, a, up, convw_g)


def _matmul_tn(a, b, n_shards, name, tm=512, tn=1024, cargo=None):
    k, m = a.shape
    n = b.shape[1]
    tm, tn = min(tm, m), min(tn, n // n_shards)
    per = n // n_shards // tn

    def body(a_ref, b_ref, o_ref, at_s):
        @pl.when(pl.program_id(1) == 0)
        def _():
            at_s[...] = a_ref[...].T

        o_ref[...] = _dot(at_s[...], b_ref[...]).astype(bf16)

    res = _call(
        body, name=name, grid=(m // tm, n // tn),
        in_specs=[pl.BlockSpec((k, tm), lambda i, j: (0, i)), pl.BlockSpec((k, tn), lambda i, j: (0, j))],
        out_specs=[pl.BlockSpec((None, tm, tn), lambda i, j: (j // per, i, j % per))],
        out_shape=[jax.ShapeDtypeStruct((n_shards, m, n // n_shards), bf16)], scratch=[pltpu.VMEM((tm, k), bf16)],
        cargo=cargo)(a, b)
    return res[0] if cargo is None else (res[0][0], res[1])


def _ffn_in_bwd_ln1(da_bf, dup_bf, wg_g, wu_g, dh2, xhat1, rstd1, ln1_g, cargo):
    s = da_bf.shape[0]
    tm = 256

    def body(da_ref, dup_ref, wg_ref, wu_ref, dh2_ref, xh_ref, rstd_ref, g_ref, dh_ref, dhb_ref, dg_ref, db_ref):
        i = pl.program_id(0)
        dx1 = ALPHA * dh2_ref[...]
        for sh in range(N_CHIPS):
            sl = slice(sh * FF_PAD, (sh + 1) * FF_PAD)
            dx1 = dx1 + _dot(da_ref[:, sl], wg_ref[sh]) + _dot(dup_ref[:, sl], wu_ref[sh])
        xhat = xh_ref[...]

        @pl.when(i == 0)
        def _():
            dg_ref[...] = jnp.zeros_like(dg_ref)
            db_ref[...] = jnp.zeros_like(db_ref)

        dg_ref[...] += jnp.sum(dx1 * xhat, axis=0, keepdims=True)
        db_ref[...] += jnp.sum(dx1, axis=0, keepdims=True)
        dh = _layer_norm_bwd(dx1, xhat, rstd_ref[:, 0:1], g_ref[...])
        dh_ref[...] = dh
        dhb_ref[...] = dh.astype(bf16)

    row = pl.BlockSpec((tm, D_MODEL), lambda i: (i, 0))
    wide = pl.BlockSpec((tm, D_FF_PAD), lambda i: (i, 0))
    wfull = pl.BlockSpec((N_CHIPS, FF_PAD, D_MODEL), lambda i: (0, 0, 0))
    vec = pl.BlockSpec((1, D_MODEL), lambda i: (0, 0))
    return _call(
        body, name="ffn_in_bwd_ln1", grid=(s // tm,),
        in_specs=[wide, wide, wfull, wfull, row, row, pl.BlockSpec((tm, LANES), lambda i: (i, 0)), vec],
        out_specs=[row, row, vec, vec],
        out_shape=[jax.ShapeDtypeStruct((s, D_MODEL), f32), jax.ShapeDtypeStruct((s, D_MODEL), bf16),
                   jax.ShapeDtypeStruct((1, D_MODEL), f32), jax.ShapeDtypeStruct((1, D_MODEL), f32)],
        cargo=cargo)(da_bf, dup_bf, wg_g, wu_g, dh2, xhat1, rstd1, ln1_g)


def _merge_bwd(dh1_bf, wout, o_bf, pm_bf, z, wba, wbp):
    s = dh1_bf.shape[0]
    tm = 256

    def body(dh_ref, wout_ref, o_ref, pm_ref, ga_ref, gp_ref, wba_ref, wbp_ref,
             dzg_ref, dya_ref, dyp_ref, do_ref, dpm_ref, dbg_ref):
        i = pl.program_id(0)
        dm = _dot_nt(dh_ref[...], wout_ref[...])
        ya = _dot(o_ref[...], wba_ref[...])
        yp = _dot(pm_ref[...], wbp_ref[...])
        ga, gp = ga_ref[...], gp_ref[...]
        dza = dm * ya * ga * (1.0 - ga)
        dzp = dm * yp * gp * (1.0 - gp)

        @pl.when(i == 0)
        def _():
            dbg_ref[...] = jnp.zeros_like(dbg_ref)

        dbg_ref[:, 0:D_MODEL] += jnp.sum(dza, axis=0, keepdims=True)
        dbg_ref[:, D_MODEL:2 * D_MODEL] += jnp.sum(dzp, axis=0, keepdims=True)
        dzg_ref[:, 0:D_MODEL] = dza.astype(bf16)
        dzg_ref[:, D_MODEL:2 * D_MODEL] = dzp.astype(bf16)
        dya = (dm * ga).astype(bf16)
        dyp = (dm * gp).astype(bf16)
        dya_ref[...] = dya
        dyp_ref[...] = dyp
        do_ref[...] = _dot_nt(dya, wba_ref[...]).astype(bf16)
        dpm_ref[...] = _dot_nt(dyp, wbp_ref[...])

    row = pl.BlockSpec((tm, D_MODEL), lambda i: (i, 0))
    half = pl.BlockSpec((tm, D_ATTN), lambda i: (i, 0))
    full = lambda r, c: pl.BlockSpec((r, c), lambda i: (0, 0))
    return _call(
        body, name="merge_bwd", grid=(s // tm,),
        in_specs=[row, full(D_MODEL, D_MODEL), half, half,
                  pl.BlockSpec((tm, D_MODEL), lambda i: (i, 2)), pl.BlockSpec((tm, D_MODEL), lambda i: (i, 3)),
                  full(D_ATTN, D_MODEL), full(D_POOL, D_MODEL)],
        out_specs=[pl.BlockSpec((tm, 2 * D_MODEL), lambda i: (i, 0)), row, row, half, half,
                   pl.BlockSpec((1, 2 * D_MODEL), lambda i: (0, 0))],
        out_shape=[jax.ShapeDtypeStruct((s, 2 * D_MODEL), bf16), jax.ShapeDtypeStruct((s, D_MODEL), bf16),
                   jax.ShapeDtypeStruct((s, D_MODEL), bf16), jax.ShapeDtypeStruct((s, D_ATTN), bf16),
                   jax.ShapeDtypeStruct((s, D_POOL), f32), jax.ShapeDtypeStruct((1, 2 * D_MODEL), f32)],
    )(dh1_bf, wout, o_bf, pm_bf, z, z, wba, wbp)


def _pool_bwd(dpm, mixed, pooled_bf, w_pool, pool_scale):
    s = dpm.shape[0]

    def body(dpm_ref, mixed_ref, pooled_ref, w_ref, sc_ref, du_ref, dw_ref, dsc_ref, pad):
        g = pl.program_id(0)
        dpm_v = dpm_ref[...]
        dsc_ref[...] = jnp.sum(dpm_v * mixed_ref[...], axis=0, keepdims=True)
        dmixed = (dpm_v * sc_ref[...]).astype(bf16)
        dw_ref[...] = _dot_tn(pooled_ref[...], dmixed)
        dpooled = _dot_nt(dmixed, w_ref[...].astype(bf16))
        t = lax.broadcasted_iota(jnp.int32, (s, POOL_GROUP), 0)
        count = jnp.minimum(t + 1, jnp.left_shift(2, g)).astype(f32)
        r = dpooled / count
        pad[0:s, :] = r
        pad[s:s + MAX_WINDOW, :] = jnp.zeros((MAX_WINDOW, POOL_GROUP), f32)
        acc = r
        snaps = []
        for d in range(1, MAX_WINDOW):
            acc = acc + pad[d:d + s, :]
            if d + 1 in (2, 4, 8, 16):
                snaps.append(acc)
        du_ref[...] = (_window_select(snaps, g) - dpooled).astype(bf16)

    blk = pl.BlockSpec((s, POOL_GROUP), lambda g: (0, g))
    return _call(
        body, name="pool_bwd", grid=(4,),
        in_specs=[blk, blk, blk, pl.BlockSpec((None, POOL_GROUP, POOL_GROUP), lambda g: (g, 0, 0)),
                  pl.BlockSpec((1, POOL_GROUP), lambda g: (0, g))],
        out_specs=[blk, pl.BlockSpec((None, POOL_GROUP, POOL_GROUP), lambda g: (g, 0, 0)),
                   pl.BlockSpec((1, POOL_GROUP), lambda g: (0, g))],
        out_shape=[jax.ShapeDtypeStruct((s, D_POOL), bf16), jax.ShapeDtypeStruct((4, POOL_GROUP, POOL_GROUP), f32),
                   jax.ShapeDtypeStruct((1, D_POOL), f32)],
        scratch=[pltpu.VMEM((s + MAX_WINDOW, POOL_GROUP), f32)])(dpm, mixed, pooled_bf, w_pool, pool_scale)


def _attn_bwd(z, o_bf, lse, do_bf, cos_t, sin_t, nb, cargo):
    s = z.shape[0]

    def body(q_ref, k_ref, v_ref, o_ref, lse_ref, do_ref, cq_ref, sq_ref, cf_ref, sf_ref,
             dq_ref, dk_ref, dv_ref, kt_s, dk_s, dv_s):
        i = pl.program_id(1)

        @pl.when(i == 0)
        def _():
            dk_s[...] = jnp.zeros_like(dk_s)
            dv_s[...] = jnp.zeros_like(dv_s)
            for j in range(nb):
                kt_s[j] = k_ref[j * MOBA_BLOCK:(j + 1) * MOBA_BLOCK, :].T.astype(bf16)

        k_heads = _head_lanes((MOBA_BLOCK, LANES))
        t_heads = _head_rows((LANES, MOBA_BLOCK))

        def query_block(c):
            q = q_ref[...]
            sels = _moba_select(q.astype(bf16), k_ref[...], c, nb)
            qs = (q * SCALE).astype(bf16)
            qraw = [jnp.where(hm, q, 0.0).astype(bf16) for hm in k_heads]
            do = do_ref[...].astype(f32)
            do_bf = do_ref[...]
            dob = [jnp.where(hm, do, 0.0).astype(bf16) for hm in k_heads]
            pick = _head_lanes((STAT_ROWS, LANES))
            r8 = lax.broadcasted_iota(jnp.int32, (STAT_ROWS, LANES), 0)
            head_pick = jnp.where(((r8 == 0) & pick[0]) | ((r8 == 1) & pick[1]), 1.0, 0.0)
            delta8 = lax.dot_general(head_pick, do * o_ref[...].astype(f32), (((1,), (1,)), ((), ())),
                                     precision=lax.Precision.HIGHEST, preferred_element_type=f32)
            lse8 = lse_ref[...]
            delta, lse_h = [delta8[0:1, :], delta8[1:2, :]], [lse8[0:1, :], lse8[1:2, :]]
            key = lax.broadcasted_iota(jnp.int32, (MOBA_BLOCK, MOBA_BLOCK), 0)
            qry = lax.broadcasted_iota(jnp.int32, (MOBA_BLOCK, MOBA_BLOCK), 1)

            def block(j, allow):
                rows = slice(j * MOBA_BLOCK, (j + 1) * MOBA_BLOCK)
                kj, vj, ktj = k_ref[rows, :], v_ref[rows, :], kt_s[j]
                dqt, dk, dv = None, None, None
                for h in range(2):
                    sc = _dot_nt(jnp.where(k_heads[h], kj, 0.0).astype(bf16), qs)
                    p = jnp.where(allow[h], jnp.exp(sc - lse_h[h]), 0.0)
                    dp = _dot_nt(jnp.where(k_heads[h], vj, 0.0).astype(bf16), do_bf)
                    ds = (p * (dp - delta[h]) * SCALE).astype(bf16)
                    tv = _dot(p.astype(bf16), dob[h])
                    tk = _dot(ds, qraw[h])
                    tq = _dot(jnp.where(t_heads[h], ktj, jnp.zeros_like(ktj)), ds)
                    dqt, dk, dv = (tq, tk, tv) if dqt is None else (dqt + tq, dk + tk, dv + tv)
                dk_s[rows, :] += dk
                dv_s[rows, :] += dv
                return dqt

            dqt = block(c, [key <= qry] * 2)
            for j in range(c):
                dqt = dqt + block(j, [sels[h][j:j + 1, :] > 0.0 for h in range(2)])
            dq_ref[...] = _rope_bwd(dqt.T, cq_ref[...], sq_ref[...]).astype(bf16)

        for c in range(nb):
            pl.when(i == c)(functools.partial(query_block, c))

        @pl.when(i == nb - 1)
        def _():
            dk_ref[...] = _rope_bwd(dk_s[...], cf_ref[...], sf_ref[...]).astype(bf16)
            dv_ref[...] = dv_s[...].astype(bf16)

    qblk = pl.BlockSpec((MOBA_BLOCK, LANES), lambda hp, i: (i, hp))
    tq = pl.BlockSpec((MOBA_BLOCK, LANES), lambda hp, i: (i, 0))
    tf = pl.BlockSpec((s, LANES), lambda hp, i: (0, 0))
    colblk = pl.BlockSpec((s, LANES), lambda hp, i: (0, hp))
    return _cargo_call(
        body, cargo, name="attn_bwd", grid=(D_ATTN // LANES, nb),
        in_specs=[qblk, pl.BlockSpec((s, LANES), lambda hp, i: (0, 4 + hp)),
                  pl.BlockSpec((s, LANES), lambda hp, i: (0, 8 + hp)), qblk,
                  pl.BlockSpec((None, STAT_ROWS, MOBA_BLOCK), lambda hp, i: (hp, 0, i)), qblk, tq, tq, tf, tf],
        out_specs=[qblk, colblk, colblk],
        out_shape=[jax.ShapeDtypeStruct((s, D_ATTN), bf16)] * 3,
        scratch=[pltpu.VMEM((nb, LANES, MOBA_BLOCK), bf16), pltpu.VMEM((s, LANES), f32), pltpu.VMEM((s, LANES), f32)],
    )(z, z, z, o_bf, lse, do_bf, cos_t, sin_t, cos_t, sin_t)


def _inproj_dx(dz_bf, win_g, dh1, cargo):
    s = dz_bf.shape[0]
    tm = 256

    def body(dz_ref, w_ref, dh_ref, gx_ref):
        acc = ALPHA * dh_ref[...]
        for sh in range(N_CHIPS):
            acc = acc + _dot_nt(dz_ref[:, sh * D_MODEL:(sh + 1) * D_MODEL], w_ref[sh])
        gx_ref[...] = acc

    row = pl.BlockSpec((tm, D_MODEL), lambda i: (i, 0))
    return _call(
        body, name="inproj_dx", grid=(s // tm,),
        in_specs=[pl.BlockSpec((tm, 4 * D_MODEL), lambda i: (i, 0)),
                  pl.BlockSpec((N_CHIPS, D_MODEL, D_MODEL), lambda i: (0, 0, 0)), row],
        out_specs=[row], out_shape=[jax.ShapeDtypeStruct((s, D_MODEL), f32)], cargo=cargo)(dz_bf, win_g, dh1)


ANY = pl.BlockSpec(memory_space=pl.ANY)


def _chip_index():
    return 2 * lax.axis_index("x") + lax.axis_index("y")


def _peer(k):
    x, y, c = lax.axis_index("x"), lax.axis_index("y"), lax.axis_index("c")
    return (x ^ (k >> 1), y ^ (k & 1), c)


def _sibling():
    return (lax.axis_index("x"), lax.axis_index("y"), 1 - lax.axis_index("c"))


class _GatherCargo:
    def __init__(self, shards, pass_on):
        self.shards, self.pass_on, n = shards, pass_on, len(shards)
        self.inputs = [a.reshape(2, a.shape[0] // 2, a.shape[1]) for a in shards]
        self.out_shape = [jax.ShapeDtypeStruct((N_CHIPS,) + a.shape, a.dtype) for a in self.inputs]
        sem, one = pltpu.SemaphoreType.DMA((N_CHIPS - 1, n)), pltpu.SemaphoreType.DMA((n,))
        self.sems = [sem, sem, sem, sem, one, one]

    def _copies(self, src, dst, sems, a):
        send_sems, recv_sems, fsend_sems, frecv_sems, local_sems, own_sems = sems
        p, c = _chip_index(), lax.axis_index("c")
        own = pltpu.make_async_remote_copy(src_ref=src[a], dst_ref=dst[a].at[p], send_sem=local_sems.at[a],
                                           recv_sem=own_sems.at[a], device_id=_sibling(), device_id_type=MESH)
        out, arrive, onward, landed = [], [], [], []
        for k in range(1, N_CHIPS):
            ici = dict(send_sem=send_sems.at[k - 1, a], recv_sem=recv_sems.at[k - 1, a], device_id=_peer(k),
                       device_id_type=MESH)
            d2d = dict(send_sem=fsend_sems.at[k - 1, a], recv_sem=frecv_sems.at[k - 1, a], device_id=_sibling(),
                       device_id_type=MESH)
            got, theirs = dst[a].at[p ^ k, c], dst[a].at[p ^ k, 1 - c]
            out.append(pltpu.make_async_remote_copy(src_ref=src[a].at[c], dst_ref=dst[a].at[p, c], **ici))
            arrive.append(pltpu.make_async_remote_copy(src_ref=src[a].at[c], dst_ref=got, **ici))
            onward.append(pltpu.make_async_remote_copy(src_ref=got, dst_ref=got, **d2d))
            landed.append(pltpu.make_async_remote_copy(src_ref=theirs, dst_ref=theirs, **d2d))
        return own, out, arrive, onward, landed

    def stages(self, steps):
        n = len(self.shards)

        def start(src, dst, sems):
            for a in range(n):
                own, out, _, _, _ = self._copies(src, dst, sems, a)
                own.start()
                for cp in out:
                    cp.start()

        def pass_on(a):
            def act(src, dst, sems):
                _, _, arrive, onward, _ = self._copies(src, dst, sems, a)
                for k in range(N_CHIPS - 1):
                    arrive[k].wait_recv()
                    onward[k].start()
            return act

        def finish(src, dst, sems):
            for a in range(n):
                own, out, _, onward, landed = self._copies(src, dst, sems, a)
                for cp in landed:
                    cp.wait_recv()
                for cp in out + onward:
                    cp.wait_send()
                own.wait()

        mids = [(min(steps - 1, int(self.pass_on[a] * steps)), pass_on(a)) for a in range(n)]
        return [(0, start)] + mids + [(steps - 1, finish)]

    def results(self, outs):
        return [o.reshape((N_CHIPS,) + a.shape) for o, a in zip(outs, self.shards)]


class _ExchangeCargo:
    def __init__(self, sums, whole):
        self.inputs, self.whole, n = sums, whole, len(sums)
        self.out_shape = [jax.ShapeDtypeStruct((N_CHIPS - 1,) + g.shape[1:], g.dtype) for g in sums]
        sem = pltpu.SemaphoreType.DMA((N_CHIPS - 1, n))
        self.sems = [sem, sem]

    def _copies(self, src, dst, sems):
        send_sems, recv_sems = sems
        p = _chip_index()
        return [pltpu.make_async_remote_copy(
            src_ref=src[a].at[0] if self.whole[a] else src[a].at[p ^ k], dst_ref=dst[a].at[k - 1],
            send_sem=send_sems.at[k - 1, a], recv_sem=recv_sems.at[k - 1, a], device_id=_peer(k), device_id_type=MESH)
            for k in range(1, N_CHIPS) for a in range(len(self.inputs))]

    def stages(self, steps):
        def start(src, dst, sems):
            for cp in self._copies(src, dst, sems):
                cp.start()

        def finish(src, dst, sems):
            copies = self._copies(src, dst, sems)
            for cp in copies:
                cp.wait_recv()
            for cp in copies:
                cp.wait_send()

        return [(0, start), (steps - 1, finish)]

    def results(self, outs):
        return list(outs)


class _SwapCargo:
    def __init__(self, split):
        self.inputs, n = split, len(split)
        self.out_shape = [jax.ShapeDtypeStruct((g.shape[0],) + g.shape[2:], g.dtype) for g in split]
        sem = pltpu.SemaphoreType.DMA((n,))
        self.sems = [sem, sem]

    def _copies(self, src, dst, sems):
        c = lax.axis_index("c")
        return [pltpu.make_async_remote_copy(src_ref=src[a].at[:, 1 - c], dst_ref=dst[a], send_sem=sems[0].at[a],
                                             recv_sem=sems[1].at[a], device_id=_sibling(), device_id_type=MESH)
                for a in range(len(self.inputs))]

    def stages(self, steps):
        def start(src, dst, sems):
            for cp in self._copies(src, dst, sems):
                cp.start()

        def finish(src, dst, sems):
            copies = self._copies(src, dst, sems)
            for cp in copies:
                cp.wait_recv()
            for cp in copies:
                cp.wait_send()

        return [(0, start), (steps - 1, finish)]

    def results(self, outs):
        return list(outs)


class _MultiCargo:
    def __init__(self, cargos):
        self.cargos = cargos
        self.inputs = [a for cg in cargos for a in cg.inputs]
        self.out_shape = [o for cg in cargos for o in cg.out_shape]
        self.sems = [s for cg in cargos for s in cg.sems]

    def _parts(self, refs, count):
        out, off = [], 0
        for cg in self.cargos:
            out.append(refs[off:off + count(cg)])
            off += count(cg)
        return out

    def stages(self, steps):
        merged = []
        for n, cg in enumerate(self.cargos):
            for at, act in cg.stages(steps):
                def part(src, dst, sems, n=n, act=act):
                    act(self._parts(src, lambda g: len(g.inputs))[n], self._parts(dst, lambda g: len(g.out_shape))[n],
                        self._parts(sems, lambda g: len(g.sems))[n])
                merged.append((at, part))
        starts = [m for m in merged if m[0] == 0]
        return starts + sorted([m for m in merged if m[0] != 0], key=lambda m: m[0])

    def results(self, outs):
        return [cg.results(part) for cg, part in zip(self.cargos, self._parts(list(outs), lambda g: len(g.out_shape)))]


def _cargo_call(body, cargo, *, name, grid, in_specs, out_specs, out_shape, scratch=()):
    n_in, n_out, n_scr = len(in_specs), len(out_specs), len(scratch)
    c_in, c_out = len(cargo.inputs), len(cargo.out_shape)
    steps = 1
    for g in grid:
        steps *= g
    stages = cargo.stages(steps)

    def wrapped(*refs):
        ins, refs = refs[:n_in], refs[n_in:]
        cin, refs = refs[:c_in], refs[c_in:]
        outs, refs = refs[:n_out], refs[n_out:]
        cout, refs = refs[:c_out], refs[c_out:]
        scr, sems = refs[:n_scr], refs[n_scr:]
        if not grid:
            for _, act in stages:
                act(cin, cout, sems)
            return
        step = 0
        for d in range(len(grid)):
            step = step * grid[d] + pl.program_id(d)
        for at, act in stages:
            if at == 0:
                pl.when(step == 0)(functools.partial(act, cin, cout, sems))
        body(*ins, *outs, *scr)
        for at, act in stages:
            if at > 0:
                pl.when(step == at)(functools.partial(act, cin, cout, sems))

    params = dict(vmem_limit_bytes=VMEM_LIMIT)
    if grid:
        params["dimension_semantics"] = ("arbitrary",) * len(grid)
    res = pl.pallas_call(
        wrapped, name=name, grid=grid, in_specs=list(in_specs) + [ANY] * c_in, out_specs=list(out_specs) + [ANY] * c_out,
        out_shape=list(out_shape) + cargo.out_shape, scratch_shapes=list(scratch) + cargo.sems,
        compiler_params=pltpu.CompilerParams(**params))
    return lambda *args: (lambda r: (r[:n_out], cargo.results(r[n_out:])))(res(*args, *cargo.inputs))


def _split_halves(g):
    return g.reshape(g.shape[0], 2, g.shape[1] // 2, g.shape[2])


def _ew_tile(rows):
    return rows if rows <= 384 else 256


def _batched_call(items, name, place=None):
    offs, total = [], 0
    for it in items:
        offs.append(total)
        total += it[0]
    in_specs, out_specs, out_shape, args, spans = [], [], [], [], []
    for (n, ins, outs, shapes, operands, _), off in zip(items, offs):
        tile = lambda t, off=off, n=n: jnp.clip(t - off, 0, n - 1)
        i_s, o_s = ins(tile), outs(tile)
        spans.append((len(i_s), len(o_s)))
        in_specs += i_s
        out_specs += o_s
        out_shape += shapes
        args += operands
    n_in = len(in_specs)

    def body(*refs):
        if place is not None:
            refs = refs[1:]
        t = pl.program_id(0)
        i0, o0 = 0, n_in
        for (n, _, _, _, _, compute), off, (ni, no) in zip(items, offs, spans):
            pl.when((t >= off) & (t < off + n))(functools.partial(compute, refs[i0:i0 + ni], refs[o0:o0 + no]))
            i0, o0 = i0 + ni, o0 + no

    params = pltpu.CompilerParams(dimension_semantics=("arbitrary",), vmem_limit_bytes=VMEM_LIMIT)
    if place is None:
        res = pl.pallas_call(body, name=name, grid=(total,), in_specs=in_specs, out_specs=out_specs,
                             out_shape=out_shape, compiler_params=params)(*args)
    else:
        res = pl.pallas_call(
            body, name=name, out_shape=out_shape, compiler_params=params,
            grid_spec=pltpu.PrefetchScalarGridSpec(num_scalar_prefetch=1, grid=(total,), in_specs=in_specs,
                                                   out_specs=out_specs))(place, *args)
    out, o0 = [], 0
    for _, no in spans:
        out.append(res[o0:o0 + no])
        o0 += no
    return out


def _add_pairs(pairs, place, name):
    def item(mine, theirs):
        lead, _, rows, cols = mine.shape
        tm = _ew_tile(rows)

        def compute(ins, outs):
            outs[0][...] = (ins[0][...].astype(f32) + ins[1][...].astype(f32)).astype(outs[0].dtype)

        blk = lambda tile: pl.BlockSpec((lead, tm, cols), lambda t, pc: (0, tile(t), 0))
        mine_half = lambda tile: pl.BlockSpec((lead, None, tm, cols), lambda t, pc: (0, pc[1], tile(t), 0))
        return (rows // tm, lambda tile: [mine_half(tile), blk(tile)], lambda tile: [blk(tile)],
                [jax.ShapeDtypeStruct(theirs.shape, theirs.dtype)], [mine, theirs], compute)

    return [out[0] for out in _batched_call([item(a, b) for a, b in pairs], name, place)]


class _NoCargo:
    def __init__(self, results=()):
        self.inputs, self.out_shape, self.sems, self._results = [], [], [], list(results)

    def stages(self, steps):
        return [(0, lambda src, dst, sems: None), (steps - 1, lambda src, dst, sems: None)]

    def results(self, outs):
        return self._results


def _cargo_alone(cargo, name):
    return _cargo_call(None, cargo, name=name, grid=(), in_specs=[], out_specs=[], out_shape=[])()[1]


def _sum_slots(jobs, place, name):
    def item(own, recv, whole):
        _, rows, cols = own.shape
        tm = _ew_tile(rows)

        def compute(ins, outs):
            r0, r1, r2 = [ins[1][k].astype(f32) for k in range(N_CHIPS - 1)]
            outs[0][...] = (ins[0][...].astype(f32) + r0) + (r1 + r2)

        ins = lambda tile: [pl.BlockSpec((None, tm, cols), lambda t, pc: (0 if whole else pc[0], tile(t), 0)),
                            pl.BlockSpec((N_CHIPS - 1, tm, cols), lambda t, pc: (0, tile(t), 0))]
        outs = lambda tile: [pl.BlockSpec((None, tm, cols), lambda t, pc: (pc[1], tile(t), 0))]
        return rows // tm, ins, outs, [jax.ShapeDtypeStruct((2, rows, cols), f32)], [own, recv], compute

    return [out[0] for out in _batched_call([item(*job) for job in jobs], name, place)]


def _sibling_fill(bufs):
    n = len(bufs)

    def body(*refs):
        src, dst = refs[:n], refs[n:2 * n]
        send_sems, recv_sems = refs[2 * n:]
        c = lax.axis_index("c")
        remote = [pltpu.make_async_remote_copy(src_ref=src[a].at[c], dst_ref=dst[a].at[c], send_sem=send_sems.at[a],
                                               recv_sem=recv_sems.at[a], device_id=_sibling(), device_id_type=MESH)
                  for a in range(n)]
        for cp in remote:
            cp.start()
        for a in range(n):
            pltpu.make_async_remote_copy(src_ref=src[a].at[c], dst_ref=dst[a].at[1 - c], send_sem=send_sems.at[a],
                                         recv_sem=recv_sems.at[a], device_id=_sibling(), device_id_type=MESH).wait_recv()
        for cp in remote:
            cp.wait_send()

    sem = pltpu.SemaphoreType.DMA((n,))
    out = pl.pallas_call(
        body, name="sibling_fill", in_specs=[ANY] * n, out_specs=[ANY] * n,
        out_shape=[jax.ShapeDtypeStruct(b.shape, b.dtype) for b in bufs],
        input_output_aliases={a: a for a in range(n)}, scratch_shapes=[sem, sem])(*bufs)
    return [o.reshape(2 * b.shape[1], b.shape[2]) for o, b in zip(out, bufs)]


class _GradReducer:
    def __init__(self, place):
        self.place, self.split, self.sums, self.slots = place, {}, {}, {}

    def swap(self, named):
        self.split.update({nm: _split_halves(g) for nm, g in named})
        return _SwapCargo([self.split[nm] for nm, _ in named])

    def summed(self, names, got, tag):
        sums = _add_pairs([(self.split[nm], g) for nm, g in zip(names, got)], self.place, "presum_" + tag)
        self.sums.update(zip(names, sums))

    def exchange(self, names):
        return _ExchangeCargo([self.sums[nm] for nm in names], [nm == "small" for nm in names])

    def arrived(self, names, slots):
        self.slots.update(zip(names, slots))

    def reduce(self, names):
        jobs = [(self.sums[nm], self.slots[nm], nm == "small") for nm in names]
        return _sibling_fill(_sum_slots(jobs, self.place, "sum_slots"))


def _adamw(jobs, name):
    def item(w, m, v, grad, tm):
        rows, cols = w.shape
        gcols = grad.shape[1]

        def compute(ins, outs):
            w_, m_, v_ = ins[0][...], ins[1][...], ins[2][...]
            g = ins[3][0:tm, 0:cols]
            m_ = ADAM_B1 * m_ + (1.0 - ADAM_B1) * g
            v_ = ADAM_B2 * v_ + (1.0 - ADAM_B2) * jnp.square(g)
            m_hat = m_ / (1.0 - ADAM_B1 ** ADAM_STEP)
            v_hat = v_ / (1.0 - ADAM_B2 ** ADAM_STEP)
            outs[0][...] = g
            outs[1][...] = -ADAM_LR * (m_hat / (jnp.sqrt(v_hat) + ADAM_EPS) + ADAM_WD * w_)
            outs[2][...] = m_
            outs[3][...] = v_

        blk = lambda tile: pl.BlockSpec((tm, cols), lambda t: (tile(t), 0))
        if tm == rows:
            gspec = lambda tile: pl.BlockSpec(grad.shape, lambda t: (0, 0))
        else:
            gspec = lambda tile: pl.BlockSpec((tm, gcols), lambda t: (tile(t), 0))
        return (rows // tm, lambda tile: [blk(tile)] * 3 + [gspec(tile)], lambda tile: [blk(tile)] * 4,
                [jax.ShapeDtypeStruct((rows, cols), f32)] * 4, [w, m, v, grad], compute)

    return _batched_call([item(*job) for job in jobs], name)


def _rope_tables(s):
    half = HEAD_DIM // 2
    inv_freq = 1.0 / (10000.0 ** (jnp.arange(half, dtype=f32) / half))
    ang = jnp.arange(s, dtype=f32)[:, None] * inv_freq[None, :]
    cos, sin = jnp.cos(ang), jnp.sin(ang)
    return jnp.tile(cos, (1, LANES // half)), jnp.tile(jnp.concatenate([-sin, sin], axis=1), (1, LANES // HEAD_DIM))


def _local_step(x, target, win_g, gather, comm, b_gate, w_pool, pool_scale, ln1_g, ln1_b, convb_g, ln2_g, ln2_b):
    s = x.shape[0]
    nb = s // MOBA_BLOCK
    cos_t, sin_t = _rope_tables(s)
    x_bf = x.astype(bf16)

    z, (wba_g, wbp_g, wout_g) = _inproj(x_bf, win_g, cos_t, sin_t, b_gate, gather["inproj"])
    wba = wba_g.transpose(1, 0, 2).reshape(D_ATTN, D_MODEL)
    wbp = wbp_g.transpose(1, 0, 2).reshape(D_POOL, D_MODEL)
    wout = wout_g.reshape(D_MODEL, D_MODEL)
    (o_bf, lse, pooled_bf, mixed, pm_bf), (wg_g, wu_g, convw_g) = _attn_fwd(z, nb, w_pool, pool_scale,
                                                                           gather["attn_fwd"])
    m_bf = _branch_merge(o_bf, pm_bf, z, wba, wbp)
    x1_bf, xhat1, rstd1 = _out_ln1(m_bf, wout, x, ln1_g, ln1_b)
    (a, up, ac, hh_bf), (wd_g,) = _ffn_up(x1_bf, wg_g, wu_g, convw_g, convb_g, gather["ffn_up"])
    wd = wd_g.reshape(D_FF_PAD, D_MODEL)
    loss, dh2, dh2_bf, d_ln2_g, d_ln2_b = _ffn_down_ln2_loss(hh_bf, wd, xhat1, ln1_g, ln1_b, ln2_g, ln2_b, target)

    da_bf, dup_bf, dconv = _ffn_act_bwd(dh2_bf, wd_g, ac, a, up, convw_g)
    ffn_shape = (N_CHIPS, FF_PAD, D_MODEL)
    d_wd = _matmul_tn(hh_bf, dh2_bf, 1, "dw_ffn_down").reshape(ffn_shape)
    d_wg, got = _matmul_tn(da_bf, x1_bf, 1, "dw_ffn_gate", cargo=comm.swap([("w_ffn_down", d_wd)]))
    comm.summed(["w_ffn_down"], got, "ffn_down")
    d_wg = d_wg.reshape(ffn_shape)
    d_wu = _matmul_tn(dup_bf, x1_bf, 1, "dw_ffn_up").reshape(ffn_shape)
    gate_up = ["w_ffn_gate", "w_ffn_up", "conv_w"]
    (dh1, dh1_bf, d_ln1_g, d_ln1_b), (slots, got) = _ffn_in_bwd_ln1(
        da_bf, dup_bf, wg_g, wu_g, dh2, xhat1, rstd1, ln1_g,
        _MultiCargo([comm.exchange(["w_ffn_down"]), comm.swap(list(zip(gate_up, [d_wg, d_wu, dconv])))]))
    comm.arrived(["w_ffn_down"], slots)
    comm.summed(gate_up, got, "ffn_gate_up")
    d_wout = _matmul_tn(m_bf, dh1_bf, 1, "dw_out").reshape(N_CHIPS, D_MODEL // N_CHIPS, D_MODEL)
    dzg_bf, dya_bf, dyp_bf, do_bf, dpm, d_bgate = _merge_bwd(dh1_bf, wout, o_bf, pm_bf, z, wba, wbp)
    d_wba = _matmul_tn(o_bf, dya_bf, N_CHIPS, "dw_branch_attn")
    d_wbp = _matmul_tn(pm_bf, dyp_bf, N_CHIPS, "dw_branch_pool")
    du_bf, d_wpool, d_pscale = _pool_bwd(dpm, mixed, pooled_bf, w_pool, pool_scale)
    small = _pack_small([d_bgate, d_wpool, d_pscale, d_ln1_g, d_ln1_b, d_ln2_g, d_ln2_b, dconv[:, 3, :],
                         loss[0:1, :]])[None]
    branch = ["w_out", "w_branch_attn", "w_branch_pool", "small"]
    (dq_bf, dk_bf, dv_bf), (slots, got) = _attn_bwd(
        z, o_bf, lse, do_bf, cos_t, sin_t, nb,
        _MultiCargo([comm.exchange(gate_up), comm.swap(list(zip(branch, [d_wout, d_wba, d_wbp, small])))]))
    comm.arrived(gate_up, slots)
    comm.summed(branch, got, "branch")
    dz_bf = jnp.concatenate([dq_bf, dk_bf, dv_bf, du_bf, dzg_bf], axis=1)
    d_win, slots = _matmul_tn(x_bf, dz_bf, N_CHIPS, "dw_in", cargo=comm.exchange(branch))
    comm.arrived(branch, slots)
    comm.summed(["w_in"], _cargo_alone(comm.swap([("w_in", d_win)]), "swap_w_in"), "w_in")
    (grad_x,), slots = _inproj_dx(dz_bf, win_g, dh1, comm.exchange(["w_in"]))
    comm.arrived(["w_in"], slots)
    return grad_x


SMALL_ROWS = 592


def _pack_small(parts):
    flat = jnp.concatenate([p.reshape(-1) for p in parts])
    return jnp.pad(flat, (0, SMALL_ROWS * LANES - flat.shape[0])).reshape(SMALL_ROWS, LANES)


def _unpack_small(packed, shapes):
    flat = packed.reshape(-1)
    out, off = [], 0
    for shp in shapes:
        n = 1
        for d in shp:
            n *= d
        out.append(flat[off:off + n].reshape(shp))
        off += n
    return out


def _pad_conv_b(cb):
    return jnp.pad(cb.reshape(N_CHIPS, FF_SHARD), ((0, 0), (0, FF_PAD - FF_SHARD)))


def kernel(x, w_in, b_gate, w_branch_attn, w_pool, pool_scale, w_branch_pool, w_out, ln1_g, ln1_b, w_ffn_gate, w_ffn_up, conv_w, conv_b, w_ffn_down, ln2_g, ln2_b, loss_target, m_w_in, m_b_gate, m_w_branch_attn, m_w_pool, m_pool_scale, m_w_branch_pool, m_w_out, m_ln1_g, m_ln1_b, m_w_ffn_gate, m_w_ffn_up, m_conv_w, m_conv_b, m_w_ffn_down, m_ln2_g, m_ln2_b, v_w_in, v_b_gate, v_w_branch_attn, v_w_pool, v_pool_scale, v_w_branch_pool, v_w_out, v_ln1_g, v_ln1_b, v_w_ffn_gate, v_w_ffn_up, v_conv_w, v_conv_b, v_w_ffn_down, v_ln2_g, v_ln2_b):
    pad_rows = lambda w: jnp.pad(w.astype(bf16), ((0, FF_PAD - FF_SHARD), (0, 0)))
    shards = [w_in[0].astype(bf16), w_branch_attn[0].astype(bf16), w_branch_pool[0].astype(bf16),
              w_out[0].astype(bf16), pad_rows(w_ffn_gate[0].T), pad_rows(w_ffn_up[0].T), pad_rows(w_ffn_down[0]),
              jnp.pad(conv_w[0], ((0, CONV_ROWS - 3), (0, FF_PAD - FF_SHARD)))]
    (win_g,) = _cargo_alone(_GatherCargo(shards[:1], [0.0]), "gather_w_in")
    gather = {"inproj": _GatherCargo(shards[1:4], [0.3, 0.5, 0.75]),
              "attn_fwd": _GatherCargo(shards[4:6] + shards[7:], [0.45, 0.88, 0.9]),
              "ffn_up": _GatherCargo(shards[6:7], [0.8])}

    place = jnp.stack([2 * lax.axis_index("x") + lax.axis_index("y"), lax.axis_index("c")]).astype(jnp.int32)
    comm = _GradReducer(place)
    convb_g = _pad_conv_b(conv_b).reshape(N_CHIPS, 1, FF_PAD)
    grad_x = _local_step(
        x[0], loss_target[0], win_g, gather, comm, b_gate, w_pool[0], pool_scale, ln1_g, ln1_b, convb_g, ln2_g, ln2_b)
    names = ["w_in", "w_branch_attn", "w_branch_pool", "w_out", "w_ffn_gate", "w_ffn_up", "w_ffn_down", "conv_w", "small"]
    grads = comm.reduce(names)

    small_names = ["b_gate", "w_pool", "pool_scale", "ln1_g", "ln1_b", "ln2_g", "ln2_b", "conv_b"]
    small_w = [b_gate, w_pool, pool_scale, ln1_g, ln1_b, ln2_g, ln2_b, _pad_conv_b(conv_b)]
    small_m = [m_b_gate, m_w_pool, m_pool_scale, m_ln1_g, m_ln1_b, m_ln2_g, m_ln2_b, _pad_conv_b(m_conv_b)]
    small_v = [v_b_gate, v_w_pool, v_pool_scale, v_ln1_g, v_ln1_b, v_ln2_g, v_ln2_b, _pad_conv_b(v_conv_b)]
    weights = [w_in[0], w_branch_attn[0], w_branch_pool[0], w_out[0], w_ffn_gate[0].T, w_ffn_up[0].T, w_ffn_down[0],
               conv_w[0], _pack_small(small_w)]
    m_in = [m_w_in[0], m_w_branch_attn[0], m_w_branch_pool[0], m_w_out[0], m_w_ffn_gate[0].T, m_w_ffn_up[0].T,
            m_w_ffn_down[0], m_conv_w[0], _pack_small(small_m)]
    v_in = [v_w_in[0], v_w_branch_attn[0], v_w_branch_pool[0], v_w_out[0], v_w_ffn_gate[0].T, v_w_ffn_up[0].T,
            v_w_ffn_down[0], v_conv_w[0], _pack_small(small_v)]
    tiles = [256, 256, 256, 128, 176, 176, 176, 3, SMALL_ROWS]
    jobs = list(zip(weights, m_in, v_in, grads, tiles))
    ffn = slice(4, 7)
    updated = _adamw(jobs[:4] + jobs[7:], "adamw_rest")
    updated = updated[:4] + _adamw(jobs[ffn], "adamw_ffn") + updated[4:]
    res = {nm: [(r.T if nm in ("w_ffn_gate", "w_ffn_up") else r)[None] for r in outs4]
           for nm, outs4 in zip(names[:-1], updated)}
    packed = updated[-1]
    loss = grads[-1][SMALL_ROWS - 4, 0]
    shapes = [w.shape for w in small_w]
    for kind in range(4):
        for nm, val in zip(small_names, _unpack_small(packed[kind], shapes)):
            if nm == "conv_b":
                val = val[:, :FF_SHARD].reshape(1, 4 * FF_SHARD)
            res.setdefault(nm, [None] * 4)[kind] = val

    order = ["w_in", "b_gate", "w_branch_attn", "w_pool", "pool_scale", "w_branch_pool", "w_out", "ln1_g", "ln1_b",
             "w_ffn_gate", "w_ffn_up", "conv_w", "conv_b", "w_ffn_down", "ln2_g", "ln2_b"]
    outs = [loss, grad_x[None]]
    for kind in range(4):
        outs += [res[nm][kind] for nm in order]
    return tuple(outs)
```

```python
import functools

import jax
import jax.numpy as jnp
from jax import lax
from jax.experimental import pallas as pl
from jax.experimental.pallas import tpu as pltpu

D_MODEL = 1024
HEAD_DIM = 64
D_ATTN = 512
D_POOL = 512
MOBA_BLOCK = 256
MOBA_TOPK = 3
POOL_GROUP = 128
MAX_WINDOW = 16
FF_SHARD = 704
FF_PAD = 768
D_FF_PAD = 4 * FF_PAD
N_CHIPS = 4
LANES = 128
ALPHA = (2.0 * 1) ** 0.25
LN_EPS = 1e-5
NEG = -1e30
SCALE = HEAD_DIM ** -0.5
ADAM_LR, ADAM_B1, ADAM_B2, ADAM_EPS, ADAM_WD, ADAM_STEP = 0.001, 0.9, 0.999, 1e-08, 0.01, 10
VMEM_LIMIT = 56 * 1024 * 1024
MESH = pl.DeviceIdType.MESH

bf16 = jnp.bfloat16
f32 = jnp.float32


def _dot(a, b):
    return jnp.dot(a, b, preferred_element_type=f32)


def _dot_nt(a, b):
    return lax.dot_general(a, b, (((1,), (1,)), ((), ())), preferred_element_type=f32)


def _dot_tn(a, b):
    return lax.dot_general(a, b, (((0,), (0,)), ((), ())), preferred_element_type=f32)


def _call(body, *, name, grid, in_specs, out_specs, out_shape, scratch=(), cargo=None):
    if cargo is not None:
        return _cargo_call(body, cargo, name=name, grid=grid, in_specs=in_specs, out_specs=out_specs,
                           out_shape=out_shape, scratch=scratch)
    return pl.pallas_call(
        body, name=name, grid=grid, in_specs=in_specs, out_specs=out_specs, out_shape=out_shape,
        scratch_shapes=list(scratch),
        compiler_params=pltpu.CompilerParams(dimension_semantics=("arbitrary",) * len(grid),
                                             vmem_limit_bytes=VMEM_LIMIT))


def _rot_half(t):
    lane = lax.broadcasted_iota(jnp.int32, t.shape, 1)
    first = (lane % HEAD_DIM) < (HEAD_DIM // 2)
    return jnp.where(first, pltpu.roll(t, LANES - HEAD_DIM // 2, 1), pltpu.roll(t, HEAD_DIM // 2, 1))


def _rope(t, cos, sin_signed):
    return t * cos + _rot_half(t) * sin_signed


def _rope_bwd(d, cos, sin_signed):
    return d * cos + _rot_half(d * sin_signed)


def _gelu_parts(a):
    cdf = 0.5 * (1.0 + lax.erf(a * (2.0 ** -0.5)))
    pdf = jnp.exp(-0.5 * a * a) * ((2.0 * jnp.pi) ** -0.5)
    return a * cdf, cdf + a * pdf


def _layer_norm(h, g, b):
    mu = jnp.mean(h, axis=-1, keepdims=True)
    xc = h - mu
    var = jnp.mean(xc * xc, axis=-1, keepdims=True)
    rstd = lax.rsqrt(var + LN_EPS)
    xhat = xc * rstd
    return xhat * g + b, xhat, rstd


def _layer_norm_bwd(dy, xhat, rstd, g):
    dxh = dy * g
    m1 = jnp.mean(dxh, axis=-1, keepdims=True)
    m2 = jnp.mean(dxh * xhat, axis=-1, keepdims=True)
    return rstd * (dxh - m1 - xhat * m2)


def _inproj(x_bf, win_g, cos_t, sin_t, b_gate, cargo):
    s = x_bf.shape[0]
    tm, tn = 1024, 512

    def body(x_ref, w_ref, cos_ref, sin_ref, b_ref, o_ref):
        j = pl.program_id(1)
        acc = _dot(x_ref[...], w_ref[...])

        @pl.when(j < 2)
        def _():
            for c in range(tn // LANES):
                sl = slice(c * LANES, (c + 1) * LANES)
                o_ref[:, sl] = _rope(acc[:, sl], cos_ref[...], sin_ref[...])

        @pl.when((j >= 2) & (j < 4))
        def _():
            o_ref[...] = acc

        @pl.when(j >= 4)
        def _():
            o_ref[...] = jax.nn.sigmoid(acc + b_ref[...])

    (z,), carried = _cargo_call(
        body, cargo, name="inproj", grid=(s // tm, 4 * D_MODEL // tn),
        in_specs=[pl.BlockSpec((tm, D_MODEL), lambda i, j: (i, 0)),
                  pl.BlockSpec((None, D_MODEL, tn), lambda i, j: (j // 2, 0, j % 2)),
                  pl.BlockSpec((tm, LANES), lambda i, j: (i, 0)),
                  pl.BlockSpec((tm, LANES), lambda i, j: (i, 0)),
                  pl.BlockSpec((1, tn), lambda i, j: (0, jnp.maximum(j - 4, 0)))],
        out_specs=[pl.BlockSpec((tm, tn), lambda i, j: (i, j))],
        out_shape=[jax.ShapeDtypeStruct((s, 4 * D_MODEL), f32)])(x_bf, win_g, cos_t, sin_t, b_gate)
    return z, carried


STAT_ROWS = 8


def _pair_rows(v0, v1, fill):
    r = lax.broadcasted_iota(jnp.int32, (STAT_ROWS, v0.shape[1]), 0)
    return jnp.where(r == 0, v0, jnp.where(r == 1, v1, fill))


def _head_lanes(shape):
    lane = lax.broadcasted_iota(jnp.int32, shape, 1)
    return lane < HEAD_DIM, lane >= HEAD_DIM


def _head_rows(shape):
    row = lax.broadcasted_iota(jnp.int32, shape, 0)
    return row < HEAD_DIM, row >= HEAD_DIM


def _moba_select(q_bf, k_all, i, nb):
    k_mean = jnp.mean(k_all.reshape(nb, MOBA_BLOCK, LANES), axis=1)
    n_io = lax.broadcasted_iota(jnp.int32, (nb, MOBA_BLOCK), 0)
    past = n_io < i
    sels = []
    for head in _head_lanes((nb, LANES)):
        gate = _dot_nt(jnp.where(head, k_mean, 0.0).astype(bf16), q_bf)
        g = jnp.where(past, gate, NEG)
        rank = jnp.zeros((nb, MOBA_BLOCK), f32)
        for m in range(nb):
            gm = g[m:m + 1, :]
            rank = rank + jnp.where((gm > g) | ((gm == g) & (m < n_io)), 1.0, 0.0)
        sels.append(jnp.where(past & (rank < MOBA_TOPK), 1.0, 0.0))
    return sels


def _attn_fwd(z, nb, w_pool, pool_scale, cargo):
    s = z.shape[0]

    def body(q_ref, k_ref, v_ref, u_ref, wp_ref, sc_ref, o_ref, lse_ref, pooled_ref, mixed_ref, pm_ref, vt_s, pad):
        pair, i = pl.program_id(0), pl.program_id(1)

        @pl.when(i == 0)
        def _():
            for j in range(nb):
                vt_s[j] = v_ref[j * MOBA_BLOCK:(j + 1) * MOBA_BLOCK, :].T.astype(bf16)
            _pool_group(pair, u_ref, wp_ref, sc_ref, pooled_ref, mixed_ref, pm_ref, pad)

        k_heads = _head_lanes((MOBA_BLOCK, LANES))
        o_heads = _head_rows((LANES, MOBA_BLOCK))

        def block(j, qs, allow, m_old, l_old):
            kj, vtj = k_ref[j * MOBA_BLOCK:(j + 1) * MOBA_BLOCK, :], vt_s[j]
            m_new, alpha, l_new, pv = [], [], [], None
            for h in range(2):
                sc = jnp.where(allow[h], _dot_nt(jnp.where(k_heads[h], kj, 0.0).astype(bf16), qs), NEG)
                mx = jnp.max(sc, axis=0, keepdims=True)
                mn = mx if m_old is None else jnp.maximum(m_old[h], mx)
                p = jnp.exp(sc - mn)
                lsum = jnp.sum(p, axis=0, keepdims=True)
                if m_old is None:
                    a = None
                else:
                    a = jnp.exp(m_old[h] - mn)
                    lsum = a * l_old[h] + lsum
                t = _dot(jnp.where(o_heads[h], vtj, jnp.zeros_like(vtj)), p.astype(bf16))
                pv = t if pv is None else pv + t
                m_new.append(mn)
                alpha.append(a)
                l_new.append(lsum)
            return m_new, alpha, l_new, pv

        def query_block(c):
            q = q_ref[...]
            sels = _moba_select(q.astype(bf16), k_ref[...], c, nb)
            qs = (q * SCALE).astype(bf16)
            key = lax.broadcasted_iota(jnp.int32, (MOBA_BLOCK, MOBA_BLOCK), 0)
            qry = lax.broadcasted_iota(jnp.int32, (MOBA_BLOCK, MOBA_BLOCK), 1)
            m, _, l, acc = block(c, qs, [key <= qry] * 2, None, None)
            for j in range(c):
                m, alpha, l, pv = block(j, qs, [sels[h][j:j + 1, :] > 0.0 for h in range(2)], m, l)
                acc = acc * jnp.where(o_heads[0], alpha[0], alpha[1]) + pv
            o_ref[...] = (acc / jnp.where(o_heads[0], l[0], l[1])).T.astype(bf16)
            lse_ref[...] = _pair_rows(m[0] + jnp.log(l[0]), m[1] + jnp.log(l[1]), 0.0)

        for c in range(nb):
            pl.when(i == c)(functools.partial(query_block, c))

    group = pl.BlockSpec((s, POOL_GROUP), lambda hp, i: (0, hp))
    return _cargo_call(
        body, cargo, name="attn_fwd", grid=(D_ATTN // LANES, nb),
        in_specs=[pl.BlockSpec((MOBA_BLOCK, LANES), lambda hp, i: (i, hp)),
                  pl.BlockSpec((s, LANES), lambda hp, i: (0, 4 + hp)),
                  pl.BlockSpec((s, LANES), lambda hp, i: (0, 8 + hp)),
                  pl.BlockSpec((s, POOL_GROUP), lambda hp, i: (0, 12 + hp)),
                  pl.BlockSpec((None, POOL_GROUP, POOL_GROUP), lambda hp, i: (hp, 0, 0)),
                  pl.BlockSpec((1, POOL_GROUP), lambda hp, i: (0, hp))],
        out_specs=[pl.BlockSpec((MOBA_BLOCK, LANES), lambda hp, i: (i, hp)),
                   pl.BlockSpec((None, STAT_ROWS, MOBA_BLOCK), lambda hp, i: (hp, 0, i)), group, group, group],
        out_shape=[jax.ShapeDtypeStruct((s, D_ATTN), bf16),
                   jax.ShapeDtypeStruct((D_ATTN // LANES, STAT_ROWS, s), f32),
                   jax.ShapeDtypeStruct((s, D_POOL), bf16), jax.ShapeDtypeStruct((s, D_POOL), f32),
                   jax.ShapeDtypeStruct((s, D_POOL), bf16)],
        scratch=[pltpu.VMEM((nb, LANES, MOBA_BLOCK), bf16), pltpu.VMEM((s + MAX_WINDOW, POOL_GROUP), f32)],
    )(z, z, z, z, w_pool, pool_scale)


def _window_select(snaps, g):
    return jnp.where(g == 0, snaps[0], jnp.where(g == 1, snaps[1], jnp.where(g == 2, snaps[2], snaps[3])))


def _pool_group(g, u_ref, w_ref, sc_ref, pooled_ref, mixed_ref, pm_ref, pad):
    s = u_ref.shape[0]
    u = u_ref[...]
    pad[0:MAX_WINDOW, :] = jnp.zeros((MAX_WINDOW, POOL_GROUP), f32)
    pad[MAX_WINDOW:MAX_WINDOW + s, :] = u
    acc = u
    snaps = []
    for d in range(1, MAX_WINDOW):
        acc = acc + pad[MAX_WINDOW - d:MAX_WINDOW - d + s, :]
        if d + 1 in (2, 4, 8, 16):
            snaps.append(acc)
    win = _window_select(snaps, g)
    t = lax.broadcasted_iota(jnp.int32, (s, POOL_GROUP), 0)
    count = jnp.minimum(t + 1, jnp.left_shift(2, g)).astype(f32)
    pooled = (win / count - u).astype(bf16)
    mixed = _dot(pooled, w_ref[...].astype(bf16))
    pooled_ref[...] = pooled
    mixed_ref[...] = mixed
    pm_ref[...] = (mixed * sc_ref[...]).astype(bf16)


def _branch_merge(o_bf, pm_bf, z, wba, wbp):
    s = o_bf.shape[0]
    tm = 512

    def body(o_ref, pm_ref, ga_ref, gp_ref, wba_ref, wbp_ref, m_ref):
        ya = _dot(o_ref[...], wba_ref[...])
        yp = _dot(pm_ref[...], wbp_ref[...])
        m_ref[...] = (ga_ref[...] * ya + gp_ref[...] * yp).astype(bf16)

    full = lambda r, c: pl.BlockSpec((r, c), lambda i: (0, 0))
    return _call(
        body, name="branch_merge", grid=(s // tm,),
        in_specs=[pl.BlockSpec((tm, D_ATTN), lambda i: (i, 0)), pl.BlockSpec((tm, D_POOL), lambda i: (i, 0)),
                  pl.BlockSpec((tm, D_MODEL), lambda i: (i, 2)), pl.BlockSpec((tm, D_MODEL), lambda i: (i, 3)),
                  full(D_ATTN, D_MODEL), full(D_POOL, D_MODEL)],
        out_specs=pl.BlockSpec((tm, D_MODEL), lambda i: (i, 0)),
        out_shape=jax.ShapeDtypeStruct((s, D_MODEL), bf16))(o_bf, pm_bf, z, z, wba, wbp)


def _out_ln1(m_bf, wout, x, ln_g, ln_b):
    s = x.shape[0]
    tm = 512

    def body(m_ref, w_ref, x_ref, g_ref, b_ref, x1_ref, xhat_ref, rstd_ref):
        h = ALPHA * x_ref[...] + _dot(m_ref[...], w_ref[...])
        y, xhat, rstd = _layer_norm(h, g_ref[...], b_ref[...])
        x1_ref[...] = y.astype(bf16)
        xhat_ref[...] = xhat
        rstd_ref[...] = jnp.broadcast_to(rstd, (tm, LANES))

    row = pl.BlockSpec((tm, D_MODEL), lambda i: (i, 0))
    vec = pl.BlockSpec((1, D_MODEL), lambda i: (0, 0))
    return _call(
        body, name="out_ln1", grid=(s // tm,),
        in_specs=[row, pl.BlockSpec((D_MODEL, D_MODEL), lambda i: (0, 0)), row, vec, vec],
        out_specs=[row, row, pl.BlockSpec((tm, LANES), lambda i: (i, 0))],
        out_shape=[jax.ShapeDtypeStruct((s, D_MODEL), bf16), jax.ShapeDtypeStruct((s, D_MODEL), f32),
                   jax.ShapeDtypeStruct((s, LANES), f32)])(m_bf, wout, x, ln_g, ln_b)


FF_TILE = 256
FF_TILES_PER_SHARD = FF_PAD // FF_TILE
CONV_PAD = 8
CONV_ROWS = 16

def _ff_weight_spec(rows):
    return pl.BlockSpec((None, rows, FF_TILE), lambda j: (j // FF_TILES_PER_SHARD, 0, j % FF_TILES_PER_SHARD))


def _ff_rows_spec():
    return pl.BlockSpec((None, FF_TILE, D_MODEL), lambda j: (j // FF_TILES_PER_SHARD, j % FF_TILES_PER_SHARD, 0))


def _row_shifts(x, shifts):
    s = x.shape[0]
    padded = jnp.concatenate([x, jnp.zeros((CONV_PAD, x.shape[1]), x.dtype)], axis=0)
    return [pltpu.roll(padded, d % (s + CONV_PAD), 0)[0:s] for d in shifts]


def _ffn_up(x1_bf, wg_g, wu_g, convw_g, convb_g, cargo):
    s = x1_bf.shape[0]

    def body(x_ref, wg_ref, wu_ref, cw_ref, cb_ref, a_ref, up_ref, ac_ref, hh_ref):
        x1 = x_ref[...]
        a = _dot_nt(x1, wg_ref[...])
        up = _dot_nt(x1, wu_ref[...])
        a_ref[...] = a
        up_ref[...] = up
        a1, a2 = _row_shifts(a, (1, 2))
        ac = a * cw_ref[2:3, :] + a1 * cw_ref[1:2, :] + a2 * cw_ref[0:1, :] + cb_ref[...]
        ac_ref[...] = ac
        hg, _ = _gelu_parts(ac)
        hh_ref[...] = (hg * up).astype(bf16)

    col = pl.BlockSpec((s, FF_TILE), lambda j: (0, j))
    wide = jax.ShapeDtypeStruct((s, D_FF_PAD), f32)
    return _call(
        body, name="ffn_up", grid=(D_FF_PAD // FF_TILE,),
        in_specs=[pl.BlockSpec((s, D_MODEL), lambda j: (0, 0)), _ff_rows_spec(), _ff_rows_spec(),
                  _ff_weight_spec(8), _ff_weight_spec(1)],
        out_specs=[col, col, col, col],
        out_shape=[wide, wide, wide, jax.ShapeDtypeStruct((s, D_FF_PAD), bf16)],
        cargo=cargo)(x1_bf, wg_g, wu_g, convw_g, convb_g)


def _ffn_down_ln2_loss(hh_bf, wd, xhat1, ln1_g, ln1_b, ln2_g, ln2_b, target):
    s = hh_bf.shape[0]
    tm = 256

    def body(hh_ref, w_ref, xh1_ref, g1_ref, b1_ref, g2_ref, b2_ref, t_ref, loss_ref, dh_ref, dhb_ref, dg_ref, db_ref):
        i = pl.program_id(0)
        x1 = xh1_ref[...] * g1_ref[...] + b1_ref[...]
        h = ALPHA * x1 + _dot(hh_ref[...], w_ref[...])
        y, xhat, rstd = _layer_norm(h, g2_ref[...], b2_ref[...])
        err = y - t_ref[...]
        part = 0.5 * jnp.sum(jnp.mean(err * err, axis=-1, keepdims=True), axis=0, keepdims=True)
        dy = err * (1.0 / D_MODEL)

        @pl.when(i == 0)
        def _():
            loss_ref[...] = jnp.zeros_like(loss_ref)
            dg_ref[...] = jnp.zeros_like(dg_ref)
            db_ref[...] = jnp.zeros_like(db_ref)

        loss_ref[...] += jnp.broadcast_to(part, loss_ref.shape)
        dg_ref[...] += jnp.sum(dy * xhat, axis=0, keepdims=True)
        db_ref[...] += jnp.sum(dy, axis=0, keepdims=True)
        dh = _layer_norm_bwd(dy, xhat, rstd, g2_ref[...])
        dh_ref[...] = dh
        dhb_ref[...] = dh.astype(bf16)

    row = pl.BlockSpec((tm, D_MODEL), lambda i: (i, 0))
    vec = pl.BlockSpec((1, D_MODEL), lambda i: (0, 0))
    return _call(
        body, name="ffn_down_ln2_loss", grid=(s // tm,),
        in_specs=[pl.BlockSpec((tm, D_FF_PAD), lambda i: (i, 0)),
                  pl.BlockSpec((D_FF_PAD, D_MODEL), lambda i: (0, 0)),
                  row, vec, vec, vec, vec, row],
        out_specs=[pl.BlockSpec((8, LANES), lambda i: (0, 0)), row, row, vec, vec],
        out_shape=[jax.ShapeDtypeStruct((8, LANES), f32), jax.ShapeDtypeStruct((s, D_MODEL), f32),
                   jax.ShapeDtypeStruct((s, D_MODEL), bf16), jax.ShapeDtypeStruct((1, D_MODEL), f32),
                   jax.ShapeDtypeStruct((1, D_MODEL), f32)])(hh_bf, wd, xhat1, ln1_g, ln1_b, ln2_g, ln2_b, target)


def _ffn_act_bwd(dh2_bf, wd_g, ac, a, up, convw_g):
    s = dh2_bf.shape[0]

    def body(dh_ref, wd_ref, ac_ref, a_ref, up_ref, cw_ref, da_ref, dup_ref, dc_ref):
        dhh = _dot_nt(dh_ref[...], wd_ref[...])
        hg, dgelu = _gelu_parts(ac_ref[...])
        dup_ref[...] = (dhh * hg).astype(bf16)
        dac = dhh * up_ref[...] * dgelu
        dac1, dac2 = _row_shifts(dac, (-1, -2))
        a = a_ref[...]
        rows = [jnp.sum(d * a, axis=0, keepdims=True) for d in (dac2, dac1, dac)]
        rows.append(jnp.sum(dac, axis=0, keepdims=True))
        rows.append(jnp.zeros((CONV_ROWS - 4, FF_TILE), f32))
        dc_ref[...] = jnp.concatenate(rows, axis=0)
        da_ref[...] = (dac * cw_ref[2:3, :] + dac1 * cw_ref[1:2, :] + dac2 * cw_ref[0:1, :]).astype(bf16)

    col = pl.BlockSpec((s, FF_TILE), lambda j: (0, j))
    return _call(
        body, name="ffn_act_bwd", grid=(D_FF_PAD // FF_TILE,),
        in_specs=[pl.BlockSpec((s, D_MODEL), lambda j: (0, 0)), _ff_rows_spec(), col, col, col, _ff_weight_spec(8)],
        out_specs=[col, col, _ff_weight_spec(CONV_ROWS)],
        out_shape=[jax.ShapeDtypeStruct((s, D_FF_PAD), bf16), jax.ShapeDtypeStruct((s, D_FF_PAD), bf16),
                   jax.ShapeDtypeStruct((N_CHIPS, CONV_ROWS, FF_PAD), f32)],
    )(dh2_bf, wd_g, ac, a, up, convw_g)


def _matmul_tn(a, b, n_shards, name, tm=512, tn=1024, cargo=None):
    k, m = a.shape
    n = b.shape[1]
    tm, tn = min(tm, m), min(tn, n // n_shards)
    per = n // n_shards // tn

    def body(a_ref, b_ref, o_ref, at_s):
        @pl.when(pl.program_id(1) == 0)
        def _():
            at_s[...] = a_ref[...].T

        o_ref[...] = _dot(at_s[...], b_ref[...]).astype(bf16)

    res = _call(
        body, name=name, grid=(m // tm, n // tn),
        in_specs=[pl.BlockSpec((k, tm), lambda i, j: (0, i)), pl.BlockSpec((k, tn), lambda i, j: (0, j))],
        out_specs=[pl.BlockSpec((None, tm, tn), lambda i, j: (j // per, i, j % per))],
        out_shape=[jax.ShapeDtypeStruct((n_shards, m, n // n_shards), bf16)], scratch=[pltpu.VMEM((tm, k), bf16)],
        cargo=cargo)(a, b)
    return res[0] if cargo is None else (res[0][0], res[1])


def _ffn_in_bwd_ln1(da_bf, dup_bf, wg_g, wu_g, dh2, xhat1, rstd1, ln1_g, cargo):
    s = da_bf.shape[0]
    tm = 256

    def body(da_ref, dup_ref, wg_ref, wu_ref, dh2_ref, xh_ref, rstd_ref, g_ref, dh_ref, dhb_ref, dg_ref, db_ref):
        i = pl.program_id(0)
        dx1 = ALPHA * dh2_ref[...]
        for sh in range(N_CHIPS):
            sl = slice(sh * FF_PAD, (sh + 1) * FF_PAD)
            dx1 = dx1 + _dot(da_ref[:, sl], wg_ref[sh]) + _dot(dup_ref[:, sl], wu_ref[sh])
        xhat = xh_ref[...]

        @pl.when(i == 0)
        def _():
            dg_ref[...] = jnp.zeros_like(dg_ref)
            db_ref[...] = jnp.zeros_like(db_ref)

        dg_ref[...] += jnp.sum(dx1 * xhat, axis=0, keepdims=True)
        db_ref[...] += jnp.sum(dx1, axis=0, keepdims=True)
        dh = _layer_norm_bwd(dx1, xhat, rstd_ref[:, 0:1], g_ref[...])
        dh_ref[...] = dh
        dhb_ref[...] = dh.astype(bf16)

    row = pl.BlockSpec((tm, D_MODEL), lambda i: (i, 0))
    wide = pl.BlockSpec((tm, D_FF_PAD), lambda i: (i, 0))
    wfull = pl.BlockSpec((N_CHIPS, FF_PAD, D_MODEL), lambda i: (0, 0, 0))
    vec = pl.BlockSpec((1, D_MODEL), lambda i: (0, 0))
    return _call(
        body, name="ffn_in_bwd_ln1", grid=(s // tm,),
        in_specs=[wide, wide, wfull, wfull, row, row, pl.BlockSpec((tm, LANES), lambda i: (i, 0)), vec],
        out_specs=[row, row, vec, vec],
        out_shape=[jax.ShapeDtypeStruct((s, D_MODEL), f32), jax.ShapeDtypeStruct((s, D_MODEL), bf16),
                   jax.ShapeDtypeStruct((1, D_MODEL), f32), jax.ShapeDtypeStruct((1, D_MODEL), f32)],
        cargo=cargo)(da_bf, dup_bf, wg_g, wu_g, dh2, xhat1, rstd1, ln1_g)


def _merge_bwd(dh1_bf, wout, o_bf, pm_bf, z, wba, wbp):
    s = dh1_bf.shape[0]
    tm = 256

    def body(dh_ref, wout_ref, o_ref, pm_ref, ga_ref, gp_ref, wba_ref, wbp_ref,
             dzg_ref, dya_ref, dyp_ref, do_ref, dpm_ref, dbg_ref):
        i = pl.program_id(0)
        dm = _dot_nt(dh_ref[...], wout_ref[...])
        ya = _dot(o_ref[...], wba_ref[...])
        yp = _dot(pm_ref[...], wbp_ref[...])
        ga, gp = ga_ref[...], gp_ref[...]
        dza = dm * ya * ga * (1.0 - ga)
        dzp = dm * yp * gp * (1.0 - gp)

        @pl.when(i == 0)
        def _():
            dbg_ref[...] = jnp.zeros_like(dbg_ref)

        dbg_ref[:, 0:D_MODEL] += jnp.sum(dza, axis=0, keepdims=True)
        dbg_ref[:, D_MODEL:2 * D_MODEL] += jnp.sum(dzp, axis=0, keepdims=True)
        dzg_ref[:, 0:D_MODEL] = dza.astype(bf16)
        dzg_ref[:, D_MODEL:2 * D_MODEL] = dzp.astype(bf16)
        dya = (dm * ga).astype(bf16)
        dyp = (dm * gp).astype(bf16)
        dya_ref[...] = dya
        dyp_ref[...] = dyp
        do_ref[...] = _dot_nt(dya, wba_ref[...]).astype(bf16)
        dpm_ref[...] = _dot_nt(dyp, wbp_ref[...])

    row = pl.BlockSpec((tm, D_MODEL), lambda i: (i, 0))
    half = pl.BlockSpec((tm, D_ATTN), lambda i: (i, 0))
    full = lambda r, c: pl.BlockSpec((r, c), lambda i: (0, 0))
    return _call(
        body, name="merge_bwd", grid=(s // tm,),
        in_specs=[row, full(D_MODEL, D_MODEL), half, half,
                  pl.BlockSpec((tm, D_MODEL), lambda i: (i, 2)), pl.BlockSpec((tm, D_MODEL), lambda i: (i, 3)),
                  full(D_ATTN, D_MODEL), full(D_POOL, D_MODEL)],
        out_specs=[pl.BlockSpec((tm, 2 * D_MODEL), lambda i: (i, 0)), row, row, half, half,
                   pl.BlockSpec((1, 2 * D_MODEL), lambda i: (0, 0))],
        out_shape=[jax.ShapeDtypeStruct((s, 2 * D_MODEL), bf16), jax.ShapeDtypeStruct((s, D_MODEL), bf16),
                   jax.ShapeDtypeStruct((s, D_MODEL), bf16), jax.ShapeDtypeStruct((s, D_ATTN), bf16),
                   jax.ShapeDtypeStruct((s, D_POOL), f32), jax.ShapeDtypeStruct((1, 2 * D_MODEL), f32)],
    )(dh1_bf, wout, o_bf, pm_bf, z, z, wba, wbp)


def _pool_bwd(dpm, mixed, pooled_bf, w_pool, pool_scale):
    s = dpm.shape[0]

    def body(dpm_ref, mixed_ref, pooled_ref, w_ref, sc_ref, du_ref, dw_ref, dsc_ref, pad):
        g = pl.program_id(0)
        dpm_v = dpm_ref[...]
        dsc_ref[...] = jnp.sum(dpm_v * mixed_ref[...], axis=0, keepdims=True)
        dmixed = (dpm_v * sc_ref[...]).astype(bf16)
        dw_ref[...] = _dot_tn(pooled_ref[...], dmixed)
        dpooled = _dot_nt(dmixed, w_ref[...].astype(bf16))
        t = lax.broadcasted_iota(jnp.int32, (s, POOL_GROUP), 0)
        count = jnp.minimum(t + 1, jnp.left_shift(2, g)).astype(f32)
        r = dpooled / count
        pad[0:s, :] = r
        pad[s:s + MAX_WINDOW, :] = jnp.zeros((MAX_WINDOW, POOL_GROUP), f32)
        acc = r
        snaps = []
        for d in range(1, MAX_WINDOW):
            acc = acc + pad[d:d + s, :]
            if d + 1 in (2, 4, 8, 16):
                snaps.append(acc)
        du_ref[...] = (_window_select(snaps, g) - dpooled).astype(bf16)

    blk = pl.BlockSpec((s, POOL_GROUP), lambda g: (0, g))
    return _call(
        body, name="pool_bwd", grid=(4,),
        in_specs=[blk, blk, blk, pl.BlockSpec((None, POOL_GROUP, POOL_GROUP), lambda g: (g, 0, 0)),
                  pl.BlockSpec((1, POOL_GROUP), lambda g: (0, g))],
        out_specs=[blk, pl.BlockSpec((None, POOL_GROUP, POOL_GROUP), lambda g: (g, 0, 0)),
                   pl.BlockSpec((1, POOL_GROUP), lambda g: (0, g))],
        out_shape=[jax.ShapeDtypeStruct((s, D_POOL), bf16), jax.ShapeDtypeStruct((4, POOL_GROUP, POOL_GROUP), f32),
                   jax.ShapeDtypeStruct((1, D_POOL), f32)],
        scratch=[pltpu.VMEM((s + MAX_WINDOW, POOL_GROUP), f32)])(dpm, mixed, pooled_bf, w_pool, pool_scale)


def _attn_bwd(z, o_bf, lse, do_bf, cos_t, sin_t, nb, cargo):
    s = z.shape[0]

    def body(q_ref, k_ref, v_ref, o_ref, lse_ref, do_ref, cq_ref, sq_ref, cf_ref, sf_ref,
             dq_ref, dk_ref, dv_ref, kt_s, dk_s, dv_s):
        i = pl.program_id(1)

        @pl.when(i == 0)
        def _():
            dk_s[...] = jnp.zeros_like(dk_s)
            dv_s[...] = jnp.zeros_like(dv_s)
            for j in range(nb):
                kt_s[j] = k_ref[j * MOBA_BLOCK:(j + 1) * MOBA_BLOCK, :].T.astype(bf16)

        k_heads = _head_lanes((MOBA_BLOCK, LANES))
        t_heads = _head_rows((LANES, MOBA_BLOCK))

        def query_block(c):
            q = q_ref[...]
            sels = _moba_select(q.astype(bf16), k_ref[...], c, nb)
            qs = (q * SCALE).astype(bf16)
            qraw = [jnp.where(hm, q, 0.0).astype(bf16) for hm in k_heads]
            do = do_ref[...].astype(f32)
            do_bf = do_ref[...]
            dob = [jnp.where(hm, do, 0.0).astype(bf16) for hm in k_heads]
            pick = _head_lanes((STAT_ROWS, LANES))
            r8 = lax.broadcasted_iota(jnp.int32, (STAT_ROWS, LANES), 0)
            head_pick = jnp.where(((r8 == 0) & pick[0]) | ((r8 == 1) & pick[1]), 1.0, 0.0)
            delta8 = lax.dot_general(head_pick, do * o_ref[...].astype(f32), (((1,), (1,)), ((), ())),
                                     precision=lax.Precision.HIGHEST, preferred_element_type=f32)
            lse8 = lse_ref[...]
            delta, lse_h = [delta8[0:1, :], delta8[1:2, :]], [lse8[0:1, :], lse8[1:2, :]]
            key = lax.broadcasted_iota(jnp.int32, (MOBA_BLOCK, MOBA_BLOCK), 0)
            qry = lax.broadcasted_iota(jnp.int32, (MOBA_BLOCK, MOBA_BLOCK), 1)

            def block(j, allow):
                rows = slice(j * MOBA_BLOCK, (j + 1) * MOBA_BLOCK)
                kj, vj, ktj = k_ref[rows, :], v_ref[rows, :], kt_s[j]
                dqt, dk, dv = None, None, None
                for h in range(2):
                    sc = _dot_nt(jnp.where(k_heads[h], kj, 0.0).astype(bf16), qs)
                    p = jnp.where(allow[h], jnp.exp(sc - lse_h[h]), 0.0)
                    dp = _dot_nt(jnp.where(k_heads[h], vj, 0.0).astype(bf16), do_bf)
                    ds = (p * (dp - delta[h]) * SCALE).astype(bf16)
                    tv = _dot(p.astype(bf16), dob[h])
                    tk = _dot(ds, qraw[h])
                    tq = _dot(jnp.where(t_heads[h], ktj, jnp.zeros_like(ktj)), ds)
                    dqt, dk, dv = (tq, tk, tv) if dqt is None else (dqt + tq, dk + tk, dv + tv)
                dk_s[rows, :] += dk
                dv_s[rows, :] += dv
                return dqt

            dqt = block(c, [key <= qry] * 2)
            for j in range(c):
                dqt = dqt + block(j, [sels[h][j:j + 1, :] > 0.0 for h in range(2)])
            dq_ref[...] = _rope_bwd(dqt.T, cq_ref[...], sq_ref[...]).astype(bf16)

        for c in range(nb):
            pl.when(i == c)(functools.partial(query_block, c))

        @pl.when(i == nb - 1)
        def _():
            dk_ref[...] = _rope_bwd(dk_s[...], cf_ref[...], sf_ref[...]).astype(bf16)
            dv_ref[...] = dv_s[...].astype(bf16)

    qblk = pl.BlockSpec((MOBA_BLOCK, LANES), lambda hp, i: (i, hp))
    tq = pl.BlockSpec((MOBA_BLOCK, LANES), lambda hp, i: (i, 0))
    tf = pl.BlockSpec((s, LANES), lambda hp, i: (0, 0))
    colblk = pl.BlockSpec((s, LANES), lambda hp, i: (0, hp))
    return _cargo_call(
        body, cargo, name="attn_bwd", grid=(D_ATTN // LANES, nb),
        in_specs=[qblk, pl.BlockSpec((s, LANES), lambda hp, i: (0, 4 + hp)),
                  pl.BlockSpec((s, LANES), lambda hp, i: (0, 8 + hp)), qblk,
                  pl.BlockSpec((None, STAT_ROWS, MOBA_BLOCK), lambda hp, i: (hp, 0, i)), qblk, tq, tq, tf, tf],
        out_specs=[qblk, colblk, colblk],
        out_shape=[jax.ShapeDtypeStruct((s, D_ATTN), bf16)] * 3,
        scratch=[pltpu.VMEM((nb, LANES, MOBA_BLOCK), bf16), pltpu.VMEM((s, LANES), f32), pltpu.VMEM((s, LANES), f32)],
    )(z, z, z, o_bf, lse, do_bf, cos_t, sin_t, cos_t, sin_t)


def _inproj_dx(dz_bf, win_g, dh1, cargo):
    s = dz_bf.shape[0]
    tm = 256

    def body(dz_ref, w_ref, dh_ref, gx_ref):
        acc = ALPHA * dh_ref[...]
        for sh in range(N_CHIPS):
            acc = acc + _dot_nt(dz_ref[:, sh * D_MODEL:(sh + 1) * D_MODEL], w_ref[sh])
        gx_ref[...] = acc

    row = pl.BlockSpec((tm, D_MODEL), lambda i: (i, 0))
    return _call(
        body, name="inproj_dx", grid=(s // tm,),
        in_specs=[pl.BlockSpec((tm, 4 * D_MODEL), lambda i: (i, 0)),
                  pl.BlockSpec((N_CHIPS, D_MODEL, D_MODEL), lambda i: (0, 0, 0)), row],
        out_specs=[row], out_shape=[jax.ShapeDtypeStruct((s, D_MODEL), f32)], cargo=cargo)(dz_bf, win_g, dh1)


ANY = pl.BlockSpec(memory_space=pl.ANY)


def _chip_index():
    return 2 * lax.axis_index("x") + lax.axis_index("y")


def _peer(k):
    x, y, c = lax.axis_index("x"), lax.axis_index("y"), lax.axis_index("c")
    return (x ^ (k >> 1), y ^ (k & 1), c)


def _sibling():
    return (lax.axis_index("x"), lax.axis_index("y"), 1 - lax.axis_index("c"))


class _GatherCargo:
    def __init__(self, shards, pass_on):
        self.shards, self.pass_on, n = shards, pass_on, len(shards)
        self.inputs = [a.reshape(2, a.shape[0] // 2, a.shape[1]) for a in shards]
        self.out_shape = [jax.ShapeDtypeStruct((N_CHIPS,) + a.shape, a.dtype) for a in self.inputs]
        sem, one = pltpu.SemaphoreType.DMA((N_CHIPS - 1, n)), pltpu.SemaphoreType.DMA((n,))
        self.sems = [sem, sem, sem, sem, one, one]

    def _copies(self, src, dst, sems, a):
        send_sems, recv_sems, fsend_sems, frecv_sems, local_sems, own_sems = sems
        p, c = _chip_index(), lax.axis_index("c")
        own = pltpu.make_async_remote_copy(src_ref=src[a], dst_ref=dst[a].at[p], send_sem=local_sems.at[a],
                                           recv_sem=own_sems.at[a], device_id=_sibling(), device_id_type=MESH)
        out, arrive, onward, landed = [], [], [], []
        for k in range(1, N_CHIPS):
            ici = dict(send_sem=send_sems.at[k - 1, a], recv_sem=recv_sems.at[k - 1, a], device_id=_peer(k),
                       device_id_type=MESH)
            d2d = dict(send_sem=fsend_sems.at[k - 1, a], recv_sem=frecv_sems.at[k - 1, a], device_id=_sibling(),
                       device_id_type=MESH)
            got, theirs = dst[a].at[p ^ k, c], dst[a].at[p ^ k, 1 - c]
            out.append(pltpu.make_async_remote_copy(src_ref=src[a].at[c], dst_ref=dst[a].at[p, c], **ici))
            arrive.append(pltpu.make_async_remote_copy(src_ref=src[a].at[c], dst_ref=got, **ici))
            onward.append(pltpu.make_async_remote_copy(src_ref=got, dst_ref=got, **d2d))
            landed.append(pltpu.make_async_remote_copy(src_ref=theirs, dst_ref=theirs, **d2d))
        return own, out, arrive, onward, landed

    def stages(self, steps):
        n = len(self.shards)

        def start(src, dst, sems):
            for a in range(n):
                own, out, _, _, _ = self._copies(src, dst, sems, a)
                own.start()
                for cp in out:
                    cp.start()

        def pass_on(a):
            def act(src, dst, sems):
                _, _, arrive, onward, _ = self._copies(src, dst, sems, a)
                for k in range(N_CHIPS - 1):
                    arrive[k].wait_recv()
                    onward[k].start()
            return act

        def finish(src, dst, sems):
            for a in range(n):
                own, out, _, onward, landed = self._copies(src, dst, sems, a)
                for cp in landed:
                    cp.wait_recv()
                for cp in out + onward:
                    cp.wait_send()
                own.wait()

        mids = [(min(steps - 1, int(self.pass_on[a] * steps)), pass_on(a)) for a in range(n)]
        return [(0, start)] + mids + [(steps - 1, finish)]

    def results(self, outs):
        return [o.reshape((N_CHIPS,) + a.shape) for o, a in zip(outs, self.shards)]


class _ExchangeCargo:
    def __init__(self, sums, whole):
        self.inputs, self.whole, n = sums, whole, len(sums)
        self.out_shape = [jax.ShapeDtypeStruct((N_CHIPS - 1,) + g.shape[1:], g.dtype) for g in sums]
        sem = pltpu.SemaphoreType.DMA((N_CHIPS - 1, n))
        self.sems = [sem, sem]

    def _copies(self, src, dst, sems):
        send_sems, recv_sems = sems
        p = _chip_index()
        return [pltpu.make_async_remote_copy(
            src_ref=src[a].at[0] if self.whole[a] else src[a].at[p ^ k], dst_ref=dst[a].at[k - 1],
            send_sem=send_sems.at[k - 1, a], recv_sem=recv_sems.at[k - 1, a], device_id=_peer(k), device_id_type=MESH)
            for k in range(1, N_CHIPS) for a in range(len(self.inputs))]

    def stages(self, steps):
        def start(src, dst, sems):
            for cp in self._copies(src, dst, sems):
                cp.start()

        def finish(src, dst, sems):
            copies = self._copies(src, dst, sems)
            for cp in copies:
                cp.wait_recv()
            for cp in copies:
                cp.wait_send()

        return [(0, start), (steps - 1, finish)]

    def results(self, outs):
        return list(outs)


class _SwapCargo:
    def __init__(self, split):
        self.inputs, n = split, len(split)
        self.out_shape = [jax.ShapeDtypeStruct((g.shape[0],) + g.shape[2:], g.dtype) for g in split]
        sem = pltpu.SemaphoreType.DMA((n,))
        self.sems = [sem, sem]

    def _copies(self, src, dst, sems):
        c = lax.axis_index("c")
        return [pltpu.make_async_remote_copy(src_ref=src[a].at[:, 1 - c], dst_ref=dst[a], send_sem=sems[0].at[a],
                                             recv_sem=sems[1].at[a], device_id=_sibling(), device_id_type=MESH)
                for a in range(len(self.inputs))]

    def stages(self, steps):
        def start(src, dst, sems):
            for cp in self._copies(src, dst, sems):
                cp.start()

        def finish(src, dst, sems):
            copies = self._copies(src, dst, sems)
            for cp in copies:
                cp.wait_recv()
            for cp in copies:
                cp.wait_send()

        return [(0, start), (steps - 1, finish)]

    def results(self, outs):
        return list(outs)


class _MultiCargo:
    def __init__(self, cargos):
        self.cargos = cargos
        self.inputs = [a for cg in cargos for a in cg.inputs]
        self.out_shape = [o for cg in cargos for o in cg.out_shape]
        self.sems = [s for cg in cargos for s in cg.sems]

    def _parts(self, refs, count):
        out, off = [], 0
        for cg in self.cargos:
            out.append(refs[off:off + count(cg)])
            off += count(cg)
        return out

    def stages(self, steps):
        merged = []
        for n, cg in enumerate(self.cargos):
            for at, act in cg.stages(steps):
                def part(src, dst, sems, n=n, act=act):
                    act(self._parts(src, lambda g: len(g.inputs))[n], self._parts(dst, lambda g: len(g.out_shape))[n],
                        self._parts(sems, lambda g: len(g.sems))[n])
                merged.append((at, part))
        starts = [m for m in merged if m[0] == 0]
        return starts + sorted([m for m in merged if m[0] != 0], key=lambda m: m[0])

    def results(self, outs):
        return [cg.results(part) for cg, part in zip(self.cargos, self._parts(list(outs), lambda g: len(g.out_shape)))]


def _cargo_call(body, cargo, *, name, grid, in_specs, out_specs, out_shape, scratch=()):
    n_in, n_out, n_scr = len(in_specs), len(out_specs), len(scratch)
    c_in, c_out = len(cargo.inputs), len(cargo.out_shape)
    steps = 1
    for g in grid:
        steps *= g
    stages = cargo.stages(steps)

    def wrapped(*refs):
        ins, refs = refs[:n_in], refs[n_in:]
        cin, refs = refs[:c_in], refs[c_in:]
        outs, refs = refs[:n_out], refs[n_out:]
        cout, refs = refs[:c_out], refs[c_out:]
        scr, sems = refs[:n_scr], refs[n_scr:]
        if not grid:
            for _, act in stages:
                act(cin, cout, sems)
            return
        step = 0
        for d in range(len(grid)):
            step = step * grid[d] + pl.program_id(d)
        for at, act in stages:
            if at == 0:
                pl.when(step == 0)(functools.partial(act, cin, cout, sems))
        body(*ins, *outs, *scr)
        for at, act in stages:
            if at > 0:
                pl.when(step == at)(functools.partial(act, cin, cout, sems))

    params = dict(vmem_limit_bytes=VMEM_LIMIT)
    if grid:
        params["dimension_semantics"] = ("arbitrary",) * len(grid)
    res = pl.pallas_call(
        wrapped, name=name, grid=grid, in_specs=list(in_specs) + [ANY] * c_in, out_specs=list(out_specs) + [ANY] * c_out,
        out_shape=list(out_shape) + cargo.out_shape, scratch_shapes=list(scratch) + cargo.sems,
        compiler_params=pltpu.CompilerParams(**params))
    return lambda *args: (lambda r: (r[:n_out], cargo.results(r[n_out:])))(res(*args, *cargo.inputs))


def _split_halves(g):
    return g.reshape(g.shape[0], 2, g.shape[1] // 2, g.shape[2])


def _ew_tile(rows):
    return rows if rows <= 384 else 256


def _batched_call(items, name, place=None):
    offs, total = [], 0
    for it in items:
        offs.append(total)
        total += it[0]
    in_specs, out_specs, out_shape, args, spans = [], [], [], [], []
    for (n, ins, outs, shapes, operands, _), off in zip(items, offs):
        tile = lambda t, off=off, n=n: jnp.clip(t - off, 0, n - 1)
        i_s, o_s = ins(tile), outs(tile)
        spans.append((len(i_s), len(o_s)))
        in_specs += i_s
        out_specs += o_s
        out_shape += shapes
        args += operands
    n_in = len(in_specs)

    def body(*refs):
        if place is not None:
            refs = refs[1:]
        t = pl.program_id(0)
        i0, o0 = 0, n_in
        for (n, _, _, _, _, compute), off, (ni, no) in zip(items, offs, spans):
            pl.when((t >= off) & (t < off + n))(functools.partial(compute, refs[i0:i0 + ni], refs[o0:o0 + no]))
            i0, o0 = i0 + ni, o0 + no

    params = pltpu.CompilerParams(dimension_semantics=("arbitrary",), vmem_limit_bytes=VMEM_LIMIT)
    if place is None:
        res = pl.pallas_call(body, name=name, grid=(total,), in_specs=in_specs, out_specs=out_specs,
                             out_shape=out_shape, compiler_params=params)(*args)
    else:
        res = pl.pallas_call(
            body, name=name, out_shape=out_shape, compiler_params=params,
            grid_spec=pltpu.PrefetchScalarGridSpec(num_scalar_prefetch=1, grid=(total,), in_specs=in_specs,
                                                   out_specs=out_specs))(place, *args)
    out, o0 = [], 0
    for _, no in spans:
        out.append(res[o0:o0 + no])
        o0 += no
    return out


def _add_pairs(pairs, place, name):
    def item(mine, theirs):
        lead, _, rows, cols = mine.shape
        tm = _ew_tile(rows)

        def compute(ins, outs):
            outs[0][...] = (ins[0][...].astype(f32) + ins[1][...].astype(f32)).astype(outs[0].dtype)

        blk = lambda tile: pl.BlockSpec((lead, tm, cols), lambda t, pc: (0, tile(t), 0))
        mine_half = lambda tile: pl.BlockSpec((lead, None, tm, cols), lambda t, pc: (0, pc[1], tile(t), 0))
        return (rows // tm, lambda tile: [mine_half(tile), blk(tile)], lambda tile: [blk(tile)],
                [jax.ShapeDtypeStruct(theirs.shape, theirs.dtype)], [mine, theirs], compute)

    return [out[0] for out in _batched_call([item(a, b) for a, b in pairs], name, place)]


class _NoCargo:
    def __init__(self, results=()):
        self.inputs, self.out_shape, self.sems, self._results = [], [], [], list(results)

    def stages(self, steps):
        return [(0, lambda src, dst, sems: None), (steps - 1, lambda src, dst, sems: None)]

    def results(self, outs):
        return self._results


def _cargo_alone(cargo, name):
    return _cargo_call(None, cargo, name=name, grid=(), in_specs=[], out_specs=[], out_shape=[])()[1]


def _sum_slots(jobs, place, name):
    def item(own, recv, whole):
        _, rows, cols = own.shape
        tm = _ew_tile(rows)

        def compute(ins, outs):
            r0, r1, r2 = [ins[1][k].astype(f32) for k in range(N_CHIPS - 1)]
            outs[0][...] = (ins[0][...].astype(f32) + r0) + (r1 + r2)

        ins = lambda tile: [pl.BlockSpec((None, tm, cols), lambda t, pc: (0 if whole else pc[0], tile(t), 0)),
                            pl.BlockSpec((N_CHIPS - 1, tm, cols), lambda t, pc: (0, tile(t), 0))]
        outs = lambda tile: [pl.BlockSpec((None, tm, cols), lambda t, pc: (pc[1], tile(t), 0))]
        return rows // tm, ins, outs, [jax.ShapeDtypeStruct((2, rows, cols), f32)], [own, recv], compute

    return [out[0] for out in _batched_call([item(*job) for job in jobs], name, place)]


def _sibling_fill(bufs):
    n = len(bufs)

    def body(*refs):
        src, dst = refs[:n], refs[n:2 * n]
        send_sems, recv_sems = refs[2 * n:]
        c = lax.axis_index("c")
        remote = [pltpu.make_async_remote_copy(src_ref=src[a].at[c], dst_ref=dst[a].at[c], send_sem=send_sems.at[a],
                                               recv_sem=recv_sems.at[a], device_id=_sibling(), device_id_type=MESH)
                  for a in range(n)]
        for cp in remote:
            cp.start()
        for a in range(n):
            pltpu.make_async_remote_copy(src_ref=src[a].at[c], dst_ref=dst[a].at[1 - c], send_sem=send_sems.at[a],
                                         recv_sem=recv_sems.at[a], device_id=_sibling(), device_id_type=MESH).wait_recv()
        for cp in remote:
            cp.wait_send()

    sem = pltpu.SemaphoreType.DMA((n,))
    out = pl.pallas_call(
        body, name="sibling_fill", in_specs=[ANY] * n, out_specs=[ANY] * n,
        out_shape=[jax.ShapeDtypeStruct(b.shape, b.dtype) for b in bufs],
        input_output_aliases={a: a for a in range(n)}, scratch_shapes=[sem, sem])(*bufs)
    return [o.reshape(2 * b.shape[1], b.shape[2]) for o, b in zip(out, bufs)]


class _GradReducer:
    def __init__(self, place):
        self.place, self.split, self.sums, self.slots = place, {}, {}, {}

    def swap(self, named):
        self.split.update({nm: _split_halves(g) for nm, g in named})
        return _SwapCargo([self.split[nm] for nm, _ in named])

    def summed(self, names, got, tag):
        sums = _add_pairs([(self.split[nm], g) for nm, g in zip(names, got)], self.place, "presum_" + tag)
        self.sums.update(zip(names, sums))

    def exchange(self, names):
        return _ExchangeCargo([self.sums[nm] for nm in names], [nm == "small" for nm in names])

    def arrived(self, names, slots):
        self.slots.update(zip(names, slots))

    def reduce(self, names):
        jobs = [(self.sums[nm], self.slots[nm], nm == "small") for nm in names]
        return _sibling_fill(_sum_slots(jobs, self.place, "sum_slots"))


def _adamw(jobs, name):
    def item(w, m, v, grad, tm):
        rows, cols = w.shape
        gcols = grad.shape[1]

        def compute(ins, outs):
            w_, m_, v_ = ins[0][...], ins[1][...], ins[2][...]
            g = ins[3][0:tm, 0:cols]
            m_ = ADAM_B1 * m_ + (1.0 - ADAM_B1) * g
            v_ = ADAM_B2 * v_ + (1.0 - ADAM_B2) * jnp.square(g)
            m_hat = m_ / (1.0 - ADAM_B1 ** ADAM_STEP)
            v_hat = v_ / (1.0 - ADAM_B2 ** ADAM_STEP)
            outs[0][...] = g
            outs[1][...] = -ADAM_LR * (m_hat / (jnp.sqrt(v_hat) + ADAM_EPS) + ADAM_WD * w_)
            outs[2][...] = m_
            outs[3][...] = v_

        blk = lambda tile: pl.BlockSpec((tm, cols), lambda t: (tile(t), 0))
        if tm == rows:
            gspec = lambda tile: pl.BlockSpec(grad.shape, lambda t: (0, 0))
        else:
            gspec = lambda tile: pl.BlockSpec((tm, gcols), lambda t: (tile(t), 0))
        return (rows // tm, lambda tile: [blk(tile)] * 3 + [gspec(tile)], lambda tile: [blk(tile)] * 4,
                [jax.ShapeDtypeStruct((rows, cols), f32)] * 4, [w, m, v, grad], compute)

    return _batched_call([item(*job) for job in jobs], name)


def _rope_tables(s):
    half = HEAD_DIM // 2
    inv_freq = 1.0 / (10000.0 ** (jnp.arange(half, dtype=f32) / half))
    ang = jnp.arange(s, dtype=f32)[:, None] * inv_freq[None, :]
    cos, sin = jnp.cos(ang), jnp.sin(ang)
    return jnp.tile(cos, (1, LANES // half)), jnp.tile(jnp.concatenate([-sin, sin], axis=1), (1, LANES // HEAD_DIM))


def _local_step(x, target, win_g, gather, comm, b_gate, w_pool, pool_scale, ln1_g, ln1_b, convb_g, ln2_g, ln2_b):
    s = x.shape[0]
    nb = s // MOBA_BLOCK
    cos_t, sin_t = _rope_tables(s)
    x_bf = x.astype(bf16)

    z, (wba_g, wbp_g, wout_g) = _inproj(x_bf, win_g, cos_t, sin_t, b_gate, gather["inproj"])
    wba = wba_g.transpose(1, 0, 2).reshape(D_ATTN, D_MODEL)
    wbp = wbp_g.transpose(1, 0, 2).reshape(D_POOL, D_MODEL)
    wout = wout_g.reshape(D_MODEL, D_MODEL)
    (o_bf, lse, pooled_bf, mixed, pm_bf), (wg_g, wu_g, convw_g) = _attn_fwd(z, nb, w_pool, pool_scale,
                                                                           gather["attn_fwd"])
    m_bf = _branch_merge(o_bf, pm_bf, z, wba, wbp)
    x1_bf, xhat1, rstd1 = _out_ln1(m_bf, wout, x, ln1_g, ln1_b)
    (a, up, ac, hh_bf), (wd_g,) = _ffn_up(x1_bf, wg_g, wu_g, convw_g, convb_g, gather["ffn_up"])
    wd = wd_g.reshape(D_FF_PAD, D_MODEL)
    loss, dh2, dh2_bf, d_ln2_g, d_ln2_b = _ffn_down_ln2_loss(hh_bf, wd, xhat1, ln1_g, ln1_b, ln2_g, ln2_b, target)

    da_bf, dup_bf, dconv = _ffn_act_bwd(dh2_bf, wd_g, ac, a, up, convw_g)
    ffn_shape = (N_CHIPS, FF_PAD, D_MODEL)
    d_wd = _matmul_tn(hh_bf, dh2_bf, 1, "dw_ffn_down").reshape(ffn_shape)
    d_wg, got = _matmul_tn(da_bf, x1_bf, 1, "dw_ffn_gate", cargo=comm.swap([("w_ffn_down", d_wd)]))
    comm.summed(["w_ffn_down"], got, "ffn_down")
    d_wg = d_wg.reshape(ffn_shape)
    d_wu = _matmul_tn(dup_bf, x1_bf, 1, "dw_ffn_up").reshape(ffn_shape)
    gate_up = ["w_ffn_gate", "w_ffn_up", "conv_w"]
    (dh1, dh1_bf, d_ln1_g, d_ln1_b), (slots, got) = _ffn_in_bwd_ln1(
        da_bf, dup_bf, wg_g, wu_g, dh2, xhat1, rstd1, ln1_g,
        _MultiCargo([comm.exchange(["w_ffn_down"]), comm.swap(list(zip(gate_up, [d_wg, d_wu, dconv])))]))
    comm.arrived(["w_ffn_down"], slots)
    comm.summed(gate_up, got, "ffn_gate_up")
    d_wout = _matmul_tn(m_bf, dh1_bf, 1, "dw_out").reshape(N_CHIPS, D_MODEL // N_CHIPS, D_MODEL)
    dzg_bf, dya_bf, dyp_bf, do_bf, dpm, d_bgate = _merge_bwd(dh1_bf, wout, o_bf, pm_bf, z, wba, wbp)
    d_wba = _matmul_tn(o_bf, dya_bf, N_CHIPS, "dw_branch_attn")
    d_wbp = _matmul_tn(pm_bf, dyp_bf, N_CHIPS, "dw_branch_pool")
    du_bf, d_wpool, d_pscale = _pool_bwd(dpm, mixed, pooled_bf, w_pool, pool_scale)
    small = _pack_small([d_bgate, d_wpool, d_pscale, d_ln1_g, d_ln1_b, d_ln2_g, d_ln2_b, dconv[:, 3, :],
                         loss[0:1, :]])[None]
    branch = ["w_out", "w_branch_attn", "w_branch_pool", "small"]
    (dq_bf, dk_bf, dv_bf), (slots, got) = _attn_bwd(
        z, o_bf, lse, do_bf, cos_t, sin_t, nb,
        _MultiCargo([comm.exchange(gate_up), comm.swap(list(zip(branch, [d_wout, d_wba, d_wbp, small])))]))
    comm.arrived(gate_up, slots)
    comm.summed(branch, got, "branch")
    dz_bf = jnp.concatenate([dq_bf, dk_bf, dv_bf, du_bf, dzg_bf], axis=1)
    d_win, slots = _matmul_tn(x_bf, dz_bf, N_CHIPS, "dw_in", cargo=comm.exchange(branch))
    comm.arrived(branch, slots)
    comm.summed(["w_in"], _cargo_alone(comm.swap([("w_in", d_win)]), "swap_w_in"), "w_in")
    (grad_x,), slots = _inproj_dx(dz_bf, win_g, dh1, comm.exchange(["w_in"]))
    comm.arrived(["w_in"], slots)
    return grad_x


SMALL_ROWS = 592


def _pack_small(parts):
    flat = jnp.concatenate([p.reshape(-1) for p in parts])
    return jnp.pad(flat, (0, SMALL_ROWS * LANES - flat.shape[0])).reshape(SMALL_ROWS, LANES)


def _unpack_small(packed, shapes):
    flat = packed.reshape(-1)
    out, off = [], 0
    for shp in shapes:
        n = 1
        for d in shp:
            n *= d
        out.append(flat[off:off + n].reshape(shp))
        off += n
    return out


def _pad_conv_b(cb):
    return jnp.pad(cb.reshape(N_CHIPS, FF_SHARD), ((0, 0), (0, FF_PAD - FF_SHARD)))


def kernel(x, w_in, b_gate, w_branch_attn, w_pool, pool_scale, w_branch_pool, w_out, ln1_g, ln1_b, w_ffn_gate, w_ffn_up, conv_w, conv_b, w_ffn_down, ln2_g, ln2_b, loss_target, m_w_in, m_b_gate, m_w_branch_attn, m_w_pool, m_pool_scale, m_w_branch_pool, m_w_out, m_ln1_g, m_ln1_b, m_w_ffn_gate, m_w_ffn_up, m_conv_w, m_conv_b, m_w_ffn_down, m_ln2_g, m_ln2_b, v_w_in, v_b_gate, v_w_branch_attn, v_w_pool, v_pool_scale, v_w_branch_pool, v_w_out, v_ln1_g, v_ln1_b, v_w_ffn_gate, v_w_ffn_up, v_conv_w, v_conv_b, v_w_ffn_down, v_ln2_g, v_ln2_b):
    pad_rows = lambda w: jnp.pad(w.astype(bf16), ((0, FF_PAD - FF_SHARD), (0, 0)))
    shards = [w_in[0].astype(bf16), w_branch_attn[0].astype(bf16), w_branch_pool[0].astype(bf16),
              w_out[0].astype(bf16), pad_rows(w_ffn_gate[0].T), pad_rows(w_ffn_up[0].T), pad_rows(w_ffn_down[0]),
              jnp.pad(conv_w[0], ((0, CONV_ROWS - 3), (0, FF_PAD - FF_SHARD)))]
    (win_g,) = _cargo_alone(_GatherCargo(shards[:1], [0.0]), "gather_w_in")
    gather = {"inproj": _GatherCargo(shards[1:4], [0.3, 0.5, 0.75]),
              "attn_fwd": _GatherCargo(shards[4:6] + shards[7:], [0.97, 0.97, 0.97]),
              "ffn_up": _GatherCargo(shards[6:7], [0.8])}

    place = jnp.stack([2 * lax.axis_index("x") + lax.axis_index("y"), lax.axis_index("c")]).astype(jnp.int32)
    comm = _GradReducer(place)
    convb_g = _pad_conv_b(conv_b).reshape(N_CHIPS, 1, FF_PAD)
    grad_x = _local_step(
        x[0], loss_target[0], win_g, gather, comm, b_gate, w_pool[0], pool_scale, ln1_g, ln1_b, convb_g, ln2_g, ln2_b)
    names = ["w_in", "w_branch_attn", "w_branch_pool", "w_out", "w_ffn_gate", "w_ffn_up", "w_ffn_down", "conv_w", "small"]
    grads = comm.reduce(names)

    small_names = ["b_gate", "w_pool", "pool_scale", "ln1_g", "ln1_b", "ln2_g", "ln2_b", "conv_b"]
    small_w = [b_gate, w_pool, pool_scale, ln1_g, ln1_b, ln2_g, ln2_b, _pad_conv_b(conv_b)]
    small_m = [m_b_gate, m_w_pool, m_pool_scale, m_ln1_g, m_ln1_b, m_ln2_g, m_ln2_b, _pad_conv_b(m_conv_b)]
    small_v = [v_b_gate, v_w_pool, v_pool_scale, v_ln1_g, v_ln1_b, v_ln2_g, v_ln2_b, _pad_conv_b(v_conv_b)]
    weights = [w_in[0], w_branch_attn[0], w_branch_pool[0], w_out[0], w_ffn_gate[0].T, w_ffn_up[0].T, w_ffn_down[0],
               conv_w[0], _pack_small(small_w)]
    m_in = [m_w_in[0], m_w_branch_attn[0], m_w_branch_pool[0], m_w_out[0], m_w_ffn_gate[0].T, m_w_ffn_up[0].T,
            m_w_ffn_down[0], m_conv_w[0], _pack_small(small_m)]
    v_in = [v_w_in[0], v_w_branch_attn[0], v_w_branch_pool[0], v_w_out[0], v_w_ffn_gate[0].T, v_w_ffn_up[0].T,
            v_w_ffn_down[0], v_conv_w[0], _pack_small(small_v)]
    tiles = [256, 256, 256, 128, 176, 176, 176, 3, SMALL_ROWS]
    jobs = list(zip(weights, m_in, v_in, grads, tiles))
    ffn = slice(4, 7)
    updated = _adamw(jobs[:4] + jobs[7:], "adamw_rest")
    updated = updated[:4] + _adamw(jobs[ffn], "adamw_ffn") + updated[4:]
    res = {nm: [(r.T if nm in ("w_ffn_gate", "w_ffn_up") else r)[None] for r in outs4]
           for nm, outs4 in zip(names[:-1], updated)}
    packed = updated[-1]
    loss = grads[-1][SMALL_ROWS - 4, 0]
    shapes = [w.shape for w in small_w]
    for kind in range(4):
        for nm, val in zip(small_names, _unpack_small(packed[kind], shapes)):
            if nm == "conv_b":
                val = val[:, :FF_SHARD].reshape(1, 4 * FF_SHARD)
            res.setdefault(nm, [None] * 4)[kind] = val

    order = ["w_in", "b_gate", "w_branch_attn", "w_pool", "pool_scale", "w_branch_pool", "w_out", "ln1_g", "ln1_b",
             "w_ffn_gate", "w_ffn_up", "conv_w", "conv_b", "w_ffn_down", "ln2_g", "ln2_b"]
    outs = [loss, grad_x[None]]
    for kind in range(4):
        outs += [res[nm][kind] for nm in order]
    return tuple(outs)
```

```python
import functools

import jax
import jax.numpy as jnp
from jax import lax
from jax.experimental import pallas as pl
from jax.experimental.pallas import tpu as pltpu

D_MODEL = 1024
HEAD_DIM = 64
D_ATTN = 512
D_POOL = 512
MOBA_BLOCK = 256
MOBA_TOPK = 3
POOL_GROUP = 128
MAX_WINDOW = 16
FF_SHARD = 704
FF_PAD = 768
D_FF_PAD = 4 * FF_PAD
N_CHIPS = 4
LANES = 128
ALPHA = (2.0 * 1) ** 0.25
LN_EPS = 1e-5
NEG = -1e30
SCALE = HEAD_DIM ** -0.5
ADAM_LR, ADAM_B1, ADAM_B2, ADAM_EPS, ADAM_WD, ADAM_STEP = 0.001, 0.9, 0.999, 1e-08, 0.01, 10
VMEM_LIMIT = 56 * 1024 * 1024
MESH = pl.DeviceIdType.MESH

bf16 = jnp.bfloat16
f32 = jnp.float32


def _dot(a, b):
    return jnp.dot(a, b, preferred_element_type=f32)


def _dot_nt(a, b):
    return lax.dot_general(a, b, (((1,), (1,)), ((), ())), preferred_element_type=f32)


def _dot_tn(a, b):
    return lax.dot_general(a, b, (((0,), (0,)), ((), ())), preferred_element_type=f32)


def _call(body, *, name, grid, in_specs, out_specs, out_shape, scratch=(), cargo=None):
    if cargo is not None:
        return _cargo_call(body, cargo, name=name, grid=grid, in_specs=in_specs, out_specs=out_specs,
                           out_shape=out_shape, scratch=scratch)
    return pl.pallas_call(
        body, name=name, grid=grid, in_specs=in_specs, out_specs=out_specs, out_shape=out_shape,
        scratch_shapes=list(scratch),
        compiler_params=pltpu.CompilerParams(dimension_semantics=("arbitrary",) * len(grid),
                                             vmem_limit_bytes=VMEM_LIMIT))


def _rot_half(t):
    lane = lax.broadcasted_iota(jnp.int32, t.shape, 1)
    first = (lane % HEAD_DIM) < (HEAD_DIM // 2)
    return jnp.where(first, pltpu.roll(t, LANES - HEAD_DIM // 2, 1), pltpu.roll(t, HEAD_DIM // 2, 1))


def _rope(t, cos, sin_signed):
    return t * cos + _rot_half(t) * sin_signed


def _rope_bwd(d, cos, sin_signed):
    return d * cos + _rot_half(d * sin_signed)


def _gelu_parts(a):
    cdf = 0.5 * (1.0 + lax.erf(a * (2.0 ** -0.5)))
    pdf = jnp.exp(-0.5 * a * a) * ((2.0 * jnp.pi) ** -0.5)
    return a * cdf, cdf + a * pdf


def _layer_norm(h, g, b):
    mu = jnp.mean(h, axis=-1, keepdims=True)
    xc = h - mu
    var = jnp.mean(xc * xc, axis=-1, keepdims=True)
    rstd = lax.rsqrt(var + LN_EPS)
    xhat = xc * rstd
    return xhat * g + b, xhat, rstd


def _layer_norm_bwd(dy, xhat, rstd, g):
    dxh = dy * g
    m1 = jnp.mean(dxh, axis=-1, keepdims=True)
    m2 = jnp.mean(dxh * xhat, axis=-1, keepdims=True)
    return rstd * (dxh - m1 - xhat * m2)


def _inproj(x_bf, win_g, cos_t, sin_t, b_gate, cargo):
    s = x_bf.shape[0]
    tm, tn = 1024, 512

    def body(x_ref, w_ref, cos_ref, sin_ref, b_ref, o_ref):
        j = pl.program_id(1)
        acc = _dot(x_ref[...], w_ref[...])

        @pl.when(j < 2)
        def _():
            for c in range(tn // LANES):
                sl = slice(c * LANES, (c + 1) * LANES)
                o_ref[:, sl] = _rope(acc[:, sl], cos_ref[...], sin_ref[...])

        @pl.when((j >= 2) & (j < 4))
        def _():
            o_ref[...] = acc

        @pl.when(j >= 4)
        def _():
            o_ref[...] = jax.nn.sigmoid(acc + b_ref[...])

    (z,), carried = _cargo_call(
        body, cargo, name="inproj", grid=(s // tm, 4 * D_MODEL // tn),
        in_specs=[pl.BlockSpec((tm, D_MODEL), lambda i, j: (i, 0)),
                  pl.BlockSpec((None, D_MODEL, tn), lambda i, j: (j // 2, 0, j % 2)),
                  pl.BlockSpec((tm, LANES), lambda i, j: (i, 0)),
                  pl.BlockSpec((tm, LANES), lambda i, j: (i, 0)),
                  pl.BlockSpec((1, tn), lambda i, j: (0, jnp.maximum(j - 4, 0)))],
        out_specs=[pl.BlockSpec((tm, tn), lambda i, j: (i, j))],
        out_shape=[jax.ShapeDtypeStruct((s, 4 * D_MODEL), f32)])(x_bf, win_g, cos_t, sin_t, b_gate)
    return z, carried


STAT_ROWS = 8


def _pair_rows(v0, v1, fill):
    r = lax.broadcasted_iota(jnp.int32, (STAT_ROWS, v0.shape[1]), 0)
    return jnp.where(r == 0, v0, jnp.where(r == 1, v1, fill))


def _head_lanes(shape):
    lane = lax.broadcasted_iota(jnp.int32, shape, 1)
    return lane < HEAD_DIM, lane >= HEAD_DIM


def _head_rows(shape):
    row = lax.broadcasted_iota(jnp.int32, shape, 0)
    return row < HEAD_DIM, row >= HEAD_DIM


def _moba_select(q_bf, k_all, i, nb):
    k_mean = jnp.mean(k_all.reshape(nb, MOBA_BLOCK, LANES), axis=1)
    n_io = lax.broadcasted_iota(jnp.int32, (nb, MOBA_BLOCK), 0)
    past = n_io < i
    sels = []
    for head in _head_lanes((nb, LANES)):
        gate = _dot_nt(jnp.where(head, k_mean, 0.0).astype(bf16), q_bf)
        g = jnp.where(past, gate, NEG)
        rank = jnp.zeros((nb, MOBA_BLOCK), f32)
        for m in range(nb):
            gm = g[m:m + 1, :]
            rank = rank + jnp.where((gm > g) | ((gm == g) & (m < n_io)), 1.0, 0.0)
        sels.append(jnp.where(past & (rank < MOBA_TOPK), 1.0, 0.0))
    return sels


def _attn_fwd(z, nb, w_pool, pool_scale, cargo):
    s = z.shape[0]

    def body(q_ref, k_ref, v_ref, u_ref, wp_ref, sc_ref, o_ref, lse_ref, pooled_ref, mixed_ref, pm_ref, vt_s, pad):
        pair, i = pl.program_id(0), pl.program_id(1)

        @pl.when(i == 0)
        def _():
            for j in range(nb):
                vt_s[j] = v_ref[j * MOBA_BLOCK:(j + 1) * MOBA_BLOCK, :].T.astype(bf16)
            _pool_group(pair, u_ref, wp_ref, sc_ref, pooled_ref, mixed_ref, pm_ref, pad)

        k_heads = _head_lanes((MOBA_BLOCK, LANES))
        o_heads = _head_rows((LANES, MOBA_BLOCK))

        def block(j, qs, allow, m_old, l_old):
            kj, vtj = k_ref[j * MOBA_BLOCK:(j + 1) * MOBA_BLOCK, :], vt_s[j]
            m_new, alpha, l_new, pv = [], [], [], None
            for h in range(2):
                sc = jnp.where(allow[h], _dot_nt(jnp.where(k_heads[h], kj, 0.0).astype(bf16), qs), NEG)
                mx = jnp.max(sc, axis=0, keepdims=True)
                mn = mx if m_old is None else jnp.maximum(m_old[h], mx)
                p = jnp.exp(sc - mn)
                lsum = jnp.sum(p, axis=0, keepdims=True)
                if m_old is None:
                    a = None
                else:
                    a = jnp.exp(m_old[h] - mn)
                    lsum = a * l_old[h] + lsum
                t = _dot(jnp.where(o_heads[h], vtj, jnp.zeros_like(vtj)), p.astype(bf16))
                pv = t if pv is None else pv + t
                m_new.append(mn)
                alpha.append(a)
                l_new.append(lsum)
            return m_new, alpha, l_new, pv

        def query_block(c):
            q = q_ref[...]
            sels = _moba_select(q.astype(bf16), k_ref[...], c, nb)
            qs = (q * SCALE).astype(bf16)
            key = lax.broadcasted_iota(jnp.int32, (MOBA_BLOCK, MOBA_BLOCK), 0)
            qry = lax.broadcasted_iota(jnp.int32, (MOBA_BLOCK, MOBA_BLOCK), 1)
            m, _, l, acc = block(c, qs, [key <= qry] * 2, None, None)
            for j in range(c):
                m, alpha, l, pv = block(j, qs, [sels[h][j:j + 1, :] > 0.0 for h in range(2)], m, l)
                acc = acc * jnp.where(o_heads[0], alpha[0], alpha[1]) + pv
            o_ref[...] = (acc / jnp.where(o_heads[0], l[0], l[1])).T.astype(bf16)
            lse_ref[...] = _pair_rows(m[0] + jnp.log(l[0]), m[1] + jnp.log(l[1]), 0.0)

        for c in range(nb):
            pl.when(i == c)(functools.partial(query_block, c))

    group = pl.BlockSpec((s, POOL_GROUP), lambda hp, i: (0, hp))
    return _cargo_call(
        body, cargo, name="attn_fwd", grid=(D_ATTN // LANES, nb),
        in_specs=[pl.BlockSpec((MOBA_BLOCK, LANES), lambda hp, i: (i, hp)),
                  pl.BlockSpec((s, LANES), lambda hp, i: (0, 4 + hp)),
                  pl.BlockSpec((s, LANES), lambda hp, i: (0, 8 + hp)),
                  pl.BlockSpec((s, POOL_GROUP), lambda hp, i: (0, 12 + hp)),
                  pl.BlockSpec((None, POOL_GROUP, POOL_GROUP), lambda hp, i: (hp, 0, 0)),
                  pl.BlockSpec((1, POOL_GROUP), lambda hp, i: (0, hp))],
        out_specs=[pl.BlockSpec((MOBA_BLOCK, LANES), lambda hp, i: (i, hp)),
                   pl.BlockSpec((None, STAT_ROWS, MOBA_BLOCK), lambda hp, i: (hp, 0, i)), group, group, group],
        out_shape=[jax.ShapeDtypeStruct((s, D_ATTN), bf16),
                   jax.ShapeDtypeStruct((D_ATTN // LANES, STAT_ROWS, s), f32),
                   jax.ShapeDtypeStruct((s, D_POOL), bf16), jax.ShapeDtypeStruct((s, D_POOL), f32),
                   jax.ShapeDtypeStruct((s, D_POOL), bf16)],
        scratch=[pltpu.VMEM((nb, LANES, MOBA_BLOCK), bf16), pltpu.VMEM((s + MAX_WINDOW, POOL_GROUP), f32)],
    )(z, z, z, z, w_pool, pool_scale)


def _window_select(snaps, g):
    return jnp.where(g == 0, snaps[0], jnp.where(g == 1, snaps[1], jnp.where(g == 2, snaps[2], snaps[3])))


def _pool_group(g, u_ref, w_ref, sc_ref, pooled_ref, mixed_ref, pm_ref, pad):
    s = u_ref.shape[0]
    u = u_ref[...]
    pad[0:MAX_WINDOW, :] = jnp.zeros((MAX_WINDOW, POOL_GROUP), f32)
    pad[MAX_WINDOW:MAX_WINDOW + s, :] = u
    acc = u
    snaps = []
    for d in range(1, MAX_WINDOW):
        acc = acc + pad[MAX_WINDOW - d:MAX_WINDOW - d + s, :]
        if d + 1 in (2, 4, 8, 16):
            snaps.append(acc)
    win = _window_select(snaps, g)
    t = lax.broadcasted_iota(jnp.int32, (s, POOL_GROUP), 0)
    count = jnp.minimum(t + 1, jnp.left_shift(2, g)).astype(f32)
    pooled = (win / count - u).astype(bf16)
    mixed = _dot(pooled, w_ref[...].astype(bf16))
    pooled_ref[...] = pooled
    mixed_ref[...] = mixed
    pm_ref[...] = (mixed * sc_ref[...]).astype(bf16)


def _branch_merge(o_bf, pm_bf, z, wba, wbp):
    s = o_bf.shape[0]
    tm = 512

    def body(o_ref, pm_ref, ga_ref, gp_ref, wba_ref, wbp_ref, m_ref):
        ya = _dot(o_ref[...], wba_ref[...])
        yp = _dot(pm_ref[...], wbp_ref[...])
        m_ref[...] = (ga_ref[...] * ya + gp_ref[...] * yp).astype(bf16)

    full = lambda r, c: pl.BlockSpec((r, c), lambda i: (0, 0))
    return _call(
        body, name="branch_merge", grid=(s // tm,),
        in_specs=[pl.BlockSpec((tm, D_ATTN), lambda i: (i, 0)), pl.BlockSpec((tm, D_POOL), lambda i: (i, 0)),
                  pl.BlockSpec((tm, D_MODEL), lambda i: (i, 2)), pl.BlockSpec((tm, D_MODEL), lambda i: (i, 3)),
                  full(D_ATTN, D_MODEL), full(D_POOL, D_MODEL)],
        out_specs=pl.BlockSpec((tm, D_MODEL), lambda i: (i, 0)),
        out_shape=jax.ShapeDtypeStruct((s, D_MODEL), bf16))(o_bf, pm_bf, z, z, wba, wbp)


def _out_ln1(m_bf, wout, x, ln_g, ln_b):
    s = x.shape[0]
    tm = 512

    def body(m_ref, w_ref, x_ref, g_ref, b_ref, x1_ref, xhat_ref, rstd_ref):
        h = ALPHA * x_ref[...] + _dot(m_ref[...], w_ref[...])
        y, xhat, rstd = _layer_norm(h, g_ref[...], b_ref[...])
        x1_ref[...] = y.astype(bf16)
        xhat_ref[...] = xhat
        rstd_ref[...] = jnp.broadcast_to(rstd, (tm, LANES))

    row = pl.BlockSpec((tm, D_MODEL), lambda i: (i, 0))
    vec = pl.BlockSpec((1, D_MODEL), lambda i: (0, 0))
    return _call(
        body, name="out_ln1", grid=(s // tm,),
        in_specs=[row, pl.BlockSpec((D_MODEL, D_MODEL), lambda i: (0, 0)), row, vec, vec],
        out_specs=[row, row, pl.BlockSpec((tm, LANES), lambda i: (i, 0))],
        out_shape=[jax.ShapeDtypeStruct((s, D_MODEL), bf16), jax.ShapeDtypeStruct((s, D_MODEL), f32),
                   jax.ShapeDtypeStruct((s, LANES), f32)])(m_bf, wout, x, ln_g, ln_b)


FF_TILE = 256
FF_TILES_PER_SHARD = FF_PAD // FF_TILE
CONV_PAD = 8
CONV_ROWS = 16

def _ff_weight_spec(rows):
    return pl.BlockSpec((None, rows, FF_TILE), lambda j: (j // FF_TILES_PER_SHARD, 0, j % FF_TILES_PER_SHARD))


def _ff_rows_spec():
    return pl.BlockSpec((None, FF_TILE, D_MODEL), lambda j: (j // FF_TILES_PER_SHARD, j % FF_TILES_PER_SHARD, 0))


def _row_shifts(x, shifts):
    s = x.shape[0]
    padded = jnp.concatenate([x, jnp.zeros((CONV_PAD, x.shape[1]), x.dtype)], axis=0)
    return [pltpu.roll(padded, d % (s + CONV_PAD), 0)[0:s] for d in shifts]


def _ffn_up(x1_bf, wg_g, wu_g, convw_g, convb_g, cargo):
    s = x1_bf.shape[0]

    def body(x_ref, wg_ref, wu_ref, cw_ref, cb_ref, a_ref, up_ref, ac_ref, hh_ref):
        x1 = x_ref[...]
        a = _dot_nt(x1, wg_ref[...])
        up = _dot_nt(x1, wu_ref[...])
        a_ref[...] = a
        up_ref[...] = up
        a1, a2 = _row_shifts(a, (1, 2))
        ac = a * cw_ref[2:3, :] + a1 * cw_ref[1:2, :] + a2 * cw_ref[0:1, :] + cb_ref[...]
        ac_ref[...] = ac
        hg, _ = _gelu_parts(ac)
        hh_ref[...] = (hg * up).astype(bf16)

    col = pl.BlockSpec((s, FF_TILE), lambda j: (0, j))
    wide = jax.ShapeDtypeStruct((s, D_FF_PAD), f32)
    return _call(
        body, name="ffn_up", grid=(D_FF_PAD // FF_TILE,),
        in_specs=[pl.BlockSpec((s, D_MODEL), lambda j: (0, 0)), _ff_rows_spec(), _ff_rows_spec(),
                  _ff_weight_spec(8), _ff_weight_spec(1)],
        out_specs=[col, col, col, col],
        out_shape=[wide, wide, wide, jax.ShapeDtypeStruct((s, D_FF_PAD), bf16)],
        cargo=cargo)(x1_bf, wg_g, wu_g, convw_g, convb_g)


def _ffn_down_ln2_loss(hh_bf, wd, xhat1, ln1_g, ln1_b, ln2_g, ln2_b, target):
    s = hh_bf.shape[0]
    tm = 256

    def body(hh_ref, w_ref, xh1_ref, g1_ref, b1_ref, g2_ref, b2_ref, t_ref, loss_ref, dh_ref, dhb_ref, dg_ref, db_ref):
        i = pl.program_id(0)
        x1 = xh1_ref[...] * g1_ref[...] + b1_ref[...]
        h = ALPHA * x1 + _dot(hh_ref[...], w_ref[...])
        y, xhat, rstd = _layer_norm(h, g2_ref[...], b2_ref[...])
        err = y - t_ref[...]
        part = 0.5 * jnp.sum(jnp.mean(err * err, axis=-1, keepdims=True), axis=0, keepdims=True)
        dy = err * (1.0 / D_MODEL)

        @pl.when(i == 0)
        def _():
            loss_ref[...] = jnp.zeros_like(loss_ref)
            dg_ref[...] = jnp.zeros_like(dg_ref)
            db_ref[...] = jnp.zeros_like(db_ref)

        loss_ref[...] += jnp.broadcast_to(part, loss_ref.shape)
        dg_ref[...] += jnp.sum(dy * xhat, axis=0, keepdims=True)
        db_ref[...] += jnp.sum(dy, axis=0, keepdims=True)
        dh = _layer_norm_bwd(dy, xhat, rstd, g2_ref[...])
        dh_ref[...] = dh
        dhb_ref[...] = dh.astype(bf16)

    row = pl.BlockSpec((tm, D_MODEL), lambda i: (i, 0))
    vec = pl.BlockSpec((1, D_MODEL), lambda i: (0, 0))
    return _call(
        body, name="ffn_down_ln2_loss", grid=(s // tm,),
        in_specs=[pl.BlockSpec((tm, D_FF_PAD), lambda i: (i, 0)),
                  pl.BlockSpec((D_FF_PAD, D_MODEL), lambda i: (0, 0)),
                  row, vec, vec, vec, vec, row],
        out_specs=[pl.BlockSpec((8, LANES), lambda i: (0, 0)), row, row, vec, vec],
        out_shape=[jax.ShapeDtypeStruct((8, LANES), f32), jax.ShapeDtypeStruct((s, D_MODEL), f32),
                   jax.ShapeDtypeStruct((s, D_MODEL), bf16), jax.ShapeDtypeStruct((1, D_MODEL), f32),
                   jax.ShapeDtypeStruct((1, D_MODEL), f32)])(hh_bf, wd, xhat1, ln1_g, ln1_b, ln2_g, ln2_b, target)


def _ffn_act_bwd(dh2_bf, wd_g, ac, a, up, convw_g):
    s = dh2_bf.shape[0]

    def body(dh_ref, wd_ref, ac_ref, a_ref, up_ref, cw_ref, da_ref, dup_ref, dc_ref):
        dhh = _dot_nt(dh_ref[...], wd_ref[...])
        hg, dgelu = _gelu_parts(ac_ref[...])
        dup_ref[...] = (dhh * hg).astype(bf16)
        dac = dhh * up_ref[...] * dgelu
        dac1, dac2 = _row_shifts(dac, (-1, -2))
        a = a_ref[...]
        rows = [jnp.sum(d * a, axis=0, keepdims=True) for d in (dac2, dac1, dac)]
        rows.append(jnp.sum(dac, axis=0, keepdims=True))
        rows.append(jnp.zeros((CONV_ROWS - 4, FF_TILE), f32))
        dc_ref[...] = jnp.concatenate(rows, axis=0)
        da_ref[...] = (dac * cw_ref[2:3, :] + dac1 * cw_ref[1:2, :] + dac2 * cw_ref[0:1, :]).astype(bf16)

    col = pl.BlockSpec((s, FF_TILE), lambda j: (0, j))
    return _call(
        body, name="ffn_act_bwd", grid=(D_FF_PAD // FF_TILE,),
        in_specs=[pl.BlockSpec((s, D_MODEL), lambda j: (0, 0)), _ff_rows_spec(), col, col, col, _ff_weight_spec(8)],
        out_specs=[col, col, _ff_weight_spec(CONV_ROWS)],
        out_shape=[jax.ShapeDtypeStruct((s, D_FF_PAD), bf16), jax.ShapeDtypeStruct((s, D_FF_PAD), bf16),
                   jax.ShapeDtypeStruct((N_CHIPS, CONV_ROWS, FF_PAD), f32)],
    )(dh2_bf, wd_g, ac, a, up, convw_g)


def _matmul_tn(a, b, n_shards, name, tm=512, tn=1024, cargo=None):
    k, m = a.shape
    n = b.shape[1]
    tm, tn = min(tm, m), min(tn, n // n_shards)
    per = n // n_shards // tn

    def body(a_ref, b_ref, o_ref, at_s):
        @pl.when(pl.program_id(1) == 0)
        def _():
            at_s[...] = a_ref[...].T

        o_ref[...] = _dot(at_s[...], b_ref[...]).astype(bf16)

    res = _call(
        body, name=name, grid=(m // tm, n // tn),
        in_specs=[pl.BlockSpec((k, tm), lambda i, j: (0, i)), pl.BlockSpec((k, tn), lambda i, j: (0, j))],
        out_specs=[pl.BlockSpec((None, tm, tn), lambda i, j: (j // per, i, j % per))],
        out_shape=[jax.ShapeDtypeStruct((n_shards, m, n // n_shards), bf16)], scratch=[pltpu.VMEM((tm, k), bf16)],
        cargo=cargo)(a, b)
    return res[0] if cargo is None else (res[0][0], res[1])


def _ffn_in_bwd_ln1(da_bf, dup_bf, wg_g, wu_g, dh2, xhat1, rstd1, ln1_g, cargo):
    s = da_bf.shape[0]
    tm = 256

    def body(da_ref, dup_ref, wg_ref, wu_ref, dh2_ref, xh_ref, rstd_ref, g_ref, dh_ref, dhb_ref, dg_ref, db_ref):
        i = pl.program_id(0)
        dx1 = ALPHA * dh2_ref[...]
        for sh in range(N_CHIPS):
            sl = slice(sh * FF_PAD, (sh + 1) * FF_PAD)
            dx1 = dx1 + _dot(da_ref[:, sl], wg_ref[sh]) + _dot(dup_ref[:, sl], wu_ref[sh])
        xhat = xh_ref[...]

        @pl.when(i == 0)
        def _():
            dg_ref[...] = jnp.zeros_like(dg_ref)
            db_ref[...] = jnp.zeros_like(db_ref)

        dg_ref[...] += jnp.sum(dx1 * xhat, axis=0, keepdims=True)
        db_ref[...] += jnp.sum(dx1, axis=0, keepdims=True)
        dh = _layer_norm_bwd(dx1, xhat, rstd_ref[:, 0:1], g_ref[...])
        dh_ref[...] = dh
        dhb_ref[...] = dh.astype(bf16)

    row = pl.BlockSpec((tm, D_MODEL), lambda i: (i, 0))
    wide = pl.BlockSpec((tm, D_FF_PAD), lambda i: (i, 0))
    wfull = pl.BlockSpec((N_CHIPS, FF_PAD, D_MODEL), lambda i: (0, 0, 0))
    vec = pl.BlockSpec((1, D_MODEL), lambda i: (0, 0))
    return _call(
        body, name="ffn_in_bwd_ln1", grid=(s // tm,),
        in_specs=[wide, wide, wfull, wfull, row, row, pl.BlockSpec((tm, LANES), lambda i: (i, 0)), vec],
        out_specs=[row, row, vec, vec],
        out_shape=[jax.ShapeDtypeStruct((s, D_MODEL), f32), jax.ShapeDtypeStruct((s, D_MODEL), bf16),
                   jax.ShapeDtypeStruct((1, D_MODEL), f32), jax.ShapeDtypeStruct((1, D_MODEL), f32)],
        cargo=cargo)(da_bf, dup_bf, wg_g, wu_g, dh2, xhat1, rstd1, ln1_g)


def _merge_bwd(dh1_bf, wout, o_bf, pm_bf, z, wba, wbp):
    s = dh1_bf.shape[0]
    tm = 256

    def body(dh_ref, wout_ref, o_ref, pm_ref, ga_ref, gp_ref, wba_ref, wbp_ref,
             dzg_ref, dya_ref, dyp_ref, do_ref, dpm_ref, dbg_ref):
        i = pl.program_id(0)
        dm = _dot_nt(dh_ref[...], wout_ref[...])
        ya = _dot(o_ref[...], wba_ref[...])
        yp = _dot(pm_ref[...], wbp_ref[...])
        ga, gp = ga_ref[...], gp_ref[...]
        dza = dm * ya * ga * (1.0 - ga)
        dzp = dm * yp * gp * (1.0 - gp)

        @pl.when(i == 0)
        def _():
            dbg_ref[...] = jnp.zeros_like(dbg_ref)

        dbg_ref[:, 0:D_MODEL] += jnp.sum(dza, axis=0, keepdims=True)
        dbg_ref[:, D_MODEL:2 * D_MODEL] += jnp.sum(dzp, axis=0, keepdims=True)
        dzg_ref[:, 0:D_MODEL] = dza.astype(bf16)
        dzg_ref[:, D_MODEL:2 * D_MODEL] = dzp.astype(bf16)
        dya = (dm * ga).astype(bf16)
        dyp = (dm * gp).astype(bf16)
        dya_ref[...] = dya
        dyp_ref[...] = dyp
        do_ref[...] = _dot_nt(dya, wba_ref[...]).astype(bf16)
        dpm_ref[...] = _dot_nt(dyp, wbp_ref[...])

    row = pl.BlockSpec((tm, D_MODEL), lambda i: (i, 0))
    half = pl.BlockSpec((tm, D_ATTN), lambda i: (i, 0))
    full = lambda r, c: pl.BlockSpec((r, c), lambda i: (0, 0))
    return _call(
        body, name="merge_bwd", grid=(s // tm,),
        in_specs=[row, full(D_MODEL, D_MODEL), half, half,
                  pl.BlockSpec((tm, D_MODEL), lambda i: (i, 2)), pl.BlockSpec((tm, D_MODEL), lambda i: (i, 3)),
                  full(D_ATTN, D_MODEL), full(D_POOL, D_MODEL)],
        out_specs=[pl.BlockSpec((tm, 2 * D_MODEL), lambda i: (i, 0)), row, row, half, half,
                   pl.BlockSpec((1, 2 * D_MODEL), lambda i: (0, 0))],
        out_shape=[jax.ShapeDtypeStruct((s, 2 * D_MODEL), bf16), jax.ShapeDtypeStruct((s, D_MODEL), bf16),
                   jax.ShapeDtypeStruct((s, D_MODEL), bf16), jax.ShapeDtypeStruct((s, D_ATTN), bf16),
                   jax.ShapeDtypeStruct((s, D_POOL), f32), jax.ShapeDtypeStruct((1, 2 * D_MODEL), f32)],
    )(dh1_bf, wout, o_bf, pm_bf, z, z, wba, wbp)


def _pool_bwd(dpm, mixed, pooled_bf, w_pool, pool_scale):
    s = dpm.shape[0]

    def body(dpm_ref, mixed_ref, pooled_ref, w_ref, sc_ref, du_ref, dw_ref, dsc_ref, pad):
        g = pl.program_id(0)
        dpm_v = dpm_ref[...]
        dsc_ref[...] = jnp.sum(dpm_v * mixed_ref[...], axis=0, keepdims=True)
        dmixed = (dpm_v * sc_ref[...]).astype(bf16)
        dw_ref[...] = _dot_tn(pooled_ref[...], dmixed)
        dpooled = _dot_nt(dmixed, w_ref[...].astype(bf16))
        t = lax.broadcasted_iota(jnp.int32, (s, POOL_GROUP), 0)
        count = jnp.minimum(t + 1, jnp.left_shift(2, g)).astype(f32)
        r = dpooled / count
        pad[0:s, :] = r
        pad[s:s + MAX_WINDOW, :] = jnp.zeros((MAX_WINDOW, POOL_GROUP), f32)
        acc = r
        snaps = []
        for d in range(1, MAX_WINDOW):
            acc = acc + pad[d:d + s, :]
            if d + 1 in (2, 4, 8, 16):
                snaps.append(acc)
        du_ref[...] = (_window_select(snaps, g) - dpooled).astype(bf16)

    blk = pl.BlockSpec((s, POOL_GROUP), lambda g: (0, g))
    return _call(
        body, name="pool_bwd", grid=(4,),
        in_specs=[blk, blk, blk, pl.BlockSpec((None, POOL_GROUP, POOL_GROUP), lambda g: (g, 0, 0)),
                  pl.BlockSpec((1, POOL_GROUP), lambda g: (0, g))],
        out_specs=[blk, pl.BlockSpec((None, POOL_GROUP, POOL_GROUP), lambda g: (g, 0, 0)),
                   pl.BlockSpec((1, POOL_GROUP), lambda g: (0, g))],
        out_shape=[jax.ShapeDtypeStruct((s, D_POOL), bf16), jax.ShapeDtypeStruct((4, POOL_GROUP, POOL_GROUP), f32),
                   jax.ShapeDtypeStruct((1, D_POOL), f32)],
        scratch=[pltpu.VMEM((s + MAX_WINDOW, POOL_GROUP), f32)])(dpm, mixed, pooled_bf, w_pool, pool_scale)


def _attn_bwd(z, o_bf, lse, do_bf, cos_t, sin_t, nb, cargo):
    s = z.shape[0]

    def body(q_ref, k_ref, v_ref, o_ref, lse_ref, do_ref, cq_ref, sq_ref, cf_ref, sf_ref,
             dq_ref, dk_ref, dv_ref, kt_s, dk_s, dv_s):
        i = pl.program_id(1)

        @pl.when(i == 0)
        def _():
            dk_s[...] = jnp.zeros_like(dk_s)
            dv_s[...] = jnp.zeros_like(dv_s)
            for j in range(nb):
                kt_s[j] = k_ref[j * MOBA_BLOCK:(j + 1) * MOBA_BLOCK, :].T.astype(bf16)

        k_heads = _head_lanes((MOBA_BLOCK, LANES))
        t_heads = _head_rows((LANES, MOBA_BLOCK))

        def query_block(c):
            q = q_ref[...]
            sels = _moba_select(q.astype(bf16), k_ref[...], c, nb)
            qs = (q * SCALE).astype(bf16)
            qraw = [jnp.where(hm, q, 0.0).astype(bf16) for hm in k_heads]
            do = do_ref[...].astype(f32)
            do_bf = do_ref[...]
            dob = [jnp.where(hm, do, 0.0).astype(bf16) for hm in k_heads]
            pick = _head_lanes((STAT_ROWS, LANES))
            r8 = lax.broadcasted_iota(jnp.int32, (STAT_ROWS, LANES), 0)
            head_pick = jnp.where(((r8 == 0) & pick[0]) | ((r8 == 1) & pick[1]), 1.0, 0.0)
            delta8 = lax.dot_general(head_pick, do * o_ref[...].astype(f32), (((1,), (1,)), ((), ())),
                                     precision=lax.Precision.HIGHEST, preferred_element_type=f32)
            lse8 = lse_ref[...]
            delta, lse_h = [delta8[0:1, :], delta8[1:2, :]], [lse8[0:1, :], lse8[1:2, :]]
            key = lax.broadcasted_iota(jnp.int32, (MOBA_BLOCK, MOBA_BLOCK), 0)
            qry = lax.broadcasted_iota(jnp.int32, (MOBA_BLOCK, MOBA_BLOCK), 1)

            def block(j, allow):
                rows = slice(j * MOBA_BLOCK, (j + 1) * MOBA_BLOCK)
                kj, vj, ktj = k_ref[rows, :], v_ref[rows, :], kt_s[j]
                dqt, dk, dv = None, None, None
                for h in range(2):
                    sc = _dot_nt(jnp.where(k_heads[h], kj, 0.0).astype(bf16), qs)
                    p = jnp.where(allow[h], jnp.exp(sc - lse_h[h]), 0.0)
                    dp = _dot_nt(jnp.where(k_heads[h], vj, 0.0).astype(bf16), do_bf)
                    ds = (p * (dp - delta[h]) * SCALE).astype(bf16)
                    tv = _dot(p.astype(bf16), dob[h])
                    tk = _dot(ds, qraw[h])
                    tq = _dot(jnp.where(t_heads[h], ktj, jnp.zeros_like(ktj)), ds)
                    dqt, dk, dv = (tq, tk, tv) if dqt is None else (dqt + tq, dk + tk, dv + tv)
                dk_s[rows, :] += dk
                dv_s[rows, :] += dv
                return dqt

            dqt = block(c, [key <= qry] * 2)
            for j in range(c):
                dqt = dqt + block(j, [sels[h][j:j + 1, :] > 0.0 for h in range(2)])
            dq_ref[...] = _rope_bwd(dqt.T, cq_ref[...], sq_ref[...]).astype(bf16)

        for c in range(nb):
            pl.when(i == c)(functools.partial(query_block, c))

        @pl.when(i == nb - 1)
        def _():
            dk_ref[...] = _rope_bwd(dk_s[...], cf_ref[...], sf_ref[...]).astype(bf16)
            dv_ref[...] = dv_s[...].astype(bf16)

    qblk = pl.BlockSpec((MOBA_BLOCK, LANES), lambda hp, i: (i, hp))
    tq = pl.BlockSpec((MOBA_BLOCK, LANES), lambda hp, i: (i, 0))
    tf = pl.BlockSpec((s, LANES), lambda hp, i: (0, 0))
    colblk = pl.BlockSpec((s, LANES), lambda hp, i: (0, hp))
    return _cargo_call(
        body, cargo, name="attn_bwd", grid=(D_ATTN // LANES, nb),
        in_specs=[qblk, pl.BlockSpec((s, LANES), lambda hp, i: (0, 4 + hp)),
                  pl.BlockSpec((s, LANES), lambda hp, i: (0, 8 + hp)), qblk,
                  pl.BlockSpec((None, STAT_ROWS, MOBA_BLOCK), lambda hp, i: (hp, 0, i)), qblk, tq, tq, tf, tf],
        out_specs=[qblk, colblk, colblk],
        out_shape=[jax.ShapeDtypeStruct((s, D_ATTN), bf16)] * 3,
        scratch=[pltpu.VMEM((nb, LANES, MOBA_BLOCK), bf16), pltpu.VMEM((s, LANES), f32), pltpu.VMEM((s, LANES), f32)],
    )(z, z, z, o_bf, lse, do_bf, cos_t, sin_t, cos_t, sin_t)


def _inproj_dx(dz_bf, win_g, dh1, cargo):
    s = dz_bf.shape[0]
    tm = 256

    def body(dz_ref, w_ref, dh_ref, gx_ref):
        acc = ALPHA * dh_ref[...]
        for sh in range(N_CHIPS):
            acc = acc + _dot_nt(dz_ref[:, sh * D_MODEL:(sh + 1) * D_MODEL], w_ref[sh])
        gx_ref[...] = acc

    row = pl.BlockSpec((tm, D_MODEL), lambda i: (i, 0))
    return _call(
        body, name="inproj_dx", grid=(s // tm,),
        in_specs=[pl.BlockSpec((tm, 4 * D_MODEL), lambda i: (i, 0)),
                  pl.BlockSpec((N_CHIPS, D_MODEL, D_MODEL), lambda i: (0, 0, 0)), row],
        out_specs=[row], out_shape=[jax.ShapeDtypeStruct((s, D_MODEL), f32)], cargo=cargo)(dz_bf, win_g, dh1)


ANY = pl.BlockSpec(memory_space=pl.ANY)


def _chip_index():
    return 2 * lax.axis_index("x") + lax.axis_index("y")


def _peer(k):
    x, y, c = lax.axis_index("x"), lax.axis_index("y"), lax.axis_index("c")
    return (x ^ (k >> 1), y ^ (k & 1), c)


def _sibling():
    return (lax.axis_index("x"), lax.axis_index("y"), 1 - lax.axis_index("c"))


class _GatherCargo:
    def __init__(self, shards, pass_on):
        self.shards, self.pass_on, n = shards, pass_on, len(shards)
        self.inputs = [a.reshape(2, a.shape[0] // 2, a.shape[1]) for a in shards]
        self.out_shape = [jax.ShapeDtypeStruct((N_CHIPS,) + a.shape, a.dtype) for a in self.inputs]
        sem, one = pltpu.SemaphoreType.DMA((N_CHIPS - 1, n)), pltpu.SemaphoreType.DMA((n,))
        self.sems = [sem, sem, sem, sem, one, one]

    def _copies(self, src, dst, sems, a):
        send_sems, recv_sems, fsend_sems, frecv_sems, local_sems, own_sems = sems
        p, c = _chip_index(), lax.axis_index("c")
        copy = lambda s, d, **kw: functools.partial(pltpu.make_async_remote_copy, src_ref=s, dst_ref=d, **kw)
        own = copy(src[a], dst[a].at[p], send_sem=local_sems.at[a], recv_sem=own_sems.at[a], device_id=_sibling(),
                   device_id_type=MESH)
        out, arrive, onward, landed = [], [], [], []
        for k in range(1, N_CHIPS):
            ici = dict(send_sem=send_sems.at[k - 1, a], recv_sem=recv_sems.at[k - 1, a], device_id=_peer(k),
                       device_id_type=MESH)
            d2d = dict(send_sem=fsend_sems.at[k - 1, a], recv_sem=frecv_sems.at[k - 1, a], device_id=_sibling(),
                       device_id_type=MESH)
            got, theirs = dst[a].at[p ^ k, c], dst[a].at[p ^ k, 1 - c]
            out.append(copy(src[a].at[c], dst[a].at[p, c], **ici))
            arrive.append(copy(src[a].at[c], got, **ici))
            onward.append(copy(got, got, **d2d))
            landed.append(copy(theirs, theirs, **d2d))
        return own, out, arrive, onward, landed

    def stages(self, steps):
        n = len(self.shards)

        def start(src, dst, sems):
            for a in range(n):
                own, out, _, _, _ = self._copies(src, dst, sems, a)
                own().start()
                for cp in out:
                    cp().start()

        def pass_on(a):
            def act(src, dst, sems):
                _, _, arrive, onward, _ = self._copies(src, dst, sems, a)
                for k in range(N_CHIPS - 1):
                    arrive[k]().wait_recv()
                    onward[k]().start()
            return act

        def finish(src, dst, sems):
            for a in range(n):
                own, out, _, onward, landed = self._copies(src, dst, sems, a)
                for cp in landed:
                    cp().wait_recv()
                for cp in out + onward:
                    cp().wait_send()
                own().wait()

        mids = [(min(steps - 1, int(self.pass_on[a] * steps)), pass_on(a)) for a in range(n)]
        return [(0, start)] + mids + [(steps - 1, finish)]

    def results(self, outs):
        return [o.reshape((N_CHIPS,) + a.shape) for o, a in zip(outs, self.shards)]


class _ExchangeCargo:
    def __init__(self, sums, whole):
        self.inputs, self.whole, n = sums, whole, len(sums)
        self.out_shape = [jax.ShapeDtypeStruct((N_CHIPS - 1,) + g.shape[1:], g.dtype) for g in sums]
        sem = pltpu.SemaphoreType.DMA((N_CHIPS - 1, n))
        self.sems = [sem, sem]

    def _copies(self, src, dst, sems):
        send_sems, recv_sems = sems
        p = _chip_index()
        return [pltpu.make_async_remote_copy(
            src_ref=src[a].at[0] if self.whole[a] else src[a].at[p ^ k], dst_ref=dst[a].at[k - 1],
            send_sem=send_sems.at[k - 1, a], recv_sem=recv_sems.at[k - 1, a], device_id=_peer(k), device_id_type=MESH)
            for k in range(1, N_CHIPS) for a in range(len(self.inputs))]

    def stages(self, steps):
        def start(src, dst, sems):
            for cp in self._copies(src, dst, sems):
                cp.start()

        def finish(src, dst, sems):
            copies = self._copies(src, dst, sems)
            for cp in copies:
                cp.wait_recv()
            for cp in copies:
                cp.wait_send()

        return [(0, start), (steps - 1, finish)]

    def results(self, outs):
        return list(outs)


class _SwapCargo:
    def __init__(self, split):
        self.inputs, n = split, len(split)
        self.out_shape = [jax.ShapeDtypeStruct((g.shape[0],) + g.shape[2:], g.dtype) for g in split]
        sem = pltpu.SemaphoreType.DMA((n,))
        self.sems = [sem, sem]

    def _copies(self, src, dst, sems):
        c = lax.axis_index("c")
        return [pltpu.make_async_remote_copy(src_ref=src[a].at[:, 1 - c], dst_ref=dst[a], send_sem=sems[0].at[a],
                                             recv_sem=sems[1].at[a], device_id=_sibling(), device_id_type=MESH)
                for a in range(len(self.inputs))]

    def stages(self, steps):
        def start(src, dst, sems):
            for cp in self._copies(src, dst, sems):
                cp.start()

        def finish(src, dst, sems):
            copies = self._copies(src, dst, sems)
            for cp in copies:
                cp.wait_recv()
            for cp in copies:
                cp.wait_send()

        return [(0, start), (steps - 1, finish)]

    def results(self, outs):
        return list(outs)


class _MultiCargo:
    def __init__(self, cargos):
        self.cargos = cargos
        self.inputs = [a for cg in cargos for a in cg.inputs]
        self.out_shape = [o for cg in cargos for o in cg.out_shape]
        self.sems = [s for cg in cargos for s in cg.sems]

    def _parts(self, refs, count):
        out, off = [], 0
        for cg in self.cargos:
            out.append(refs[off:off + count(cg)])
            off += count(cg)
        return out

    def stages(self, steps):
        merged = []
        for n, cg in enumerate(self.cargos):
            for at, act in cg.stages(steps):
                def part(src, dst, sems, n=n, act=act):
                    act(self._parts(src, lambda g: len(g.inputs))[n], self._parts(dst, lambda g: len(g.out_shape))[n],
                        self._parts(sems, lambda g: len(g.sems))[n])
                merged.append((at, part))
        starts = [m for m in merged if m[0] == 0]
        return starts + sorted([m for m in merged if m[0] != 0], key=lambda m: m[0])

    def results(self, outs):
        return [cg.results(part) for cg, part in zip(self.cargos, self._parts(list(outs), lambda g: len(g.out_shape)))]


def _cargo_call(body, cargo, *, name, grid, in_specs, out_specs, out_shape, scratch=()):
    n_in, n_out, n_scr = len(in_specs), len(out_specs), len(scratch)
    c_in, c_out = len(cargo.inputs), len(cargo.out_shape)
    steps = 1
    for g in grid:
        steps *= g
    stages = cargo.stages(steps)

    def wrapped(*refs):
        ins, refs = refs[:n_in], refs[n_in:]
        cin, refs = refs[:c_in], refs[c_in:]
        outs, refs = refs[:n_out], refs[n_out:]
        cout, refs = refs[:c_out], refs[c_out:]
        scr, sems = refs[:n_scr], refs[n_scr:]
        if not grid:
            for _, act in stages:
                act(cin, cout, sems)
            return
        step = 0
        for d in range(len(grid)):
            step = step * grid[d] + pl.program_id(d)
        for at, act in stages:
            if at == 0:
                pl.when(step == 0)(functools.partial(act, cin, cout, sems))
        body(*ins, *outs, *scr)
        for at, act in stages:
            if at > 0:
                pl.when(step == at)(functools.partial(act, cin, cout, sems))

    params = dict(vmem_limit_bytes=VMEM_LIMIT)
    if grid:
        params["dimension_semantics"] = ("arbitrary",) * len(grid)
    res = pl.pallas_call(
        wrapped, name=name, grid=grid, in_specs=list(in_specs) + [ANY] * c_in, out_specs=list(out_specs) + [ANY] * c_out,
        out_shape=list(out_shape) + cargo.out_shape, scratch_shapes=list(scratch) + cargo.sems,
        compiler_params=pltpu.CompilerParams(**params))
    return lambda *args: (lambda r: (r[:n_out], cargo.results(r[n_out:])))(res(*args, *cargo.inputs))


def _split_halves(g):
    return g.reshape(g.shape[0], 2, g.shape[1] // 2, g.shape[2])


def _ew_tile(rows):
    return rows if rows <= 384 else 256


def _batched_call(items, name, place=None, cargo=None):
    offs, total = [], 0
    for it in items:
        offs.append(total)
        total += it[0]
    in_specs, out_specs, out_shape, args, spans = [], [], [], [], []
    for (n, ins, outs, shapes, operands, _), off in zip(items, offs):
        tile = lambda t, off=off, n=n: jnp.clip(t - off, 0, n - 1)
        i_s, o_s = ins(tile), outs(tile)
        spans.append((len(i_s), len(o_s)))
        in_specs += i_s
        out_specs += o_s
        out_shape += shapes
        args += operands
    n_in = len(in_specs)

    def body(*refs):
        if place is not None:
            refs = refs[1:]
        t = pl.program_id(0)
        i0, o0 = 0, n_in
        for (n, _, _, _, _, compute), off, (ni, no) in zip(items, offs, spans):
            pl.when((t >= off) & (t < off + n))(functools.partial(compute, refs[i0:i0 + ni], refs[o0:o0 + no]))
            i0, o0 = i0 + ni, o0 + no

    params = pltpu.CompilerParams(dimension_semantics=("arbitrary",), vmem_limit_bytes=VMEM_LIMIT)
    carried = None
    if cargo is not None:
        res, carried = _cargo_call(body, cargo, name=name, grid=(total,), in_specs=in_specs, out_specs=out_specs,
                                   out_shape=out_shape)(*args)
    elif place is None:
        res = pl.pallas_call(body, name=name, grid=(total,), in_specs=in_specs, out_specs=out_specs,
                             out_shape=out_shape, compiler_params=params)(*args)
    else:
        res = pl.pallas_call(
            body, name=name, out_shape=out_shape, compiler_params=params,
            grid_spec=pltpu.PrefetchScalarGridSpec(num_scalar_prefetch=1, grid=(total,), in_specs=in_specs,
                                                   out_specs=out_specs))(place, *args)
    out, o0 = [], 0
    for _, no in spans:
        out.append(res[o0:o0 + no])
        o0 += no
    return out if cargo is None else (out, carried)


def _add_pairs(pairs, place, name):
    def item(mine, theirs):
        lead, _, rows, cols = mine.shape
        tm = _ew_tile(rows)

        def compute(ins, outs):
            outs[0][...] = (ins[0][...].astype(f32) + ins[1][...].astype(f32)).astype(outs[0].dtype)

        blk = lambda tile: pl.BlockSpec((lead, tm, cols), lambda t, pc: (0, tile(t), 0))
        mine_half = lambda tile: pl.BlockSpec((lead, None, tm, cols), lambda t, pc: (0, pc[1], tile(t), 0))
        return (rows // tm, lambda tile: [mine_half(tile), blk(tile)], lambda tile: [blk(tile)],
                [jax.ShapeDtypeStruct(theirs.shape, theirs.dtype)], [mine, theirs], compute)

    return [out[0] for out in _batched_call([item(a, b) for a, b in pairs], name, place)]


def _cargo_alone(cargo, name):
    return _cargo_call(None, cargo, name=name, grid=(), in_specs=[], out_specs=[], out_shape=[])()[1]


def _bf16_casts(x, mats, padded, cargo, name):
    def rows_item(a, tm):
        rows, cols = a.shape

        def compute(ins, outs):
            outs[0][...] = ins[0][...].astype(bf16)

        blk = lambda tile: [pl.BlockSpec((tm, cols), lambda t: (tile(t), 0))]
        return rows // tm, blk, blk, [jax.ShapeDtypeStruct(a.shape, bf16)], [a], compute

    def whole_item(w, to_rows):
        rows, cols = w.shape

        def compute(ins, outs):
            outs[0][0:rows, :] = ins[0][...].astype(bf16)
            if to_rows > rows:
                outs[0][rows:to_rows, :] = jnp.zeros((to_rows - rows, cols), bf16)

        return (1, lambda tile: [pl.BlockSpec((rows, cols), lambda t: (0, 0))],
                lambda tile: [pl.BlockSpec((to_rows, cols), lambda t: (0, 0))],
                [jax.ShapeDtypeStruct((to_rows, cols), bf16)], [w], compute)

    items = [rows_item(x, 256)] + [whole_item(w, r) for w, r in zip(mats, padded)]
    outs, carried = _batched_call(items, name, cargo=cargo)
    return [o[0] for o in outs], carried


def _sum_slots(jobs, place, name):
    def item(own, recv, whole):
        _, rows, cols = own.shape
        tm = _ew_tile(rows)

        def compute(ins, outs):
            r0, r1, r2 = [ins[1][k].astype(f32) for k in range(N_CHIPS - 1)]
            outs[0][...] = (ins[0][...].astype(f32) + r0) + (r1 + r2)

        ins = lambda tile: [pl.BlockSpec((None, tm, cols), lambda t, pc: (0 if whole else pc[0], tile(t), 0)),
                            pl.BlockSpec((N_CHIPS - 1, tm, cols), lambda t, pc: (0, tile(t), 0))]
        outs = lambda tile: [pl.BlockSpec((None, tm, cols), lambda t, pc: (pc[1], tile(t), 0))]
        return rows // tm, ins, outs, [jax.ShapeDtypeStruct((2, rows, cols), f32)], [own, recv], compute

    return [out[0] for out in _batched_call([item(*job) for job in jobs], name, place)]


def _sibling_fill(bufs):
    n = len(bufs)

    def body(*refs):
        src, dst = refs[:n], refs[n:2 * n]
        send_sems, recv_sems = refs[2 * n:]
        c = lax.axis_index("c")
        remote = [pltpu.make_async_remote_copy(src_ref=src[a].at[c], dst_ref=dst[a].at[c], send_sem=send_sems.at[a],
                                               recv_sem=recv_sems.at[a], device_id=_sibling(), device_id_type=MESH)
                  for a in range(n)]
        for cp in remote:
            cp.start()
        for a in range(n):
            pltpu.make_async_remote_copy(src_ref=src[a].at[c], dst_ref=dst[a].at[1 - c], send_sem=send_sems.at[a],
                                         recv_sem=recv_sems.at[a], device_id=_sibling(), device_id_type=MESH).wait_recv()
        for cp in remote:
            cp.wait_send()

    sem = pltpu.SemaphoreType.DMA((n,))
    out = pl.pallas_call(
        body, name="sibling_fill", in_specs=[ANY] * n, out_specs=[ANY] * n,
        out_shape=[jax.ShapeDtypeStruct(b.shape, b.dtype) for b in bufs],
        input_output_aliases={a: a for a in range(n)}, scratch_shapes=[sem, sem])(*bufs)
    return [o.reshape(2 * b.shape[1], b.shape[2]) for o, b in zip(out, bufs)]


class _GradReducer:
    def __init__(self, place):
        self.place, self.split, self.sums, self.slots = place, {}, {}, {}

    def swap(self, named):
        self.split.update({nm: _split_halves(g) for nm, g in named})
        return _SwapCargo([self.split[nm] for nm, _ in named])

    def summed(self, names, got, tag):
        sums = _add_pairs([(self.split[nm], g) for nm, g in zip(names, got)], self.place, "presum_" + tag)
        self.sums.update(zip(names, sums))

    def exchange(self, names):
        return _ExchangeCargo([self.sums[nm] for nm in names], [nm == "small" for nm in names])

    def arrived(self, names, slots):
        self.slots.update(zip(names, slots))

    def reduce(self, names):
        jobs = [(self.sums[nm], self.slots[nm], nm == "small") for nm in names]
        return _sibling_fill(_sum_slots(jobs, self.place, "sum_slots"))


def _adamw(jobs, name):
    def item(w, m, v, grad, tm):
        rows, cols = w.shape
        gcols = grad.shape[1]

        def compute(ins, outs):
            w_, m_, v_ = ins[0][...], ins[1][...], ins[2][...]
            g = ins[3][0:tm, 0:cols]
            m_ = ADAM_B1 * m_ + (1.0 - ADAM_B1) * g
            v_ = ADAM_B2 * v_ + (1.0 - ADAM_B2) * jnp.square(g)
            m_hat = m_ / (1.0 - ADAM_B1 ** ADAM_STEP)
            v_hat = v_ / (1.0 - ADAM_B2 ** ADAM_STEP)
            outs[0][...] = g
            outs[1][...] = -ADAM_LR * (m_hat / (jnp.sqrt(v_hat) + ADAM_EPS) + ADAM_WD * w_)
            outs[2][...] = m_
            outs[3][...] = v_

        blk = lambda tile: pl.BlockSpec((tm, cols), lambda t: (tile(t), 0))
        if tm == rows:
            gspec = lambda tile: pl.BlockSpec(grad.shape, lambda t: (0, 0))
        else:
            gspec = lambda tile: pl.BlockSpec((tm, gcols), lambda t: (tile(t), 0))
        return (rows // tm, lambda tile: [blk(tile)] * 3 + [gspec(tile)], lambda tile: [blk(tile)] * 4,
                [jax.ShapeDtypeStruct((rows, cols), f32)] * 4, [w, m, v, grad], compute)

    return _batched_call([item(*job) for job in jobs], name)


def _rope_tables(s):
    half = HEAD_DIM // 2
    lane = jnp.arange(LANES)
    inv_freq = 1.0 / (10000.0 ** ((lane % half).astype(f32) / half))
    ang = jnp.arange(s, dtype=f32)[:, None] * inv_freq[None, :]
    return jnp.cos(ang), jnp.sin(ang) * jnp.where(lane % HEAD_DIM < half, -1.0, 1.0)[None, :]


def _local_step(x, x_bf, target, win_g, gather, comm, b_gate, w_pool, pool_scale, ln1_g, ln1_b, convb_g, ln2_g, ln2_b):
    s = x.shape[0]
    nb = s // MOBA_BLOCK
    cos_t, sin_t = _rope_tables(s)

    z, (wba_g, wbp_g, wout_g) = _inproj(x_bf, win_g, cos_t, sin_t, b_gate, gather["inproj"])
    wba = wba_g.transpose(1, 0, 2).reshape(D_ATTN, D_MODEL)
    wbp = wbp_g.transpose(1, 0, 2).reshape(D_POOL, D_MODEL)
    wout = wout_g.reshape(D_MODEL, D_MODEL)
    (o_bf, lse, pooled_bf, mixed, pm_bf), (wg_g, wu_g, convw_g) = _attn_fwd(z, nb, w_pool, pool_scale,
                                                                           gather["attn_fwd"])
    m_bf = _branch_merge(o_bf, pm_bf, z, wba, wbp)
    x1_bf, xhat1, rstd1 = _out_ln1(m_bf, wout, x, ln1_g, ln1_b)
    (a, up, ac, hh_bf), (wd_g,) = _ffn_up(x1_bf, wg_g, wu_g, convw_g, convb_g, gather["ffn_up"])
    wd = wd_g.reshape(D_FF_PAD, D_MODEL)
    loss, dh2, dh2_bf, d_ln2_g, d_ln2_b = _ffn_down_ln2_loss(hh_bf, wd, xhat1, ln1_g, ln1_b, ln2_g, ln2_b, target)

    da_bf, dup_bf, dconv = _ffn_act_bwd(dh2_bf, wd_g, ac, a, up, convw_g)
    ffn_shape = (N_CHIPS, FF_PAD, D_MODEL)
    d_wd = _matmul_tn(hh_bf, dh2_bf, 1, "dw_ffn_down").reshape(ffn_shape)
    d_wg, got = _matmul_tn(da_bf, x1_bf, 1, "dw_ffn_gate", cargo=comm.swap([("w_ffn_down", d_wd)]))
    comm.summed(["w_ffn_down"], got, "ffn_down")
    d_wg = d_wg.reshape(ffn_shape)
    d_wu = _matmul_tn(dup_bf, x1_bf, 1, "dw_ffn_up").reshape(ffn_shape)
    gate_up = ["w_ffn_gate", "w_ffn_up", "conv_w"]
    (dh1, dh1_bf, d_ln1_g, d_ln1_b), (slots, got) = _ffn_in_bwd_ln1(
        da_bf, dup_bf, wg_g, wu_g, dh2, xhat1, rstd1, ln1_g,
        _MultiCargo([comm.exchange(["w_ffn_down"]), comm.swap(list(zip(gate_up, [d_wg, d_wu, dconv])))]))
    comm.arrived(["w_ffn_down"], slots)
    comm.summed(gate_up, got, "ffn_gate_up")
    d_wout = _matmul_tn(m_bf, dh1_bf, 1, "dw_out").reshape(N_CHIPS, D_MODEL // N_CHIPS, D_MODEL)
    dzg_bf, dya_bf, dyp_bf, do_bf, dpm, d_bgate = _merge_bwd(dh1_bf, wout, o_bf, pm_bf, z, wba, wbp)
    d_wba = _matmul_tn(o_bf, dya_bf, N_CHIPS, "dw_branch_attn")
    d_wbp = _matmul_tn(pm_bf, dyp_bf, N_CHIPS, "dw_branch_pool")
    du_bf, d_wpool, d_pscale = _pool_bwd(dpm, mixed, pooled_bf, w_pool, pool_scale)
    small = _pack_small([d_bgate, d_wpool, d_pscale, d_ln1_g, d_ln1_b, d_ln2_g, d_ln2_b, dconv[:, 3, :],
                         loss[0:1, :]])[None]
    branch = ["w_out", "w_branch_attn", "w_branch_pool", "small"]
    (dq_bf, dk_bf, dv_bf), (slots, got) = _attn_bwd(
        z, o_bf, lse, do_bf, cos_t, sin_t, nb,
        _MultiCargo([comm.exchange(gate_up), comm.swap(list(zip(branch, [d_wout, d_wba, d_wbp, small])))]))
    comm.arrived(gate_up, slots)
    comm.summed(branch, got, "branch")
    dz_bf = jnp.concatenate([dq_bf, dk_bf, dv_bf, du_bf, dzg_bf], axis=1)
    d_win, slots = _matmul_tn(x_bf, dz_bf, N_CHIPS, "dw_in", cargo=comm.exchange(branch))
    comm.arrived(branch, slots)
    comm.summed(["w_in"], _cargo_alone(comm.swap([("w_in", d_win)]), "swap_w_in"), "w_in")
    (grad_x,), slots = _inproj_dx(dz_bf, win_g, dh1, comm.exchange(["w_in"]))
    comm.arrived(["w_in"], slots)
    return grad_x


SMALL_ROWS = 592


def _pack_small(parts):
    flat = jnp.concatenate([p.reshape(-1) for p in parts])
    return jnp.pad(flat, (0, SMALL_ROWS * LANES - flat.shape[0])).reshape(SMALL_ROWS, LANES)


def _unpack_small(packed, shapes):
    flat = packed.reshape(-1)
    out, off = [], 0
    for shp in shapes:
        n = 1
        for d in shp:
            n *= d
        out.append(flat[off:off + n].reshape(shp))
        off += n
    return out


def _pad_conv_b(cb):
    return jnp.pad(cb.reshape(N_CHIPS, FF_SHARD), ((0, 0), (0, FF_PAD - FF_SHARD)))


def kernel(x, w_in, b_gate, w_branch_attn, w_pool, pool_scale, w_branch_pool, w_out, ln1_g, ln1_b, w_ffn_gate, w_ffn_up, conv_w, conv_b, w_ffn_down, ln2_g, ln2_b, loss_target, m_w_in, m_b_gate, m_w_branch_attn, m_w_pool, m_pool_scale, m_w_branch_pool, m_w_out, m_ln1_g, m_ln1_b, m_w_ffn_gate, m_w_ffn_up, m_conv_w, m_conv_b, m_w_ffn_down, m_ln2_g, m_ln2_b, v_w_in, v_b_gate, v_w_branch_attn, v_w_pool, v_pool_scale, v_w_branch_pool, v_w_out, v_ln1_g, v_ln1_b, v_w_ffn_gate, v_w_ffn_up, v_conv_w, v_conv_b, v_w_ffn_down, v_ln2_g, v_ln2_b):
    mats = [w_branch_attn[0], w_branch_pool[0], w_out[0], w_ffn_gate[0].T, w_ffn_up[0].T, w_ffn_down[0]]
    padded = [w.shape[0] for w in mats[:3]] + [FF_PAD] * 3
    (x_bf, *casted), (win_g,) = _bf16_casts(x[0], mats, padded, _GatherCargo([w_in[0].astype(bf16)], [0.9]),
                                            "gather_w_in")
    shards = [None] + casted + [jnp.pad(conv_w[0], ((0, CONV_ROWS - 3), (0, FF_PAD - FF_SHARD)))]
    gather = {"inproj": _GatherCargo(shards[1:4], [0.3, 0.5, 0.75]),
              "attn_fwd": _GatherCargo(shards[4:6] + shards[7:], [0.97, 0.97, 0.97]),
              "ffn_up": _GatherCargo(shards[6:7], [0.8])}

    place = jnp.stack([2 * lax.axis_index("x") + lax.axis_index("y"), lax.axis_index("c")]).astype(jnp.int32)
    comm = _GradReducer(place)
    convb_g = _pad_conv_b(conv_b).reshape(N_CHIPS, 1, FF_PAD)
    grad_x = _local_step(
        x[0], x_bf, loss_target[0], win_g, gather, comm, b_gate, w_pool[0], pool_scale, ln1_g, ln1_b, convb_g, ln2_g, ln2_b)
    names = ["w_in", "w_branch_attn", "w_branch_pool", "w_out", "w_ffn_gate", "w_ffn_up", "w_ffn_down", "conv_w", "small"]
    grads = comm.reduce(names)

    small_names = ["b_gate", "w_pool", "pool_scale", "ln1_g", "ln1_b", "ln2_g", "ln2_b", "conv_b"]
    small_w = [b_gate, w_pool, pool_scale, ln1_g, ln1_b, ln2_g, ln2_b, _pad_conv_b(conv_b)]
    small_m = [m_b_gate, m_w_pool, m_pool_scale, m_ln1_g, m_ln1_b, m_ln2_g, m_ln2_b, _pad_conv_b(m_conv_b)]
    small_v = [v_b_gate, v_w_pool, v_pool_scale, v_ln1_g, v_ln1_b, v_ln2_g, v_ln2_b, _pad_conv_b(v_conv_b)]
    weights = [w_in[0], w_branch_attn[0], w_branch_pool[0], w_out[0], w_ffn_gate[0].T, w_ffn_up[0].T, w_ffn_down[0],
               conv_w[0], _pack_small(small_w)]
    m_in = [m_w_in[0], m_w_branch_attn[0], m_w_branch_pool[0], m_w_out[0], m_w_ffn_gate[0].T, m_w_ffn_up[0].T,
            m_w_ffn_down[0], m_conv_w[0], _pack_small(small_m)]
    v_in = [v_w_in[0], v_w_branch_attn[0], v_w_branch_pool[0], v_w_out[0], v_w_ffn_gate[0].T, v_w_ffn_up[0].T,
            v_w_ffn_down[0], v_conv_w[0], _pack_small(small_v)]
    tiles = [256, 256, 256, 128, 176, 176, 176, 3, SMALL_ROWS]
    jobs = list(zip(weights, m_in, v_in, grads, tiles))
    ffn = slice(4, 7)
    updated = _adamw(jobs[:4] + jobs[7:], "adamw_rest")
    updated = updated[:4] + _adamw(jobs[ffn], "adamw_ffn") + updated[4:]
    res = {nm: [(r.T if nm in ("w_ffn_gate", "w_ffn_up") else r)[None] for r in outs4]
           for nm, outs4 in zip(names[:-1], updated)}
    packed = updated[-1]
    loss = grads[-1][SMALL_ROWS - 4, 0]
    shapes = [w.shape for w in small_w]
    for kind in range(4):
        for nm, val in zip(small_names, _unpack_small(packed[kind], shapes)):
            if nm == "conv_b":
                val = val[:, :FF_SHARD].reshape(1, 4 * FF_SHARD)
            res.setdefault(nm, [None] * 4)[kind] = val

    order = ["w_in", "b_gate", "w_branch_attn", "w_pool", "pool_scale", "w_branch_pool", "w_out", "ln1_g", "ln1_b",
             "w_ffn_gate", "w_ffn_up", "conv_w", "conv_b", "w_ffn_down", "ln2_g", "ln2_b"]
    outs = [loss, grad_x[None]]
    for kind in range(4):
        outs += [res[nm][kind] for nm in order]
    return tuple(outs)
```

```python
import functools

import jax
import jax.numpy as jnp
from jax import lax
from jax.experimental import pallas as pl
from jax.experimental.pallas import tpu as pltpu

D_MODEL = 1024
HEAD_DIM = 64
D_ATTN = 512
D_POOL = 512
MOBA_BLOCK = 256
MOBA_TOPK = 3
POOL_GROUP = 128
MAX_WINDOW = 16
FF_SHARD = 704
FF_PAD = 768
D_FF_PAD = 4 * FF_PAD
N_CHIPS = 4
LANES = 128
ALPHA = (2.0 * 1) ** 0.25
LN_EPS = 1e-5
NEG = -1e30
SCALE = HEAD_DIM ** -0.5
ADAM_LR, ADAM_B1, ADAM_B2, ADAM_EPS, ADAM_WD, ADAM_STEP = 0.001, 0.9, 0.999, 1e-08, 0.01, 10
VMEM_LIMIT = 56 * 1024 * 1024
MESH = pl.DeviceIdType.MESH

bf16 = jnp.bfloat16
f32 = jnp.float32


def _dot(a, b):
    return jnp.dot(a, b, preferred_element_type=f32)


def _dot_nt(a, b):
    return lax.dot_general(a, b, (((1,), (1,)), ((), ())), preferred_element_type=f32)


def _dot_tn(a, b):
    return lax.dot_general(a, b, (((0,), (0,)), ((), ())), preferred_element_type=f32)


def _call(body, *, name, grid, in_specs, out_specs, out_shape, scratch=(), cargo=None):
    if cargo is not None:
        return _cargo_call(body, cargo, name=name, grid=grid, in_specs=in_specs, out_specs=out_specs,
                           out_shape=out_shape, scratch=scratch)
    return pl.pallas_call(
        body, name=name, grid=grid, in_specs=in_specs, out_specs=out_specs, out_shape=out_shape,
        scratch_shapes=list(scratch),
        compiler_params=pltpu.CompilerParams(dimension_semantics=("arbitrary",) * len(grid),
                                             vmem_limit_bytes=VMEM_LIMIT))


def _rot_half(t):
    lane = lax.broadcasted_iota(jnp.int32, t.shape, 1)
    first = (lane % HEAD_DIM) < (HEAD_DIM // 2)
    return jnp.where(first, pltpu.roll(t, LANES - HEAD_DIM // 2, 1), pltpu.roll(t, HEAD_DIM // 2, 1))


def _rope(t, cos, sin_signed):
    return t * cos + _rot_half(t) * sin_signed


def _rope_bwd(d, cos, sin_signed):
    return d * cos + _rot_half(d * sin_signed)


def _gelu_parts(a):
    cdf = 0.5 * (1.0 + lax.erf(a * (2.0 ** -0.5)))
    pdf = jnp.exp(-0.5 * a * a) * ((2.0 * jnp.pi) ** -0.5)
    return a * cdf, cdf + a * pdf


def _layer_norm(h, g, b):
    mu = jnp.mean(h, axis=-1, keepdims=True)
    xc = h - mu
    var = jnp.mean(xc * xc, axis=-1, keepdims=True)
    rstd = lax.rsqrt(var + LN_EPS)
    xhat = xc * rstd
    return xhat * g + b, xhat, rstd


def _layer_norm_bwd(dy, xhat, rstd, g):
    dxh = dy * g
    m1 = jnp.mean(dxh, axis=-1, keepdims=True)
    m2 = jnp.mean(dxh * xhat, axis=-1, keepdims=True)
    return rstd * (dxh - m1 - xhat * m2)


def _inproj(x_bf, win_g, cos_t, sin_t, b_gate, cargo):
    s = x_bf.shape[0]
    tm, tn = 1024, 512

    def body(x_ref, w_ref, cos_ref, sin_ref, b_ref, o_ref):
        j = pl.program_id(1)
        acc = _dot(x_ref[...], w_ref[...])

        @pl.when(j < 2)
        def _():
            for c in range(tn // LANES):
                sl = slice(c * LANES, (c + 1) * LANES)
                o_ref[:, sl] = _rope(acc[:, sl], cos_ref[...], sin_ref[...])

        @pl.when((j >= 2) & (j < 4))
        def _():
            o_ref[...] = acc

        @pl.when(j >= 4)
        def _():
            o_ref[...] = jax.nn.sigmoid(acc + b_ref[...])

    (z,), carried = _cargo_call(
        body, cargo, name="inproj", grid=(s // tm, 4 * D_MODEL // tn),
        in_specs=[pl.BlockSpec((tm, D_MODEL), lambda i, j: (i, 0)),
                  pl.BlockSpec((None, D_MODEL, tn), lambda i, j: (j // 2, 0, j % 2)),
                  pl.BlockSpec((tm, LANES), lambda i, j: (i, 0)),
                  pl.BlockSpec((tm, LANES), lambda i, j: (i, 0)),
                  pl.BlockSpec((1, tn), lambda i, j: (0, jnp.maximum(j - 4, 0)))],
        out_specs=[pl.BlockSpec((tm, tn), lambda i, j: (i, j))],
        out_shape=[jax.ShapeDtypeStruct((s, 4 * D_MODEL), f32)])(x_bf, win_g, cos_t, sin_t, b_gate)
    return z, carried


STAT_ROWS = 8


def _pair_rows(v0, v1, fill):
    r = lax.broadcasted_iota(jnp.int32, (STAT_ROWS, v0.shape[1]), 0)
    return jnp.where(r == 0, v0, jnp.where(r == 1, v1, fill))


def _head_lanes(shape):
    lane = lax.broadcasted_iota(jnp.int32, shape, 1)
    return lane < HEAD_DIM, lane >= HEAD_DIM


def _head_rows(shape):
    row = lax.broadcasted_iota(jnp.int32, shape, 0)
    return row < HEAD_DIM, row >= HEAD_DIM


def _moba_select(q_bf, k_all, i, nb):
    k_mean = jnp.mean(k_all.reshape(nb, MOBA_BLOCK, LANES), axis=1)
    n_io = lax.broadcasted_iota(jnp.int32, (nb, MOBA_BLOCK), 0)
    past = n_io < i
    sels = []
    for head in _head_lanes((nb, LANES)):
        gate = _dot_nt(jnp.where(head, k_mean, 0.0).astype(bf16), q_bf)
        g = jnp.where(past, gate, NEG)
        rank = jnp.zeros((nb, MOBA_BLOCK), f32)
        for m in range(nb):
            gm = g[m:m + 1, :]
            rank = rank + jnp.where((gm > g) | ((gm == g) & (m < n_io)), 1.0, 0.0)
        sels.append(jnp.where(past & (rank < MOBA_TOPK), 1.0, 0.0))
    return sels


def _attn_fwd(z, nb, w_pool, pool_scale, cargo):
    s = z.shape[0]

    def body(q_ref, k_ref, v_ref, u_ref, wp_ref, sc_ref, o_ref, lse_ref, pooled_ref, mixed_ref, pm_ref, vt_s, pad):
        pair, i = pl.program_id(0), pl.program_id(1)

        @pl.when(i == 0)
        def _():
            for j in range(nb):
                vt_s[j] = v_ref[j * MOBA_BLOCK:(j + 1) * MOBA_BLOCK, :].T.astype(bf16)
            _pool_group(pair, u_ref, wp_ref, sc_ref, pooled_ref, mixed_ref, pm_ref, pad)

        k_heads = _head_lanes((MOBA_BLOCK, LANES))
        o_heads = _head_rows((LANES, MOBA_BLOCK))

        def block(j, qs, allow, m_old, l_old):
            kj, vtj = k_ref[j * MOBA_BLOCK:(j + 1) * MOBA_BLOCK, :], vt_s[j]
            m_new, alpha, l_new, pv = [], [], [], None
            for h in range(2):
                sc = jnp.where(allow[h], _dot_nt(jnp.where(k_heads[h], kj, 0.0).astype(bf16), qs), NEG)
                mx = jnp.max(sc, axis=0, keepdims=True)
                mn = mx if m_old is None else jnp.maximum(m_old[h], mx)
                p = jnp.exp(sc - mn)
                lsum = jnp.sum(p, axis=0, keepdims=True)
                if m_old is None:
                    a = None
                else:
                    a = jnp.exp(m_old[h] - mn)
                    lsum = a * l_old[h] + lsum
                t = _dot(jnp.where(o_heads[h], vtj, jnp.zeros_like(vtj)), p.astype(bf16))
                pv = t if pv is None else pv + t
                m_new.append(mn)
                alpha.append(a)
                l_new.append(lsum)
            return m_new, alpha, l_new, pv

        def query_block(c):
            q = q_ref[...]
            sels = _moba_select(q.astype(bf16), k_ref[...], c, nb)
            qs = (q * SCALE).astype(bf16)
            key = lax.broadcasted_iota(jnp.int32, (MOBA_BLOCK, MOBA_BLOCK), 0)
            qry = lax.broadcasted_iota(jnp.int32, (MOBA_BLOCK, MOBA_BLOCK), 1)
            m, _, l, acc = block(c, qs, [key <= qry] * 2, None, None)
            for j in range(c):
                m, alpha, l, pv = block(j, qs, [sels[h][j:j + 1, :] > 0.0 for h in range(2)], m, l)
                acc = acc * jnp.where(o_heads[0], alpha[0], alpha[1]) + pv
            o_ref[...] = (acc / jnp.where(o_heads[0], l[0], l[1])).T.astype(bf16)
            lse_ref[...] = _pair_rows(m[0] + jnp.log(l[0]), m[1] + jnp.log(l[1]), 0.0)

        for c in range(nb):
            pl.when(i == c)(functools.partial(query_block, c))

    group = pl.BlockSpec((s, POOL_GROUP), lambda hp, i: (0, hp))
    return _cargo_call(
        body, cargo, name="attn_fwd", grid=(D_ATTN // LANES, nb),
        in_specs=[pl.BlockSpec((MOBA_BLOCK, LANES), lambda hp, i: (i, hp)),
                  pl.BlockSpec((s, LANES), lambda hp, i: (0, 4 + hp)),
                  pl.BlockSpec((s, LANES), lambda hp, i: (0, 8 + hp)),
                  pl.BlockSpec((s, POOL_GROUP), lambda hp, i: (0, 12 + hp)),
                  pl.BlockSpec((None, POOL_GROUP, POOL_GROUP), lambda hp, i: (hp, 0, 0)),
                  pl.BlockSpec((1, POOL_GROUP), lambda hp, i: (0, hp))],
        out_specs=[pl.BlockSpec((MOBA_BLOCK, LANES), lambda hp, i: (i, hp)),
                   pl.BlockSpec((None, STAT_ROWS, MOBA_BLOCK), lambda hp, i: (hp, 0, i)), group, group, group],
        out_shape=[jax.ShapeDtypeStruct((s, D_ATTN), bf16),
                   jax.ShapeDtypeStruct((D_ATTN // LANES, STAT_ROWS, s), f32),
                   jax.ShapeDtypeStruct((s, D_POOL), bf16), jax.ShapeDtypeStruct((s, D_POOL), f32),
                   jax.ShapeDtypeStruct((s, D_POOL), bf16)],
        scratch=[pltpu.VMEM((nb, LANES, MOBA_BLOCK), bf16), pltpu.VMEM((s + MAX_WINDOW, POOL_GROUP), f32)],
    )(z, z, z, z, w_pool, pool_scale)


def _window_select(snaps, g):
    return jnp.where(g == 0, snaps[0], jnp.where(g == 1, snaps[1], jnp.where(g == 2, snaps[2], snaps[3])))


def _pool_group(g, u_ref, w_ref, sc_ref, pooled_ref, mixed_ref, pm_ref, pad):
    s = u_ref.shape[0]
    u = u_ref[...]
    pad[0:MAX_WINDOW, :] = jnp.zeros((MAX_WINDOW, POOL_GROUP), f32)
    pad[MAX_WINDOW:MAX_WINDOW + s, :] = u
    acc = u
    snaps = []
    for d in range(1, MAX_WINDOW):
        acc = acc + pad[MAX_WINDOW - d:MAX_WINDOW - d + s, :]
        if d + 1 in (2, 4, 8, 16):
            snaps.append(acc)
    win = _window_select(snaps, g)
    t = lax.broadcasted_iota(jnp.int32, (s, POOL_GROUP), 0)
    count = jnp.minimum(t + 1, jnp.left_shift(2, g)).astype(f32)
    pooled = (win / count - u).astype(bf16)
    mixed = _dot(pooled, w_ref[...].astype(bf16))
    pooled_ref[...] = pooled
    mixed_ref[...] = mixed
    pm_ref[...] = (mixed * sc_ref[...]).astype(bf16)


def _branch_merge(o_bf, pm_bf, z, wba, wbp):
    s = o_bf.shape[0]
    tm = 512

    def body(o_ref, pm_ref, ga_ref, gp_ref, wba_ref, wbp_ref, m_ref):
        ya = _dot(o_ref[...], wba_ref[...])
        yp = _dot(pm_ref[...], wbp_ref[...])
        m_ref[...] = (ga_ref[...] * ya + gp_ref[...] * yp).astype(bf16)

    full = lambda r, c: pl.BlockSpec((r, c), lambda i: (0, 0))
    return _call(
        body, name="branch_merge", grid=(s // tm,),
        in_specs=[pl.BlockSpec((tm, D_ATTN), lambda i: (i, 0)), pl.BlockSpec((tm, D_POOL), lambda i: (i, 0)),
                  pl.BlockSpec((tm, D_MODEL), lambda i: (i, 2)), pl.BlockSpec((tm, D_MODEL), lambda i: (i, 3)),
                  full(D_ATTN, D_MODEL), full(D_POOL, D_MODEL)],
        out_specs=pl.BlockSpec((tm, D_MODEL), lambda i: (i, 0)),
        out_shape=jax.ShapeDtypeStruct((s, D_MODEL), bf16))(o_bf, pm_bf, z, z, wba, wbp)


def _out_ln1(m_bf, wout, x, ln_g, ln_b):
    s = x.shape[0]
    tm = 512

    def body(m_ref, w_ref, x_ref, g_ref, b_ref, x1_ref, xhat_ref, rstd_ref):
        h = ALPHA * x_ref[...] + _dot(m_ref[...], w_ref[...])
        y, xhat, rstd = _layer_norm(h, g_ref[...], b_ref[...])
        x1_ref[...] = y.astype(bf16)
        xhat_ref[...] = xhat
        rstd_ref[...] = jnp.broadcast_to(rstd, (tm, LANES))

    row = pl.BlockSpec((tm, D_MODEL), lambda i: (i, 0))
    vec = pl.BlockSpec((1, D_MODEL), lambda i: (0, 0))
    return _call(
        body, name="out_ln1", grid=(s // tm,),
        in_specs=[row, pl.BlockSpec((D_MODEL, D_MODEL), lambda i: (0, 0)), row, vec, vec],
        out_specs=[row, row, pl.BlockSpec((tm, LANES), lambda i: (i, 0))],
        out_shape=[jax.ShapeDtypeStruct((s, D_MODEL), bf16), jax.ShapeDtypeStruct((s, D_MODEL), f32),
                   jax.ShapeDtypeStruct((s, LANES), f32)])(m_bf, wout, x, ln_g, ln_b)


FF_TILE = 256
FF_TILES_PER_SHARD = FF_PAD // FF_TILE
CONV_PAD = 8
CONV_ROWS = 16

def _ff_weight_spec(rows):
    return pl.BlockSpec((None, rows, FF_TILE), lambda j: (j // FF_TILES_PER_SHARD, 0, j % FF_TILES_PER_SHARD))


def _ff_rows_spec():
    return pl.BlockSpec((None, FF_TILE, D_MODEL), lambda j: (j // FF_TILES_PER_SHARD, j % FF_TILES_PER_SHARD, 0))


def _row_shifts(x, shifts):
    s = x.shape[0]
    padded = jnp.concatenate([x, jnp.zeros((CONV_PAD, x.shape[1]), x.dtype)], axis=0)
    return [pltpu.roll(padded, d % (s + CONV_PAD), 0)[0:s] for d in shifts]


def _ffn_up(x1_bf, wg_g, wu_g, convw_g, convb_g, cargo):
    s = x1_bf.shape[0]

    def body(x_ref, wg_ref, wu_ref, cw_ref, cb_ref, a_ref, up_ref, ac_ref, hh_ref):
        x1 = x_ref[...]
        a = _dot_nt(x1, wg_ref[...])
        up = _dot_nt(x1, wu_ref[...])
        a_ref[...] = a
        up_ref[...] = up
        a1, a2 = _row_shifts(a, (1, 2))
        ac = a * cw_ref[2:3, :] + a1 * cw_ref[1:2, :] + a2 * cw_ref[0:1, :] + cb_ref[...]
        ac_ref[...] = ac
        hg, _ = _gelu_parts(ac)
        hh_ref[...] = (hg * up).astype(bf16)

    col = pl.BlockSpec((s, FF_TILE), lambda j: (0, j))
    wide = jax.ShapeDtypeStruct((s, D_FF_PAD), f32)
    return _call(
        body, name="ffn_up", grid=(D_FF_PAD // FF_TILE,),
        in_specs=[pl.BlockSpec((s, D_MODEL), lambda j: (0, 0)), _ff_rows_spec(), _ff_rows_spec(),
                  _ff_weight_spec(8), _ff_weight_spec(1)],
        out_specs=[col, col, col, col],
        out_shape=[wide, wide, wide, jax.ShapeDtypeStruct((s, D_FF_PAD), bf16)],
        cargo=cargo)(x1_bf, wg_g, wu_g, convw_g, convb_g)


def _ffn_down_ln2_loss(hh_bf, wd, xhat1, ln1_g, ln1_b, ln2_g, ln2_b, target):
    s = hh_bf.shape[0]
    tm = 256

    def body(hh_ref, w_ref, xh1_ref, g1_ref, b1_ref, g2_ref, b2_ref, t_ref, loss_ref, dh_ref, dhb_ref, dg_ref, db_ref):
        i = pl.program_id(0)
        x1 = xh1_ref[...] * g1_ref[...] + b1_ref[...]
        h = ALPHA * x1 + _dot(hh_ref[...], w_ref[...])
        y, xhat, rstd = _layer_norm(h, g2_ref[...], b2_ref[...])
        err = y - t_ref[...]
        part = 0.5 * jnp.sum(jnp.mean(err * err, axis=-1, keepdims=True), axis=0, keepdims=True)
        dy = err * (1.0 / D_MODEL)

        @pl.when(i == 0)
        def _():
            loss_ref[...] = jnp.zeros_like(loss_ref)
            dg_ref[...] = jnp.zeros_like(dg_ref)
            db_ref[...] = jnp.zeros_like(db_ref)

        loss_ref[...] += jnp.broadcast_to(part, loss_ref.shape)
        dg_ref[...] += jnp.sum(dy * xhat, axis=0, keepdims=True)
        db_ref[...] += jnp.sum(dy, axis=0, keepdims=True)
        dh = _layer_norm_bwd(dy, xhat, rstd, g2_ref[...])
        dh_ref[...] = dh
        dhb_ref[...] = dh.astype(bf16)

    row = pl.BlockSpec((tm, D_MODEL), lambda i: (i, 0))
    vec = pl.BlockSpec((1, D_MODEL), lambda i: (0, 0))
    return _call(
        body, name="ffn_down_ln2_loss", grid=(s // tm,),
        in_specs=[pl.BlockSpec((tm, D_FF_PAD), lambda i: (i, 0)),
                  pl.BlockSpec((D_FF_PAD, D_MODEL), lambda i: (0, 0)),
                  row, vec, vec, vec, vec, row],
        out_specs=[pl.BlockSpec((8, LANES), lambda i: (0, 0)), row, row, vec, vec],
        out_shape=[jax.ShapeDtypeStruct((8, LANES), f32), jax.ShapeDtypeStruct((s, D_MODEL), f32),
                   jax.ShapeDtypeStruct((s, D_MODEL), bf16), jax.ShapeDtypeStruct((1, D_MODEL), f32),
                   jax.ShapeDtypeStruct((1, D_MODEL), f32)])(hh_bf, wd, xhat1, ln1_g, ln1_b, ln2_g, ln2_b, target)


def _ffn_act_bwd(dh2_bf, wd_g, ac, a, up, convw_g):
    s = dh2_bf.shape[0]

    def body(dh_ref, wd_ref, ac_ref, a_ref, up_ref, cw_ref, da_ref, dup_ref, dc_ref):
        dhh = _dot_nt(dh_ref[...], wd_ref[...])
        hg, dgelu = _gelu_parts(ac_ref[...])
        dup_ref[...] = (dhh * hg).astype(bf16)
        dac = dhh * up_ref[...] * dgelu
        dac1, dac2 = _row_shifts(dac, (-1, -2))
        a = a_ref[...]
        rows = [jnp.sum(d * a, axis=0, keepdims=True) for d in (dac2, dac1, dac)]
        rows.append(jnp.sum(dac, axis=0, keepdims=True))
        rows.append(jnp.zeros((CONV_ROWS - 4, FF_TILE), f32))
        dc_ref[...] = jnp.concatenate(rows, axis=0)
        da_ref[...] = (dac * cw_ref[2:3, :] + dac1 * cw_ref[1:2, :] + dac2 * cw_ref[0:1, :]).astype(bf16)

    col = pl.BlockSpec((s, FF_TILE), lambda j: (0, j))
    return _call(
        body, name="ffn_act_bwd", grid=(D_FF_PAD // FF_TILE,),
        in_specs=[pl.BlockSpec((s, D_MODEL), lambda j: (0, 0)), _ff_rows_spec(), col, col, col, _ff_weight_spec(8)],
        out_specs=[col, col, _ff_weight_spec(CONV_ROWS)],
        out_shape=[jax.ShapeDtypeStruct((s, D_FF_PAD), bf16), jax.ShapeDtypeStruct((s, D_FF_PAD), bf16),
                   jax.ShapeDtypeStruct((N_CHIPS, CONV_ROWS, FF_PAD), f32)],
    )(dh2_bf, wd_g, ac, a, up, convw_g)


def _matmul_tn(a, b, n_shards, name, tm=512, tn=1024, cargo=None):
    k, m = a.shape
    n = b.shape[1]
    tm, tn = min(tm, m), min(tn, n // n_shards)
    per = n // n_shards // tn

    def body(a_ref, b_ref, o_ref, at_s):
        @pl.when(pl.program_id(1) == 0)
        def _():
            at_s[...] = a_ref[...].T

        o_ref[...] = _dot(at_s[...], b_ref[...]).astype(bf16)

    res = _call(
        body, name=name, grid=(m // tm, n // tn),
        in_specs=[pl.BlockSpec((k, tm), lambda i, j: (0, i)), pl.BlockSpec((k, tn), lambda i, j: (0, j))],
        out_specs=[pl.BlockSpec((None, tm, tn), lambda i, j: (j // per, i, j % per))],
        out_shape=[jax.ShapeDtypeStruct((n_shards, m, n // n_shards), bf16)], scratch=[pltpu.VMEM((tm, k), bf16)],
        cargo=cargo)(a, b)
    return res[0] if cargo is None else (res[0][0], res[1])


def _ffn_in_bwd_ln1(da_bf, dup_bf, wg_g, wu_g, dh2, xhat1, rstd1, ln1_g, cargo):
    s = da_bf.shape[0]
    tm = 256

    def body(da_ref, dup_ref, wg_ref, wu_ref, dh2_ref, xh_ref, rstd_ref, g_ref, dh_ref, dhb_ref, dg_ref, db_ref):
        i = pl.program_id(0)
        dx1 = ALPHA * dh2_ref[...]
        for sh in range(N_CHIPS):
            sl = slice(sh * FF_PAD, (sh + 1) * FF_PAD)
            dx1 = dx1 + _dot(da_ref[:, sl], wg_ref[sh]) + _dot(dup_ref[:, sl], wu_ref[sh])
        xhat = xh_ref[...]

        @pl.when(i == 0)
        def _():
            dg_ref[...] = jnp.zeros_like(dg_ref)
            db_ref[...] = jnp.zeros_like(db_ref)

        dg_ref[...] += jnp.sum(dx1 * xhat, axis=0, keepdims=True)
        db_ref[...] += jnp.sum(dx1, axis=0, keepdims=True)
        dh = _layer_norm_bwd(dx1, xhat, rstd_ref[:, 0:1], g_ref[...])
        dh_ref[...] = dh
        dhb_ref[...] = dh.astype(bf16)

    row = pl.BlockSpec((tm, D_MODEL), lambda i: (i, 0))
    wide = pl.BlockSpec((tm, D_FF_PAD), lambda i: (i, 0))
    wfull = pl.BlockSpec((N_CHIPS, FF_PAD, D_MODEL), lambda i: (0, 0, 0))
    vec = pl.BlockSpec((1, D_MODEL), lambda i: (0, 0))
    return _call(
        body, name="ffn_in_bwd_ln1", grid=(s // tm,),
        in_specs=[wide, wide, wfull, wfull, row, row, pl.BlockSpec((tm, LANES), lambda i: (i, 0)), vec],
        out_specs=[row, row, vec, vec],
        out_shape=[jax.ShapeDtypeStruct((s, D_MODEL), f32), jax.ShapeDtypeStruct((s, D_MODEL), bf16),
                   jax.ShapeDtypeStruct((1, D_MODEL), f32), jax.ShapeDtypeStruct((1, D_MODEL), f32)],
        cargo=cargo)(da_bf, dup_bf, wg_g, wu_g, dh2, xhat1, rstd1, ln1_g)


def _merge_bwd(dh1_bf, wout, o_bf, pm_bf, z, wba, wbp):
    s = dh1_bf.shape[0]
    tm = 256

    def body(dh_ref, wout_ref, o_ref, pm_ref, ga_ref, gp_ref, wba_ref, wbp_ref,
             dzg_ref, dya_ref, dyp_ref, do_ref, dpm_ref, dbg_ref):
        i = pl.program_id(0)
        dm = _dot_nt(dh_ref[...], wout_ref[...])
        ya = _dot(o_ref[...], wba_ref[...])
        yp = _dot(pm_ref[...], wbp_ref[...])
        ga, gp = ga_ref[...], gp_ref[...]
        dza = dm * ya * ga * (1.0 - ga)
        dzp = dm * yp * gp * (1.0 - gp)

        @pl.when(i == 0)
        def _():
            dbg_ref[...] = jnp.zeros_like(dbg_ref)

        dbg_ref[:, 0:D_MODEL] += jnp.sum(dza, axis=0, keepdims=True)
        dbg_ref[:, D_MODEL:2 * D_MODEL] += jnp.sum(dzp, axis=0, keepdims=True)
        dzg_ref[:, 0:D_MODEL] = dza.astype(bf16)
        dzg_ref[:, D_MODEL:2 * D_MODEL] = dzp.astype(bf16)
        dya = (dm * ga).astype(bf16)
        dyp = (dm * gp).astype(bf16)
        dya_ref[...] = dya
        dyp_ref[...] = dyp
        do_ref[...] = _dot_nt(dya, wba_ref[...]).astype(bf16)
        dpm_ref[...] = _dot_nt(dyp, wbp_ref[...])

    row = pl.BlockSpec((tm, D_MODEL), lambda i: (i, 0))
    half = pl.BlockSpec((tm, D_ATTN), lambda i: (i, 0))
    full = lambda r, c: pl.BlockSpec((r, c), lambda i: (0, 0))
    return _call(
        body, name="merge_bwd", grid=(s // tm,),
        in_specs=[row, full(D_MODEL, D_MODEL), half, half,
                  pl.BlockSpec((tm, D_MODEL), lambda i: (i, 2)), pl.BlockSpec((tm, D_MODEL), lambda i: (i, 3)),
                  full(D_ATTN, D_MODEL), full(D_POOL, D_MODEL)],
        out_specs=[pl.BlockSpec((tm, 2 * D_MODEL), lambda i: (i, 0)), row, row, half, half,
                   pl.BlockSpec((1, 2 * D_MODEL), lambda i: (0, 0))],
        out_shape=[jax.ShapeDtypeStruct((s, 2 * D_MODEL), bf16), jax.ShapeDtypeStruct((s, D_MODEL), bf16),
                   jax.ShapeDtypeStruct((s, D_MODEL), bf16), jax.ShapeDtypeStruct((s, D_ATTN), bf16),
                   jax.ShapeDtypeStruct((s, D_POOL), f32), jax.ShapeDtypeStruct((1, 2 * D_MODEL), f32)],
    )(dh1_bf, wout, o_bf, pm_bf, z, z, wba, wbp)


def _pool_bwd(dpm, mixed, pooled_bf, w_pool, pool_scale):
    s = dpm.shape[0]

    def body(dpm_ref, mixed_ref, pooled_ref, w_ref, sc_ref, du_ref, dw_ref, dsc_ref, pad):
        g = pl.program_id(0)
        dpm_v = dpm_ref[...]
        dsc_ref[...] = jnp.sum(dpm_v * mixed_ref[...], axis=0, keepdims=True)
        dmixed = (dpm_v * sc_ref[...]).astype(bf16)
        dw_ref[...] = _dot_tn(pooled_ref[...], dmixed)
        dpooled = _dot_nt(dmixed, w_ref[...].astype(bf16))
        t = lax.broadcasted_iota(jnp.int32, (s, POOL_GROUP), 0)
        count = jnp.minimum(t + 1, jnp.left_shift(2, g)).astype(f32)
        r = dpooled / count
        pad[0:s, :] = r
        pad[s:s + MAX_WINDOW, :] = jnp.zeros((MAX_WINDOW, POOL_GROUP), f32)
        acc = r
        snaps = []
        for d in range(1, MAX_WINDOW):
            acc = acc + pad[d:d + s, :]
            if d + 1 in (2, 4, 8, 16):
                snaps.append(acc)
        du_ref[...] = (_window_select(snaps, g) - dpooled).astype(bf16)

    blk = pl.BlockSpec((s, POOL_GROUP), lambda g: (0, g))
    return _call(
        body, name="pool_bwd", grid=(4,),
        in_specs=[blk, blk, blk, pl.BlockSpec((None, POOL_GROUP, POOL_GROUP), lambda g: (g, 0, 0)),
                  pl.BlockSpec((1, POOL_GROUP), lambda g: (0, g))],
        out_specs=[blk, pl.BlockSpec((None, POOL_GROUP, POOL_GROUP), lambda g: (g, 0, 0)),
                   pl.BlockSpec((1, POOL_GROUP), lambda g: (0, g))],
        out_shape=[jax.ShapeDtypeStruct((s, D_POOL), bf16), jax.ShapeDtypeStruct((4, POOL_GROUP, POOL_GROUP), f32),
                   jax.ShapeDtypeStruct((1, D_POOL), f32)],
        scratch=[pltpu.VMEM((s + MAX_WINDOW, POOL_GROUP), f32)])(dpm, mixed, pooled_bf, w_pool, pool_scale)


def _attn_bwd(z, o_bf, lse, do_bf, cos_t, sin_t, nb, cargo):
    s = z.shape[0]

    def body(q_ref, k_ref, v_ref, o_ref, lse_ref, do_ref, cq_ref, sq_ref, cf_ref, sf_ref,
             dq_ref, dk_ref, dv_ref, kt_s, dk_s, dv_s):
        i = pl.program_id(1)

        @pl.when(i == 0)
        def _():
            dk_s[...] = jnp.zeros_like(dk_s)
            dv_s[...] = jnp.zeros_like(dv_s)
            for j in range(nb):
                kt_s[j] = k_ref[j * MOBA_BLOCK:(j + 1) * MOBA_BLOCK, :].T.astype(bf16)

        k_heads = _head_lanes((MOBA_BLOCK, LANES))
        t_heads = _head_rows((LANES, MOBA_BLOCK))

        def query_block(c):
            q = q_ref[...]
            sels = _moba_select(q.astype(bf16), k_ref[...], c, nb)
            qs = (q * SCALE).astype(bf16)
            qraw = [jnp.where(hm, q, 0.0).astype(bf16) for hm in k_heads]
            do = do_ref[...].astype(f32)
            do_bf = do_ref[...]
            dob = [jnp.where(hm, do, 0.0).astype(bf16) for hm in k_heads]
            pick = _head_lanes((STAT_ROWS, LANES))
            r8 = lax.broadcasted_iota(jnp.int32, (STAT_ROWS, LANES), 0)
            head_pick = jnp.where(((r8 == 0) & pick[0]) | ((r8 == 1) & pick[1]), 1.0, 0.0)
            delta8 = lax.dot_general(head_pick, do * o_ref[...].astype(f32), (((1,), (1,)), ((), ())),
                                     precision=lax.Precision.HIGHEST, preferred_element_type=f32)
            lse8 = lse_ref[...]
            delta, lse_h = [delta8[0:1, :], delta8[1:2, :]], [lse8[0:1, :], lse8[1:2, :]]
            key = lax.broadcasted_iota(jnp.int32, (MOBA_BLOCK, MOBA_BLOCK), 0)
            qry = lax.broadcasted_iota(jnp.int32, (MOBA_BLOCK, MOBA_BLOCK), 1)

            def block(j, allow):
                rows = slice(j * MOBA_BLOCK, (j + 1) * MOBA_BLOCK)
                kj, vj, ktj = k_ref[rows, :], v_ref[rows, :], kt_s[j]
                dqt, dk, dv = None, None, None
                for h in range(2):
                    sc = _dot_nt(jnp.where(k_heads[h], kj, 0.0).astype(bf16), qs)
                    p = jnp.where(allow[h], jnp.exp(sc - lse_h[h]), 0.0)
                    dp = _dot_nt(jnp.where(k_heads[h], vj, 0.0).astype(bf16), do_bf)
                    ds = (p * (dp - delta[h]) * SCALE).astype(bf16)
                    tv = _dot(p.astype(bf16), dob[h])
                    tk = _dot(ds, qraw[h])
                    tq = _dot(jnp.where(t_heads[h], ktj, jnp.zeros_like(ktj)), ds)
                    dqt, dk, dv = (tq, tk, tv) if dqt is None else (dqt + tq, dk + tk, dv + tv)
                dk_s[rows, :] += dk
                dv_s[rows, :] += dv
                return dqt

            dqt = block(c, [key <= qry] * 2)
            for j in range(c):
                dqt = dqt + block(j, [sels[h][j:j + 1, :] > 0.0 for h in range(2)])
            dq_ref[...] = _rope_bwd(dqt.T, cq_ref[...], sq_ref[...]).astype(bf16)

        for c in range(nb):
            pl.when(i == c)(functools.partial(query_block, c))

        @pl.when(i == nb - 1)
        def _():
            dk_ref[...] = _rope_bwd(dk_s[...], cf_ref[...], sf_ref[...]).astype(bf16)
            dv_ref[...] = dv_s[...].astype(bf16)

    qblk = pl.BlockSpec((MOBA_BLOCK, LANES), lambda hp, i: (i, hp))
    tq = pl.BlockSpec((MOBA_BLOCK, LANES), lambda hp, i: (i, 0))
    tf = pl.BlockSpec((s, LANES), lambda hp, i: (0, 0))
    colblk = pl.BlockSpec((s, LANES), lambda hp, i: (0, hp))
    return _cargo_call(
        body, cargo, name="attn_bwd", grid=(D_ATTN // LANES, nb),
        in_specs=[qblk, pl.BlockSpec((s, LANES), lambda hp, i: (0, 4 + hp)),
                  pl.BlockSpec((s, LANES), lambda hp, i: (0, 8 + hp)), qblk,
                  pl.BlockSpec((None, STAT_ROWS, MOBA_BLOCK), lambda hp, i: (hp, 0, i)), qblk, tq, tq, tf, tf],
        out_specs=[qblk, colblk, colblk],
        out_shape=[jax.ShapeDtypeStruct((s, D_ATTN), bf16)] * 3,
        scratch=[pltpu.VMEM((nb, LANES, MOBA_BLOCK), bf16), pltpu.VMEM((s, LANES), f32), pltpu.VMEM((s, LANES), f32)],
    )(z, z, z, o_bf, lse, do_bf, cos_t, sin_t, cos_t, sin_t)


def _inproj_dx(dz_bf, win_g, dh1, cargo):
    s = dz_bf.shape[0]
    tm = 256

    def body(dz_ref, w_ref, dh_ref, gx_ref):
        acc = ALPHA * dh_ref[...]
        for sh in range(N_CHIPS):
            acc = acc + _dot_nt(dz_ref[:, sh * D_MODEL:(sh + 1) * D_MODEL], w_ref[sh])
        gx_ref[...] = acc

    row = pl.BlockSpec((tm, D_MODEL), lambda i: (i, 0))
    return _call(
        body, name="inproj_dx", grid=(s // tm,),
        in_specs=[pl.BlockSpec((tm, 4 * D_MODEL), lambda i: (i, 0)),
                  pl.BlockSpec((N_CHIPS, D_MODEL, D_MODEL), lambda i: (0, 0, 0)), row],
        out_specs=[row], out_shape=[jax.ShapeDtypeStruct((s, D_MODEL), f32)], cargo=cargo)(dz_bf, win_g, dh1)


ANY = pl.BlockSpec(memory_space=pl.ANY)


def _chip_index():
    return 2 * lax.axis_index("x") + lax.axis_index("y")


def _peer(k):
    x, y, c = lax.axis_index("x"), lax.axis_index("y"), lax.axis_index("c")
    return (x ^ (k >> 1), y ^ (k & 1), c)


def _sibling():
    return (lax.axis_index("x"), lax.axis_index("y"), 1 - lax.axis_index("c"))


class _GatherCargo:
    def __init__(self, shards, pass_on):
        self.shards, self.pass_on, n = shards, pass_on, len(shards)
        self.inputs = [a.reshape(2, a.shape[0] // 2, a.shape[1]) for a in shards]
        self.out_shape = [jax.ShapeDtypeStruct((N_CHIPS,) + a.shape, a.dtype) for a in self.inputs]
        sem, one = pltpu.SemaphoreType.DMA((N_CHIPS - 1, n)), pltpu.SemaphoreType.DMA((n,))
        self.sems = [sem, sem, sem, sem, one, one]

    def _copies(self, src, dst, sems, a):
        send_sems, recv_sems, fsend_sems, frecv_sems, local_sems, own_sems = sems
        p, c = _chip_index(), lax.axis_index("c")
        copy = lambda s, d, **kw: functools.partial(pltpu.make_async_remote_copy, src_ref=s, dst_ref=d, **kw)
        own = copy(src[a], dst[a].at[p], send_sem=local_sems.at[a], recv_sem=own_sems.at[a], device_id=_sibling(),
                   device_id_type=MESH)
        out, arrive, onward, landed = [], [], [], []
        for k in range(1, N_CHIPS):
            ici = dict(send_sem=send_sems.at[k - 1, a], recv_sem=recv_sems.at[k - 1, a], device_id=_peer(k),
                       device_id_type=MESH)
            d2d = dict(send_sem=fsend_sems.at[k - 1, a], recv_sem=frecv_sems.at[k - 1, a], device_id=_sibling(),
                       device_id_type=MESH)
            got, theirs = dst[a].at[p ^ k, c], dst[a].at[p ^ k, 1 - c]
            out.append(copy(src[a].at[c], dst[a].at[p, c], **ici))
            arrive.append(copy(src[a].at[c], got, **ici))
            onward.append(copy(got, got, **d2d))
            landed.append(copy(theirs, theirs, **d2d))
        return own, out, arrive, onward, landed

    def stages(self, steps):
        n = len(self.shards)

        def start(src, dst, sems):
            for a in range(n):
                own, out, _, _, _ = self._copies(src, dst, sems, a)
                own().start()
                for cp in out:
                    cp().start()

        def pass_on(a):
            def act(src, dst, sems):
                _, _, arrive, onward, _ = self._copies(src, dst, sems, a)
                for k in range(N_CHIPS - 1):
                    arrive[k]().wait_recv()
                    onward[k]().start()
            return act

        def finish(src, dst, sems):
            for a in range(n):
                own, out, _, onward, landed = self._copies(src, dst, sems, a)
                for cp in landed:
                    cp().wait_recv()
                for cp in out + onward:
                    cp().wait_send()
                own().wait()

        mids = [(min(steps - 1, int(self.pass_on[a] * steps)), pass_on(a)) for a in range(n)]
        return [(0, start)] + mids + [(steps - 1, finish)]

    def results(self, outs):
        return [o.reshape((N_CHIPS,) + a.shape) for o, a in zip(outs, self.shards)]


class _ExchangeCargo:
    def __init__(self, sums, whole):
        self.inputs, self.whole, n = sums, whole, len(sums)
        self.out_shape = [jax.ShapeDtypeStruct((N_CHIPS - 1,) + g.shape[1:], g.dtype) for g in sums]
        sem = pltpu.SemaphoreType.DMA((N_CHIPS - 1, n))
        self.sems = [sem, sem]

    def _copies(self, src, dst, sems):
        send_sems, recv_sems = sems
        p = _chip_index()
        return [pltpu.make_async_remote_copy(
            src_ref=src[a].at[0] if self.whole[a] else src[a].at[p ^ k], dst_ref=dst[a].at[k - 1],
            send_sem=send_sems.at[k - 1, a], recv_sem=recv_sems.at[k - 1, a], device_id=_peer(k), device_id_type=MESH)
            for k in range(1, N_CHIPS) for a in range(len(self.inputs))]

    def stages(self, steps):
        def start(src, dst, sems):
            for cp in self._copies(src, dst, sems):
                cp.start()

        def finish(src, dst, sems):
            copies = self._copies(src, dst, sems)
            for cp in copies:
                cp.wait_recv()
            for cp in copies:
                cp.wait_send()

        return [(0, start), (steps - 1, finish)]

    def results(self, outs):
        return list(outs)


class _SwapCargo:
    def __init__(self, split):
        self.inputs, n = split, len(split)
        self.out_shape = [jax.ShapeDtypeStruct((g.shape[0],) + g.shape[2:], g.dtype) for g in split]
        sem = pltpu.SemaphoreType.DMA((n,))
        self.sems = [sem, sem]

    def _copies(self, src, dst, sems):
        c = lax.axis_index("c")
        return [pltpu.make_async_remote_copy(src_ref=src[a].at[:, 1 - c], dst_ref=dst[a], send_sem=sems[0].at[a],
                                             recv_sem=sems[1].at[a], device_id=_sibling(), device_id_type=MESH)
                for a in range(len(self.inputs))]

    def stages(self, steps):
        def start(src, dst, sems):
            for cp in self._copies(src, dst, sems):
                cp.start()

        def finish(src, dst, sems):
            copies = self._copies(src, dst, sems)
            for cp in copies:
                cp.wait_recv()
            for cp in copies:
                cp.wait_send()

        return [(0, start), (steps - 1, finish)]

    def results(self, outs):
        return list(outs)


class _MultiCargo:
    def __init__(self, cargos):
        self.cargos = cargos
        self.inputs = [a for cg in cargos for a in cg.inputs]
        self.out_shape = [o for cg in cargos for o in cg.out_shape]
        self.sems = [s for cg in cargos for s in cg.sems]

    def _parts(self, refs, count):
        out, off = [], 0
        for cg in self.cargos:
            out.append(refs[off:off + count(cg)])
            off += count(cg)
        return out

    def stages(self, steps):
        merged = []
        for n, cg in enumerate(self.cargos):
            for at, act in cg.stages(steps):
                def part(src, dst, sems, n=n, act=act):
                    act(self._parts(src, lambda g: len(g.inputs))[n], self._parts(dst, lambda g: len(g.out_shape))[n],
                        self._parts(sems, lambda g: len(g.sems))[n])
                merged.append((at, part))
        starts = [m for m in merged if m[0] == 0]
        return starts + sorted([m for m in merged if m[0] != 0], key=lambda m: m[0])

    def results(self, outs):
        return [cg.results(part) for cg, part in zip(self.cargos, self._parts(list(outs), lambda g: len(g.out_shape)))]


def _cargo_call(body, cargo, *, name, grid, in_specs, out_specs, out_shape, scratch=()):
    n_in, n_out, n_scr = len(in_specs), len(out_specs), len(scratch)
    c_in, c_out = len(cargo.inputs), len(cargo.out_shape)
    steps = 1
    for g in grid:
        steps *= g
    stages = cargo.stages(steps)

    def wrapped(*refs):
        ins, refs = refs[:n_in], refs[n_in:]
        cin, refs = refs[:c_in], refs[c_in:]
        outs, refs = refs[:n_out], refs[n_out:]
        cout, refs = refs[:c_out], refs[c_out:]
        scr, sems = refs[:n_scr], refs[n_scr:]
        if not grid:
            for _, act in stages:
                act(cin, cout, sems)
            return
        step = 0
        for d in range(len(grid)):
            step = step * grid[d] + pl.program_id(d)
        for at, act in stages:
            if at == 0:
                pl.when(step == 0)(functools.partial(act, cin, cout, sems))
        body(*ins, *outs, *scr)
        for at, act in stages:
            if at > 0:
                pl.when(step == at)(functools.partial(act, cin, cout, sems))

    params = dict(vmem_limit_bytes=VMEM_LIMIT)
    if grid:
        params["dimension_semantics"] = ("arbitrary",) * len(grid)
    res = pl.pallas_call(
        wrapped, name=name, grid=grid, in_specs=list(in_specs) + [ANY] * c_in, out_specs=list(out_specs) + [ANY] * c_out,
        out_shape=list(out_shape) + cargo.out_shape, scratch_shapes=list(scratch) + cargo.sems,
        compiler_params=pltpu.CompilerParams(**params))
    return lambda *args: (lambda r: (r[:n_out], cargo.results(r[n_out:])))(res(*args, *cargo.inputs))


def _split_halves(g):
    return g.reshape(g.shape[0], 2, g.shape[1] // 2, g.shape[2])


def _ew_tile(rows):
    return rows if rows <= 384 else 256


def _batched_call(items, name, place=None, cargo=None):
    offs, total = [], 0
    for it in items:
        offs.append(total)
        total += it[0]
    in_specs, out_specs, out_shape, args, spans = [], [], [], [], []
    for (n, ins, outs, shapes, operands, _), off in zip(items, offs):
        tile = lambda t, off=off, n=n: jnp.clip(t - off, 0, n - 1)
        i_s, o_s = ins(tile), outs(tile)
        spans.append((len(i_s), len(o_s)))
        in_specs += i_s
        out_specs += o_s
        out_shape += shapes
        args += operands
    n_in = len(in_specs)

    def body(*refs):
        if place is not None:
            refs = refs[1:]
        t = pl.program_id(0)
        i0, o0 = 0, n_in
        for (n, _, _, _, _, compute), off, (ni, no) in zip(items, offs, spans):
            pl.when((t >= off) & (t < off + n))(functools.partial(compute, refs[i0:i0 + ni], refs[o0:o0 + no], t - off))
            i0, o0 = i0 + ni, o0 + no

    params = pltpu.CompilerParams(dimension_semantics=("arbitrary",), vmem_limit_bytes=VMEM_LIMIT)
    carried = None
    if cargo is not None:
        res, carried = _cargo_call(body, cargo, name=name, grid=(total,), in_specs=in_specs, out_specs=out_specs,
                                   out_shape=out_shape)(*args)
    elif place is None:
        res = pl.pallas_call(body, name=name, grid=(total,), in_specs=in_specs, out_specs=out_specs,
                             out_shape=out_shape, compiler_params=params)(*args)
    else:
        res = pl.pallas_call(
            body, name=name, out_shape=out_shape, compiler_params=params,
            grid_spec=pltpu.PrefetchScalarGridSpec(num_scalar_prefetch=1, grid=(total,), in_specs=in_specs,
                                                   out_specs=out_specs))(place, *args)
    out, o0 = [], 0
    for _, no in spans:
        out.append(res[o0:o0 + no])
        o0 += no
    return out if cargo is None else (out, carried)


def _add_pairs(pairs, place, name):
    def item(mine, theirs):
        lead, _, rows, cols = mine.shape
        tm = _ew_tile(rows)

        def compute(ins, outs, tile):
            outs[0][...] = (ins[0][...].astype(f32) + ins[1][...].astype(f32)).astype(outs[0].dtype)

        blk = lambda tile: pl.BlockSpec((lead, tm, cols), lambda t, pc: (0, tile(t), 0))
        mine_half = lambda tile: pl.BlockSpec((lead, None, tm, cols), lambda t, pc: (0, pc[1], tile(t), 0))
        return (rows // tm, lambda tile: [mine_half(tile), blk(tile)], lambda tile: [blk(tile)],
                [jax.ShapeDtypeStruct(theirs.shape, theirs.dtype)], [mine, theirs], compute)

    return [out[0] for out in _batched_call([item(a, b) for a, b in pairs], name, place)]


def _cargo_alone(cargo, name):
    return _cargo_call(None, cargo, name=name, grid=(), in_specs=[], out_specs=[], out_shape=[])()[1]


def _bf16_casts(x, mats, padded, cargo, name):
    def item(a, to_rows, tm):
        rows, cols = a.shape
        full = rows // tm

        def compute(ins, outs, tile):
            outs[0][...] = jnp.where(tile < full, ins[0][...], 0.0).astype(bf16)

        return (to_rows // tm, lambda tile: [pl.BlockSpec((tm, cols), lambda t: (jnp.minimum(tile(t), full - 1), 0))],
                lambda tile: [pl.BlockSpec((tm, cols), lambda t: (tile(t), 0))],
                [jax.ShapeDtypeStruct((to_rows, cols), bf16)], [a], compute)

    items = [item(x, x.shape[0], 128)] + [item(w, r, 64 if r > w.shape[0] else 128) for w, r in zip(mats, padded)]
    outs, carried = _batched_call(items, name, cargo=cargo)
    return [o[0] for o in outs], carried


def _sum_slots(jobs, place, name):
    def item(own, recv, whole):
        _, rows, cols = own.shape
        tm = _ew_tile(rows)

        def compute(ins, outs, tile):
            r0, r1, r2 = [ins[1][k].astype(f32) for k in range(N_CHIPS - 1)]
            outs[0][...] = (ins[0][...].astype(f32) + r0) + (r1 + r2)

        ins = lambda tile: [pl.BlockSpec((None, tm, cols), lambda t, pc: (0 if whole else pc[0], tile(t), 0)),
                            pl.BlockSpec((N_CHIPS - 1, tm, cols), lambda t, pc: (0, tile(t), 0))]
        outs = lambda tile: [pl.BlockSpec((None, tm, cols), lambda t, pc: (pc[1], tile(t), 0))]
        return rows // tm, ins, outs, [jax.ShapeDtypeStruct((2, rows, cols), f32)], [own, recv], compute

    return [out[0] for out in _batched_call([item(*job) for job in jobs], name, place)]


def _sibling_fill(bufs):
    n = len(bufs)

    def body(*refs):
        src, dst = refs[:n], refs[n:2 * n]
        send_sems, recv_sems = refs[2 * n:]
        c = lax.axis_index("c")
        remote = [pltpu.make_async_remote_copy(src_ref=src[a].at[c], dst_ref=dst[a].at[c], send_sem=send_sems.at[a],
                                               recv_sem=recv_sems.at[a], device_id=_sibling(), device_id_type=MESH)
                  for a in range(n)]
        for cp in remote:
            cp.start()
        for a in range(n):
            pltpu.make_async_remote_copy(src_ref=src[a].at[c], dst_ref=dst[a].at[1 - c], send_sem=send_sems.at[a],
                                         recv_sem=recv_sems.at[a], device_id=_sibling(), device_id_type=MESH).wait_recv()
        for cp in remote:
            cp.wait_send()

    sem = pltpu.SemaphoreType.DMA((n,))
    out = pl.pallas_call(
        body, name="sibling_fill", in_specs=[ANY] * n, out_specs=[ANY] * n,
        out_shape=[jax.ShapeDtypeStruct(b.shape, b.dtype) for b in bufs],
        input_output_aliases={a: a for a in range(n)}, scratch_shapes=[sem, sem])(*bufs)
    return [o.reshape(2 * b.shape[1], b.shape[2]) for o, b in zip(out, bufs)]


class _GradReducer:
    def __init__(self, place):
        self.place, self.split, self.sums, self.slots = place, {}, {}, {}

    def swap(self, named):
        self.split.update({nm: _split_halves(g) for nm, g in named})
        return _SwapCargo([self.split[nm] for nm, _ in named])

    def summed(self, names, got, tag):
        sums = _add_pairs([(self.split[nm], g) for nm, g in zip(names, got)], self.place, "presum_" + tag)
        self.sums.update(zip(names, sums))

    def exchange(self, names):
        return _ExchangeCargo([self.sums[nm] for nm in names], [nm == "small" for nm in names])

    def arrived(self, names, slots):
        self.slots.update(zip(names, slots))

    def reduce(self, names):
        jobs = [(self.sums[nm], self.slots[nm], nm == "small") for nm in names]
        return _sibling_fill(_sum_slots(jobs, self.place, "sum_slots"))


def _adamw_update(w, m, v, g):
    m = ADAM_B1 * m + (1.0 - ADAM_B1) * g
    v = ADAM_B2 * v + (1.0 - ADAM_B2) * jnp.square(g)
    m_hat = m / (1.0 - ADAM_B1 ** ADAM_STEP)
    v_hat = v / (1.0 - ADAM_B2 ** ADAM_STEP)
    return -ADAM_LR * (m_hat / (jnp.sqrt(v_hat) + ADAM_EPS) + ADAM_WD * w), m, v


def _adamw(jobs, name):
    def item(w, m, v, grad, tm):
        rows, cols = w.shape
        gcols = grad.shape[1]

        def compute(ins, outs, tile):
            g = ins[3][0:tm, 0:cols]
            outs[0][...] = g
            outs[1][...], outs[2][...], outs[3][...] = _adamw_update(ins[0][...], ins[1][...], ins[2][...], g)

        blk = lambda tile: pl.BlockSpec((tm, cols), lambda t: (tile(t), 0))
        if tm == rows:
            gspec = lambda tile: pl.BlockSpec(grad.shape, lambda t: (0, 0))
        else:
            gspec = lambda tile: pl.BlockSpec((tm, gcols), lambda t: (tile(t), 0))
        return (rows // tm, lambda tile: [blk(tile)] * 3 + [gspec(tile)], lambda tile: [blk(tile)] * 4,
                [jax.ShapeDtypeStruct((rows, cols), f32)] * 4, [w, m, v, grad], compute)

    return _batched_call([item(*job) for job in jobs], name)


def _rope_tables(s):
    half = HEAD_DIM // 2
    inv_freq = 1.0 / (10000.0 ** (jnp.arange(half, dtype=f32) / half))
    ang = jnp.arange(s, dtype=f32)[:, None] * inv_freq[None, :]
    cos, sin = jnp.cos(ang), jnp.sin(ang)
    return jnp.tile(cos, (1, LANES // half)), jnp.tile(jnp.concatenate([-sin, sin], axis=1), (1, LANES // HEAD_DIM))


def _local_step(x, x_bf, target, win_g, gather, comm, b_gate, w_pool, pool_scale, ln1_g, ln1_b, convb_g, ln2_g, ln2_b):
    s = x.shape[0]
    nb = s // MOBA_BLOCK
    cos_t, sin_t = _rope_tables(s)

    z, (wba_g, wbp_g, wout_g) = _inproj(x_bf, win_g, cos_t, sin_t, b_gate, gather["inproj"])
    wba = wba_g.transpose(1, 0, 2).reshape(D_ATTN, D_MODEL)
    wbp = wbp_g.transpose(1, 0, 2).reshape(D_POOL, D_MODEL)
    wout = wout_g.reshape(D_MODEL, D_MODEL)
    (o_bf, lse, pooled_bf, mixed, pm_bf), (wg_g, wu_g, convw_g) = _attn_fwd(z, nb, w_pool, pool_scale,
                                                                           gather["attn_fwd"])
    m_bf = _branch_merge(o_bf, pm_bf, z, wba, wbp)
    x1_bf, xhat1, rstd1 = _out_ln1(m_bf, wout, x, ln1_g, ln1_b)
    (a, up, ac, hh_bf), (wd_g,) = _ffn_up(x1_bf, wg_g, wu_g, convw_g, convb_g, gather["ffn_up"])
    wd = wd_g.reshape(D_FF_PAD, D_MODEL)
    loss, dh2, dh2_bf, d_ln2_g, d_ln2_b = _ffn_down_ln2_loss(hh_bf, wd, xhat1, ln1_g, ln1_b, ln2_g, ln2_b, target)

    da_bf, dup_bf, dconv = _ffn_act_bwd(dh2_bf, wd_g, ac, a, up, convw_g)
    ffn_shape = (N_CHIPS, FF_PAD, D_MODEL)
    d_wd = _matmul_tn(hh_bf, dh2_bf, 1, "dw_ffn_down").reshape(ffn_shape)
    d_wg, got = _matmul_tn(da_bf, x1_bf, 1, "dw_ffn_gate", cargo=comm.swap([("w_ffn_down", d_wd)]))
    comm.summed(["w_ffn_down"], got, "ffn_down")
    d_wg = d_wg.reshape(ffn_shape)
    d_wu = _matmul_tn(dup_bf, x1_bf, 1, "dw_ffn_up").reshape(ffn_shape)
    gate_up = ["w_ffn_gate", "w_ffn_up", "conv_w"]
    (dh1, dh1_bf, d_ln1_g, d_ln1_b), (slots, got) = _ffn_in_bwd_ln1(
        da_bf, dup_bf, wg_g, wu_g, dh2, xhat1, rstd1, ln1_g,
        _MultiCargo([comm.exchange(["w_ffn_down"]), comm.swap(list(zip(gate_up, [d_wg, d_wu, dconv])))]))
    comm.arrived(["w_ffn_down"], slots)
    comm.summed(gate_up, got, "ffn_gate_up")
    d_wout = _matmul_tn(m_bf, dh1_bf, 1, "dw_out").reshape(N_CHIPS, D_MODEL // N_CHIPS, D_MODEL)
    dzg_bf, dya_bf, dyp_bf, do_bf, dpm, d_bgate = _merge_bwd(dh1_bf, wout, o_bf, pm_bf, z, wba, wbp)
    d_wba = _matmul_tn(o_bf, dya_bf, N_CHIPS, "dw_branch_attn")
    d_wbp = _matmul_tn(pm_bf, dyp_bf, N_CHIPS, "dw_branch_pool")
    du_bf, d_wpool, d_pscale = _pool_bwd(dpm, mixed, pooled_bf, w_pool, pool_scale)
    d_convb = dconv[:, 3, :FF_SHARD].reshape(1, N_CHIPS * FF_SHARD)
    small = _pack_small([d_bgate, d_wpool.reshape(-1, POOL_GROUP), d_pscale, d_ln1_g, d_ln1_b, d_ln2_g, d_ln2_b,
                         d_convb], loss)[None]
    branch = ["w_out", "w_branch_attn", "w_branch_pool", "small"]
    (dq_bf, dk_bf, dv_bf), (slots, got) = _attn_bwd(
        z, o_bf, lse, do_bf, cos_t, sin_t, nb,
        _MultiCargo([comm.exchange(gate_up), comm.swap(list(zip(branch, [d_wout, d_wba, d_wbp, small])))]))
    comm.arrived(gate_up, slots)
    comm.summed(branch, got, "branch")
    dz_bf = jnp.concatenate([dq_bf, dk_bf, dv_bf, du_bf, dzg_bf], axis=1)
    d_win, slots = _matmul_tn(x_bf, dz_bf, N_CHIPS, "dw_in", cargo=comm.exchange(branch))
    comm.arrived(branch, slots)
    comm.summed(["w_in"], _cargo_alone(comm.swap([("w_in", d_win)]), "swap_w_in"), "w_in")
    (grad_x,), slots = _inproj_dx(dz_bf, win_g, dh1, comm.exchange(["w_in"]))
    comm.arrived(["w_in"], slots)
    return grad_x


SMALL_WEIGHTS = [("b_gate", 2 * D_MODEL), ("w_pool", 4 * POOL_GROUP * POOL_GROUP), ("pool_scale", D_POOL),
                 ("ln1_g", D_MODEL), ("ln1_b", D_MODEL), ("ln2_g", D_MODEL), ("ln2_b", D_MODEL),
                 ("conv_b", N_CHIPS * FF_SHARD)]
LOSS_ROW = 588
SMALL_ROWS = 592


def _small_pieces(ref, first_row, n):
    if ref.shape[0] > 1:
        return [(slice(first_row, first_row + n // LANES), (slice(None), slice(None)))]
    return [(slice(first_row + i, first_row + i + 1), (slice(None), slice(i * LANES, (i + 1) * LANES)))
            for i in range(n // LANES)]


def _small_first_rows():
    rows, r = [], 0
    for _, n in SMALL_WEIGHTS:
        rows.append(r)
        r += n // LANES
    return rows


def _pack_small(parts, loss):
    def body(*refs):
        out = refs[-1]
        out[...] = jnp.zeros_like(out)
        for src, r0, (_, n) in zip(refs, _small_first_rows(), SMALL_WEIGHTS):
            for rows, where in _small_pieces(src, r0, n):
                out[rows, :] = src[where]
        out[LOSS_ROW:LOSS_ROW + 1, :] = refs[len(parts)][0:1, :]

    return pl.pallas_call(body, name="pack_small", out_shape=jax.ShapeDtypeStruct((SMALL_ROWS, LANES), f32))(
        *parts, loss)


def _adamw_small(g_packed, triples):
    n_in = 3 * len(triples)

    def body(g_ref, *refs):
        for k, (r0, (_, n)) in enumerate(zip(_small_first_rows(), SMALL_WEIGHTS)):
            w_ref, m_ref, v_ref = refs[3 * k:3 * k + 3]
            outs = refs[n_in + 4 * k:n_in + 4 * k + 4]
            for rows, where in _small_pieces(w_ref, r0, n):
                g = g_ref[rows, :]
                for o_ref, val in zip(outs, (g,) + _adamw_update(w_ref[where], m_ref[where], v_ref[where], g)):
                    o_ref[where] = val

    res = pl.pallas_call(
        body, name="adamw_small",
        out_shape=[jax.ShapeDtypeStruct(t[0].shape, f32) for t in triples for _ in range(4)],
    )(g_packed, *[a for t in triples for a in t])
    return [res[4 * k:4 * k + 4] for k in range(len(triples))]


def _pad_conv_b(cb):
    return jnp.pad(cb.reshape(N_CHIPS, FF_SHARD), ((0, 0), (0, FF_PAD - FF_SHARD)))


def kernel(x, w_in, b_gate, w_branch_attn, w_pool, pool_scale, w_branch_pool, w_out, ln1_g, ln1_b, w_ffn_gate, w_ffn_up, conv_w, conv_b, w_ffn_down, ln2_g, ln2_b, loss_target, m_w_in, m_b_gate, m_w_branch_attn, m_w_pool, m_pool_scale, m_w_branch_pool, m_w_out, m_ln1_g, m_ln1_b, m_w_ffn_gate, m_w_ffn_up, m_conv_w, m_conv_b, m_w_ffn_down, m_ln2_g, m_ln2_b, v_w_in, v_b_gate, v_w_branch_attn, v_w_pool, v_pool_scale, v_w_branch_pool, v_w_out, v_ln1_g, v_ln1_b, v_w_ffn_gate, v_w_ffn_up, v_conv_w, v_conv_b, v_w_ffn_down, v_ln2_g, v_ln2_b):
    mats = [w_branch_attn[0], w_branch_pool[0], w_out[0], w_ffn_gate[0].T, w_ffn_up[0].T, w_ffn_down[0]]
    padded = [w.shape[0] for w in mats[:3]] + [FF_PAD] * 3
    (x_bf, *casted), (win_g,) = _bf16_casts(x[0], mats, padded, _GatherCargo([w_in[0].astype(bf16)], [0.9]),
                                            "gather_w_in")
    shards = [None] + casted + [jnp.pad(conv_w[0], ((0, CONV_ROWS - 3), (0, FF_PAD - FF_SHARD)))]
    gather = {"inproj": _GatherCargo(shards[1:4], [0.3, 0.5, 0.75]),
              "attn_fwd": _GatherCargo(shards[4:6] + shards[7:], [0.97, 0.97, 0.97]),
              "ffn_up": _GatherCargo(shards[6:7], [0.8])}

    place = jnp.stack([2 * lax.axis_index("x") + lax.axis_index("y"), lax.axis_index("c")]).astype(jnp.int32)
    comm = _GradReducer(place)
    convb_g = _pad_conv_b(conv_b).reshape(N_CHIPS, 1, FF_PAD)
    grad_x = _local_step(
        x[0], x_bf, loss_target[0], win_g, gather, comm, b_gate, w_pool[0], pool_scale, ln1_g, ln1_b, convb_g, ln2_g, ln2_b)
    names = ["w_in", "w_branch_attn", "w_branch_pool", "w_out", "w_ffn_gate", "w_ffn_up", "w_ffn_down", "conv_w", "small"]
    grads = comm.reduce(names)

    weights = [w_in[0], w_branch_attn[0], w_branch_pool[0], w_out[0], w_ffn_gate[0].T, w_ffn_up[0].T, w_ffn_down[0],
               conv_w[0]]
    m_in = [m_w_in[0], m_w_branch_attn[0], m_w_branch_pool[0], m_w_out[0], m_w_ffn_gate[0].T, m_w_ffn_up[0].T,
            m_w_ffn_down[0], m_conv_w[0]]
    v_in = [v_w_in[0], v_w_branch_attn[0], v_w_branch_pool[0], v_w_out[0], v_w_ffn_gate[0].T, v_w_ffn_up[0].T,
            v_w_ffn_down[0], v_conv_w[0]]
    tiles = [256, 256, 256, 128, 176, 176, 176, 3]
    jobs = list(zip(weights, m_in, v_in, grads, tiles))
    ffn = slice(4, 7)
    updated = _adamw(jobs[:4] + jobs[7:], "adamw_rest")
    updated = updated[:4] + _adamw(jobs[ffn], "adamw_ffn") + updated[4:]
    res = {nm: [(r.T if nm in ("w_ffn_gate", "w_ffn_up") else r)[None] for r in outs4]
           for nm, outs4 in zip(names[:-1], updated)}

    small_w = [b_gate, w_pool, pool_scale, ln1_g, ln1_b, ln2_g, ln2_b, conv_b]
    small_m = [m_b_gate, m_w_pool, m_pool_scale, m_ln1_g, m_ln1_b, m_ln2_g, m_ln2_b, m_conv_b]
    small_v = [v_b_gate, v_w_pool, v_pool_scale, v_ln1_g, v_ln1_b, v_ln2_g, v_ln2_b, v_conv_b]
    as_rows = lambda a: a.reshape(-1, POOL_GROUP) if a.ndim == 4 else a
    small_out = _adamw_small(grads[-1], [tuple(as_rows(a) for a in t) for t in zip(small_w, small_m, small_v)])
    for (nm, _), w, outs4 in zip(SMALL_WEIGHTS, small_w, small_out):
        res[nm] = [o.reshape(w.shape) for o in outs4]
    loss = grads[-1][LOSS_ROW, 0]

    order = ["w_in", "b_gate", "w_branch_attn", "w_pool", "pool_scale", "w_branch_pool", "w_out", "ln1_g", "ln1_b",
             "w_ffn_gate", "w_ffn_up", "conv_w", "conv_b", "w_ffn_down", "ln2_g", "ln2_b"]
    outs = [loss, grad_x[None]]
    for kind in range(4):
        outs += [res[nm][kind] for nm in order]
    return tuple(outs)
```

```python
import functools

import jax
import jax.numpy as jnp
from jax import lax
from jax.experimental import pallas as pl
from jax.experimental.pallas import tpu as pltpu

D_MODEL = 1024
HEAD_DIM = 64
D_ATTN = 512
D_POOL = 512
MOBA_BLOCK = 256
MOBA_TOPK = 3
POOL_GROUP = 128
MAX_WINDOW = 16
FF_SHARD = 704
FF_PAD = 768
D_FF_PAD = 4 * FF_PAD
N_CHIPS = 4
LANES = 128
ALPHA = (2.0 * 1) ** 0.25
LN_EPS = 1e-5
NEG = -1e30
SCALE = HEAD_DIM ** -0.5
ADAM_LR, ADAM_B1, ADAM_B2, ADAM_EPS, ADAM_WD, ADAM_STEP = 0.001, 0.9, 0.999, 1e-08, 0.01, 10
VMEM_LIMIT = 56 * 1024 * 1024
MESH = pl.DeviceIdType.MESH

bf16 = jnp.bfloat16
f32 = jnp.float32


def _dot(a, b):
    return jnp.dot(a, b, preferred_element_type=f32)


def _dot_nt(a, b):
    return lax.dot_general(a, b, (((1,), (1,)), ((), ())), preferred_element_type=f32)


def _dot_tn(a, b):
    return lax.dot_general(a, b, (((0,), (0,)), ((), ())), preferred_element_type=f32)


def _call(body, *, name, grid, in_specs, out_specs, out_shape, scratch=(), cargo=None):
    if cargo is not None:
        return _cargo_call(body, cargo, name=name, grid=grid, in_specs=in_specs, out_specs=out_specs,
                           out_shape=out_shape, scratch=scratch)
    return pl.pallas_call(
        body, name=name, grid=grid, in_specs=in_specs, out_specs=out_specs, out_shape=out_shape,
        scratch_shapes=list(scratch),
        compiler_params=pltpu.CompilerParams(dimension_semantics=("arbitrary",) * len(grid),
                                             vmem_limit_bytes=VMEM_LIMIT))


def _rot_half(t):
    lane = lax.broadcasted_iota(jnp.int32, t.shape, 1)
    first = (lane % HEAD_DIM) < (HEAD_DIM // 2)
    return jnp.where(first, pltpu.roll(t, LANES - HEAD_DIM // 2, 1), pltpu.roll(t, HEAD_DIM // 2, 1))


def _rope(t, cos, sin_signed):
    return t * cos + _rot_half(t) * sin_signed


def _rope_bwd(d, cos, sin_signed):
    return d * cos + _rot_half(d * sin_signed)


def _gelu_parts(a):
    cdf = 0.5 * (1.0 + lax.erf(a * (2.0 ** -0.5)))
    pdf = jnp.exp(-0.5 * a * a) * ((2.0 * jnp.pi) ** -0.5)
    return a * cdf, cdf + a * pdf


def _layer_norm(h, g, b):
    mu = jnp.mean(h, axis=-1, keepdims=True)
    xc = h - mu
    var = jnp.mean(xc * xc, axis=-1, keepdims=True)
    rstd = lax.rsqrt(var + LN_EPS)
    xhat = xc * rstd
    return xhat * g + b, xhat, rstd


def _layer_norm_bwd(dy, xhat, rstd, g):
    dxh = dy * g
    m1 = jnp.mean(dxh, axis=-1, keepdims=True)
    m2 = jnp.mean(dxh * xhat, axis=-1, keepdims=True)
    return rstd * (dxh - m1 - xhat * m2)


def _inproj(x_bf, win_g, cos_t, sin_t, b_gate, cargo):
    s = x_bf.shape[0]
    tm, tn = 1024, 512

    def body(x_ref, w_ref, cos_ref, sin_ref, b_ref, o_ref):
        j = pl.program_id(1)
        acc = _dot(x_ref[...], w_ref[...])

        @pl.when(j < 2)
        def _():
            for c in range(tn // LANES):
                sl = slice(c * LANES, (c + 1) * LANES)
                o_ref[:, sl] = _rope(acc[:, sl], cos_ref[...], sin_ref[...])

        @pl.when((j >= 2) & (j < 4))
        def _():
            o_ref[...] = acc

        @pl.when(j >= 4)
        def _():
            o_ref[...] = jax.nn.sigmoid(acc + b_ref[...])

    (z,), carried = _cargo_call(
        body, cargo, name="inproj", grid=(s // tm, 4 * D_MODEL // tn),
        in_specs=[pl.BlockSpec((tm, D_MODEL), lambda i, j: (i, 0)),
                  pl.BlockSpec((None, D_MODEL, tn), lambda i, j: (j // 2, 0, j % 2)),
                  pl.BlockSpec((tm, LANES), lambda i, j: (i, 0)),
                  pl.BlockSpec((tm, LANES), lambda i, j: (i, 0)),
                  pl.BlockSpec((1, tn), lambda i, j: (0, jnp.maximum(j - 4, 0)))],
        out_specs=[pl.BlockSpec((tm, tn), lambda i, j: (i, j))],
        out_shape=[jax.ShapeDtypeStruct((s, 4 * D_MODEL), f32)])(x_bf, win_g, cos_t, sin_t, b_gate)
    return z, carried


STAT_ROWS = 8


def _pair_rows(v0, v1, fill):
    r = lax.broadcasted_iota(jnp.int32, (STAT_ROWS, v0.shape[1]), 0)
    return jnp.where(r == 0, v0, jnp.where(r == 1, v1, fill))


def _head_lanes(shape):
    lane = lax.broadcasted_iota(jnp.int32, shape, 1)
    return lane < HEAD_DIM, lane >= HEAD_DIM


def _head_rows(shape):
    row = lax.broadcasted_iota(jnp.int32, shape, 0)
    return row < HEAD_DIM, row >= HEAD_DIM


def _moba_select(q_bf, k_all, i, nb):
    k_mean = jnp.mean(k_all.reshape(nb, MOBA_BLOCK, LANES), axis=1)
    n_io = lax.broadcasted_iota(jnp.int32, (nb, MOBA_BLOCK), 0)
    past = n_io < i
    sels = []
    for head in _head_lanes((nb, LANES)):
        gate = _dot_nt(jnp.where(head, k_mean, 0.0).astype(bf16), q_bf)
        g = jnp.where(past, gate, NEG)
        rank = jnp.zeros((nb, MOBA_BLOCK), f32)
        for m in range(nb):
            gm = g[m:m + 1, :]
            rank = rank + jnp.where((gm > g) | ((gm == g) & (m < n_io)), 1.0, 0.0)
        sels.append(jnp.where(past & (rank < MOBA_TOPK), 1.0, 0.0))
    return sels


def _attn_fwd(z, nb, w_pool, pool_scale, cargo):
    s = z.shape[0]

    def body(q_ref, k_ref, v_ref, u_ref, wp_ref, sc_ref, o_ref, lse_ref, pooled_ref, mixed_ref, pm_ref, vt_s, pad):
        pair, i = pl.program_id(0), pl.program_id(1)

        @pl.when(i == 0)
        def _():
            for j in range(nb):
                vt_s[j] = v_ref[j * MOBA_BLOCK:(j + 1) * MOBA_BLOCK, :].T.astype(bf16)
            _pool_group(pair, u_ref, wp_ref, sc_ref, pooled_ref, mixed_ref, pm_ref, pad)

        k_heads = _head_lanes((MOBA_BLOCK, LANES))
        o_heads = _head_rows((LANES, MOBA_BLOCK))

        def block(j, qs, allow, m_old, l_old):
            kj, vtj = k_ref[j * MOBA_BLOCK:(j + 1) * MOBA_BLOCK, :], vt_s[j]
            m_new, alpha, l_new, pv = [], [], [], None
            for h in range(2):
                sc = jnp.where(allow[h], _dot_nt(jnp.where(k_heads[h], kj, 0.0).astype(bf16), qs), NEG)
                mx = jnp.max(sc, axis=0, keepdims=True)
                mn = mx if m_old is None else jnp.maximum(m_old[h], mx)
                p = jnp.exp(sc - mn)
                lsum = jnp.sum(p, axis=0, keepdims=True)
                if m_old is None:
                    a = None
                else:
                    a = jnp.exp(m_old[h] - mn)
                    lsum = a * l_old[h] + lsum
                t = _dot(jnp.where(o_heads[h], vtj, jnp.zeros_like(vtj)), p.astype(bf16))
                pv = t if pv is None else pv + t
                m_new.append(mn)
                alpha.append(a)
                l_new.append(lsum)
            return m_new, alpha, l_new, pv

        def query_block(c):
            q = q_ref[...]
            sels = _moba_select(q.astype(bf16), k_ref[...], c, nb)
            qs = (q * SCALE).astype(bf16)
            key = lax.broadcasted_iota(jnp.int32, (MOBA_BLOCK, MOBA_BLOCK), 0)
            qry = lax.broadcasted_iota(jnp.int32, (MOBA_BLOCK, MOBA_BLOCK), 1)
            m, _, l, acc = block(c, qs, [key <= qry] * 2, None, None)
            for j in range(c):
                m, alpha, l, pv = block(j, qs, [sels[h][j:j + 1, :] > 0.0 for h in range(2)], m, l)
                acc = acc * jnp.where(o_heads[0], alpha[0], alpha[1]) + pv
            o_ref[...] = (acc / jnp.where(o_heads[0], l[0], l[1])).T.astype(bf16)
            lse_ref[...] = _pair_rows(m[0] + jnp.log(l[0]), m[1] + jnp.log(l[1]), 0.0)

        for c in range(nb):
            pl.when(i == c)(functools.partial(query_block, c))

    group = pl.BlockSpec((s, POOL_GROUP), lambda hp, i: (0, hp))
    return _cargo_call(
        body, cargo, name="attn_fwd", grid=(D_ATTN // LANES, nb),
        in_specs=[pl.BlockSpec((MOBA_BLOCK, LANES), lambda hp, i: (i, hp)),
                  pl.BlockSpec((s, LANES), lambda hp, i: (0, 4 + hp)),
                  pl.BlockSpec((s, LANES), lambda hp, i: (0, 8 + hp)),
                  pl.BlockSpec((s, POOL_GROUP), lambda hp, i: (0, 12 + hp)),
                  pl.BlockSpec((None, POOL_GROUP, POOL_GROUP), lambda hp, i: (hp, 0, 0)),
                  pl.BlockSpec((1, POOL_GROUP), lambda hp, i: (0, hp))],
        out_specs=[pl.BlockSpec((MOBA_BLOCK, LANES), lambda hp, i: (i, hp)),
                   pl.BlockSpec((None, STAT_ROWS, MOBA_BLOCK), lambda hp, i: (hp, 0, i)), group, group, group],
        out_shape=[jax.ShapeDtypeStruct((s, D_ATTN), bf16),
                   jax.ShapeDtypeStruct((D_ATTN // LANES, STAT_ROWS, s), f32),
                   jax.ShapeDtypeStruct((s, D_POOL), bf16), jax.ShapeDtypeStruct((s, D_POOL), f32),
                   jax.ShapeDtypeStruct((s, D_POOL), bf16)],
        scratch=[pltpu.VMEM((nb, LANES, MOBA_BLOCK), bf16), pltpu.VMEM((s + MAX_WINDOW, POOL_GROUP), f32)],
    )(z, z, z, z, w_pool, pool_scale)


def _window_select(snaps, g):
    return jnp.where(g == 0, snaps[0], jnp.where(g == 1, snaps[1], jnp.where(g == 2, snaps[2], snaps[3])))


def _pool_group(g, u_ref, w_ref, sc_ref, pooled_ref, mixed_ref, pm_ref, pad):
    s = u_ref.shape[0]
    u = u_ref[...]
    pad[0:MAX_WINDOW, :] = jnp.zeros((MAX_WINDOW, POOL_GROUP), f32)
    pad[MAX_WINDOW:MAX_WINDOW + s, :] = u
    acc = u
    snaps = []
    for d in range(1, MAX_WINDOW):
        acc = acc + pad[MAX_WINDOW - d:MAX_WINDOW - d + s, :]
        if d + 1 in (2, 4, 8, 16):
            snaps.append(acc)
    win = _window_select(snaps, g)
    t = lax.broadcasted_iota(jnp.int32, (s, POOL_GROUP), 0)
    count = jnp.minimum(t + 1, jnp.left_shift(2, g)).astype(f32)
    pooled = (win / count - u).astype(bf16)
    mixed = _dot(pooled, w_ref[...].astype(bf16))
    pooled_ref[...] = pooled
    mixed_ref[...] = mixed
    pm_ref[...] = (mixed * sc_ref[...]).astype(bf16)


def _branch_merge(o_bf, pm_bf, z, wba, wbp):
    s = o_bf.shape[0]
    tm = 512

    def body(o_ref, pm_ref, ga_ref, gp_ref, wba_ref, wbp_ref, m_ref):
        ya = _dot(o_ref[...], wba_ref[...])
        yp = _dot(pm_ref[...], wbp_ref[...])
        m_ref[...] = (ga_ref[...] * ya + gp_ref[...] * yp).astype(bf16)

    full = lambda r, c: pl.BlockSpec((r, c), lambda i: (0, 0))
    return _call(
        body, name="branch_merge", grid=(s // tm,),
        in_specs=[pl.BlockSpec((tm, D_ATTN), lambda i: (i, 0)), pl.BlockSpec((tm, D_POOL), lambda i: (i, 0)),
                  pl.BlockSpec((tm, D_MODEL), lambda i: (i, 2)), pl.BlockSpec((tm, D_MODEL), lambda i: (i, 3)),
                  full(D_ATTN, D_MODEL), full(D_POOL, D_MODEL)],
        out_specs=pl.BlockSpec((tm, D_MODEL), lambda i: (i, 0)),
        out_shape=jax.ShapeDtypeStruct((s, D_MODEL), bf16))(o_bf, pm_bf, z, z, wba, wbp)


def _out_ln1(m_bf, wout, x, ln_g, ln_b):
    s = x.shape[0]
    tm = 512

    def body(m_ref, w_ref, x_ref, g_ref, b_ref, x1_ref, xhat_ref, rstd_ref):
        h = ALPHA * x_ref[...] + _dot(m_ref[...], w_ref[...])
        y, xhat, rstd = _layer_norm(h, g_ref[...], b_ref[...])
        x1_ref[...] = y.astype(bf16)
        xhat_ref[...] = xhat
        rstd_ref[...] = jnp.broadcast_to(rstd, (tm, LANES))

    row = pl.BlockSpec((tm, D_MODEL), lambda i: (i, 0))
    vec = pl.BlockSpec((1, D_MODEL), lambda i: (0, 0))
    return _call(
        body, name="out_ln1", grid=(s // tm,),
        in_specs=[row, pl.BlockSpec((D_MODEL, D_MODEL), lambda i: (0, 0)), row, vec, vec],
        out_specs=[row, row, pl.BlockSpec((tm, LANES), lambda i: (i, 0))],
        out_shape=[jax.ShapeDtypeStruct((s, D_MODEL), bf16), jax.ShapeDtypeStruct((s, D_MODEL), f32),
                   jax.ShapeDtypeStruct((s, LANES), f32)])(m_bf, wout, x, ln_g, ln_b)


FF_TILE = 256
FF_TILES_PER_SHARD = FF_PAD // FF_TILE
CONV_PAD = 8
CONV_ROWS = 16

def _ff_weight_spec(rows):
    return pl.BlockSpec((None, rows, FF_TILE), lambda j: (j // FF_TILES_PER_SHARD, 0, j % FF_TILES_PER_SHARD))


def _ff_rows_spec():
    return pl.BlockSpec((None, FF_TILE, D_MODEL), lambda j: (j // FF_TILES_PER_SHARD, j % FF_TILES_PER_SHARD, 0))


def _row_shifts(x, shifts):
    s = x.shape[0]
    padded = jnp.concatenate([x, jnp.zeros((CONV_PAD, x.shape[1]), x.dtype)], axis=0)
    return [pltpu.roll(padded, d % (s + CONV_PAD), 0)[0:s] for d in shifts]


def _ffn_up(x1_bf, wg_g, wu_g, convw_g, convb_g, cargo):
    s = x1_bf.shape[0]

    def body(x_ref, wg_ref, wu_ref, cw_ref, cb_ref, a_ref, up_ref, ac_ref, hh_ref):
        x1 = x_ref[...]
        a = _dot_nt(x1, wg_ref[...])
        up = _dot_nt(x1, wu_ref[...])
        a_ref[...] = a
        up_ref[...] = up
        a1, a2 = _row_shifts(a, (1, 2))
        ac = a * cw_ref[2:3, :] + a1 * cw_ref[1:2, :] + a2 * cw_ref[0:1, :] + cb_ref[...]
        ac_ref[...] = ac
        hg, _ = _gelu_parts(ac)
        hh_ref[...] = (hg * up).astype(bf16)

    col = pl.BlockSpec((s, FF_TILE), lambda j: (0, j))
    wide = jax.ShapeDtypeStruct((s, D_FF_PAD), f32)
    return _call(
        body, name="ffn_up", grid=(D_FF_PAD // FF_TILE,),
        in_specs=[pl.BlockSpec((s, D_MODEL), lambda j: (0, 0)), _ff_rows_spec(), _ff_rows_spec(),
                  _ff_weight_spec(8), _ff_weight_spec(1)],
        out_specs=[col, col, col, col],
        out_shape=[wide, wide, wide, jax.ShapeDtypeStruct((s, D_FF_PAD), bf16)],
        cargo=cargo)(x1_bf, wg_g, wu_g, convw_g, convb_g)


def _ffn_down_ln2_loss(hh_bf, wd, xhat1, ln1_g, ln1_b, ln2_g, ln2_b, target):
    s = hh_bf.shape[0]
    tm = 256

    def body(hh_ref, w_ref, xh1_ref, g1_ref, b1_ref, g2_ref, b2_ref, t_ref, loss_ref, dh_ref, dhb_ref, dg_ref, db_ref):
        i = pl.program_id(0)
        x1 = xh1_ref[...] * g1_ref[...] + b1_ref[...]
        h = ALPHA * x1 + _dot(hh_ref[...], w_ref[...])
        y, xhat, rstd = _layer_norm(h, g2_ref[...], b2_ref[...])
        err = y - t_ref[...]
        part = 0.5 * jnp.sum(jnp.mean(err * err, axis=-1, keepdims=True), axis=0, keepdims=True)
        dy = err * (1.0 / D_MODEL)

        @pl.when(i == 0)
        def _():
            loss_ref[...] = jnp.zeros_like(loss_ref)
            dg_ref[...] = jnp.zeros_like(dg_ref)
            db_ref[...] = jnp.zeros_like(db_ref)

        loss_ref[...] += jnp.broadcast_to(part, loss_ref.shape)
        dg_ref[...] += jnp.sum(dy * xhat, axis=0, keepdims=True)
        db_ref[...] += jnp.sum(dy, axis=0, keepdims=True)
        dh = _layer_norm_bwd(dy, xhat, rstd, g2_ref[...])
        dh_ref[...] = dh
        dhb_ref[...] = dh.astype(bf16)

    row = pl.BlockSpec((tm, D_MODEL), lambda i: (i, 0))
    vec = pl.BlockSpec((1, D_MODEL), lambda i: (0, 0))
    return _call(
        body, name="ffn_down_ln2_loss", grid=(s // tm,),
        in_specs=[pl.BlockSpec((tm, D_FF_PAD), lambda i: (i, 0)),
                  pl.BlockSpec((D_FF_PAD, D_MODEL), lambda i: (0, 0)),
                  row, vec, vec, vec, vec, row],
        out_specs=[pl.BlockSpec((8, LANES), lambda i: (0, 0)), row, row, vec, vec],
        out_shape=[jax.ShapeDtypeStruct((8, LANES), f32), jax.ShapeDtypeStruct((s, D_MODEL), f32),
                   jax.ShapeDtypeStruct((s, D_MODEL), bf16), jax.ShapeDtypeStruct((1, D_MODEL), f32),
                   jax.ShapeDtypeStruct((1, D_MODEL), f32)])(hh_bf, wd, xhat1, ln1_g, ln1_b, ln2_g, ln2_b, target)


def _ffn_act_bwd(dh2_bf, wd_g, ac, a, up, convw_g):
    s = dh2_bf.shape[0]

    def body(dh_ref, wd_ref, ac_ref, a_ref, up_ref, cw_ref, da_ref, dup_ref, dc_ref):
        dhh = _dot_nt(dh_ref[...], wd_ref[...])
        hg, dgelu = _gelu_parts(ac_ref[...])
        dup_ref[...] = (dhh * hg).astype(bf16)
        dac = dhh * up_ref[...] * dgelu
        dac1, dac2 = _row_shifts(dac, (-1, -2))
        a = a_ref[...]
        rows = [jnp.sum(d * a, axis=0, keepdims=True) for d in (dac2, dac1, dac)]
        rows.append(jnp.sum(dac, axis=0, keepdims=True))
        rows.append(jnp.zeros((CONV_ROWS - 4, FF_TILE), f32))
        dc_ref[...] = jnp.concatenate(rows, axis=0)
        da_ref[...] = (dac * cw_ref[2:3, :] + dac1 * cw_ref[1:2, :] + dac2 * cw_ref[0:1, :]).astype(bf16)

    col = pl.BlockSpec((s, FF_TILE), lambda j: (0, j))
    return _call(
        body, name="ffn_act_bwd", grid=(D_FF_PAD // FF_TILE,),
        in_specs=[pl.BlockSpec((s, D_MODEL), lambda j: (0, 0)), _ff_rows_spec(), col, col, col, _ff_weight_spec(8)],
        out_specs=[col, col, _ff_weight_spec(CONV_ROWS)],
        out_shape=[jax.ShapeDtypeStruct((s, D_FF_PAD), bf16), jax.ShapeDtypeStruct((s, D_FF_PAD), bf16),
                   jax.ShapeDtypeStruct((N_CHIPS, CONV_ROWS, FF_PAD), f32)],
    )(dh2_bf, wd_g, ac, a, up, convw_g)


def _matmul_tn(a, b, n_shards, name, tm=512, tn=1024, cargo=None):
    k, m = a.shape
    n = b.shape[1]
    tm, tn = min(tm, m), min(tn, n // n_shards)
    per = n // n_shards // tn

    def body(a_ref, b_ref, o_ref, at_s):
        @pl.when(pl.program_id(1) == 0)
        def _():
            at_s[...] = a_ref[...].T

        o_ref[...] = _dot(at_s[...], b_ref[...]).astype(bf16)

    res = _call(
        body, name=name, grid=(m // tm, n // tn),
        in_specs=[pl.BlockSpec((k, tm), lambda i, j: (0, i)), pl.BlockSpec((k, tn), lambda i, j: (0, j))],
        out_specs=[pl.BlockSpec((None, tm, tn), lambda i, j: (j // per, i, j % per))],
        out_shape=[jax.ShapeDtypeStruct((n_shards, m, n // n_shards), bf16)], scratch=[pltpu.VMEM((tm, k), bf16)],
        cargo=cargo)(a, b)
    return res[0] if cargo is None else (res[0][0], res[1])


def _dw_in(x_bf, dqkv, du, dzg, cargo):
    k, m = x_bf.shape
    tm, half = 512, D_ATTN

    def body(a_ref, dqkv_ref, du_ref, dzg_ref, o_ref, at_s):
        j = pl.program_id(1)

        @pl.when(j == 0)
        def _():
            at_s[...] = a_ref[...].T
            o_ref[:, 0:half] = _dot(at_s[...], dqkv_ref[0]).astype(bf16)
            o_ref[:, half:2 * half] = _dot(at_s[...], dqkv_ref[1]).astype(bf16)

        @pl.when(j == 1)
        def _():
            o_ref[:, 0:half] = _dot(at_s[...], dqkv_ref[2]).astype(bf16)
            o_ref[:, half:2 * half] = _dot(at_s[...], du_ref[...]).astype(bf16)

        @pl.when(j >= 2)
        def _():
            o_ref[...] = _dot(at_s[...], dzg_ref[...]).astype(bf16)

    (d_win,), carried = _cargo_call(
        body, cargo, name="dw_in", grid=(m // tm, N_CHIPS),
        in_specs=[pl.BlockSpec((k, tm), lambda i, j: (0, i)), pl.BlockSpec((3, k, half), lambda i, j: (0, 0, 0)),
                  pl.BlockSpec((k, half), lambda i, j: (0, 0)),
                  pl.BlockSpec((k, D_MODEL), lambda i, j: (0, jnp.maximum(j - 2, 0)))],
        out_specs=[pl.BlockSpec((None, tm, D_MODEL), lambda i, j: (j, i, 0))],
        out_shape=[jax.ShapeDtypeStruct((N_CHIPS, m, D_MODEL), bf16)], scratch=[pltpu.VMEM((tm, k), bf16)],
    )(x_bf, dqkv, du, dzg)
    return d_win, carried


def _ffn_in_bwd_ln1(da_bf, dup_bf, wg_g, wu_g, dh2, xhat1, rstd1, ln1_g, cargo):
    s = da_bf.shape[0]
    tm = 256

    def body(da_ref, dup_ref, wg_ref, wu_ref, dh2_ref, xh_ref, rstd_ref, g_ref, dh_ref, dhb_ref, dg_ref, db_ref):
        i = pl.program_id(0)
        dx1 = ALPHA * dh2_ref[...]
        for sh in range(N_CHIPS):
            sl = slice(sh * FF_PAD, (sh + 1) * FF_PAD)
            dx1 = dx1 + _dot(da_ref[:, sl], wg_ref[sh]) + _dot(dup_ref[:, sl], wu_ref[sh])
        xhat = xh_ref[...]

        @pl.when(i == 0)
        def _():
            dg_ref[...] = jnp.zeros_like(dg_ref)
            db_ref[...] = jnp.zeros_like(db_ref)

        dg_ref[...] += jnp.sum(dx1 * xhat, axis=0, keepdims=True)
        db_ref[...] += jnp.sum(dx1, axis=0, keepdims=True)
        dh = _layer_norm_bwd(dx1, xhat, rstd_ref[:, 0:1], g_ref[...])
        dh_ref[...] = dh
        dhb_ref[...] = dh.astype(bf16)

    row = pl.BlockSpec((tm, D_MODEL), lambda i: (i, 0))
    wide = pl.BlockSpec((tm, D_FF_PAD), lambda i: (i, 0))
    wfull = pl.BlockSpec((N_CHIPS, FF_PAD, D_MODEL), lambda i: (0, 0, 0))
    vec = pl.BlockSpec((1, D_MODEL), lambda i: (0, 0))
    return _call(
        body, name="ffn_in_bwd_ln1", grid=(s // tm,),
        in_specs=[wide, wide, wfull, wfull, row, row, pl.BlockSpec((tm, LANES), lambda i: (i, 0)), vec],
        out_specs=[row, row, vec, vec],
        out_shape=[jax.ShapeDtypeStruct((s, D_MODEL), f32), jax.ShapeDtypeStruct((s, D_MODEL), bf16),
                   jax.ShapeDtypeStruct((1, D_MODEL), f32), jax.ShapeDtypeStruct((1, D_MODEL), f32)],
        cargo=cargo)(da_bf, dup_bf, wg_g, wu_g, dh2, xhat1, rstd1, ln1_g)


def _merge_bwd(dh1_bf, wout, o_bf, pm_bf, z, wba, wbp):
    s = dh1_bf.shape[0]
    tm = 256

    def body(dh_ref, wout_ref, o_ref, pm_ref, ga_ref, gp_ref, wba_ref, wbp_ref,
             dzg_ref, dya_ref, dyp_ref, do_ref, dpm_ref, dbg_ref):
        i = pl.program_id(0)
        dm = _dot_nt(dh_ref[...], wout_ref[...])
        ya = _dot(o_ref[...], wba_ref[...])
        yp = _dot(pm_ref[...], wbp_ref[...])
        ga, gp = ga_ref[...], gp_ref[...]
        dza = dm * ya * ga * (1.0 - ga)
        dzp = dm * yp * gp * (1.0 - gp)

        @pl.when(i == 0)
        def _():
            dbg_ref[...] = jnp.zeros_like(dbg_ref)

        dbg_ref[:, 0:D_MODEL] += jnp.sum(dza, axis=0, keepdims=True)
        dbg_ref[:, D_MODEL:2 * D_MODEL] += jnp.sum(dzp, axis=0, keepdims=True)
        dzg_ref[:, 0:D_MODEL] = dza.astype(bf16)
        dzg_ref[:, D_MODEL:2 * D_MODEL] = dzp.astype(bf16)
        dya = (dm * ga).astype(bf16)
        dyp = (dm * gp).astype(bf16)
        dya_ref[...] = dya
        dyp_ref[...] = dyp
        do_ref[...] = _dot_nt(dya, wba_ref[...]).astype(bf16)
        dpm_ref[...] = _dot_nt(dyp, wbp_ref[...])

    row = pl.BlockSpec((tm, D_MODEL), lambda i: (i, 0))
    half = pl.BlockSpec((tm, D_ATTN), lambda i: (i, 0))
    full = lambda r, c: pl.BlockSpec((r, c), lambda i: (0, 0))
    return _call(
        body, name="merge_bwd", grid=(s // tm,),
        in_specs=[row, full(D_MODEL, D_MODEL), half, half,
                  pl.BlockSpec((tm, D_MODEL), lambda i: (i, 2)), pl.BlockSpec((tm, D_MODEL), lambda i: (i, 3)),
                  full(D_ATTN, D_MODEL), full(D_POOL, D_MODEL)],
        out_specs=[pl.BlockSpec((tm, 2 * D_MODEL), lambda i: (i, 0)), row, row, half, half,
                   pl.BlockSpec((1, 2 * D_MODEL), lambda i: (0, 0))],
        out_shape=[jax.ShapeDtypeStruct((s, 2 * D_MODEL), bf16), jax.ShapeDtypeStruct((s, D_MODEL), bf16),
                   jax.ShapeDtypeStruct((s, D_MODEL), bf16), jax.ShapeDtypeStruct((s, D_ATTN), bf16),
                   jax.ShapeDtypeStruct((s, D_POOL), f32), jax.ShapeDtypeStruct((1, 2 * D_MODEL), f32)],
    )(dh1_bf, wout, o_bf, pm_bf, z, z, wba, wbp)


def _pool_bwd(dpm, mixed, pooled_bf, w_pool, pool_scale):
    s = dpm.shape[0]

    def body(dpm_ref, mixed_ref, pooled_ref, w_ref, sc_ref, du_ref, dw_ref, dsc_ref, pad):
        g = pl.program_id(0)
        dpm_v = dpm_ref[...]
        dsc_ref[...] = jnp.sum(dpm_v * mixed_ref[...], axis=0, keepdims=True)
        dmixed = (dpm_v * sc_ref[...]).astype(bf16)
        dw_ref[...] = _dot_tn(pooled_ref[...], dmixed)
        dpooled = _dot_nt(dmixed, w_ref[...].astype(bf16))
        t = lax.broadcasted_iota(jnp.int32, (s, POOL_GROUP), 0)
        count = jnp.minimum(t + 1, jnp.left_shift(2, g)).astype(f32)
        r = dpooled / count
        pad[0:s, :] = r
        pad[s:s + MAX_WINDOW, :] = jnp.zeros((MAX_WINDOW, POOL_GROUP), f32)
        acc = r
        snaps = []
        for d in range(1, MAX_WINDOW):
            acc = acc + pad[d:d + s, :]
            if d + 1 in (2, 4, 8, 16):
                snaps.append(acc)
        du_ref[...] = (_window_select(snaps, g) - dpooled).astype(bf16)

    blk = pl.BlockSpec((s, POOL_GROUP), lambda g: (0, g))
    return _call(
        body, name="pool_bwd", grid=(4,),
        in_specs=[blk, blk, blk, pl.BlockSpec((None, POOL_GROUP, POOL_GROUP), lambda g: (g, 0, 0)),
                  pl.BlockSpec((1, POOL_GROUP), lambda g: (0, g))],
        out_specs=[blk, pl.BlockSpec((None, POOL_GROUP, POOL_GROUP), lambda g: (g, 0, 0)),
                   pl.BlockSpec((1, POOL_GROUP), lambda g: (0, g))],
        out_shape=[jax.ShapeDtypeStruct((s, D_POOL), bf16), jax.ShapeDtypeStruct((4, POOL_GROUP, POOL_GROUP), f32),
                   jax.ShapeDtypeStruct((1, D_POOL), f32)],
        scratch=[pltpu.VMEM((s + MAX_WINDOW, POOL_GROUP), f32)])(dpm, mixed, pooled_bf, w_pool, pool_scale)


def _attn_bwd(z, o_bf, lse, do_bf, cos_t, sin_t, nb, cargo):
    s = z.shape[0]

    def body(q_ref, k_ref, v_ref, o_ref, lse_ref, do_ref, cq_ref, sq_ref, cf_ref, sf_ref,
             dqkv_ref, kt_s, dk_s, dv_s):
        i = pl.program_id(1)

        @pl.when(i == 0)
        def _():
            dk_s[...] = jnp.zeros_like(dk_s)
            dv_s[...] = jnp.zeros_like(dv_s)
            for j in range(nb):
                kt_s[j] = k_ref[j * MOBA_BLOCK:(j + 1) * MOBA_BLOCK, :].T.astype(bf16)

        k_heads = _head_lanes((MOBA_BLOCK, LANES))
        t_heads = _head_rows((LANES, MOBA_BLOCK))

        def query_block(c):
            q = q_ref[...]
            sels = _moba_select(q.astype(bf16), k_ref[...], c, nb)
            qs = (q * SCALE).astype(bf16)
            qraw = [jnp.where(hm, q, 0.0).astype(bf16) for hm in k_heads]
            do = do_ref[...].astype(f32)
            do_bf = do_ref[...]
            dob = [jnp.where(hm, do, 0.0).astype(bf16) for hm in k_heads]
            pick = _head_lanes((STAT_ROWS, LANES))
            r8 = lax.broadcasted_iota(jnp.int32, (STAT_ROWS, LANES), 0)
            head_pick = jnp.where(((r8 == 0) & pick[0]) | ((r8 == 1) & pick[1]), 1.0, 0.0)
            delta8 = lax.dot_general(head_pick, do * o_ref[...].astype(f32), (((1,), (1,)), ((), ())),
                                     precision=lax.Precision.HIGHEST, preferred_element_type=f32)
            lse8 = lse_ref[...]
            delta, lse_h = [delta8[0:1, :], delta8[1:2, :]], [lse8[0:1, :], lse8[1:2, :]]
            key = lax.broadcasted_iota(jnp.int32, (MOBA_BLOCK, MOBA_BLOCK), 0)
            qry = lax.broadcasted_iota(jnp.int32, (MOBA_BLOCK, MOBA_BLOCK), 1)

            def block(j, allow):
                rows = slice(j * MOBA_BLOCK, (j + 1) * MOBA_BLOCK)
                kj, vj, ktj = k_ref[rows, :], v_ref[rows, :], kt_s[j]
                dqt, dk, dv = None, None, None
                for h in range(2):
                    sc = _dot_nt(jnp.where(k_heads[h], kj, 0.0).astype(bf16), qs)
                    p = jnp.where(allow[h], jnp.exp(sc - lse_h[h]), 0.0)
                    dp = _dot_nt(jnp.where(k_heads[h], vj, 0.0).astype(bf16), do_bf)
                    ds = (p * (dp - delta[h]) * SCALE).astype(bf16)
                    tv = _dot(p.astype(bf16), dob[h])
                    tk = _dot(ds, qraw[h])
                    tq = _dot(jnp.where(t_heads[h], ktj, jnp.zeros_like(ktj)), ds)
                    dqt, dk, dv = (tq, tk, tv) if dqt is None else (dqt + tq, dk + tk, dv + tv)
                dk_s[rows, :] += dk
                dv_s[rows, :] += dv
                return dqt

            dqt = block(c, [key <= qry] * 2)
            for j in range(c):
                dqt = dqt + block(j, [sels[h][j:j + 1, :] > 0.0 for h in range(2)])
            dqkv_ref[0, c * MOBA_BLOCK:(c + 1) * MOBA_BLOCK, :] = _rope_bwd(dqt.T, cq_ref[...], sq_ref[...]).astype(bf16)

        for c in range(nb):
            pl.when(i == c)(functools.partial(query_block, c))

        @pl.when(i == nb - 1)
        def _():
            dqkv_ref[1] = _rope_bwd(dk_s[...], cf_ref[...], sf_ref[...]).astype(bf16)
            dqkv_ref[2] = dv_s[...].astype(bf16)

    qblk = pl.BlockSpec((MOBA_BLOCK, LANES), lambda hp, i: (i, hp))
    tq = pl.BlockSpec((MOBA_BLOCK, LANES), lambda hp, i: (i, 0))
    tf = pl.BlockSpec((s, LANES), lambda hp, i: (0, 0))
    return _cargo_call(
        body, cargo, name="attn_bwd", grid=(D_ATTN // LANES, nb),
        in_specs=[qblk, pl.BlockSpec((s, LANES), lambda hp, i: (0, 4 + hp)),
                  pl.BlockSpec((s, LANES), lambda hp, i: (0, 8 + hp)), qblk,
                  pl.BlockSpec((None, STAT_ROWS, MOBA_BLOCK), lambda hp, i: (hp, 0, i)), qblk, tq, tq, tf, tf],
        out_specs=[pl.BlockSpec((3, s, LANES), lambda hp, i: (0, 0, hp))],
        out_shape=[jax.ShapeDtypeStruct((3, s, D_ATTN), bf16)],
        scratch=[pltpu.VMEM((nb, LANES, MOBA_BLOCK), bf16), pltpu.VMEM((s, LANES), f32), pltpu.VMEM((s, LANES), f32)],
    )(z, z, z, o_bf, lse, do_bf, cos_t, sin_t, cos_t, sin_t)


def _inproj_dx(dqkv, du, dzg, win_g, dh1, cargo):
    s = du.shape[0]
    tm, half = 256, D_ATTN

    def body(dqkv_ref, du_ref, dzg_ref, w_ref, dh_ref, gx_ref):
        acc = ALPHA * dh_ref[...]
        for n, part in enumerate([dqkv_ref[0], dqkv_ref[1], dqkv_ref[2], du_ref[...]]):
            acc = acc + _dot_nt(part, w_ref[n // 2, :, (n % 2) * half:(n % 2 + 1) * half])
        for sh in range(2, N_CHIPS):
            acc = acc + _dot_nt(dzg_ref[:, (sh - 2) * D_MODEL:(sh - 1) * D_MODEL], w_ref[sh])
        gx_ref[...] = acc

    row = pl.BlockSpec((tm, D_MODEL), lambda i: (i, 0))
    return _call(
        body, name="inproj_dx", grid=(s // tm,),
        in_specs=[pl.BlockSpec((3, tm, half), lambda i: (0, i, 0)), pl.BlockSpec((tm, half), lambda i: (i, 0)),
                  pl.BlockSpec((tm, 2 * D_MODEL), lambda i: (i, 0)),
                  pl.BlockSpec((N_CHIPS, D_MODEL, D_MODEL), lambda i: (0, 0, 0)), row],
        out_specs=[row], out_shape=[jax.ShapeDtypeStruct((s, D_MODEL), f32)], cargo=cargo)(dqkv, du, dzg, win_g, dh1)


ANY = pl.BlockSpec(memory_space=pl.ANY)


def _chip_index():
    return 2 * lax.axis_index("x") + lax.axis_index("y")


def _peer(k):
    x, y, c = lax.axis_index("x"), lax.axis_index("y"), lax.axis_index("c")
    return (x ^ (k >> 1), y ^ (k & 1), c)


def _sibling():
    return (lax.axis_index("x"), lax.axis_index("y"), 1 - lax.axis_index("c"))


class _GatherCargo:
    def __init__(self, shards, pass_on):
        self.shards, self.pass_on, n = shards, pass_on, len(shards)
        self.inputs = [a.reshape(2, a.shape[0] // 2, a.shape[1]) for a in shards]
        self.out_shape = [jax.ShapeDtypeStruct((N_CHIPS,) + a.shape, a.dtype) for a in self.inputs]
        sem, one = pltpu.SemaphoreType.DMA((N_CHIPS - 1, n)), pltpu.SemaphoreType.DMA((n,))
        self.sems = [sem, sem, sem, sem, one, one]

    def _copies(self, src, dst, sems, a):
        send_sems, recv_sems, fsend_sems, frecv_sems, local_sems, own_sems = sems
        p, c = _chip_index(), lax.axis_index("c")
        copy = lambda s, d, **kw: functools.partial(pltpu.make_async_remote_copy, src_ref=s, dst_ref=d, **kw)
        own = copy(src[a], dst[a].at[p], send_sem=local_sems.at[a], recv_sem=own_sems.at[a], device_id=_sibling(),
                   device_id_type=MESH)
        out, arrive, onward, landed = [], [], [], []
        for k in range(1, N_CHIPS):
            ici = dict(send_sem=send_sems.at[k - 1, a], recv_sem=recv_sems.at[k - 1, a], device_id=_peer(k),
                       device_id_type=MESH)
            d2d = dict(send_sem=fsend_sems.at[k - 1, a], recv_sem=frecv_sems.at[k - 1, a], device_id=_sibling(),
                       device_id_type=MESH)
            got, theirs = dst[a].at[p ^ k, c], dst[a].at[p ^ k, 1 - c]
            out.append(copy(src[a].at[c], dst[a].at[p, c], **ici))
            arrive.append(copy(src[a].at[c], got, **ici))
            onward.append(copy(got, got, **d2d))
            landed.append(copy(theirs, theirs, **d2d))
        return own, out, arrive, onward, landed

    def stages(self, steps):
        n = len(self.shards)

        def start(src, dst, sems):
            for a in range(n):
                own, out, _, _, _ = self._copies(src, dst, sems, a)
                own().start()
                for cp in out:
                    cp().start()

        def pass_on(a):
            def act(src, dst, sems):
                _, _, arrive, onward, _ = self._copies(src, dst, sems, a)
                for k in range(N_CHIPS - 1):
                    arrive[k]().wait_recv()
                    onward[k]().start()
            return act

        def finish(src, dst, sems):
            for a in range(n):
                own, out, _, onward, landed = self._copies(src, dst, sems, a)
                for cp in landed:
                    cp().wait_recv()
                for cp in out + onward:
                    cp().wait_send()
                own().wait()

        mids = [(min(steps - 1, int(self.pass_on[a] * steps)), pass_on(a)) for a in range(n)]
        return [(0, start)] + mids + [(steps - 1, finish)]

    def results(self, outs):
        return [o.reshape((N_CHIPS,) + a.shape) for o, a in zip(outs, self.shards)]


class _ExchangeCargo:
    def __init__(self, sums, whole):
        self.inputs, self.whole, n = sums, whole, len(sums)
        self.out_shape = [jax.ShapeDtypeStruct((N_CHIPS - 1,) + g.shape[1:], g.dtype) for g in sums]
        sem = pltpu.SemaphoreType.DMA((N_CHIPS - 1, n))
        self.sems = [sem, sem]

    def _copies(self, src, dst, sems):
        send_sems, recv_sems = sems
        p = _chip_index()
        return [pltpu.make_async_remote_copy(
            src_ref=src[a].at[0] if self.whole[a] else src[a].at[p ^ k], dst_ref=dst[a].at[k - 1],
            send_sem=send_sems.at[k - 1, a], recv_sem=recv_sems.at[k - 1, a], device_id=_peer(k), device_id_type=MESH)
            for k in range(1, N_CHIPS) for a in range(len(self.inputs))]

    def stages(self, steps):
        def start(src, dst, sems):
            for cp in self._copies(src, dst, sems):
                cp.start()

        def finish(src, dst, sems):
            copies = self._copies(src, dst, sems)
            for cp in copies:
                cp.wait_recv()
            for cp in copies:
                cp.wait_send()

        return [(0, start), (steps - 1, finish)]

    def results(self, outs):
        return list(outs)


class _SwapCargo:
    def __init__(self, split):
        self.inputs, n = split, len(split)
        self.out_shape = [jax.ShapeDtypeStruct((g.shape[0],) + g.shape[2:], g.dtype) for g in split]
        sem = pltpu.SemaphoreType.DMA((n,))
        self.sems = [sem, sem]

    def _copies(self, src, dst, sems):
        c = lax.axis_index("c")
        return [pltpu.make_async_remote_copy(src_ref=src[a].at[:, 1 - c], dst_ref=dst[a], send_sem=sems[0].at[a],
                                             recv_sem=sems[1].at[a], device_id=_sibling(), device_id_type=MESH)
                for a in range(len(self.inputs))]

    def stages(self, steps):
        def start(src, dst, sems):
            for cp in self._copies(src, dst, sems):
                cp.start()

        def finish(src, dst, sems):
            copies = self._copies(src, dst, sems)
            for cp in copies:
                cp.wait_recv()
            for cp in copies:
                cp.wait_send()

        return [(0, start), (steps - 1, finish)]

    def results(self, outs):
        return list(outs)


class _MultiCargo:
    def __init__(self, cargos):
        self.cargos = cargos
        self.inputs = [a for cg in cargos for a in cg.inputs]
        self.out_shape = [o for cg in cargos for o in cg.out_shape]
        self.sems = [s for cg in cargos for s in cg.sems]

    def _parts(self, refs, count):
        out, off = [], 0
        for cg in self.cargos:
            out.append(refs[off:off + count(cg)])
            off += count(cg)
        return out

    def stages(self, steps):
        merged = []
        for n, cg in enumerate(self.cargos):
            for at, act in cg.stages(steps):
                def part(src, dst, sems, n=n, act=act):
                    act(self._parts(src, lambda g: len(g.inputs))[n], self._parts(dst, lambda g: len(g.out_shape))[n],
                        self._parts(sems, lambda g: len(g.sems))[n])
                merged.append((at, part))
        starts = [m for m in merged if m[0] == 0]
        return starts + sorted([m for m in merged if m[0] != 0], key=lambda m: m[0])

    def results(self, outs):
        return [cg.results(part) for cg, part in zip(self.cargos, self._parts(list(outs), lambda g: len(g.out_shape)))]


def _cargo_call(body, cargo, *, name, grid, in_specs, out_specs, out_shape, scratch=()):
    n_in, n_out, n_scr = len(in_specs), len(out_specs), len(scratch)
    c_in, c_out = len(cargo.inputs), len(cargo.out_shape)
    steps = 1
    for g in grid:
        steps *= g
    stages = cargo.stages(steps)

    def wrapped(*refs):
        ins, refs = refs[:n_in], refs[n_in:]
        cin, refs = refs[:c_in], refs[c_in:]
        outs, refs = refs[:n_out], refs[n_out:]
        cout, refs = refs[:c_out], refs[c_out:]
        scr, sems = refs[:n_scr], refs[n_scr:]
        if not grid:
            for _, act in stages:
                act(cin, cout, sems)
            return
        step = 0
        for d in range(len(grid)):
            step = step * grid[d] + pl.program_id(d)
        for at, act in stages:
            if at == 0:
                pl.when(step == 0)(functools.partial(act, cin, cout, sems))
        body(*ins, *outs, *scr)
        for at, act in stages:
            if at > 0:
                pl.when(step == at)(functools.partial(act, cin, cout, sems))

    params = dict(vmem_limit_bytes=VMEM_LIMIT)
    if grid:
        params["dimension_semantics"] = ("arbitrary",) * len(grid)
    res = pl.pallas_call(
        wrapped, name=name, grid=grid, in_specs=list(in_specs) + [ANY] * c_in, out_specs=list(out_specs) + [ANY] * c_out,
        out_shape=list(out_shape) + cargo.out_shape, scratch_shapes=list(scratch) + cargo.sems,
        compiler_params=pltpu.CompilerParams(**params))
    return lambda *args: (lambda r: (r[:n_out], cargo.results(r[n_out:])))(res(*args, *cargo.inputs))


def _split_halves(g):
    return g.reshape(g.shape[0], 2, g.shape[1] // 2, g.shape[2])


def _ew_tile(rows):
    return rows if rows <= 384 else 256


def _batched_call(items, name, place=None, cargo=None):
    offs, total = [], 0
    for it in items:
        offs.append(total)
        total += it[0]
    in_specs, out_specs, out_shape, args, spans = [], [], [], [], []
    for (n, ins, outs, shapes, operands, _), off in zip(items, offs):
        tile = lambda t, off=off, n=n: jnp.clip(t - off, 0, n - 1)
        i_s, o_s = ins(tile), outs(tile)
        spans.append((len(i_s), len(o_s)))
        in_specs += i_s
        out_specs += o_s
        out_shape += shapes
        args += operands
    n_in = len(in_specs)

    def body(*refs):
        if place is not None:
            refs = refs[1:]
        t = pl.program_id(0)
        i0, o0 = 0, n_in
        for (n, _, _, _, _, compute), off, (ni, no) in zip(items, offs, spans):
            pl.when((t >= off) & (t < off + n))(functools.partial(compute, refs[i0:i0 + ni], refs[o0:o0 + no], t - off))
            i0, o0 = i0 + ni, o0 + no

    params = pltpu.CompilerParams(dimension_semantics=("arbitrary",), vmem_limit_bytes=VMEM_LIMIT)
    carried = None
    if cargo is not None:
        res, carried = _cargo_call(body, cargo, name=name, grid=(total,), in_specs=in_specs, out_specs=out_specs,
                                   out_shape=out_shape)(*args)
    elif place is None:
        res = pl.pallas_call(body, name=name, grid=(total,), in_specs=in_specs, out_specs=out_specs,
                             out_shape=out_shape, compiler_params=params)(*args)
    else:
        res = pl.pallas_call(
            body, name=name, out_shape=out_shape, compiler_params=params,
            grid_spec=pltpu.PrefetchScalarGridSpec(num_scalar_prefetch=1, grid=(total,), in_specs=in_specs,
                                                   out_specs=out_specs))(place, *args)
    out, o0 = [], 0
    for _, no in spans:
        out.append(res[o0:o0 + no])
        o0 += no
    return out if cargo is None else (out, carried)


def _add_pairs(pairs, place, name):
    def item(mine, theirs):
        lead, _, rows, cols = mine.shape
        tm = _ew_tile(rows)

        def compute(ins, outs, tile):
            outs[0][...] = (ins[0][...].astype(f32) + ins[1][...].astype(f32)).astype(outs[0].dtype)

        blk = lambda tile: pl.BlockSpec((lead, tm, cols), lambda t, pc: (0, tile(t), 0))
        mine_half = lambda tile: pl.BlockSpec((lead, None, tm, cols), lambda t, pc: (0, pc[1], tile(t), 0))
        return (rows // tm, lambda tile: [mine_half(tile), blk(tile)], lambda tile: [blk(tile)],
                [jax.ShapeDtypeStruct(theirs.shape, theirs.dtype)], [mine, theirs], compute)

    return [out[0] for out in _batched_call([item(a, b) for a, b in pairs], name, place)]


def _cargo_alone(cargo, name):
    return _cargo_call(None, cargo, name=name, grid=(), in_specs=[], out_specs=[], out_shape=[])()[1]


def _bf16_casts(x, mats, padded, cargo, name):
    def item(a, to_rows, tm):
        rows, cols = a.shape
        full = rows // tm

        def compute(ins, outs, tile):
            outs[0][...] = jnp.where(tile < full, ins[0][...], 0.0).astype(bf16)

        return (to_rows // tm, lambda tile: [pl.BlockSpec((tm, cols), lambda t: (jnp.minimum(tile(t), full - 1), 0))],
                lambda tile: [pl.BlockSpec((tm, cols), lambda t: (tile(t), 0))],
                [jax.ShapeDtypeStruct((to_rows, cols), bf16)], [a], compute)

    items = [item(x, x.shape[0], 128)] + [item(w, r, 64 if r > w.shape[0] else 128) for w, r in zip(mats, padded)]
    outs, carried = _batched_call(items, name, cargo=cargo)
    return [o[0] for o in outs], carried


def _sum_slots(jobs, place, name):
    def item(own, recv, whole):
        _, rows, cols = own.shape
        tm = _ew_tile(rows)

        def compute(ins, outs, tile):
            r0, r1, r2 = [ins[1][k].astype(f32) for k in range(N_CHIPS - 1)]
            outs[0][...] = (ins[0][...].astype(f32) + r0) + (r1 + r2)

        ins = lambda tile: [pl.BlockSpec((None, tm, cols), lambda t, pc: (0 if whole else pc[0], tile(t), 0)),
                            pl.BlockSpec((N_CHIPS - 1, tm, cols), lambda t, pc: (0, tile(t), 0))]
        outs = lambda tile: [pl.BlockSpec((None, tm, cols), lambda t, pc: (pc[1], tile(t), 0))]
        return rows // tm, ins, outs, [jax.ShapeDtypeStruct((2, rows, cols), f32)], [own, recv], compute

    return [out[0] for out in _batched_call([item(*job) for job in jobs], name, place)]


def _sibling_fill(bufs):
    n = len(bufs)

    def body(*refs):
        src, dst = refs[:n], refs[n:2 * n]
        send_sems, recv_sems = refs[2 * n:]
        c = lax.axis_index("c")
        remote = [pltpu.make_async_remote_copy(src_ref=src[a].at[c], dst_ref=dst[a].at[c], send_sem=send_sems.at[a],
                                               recv_sem=recv_sems.at[a], device_id=_sibling(), device_id_type=MESH)
                  for a in range(n)]
        for cp in remote:
            cp.start()
        for a in range(n):
            pltpu.make_async_remote_copy(src_ref=src[a].at[c], dst_ref=dst[a].at[1 - c], send_sem=send_sems.at[a],
                                         recv_sem=recv_sems.at[a], device_id=_sibling(), device_id_type=MESH).wait_recv()
        for cp in remote:
            cp.wait_send()

    sem = pltpu.SemaphoreType.DMA((n,))
    out = pl.pallas_call(
        body, name="sibling_fill", in_specs=[ANY] * n, out_specs=[ANY] * n,
        out_shape=[jax.ShapeDtypeStruct(b.shape, b.dtype) for b in bufs],
        input_output_aliases={a: a for a in range(n)}, scratch_shapes=[sem, sem])(*bufs)
    return [o.reshape(2 * b.shape[1], b.shape[2]) for o, b in zip(out, bufs)]


class _GradReducer:
    def __init__(self, place):
        self.place, self.split, self.sums, self.slots = place, {}, {}, {}

    def swap(self, named):
        self.split.update({nm: _split_halves(g) for nm, g in named})
        return _SwapCargo([self.split[nm] for nm, _ in named])

    def summed(self, names, got, tag):
        sums = _add_pairs([(self.split[nm], g) for nm, g in zip(names, got)], self.place, "presum_" + tag)
        self.sums.update(zip(names, sums))

    def exchange(self, names):
        return _ExchangeCargo([self.sums[nm] for nm in names], [nm == "small" for nm in names])

    def arrived(self, names, slots):
        self.slots.update(zip(names, slots))

    def reduce(self, names):
        jobs = [(self.sums[nm], self.slots[nm], nm == "small") for nm in names]
        return _sibling_fill(_sum_slots(jobs, self.place, "sum_slots"))


def _adamw_update(w, m, v, g):
    m = ADAM_B1 * m + (1.0 - ADAM_B1) * g
    v = ADAM_B2 * v + (1.0 - ADAM_B2) * jnp.square(g)
    m_hat = m / (1.0 - ADAM_B1 ** ADAM_STEP)
    v_hat = v / (1.0 - ADAM_B2 ** ADAM_STEP)
    return -ADAM_LR * (m_hat / (jnp.sqrt(v_hat) + ADAM_EPS) + ADAM_WD * w), m, v


def _adamw(jobs, name):
    def item(w, m, v, grad, tm):
        rows, cols = w.shape
        gcols = grad.shape[1]

        def compute(ins, outs, tile):
            g = ins[3][0:tm, 0:cols]
            outs[0][...] = g
            outs[1][...], outs[2][...], outs[3][...] = _adamw_update(ins[0][...], ins[1][...], ins[2][...], g)

        blk = lambda tile: pl.BlockSpec((tm, cols), lambda t: (tile(t), 0))
        if tm == rows:
            gspec = lambda tile: pl.BlockSpec(grad.shape, lambda t: (0, 0))
        else:
            gspec = lambda tile: pl.BlockSpec((tm, gcols), lambda t: (tile(t), 0))
        return (rows // tm, lambda tile: [blk(tile)] * 3 + [gspec(tile)], lambda tile: [blk(tile)] * 4,
                [jax.ShapeDtypeStruct((rows, cols), f32)] * 4, [w, m, v, grad], compute)

    return _batched_call([item(*job) for job in jobs], name)


def _rope_tables(s):
    half = HEAD_DIM // 2
    inv_freq = 1.0 / (10000.0 ** (jnp.arange(half, dtype=f32) / half))
    ang = jnp.arange(s, dtype=f32)[:, None] * inv_freq[None, :]
    cos, sin = jnp.cos(ang), jnp.sin(ang)
    return jnp.tile(cos, (1, LANES // half)), jnp.tile(jnp.concatenate([-sin, sin], axis=1), (1, LANES // HEAD_DIM))


def _local_step(x, x_bf, target, win_g, gather, comm, b_gate, w_pool, pool_scale, ln1_g, ln1_b, convb_g, ln2_g, ln2_b):
    s = x.shape[0]
    nb = s // MOBA_BLOCK
    cos_t, sin_t = _rope_tables(s)

    z, (wba_g, wbp_g, wout_g) = _inproj(x_bf, win_g, cos_t, sin_t, b_gate, gather["inproj"])
    wba = wba_g.transpose(1, 0, 2).reshape(D_ATTN, D_MODEL)
    wbp = wbp_g.transpose(1, 0, 2).reshape(D_POOL, D_MODEL)
    wout = wout_g.reshape(D_MODEL, D_MODEL)
    (o_bf, lse, pooled_bf, mixed, pm_bf), (wg_g, wu_g, convw_g) = _attn_fwd(z, nb, w_pool, pool_scale,
                                                                           gather["attn_fwd"])
    m_bf = _branch_merge(o_bf, pm_bf, z, wba, wbp)
    x1_bf, xhat1, rstd1 = _out_ln1(m_bf, wout, x, ln1_g, ln1_b)
    (a, up, ac, hh_bf), (wd_g,) = _ffn_up(x1_bf, wg_g, wu_g, convw_g, convb_g, gather["ffn_up"])
    wd = wd_g.reshape(D_FF_PAD, D_MODEL)
    loss, dh2, dh2_bf, d_ln2_g, d_ln2_b = _ffn_down_ln2_loss(hh_bf, wd, xhat1, ln1_g, ln1_b, ln2_g, ln2_b, target)

    da_bf, dup_bf, dconv = _ffn_act_bwd(dh2_bf, wd_g, ac, a, up, convw_g)
    ffn_shape = (N_CHIPS, FF_PAD, D_MODEL)
    d_wd = _matmul_tn(hh_bf, dh2_bf, 1, "dw_ffn_down").reshape(ffn_shape)
    d_wg, got = _matmul_tn(da_bf, x1_bf, 1, "dw_ffn_gate", cargo=comm.swap([("w_ffn_down", d_wd)]))
    comm.summed(["w_ffn_down"], got, "ffn_down")
    d_wg = d_wg.reshape(ffn_shape)
    d_wu = _matmul_tn(dup_bf, x1_bf, 1, "dw_ffn_up").reshape(ffn_shape)
    gate_up = ["w_ffn_gate", "w_ffn_up", "conv_w"]
    (dh1, dh1_bf, d_ln1_g, d_ln1_b), (slots, got) = _ffn_in_bwd_ln1(
        da_bf, dup_bf, wg_g, wu_g, dh2, xhat1, rstd1, ln1_g,
        _MultiCargo([comm.exchange(["w_ffn_down"]), comm.swap(list(zip(gate_up, [d_wg, d_wu, dconv])))]))
    comm.arrived(["w_ffn_down"], slots)
    comm.summed(gate_up, got, "ffn_gate_up")
    d_wout = _matmul_tn(m_bf, dh1_bf, 1, "dw_out").reshape(N_CHIPS, D_MODEL // N_CHIPS, D_MODEL)
    dzg_bf, dya_bf, dyp_bf, do_bf, dpm, d_bgate = _merge_bwd(dh1_bf, wout, o_bf, pm_bf, z, wba, wbp)
    d_wba = _matmul_tn(o_bf, dya_bf, N_CHIPS, "dw_branch_attn")
    d_wbp = _matmul_tn(pm_bf, dyp_bf, N_CHIPS, "dw_branch_pool")
    du_bf, d_wpool, d_pscale = _pool_bwd(dpm, mixed, pooled_bf, w_pool, pool_scale)
    d_convb = dconv[:, 3, :FF_SHARD].reshape(1, N_CHIPS * FF_SHARD)
    small = _pack_small([d_bgate, d_wpool.reshape(-1, POOL_GROUP), d_pscale, d_ln1_g, d_ln1_b, d_ln2_g, d_ln2_b,
                         d_convb], loss)[None]
    branch = ["w_out", "w_branch_attn", "w_branch_pool", "small"]
    (dqkv_bf,), (slots, got) = _attn_bwd(
        z, o_bf, lse, do_bf, cos_t, sin_t, nb,
        _MultiCargo([comm.exchange(gate_up), comm.swap(list(zip(branch, [d_wout, d_wba, d_wbp, small])))]))
    comm.arrived(gate_up, slots)
    comm.summed(branch, got, "branch")
    d_win, slots = _dw_in(x_bf, dqkv_bf, du_bf, dzg_bf, comm.exchange(branch))
    comm.arrived(branch, slots)
    comm.summed(["w_in"], _cargo_alone(comm.swap([("w_in", d_win)]), "swap_w_in"), "w_in")
    (grad_x,), slots = _inproj_dx(dqkv_bf, du_bf, dzg_bf, win_g, dh1, comm.exchange(["w_in"]))
    comm.arrived(["w_in"], slots)
    return grad_x


SMALL_WEIGHTS = [("b_gate", 2 * D_MODEL), ("w_pool", 4 * POOL_GROUP * POOL_GROUP), ("pool_scale", D_POOL),
                 ("ln1_g", D_MODEL), ("ln1_b", D_MODEL), ("ln2_g", D_MODEL), ("ln2_b", D_MODEL),
                 ("conv_b", N_CHIPS * FF_SHARD)]
LOSS_ROW = 588
SMALL_ROWS = 592


def _small_pieces(ref, first_row, n):
    if ref.shape[0] > 1:
        return [(slice(first_row, first_row + n // LANES), (slice(None), slice(None)))]
    return [(slice(first_row + i, first_row + i + 1), (slice(None), slice(i * LANES, (i + 1) * LANES)))
            for i in range(n // LANES)]


def _small_first_rows():
    rows, r = [], 0
    for _, n in SMALL_WEIGHTS:
        rows.append(r)
        r += n // LANES
    return rows


def _pack_small(parts, loss):
    def body(*refs):
        out = refs[-1]
        out[...] = jnp.zeros_like(out)
        for src, r0, (_, n) in zip(refs, _small_first_rows(), SMALL_WEIGHTS):
            for rows, where in _small_pieces(src, r0, n):
                out[rows, :] = src[where]
        out[LOSS_ROW:LOSS_ROW + 1, :] = refs[len(parts)][0:1, :]

    return pl.pallas_call(body, name="pack_small", out_shape=jax.ShapeDtypeStruct((SMALL_ROWS, LANES), f32))(
        *parts, loss)


def _adamw_small(g_packed, triples):
    n_in = 3 * len(triples)

    def body(g_ref, *refs):
        for k, (r0, (_, n)) in enumerate(zip(_small_first_rows(), SMALL_WEIGHTS)):
            w_ref, m_ref, v_ref = refs[3 * k:3 * k + 3]
            outs = refs[n_in + 4 * k:n_in + 4 * k + 4]
            for rows, where in _small_pieces(w_ref, r0, n):
                g = g_ref[rows, :]
                for o_ref, val in zip(outs, (g,) + _adamw_update(w_ref[where], m_ref[where], v_ref[where], g)):
                    o_ref[where] = val

    res = pl.pallas_call(
        body, name="adamw_small",
        out_shape=[jax.ShapeDtypeStruct(t[0].shape, f32) for t in triples for _ in range(4)],
    )(g_packed, *[a for t in triples for a in t])
    return [res[4 * k:4 * k + 4] for k in range(len(triples))]


def _pad_conv_b(cb):
    return jnp.pad(cb.reshape(N_CHIPS, FF_SHARD), ((0, 0), (0, FF_PAD - FF_SHARD)))


def kernel(x, w_in, b_gate, w_branch_attn, w_pool, pool_scale, w_branch_pool, w_out, ln1_g, ln1_b, w_ffn_gate, w_ffn_up, conv_w, conv_b, w_ffn_down, ln2_g, ln2_b, loss_target, m_w_in, m_b_gate, m_w_branch_attn, m_w_pool, m_pool_scale, m_w_branch_pool, m_w_out, m_ln1_g, m_ln1_b, m_w_ffn_gate, m_w_ffn_up, m_conv_w, m_conv_b, m_w_ffn_down, m_ln2_g, m_ln2_b, v_w_in, v_b_gate, v_w_branch_attn, v_w_pool, v_pool_scale, v_w_branch_pool, v_w_out, v_ln1_g, v_ln1_b, v_w_ffn_gate, v_w_ffn_up, v_conv_w, v_conv_b, v_w_ffn_down, v_ln2_g, v_ln2_b):
    mats = [w_branch_attn[0], w_branch_pool[0], w_out[0], w_ffn_gate[0].T, w_ffn_up[0].T, w_ffn_down[0]]
    padded = [w.shape[0] for w in mats[:3]] + [FF_PAD] * 3
    (x_bf, *casted), (win_g,) = _bf16_casts(x[0], mats, padded, _GatherCargo([w_in[0].astype(bf16)], [0.9]),
                                            "gather_w_in")
    shards = [None] + casted + [jnp.pad(conv_w[0], ((0, CONV_ROWS - 3), (0, FF_PAD - FF_SHARD)))]
    gather = {"inproj": _GatherCargo(shards[1:4], [0.3, 0.5, 0.75]),
              "attn_fwd": _GatherCargo(shards[4:6] + shards[7:], [0.97, 0.97, 0.97]),
              "ffn_up": _GatherCargo(shards[6:7], [0.8])}

    place = jnp.stack([2 * lax.axis_index("x") + lax.axis_index("y"), lax.axis_index("c")]).astype(jnp.int32)
    comm = _GradReducer(place)
    convb_g = _pad_conv_b(conv_b).reshape(N_CHIPS, 1, FF_PAD)
    grad_x = _local_step(
        x[0], x_bf, loss_target[0], win_g, gather, comm, b_gate, w_pool[0], pool_scale, ln1_g, ln1_b, convb_g, ln2_g, ln2_b)
    names = ["w_in", "w_branch_attn", "w_branch_pool", "w_out", "w_ffn_gate", "w_ffn_up", "w_ffn_down", "conv_w", "small"]
    grads = comm.reduce(names)

    weights = [w_in[0], w_branch_attn[0], w_branch_pool[0], w_out[0], w_ffn_gate[0].T, w_ffn_up[0].T, w_ffn_down[0],
               conv_w[0]]
    m_in = [m_w_in[0], m_w_branch_attn[0], m_w_branch_pool[0], m_w_out[0], m_w_ffn_gate[0].T, m_w_ffn_up[0].T,
            m_w_ffn_down[0], m_conv_w[0]]
    v_in = [v_w_in[0], v_w_branch_attn[0], v_w_branch_pool[0], v_w_out[0], v_w_ffn_gate[0].T, v_w_ffn_up[0].T,
            v_w_ffn_down[0], v_conv_w[0]]
    tiles = [256, 256, 256, 128, 176, 176, 176, 3]
    jobs = list(zip(weights, m_in, v_in, grads, tiles))
    ffn = slice(4, 7)
    updated = _adamw(jobs[:4] + jobs[7:], "adamw_rest")
    updated = updated[:4] + _adamw(jobs[ffn], "adamw_ffn") + updated[4:]
    res = {nm: [(r.T if nm in ("w_ffn_gate", "w_ffn_up") else r)[None] for r in outs4]
           for nm, outs4 in zip(names[:-1], updated)}

    small_w = [b_gate, w_pool, pool_scale, ln1_g, ln1_b, ln2_g, ln2_b, conv_b]
    small_m = [m_b_gate, m_w_pool, m_pool_scale, m_ln1_g, m_ln1_b, m_ln2_g, m_ln2_b, m_conv_b]
    small_v = [v_b_gate, v_w_pool, v_pool_scale, v_ln1_g, v_ln1_b, v_ln2_g, v_ln2_b, v_conv_b]
    as_rows = lambda a: a.reshape(-1, POOL_GROUP) if a.ndim == 4 else a
    small_out = _adamw_small(grads[-1], [tuple(as_rows(a) for a in t) for t in zip(small_w, small_m, small_v)])
    for (nm, _), w, outs4 in zip(SMALL_WEIGHTS, small_w, small_out):
        res[nm] = [o.reshape(w.shape) for o in outs4]
    loss = grads[-1][LOSS_ROW, 0]

    order = ["w_in", "b_gate", "w_branch_attn", "w_pool", "pool_scale", "w_branch_pool", "w_out", "ln1_g", "ln1_b",
             "w_ffn_gate", "w_ffn_up", "conv_w", "conv_b", "w_ffn_down", "ln2_g", "ln2_b"]
    outs = [loss, grad_x[None]]
    for kind in range(4):
        outs += [res[nm][kind] for nm in order]
    return tuple(outs)
```

```python
import functools

import jax
import jax.numpy as jnp
from jax import lax
from jax.experimental import pallas as pl
from jax.experimental.pallas import tpu as pltpu

D_MODEL = 1024
HEAD_DIM = 64
D_ATTN = 512
D_POOL = 512
MOBA_BLOCK = 256
MOBA_TOPK = 3
POOL_GROUP = 128
MAX_WINDOW = 16
FF_SHARD = 704
FF_PAD = 768
D_FF_PAD = 4 * FF_PAD
N_CHIPS = 4
LANES = 128
ALPHA = (2.0 * 1) ** 0.25
LN_EPS = 1e-5
NEG = -1e30
SCALE = HEAD_DIM ** -0.5
ADAM_LR, ADAM_B1, ADAM_B2, ADAM_EPS, ADAM_WD, ADAM_STEP = 0.001, 0.9, 0.999, 1e-08, 0.01, 10
VMEM_LIMIT = 56 * 1024 * 1024
MESH = pl.DeviceIdType.MESH

bf16 = jnp.bfloat16
f32 = jnp.float32


def _dot(a, b):
    return jnp.dot(a, b, preferred_element_type=f32)


def _dot_nt(a, b):
    return lax.dot_general(a, b, (((1,), (1,)), ((), ())), preferred_element_type=f32)


def _dot_tn(a, b):
    return lax.dot_general(a, b, (((0,), (0,)), ((), ())), preferred_element_type=f32)


def _call(body, *, name, grid, in_specs, out_specs, out_shape, scratch=(), cargo=None):
    if cargo is not None:
        return _cargo_call(body, cargo, name=name, grid=grid, in_specs=in_specs, out_specs=out_specs,
                           out_shape=out_shape, scratch=scratch)
    return pl.pallas_call(
        body, name=name, grid=grid, in_specs=in_specs, out_specs=out_specs, out_shape=out_shape,
        scratch_shapes=list(scratch),
        compiler_params=pltpu.CompilerParams(dimension_semantics=("arbitrary",) * len(grid),
                                             vmem_limit_bytes=VMEM_LIMIT))


def _rot_half(t):
    lane = lax.broadcasted_iota(jnp.int32, t.shape, 1)
    first = (lane % HEAD_DIM) < (HEAD_DIM // 2)
    return jnp.where(first, pltpu.roll(t, LANES - HEAD_DIM // 2, 1), pltpu.roll(t, HEAD_DIM // 2, 1))


def _rope(t, cos, sin_signed):
    return t * cos + _rot_half(t) * sin_signed


def _rope_bwd(d, cos, sin_signed):
    return d * cos + _rot_half(d * sin_signed)


def _gelu_parts(a):
    cdf = 0.5 * (1.0 + lax.erf(a * (2.0 ** -0.5)))
    pdf = jnp.exp(-0.5 * a * a) * ((2.0 * jnp.pi) ** -0.5)
    return a * cdf, cdf + a * pdf


def _layer_norm(h, g, b):
    mu = jnp.mean(h, axis=-1, keepdims=True)
    xc = h - mu
    var = jnp.mean(xc * xc, axis=-1, keepdims=True)
    rstd = lax.rsqrt(var + LN_EPS)
    xhat = xc * rstd
    return xhat * g + b, xhat, rstd


def _layer_norm_bwd(dy, xhat, rstd, g):
    dxh = dy * g
    m1 = jnp.mean(dxh, axis=-1, keepdims=True)
    m2 = jnp.mean(dxh * xhat, axis=-1, keepdims=True)
    return rstd * (dxh - m1 - xhat * m2)


def _inproj(x_bf, win_g, cos_t, sin_t, b_gate, cargo):
    s = x_bf.shape[0]
    tm, tn = 1024, 512

    def body(x_ref, w_ref, cos_ref, sin_ref, b_ref, o_ref):
        j = pl.program_id(1)
        acc = _dot(x_ref[...], w_ref[...])

        @pl.when(j < 2)
        def _():
            for c in range(tn // LANES):
                sl = slice(c * LANES, (c + 1) * LANES)
                o_ref[:, sl] = _rope(acc[:, sl], cos_ref[...], sin_ref[...])

        @pl.when((j >= 2) & (j < 4))
        def _():
            o_ref[...] = acc

        @pl.when(j >= 4)
        def _():
            o_ref[...] = jax.nn.sigmoid(acc + b_ref[...])

    (z,), carried = _cargo_call(
        body, cargo, name="inproj", grid=(s // tm, 4 * D_MODEL // tn),
        in_specs=[pl.BlockSpec((tm, D_MODEL), lambda i, j: (i, 0)),
                  pl.BlockSpec((None, D_MODEL, tn), lambda i, j: (j // 2, 0, j % 2)),
                  pl.BlockSpec((tm, LANES), lambda i, j: (i, 0)),
                  pl.BlockSpec((tm, LANES), lambda i, j: (i, 0)),
                  pl.BlockSpec((1, tn), lambda i, j: (0, jnp.maximum(j - 4, 0)))],
        out_specs=[pl.BlockSpec((tm, tn), lambda i, j: (i, j))],
        out_shape=[jax.ShapeDtypeStruct((s, 4 * D_MODEL), f32)])(x_bf, win_g, cos_t, sin_t, b_gate)
    return z, carried


STAT_ROWS = 8


def _pair_rows(v0, v1, fill):
    r = lax.broadcasted_iota(jnp.int32, (STAT_ROWS, v0.shape[1]), 0)
    return jnp.where(r == 0, v0, jnp.where(r == 1, v1, fill))


def _head_lanes(shape):
    lane = lax.broadcasted_iota(jnp.int32, shape, 1)
    return lane < HEAD_DIM, lane >= HEAD_DIM


def _head_rows(shape):
    row = lax.broadcasted_iota(jnp.int32, shape, 0)
    return row < HEAD_DIM, row >= HEAD_DIM


def _moba_select(q_bf, k_all, i, nb):
    k_mean = jnp.mean(k_all.reshape(nb, MOBA_BLOCK, LANES), axis=1)
    n_io = lax.broadcasted_iota(jnp.int32, (nb, MOBA_BLOCK), 0)
    past = n_io < i
    sels = []
    for head in _head_lanes((nb, LANES)):
        gate = _dot_nt(jnp.where(head, k_mean, 0.0).astype(bf16), q_bf)
        g = jnp.where(past, gate, NEG)
        rank = jnp.zeros((nb, MOBA_BLOCK), f32)
        for m in range(nb):
            gm = g[m:m + 1, :]
            rank = rank + jnp.where((gm > g) | ((gm == g) & (m < n_io)), 1.0, 0.0)
        sels.append(jnp.where(past & (rank < MOBA_TOPK), 1.0, 0.0))
    return sels


def _attn_fwd(z, nb, w_pool, pool_scale, cargo):
    s = z.shape[0]

    def body(q_ref, k_ref, v_ref, u_ref, wp_ref, sc_ref, o_ref, lse_ref, pooled_ref, mixed_ref, pm_ref, vt_s, pad):
        pair, i = pl.program_id(0), pl.program_id(1)

        @pl.when(i == 0)
        def _():
            for j in range(nb):
                vt_s[j] = v_ref[j * MOBA_BLOCK:(j + 1) * MOBA_BLOCK, :].T.astype(bf16)
            _pool_group(pair, u_ref, wp_ref, sc_ref, pooled_ref, mixed_ref, pm_ref, pad)

        k_heads = _head_lanes((MOBA_BLOCK, LANES))
        o_heads = _head_rows((LANES, MOBA_BLOCK))

        def block(j, qs, allow, m_old, l_old):
            kj, vtj = k_ref[j * MOBA_BLOCK:(j + 1) * MOBA_BLOCK, :], vt_s[j]
            m_new, alpha, l_new, pv = [], [], [], None
            for h in range(2):
                sc = jnp.where(allow[h], _dot_nt(jnp.where(k_heads[h], kj, 0.0).astype(bf16), qs), NEG)
                mx = jnp.max(sc, axis=0, keepdims=True)
                mn = mx if m_old is None else jnp.maximum(m_old[h], mx)
                p = jnp.exp(sc - mn)
                lsum = jnp.sum(p, axis=0, keepdims=True)
                if m_old is None:
                    a = None
                else:
                    a = jnp.exp(m_old[h] - mn)
                    lsum = a * l_old[h] + lsum
                t = _dot(jnp.where(o_heads[h], vtj, jnp.zeros_like(vtj)), p.astype(bf16))
                pv = t if pv is None else pv + t
                m_new.append(mn)
                alpha.append(a)
                l_new.append(lsum)
            return m_new, alpha, l_new, pv

        def query_block(c):
            q = q_ref[...]
            sels = _moba_select(q.astype(bf16), k_ref[...], c, nb)
            qs = (q * SCALE).astype(bf16)
            key = lax.broadcasted_iota(jnp.int32, (MOBA_BLOCK, MOBA_BLOCK), 0)
            qry = lax.broadcasted_iota(jnp.int32, (MOBA_BLOCK, MOBA_BLOCK), 1)
            m, _, l, acc = block(c, qs, [key <= qry] * 2, None, None)
            for j in range(c):
                m, alpha, l, pv = block(j, qs, [sels[h][j:j + 1, :] > 0.0 for h in range(2)], m, l)
                acc = acc * jnp.where(o_heads[0], alpha[0], alpha[1]) + pv
            o_ref[...] = (acc / jnp.where(o_heads[0], l[0], l[1])).T.astype(bf16)
            lse_ref[...] = _pair_rows(m[0] + jnp.log(l[0]), m[1] + jnp.log(l[1]), 0.0)

        for c in range(nb):
            pl.when(i == c)(functools.partial(query_block, c))

    group = pl.BlockSpec((s, POOL_GROUP), lambda hp, i: (0, hp))
    return _cargo_call(
        body, cargo, name="attn_fwd", grid=(D_ATTN // LANES, nb),
        in_specs=[pl.BlockSpec((MOBA_BLOCK, LANES), lambda hp, i: (i, hp)),
                  pl.BlockSpec((s, LANES), lambda hp, i: (0, 4 + hp)),
                  pl.BlockSpec((s, LANES), lambda hp, i: (0, 8 + hp)),
                  pl.BlockSpec((s, POOL_GROUP), lambda hp, i: (0, 12 + hp)),
                  pl.BlockSpec((None, POOL_GROUP, POOL_GROUP), lambda hp, i: (hp, 0, 0)),
                  pl.BlockSpec((1, POOL_GROUP), lambda hp, i: (0, hp))],
        out_specs=[pl.BlockSpec((MOBA_BLOCK, LANES), lambda hp, i: (i, hp)),
                   pl.BlockSpec((None, STAT_ROWS, MOBA_BLOCK), lambda hp, i: (hp, 0, i)), group, group, group],
        out_shape=[jax.ShapeDtypeStruct((s, D_ATTN), bf16),
                   jax.ShapeDtypeStruct((D_ATTN // LANES, STAT_ROWS, s), f32),
                   jax.ShapeDtypeStruct((s, D_POOL), bf16), jax.ShapeDtypeStruct((s, D_POOL), f32),
                   jax.ShapeDtypeStruct((s, D_POOL), bf16)],
        scratch=[pltpu.VMEM((nb, LANES, MOBA_BLOCK), bf16), pltpu.VMEM((s + MAX_WINDOW, POOL_GROUP), f32)],
    )(z, z, z, z, w_pool, pool_scale)


def _window_select(snaps, g):
    return jnp.where(g == 0, snaps[0], jnp.where(g == 1, snaps[1], jnp.where(g == 2, snaps[2], snaps[3])))


def _pool_group(g, u_ref, w_ref, sc_ref, pooled_ref, mixed_ref, pm_ref, pad):
    s = u_ref.shape[0]
    u = u_ref[...]
    pad[0:MAX_WINDOW, :] = jnp.zeros((MAX_WINDOW, POOL_GROUP), f32)
    pad[MAX_WINDOW:MAX_WINDOW + s, :] = u
    acc = u
    snaps = []
    for d in range(1, MAX_WINDOW):
        acc = acc + pad[MAX_WINDOW - d:MAX_WINDOW - d + s, :]
        if d + 1 in (2, 4, 8, 16):
            snaps.append(acc)
    win = _window_select(snaps, g)
    t = lax.broadcasted_iota(jnp.int32, (s, POOL_GROUP), 0)
    count = jnp.minimum(t + 1, jnp.left_shift(2, g)).astype(f32)
    pooled = (win / count - u).astype(bf16)
    mixed = _dot(pooled, w_ref[...].astype(bf16))
    pooled_ref[...] = pooled
    mixed_ref[...] = mixed
    pm_ref[...] = (mixed * sc_ref[...]).astype(bf16)


def _branch_merge(o_bf, pm_bf, z, wba, wbp):
    s = o_bf.shape[0]
    tm = 512

    def body(o_ref, pm_ref, ga_ref, gp_ref, wba_ref, wbp_ref, m_ref):
        ya = _dot(o_ref[...], wba_ref[...])
        yp = _dot(pm_ref[...], wbp_ref[...])
        m_ref[...] = (ga_ref[...] * ya + gp_ref[...] * yp).astype(bf16)

    full = lambda r, c: pl.BlockSpec((r, c), lambda i: (0, 0))
    return _call(
        body, name="branch_merge", grid=(s // tm,),
        in_specs=[pl.BlockSpec((tm, D_ATTN), lambda i: (i, 0)), pl.BlockSpec((tm, D_POOL), lambda i: (i, 0)),
                  pl.BlockSpec((tm, D_MODEL), lambda i: (i, 2)), pl.BlockSpec((tm, D_MODEL), lambda i: (i, 3)),
                  full(D_ATTN, D_MODEL), full(D_POOL, D_MODEL)],
        out_specs=pl.BlockSpec((tm, D_MODEL), lambda i: (i, 0)),
        out_shape=jax.ShapeDtypeStruct((s, D_MODEL), bf16))(o_bf, pm_bf, z, z, wba, wbp)


def _out_ln1(m_bf, wout, x, ln_g, ln_b):
    s = x.shape[0]
    tm = 512

    def body(m_ref, w_ref, x_ref, g_ref, b_ref, x1_ref, xhat_ref, rstd_ref):
        h = ALPHA * x_ref[...] + _dot(m_ref[...], w_ref[...])
        y, xhat, rstd = _layer_norm(h, g_ref[...], b_ref[...])
        x1_ref[...] = y.astype(bf16)
        xhat_ref[...] = xhat
        rstd_ref[...] = jnp.broadcast_to(rstd, (tm, LANES))

    row = pl.BlockSpec((tm, D_MODEL), lambda i: (i, 0))
    vec = pl.BlockSpec((1, D_MODEL), lambda i: (0, 0))
    return _call(
        body, name="out_ln1", grid=(s // tm,),
        in_specs=[row, pl.BlockSpec((D_MODEL, D_MODEL), lambda i: (0, 0)), row, vec, vec],
        out_specs=[row, row, pl.BlockSpec((tm, LANES), lambda i: (i, 0))],
        out_shape=[jax.ShapeDtypeStruct((s, D_MODEL), bf16), jax.ShapeDtypeStruct((s, D_MODEL), f32),
                   jax.ShapeDtypeStruct((s, LANES), f32)])(m_bf, wout, x, ln_g, ln_b)


FF_TILE = 256
FF_TILES_PER_SHARD = FF_PAD // FF_TILE
CONV_PAD = 8
CONV_ROWS = 16

def _ff_weight_spec(rows):
    return pl.BlockSpec((None, rows, FF_TILE), lambda j: (j // FF_TILES_PER_SHARD, 0, j % FF_TILES_PER_SHARD))


def _ff_rows_spec():
    return pl.BlockSpec((None, FF_TILE, D_MODEL), lambda j: (j // FF_TILES_PER_SHARD, j % FF_TILES_PER_SHARD, 0))


def _row_shifts(x, shifts):
    s = x.shape[0]
    padded = jnp.concatenate([x, jnp.zeros((CONV_PAD, x.shape[1]), x.dtype)], axis=0)
    return [pltpu.roll(padded, d % (s + CONV_PAD), 0)[0:s] for d in shifts]


def _ffn_up(x1_bf, wg_g, wu_g, convw_g, convb_g, cargo):
    s = x1_bf.shape[0]

    def body(x_ref, wg_ref, wu_ref, cw_ref, cb_ref, a_ref, up_ref, ac_ref, hh_ref):
        x1 = x_ref[...]
        a = _dot_nt(x1, wg_ref[...])
        up = _dot_nt(x1, wu_ref[...])
        a_ref[...] = a
        up_ref[...] = up
        a1, a2 = _row_shifts(a, (1, 2))
        ac = a * cw_ref[2:3, :] + a1 * cw_ref[1:2, :] + a2 * cw_ref[0:1, :] + cb_ref[...]
        ac_ref[...] = ac
        hg, _ = _gelu_parts(ac)
        hh_ref[...] = (hg * up).astype(bf16)

    col = pl.BlockSpec((s, FF_TILE), lambda j: (0, j))
    wide = jax.ShapeDtypeStruct((s, D_FF_PAD), f32)
    return _call(
        body, name="ffn_up", grid=(D_FF_PAD // FF_TILE,),
        in_specs=[pl.BlockSpec((s, D_MODEL), lambda j: (0, 0)), _ff_rows_spec(), _ff_rows_spec(),
                  _ff_weight_spec(8), _ff_weight_spec(1)],
        out_specs=[col, col, col, col],
        out_shape=[wide, wide, wide, jax.ShapeDtypeStruct((s, D_FF_PAD), bf16)],
        cargo=cargo)(x1_bf, wg_g, wu_g, convw_g, convb_g)


def _ffn_down_ln2_loss(hh_bf, wd, xhat1, ln1_g, ln1_b, ln2_g, ln2_b, target):
    s = hh_bf.shape[0]
    tm = 256

    def body(hh_ref, w_ref, xh1_ref, g1_ref, b1_ref, g2_ref, b2_ref, t_ref, loss_ref, dh_ref, dhb_ref, dg_ref, db_ref):
        i = pl.program_id(0)
        x1 = xh1_ref[...] * g1_ref[...] + b1_ref[...]
        h = ALPHA * x1 + _dot(hh_ref[...], w_ref[...])
        y, xhat, rstd = _layer_norm(h, g2_ref[...], b2_ref[...])
        err = y - t_ref[...]
        part = 0.5 * jnp.sum(jnp.mean(err * err, axis=-1, keepdims=True), axis=0, keepdims=True)
        dy = err * (1.0 / D_MODEL)

        @pl.when(i == 0)
        def _():
            loss_ref[...] = jnp.zeros_like(loss_ref)
            dg_ref[...] = jnp.zeros_like(dg_ref)
            db_ref[...] = jnp.zeros_like(db_ref)

        loss_ref[...] += jnp.broadcast_to(part, loss_ref.shape)
        dg_ref[...] += jnp.sum(dy * xhat, axis=0, keepdims=True)
        db_ref[...] += jnp.sum(dy, axis=0, keepdims=True)
        dh = _layer_norm_bwd(dy, xhat, rstd, g2_ref[...])
        dh_ref[...] = dh
        dhb_ref[...] = dh.astype(bf16)

    row = pl.BlockSpec((tm, D_MODEL), lambda i: (i, 0))
    vec = pl.BlockSpec((1, D_MODEL), lambda i: (0, 0))
    return _call(
        body, name="ffn_down_ln2_loss", grid=(s // tm,),
        in_specs=[pl.BlockSpec((tm, D_FF_PAD), lambda i: (i, 0)),
                  pl.BlockSpec((D_FF_PAD, D_MODEL), lambda i: (0, 0)),
                  row, vec, vec, vec, vec, row],
        out_specs=[pl.BlockSpec((8, LANES), lambda i: (0, 0)), row, row, vec, vec],
        out_shape=[jax.ShapeDtypeStruct((8, LANES), f32), jax.ShapeDtypeStruct((s, D_MODEL), f32),
                   jax.ShapeDtypeStruct((s, D_MODEL), bf16), jax.ShapeDtypeStruct((1, D_MODEL), f32),
                   jax.ShapeDtypeStruct((1, D_MODEL), f32)])(hh_bf, wd, xhat1, ln1_g, ln1_b, ln2_g, ln2_b, target)


def _ffn_act_bwd(dh2_bf, wd_g, ac, a, up, convw_g):
    s = dh2_bf.shape[0]

    def body(dh_ref, wd_ref, ac_ref, a_ref, up_ref, cw_ref, da_ref, dup_ref, dc_ref):
        dhh = _dot_nt(dh_ref[...], wd_ref[...])
        hg, dgelu = _gelu_parts(ac_ref[...])
        dup_ref[...] = (dhh * hg).astype(bf16)
        dac = dhh * up_ref[...] * dgelu
        dac1, dac2 = _row_shifts(dac, (-1, -2))
        a = a_ref[...]
        rows = [jnp.sum(d * a, axis=0, keepdims=True) for d in (dac2, dac1, dac)]
        rows.append(jnp.sum(dac, axis=0, keepdims=True))
        rows.append(jnp.zeros((CONV_ROWS - 4, FF_TILE), f32))
        dc_ref[...] = jnp.concatenate(rows, axis=0)
        da_ref[...] = (dac * cw_ref[2:3, :] + dac1 * cw_ref[1:2, :] + dac2 * cw_ref[0:1, :]).astype(bf16)

    col = pl.BlockSpec((s, FF_TILE), lambda j: (0, j))
    return _call(
        body, name="ffn_act_bwd", grid=(D_FF_PAD // FF_TILE,),
        in_specs=[pl.BlockSpec((s, D_MODEL), lambda j: (0, 0)), _ff_rows_spec(), col, col, col, _ff_weight_spec(8)],
        out_specs=[col, col, _ff_weight_spec(CONV_ROWS)],
        out_shape=[jax.ShapeDtypeStruct((s, D_FF_PAD), bf16), jax.ShapeDtypeStruct((s, D_FF_PAD), bf16),
                   jax.ShapeDtypeStruct((N_CHIPS, CONV_ROWS, FF_PAD), f32)],
    )(dh2_bf, wd_g, ac, a, up, convw_g)


def _matmul_tn(a, b, n_shards, name, tm=512, tn=1024, cargo=None):
    k, m = a.shape
    n = b.shape[1]
    tm, tn = min(tm, m), min(tn, n // n_shards)
    per = n // n_shards // tn

    def body(a_ref, b_ref, o_ref, at_s):
        @pl.when(pl.program_id(1) == 0)
        def _():
            at_s[...] = a_ref[...].T

        o_ref[...] = _dot(at_s[...], b_ref[...]).astype(bf16)

    res = _call(
        body, name=name, grid=(m // tm, n // tn),
        in_specs=[pl.BlockSpec((k, tm), lambda i, j: (0, i)), pl.BlockSpec((k, tn), lambda i, j: (0, j))],
        out_specs=[pl.BlockSpec((None, tm, tn), lambda i, j: (j // per, i, j % per))],
        out_shape=[jax.ShapeDtypeStruct((n_shards, m, n // n_shards), bf16)], scratch=[pltpu.VMEM((tm, k), bf16)],
        cargo=cargo)(a, b)
    return res[0] if cargo is None else (res[0][0], res[1])


def _dw_in(x_bf, dqkv, du, dzg, cargo):
    k, m = x_bf.shape
    tm, half = 512, D_ATTN

    def body(a_ref, dqkv_ref, du_ref, dzg_ref, o_ref, at_s):
        j = pl.program_id(1)

        @pl.when(j == 0)
        def _():
            at_s[...] = a_ref[...].T
            o_ref[:, 0:half] = _dot(at_s[...], dqkv_ref[0]).astype(bf16)
            o_ref[:, half:2 * half] = _dot(at_s[...], dqkv_ref[1]).astype(bf16)

        @pl.when(j == 1)
        def _():
            o_ref[:, 0:half] = _dot(at_s[...], dqkv_ref[2]).astype(bf16)
            o_ref[:, half:2 * half] = _dot(at_s[...], du_ref[...]).astype(bf16)

        @pl.when(j >= 2)
        def _():
            o_ref[...] = _dot(at_s[...], dzg_ref[...]).astype(bf16)

    (d_win,), carried = _cargo_call(
        body, cargo, name="dw_in", grid=(m // tm, N_CHIPS),
        in_specs=[pl.BlockSpec((k, tm), lambda i, j: (0, i)), pl.BlockSpec((3, k, half), lambda i, j: (0, 0, 0)),
                  pl.BlockSpec((k, half), lambda i, j: (0, 0)),
                  pl.BlockSpec((k, D_MODEL), lambda i, j: (0, jnp.maximum(j - 2, 0)))],
        out_specs=[pl.BlockSpec((None, tm, D_MODEL), lambda i, j: (j, i, 0))],
        out_shape=[jax.ShapeDtypeStruct((N_CHIPS, m, D_MODEL), bf16)], scratch=[pltpu.VMEM((tm, k), bf16)],
    )(x_bf, dqkv, du, dzg)
    return d_win, carried


def _ffn_in_bwd_ln1(da_bf, dup_bf, wg_g, wu_g, dh2, xhat1, rstd1, ln1_g, cargo):
    s = da_bf.shape[0]
    tm = 256

    def body(da_ref, dup_ref, wg_ref, wu_ref, dh2_ref, xh_ref, rstd_ref, g_ref, dh_ref, dhb_ref, dg_ref, db_ref):
        i = pl.program_id(0)
        dx1 = ALPHA * dh2_ref[...]
        for sh in range(N_CHIPS):
            sl = slice(sh * FF_PAD, (sh + 1) * FF_PAD)
            dx1 = dx1 + _dot(da_ref[:, sl], wg_ref[sh]) + _dot(dup_ref[:, sl], wu_ref[sh])
        xhat = xh_ref[...]

        @pl.when(i == 0)
        def _():
            dg_ref[...] = jnp.zeros_like(dg_ref)
            db_ref[...] = jnp.zeros_like(db_ref)

        dg_ref[...] += jnp.sum(dx1 * xhat, axis=0, keepdims=True)
        db_ref[...] += jnp.sum(dx1, axis=0, keepdims=True)
        dh = _layer_norm_bwd(dx1, xhat, rstd_ref[:, 0:1], g_ref[...])
        dh_ref[...] = dh
        dhb_ref[...] = dh.astype(bf16)

    row = pl.BlockSpec((tm, D_MODEL), lambda i: (i, 0))
    wide = pl.BlockSpec((tm, D_FF_PAD), lambda i: (i, 0))
    wfull = pl.BlockSpec((N_CHIPS, FF_PAD, D_MODEL), lambda i: (0, 0, 0))
    vec = pl.BlockSpec((1, D_MODEL), lambda i: (0, 0))
    return _call(
        body, name="ffn_in_bwd_ln1", grid=(s // tm,),
        in_specs=[wide, wide, wfull, wfull, row, row, pl.BlockSpec((tm, LANES), lambda i: (i, 0)), vec],
        out_specs=[row, row, vec, vec],
        out_shape=[jax.ShapeDtypeStruct((s, D_MODEL), f32), jax.ShapeDtypeStruct((s, D_MODEL), bf16),
                   jax.ShapeDtypeStruct((1, D_MODEL), f32), jax.ShapeDtypeStruct((1, D_MODEL), f32)],
        cargo=cargo)(da_bf, dup_bf, wg_g, wu_g, dh2, xhat1, rstd1, ln1_g)


def _merge_bwd(dh1_bf, wout, o_bf, pm_bf, z, wba, wbp):
    s = dh1_bf.shape[0]
    tm = 256

    def body(dh_ref, wout_ref, o_ref, pm_ref, ga_ref, gp_ref, wba_ref, wbp_ref,
             dzg_ref, dya_ref, dyp_ref, do_ref, dpm_ref, dbg_ref):
        i = pl.program_id(0)
        dm = _dot_nt(dh_ref[...], wout_ref[...])
        ya = _dot(o_ref[...], wba_ref[...])
        yp = _dot(pm_ref[...], wbp_ref[...])
        ga, gp = ga_ref[...], gp_ref[...]
        dza = dm * ya * ga * (1.0 - ga)
        dzp = dm * yp * gp * (1.0 - gp)

        @pl.when(i == 0)
        def _():
            dbg_ref[...] = jnp.zeros_like(dbg_ref)

        dbg_ref[:, 0:D_MODEL] += jnp.sum(dza, axis=0, keepdims=True)
        dbg_ref[:, D_MODEL:2 * D_MODEL] += jnp.sum(dzp, axis=0, keepdims=True)
        dzg_ref[:, 0:D_MODEL] = dza.astype(bf16)
        dzg_ref[:, D_MODEL:2 * D_MODEL] = dzp.astype(bf16)
        dya = (dm * ga).astype(bf16)
        dyp = (dm * gp).astype(bf16)
        dya_ref[...] = dya
        dyp_ref[...] = dyp
        do_ref[...] = _dot_nt(dya, wba_ref[...]).astype(bf16)
        dpm_ref[...] = _dot_nt(dyp, wbp_ref[...])

    row = pl.BlockSpec((tm, D_MODEL), lambda i: (i, 0))
    half = pl.BlockSpec((tm, D_ATTN), lambda i: (i, 0))
    full = lambda r, c: pl.BlockSpec((r, c), lambda i: (0, 0))
    return _call(
        body, name="merge_bwd", grid=(s // tm,),
        in_specs=[row, full(D_MODEL, D_MODEL), half, half,
                  pl.BlockSpec((tm, D_MODEL), lambda i: (i, 2)), pl.BlockSpec((tm, D_MODEL), lambda i: (i, 3)),
                  full(D_ATTN, D_MODEL), full(D_POOL, D_MODEL)],
        out_specs=[pl.BlockSpec((tm, 2 * D_MODEL), lambda i: (i, 0)), row, row, half, half,
                   pl.BlockSpec((1, 2 * D_MODEL), lambda i: (0, 0))],
        out_shape=[jax.ShapeDtypeStruct((s, 2 * D_MODEL), bf16), jax.ShapeDtypeStruct((s, D_MODEL), bf16),
                   jax.ShapeDtypeStruct((s, D_MODEL), bf16), jax.ShapeDtypeStruct((s, D_ATTN), bf16),
                   jax.ShapeDtypeStruct((s, D_POOL), f32), jax.ShapeDtypeStruct((1, 2 * D_MODEL), f32)],
    )(dh1_bf, wout, o_bf, pm_bf, z, z, wba, wbp)


def _pool_bwd(dpm, mixed, pooled_bf, w_pool, pool_scale):
    s = dpm.shape[0]

    def body(dpm_ref, mixed_ref, pooled_ref, w_ref, sc_ref, du_ref, dw_ref, dsc_ref, pad):
        g = pl.program_id(0)
        dpm_v = dpm_ref[...]
        dsc_ref[...] = jnp.sum(dpm_v * mixed_ref[...], axis=0, keepdims=True)
        dmixed = (dpm_v * sc_ref[...]).astype(bf16)
        dw_ref[...] = _dot_tn(pooled_ref[...], dmixed)
        dpooled = _dot_nt(dmixed, w_ref[...].astype(bf16))
        t = lax.broadcasted_iota(jnp.int32, (s, POOL_GROUP), 0)
        count = jnp.minimum(t + 1, jnp.left_shift(2, g)).astype(f32)
        r = dpooled / count
        pad[0:s, :] = r
        pad[s:s + MAX_WINDOW, :] = jnp.zeros((MAX_WINDOW, POOL_GROUP), f32)
        acc = r
        snaps = []
        for d in range(1, MAX_WINDOW):
            acc = acc + pad[d:d + s, :]
            if d + 1 in (2, 4, 8, 16):
                snaps.append(acc)
        du_ref[...] = (_window_select(snaps, g) - dpooled).astype(bf16)

    blk = pl.BlockSpec((s, POOL_GROUP), lambda g: (0, g))
    return _call(
        body, name="pool_bwd", grid=(4,),
        in_specs=[blk, blk, blk, pl.BlockSpec((None, POOL_GROUP, POOL_GROUP), lambda g: (g, 0, 0)),
                  pl.BlockSpec((1, POOL_GROUP), lambda g: (0, g))],
        out_specs=[blk, pl.BlockSpec((None, POOL_GROUP, POOL_GROUP), lambda g: (g, 0, 0)),
                   pl.BlockSpec((1, POOL_GROUP), lambda g: (0, g))],
        out_shape=[jax.ShapeDtypeStruct((s, D_POOL), bf16), jax.ShapeDtypeStruct((4, POOL_GROUP, POOL_GROUP), f32),
                   jax.ShapeDtypeStruct((1, D_POOL), f32)],
        scratch=[pltpu.VMEM((s + MAX_WINDOW, POOL_GROUP), f32)])(dpm, mixed, pooled_bf, w_pool, pool_scale)


def _attn_bwd(z, o_bf, lse, do_bf, cos_t, sin_t, nb, cargo):
    s = z.shape[0]

    def body(q_ref, k_ref, v_ref, o_ref, lse_ref, do_ref, cq_ref, sq_ref, cf_ref, sf_ref,
             dqkv_ref, kt_s, dk_s, dv_s):
        i = pl.program_id(1)

        @pl.when(i == 0)
        def _():
            dk_s[...] = jnp.zeros_like(dk_s)
            dv_s[...] = jnp.zeros_like(dv_s)
            for j in range(nb):
                kt_s[j] = k_ref[j * MOBA_BLOCK:(j + 1) * MOBA_BLOCK, :].T.astype(bf16)

        k_heads = _head_lanes((MOBA_BLOCK, LANES))
        t_heads = _head_rows((LANES, MOBA_BLOCK))

        def query_block(c):
            q = q_ref[...]
            sels = _moba_select(q.astype(bf16), k_ref[...], c, nb)
            qs = (q * SCALE).astype(bf16)
            qraw = [jnp.where(hm, q, 0.0).astype(bf16) for hm in k_heads]
            do = do_ref[...].astype(f32)
            do_bf = do_ref[...]
            dob = [jnp.where(hm, do, 0.0).astype(bf16) for hm in k_heads]
            pick = _head_lanes((STAT_ROWS, LANES))
            r8 = lax.broadcasted_iota(jnp.int32, (STAT_ROWS, LANES), 0)
            head_pick = jnp.where(((r8 == 0) & pick[0]) | ((r8 == 1) & pick[1]), 1.0, 0.0)
            delta8 = lax.dot_general(head_pick, do * o_ref[...].astype(f32), (((1,), (1,)), ((), ())),
                                     precision=lax.Precision.HIGHEST, preferred_element_type=f32)
            lse8 = lse_ref[...]
            delta, lse_h = [delta8[0:1, :], delta8[1:2, :]], [lse8[0:1, :], lse8[1:2, :]]
            key = lax.broadcasted_iota(jnp.int32, (MOBA_BLOCK, MOBA_BLOCK), 0)
            qry = lax.broadcasted_iota(jnp.int32, (MOBA_BLOCK, MOBA_BLOCK), 1)

            def block(j, allow):
                rows = slice(j * MOBA_BLOCK, (j + 1) * MOBA_BLOCK)
                kj, vj, ktj = k_ref[rows, :], v_ref[rows, :], kt_s[j]
                dqt, dk, dv = None, None, None
                for h in range(2):
                    sc = _dot_nt(jnp.where(k_heads[h], kj, 0.0).astype(bf16), qs)
                    p = jnp.where(allow[h], jnp.exp(sc - lse_h[h]), 0.0)
                    dp = _dot_nt(jnp.where(k_heads[h], vj, 0.0).astype(bf16), do_bf)
                    ds = (p * (dp - delta[h]) * SCALE).astype(bf16)
                    tv = _dot(p.astype(bf16), dob[h])
                    tk = _dot(ds, qraw[h])
                    tq = _dot(jnp.where(t_heads[h], ktj, jnp.zeros_like(ktj)), ds)
                    dqt, dk, dv = (tq, tk, tv) if dqt is None else (dqt + tq, dk + tk, dv + tv)
                dk_s[rows, :] += dk
                dv_s[rows, :] += dv
                return dqt

            dqt = block(c, [key <= qry] * 2)
            for j in range(c):
                dqt = dqt + block(j, [sels[h][j:j + 1, :] > 0.0 for h in range(2)])
            dqkv_ref[0, c * MOBA_BLOCK:(c + 1) * MOBA_BLOCK, :] = _rope_bwd(dqt.T, cq_ref[...], sq_ref[...]).astype(bf16)

        for c in range(nb):
            pl.when(i == c)(functools.partial(query_block, c))

        @pl.when(i == nb - 1)
        def _():
            dqkv_ref[1] = _rope_bwd(dk_s[...], cf_ref[...], sf_ref[...]).astype(bf16)
            dqkv_ref[2] = dv_s[...].astype(bf16)

    qblk = pl.BlockSpec((MOBA_BLOCK, LANES), lambda hp, i: (i, hp))
    tq = pl.BlockSpec((MOBA_BLOCK, LANES), lambda hp, i: (i, 0))
    tf = pl.BlockSpec((s, LANES), lambda hp, i: (0, 0))
    return _cargo_call(
        body, cargo, name="attn_bwd", grid=(D_ATTN // LANES, nb),
        in_specs=[qblk, pl.BlockSpec((s, LANES), lambda hp, i: (0, 4 + hp)),
                  pl.BlockSpec((s, LANES), lambda hp, i: (0, 8 + hp)), qblk,
                  pl.BlockSpec((None, STAT_ROWS, MOBA_BLOCK), lambda hp, i: (hp, 0, i)), qblk, tq, tq, tf, tf],
        out_specs=[pl.BlockSpec((3, s, LANES), lambda hp, i: (0, 0, hp))],
        out_shape=[jax.ShapeDtypeStruct((3, s, D_ATTN), bf16)],
        scratch=[pltpu.VMEM((nb, LANES, MOBA_BLOCK), bf16), pltpu.VMEM((s, LANES), f32), pltpu.VMEM((s, LANES), f32)],
    )(z, z, z, o_bf, lse, do_bf, cos_t, sin_t, cos_t, sin_t)


def _inproj_dx(dqkv, du, dzg, win_g, dh1, cargo):
    s = du.shape[0]
    tm, half = 256, D_ATTN

    def body(dqkv_ref, du_ref, dzg_ref, w_ref, dh_ref, gx_ref):
        acc = ALPHA * dh_ref[...]
        for n, part in enumerate([dqkv_ref[0], dqkv_ref[1], dqkv_ref[2], du_ref[...]]):
            acc = acc + _dot_nt(part, w_ref[n // 2, :, (n % 2) * half:(n % 2 + 1) * half])
        for sh in range(2, N_CHIPS):
            acc = acc + _dot_nt(dzg_ref[:, (sh - 2) * D_MODEL:(sh - 1) * D_MODEL], w_ref[sh])
        gx_ref[...] = acc

    row = pl.BlockSpec((tm, D_MODEL), lambda i: (i, 0))
    return _call(
        body, name="inproj_dx", grid=(s // tm,),
        in_specs=[pl.BlockSpec((3, tm, half), lambda i: (0, i, 0)), pl.BlockSpec((tm, half), lambda i: (i, 0)),
                  pl.BlockSpec((tm, 2 * D_MODEL), lambda i: (i, 0)),
                  pl.BlockSpec((N_CHIPS, D_MODEL, D_MODEL), lambda i: (0, 0, 0)), row],
        out_specs=[row], out_shape=[jax.ShapeDtypeStruct((s, D_MODEL), f32)], cargo=cargo)(dqkv, du, dzg, win_g, dh1)


ANY = pl.BlockSpec(memory_space=pl.ANY)


def _chip_index():
    return 2 * lax.axis_index("x") + lax.axis_index("y")


def _peer(k):
    x, y, c = lax.axis_index("x"), lax.axis_index("y"), lax.axis_index("c")
    return (x ^ (k >> 1), y ^ (k & 1), c)


def _sibling():
    return (lax.axis_index("x"), lax.axis_index("y"), 1 - lax.axis_index("c"))


class _GatherCargo:
    def __init__(self, shards, pass_on):
        self.shards, self.pass_on, n = shards, pass_on, len(shards)
        self.inputs = [a.reshape(2, a.shape[0] // 2, a.shape[1]) for a in shards]
        self.out_shape = [jax.ShapeDtypeStruct((N_CHIPS,) + a.shape, a.dtype) for a in self.inputs]
        sem, one = pltpu.SemaphoreType.DMA((N_CHIPS - 1, n)), pltpu.SemaphoreType.DMA((n,))
        self.sems = [sem, sem, sem, sem, one, one]

    def _copies(self, src, dst, sems, a):
        send_sems, recv_sems, fsend_sems, frecv_sems, local_sems, own_sems = sems
        p, c = _chip_index(), lax.axis_index("c")
        copy = lambda s, d, **kw: functools.partial(pltpu.make_async_remote_copy, src_ref=s, dst_ref=d, **kw)
        own = copy(src[a], dst[a].at[p], send_sem=local_sems.at[a], recv_sem=own_sems.at[a], device_id=_sibling(),
                   device_id_type=MESH)
        out, arrive, onward, landed = [], [], [], []
        for k in range(1, N_CHIPS):
            ici = dict(send_sem=send_sems.at[k - 1, a], recv_sem=recv_sems.at[k - 1, a], device_id=_peer(k),
                       device_id_type=MESH)
            d2d = dict(send_sem=fsend_sems.at[k - 1, a], recv_sem=frecv_sems.at[k - 1, a], device_id=_sibling(),
                       device_id_type=MESH)
            got, theirs = dst[a].at[p ^ k, c], dst[a].at[p ^ k, 1 - c]
            out.append(copy(src[a].at[c], dst[a].at[p, c], **ici))
            arrive.append(copy(src[a].at[c], got, **ici))
            onward.append(copy(got, got, **d2d))
            landed.append(copy(theirs, theirs, **d2d))
        return own, out, arrive, onward, landed

    def stages(self, steps):
        n = len(self.shards)

        def start(src, dst, sems):
            for a in range(n):
                own, out, _, _, _ = self._copies(src, dst, sems, a)
                own().start()
                for cp in out:
                    cp().start()

        def pass_on(a):
            def act(src, dst, sems):
                _, _, arrive, onward, _ = self._copies(src, dst, sems, a)
                for k in range(N_CHIPS - 1):
                    arrive[k]().wait_recv()
                    onward[k]().start()
            return act

        def finish(src, dst, sems):
            for a in range(n):
                own, out, _, onward, landed = self._copies(src, dst, sems, a)
                for cp in landed:
                    cp().wait_recv()
                for cp in out + onward:
                    cp().wait_send()
                own().wait()

        mids = [(min(steps - 1, int(self.pass_on[a] * steps)), pass_on(a)) for a in range(n)]
        return [(0, start)] + mids + [(steps - 1, finish)]

    def results(self, outs):
        return [o.reshape((N_CHIPS,) + a.shape) for o, a in zip(outs, self.shards)]


class _ExchangeCargo:
    def __init__(self, sums, whole):
        self.inputs, self.whole, n = sums, whole, len(sums)
        self.out_shape = [jax.ShapeDtypeStruct((N_CHIPS - 1,) + g.shape[1:], g.dtype) for g in sums]
        sem = pltpu.SemaphoreType.DMA((N_CHIPS - 1, n))
        self.sems = [sem, sem]

    def _copies(self, src, dst, sems):
        send_sems, recv_sems = sems
        p = _chip_index()
        return [pltpu.make_async_remote_copy(
            src_ref=src[a].at[0] if self.whole[a] else src[a].at[p ^ k], dst_ref=dst[a].at[k - 1],
            send_sem=send_sems.at[k - 1, a], recv_sem=recv_sems.at[k - 1, a], device_id=_peer(k), device_id_type=MESH)
            for k in range(1, N_CHIPS) for a in range(len(self.inputs))]

    def stages(self, steps):
        def start(src, dst, sems):
            for cp in self._copies(src, dst, sems):
                cp.start()

        def finish(src, dst, sems):
            copies = self._copies(src, dst, sems)
            for cp in copies:
                cp.wait_recv()
            for cp in copies:
                cp.wait_send()

        return [(0, start), (steps - 1, finish)]

    def results(self, outs):
        return list(outs)


class _SwapCargo:
    def __init__(self, split):
        self.inputs, n = split, len(split)
        self.out_shape = [jax.ShapeDtypeStruct((g.shape[0],) + g.shape[2:], g.dtype) for g in split]
        sem = pltpu.SemaphoreType.DMA((n,))
        self.sems = [sem, sem]

    def _copies(self, src, dst, sems):
        c = lax.axis_index("c")
        return [pltpu.make_async_remote_copy(src_ref=src[a].at[:, 1 - c], dst_ref=dst[a], send_sem=sems[0].at[a],
                                             recv_sem=sems[1].at[a], device_id=_sibling(), device_id_type=MESH)
                for a in range(len(self.inputs))]

    def stages(self, steps):
        def start(src, dst, sems):
            for cp in self._copies(src, dst, sems):
                cp.start()

        def finish(src, dst, sems):
            copies = self._copies(src, dst, sems)
            for cp in copies:
                cp.wait_recv()
            for cp in copies:
                cp.wait_send()

        return [(0, start), (steps - 1, finish)]

    def results(self, outs):
        return list(outs)


class _MultiCargo:
    def __init__(self, cargos):
        self.cargos = cargos
        self.inputs = [a for cg in cargos for a in cg.inputs]
        self.out_shape = [o for cg in cargos for o in cg.out_shape]
        self.sems = [s for cg in cargos for s in cg.sems]

    def _parts(self, refs, count):
        out, off = [], 0
        for cg in self.cargos:
            out.append(refs[off:off + count(cg)])
            off += count(cg)
        return out

    def stages(self, steps):
        merged = []
        for n, cg in enumerate(self.cargos):
            for at, act in cg.stages(steps):
                def part(src, dst, sems, n=n, act=act):
                    act(self._parts(src, lambda g: len(g.inputs))[n], self._parts(dst, lambda g: len(g.out_shape))[n],
                        self._parts(sems, lambda g: len(g.sems))[n])
                merged.append((at, part))
        starts = [m for m in merged if m[0] == 0]
        return starts + sorted([m for m in merged if m[0] != 0], key=lambda m: m[0])

    def results(self, outs):
        return [cg.results(part) for cg, part in zip(self.cargos, self._parts(list(outs), lambda g: len(g.out_shape)))]


def _cargo_call(body, cargo, *, name, grid, in_specs, out_specs, out_shape, scratch=()):
    n_in, n_out, n_scr = len(in_specs), len(out_specs), len(scratch)
    c_in, c_out = len(cargo.inputs), len(cargo.out_shape)
    steps = 1
    for g in grid:
        steps *= g
    stages = cargo.stages(steps)

    def wrapped(*refs):
        ins, refs = refs[:n_in], refs[n_in:]
        cin, refs = refs[:c_in], refs[c_in:]
        outs, refs = refs[:n_out], refs[n_out:]
        cout, refs = refs[:c_out], refs[c_out:]
        scr, sems = refs[:n_scr], refs[n_scr:]
        if not grid:
            for _, act in stages:
                act(cin, cout, sems)
            return
        step = 0
        for d in range(len(grid)):
            step = step * grid[d] + pl.program_id(d)
        for at, act in stages:
            if at == 0:
                pl.when(step == 0)(functools.partial(act, cin, cout, sems))
        body(*ins, *outs, *scr)
        for at, act in stages:
            if at > 0:
                pl.when(step == at)(functools.partial(act, cin, cout, sems))

    params = dict(vmem_limit_bytes=VMEM_LIMIT)
    if grid:
        params["dimension_semantics"] = ("arbitrary",) * len(grid)
    res = pl.pallas_call(
        wrapped, name=name, grid=grid, in_specs=list(in_specs) + [ANY] * c_in, out_specs=list(out_specs) + [ANY] * c_out,
        out_shape=list(out_shape) + cargo.out_shape, scratch_shapes=list(scratch) + cargo.sems,
        compiler_params=pltpu.CompilerParams(**params))
    return lambda *args: (lambda r: (r[:n_out], cargo.results(r[n_out:])))(res(*args, *cargo.inputs))


def _split_halves(g):
    return g.reshape(g.shape[0], 2, g.shape[1] // 2, g.shape[2])


def _ew_tile(rows):
    return rows if rows <= 384 else 256


def _batched_call(items, name, place=None, cargo=None):
    offs, total = [], 0
    for it in items:
        offs.append(total)
        total += it[0]
    in_specs, out_specs, out_shape, args, spans = [], [], [], [], []
    for (n, ins, outs, shapes, operands, _), off in zip(items, offs):
        tile = lambda t, off=off, n=n: jnp.clip(t - off, 0, n - 1)
        i_s, o_s = ins(tile), outs(tile)
        spans.append((len(i_s), len(o_s)))
        in_specs += i_s
        out_specs += o_s
        out_shape += shapes
        args += operands
    n_in = len(in_specs)

    def body(*refs):
        if place is not None:
            refs = refs[1:]
        t = pl.program_id(0)
        i0, o0 = 0, n_in
        for (n, _, _, _, _, compute), off, (ni, no) in zip(items, offs, spans):
            pl.when((t >= off) & (t < off + n))(functools.partial(compute, refs[i0:i0 + ni], refs[o0:o0 + no], t - off))
            i0, o0 = i0 + ni, o0 + no

    params = pltpu.CompilerParams(dimension_semantics=("arbitrary",), vmem_limit_bytes=VMEM_LIMIT)
    carried = None
    if cargo is not None:
        res, carried = _cargo_call(body, cargo, name=name, grid=(total,), in_specs=in_specs, out_specs=out_specs,
                                   out_shape=out_shape)(*args)
    elif place is None:
        res = pl.pallas_call(body, name=name, grid=(total,), in_specs=in_specs, out_specs=out_specs,
                             out_shape=out_shape, compiler_params=params)(*args)
    else:
        res = pl.pallas_call(
            body, name=name, out_shape=out_shape, compiler_params=params,
            grid_spec=pltpu.PrefetchScalarGridSpec(num_scalar_prefetch=1, grid=(total,), in_specs=in_specs,
                                                   out_specs=out_specs))(place, *args)
    out, o0 = [], 0
    for _, no in spans:
        out.append(res[o0:o0 + no])
        o0 += no
    return out if cargo is None else (out, carried)


def _add_pairs(pairs, place, name):
    def item(mine, theirs):
        lead, _, rows, cols = mine.shape
        tm = _ew_tile(rows)

        def compute(ins, outs, tile):
            outs[0][...] = (ins[0][...].astype(f32) + ins[1][...].astype(f32)).astype(outs[0].dtype)

        blk = lambda tile: pl.BlockSpec((lead, tm, cols), lambda t, pc: (0, tile(t), 0))
        mine_half = lambda tile: pl.BlockSpec((lead, None, tm, cols), lambda t, pc: (0, pc[1], tile(t), 0))
        return (rows // tm, lambda tile: [mine_half(tile), blk(tile)], lambda tile: [blk(tile)],
                [jax.ShapeDtypeStruct(theirs.shape, theirs.dtype)], [mine, theirs], compute)

    return [out[0] for out in _batched_call([item(a, b) for a, b in pairs], name, place)]


def _cargo_alone(cargo, name):
    return _cargo_call(None, cargo, name=name, grid=(), in_specs=[], out_specs=[], out_shape=[])()[1]


def _bf16_casts(x, mats, padded, cargo, name):
    def item(a, to_rows, tm):
        rows, cols = a.shape
        full = rows // tm

        def compute(ins, outs, tile):
            outs[0][...] = jnp.where(tile < full, ins[0][...], 0.0).astype(bf16)

        return (to_rows // tm, lambda tile: [pl.BlockSpec((tm, cols), lambda t: (jnp.minimum(tile(t), full - 1), 0))],
                lambda tile: [pl.BlockSpec((tm, cols), lambda t: (tile(t), 0))],
                [jax.ShapeDtypeStruct((to_rows, cols), bf16)], [a], compute)

    items = [item(x, x.shape[0], 128)] + [item(w, r, 64 if r > w.shape[0] else 128) for w, r in zip(mats, padded)]
    outs, carried = _batched_call(items, name, cargo=cargo)
    return [o[0] for o in outs], carried


def _sum_slots(jobs, place, name):
    def item(own, recv, whole):
        _, rows, cols = own.shape
        tm = _ew_tile(rows)

        def compute(ins, outs, tile):
            r0, r1, r2 = [ins[1][k].astype(f32) for k in range(N_CHIPS - 1)]
            outs[0][...] = (ins[0][...].astype(f32) + r0) + (r1 + r2)

        ins = lambda tile: [pl.BlockSpec((None, tm, cols), lambda t, pc: (0 if whole else pc[0], tile(t), 0)),
                            pl.BlockSpec((N_CHIPS - 1, tm, cols), lambda t, pc: (0, tile(t), 0))]
        outs = lambda tile: [pl.BlockSpec((None, tm, cols), lambda t, pc: (pc[1], tile(t), 0))]
        return rows // tm, ins, outs, [jax.ShapeDtypeStruct((2, rows, cols), f32)], [own, recv], compute

    return [out[0] for out in _batched_call([item(*job) for job in jobs], name, place)]


def _sibling_fill(bufs):
    n = len(bufs)

    def body(*refs):
        src, dst = refs[:n], refs[n:2 * n]
        send_sems, recv_sems = refs[2 * n:]
        c = lax.axis_index("c")
        remote = [pltpu.make_async_remote_copy(src_ref=src[a].at[c], dst_ref=dst[a].at[c], send_sem=send_sems.at[a],
                                               recv_sem=recv_sems.at[a], device_id=_sibling(), device_id_type=MESH)
                  for a in range(n)]
        for cp in remote:
            cp.start()
        for a in range(n):
            pltpu.make_async_remote_copy(src_ref=src[a].at[c], dst_ref=dst[a].at[1 - c], send_sem=send_sems.at[a],
                                         recv_sem=recv_sems.at[a], device_id=_sibling(), device_id_type=MESH).wait_recv()
        for cp in remote:
            cp.wait_send()

    sem = pltpu.SemaphoreType.DMA((n,))
    out = pl.pallas_call(
        body, name="sibling_fill", in_specs=[ANY] * n, out_specs=[ANY] * n,
        out_shape=[jax.ShapeDtypeStruct(b.shape, b.dtype) for b in bufs],
        input_output_aliases={a: a for a in range(n)}, scratch_shapes=[sem, sem])(*bufs)
    return [o.reshape(2 * b.shape[1], b.shape[2]) for o, b in zip(out, bufs)]


class _GradReducer:
    def __init__(self, place):
        self.place, self.split, self.sums, self.slots = place, {}, {}, {}

    def swap(self, named):
        self.split.update({nm: _split_halves(g) for nm, g in named})
        return _SwapCargo([self.split[nm] for nm, _ in named])

    def summed(self, names, got, tag):
        sums = _add_pairs([(self.split[nm], g) for nm, g in zip(names, got)], self.place, "presum_" + tag)
        self.sums.update(zip(names, sums))

    def exchange(self, names):
        return _ExchangeCargo([self.sums[nm] for nm in names], [nm == "small" for nm in names])

    def arrived(self, names, slots):
        self.slots.update(zip(names, slots))

    def reduce(self, names):
        jobs = [(self.sums[nm], self.slots[nm], nm == "small") for nm in names]
        return _sibling_fill(_sum_slots(jobs, self.place, "sum_slots"))


def _adamw_update(w, m, v, g):
    m = ADAM_B1 * m + (1.0 - ADAM_B1) * g
    v = ADAM_B2 * v + (1.0 - ADAM_B2) * jnp.square(g)
    m_hat = m / (1.0 - ADAM_B1 ** ADAM_STEP)
    v_hat = v / (1.0 - ADAM_B2 ** ADAM_STEP)
    return -ADAM_LR * (m_hat / (jnp.sqrt(v_hat) + ADAM_EPS) + ADAM_WD * w), m, v


def _adamw(jobs, name):
    def item(w, m, v, grad, tm):
        rows, cols = w.shape
        gcols = grad.shape[1]

        def compute(ins, outs, tile):
            g = ins[3][0:tm, 0:cols]
            outs[0][...] = g
            outs[1][...], outs[2][...], outs[3][...] = _adamw_update(ins[0][...], ins[1][...], ins[2][...], g)

        blk = lambda tile: pl.BlockSpec((tm, cols), lambda t: (tile(t), 0))
        if tm == rows:
            gspec = lambda tile: pl.BlockSpec(grad.shape, lambda t: (0, 0))
        else:
            gspec = lambda tile: pl.BlockSpec((tm, gcols), lambda t: (tile(t), 0))
        return (rows // tm, lambda tile: [blk(tile)] * 3 + [gspec(tile)], lambda tile: [blk(tile)] * 4,
                [jax.ShapeDtypeStruct((rows, cols), f32)] * 4, [w, m, v, grad], compute)

    return _batched_call([item(*job) for job in jobs], name)


def _rope_tables(s):
    half = HEAD_DIM // 2
    inv_freq = 1.0 / (10000.0 ** (jnp.arange(half, dtype=f32) / half))
    ang = jnp.arange(s, dtype=f32)[:, None] * inv_freq[None, :]
    cos, sin = jnp.cos(ang), jnp.sin(ang)
    return jnp.tile(cos, (1, LANES // half)), jnp.tile(jnp.concatenate([-sin, sin], axis=1), (1, LANES // HEAD_DIM))


def _local_step(x, x_bf, target, win_g, gather, comm, b_gate, w_pool, pool_scale, ln1_g, ln1_b, convb_g, ln2_g, ln2_b):
    s = x.shape[0]
    nb = s // MOBA_BLOCK
    cos_t, sin_t = _rope_tables(s)

    z, (wba_g, wbp_g, wout_g) = _inproj(x_bf, win_g, cos_t, sin_t, b_gate, gather["inproj"])
    wba = wba_g.transpose(1, 0, 2).reshape(D_ATTN, D_MODEL)
    wbp = wbp_g.transpose(1, 0, 2).reshape(D_POOL, D_MODEL)
    wout = wout_g.reshape(D_MODEL, D_MODEL)
    (o_bf, lse, pooled_bf, mixed, pm_bf), (wg_g, wu_g, convw_g) = _attn_fwd(z, nb, w_pool, pool_scale,
                                                                           gather["attn_fwd"])
    m_bf = _branch_merge(o_bf, pm_bf, z, wba, wbp)
    x1_bf, xhat1, rstd1 = _out_ln1(m_bf, wout, x, ln1_g, ln1_b)
    (a, up, ac, hh_bf), (wd_g,) = _ffn_up(x1_bf, wg_g, wu_g, convw_g, convb_g, gather["ffn_up"])
    wd = wd_g.reshape(D_FF_PAD, D_MODEL)
    loss, dh2, dh2_bf, d_ln2_g, d_ln2_b = _ffn_down_ln2_loss(hh_bf, wd, xhat1, ln1_g, ln1_b, ln2_g, ln2_b, target)

    da_bf, dup_bf, dconv = _ffn_act_bwd(dh2_bf, wd_g, ac, a, up, convw_g)
    ffn_shape = (N_CHIPS, FF_PAD, D_MODEL)
    d_wd = _matmul_tn(hh_bf, dh2_bf, 1, "dw_ffn_down").reshape(ffn_shape)
    d_wg, got = _matmul_tn(da_bf, x1_bf, 1, "dw_ffn_gate", cargo=comm.swap([("w_ffn_down", d_wd)]))
    comm.summed(["w_ffn_down"], got, "ffn_down")
    d_wg = d_wg.reshape(ffn_shape)
    d_wu = _matmul_tn(dup_bf, x1_bf, 1, "dw_ffn_up").reshape(ffn_shape)
    gate_up = ["w_ffn_gate", "w_ffn_up", "conv_w"]
    (dh1, dh1_bf, d_ln1_g, d_ln1_b), (slots, got) = _ffn_in_bwd_ln1(
        da_bf, dup_bf, wg_g, wu_g, dh2, xhat1, rstd1, ln1_g,
        _MultiCargo([comm.exchange(["w_ffn_down"]), comm.swap(list(zip(gate_up, [d_wg, d_wu, dconv])))]))
    comm.arrived(["w_ffn_down"], slots)
    comm.summed(gate_up, got, "ffn_gate_up")
    d_wout = _matmul_tn(m_bf, dh1_bf, 1, "dw_out").reshape(N_CHIPS, D_MODEL // N_CHIPS, D_MODEL)
    dzg_bf, dya_bf, dyp_bf, do_bf, dpm, d_bgate = _merge_bwd(dh1_bf, wout, o_bf, pm_bf, z, wba, wbp)
    d_wba = _matmul_tn(o_bf, dya_bf, N_CHIPS, "dw_branch_attn")
    d_wbp = _matmul_tn(pm_bf, dyp_bf, N_CHIPS, "dw_branch_pool")
    du_bf, d_wpool, d_pscale = _pool_bwd(dpm, mixed, pooled_bf, w_pool, pool_scale)
    d_convb = dconv[:, 3, :FF_SHARD].reshape(1, N_CHIPS * FF_SHARD)
    small = _pack_small([d_bgate, d_wpool.reshape(-1, POOL_GROUP), d_pscale, d_ln1_g, d_ln1_b, d_ln2_g, d_ln2_b,
                         d_convb], loss)[None]
    branch = ["w_out", "w_branch_attn", "w_branch_pool", "small"]
    (dqkv_bf,), (slots, got) = _attn_bwd(
        z, o_bf, lse, do_bf, cos_t, sin_t, nb,
        _MultiCargo([comm.exchange(gate_up), comm.swap(list(zip(branch, [d_wout, d_wba, d_wbp, small])))]))
    comm.arrived(gate_up, slots)
    comm.summed(branch, got, "branch")
    d_win, slots = _dw_in(x_bf, dqkv_bf, du_bf, dzg_bf, comm.exchange(branch))
    comm.arrived(branch, slots)
    comm.summed(["w_in"], _cargo_alone(comm.swap([("w_in", d_win)]), "swap_w_in"), "w_in")
    (grad_x,), slots = _inproj_dx(dqkv_bf, du_bf, dzg_bf, win_g, dh1, comm.exchange(["w_in"]))
    comm.arrived(["w_in"], slots)
    return grad_x


SMALL_WEIGHTS = [("b_gate", 2 * D_MODEL), ("w_pool", 4 * POOL_GROUP * POOL_GROUP), ("pool_scale", D_POOL),
                 ("ln1_g", D_MODEL), ("ln1_b", D_MODEL), ("ln2_g", D_MODEL), ("ln2_b", D_MODEL),
                 ("conv_b", N_CHIPS * FF_SHARD)]
LOSS_ROW = 588
SMALL_ROWS = 592


def _small_pieces(ref, first_row, n):
    if ref.shape[0] > 1:
        return [(slice(first_row, first_row + n // LANES), (slice(None), slice(None)))]
    return [(slice(first_row + i, first_row + i + 1), (slice(None), slice(i * LANES, (i + 1) * LANES)))
            for i in range(n // LANES)]


def _small_first_rows():
    rows, r = [], 0
    for _, n in SMALL_WEIGHTS:
        rows.append(r)
        r += n // LANES
    return rows


def _pack_small(parts, loss):
    def body(*refs):
        out = refs[-1]
        out[...] = jnp.zeros_like(out)
        for src, r0, (_, n) in zip(refs, _small_first_rows(), SMALL_WEIGHTS):
            for rows, where in _small_pieces(src, r0, n):
                out[rows, :] = src[where]
        out[LOSS_ROW:LOSS_ROW + 1, :] = refs[len(parts)][0:1, :]

    return pl.pallas_call(body, name="pack_small", out_shape=jax.ShapeDtypeStruct((SMALL_ROWS, LANES), f32))(
        *parts, loss)


def _adamw_small(g_packed, triples):
    n_in = 3 * len(triples)

    def body(g_ref, *refs):
        for k, (r0, (_, n)) in enumerate(zip(_small_first_rows(), SMALL_WEIGHTS)):
            w_ref, m_ref, v_ref = refs[3 * k:3 * k + 3]
            outs = refs[n_in + 4 * k:n_in + 4 * k + 4]
            for rows, where in _small_pieces(w_ref, r0, n):
                g = g_ref[rows, :]
                for o_ref, val in zip(outs, (g,) + _adamw_update(w_ref[where], m_ref[where], v_ref[where], g)):
                    o_ref[where] = val

    res = pl.pallas_call(
        body, name="adamw_small",
        out_shape=[jax.ShapeDtypeStruct(t[0].shape, f32) for t in triples for _ in range(4)],
    )(g_packed, *[a for t in triples for a in t])
    return [res[4 * k:4 * k + 4] for k in range(len(triples))]


def _pad_conv_b(cb):
    return jnp.pad(cb.reshape(N_CHIPS, FF_SHARD), ((0, 0), (0, FF_PAD - FF_SHARD)))


def kernel(x, w_in, b_gate, w_branch_attn, w_pool, pool_scale, w_branch_pool, w_out, ln1_g, ln1_b, w_ffn_gate, w_ffn_up, conv_w, conv_b, w_ffn_down, ln2_g, ln2_b, loss_target, m_w_in, m_b_gate, m_w_branch_attn, m_w_pool, m_pool_scale, m_w_branch_pool, m_w_out, m_ln1_g, m_ln1_b, m_w_ffn_gate, m_w_ffn_up, m_conv_w, m_conv_b, m_w_ffn_down, m_ln2_g, m_ln2_b, v_w_in, v_b_gate, v_w_branch_attn, v_w_pool, v_pool_scale, v_w_branch_pool, v_w_out, v_ln1_g, v_ln1_b, v_w_ffn_gate, v_w_ffn_up, v_conv_w, v_conv_b, v_w_ffn_down, v_ln2_g, v_ln2_b):
    mats = [w_branch_attn[0], w_branch_pool[0], w_out[0], w_ffn_gate[0].T, w_ffn_up[0].T, w_ffn_down[0]]
    padded = [w.shape[0] for w in mats[:3]] + [FF_PAD] * 3
    (x_bf, *casted), (win_g,) = _bf16_casts(x[0], mats, padded, _GatherCargo([w_in[0].astype(bf16)], [0.9]),
                                            "gather_w_in")
    shards = [None] + casted + [jnp.pad(conv_w[0], ((0, CONV_ROWS - 3), (0, FF_PAD - FF_SHARD)))]
    gather = {"inproj": _GatherCargo(shards[1:4], [0.99, 0.99, 0.99]),
              "attn_fwd": _GatherCargo(shards[4:6] + shards[7:], [0.97, 0.97, 0.97]),
              "ffn_up": _GatherCargo(shards[6:7], [0.99])}

    place = jnp.stack([2 * lax.axis_index("x") + lax.axis_index("y"), lax.axis_index("c")]).astype(jnp.int32)
    comm = _GradReducer(place)
    convb_g = _pad_conv_b(conv_b).reshape(N_CHIPS, 1, FF_PAD)
    grad_x = _local_step(
        x[0], x_bf, loss_target[0], win_g, gather, comm, b_gate, w_pool[0], pool_scale, ln1_g, ln1_b, convb_g, ln2_g, ln2_b)
    names = ["w_in", "w_branch_attn", "w_branch_pool", "w_out", "w_ffn_gate", "w_ffn_up", "w_ffn_down", "conv_w", "small"]
    grads = comm.reduce(names)

    weights = [w_in[0], w_branch_attn[0], w_branch_pool[0], w_out[0], w_ffn_gate[0].T, w_ffn_up[0].T, w_ffn_down[0],
               conv_w[0]]
    m_in = [m_w_in[0], m_w_branch_attn[0], m_w_branch_pool[0], m_w_out[0], m_w_ffn_gate[0].T, m_w_ffn_up[0].T,
            m_w_ffn_down[0], m_conv_w[0]]
    v_in = [v_w_in[0], v_w_branch_attn[0], v_w_branch_pool[0], v_w_out[0], v_w_ffn_gate[0].T, v_w_ffn_up[0].T,
            v_w_ffn_down[0], v_conv_w[0]]
    tiles = [256, 256, 256, 128, 176, 176, 176, 3]
    jobs = list(zip(weights, m_in, v_in, grads, tiles))
    ffn = slice(4, 7)
    updated = _adamw(jobs[:4] + jobs[7:], "adamw_rest")
    updated = updated[:4] + _adamw(jobs[ffn], "adamw_ffn") + updated[4:]
    res = {nm: [(r.T if nm in ("w_ffn_gate", "w_ffn_up") else r)[None] for r in outs4]
           for nm, outs4 in zip(names[:-1], updated)}

    small_w = [b_gate, w_pool, pool_scale, ln1_g, ln1_b, ln2_g, ln2_b, conv_b]
    small_m = [m_b_gate, m_w_pool, m_pool_scale, m_ln1_g, m_ln1_b, m_ln2_g, m_ln2_b, m_conv_b]
    small_v = [v_b_gate, v_w_pool, v_pool_scale, v_ln1_g, v_ln1_b, v_ln2_g, v_ln2_b, v_conv_b]
    as_rows = lambda a: a.reshape(-1, POOL_GROUP) if a.ndim == 4 else a
    small_out = _adamw_small(grads[-1], [tuple(as_rows(a) for a in t) for t in zip(small_w, small_m, small_v)])
    for (nm, _), w, outs4 in zip(SMALL_WEIGHTS, small_w, small_out):
        res[nm] = [o.reshape(w.shape) for o in outs4]
    loss = grads[-1][LOSS_ROW, 0]

    order = ["w_in", "b_gate", "w_branch_attn", "w_pool", "pool_scale", "w_branch_pool", "w_out", "ln1_g", "ln1_b",
             "w_ffn_gate", "w_ffn_up", "conv_w", "conv_b", "w_ffn_down", "ln2_g", "ln2_b"]
    outs = [loss, grad_x[None]]
    for kind in range(4):
        outs += [res[nm][kind] for nm in order]
    return tuple(outs)
```

```python
import functools

import jax
import jax.numpy as jnp
from jax import lax
from jax.experimental import pallas as pl
from jax.experimental.pallas import tpu as pltpu

D_MODEL = 1024
HEAD_DIM = 64
D_ATTN = 512
D_POOL = 512
MOBA_BLOCK = 256
MOBA_TOPK = 3
POOL_GROUP = 128
MAX_WINDOW = 16
FF_SHARD = 704
FF_PAD = 768
D_FF_PAD = 4 * FF_PAD
N_CHIPS = 4
LANES = 128
ALPHA = (2.0 * 1) ** 0.25
LN_EPS = 1e-5
NEG = -1e30
SCALE = HEAD_DIM ** -0.5
ADAM_LR, ADAM_B1, ADAM_B2, ADAM_EPS, ADAM_WD, ADAM_STEP = 0.001, 0.9, 0.999, 1e-08, 0.01, 10
VMEM_LIMIT = 56 * 1024 * 1024
MESH = pl.DeviceIdType.MESH

bf16 = jnp.bfloat16
f32 = jnp.float32


def _dot(a, b):
    return jnp.dot(a, b, preferred_element_type=f32)


def _dot_nt(a, b):
    return lax.dot_general(a, b, (((1,), (1,)), ((), ())), preferred_element_type=f32)


def _dot_tn(a, b):
    return lax.dot_general(a, b, (((0,), (0,)), ((), ())), preferred_element_type=f32)


def _call(body, *, name, grid, in_specs, out_specs, out_shape, scratch=(), cargo=None):
    if cargo is not None:
        return _cargo_call(body, cargo, name=name, grid=grid, in_specs=in_specs, out_specs=out_specs,
                           out_shape=out_shape, scratch=scratch)
    return pl.pallas_call(
        body, name=name, grid=grid, in_specs=in_specs, out_specs=out_specs, out_shape=out_shape,
        scratch_shapes=list(scratch),
        compiler_params=pltpu.CompilerParams(dimension_semantics=("arbitrary",) * len(grid),
                                             vmem_limit_bytes=VMEM_LIMIT))


def _rot_half(t):
    lane = lax.broadcasted_iota(jnp.int32, t.shape, 1)
    first = (lane % HEAD_DIM) < (HEAD_DIM // 2)
    return jnp.where(first, pltpu.roll(t, LANES - HEAD_DIM // 2, 1), pltpu.roll(t, HEAD_DIM // 2, 1))


def _rope(t, cos, sin_signed):
    return t * cos + _rot_half(t) * sin_signed


def _rope_bwd(d, cos, sin_signed):
    return d * cos + _rot_half(d * sin_signed)


def _gelu_parts(a):
    cdf = 0.5 * (1.0 + lax.erf(a * (2.0 ** -0.5)))
    pdf = jnp.exp(-0.5 * a * a) * ((2.0 * jnp.pi) ** -0.5)
    return a * cdf, cdf + a * pdf


def _layer_norm(h, g, b):
    mu = jnp.mean(h, axis=-1, keepdims=True)
    xc = h - mu
    var = jnp.mean(xc * xc, axis=-1, keepdims=True)
    rstd = lax.rsqrt(var + LN_EPS)
    xhat = xc * rstd
    return xhat * g + b, xhat, rstd


def _layer_norm_bwd(dy, xhat, rstd, g):
    dxh = dy * g
    m1 = jnp.mean(dxh, axis=-1, keepdims=True)
    m2 = jnp.mean(dxh * xhat, axis=-1, keepdims=True)
    return rstd * (dxh - m1 - xhat * m2)


def _inproj(x_bf, win_g, cos_t, sin_t, b_gate, cargo):
    s = x_bf.shape[0]
    tm, tn = 1024, 512

    def body(x_ref, w_ref, cos_ref, sin_ref, b_ref, o_ref):
        j = pl.program_id(1)
        acc = _dot(x_ref[...], w_ref[...])

        @pl.when(j < 2)
        def _():
            for c in range(tn // LANES):
                sl = slice(c * LANES, (c + 1) * LANES)
                o_ref[:, sl] = _rope(acc[:, sl], cos_ref[...], sin_ref[...])

        @pl.when((j >= 2) & (j < 4))
        def _():
            o_ref[...] = acc

        @pl.when(j >= 4)
        def _():
            o_ref[...] = jax.nn.sigmoid(acc + b_ref[...])

    (z,), carried = _cargo_call(
        body, cargo, name="inproj", grid=(s // tm, 4 * D_MODEL // tn),
        in_specs=[pl.BlockSpec((tm, D_MODEL), lambda i, j: (i, 0)),
                  pl.BlockSpec((None, D_MODEL, tn), lambda i, j: (j // 2, 0, j % 2)),
                  pl.BlockSpec((tm, LANES), lambda i, j: (i, 0)),
                  pl.BlockSpec((tm, LANES), lambda i, j: (i, 0)),
                  pl.BlockSpec((1, tn), lambda i, j: (0, jnp.maximum(j - 4, 0)))],
        out_specs=[pl.BlockSpec((tm, tn), lambda i, j: (i, j))],
        out_shape=[jax.ShapeDtypeStruct((s, 4 * D_MODEL), f32)])(x_bf, win_g, cos_t, sin_t, b_gate)
    return z, carried


STAT_ROWS = 8


def _pair_rows(v0, v1, fill):
    r = lax.broadcasted_iota(jnp.int32, (STAT_ROWS, v0.shape[1]), 0)
    return jnp.where(r == 0, v0, jnp.where(r == 1, v1, fill))


def _head_lanes(shape):
    lane = lax.broadcasted_iota(jnp.int32, shape, 1)
    return lane < HEAD_DIM, lane >= HEAD_DIM


def _head_rows(shape):
    row = lax.broadcasted_iota(jnp.int32, shape, 0)
    return row < HEAD_DIM, row >= HEAD_DIM


def _moba_select(q_bf, k_all, i, nb):
    k_mean = jnp.mean(k_all.reshape(nb, MOBA_BLOCK, LANES), axis=1)
    n_io = lax.broadcasted_iota(jnp.int32, (nb, MOBA_BLOCK), 0)
    past = n_io < i
    sels = []
    for head in _head_lanes((nb, LANES)):
        gate = _dot_nt(jnp.where(head, k_mean, 0.0).astype(bf16), q_bf)
        g = jnp.where(past, gate, NEG)
        rank = jnp.zeros((nb, MOBA_BLOCK), f32)
        for m in range(nb):
            gm = g[m:m + 1, :]
            rank = rank + jnp.where((gm > g) | ((gm == g) & (m < n_io)), 1.0, 0.0)
        sels.append(jnp.where(past & (rank < MOBA_TOPK), 1.0, 0.0))
    return sels


def _attn_fwd(z, nb, w_pool, pool_scale, cargo):
    s = z.shape[0]

    def body(q_ref, k_ref, v_ref, u_ref, wp_ref, sc_ref, o_ref, lse_ref, pooled_ref, mixed_ref, pm_ref, vt_s, pad):
        pair, i = pl.program_id(0), pl.program_id(1)

        @pl.when(i == 0)
        def _():
            for j in range(nb):
                vt_s[j] = v_ref[j * MOBA_BLOCK:(j + 1) * MOBA_BLOCK, :].T.astype(bf16)
            _pool_group(pair, u_ref, wp_ref, sc_ref, pooled_ref, mixed_ref, pm_ref, pad)

        k_heads = _head_lanes((MOBA_BLOCK, LANES))
        o_heads = _head_rows((LANES, MOBA_BLOCK))

        def block(j, qs, allow, m_old, l_old):
            kj, vtj = k_ref[j * MOBA_BLOCK:(j + 1) * MOBA_BLOCK, :], vt_s[j]
            m_new, alpha, l_new, pv = [], [], [], None
            for h in range(2):
                sc = jnp.where(allow[h], _dot_nt(jnp.where(k_heads[h], kj, 0.0).astype(bf16), qs), NEG)
                mx = jnp.max(sc, axis=0, keepdims=True)
                mn = mx if m_old is None else jnp.maximum(m_old[h], mx)
                p = jnp.exp(sc - mn)
                lsum = jnp.sum(p, axis=0, keepdims=True)
                if m_old is None:
                    a = None
                else:
                    a = jnp.exp(m_old[h] - mn)
                    lsum = a * l_old[h] + lsum
                t = _dot(jnp.where(o_heads[h], vtj, jnp.zeros_like(vtj)), p.astype(bf16))
                pv = t if pv is None else pv + t
                m_new.append(mn)
                alpha.append(a)
                l_new.append(lsum)
            return m_new, alpha, l_new, pv

        def query_block(c):
            q = q_ref[...]
            sels = _moba_select(q.astype(bf16), k_ref[...], c, nb)
            qs = (q * SCALE).astype(bf16)
            key = lax.broadcasted_iota(jnp.int32, (MOBA_BLOCK, MOBA_BLOCK), 0)
            qry = lax.broadcasted_iota(jnp.int32, (MOBA_BLOCK, MOBA_BLOCK), 1)
            m, _, l, acc = block(c, qs, [key <= qry] * 2, None, None)
            for j in range(c):
                m, alpha, l, pv = block(j, qs, [sels[h][j:j + 1, :] > 0.0 for h in range(2)], m, l)
                acc = acc * jnp.where(o_heads[0], alpha[0], alpha[1]) + pv
            o_ref[...] = (acc / jnp.where(o_heads[0], l[0], l[1])).T.astype(bf16)
            lse_ref[...] = _pair_rows(m[0] + jnp.log(l[0]), m[1] + jnp.log(l[1]), 0.0)

        for c in range(nb):
            pl.when(i == c)(functools.partial(query_block, c))

    group = pl.BlockSpec((s, POOL_GROUP), lambda hp, i: (0, hp))
    return _cargo_call(
        body, cargo, name="attn_fwd", grid=(D_ATTN // LANES, nb),
        in_specs=[pl.BlockSpec((MOBA_BLOCK, LANES), lambda hp, i: (i, hp)),
                  pl.BlockSpec((s, LANES), lambda hp, i: (0, 4 + hp)),
                  pl.BlockSpec((s, LANES), lambda hp, i: (0, 8 + hp)),
                  pl.BlockSpec((s, POOL_GROUP), lambda hp, i: (0, 12 + hp)),
                  pl.BlockSpec((None, POOL_GROUP, POOL_GROUP), lambda hp, i: (hp, 0, 0)),
                  pl.BlockSpec((1, POOL_GROUP), lambda hp, i: (0, hp))],
        out_specs=[pl.BlockSpec((MOBA_BLOCK, LANES), lambda hp, i: (i, hp)),
                   pl.BlockSpec((None, STAT_ROWS, MOBA_BLOCK), lambda hp, i: (hp, 0, i)), group, group, group],
        out_shape=[jax.ShapeDtypeStruct((s, D_ATTN), bf16),
                   jax.ShapeDtypeStruct((D_ATTN // LANES, STAT_ROWS, s), f32),
                   jax.ShapeDtypeStruct((s, D_POOL), bf16), jax.ShapeDtypeStruct((s, D_POOL), f32),
                   jax.ShapeDtypeStruct((s, D_POOL), bf16)],
        scratch=[pltpu.VMEM((nb, LANES, MOBA_BLOCK), bf16), pltpu.VMEM((s + MAX_WINDOW, POOL_GROUP), f32)],
    )(z, z, z, z, w_pool, pool_scale)


def _window_select(snaps, g):
    return jnp.where(g == 0, snaps[0], jnp.where(g == 1, snaps[1], jnp.where(g == 2, snaps[2], snaps[3])))


def _pool_group(g, u_ref, w_ref, sc_ref, pooled_ref, mixed_ref, pm_ref, pad):
    s = u_ref.shape[0]
    u = u_ref[...]
    pad[0:MAX_WINDOW, :] = jnp.zeros((MAX_WINDOW, POOL_GROUP), f32)
    pad[MAX_WINDOW:MAX_WINDOW + s, :] = u
    acc = u
    snaps = []
    for d in range(1, MAX_WINDOW):
        acc = acc + pad[MAX_WINDOW - d:MAX_WINDOW - d + s, :]
        if d + 1 in (2, 4, 8, 16):
            snaps.append(acc)
    win = _window_select(snaps, g)
    t = lax.broadcasted_iota(jnp.int32, (s, POOL_GROUP), 0)
    count = jnp.minimum(t + 1, jnp.left_shift(2, g)).astype(f32)
    pooled = (win / count - u).astype(bf16)
    mixed = _dot(pooled, w_ref[...].astype(bf16))
    pooled_ref[...] = pooled
    mixed_ref[...] = mixed
    pm_ref[...] = (mixed * sc_ref[...]).astype(bf16)


def _branch_merge(o_bf, pm_bf, z, wba, wbp):
    s = o_bf.shape[0]
    tm = 512

    def body(o_ref, pm_ref, ga_ref, gp_ref, wba_ref, wbp_ref, m_ref):
        ya = _dot(o_ref[...], wba_ref[...])
        yp = _dot(pm_ref[...], wbp_ref[...])
        m_ref[...] = (ga_ref[...] * ya + gp_ref[...] * yp).astype(bf16)

    full = lambda r, c: pl.BlockSpec((r, c), lambda i: (0, 0))
    return _call(
        body, name="branch_merge", grid=(s // tm,),
        in_specs=[pl.BlockSpec((tm, D_ATTN), lambda i: (i, 0)), pl.BlockSpec((tm, D_POOL), lambda i: (i, 0)),
                  pl.BlockSpec((tm, D_MODEL), lambda i: (i, 2)), pl.BlockSpec((tm, D_MODEL), lambda i: (i, 3)),
                  full(D_ATTN, D_MODEL), full(D_POOL, D_MODEL)],
        out_specs=pl.BlockSpec((tm, D_MODEL), lambda i: (i, 0)),
        out_shape=jax.ShapeDtypeStruct((s, D_MODEL), bf16))(o_bf, pm_bf, z, z, wba, wbp)


def _out_ln1(m_bf, wout, x, ln_g, ln_b):
    s = x.shape[0]
    tm = 512

    def body(m_ref, w_ref, x_ref, g_ref, b_ref, x1_ref, xhat_ref, rstd_ref):
        h = ALPHA * x_ref[...] + _dot(m_ref[...], w_ref[...])
        y, xhat, rstd = _layer_norm(h, g_ref[...], b_ref[...])
        x1_ref[...] = y.astype(bf16)
        xhat_ref[...] = xhat
        rstd_ref[...] = jnp.broadcast_to(rstd, (tm, LANES))

    row = pl.BlockSpec((tm, D_MODEL), lambda i: (i, 0))
    vec = pl.BlockSpec((1, D_MODEL), lambda i: (0, 0))
    return _call(
        body, name="out_ln1", grid=(s // tm,),
        in_specs=[row, pl.BlockSpec((D_MODEL, D_MODEL), lambda i: (0, 0)), row, vec, vec],
        out_specs=[row, row, pl.BlockSpec((tm, LANES), lambda i: (i, 0))],
        out_shape=[jax.ShapeDtypeStruct((s, D_MODEL), bf16), jax.ShapeDtypeStruct((s, D_MODEL), f32),
                   jax.ShapeDtypeStruct((s, LANES), f32)])(m_bf, wout, x, ln_g, ln_b)


FF_TILE = 256
FF_TILES_PER_SHARD = FF_PAD // FF_TILE
CONV_PAD = 8
CONV_ROWS = 16

def _ff_weight_spec(rows):
    return pl.BlockSpec((None, rows, FF_TILE), lambda j: (j // FF_TILES_PER_SHARD, 0, j % FF_TILES_PER_SHARD))


def _ff_rows_spec():
    return pl.BlockSpec((None, FF_TILE, D_MODEL), lambda j: (j // FF_TILES_PER_SHARD, j % FF_TILES_PER_SHARD, 0))


def _row_shifts(x, shifts):
    s = x.shape[0]
    padded = jnp.concatenate([x, jnp.zeros((CONV_PAD, x.shape[1]), x.dtype)], axis=0)
    return [pltpu.roll(padded, d % (s + CONV_PAD), 0)[0:s] for d in shifts]


def _ffn_up(x1_bf, wg_g, wu_g, convw_g, convb_g, cargo):
    s = x1_bf.shape[0]

    def body(x_ref, wg_ref, wu_ref, cw_ref, cb_ref, a_ref, up_ref, ac_ref, hh_ref):
        x1 = x_ref[...]
        a = _dot_nt(x1, wg_ref[...])
        up = _dot_nt(x1, wu_ref[...])
        a_ref[...] = a
        up_ref[...] = up
        a1, a2 = _row_shifts(a, (1, 2))
        ac = a * cw_ref[2:3, :] + a1 * cw_ref[1:2, :] + a2 * cw_ref[0:1, :] + cb_ref[...]
        ac_ref[...] = ac
        hg, _ = _gelu_parts(ac)
        hh_ref[...] = (hg * up).astype(bf16)

    col = pl.BlockSpec((s, FF_TILE), lambda j: (0, j))
    wide = jax.ShapeDtypeStruct((s, D_FF_PAD), f32)
    return _call(
        body, name="ffn_up", grid=(D_FF_PAD // FF_TILE,),
        in_specs=[pl.BlockSpec((s, D_MODEL), lambda j: (0, 0)), _ff_rows_spec(), _ff_rows_spec(),
                  _ff_weight_spec(8), _ff_weight_spec(1)],
        out_specs=[col, col, col, col],
        out_shape=[wide, wide, wide, jax.ShapeDtypeStruct((s, D_FF_PAD), bf16)],
        cargo=cargo)(x1_bf, wg_g, wu_g, convw_g, convb_g)


def _ffn_down_ln2_loss(hh_bf, wd, xhat1, ln1_g, ln1_b, ln2_g, ln2_b, target):
    s = hh_bf.shape[0]
    tm = 256

    def body(hh_ref, w_ref, xh1_ref, g1_ref, b1_ref, g2_ref, b2_ref, t_ref, loss_ref, dh_ref, dhb_ref, dg_ref, db_ref):
        i = pl.program_id(0)
        x1 = xh1_ref[...] * g1_ref[...] + b1_ref[...]
        h = ALPHA * x1 + _dot(hh_ref[...], w_ref[...])
        y, xhat, rstd = _layer_norm(h, g2_ref[...], b2_ref[...])
        err = y - t_ref[...]
        part = 0.5 * jnp.sum(jnp.mean(err * err, axis=-1, keepdims=True), axis=0, keepdims=True)
        dy = err * (1.0 / D_MODEL)

        @pl.when(i == 0)
        def _():
            loss_ref[...] = jnp.zeros_like(loss_ref)
            dg_ref[...] = jnp.zeros_like(dg_ref)
            db_ref[...] = jnp.zeros_like(db_ref)

        loss_ref[...] += jnp.broadcast_to(part, loss_ref.shape)
        dg_ref[...] += jnp.sum(dy * xhat, axis=0, keepdims=True)
        db_ref[...] += jnp.sum(dy, axis=0, keepdims=True)
        dh = _layer_norm_bwd(dy, xhat, rstd, g2_ref[...])
        dh_ref[...] = dh
        dhb_ref[...] = dh.astype(bf16)

    row = pl.BlockSpec((tm, D_MODEL), lambda i: (i, 0))
    vec = pl.BlockSpec((1, D_MODEL), lambda i: (0, 0))
    return _call(
        body, name="ffn_down_ln2_loss", grid=(s // tm,),
        in_specs=[pl.BlockSpec((tm, D_FF_PAD), lambda i: (i, 0)),
                  pl.BlockSpec((D_FF_PAD, D_MODEL), lambda i: (0, 0)),
                  row, vec, vec, vec, vec, row],
        out_specs=[pl.BlockSpec((8, LANES), lambda i: (0, 0)), row, row, vec, vec],
        out_shape=[jax.ShapeDtypeStruct((8, LANES), f32), jax.ShapeDtypeStruct((s, D_MODEL), f32),
                   jax.ShapeDtypeStruct((s, D_MODEL), bf16), jax.ShapeDtypeStruct((1, D_MODEL), f32),
                   jax.ShapeDtypeStruct((1, D_MODEL), f32)])(hh_bf, wd, xhat1, ln1_g, ln1_b, ln2_g, ln2_b, target)


def _ffn_act_bwd(dh2_bf, wd_g, ac, a, up, convw_g):
    s = dh2_bf.shape[0]

    def body(dh_ref, wd_ref, ac_ref, a_ref, up_ref, cw_ref, da_ref, dup_ref, dc_ref):
        dhh = _dot_nt(dh_ref[...], wd_ref[...])
        hg, dgelu = _gelu_parts(ac_ref[...])
        dup_ref[...] = (dhh * hg).astype(bf16)
        dac = dhh * up_ref[...] * dgelu
        dac1, dac2 = _row_shifts(dac, (-1, -2))
        a = a_ref[...]
        rows = [jnp.sum(d * a, axis=0, keepdims=True) for d in (dac2, dac1, dac)]
        rows.append(jnp.sum(dac, axis=0, keepdims=True))
        rows.append(jnp.zeros((CONV_ROWS - 4, FF_TILE), f32))
        dc_ref[...] = jnp.concatenate(rows, axis=0)
        da_ref[...] = (dac * cw_ref[2:3, :] + dac1 * cw_ref[1:2, :] + dac2 * cw_ref[0:1, :]).astype(bf16)

    col = pl.BlockSpec((s, FF_TILE), lambda j: (0, j))
    return _call(
        body, name="ffn_act_bwd", grid=(D_FF_PAD // FF_TILE,),
        in_specs=[pl.BlockSpec((s, D_MODEL), lambda j: (0, 0)), _ff_rows_spec(), col, col, col, _ff_weight_spec(8)],
        out_specs=[col, col, _ff_weight_spec(CONV_ROWS)],
        out_shape=[jax.ShapeDtypeStruct((s, D_FF_PAD), bf16), jax.ShapeDtypeStruct((s, D_FF_PAD), bf16),
                   jax.ShapeDtypeStruct((N_CHIPS, CONV_ROWS, FF_PAD), f32)],
    )(dh2_bf, wd_g, ac, a, up, convw_g)


def _matmul_tn(a, b, n_shards, name, tm=512, tn=1024, cargo=None):
    k, m = a.shape
    n = b.shape[1]
    tm, tn = min(tm, m), min(tn, n // n_shards)
    per = n // n_shards // tn

    def body(a_ref, b_ref, o_ref, at_s):
        @pl.when(pl.program_id(1) == 0)
        def _():
            at_s[...] = a_ref[...].T

        o_ref[...] = _dot(at_s[...], b_ref[...]).astype(bf16)

    res = _call(
        body, name=name, grid=(m // tm, n // tn),
        in_specs=[pl.BlockSpec((k, tm), lambda i, j: (0, i)), pl.BlockSpec((k, tn), lambda i, j: (0, j))],
        out_specs=[pl.BlockSpec((None, tm, tn), lambda i, j: (j // per, i, j % per))],
        out_shape=[jax.ShapeDtypeStruct((n_shards, m, n // n_shards), bf16)], scratch=[pltpu.VMEM((tm, k), bf16)],
        cargo=cargo)(a, b)
    return res[0] if cargo is None else (res[0][0], res[1])


def _dw_in(x_bf, dqkv, du, dzg, cargo):
    k, m = x_bf.shape
    tm, half = 512, D_ATTN

    def body(a_ref, dqkv_ref, du_ref, dzg_ref, o_ref, at_s):
        j = pl.program_id(1)

        @pl.when(j == 0)
        def _():
            at_s[...] = a_ref[...].T
            o_ref[:, 0:half] = _dot(at_s[...], dqkv_ref[0]).astype(bf16)
            o_ref[:, half:2 * half] = _dot(at_s[...], dqkv_ref[1]).astype(bf16)

        @pl.when(j == 1)
        def _():
            o_ref[:, 0:half] = _dot(at_s[...], dqkv_ref[2]).astype(bf16)
            o_ref[:, half:2 * half] = _dot(at_s[...], du_ref[...]).astype(bf16)

        @pl.when(j >= 2)
        def _():
            o_ref[...] = _dot(at_s[...], dzg_ref[...]).astype(bf16)

    (d_win,), carried = _cargo_call(
        body, cargo, name="dw_in", grid=(m // tm, N_CHIPS),
        in_specs=[pl.BlockSpec((k, tm), lambda i, j: (0, i)), pl.BlockSpec((3, k, half), lambda i, j: (0, 0, 0)),
                  pl.BlockSpec((k, half), lambda i, j: (0, 0)),
                  pl.BlockSpec((k, D_MODEL), lambda i, j: (0, jnp.maximum(j - 2, 0)))],
        out_specs=[pl.BlockSpec((None, tm, D_MODEL), lambda i, j: (j, i, 0))],
        out_shape=[jax.ShapeDtypeStruct((N_CHIPS, m, D_MODEL), bf16)], scratch=[pltpu.VMEM((tm, k), bf16)],
    )(x_bf, dqkv, du, dzg)
    return d_win, carried


def _ffn_in_bwd_ln1(da_bf, dup_bf, wg_g, wu_g, dh2, xhat1, rstd1, ln1_g, cargo):
    s = da_bf.shape[0]
    tm = 256

    def body(da_ref, dup_ref, wg_ref, wu_ref, dh2_ref, xh_ref, rstd_ref, g_ref, dh_ref, dhb_ref, dg_ref, db_ref):
        i = pl.program_id(0)
        dx1 = ALPHA * dh2_ref[...]
        for sh in range(N_CHIPS):
            sl = slice(sh * FF_PAD, (sh + 1) * FF_PAD)
            dx1 = dx1 + _dot(da_ref[:, sl], wg_ref[sh]) + _dot(dup_ref[:, sl], wu_ref[sh])
        xhat = xh_ref[...]

        @pl.when(i == 0)
        def _():
            dg_ref[...] = jnp.zeros_like(dg_ref)
            db_ref[...] = jnp.zeros_like(db_ref)

        dg_ref[...] += jnp.sum(dx1 * xhat, axis=0, keepdims=True)
        db_ref[...] += jnp.sum(dx1, axis=0, keepdims=True)
        dh = _layer_norm_bwd(dx1, xhat, rstd_ref[:, 0:1], g_ref[...])
        dh_ref[...] = dh
        dhb_ref[...] = dh.astype(bf16)

    row = pl.BlockSpec((tm, D_MODEL), lambda i: (i, 0))
    wide = pl.BlockSpec((tm, D_FF_PAD), lambda i: (i, 0))
    wfull = pl.BlockSpec((N_CHIPS, FF_PAD, D_MODEL), lambda i: (0, 0, 0))
    vec = pl.BlockSpec((1, D_MODEL), lambda i: (0, 0))
    return _call(
        body, name="ffn_in_bwd_ln1", grid=(s // tm,),
        in_specs=[wide, wide, wfull, wfull, row, row, pl.BlockSpec((tm, LANES), lambda i: (i, 0)), vec],
        out_specs=[row, row, vec, vec],
        out_shape=[jax.ShapeDtypeStruct((s, D_MODEL), f32), jax.ShapeDtypeStruct((s, D_MODEL), bf16),
                   jax.ShapeDtypeStruct((1, D_MODEL), f32), jax.ShapeDtypeStruct((1, D_MODEL), f32)],
        cargo=cargo)(da_bf, dup_bf, wg_g, wu_g, dh2, xhat1, rstd1, ln1_g)


def _merge_bwd(dh1_bf, wout, o_bf, pm_bf, z, wba, wbp):
    s = dh1_bf.shape[0]
    tm = 256

    def body(dh_ref, wout_ref, o_ref, pm_ref, ga_ref, gp_ref, wba_ref, wbp_ref,
             dzg_ref, dya_ref, dyp_ref, do_ref, dpm_ref, dbg_ref):
        i = pl.program_id(0)
        dm = _dot_nt(dh_ref[...], wout_ref[...])
        ya = _dot(o_ref[...], wba_ref[...])
        yp = _dot(pm_ref[...], wbp_ref[...])
        ga, gp = ga_ref[...], gp_ref[...]
        dza = dm * ya * ga * (1.0 - ga)
        dzp = dm * yp * gp * (1.0 - gp)

        @pl.when(i == 0)
        def _():
            dbg_ref[...] = jnp.zeros_like(dbg_ref)

        dbg_ref[:, 0:D_MODEL] += jnp.sum(dza, axis=0, keepdims=True)
        dbg_ref[:, D_MODEL:2 * D_MODEL] += jnp.sum(dzp, axis=0, keepdims=True)
        dzg_ref[:, 0:D_MODEL] = dza.astype(bf16)
        dzg_ref[:, D_MODEL:2 * D_MODEL] = dzp.astype(bf16)
        dya = (dm * ga).astype(bf16)
        dyp = (dm * gp).astype(bf16)
        dya_ref[...] = dya
        dyp_ref[...] = dyp
        do_ref[...] = _dot_nt(dya, wba_ref[...]).astype(bf16)
        dpm_ref[...] = _dot_nt(dyp, wbp_ref[...])

    row = pl.BlockSpec((tm, D_MODEL), lambda i: (i, 0))
    half = pl.BlockSpec((tm, D_ATTN), lambda i: (i, 0))
    full = lambda r, c: pl.BlockSpec((r, c), lambda i: (0, 0))
    return _call(
        body, name="merge_bwd", grid=(s // tm,),
        in_specs=[row, full(D_MODEL, D_MODEL), half, half,
                  pl.BlockSpec((tm, D_MODEL), lambda i: (i, 2)), pl.BlockSpec((tm, D_MODEL), lambda i: (i, 3)),
                  full(D_ATTN, D_MODEL), full(D_POOL, D_MODEL)],
        out_specs=[pl.BlockSpec((tm, 2 * D_MODEL), lambda i: (i, 0)), row, row, half, half,
                   pl.BlockSpec((1, 2 * D_MODEL), lambda i: (0, 0))],
        out_shape=[jax.ShapeDtypeStruct((s, 2 * D_MODEL), bf16), jax.ShapeDtypeStruct((s, D_MODEL), bf16),
                   jax.ShapeDtypeStruct((s, D_MODEL), bf16), jax.ShapeDtypeStruct((s, D_ATTN), bf16),
                   jax.ShapeDtypeStruct((s, D_POOL), f32), jax.ShapeDtypeStruct((1, 2 * D_MODEL), f32)],
    )(dh1_bf, wout, o_bf, pm_bf, z, z, wba, wbp)


def _pool_bwd(dpm, mixed, pooled_bf, w_pool, pool_scale):
    s = dpm.shape[0]

    def body(dpm_ref, mixed_ref, pooled_ref, w_ref, sc_ref, du_ref, dw_ref, dsc_ref, pad):
        g = pl.program_id(0)
        dpm_v = dpm_ref[...]
        dsc_ref[...] = jnp.sum(dpm_v * mixed_ref[...], axis=0, keepdims=True)
        dmixed = (dpm_v * sc_ref[...]).astype(bf16)
        dw_ref[...] = _dot_tn(pooled_ref[...], dmixed)
        dpooled = _dot_nt(dmixed, w_ref[...].astype(bf16))
        t = lax.broadcasted_iota(jnp.int32, (s, POOL_GROUP), 0)
        count = jnp.minimum(t + 1, jnp.left_shift(2, g)).astype(f32)
        r = dpooled / count
        pad[0:s, :] = r
        pad[s:s + MAX_WINDOW, :] = jnp.zeros((MAX_WINDOW, POOL_GROUP), f32)
        acc = r
        snaps = []
        for d in range(1, MAX_WINDOW):
            acc = acc + pad[d:d + s, :]
            if d + 1 in (2, 4, 8, 16):
                snaps.append(acc)
        du_ref[...] = (_window_select(snaps, g) - dpooled).astype(bf16)

    blk = pl.BlockSpec((s, POOL_GROUP), lambda g: (0, g))
    return _call(
        body, name="pool_bwd", grid=(4,),
        in_specs=[blk, blk, blk, pl.BlockSpec((None, POOL_GROUP, POOL_GROUP), lambda g: (g, 0, 0)),
                  pl.BlockSpec((1, POOL_GROUP), lambda g: (0, g))],
        out_specs=[blk, pl.BlockSpec((None, POOL_GROUP, POOL_GROUP), lambda g: (g, 0, 0)),
                   pl.BlockSpec((1, POOL_GROUP), lambda g: (0, g))],
        out_shape=[jax.ShapeDtypeStruct((s, D_POOL), bf16), jax.ShapeDtypeStruct((4, POOL_GROUP, POOL_GROUP), f32),
                   jax.ShapeDtypeStruct((1, D_POOL), f32)],
        scratch=[pltpu.VMEM((s + MAX_WINDOW, POOL_GROUP), f32)])(dpm, mixed, pooled_bf, w_pool, pool_scale)


def _attn_bwd(z, o_bf, lse, do_bf, cos_t, sin_t, nb, cargo):
    s = z.shape[0]

    def body(q_ref, k_ref, v_ref, o_ref, lse_ref, do_ref, cq_ref, sq_ref, cf_ref, sf_ref,
             dqkv_ref, kt_s, dk_s, dv_s):
        i = pl.program_id(1)

        @pl.when(i == 0)
        def _():
            dk_s[...] = jnp.zeros_like(dk_s)
            dv_s[...] = jnp.zeros_like(dv_s)
            for j in range(nb):
                kt_s[j] = k_ref[j * MOBA_BLOCK:(j + 1) * MOBA_BLOCK, :].T.astype(bf16)

        k_heads = _head_lanes((MOBA_BLOCK, LANES))
        t_heads = _head_rows((LANES, MOBA_BLOCK))

        def query_block(c):
            q = q_ref[...]
            sels = _moba_select(q.astype(bf16), k_ref[...], c, nb)
            qs = (q * SCALE).astype(bf16)
            qraw = [jnp.where(hm, q, 0.0).astype(bf16) for hm in k_heads]
            do = do_ref[...].astype(f32)
            do_bf = do_ref[...]
            dob = [jnp.where(hm, do, 0.0).astype(bf16) for hm in k_heads]
            pick = _head_lanes((STAT_ROWS, LANES))
            r8 = lax.broadcasted_iota(jnp.int32, (STAT_ROWS, LANES), 0)
            head_pick = jnp.where(((r8 == 0) & pick[0]) | ((r8 == 1) & pick[1]), 1.0, 0.0)
            delta8 = lax.dot_general(head_pick, do * o_ref[...].astype(f32), (((1,), (1,)), ((), ())),
                                     precision=lax.Precision.HIGHEST, preferred_element_type=f32)
            lse8 = lse_ref[...]
            delta, lse_h = [delta8[0:1, :], delta8[1:2, :]], [lse8[0:1, :], lse8[1:2, :]]
            key = lax.broadcasted_iota(jnp.int32, (MOBA_BLOCK, MOBA_BLOCK), 0)
            qry = lax.broadcasted_iota(jnp.int32, (MOBA_BLOCK, MOBA_BLOCK), 1)

            def block(j, allow):
                rows = slice(j * MOBA_BLOCK, (j + 1) * MOBA_BLOCK)
                kj, vj, ktj = k_ref[rows, :], v_ref[rows, :], kt_s[j]
                dqt, dk, dv = None, None, None
                for h in range(2):
                    sc = _dot_nt(jnp.where(k_heads[h], kj, 0.0).astype(bf16), qs)
                    p = jnp.where(allow[h], jnp.exp(sc - lse_h[h]), 0.0)
                    dp = _dot_nt(jnp.where(k_heads[h], vj, 0.0).astype(bf16), do_bf)
                    ds = (p * (dp - delta[h]) * SCALE).astype(bf16)
                    tv = _dot(p.astype(bf16), dob[h])
                    tk = _dot(ds, qraw[h])
                    tq = _dot(jnp.where(t_heads[h], ktj, jnp.zeros_like(ktj)), ds)
                    dqt, dk, dv = (tq, tk, tv) if dqt is None else (dqt + tq, dk + tk, dv + tv)
                dk_s[rows, :] += dk
                dv_s[rows, :] += dv
                return dqt

            dqt = block(c, [key <= qry] * 2)
            for j in range(c):
                dqt = dqt + block(j, [sels[h][j:j + 1, :] > 0.0 for h in range(2)])
            dqkv_ref[0, c * MOBA_BLOCK:(c + 1) * MOBA_BLOCK, :] = _rope_bwd(dqt.T, cq_ref[...], sq_ref[...]).astype(bf16)

        for c in range(nb):
            pl.when(i == c)(functools.partial(query_block, c))

        @pl.when(i == nb - 1)
        def _():
            dqkv_ref[1] = _rope_bwd(dk_s[...], cf_ref[...], sf_ref[...]).astype(bf16)
            dqkv_ref[2] = dv_s[...].astype(bf16)

    qblk = pl.BlockSpec((MOBA_BLOCK, LANES), lambda hp, i: (i, hp))
    tq = pl.BlockSpec((MOBA_BLOCK, LANES), lambda hp, i: (i, 0))
    tf = pl.BlockSpec((s, LANES), lambda hp, i: (0, 0))
    return _cargo_call(
        body, cargo, name="attn_bwd", grid=(D_ATTN // LANES, nb),
        in_specs=[qblk, pl.BlockSpec((s, LANES), lambda hp, i: (0, 4 + hp)),
                  pl.BlockSpec((s, LANES), lambda hp, i: (0, 8 + hp)), qblk,
                  pl.BlockSpec((None, STAT_ROWS, MOBA_BLOCK), lambda hp, i: (hp, 0, i)), qblk, tq, tq, tf, tf],
        out_specs=[pl.BlockSpec((3, s, LANES), lambda hp, i: (0, 0, hp))],
        out_shape=[jax.ShapeDtypeStruct((3, s, D_ATTN), bf16)],
        scratch=[pltpu.VMEM((nb, LANES, MOBA_BLOCK), bf16), pltpu.VMEM((s, LANES), f32), pltpu.VMEM((s, LANES), f32)],
    )(z, z, z, o_bf, lse, do_bf, cos_t, sin_t, cos_t, sin_t)


def _inproj_dx(dqkv, du, dzg, win_g, dh1, more_items, place, cargo):
    s = du.shape[0]
    tm, half = 256, D_ATTN

    def compute(ins, outs, tile):
        dqkv_ref, du_ref, dzg_ref, w_ref, dh_ref = ins
        acc = ALPHA * dh_ref[...]
        for n, part in enumerate([dqkv_ref[0], dqkv_ref[1], dqkv_ref[2], du_ref[...]]):
            acc = acc + _dot_nt(part, w_ref[n // 2, :, (n % 2) * half:(n % 2 + 1) * half])
        for sh in range(2, N_CHIPS):
            acc = acc + _dot_nt(dzg_ref[:, (sh - 2) * D_MODEL:(sh - 1) * D_MODEL], w_ref[sh])
        outs[0][...] = acc

    row = lambda tile: pl.BlockSpec((tm, D_MODEL), lambda t, pc: (tile(t), 0))
    ins = lambda tile: [pl.BlockSpec((3, tm, half), lambda t, pc: (0, tile(t), 0)),
                        pl.BlockSpec((tm, half), lambda t, pc: (tile(t), 0)),
                        pl.BlockSpec((tm, 2 * D_MODEL), lambda t, pc: (tile(t), 0)),
                        pl.BlockSpec((N_CHIPS, D_MODEL, D_MODEL), lambda t, pc: (0, 0, 0)), row(tile)]
    main = (s // tm, ins, lambda tile: [row(tile)], [jax.ShapeDtypeStruct((s, D_MODEL), f32)],
            [dqkv, du, dzg, win_g, dh1], compute)
    outs, carried = _batched_call([main] + more_items, "inproj_dx", place, cargo)
    return (outs[0][0], [o[0] for o in outs[1:]]), carried


ANY = pl.BlockSpec(memory_space=pl.ANY)


def _chip_index():
    return 2 * lax.axis_index("x") + lax.axis_index("y")


def _peer(k):
    x, y, c = lax.axis_index("x"), lax.axis_index("y"), lax.axis_index("c")
    return (x ^ (k >> 1), y ^ (k & 1), c)


def _sibling():
    return (lax.axis_index("x"), lax.axis_index("y"), 1 - lax.axis_index("c"))


class _GatherCargo:
    def __init__(self, shards, pass_on):
        self.shards, self.pass_on, n = shards, pass_on, len(shards)
        self.inputs = [a.reshape(2, a.shape[0] // 2, a.shape[1]) for a in shards]
        self.out_shape = [jax.ShapeDtypeStruct((N_CHIPS,) + a.shape, a.dtype) for a in self.inputs]
        sem, one = pltpu.SemaphoreType.DMA((N_CHIPS - 1, n)), pltpu.SemaphoreType.DMA((n,))
        self.sems = [sem, sem, sem, sem, one, one]

    def _copies(self, src, dst, sems, a):
        send_sems, recv_sems, fsend_sems, frecv_sems, local_sems, own_sems = sems
        p, c = _chip_index(), lax.axis_index("c")
        copy = lambda s, d, **kw: functools.partial(pltpu.make_async_remote_copy, src_ref=s, dst_ref=d, **kw)
        own = copy(src[a], dst[a].at[p], send_sem=local_sems.at[a], recv_sem=own_sems.at[a], device_id=_sibling(),
                   device_id_type=MESH)
        out, arrive, onward, landed = [], [], [], []
        for k in range(1, N_CHIPS):
            ici = dict(send_sem=send_sems.at[k - 1, a], recv_sem=recv_sems.at[k - 1, a], device_id=_peer(k),
                       device_id_type=MESH)
            d2d = dict(send_sem=fsend_sems.at[k - 1, a], recv_sem=frecv_sems.at[k - 1, a], device_id=_sibling(),
                       device_id_type=MESH)
            got, theirs = dst[a].at[p ^ k, c], dst[a].at[p ^ k, 1 - c]
            out.append(copy(src[a].at[c], dst[a].at[p, c], **ici))
            arrive.append(copy(src[a].at[c], got, **ici))
            onward.append(copy(got, got, **d2d))
            landed.append(copy(theirs, theirs, **d2d))
        return own, out, arrive, onward, landed

    def stages(self, steps):
        n = len(self.shards)

        def start(src, dst, sems):
            for a in range(n):
                own, out, _, _, _ = self._copies(src, dst, sems, a)
                own().start()
                for cp in out:
                    cp().start()

        def pass_on(a):
            def act(src, dst, sems):
                _, _, arrive, onward, _ = self._copies(src, dst, sems, a)
                for k in range(N_CHIPS - 1):
                    arrive[k]().wait_recv()
                    onward[k]().start()
            return act

        def finish(src, dst, sems):
            for a in range(n):
                own, out, _, onward, landed = self._copies(src, dst, sems, a)
                for cp in landed:
                    cp().wait_recv()
                for cp in out + onward:
                    cp().wait_send()
                own().wait()

        mids = [(min(steps - 1, int(self.pass_on[a] * steps)), pass_on(a)) for a in range(n)]
        return [(0, start)] + mids + [(steps - 1, finish)]

    def results(self, outs):
        return [o.reshape((N_CHIPS,) + a.shape) for o, a in zip(outs, self.shards)]


class _ExchangeCargo:
    def __init__(self, sums, whole):
        self.inputs, self.whole, n = sums, whole, len(sums)
        self.out_shape = [jax.ShapeDtypeStruct((N_CHIPS - 1,) + g.shape[1:], g.dtype) for g in sums]
        sem = pltpu.SemaphoreType.DMA((N_CHIPS - 1, n))
        self.sems = [sem, sem]

    def _copies(self, src, dst, sems):
        send_sems, recv_sems = sems
        p = _chip_index()
        return [pltpu.make_async_remote_copy(
            src_ref=src[a].at[0] if self.whole[a] else src[a].at[p ^ k], dst_ref=dst[a].at[k - 1],
            send_sem=send_sems.at[k - 1, a], recv_sem=recv_sems.at[k - 1, a], device_id=_peer(k), device_id_type=MESH)
            for k in range(1, N_CHIPS) for a in range(len(self.inputs))]

    def stages(self, steps):
        def start(src, dst, sems):
            for cp in self._copies(src, dst, sems):
                cp.start()

        def finish(src, dst, sems):
            copies = self._copies(src, dst, sems)
            for cp in copies:
                cp.wait_recv()
            for cp in copies:
                cp.wait_send()

        return [(0, start), (steps - 1, finish)]

    def results(self, outs):
        return list(outs)


class _SwapCargo:
    def __init__(self, split):
        self.inputs, n = split, len(split)
        self.out_shape = [jax.ShapeDtypeStruct((g.shape[0],) + g.shape[2:], g.dtype) for g in split]
        sem = pltpu.SemaphoreType.DMA((n,))
        self.sems = [sem, sem]

    def _copies(self, src, dst, sems):
        c = lax.axis_index("c")
        return [pltpu.make_async_remote_copy(src_ref=src[a].at[:, 1 - c], dst_ref=dst[a], send_sem=sems[0].at[a],
                                             recv_sem=sems[1].at[a], device_id=_sibling(), device_id_type=MESH)
                for a in range(len(self.inputs))]

    def stages(self, steps):
        def start(src, dst, sems):
            for cp in self._copies(src, dst, sems):
                cp.start()

        def finish(src, dst, sems):
            copies = self._copies(src, dst, sems)
            for cp in copies:
                cp.wait_recv()
            for cp in copies:
                cp.wait_send()

        return [(0, start), (steps - 1, finish)]

    def results(self, outs):
        return list(outs)


class _MultiCargo:
    def __init__(self, cargos):
        self.cargos = cargos
        self.inputs = [a for cg in cargos for a in cg.inputs]
        self.out_shape = [o for cg in cargos for o in cg.out_shape]
        self.sems = [s for cg in cargos for s in cg.sems]

    def _parts(self, refs, count):
        out, off = [], 0
        for cg in self.cargos:
            out.append(refs[off:off + count(cg)])
            off += count(cg)
        return out

    def stages(self, steps):
        merged = []
        for n, cg in enumerate(self.cargos):
            for at, act in cg.stages(steps):
                def part(src, dst, sems, n=n, act=act):
                    act(self._parts(src, lambda g: len(g.inputs))[n], self._parts(dst, lambda g: len(g.out_shape))[n],
                        self._parts(sems, lambda g: len(g.sems))[n])
                merged.append((at, part))
        starts = [m for m in merged if m[0] == 0]
        return starts + sorted([m for m in merged if m[0] != 0], key=lambda m: m[0])

    def results(self, outs):
        return [cg.results(part) for cg, part in zip(self.cargos, self._parts(list(outs), lambda g: len(g.out_shape)))]


def _cargo_call(body, cargo, *, name, grid, in_specs, out_specs, out_shape, scratch=(), prefetch=None):
    n_in, n_out, n_scr = len(in_specs), len(out_specs), len(scratch)
    c_in, c_out = len(cargo.inputs), len(cargo.out_shape)
    steps = 1
    for g in grid:
        steps *= g
    stages = cargo.stages(steps)

    def wrapped(*refs):
        head, refs = (refs[:1], refs[1:]) if prefetch is not None else ((), refs)
        ins, refs = refs[:n_in], refs[n_in:]
        cin, refs = refs[:c_in], refs[c_in:]
        outs, refs = refs[:n_out], refs[n_out:]
        cout, refs = refs[:c_out], refs[c_out:]
        scr, sems = refs[:n_scr], refs[n_scr:]
        if not grid:
            for _, act in stages:
                act(cin, cout, sems)
            return
        step = 0
        for d in range(len(grid)):
            step = step * grid[d] + pl.program_id(d)
        for at, act in stages:
            if at == 0:
                pl.when(step == 0)(functools.partial(act, cin, cout, sems))
        body(*head, *ins, *outs, *scr)
        for at, act in stages:
            if at > 0:
                pl.when(step == at)(functools.partial(act, cin, cout, sems))

    params = dict(vmem_limit_bytes=VMEM_LIMIT)
    if grid:
        params["dimension_semantics"] = ("arbitrary",) * len(grid)
    specs = dict(grid=grid, in_specs=list(in_specs) + [ANY] * c_in, out_specs=list(out_specs) + [ANY] * c_out,
                 scratch_shapes=list(scratch) + cargo.sems)
    if prefetch is not None:
        specs = dict(grid_spec=pltpu.PrefetchScalarGridSpec(num_scalar_prefetch=1, **specs))
    res = pl.pallas_call(wrapped, name=name, out_shape=list(out_shape) + cargo.out_shape,
                         compiler_params=pltpu.CompilerParams(**params), **specs)
    head = () if prefetch is None else (prefetch,)
    return lambda *args: (lambda r: (r[:n_out], cargo.results(r[n_out:])))(res(*head, *args, *cargo.inputs))


def _split_halves(g):
    return g.reshape(g.shape[0], 2, g.shape[1] // 2, g.shape[2])


def _ew_tile(rows):
    return rows if rows <= 384 else 256


def _batched_call(items, name, place=None, cargo=None):
    offs, total = [], 0
    for it in items:
        offs.append(total)
        total += it[0]
    in_specs, out_specs, out_shape, args, spans = [], [], [], [], []
    for (n, ins, outs, shapes, operands, _), off in zip(items, offs):
        tile = lambda t, off=off, n=n: jnp.clip(t - off, 0, n - 1)
        i_s, o_s = ins(tile), outs(tile)
        spans.append((len(i_s), len(o_s)))
        in_specs += i_s
        out_specs += o_s
        out_shape += shapes
        args += operands
    n_in = len(in_specs)

    def body(*refs):
        if place is not None:
            refs = refs[1:]
        t = pl.program_id(0)
        i0, o0 = 0, n_in
        for (n, _, _, _, _, compute), off, (ni, no) in zip(items, offs, spans):
            pl.when((t >= off) & (t < off + n))(functools.partial(compute, refs[i0:i0 + ni], refs[o0:o0 + no], t - off))
            i0, o0 = i0 + ni, o0 + no

    params = pltpu.CompilerParams(dimension_semantics=("arbitrary",), vmem_limit_bytes=VMEM_LIMIT)
    carried = None
    if cargo is not None:
        res, carried = _cargo_call(body, cargo, name=name, grid=(total,), in_specs=in_specs, out_specs=out_specs,
                                   out_shape=out_shape, prefetch=place)(*args)
    elif place is None:
        res = pl.pallas_call(body, name=name, grid=(total,), in_specs=in_specs, out_specs=out_specs,
                             out_shape=out_shape, compiler_params=params)(*args)
    else:
        res = pl.pallas_call(
            body, name=name, out_shape=out_shape, compiler_params=params,
            grid_spec=pltpu.PrefetchScalarGridSpec(num_scalar_prefetch=1, grid=(total,), in_specs=in_specs,
                                                   out_specs=out_specs))(place, *args)
    out, o0 = [], 0
    for _, no in spans:
        out.append(res[o0:o0 + no])
        o0 += no
    return out if cargo is None else (out, carried)


def _add_pairs(pairs, place, name):
    def item(mine, theirs):
        lead, _, rows, cols = mine.shape
        tm = _ew_tile(rows)

        def compute(ins, outs, tile):
            outs[0][...] = (ins[0][...].astype(f32) + ins[1][...].astype(f32)).astype(outs[0].dtype)

        blk = lambda tile: pl.BlockSpec((lead, tm, cols), lambda t, pc: (0, tile(t), 0))
        mine_half = lambda tile: pl.BlockSpec((lead, None, tm, cols), lambda t, pc: (0, pc[1], tile(t), 0))
        return (rows // tm, lambda tile: [mine_half(tile), blk(tile)], lambda tile: [blk(tile)],
                [jax.ShapeDtypeStruct(theirs.shape, theirs.dtype)], [mine, theirs], compute)

    return [out[0] for out in _batched_call([item(a, b) for a, b in pairs], name, place)]


def _cargo_alone(cargo, name):
    return _cargo_call(None, cargo, name=name, grid=(), in_specs=[], out_specs=[], out_shape=[])()[1]


def _bf16_casts(x, mats, padded, cargo, name):
    def item(a, to_rows, tm):
        rows, cols = a.shape
        full = rows // tm

        def compute(ins, outs, tile):
            outs[0][...] = jnp.where(tile < full, ins[0][...], 0.0).astype(bf16)

        return (to_rows // tm, lambda tile: [pl.BlockSpec((tm, cols), lambda t: (jnp.minimum(tile(t), full - 1), 0))],
                lambda tile: [pl.BlockSpec((tm, cols), lambda t: (tile(t), 0))],
                [jax.ShapeDtypeStruct((to_rows, cols), bf16)], [a], compute)

    items = [item(x, x.shape[0], 128)] + [item(w, r, 64 if r > w.shape[0] else 128) for w, r in zip(mats, padded)]
    outs, carried = _batched_call(items, name, cargo=cargo)
    return [o[0] for o in outs], carried


def _sum_slots_items(jobs, max_rows=None):
    def item(own, recv, whole):
        _, rows, cols = own.shape
        tm = _ew_tile(rows)
        while max_rows is not None and tm > max_rows and tm % 16 == 0:
            tm //= 2

        def compute(ins, outs, tile):
            r0, r1, r2 = [ins[1][k].astype(f32) for k in range(N_CHIPS - 1)]
            outs[0][...] = (ins[0][...].astype(f32) + r0) + (r1 + r2)

        ins = lambda tile: [pl.BlockSpec((None, tm, cols), lambda t, pc: (0 if whole else pc[0], tile(t), 0)),
                            pl.BlockSpec((N_CHIPS - 1, tm, cols), lambda t, pc: (0, tile(t), 0))]
        outs = lambda tile: [pl.BlockSpec((None, tm, cols), lambda t, pc: (pc[1], tile(t), 0))]
        return rows // tm, ins, outs, [jax.ShapeDtypeStruct((2, rows, cols), f32)], [own, recv], compute

    return [item(*job) for job in jobs]


def _sibling_fill(bufs):
    n = len(bufs)

    def body(*refs):
        src, dst = refs[:n], refs[n:2 * n]
        send_sems, recv_sems = refs[2 * n:]
        c = lax.axis_index("c")
        remote = [pltpu.make_async_remote_copy(src_ref=src[a].at[c], dst_ref=dst[a].at[c], send_sem=send_sems.at[a],
                                               recv_sem=recv_sems.at[a], device_id=_sibling(), device_id_type=MESH)
                  for a in range(n)]
        for cp in remote:
            cp.start()
        for a in range(n):
            pltpu.make_async_remote_copy(src_ref=src[a].at[c], dst_ref=dst[a].at[1 - c], send_sem=send_sems.at[a],
                                         recv_sem=recv_sems.at[a], device_id=_sibling(), device_id_type=MESH).wait_recv()
        for cp in remote:
            cp.wait_send()

    sem = pltpu.SemaphoreType.DMA((n,))
    out = pl.pallas_call(
        body, name="sibling_fill", in_specs=[ANY] * n, out_specs=[ANY] * n,
        out_shape=[jax.ShapeDtypeStruct(b.shape, b.dtype) for b in bufs],
        input_output_aliases={a: a for a in range(n)}, scratch_shapes=[sem, sem])(*bufs)
    return [o.reshape(2 * b.shape[1], b.shape[2]) for o, b in zip(out, bufs)]


class _GradReducer:
    def __init__(self, place):
        self.place, self.split, self.sums, self.slots, self.bufs = place, {}, {}, {}, {}

    def swap(self, named):
        self.split.update({nm: _split_halves(g) for nm, g in named})
        return _SwapCargo([self.split[nm] for nm, _ in named])

    def summed(self, names, got, tag):
        sums = _add_pairs([(self.split[nm], g) for nm, g in zip(names, got)], self.place, "presum_" + tag)
        self.sums.update(zip(names, sums))

    def exchange(self, names):
        return _ExchangeCargo([self.sums[nm] for nm in names], [nm == "small" for nm in names])

    def arrived(self, names, slots):
        self.slots.update(zip(names, slots))

    def tree_sum_items(self, names, max_rows=None):
        return _sum_slots_items([(self.sums[nm], self.slots[nm], nm == "small") for nm in names], max_rows)

    def tree_summed(self, names, bufs):
        self.bufs.update(zip(names, bufs))

    def reduce(self, names):
        rest = [nm for nm in names if nm not in self.bufs]
        outs = _batched_call(self.tree_sum_items(rest), "sum_slots", self.place)
        self.tree_summed(rest, [out[0] for out in outs])
        return _sibling_fill([self.bufs[nm] for nm in names])


def _adamw_update(w, m, v, g):
    m = ADAM_B1 * m + (1.0 - ADAM_B1) * g
    v = ADAM_B2 * v + (1.0 - ADAM_B2) * jnp.square(g)
    m_hat = m / (1.0 - ADAM_B1 ** ADAM_STEP)
    v_hat = v / (1.0 - ADAM_B2 ** ADAM_STEP)
    return -ADAM_LR * (m_hat / (jnp.sqrt(v_hat) + ADAM_EPS) + ADAM_WD * w), m, v


def _adamw(jobs, name):
    def item(w, m, v, grad, tm):
        rows, cols = w.shape
        gcols = grad.shape[1]

        def compute(ins, outs, tile):
            g = ins[3][0:tm, 0:cols]
            outs[0][...] = g
            outs[1][...], outs[2][...], outs[3][...] = _adamw_update(ins[0][...], ins[1][...], ins[2][...], g)

        blk = lambda tile: pl.BlockSpec((tm, cols), lambda t: (tile(t), 0))
        if tm == rows:
            gspec = lambda tile: pl.BlockSpec(grad.shape, lambda t: (0, 0))
        else:
            gspec = lambda tile: pl.BlockSpec((tm, gcols), lambda t: (tile(t), 0))
        return (rows // tm, lambda tile: [blk(tile)] * 3 + [gspec(tile)], lambda tile: [blk(tile)] * 4,
                [jax.ShapeDtypeStruct((rows, cols), f32)] * 4, [w, m, v, grad], compute)

    return _batched_call([item(*job) for job in jobs], name)


def _rope_tables(s):
    half = HEAD_DIM // 2
    inv_freq = 1.0 / (10000.0 ** (jnp.arange(half, dtype=f32) / half))
    ang = jnp.arange(s, dtype=f32)[:, None] * inv_freq[None, :]
    cos, sin = jnp.cos(ang), jnp.sin(ang)
    return jnp.tile(cos, (1, LANES // half)), jnp.tile(jnp.concatenate([-sin, sin], axis=1), (1, LANES // HEAD_DIM))


def _local_step(x, x_bf, target, win_g, gather, comm, b_gate, w_pool, pool_scale, ln1_g, ln1_b, convb_g, ln2_g, ln2_b):
    s = x.shape[0]
    nb = s // MOBA_BLOCK
    cos_t, sin_t = _rope_tables(s)

    z, (wba_g, wbp_g, wout_g) = _inproj(x_bf, win_g, cos_t, sin_t, b_gate, gather["inproj"])
    wba = wba_g.transpose(1, 0, 2).reshape(D_ATTN, D_MODEL)
    wbp = wbp_g.transpose(1, 0, 2).reshape(D_POOL, D_MODEL)
    wout = wout_g.reshape(D_MODEL, D_MODEL)
    (o_bf, lse, pooled_bf, mixed, pm_bf), (wg_g, wu_g, convw_g) = _attn_fwd(z, nb, w_pool, pool_scale,
                                                                           gather["attn_fwd"])
    m_bf = _branch_merge(o_bf, pm_bf, z, wba, wbp)
    x1_bf, xhat1, rstd1 = _out_ln1(m_bf, wout, x, ln1_g, ln1_b)
    (a, up, ac, hh_bf), (wd_g,) = _ffn_up(x1_bf, wg_g, wu_g, convw_g, convb_g, gather["ffn_up"])
    wd = wd_g.reshape(D_FF_PAD, D_MODEL)
    loss, dh2, dh2_bf, d_ln2_g, d_ln2_b = _ffn_down_ln2_loss(hh_bf, wd, xhat1, ln1_g, ln1_b, ln2_g, ln2_b, target)

    da_bf, dup_bf, dconv = _ffn_act_bwd(dh2_bf, wd_g, ac, a, up, convw_g)
    ffn_shape = (N_CHIPS, FF_PAD, D_MODEL)
    d_wd = _matmul_tn(hh_bf, dh2_bf, 1, "dw_ffn_down").reshape(ffn_shape)
    d_wg, got = _matmul_tn(da_bf, x1_bf, 1, "dw_ffn_gate", cargo=comm.swap([("w_ffn_down", d_wd)]))
    comm.summed(["w_ffn_down"], got, "ffn_down")
    d_wg = d_wg.reshape(ffn_shape)
    d_wu = _matmul_tn(dup_bf, x1_bf, 1, "dw_ffn_up").reshape(ffn_shape)
    gate_up = ["w_ffn_gate", "w_ffn_up", "conv_w"]
    (dh1, dh1_bf, d_ln1_g, d_ln1_b), (slots, got) = _ffn_in_bwd_ln1(
        da_bf, dup_bf, wg_g, wu_g, dh2, xhat1, rstd1, ln1_g,
        _MultiCargo([comm.exchange(["w_ffn_down"]), comm.swap(list(zip(gate_up, [d_wg, d_wu, dconv])))]))
    comm.arrived(["w_ffn_down"], slots)
    comm.summed(gate_up, got, "ffn_gate_up")
    d_wout = _matmul_tn(m_bf, dh1_bf, 1, "dw_out").reshape(N_CHIPS, D_MODEL // N_CHIPS, D_MODEL)
    dzg_bf, dya_bf, dyp_bf, do_bf, dpm, d_bgate = _merge_bwd(dh1_bf, wout, o_bf, pm_bf, z, wba, wbp)
    d_wba = _matmul_tn(o_bf, dya_bf, N_CHIPS, "dw_branch_attn")
    d_wbp = _matmul_tn(pm_bf, dyp_bf, N_CHIPS, "dw_branch_pool")
    du_bf, d_wpool, d_pscale = _pool_bwd(dpm, mixed, pooled_bf, w_pool, pool_scale)
    d_convb = dconv[:, 3, :FF_SHARD].reshape(1, N_CHIPS * FF_SHARD)
    small = _pack_small([d_bgate, d_wpool.reshape(-1, POOL_GROUP), d_pscale, d_ln1_g, d_ln1_b, d_ln2_g, d_ln2_b,
                         d_convb], loss)[None]
    branch = ["w_out", "w_branch_attn", "w_branch_pool", "small"]
    (dqkv_bf,), (slots, got) = _attn_bwd(
        z, o_bf, lse, do_bf, cos_t, sin_t, nb,
        _MultiCargo([comm.exchange(gate_up), comm.swap(list(zip(branch, [d_wout, d_wba, d_wbp, small])))]))
    comm.arrived(gate_up, slots)
    comm.summed(branch, got, "branch")
    d_win, slots = _dw_in(x_bf, dqkv_bf, du_bf, dzg_bf, comm.exchange(branch))
    comm.arrived(branch, slots)
    comm.summed(["w_in"], _cargo_alone(comm.swap([("w_in", d_win)]), "swap_w_in"), "w_in")
    others = ["w_ffn_down"] + gate_up + branch
    (grad_x, bufs), slots = _inproj_dx(dqkv_bf, du_bf, dzg_bf, win_g, dh1, comm.tree_sum_items(others, 192),
                                       comm.place, comm.exchange(["w_in"]))
    comm.tree_summed(others, bufs)
    comm.arrived(["w_in"], slots)
    return grad_x


SMALL_WEIGHTS = [("b_gate", 2 * D_MODEL), ("w_pool", 4 * POOL_GROUP * POOL_GROUP), ("pool_scale", D_POOL),
                 ("ln1_g", D_MODEL), ("ln1_b", D_MODEL), ("ln2_g", D_MODEL), ("ln2_b", D_MODEL),
                 ("conv_b", N_CHIPS * FF_SHARD)]
LOSS_ROW = 588
SMALL_ROWS = 592


def _small_pieces(ref, first_row, n):
    if ref.shape[0] > 1:
        return [(slice(first_row, first_row + n // LANES), (slice(None), slice(None)))]
    return [(slice(first_row + i, first_row + i + 1), (slice(None), slice(i * LANES, (i + 1) * LANES)))
            for i in range(n // LANES)]


def _small_first_rows():
    rows, r = [], 0
    for _, n in SMALL_WEIGHTS:
        rows.append(r)
        r += n // LANES
    return rows


def _pack_small(parts, loss):
    def body(*refs):
        out = refs[-1]
        out[...] = jnp.zeros_like(out)
        for src, r0, (_, n) in zip(refs, _small_first_rows(), SMALL_WEIGHTS):
            for rows, where in _small_pieces(src, r0, n):
                out[rows, :] = src[where]
        out[LOSS_ROW:LOSS_ROW + 1, :] = refs[len(parts)][0:1, :]

    return pl.pallas_call(body, name="pack_small", out_shape=jax.ShapeDtypeStruct((SMALL_ROWS, LANES), f32))(
        *parts, loss)


def _adamw_small(g_packed, triples):
    n_in = 3 * len(triples)

    def body(g_ref, *refs):
        for k, (r0, (_, n)) in enumerate(zip(_small_first_rows(), SMALL_WEIGHTS)):
            w_ref, m_ref, v_ref = refs[3 * k:3 * k + 3]
            outs = refs[n_in + 4 * k:n_in + 4 * k + 4]
            for rows, where in _small_pieces(w_ref, r0, n):
                g = g_ref[rows, :]
                for o_ref, val in zip(outs, (g,) + _adamw_update(w_ref[where], m_ref[where], v_ref[where], g)):
                    o_ref[where] = val

    res = pl.pallas_call(
        body, name="adamw_small",
        out_shape=[jax.ShapeDtypeStruct(t[0].shape, f32) for t in triples for _ in range(4)],
    )(g_packed, *[a for t in triples for a in t])
    return [res[4 * k:4 * k + 4] for k in range(len(triples))]


def _pad_conv_b(cb):
    return jnp.pad(cb.reshape(N_CHIPS, FF_SHARD), ((0, 0), (0, FF_PAD - FF_SHARD)))


def kernel(x, w_in, b_gate, w_branch_attn, w_pool, pool_scale, w_branch_pool, w_out, ln1_g, ln1_b, w_ffn_gate, w_ffn_up, conv_w, conv_b, w_ffn_down, ln2_g, ln2_b, loss_target, m_w_in, m_b_gate, m_w_branch_attn, m_w_pool, m_pool_scale, m_w_branch_pool, m_w_out, m_ln1_g, m_ln1_b, m_w_ffn_gate, m_w_ffn_up, m_conv_w, m_conv_b, m_w_ffn_down, m_ln2_g, m_ln2_b, v_w_in, v_b_gate, v_w_branch_attn, v_w_pool, v_pool_scale, v_w_branch_pool, v_w_out, v_ln1_g, v_ln1_b, v_w_ffn_gate, v_w_ffn_up, v_conv_w, v_conv_b, v_w_ffn_down, v_ln2_g, v_ln2_b):
    mats = [w_branch_attn[0], w_branch_pool[0], w_out[0], w_ffn_gate[0].T, w_ffn_up[0].T, w_ffn_down[0]]
    padded = [w.shape[0] for w in mats[:3]] + [FF_PAD] * 3
    (x_bf, *casted), (win_g,) = _bf16_casts(x[0], mats, padded, _GatherCargo([w_in[0].astype(bf16)], [0.9]),
                                            "gather_w_in")
    shards = [None] + casted + [jnp.pad(conv_w[0], ((0, CONV_ROWS - 3), (0, FF_PAD - FF_SHARD)))]
    gather = {"inproj": _GatherCargo(shards[1:4], [0.3, 0.5, 0.75]),
              "attn_fwd": _GatherCargo(shards[4:6] + shards[7:], [0.97, 0.97, 0.97]),
              "ffn_up": _GatherCargo(shards[6:7], [0.8])}

    place = jnp.stack([2 * lax.axis_index("x") + lax.axis_index("y"), lax.axis_index("c")]).astype(jnp.int32)
    comm = _GradReducer(place)
    convb_g = _pad_conv_b(conv_b).reshape(N_CHIPS, 1, FF_PAD)
    grad_x = _local_step(
        x[0], x_bf, loss_target[0], win_g, gather, comm, b_gate, w_pool[0], pool_scale, ln1_g, ln1_b, convb_g, ln2_g, ln2_b)
    names = ["w_in", "w_branch_attn", "w_branch_pool", "w_out", "w_ffn_gate", "w_ffn_up", "w_ffn_down", "conv_w", "small"]
    grads = comm.reduce(names)

    weights = [w_in[0], w_branch_attn[0], w_branch_pool[0], w_out[0], w_ffn_gate[0].T, w_ffn_up[0].T, w_ffn_down[0],
               conv_w[0]]
    m_in = [m_w_in[0], m_w_branch_attn[0], m_w_branch_pool[0], m_w_out[0], m_w_ffn_gate[0].T, m_w_ffn_up[0].T,
            m_w_ffn_down[0], m_conv_w[0]]
    v_in = [v_w_in[0], v_w_branch_attn[0], v_w_branch_pool[0], v_w_out[0], v_w_ffn_gate[0].T, v_w_ffn_up[0].T,
            v_w_ffn_down[0], v_conv_w[0]]
    tiles = [256, 256, 256, 128, 176, 176, 176, 3]
    jobs = list(zip(weights, m_in, v_in, grads, tiles))
    ffn = slice(4, 7)
    updated = _adamw(jobs[:4] + jobs[7:], "adamw_rest")
    updated = updated[:4] + _adamw(jobs[ffn], "adamw_ffn") + updated[4:]
    res = {nm: [(r.T if nm in ("w_ffn_gate", "w_ffn_up") else r)[None] for r in outs4]
           for nm, outs4 in zip(names[:-1], updated)}

    small_w = [b_gate, w_pool, pool_scale, ln1_g, ln1_b, ln2_g, ln2_b, conv_b]
    small_m = [m_b_gate, m_w_pool, m_pool_scale, m_ln1_g, m_ln1_b, m_ln2_g, m_ln2_b, m_conv_b]
    small_v = [v_b_gate, v_w_pool, v_pool_scale, v_ln1_g, v_ln1_b, v_ln2_g, v_ln2_b, v_conv_b]
    as_rows = lambda a: a.reshape(-1, POOL_GROUP) if a.ndim == 4 else a
    small_out = _adamw_small(grads[-1], [tuple(as_rows(a) for a in t) for t in zip(small_w, small_m, small_v)])
    for (nm, _), w, outs4 in zip(SMALL_WEIGHTS, small_w, small_out):
        res[nm] = [o.reshape(w.shape) for o in outs4]
    loss = grads[-1][LOSS_ROW, 0]

    order = ["w_in", "b_gate", "w_branch_attn", "w_pool", "pool_scale", "w_branch_pool", "w_out", "ln1_g", "ln1_b",
             "w_ffn_gate", "w_ffn_up", "conv_w", "conv_b", "w_ffn_down", "ln2_g", "ln2_b"]
    outs = [loss, grad_x[None]]
    for kind in range(4):
        outs += [res[nm][kind] for nm in order]
    return tuple(outs)
```

```python
import functools

import jax
import jax.numpy as jnp
from jax import lax
from jax.experimental import pallas as pl
from jax.experimental.pallas import tpu as pltpu

D_MODEL = 1024
HEAD_DIM = 64
D_ATTN = 512
D_POOL = 512
MOBA_BLOCK = 256
MOBA_TOPK = 3
POOL_GROUP = 128
MAX_WINDOW = 16
FF_SHARD = 704
FF_PAD = 768
D_FF_PAD = 4 * FF_PAD
N_CHIPS = 4
LANES = 128
ALPHA = (2.0 * 1) ** 0.25
LN_EPS = 1e-5
NEG = -1e30
SCALE = HEAD_DIM ** -0.5
ADAM_LR, ADAM_B1, ADAM_B2, ADAM_EPS, ADAM_WD, ADAM_STEP = 0.001, 0.9, 0.999, 1e-08, 0.01, 10
VMEM_LIMIT = 56 * 1024 * 1024
MESH = pl.DeviceIdType.MESH

bf16 = jnp.bfloat16
f32 = jnp.float32


def _dot(a, b):
    return jnp.dot(a, b, preferred_element_type=f32)


def _dot_nt(a, b):
    return lax.dot_general(a, b, (((1,), (1,)), ((), ())), preferred_element_type=f32)


def _dot_tn(a, b):
    return lax.dot_general(a, b, (((0,), (0,)), ((), ())), preferred_element_type=f32)


def _dot_shards(a, w_ref):
    return jnp.concatenate([_dot(a, w_ref[p]) for p in range(w_ref.shape[0])], axis=1)


def _dot_nt_shards(d, w_ref):
    n = w_ref.shape[2]
    acc = _dot_nt(d[:, 0:n], w_ref[0])
    for p in range(1, w_ref.shape[0]):
        acc = acc + _dot_nt(d[:, p * n:(p + 1) * n], w_ref[p])
    return acc


def _column_shards_spec(k, n):
    return pl.BlockSpec((N_CHIPS, k, n // N_CHIPS), lambda i: (0, 0, 0))


def _call(body, *, name, grid, in_specs, out_specs, out_shape, scratch=(), cargo=None):
    if cargo is not None:
        return _cargo_call(body, cargo, name=name, grid=grid, in_specs=in_specs, out_specs=out_specs,
                           out_shape=out_shape, scratch=scratch)
    return pl.pallas_call(
        body, name=name, grid=grid, in_specs=in_specs, out_specs=out_specs, out_shape=out_shape,
        scratch_shapes=list(scratch),
        compiler_params=pltpu.CompilerParams(dimension_semantics=("arbitrary",) * len(grid),
                                             vmem_limit_bytes=VMEM_LIMIT))


def _rot_half(t):
    lane = lax.broadcasted_iota(jnp.int32, t.shape, 1)
    first = (lane % HEAD_DIM) < (HEAD_DIM // 2)
    return jnp.where(first, pltpu.roll(t, LANES - HEAD_DIM // 2, 1), pltpu.roll(t, HEAD_DIM // 2, 1))


def _rope(t, cos, sin_signed):
    return t * cos + _rot_half(t) * sin_signed


def _rope_bwd(d, cos, sin_signed):
    return d * cos + _rot_half(d * sin_signed)


def _gelu_parts(a):
    cdf = 0.5 * (1.0 + lax.erf(a * (2.0 ** -0.5)))
    pdf = jnp.exp(-0.5 * a * a) * ((2.0 * jnp.pi) ** -0.5)
    return a * cdf, cdf + a * pdf


def _layer_norm(h, g, b):
    mu = jnp.mean(h, axis=-1, keepdims=True)
    xc = h - mu
    var = jnp.mean(xc * xc, axis=-1, keepdims=True)
    rstd = lax.rsqrt(var + LN_EPS)
    xhat = xc * rstd
    return xhat * g + b, xhat, rstd


def _layer_norm_bwd(dy, xhat, rstd, g):
    dxh = dy * g
    m1 = jnp.mean(dxh, axis=-1, keepdims=True)
    m2 = jnp.mean(dxh * xhat, axis=-1, keepdims=True)
    return rstd * (dxh - m1 - xhat * m2)


def _inproj(x_bf, win_g, cos_t, sin_t, b_gate, cargo):
    s = x_bf.shape[0]
    tm, tn = 1024, 512

    def body(x_ref, w_ref, cos_ref, sin_ref, b_ref, o_ref):
        j = pl.program_id(1)
        acc = _dot(x_ref[...], w_ref[...])

        @pl.when(j < 2)
        def _():
            for c in range(tn // LANES):
                sl = slice(c * LANES, (c + 1) * LANES)
                o_ref[:, sl] = _rope(acc[:, sl], cos_ref[...], sin_ref[...])

        @pl.when((j >= 2) & (j < 4))
        def _():
            o_ref[...] = acc

        @pl.when(j >= 4)
        def _():
            o_ref[...] = jax.nn.sigmoid(acc + b_ref[...])

    (z,), carried = _cargo_call(
        body, cargo, name="inproj", grid=(s // tm, 4 * D_MODEL // tn),
        in_specs=[pl.BlockSpec((tm, D_MODEL), lambda i, j: (i, 0)),
                  pl.BlockSpec((None, D_MODEL, tn), lambda i, j: (j // 2, 0, j % 2)),
                  pl.BlockSpec((tm, LANES), lambda i, j: (i, 0)),
                  pl.BlockSpec((tm, LANES), lambda i, j: (i, 0)),
                  pl.BlockSpec((1, tn), lambda i, j: (0, jnp.maximum(j - 4, 0)))],
        out_specs=[pl.BlockSpec((tm, tn), lambda i, j: (i, j))],
        out_shape=[jax.ShapeDtypeStruct((s, 4 * D_MODEL), f32)])(x_bf, win_g, cos_t, sin_t, b_gate)
    return z, carried


STAT_ROWS = 8


def _pair_rows(v0, v1, fill):
    r = lax.broadcasted_iota(jnp.int32, (STAT_ROWS, v0.shape[1]), 0)
    return jnp.where(r == 0, v0, jnp.where(r == 1, v1, fill))


def _head_lanes(shape):
    lane = lax.broadcasted_iota(jnp.int32, shape, 1)
    return lane < HEAD_DIM, lane >= HEAD_DIM


def _head_rows(shape):
    row = lax.broadcasted_iota(jnp.int32, shape, 0)
    return row < HEAD_DIM, row >= HEAD_DIM


def _moba_select(q_bf, k_all, i, nb):
    k_mean = jnp.mean(k_all.reshape(nb, MOBA_BLOCK, LANES), axis=1)
    n_io = lax.broadcasted_iota(jnp.int32, (nb, MOBA_BLOCK), 0)
    past = n_io < i
    sels = []
    for head in _head_lanes((nb, LANES)):
        gate = _dot_nt(jnp.where(head, k_mean, 0.0).astype(bf16), q_bf)
        g = jnp.where(past, gate, NEG)
        rank = jnp.zeros((nb, MOBA_BLOCK), f32)
        for m in range(nb):
            gm = g[m:m + 1, :]
            rank = rank + jnp.where((gm > g) | ((gm == g) & (m < n_io)), 1.0, 0.0)
        sels.append(jnp.where(past & (rank < MOBA_TOPK), 1.0, 0.0))
    return sels


def _attn_fwd(z, nb, w_pool, pool_scale, cargo):
    s = z.shape[0]

    def body(q_ref, k_ref, v_ref, u_ref, wp_ref, sc_ref, o_ref, lse_ref, pooled_ref, mixed_ref, pm_ref, vt_s, pad):
        pair, i = pl.program_id(0), pl.program_id(1)

        @pl.when(i == 0)
        def _():
            for j in range(nb):
                vt_s[j] = v_ref[j * MOBA_BLOCK:(j + 1) * MOBA_BLOCK, :].T.astype(bf16)
            _pool_group(pair, u_ref, wp_ref, sc_ref, pooled_ref, mixed_ref, pm_ref, pad)

        k_heads = _head_lanes((MOBA_BLOCK, LANES))
        o_heads = _head_rows((LANES, MOBA_BLOCK))

        def block(j, qs, allow, m_old, l_old):
            kj, vtj = k_ref[j * MOBA_BLOCK:(j + 1) * MOBA_BLOCK, :], vt_s[j]
            m_new, alpha, l_new, pv = [], [], [], None
            for h in range(2):
                sc = jnp.where(allow[h], _dot_nt(jnp.where(k_heads[h], kj, 0.0).astype(bf16), qs), NEG)
                mx = jnp.max(sc, axis=0, keepdims=True)
                mn = mx if m_old is None else jnp.maximum(m_old[h], mx)
                p = jnp.exp(sc - mn)
                lsum = jnp.sum(p, axis=0, keepdims=True)
                if m_old is None:
                    a = None
                else:
                    a = jnp.exp(m_old[h] - mn)
                    lsum = a * l_old[h] + lsum
                t = _dot(jnp.where(o_heads[h], vtj, jnp.zeros_like(vtj)), p.astype(bf16))
                pv = t if pv is None else pv + t
                m_new.append(mn)
                alpha.append(a)
                l_new.append(lsum)
            return m_new, alpha, l_new, pv

        def query_block(c):
            q = q_ref[...]
            sels = _moba_select(q.astype(bf16), k_ref[...], c, nb)
            qs = (q * SCALE).astype(bf16)
            key = lax.broadcasted_iota(jnp.int32, (MOBA_BLOCK, MOBA_BLOCK), 0)
            qry = lax.broadcasted_iota(jnp.int32, (MOBA_BLOCK, MOBA_BLOCK), 1)
            m, _, l, acc = block(c, qs, [key <= qry] * 2, None, None)
            for j in range(c):
                m, alpha, l, pv = block(j, qs, [sels[h][j:j + 1, :] > 0.0 for h in range(2)], m, l)
                acc = acc * jnp.where(o_heads[0], alpha[0], alpha[1]) + pv
            o_ref[...] = (acc / jnp.where(o_heads[0], l[0], l[1])).T.astype(bf16)
            lse_ref[...] = _pair_rows(m[0] + jnp.log(l[0]), m[1] + jnp.log(l[1]), 0.0)

        for c in range(nb):
            pl.when(i == c)(functools.partial(query_block, c))

    group = pl.BlockSpec((s, POOL_GROUP), lambda hp, i: (0, hp))
    return _cargo_call(
        body, cargo, name="attn_fwd", grid=(D_ATTN // LANES, nb),
        in_specs=[pl.BlockSpec((MOBA_BLOCK, LANES), lambda hp, i: (i, hp)),
                  pl.BlockSpec((s, LANES), lambda hp, i: (0, 4 + hp)),
                  pl.BlockSpec((s, LANES), lambda hp, i: (0, 8 + hp)),
                  pl.BlockSpec((s, POOL_GROUP), lambda hp, i: (0, 12 + hp)),
                  pl.BlockSpec((None, POOL_GROUP, POOL_GROUP), lambda hp, i: (hp, 0, 0)),
                  pl.BlockSpec((1, POOL_GROUP), lambda hp, i: (0, hp))],
        out_specs=[pl.BlockSpec((MOBA_BLOCK, LANES), lambda hp, i: (i, hp)),
                   pl.BlockSpec((None, STAT_ROWS, MOBA_BLOCK), lambda hp, i: (hp, 0, i)), group, group, group],
        out_shape=[jax.ShapeDtypeStruct((s, D_ATTN), bf16),
                   jax.ShapeDtypeStruct((D_ATTN // LANES, STAT_ROWS, s), f32),
                   jax.ShapeDtypeStruct((s, D_POOL), bf16), jax.ShapeDtypeStruct((s, D_POOL), f32),
                   jax.ShapeDtypeStruct((s, D_POOL), bf16)],
        scratch=[pltpu.VMEM((nb, LANES, MOBA_BLOCK), bf16), pltpu.VMEM((s + MAX_WINDOW, POOL_GROUP), f32)],
    )(z, z, z, z, w_pool, pool_scale)


def _window_select(snaps, g):
    return jnp.where(g == 0, snaps[0], jnp.where(g == 1, snaps[1], jnp.where(g == 2, snaps[2], snaps[3])))


def _pool_group(g, u_ref, w_ref, sc_ref, pooled_ref, mixed_ref, pm_ref, pad):
    s = u_ref.shape[0]
    u = u_ref[...]
    pad[0:MAX_WINDOW, :] = jnp.zeros((MAX_WINDOW, POOL_GROUP), f32)
    pad[MAX_WINDOW:MAX_WINDOW + s, :] = u
    acc = u
    snaps = []
    for d in range(1, MAX_WINDOW):
        acc = acc + pad[MAX_WINDOW - d:MAX_WINDOW - d + s, :]
        if d + 1 in (2, 4, 8, 16):
            snaps.append(acc)
    win = _window_select(snaps, g)
    t = lax.broadcasted_iota(jnp.int32, (s, POOL_GROUP), 0)
    count = jnp.minimum(t + 1, jnp.left_shift(2, g)).astype(f32)
    pooled = (win / count - u).astype(bf16)
    mixed = _dot(pooled, w_ref[...].astype(bf16))
    pooled_ref[...] = pooled
    mixed_ref[...] = mixed
    pm_ref[...] = (mixed * sc_ref[...]).astype(bf16)


def _branch_merge(o_bf, pm_bf, z, wba, wbp):
    s = o_bf.shape[0]
    tm = 512

    def body(o_ref, pm_ref, ga_ref, gp_ref, wba_ref, wbp_ref, m_ref):
        ya = _dot_shards(o_ref[...], wba_ref)
        yp = _dot_shards(pm_ref[...], wbp_ref)
        m_ref[...] = (ga_ref[...] * ya + gp_ref[...] * yp).astype(bf16)

    return _call(
        body, name="branch_merge", grid=(s // tm,),
        in_specs=[pl.BlockSpec((tm, D_ATTN), lambda i: (i, 0)), pl.BlockSpec((tm, D_POOL), lambda i: (i, 0)),
                  pl.BlockSpec((tm, D_MODEL), lambda i: (i, 2)), pl.BlockSpec((tm, D_MODEL), lambda i: (i, 3)),
                  _column_shards_spec(D_ATTN, D_MODEL), _column_shards_spec(D_POOL, D_MODEL)],
        out_specs=pl.BlockSpec((tm, D_MODEL), lambda i: (i, 0)),
        out_shape=jax.ShapeDtypeStruct((s, D_MODEL), bf16))(o_bf, pm_bf, z, z, wba, wbp)


def _out_ln1(m_bf, wout, x, ln_g, ln_b):
    s = x.shape[0]
    tm = 512

    def body(m_ref, w_ref, x_ref, g_ref, b_ref, x1_ref, xhat_ref, rstd_ref):
        h = ALPHA * x_ref[...] + _dot(m_ref[...], w_ref[...])
        y, xhat, rstd = _layer_norm(h, g_ref[...], b_ref[...])
        x1_ref[...] = y.astype(bf16)
        xhat_ref[...] = xhat
        rstd_ref[...] = jnp.broadcast_to(rstd, (tm, LANES))

    row = pl.BlockSpec((tm, D_MODEL), lambda i: (i, 0))
    vec = pl.BlockSpec((1, D_MODEL), lambda i: (0, 0))
    return _call(
        body, name="out_ln1", grid=(s // tm,),
        in_specs=[row, pl.BlockSpec((D_MODEL, D_MODEL), lambda i: (0, 0)), row, vec, vec],
        out_specs=[row, row, pl.BlockSpec((tm, LANES), lambda i: (i, 0))],
        out_shape=[jax.ShapeDtypeStruct((s, D_MODEL), bf16), jax.ShapeDtypeStruct((s, D_MODEL), f32),
                   jax.ShapeDtypeStruct((s, LANES), f32)])(m_bf, wout, x, ln_g, ln_b)


FF_TILE = 256
FF_TILES_PER_SHARD = FF_PAD // FF_TILE
CONV_PAD = 8
CONV_ROWS = 16

def _ff_weight_spec(rows):
    return pl.BlockSpec((None, rows, FF_TILE), lambda j: (j // FF_TILES_PER_SHARD, 0, j % FF_TILES_PER_SHARD))


def _ff_rows_spec():
    return pl.BlockSpec((None, FF_TILE, D_MODEL), lambda j: (j // FF_TILES_PER_SHARD, j % FF_TILES_PER_SHARD, 0))


def _row_shifts(x, shifts):
    s = x.shape[0]
    padded = jnp.concatenate([x, jnp.zeros((CONV_PAD, x.shape[1]), x.dtype)], axis=0)
    return [pltpu.roll(padded, d % (s + CONV_PAD), 0)[0:s] for d in shifts]


def _ffn_up(x1_bf, wg_g, wu_g, convw_g, convb_g, cargo):
    s = x1_bf.shape[0]

    def body(x_ref, wg_ref, wu_ref, cw_ref, cb_ref, a_ref, up_ref, ac_ref, hh_ref):
        x1 = x_ref[...]
        a = _dot_nt(x1, wg_ref[...])
        up = _dot_nt(x1, wu_ref[...])
        a_ref[...] = a
        up_ref[...] = up
        a1, a2 = _row_shifts(a, (1, 2))
        ac = a * cw_ref[2:3, :] + a1 * cw_ref[1:2, :] + a2 * cw_ref[0:1, :] + cb_ref[...]
        ac_ref[...] = ac
        hg, _ = _gelu_parts(ac)
        hh_ref[...] = (hg * up).astype(bf16)

    col = pl.BlockSpec((s, FF_TILE), lambda j: (0, j))
    wide = jax.ShapeDtypeStruct((s, D_FF_PAD), f32)
    return _call(
        body, name="ffn_up", grid=(D_FF_PAD // FF_TILE,),
        in_specs=[pl.BlockSpec((s, D_MODEL), lambda j: (0, 0)), _ff_rows_spec(), _ff_rows_spec(),
                  _ff_weight_spec(8), _ff_weight_spec(1)],
        out_specs=[col, col, col, col],
        out_shape=[wide, wide, wide, jax.ShapeDtypeStruct((s, D_FF_PAD), bf16)],
        cargo=cargo)(x1_bf, wg_g, wu_g, convw_g, convb_g)


def _ffn_down_ln2_loss(hh_bf, wd, xhat1, ln1_g, ln1_b, ln2_g, ln2_b, target):
    s = hh_bf.shape[0]
    tm = 256

    def body(hh_ref, w_ref, xh1_ref, g1_ref, b1_ref, g2_ref, b2_ref, t_ref, loss_ref, dh_ref, dhb_ref, dg_ref, db_ref):
        i = pl.program_id(0)
        x1 = xh1_ref[...] * g1_ref[...] + b1_ref[...]
        h = ALPHA * x1 + _dot(hh_ref[...], w_ref[...])
        y, xhat, rstd = _layer_norm(h, g2_ref[...], b2_ref[...])
        err = y - t_ref[...]
        part = 0.5 * jnp.sum(jnp.mean(err * err, axis=-1, keepdims=True), axis=0, keepdims=True)
        dy = err * (1.0 / D_MODEL)

        @pl.when(i == 0)
        def _():
            loss_ref[...] = jnp.zeros_like(loss_ref)
            dg_ref[...] = jnp.zeros_like(dg_ref)
            db_ref[...] = jnp.zeros_like(db_ref)

        loss_ref[...] += jnp.broadcast_to(part, loss_ref.shape)
        dg_ref[...] += jnp.sum(dy * xhat, axis=0, keepdims=True)
        db_ref[...] += jnp.sum(dy, axis=0, keepdims=True)
        dh = _layer_norm_bwd(dy, xhat, rstd, g2_ref[...])
        dh_ref[...] = dh
        dhb_ref[...] = dh.astype(bf16)

    row = pl.BlockSpec((tm, D_MODEL), lambda i: (i, 0))
    vec = pl.BlockSpec((1, D_MODEL), lambda i: (0, 0))
    return _call(
        body, name="ffn_down_ln2_loss", grid=(s // tm,),
        in_specs=[pl.BlockSpec((tm, D_FF_PAD), lambda i: (i, 0)),
                  pl.BlockSpec((D_FF_PAD, D_MODEL), lambda i: (0, 0)),
                  row, vec, vec, vec, vec, row],
        out_specs=[pl.BlockSpec((8, LANES), lambda i: (0, 0)), row, row, vec, vec],
        out_shape=[jax.ShapeDtypeStruct((8, LANES), f32), jax.ShapeDtypeStruct((s, D_MODEL), f32),
                   jax.ShapeDtypeStruct((s, D_MODEL), bf16), jax.ShapeDtypeStruct((1, D_MODEL), f32),
                   jax.ShapeDtypeStruct((1, D_MODEL), f32)])(hh_bf, wd, xhat1, ln1_g, ln1_b, ln2_g, ln2_b, target)


def _ffn_act_bwd(dh2_bf, wd_g, ac, a, up, convw_g):
    s = dh2_bf.shape[0]

    def body(dh_ref, wd_ref, ac_ref, a_ref, up_ref, cw_ref, da_ref, dup_ref, dc_ref):
        dhh = _dot_nt(dh_ref[...], wd_ref[...])
        hg, dgelu = _gelu_parts(ac_ref[...])
        dup_ref[...] = (dhh * hg).astype(bf16)
        dac = dhh * up_ref[...] * dgelu
        dac1, dac2 = _row_shifts(dac, (-1, -2))
        a = a_ref[...]
        rows = [jnp.sum(d * a, axis=0, keepdims=True) for d in (dac2, dac1, dac)]
        rows.append(jnp.sum(dac, axis=0, keepdims=True))
        rows.append(jnp.zeros((CONV_ROWS - 4, FF_TILE), f32))
        dc_ref[...] = jnp.concatenate(rows, axis=0)
        da_ref[...] = (dac * cw_ref[2:3, :] + dac1 * cw_ref[1:2, :] + dac2 * cw_ref[0:1, :]).astype(bf16)

    col = pl.BlockSpec((s, FF_TILE), lambda j: (0, j))
    return _call(
        body, name="ffn_act_bwd", grid=(D_FF_PAD // FF_TILE,),
        in_specs=[pl.BlockSpec((s, D_MODEL), lambda j: (0, 0)), _ff_rows_spec(), col, col, col, _ff_weight_spec(8)],
        out_specs=[col, col, _ff_weight_spec(CONV_ROWS)],
        out_shape=[jax.ShapeDtypeStruct((s, D_FF_PAD), bf16), jax.ShapeDtypeStruct((s, D_FF_PAD), bf16),
                   jax.ShapeDtypeStruct((N_CHIPS, CONV_ROWS, FF_PAD), f32)],
    )(dh2_bf, wd_g, ac, a, up, convw_g)


def _matmul_tn(a, b, n_shards, name, tm=512, tn=1024, cargo=None):
    k, m = a.shape
    n = b.shape[1]
    tm, tn = min(tm, m), min(tn, n // n_shards)
    per = n // n_shards // tn

    def body(a_ref, b_ref, o_ref, at_s):
        @pl.when(pl.program_id(1) == 0)
        def _():
            at_s[...] = a_ref[...].T

        o_ref[...] = _dot(at_s[...], b_ref[...]).astype(bf16)

    res = _call(
        body, name=name, grid=(m // tm, n // tn),
        in_specs=[pl.BlockSpec((k, tm), lambda i, j: (0, i)), pl.BlockSpec((k, tn), lambda i, j: (0, j))],
        out_specs=[pl.BlockSpec((None, tm, tn), lambda i, j: (j // per, i, j % per))],
        out_shape=[jax.ShapeDtypeStruct((n_shards, m, n // n_shards), bf16)], scratch=[pltpu.VMEM((tm, k), bf16)],
        cargo=cargo)(a, b)
    return res[0] if cargo is None else (res[0][0], res[1])


def _dw_in(x_bf, dqkv, du, dzg, cargo):
    k, m = x_bf.shape
    tm, half = 512, D_ATTN

    def body(a_ref, dqkv_ref, du_ref, dzg_ref, o_ref, at_s):
        j = pl.program_id(1)

        @pl.when(j == 0)
        def _():
            at_s[...] = a_ref[...].T
            o_ref[:, 0:half] = _dot(at_s[...], dqkv_ref[0]).astype(bf16)
            o_ref[:, half:2 * half] = _dot(at_s[...], dqkv_ref[1]).astype(bf16)

        @pl.when(j == 1)
        def _():
            o_ref[:, 0:half] = _dot(at_s[...], dqkv_ref[2]).astype(bf16)
            o_ref[:, half:2 * half] = _dot(at_s[...], du_ref[...]).astype(bf16)

        @pl.when(j >= 2)
        def _():
            o_ref[...] = _dot(at_s[...], dzg_ref[...]).astype(bf16)

    (d_win,), carried = _cargo_call(
        body, cargo, name="dw_in", grid=(m // tm, N_CHIPS),
        in_specs=[pl.BlockSpec((k, tm), lambda i, j: (0, i)), pl.BlockSpec((3, k, half), lambda i, j: (0, 0, 0)),
                  pl.BlockSpec((k, half), lambda i, j: (0, 0)),
                  pl.BlockSpec((k, D_MODEL), lambda i, j: (0, jnp.maximum(j - 2, 0)))],
        out_specs=[pl.BlockSpec((None, tm, D_MODEL), lambda i, j: (j, i, 0))],
        out_shape=[jax.ShapeDtypeStruct((N_CHIPS, m, D_MODEL), bf16)], scratch=[pltpu.VMEM((tm, k), bf16)],
    )(x_bf, dqkv, du, dzg)
    return d_win, carried


def _ffn_in_bwd_ln1(da_bf, dup_bf, wg_g, wu_g, dh2, xhat1, rstd1, ln1_g, cargo):
    s = da_bf.shape[0]
    tm = 256

    def body(da_ref, dup_ref, wg_ref, wu_ref, dh2_ref, xh_ref, rstd_ref, g_ref, dh_ref, dhb_ref, dg_ref, db_ref):
        i = pl.program_id(0)
        dx1 = ALPHA * dh2_ref[...]
        for sh in range(N_CHIPS):
            sl = slice(sh * FF_PAD, (sh + 1) * FF_PAD)
            dx1 = dx1 + _dot(da_ref[:, sl], wg_ref[sh]) + _dot(dup_ref[:, sl], wu_ref[sh])
        xhat = xh_ref[...]

        @pl.when(i == 0)
        def _():
            dg_ref[...] = jnp.zeros_like(dg_ref)
            db_ref[...] = jnp.zeros_like(db_ref)

        dg_ref[...] += jnp.sum(dx1 * xhat, axis=0, keepdims=True)
        db_ref[...] += jnp.sum(dx1, axis=0, keepdims=True)
        dh = _layer_norm_bwd(dx1, xhat, rstd_ref[:, 0:1], g_ref[...])
        dh_ref[...] = dh
        dhb_ref[...] = dh.astype(bf16)

    row = pl.BlockSpec((tm, D_MODEL), lambda i: (i, 0))
    wide = pl.BlockSpec((tm, D_FF_PAD), lambda i: (i, 0))
    wfull = pl.BlockSpec((N_CHIPS, FF_PAD, D_MODEL), lambda i: (0, 0, 0))
    vec = pl.BlockSpec((1, D_MODEL), lambda i: (0, 0))
    return _call(
        body, name="ffn_in_bwd_ln1", grid=(s // tm,),
        in_specs=[wide, wide, wfull, wfull, row, row, pl.BlockSpec((tm, LANES), lambda i: (i, 0)), vec],
        out_specs=[row, row, vec, vec],
        out_shape=[jax.ShapeDtypeStruct((s, D_MODEL), f32), jax.ShapeDtypeStruct((s, D_MODEL), bf16),
                   jax.ShapeDtypeStruct((1, D_MODEL), f32), jax.ShapeDtypeStruct((1, D_MODEL), f32)],
        cargo=cargo)(da_bf, dup_bf, wg_g, wu_g, dh2, xhat1, rstd1, ln1_g)


def _merge_bwd(dh1_bf, wout, o_bf, pm_bf, z, wba, wbp):
    s = dh1_bf.shape[0]
    tm = 256

    def body(dh_ref, wout_ref, o_ref, pm_ref, ga_ref, gp_ref, wba_ref, wbp_ref,
             dzg_ref, dya_ref, dyp_ref, do_ref, dpm_ref, dbg_ref):
        i = pl.program_id(0)
        dm = _dot_nt(dh_ref[...], wout_ref[...])
        ya = _dot_shards(o_ref[...], wba_ref)
        yp = _dot_shards(pm_ref[...], wbp_ref)
        ga, gp = ga_ref[...], gp_ref[...]
        dza = dm * ya * ga * (1.0 - ga)
        dzp = dm * yp * gp * (1.0 - gp)

        @pl.when(i == 0)
        def _():
            dbg_ref[...] = jnp.zeros_like(dbg_ref)

        dbg_ref[:, 0:D_MODEL] += jnp.sum(dza, axis=0, keepdims=True)
        dbg_ref[:, D_MODEL:2 * D_MODEL] += jnp.sum(dzp, axis=0, keepdims=True)
        dzg_ref[:, 0:D_MODEL] = dza.astype(bf16)
        dzg_ref[:, D_MODEL:2 * D_MODEL] = dzp.astype(bf16)
        dya = (dm * ga).astype(bf16)
        dyp = (dm * gp).astype(bf16)
        dya_ref[...] = dya
        dyp_ref[...] = dyp
        do_ref[...] = _dot_nt_shards(dya, wba_ref).astype(bf16)
        dpm_ref[...] = _dot_nt_shards(dyp, wbp_ref)

    row = pl.BlockSpec((tm, D_MODEL), lambda i: (i, 0))
    half = pl.BlockSpec((tm, D_ATTN), lambda i: (i, 0))
    full = lambda r, c: pl.BlockSpec((r, c), lambda i: (0, 0))
    return _call(
        body, name="merge_bwd", grid=(s // tm,),
        in_specs=[row, full(D_MODEL, D_MODEL), half, half,
                  pl.BlockSpec((tm, D_MODEL), lambda i: (i, 2)), pl.BlockSpec((tm, D_MODEL), lambda i: (i, 3)),
                  _column_shards_spec(D_ATTN, D_MODEL), _column_shards_spec(D_POOL, D_MODEL)],
        out_specs=[pl.BlockSpec((tm, 2 * D_MODEL), lambda i: (i, 0)), row, row, half, half,
                   pl.BlockSpec((1, 2 * D_MODEL), lambda i: (0, 0))],
        out_shape=[jax.ShapeDtypeStruct((s, 2 * D_MODEL), bf16), jax.ShapeDtypeStruct((s, D_MODEL), bf16),
                   jax.ShapeDtypeStruct((s, D_MODEL), bf16), jax.ShapeDtypeStruct((s, D_ATTN), bf16),
                   jax.ShapeDtypeStruct((s, D_POOL), f32), jax.ShapeDtypeStruct((1, 2 * D_MODEL), f32)],
    )(dh1_bf, wout, o_bf, pm_bf, z, z, wba, wbp)


def _pool_bwd(dpm, mixed, pooled_bf, w_pool, pool_scale):
    s = dpm.shape[0]

    def body(dpm_ref, mixed_ref, pooled_ref, w_ref, sc_ref, du_ref, dw_ref, dsc_ref, pad):
        g = pl.program_id(0)
        dpm_v = dpm_ref[...]
        dsc_ref[...] = jnp.sum(dpm_v * mixed_ref[...], axis=0, keepdims=True)
        dmixed = (dpm_v * sc_ref[...]).astype(bf16)
        dw_ref[...] = _dot_tn(pooled_ref[...], dmixed)
        dpooled = _dot_nt(dmixed, w_ref[...].astype(bf16))
        t = lax.broadcasted_iota(jnp.int32, (s, POOL_GROUP), 0)
        count = jnp.minimum(t + 1, jnp.left_shift(2, g)).astype(f32)
        r = dpooled / count
        pad[0:s, :] = r
        pad[s:s + MAX_WINDOW, :] = jnp.zeros((MAX_WINDOW, POOL_GROUP), f32)
        acc = r
        snaps = []
        for d in range(1, MAX_WINDOW):
            acc = acc + pad[d:d + s, :]
            if d + 1 in (2, 4, 8, 16):
                snaps.append(acc)
        du_ref[...] = (_window_select(snaps, g) - dpooled).astype(bf16)

    blk = pl.BlockSpec((s, POOL_GROUP), lambda g: (0, g))
    return _call(
        body, name="pool_bwd", grid=(4,),
        in_specs=[blk, blk, blk, pl.BlockSpec((None, POOL_GROUP, POOL_GROUP), lambda g: (g, 0, 0)),
                  pl.BlockSpec((1, POOL_GROUP), lambda g: (0, g))],
        out_specs=[blk, pl.BlockSpec((None, POOL_GROUP, POOL_GROUP), lambda g: (g, 0, 0)),
                   pl.BlockSpec((1, POOL_GROUP), lambda g: (0, g))],
        out_shape=[jax.ShapeDtypeStruct((s, D_POOL), bf16), jax.ShapeDtypeStruct((4, POOL_GROUP, POOL_GROUP), f32),
                   jax.ShapeDtypeStruct((1, D_POOL), f32)],
        scratch=[pltpu.VMEM((s + MAX_WINDOW, POOL_GROUP), f32)])(dpm, mixed, pooled_bf, w_pool, pool_scale)


def _attn_bwd(z, o_bf, lse, do_bf, cos_t, sin_t, nb, cargo):
    s = z.shape[0]

    def body(q_ref, k_ref, v_ref, o_ref, lse_ref, do_ref, cq_ref, sq_ref, cf_ref, sf_ref,
             dqkv_ref, kt_s, dk_s, dv_s):
        i = pl.program_id(1)

        @pl.when(i == 0)
        def _():
            dk_s[...] = jnp.zeros_like(dk_s)
            dv_s[...] = jnp.zeros_like(dv_s)
            for j in range(nb):
                kt_s[j] = k_ref[j * MOBA_BLOCK:(j + 1) * MOBA_BLOCK, :].T.astype(bf16)

        k_heads = _head_lanes((MOBA_BLOCK, LANES))
        t_heads = _head_rows((LANES, MOBA_BLOCK))

        def query_block(c):
            q = q_ref[...]
            sels = _moba_select(q.astype(bf16), k_ref[...], c, nb)
            qs = (q * SCALE).astype(bf16)
            qraw = [jnp.where(hm, q, 0.0).astype(bf16) for hm in k_heads]
            do = do_ref[...].astype(f32)
            do_bf = do_ref[...]
            dob = [jnp.where(hm, do, 0.0).astype(bf16) for hm in k_heads]
            pick = _head_lanes((STAT_ROWS, LANES))
            r8 = lax.broadcasted_iota(jnp.int32, (STAT_ROWS, LANES), 0)
            head_pick = jnp.where(((r8 == 0) & pick[0]) | ((r8 == 1) & pick[1]), 1.0, 0.0)
            delta8 = lax.dot_general(head_pick, do * o_ref[...].astype(f32), (((1,), (1,)), ((), ())),
                                     precision=lax.Precision.HIGHEST, preferred_element_type=f32)
            lse8 = lse_ref[...]
            delta, lse_h = [delta8[0:1, :], delta8[1:2, :]], [lse8[0:1, :], lse8[1:2, :]]
            key = lax.broadcasted_iota(jnp.int32, (MOBA_BLOCK, MOBA_BLOCK), 0)
            qry = lax.broadcasted_iota(jnp.int32, (MOBA_BLOCK, MOBA_BLOCK), 1)

            def block(j, allow):
                rows = slice(j * MOBA_BLOCK, (j + 1) * MOBA_BLOCK)
                kj, vj, ktj = k_ref[rows, :], v_ref[rows, :], kt_s[j]
                dqt, dk, dv = None, None, None
                for h in range(2):
                    sc = _dot_nt(jnp.where(k_heads[h], kj, 0.0).astype(bf16), qs)
                    p = jnp.where(allow[h], jnp.exp(sc - lse_h[h]), 0.0)
                    dp = _dot_nt(jnp.where(k_heads[h], vj, 0.0).astype(bf16), do_bf)
                    ds = (p * (dp - delta[h]) * SCALE).astype(bf16)
                    tv = _dot(p.astype(bf16), dob[h])
                    tk = _dot(ds, qraw[h])
                    tq = _dot(jnp.where(t_heads[h], ktj, jnp.zeros_like(ktj)), ds)
                    dqt, dk, dv = (tq, tk, tv) if dqt is None else (dqt + tq, dk + tk, dv + tv)
                dk_s[rows, :] += dk
                dv_s[rows, :] += dv
                return dqt

            dqt = block(c, [key <= qry] * 2)
            for j in range(c):
                dqt = dqt + block(j, [sels[h][j:j + 1, :] > 0.0 for h in range(2)])
            dqkv_ref[0, c * MOBA_BLOCK:(c + 1) * MOBA_BLOCK, :] = _rope_bwd(dqt.T, cq_ref[...], sq_ref[...]).astype(bf16)

        for c in range(nb):
            pl.when(i == c)(functools.partial(query_block, c))

        @pl.when(i == nb - 1)
        def _():
            dqkv_ref[1] = _rope_bwd(dk_s[...], cf_ref[...], sf_ref[...]).astype(bf16)
            dqkv_ref[2] = dv_s[...].astype(bf16)

    qblk = pl.BlockSpec((MOBA_BLOCK, LANES), lambda hp, i: (i, hp))
    tq = pl.BlockSpec((MOBA_BLOCK, LANES), lambda hp, i: (i, 0))
    tf = pl.BlockSpec((s, LANES), lambda hp, i: (0, 0))
    return _cargo_call(
        body, cargo, name="attn_bwd", grid=(D_ATTN // LANES, nb),
        in_specs=[qblk, pl.BlockSpec((s, LANES), lambda hp, i: (0, 4 + hp)),
                  pl.BlockSpec((s, LANES), lambda hp, i: (0, 8 + hp)), qblk,
                  pl.BlockSpec((None, STAT_ROWS, MOBA_BLOCK), lambda hp, i: (hp, 0, i)), qblk, tq, tq, tf, tf],
        out_specs=[pl.BlockSpec((3, s, LANES), lambda hp, i: (0, 0, hp))],
        out_shape=[jax.ShapeDtypeStruct((3, s, D_ATTN), bf16)],
        scratch=[pltpu.VMEM((nb, LANES, MOBA_BLOCK), bf16), pltpu.VMEM((s, LANES), f32), pltpu.VMEM((s, LANES), f32)],
    )(z, z, z, o_bf, lse, do_bf, cos_t, sin_t, cos_t, sin_t)


def _inproj_dx(dqkv, du, dzg, win_g, dh1, more_items, place, cargo):
    s = du.shape[0]
    tm, half = 256, D_ATTN

    def compute(ins, outs, tile):
        dqkv_ref, du_ref, dzg_ref, w_ref, dh_ref = ins
        acc = ALPHA * dh_ref[...]
        for n, part in enumerate([dqkv_ref[0], dqkv_ref[1], dqkv_ref[2], du_ref[...]]):
            acc = acc + _dot_nt(part, w_ref[n // 2, :, (n % 2) * half:(n % 2 + 1) * half])
        for sh in range(2, N_CHIPS):
            acc = acc + _dot_nt(dzg_ref[:, (sh - 2) * D_MODEL:(sh - 1) * D_MODEL], w_ref[sh])
        outs[0][...] = acc

    row = lambda tile: pl.BlockSpec((tm, D_MODEL), lambda t, pc: (tile(t), 0))
    ins = lambda tile: [pl.BlockSpec((3, tm, half), lambda t, pc: (0, tile(t), 0)),
                        pl.BlockSpec((tm, half), lambda t, pc: (tile(t), 0)),
                        pl.BlockSpec((tm, 2 * D_MODEL), lambda t, pc: (tile(t), 0)),
                        pl.BlockSpec((N_CHIPS, D_MODEL, D_MODEL), lambda t, pc: (0, 0, 0)), row(tile)]
    main = (s // tm, ins, lambda tile: [row(tile)], [jax.ShapeDtypeStruct((s, D_MODEL), f32)],
            [dqkv, du, dzg, win_g, dh1], compute)
    outs, carried = _batched_call([main] + more_items, "inproj_dx", place, cargo)
    return (outs[0][0], [o[0] for o in outs[1:]]), carried


ANY = pl.BlockSpec(memory_space=pl.ANY)


def _chip_index():
    return 2 * lax.axis_index("x") + lax.axis_index("y")


def _peer(k):
    x, y, c = lax.axis_index("x"), lax.axis_index("y"), lax.axis_index("c")
    return (x ^ (k >> 1), y ^ (k & 1), c)


def _sibling():
    return (lax.axis_index("x"), lax.axis_index("y"), 1 - lax.axis_index("c"))


class _GatherCargo:
    def __init__(self, shards, pass_on):
        self.shards, self.pass_on, n = shards, pass_on, len(shards)
        self.inputs = [a.reshape(2, a.shape[0] // 2, a.shape[1]) for a in shards]
        self.out_shape = [jax.ShapeDtypeStruct((N_CHIPS,) + a.shape, a.dtype) for a in self.inputs]
        sem, one = pltpu.SemaphoreType.DMA((N_CHIPS - 1, n)), pltpu.SemaphoreType.DMA((n,))
        self.sems = [sem, sem, sem, sem, one, one]

    def _copies(self, src, dst, sems, a):
        send_sems, recv_sems, fsend_sems, frecv_sems, local_sems, own_sems = sems
        p, c = _chip_index(), lax.axis_index("c")
        copy = lambda s, d, **kw: functools.partial(pltpu.make_async_remote_copy, src_ref=s, dst_ref=d, **kw)
        own = copy(src[a], dst[a].at[p], send_sem=local_sems.at[a], recv_sem=own_sems.at[a], device_id=_sibling(),
                   device_id_type=MESH)
        out, arrive, onward, landed = [], [], [], []
        for k in range(1, N_CHIPS):
            ici = dict(send_sem=send_sems.at[k - 1, a], recv_sem=recv_sems.at[k - 1, a], device_id=_peer(k),
                       device_id_type=MESH)
            d2d = dict(send_sem=fsend_sems.at[k - 1, a], recv_sem=frecv_sems.at[k - 1, a], device_id=_sibling(),
                       device_id_type=MESH)
            got, theirs = dst[a].at[p ^ k, c], dst[a].at[p ^ k, 1 - c]
            out.append(copy(src[a].at[c], dst[a].at[p, c], **ici))
            arrive.append(copy(src[a].at[c], got, **ici))
            onward.append(copy(got, got, **d2d))
            landed.append(copy(theirs, theirs, **d2d))
        return own, out, arrive, onward, landed

    def stages(self, steps):
        n = len(self.shards)

        def start(src, dst, sems):
            for a in range(n):
                own, out, _, _, _ = self._copies(src, dst, sems, a)
                own().start()
                for cp in out:
                    cp().start()

        def pass_on(a):
            def act(src, dst, sems):
                _, _, arrive, onward, _ = self._copies(src, dst, sems, a)
                for k in range(N_CHIPS - 1):
                    arrive[k]().wait_recv()
                    onward[k]().start()
            return act

        def finish(src, dst, sems):
            for a in range(n):
                own, out, _, onward, landed = self._copies(src, dst, sems, a)
                for cp in landed:
                    cp().wait_recv()
                for cp in out + onward:
                    cp().wait_send()
                own().wait()

        mids = [(min(steps - 1, int(self.pass_on[a] * steps)), pass_on(a)) for a in range(n)]
        return [(0, start)] + mids + [(steps - 1, finish)]

    def results(self, outs):
        return [o.reshape((N_CHIPS,) + a.shape) for o, a in zip(outs, self.shards)]


class _ExchangeCargo:
    def __init__(self, sums, whole):
        self.inputs, self.whole, n = sums, whole, len(sums)
        self.out_shape = [jax.ShapeDtypeStruct((N_CHIPS - 1,) + g.shape[1:], g.dtype) for g in sums]
        sem = pltpu.SemaphoreType.DMA((N_CHIPS - 1, n))
        self.sems = [sem, sem]

    def _copies(self, src, dst, sems):
        send_sems, recv_sems = sems
        p = _chip_index()
        return [pltpu.make_async_remote_copy(
            src_ref=src[a].at[0] if self.whole[a] else src[a].at[p ^ k], dst_ref=dst[a].at[k - 1],
            send_sem=send_sems.at[k - 1, a], recv_sem=recv_sems.at[k - 1, a], device_id=_peer(k), device_id_type=MESH)
            for k in range(1, N_CHIPS) for a in range(len(self.inputs))]

    def stages(self, steps):
        def start(src, dst, sems):
            for cp in self._copies(src, dst, sems):
                cp.start()

        def finish(src, dst, sems):
            copies = self._copies(src, dst, sems)
            for cp in copies:
                cp.wait_recv()
            for cp in copies:
                cp.wait_send()

        return [(0, start), (steps - 1, finish)]

    def results(self, outs):
        return list(outs)


class _SwapCargo:
    def __init__(self, split):
        self.inputs, n = split, len(split)
        self.out_shape = [jax.ShapeDtypeStruct((g.shape[0],) + g.shape[2:], g.dtype) for g in split]
        sem = pltpu.SemaphoreType.DMA((n,))
        self.sems = [sem, sem]

    def _copies(self, src, dst, sems):
        c = lax.axis_index("c")
        return [pltpu.make_async_remote_copy(src_ref=src[a].at[:, 1 - c], dst_ref=dst[a], send_sem=sems[0].at[a],
                                             recv_sem=sems[1].at[a], device_id=_sibling(), device_id_type=MESH)
                for a in range(len(self.inputs))]

    def stages(self, steps):
        def start(src, dst, sems):
            for cp in self._copies(src, dst, sems):
                cp.start()

        def finish(src, dst, sems):
            copies = self._copies(src, dst, sems)
            for cp in copies:
                cp.wait_recv()
            for cp in copies:
                cp.wait_send()

        return [(0, start), (steps - 1, finish)]

    def results(self, outs):
        return list(outs)


class _MultiCargo:
    def __init__(self, cargos):
        self.cargos = cargos
        self.inputs = [a for cg in cargos for a in cg.inputs]
        self.out_shape = [o for cg in cargos for o in cg.out_shape]
        self.sems = [s for cg in cargos for s in cg.sems]

    def _parts(self, refs, count):
        out, off = [], 0
        for cg in self.cargos:
            out.append(refs[off:off + count(cg)])
            off += count(cg)
        return out

    def stages(self, steps):
        merged = []
        for n, cg in enumerate(self.cargos):
            for at, act in cg.stages(steps):
                def part(src, dst, sems, n=n, act=act):
                    act(self._parts(src, lambda g: len(g.inputs))[n], self._parts(dst, lambda g: len(g.out_shape))[n],
                        self._parts(sems, lambda g: len(g.sems))[n])
                merged.append((at, part))
        starts = [m for m in merged if m[0] == 0]
        return starts + sorted([m for m in merged if m[0] != 0], key=lambda m: m[0])

    def results(self, outs):
        return [cg.results(part) for cg, part in zip(self.cargos, self._parts(list(outs), lambda g: len(g.out_shape)))]


def _cargo_call(body, cargo, *, name, grid, in_specs, out_specs, out_shape, scratch=(), prefetch=None):
    n_in, n_out, n_scr = len(in_specs), len(out_specs), len(scratch)
    c_in, c_out = len(cargo.inputs), len(cargo.out_shape)
    steps = 1
    for g in grid:
        steps *= g
    stages = cargo.stages(steps)

    def wrapped(*refs):
        head, refs = (refs[:1], refs[1:]) if prefetch is not None else ((), refs)
        ins, refs = refs[:n_in], refs[n_in:]
        cin, refs = refs[:c_in], refs[c_in:]
        outs, refs = refs[:n_out], refs[n_out:]
        cout, refs = refs[:c_out], refs[c_out:]
        scr, sems = refs[:n_scr], refs[n_scr:]
        if not grid:
            for _, act in stages:
                act(cin, cout, sems)
            return
        step = 0
        for d in range(len(grid)):
            step = step * grid[d] + pl.program_id(d)
        for at, act in stages:
            if at == 0:
                pl.when(step == 0)(functools.partial(act, cin, cout, sems))
        body(*head, *ins, *outs, *scr)
        for at, act in stages:
            if at > 0:
                pl.when(step == at)(functools.partial(act, cin, cout, sems))

    params = dict(vmem_limit_bytes=VMEM_LIMIT)
    if grid:
        params["dimension_semantics"] = ("arbitrary",) * len(grid)
    specs = dict(grid=grid, in_specs=list(in_specs) + [ANY] * c_in, out_specs=list(out_specs) + [ANY] * c_out,
                 scratch_shapes=list(scratch) + cargo.sems)
    if prefetch is not None:
        specs = dict(grid_spec=pltpu.PrefetchScalarGridSpec(num_scalar_prefetch=1, **specs))
    res = pl.pallas_call(wrapped, name=name, out_shape=list(out_shape) + cargo.out_shape,
                         compiler_params=pltpu.CompilerParams(**params), **specs)
    head = () if prefetch is None else (prefetch,)
    return lambda *args: (lambda r: (r[:n_out], cargo.results(r[n_out:])))(res(*head, *args, *cargo.inputs))


def _split_halves(g):
    return g.reshape(g.shape[0], 2, g.shape[1] // 2, g.shape[2])


def _ew_tile(rows):
    return rows if rows <= 384 else 256


def _batched_call(items, name, place=None, cargo=None):
    offs, total = [], 0
    for it in items:
        offs.append(total)
        total += it[0]
    in_specs, out_specs, out_shape, args, spans = [], [], [], [], []
    for (n, ins, outs, shapes, operands, _), off in zip(items, offs):
        tile = lambda t, off=off, n=n: jnp.clip(t - off, 0, n - 1)
        i_s, o_s = ins(tile), outs(tile)
        spans.append((len(i_s), len(o_s)))
        in_specs += i_s
        out_specs += o_s
        out_shape += shapes
        args += operands
    n_in = len(in_specs)

    def body(*refs):
        if place is not None:
            refs = refs[1:]
        t = pl.program_id(0)
        i0, o0 = 0, n_in
        for (n, _, _, _, _, compute), off, (ni, no) in zip(items, offs, spans):
            pl.when((t >= off) & (t < off + n))(functools.partial(compute, refs[i0:i0 + ni], refs[o0:o0 + no], t - off))
            i0, o0 = i0 + ni, o0 + no

    params = pltpu.CompilerParams(dimension_semantics=("arbitrary",), vmem_limit_bytes=VMEM_LIMIT)
    carried = None
    if cargo is not None:
        res, carried = _cargo_call(body, cargo, name=name, grid=(total,), in_specs=in_specs, out_specs=out_specs,
                                   out_shape=out_shape, prefetch=place)(*args)
    elif place is None:
        res = pl.pallas_call(body, name=name, grid=(total,), in_specs=in_specs, out_specs=out_specs,
                             out_shape=out_shape, compiler_params=params)(*args)
    else:
        res = pl.pallas_call(
            body, name=name, out_shape=out_shape, compiler_params=params,
            grid_spec=pltpu.PrefetchScalarGridSpec(num_scalar_prefetch=1, grid=(total,), in_specs=in_specs,
                                                   out_specs=out_specs))(place, *args)
    out, o0 = [], 0
    for _, no in spans:
        out.append(res[o0:o0 + no])
        o0 += no
    return out if cargo is None else (out, carried)


def _add_pairs(pairs, place, name):
    def item(mine, theirs):
        lead, _, rows, cols = mine.shape
        tm = _ew_tile(rows)

        def compute(ins, outs, tile):
            outs[0][...] = (ins[0][...].astype(f32) + ins[1][...].astype(f32)).astype(outs[0].dtype)

        blk = lambda tile: pl.BlockSpec((lead, tm, cols), lambda t, pc: (0, tile(t), 0))
        mine_half = lambda tile: pl.BlockSpec((lead, None, tm, cols), lambda t, pc: (0, pc[1], tile(t), 0))
        return (rows // tm, lambda tile: [mine_half(tile), blk(tile)], lambda tile: [blk(tile)],
                [jax.ShapeDtypeStruct(theirs.shape, theirs.dtype)], [mine, theirs], compute)

    return [out[0] for out in _batched_call([item(a, b) for a, b in pairs], name, place)]


def _cargo_alone(cargo, name):
    return _cargo_call(None, cargo, name=name, grid=(), in_specs=[], out_specs=[], out_shape=[])()[1]


def _bf16_casts(x, mats, padded, cargo, name):
    def item(a, to_rows, tm):
        rows, cols = a.shape
        full = rows // tm

        def compute(ins, outs, tile):
            outs[0][...] = jnp.where(tile < full, ins[0][...], 0.0).astype(bf16)

        return (to_rows // tm, lambda tile: [pl.BlockSpec((tm, cols), lambda t: (jnp.minimum(tile(t), full - 1), 0))],
                lambda tile: [pl.BlockSpec((tm, cols), lambda t: (tile(t), 0))],
                [jax.ShapeDtypeStruct((to_rows, cols), bf16)], [a], compute)

    items = [item(x, x.shape[0], 128)] + [item(w, r, 64 if r > w.shape[0] else 128) for w, r in zip(mats, padded)]
    outs, carried = _batched_call(items, name, cargo=cargo)
    return [o[0] for o in outs], carried


def _sum_slots_items(jobs, max_rows=None):
    def item(own, recv, whole):
        _, rows, cols = own.shape
        tm = _ew_tile(rows)
        while max_rows is not None and tm > max_rows and tm % 16 == 0:
            tm //= 2

        def compute(ins, outs, tile):
            r0, r1, r2 = [ins[1][k].astype(f32) for k in range(N_CHIPS - 1)]
            outs[0][...] = (ins[0][...].astype(f32) + r0) + (r1 + r2)

        ins = lambda tile: [pl.BlockSpec((None, tm, cols), lambda t, pc: (0 if whole else pc[0], tile(t), 0)),
                            pl.BlockSpec((N_CHIPS - 1, tm, cols), lambda t, pc: (0, tile(t), 0))]
        outs = lambda tile: [pl.BlockSpec((None, tm, cols), lambda t, pc: (pc[1], tile(t), 0))]
        return rows // tm, ins, outs, [jax.ShapeDtypeStruct((2, rows, cols), f32)], [own, recv], compute

    return [item(*job) for job in jobs]


def _sibling_fill(bufs):
    n = len(bufs)

    def body(*refs):
        src, dst = refs[:n], refs[n:2 * n]
        send_sems, recv_sems = refs[2 * n:]
        c = lax.axis_index("c")
        remote = [pltpu.make_async_remote_copy(src_ref=src[a].at[c], dst_ref=dst[a].at[c], send_sem=send_sems.at[a],
                                               recv_sem=recv_sems.at[a], device_id=_sibling(), device_id_type=MESH)
                  for a in range(n)]
        for cp in remote:
            cp.start()
        for a in range(n):
            pltpu.make_async_remote_copy(src_ref=src[a].at[c], dst_ref=dst[a].at[1 - c], send_sem=send_sems.at[a],
                                         recv_sem=recv_sems.at[a], device_id=_sibling(), device_id_type=MESH).wait_recv()
        for cp in remote:
            cp.wait_send()

    sem = pltpu.SemaphoreType.DMA((n,))
    out = pl.pallas_call(
        body, name="sibling_fill", in_specs=[ANY] * n, out_specs=[ANY] * n,
        out_shape=[jax.ShapeDtypeStruct(b.shape, b.dtype) for b in bufs],
        input_output_aliases={a: a for a in range(n)}, scratch_shapes=[sem, sem])(*bufs)
    return [o.reshape(2 * b.shape[1], b.shape[2]) for o, b in zip(out, bufs)]


class _GradReducer:
    def __init__(self, place):
        self.place, self.split, self.sums, self.slots, self.bufs = place, {}, {}, {}, {}

    def swap(self, named):
        self.split.update({nm: _split_halves(g) for nm, g in named})
        return _SwapCargo([self.split[nm] for nm, _ in named])

    def summed(self, names, got, tag):
        sums = _add_pairs([(self.split[nm], g) for nm, g in zip(names, got)], self.place, "presum_" + tag)
        self.sums.update(zip(names, sums))

    def exchange(self, names):
        return _ExchangeCargo([self.sums[nm] for nm in names], [nm == "small" for nm in names])

    def arrived(self, names, slots):
        self.slots.update(zip(names, slots))

    def tree_sum_items(self, names, max_rows=None):
        return _sum_slots_items([(self.sums[nm], self.slots[nm], nm == "small") for nm in names], max_rows)

    def tree_summed(self, names, bufs):
        self.bufs.update(zip(names, bufs))

    def reduce(self, names):
        rest = [nm for nm in names if nm not in self.bufs]
        outs = _batched_call(self.tree_sum_items(rest), "sum_slots", self.place)
        self.tree_summed(rest, [out[0] for out in outs])
        return _sibling_fill([self.bufs[nm] for nm in names])


def _adamw_update(w, m, v, g):
    m = ADAM_B1 * m + (1.0 - ADAM_B1) * g
    v = ADAM_B2 * v + (1.0 - ADAM_B2) * jnp.square(g)
    m_hat = m / (1.0 - ADAM_B1 ** ADAM_STEP)
    v_hat = v / (1.0 - ADAM_B2 ** ADAM_STEP)
    return -ADAM_LR * (m_hat / (jnp.sqrt(v_hat) + ADAM_EPS) + ADAM_WD * w), m, v


def _adamw(jobs, name):
    def item(w, m, v, grad, tm):
        rows, cols = w.shape
        gcols = grad.shape[1]

        def compute(ins, outs, tile):
            g = ins[3][0:tm, 0:cols]
            outs[0][...] = g
            outs[1][...], outs[2][...], outs[3][...] = _adamw_update(ins[0][...], ins[1][...], ins[2][...], g)

        blk = lambda tile: pl.BlockSpec((tm, cols), lambda t: (tile(t), 0))
        if tm == rows:
            gspec = lambda tile: pl.BlockSpec(grad.shape, lambda t: (0, 0))
        else:
            gspec = lambda tile: pl.BlockSpec((tm, gcols), lambda t: (tile(t), 0))
        return (rows // tm, lambda tile: [blk(tile)] * 3 + [gspec(tile)], lambda tile: [blk(tile)] * 4,
                [jax.ShapeDtypeStruct((rows, cols), f32)] * 4, [w, m, v, grad], compute)

    return _batched_call([item(*job) for job in jobs], name)


def _rope_tables(s):
    half = HEAD_DIM // 2
    inv_freq = 1.0 / (10000.0 ** (jnp.arange(half, dtype=f32) / half))
    ang = jnp.arange(s, dtype=f32)[:, None] * inv_freq[None, :]
    cos, sin = jnp.cos(ang), jnp.sin(ang)
    return jnp.tile(cos, (1, LANES // half)), jnp.tile(jnp.concatenate([-sin, sin], axis=1), (1, LANES // HEAD_DIM))


def _local_step(x, x_bf, target, win_g, gather, comm, b_gate, w_pool, pool_scale, ln1_g, ln1_b, convb_g, ln2_g, ln2_b):
    s = x.shape[0]
    nb = s // MOBA_BLOCK
    cos_t, sin_t = _rope_tables(s)

    z, (wba_g, wbp_g, wout_g) = _inproj(x_bf, win_g, cos_t, sin_t, b_gate, gather["inproj"])
    wout = wout_g.reshape(D_MODEL, D_MODEL)
    (o_bf, lse, pooled_bf, mixed, pm_bf), (wg_g, wu_g, convw_g) = _attn_fwd(z, nb, w_pool, pool_scale,
                                                                           gather["attn_fwd"])
    m_bf = _branch_merge(o_bf, pm_bf, z, wba_g, wbp_g)
    x1_bf, xhat1, rstd1 = _out_ln1(m_bf, wout, x, ln1_g, ln1_b)
    (a, up, ac, hh_bf), (wd_g,) = _ffn_up(x1_bf, wg_g, wu_g, convw_g, convb_g, gather["ffn_up"])
    wd = wd_g.reshape(D_FF_PAD, D_MODEL)
    loss, dh2, dh2_bf, d_ln2_g, d_ln2_b = _ffn_down_ln2_loss(hh_bf, wd, xhat1, ln1_g, ln1_b, ln2_g, ln2_b, target)

    da_bf, dup_bf, dconv = _ffn_act_bwd(dh2_bf, wd_g, ac, a, up, convw_g)
    ffn_shape = (N_CHIPS, FF_PAD, D_MODEL)
    d_wd = _matmul_tn(hh_bf, dh2_bf, 1, "dw_ffn_down").reshape(ffn_shape)
    d_wg, got = _matmul_tn(da_bf, x1_bf, 1, "dw_ffn_gate", cargo=comm.swap([("w_ffn_down", d_wd)]))
    comm.summed(["w_ffn_down"], got, "ffn_down")
    d_wg = d_wg.reshape(ffn_shape)
    d_wu = _matmul_tn(dup_bf, x1_bf, 1, "dw_ffn_up").reshape(ffn_shape)
    gate_up = ["w_ffn_gate", "w_ffn_up", "conv_w"]
    (dh1, dh1_bf, d_ln1_g, d_ln1_b), (slots, got) = _ffn_in_bwd_ln1(
        da_bf, dup_bf, wg_g, wu_g, dh2, xhat1, rstd1, ln1_g,
        _MultiCargo([comm.exchange(["w_ffn_down"]), comm.swap(list(zip(gate_up, [d_wg, d_wu, dconv])))]))
    comm.arrived(["w_ffn_down"], slots)
    comm.summed(gate_up, got, "ffn_gate_up")
    d_wout = _matmul_tn(m_bf, dh1_bf, 1, "dw_out").reshape(N_CHIPS, D_MODEL // N_CHIPS, D_MODEL)
    dzg_bf, dya_bf, dyp_bf, do_bf, dpm, d_bgate = _merge_bwd(dh1_bf, wout, o_bf, pm_bf, z, wba_g, wbp_g)
    d_wba = _matmul_tn(o_bf, dya_bf, N_CHIPS, "dw_branch_attn")
    d_wbp = _matmul_tn(pm_bf, dyp_bf, N_CHIPS, "dw_branch_pool")
    du_bf, d_wpool, d_pscale = _pool_bwd(dpm, mixed, pooled_bf, w_pool, pool_scale)
    d_convb = dconv[:, 3, :FF_SHARD].reshape(1, N_CHIPS * FF_SHARD)
    small = _pack_small([d_bgate, d_wpool.reshape(-1, POOL_GROUP), d_pscale, d_ln1_g, d_ln1_b, d_ln2_g, d_ln2_b,
                         d_convb], loss)[None]
    branch = ["w_out", "w_branch_attn", "w_branch_pool", "small"]
    (dqkv_bf,), (slots, got) = _attn_bwd(
        z, o_bf, lse, do_bf, cos_t, sin_t, nb,
        _MultiCargo([comm.exchange(gate_up), comm.swap(list(zip(branch, [d_wout, d_wba, d_wbp, small])))]))
    comm.arrived(gate_up, slots)
    comm.summed(branch, got, "branch")
    d_win, slots = _dw_in(x_bf, dqkv_bf, du_bf, dzg_bf, comm.exchange(branch))
    comm.arrived(branch, slots)
    comm.summed(["w_in"], _cargo_alone(comm.swap([("w_in", d_win)]), "swap_w_in"), "w_in")
    others = ["w_ffn_down"] + gate_up + branch
    (grad_x, bufs), slots = _inproj_dx(dqkv_bf, du_bf, dzg_bf, win_g, dh1, comm.tree_sum_items(others, 192),
                                       comm.place, comm.exchange(["w_in"]))
    comm.tree_summed(others, bufs)
    comm.arrived(["w_in"], slots)
    return grad_x


SMALL_WEIGHTS = [("b_gate", 2 * D_MODEL), ("w_pool", 4 * POOL_GROUP * POOL_GROUP), ("pool_scale", D_POOL),
                 ("ln1_g", D_MODEL), ("ln1_b", D_MODEL), ("ln2_g", D_MODEL), ("ln2_b", D_MODEL),
                 ("conv_b", N_CHIPS * FF_SHARD)]
LOSS_ROW = 588
SMALL_ROWS = 592


def _small_pieces(ref, first_row, n):
    if ref.shape[0] > 1:
        return [(slice(first_row, first_row + n // LANES), (slice(None), slice(None)))]
    return [(slice(first_row + i, first_row + i + 1), (slice(None), slice(i * LANES, (i + 1) * LANES)))
            for i in range(n // LANES)]


def _small_first_rows():
    rows, r = [], 0
    for _, n in SMALL_WEIGHTS:
        rows.append(r)
        r += n // LANES
    return rows


def _pack_small(parts, loss):
    def body(*refs):
        out = refs[-1]
        out[...] = jnp.zeros_like(out)
        for src, r0, (_, n) in zip(refs, _small_first_rows(), SMALL_WEIGHTS):
            for rows, where in _small_pieces(src, r0, n):
                out[rows, :] = src[where]
        out[LOSS_ROW:LOSS_ROW + 1, :] = refs[len(parts)][0:1, :]

    return pl.pallas_call(body, name="pack_small", out_shape=jax.ShapeDtypeStruct((SMALL_ROWS, LANES), f32))(
        *parts, loss)


def _adamw_small(g_packed, g_conv, triples):
    n_in = 3 * len(triples)

    def body(g_ref, gc_ref, *refs):
        def update(k, g, where):
            w_ref, m_ref, v_ref = refs[3 * k:3 * k + 3]
            new = (g,) + _adamw_update(w_ref[where], m_ref[where], v_ref[where], g)
            for o_ref, val in zip(refs[n_in + 4 * k:n_in + 4 * k + 4], new):
                o_ref[where] = val

        for k, (r0, (_, n)) in enumerate(zip(_small_first_rows(), SMALL_WEIGHTS)):
            for rows, where in _small_pieces(refs[3 * k], r0, n):
                update(k, g_ref[rows, :], where)
        for tap in range(3):
            update(len(SMALL_WEIGHTS), gc_ref[tap:tap + 1, 0:FF_SHARD], (tap,))

    res = pl.pallas_call(
        body, name="adamw_small",
        out_shape=[jax.ShapeDtypeStruct(t[0].shape, f32) for t in triples for _ in range(4)],
    )(g_packed, g_conv, *[a for t in triples for a in t])
    return [res[4 * k:4 * k + 4] for k in range(len(triples))]


def _pad_conv_b(cb):
    return jnp.pad(cb.reshape(N_CHIPS, FF_SHARD), ((0, 0), (0, FF_PAD - FF_SHARD)))


def kernel(x, w_in, b_gate, w_branch_attn, w_pool, pool_scale, w_branch_pool, w_out, ln1_g, ln1_b, w_ffn_gate, w_ffn_up, conv_w, conv_b, w_ffn_down, ln2_g, ln2_b, loss_target, m_w_in, m_b_gate, m_w_branch_attn, m_w_pool, m_pool_scale, m_w_branch_pool, m_w_out, m_ln1_g, m_ln1_b, m_w_ffn_gate, m_w_ffn_up, m_conv_w, m_conv_b, m_w_ffn_down, m_ln2_g, m_ln2_b, v_w_in, v_b_gate, v_w_branch_attn, v_w_pool, v_pool_scale, v_w_branch_pool, v_w_out, v_ln1_g, v_ln1_b, v_w_ffn_gate, v_w_ffn_up, v_conv_w, v_conv_b, v_w_ffn_down, v_ln2_g, v_ln2_b):
    mats = [w_branch_attn[0], w_branch_pool[0], w_out[0], w_ffn_gate[0].T, w_ffn_up[0].T, w_ffn_down[0]]
    padded = [w.shape[0] for w in mats[:3]] + [FF_PAD] * 3
    (x_bf, *casted), (win_g,) = _bf16_casts(x[0], mats, padded, _GatherCargo([w_in[0].astype(bf16)], [0.9]),
                                            "gather_w_in")
    shards = [None] + casted + [jnp.pad(conv_w[0], ((0, CONV_ROWS - 3), (0, FF_PAD - FF_SHARD)))]
    gather = {"inproj": _GatherCargo(shards[1:4], [0.3, 0.5, 0.75]),
              "attn_fwd": _GatherCargo(shards[4:6] + shards[7:], [0.97, 0.97, 0.97]),
              "ffn_up": _GatherCargo(shards[6:7], [0.8])}

    place = jnp.stack([2 * lax.axis_index("x") + lax.axis_index("y"), lax.axis_index("c")]).astype(jnp.int32)
    comm = _GradReducer(place)
    convb_g = _pad_conv_b(conv_b).reshape(N_CHIPS, 1, FF_PAD)
    grad_x = _local_step(
        x[0], x_bf, loss_target[0], win_g, gather, comm, b_gate, w_pool[0], pool_scale, ln1_g, ln1_b, convb_g, ln2_g, ln2_b)
    names = ["w_in", "w_branch_attn", "w_branch_pool", "w_out", "w_ffn_gate", "w_ffn_up", "w_ffn_down", "conv_w", "small"]
    grads = comm.reduce(names)

    weights = [w_in[0], w_branch_attn[0], w_branch_pool[0], w_out[0], w_ffn_gate[0].T, w_ffn_up[0].T, w_ffn_down[0]]
    m_in = [m_w_in[0], m_w_branch_attn[0], m_w_branch_pool[0], m_w_out[0], m_w_ffn_gate[0].T, m_w_ffn_up[0].T,
            m_w_ffn_down[0]]
    v_in = [v_w_in[0], v_w_branch_attn[0], v_w_branch_pool[0], v_w_out[0], v_w_ffn_gate[0].T, v_w_ffn_up[0].T,
            v_w_ffn_down[0]]
    tiles = [256, 256, 256, 128, 176, 176, 176]
    jobs = list(zip(weights, m_in, v_in, grads, tiles))
    updated = _adamw(jobs[:4], "adamw_rest") + _adamw(jobs[4:], "adamw_ffn")
    res = {nm: [(r.T if nm in ("w_ffn_gate", "w_ffn_up") else r)[None] for r in outs4]
           for nm, outs4 in zip(names, updated)}

    small_w = [b_gate, w_pool, pool_scale, ln1_g, ln1_b, ln2_g, ln2_b, conv_b, conv_w]
    small_m = [m_b_gate, m_w_pool, m_pool_scale, m_ln1_g, m_ln1_b, m_ln2_g, m_ln2_b, m_conv_b, m_conv_w]
    small_v = [v_b_gate, v_w_pool, v_pool_scale, v_ln1_g, v_ln1_b, v_ln2_g, v_ln2_b, v_conv_b, v_conv_w]

    def as_kernel_sees(a):
        return a.reshape(-1, POOL_GROUP) if a.ndim == 4 else a.transpose(1, 0, 2) if a.ndim == 3 else a

    small_out = _adamw_small(grads[-1], grads[-2],
                             [tuple(as_kernel_sees(a) for a in t) for t in zip(small_w, small_m, small_v)])
    for nm, w, outs4 in zip([nm for nm, _ in SMALL_WEIGHTS] + ["conv_w"], small_w, small_out):
        res[nm] = [o.transpose(1, 0, 2) if w.ndim == 3 else o.reshape(w.shape) for o in outs4]
    loss = grads[-1][LOSS_ROW, 0]

    order = ["w_in", "b_gate", "w_branch_attn", "w_pool", "pool_scale", "w_branch_pool", "w_out", "ln1_g", "ln1_b",
             "w_ffn_gate", "w_ffn_up", "conv_w", "conv_b", "w_ffn_down", "ln2_g", "ln2_b"]
    outs = [loss, grad_x[None]]
    for kind in range(4):
        outs += [res[nm][kind] for nm in order]
    return tuple(outs)
```

```python
import functools

import jax
import jax.numpy as jnp
from jax import lax
from jax.experimental import pallas as pl
from jax.experimental.pallas import tpu as pltpu

D_MODEL = 1024
HEAD_DIM = 64
D_ATTN = 512
D_POOL = 512
MOBA_BLOCK = 256
MOBA_TOPK = 3
POOL_GROUP = 128
MAX_WINDOW = 16
FF_SHARD = 704
FF_PAD = 768
D_FF_PAD = 4 * FF_PAD
N_CHIPS = 4
LANES = 128
ALPHA = (2.0 * 1) ** 0.25
LN_EPS = 1e-5
NEG = -1e30
SCALE = HEAD_DIM ** -0.5
ADAM_LR, ADAM_B1, ADAM_B2, ADAM_EPS, ADAM_WD, ADAM_STEP = 0.001, 0.9, 0.999, 1e-08, 0.01, 10
VMEM_LIMIT = 56 * 1024 * 1024
MESH = pl.DeviceIdType.MESH

bf16 = jnp.bfloat16
f32 = jnp.float32


def _dot(a, b):
    return jnp.dot(a, b, preferred_element_type=f32)


def _dot_nt(a, b):
    return lax.dot_general(a, b, (((1,), (1,)), ((), ())), preferred_element_type=f32)


def _dot_tn(a, b):
    return lax.dot_general(a, b, (((0,), (0,)), ((), ())), preferred_element_type=f32)


def _dot_shards(a, w_ref):
    return jnp.concatenate([_dot(a, w_ref[p]) for p in range(w_ref.shape[0])], axis=1)


def _dot_nt_shards(d, w_ref):
    n = w_ref.shape[2]
    acc = _dot_nt(d[:, 0:n], w_ref[0])
    for p in range(1, w_ref.shape[0]):
        acc = acc + _dot_nt(d[:, p * n:(p + 1) * n], w_ref[p])
    return acc


def _column_shards_spec(k, n):
    return pl.BlockSpec((N_CHIPS, k, n // N_CHIPS), lambda i: (0, 0, 0))


def _call(body, *, name, grid, in_specs, out_specs, out_shape, scratch=(), cargo=None):
    if cargo is not None:
        return _cargo_call(body, cargo, name=name, grid=grid, in_specs=in_specs, out_specs=out_specs,
                           out_shape=out_shape, scratch=scratch)
    return pl.pallas_call(
        body, name=name, grid=grid, in_specs=in_specs, out_specs=out_specs, out_shape=out_shape,
        scratch_shapes=list(scratch),
        compiler_params=pltpu.CompilerParams(dimension_semantics=("arbitrary",) * len(grid),
                                             vmem_limit_bytes=VMEM_LIMIT))


def _rot_half(t):
    lane = lax.broadcasted_iota(jnp.int32, t.shape, 1)
    first = (lane % HEAD_DIM) < (HEAD_DIM // 2)
    return jnp.where(first, pltpu.roll(t, LANES - HEAD_DIM // 2, 1), pltpu.roll(t, HEAD_DIM // 2, 1))


def _rope(t, cos, sin_signed):
    return t * cos + _rot_half(t) * sin_signed


def _rope_bwd(d, cos, sin_signed):
    return d * cos + _rot_half(d * sin_signed)


def _gelu_parts(a):
    cdf = 0.5 * (1.0 + lax.erf(a * (2.0 ** -0.5)))
    pdf = jnp.exp(-0.5 * a * a) * ((2.0 * jnp.pi) ** -0.5)
    return a * cdf, cdf + a * pdf


def _layer_norm(h, g, b):
    mu = jnp.mean(h, axis=-1, keepdims=True)
    xc = h - mu
    var = jnp.mean(xc * xc, axis=-1, keepdims=True)
    rstd = lax.rsqrt(var + LN_EPS)
    xhat = xc * rstd
    return xhat * g + b, xhat, rstd


def _layer_norm_bwd(dy, xhat, rstd, g):
    dxh = dy * g
    m1 = jnp.mean(dxh, axis=-1, keepdims=True)
    m2 = jnp.mean(dxh * xhat, axis=-1, keepdims=True)
    return rstd * (dxh - m1 - xhat * m2)


def _inproj(x_bf, win_g, cos_t, sin_t, b_gate, cargo):
    s = x_bf.shape[0]
    tm, tn = 1024, 512

    def body(x_ref, w_ref, cos_ref, sin_ref, b_ref, o_ref):
        j = pl.program_id(1)
        acc = _dot(x_ref[...], w_ref[...])

        @pl.when(j < 2)
        def _():
            for c in range(tn // LANES):
                sl = slice(c * LANES, (c + 1) * LANES)
                o_ref[:, sl] = _rope(acc[:, sl], cos_ref[...], sin_ref[...])

        @pl.when((j >= 2) & (j < 4))
        def _():
            o_ref[...] = acc

        @pl.when(j >= 4)
        def _():
            o_ref[...] = jax.nn.sigmoid(acc + b_ref[...])

    (z,), carried = _cargo_call(
        body, cargo, name="inproj", grid=(s // tm, 4 * D_MODEL // tn),
        in_specs=[pl.BlockSpec((tm, D_MODEL), lambda i, j: (i, 0)),
                  pl.BlockSpec((None, D_MODEL, tn), lambda i, j: (j // 2, 0, j % 2)),
                  pl.BlockSpec((tm, LANES), lambda i, j: (i, 0)),
                  pl.BlockSpec((tm, LANES), lambda i, j: (i, 0)),
                  pl.BlockSpec((1, tn), lambda i, j: (0, jnp.maximum(j - 4, 0)))],
        out_specs=[pl.BlockSpec((tm, tn), lambda i, j: (i, j))],
        out_shape=[jax.ShapeDtypeStruct((s, 4 * D_MODEL), f32)])(x_bf, win_g, cos_t, sin_t, b_gate)
    return z, carried


STAT_ROWS = 8


def _pair_rows(v0, v1, fill):
    r = lax.broadcasted_iota(jnp.int32, (STAT_ROWS, v0.shape[1]), 0)
    return jnp.where(r == 0, v0, jnp.where(r == 1, v1, fill))


def _head_lanes(shape):
    lane = lax.broadcasted_iota(jnp.int32, shape, 1)
    return lane < HEAD_DIM, lane >= HEAD_DIM


def _head_rows(shape):
    row = lax.broadcasted_iota(jnp.int32, shape, 0)
    return row < HEAD_DIM, row >= HEAD_DIM


def _moba_select(q_bf, k_all, i, nb):
    k_mean = jnp.mean(k_all.reshape(nb, MOBA_BLOCK, LANES), axis=1)
    n_io = lax.broadcasted_iota(jnp.int32, (nb, MOBA_BLOCK), 0)
    past = n_io < i
    sels = []
    for head in _head_lanes((nb, LANES)):
        gate = _dot_nt(jnp.where(head, k_mean, 0.0).astype(bf16), q_bf)
        g = jnp.where(past, gate, NEG)
        rank = jnp.zeros((nb, MOBA_BLOCK), f32)
        for m in range(nb):
            gm = g[m:m + 1, :]
            rank = rank + jnp.where((gm > g) | ((gm == g) & (m < n_io)), 1.0, 0.0)
        sels.append(jnp.where(past & (rank < MOBA_TOPK), 1.0, 0.0))
    return sels


def _attn_fwd(z, nb, w_pool, pool_scale, cargo):
    s = z.shape[0]

    def body(q_ref, k_ref, v_ref, u_ref, wp_ref, sc_ref, o_ref, lse_ref, pooled_ref, mixed_ref, pm_ref, vt_s, pad):
        pair, i = pl.program_id(0), pl.program_id(1)

        @pl.when(i == 0)
        def _():
            for j in range(nb):
                vt_s[j] = v_ref[j * MOBA_BLOCK:(j + 1) * MOBA_BLOCK, :].T.astype(bf16)
            _pool_group(pair, u_ref, wp_ref, sc_ref, pooled_ref, mixed_ref, pm_ref, pad)

        k_heads = _head_lanes((MOBA_BLOCK, LANES))
        o_heads = _head_rows((LANES, MOBA_BLOCK))

        def block(j, qs, allow, m_old, l_old):
            kj, vtj = k_ref[j * MOBA_BLOCK:(j + 1) * MOBA_BLOCK, :], vt_s[j]
            m_new, alpha, l_new, pv = [], [], [], None
            for h in range(2):
                sc = jnp.where(allow[h], _dot_nt(jnp.where(k_heads[h], kj, 0.0).astype(bf16), qs), NEG)
                mx = jnp.max(sc, axis=0, keepdims=True)
                mn = mx if m_old is None else jnp.maximum(m_old[h], mx)
                p = jnp.exp(sc - mn)
                lsum = jnp.sum(p, axis=0, keepdims=True)
                if m_old is None:
                    a = None
                else:
                    a = jnp.exp(m_old[h] - mn)
                    lsum = a * l_old[h] + lsum
                t = _dot(jnp.where(o_heads[h], vtj, jnp.zeros_like(vtj)), p.astype(bf16))
                pv = t if pv is None else pv + t
                m_new.append(mn)
                alpha.append(a)
                l_new.append(lsum)
            return m_new, alpha, l_new, pv

        def query_block(c):
            q = q_ref[...]
            sels = _moba_select(q.astype(bf16), k_ref[...], c, nb)
            qs = (q * SCALE).astype(bf16)
            key = lax.broadcasted_iota(jnp.int32, (MOBA_BLOCK, MOBA_BLOCK), 0)
            qry = lax.broadcasted_iota(jnp.int32, (MOBA_BLOCK, MOBA_BLOCK), 1)
            m, _, l, acc = block(c, qs, [key <= qry] * 2, None, None)
            for j in range(c):
                m, alpha, l, pv = block(j, qs, [sels[h][j:j + 1, :] > 0.0 for h in range(2)], m, l)
                acc = acc * jnp.where(o_heads[0], alpha[0], alpha[1]) + pv
            o_ref[...] = (acc / jnp.where(o_heads[0], l[0], l[1])).T.astype(bf16)
            lse_ref[...] = _pair_rows(m[0] + jnp.log(l[0]), m[1] + jnp.log(l[1]), 0.0)

        for c in range(nb):
            pl.when(i == c)(functools.partial(query_block, c))

    group = pl.BlockSpec((s, POOL_GROUP), lambda hp, i: (0, hp))
    return _cargo_call(
        body, cargo, name="attn_fwd", grid=(D_ATTN // LANES, nb),
        in_specs=[pl.BlockSpec((MOBA_BLOCK, LANES), lambda hp, i: (i, hp)),
                  pl.BlockSpec((s, LANES), lambda hp, i: (0, 4 + hp)),
                  pl.BlockSpec((s, LANES), lambda hp, i: (0, 8 + hp)),
                  pl.BlockSpec((s, POOL_GROUP), lambda hp, i: (0, 12 + hp)),
                  pl.BlockSpec((None, POOL_GROUP, POOL_GROUP), lambda hp, i: (hp, 0, 0)),
                  pl.BlockSpec((1, POOL_GROUP), lambda hp, i: (0, hp))],
        out_specs=[pl.BlockSpec((MOBA_BLOCK, LANES), lambda hp, i: (i, hp)),
                   pl.BlockSpec((None, STAT_ROWS, MOBA_BLOCK), lambda hp, i: (hp, 0, i)), group, group, group],
        out_shape=[jax.ShapeDtypeStruct((s, D_ATTN), bf16),
                   jax.ShapeDtypeStruct((D_ATTN // LANES, STAT_ROWS, s), f32),
                   jax.ShapeDtypeStruct((s, D_POOL), bf16), jax.ShapeDtypeStruct((s, D_POOL), f32),
                   jax.ShapeDtypeStruct((s, D_POOL), bf16)],
        scratch=[pltpu.VMEM((nb, LANES, MOBA_BLOCK), bf16), pltpu.VMEM((s + MAX_WINDOW, POOL_GROUP), f32)],
    )(z, z, z, z, w_pool, pool_scale)


def _window_select(snaps, g):
    return jnp.where(g == 0, snaps[0], jnp.where(g == 1, snaps[1], jnp.where(g == 2, snaps[2], snaps[3])))


def _pool_group(g, u_ref, w_ref, sc_ref, pooled_ref, mixed_ref, pm_ref, pad):
    s = u_ref.shape[0]
    u = u_ref[...]
    pad[0:MAX_WINDOW, :] = jnp.zeros((MAX_WINDOW, POOL_GROUP), f32)
    pad[MAX_WINDOW:MAX_WINDOW + s, :] = u
    acc = u
    snaps = []
    for d in range(1, MAX_WINDOW):
        acc = acc + pad[MAX_WINDOW - d:MAX_WINDOW - d + s, :]
        if d + 1 in (2, 4, 8, 16):
            snaps.append(acc)
    win = _window_select(snaps, g)
    t = lax.broadcasted_iota(jnp.int32, (s, POOL_GROUP), 0)
    count = jnp.minimum(t + 1, jnp.left_shift(2, g)).astype(f32)
    pooled = (win / count - u).astype(bf16)
    mixed = _dot(pooled, w_ref[...].astype(bf16))
    pooled_ref[...] = pooled
    mixed_ref[...] = mixed
    pm_ref[...] = (mixed * sc_ref[...]).astype(bf16)


def _branch_merge(o_bf, pm_bf, z, wba, wbp):
    s = o_bf.shape[0]
    tm = 512

    def body(o_ref, pm_ref, ga_ref, gp_ref, wba_ref, wbp_ref, m_ref):
        ya = _dot_shards(o_ref[...], wba_ref)
        yp = _dot_shards(pm_ref[...], wbp_ref)
        m_ref[...] = (ga_ref[...] * ya + gp_ref[...] * yp).astype(bf16)

    return _call(
        body, name="branch_merge", grid=(s // tm,),
        in_specs=[pl.BlockSpec((tm, D_ATTN), lambda i: (i, 0)), pl.BlockSpec((tm, D_POOL), lambda i: (i, 0)),
                  pl.BlockSpec((tm, D_MODEL), lambda i: (i, 2)), pl.BlockSpec((tm, D_MODEL), lambda i: (i, 3)),
                  _column_shards_spec(D_ATTN, D_MODEL), _column_shards_spec(D_POOL, D_MODEL)],
        out_specs=pl.BlockSpec((tm, D_MODEL), lambda i: (i, 0)),
        out_shape=jax.ShapeDtypeStruct((s, D_MODEL), bf16))(o_bf, pm_bf, z, z, wba, wbp)


def _out_ln1(m_bf, wout, x, ln_g, ln_b):
    s = x.shape[0]
    tm = 512

    def body(m_ref, w_ref, x_ref, g_ref, b_ref, x1_ref, xhat_ref, rstd_ref):
        h = ALPHA * x_ref[...] + _dot(m_ref[...], w_ref[...])
        y, xhat, rstd = _layer_norm(h, g_ref[...], b_ref[...])
        x1_ref[...] = y.astype(bf16)
        xhat_ref[...] = xhat
        rstd_ref[...] = jnp.broadcast_to(rstd, (tm, LANES))

    row = pl.BlockSpec((tm, D_MODEL), lambda i: (i, 0))
    vec = pl.BlockSpec((1, D_MODEL), lambda i: (0, 0))
    return _call(
        body, name="out_ln1", grid=(s // tm,),
        in_specs=[row, pl.BlockSpec((D_MODEL, D_MODEL), lambda i: (0, 0)), row, vec, vec],
        out_specs=[row, row, pl.BlockSpec((tm, LANES), lambda i: (i, 0))],
        out_shape=[jax.ShapeDtypeStruct((s, D_MODEL), bf16), jax.ShapeDtypeStruct((s, D_MODEL), f32),
                   jax.ShapeDtypeStruct((s, LANES), f32)])(m_bf, wout, x, ln_g, ln_b)


FF_TILE = 256
FF_TILES_PER_SHARD = FF_PAD // FF_TILE
CONV_PAD = 8
CONV_ROWS = 16

def _ff_weight_spec(rows):
    return pl.BlockSpec((None, rows, FF_TILE), lambda j: (j // FF_TILES_PER_SHARD, 0, j % FF_TILES_PER_SHARD))


def _ff_rows_spec():
    return pl.BlockSpec((None, FF_TILE, D_MODEL), lambda j: (j // FF_TILES_PER_SHARD, j % FF_TILES_PER_SHARD, 0))


def _row_shifts(x, shifts):
    s = x.shape[0]
    padded = jnp.concatenate([x, jnp.zeros((CONV_PAD, x.shape[1]), x.dtype)], axis=0)
    return [pltpu.roll(padded, d % (s + CONV_PAD), 0)[0:s] for d in shifts]


def _ffn_up(x1_bf, wg_g, wu_g, convw_g, convb_g, cargo):
    s = x1_bf.shape[0]

    def body(x_ref, wg_ref, wu_ref, cw_ref, cb_ref, a_ref, up_ref, ac_ref, hh_ref):
        x1 = x_ref[...]
        a = _dot_nt(x1, wg_ref[...])
        up = _dot_nt(x1, wu_ref[...])
        a_ref[...] = a
        up_ref[...] = up
        a1, a2 = _row_shifts(a, (1, 2))
        ac = a * cw_ref[2:3, :] + a1 * cw_ref[1:2, :] + a2 * cw_ref[0:1, :] + cb_ref[...]
        ac_ref[...] = ac
        hg, _ = _gelu_parts(ac)
        hh_ref[...] = (hg * up).astype(bf16)

    col = pl.BlockSpec((s, FF_TILE), lambda j: (0, j))
    wide = jax.ShapeDtypeStruct((s, D_FF_PAD), f32)
    return _call(
        body, name="ffn_up", grid=(D_FF_PAD // FF_TILE,),
        in_specs=[pl.BlockSpec((s, D_MODEL), lambda j: (0, 0)), _ff_rows_spec(), _ff_rows_spec(),
                  _ff_weight_spec(8), _ff_weight_spec(1)],
        out_specs=[col, col, col, col],
        out_shape=[wide, wide, wide, jax.ShapeDtypeStruct((s, D_FF_PAD), bf16)],
        cargo=cargo)(x1_bf, wg_g, wu_g, convw_g, convb_g)


def _ffn_down_ln2_loss(hh_bf, wd, xhat1, ln1_g, ln1_b, ln2_g, ln2_b, target):
    s = hh_bf.shape[0]
    tm = 256

    def body(hh_ref, w_ref, xh1_ref, g1_ref, b1_ref, g2_ref, b2_ref, t_ref, loss_ref, dh_ref, dhb_ref, dg_ref, db_ref):
        i = pl.program_id(0)
        x1 = xh1_ref[...] * g1_ref[...] + b1_ref[...]
        h = ALPHA * x1 + _dot(hh_ref[...], w_ref[...])
        y, xhat, rstd = _layer_norm(h, g2_ref[...], b2_ref[...])
        err = y - t_ref[...]
        part = 0.5 * jnp.sum(jnp.mean(err * err, axis=-1, keepdims=True), axis=0, keepdims=True)
        dy = err * (1.0 / D_MODEL)

        @pl.when(i == 0)
        def _():
            loss_ref[...] = jnp.zeros_like(loss_ref)
            dg_ref[...] = jnp.zeros_like(dg_ref)
            db_ref[...] = jnp.zeros_like(db_ref)

        loss_ref[...] += jnp.broadcast_to(part, loss_ref.shape)
        dg_ref[...] += jnp.sum(dy * xhat, axis=0, keepdims=True)
        db_ref[...] += jnp.sum(dy, axis=0, keepdims=True)
        dh = _layer_norm_bwd(dy, xhat, rstd, g2_ref[...])
        dh_ref[...] = dh
        dhb_ref[...] = dh.astype(bf16)

    row = pl.BlockSpec((tm, D_MODEL), lambda i: (i, 0))
    vec = pl.BlockSpec((1, D_MODEL), lambda i: (0, 0))
    return _call(
        body, name="ffn_down_ln2_loss", grid=(s // tm,),
        in_specs=[pl.BlockSpec((tm, D_FF_PAD), lambda i: (i, 0)),
                  pl.BlockSpec((D_FF_PAD, D_MODEL), lambda i: (0, 0)),
                  row, vec, vec, vec, vec, row],
        out_specs=[pl.BlockSpec((8, LANES), lambda i: (0, 0)), row, row, vec, vec],
        out_shape=[jax.ShapeDtypeStruct((8, LANES), f32), jax.ShapeDtypeStruct((s, D_MODEL), f32),
                   jax.ShapeDtypeStruct((s, D_MODEL), bf16), jax.ShapeDtypeStruct((1, D_MODEL), f32),
                   jax.ShapeDtypeStruct((1, D_MODEL), f32)])(hh_bf, wd, xhat1, ln1_g, ln1_b, ln2_g, ln2_b, target)


def _ffn_act_bwd(dh2_bf, wd_g, ac, a, up, convw_g):
    s = dh2_bf.shape[0]

    def body(dh_ref, wd_ref, ac_ref, a_ref, up_ref, cw_ref, da_ref, dup_ref, dc_ref):
        dhh = _dot_nt(dh_ref[...], wd_ref[...])
        hg, dgelu = _gelu_parts(ac_ref[...])
        dup_ref[...] = (dhh * hg).astype(bf16)
        dac = dhh * up_ref[...] * dgelu
        dac1, dac2 = _row_shifts(dac, (-1, -2))
        a = a_ref[...]
        rows = [jnp.sum(d * a, axis=0, keepdims=True) for d in (dac2, dac1, dac)]
        rows.append(jnp.sum(dac, axis=0, keepdims=True))
        rows.append(jnp.zeros((CONV_ROWS - 4, FF_TILE), f32))
        dc_ref[...] = jnp.concatenate(rows, axis=0)
        da_ref[...] = (dac * cw_ref[2:3, :] + dac1 * cw_ref[1:2, :] + dac2 * cw_ref[0:1, :]).astype(bf16)

    col = pl.BlockSpec((s, FF_TILE), lambda j: (0, j))
    return _call(
        body, name="ffn_act_bwd", grid=(D_FF_PAD // FF_TILE,),
        in_specs=[pl.BlockSpec((s, D_MODEL), lambda j: (0, 0)), _ff_rows_spec(), col, col, col, _ff_weight_spec(8)],
        out_specs=[col, col, _ff_weight_spec(CONV_ROWS)],
        out_shape=[jax.ShapeDtypeStruct((s, D_FF_PAD), bf16), jax.ShapeDtypeStruct((s, D_FF_PAD), bf16),
                   jax.ShapeDtypeStruct((N_CHIPS, CONV_ROWS, FF_PAD), f32)],
    )(dh2_bf, wd_g, ac, a, up, convw_g)


def _matmul_tn(a, b, n_shards, name, tm=512, tn=1024, cargo=None):
    k, m = a.shape
    n = b.shape[1]
    tm, tn = min(tm, m), min(tn, n // n_shards)
    per = n // n_shards // tn

    def body(a_ref, b_ref, o_ref, at_s):
        @pl.when(pl.program_id(1) == 0)
        def _():
            at_s[...] = a_ref[...].T

        o_ref[...] = _dot(at_s[...], b_ref[...]).astype(bf16)

    res = _call(
        body, name=name, grid=(m // tm, n // tn),
        in_specs=[pl.BlockSpec((k, tm), lambda i, j: (0, i)), pl.BlockSpec((k, tn), lambda i, j: (0, j))],
        out_specs=[pl.BlockSpec((None, tm, tn), lambda i, j: (j // per, i, j % per))],
        out_shape=[jax.ShapeDtypeStruct((n_shards, m, n // n_shards), bf16)], scratch=[pltpu.VMEM((tm, k), bf16)],
        cargo=cargo)(a, b)
    return res[0] if cargo is None else (res[0][0], res[1])


def _dw_in(x_bf, dqkv, du, dzg, cargo):
    k, m = x_bf.shape
    tm, half = 512, D_ATTN

    def body(a_ref, dqkv_ref, du_ref, dzg_ref, o_ref, at_s):
        j = pl.program_id(1)

        @pl.when(j == 0)
        def _():
            at_s[...] = a_ref[...].T
            o_ref[:, 0:half] = _dot(at_s[...], dqkv_ref[0]).astype(bf16)
            o_ref[:, half:2 * half] = _dot(at_s[...], dqkv_ref[1]).astype(bf16)

        @pl.when(j == 1)
        def _():
            o_ref[:, 0:half] = _dot(at_s[...], dqkv_ref[2]).astype(bf16)
            o_ref[:, half:2 * half] = _dot(at_s[...], du_ref[...]).astype(bf16)

        @pl.when(j >= 2)
        def _():
            o_ref[...] = _dot(at_s[...], dzg_ref[...]).astype(bf16)

    (d_win,), carried = _cargo_call(
        body, cargo, name="dw_in", grid=(m // tm, N_CHIPS),
        in_specs=[pl.BlockSpec((k, tm), lambda i, j: (0, i)), pl.BlockSpec((3, k, half), lambda i, j: (0, 0, 0)),
                  pl.BlockSpec((k, half), lambda i, j: (0, 0)),
                  pl.BlockSpec((k, D_MODEL), lambda i, j: (0, jnp.maximum(j - 2, 0)))],
        out_specs=[pl.BlockSpec((None, tm, D_MODEL), lambda i, j: (j, i, 0))],
        out_shape=[jax.ShapeDtypeStruct((N_CHIPS, m, D_MODEL), bf16)], scratch=[pltpu.VMEM((tm, k), bf16)],
    )(x_bf, dqkv, du, dzg)
    return d_win, carried


def _ffn_in_bwd_ln1(da_bf, dup_bf, wg_g, wu_g, dh2, xhat1, rstd1, ln1_g, cargo):
    s = da_bf.shape[0]
    tm = 256

    def body(da_ref, dup_ref, wg_ref, wu_ref, dh2_ref, xh_ref, rstd_ref, g_ref, dh_ref, dhb_ref, dg_ref, db_ref):
        i = pl.program_id(0)
        dx1 = ALPHA * dh2_ref[...]
        for sh in range(N_CHIPS):
            sl = slice(sh * FF_PAD, (sh + 1) * FF_PAD)
            dx1 = dx1 + _dot(da_ref[:, sl], wg_ref[sh]) + _dot(dup_ref[:, sl], wu_ref[sh])
        xhat = xh_ref[...]

        @pl.when(i == 0)
        def _():
            dg_ref[...] = jnp.zeros_like(dg_ref)
            db_ref[...] = jnp.zeros_like(db_ref)

        dg_ref[...] += jnp.sum(dx1 * xhat, axis=0, keepdims=True)
        db_ref[...] += jnp.sum(dx1, axis=0, keepdims=True)
        dh = _layer_norm_bwd(dx1, xhat, rstd_ref[:, 0:1], g_ref[...])
        dh_ref[...] = dh
        dhb_ref[...] = dh.astype(bf16)

    row = pl.BlockSpec((tm, D_MODEL), lambda i: (i, 0))
    wide = pl.BlockSpec((tm, D_FF_PAD), lambda i: (i, 0))
    wfull = pl.BlockSpec((N_CHIPS, FF_PAD, D_MODEL), lambda i: (0, 0, 0))
    vec = pl.BlockSpec((1, D_MODEL), lambda i: (0, 0))
    return _call(
        body, name="ffn_in_bwd_ln1", grid=(s // tm,),
        in_specs=[wide, wide, wfull, wfull, row, row, pl.BlockSpec((tm, LANES), lambda i: (i, 0)), vec],
        out_specs=[row, row, vec, vec],
        out_shape=[jax.ShapeDtypeStruct((s, D_MODEL), f32), jax.ShapeDtypeStruct((s, D_MODEL), bf16),
                   jax.ShapeDtypeStruct((1, D_MODEL), f32), jax.ShapeDtypeStruct((1, D_MODEL), f32)],
        cargo=cargo)(da_bf, dup_bf, wg_g, wu_g, dh2, xhat1, rstd1, ln1_g)


def _merge_bwd(dh1_bf, wout, o_bf, pm_bf, z, wba, wbp):
    s = dh1_bf.shape[0]
    tm = 256

    def body(dh_ref, wout_ref, o_ref, pm_ref, ga_ref, gp_ref, wba_ref, wbp_ref,
             dzg_ref, dya_ref, dyp_ref, do_ref, dpm_ref, dbg_ref):
        i = pl.program_id(0)
        dm = _dot_nt(dh_ref[...], wout_ref[...])
        ya = _dot_shards(o_ref[...], wba_ref)
        yp = _dot_shards(pm_ref[...], wbp_ref)
        ga, gp = ga_ref[...], gp_ref[...]
        dza = dm * ya * ga * (1.0 - ga)
        dzp = dm * yp * gp * (1.0 - gp)

        @pl.when(i == 0)
        def _():
            dbg_ref[...] = jnp.zeros_like(dbg_ref)

        dbg_ref[:, 0:D_MODEL] += jnp.sum(dza, axis=0, keepdims=True)
        dbg_ref[:, D_MODEL:2 * D_MODEL] += jnp.sum(dzp, axis=0, keepdims=True)
        dzg_ref[:, 0:D_MODEL] = dza.astype(bf16)
        dzg_ref[:, D_MODEL:2 * D_MODEL] = dzp.astype(bf16)
        dya = (dm * ga).astype(bf16)
        dyp = (dm * gp).astype(bf16)
        dya_ref[...] = dya
        dyp_ref[...] = dyp
        do_ref[...] = _dot_nt_shards(dya, wba_ref).astype(bf16)
        dpm_ref[...] = _dot_nt_shards(dyp, wbp_ref)

    row = pl.BlockSpec((tm, D_MODEL), lambda i: (i, 0))
    half = pl.BlockSpec((tm, D_ATTN), lambda i: (i, 0))
    full = lambda r, c: pl.BlockSpec((r, c), lambda i: (0, 0))
    return _call(
        body, name="merge_bwd", grid=(s // tm,),
        in_specs=[row, full(D_MODEL, D_MODEL), half, half,
                  pl.BlockSpec((tm, D_MODEL), lambda i: (i, 2)), pl.BlockSpec((tm, D_MODEL), lambda i: (i, 3)),
                  _column_shards_spec(D_ATTN, D_MODEL), _column_shards_spec(D_POOL, D_MODEL)],
        out_specs=[pl.BlockSpec((tm, 2 * D_MODEL), lambda i: (i, 0)), row, row, half, half,
                   pl.BlockSpec((1, 2 * D_MODEL), lambda i: (0, 0))],
        out_shape=[jax.ShapeDtypeStruct((s, 2 * D_MODEL), bf16), jax.ShapeDtypeStruct((s, D_MODEL), bf16),
                   jax.ShapeDtypeStruct((s, D_MODEL), bf16), jax.ShapeDtypeStruct((s, D_ATTN), bf16),
                   jax.ShapeDtypeStruct((s, D_POOL), f32), jax.ShapeDtypeStruct((1, 2 * D_MODEL), f32)],
    )(dh1_bf, wout, o_bf, pm_bf, z, z, wba, wbp)


def _pool_bwd(dpm, mixed, pooled_bf, w_pool, pool_scale):
    s = dpm.shape[0]

    def body(dpm_ref, mixed_ref, pooled_ref, w_ref, sc_ref, du_ref, dw_ref, dsc_ref, pad):
        g = pl.program_id(0)
        dpm_v = dpm_ref[...]
        dsc_ref[...] = jnp.sum(dpm_v * mixed_ref[...], axis=0, keepdims=True)
        dmixed = (dpm_v * sc_ref[...]).astype(bf16)
        dw_ref[...] = _dot_tn(pooled_ref[...], dmixed)
        dpooled = _dot_nt(dmixed, w_ref[...].astype(bf16))
        t = lax.broadcasted_iota(jnp.int32, (s, POOL_GROUP), 0)
        count = jnp.minimum(t + 1, jnp.left_shift(2, g)).astype(f32)
        r = dpooled / count
        pad[0:s, :] = r
        pad[s:s + MAX_WINDOW, :] = jnp.zeros((MAX_WINDOW, POOL_GROUP), f32)
        acc = r
        snaps = []
        for d in range(1, MAX_WINDOW):
            acc = acc + pad[d:d + s, :]
            if d + 1 in (2, 4, 8, 16):
                snaps.append(acc)
        du_ref[...] = (_window_select(snaps, g) - dpooled).astype(bf16)

    blk = pl.BlockSpec((s, POOL_GROUP), lambda g: (0, g))
    return _call(
        body, name="pool_bwd", grid=(4,),
        in_specs=[blk, blk, blk, pl.BlockSpec((None, POOL_GROUP, POOL_GROUP), lambda g: (g, 0, 0)),
                  pl.BlockSpec((1, POOL_GROUP), lambda g: (0, g))],
        out_specs=[blk, pl.BlockSpec((None, POOL_GROUP, POOL_GROUP), lambda g: (g, 0, 0)),
                   pl.BlockSpec((1, POOL_GROUP), lambda g: (0, g))],
        out_shape=[jax.ShapeDtypeStruct((s, D_POOL), bf16), jax.ShapeDtypeStruct((4, POOL_GROUP, POOL_GROUP), f32),
                   jax.ShapeDtypeStruct((1, D_POOL), f32)],
        scratch=[pltpu.VMEM((s + MAX_WINDOW, POOL_GROUP), f32)])(dpm, mixed, pooled_bf, w_pool, pool_scale)


def _attn_bwd(z, o_bf, lse, do_bf, cos_t, sin_t, nb, cargo):
    s = z.shape[0]

    def body(q_ref, k_ref, v_ref, o_ref, lse_ref, do_ref, cq_ref, sq_ref, cf_ref, sf_ref,
             dqkv_ref, kt_s, dk_s, dv_s):
        i = pl.program_id(1)

        @pl.when(i == 0)
        def _():
            dk_s[...] = jnp.zeros_like(dk_s)
            dv_s[...] = jnp.zeros_like(dv_s)
            for j in range(nb):
                kt_s[j] = k_ref[j * MOBA_BLOCK:(j + 1) * MOBA_BLOCK, :].T.astype(bf16)

        k_heads = _head_lanes((MOBA_BLOCK, LANES))
        t_heads = _head_rows((LANES, MOBA_BLOCK))

        def query_block(c):
            q = q_ref[...]
            sels = _moba_select(q.astype(bf16), k_ref[...], c, nb)
            qs = (q * SCALE).astype(bf16)
            qraw = [jnp.where(hm, q, 0.0).astype(bf16) for hm in k_heads]
            do = do_ref[...].astype(f32)
            do_bf = do_ref[...]
            dob = [jnp.where(hm, do, 0.0).astype(bf16) for hm in k_heads]
            pick = _head_lanes((STAT_ROWS, LANES))
            r8 = lax.broadcasted_iota(jnp.int32, (STAT_ROWS, LANES), 0)
            head_pick = jnp.where(((r8 == 0) & pick[0]) | ((r8 == 1) & pick[1]), 1.0, 0.0)
            delta8 = lax.dot_general(head_pick, do * o_ref[...].astype(f32), (((1,), (1,)), ((), ())),
                                     precision=lax.Precision.HIGHEST, preferred_element_type=f32)
            lse8 = lse_ref[...]
            delta, lse_h = [delta8[0:1, :], delta8[1:2, :]], [lse8[0:1, :], lse8[1:2, :]]
            key = lax.broadcasted_iota(jnp.int32, (MOBA_BLOCK, MOBA_BLOCK), 0)
            qry = lax.broadcasted_iota(jnp.int32, (MOBA_BLOCK, MOBA_BLOCK), 1)

            def block(j, allow):
                rows = slice(j * MOBA_BLOCK, (j + 1) * MOBA_BLOCK)
                kj, vj, ktj = k_ref[rows, :], v_ref[rows, :], kt_s[j]
                dqt, dk, dv = None, None, None
                for h in range(2):
                    sc = _dot_nt(jnp.where(k_heads[h], kj, 0.0).astype(bf16), qs)
                    p = jnp.where(allow[h], jnp.exp(sc - lse_h[h]), 0.0)
                    dp = _dot_nt(jnp.where(k_heads[h], vj, 0.0).astype(bf16), do_bf)
                    ds = (p * (dp - delta[h]) * SCALE).astype(bf16)
                    tv = _dot(p.astype(bf16), dob[h])
                    tk = _dot(ds, qraw[h])
                    tq = _dot(jnp.where(t_heads[h], ktj, jnp.zeros_like(ktj)), ds)
                    dqt, dk, dv = (tq, tk, tv) if dqt is None else (dqt + tq, dk + tk, dv + tv)
                dk_s[rows, :] += dk
                dv_s[rows, :] += dv
                return dqt

            dqt = block(c, [key <= qry] * 2)
            for j in range(c):
                dqt = dqt + block(j, [sels[h][j:j + 1, :] > 0.0 for h in range(2)])
            dqkv_ref[0, c * MOBA_BLOCK:(c + 1) * MOBA_BLOCK, :] = _rope_bwd(dqt.T, cq_ref[...], sq_ref[...]).astype(bf16)

        for c in range(nb):
            pl.when(i == c)(functools.partial(query_block, c))

        @pl.when(i == nb - 1)
        def _():
            dqkv_ref[1] = _rope_bwd(dk_s[...], cf_ref[...], sf_ref[...]).astype(bf16)
            dqkv_ref[2] = dv_s[...].astype(bf16)

    qblk = pl.BlockSpec((MOBA_BLOCK, LANES), lambda hp, i: (i, hp))
    tq = pl.BlockSpec((MOBA_BLOCK, LANES), lambda hp, i: (i, 0))
    tf = pl.BlockSpec((s, LANES), lambda hp, i: (0, 0))
    return _cargo_call(
        body, cargo, name="attn_bwd", grid=(D_ATTN // LANES, nb),
        in_specs=[qblk, pl.BlockSpec((s, LANES), lambda hp, i: (0, 4 + hp)),
                  pl.BlockSpec((s, LANES), lambda hp, i: (0, 8 + hp)), qblk,
                  pl.BlockSpec((None, STAT_ROWS, MOBA_BLOCK), lambda hp, i: (hp, 0, i)), qblk, tq, tq, tf, tf],
        out_specs=[pl.BlockSpec((3, s, LANES), lambda hp, i: (0, 0, hp))],
        out_shape=[jax.ShapeDtypeStruct((3, s, D_ATTN), bf16)],
        scratch=[pltpu.VMEM((nb, LANES, MOBA_BLOCK), bf16), pltpu.VMEM((s, LANES), f32), pltpu.VMEM((s, LANES), f32)],
    )(z, z, z, o_bf, lse, do_bf, cos_t, sin_t, cos_t, sin_t)


def _inproj_dx(dqkv, du, dzg, win_g, dh1, more_items, place, cargo):
    s = du.shape[0]
    tm, half = 256, D_ATTN

    def compute(ins, outs, tile):
        dqkv_ref, du_ref, dzg_ref, w_ref, dh_ref = ins
        acc = ALPHA * dh_ref[...]
        for n, part in enumerate([dqkv_ref[0], dqkv_ref[1], dqkv_ref[2], du_ref[...]]):
            acc = acc + _dot_nt(part, w_ref[n // 2, :, (n % 2) * half:(n % 2 + 1) * half])
        for sh in range(2, N_CHIPS):
            acc = acc + _dot_nt(dzg_ref[:, (sh - 2) * D_MODEL:(sh - 1) * D_MODEL], w_ref[sh])
        outs[0][...] = acc

    row = lambda tile: pl.BlockSpec((tm, D_MODEL), lambda t, pc: (tile(t), 0))
    ins = lambda tile: [pl.BlockSpec((3, tm, half), lambda t, pc: (0, tile(t), 0)),
                        pl.BlockSpec((tm, half), lambda t, pc: (tile(t), 0)),
                        pl.BlockSpec((tm, 2 * D_MODEL), lambda t, pc: (tile(t), 0)),
                        pl.BlockSpec((N_CHIPS, D_MODEL, D_MODEL), lambda t, pc: (0, 0, 0)), row(tile)]
    main = (s // tm, ins, lambda tile: [row(tile)], [jax.ShapeDtypeStruct((s, D_MODEL), f32)],
            [dqkv, du, dzg, win_g, dh1], compute)
    outs, carried = _batched_call([main] + more_items, "inproj_dx", place, cargo)
    return (outs[0][0], [o[0] for o in outs[1:]]), carried


ANY = pl.BlockSpec(memory_space=pl.ANY)


def _chip_index():
    return 2 * lax.axis_index("x") + lax.axis_index("y")


def _peer(k):
    x, y, c = lax.axis_index("x"), lax.axis_index("y"), lax.axis_index("c")
    return (x ^ (k >> 1), y ^ (k & 1), c)


def _sibling():
    return (lax.axis_index("x"), lax.axis_index("y"), 1 - lax.axis_index("c"))


class _GatherCargo:
    aliased = ()

    def __init__(self, shards, pass_on):
        self.shards, self.pass_on, n = shards, pass_on, len(shards)
        self.inputs = [a.reshape(2, a.shape[0] // 2, a.shape[1]) for a in shards]
        self.out_shape = [jax.ShapeDtypeStruct((N_CHIPS,) + a.shape, a.dtype) for a in self.inputs]
        sem, one = pltpu.SemaphoreType.DMA((N_CHIPS - 1, n)), pltpu.SemaphoreType.DMA((n,))
        self.sems = [sem, sem, sem, sem, one, one]

    def _copies(self, src, dst, sems, a):
        send_sems, recv_sems, fsend_sems, frecv_sems, local_sems, own_sems = sems
        p, c = _chip_index(), lax.axis_index("c")
        copy = lambda s, d, **kw: functools.partial(pltpu.make_async_remote_copy, src_ref=s, dst_ref=d, **kw)
        own = copy(src[a], dst[a].at[p], send_sem=local_sems.at[a], recv_sem=own_sems.at[a], device_id=_sibling(),
                   device_id_type=MESH)
        out, arrive, onward, landed = [], [], [], []
        for k in range(1, N_CHIPS):
            ici = dict(send_sem=send_sems.at[k - 1, a], recv_sem=recv_sems.at[k - 1, a], device_id=_peer(k),
                       device_id_type=MESH)
            d2d = dict(send_sem=fsend_sems.at[k - 1, a], recv_sem=frecv_sems.at[k - 1, a], device_id=_sibling(),
                       device_id_type=MESH)
            got, theirs = dst[a].at[p ^ k, c], dst[a].at[p ^ k, 1 - c]
            out.append(copy(src[a].at[c], dst[a].at[p, c], **ici))
            arrive.append(copy(src[a].at[c], got, **ici))
            onward.append(copy(got, got, **d2d))
            landed.append(copy(theirs, theirs, **d2d))
        return own, out, arrive, onward, landed

    def stages(self, steps):
        n = len(self.shards)

        def start(src, dst, sems):
            for a in range(n):
                own, out, _, _, _ = self._copies(src, dst, sems, a)
                own().start()
                for cp in out:
                    cp().start()

        def pass_on(a):
            def act(src, dst, sems):
                _, _, arrive, onward, _ = self._copies(src, dst, sems, a)
                for k in range(N_CHIPS - 1):
                    arrive[k]().wait_recv()
                    onward[k]().start()
            return act

        def finish(src, dst, sems):
            for a in range(n):
                own, out, _, onward, landed = self._copies(src, dst, sems, a)
                for cp in landed:
                    cp().wait_recv()
                for cp in out + onward:
                    cp().wait_send()
                own().wait()

        mids = [(min(steps - 1, int(self.pass_on[a] * steps)), pass_on(a)) for a in range(n)]
        return [(0, start)] + mids + [(steps - 1, finish)]

    def results(self, outs):
        return [o.reshape((N_CHIPS,) + a.shape) for o, a in zip(outs, self.shards)]


class _ExchangeCargo:
    aliased = ()

    def __init__(self, sums, whole):
        self.inputs, self.whole, n = sums, whole, len(sums)
        self.out_shape = [jax.ShapeDtypeStruct((N_CHIPS - 1,) + g.shape[1:], g.dtype) for g in sums]
        sem = pltpu.SemaphoreType.DMA((N_CHIPS - 1, n))
        self.sems = [sem, sem]

    def _copies(self, src, dst, sems):
        send_sems, recv_sems = sems
        p = _chip_index()
        return [pltpu.make_async_remote_copy(
            src_ref=src[a].at[0] if self.whole[a] else src[a].at[p ^ k], dst_ref=dst[a].at[k - 1],
            send_sem=send_sems.at[k - 1, a], recv_sem=recv_sems.at[k - 1, a], device_id=_peer(k), device_id_type=MESH)
            for k in range(1, N_CHIPS) for a in range(len(self.inputs))]

    def stages(self, steps):
        def start(src, dst, sems):
            for cp in self._copies(src, dst, sems):
                cp.start()

        def finish(src, dst, sems):
            copies = self._copies(src, dst, sems)
            for cp in copies:
                cp.wait_recv()
            for cp in copies:
                cp.wait_send()

        return [(0, start), (steps - 1, finish)]

    def results(self, outs):
        return list(outs)


class _SwapCargo:
    aliased = ()

    def __init__(self, split):
        self.inputs, n = split, len(split)
        self.out_shape = [jax.ShapeDtypeStruct((g.shape[0],) + g.shape[2:], g.dtype) for g in split]
        sem = pltpu.SemaphoreType.DMA((n,))
        self.sems = [sem, sem]

    def _copies(self, src, dst, sems):
        c = lax.axis_index("c")
        return [pltpu.make_async_remote_copy(src_ref=src[a].at[:, 1 - c], dst_ref=dst[a], send_sem=sems[0].at[a],
                                             recv_sem=sems[1].at[a], device_id=_sibling(), device_id_type=MESH)
                for a in range(len(self.inputs))]

    def stages(self, steps):
        def start(src, dst, sems):
            for cp in self._copies(src, dst, sems):
                cp.start()

        def finish(src, dst, sems):
            copies = self._copies(src, dst, sems)
            for cp in copies:
                cp.wait_recv()
            for cp in copies:
                cp.wait_send()

        return [(0, start), (steps - 1, finish)]

    def results(self, outs):
        return list(outs)


class _FillCargo:
    def __init__(self, bufs):
        self.inputs, n = bufs, len(bufs)
        self.out_shape = [jax.ShapeDtypeStruct(b.shape, b.dtype) for b in bufs]
        self.aliased = [(a, a) for a in range(n)]
        sem = pltpu.SemaphoreType.DMA((n,))
        self.sems = [sem, sem]

    def _copies(self, src, dst, sems, landing):
        c = lax.axis_index("c")
        half = 1 - c if landing else c
        return [pltpu.make_async_remote_copy(src_ref=src[a].at[c], dst_ref=dst[a].at[half], send_sem=sems[0].at[a],
                                             recv_sem=sems[1].at[a], device_id=_sibling(), device_id_type=MESH)
                for a in range(len(self.inputs))]

    def stages(self, steps):
        def start(src, dst, sems):
            for cp in self._copies(src, dst, sems, False):
                cp.start()

        def finish(src, dst, sems):
            for cp in self._copies(src, dst, sems, True):
                cp.wait_recv()
            for cp in self._copies(src, dst, sems, False):
                cp.wait_send()

        return [(0, start), (steps - 1, finish)]

    def results(self, outs):
        return [o.reshape(2 * b.shape[1], b.shape[2]) for o, b in zip(outs, self.inputs)]


class _MultiCargo:
    def __init__(self, cargos):
        self.cargos = cargos
        self.inputs = [a for cg in cargos for a in cg.inputs]
        self.out_shape = [o for cg in cargos for o in cg.out_shape]
        self.sems = [s for cg in cargos for s in cg.sems]
        self.aliased, i0, o0 = [], 0, 0
        for cg in cargos:
            self.aliased += [(i0 + i, o0 + o) for i, o in cg.aliased]
            i0, o0 = i0 + len(cg.inputs), o0 + len(cg.out_shape)

    def _parts(self, refs, count):
        out, off = [], 0
        for cg in self.cargos:
            out.append(refs[off:off + count(cg)])
            off += count(cg)
        return out

    def stages(self, steps):
        merged = []
        for n, cg in enumerate(self.cargos):
            for at, act in cg.stages(steps):
                def part(src, dst, sems, n=n, act=act):
                    act(self._parts(src, lambda g: len(g.inputs))[n], self._parts(dst, lambda g: len(g.out_shape))[n],
                        self._parts(sems, lambda g: len(g.sems))[n])
                merged.append((at, part))
        starts = [m for m in merged if m[0] == 0]
        return starts + sorted([m for m in merged if m[0] != 0], key=lambda m: m[0])

    def results(self, outs):
        return [cg.results(part) for cg, part in zip(self.cargos, self._parts(list(outs), lambda g: len(g.out_shape)))]


def _cargo_call(body, cargo, *, name, grid, in_specs, out_specs, out_shape, scratch=(), prefetch=None):
    n_in, n_out, n_scr = len(in_specs), len(out_specs), len(scratch)
    c_in, c_out = len(cargo.inputs), len(cargo.out_shape)
    steps = 1
    for g in grid:
        steps *= g
    stages = cargo.stages(steps)

    def wrapped(*refs):
        head, refs = (refs[:1], refs[1:]) if prefetch is not None else ((), refs)
        ins, refs = refs[:n_in], refs[n_in:]
        cin, refs = refs[:c_in], refs[c_in:]
        outs, refs = refs[:n_out], refs[n_out:]
        cout, refs = refs[:c_out], refs[c_out:]
        scr, sems = refs[:n_scr], refs[n_scr:]
        if not grid:
            for _, act in stages:
                act(cin, cout, sems)
            return
        step = 0
        for d in range(len(grid)):
            step = step * grid[d] + pl.program_id(d)
        for at, act in stages:
            if at == 0:
                pl.when(step == 0)(functools.partial(act, cin, cout, sems))
        body(*head, *ins, *outs, *scr)
        for at, act in stages:
            if at > 0:
                pl.when(step == at)(functools.partial(act, cin, cout, sems))

    params = dict(vmem_limit_bytes=VMEM_LIMIT)
    if grid:
        params["dimension_semantics"] = ("arbitrary",) * len(grid)
    specs = dict(grid=grid, in_specs=list(in_specs) + [ANY] * c_in, out_specs=list(out_specs) + [ANY] * c_out,
                 scratch_shapes=list(scratch) + cargo.sems)
    if prefetch is not None:
        specs = dict(grid_spec=pltpu.PrefetchScalarGridSpec(num_scalar_prefetch=1, **specs))
    head = () if prefetch is None else (prefetch,)
    res = pl.pallas_call(wrapped, name=name, out_shape=list(out_shape) + cargo.out_shape,
                         input_output_aliases={len(head) + n_in + i: n_out + o for i, o in cargo.aliased},
                         compiler_params=pltpu.CompilerParams(**params), **specs)
    return lambda *args: (lambda r: (r[:n_out], cargo.results(r[n_out:])))(res(*head, *args, *cargo.inputs))


def _split_halves(g):
    return g.reshape(g.shape[0], 2, g.shape[1] // 2, g.shape[2])


def _ew_tile(rows):
    return rows if rows <= 384 else 256


def _batched_call(items, name, place=None, cargo=None):
    offs, total = [], 0
    for it in items:
        offs.append(total)
        total += it[0]
    in_specs, out_specs, out_shape, args, spans = [], [], [], [], []
    for (n, ins, outs, shapes, operands, _), off in zip(items, offs):
        tile = lambda t, off=off, n=n: jnp.clip(t - off, 0, n - 1)
        i_s, o_s = ins(tile), outs(tile)
        spans.append((len(i_s), len(o_s)))
        in_specs += i_s
        out_specs += o_s
        out_shape += shapes
        args += operands
    n_in = len(in_specs)

    def body(*refs):
        if place is not None:
            refs = refs[1:]
        t = pl.program_id(0)
        i0, o0 = 0, n_in
        for (n, _, _, _, _, compute), off, (ni, no) in zip(items, offs, spans):
            pl.when((t >= off) & (t < off + n))(functools.partial(compute, refs[i0:i0 + ni], refs[o0:o0 + no], t - off))
            i0, o0 = i0 + ni, o0 + no

    params = pltpu.CompilerParams(dimension_semantics=("arbitrary",), vmem_limit_bytes=VMEM_LIMIT)
    carried = None
    if cargo is not None:
        res, carried = _cargo_call(body, cargo, name=name, grid=(total,), in_specs=in_specs, out_specs=out_specs,
                                   out_shape=out_shape, prefetch=place)(*args)
    elif place is None:
        res = pl.pallas_call(body, name=name, grid=(total,), in_specs=in_specs, out_specs=out_specs,
                             out_shape=out_shape, compiler_params=params)(*args)
    else:
        res = pl.pallas_call(
            body, name=name, out_shape=out_shape, compiler_params=params,
            grid_spec=pltpu.PrefetchScalarGridSpec(num_scalar_prefetch=1, grid=(total,), in_specs=in_specs,
                                                   out_specs=out_specs))(place, *args)
    out, o0 = [], 0
    for _, no in spans:
        out.append(res[o0:o0 + no])
        o0 += no
    return out if cargo is None else (out, carried)


def _add_pairs(pairs, place, name):
    def item(mine, theirs):
        lead, _, rows, cols = mine.shape
        tm = _ew_tile(rows)

        def compute(ins, outs, tile):
            outs[0][...] = (ins[0][...].astype(f32) + ins[1][...].astype(f32)).astype(outs[0].dtype)

        blk = lambda tile: pl.BlockSpec((lead, tm, cols), lambda t, pc: (0, tile(t), 0))
        mine_half = lambda tile: pl.BlockSpec((lead, None, tm, cols), lambda t, pc: (0, pc[1], tile(t), 0))
        return (rows // tm, lambda tile: [mine_half(tile), blk(tile)], lambda tile: [blk(tile)],
                [jax.ShapeDtypeStruct(theirs.shape, theirs.dtype)], [mine, theirs], compute)

    return [out[0] for out in _batched_call([item(a, b) for a, b in pairs], name, place)]


def _cargo_alone(cargo, name):
    return _cargo_call(None, cargo, name=name, grid=(), in_specs=[], out_specs=[], out_shape=[])()[1]


def _bf16_casts(x, mats, padded, cargo, name):
    def item(a, to_rows, tm):
        rows, cols = a.shape
        full = rows // tm

        def compute(ins, outs, tile):
            outs[0][...] = jnp.where(tile < full, ins[0][...], 0.0).astype(bf16)

        return (to_rows // tm, lambda tile: [pl.BlockSpec((tm, cols), lambda t: (jnp.minimum(tile(t), full - 1), 0))],
                lambda tile: [pl.BlockSpec((tm, cols), lambda t: (tile(t), 0))],
                [jax.ShapeDtypeStruct((to_rows, cols), bf16)], [a], compute)

    items = [item(x, x.shape[0], 128)] + [item(w, r, 64 if r > w.shape[0] else 128) for w, r in zip(mats, padded)]
    outs, carried = _batched_call(items, name, cargo=cargo)
    return [o[0] for o in outs], carried


def _sum_slots_items(jobs, max_rows=None):
    def item(own, recv, whole):
        _, rows, cols = own.shape
        tm = _ew_tile(rows)
        while max_rows is not None and tm > max_rows and tm % 16 == 0:
            tm //= 2

        def compute(ins, outs, tile):
            r0, r1, r2 = [ins[1][k].astype(f32) for k in range(N_CHIPS - 1)]
            outs[0][...] = (ins[0][...].astype(f32) + r0) + (r1 + r2)

        ins = lambda tile: [pl.BlockSpec((None, tm, cols), lambda t, pc: (0 if whole else pc[0], tile(t), 0)),
                            pl.BlockSpec((N_CHIPS - 1, tm, cols), lambda t, pc: (0, tile(t), 0))]
        outs = lambda tile: [pl.BlockSpec((None, tm, cols), lambda t, pc: (pc[1], tile(t), 0))]
        return rows // tm, ins, outs, [jax.ShapeDtypeStruct((2, rows, cols), f32)], [own, recv], compute

    return [item(*job) for job in jobs]


class _GradReducer:
    def __init__(self, place):
        self.place, self.split, self.sums, self.slots, self.bufs, self.full = place, {}, {}, {}, {}, {}

    def swap(self, named):
        self.split.update({nm: _split_halves(g) for nm, g in named})
        return _SwapCargo([self.split[nm] for nm, _ in named])

    def summed(self, names, got, tag):
        sums = _add_pairs([(self.split[nm], g) for nm, g in zip(names, got)], self.place, "presum_" + tag)
        self.sums.update(zip(names, sums))

    def exchange(self, names):
        return _ExchangeCargo([self.sums[nm] for nm in names], [nm == "small" for nm in names])

    def arrived(self, names, slots):
        self.slots.update(zip(names, slots))

    def tree_sum_items(self, names, max_rows=None):
        return _sum_slots_items([(self.sums[nm], self.slots[nm], nm == "small") for nm in names], max_rows)

    def tree_summed(self, names, bufs):
        self.bufs.update(zip(names, bufs))

    def fill(self, names):
        return _FillCargo([self.bufs[nm] for nm in names])

    def filled(self, names, grads):
        self.full.update(zip(names, grads))

    def reduce(self, names):
        rest = [nm for nm in names if nm not in self.bufs]
        outs = _batched_call(self.tree_sum_items(rest), "sum_slots", self.place)
        self.tree_summed(rest, [out[0] for out in outs])
        rest = [nm for nm in names if nm not in self.full]
        self.filled(rest, _cargo_alone(self.fill(rest), "sibling_fill"))
        return [self.full[nm] for nm in names]


def _adamw_update(w, m, v, g):
    m = ADAM_B1 * m + (1.0 - ADAM_B1) * g
    v = ADAM_B2 * v + (1.0 - ADAM_B2) * jnp.square(g)
    m_hat = m / (1.0 - ADAM_B1 ** ADAM_STEP)
    v_hat = v / (1.0 - ADAM_B2 ** ADAM_STEP)
    return -ADAM_LR * (m_hat / (jnp.sqrt(v_hat) + ADAM_EPS) + ADAM_WD * w), m, v


def _adamw(jobs, name):
    def item(w, m, v, grad, tm):
        rows, cols = w.shape
        gcols = grad.shape[1]

        def compute(ins, outs, tile):
            g = ins[3][0:tm, 0:cols]
            outs[0][...] = g
            outs[1][...], outs[2][...], outs[3][...] = _adamw_update(ins[0][...], ins[1][...], ins[2][...], g)

        blk = lambda tile: pl.BlockSpec((tm, cols), lambda t: (tile(t), 0))
        if tm == rows:
            gspec = lambda tile: pl.BlockSpec(grad.shape, lambda t: (0, 0))
        else:
            gspec = lambda tile: pl.BlockSpec((tm, gcols), lambda t: (tile(t), 0))
        return (rows // tm, lambda tile: [blk(tile)] * 3 + [gspec(tile)], lambda tile: [blk(tile)] * 4,
                [jax.ShapeDtypeStruct((rows, cols), f32)] * 4, [w, m, v, grad], compute)

    return _batched_call([item(*job) for job in jobs], name)


def _rope_tables(s):
    half = HEAD_DIM // 2
    inv_freq = 1.0 / (10000.0 ** (jnp.arange(half, dtype=f32) / half))
    ang = jnp.arange(s, dtype=f32)[:, None] * inv_freq[None, :]
    cos, sin = jnp.cos(ang), jnp.sin(ang)
    return jnp.tile(cos, (1, LANES // half)), jnp.tile(jnp.concatenate([-sin, sin], axis=1), (1, LANES // HEAD_DIM))


def _local_step(x, x_bf, target, win_g, gather, comm, b_gate, w_pool, pool_scale, ln1_g, ln1_b, convb_g, ln2_g, ln2_b):
    s = x.shape[0]
    nb = s // MOBA_BLOCK
    cos_t, sin_t = _rope_tables(s)

    z, (wba_g, wbp_g, wout_g) = _inproj(x_bf, win_g, cos_t, sin_t, b_gate, gather["inproj"])
    wout = wout_g.reshape(D_MODEL, D_MODEL)
    (o_bf, lse, pooled_bf, mixed, pm_bf), (wg_g, wu_g, convw_g) = _attn_fwd(z, nb, w_pool, pool_scale,
                                                                           gather["attn_fwd"])
    m_bf = _branch_merge(o_bf, pm_bf, z, wba_g, wbp_g)
    x1_bf, xhat1, rstd1 = _out_ln1(m_bf, wout, x, ln1_g, ln1_b)
    (a, up, ac, hh_bf), (wd_g,) = _ffn_up(x1_bf, wg_g, wu_g, convw_g, convb_g, gather["ffn_up"])
    wd = wd_g.reshape(D_FF_PAD, D_MODEL)
    loss, dh2, dh2_bf, d_ln2_g, d_ln2_b = _ffn_down_ln2_loss(hh_bf, wd, xhat1, ln1_g, ln1_b, ln2_g, ln2_b, target)

    da_bf, dup_bf, dconv = _ffn_act_bwd(dh2_bf, wd_g, ac, a, up, convw_g)
    ffn_shape = (N_CHIPS, FF_PAD, D_MODEL)
    d_wd = _matmul_tn(hh_bf, dh2_bf, 1, "dw_ffn_down").reshape(ffn_shape)
    d_wg, got = _matmul_tn(da_bf, x1_bf, 1, "dw_ffn_gate", cargo=comm.swap([("w_ffn_down", d_wd)]))
    comm.summed(["w_ffn_down"], got, "ffn_down")
    d_wg = d_wg.reshape(ffn_shape)
    d_wu = _matmul_tn(dup_bf, x1_bf, 1, "dw_ffn_up").reshape(ffn_shape)
    gate_up = ["w_ffn_gate", "w_ffn_up", "conv_w"]
    (dh1, dh1_bf, d_ln1_g, d_ln1_b), (slots, got) = _ffn_in_bwd_ln1(
        da_bf, dup_bf, wg_g, wu_g, dh2, xhat1, rstd1, ln1_g,
        _MultiCargo([comm.exchange(["w_ffn_down"]), comm.swap(list(zip(gate_up, [d_wg, d_wu, dconv])))]))
    comm.arrived(["w_ffn_down"], slots)
    comm.summed(gate_up, got, "ffn_gate_up")
    d_wout = _matmul_tn(m_bf, dh1_bf, 1, "dw_out").reshape(N_CHIPS, D_MODEL // N_CHIPS, D_MODEL)
    dzg_bf, dya_bf, dyp_bf, do_bf, dpm, d_bgate = _merge_bwd(dh1_bf, wout, o_bf, pm_bf, z, wba_g, wbp_g)
    d_wba = _matmul_tn(o_bf, dya_bf, N_CHIPS, "dw_branch_attn")
    d_wbp = _matmul_tn(pm_bf, dyp_bf, N_CHIPS, "dw_branch_pool")
    du_bf, d_wpool, d_pscale = _pool_bwd(dpm, mixed, pooled_bf, w_pool, pool_scale)
    d_convb = dconv[:, 3, :FF_SHARD].reshape(1, N_CHIPS * FF_SHARD)
    small = _pack_small([d_bgate, d_wpool.reshape(-1, POOL_GROUP), d_pscale, d_ln1_g, d_ln1_b, d_ln2_g, d_ln2_b,
                         d_convb], loss)[None]
    branch = ["w_out", "w_branch_attn", "w_branch_pool", "small"]
    (dqkv_bf,), (slots, got) = _attn_bwd(
        z, o_bf, lse, do_bf, cos_t, sin_t, nb,
        _MultiCargo([comm.exchange(gate_up), comm.swap(list(zip(branch, [d_wout, d_wba, d_wbp, small])))]))
    comm.arrived(gate_up, slots)
    comm.summed(branch, got, "branch")
    d_win, slots = _dw_in(x_bf, dqkv_bf, du_bf, dzg_bf, comm.exchange(branch))
    comm.arrived(branch, slots)
    ffn = ["w_ffn_down"] + gate_up
    bufs, got = _batched_call(comm.tree_sum_items(ffn), "swap_w_in", comm.place, comm.swap([("w_in", d_win)]))
    comm.tree_summed(ffn, [b[0] for b in bufs])
    comm.summed(["w_in"], got, "w_in")
    (grad_x, bufs), (slots, full) = _inproj_dx(
        dqkv_bf, du_bf, dzg_bf, win_g, dh1, comm.tree_sum_items(branch, 192), comm.place,
        _MultiCargo([comm.exchange(["w_in"]), comm.fill(ffn)]))
    comm.tree_summed(branch, bufs)
    comm.filled(ffn, full)
    comm.arrived(["w_in"], slots)
    return grad_x


SMALL_WEIGHTS = [("b_gate", 2 * D_MODEL), ("w_pool", 4 * POOL_GROUP * POOL_GROUP), ("pool_scale", D_POOL),
                 ("ln1_g", D_MODEL), ("ln1_b", D_MODEL), ("ln2_g", D_MODEL), ("ln2_b", D_MODEL),
                 ("conv_b", N_CHIPS * FF_SHARD)]
LOSS_ROW = 588
SMALL_ROWS = 592


def _small_pieces(ref, first_row, n):
    if ref.shape[0] > 1:
        return [(slice(first_row, first_row + n // LANES), (slice(None), slice(None)))]
    return [(slice(first_row + i, first_row + i + 1), (slice(None), slice(i * LANES, (i + 1) * LANES)))
            for i in range(n // LANES)]


def _small_first_rows():
    rows, r = [], 0
    for _, n in SMALL_WEIGHTS:
        rows.append(r)
        r += n // LANES
    return rows


def _pack_small(parts, loss):
    def body(*refs):
        out = refs[-1]
        out[...] = jnp.zeros_like(out)
        for src, r0, (_, n) in zip(refs, _small_first_rows(), SMALL_WEIGHTS):
            for rows, where in _small_pieces(src, r0, n):
                out[rows, :] = src[where]
        out[LOSS_ROW:LOSS_ROW + 1, :] = refs[len(parts)][0:1, :]

    return pl.pallas_call(body, name="pack_small", out_shape=jax.ShapeDtypeStruct((SMALL_ROWS, LANES), f32))(
        *parts, loss)


def _adamw_small(g_packed, g_conv, triples):
    n_in = 3 * len(triples)

    def body(g_ref, gc_ref, *refs):
        def update(k, g, where):
            w_ref, m_ref, v_ref = refs[3 * k:3 * k + 3]
            new = (g,) + _adamw_update(w_ref[where], m_ref[where], v_ref[where], g)
            for o_ref, val in zip(refs[n_in + 4 * k:n_in + 4 * k + 4], new):
                o_ref[where] = val

        for k, (r0, (_, n)) in enumerate(zip(_small_first_rows(), SMALL_WEIGHTS)):
            for rows, where in _small_pieces(refs[3 * k], r0, n):
                update(k, g_ref[rows, :], where)
        for tap in range(3):
            update(len(SMALL_WEIGHTS), gc_ref[tap:tap + 1, 0:FF_SHARD], (tap,))

    res = pl.pallas_call(
        body, name="adamw_small",
        out_shape=[jax.ShapeDtypeStruct(t[0].shape, f32) for t in triples for _ in range(4)],
    )(g_packed, g_conv, *[a for t in triples for a in t])
    return [res[4 * k:4 * k + 4] for k in range(len(triples))]


def _pad_conv_b(cb):
    return jnp.pad(cb.reshape(N_CHIPS, FF_SHARD), ((0, 0), (0, FF_PAD - FF_SHARD)))


def kernel(x, w_in, b_gate, w_branch_attn, w_pool, pool_scale, w_branch_pool, w_out, ln1_g, ln1_b, w_ffn_gate, w_ffn_up, conv_w, conv_b, w_ffn_down, ln2_g, ln2_b, loss_target, m_w_in, m_b_gate, m_w_branch_attn, m_w_pool, m_pool_scale, m_w_branch_pool, m_w_out, m_ln1_g, m_ln1_b, m_w_ffn_gate, m_w_ffn_up, m_conv_w, m_conv_b, m_w_ffn_down, m_ln2_g, m_ln2_b, v_w_in, v_b_gate, v_w_branch_attn, v_w_pool, v_pool_scale, v_w_branch_pool, v_w_out, v_ln1_g, v_ln1_b, v_w_ffn_gate, v_w_ffn_up, v_conv_w, v_conv_b, v_w_ffn_down, v_ln2_g, v_ln2_b):
    mats = [w_branch_attn[0], w_branch_pool[0], w_out[0], w_ffn_gate[0].T, w_ffn_up[0].T, w_ffn_down[0]]
    padded = [w.shape[0] for w in mats[:3]] + [FF_PAD] * 3
    (x_bf, *casted), (win_g,) = _bf16_casts(x[0], mats, padded, _GatherCargo([w_in[0].astype(bf16)], [0.9]),
                                            "gather_w_in")
    shards = [None] + casted + [jnp.pad(conv_w[0], ((0, CONV_ROWS - 3), (0, FF_PAD - FF_SHARD)))]
    gather = {"inproj": _GatherCargo(shards[1:4], [0.3, 0.5, 0.75]),
              "attn_fwd": _GatherCargo(shards[4:6] + shards[7:], [0.97, 0.97, 0.97]),
              "ffn_up": _GatherCargo(shards[6:7], [0.8])}

    place = jnp.stack([2 * lax.axis_index("x") + lax.axis_index("y"), lax.axis_index("c")]).astype(jnp.int32)
    comm = _GradReducer(place)
    convb_g = _pad_conv_b(conv_b).reshape(N_CHIPS, 1, FF_PAD)
    grad_x = _local_step(
        x[0], x_bf, loss_target[0], win_g, gather, comm, b_gate, w_pool[0], pool_scale, ln1_g, ln1_b, convb_g, ln2_g, ln2_b)
    names = ["w_in", "w_branch_attn", "w_branch_pool", "w_out", "w_ffn_gate", "w_ffn_up", "w_ffn_down", "conv_w", "small"]
    grads = comm.reduce(names)

    weights = [w_in[0], w_branch_attn[0], w_branch_pool[0], w_out[0], w_ffn_gate[0].T, w_ffn_up[0].T, w_ffn_down[0]]
    m_in = [m_w_in[0], m_w_branch_attn[0], m_w_branch_pool[0], m_w_out[0], m_w_ffn_gate[0].T, m_w_ffn_up[0].T,
            m_w_ffn_down[0]]
    v_in = [v_w_in[0], v_w_branch_attn[0], v_w_branch_pool[0], v_w_out[0], v_w_ffn_gate[0].T, v_w_ffn_up[0].T,
            v_w_ffn_down[0]]
    tiles = [256, 256, 256, 128, 176, 176, 176]
    jobs = list(zip(weights, m_in, v_in, grads, tiles))
    updated = _adamw(jobs[:4], "adamw_rest") + _adamw(jobs[4:], "adamw_ffn")
    res = {nm: [(r.T if nm in ("w_ffn_gate", "w_ffn_up") else r)[None] for r in outs4]
           for nm, outs4 in zip(names, updated)}

    small_w = [b_gate, w_pool, pool_scale, ln1_g, ln1_b, ln2_g, ln2_b, conv_b, conv_w]
    small_m = [m_b_gate, m_w_pool, m_pool_scale, m_ln1_g, m_ln1_b, m_ln2_g, m_ln2_b, m_conv_b, m_conv_w]
    small_v = [v_b_gate, v_w_pool, v_pool_scale, v_ln1_g, v_ln1_b, v_ln2_g, v_ln2_b, v_conv_b, v_conv_w]

    def as_kernel_sees(a):
        return a.reshape(-1, POOL_GROUP) if a.ndim == 4 else a.transpose(1, 0, 2) if a.ndim == 3 else a

    small_out = _adamw_small(grads[-1], grads[-2],
                             [tuple(as_kernel_sees(a) for a in t) for t in zip(small_w, small_m, small_v)])
    for nm, w, outs4 in zip([nm for nm, _ in SMALL_WEIGHTS] + ["conv_w"], small_w, small_out):
        res[nm] = [o.transpose(1, 0, 2) if w.ndim == 3 else o.reshape(w.shape) for o in outs4]
    loss = grads[-1][LOSS_ROW, 0]

    order = ["w_in", "b_gate", "w_branch_attn", "w_pool", "pool_scale", "w_branch_pool", "w_out", "ln1_g", "ln1_b",
             "w_ffn_gate", "w_ffn_up", "conv_w", "conv_b", "w_ffn_down", "ln2_g", "ln2_b"]
    outs = [loss, grad_x[None]]
    for kind in range(4):
        outs += [res[nm][kind] for nm in order]
    return tuple(outs)
```

```python
import functools

import jax
import jax.numpy as jnp
from jax import lax
from jax.experimental import pallas as pl
from jax.experimental.pallas import tpu as pltpu

D_MODEL = 1024
HEAD_DIM = 64
D_ATTN = 512
D_POOL = 512
MOBA_BLOCK = 256
MOBA_TOPK = 3
POOL_GROUP = 128
MAX_WINDOW = 16
FF_SHARD = 704
FF_PAD = 768
D_FF_PAD = 4 * FF_PAD
N_CHIPS = 4
LANES = 128
ALPHA = (2.0 * 1) ** 0.25
LN_EPS = 1e-5
NEG = -1e30
SCALE = HEAD_DIM ** -0.5
ADAM_LR, ADAM_B1, ADAM_B2, ADAM_EPS, ADAM_WD, ADAM_STEP = 0.001, 0.9, 0.999, 1e-08, 0.01, 10
VMEM_LIMIT = 56 * 1024 * 1024
MESH = pl.DeviceIdType.MESH

bf16 = jnp.bfloat16
f32 = jnp.float32


def _dot(a, b):
    return jnp.dot(a, b, preferred_element_type=f32)


def _dot_nt(a, b):
    return lax.dot_general(a, b, (((1,), (1,)), ((), ())), preferred_element_type=f32)


def _dot_tn(a, b):
    return lax.dot_general(a, b, (((0,), (0,)), ((), ())), preferred_element_type=f32)


def _dot_shards(a, w_ref):
    return jnp.concatenate([_dot(a, w_ref[p]) for p in range(w_ref.shape[0])], axis=1)


def _dot_nt_shards(d, w_ref):
    n = w_ref.shape[2]
    acc = _dot_nt(d[:, 0:n], w_ref[0])
    for p in range(1, w_ref.shape[0]):
        acc = acc + _dot_nt(d[:, p * n:(p + 1) * n], w_ref[p])
    return acc


def _column_shards_spec(k, n):
    return pl.BlockSpec((N_CHIPS, k, n // N_CHIPS), lambda i: (0, 0, 0))


def _call(body, *, name, grid, in_specs, out_specs, out_shape, scratch=(), cargo=None):
    if cargo is not None:
        return _cargo_call(body, cargo, name=name, grid=grid, in_specs=in_specs, out_specs=out_specs,
                           out_shape=out_shape, scratch=scratch)
    return pl.pallas_call(
        body, name=name, grid=grid, in_specs=in_specs, out_specs=out_specs, out_shape=out_shape,
        scratch_shapes=list(scratch),
        compiler_params=pltpu.CompilerParams(dimension_semantics=("arbitrary",) * len(grid),
                                             vmem_limit_bytes=VMEM_LIMIT))


def _rot_half(t):
    lane = lax.broadcasted_iota(jnp.int32, t.shape, 1)
    first = (lane % HEAD_DIM) < (HEAD_DIM // 2)
    return jnp.where(first, pltpu.roll(t, LANES - HEAD_DIM // 2, 1), pltpu.roll(t, HEAD_DIM // 2, 1))


def _rope(t, cos, sin_signed):
    return t * cos + _rot_half(t) * sin_signed


def _rope_bwd(d, cos, sin_signed):
    return d * cos + _rot_half(d * sin_signed)


def _gelu_parts(a):
    cdf = 0.5 * (1.0 + lax.erf(a * (2.0 ** -0.5)))
    pdf = jnp.exp(-0.5 * a * a) * ((2.0 * jnp.pi) ** -0.5)
    return a * cdf, cdf + a * pdf


def _layer_norm(h, g, b):
    mu = jnp.mean(h, axis=-1, keepdims=True)
    xc = h - mu
    var = jnp.mean(xc * xc, axis=-1, keepdims=True)
    rstd = lax.rsqrt(var + LN_EPS)
    xhat = xc * rstd
    return xhat * g + b, xhat, rstd


def _layer_norm_bwd(dy, xhat, rstd, g):
    dxh = dy * g
    m1 = jnp.mean(dxh, axis=-1, keepdims=True)
    m2 = jnp.mean(dxh * xhat, axis=-1, keepdims=True)
    return rstd * (dxh - m1 - xhat * m2)


def _inproj(x_bf, win_g, cos_t, sin_t, b_gate, cargo):
    s = x_bf.shape[0]
    tm, tn = 1024, 512

    def body(x_ref, w_ref, cos_ref, sin_ref, b_ref, o_ref):
        j = pl.program_id(1)
        acc = _dot(x_ref[...], w_ref[...])

        @pl.when(j < 2)
        def _():
            for c in range(tn // LANES):
                sl = slice(c * LANES, (c + 1) * LANES)
                o_ref[:, sl] = _rope(acc[:, sl], cos_ref[...], sin_ref[...])

        @pl.when((j >= 2) & (j < 4))
        def _():
            o_ref[...] = acc

        @pl.when(j >= 4)
        def _():
            o_ref[...] = jax.nn.sigmoid(acc + b_ref[...])

    (z,), carried = _cargo_call(
        body, cargo, name="inproj", grid=(s // tm, 4 * D_MODEL // tn),
        in_specs=[pl.BlockSpec((tm, D_MODEL), lambda i, j: (i, 0)),
                  pl.BlockSpec((None, D_MODEL, tn), lambda i, j: (j // 2, 0, j % 2)),
                  pl.BlockSpec((tm, LANES), lambda i, j: (i, 0)),
                  pl.BlockSpec((tm, LANES), lambda i, j: (i, 0)),
                  pl.BlockSpec((1, tn), lambda i, j: (0, jnp.maximum(j - 4, 0)))],
        out_specs=[pl.BlockSpec((tm, tn), lambda i, j: (i, j))],
        out_shape=[jax.ShapeDtypeStruct((s, 4 * D_MODEL), f32)])(x_bf, win_g, cos_t, sin_t, b_gate)
    return z, carried


STAT_ROWS = 8


def _pair_rows(v0, v1, fill):
    r = lax.broadcasted_iota(jnp.int32, (STAT_ROWS, v0.shape[1]), 0)
    return jnp.where(r == 0, v0, jnp.where(r == 1, v1, fill))


def _head_lanes(shape):
    lane = lax.broadcasted_iota(jnp.int32, shape, 1)
    return lane < HEAD_DIM, lane >= HEAD_DIM


def _head_rows(shape):
    row = lax.broadcasted_iota(jnp.int32, shape, 0)
    return row < HEAD_DIM, row >= HEAD_DIM


def _moba_select(q_bf, k_all, i, nb):
    k_mean = jnp.mean(k_all.reshape(nb, MOBA_BLOCK, LANES), axis=1)
    n_io = lax.broadcasted_iota(jnp.int32, (nb, MOBA_BLOCK), 0)
    past = n_io < i
    sels = []
    for head in _head_lanes((nb, LANES)):
        gate = _dot_nt(jnp.where(head, k_mean, 0.0).astype(bf16), q_bf)
        g = jnp.where(past, gate, NEG)
        rank = jnp.zeros((nb, MOBA_BLOCK), f32)
        for m in range(nb):
            gm = g[m:m + 1, :]
            rank = rank + jnp.where((gm > g) | ((gm == g) & (m < n_io)), 1.0, 0.0)
        sels.append(jnp.where(past & (rank < MOBA_TOPK), 1.0, 0.0))
    return sels


def _attn_fwd(z, nb, w_pool, pool_scale, cargo):
    s = z.shape[0]

    def body(q_ref, k_ref, v_ref, u_ref, wp_ref, sc_ref, o_ref, lse_ref, pooled_ref, mixed_ref, pm_ref, vt_s, pad):
        pair, i = pl.program_id(0), pl.program_id(1)

        @pl.when(i == 0)
        def _():
            for j in range(nb):
                vt_s[j] = v_ref[j * MOBA_BLOCK:(j + 1) * MOBA_BLOCK, :].T.astype(bf16)
            _pool_group(pair, u_ref, wp_ref, sc_ref, pooled_ref, mixed_ref, pm_ref, pad)

        k_heads = _head_lanes((MOBA_BLOCK, LANES))
        o_heads = _head_rows((LANES, MOBA_BLOCK))

        def block(j, qs, allow, m_old, l_old):
            kj, vtj = k_ref[j * MOBA_BLOCK:(j + 1) * MOBA_BLOCK, :], vt_s[j]
            m_new, alpha, l_new, pv = [], [], [], None
            for h in range(2):
                sc = jnp.where(allow[h], _dot_nt(jnp.where(k_heads[h], kj, 0.0).astype(bf16), qs), NEG)
                mx = jnp.max(sc, axis=0, keepdims=True)
                mn = mx if m_old is None else jnp.maximum(m_old[h], mx)
                p = jnp.exp(sc - mn)
                lsum = jnp.sum(p, axis=0, keepdims=True)
                if m_old is None:
                    a = None
                else:
                    a = jnp.exp(m_old[h] - mn)
                    lsum = a * l_old[h] + lsum
                t = _dot(jnp.where(o_heads[h], vtj, jnp.zeros_like(vtj)), p.astype(bf16))
                pv = t if pv is None else pv + t
                m_new.append(mn)
                alpha.append(a)
                l_new.append(lsum)
            return m_new, alpha, l_new, pv

        def query_block(c):
            q = q_ref[...]
            sels = _moba_select(q.astype(bf16), k_ref[...], c, nb)
            qs = (q * SCALE).astype(bf16)
            key = lax.broadcasted_iota(jnp.int32, (MOBA_BLOCK, MOBA_BLOCK), 0)
            qry = lax.broadcasted_iota(jnp.int32, (MOBA_BLOCK, MOBA_BLOCK), 1)
            m, _, l, acc = block(c, qs, [key <= qry] * 2, None, None)
            for j in range(c):
                m, alpha, l, pv = block(j, qs, [sels[h][j:j + 1, :] > 0.0 for h in range(2)], m, l)
                acc = acc * jnp.where(o_heads[0], alpha[0], alpha[1]) + pv
            o_ref[...] = (acc / jnp.where(o_heads[0], l[0], l[1])).T.astype(bf16)
            lse_ref[...] = _pair_rows(m[0] + jnp.log(l[0]), m[1] + jnp.log(l[1]), 0.0)

        for c in range(nb):
            pl.when(i == c)(functools.partial(query_block, c))

    group = pl.BlockSpec((s, POOL_GROUP), lambda hp, i: (0, hp))
    return _cargo_call(
        body, cargo, name="attn_fwd", grid=(D_ATTN // LANES, nb),
        in_specs=[pl.BlockSpec((MOBA_BLOCK, LANES), lambda hp, i: (i, hp)),
                  pl.BlockSpec((s, LANES), lambda hp, i: (0, 4 + hp)),
                  pl.BlockSpec((s, LANES), lambda hp, i: (0, 8 + hp)),
                  pl.BlockSpec((s, POOL_GROUP), lambda hp, i: (0, 12 + hp)),
                  pl.BlockSpec((None, POOL_GROUP, POOL_GROUP), lambda hp, i: (hp, 0, 0)),
                  pl.BlockSpec((1, POOL_GROUP), lambda hp, i: (0, hp))],
        out_specs=[pl.BlockSpec((MOBA_BLOCK, LANES), lambda hp, i: (i, hp)),
                   pl.BlockSpec((None, STAT_ROWS, MOBA_BLOCK), lambda hp, i: (hp, 0, i)), group, group, group],
        out_shape=[jax.ShapeDtypeStruct((s, D_ATTN), bf16),
                   jax.ShapeDtypeStruct((D_ATTN // LANES, STAT_ROWS, s), f32),
                   jax.ShapeDtypeStruct((s, D_POOL), bf16), jax.ShapeDtypeStruct((s, D_POOL), f32),
                   jax.ShapeDtypeStruct((s, D_POOL), bf16)],
        scratch=[pltpu.VMEM((nb, LANES, MOBA_BLOCK), bf16), pltpu.VMEM((s + MAX_WINDOW, POOL_GROUP), f32)],
    )(z, z, z, z, w_pool, pool_scale)


def _window_select(snaps, g):
    return jnp.where(g == 0, snaps[0], jnp.where(g == 1, snaps[1], jnp.where(g == 2, snaps[2], snaps[3])))


def _pool_group(g, u_ref, w_ref, sc_ref, pooled_ref, mixed_ref, pm_ref, pad):
    s = u_ref.shape[0]
    u = u_ref[...]
    pad[0:MAX_WINDOW, :] = jnp.zeros((MAX_WINDOW, POOL_GROUP), f32)
    pad[MAX_WINDOW:MAX_WINDOW + s, :] = u
    acc = u
    snaps = []
    for d in range(1, MAX_WINDOW):
        acc = acc + pad[MAX_WINDOW - d:MAX_WINDOW - d + s, :]
        if d + 1 in (2, 4, 8, 16):
            snaps.append(acc)
    win = _window_select(snaps, g)
    t = lax.broadcasted_iota(jnp.int32, (s, POOL_GROUP), 0)
    count = jnp.minimum(t + 1, jnp.left_shift(2, g)).astype(f32)
    pooled = (win / count - u).astype(bf16)
    mixed = _dot(pooled, w_ref[...].astype(bf16))
    pooled_ref[...] = pooled
    mixed_ref[...] = mixed
    pm_ref[...] = (mixed * sc_ref[...]).astype(bf16)


def _branch_merge(o_bf, pm_bf, z, wba, wbp):
    s = o_bf.shape[0]
    tm = 512

    def body(o_ref, pm_ref, ga_ref, gp_ref, wba_ref, wbp_ref, m_ref):
        ya = _dot_shards(o_ref[...], wba_ref)
        yp = _dot_shards(pm_ref[...], wbp_ref)
        m_ref[...] = (ga_ref[...] * ya + gp_ref[...] * yp).astype(bf16)

    return _call(
        body, name="branch_merge", grid=(s // tm,),
        in_specs=[pl.BlockSpec((tm, D_ATTN), lambda i: (i, 0)), pl.BlockSpec((tm, D_POOL), lambda i: (i, 0)),
                  pl.BlockSpec((tm, D_MODEL), lambda i: (i, 2)), pl.BlockSpec((tm, D_MODEL), lambda i: (i, 3)),
                  _column_shards_spec(D_ATTN, D_MODEL), _column_shards_spec(D_POOL, D_MODEL)],
        out_specs=pl.BlockSpec((tm, D_MODEL), lambda i: (i, 0)),
        out_shape=jax.ShapeDtypeStruct((s, D_MODEL), bf16))(o_bf, pm_bf, z, z, wba, wbp)


def _out_ln1(m_bf, wout, x, ln_g, ln_b):
    s = x.shape[0]
    tm = 512

    def body(m_ref, w_ref, x_ref, g_ref, b_ref, x1_ref, xhat_ref, rstd_ref):
        h = ALPHA * x_ref[...] + _dot(m_ref[...], w_ref[...])
        y, xhat, rstd = _layer_norm(h, g_ref[...], b_ref[...])
        x1_ref[...] = y.astype(bf16)
        xhat_ref[...] = xhat
        rstd_ref[...] = jnp.broadcast_to(rstd, (tm, LANES))

    row = pl.BlockSpec((tm, D_MODEL), lambda i: (i, 0))
    vec = pl.BlockSpec((1, D_MODEL), lambda i: (0, 0))
    return _call(
        body, name="out_ln1", grid=(s // tm,),
        in_specs=[row, pl.BlockSpec((D_MODEL, D_MODEL), lambda i: (0, 0)), row, vec, vec],
        out_specs=[row, row, pl.BlockSpec((tm, LANES), lambda i: (i, 0))],
        out_shape=[jax.ShapeDtypeStruct((s, D_MODEL), bf16), jax.ShapeDtypeStruct((s, D_MODEL), f32),
                   jax.ShapeDtypeStruct((s, LANES), f32)])(m_bf, wout, x, ln_g, ln_b)


FF_TILE = 256
FF_TILES_PER_SHARD = FF_PAD // FF_TILE
CONV_PAD = 8
CONV_ROWS = 16

def _ff_weight_spec(rows):
    return pl.BlockSpec((None, rows, FF_TILE), lambda j: (j // FF_TILES_PER_SHARD, 0, j % FF_TILES_PER_SHARD))


def _ff_rows_spec():
    return pl.BlockSpec((None, FF_TILE, D_MODEL), lambda j: (j // FF_TILES_PER_SHARD, j % FF_TILES_PER_SHARD, 0))


def _row_shifts(x, shifts):
    s = x.shape[0]
    padded = jnp.concatenate([x, jnp.zeros((CONV_PAD, x.shape[1]), x.dtype)], axis=0)
    return [pltpu.roll(padded, d % (s + CONV_PAD), 0)[0:s] for d in shifts]


def _ffn_up(x1_bf, wg_g, wu_g, convw_g, convb_g, cargo):
    s = x1_bf.shape[0]

    def body(x_ref, wg_ref, wu_ref, cw_ref, cb_ref, a_ref, up_ref, ac_ref, hh_ref):
        x1 = x_ref[...]
        a = _dot_nt(x1, wg_ref[...])
        up = _dot_nt(x1, wu_ref[...])
        a_ref[...] = a
        up_ref[...] = up
        a1, a2 = _row_shifts(a, (1, 2))
        ac = a * cw_ref[2:3, :] + a1 * cw_ref[1:2, :] + a2 * cw_ref[0:1, :] + cb_ref[...]
        ac_ref[...] = ac
        hg, _ = _gelu_parts(ac)
        hh_ref[...] = (hg * up).astype(bf16)

    col = pl.BlockSpec((s, FF_TILE), lambda j: (0, j))
    wide = jax.ShapeDtypeStruct((s, D_FF_PAD), f32)
    return _call(
        body, name="ffn_up", grid=(D_FF_PAD // FF_TILE,),
        in_specs=[pl.BlockSpec((s, D_MODEL), lambda j: (0, 0)), _ff_rows_spec(), _ff_rows_spec(),
                  _ff_weight_spec(8), _ff_weight_spec(1)],
        out_specs=[col, col, col, col],
        out_shape=[wide, wide, wide, jax.ShapeDtypeStruct((s, D_FF_PAD), bf16)],
        cargo=cargo)(x1_bf, wg_g, wu_g, convw_g, convb_g)


def _ffn_down_ln2_loss(hh_bf, wd, xhat1, ln1_g, ln1_b, ln2_g, ln2_b, target):
    s = hh_bf.shape[0]
    tm = 256

    def body(hh_ref, w_ref, xh1_ref, g1_ref, b1_ref, g2_ref, b2_ref, t_ref, loss_ref, dh_ref, dhb_ref, dg_ref, db_ref):
        i = pl.program_id(0)
        x1 = xh1_ref[...] * g1_ref[...] + b1_ref[...]
        h = ALPHA * x1 + _dot(hh_ref[...], w_ref[...])
        y, xhat, rstd = _layer_norm(h, g2_ref[...], b2_ref[...])
        err = y - t_ref[...]
        part = 0.5 * jnp.sum(jnp.mean(err * err, axis=-1, keepdims=True), axis=0, keepdims=True)
        dy = err * (1.0 / D_MODEL)

        @pl.when(i == 0)
        def _():
            loss_ref[...] = jnp.zeros_like(loss_ref)
            dg_ref[...] = jnp.zeros_like(dg_ref)
            db_ref[...] = jnp.zeros_like(db_ref)

        loss_ref[...] += jnp.broadcast_to(part, loss_ref.shape)
        dg_ref[...] += jnp.sum(dy * xhat, axis=0, keepdims=True)
        db_ref[...] += jnp.sum(dy, axis=0, keepdims=True)
        dh = _layer_norm_bwd(dy, xhat, rstd, g2_ref[...])
        dh_ref[...] = dh
        dhb_ref[...] = dh.astype(bf16)

    row = pl.BlockSpec((tm, D_MODEL), lambda i: (i, 0))
    vec = pl.BlockSpec((1, D_MODEL), lambda i: (0, 0))
    return _call(
        body, name="ffn_down_ln2_loss", grid=(s // tm,),
        in_specs=[pl.BlockSpec((tm, D_FF_PAD), lambda i: (i, 0)),
                  pl.BlockSpec((D_FF_PAD, D_MODEL), lambda i: (0, 0)),
                  row, vec, vec, vec, vec, row],
        out_specs=[pl.BlockSpec((8, LANES), lambda i: (0, 0)), row, row, vec, vec],
        out_shape=[jax.ShapeDtypeStruct((8, LANES), f32), jax.ShapeDtypeStruct((s, D_MODEL), f32),
                   jax.ShapeDtypeStruct((s, D_MODEL), bf16), jax.ShapeDtypeStruct((1, D_MODEL), f32),
                   jax.ShapeDtypeStruct((1, D_MODEL), f32)])(hh_bf, wd, xhat1, ln1_g, ln1_b, ln2_g, ln2_b, target)


def _ffn_act_bwd(dh2_bf, wd_g, ac, a, up, convw_g):
    s = dh2_bf.shape[0]

    def body(dh_ref, wd_ref, ac_ref, a_ref, up_ref, cw_ref, da_ref, dup_ref, dc_ref):
        dhh = _dot_nt(dh_ref[...], wd_ref[...])
        hg, dgelu = _gelu_parts(ac_ref[...])
        dup_ref[...] = (dhh * hg).astype(bf16)
        dac = dhh * up_ref[...] * dgelu
        dac1, dac2 = _row_shifts(dac, (-1, -2))
        a = a_ref[...]
        rows = [jnp.sum(d * a, axis=0, keepdims=True) for d in (dac2, dac1, dac)]
        rows.append(jnp.sum(dac, axis=0, keepdims=True))
        rows.append(jnp.zeros((CONV_ROWS - 4, FF_TILE), f32))
        dc_ref[...] = jnp.concatenate(rows, axis=0)
        da_ref[...] = (dac * cw_ref[2:3, :] + dac1 * cw_ref[1:2, :] + dac2 * cw_ref[0:1, :]).astype(bf16)

    col = pl.BlockSpec((s, FF_TILE), lambda j: (0, j))
    return _call(
        body, name="ffn_act_bwd", grid=(D_FF_PAD // FF_TILE,),
        in_specs=[pl.BlockSpec((s, D_MODEL), lambda j: (0, 0)), _ff_rows_spec(), col, col, col, _ff_weight_spec(8)],
        out_specs=[col, col, _ff_weight_spec(CONV_ROWS)],
        out_shape=[jax.ShapeDtypeStruct((s, D_FF_PAD), bf16), jax.ShapeDtypeStruct((s, D_FF_PAD), bf16),
                   jax.ShapeDtypeStruct((N_CHIPS, CONV_ROWS, FF_PAD), f32)],
    )(dh2_bf, wd_g, ac, a, up, convw_g)


def _matmul_tn(a, b, n_shards, name, tm=512, tn=1024, cargo=None):
    k, m = a.shape
    n = b.shape[1]
    tm, tn = min(tm, m), min(tn, n // n_shards)
    per = n // n_shards // tn

    def body(a_ref, b_ref, o_ref, at_s):
        @pl.when(pl.program_id(1) == 0)
        def _():
            at_s[...] = a_ref[...].T

        o_ref[...] = _dot(at_s[...], b_ref[...]).astype(bf16)

    res = _call(
        body, name=name, grid=(m // tm, n // tn),
        in_specs=[pl.BlockSpec((k, tm), lambda i, j: (0, i)), pl.BlockSpec((k, tn), lambda i, j: (0, j))],
        out_specs=[pl.BlockSpec((None, tm, tn), lambda i, j: (j // per, i, j % per))],
        out_shape=[jax.ShapeDtypeStruct((n_shards, m, n // n_shards), bf16)], scratch=[pltpu.VMEM((tm, k), bf16)],
        cargo=cargo)(a, b)
    return res[0] if cargo is None else (res[0][0], res[1])


def _dw_in(x_bf, dqkv, du, dzg, cargo):
    k, m = x_bf.shape
    tm, half = 512, D_ATTN

    def body(a_ref, dqkv_ref, du_ref, dzg_ref, o_ref, at_s):
        j = pl.program_id(1)

        @pl.when(j == 0)
        def _():
            at_s[...] = a_ref[...].T
            o_ref[:, 0:half] = _dot(at_s[...], dqkv_ref[0]).astype(bf16)
            o_ref[:, half:2 * half] = _dot(at_s[...], dqkv_ref[1]).astype(bf16)

        @pl.when(j == 1)
        def _():
            o_ref[:, 0:half] = _dot(at_s[...], dqkv_ref[2]).astype(bf16)
            o_ref[:, half:2 * half] = _dot(at_s[...], du_ref[...]).astype(bf16)

        @pl.when(j >= 2)
        def _():
            o_ref[...] = _dot(at_s[...], dzg_ref[...]).astype(bf16)

    (d_win,), carried = _cargo_call(
        body, cargo, name="dw_in", grid=(m // tm, N_CHIPS),
        in_specs=[pl.BlockSpec((k, tm), lambda i, j: (0, i)), pl.BlockSpec((3, k, half), lambda i, j: (0, 0, 0)),
                  pl.BlockSpec((k, half), lambda i, j: (0, 0)),
                  pl.BlockSpec((k, D_MODEL), lambda i, j: (0, jnp.maximum(j - 2, 0)))],
        out_specs=[pl.BlockSpec((None, tm, D_MODEL), lambda i, j: (j, i, 0))],
        out_shape=[jax.ShapeDtypeStruct((N_CHIPS, m, D_MODEL), bf16)], scratch=[pltpu.VMEM((tm, k), bf16)],
    )(x_bf, dqkv, du, dzg)
    return d_win, carried


def _ffn_in_bwd_ln1(da_bf, dup_bf, wg_g, wu_g, dh2, xhat1, rstd1, ln1_g, cargo):
    s = da_bf.shape[0]
    tm = 256

    def body(da_ref, dup_ref, wg_ref, wu_ref, dh2_ref, xh_ref, rstd_ref, g_ref, dh_ref, dhb_ref, dg_ref, db_ref):
        i = pl.program_id(0)
        dx1 = ALPHA * dh2_ref[...]
        for sh in range(N_CHIPS):
            sl = slice(sh * FF_PAD, (sh + 1) * FF_PAD)
            dx1 = dx1 + _dot(da_ref[:, sl], wg_ref[sh]) + _dot(dup_ref[:, sl], wu_ref[sh])
        xhat = xh_ref[...]

        @pl.when(i == 0)
        def _():
            dg_ref[...] = jnp.zeros_like(dg_ref)
            db_ref[...] = jnp.zeros_like(db_ref)

        dg_ref[...] += jnp.sum(dx1 * xhat, axis=0, keepdims=True)
        db_ref[...] += jnp.sum(dx1, axis=0, keepdims=True)
        dh = _layer_norm_bwd(dx1, xhat, rstd_ref[:, 0:1], g_ref[...])
        dh_ref[...] = dh
        dhb_ref[...] = dh.astype(bf16)

    row = pl.BlockSpec((tm, D_MODEL), lambda i: (i, 0))
    wide = pl.BlockSpec((tm, D_FF_PAD), lambda i: (i, 0))
    wfull = pl.BlockSpec((N_CHIPS, FF_PAD, D_MODEL), lambda i: (0, 0, 0))
    vec = pl.BlockSpec((1, D_MODEL), lambda i: (0, 0))
    return _call(
        body, name="ffn_in_bwd_ln1", grid=(s // tm,),
        in_specs=[wide, wide, wfull, wfull, row, row, pl.BlockSpec((tm, LANES), lambda i: (i, 0)), vec],
        out_specs=[row, row, vec, vec],
        out_shape=[jax.ShapeDtypeStruct((s, D_MODEL), f32), jax.ShapeDtypeStruct((s, D_MODEL), bf16),
                   jax.ShapeDtypeStruct((1, D_MODEL), f32), jax.ShapeDtypeStruct((1, D_MODEL), f32)],
        cargo=cargo)(da_bf, dup_bf, wg_g, wu_g, dh2, xhat1, rstd1, ln1_g)


def _merge_bwd(dh1_bf, wout, o_bf, pm_bf, z, wba, wbp):
    s = dh1_bf.shape[0]
    tm = 256

    def body(dh_ref, wout_ref, o_ref, pm_ref, ga_ref, gp_ref, wba_ref, wbp_ref,
             dzg_ref, dya_ref, dyp_ref, do_ref, dpm_ref, dbg_ref):
        i = pl.program_id(0)
        dm = _dot_nt(dh_ref[...], wout_ref[...])
        ya = _dot_shards(o_ref[...], wba_ref)
        yp = _dot_shards(pm_ref[...], wbp_ref)
        ga, gp = ga_ref[...], gp_ref[...]
        dza = dm * ya * ga * (1.0 - ga)
        dzp = dm * yp * gp * (1.0 - gp)

        @pl.when(i == 0)
        def _():
            dbg_ref[...] = jnp.zeros_like(dbg_ref)

        dbg_ref[:, 0:D_MODEL] += jnp.sum(dza, axis=0, keepdims=True)
        dbg_ref[:, D_MODEL:2 * D_MODEL] += jnp.sum(dzp, axis=0, keepdims=True)
        dzg_ref[:, 0:D_MODEL] = dza.astype(bf16)
        dzg_ref[:, D_MODEL:2 * D_MODEL] = dzp.astype(bf16)
        dya = (dm * ga).astype(bf16)
        dyp = (dm * gp).astype(bf16)
        dya_ref[...] = dya
        dyp_ref[...] = dyp
        do_ref[...] = _dot_nt_shards(dya, wba_ref).astype(bf16)
        dpm_ref[...] = _dot_nt_shards(dyp, wbp_ref)

    row = pl.BlockSpec((tm, D_MODEL), lambda i: (i, 0))
    half = pl.BlockSpec((tm, D_ATTN), lambda i: (i, 0))
    full = lambda r, c: pl.BlockSpec((r, c), lambda i: (0, 0))
    return _call(
        body, name="merge_bwd", grid=(s // tm,),
        in_specs=[row, full(D_MODEL, D_MODEL), half, half,
                  pl.BlockSpec((tm, D_MODEL), lambda i: (i, 2)), pl.BlockSpec((tm, D_MODEL), lambda i: (i, 3)),
                  _column_shards_spec(D_ATTN, D_MODEL), _column_shards_spec(D_POOL, D_MODEL)],
        out_specs=[pl.BlockSpec((tm, 2 * D_MODEL), lambda i: (i, 0)), row, row, half, half,
                   pl.BlockSpec((1, 2 * D_MODEL), lambda i: (0, 0))],
        out_shape=[jax.ShapeDtypeStruct((s, 2 * D_MODEL), bf16), jax.ShapeDtypeStruct((s, D_MODEL), bf16),
                   jax.ShapeDtypeStruct((s, D_MODEL), bf16), jax.ShapeDtypeStruct((s, D_ATTN), bf16),
                   jax.ShapeDtypeStruct((s, D_POOL), f32), jax.ShapeDtypeStruct((1, 2 * D_MODEL), f32)],
    )(dh1_bf, wout, o_bf, pm_bf, z, z, wba, wbp)


def _pool_bwd(dpm, mixed, pooled_bf, w_pool, pool_scale):
    s = dpm.shape[0]

    def body(dpm_ref, mixed_ref, pooled_ref, w_ref, sc_ref, du_ref, dw_ref, dsc_ref, pad):
        g = pl.program_id(0)
        dpm_v = dpm_ref[...]
        dsc_ref[...] = jnp.sum(dpm_v * mixed_ref[...], axis=0, keepdims=True)
        dmixed = (dpm_v * sc_ref[...]).astype(bf16)
        dw_ref[...] = _dot_tn(pooled_ref[...], dmixed)
        dpooled = _dot_nt(dmixed, w_ref[...].astype(bf16))
        t = lax.broadcasted_iota(jnp.int32, (s, POOL_GROUP), 0)
        count = jnp.minimum(t + 1, jnp.left_shift(2, g)).astype(f32)
        r = dpooled / count
        pad[0:s, :] = r
        pad[s:s + MAX_WINDOW, :] = jnp.zeros((MAX_WINDOW, POOL_GROUP), f32)
        acc = r
        snaps = []
        for d in range(1, MAX_WINDOW):
            acc = acc + pad[d:d + s, :]
            if d + 1 in (2, 4, 8, 16):
                snaps.append(acc)
        du_ref[...] = (_window_select(snaps, g) - dpooled).astype(bf16)

    blk = pl.BlockSpec((s, POOL_GROUP), lambda g: (0, g))
    return _call(
        body, name="pool_bwd", grid=(4,),
        in_specs=[blk, blk, blk, pl.BlockSpec((None, POOL_GROUP, POOL_GROUP), lambda g: (g, 0, 0)),
                  pl.BlockSpec((1, POOL_GROUP), lambda g: (0, g))],
        out_specs=[blk, pl.BlockSpec((None, POOL_GROUP, POOL_GROUP), lambda g: (g, 0, 0)),
                   pl.BlockSpec((1, POOL_GROUP), lambda g: (0, g))],
        out_shape=[jax.ShapeDtypeStruct((s, D_POOL), bf16), jax.ShapeDtypeStruct((4, POOL_GROUP, POOL_GROUP), f32),
                   jax.ShapeDtypeStruct((1, D_POOL), f32)],
        scratch=[pltpu.VMEM((s + MAX_WINDOW, POOL_GROUP), f32)])(dpm, mixed, pooled_bf, w_pool, pool_scale)


def _attn_bwd(z, o_bf, lse, do_bf, cos_t, sin_t, nb, cargo):
    s = z.shape[0]

    def body(q_ref, k_ref, v_ref, o_ref, lse_ref, do_ref, cq_ref, sq_ref, cf_ref, sf_ref,
             dqkv_ref, kt_s, dk_s, dv_s):
        i = pl.program_id(1)

        @pl.when(i == 0)
        def _():
            dk_s[...] = jnp.zeros_like(dk_s)
            dv_s[...] = jnp.zeros_like(dv_s)
            for j in range(nb):
                kt_s[j] = k_ref[j * MOBA_BLOCK:(j + 1) * MOBA_BLOCK, :].T.astype(bf16)

        k_heads = _head_lanes((MOBA_BLOCK, LANES))
        t_heads = _head_rows((LANES, MOBA_BLOCK))

        def query_block(c):
            q = q_ref[...]
            sels = _moba_select(q.astype(bf16), k_ref[...], c, nb)
            qs = (q * SCALE).astype(bf16)
            qraw = [jnp.where(hm, q, 0.0).astype(bf16) for hm in k_heads]
            do = do_ref[...].astype(f32)
            do_bf = do_ref[...]
            dob = [jnp.where(hm, do, 0.0).astype(bf16) for hm in k_heads]
            pick = _head_lanes((STAT_ROWS, LANES))
            r8 = lax.broadcasted_iota(jnp.int32, (STAT_ROWS, LANES), 0)
            head_pick = jnp.where(((r8 == 0) & pick[0]) | ((r8 == 1) & pick[1]), 1.0, 0.0)
            delta8 = lax.dot_general(head_pick, do * o_ref[...].astype(f32), (((1,), (1,)), ((), ())),
                                     precision=lax.Precision.HIGHEST, preferred_element_type=f32)
            lse8 = lse_ref[...]
            delta, lse_h = [delta8[0:1, :], delta8[1:2, :]], [lse8[0:1, :], lse8[1:2, :]]
            key = lax.broadcasted_iota(jnp.int32, (MOBA_BLOCK, MOBA_BLOCK), 0)
            qry = lax.broadcasted_iota(jnp.int32, (MOBA_BLOCK, MOBA_BLOCK), 1)

            def block(j, allow):
                rows = slice(j * MOBA_BLOCK, (j + 1) * MOBA_BLOCK)
                kj, vj, ktj = k_ref[rows, :], v_ref[rows, :], kt_s[j]
                dqt, dk, dv = None, None, None
                for h in range(2):
                    sc = _dot_nt(jnp.where(k_heads[h], kj, 0.0).astype(bf16), qs)
                    p = jnp.where(allow[h], jnp.exp(sc - lse_h[h]), 0.0)
                    dp = _dot_nt(jnp.where(k_heads[h], vj, 0.0).astype(bf16), do_bf)
                    ds = (p * (dp - delta[h]) * SCALE).astype(bf16)
                    tv = _dot(p.astype(bf16), dob[h])
                    tk = _dot(ds, qraw[h])
                    tq = _dot(jnp.where(t_heads[h], ktj, jnp.zeros_like(ktj)), ds)
                    dqt, dk, dv = (tq, tk, tv) if dqt is None else (dqt + tq, dk + tk, dv + tv)
                dk_s[rows, :] += dk
                dv_s[rows, :] += dv
                return dqt

            dqt = block(c, [key <= qry] * 2)
            for j in range(c):
                dqt = dqt + block(j, [sels[h][j:j + 1, :] > 0.0 for h in range(2)])
            dqkv_ref[0, c * MOBA_BLOCK:(c + 1) * MOBA_BLOCK, :] = _rope_bwd(dqt.T, cq_ref[...], sq_ref[...]).astype(bf16)

        for c in range(nb):
            pl.when(i == c)(functools.partial(query_block, c))

        @pl.when(i == nb - 1)
        def _():
            dqkv_ref[1] = _rope_bwd(dk_s[...], cf_ref[...], sf_ref[...]).astype(bf16)
            dqkv_ref[2] = dv_s[...].astype(bf16)

    qblk = pl.BlockSpec((MOBA_BLOCK, LANES), lambda hp, i: (i, hp))
    tq = pl.BlockSpec((MOBA_BLOCK, LANES), lambda hp, i: (i, 0))
    tf = pl.BlockSpec((s, LANES), lambda hp, i: (0, 0))
    return _cargo_call(
        body, cargo, name="attn_bwd", grid=(D_ATTN // LANES, nb),
        in_specs=[qblk, pl.BlockSpec((s, LANES), lambda hp, i: (0, 4 + hp)),
                  pl.BlockSpec((s, LANES), lambda hp, i: (0, 8 + hp)), qblk,
                  pl.BlockSpec((None, STAT_ROWS, MOBA_BLOCK), lambda hp, i: (hp, 0, i)), qblk, tq, tq, tf, tf],
        out_specs=[pl.BlockSpec((3, s, LANES), lambda hp, i: (0, 0, hp))],
        out_shape=[jax.ShapeDtypeStruct((3, s, D_ATTN), bf16)],
        scratch=[pltpu.VMEM((nb, LANES, MOBA_BLOCK), bf16), pltpu.VMEM((s, LANES), f32), pltpu.VMEM((s, LANES), f32)],
    )(z, z, z, o_bf, lse, do_bf, cos_t, sin_t, cos_t, sin_t)


def _inproj_dx(dqkv, du, dzg, win_g, dh1, more_items, place, cargo):
    s = du.shape[0]
    tm, half = 256, D_ATTN

    def compute(ins, outs, tile):
        dqkv_ref, du_ref, dzg_ref, w_ref, dh_ref = ins
        acc = ALPHA * dh_ref[...]
        for n, part in enumerate([dqkv_ref[0], dqkv_ref[1], dqkv_ref[2], du_ref[...]]):
            acc = acc + _dot_nt(part, w_ref[n // 2, :, (n % 2) * half:(n % 2 + 1) * half])
        for sh in range(2, N_CHIPS):
            acc = acc + _dot_nt(dzg_ref[:, (sh - 2) * D_MODEL:(sh - 1) * D_MODEL], w_ref[sh])
        outs[0][...] = acc

    row = lambda tile: pl.BlockSpec((tm, D_MODEL), lambda t, pc: (tile(t), 0))
    ins = lambda tile: [pl.BlockSpec((3, tm, half), lambda t, pc: (0, tile(t), 0)),
                        pl.BlockSpec((tm, half), lambda t, pc: (tile(t), 0)),
                        pl.BlockSpec((tm, 2 * D_MODEL), lambda t, pc: (tile(t), 0)),
                        pl.BlockSpec((N_CHIPS, D_MODEL, D_MODEL), lambda t, pc: (0, 0, 0)), row(tile)]
    main = (s // tm, ins, lambda tile: [row(tile)], [jax.ShapeDtypeStruct((s, D_MODEL), f32)],
            [dqkv, du, dzg, win_g, dh1], compute)
    outs, carried = _batched_call([main] + more_items, "inproj_dx", place, cargo)
    return (outs[0][0], [o[0] for o in outs[1:]]), carried


ANY = pl.BlockSpec(memory_space=pl.ANY)


def _chip_index():
    return 2 * lax.axis_index("x") + lax.axis_index("y")


def _peer(k):
    x, y, c = lax.axis_index("x"), lax.axis_index("y"), lax.axis_index("c")
    return (x ^ (k >> 1), y ^ (k & 1), c)


def _sibling():
    return (lax.axis_index("x"), lax.axis_index("y"), 1 - lax.axis_index("c"))


class _GatherCargo:
    aliased = ()

    def __init__(self, shards, pass_on):
        self.shards, self.pass_on, n = shards, pass_on, len(shards)
        self.inputs = [a.reshape(2, a.shape[0] // 2, a.shape[1]) for a in shards]
        self.out_shape = [jax.ShapeDtypeStruct((N_CHIPS,) + a.shape, a.dtype) for a in self.inputs]
        sem, one = pltpu.SemaphoreType.DMA((N_CHIPS - 1, n)), pltpu.SemaphoreType.DMA((n,))
        self.sems = [sem, sem, sem, sem, one, one]

    def _copies(self, src, dst, sems, a):
        send_sems, recv_sems, fsend_sems, frecv_sems, local_sems, own_sems = sems
        p, c = _chip_index(), lax.axis_index("c")
        copy = lambda s, d, **kw: functools.partial(pltpu.make_async_remote_copy, src_ref=s, dst_ref=d, **kw)
        own = copy(src[a], dst[a].at[p], send_sem=local_sems.at[a], recv_sem=own_sems.at[a], device_id=_sibling(),
                   device_id_type=MESH)
        out, arrive, onward, landed = [], [], [], []
        for k in range(1, N_CHIPS):
            ici = dict(send_sem=send_sems.at[k - 1, a], recv_sem=recv_sems.at[k - 1, a], device_id=_peer(k),
                       device_id_type=MESH)
            d2d = dict(send_sem=fsend_sems.at[k - 1, a], recv_sem=frecv_sems.at[k - 1, a], device_id=_sibling(),
                       device_id_type=MESH)
            got, theirs = dst[a].at[p ^ k, c], dst[a].at[p ^ k, 1 - c]
            out.append(copy(src[a].at[c], dst[a].at[p, c], **ici))
            arrive.append(copy(src[a].at[c], got, **ici))
            onward.append(copy(got, got, **d2d))
            landed.append(copy(theirs, theirs, **d2d))
        return own, out, arrive, onward, landed

    def stages(self, steps):
        n = len(self.shards)

        def start(src, dst, sems):
            for a in range(n):
                own, out, _, _, _ = self._copies(src, dst, sems, a)
                own().start()
                for cp in out:
                    cp().start()

        def pass_on(a):
            def act(src, dst, sems):
                _, _, arrive, onward, _ = self._copies(src, dst, sems, a)
                for k in range(N_CHIPS - 1):
                    arrive[k]().wait_recv()
                    onward[k]().start()
            return act

        def finish(src, dst, sems):
            for a in range(n):
                own, out, _, onward, landed = self._copies(src, dst, sems, a)
                for cp in landed:
                    cp().wait_recv()
                for cp in out + onward:
                    cp().wait_send()
                own().wait()

        mids = [(min(steps - 1, int(self.pass_on[a] * steps)), pass_on(a)) for a in range(n)]
        return [(0, start)] + mids + [(steps - 1, finish)]

    def results(self, outs):
        return [o.reshape((N_CHIPS,) + a.shape) for o, a in zip(outs, self.shards)]


class _ExchangeCargo:
    aliased = ()

    def __init__(self, sums, whole):
        self.inputs, self.whole, n = sums, whole, len(sums)
        self.out_shape = [jax.ShapeDtypeStruct((N_CHIPS - 1,) + g.shape[1:], g.dtype) for g in sums]
        sem = pltpu.SemaphoreType.DMA((N_CHIPS - 1, n))
        self.sems = [sem, sem]

    def _copies(self, src, dst, sems):
        send_sems, recv_sems = sems
        p = _chip_index()
        return [pltpu.make_async_remote_copy(
            src_ref=src[a].at[0] if self.whole[a] else src[a].at[p ^ k], dst_ref=dst[a].at[k - 1],
            send_sem=send_sems.at[k - 1, a], recv_sem=recv_sems.at[k - 1, a], device_id=_peer(k), device_id_type=MESH)
            for k in range(1, N_CHIPS) for a in range(len(self.inputs))]

    def stages(self, steps):
        def start(src, dst, sems):
            for cp in self._copies(src, dst, sems):
                cp.start()

        def finish(src, dst, sems):
            copies = self._copies(src, dst, sems)
            for cp in copies:
                cp.wait_recv()
            for cp in copies:
                cp.wait_send()

        return [(0, start), (steps - 1, finish)]

    def results(self, outs):
        return list(outs)


class _SwapCargo:
    aliased = ()

    def __init__(self, split):
        self.inputs, n = split, len(split)
        self.out_shape = [jax.ShapeDtypeStruct((g.shape[0],) + g.shape[2:], g.dtype) for g in split]
        sem = pltpu.SemaphoreType.DMA((n,))
        self.sems = [sem, sem]

    def _copies(self, src, dst, sems):
        c = lax.axis_index("c")
        return [pltpu.make_async_remote_copy(src_ref=src[a].at[:, 1 - c], dst_ref=dst[a], send_sem=sems[0].at[a],
                                             recv_sem=sems[1].at[a], device_id=_sibling(), device_id_type=MESH)
                for a in range(len(self.inputs))]

    def stages(self, steps):
        def start(src, dst, sems):
            for cp in self._copies(src, dst, sems):
                cp.start()

        def finish(src, dst, sems):
            copies = self._copies(src, dst, sems)
            for cp in copies:
                cp.wait_recv()
            for cp in copies:
                cp.wait_send()

        return [(0, start), (steps - 1, finish)]

    def results(self, outs):
        return list(outs)


class _FillCargo:
    def __init__(self, bufs):
        self.inputs, n = bufs, len(bufs)
        self.out_shape = [jax.ShapeDtypeStruct(b.shape, b.dtype) for b in bufs]
        self.aliased = [(a, a) for a in range(n)]
        sem = pltpu.SemaphoreType.DMA((n,))
        self.sems = [sem, sem]

    def _copies(self, src, dst, sems, landing):
        c = lax.axis_index("c")
        half = 1 - c if landing else c
        return [pltpu.make_async_remote_copy(src_ref=src[a].at[c], dst_ref=dst[a].at[half], send_sem=sems[0].at[a],
                                             recv_sem=sems[1].at[a], device_id=_sibling(), device_id_type=MESH)
                for a in range(len(self.inputs))]

    def stages(self, steps):
        def start(src, dst, sems):
            for cp in self._copies(src, dst, sems, False):
                cp.start()

        def finish(src, dst, sems):
            for cp in self._copies(src, dst, sems, True):
                cp.wait_recv()
            for cp in self._copies(src, dst, sems, False):
                cp.wait_send()

        return [(0, start), (steps - 1, finish)]

    def results(self, outs):
        return [o.reshape(2 * b.shape[1], b.shape[2]) for o, b in zip(outs, self.inputs)]


class _MultiCargo:
    def __init__(self, cargos):
        self.cargos = cargos
        self.inputs = [a for cg in cargos for a in cg.inputs]
        self.out_shape = [o for cg in cargos for o in cg.out_shape]
        self.sems = [s for cg in cargos for s in cg.sems]
        self.aliased, i0, o0 = [], 0, 0
        for cg in cargos:
            self.aliased += [(i0 + i, o0 + o) for i, o in cg.aliased]
            i0, o0 = i0 + len(cg.inputs), o0 + len(cg.out_shape)

    def _parts(self, refs, count):
        out, off = [], 0
        for cg in self.cargos:
            out.append(refs[off:off + count(cg)])
            off += count(cg)
        return out

    def stages(self, steps):
        merged = []
        for n, cg in enumerate(self.cargos):
            for at, act in cg.stages(steps):
                def part(src, dst, sems, n=n, act=act):
                    act(self._parts(src, lambda g: len(g.inputs))[n], self._parts(dst, lambda g: len(g.out_shape))[n],
                        self._parts(sems, lambda g: len(g.sems))[n])
                merged.append((at, part))
        starts = [m for m in merged if m[0] == 0]
        return starts + sorted([m for m in merged if m[0] != 0], key=lambda m: m[0])

    def results(self, outs):
        return [cg.results(part) for cg, part in zip(self.cargos, self._parts(list(outs), lambda g: len(g.out_shape)))]


def _cargo_call(body, cargo, *, name, grid, in_specs, out_specs, out_shape, scratch=(), prefetch=None):
    n_in, n_out, n_scr = len(in_specs), len(out_specs), len(scratch)
    c_in, c_out = len(cargo.inputs), len(cargo.out_shape)
    steps = 1
    for g in grid:
        steps *= g
    stages = cargo.stages(steps)

    def wrapped(*refs):
        head, refs = (refs[:1], refs[1:]) if prefetch is not None else ((), refs)
        ins, refs = refs[:n_in], refs[n_in:]
        cin, refs = refs[:c_in], refs[c_in:]
        outs, refs = refs[:n_out], refs[n_out:]
        cout, refs = refs[:c_out], refs[c_out:]
        scr, sems = refs[:n_scr], refs[n_scr:]
        if not grid:
            for _, act in stages:
                act(cin, cout, sems)
            return
        step = 0
        for d in range(len(grid)):
            step = step * grid[d] + pl.program_id(d)
        for at, act in stages:
            if at == 0:
                pl.when(step == 0)(functools.partial(act, cin, cout, sems))
        body(*head, *ins, *outs, *scr)
        for at, act in stages:
            if at > 0:
                pl.when(step == at)(functools.partial(act, cin, cout, sems))

    params = dict(vmem_limit_bytes=VMEM_LIMIT)
    if grid:
        params["dimension_semantics"] = ("arbitrary",) * len(grid)
    specs = dict(grid=grid, in_specs=list(in_specs) + [ANY] * c_in, out_specs=list(out_specs) + [ANY] * c_out,
                 scratch_shapes=list(scratch) + cargo.sems)
    if prefetch is not None:
        specs = dict(grid_spec=pltpu.PrefetchScalarGridSpec(num_scalar_prefetch=1, **specs))
    head = () if prefetch is None else (prefetch,)
    res = pl.pallas_call(wrapped, name=name, out_shape=list(out_shape) + cargo.out_shape,
                         input_output_aliases={len(head) + n_in + i: n_out + o for i, o in cargo.aliased},
                         compiler_params=pltpu.CompilerParams(**params), **specs)
    return lambda *args: (lambda r: (r[:n_out], cargo.results(r[n_out:])))(res(*head, *args, *cargo.inputs))


def _split_halves(g):
    return g.reshape(g.shape[0], 2, g.shape[1] // 2, g.shape[2])


def _ew_tile(rows):
    return rows if rows <= 384 else 256


def _batched_call(items, name, place=None, cargo=None):
    offs, total = [], 0
    for it in items:
        offs.append(total)
        total += it[0]
    in_specs, out_specs, out_shape, args, spans = [], [], [], [], []
    for (n, ins, outs, shapes, operands, _), off in zip(items, offs):
        tile = lambda t, off=off, n=n: jnp.clip(t - off, 0, n - 1)
        i_s, o_s = ins(tile), outs(tile)
        spans.append((len(i_s), len(o_s)))
        in_specs += i_s
        out_specs += o_s
        out_shape += shapes
        args += operands
    n_in = len(in_specs)

    def body(*refs):
        if place is not None:
            refs = refs[1:]
        t = pl.program_id(0)
        i0, o0 = 0, n_in
        for (n, _, _, _, _, compute), off, (ni, no) in zip(items, offs, spans):
            pl.when((t >= off) & (t < off + n))(functools.partial(compute, refs[i0:i0 + ni], refs[o0:o0 + no], t - off))
            i0, o0 = i0 + ni, o0 + no

    params = pltpu.CompilerParams(dimension_semantics=("arbitrary",), vmem_limit_bytes=VMEM_LIMIT)
    carried = None
    if cargo is not None:
        res, carried = _cargo_call(body, cargo, name=name, grid=(total,), in_specs=in_specs, out_specs=out_specs,
                                   out_shape=out_shape, prefetch=place)(*args)
    elif place is None:
        res = pl.pallas_call(body, name=name, grid=(total,), in_specs=in_specs, out_specs=out_specs,
                             out_shape=out_shape, compiler_params=params)(*args)
    else:
        res = pl.pallas_call(
            body, name=name, out_shape=out_shape, compiler_params=params,
            grid_spec=pltpu.PrefetchScalarGridSpec(num_scalar_prefetch=1, grid=(total,), in_specs=in_specs,
                                                   out_specs=out_specs))(place, *args)
    out, o0 = [], 0
    for _, no in spans:
        out.append(res[o0:o0 + no])
        o0 += no
    return out if cargo is None else (out, carried)


def _add_pairs(pairs, place, name):
    def item(mine, theirs):
        lead, _, rows, cols = mine.shape
        tm = _ew_tile(rows)

        def compute(ins, outs, tile):
            outs[0][...] = (ins[0][...].astype(f32) + ins[1][...].astype(f32)).astype(outs[0].dtype)

        blk = lambda tile: pl.BlockSpec((lead, tm, cols), lambda t, pc: (0, tile(t), 0))
        mine_half = lambda tile: pl.BlockSpec((lead, None, tm, cols), lambda t, pc: (0, pc[1], tile(t), 0))
        return (rows // tm, lambda tile: [mine_half(tile), blk(tile)], lambda tile: [blk(tile)],
                [jax.ShapeDtypeStruct(theirs.shape, theirs.dtype)], [mine, theirs], compute)

    return [out[0] for out in _batched_call([item(a, b) for a, b in pairs], name, place)]


def _cargo_alone(cargo, name):
    return _cargo_call(None, cargo, name=name, grid=(), in_specs=[], out_specs=[], out_shape=[])()[1]


def _bf16_casts(x, mats, padded, cargo, name):
    def item(a, to_rows, tm):
        rows, cols = a.shape
        full = rows // tm

        def compute(ins, outs, tile):
            outs[0][...] = jnp.where(tile < full, ins[0][...], 0.0).astype(bf16)

        return (to_rows // tm, lambda tile: [pl.BlockSpec((tm, cols), lambda t: (jnp.minimum(tile(t), full - 1), 0))],
                lambda tile: [pl.BlockSpec((tm, cols), lambda t: (tile(t), 0))],
                [jax.ShapeDtypeStruct((to_rows, cols), bf16)], [a], compute)

    items = [item(x, x.shape[0], 128)] + [item(w, r, 64 if r > w.shape[0] else 128) for w, r in zip(mats, padded)]
    outs, carried = _batched_call(items, name, cargo=cargo)
    return [o[0] for o in outs], carried


def _sum_slots_items(jobs, max_rows=None):
    def item(own, recv, whole):
        _, rows, cols = own.shape
        tm = _ew_tile(rows)
        while max_rows is not None and tm > max_rows and tm % 16 == 0:
            tm //= 2

        def compute(ins, outs, tile):
            r0, r1, r2 = [ins[1][k].astype(f32) for k in range(N_CHIPS - 1)]
            outs[0][...] = (ins[0][...].astype(f32) + r0) + (r1 + r2)

        ins = lambda tile: [pl.BlockSpec((None, tm, cols), lambda t, pc: (0 if whole else pc[0], tile(t), 0)),
                            pl.BlockSpec((N_CHIPS - 1, tm, cols), lambda t, pc: (0, tile(t), 0))]
        outs = lambda tile: [pl.BlockSpec((None, tm, cols), lambda t, pc: (pc[1], tile(t), 0))]
        return rows // tm, ins, outs, [jax.ShapeDtypeStruct((2, rows, cols), f32)], [own, recv], compute

    return [item(*job) for job in jobs]


class _GradReducer:
    def __init__(self, place):
        self.place, self.split, self.sums, self.slots, self.bufs, self.full = place, {}, {}, {}, {}, {}

    def swap(self, named):
        self.split.update({nm: _split_halves(g) for nm, g in named})
        return _SwapCargo([self.split[nm] for nm, _ in named])

    def summed(self, names, got, tag):
        sums = _add_pairs([(self.split[nm], g) for nm, g in zip(names, got)], self.place, "presum_" + tag)
        self.sums.update(zip(names, sums))

    def exchange(self, names):
        return _ExchangeCargo([self.sums[nm] for nm in names], [nm == "small" for nm in names])

    def arrived(self, names, slots):
        self.slots.update(zip(names, slots))

    def tree_sum_items(self, names, max_rows=None):
        return _sum_slots_items([(self.sums[nm], self.slots[nm], nm == "small") for nm in names], max_rows)

    def tree_summed(self, names, bufs):
        self.bufs.update(zip(names, bufs))

    def fill(self, names):
        return _FillCargo([self.bufs[nm] for nm in names])

    def filled(self, names, grads):
        self.full.update(zip(names, grads))

    def reduce(self, names):
        rest = [nm for nm in names if nm not in self.bufs]
        outs = _batched_call(self.tree_sum_items(rest), "sum_slots", self.place)
        self.tree_summed(rest, [out[0] for out in outs])
        rest = [nm for nm in names if nm not in self.full]
        self.filled(rest, _cargo_alone(self.fill(rest), "sibling_fill"))
        return [self.full[nm] for nm in names]


def _adamw_update(w, m, v, g):
    m = ADAM_B1 * m + (1.0 - ADAM_B1) * g
    v = ADAM_B2 * v + (1.0 - ADAM_B2) * jnp.square(g)
    m_hat = m / (1.0 - ADAM_B1 ** ADAM_STEP)
    v_hat = v / (1.0 - ADAM_B2 ** ADAM_STEP)
    return -ADAM_LR * (m_hat / (jnp.sqrt(v_hat) + ADAM_EPS) + ADAM_WD * w), m, v


def _adamw(jobs, name):
    def item(w, m, v, grad, tm):
        rows, cols = w.shape
        gcols = grad.shape[1]

        def compute(ins, outs, tile):
            g = ins[3][0:tm, 0:cols]
            outs[0][...] = g
            outs[1][...], outs[2][...], outs[3][...] = _adamw_update(ins[0][...], ins[1][...], ins[2][...], g)

        blk = lambda tile: pl.BlockSpec((tm, cols), lambda t: (tile(t), 0))
        if tm == rows:
            gspec = lambda tile: pl.BlockSpec(grad.shape, lambda t: (0, 0))
        else:
            gspec = lambda tile: pl.BlockSpec((tm, gcols), lambda t: (tile(t), 0))
        return (rows // tm, lambda tile: [blk(tile)] * 3 + [gspec(tile)], lambda tile: [blk(tile)] * 4,
                [jax.ShapeDtypeStruct((rows, cols), f32)] * 4, [w, m, v, grad], compute)

    return _batched_call([item(*job) for job in jobs], name)


def _rope_tables(s):
    half = HEAD_DIM // 2
    inv_freq = 1.0 / (10000.0 ** (jnp.arange(half, dtype=f32) / half))
    ang = jnp.arange(s, dtype=f32)[:, None] * inv_freq[None, :]
    cos, sin = jnp.cos(ang), jnp.sin(ang)
    return jnp.tile(cos, (1, LANES // half)), jnp.tile(jnp.concatenate([-sin, sin], axis=1), (1, LANES // HEAD_DIM))


def _local_step(x, x_bf, target, win_g, gather, comm, b_gate, w_pool, pool_scale, ln1_g, ln1_b, convb_g, ln2_g, ln2_b):
    s = x.shape[0]
    nb = s // MOBA_BLOCK
    cos_t, sin_t = _rope_tables(s)

    z, (wba_g, wbp_g, wout_g) = _inproj(x_bf, win_g, cos_t, sin_t, b_gate, gather["inproj"])
    wout = wout_g.reshape(D_MODEL, D_MODEL)
    (o_bf, lse, pooled_bf, mixed, pm_bf), (wg_g, wu_g, convw_g) = _attn_fwd(z, nb, w_pool, pool_scale,
                                                                           gather["attn_fwd"])
    m_bf = _branch_merge(o_bf, pm_bf, z, wba_g, wbp_g)
    x1_bf, xhat1, rstd1 = _out_ln1(m_bf, wout, x, ln1_g, ln1_b)
    (a, up, ac, hh_bf), (wd_g,) = _ffn_up(x1_bf, wg_g, wu_g, convw_g, convb_g, gather["ffn_up"])
    wd = wd_g.reshape(D_FF_PAD, D_MODEL)
    loss, dh2, dh2_bf, d_ln2_g, d_ln2_b = _ffn_down_ln2_loss(hh_bf, wd, xhat1, ln1_g, ln1_b, ln2_g, ln2_b, target)

    da_bf, dup_bf, dconv = _ffn_act_bwd(dh2_bf, wd_g, ac, a, up, convw_g)
    ffn_shape = (N_CHIPS, FF_PAD, D_MODEL)
    d_wd = _matmul_tn(hh_bf, dh2_bf, 1, "dw_ffn_down").reshape(ffn_shape)
    d_wg, got = _matmul_tn(da_bf, x1_bf, 1, "dw_ffn_gate", cargo=comm.swap([("w_ffn_down", d_wd)]))
    comm.summed(["w_ffn_down"], got, "ffn_down")
    d_wg = d_wg.reshape(ffn_shape)
    d_wu = _matmul_tn(dup_bf, x1_bf, 1, "dw_ffn_up").reshape(ffn_shape)
    gate_up = ["w_ffn_gate", "w_ffn_up", "conv_w"]
    (dh1, dh1_bf, d_ln1_g, d_ln1_b), (slots, got) = _ffn_in_bwd_ln1(
        da_bf, dup_bf, wg_g, wu_g, dh2, xhat1, rstd1, ln1_g,
        _MultiCargo([comm.exchange(["w_ffn_down"]), comm.swap(list(zip(gate_up, [d_wg, d_wu, dconv])))]))
    comm.arrived(["w_ffn_down"], slots)
    comm.summed(gate_up, got, "ffn_gate_up")
    d_wout = _matmul_tn(m_bf, dh1_bf, 1, "dw_out").reshape(N_CHIPS, D_MODEL // N_CHIPS, D_MODEL)
    dzg_bf, dya_bf, dyp_bf, do_bf, dpm, d_bgate = _merge_bwd(dh1_bf, wout, o_bf, pm_bf, z, wba_g, wbp_g)
    d_wba = _matmul_tn(o_bf, dya_bf, N_CHIPS, "dw_branch_attn")
    d_wbp = _matmul_tn(pm_bf, dyp_bf, N_CHIPS, "dw_branch_pool")
    du_bf, d_wpool, d_pscale = _pool_bwd(dpm, mixed, pooled_bf, w_pool, pool_scale)
    d_convb = dconv[:, 3, :FF_SHARD].reshape(1, N_CHIPS * FF_SHARD)
    small = _pack_small([d_bgate, d_wpool.reshape(-1, POOL_GROUP), d_pscale, d_ln1_g, d_ln1_b, d_ln2_g, d_ln2_b,
                         d_convb], loss)[None]
    branch = ["w_out", "w_branch_attn", "w_branch_pool", "small"]
    (dqkv_bf,), (slots, got) = _attn_bwd(
        z, o_bf, lse, do_bf, cos_t, sin_t, nb,
        _MultiCargo([comm.exchange(gate_up), comm.swap(list(zip(branch, [d_wout, d_wba, d_wbp, small])))]))
    comm.arrived(gate_up, slots)
    comm.summed(branch, got, "branch")
    d_win, slots = _dw_in(x_bf, dqkv_bf, du_bf, dzg_bf, comm.exchange(branch))
    comm.arrived(branch, slots)
    ffn = ["w_ffn_down"] + gate_up
    bufs, got = _batched_call(comm.tree_sum_items(ffn, 96), "swap_w_in", comm.place, comm.swap([("w_in", d_win)]))
    comm.tree_summed(ffn, [b[0] for b in bufs])
    comm.summed(["w_in"], got, "w_in")
    (grad_x, bufs), (slots, full) = _inproj_dx(
        dqkv_bf, du_bf, dzg_bf, win_g, dh1, comm.tree_sum_items(branch, 192), comm.place,
        _MultiCargo([comm.exchange(["w_in"]), comm.fill(ffn)]))
    comm.tree_summed(branch, bufs)
    comm.filled(ffn, full)
    comm.arrived(["w_in"], slots)
    return grad_x


SMALL_WEIGHTS = [("b_gate", 2 * D_MODEL), ("w_pool", 4 * POOL_GROUP * POOL_GROUP), ("pool_scale", D_POOL),
                 ("ln1_g", D_MODEL), ("ln1_b", D_MODEL), ("ln2_g", D_MODEL), ("ln2_b", D_MODEL),
                 ("conv_b", N_CHIPS * FF_SHARD)]
LOSS_ROW = 588
SMALL_ROWS = 592


def _small_pieces(ref, first_row, n):
    if ref.shape[0] > 1:
        return [(slice(first_row, first_row + n // LANES), (slice(None), slice(None)))]
    return [(slice(first_row + i, first_row + i + 1), (slice(None), slice(i * LANES, (i + 1) * LANES)))
            for i in range(n // LANES)]


def _small_first_rows():
    rows, r = [], 0
    for _, n in SMALL_WEIGHTS:
        rows.append(r)
        r += n // LANES
    return rows


def _pack_small(parts, loss):
    def body(*refs):
        out = refs[-1]
        out[...] = jnp.zeros_like(out)
        for src, r0, (_, n) in zip(refs, _small_first_rows(), SMALL_WEIGHTS):
            for rows, where in _small_pieces(src, r0, n):
                out[rows, :] = src[where]
        out[LOSS_ROW:LOSS_ROW + 1, :] = refs[len(parts)][0:1, :]

    return pl.pallas_call(body, name="pack_small", out_shape=jax.ShapeDtypeStruct((SMALL_ROWS, LANES), f32))(
        *parts, loss)


def _adamw_small(g_packed, g_conv, triples):
    n_in = 3 * len(triples)

    def body(g_ref, gc_ref, *refs):
        def update(k, g, where):
            w_ref, m_ref, v_ref = refs[3 * k:3 * k + 3]
            new = (g,) + _adamw_update(w_ref[where], m_ref[where], v_ref[where], g)
            for o_ref, val in zip(refs[n_in + 4 * k:n_in + 4 * k + 4], new):
                o_ref[where] = val

        for k, (r0, (_, n)) in enumerate(zip(_small_first_rows(), SMALL_WEIGHTS)):
            for rows, where in _small_pieces(refs[3 * k], r0, n):
                update(k, g_ref[rows, :], where)
        for tap in range(3):
            update(len(SMALL_WEIGHTS), gc_ref[tap:tap + 1, 0:FF_SHARD], (tap,))

    res = pl.pallas_call(
        body, name="adamw_small",
        out_shape=[jax.ShapeDtypeStruct(t[0].shape, f32) for t in triples for _ in range(4)],
    )(g_packed, g_conv, *[a for t in triples for a in t])
    return [res[4 * k:4 * k + 4] for k in range(len(triples))]


def _pad_conv_b(cb):
    return jnp.pad(cb.reshape(N_CHIPS, FF_SHARD), ((0, 0), (0, FF_PAD - FF_SHARD)))


def kernel(x, w_in, b_gate, w_branch_attn, w_pool, pool_scale, w_branch_pool, w_out, ln1_g, ln1_b, w_ffn_gate, w_ffn_up, conv_w, conv_b, w_ffn_down, ln2_g, ln2_b, loss_target, m_w_in, m_b_gate, m_w_branch_attn, m_w_pool, m_pool_scale, m_w_branch_pool, m_w_out, m_ln1_g, m_ln1_b, m_w_ffn_gate, m_w_ffn_up, m_conv_w, m_conv_b, m_w_ffn_down, m_ln2_g, m_ln2_b, v_w_in, v_b_gate, v_w_branch_attn, v_w_pool, v_pool_scale, v_w_branch_pool, v_w_out, v_ln1_g, v_ln1_b, v_w_ffn_gate, v_w_ffn_up, v_conv_w, v_conv_b, v_w_ffn_down, v_ln2_g, v_ln2_b):
    mats = [w_branch_attn[0], w_branch_pool[0], w_out[0], w_ffn_gate[0].T, w_ffn_up[0].T, w_ffn_down[0]]
    padded = [w.shape[0] for w in mats[:3]] + [FF_PAD] * 3
    (x_bf, *casted), (win_g,) = _bf16_casts(x[0], mats, padded, _GatherCargo([w_in[0].astype(bf16)], [0.9]),
                                            "gather_w_in")
    shards = [None] + casted + [jnp.pad(conv_w[0], ((0, CONV_ROWS - 3), (0, FF_PAD - FF_SHARD)))]
    gather = {"inproj": _GatherCargo(shards[1:4], [0.3, 0.5, 0.75]),
              "attn_fwd": _GatherCargo(shards[4:6] + shards[7:], [0.97, 0.97, 0.97]),
              "ffn_up": _GatherCargo(shards[6:7], [0.8])}

    place = jnp.stack([2 * lax.axis_index("x") + lax.axis_index("y"), lax.axis_index("c")]).astype(jnp.int32)
    comm = _GradReducer(place)
    convb_g = _pad_conv_b(conv_b).reshape(N_CHIPS, 1, FF_PAD)
    grad_x = _local_step(
        x[0], x_bf, loss_target[0], win_g, gather, comm, b_gate, w_pool[0], pool_scale, ln1_g, ln1_b, convb_g, ln2_g, ln2_b)
    names = ["w_in", "w_branch_attn", "w_branch_pool", "w_out", "w_ffn_gate", "w_ffn_up", "w_ffn_down", "conv_w", "small"]
    grads = comm.reduce(names)

    weights = [w_in[0], w_branch_attn[0], w_branch_pool[0], w_out[0], w_ffn_gate[0].T, w_ffn_up[0].T, w_ffn_down[0]]
    m_in = [m_w_in[0], m_w_branch_attn[0], m_w_branch_pool[0], m_w_out[0], m_w_ffn_gate[0].T, m_w_ffn_up[0].T,
            m_w_ffn_down[0]]
    v_in = [v_w_in[0], v_w_branch_attn[0], v_w_branch_pool[0], v_w_out[0], v_w_ffn_gate[0].T, v_w_ffn_up[0].T,
            v_w_ffn_down[0]]
    tiles = [256, 256, 256, 128, 176, 176, 176]
    jobs = list(zip(weights, m_in, v_in, grads, tiles))
    updated = _adamw(jobs[:4], "adamw_rest") + _adamw(jobs[4:], "adamw_ffn")
    res = {nm: [(r.T if nm in ("w_ffn_gate", "w_ffn_up") else r)[None] for r in outs4]
           for nm, outs4 in zip(names, updated)}

    small_w = [b_gate, w_pool, pool_scale, ln1_g, ln1_b, ln2_g, ln2_b, conv_b, conv_w]
    small_m = [m_b_gate, m_w_pool, m_pool_scale, m_ln1_g, m_ln1_b, m_ln2_g, m_ln2_b, m_conv_b, m_conv_w]
    small_v = [v_b_gate, v_w_pool, v_pool_scale, v_ln1_g, v_ln1_b, v_ln2_g, v_ln2_b, v_conv_b, v_conv_w]

    def as_kernel_sees(a):
        return a.reshape(-1, POOL_GROUP) if a.ndim == 4 else a.transpose(1, 0, 2) if a.ndim == 3 else a

    small_out = _adamw_small(grads[-1], grads[-2],
                             [tuple(as_kernel_sees(a) for a in t) for t in zip(small_w, small_m, small_v)])
    for nm, w, outs4 in zip([nm for nm, _ in SMALL_WEIGHTS] + ["conv_w"], small_w, small_out):
        res[nm] = [o.transpose(1, 0, 2) if w.ndim == 3 else o.reshape(w.shape) for o in outs4]
    loss = grads[-1][LOSS_ROW, 0]

    order = ["w_in", "b_gate", "w_branch_attn", "w_pool", "pool_scale", "w_branch_pool", "w_out", "ln1_g", "ln1_b",
             "w_ffn_gate", "w_ffn_up", "conv_w", "conv_b", "w_ffn_down", "ln2_g", "ln2_b"]
    outs = [loss, grad_x[None]]
    for kind in range(4):
        outs += [res[nm][kind] for nm in order]
    return tuple(outs)
```

```python
import functools

import jax
import jax.numpy as jnp
from jax import lax
from jax.experimental import pallas as pl
from jax.experimental.pallas import tpu as pltpu

D_MODEL = 1024
HEAD_DIM = 64
D_ATTN = 512
D_POOL = 512
MOBA_BLOCK = 256
MOBA_TOPK = 3
POOL_GROUP = 128
MAX_WINDOW = 16
FF_SHARD = 704
FF_PAD = 768
D_FF_PAD = 4 * FF_PAD
N_CHIPS = 4
LANES = 128
ALPHA = (2.0 * 1) ** 0.25
LN_EPS = 1e-5
NEG = -1e30
SCALE = HEAD_DIM ** -0.5
ADAM_LR, ADAM_B1, ADAM_B2, ADAM_EPS, ADAM_WD, ADAM_STEP = 0.001, 0.9, 0.999, 1e-08, 0.01, 10
VMEM_LIMIT = 56 * 1024 * 1024
MESH = pl.DeviceIdType.MESH

bf16 = jnp.bfloat16
f32 = jnp.float32


def _dot(a, b):
    return jnp.dot(a, b, preferred_element_type=f32)


def _dot_nt(a, b):
    return lax.dot_general(a, b, (((1,), (1,)), ((), ())), preferred_element_type=f32)


def _dot_tn(a, b):
    return lax.dot_general(a, b, (((0,), (0,)), ((), ())), preferred_element_type=f32)


def _dot_shards(a, w_ref):
    return jnp.concatenate([_dot(a, w_ref[p]) for p in range(w_ref.shape[0])], axis=1)


def _dot_nt_shards(d, w_ref):
    n = w_ref.shape[2]
    acc = _dot_nt(d[:, 0:n], w_ref[0])
    for p in range(1, w_ref.shape[0]):
        acc = acc + _dot_nt(d[:, p * n:(p + 1) * n], w_ref[p])
    return acc


def _column_shards_spec(k, n):
    return pl.BlockSpec((N_CHIPS, k, n // N_CHIPS), lambda i: (0, 0, 0))


def _call(body, *, name, grid, in_specs, out_specs, out_shape, scratch=(), cargo=None):
    if cargo is not None:
        return _cargo_call(body, cargo, name=name, grid=grid, in_specs=in_specs, out_specs=out_specs,
                           out_shape=out_shape, scratch=scratch)
    return pl.pallas_call(
        body, name=name, grid=grid, in_specs=in_specs, out_specs=out_specs, out_shape=out_shape,
        scratch_shapes=list(scratch),
        compiler_params=pltpu.CompilerParams(dimension_semantics=("arbitrary",) * len(grid),
                                             vmem_limit_bytes=VMEM_LIMIT))


def _rot_half(t):
    lane = lax.broadcasted_iota(jnp.int32, t.shape, 1)
    first = (lane % HEAD_DIM) < (HEAD_DIM // 2)
    return jnp.where(first, pltpu.roll(t, LANES - HEAD_DIM // 2, 1), pltpu.roll(t, HEAD_DIM // 2, 1))


def _rope(t, cos, sin_signed):
    return t * cos + _rot_half(t) * sin_signed


def _rope_bwd(d, cos, sin_signed):
    return d * cos + _rot_half(d * sin_signed)


def _gelu_parts(a):
    cdf = 0.5 * (1.0 + lax.erf(a * (2.0 ** -0.5)))
    pdf = jnp.exp(-0.5 * a * a) * ((2.0 * jnp.pi) ** -0.5)
    return a * cdf, cdf + a * pdf


def _layer_norm(h, g, b):
    mu = jnp.mean(h, axis=-1, keepdims=True)
    xc = h - mu
    var = jnp.mean(xc * xc, axis=-1, keepdims=True)
    rstd = lax.rsqrt(var + LN_EPS)
    xhat = xc * rstd
    return xhat * g + b, xhat, rstd


def _layer_norm_bwd(dy, xhat, rstd, g):
    dxh = dy * g
    m1 = jnp.mean(dxh, axis=-1, keepdims=True)
    m2 = jnp.mean(dxh * xhat, axis=-1, keepdims=True)
    return rstd * (dxh - m1 - xhat * m2)


def _inproj(x_bf, win_g, cos_t, sin_t, b_gate, cargo):
    s = x_bf.shape[0]
    tm, tn = 1024, 512

    def body(x_ref, w_ref, cos_ref, sin_ref, b_ref, o_ref):
        j = pl.program_id(1)
        acc = _dot(x_ref[...], w_ref[...])

        @pl.when(j < 2)
        def _():
            for c in range(tn // LANES):
                sl = slice(c * LANES, (c + 1) * LANES)
                o_ref[:, sl] = _rope(acc[:, sl], cos_ref[...], sin_ref[...])

        @pl.when((j >= 2) & (j < 4))
        def _():
            o_ref[...] = acc

        @pl.when(j >= 4)
        def _():
            o_ref[...] = jax.nn.sigmoid(acc + b_ref[...])

    (z,), carried = _cargo_call(
        body, cargo, name="inproj", grid=(s // tm, 4 * D_MODEL // tn),
        in_specs=[pl.BlockSpec((tm, D_MODEL), lambda i, j: (i, 0)),
                  pl.BlockSpec((None, D_MODEL, tn), lambda i, j: (j // 2, 0, j % 2)),
                  pl.BlockSpec((tm, LANES), lambda i, j: (i, 0)),
                  pl.BlockSpec((tm, LANES), lambda i, j: (i, 0)),
                  pl.BlockSpec((1, tn), lambda i, j: (0, jnp.maximum(j - 4, 0)))],
        out_specs=[pl.BlockSpec((tm, tn), lambda i, j: (i, j))],
        out_shape=[jax.ShapeDtypeStruct((s, 4 * D_MODEL), f32)])(x_bf, win_g, cos_t, sin_t, b_gate)
    return z, carried


STAT_ROWS = 8


def _pair_rows(v0, v1, fill):
    r = lax.broadcasted_iota(jnp.int32, (STAT_ROWS, v0.shape[1]), 0)
    return jnp.where(r == 0, v0, jnp.where(r == 1, v1, fill))


def _head_lanes(shape):
    lane = lax.broadcasted_iota(jnp.int32, shape, 1)
    return lane < HEAD_DIM, lane >= HEAD_DIM


def _head_rows(shape):
    row = lax.broadcasted_iota(jnp.int32, shape, 0)
    return row < HEAD_DIM, row >= HEAD_DIM


def _moba_select(q_bf, k_all, i, nb):
    k_mean = jnp.mean(k_all.reshape(nb, MOBA_BLOCK, LANES), axis=1)
    n_io = lax.broadcasted_iota(jnp.int32, (nb, MOBA_BLOCK), 0)
    past = n_io < i
    sels = []
    for head in _head_lanes((nb, LANES)):
        gate = _dot_nt(jnp.where(head, k_mean, 0.0).astype(bf16), q_bf)
        g = jnp.where(past, gate, NEG)
        rank = jnp.zeros((nb, MOBA_BLOCK), f32)
        for m in range(nb):
            gm = g[m:m + 1, :]
            rank = rank + jnp.where((gm > g) | ((gm == g) & (m < n_io)), 1.0, 0.0)
        sels.append(jnp.where(past & (rank < MOBA_TOPK), 1.0, 0.0))
    return sels


def _attn_fwd(z, nb, w_pool, pool_scale, cargo):
    s = z.shape[0]

    def body(q_ref, k_ref, v_ref, u_ref, wp_ref, sc_ref, o_ref, lse_ref, pooled_ref, mixed_ref, pm_ref, vt_s, pad):
        pair, i = pl.program_id(0), pl.program_id(1)

        @pl.when(i == 0)
        def _():
            for j in range(nb):
                vt_s[j] = v_ref[j * MOBA_BLOCK:(j + 1) * MOBA_BLOCK, :].T.astype(bf16)
            _pool_group(pair, u_ref, wp_ref, sc_ref, pooled_ref, mixed_ref, pm_ref, pad)

        k_heads = _head_lanes((MOBA_BLOCK, LANES))
        o_heads = _head_rows((LANES, MOBA_BLOCK))

        def block(j, qs, allow, m_old, l_old):
            kj, vtj = k_ref[j * MOBA_BLOCK:(j + 1) * MOBA_BLOCK, :], vt_s[j]
            m_new, alpha, l_new, pv = [], [], [], None
            for h in range(2):
                sc = jnp.where(allow[h], _dot_nt(jnp.where(k_heads[h], kj, 0.0).astype(bf16), qs), NEG)
                mx = jnp.max(sc, axis=0, keepdims=True)
                mn = mx if m_old is None else jnp.maximum(m_old[h], mx)
                p = jnp.exp(sc - mn)
                lsum = jnp.sum(p, axis=0, keepdims=True)
                if m_old is None:
                    a = None
                else:
                    a = jnp.exp(m_old[h] - mn)
                    lsum = a * l_old[h] + lsum
                t = _dot(jnp.where(o_heads[h], vtj, jnp.zeros_like(vtj)), p.astype(bf16))
                pv = t if pv is None else pv + t
                m_new.append(mn)
                alpha.append(a)
                l_new.append(lsum)
            return m_new, alpha, l_new, pv

        def query_block(c):
            q = q_ref[...]
            sels = _moba_select(q.astype(bf16), k_ref[...], c, nb)
            qs = (q * SCALE).astype(bf16)
            key = lax.broadcasted_iota(jnp.int32, (MOBA_BLOCK, MOBA_BLOCK), 0)
            qry = lax.broadcasted_iota(jnp.int32, (MOBA_BLOCK, MOBA_BLOCK), 1)
            m, _, l, acc = block(c, qs, [key <= qry] * 2, None, None)
            for j in range(c):
                m, alpha, l, pv = block(j, qs, [sels[h][j:j + 1, :] > 0.0 for h in range(2)], m, l)
                acc = acc * jnp.where(o_heads[0], alpha[0], alpha[1]) + pv
            o_ref[...] = (acc / jnp.where(o_heads[0], l[0], l[1])).T.astype(bf16)
            lse_ref[...] = _pair_rows(m[0] + jnp.log(l[0]), m[1] + jnp.log(l[1]), 0.0)

        for c in range(nb):
            pl.when(i == c)(functools.partial(query_block, c))

    group = pl.BlockSpec((s, POOL_GROUP), lambda hp, i: (0, hp))
    return _cargo_call(
        body, cargo, name="attn_fwd", grid=(D_ATTN // LANES, nb),
        in_specs=[pl.BlockSpec((MOBA_BLOCK, LANES), lambda hp, i: (i, hp)),
                  pl.BlockSpec((s, LANES), lambda hp, i: (0, 4 + hp)),
                  pl.BlockSpec((s, LANES), lambda hp, i: (0, 8 + hp)),
                  pl.BlockSpec((s, POOL_GROUP), lambda hp, i: (0, 12 + hp)),
                  pl.BlockSpec((None, POOL_GROUP, POOL_GROUP), lambda hp, i: (hp, 0, 0)),
                  pl.BlockSpec((1, POOL_GROUP), lambda hp, i: (0, hp))],
        out_specs=[pl.BlockSpec((MOBA_BLOCK, LANES), lambda hp, i: (i, hp)),
                   pl.BlockSpec((None, STAT_ROWS, MOBA_BLOCK), lambda hp, i: (hp, 0, i)), group, group, group],
        out_shape=[jax.ShapeDtypeStruct((s, D_ATTN), bf16),
                   jax.ShapeDtypeStruct((D_ATTN // LANES, STAT_ROWS, s), f32),
                   jax.ShapeDtypeStruct((s, D_POOL), bf16), jax.ShapeDtypeStruct((s, D_POOL), f32),
                   jax.ShapeDtypeStruct((s, D_POOL), bf16)],
        scratch=[pltpu.VMEM((nb, LANES, MOBA_BLOCK), bf16), pltpu.VMEM((s + MAX_WINDOW, POOL_GROUP), f32)],
    )(z, z, z, z, w_pool, pool_scale)


def _window_select(snaps, g):
    return jnp.where(g == 0, snaps[0], jnp.where(g == 1, snaps[1], jnp.where(g == 2, snaps[2], snaps[3])))


def _pool_group(g, u_ref, w_ref, sc_ref, pooled_ref, mixed_ref, pm_ref, pad):
    s = u_ref.shape[0]
    u = u_ref[...]
    pad[0:MAX_WINDOW, :] = jnp.zeros((MAX_WINDOW, POOL_GROUP), f32)
    pad[MAX_WINDOW:MAX_WINDOW + s, :] = u
    acc = u
    snaps = []
    for d in range(1, MAX_WINDOW):
        acc = acc + pad[MAX_WINDOW - d:MAX_WINDOW - d + s, :]
        if d + 1 in (2, 4, 8, 16):
            snaps.append(acc)
    win = _window_select(snaps, g)
    t = lax.broadcasted_iota(jnp.int32, (s, POOL_GROUP), 0)
    count = jnp.minimum(t + 1, jnp.left_shift(2, g)).astype(f32)
    pooled = (win / count - u).astype(bf16)
    mixed = _dot(pooled, w_ref[...].astype(bf16))
    pooled_ref[...] = pooled
    mixed_ref[...] = mixed
    pm_ref[...] = (mixed * sc_ref[...]).astype(bf16)


def _branch_merge(o_bf, pm_bf, z, wba, wbp):
    s = o_bf.shape[0]
    tm = 512

    def body(o_ref, pm_ref, ga_ref, gp_ref, wba_ref, wbp_ref, m_ref):
        ya = _dot_shards(o_ref[...], wba_ref)
        yp = _dot_shards(pm_ref[...], wbp_ref)
        m_ref[...] = (ga_ref[...] * ya + gp_ref[...] * yp).astype(bf16)

    return _call(
        body, name="branch_merge", grid=(s // tm,),
        in_specs=[pl.BlockSpec((tm, D_ATTN), lambda i: (i, 0)), pl.BlockSpec((tm, D_POOL), lambda i: (i, 0)),
                  pl.BlockSpec((tm, D_MODEL), lambda i: (i, 2)), pl.BlockSpec((tm, D_MODEL), lambda i: (i, 3)),
                  _column_shards_spec(D_ATTN, D_MODEL), _column_shards_spec(D_POOL, D_MODEL)],
        out_specs=pl.BlockSpec((tm, D_MODEL), lambda i: (i, 0)),
        out_shape=jax.ShapeDtypeStruct((s, D_MODEL), bf16))(o_bf, pm_bf, z, z, wba, wbp)


def _out_ln1(m_bf, wout, x, ln_g, ln_b):
    s = x.shape[0]
    tm = 512

    def body(m_ref, w_ref, x_ref, g_ref, b_ref, x1_ref, xhat_ref, rstd_ref):
        h = ALPHA * x_ref[...] + _dot(m_ref[...], w_ref[...])
        y, xhat, rstd = _layer_norm(h, g_ref[...], b_ref[...])
        x1_ref[...] = y.astype(bf16)
        xhat_ref[...] = xhat
        rstd_ref[...] = jnp.broadcast_to(rstd, (tm, LANES))

    row = pl.BlockSpec((tm, D_MODEL), lambda i: (i, 0))
    vec = pl.BlockSpec((1, D_MODEL), lambda i: (0, 0))
    return _call(
        body, name="out_ln1", grid=(s // tm,),
        in_specs=[row, pl.BlockSpec((D_MODEL, D_MODEL), lambda i: (0, 0)), row, vec, vec],
        out_specs=[row, row, pl.BlockSpec((tm, LANES), lambda i: (i, 0))],
        out_shape=[jax.ShapeDtypeStruct((s, D_MODEL), bf16), jax.ShapeDtypeStruct((s, D_MODEL), f32),
                   jax.ShapeDtypeStruct((s, LANES), f32)])(m_bf, wout, x, ln_g, ln_b)


FF_TILE = 256
FF_TILES_PER_SHARD = FF_PAD // FF_TILE
CONV_PAD = 8
CONV_ROWS = 16

def _ff_weight_spec(rows):
    return pl.BlockSpec((None, rows, FF_TILE), lambda j: (j // FF_TILES_PER_SHARD, 0, j % FF_TILES_PER_SHARD))


def _ff_rows_spec():
    return pl.BlockSpec((None, FF_TILE, D_MODEL), lambda j: (j // FF_TILES_PER_SHARD, j % FF_TILES_PER_SHARD, 0))


def _row_shifts(x, shifts):
    s = x.shape[0]
    padded = jnp.concatenate([x, jnp.zeros((CONV_PAD, x.shape[1]), x.dtype)], axis=0)
    return [pltpu.roll(padded, d % (s + CONV_PAD), 0)[0:s] for d in shifts]


def _ffn_up(x1_bf, wg_g, wu_g, convw_g, convb_g, cargo):
    s = x1_bf.shape[0]

    def body(x_ref, wg_ref, wu_ref, cw_ref, cb_ref, a_ref, up_ref, ac_ref, hh_ref):
        x1 = x_ref[...]
        a = _dot_nt(x1, wg_ref[...])
        up = _dot_nt(x1, wu_ref[...])
        a_ref[...] = a
        up_ref[...] = up
        a1, a2 = _row_shifts(a, (1, 2))
        ac = a * cw_ref[2:3, :] + a1 * cw_ref[1:2, :] + a2 * cw_ref[0:1, :] + cb_ref[...]
        ac_ref[...] = ac
        hg, _ = _gelu_parts(ac)
        hh_ref[...] = (hg * up).astype(bf16)

    col = pl.BlockSpec((s, FF_TILE), lambda j: (0, j))
    wide = jax.ShapeDtypeStruct((s, D_FF_PAD), f32)
    return _call(
        body, name="ffn_up", grid=(D_FF_PAD // FF_TILE,),
        in_specs=[pl.BlockSpec((s, D_MODEL), lambda j: (0, 0)), _ff_rows_spec(), _ff_rows_spec(),
                  _ff_weight_spec(8), _ff_weight_spec(1)],
        out_specs=[col, col, col, col],
        out_shape=[wide, wide, wide, jax.ShapeDtypeStruct((s, D_FF_PAD), bf16)],
        cargo=cargo)(x1_bf, wg_g, wu_g, convw_g, convb_g)


def _ffn_down_ln2_loss(hh_bf, wd, xhat1, ln1_g, ln1_b, ln2_g, ln2_b, target):
    s = hh_bf.shape[0]
    tm = 256

    def body(hh_ref, w_ref, xh1_ref, g1_ref, b1_ref, g2_ref, b2_ref, t_ref, loss_ref, dh_ref, dhb_ref, dg_ref, db_ref):
        i = pl.program_id(0)
        x1 = xh1_ref[...] * g1_ref[...] + b1_ref[...]
        h = ALPHA * x1 + _dot(hh_ref[...], w_ref[...])
        y, xhat, rstd = _layer_norm(h, g2_ref[...], b2_ref[...])
        err = y - t_ref[...]
        part = 0.5 * jnp.sum(jnp.mean(err * err, axis=-1, keepdims=True), axis=0, keepdims=True)
        dy = err * (1.0 / D_MODEL)

        @pl.when(i == 0)
        def _():
            loss_ref[...] = jnp.zeros_like(loss_ref)
            dg_ref[...] = jnp.zeros_like(dg_ref)
            db_ref[...] = jnp.zeros_like(db_ref)

        loss_ref[...] += jnp.broadcast_to(part, loss_ref.shape)
        dg_ref[...] += jnp.sum(dy * xhat, axis=0, keepdims=True)
        db_ref[...] += jnp.sum(dy, axis=0, keepdims=True)
        dh = _layer_norm_bwd(dy, xhat, rstd, g2_ref[...])
        dh_ref[...] = dh
        dhb_ref[...] = dh.astype(bf16)

    row = pl.BlockSpec((tm, D_MODEL), lambda i: (i, 0))
    vec = pl.BlockSpec((1, D_MODEL), lambda i: (0, 0))
    return _call(
        body, name="ffn_down_ln2_loss", grid=(s // tm,),
        in_specs=[pl.BlockSpec((tm, D_FF_PAD), lambda i: (i, 0)),
                  pl.BlockSpec((D_FF_PAD, D_MODEL), lambda i: (0, 0)),
                  row, vec, vec, vec, vec, row],
        out_specs=[pl.BlockSpec((8, LANES), lambda i: (0, 0)), row, row, vec, vec],
        out_shape=[jax.ShapeDtypeStruct((8, LANES), f32), jax.ShapeDtypeStruct((s, D_MODEL), f32),
                   jax.ShapeDtypeStruct((s, D_MODEL), bf16), jax.ShapeDtypeStruct((1, D_MODEL), f32),
                   jax.ShapeDtypeStruct((1, D_MODEL), f32)])(hh_bf, wd, xhat1, ln1_g, ln1_b, ln2_g, ln2_b, target)


def _ffn_act_bwd(dh2_bf, wd_g, ac, a, up, convw_g):
    s = dh2_bf.shape[0]

    def body(dh_ref, wd_ref, ac_ref, a_ref, up_ref, cw_ref, da_ref, dup_ref, dc_ref):
        dhh = _dot_nt(dh_ref[...], wd_ref[...])
        hg, dgelu = _gelu_parts(ac_ref[...])
        dup_ref[...] = (dhh * hg).astype(bf16)
        dac = dhh * up_ref[...] * dgelu
        dac1, dac2 = _row_shifts(dac, (-1, -2))
        a = a_ref[...]
        rows = [jnp.sum(d * a, axis=0, keepdims=True) for d in (dac2, dac1, dac)]
        rows.append(jnp.sum(dac, axis=0, keepdims=True))
        rows.append(jnp.zeros((CONV_ROWS - 4, FF_TILE), f32))
        dc_ref[...] = jnp.concatenate(rows, axis=0)
        da_ref[...] = (dac * cw_ref[2:3, :] + dac1 * cw_ref[1:2, :] + dac2 * cw_ref[0:1, :]).astype(bf16)

    col = pl.BlockSpec((s, FF_TILE), lambda j: (0, j))
    return _call(
        body, name="ffn_act_bwd", grid=(D_FF_PAD // FF_TILE,),
        in_specs=[pl.BlockSpec((s, D_MODEL), lambda j: (0, 0)), _ff_rows_spec(), col, col, col, _ff_weight_spec(8)],
        out_specs=[col, col, _ff_weight_spec(CONV_ROWS)],
        out_shape=[jax.ShapeDtypeStruct((s, D_FF_PAD), bf16), jax.ShapeDtypeStruct((s, D_FF_PAD), bf16),
                   jax.ShapeDtypeStruct((N_CHIPS, CONV_ROWS, FF_PAD), f32)],
    )(dh2_bf, wd_g, ac, a, up, convw_g)


def _matmul_tn(a, b, n_shards, name, tm=512, tn=1024, cargo=None):
    k, m = a.shape
    n = b.shape[1]
    tm, tn = min(tm, m), min(tn, n // n_shards)
    per = n // n_shards // tn

    def body(a_ref, b_ref, o_ref, at_s):
        @pl.when(pl.program_id(1) == 0)
        def _():
            at_s[...] = a_ref[...].T

        o_ref[...] = _dot(at_s[...], b_ref[...]).astype(bf16)

    res = _call(
        body, name=name, grid=(m // tm, n // tn),
        in_specs=[pl.BlockSpec((k, tm), lambda i, j: (0, i)), pl.BlockSpec((k, tn), lambda i, j: (0, j))],
        out_specs=[pl.BlockSpec((None, tm, tn), lambda i, j: (j // per, i, j % per))],
        out_shape=[jax.ShapeDtypeStruct((n_shards, m, n // n_shards), bf16)], scratch=[pltpu.VMEM((tm, k), bf16)],
        cargo=cargo)(a, b)
    return res[0] if cargo is None else (res[0][0], res[1])


def _dw_in(x_bf, dqkv, du, dzg, cargo):
    k, m = x_bf.shape
    tm, half = 512, D_ATTN

    def body(a_ref, dqkv_ref, du_ref, dzg_ref, o_ref, at_s):
        j = pl.program_id(1)

        @pl.when(j == 0)
        def _():
            at_s[...] = a_ref[...].T
            o_ref[:, 0:half] = _dot(at_s[...], dqkv_ref[0]).astype(bf16)
            o_ref[:, half:2 * half] = _dot(at_s[...], dqkv_ref[1]).astype(bf16)

        @pl.when(j == 1)
        def _():
            o_ref[:, 0:half] = _dot(at_s[...], dqkv_ref[2]).astype(bf16)
            o_ref[:, half:2 * half] = _dot(at_s[...], du_ref[...]).astype(bf16)

        @pl.when(j >= 2)
        def _():
            o_ref[...] = _dot(at_s[...], dzg_ref[...]).astype(bf16)

    (d_win,), carried = _cargo_call(
        body, cargo, name="dw_in", grid=(m // tm, N_CHIPS),
        in_specs=[pl.BlockSpec((k, tm), lambda i, j: (0, i)), pl.BlockSpec((3, k, half), lambda i, j: (0, 0, 0)),
                  pl.BlockSpec((k, half), lambda i, j: (0, 0)),
                  pl.BlockSpec((k, D_MODEL), lambda i, j: (0, jnp.maximum(j - 2, 0)))],
        out_specs=[pl.BlockSpec((None, tm, D_MODEL), lambda i, j: (j, i, 0))],
        out_shape=[jax.ShapeDtypeStruct((N_CHIPS, m, D_MODEL), bf16)], scratch=[pltpu.VMEM((tm, k), bf16)],
    )(x_bf, dqkv, du, dzg)
    return d_win, carried


def _ffn_in_bwd_ln1(da_bf, dup_bf, wg_g, wu_g, dh2, xhat1, rstd1, ln1_g, cargo):
    s = da_bf.shape[0]
    tm = 256

    def body(da_ref, dup_ref, wg_ref, wu_ref, dh2_ref, xh_ref, rstd_ref, g_ref, dh_ref, dhb_ref, dg_ref, db_ref):
        i = pl.program_id(0)
        dx1 = ALPHA * dh2_ref[...]
        for sh in range(N_CHIPS):
            sl = slice(sh * FF_PAD, (sh + 1) * FF_PAD)
            dx1 = dx1 + _dot(da_ref[:, sl], wg_ref[sh]) + _dot(dup_ref[:, sl], wu_ref[sh])
        xhat = xh_ref[...]

        @pl.when(i == 0)
        def _():
            dg_ref[...] = jnp.zeros_like(dg_ref)
            db_ref[...] = jnp.zeros_like(db_ref)

        dg_ref[...] += jnp.sum(dx1 * xhat, axis=0, keepdims=True)
        db_ref[...] += jnp.sum(dx1, axis=0, keepdims=True)
        dh = _layer_norm_bwd(dx1, xhat, rstd_ref[:, 0:1], g_ref[...])
        dh_ref[...] = dh
        dhb_ref[...] = dh.astype(bf16)

    row = pl.BlockSpec((tm, D_MODEL), lambda i: (i, 0))
    wide = pl.BlockSpec((tm, D_FF_PAD), lambda i: (i, 0))
    wfull = pl.BlockSpec((N_CHIPS, FF_PAD, D_MODEL), lambda i: (0, 0, 0))
    vec = pl.BlockSpec((1, D_MODEL), lambda i: (0, 0))
    return _call(
        body, name="ffn_in_bwd_ln1", grid=(s // tm,),
        in_specs=[wide, wide, wfull, wfull, row, row, pl.BlockSpec((tm, LANES), lambda i: (i, 0)), vec],
        out_specs=[row, row, vec, vec],
        out_shape=[jax.ShapeDtypeStruct((s, D_MODEL), f32), jax.ShapeDtypeStruct((s, D_MODEL), bf16),
                   jax.ShapeDtypeStruct((1, D_MODEL), f32), jax.ShapeDtypeStruct((1, D_MODEL), f32)],
        cargo=cargo)(da_bf, dup_bf, wg_g, wu_g, dh2, xhat1, rstd1, ln1_g)


def _merge_bwd(dh1_bf, wout, o_bf, pm_bf, z, wba, wbp):
    s = dh1_bf.shape[0]
    tm = 256

    def body(dh_ref, wout_ref, o_ref, pm_ref, ga_ref, gp_ref, wba_ref, wbp_ref,
             dzg_ref, dya_ref, dyp_ref, do_ref, dpm_ref, dbg_ref):
        i = pl.program_id(0)
        dm = _dot_nt(dh_ref[...], wout_ref[...])
        ya = _dot_shards(o_ref[...], wba_ref)
        yp = _dot_shards(pm_ref[...], wbp_ref)
        ga, gp = ga_ref[...], gp_ref[...]
        dza = dm * ya * ga * (1.0 - ga)
        dzp = dm * yp * gp * (1.0 - gp)

        @pl.when(i == 0)
        def _():
            dbg_ref[...] = jnp.zeros_like(dbg_ref)

        dbg_ref[:, 0:D_MODEL] += jnp.sum(dza, axis=0, keepdims=True)
        dbg_ref[:, D_MODEL:2 * D_MODEL] += jnp.sum(dzp, axis=0, keepdims=True)
        dzg_ref[:, 0:D_MODEL] = dza.astype(bf16)
        dzg_ref[:, D_MODEL:2 * D_MODEL] = dzp.astype(bf16)
        dya = (dm * ga).astype(bf16)
        dyp = (dm * gp).astype(bf16)
        dya_ref[...] = dya
        dyp_ref[...] = dyp
        do_ref[...] = _dot_nt_shards(dya, wba_ref).astype(bf16)
        dpm_ref[...] = _dot_nt_shards(dyp, wbp_ref)

    row = pl.BlockSpec((tm, D_MODEL), lambda i: (i, 0))
    half = pl.BlockSpec((tm, D_ATTN), lambda i: (i, 0))
    full = lambda r, c: pl.BlockSpec((r, c), lambda i: (0, 0))
    return _call(
        body, name="merge_bwd", grid=(s // tm,),
        in_specs=[row, full(D_MODEL, D_MODEL), half, half,
                  pl.BlockSpec((tm, D_MODEL), lambda i: (i, 2)), pl.BlockSpec((tm, D_MODEL), lambda i: (i, 3)),
                  _column_shards_spec(D_ATTN, D_MODEL), _column_shards_spec(D_POOL, D_MODEL)],
        out_specs=[pl.BlockSpec((tm, 2 * D_MODEL), lambda i: (i, 0)), row, row, half, half,
                   pl.BlockSpec((1, 2 * D_MODEL), lambda i: (0, 0))],
        out_shape=[jax.ShapeDtypeStruct((s, 2 * D_MODEL), bf16), jax.ShapeDtypeStruct((s, D_MODEL), bf16),
                   jax.ShapeDtypeStruct((s, D_MODEL), bf16), jax.ShapeDtypeStruct((s, D_ATTN), bf16),
                   jax.ShapeDtypeStruct((s, D_POOL), f32), jax.ShapeDtypeStruct((1, 2 * D_MODEL), f32)],
    )(dh1_bf, wout, o_bf, pm_bf, z, z, wba, wbp)


def _pool_bwd(dpm, mixed, pooled_bf, w_pool, pool_scale):
    s = dpm.shape[0]

    def body(dpm_ref, mixed_ref, pooled_ref, w_ref, sc_ref, du_ref, dw_ref, dsc_ref, pad):
        g = pl.program_id(0)
        dpm_v = dpm_ref[...]
        dsc_ref[...] = jnp.sum(dpm_v * mixed_ref[...], axis=0, keepdims=True)
        dmixed = (dpm_v * sc_ref[...]).astype(bf16)
        dw_ref[...] = _dot_tn(pooled_ref[...], dmixed)
        dpooled = _dot_nt(dmixed, w_ref[...].astype(bf16))
        t = lax.broadcasted_iota(jnp.int32, (s, POOL_GROUP), 0)
        count = jnp.minimum(t + 1, jnp.left_shift(2, g)).astype(f32)
        r = dpooled / count
        pad[0:s, :] = r
        pad[s:s + MAX_WINDOW, :] = jnp.zeros((MAX_WINDOW, POOL_GROUP), f32)
        acc = r
        snaps = []
        for d in range(1, MAX_WINDOW):
            acc = acc + pad[d:d + s, :]
            if d + 1 in (2, 4, 8, 16):
                snaps.append(acc)
        du_ref[...] = (_window_select(snaps, g) - dpooled).astype(bf16)

    blk = pl.BlockSpec((s, POOL_GROUP), lambda g: (0, g))
    return _call(
        body, name="pool_bwd", grid=(4,),
        in_specs=[blk, blk, blk, pl.BlockSpec((None, POOL_GROUP, POOL_GROUP), lambda g: (g, 0, 0)),
                  pl.BlockSpec((1, POOL_GROUP), lambda g: (0, g))],
        out_specs=[blk, pl.BlockSpec((None, POOL_GROUP, POOL_GROUP), lambda g: (g, 0, 0)),
                   pl.BlockSpec((1, POOL_GROUP), lambda g: (0, g))],
        out_shape=[jax.ShapeDtypeStruct((s, D_POOL), bf16), jax.ShapeDtypeStruct((4, POOL_GROUP, POOL_GROUP), f32),
                   jax.ShapeDtypeStruct((1, D_POOL), f32)],
        scratch=[pltpu.VMEM((s + MAX_WINDOW, POOL_GROUP), f32)])(dpm, mixed, pooled_bf, w_pool, pool_scale)


def _attn_bwd(z, o_bf, lse, do_bf, cos_t, sin_t, nb, cargo):
    s = z.shape[0]

    def body(q_ref, k_ref, v_ref, o_ref, lse_ref, do_ref, cq_ref, sq_ref, cf_ref, sf_ref,
             dqkv_ref, kt_s, dk_s, dv_s):
        i = pl.program_id(1)

        @pl.when(i == 0)
        def _():
            dk_s[...] = jnp.zeros_like(dk_s)
            dv_s[...] = jnp.zeros_like(dv_s)
            for j in range(nb):
                kt_s[j] = k_ref[j * MOBA_BLOCK:(j + 1) * MOBA_BLOCK, :].T.astype(bf16)

        k_heads = _head_lanes((MOBA_BLOCK, LANES))
        t_heads = _head_rows((LANES, MOBA_BLOCK))

        def query_block(c):
            q = q_ref[...]
            sels = _moba_select(q.astype(bf16), k_ref[...], c, nb)
            qs = (q * SCALE).astype(bf16)
            qraw = [jnp.where(hm, q, 0.0).astype(bf16) for hm in k_heads]
            do = do_ref[...].astype(f32)
            do_bf = do_ref[...]
            dob = [jnp.where(hm, do, 0.0).astype(bf16) for hm in k_heads]
            pick = _head_lanes((STAT_ROWS, LANES))
            r8 = lax.broadcasted_iota(jnp.int32, (STAT_ROWS, LANES), 0)
            head_pick = jnp.where(((r8 == 0) & pick[0]) | ((r8 == 1) & pick[1]), 1.0, 0.0)
            delta8 = lax.dot_general(head_pick, do * o_ref[...].astype(f32), (((1,), (1,)), ((), ())),
                                     precision=lax.Precision.HIGHEST, preferred_element_type=f32)
            lse8 = lse_ref[...]
            delta, lse_h = [delta8[0:1, :], delta8[1:2, :]], [lse8[0:1, :], lse8[1:2, :]]
            key = lax.broadcasted_iota(jnp.int32, (MOBA_BLOCK, MOBA_BLOCK), 0)
            qry = lax.broadcasted_iota(jnp.int32, (MOBA_BLOCK, MOBA_BLOCK), 1)

            def block(j, allow):
                rows = slice(j * MOBA_BLOCK, (j + 1) * MOBA_BLOCK)
                kj, vj, ktj = k_ref[rows, :], v_ref[rows, :], kt_s[j]
                dqt, dk, dv = None, None, None
                for h in range(2):
                    sc = _dot_nt(jnp.where(k_heads[h], kj, 0.0).astype(bf16), qs)
                    p = jnp.where(allow[h], jnp.exp(sc - lse_h[h]), 0.0)
                    dp = _dot_nt(jnp.where(k_heads[h], vj, 0.0).astype(bf16), do_bf)
                    ds = (p * (dp - delta[h]) * SCALE).astype(bf16)
                    tv = _dot(p.astype(bf16), dob[h])
                    tk = _dot(ds, qraw[h])
                    tq = _dot(jnp.where(t_heads[h], ktj, jnp.zeros_like(ktj)), ds)
                    dqt, dk, dv = (tq, tk, tv) if dqt is None else (dqt + tq, dk + tk, dv + tv)
                dk_s[rows, :] += dk
                dv_s[rows, :] += dv
                return dqt

            dqt = block(c, [key <= qry] * 2)
            for j in range(c):
                dqt = dqt + block(j, [sels[h][j:j + 1, :] > 0.0 for h in range(2)])
            dqkv_ref[0, c * MOBA_BLOCK:(c + 1) * MOBA_BLOCK, :] = _rope_bwd(dqt.T, cq_ref[...], sq_ref[...]).astype(bf16)

        for c in range(nb):
            pl.when(i == c)(functools.partial(query_block, c))

        @pl.when(i == nb - 1)
        def _():
            dqkv_ref[1] = _rope_bwd(dk_s[...], cf_ref[...], sf_ref[...]).astype(bf16)
            dqkv_ref[2] = dv_s[...].astype(bf16)

    qblk = pl.BlockSpec((MOBA_BLOCK, LANES), lambda hp, i: (i, hp))
    tq = pl.BlockSpec((MOBA_BLOCK, LANES), lambda hp, i: (i, 0))
    tf = pl.BlockSpec((s, LANES), lambda hp, i: (0, 0))
    return _cargo_call(
        body, cargo, name="attn_bwd", grid=(D_ATTN // LANES, nb),
        in_specs=[qblk, pl.BlockSpec((s, LANES), lambda hp, i: (0, 4 + hp)),
                  pl.BlockSpec((s, LANES), lambda hp, i: (0, 8 + hp)), qblk,
                  pl.BlockSpec((None, STAT_ROWS, MOBA_BLOCK), lambda hp, i: (hp, 0, i)), qblk, tq, tq, tf, tf],
        out_specs=[pl.BlockSpec((3, s, LANES), lambda hp, i: (0, 0, hp))],
        out_shape=[jax.ShapeDtypeStruct((3, s, D_ATTN), bf16)],
        scratch=[pltpu.VMEM((nb, LANES, MOBA_BLOCK), bf16), pltpu.VMEM((s, LANES), f32), pltpu.VMEM((s, LANES), f32)],
    )(z, z, z, o_bf, lse, do_bf, cos_t, sin_t, cos_t, sin_t)


def _inproj_dx(dqkv, du, dzg, win_g, dh1, more_items, place, cargo):
    s = du.shape[0]
    tm, half = 256, D_ATTN

    def compute(ins, outs, tile):
        dqkv_ref, du_ref, dzg_ref, w_ref, dh_ref = ins
        acc = ALPHA * dh_ref[...]
        for n, part in enumerate([dqkv_ref[0], dqkv_ref[1], dqkv_ref[2], du_ref[...]]):
            acc = acc + _dot_nt(part, w_ref[n // 2, :, (n % 2) * half:(n % 2 + 1) * half])
        for sh in range(2, N_CHIPS):
            acc = acc + _dot_nt(dzg_ref[:, (sh - 2) * D_MODEL:(sh - 1) * D_MODEL], w_ref[sh])
        outs[0][...] = acc

    row = lambda tile: pl.BlockSpec((tm, D_MODEL), lambda t, pc: (tile(t), 0))
    ins = lambda tile: [pl.BlockSpec((3, tm, half), lambda t, pc: (0, tile(t), 0)),
                        pl.BlockSpec((tm, half), lambda t, pc: (tile(t), 0)),
                        pl.BlockSpec((tm, 2 * D_MODEL), lambda t, pc: (tile(t), 0)),
                        pl.BlockSpec((N_CHIPS, D_MODEL, D_MODEL), lambda t, pc: (0, 0, 0)), row(tile)]
    main = (s // tm, ins, lambda tile: [row(tile)], [jax.ShapeDtypeStruct((s, D_MODEL), f32)],
            [dqkv, du, dzg, win_g, dh1], compute)
    outs, carried = _batched_call([main] + more_items, "inproj_dx", place, cargo)
    return (outs[0][0], [o[0] for o in outs[1:]]), carried


ANY = pl.BlockSpec(memory_space=pl.ANY)


def _chip_index():
    return 2 * lax.axis_index("x") + lax.axis_index("y")


def _peer(k):
    x, y, c = lax.axis_index("x"), lax.axis_index("y"), lax.axis_index("c")
    return (x ^ (k >> 1), y ^ (k & 1), c)


def _sibling():
    return (lax.axis_index("x"), lax.axis_index("y"), 1 - lax.axis_index("c"))


class _GatherCargo:
    aliased = ()

    def __init__(self, shards, pass_on):
        self.shards, self.pass_on, n = shards, pass_on, len(shards)
        self.inputs = [a.reshape(2, a.shape[0] // 2, a.shape[1]) for a in shards]
        self.out_shape = [jax.ShapeDtypeStruct((N_CHIPS,) + a.shape, a.dtype) for a in self.inputs]
        two, three, one = [pltpu.SemaphoreType.DMA(shp) for shp in ((2, n), (N_CHIPS - 1, n), (n,))]
        self.sems = [two, two, two, two, three, three, one, one]

    def _relay_rows(self, a):
        h = self.inputs[a].shape[1]
        cut = h // 2 if h % 32 == 0 else h
        return {1: (0, cut), 2: (cut, h)}

    def _copies(self, src, dst, sems, a):
        send_sems, recv_sems, rsend_sems, rrecv_sems, fsend_sems, frecv_sems, local_sems, own_sems = sems
        p, c = _chip_index(), lax.axis_index("c")
        copy = lambda s, d, **kw: functools.partial(pltpu.make_async_remote_copy, src_ref=s, dst_ref=d, **kw)
        own = copy(src[a], dst[a].at[p], send_sem=local_sems.at[a], recv_sem=own_sems.at[a], device_id=_sibling(),
                   device_id_type=MESH)
        out, arrive, relay, relayed, onward, landed = {}, {}, {}, {}, {}, {}
        for k in range(1, N_CHIPS):
            d2d = dict(send_sem=fsend_sems.at[k - 1, a], recv_sem=frecv_sems.at[k - 1, a], device_id=_sibling(),
                       device_id_type=MESH)
            got, theirs = dst[a].at[p ^ k, c], dst[a].at[p ^ k, 1 - c]
            onward[k] = copy(got, got, **d2d)
            landed[k] = copy(theirs, theirs, **d2d)
        for k, (lo, hi) in self._relay_rows(a).items():
            ici = dict(send_sem=send_sems.at[k - 1, a], recv_sem=recv_sems.at[k - 1, a], device_id=_peer(k),
                       device_id_type=MESH)
            out[k] = copy(src[a].at[c], dst[a].at[p, c], **ici)
            arrive[k] = copy(src[a].at[c], dst[a].at[p ^ k, c], **ici)
            if hi > lo:
                via = dict(send_sem=rsend_sems.at[k - 1, a], recv_sem=rrecv_sems.at[k - 1, a], device_id=_peer(3 - k),
                           device_id_type=MESH)
                rows = dst[a].at[p ^ k, c, lo:hi]
                relay[k] = copy(rows, rows, **via)
                mine = dst[a].at[p ^ 3, c, lo:hi]
                relayed[k] = copy(mine, mine, **via)
        return own, out, arrive, relay, relayed, onward, landed

    def stages(self, steps):
        n = len(self.shards)

        def start(src, dst, sems):
            for a in range(n):
                own, out, _, _, _, _, _ = self._copies(src, dst, sems, a)
                own().start()
                for cp in out.values():
                    cp().start()

        def pass_on(a):
            def act(src, dst, sems):
                _, _, arrive, relay, _, onward, _ = self._copies(src, dst, sems, a)
                for k in arrive:
                    arrive[k]().wait_recv()
                    if k in relay:
                        relay[k]().start()
                    onward[k]().start()
            return act

        def finish(src, dst, sems):
            for a in range(n):
                _, _, _, _, relayed, onward, _ = self._copies(src, dst, sems, a)
                for cp in relayed.values():
                    cp().wait_recv()
                onward[3]().start()
            for a in range(n):
                own, out, _, relay, _, onward, landed = self._copies(src, dst, sems, a)
                for cp in landed.values():
                    cp().wait_recv()
                for cp in list(out.values()) + list(relay.values()) + list(onward.values()):
                    cp().wait_send()
                own().wait()

        mids = [(min(steps - 1, int(self.pass_on[a] * steps)), pass_on(a)) for a in range(n)]
        return [(0, start)] + mids + [(steps - 1, finish)]

    def results(self, outs):
        return [o.reshape((N_CHIPS,) + a.shape) for o, a in zip(outs, self.shards)]


class _ExchangeCargo:
    aliased = ()

    def __init__(self, sums, whole):
        self.inputs, self.whole, n = sums, whole, len(sums)
        self.out_shape = [jax.ShapeDtypeStruct((N_CHIPS - 1,) + g.shape[1:], g.dtype) for g in sums]
        sem = pltpu.SemaphoreType.DMA((N_CHIPS - 1, n))
        self.sems = [sem, sem]

    def _copies(self, src, dst, sems):
        send_sems, recv_sems = sems
        p = _chip_index()
        return [pltpu.make_async_remote_copy(
            src_ref=src[a].at[0] if self.whole[a] else src[a].at[p ^ k], dst_ref=dst[a].at[k - 1],
            send_sem=send_sems.at[k - 1, a], recv_sem=recv_sems.at[k - 1, a], device_id=_peer(k), device_id_type=MESH)
            for k in range(1, N_CHIPS) for a in range(len(self.inputs))]

    def stages(self, steps):
        def start(src, dst, sems):
            for cp in self._copies(src, dst, sems):
                cp.start()

        def finish(src, dst, sems):
            copies = self._copies(src, dst, sems)
            for cp in copies:
                cp.wait_recv()
            for cp in copies:
                cp.wait_send()

        return [(0, start), (steps - 1, finish)]

    def results(self, outs):
        return list(outs)


class _SwapCargo:
    aliased = ()

    def __init__(self, split):
        self.inputs, n = split, len(split)
        self.out_shape = [jax.ShapeDtypeStruct((g.shape[0],) + g.shape[2:], g.dtype) for g in split]
        sem = pltpu.SemaphoreType.DMA((n,))
        self.sems = [sem, sem]

    def _copies(self, src, dst, sems):
        c = lax.axis_index("c")
        return [pltpu.make_async_remote_copy(src_ref=src[a].at[:, 1 - c], dst_ref=dst[a], send_sem=sems[0].at[a],
                                             recv_sem=sems[1].at[a], device_id=_sibling(), device_id_type=MESH)
                for a in range(len(self.inputs))]

    def stages(self, steps):
        def start(src, dst, sems):
            for cp in self._copies(src, dst, sems):
                cp.start()

        def finish(src, dst, sems):
            copies = self._copies(src, dst, sems)
            for cp in copies:
                cp.wait_recv()
            for cp in copies:
                cp.wait_send()

        return [(0, start), (steps - 1, finish)]

    def results(self, outs):
        return list(outs)


class _FillCargo:
    def __init__(self, bufs):
        self.inputs, n = bufs, len(bufs)
        self.out_shape = [jax.ShapeDtypeStruct(b.shape, b.dtype) for b in bufs]
        self.aliased = [(a, a) for a in range(n)]
        sem = pltpu.SemaphoreType.DMA((n,))
        self.sems = [sem, sem]

    def _copies(self, src, dst, sems, landing):
        c = lax.axis_index("c")
        half = 1 - c if landing else c
        return [pltpu.make_async_remote_copy(src_ref=src[a].at[c], dst_ref=dst[a].at[half], send_sem=sems[0].at[a],
                                             recv_sem=sems[1].at[a], device_id=_sibling(), device_id_type=MESH)
                for a in range(len(self.inputs))]

    def stages(self, steps):
        def start(src, dst, sems):
            for cp in self._copies(src, dst, sems, False):
                cp.start()

        def finish(src, dst, sems):
            for cp in self._copies(src, dst, sems, True):
                cp.wait_recv()
            for cp in self._copies(src, dst, sems, False):
                cp.wait_send()

        return [(0, start), (steps - 1, finish)]

    def results(self, outs):
        return [o.reshape(2 * b.shape[1], b.shape[2]) for o, b in zip(outs, self.inputs)]


class _MultiCargo:
    def __init__(self, cargos):
        self.cargos = cargos
        self.inputs = [a for cg in cargos for a in cg.inputs]
        self.out_shape = [o for cg in cargos for o in cg.out_shape]
        self.sems = [s for cg in cargos for s in cg.sems]
        self.aliased, i0, o0 = [], 0, 0
        for cg in cargos:
            self.aliased += [(i0 + i, o0 + o) for i, o in cg.aliased]
            i0, o0 = i0 + len(cg.inputs), o0 + len(cg.out_shape)

    def _parts(self, refs, count):
        out, off = [], 0
        for cg in self.cargos:
            out.append(refs[off:off + count(cg)])
            off += count(cg)
        return out

    def stages(self, steps):
        merged = []
        for n, cg in enumerate(self.cargos):
            for at, act in cg.stages(steps):
                def part(src, dst, sems, n=n, act=act):
                    act(self._parts(src, lambda g: len(g.inputs))[n], self._parts(dst, lambda g: len(g.out_shape))[n],
                        self._parts(sems, lambda g: len(g.sems))[n])
                merged.append((at, part))
        starts = [m for m in merged if m[0] == 0]
        return starts + sorted([m for m in merged if m[0] != 0], key=lambda m: m[0])

    def results(self, outs):
        return [cg.results(part) for cg, part in zip(self.cargos, self._parts(list(outs), lambda g: len(g.out_shape)))]


def _cargo_call(body, cargo, *, name, grid, in_specs, out_specs, out_shape, scratch=(), prefetch=None):
    n_in, n_out, n_scr = len(in_specs), len(out_specs), len(scratch)
    c_in, c_out = len(cargo.inputs), len(cargo.out_shape)
    steps = 1
    for g in grid:
        steps *= g
    stages = cargo.stages(steps)

    def wrapped(*refs):
        head, refs = (refs[:1], refs[1:]) if prefetch is not None else ((), refs)
        ins, refs = refs[:n_in], refs[n_in:]
        cin, refs = refs[:c_in], refs[c_in:]
        outs, refs = refs[:n_out], refs[n_out:]
        cout, refs = refs[:c_out], refs[c_out:]
        scr, sems = refs[:n_scr], refs[n_scr:]
        if not grid:
            for _, act in stages:
                act(cin, cout, sems)
            return
        step = 0
        for d in range(len(grid)):
            step = step * grid[d] + pl.program_id(d)
        for at, act in stages:
            if at == 0:
                pl.when(step == 0)(functools.partial(act, cin, cout, sems))
        body(*head, *ins, *outs, *scr)
        for at, act in stages:
            if at > 0:
                pl.when(step == at)(functools.partial(act, cin, cout, sems))

    params = dict(vmem_limit_bytes=VMEM_LIMIT)
    if grid:
        params["dimension_semantics"] = ("arbitrary",) * len(grid)
    specs = dict(grid=grid, in_specs=list(in_specs) + [ANY] * c_in, out_specs=list(out_specs) + [ANY] * c_out,
                 scratch_shapes=list(scratch) + cargo.sems)
    if prefetch is not None:
        specs = dict(grid_spec=pltpu.PrefetchScalarGridSpec(num_scalar_prefetch=1, **specs))
    head = () if prefetch is None else (prefetch,)
    res = pl.pallas_call(wrapped, name=name, out_shape=list(out_shape) + cargo.out_shape,
                         input_output_aliases={len(head) + n_in + i: n_out + o for i, o in cargo.aliased},
                         compiler_params=pltpu.CompilerParams(**params), **specs)
    return lambda *args: (lambda r: (r[:n_out], cargo.results(r[n_out:])))(res(*head, *args, *cargo.inputs))


def _split_halves(g):
    return g.reshape(g.shape[0], 2, g.shape[1] // 2, g.shape[2])


def _ew_tile(rows):
    return rows if rows <= 384 else 256


def _batched_call(items, name, place=None, cargo=None):
    offs, total = [], 0
    for it in items:
        offs.append(total)
        total += it[0]
    in_specs, out_specs, out_shape, args, spans = [], [], [], [], []
    for (n, ins, outs, shapes, operands, _), off in zip(items, offs):
        tile = lambda t, off=off, n=n: jnp.clip(t - off, 0, n - 1)
        i_s, o_s = ins(tile), outs(tile)
        spans.append((len(i_s), len(o_s)))
        in_specs += i_s
        out_specs += o_s
        out_shape += shapes
        args += operands
    n_in = len(in_specs)

    def body(*refs):
        if place is not None:
            refs = refs[1:]
        t = pl.program_id(0)
        i0, o0 = 0, n_in
        for (n, _, _, _, _, compute), off, (ni, no) in zip(items, offs, spans):
            pl.when((t >= off) & (t < off + n))(functools.partial(compute, refs[i0:i0 + ni], refs[o0:o0 + no], t - off))
            i0, o0 = i0 + ni, o0 + no

    params = pltpu.CompilerParams(dimension_semantics=("arbitrary",), vmem_limit_bytes=VMEM_LIMIT)
    carried = None
    if cargo is not None:
        res, carried = _cargo_call(body, cargo, name=name, grid=(total,), in_specs=in_specs, out_specs=out_specs,
                                   out_shape=out_shape, prefetch=place)(*args)
    elif place is None:
        res = pl.pallas_call(body, name=name, grid=(total,), in_specs=in_specs, out_specs=out_specs,
                             out_shape=out_shape, compiler_params=params)(*args)
    else:
        res = pl.pallas_call(
            body, name=name, out_shape=out_shape, compiler_params=params,
            grid_spec=pltpu.PrefetchScalarGridSpec(num_scalar_prefetch=1, grid=(total,), in_specs=in_specs,
                                                   out_specs=out_specs))(place, *args)
    out, o0 = [], 0
    for _, no in spans:
        out.append(res[o0:o0 + no])
        o0 += no
    return out if cargo is None else (out, carried)


def _add_pairs(pairs, place, name):
    def item(mine, theirs):
        lead, _, rows, cols = mine.shape
        tm = _ew_tile(rows)

        def compute(ins, outs, tile):
            outs[0][...] = (ins[0][...].astype(f32) + ins[1][...].astype(f32)).astype(outs[0].dtype)

        blk = lambda tile: pl.BlockSpec((lead, tm, cols), lambda t, pc: (0, tile(t), 0))
        mine_half = lambda tile: pl.BlockSpec((lead, None, tm, cols), lambda t, pc: (0, pc[1], tile(t), 0))
        return (rows // tm, lambda tile: [mine_half(tile), blk(tile)], lambda tile: [blk(tile)],
                [jax.ShapeDtypeStruct(theirs.shape, theirs.dtype)], [mine, theirs], compute)

    return [out[0] for out in _batched_call([item(a, b) for a, b in pairs], name, place)]


def _cargo_alone(cargo, name):
    return _cargo_call(None, cargo, name=name, grid=(), in_specs=[], out_specs=[], out_shape=[])()[1]


def _bf16_casts(x, mats, padded, cargo, name):
    def item(a, to_rows, tm):
        rows, cols = a.shape
        full = rows // tm

        def compute(ins, outs, tile):
            outs[0][...] = jnp.where(tile < full, ins[0][...], 0.0).astype(bf16)

        return (to_rows // tm, lambda tile: [pl.BlockSpec((tm, cols), lambda t: (jnp.minimum(tile(t), full - 1), 0))],
                lambda tile: [pl.BlockSpec((tm, cols), lambda t: (tile(t), 0))],
                [jax.ShapeDtypeStruct((to_rows, cols), bf16)], [a], compute)

    items = [item(x, x.shape[0], 128)] + [item(w, r, 64 if r > w.shape[0] else 128) for w, r in zip(mats, padded)]
    outs, carried = _batched_call(items, name, cargo=cargo)
    return [o[0] for o in outs], carried


def _sum_slots_items(jobs, max_rows=None):
    def item(own, recv, whole):
        _, rows, cols = own.shape
        tm = _ew_tile(rows)
        while max_rows is not None and tm > max_rows and tm % 16 == 0:
            tm //= 2

        def compute(ins, outs, tile):
            r0, r1, r2 = [ins[1][k].astype(f32) for k in range(N_CHIPS - 1)]
            outs[0][...] = (ins[0][...].astype(f32) + r0) + (r1 + r2)

        ins = lambda tile: [pl.BlockSpec((None, tm, cols), lambda t, pc: (0 if whole else pc[0], tile(t), 0)),
                            pl.BlockSpec((N_CHIPS - 1, tm, cols), lambda t, pc: (0, tile(t), 0))]
        outs = lambda tile: [pl.BlockSpec((None, tm, cols), lambda t, pc: (pc[1], tile(t), 0))]
        return rows // tm, ins, outs, [jax.ShapeDtypeStruct((2, rows, cols), f32)], [own, recv], compute

    return [item(*job) for job in jobs]


class _GradReducer:
    def __init__(self, place):
        self.place, self.split, self.sums, self.slots, self.bufs, self.full = place, {}, {}, {}, {}, {}

    def swap(self, named):
        self.split.update({nm: _split_halves(g) for nm, g in named})
        return _SwapCargo([self.split[nm] for nm, _ in named])

    def summed(self, names, got, tag):
        sums = _add_pairs([(self.split[nm], g) for nm, g in zip(names, got)], self.place, "presum_" + tag)
        self.sums.update(zip(names, sums))

    def exchange(self, names):
        return _ExchangeCargo([self.sums[nm] for nm in names], [nm == "small" for nm in names])

    def arrived(self, names, slots):
        self.slots.update(zip(names, slots))

    def tree_sum_items(self, names, max_rows=None):
        return _sum_slots_items([(self.sums[nm], self.slots[nm], nm == "small") for nm in names], max_rows)

    def tree_summed(self, names, bufs):
        self.bufs.update(zip(names, bufs))

    def fill(self, names):
        return _FillCargo([self.bufs[nm] for nm in names])

    def filled(self, names, grads):
        self.full.update(zip(names, grads))

    def reduce(self, names):
        rest = [nm for nm in names if nm not in self.bufs]
        outs = _batched_call(self.tree_sum_items(rest), "sum_slots", self.place)
        self.tree_summed(rest, [out[0] for out in outs])
        rest = [nm for nm in names if nm not in self.full]
        self.filled(rest, _cargo_alone(self.fill(rest), "sibling_fill"))
        return [self.full[nm] for nm in names]


def _adamw_update(w, m, v, g):
    m = ADAM_B1 * m + (1.0 - ADAM_B1) * g
    v = ADAM_B2 * v + (1.0 - ADAM_B2) * jnp.square(g)
    m_hat = m / (1.0 - ADAM_B1 ** ADAM_STEP)
    v_hat = v / (1.0 - ADAM_B2 ** ADAM_STEP)
    return -ADAM_LR * (m_hat / (jnp.sqrt(v_hat) + ADAM_EPS) + ADAM_WD * w), m, v


def _adamw(jobs, name):
    def item(w, m, v, grad, tm):
        rows, cols = w.shape
        gcols = grad.shape[1]

        def compute(ins, outs, tile):
            g = ins[3][0:tm, 0:cols]
            outs[0][...] = g
            outs[1][...], outs[2][...], outs[3][...] = _adamw_update(ins[0][...], ins[1][...], ins[2][...], g)

        blk = lambda tile: pl.BlockSpec((tm, cols), lambda t: (tile(t), 0))
        if tm == rows:
            gspec = lambda tile: pl.BlockSpec(grad.shape, lambda t: (0, 0))
        else:
            gspec = lambda tile: pl.BlockSpec((tm, gcols), lambda t: (tile(t), 0))
        return (rows // tm, lambda tile: [blk(tile)] * 3 + [gspec(tile)], lambda tile: [blk(tile)] * 4,
                [jax.ShapeDtypeStruct((rows, cols), f32)] * 4, [w, m, v, grad], compute)

    return _batched_call([item(*job) for job in jobs], name)


def _rope_tables(s):
    half = HEAD_DIM // 2
    inv_freq = 1.0 / (10000.0 ** (jnp.arange(half, dtype=f32) / half))
    ang = jnp.arange(s, dtype=f32)[:, None] * inv_freq[None, :]
    cos, sin = jnp.cos(ang), jnp.sin(ang)
    return jnp.tile(cos, (1, LANES // half)), jnp.tile(jnp.concatenate([-sin, sin], axis=1), (1, LANES // HEAD_DIM))


def _local_step(x, x_bf, target, win_g, gather, comm, b_gate, w_pool, pool_scale, ln1_g, ln1_b, convb_g, ln2_g, ln2_b):
    s = x.shape[0]
    nb = s // MOBA_BLOCK
    cos_t, sin_t = _rope_tables(s)

    z, (wba_g, wbp_g, wout_g) = _inproj(x_bf, win_g, cos_t, sin_t, b_gate, gather["inproj"])
    wout = wout_g.reshape(D_MODEL, D_MODEL)
    (o_bf, lse, pooled_bf, mixed, pm_bf), (wg_g, wu_g, convw_g) = _attn_fwd(z, nb, w_pool, pool_scale,
                                                                           gather["attn_fwd"])
    m_bf = _branch_merge(o_bf, pm_bf, z, wba_g, wbp_g)
    x1_bf, xhat1, rstd1 = _out_ln1(m_bf, wout, x, ln1_g, ln1_b)
    (a, up, ac, hh_bf), (wd_g,) = _ffn_up(x1_bf, wg_g, wu_g, convw_g, convb_g, gather["ffn_up"])
    wd = wd_g.reshape(D_FF_PAD, D_MODEL)
    loss, dh2, dh2_bf, d_ln2_g, d_ln2_b = _ffn_down_ln2_loss(hh_bf, wd, xhat1, ln1_g, ln1_b, ln2_g, ln2_b, target)

    da_bf, dup_bf, dconv = _ffn_act_bwd(dh2_bf, wd_g, ac, a, up, convw_g)
    ffn_shape = (N_CHIPS, FF_PAD, D_MODEL)
    d_wd = _matmul_tn(hh_bf, dh2_bf, 1, "dw_ffn_down").reshape(ffn_shape)
    d_wg, got = _matmul_tn(da_bf, x1_bf, 1, "dw_ffn_gate", cargo=comm.swap([("w_ffn_down", d_wd)]))
    comm.summed(["w_ffn_down"], got, "ffn_down")
    d_wg = d_wg.reshape(ffn_shape)
    d_wu = _matmul_tn(dup_bf, x1_bf, 1, "dw_ffn_up").reshape(ffn_shape)
    gate_up = ["w_ffn_gate", "w_ffn_up", "conv_w"]
    (dh1, dh1_bf, d_ln1_g, d_ln1_b), (slots, got) = _ffn_in_bwd_ln1(
        da_bf, dup_bf, wg_g, wu_g, dh2, xhat1, rstd1, ln1_g,
        _MultiCargo([comm.exchange(["w_ffn_down"]), comm.swap(list(zip(gate_up, [d_wg, d_wu, dconv])))]))
    comm.arrived(["w_ffn_down"], slots)
    comm.summed(gate_up, got, "ffn_gate_up")
    d_wout = _matmul_tn(m_bf, dh1_bf, 1, "dw_out").reshape(N_CHIPS, D_MODEL // N_CHIPS, D_MODEL)
    dzg_bf, dya_bf, dyp_bf, do_bf, dpm, d_bgate = _merge_bwd(dh1_bf, wout, o_bf, pm_bf, z, wba_g, wbp_g)
    d_wba = _matmul_tn(o_bf, dya_bf, N_CHIPS, "dw_branch_attn")
    d_wbp = _matmul_tn(pm_bf, dyp_bf, N_CHIPS, "dw_branch_pool")
    du_bf, d_wpool, d_pscale = _pool_bwd(dpm, mixed, pooled_bf, w_pool, pool_scale)
    d_convb = dconv[:, 3, :FF_SHARD].reshape(1, N_CHIPS * FF_SHARD)
    small = _pack_small([d_bgate, d_wpool.reshape(-1, POOL_GROUP), d_pscale, d_ln1_g, d_ln1_b, d_ln2_g, d_ln2_b,
                         d_convb], loss)[None]
    branch = ["w_out", "w_branch_attn", "w_branch_pool", "small"]
    (dqkv_bf,), (slots, got) = _attn_bwd(
        z, o_bf, lse, do_bf, cos_t, sin_t, nb,
        _MultiCargo([comm.exchange(gate_up), comm.swap(list(zip(branch, [d_wout, d_wba, d_wbp, small])))]))
    comm.arrived(gate_up, slots)
    comm.summed(branch, got, "branch")
    d_win, slots = _dw_in(x_bf, dqkv_bf, du_bf, dzg_bf, comm.exchange(branch))
    comm.arrived(branch, slots)
    ffn = ["w_ffn_down"] + gate_up
    bufs, got = _batched_call(comm.tree_sum_items(ffn), "swap_w_in", comm.place, comm.swap([("w_in", d_win)]))
    comm.tree_summed(ffn, [b[0] for b in bufs])
    comm.summed(["w_in"], got, "w_in")
    (grad_x, bufs), (slots, full) = _inproj_dx(
        dqkv_bf, du_bf, dzg_bf, win_g, dh1, comm.tree_sum_items(branch, 192), comm.place,
        _MultiCargo([comm.exchange(["w_in"]), comm.fill(ffn)]))
    comm.tree_summed(branch, bufs)
    comm.filled(ffn, full)
    comm.arrived(["w_in"], slots)
    return grad_x


SMALL_WEIGHTS = [("b_gate", 2 * D_MODEL), ("w_pool", 4 * POOL_GROUP * POOL_GROUP), ("pool_scale", D_POOL),
                 ("ln1_g", D_MODEL), ("ln1_b", D_MODEL), ("ln2_g", D_MODEL), ("ln2_b", D_MODEL),
                 ("conv_b", N_CHIPS * FF_SHARD)]
LOSS_ROW = 588
SMALL_ROWS = 592


def _small_pieces(ref, first_row, n):
    if ref.shape[0] > 1:
        return [(slice(first_row, first_row + n // LANES), (slice(None), slice(None)))]
    return [(slice(first_row + i, first_row + i + 1), (slice(None), slice(i * LANES, (i + 1) * LANES)))
            for i in range(n // LANES)]


def _small_first_rows():
    rows, r = [], 0
    for _, n in SMALL_WEIGHTS:
        rows.append(r)
        r += n // LANES
    return rows


def _pack_small(parts, loss):
    def body(*refs):
        out = refs[-1]
        out[...] = jnp.zeros_like(out)
        for src, r0, (_, n) in zip(refs, _small_first_rows(), SMALL_WEIGHTS):
            for rows, where in _small_pieces(src, r0, n):
                out[rows, :] = src[where]
        out[LOSS_ROW:LOSS_ROW + 1, :] = refs[len(parts)][0:1, :]

    return pl.pallas_call(body, name="pack_small", out_shape=jax.ShapeDtypeStruct((SMALL_ROWS, LANES), f32))(
        *parts, loss)


def _adamw_small(g_packed, g_conv, triples):
    n_in = 3 * len(triples)

    def body(g_ref, gc_ref, *refs):
        def update(k, g, where):
            w_ref, m_ref, v_ref = refs[3 * k:3 * k + 3]
            new = (g,) + _adamw_update(w_ref[where], m_ref[where], v_ref[where], g)
            for o_ref, val in zip(refs[n_in + 4 * k:n_in + 4 * k + 4], new):
                o_ref[where] = val

        for k, (r0, (_, n)) in enumerate(zip(_small_first_rows(), SMALL_WEIGHTS)):
            for rows, where in _small_pieces(refs[3 * k], r0, n):
                update(k, g_ref[rows, :], where)
        for tap in range(3):
            update(len(SMALL_WEIGHTS), gc_ref[tap:tap + 1, 0:FF_SHARD], (tap,))

    res = pl.pallas_call(
        body, name="adamw_small",
        out_shape=[jax.ShapeDtypeStruct(t[0].shape, f32) for t in triples for _ in range(4)],
    )(g_packed, g_conv, *[a for t in triples for a in t])
    return [res[4 * k:4 * k + 4] for k in range(len(triples))]


def _pad_conv_b(cb):
    return jnp.pad(cb.reshape(N_CHIPS, FF_SHARD), ((0, 0), (0, FF_PAD - FF_SHARD)))


def kernel(x, w_in, b_gate, w_branch_attn, w_pool, pool_scale, w_branch_pool, w_out, ln1_g, ln1_b, w_ffn_gate, w_ffn_up, conv_w, conv_b, w_ffn_down, ln2_g, ln2_b, loss_target, m_w_in, m_b_gate, m_w_branch_attn, m_w_pool, m_pool_scale, m_w_branch_pool, m_w_out, m_ln1_g, m_ln1_b, m_w_ffn_gate, m_w_ffn_up, m_conv_w, m_conv_b, m_w_ffn_down, m_ln2_g, m_ln2_b, v_w_in, v_b_gate, v_w_branch_attn, v_w_pool, v_pool_scale, v_w_branch_pool, v_w_out, v_ln1_g, v_ln1_b, v_w_ffn_gate, v_w_ffn_up, v_conv_w, v_conv_b, v_w_ffn_down, v_ln2_g, v_ln2_b):
    mats = [w_branch_attn[0], w_branch_pool[0], w_out[0], w_ffn_gate[0].T, w_ffn_up[0].T, w_ffn_down[0]]
    padded = [w.shape[0] for w in mats[:3]] + [FF_PAD] * 3
    (x_bf, *casted), (win_g,) = _bf16_casts(x[0], mats, padded, _GatherCargo([w_in[0].astype(bf16)], [0.6]),
                                            "gather_w_in")
    shards = [None] + casted + [jnp.pad(conv_w[0], ((0, CONV_ROWS - 3), (0, FF_PAD - FF_SHARD)))]
    gather = {"inproj": _GatherCargo(shards[1:4], [0.4, 0.45, 0.5]),
              "attn_fwd": _GatherCargo(shards[4:6] + shards[7:], [0.7, 0.7, 0.7]),
              "ffn_up": _GatherCargo(shards[6:7], [0.65])}

    place = jnp.stack([2 * lax.axis_index("x") + lax.axis_index("y"), lax.axis_index("c")]).astype(jnp.int32)
    comm = _GradReducer(place)
    convb_g = _pad_conv_b(conv_b).reshape(N_CHIPS, 1, FF_PAD)
    grad_x = _local_step(
        x[0], x_bf, loss_target[0], win_g, gather, comm, b_gate, w_pool[0], pool_scale, ln1_g, ln1_b, convb_g, ln2_g, ln2_b)
    names = ["w_in", "w_branch_attn", "w_branch_pool", "w_out", "w_ffn_gate", "w_ffn_up", "w_ffn_down", "conv_w", "small"]
    grads = comm.reduce(names)

    weights = [w_in[0], w_branch_attn[0], w_branch_pool[0], w_out[0], w_ffn_gate[0].T, w_ffn_up[0].T, w_ffn_down[0]]
    m_in = [m_w_in[0], m_w_branch_attn[0], m_w_branch_pool[0], m_w_out[0], m_w_ffn_gate[0].T, m_w_ffn_up[0].T,
            m_w_ffn_down[0]]
    v_in = [v_w_in[0], v_w_branch_attn[0], v_w_branch_pool[0], v_w_out[0], v_w_ffn_gate[0].T, v_w_ffn_up[0].T,
            v_w_ffn_down[0]]
    tiles = [256, 256, 256, 128, 176, 176, 176]
    jobs = list(zip(weights, m_in, v_in, grads, tiles))
    updated = _adamw(jobs[:4], "adamw_rest") + _adamw(jobs[4:], "adamw_ffn")
    res = {nm: [(r.T if nm in ("w_ffn_gate", "w_ffn_up") else r)[None] for r in outs4]
           for nm, outs4 in zip(names, updated)}

    small_w = [b_gate, w_pool, pool_scale, ln1_g, ln1_b, ln2_g, ln2_b, conv_b, conv_w]
    small_m = [m_b_gate, m_w_pool, m_pool_scale, m_ln1_g, m_ln1_b, m_ln2_g, m_ln2_b, m_conv_b, m_conv_w]
    small_v = [v_b_gate, v_w_pool, v_pool_scale, v_ln1_g, v_ln1_b, v_ln2_g, v_ln2_b, v_conv_b, v_conv_w]

    def as_kernel_sees(a):
        return a.reshape(-1, POOL_GROUP) if a.ndim == 4 else a.transpose(1, 0, 2) if a.ndim == 3 else a

    small_out = _adamw_small(grads[-1], grads[-2],
                             [tuple(as_kernel_sees(a) for a in t) for t in zip(small_w, small_m, small_v)])
    for nm, w, outs4 in zip([nm for nm, _ in SMALL_WEIGHTS] + ["conv_w"], small_w, small_out):
        res[nm] = [o.transpose(1, 0, 2) if w.ndim == 3 else o.reshape(w.shape) for o in outs4]
    loss = grads[-1][LOSS_ROW, 0]

    order = ["w_in", "b_gate", "w_branch_attn", "w_pool", "pool_scale", "w_branch_pool", "w_out", "ln1_g", "ln1_b",
             "w_ffn_gate", "w_ffn_up", "conv_w", "conv_b", "w_ffn_down", "ln2_g", "ln2_b"]
    outs = [loss, grad_x[None]]
    for kind in range(4):
        outs += [res[nm][kind] for nm in order]
    return tuple(outs)
```

```python
import functools

import jax
import jax.numpy as jnp
from jax import lax
from jax.experimental import pallas as pl
from jax.experimental.pallas import tpu as pltpu

D_MODEL = 1024
HEAD_DIM = 64
D_ATTN = 512
D_POOL = 512
MOBA_BLOCK = 256
MOBA_TOPK = 3
POOL_GROUP = 128
MAX_WINDOW = 16
FF_SHARD = 704
FF_PAD = 768
D_FF_PAD = 4 * FF_PAD
N_CHIPS = 4
LANES = 128
ALPHA = (2.0 * 1) ** 0.25
LN_EPS = 1e-5
NEG = -1e30
SCALE = HEAD_DIM ** -0.5
ADAM_LR, ADAM_B1, ADAM_B2, ADAM_EPS, ADAM_WD, ADAM_STEP = 0.001, 0.9, 0.999, 1e-08, 0.01, 10
VMEM_LIMIT = 56 * 1024 * 1024
MESH = pl.DeviceIdType.MESH

bf16 = jnp.bfloat16
f32 = jnp.float32


def _dot(a, b):
    return jnp.dot(a, b, preferred_element_type=f32)


def _dot_nt(a, b):
    return lax.dot_general(a, b, (((1,), (1,)), ((), ())), preferred_element_type=f32)


def _dot_tn(a, b):
    return lax.dot_general(a, b, (((0,), (0,)), ((), ())), preferred_element_type=f32)


def _dot_shards(a, w_ref):
    return jnp.concatenate([_dot(a, w_ref[p]) for p in range(w_ref.shape[0])], axis=1)


def _dot_nt_shards(d, w_ref):
    n = w_ref.shape[2]
    acc = _dot_nt(d[:, 0:n], w_ref[0])
    for p in range(1, w_ref.shape[0]):
        acc = acc + _dot_nt(d[:, p * n:(p + 1) * n], w_ref[p])
    return acc


def _column_shards_spec(k, n):
    return pl.BlockSpec((N_CHIPS, k, n // N_CHIPS), lambda i: (0, 0, 0))


def _call(body, *, name, grid, in_specs, out_specs, out_shape, scratch=(), cargo=None):
    if cargo is not None:
        return _cargo_call(body, cargo, name=name, grid=grid, in_specs=in_specs, out_specs=out_specs,
                           out_shape=out_shape, scratch=scratch)
    return pl.pallas_call(
        body, name=name, grid=grid, in_specs=in_specs, out_specs=out_specs, out_shape=out_shape,
        scratch_shapes=list(scratch),
        compiler_params=pltpu.CompilerParams(dimension_semantics=("arbitrary",) * len(grid),
                                             vmem_limit_bytes=VMEM_LIMIT))


def _rot_half(t):
    lane = lax.broadcasted_iota(jnp.int32, t.shape, 1)
    first = (lane % HEAD_DIM) < (HEAD_DIM // 2)
    return jnp.where(first, pltpu.roll(t, LANES - HEAD_DIM // 2, 1), pltpu.roll(t, HEAD_DIM // 2, 1))


def _rope(t, cos, sin_signed):
    return t * cos + _rot_half(t) * sin_signed


def _rope_bwd(d, cos, sin_signed):
    return d * cos + _rot_half(d * sin_signed)


def _gelu_parts(a):
    cdf = 0.5 * (1.0 + lax.erf(a * (2.0 ** -0.5)))
    pdf = jnp.exp(-0.5 * a * a) * ((2.0 * jnp.pi) ** -0.5)
    return a * cdf, cdf + a * pdf


def _layer_norm(h, g, b):
    mu = jnp.mean(h, axis=-1, keepdims=True)
    xc = h - mu
    var = jnp.mean(xc * xc, axis=-1, keepdims=True)
    rstd = lax.rsqrt(var + LN_EPS)
    xhat = xc * rstd
    return xhat * g + b, xhat, rstd


def _layer_norm_bwd(dy, xhat, rstd, g):
    dxh = dy * g
    m1 = jnp.mean(dxh, axis=-1, keepdims=True)
    m2 = jnp.mean(dxh * xhat, axis=-1, keepdims=True)
    return rstd * (dxh - m1 - xhat * m2)


def _inproj(x_bf, win_g, cos_t, sin_t, b_gate, cargo):
    s = x_bf.shape[0]
    tm, tn = 1024, 512

    def body(x_ref, w_ref, cos_ref, sin_ref, b_ref, o_ref):
        j = pl.program_id(1)
        acc = _dot(x_ref[...], w_ref[...])

        @pl.when(j < 2)
        def _():
            for c in range(tn // LANES):
                sl = slice(c * LANES, (c + 1) * LANES)
                o_ref[:, sl] = _rope(acc[:, sl], cos_ref[...], sin_ref[...])

        @pl.when((j >= 2) & (j < 4))
        def _():
            o_ref[...] = acc

        @pl.when(j >= 4)
        def _():
            o_ref[...] = jax.nn.sigmoid(acc + b_ref[...])

    (z,), carried = _cargo_call(
        body, cargo, name="inproj", grid=(s // tm, 4 * D_MODEL // tn),
        in_specs=[pl.BlockSpec((tm, D_MODEL), lambda i, j: (i, 0)),
                  pl.BlockSpec((None, D_MODEL, tn), lambda i, j: (j // 2, 0, j % 2)),
                  pl.BlockSpec((tm, LANES), lambda i, j: (i, 0)),
                  pl.BlockSpec((tm, LANES), lambda i, j: (i, 0)),
                  pl.BlockSpec((1, tn), lambda i, j: (0, jnp.maximum(j - 4, 0)))],
        out_specs=[pl.BlockSpec((tm, tn), lambda i, j: (i, j))],
        out_shape=[jax.ShapeDtypeStruct((s, 4 * D_MODEL), f32)])(x_bf, win_g, cos_t, sin_t, b_gate)
    return z, carried


STAT_ROWS = 8


def _pair_rows(v0, v1, fill):
    r = lax.broadcasted_iota(jnp.int32, (STAT_ROWS, v0.shape[1]), 0)
    return jnp.where(r == 0, v0, jnp.where(r == 1, v1, fill))


def _head_lanes(shape):
    lane = lax.broadcasted_iota(jnp.int32, shape, 1)
    return lane < HEAD_DIM, lane >= HEAD_DIM


def _head_rows(shape):
    row = lax.broadcasted_iota(jnp.int32, shape, 0)
    return row < HEAD_DIM, row >= HEAD_DIM


def _moba_select(q_bf, k_all, i, nb):
    k_mean = jnp.mean(k_all.reshape(nb, MOBA_BLOCK, LANES), axis=1)
    n_io = lax.broadcasted_iota(jnp.int32, (nb, MOBA_BLOCK), 0)
    past = n_io < i
    sels = []
    for head in _head_lanes((nb, LANES)):
        gate = _dot_nt(jnp.where(head, k_mean, 0.0).astype(bf16), q_bf)
        g = jnp.where(past, gate, NEG)
        rank = jnp.zeros((nb, MOBA_BLOCK), f32)
        for m in range(nb):
            gm = g[m:m + 1, :]
            rank = rank + jnp.where((gm > g) | ((gm == g) & (m < n_io)), 1.0, 0.0)
        sels.append(jnp.where(past & (rank < MOBA_TOPK), 1.0, 0.0))
    return sels


def _attn_fwd(z, nb, w_pool, pool_scale, cargo):
    s = z.shape[0]

    def body(q_ref, k_ref, v_ref, u_ref, wp_ref, sc_ref, o_ref, lse_ref, pooled_ref, mixed_ref, pm_ref, vt_s, pad):
        pair, i = pl.program_id(0), pl.program_id(1)

        @pl.when(i == 0)
        def _():
            for j in range(nb):
                vt_s[j] = v_ref[j * MOBA_BLOCK:(j + 1) * MOBA_BLOCK, :].T.astype(bf16)
            _pool_group(pair, u_ref, wp_ref, sc_ref, pooled_ref, mixed_ref, pm_ref, pad)

        k_heads = _head_lanes((MOBA_BLOCK, LANES))
        o_heads = _head_rows((LANES, MOBA_BLOCK))

        def block(j, qs, allow, m_old, l_old):
            kj, vtj = k_ref[j * MOBA_BLOCK:(j + 1) * MOBA_BLOCK, :], vt_s[j]
            m_new, alpha, l_new, pv = [], [], [], None
            for h in range(2):
                sc = jnp.where(allow[h], _dot_nt(jnp.where(k_heads[h], kj, 0.0).astype(bf16), qs), NEG)
                mx = jnp.max(sc, axis=0, keepdims=True)
                mn = mx if m_old is None else jnp.maximum(m_old[h], mx)
                p = jnp.exp(sc - mn)
                lsum = jnp.sum(p, axis=0, keepdims=True)
                if m_old is None:
                    a = None
                else:
                    a = jnp.exp(m_old[h] - mn)
                    lsum = a * l_old[h] + lsum
                t = _dot(jnp.where(o_heads[h], vtj, jnp.zeros_like(vtj)), p.astype(bf16))
                pv = t if pv is None else pv + t
                m_new.append(mn)
                alpha.append(a)
                l_new.append(lsum)
            return m_new, alpha, l_new, pv

        def query_block(c):
            q = q_ref[...]
            sels = _moba_select(q.astype(bf16), k_ref[...], c, nb)
            qs = (q * SCALE).astype(bf16)
            key = lax.broadcasted_iota(jnp.int32, (MOBA_BLOCK, MOBA_BLOCK), 0)
            qry = lax.broadcasted_iota(jnp.int32, (MOBA_BLOCK, MOBA_BLOCK), 1)
            m, _, l, acc = block(c, qs, [key <= qry] * 2, None, None)
            for j in range(c):
                m, alpha, l, pv = block(j, qs, [sels[h][j:j + 1, :] > 0.0 for h in range(2)], m, l)
                acc = acc * jnp.where(o_heads[0], alpha[0], alpha[1]) + pv
            o_ref[...] = (acc / jnp.where(o_heads[0], l[0], l[1])).T.astype(bf16)
            lse_ref[...] = _pair_rows(m[0] + jnp.log(l[0]), m[1] + jnp.log(l[1]), 0.0)

        for c in range(nb):
            pl.when(i == c)(functools.partial(query_block, c))

    group = pl.BlockSpec((s, POOL_GROUP), lambda hp, i: (0, hp))
    return _cargo_call(
        body, cargo, name="attn_fwd", grid=(D_ATTN // LANES, nb),
        in_specs=[pl.BlockSpec((MOBA_BLOCK, LANES), lambda hp, i: (i, hp)),
                  pl.BlockSpec((s, LANES), lambda hp, i: (0, 4 + hp)),
                  pl.BlockSpec((s, LANES), lambda hp, i: (0, 8 + hp)),
                  pl.BlockSpec((s, POOL_GROUP), lambda hp, i: (0, 12 + hp)),
                  pl.BlockSpec((None, POOL_GROUP, POOL_GROUP), lambda hp, i: (hp, 0, 0)),
                  pl.BlockSpec((1, POOL_GROUP), lambda hp, i: (0, hp))],
        out_specs=[pl.BlockSpec((MOBA_BLOCK, LANES), lambda hp, i: (i, hp)),
                   pl.BlockSpec((None, STAT_ROWS, MOBA_BLOCK), lambda hp, i: (hp, 0, i)), group, group, group],
        out_shape=[jax.ShapeDtypeStruct((s, D_ATTN), bf16),
                   jax.ShapeDtypeStruct((D_ATTN // LANES, STAT_ROWS, s), f32),
                   jax.ShapeDtypeStruct((s, D_POOL), bf16), jax.ShapeDtypeStruct((s, D_POOL), f32),
                   jax.ShapeDtypeStruct((s, D_POOL), bf16)],
        scratch=[pltpu.VMEM((nb, LANES, MOBA_BLOCK), bf16), pltpu.VMEM((s + MAX_WINDOW, POOL_GROUP), f32)],
    )(z, z, z, z, w_pool, pool_scale)


def _window_select(snaps, g):
    return jnp.where(g == 0, snaps[0], jnp.where(g == 1, snaps[1], jnp.where(g == 2, snaps[2], snaps[3])))


def _pool_group(g, u_ref, w_ref, sc_ref, pooled_ref, mixed_ref, pm_ref, pad):
    s = u_ref.shape[0]
    u = u_ref[...]
    pad[0:MAX_WINDOW, :] = jnp.zeros((MAX_WINDOW, POOL_GROUP), f32)
    pad[MAX_WINDOW:MAX_WINDOW + s, :] = u
    acc = u
    snaps = []
    for d in range(1, MAX_WINDOW):
        acc = acc + pad[MAX_WINDOW - d:MAX_WINDOW - d + s, :]
        if d + 1 in (2, 4, 8, 16):
            snaps.append(acc)
    win = _window_select(snaps, g)
    t = lax.broadcasted_iota(jnp.int32, (s, POOL_GROUP), 0)
    count = jnp.minimum(t + 1, jnp.left_shift(2, g)).astype(f32)
    pooled = (win / count - u).astype(bf16)
    mixed = _dot(pooled, w_ref[...].astype(bf16))
    pooled_ref[...] = pooled
    mixed_ref[...] = mixed
    pm_ref[...] = (mixed * sc_ref[...]).astype(bf16)


def _branch_merge(o_bf, pm_bf, z, wba, wbp):
    s = o_bf.shape[0]
    tm = 512

    def body(o_ref, pm_ref, ga_ref, gp_ref, wba_ref, wbp_ref, m_ref):
        ya = _dot_shards(o_ref[...], wba_ref)
        yp = _dot_shards(pm_ref[...], wbp_ref)
        m_ref[...] = (ga_ref[...] * ya + gp_ref[...] * yp).astype(bf16)

    return _call(
        body, name="branch_merge", grid=(s // tm,),
        in_specs=[pl.BlockSpec((tm, D_ATTN), lambda i: (i, 0)), pl.BlockSpec((tm, D_POOL), lambda i: (i, 0)),
                  pl.BlockSpec((tm, D_MODEL), lambda i: (i, 2)), pl.BlockSpec((tm, D_MODEL), lambda i: (i, 3)),
                  _column_shards_spec(D_ATTN, D_MODEL), _column_shards_spec(D_POOL, D_MODEL)],
        out_specs=pl.BlockSpec((tm, D_MODEL), lambda i: (i, 0)),
        out_shape=jax.ShapeDtypeStruct((s, D_MODEL), bf16))(o_bf, pm_bf, z, z, wba, wbp)


def _out_ln1(m_bf, wout, x, ln_g, ln_b):
    s = x.shape[0]
    tm = 512

    def body(m_ref, w_ref, x_ref, g_ref, b_ref, x1_ref, xhat_ref, rstd_ref):
        h = ALPHA * x_ref[...] + _dot(m_ref[...], w_ref[...])
        y, xhat, rstd = _layer_norm(h, g_ref[...], b_ref[...])
        x1_ref[...] = y.astype(bf16)
        xhat_ref[...] = xhat
        rstd_ref[...] = jnp.broadcast_to(rstd, (tm, LANES))

    row = pl.BlockSpec((tm, D_MODEL), lambda i: (i, 0))
    vec = pl.BlockSpec((1, D_MODEL), lambda i: (0, 0))
    return _call(
        body, name="out_ln1", grid=(s // tm,),
        in_specs=[row, pl.BlockSpec((D_MODEL, D_MODEL), lambda i: (0, 0)), row, vec, vec],
        out_specs=[row, row, pl.BlockSpec((tm, LANES), lambda i: (i, 0))],
        out_shape=[jax.ShapeDtypeStruct((s, D_MODEL), bf16), jax.ShapeDtypeStruct((s, D_MODEL), f32),
                   jax.ShapeDtypeStruct((s, LANES), f32)])(m_bf, wout, x, ln_g, ln_b)


FF_TILE = 256
FF_TILES_PER_SHARD = FF_PAD // FF_TILE
CONV_PAD = 8
CONV_ROWS = 16

def _ff_weight_spec(rows):
    return pl.BlockSpec((None, rows, FF_TILE), lambda j: (j // FF_TILES_PER_SHARD, 0, j % FF_TILES_PER_SHARD))


def _ff_rows_spec():
    return pl.BlockSpec((None, FF_TILE, D_MODEL), lambda j: (j // FF_TILES_PER_SHARD, j % FF_TILES_PER_SHARD, 0))


def _row_shifts(x, shifts):
    s = x.shape[0]
    padded = jnp.concatenate([x, jnp.zeros((CONV_PAD, x.shape[1]), x.dtype)], axis=0)
    return [pltpu.roll(padded, d % (s + CONV_PAD), 0)[0:s] for d in shifts]


def _ffn_up(x1_bf, wg_g, wu_g, convw_g, convb_g, cargo):
    s = x1_bf.shape[0]

    def body(x_ref, wg_ref, wu_ref, cw_ref, cb_ref, a_ref, up_ref, ac_ref, hh_ref):
        x1 = x_ref[...]
        a = _dot_nt(x1, wg_ref[...])
        up = _dot_nt(x1, wu_ref[...])
        a_ref[...] = a
        up_ref[...] = up
        a1, a2 = _row_shifts(a, (1, 2))
        ac = a * cw_ref[2:3, :] + a1 * cw_ref[1:2, :] + a2 * cw_ref[0:1, :] + cb_ref[...]
        ac_ref[...] = ac
        hg, _ = _gelu_parts(ac)
        hh_ref[...] = (hg * up).astype(bf16)

    col = pl.BlockSpec((s, FF_TILE), lambda j: (0, j))
    wide = jax.ShapeDtypeStruct((s, D_FF_PAD), f32)
    return _call(
        body, name="ffn_up", grid=(D_FF_PAD // FF_TILE,),
        in_specs=[pl.BlockSpec((s, D_MODEL), lambda j: (0, 0)), _ff_rows_spec(), _ff_rows_spec(),
                  _ff_weight_spec(8), _ff_weight_spec(1)],
        out_specs=[col, col, col, col],
        out_shape=[wide, wide, wide, jax.ShapeDtypeStruct((s, D_FF_PAD), bf16)],
        cargo=cargo)(x1_bf, wg_g, wu_g, convw_g, convb_g)


def _ffn_down_ln2_loss(hh_bf, wd, xhat1, ln1_g, ln1_b, ln2_g, ln2_b, target):
    s = hh_bf.shape[0]
    tm = 256

    def body(hh_ref, w_ref, xh1_ref, g1_ref, b1_ref, g2_ref, b2_ref, t_ref, loss_ref, dh_ref, dhb_ref, dg_ref, db_ref):
        i = pl.program_id(0)
        x1 = xh1_ref[...] * g1_ref[...] + b1_ref[...]
        h = ALPHA * x1 + _dot(hh_ref[...], w_ref[...])
        y, xhat, rstd = _layer_norm(h, g2_ref[...], b2_ref[...])
        err = y - t_ref[...]
        part = 0.5 * jnp.sum(jnp.mean(err * err, axis=-1, keepdims=True), axis=0, keepdims=True)
        dy = err * (1.0 / D_MODEL)

        @pl.when(i == 0)
        def _():
            loss_ref[...] = jnp.zeros_like(loss_ref)
            dg_ref[...] = jnp.zeros_like(dg_ref)
            db_ref[...] = jnp.zeros_like(db_ref)

        loss_ref[...] += jnp.broadcast_to(part, loss_ref.shape)
        dg_ref[...] += jnp.sum(dy * xhat, axis=0, keepdims=True)
        db_ref[...] += jnp.sum(dy, axis=0, keepdims=True)
        dh = _layer_norm_bwd(dy, xhat, rstd, g2_ref[...])
        dh_ref[...] = dh
        dhb_ref[...] = dh.astype(bf16)

    row = pl.BlockSpec((tm, D_MODEL), lambda i: (i, 0))
    vec = pl.BlockSpec((1, D_MODEL), lambda i: (0, 0))
    return _call(
        body, name="ffn_down_ln2_loss", grid=(s // tm,),
        in_specs=[pl.BlockSpec((tm, D_FF_PAD), lambda i: (i, 0)),
                  pl.BlockSpec((D_FF_PAD, D_MODEL), lambda i: (0, 0)),
                  row, vec, vec, vec, vec, row],
        out_specs=[pl.BlockSpec((8, LANES), lambda i: (0, 0)), row, row, vec, vec],
        out_shape=[jax.ShapeDtypeStruct((8, LANES), f32), jax.ShapeDtypeStruct((s, D_MODEL), f32),
                   jax.ShapeDtypeStruct((s, D_MODEL), bf16), jax.ShapeDtypeStruct((1, D_MODEL), f32),
                   jax.ShapeDtypeStruct((1, D_MODEL), f32)])(hh_bf, wd, xhat1, ln1_g, ln1_b, ln2_g, ln2_b, target)


def _ffn_act_bwd(dh2_bf, wd_g, ac, a, up, convw_g):
    s = dh2_bf.shape[0]

    def body(dh_ref, wd_ref, ac_ref, a_ref, up_ref, cw_ref, da_ref, dup_ref, dc_ref):
        dhh = _dot_nt(dh_ref[...], wd_ref[...])
        hg, dgelu = _gelu_parts(ac_ref[...])
        dup_ref[...] = (dhh * hg).astype(bf16)
        dac = dhh * up_ref[...] * dgelu
        dac1, dac2 = _row_shifts(dac, (-1, -2))
        a = a_ref[...]
        rows = [jnp.sum(d * a, axis=0, keepdims=True) for d in (dac2, dac1, dac)]
        rows.append(jnp.sum(dac, axis=0, keepdims=True))
        rows.append(jnp.zeros((CONV_ROWS - 4, FF_TILE), f32))
        dc_ref[...] = jnp.concatenate(rows, axis=0)
        da_ref[...] = (dac * cw_ref[2:3, :] + dac1 * cw_ref[1:2, :] + dac2 * cw_ref[0:1, :]).astype(bf16)

    col = pl.BlockSpec((s, FF_TILE), lambda j: (0, j))
    return _call(
        body, name="ffn_act_bwd", grid=(D_FF_PAD // FF_TILE,),
        in_specs=[pl.BlockSpec((s, D_MODEL), lambda j: (0, 0)), _ff_rows_spec(), col, col, col, _ff_weight_spec(8)],
        out_specs=[col, col, _ff_weight_spec(CONV_ROWS)],
        out_shape=[jax.ShapeDtypeStruct((s, D_FF_PAD), bf16), jax.ShapeDtypeStruct((s, D_FF_PAD), bf16),
                   jax.ShapeDtypeStruct((N_CHIPS, CONV_ROWS, FF_PAD), f32)],
    )(dh2_bf, wd_g, ac, a, up, convw_g)


def _matmul_tn(a, b, n_shards, name, tm=512, tn=1024, cargo=None):
    k, m = a.shape
    n = b.shape[1]
    tm, tn = min(tm, m), min(tn, n // n_shards)
    per = n // n_shards // tn

    def body(a_ref, b_ref, o_ref, at_s):
        @pl.when(pl.program_id(1) == 0)
        def _():
            at_s[...] = a_ref[...].T

        o_ref[...] = _dot(at_s[...], b_ref[...]).astype(bf16)

    res = _call(
        body, name=name, grid=(m // tm, n // tn),
        in_specs=[pl.BlockSpec((k, tm), lambda i, j: (0, i)), pl.BlockSpec((k, tn), lambda i, j: (0, j))],
        out_specs=[pl.BlockSpec((None, tm, tn), lambda i, j: (j // per, i, j % per))],
        out_shape=[jax.ShapeDtypeStruct((n_shards, m, n // n_shards), bf16)], scratch=[pltpu.VMEM((tm, k), bf16)],
        cargo=cargo)(a, b)
    return res[0] if cargo is None else (res[0][0], res[1])


def _dw_in(x_bf, dqkv, du, dzg, cargo):
    k, m = x_bf.shape
    tm, half = 512, D_ATTN

    def body(a_ref, dqkv_ref, du_ref, dzg_ref, o_ref, at_s):
        j = pl.program_id(1)

        @pl.when(j == 0)
        def _():
            at_s[...] = a_ref[...].T
            o_ref[:, 0:half] = _dot(at_s[...], dqkv_ref[0]).astype(bf16)
            o_ref[:, half:2 * half] = _dot(at_s[...], dqkv_ref[1]).astype(bf16)

        @pl.when(j == 1)
        def _():
            o_ref[:, 0:half] = _dot(at_s[...], dqkv_ref[2]).astype(bf16)
            o_ref[:, half:2 * half] = _dot(at_s[...], du_ref[...]).astype(bf16)

        @pl.when(j >= 2)
        def _():
            o_ref[...] = _dot(at_s[...], dzg_ref[...]).astype(bf16)

    (d_win,), carried = _cargo_call(
        body, cargo, name="dw_in", grid=(m // tm, N_CHIPS),
        in_specs=[pl.BlockSpec((k, tm), lambda i, j: (0, i)), pl.BlockSpec((3, k, half), lambda i, j: (0, 0, 0)),
                  pl.BlockSpec((k, half), lambda i, j: (0, 0)),
                  pl.BlockSpec((k, D_MODEL), lambda i, j: (0, jnp.maximum(j - 2, 0)))],
        out_specs=[pl.BlockSpec((None, tm, D_MODEL), lambda i, j: (j, i, 0))],
        out_shape=[jax.ShapeDtypeStruct((N_CHIPS, m, D_MODEL), bf16)], scratch=[pltpu.VMEM((tm, k), bf16)],
    )(x_bf, dqkv, du, dzg)
    return d_win, carried


def _ffn_in_bwd_ln1(da_bf, dup_bf, wg_g, wu_g, dh2, xhat1, rstd1, ln1_g, cargo):
    s = da_bf.shape[0]
    tm = 256

    def body(da_ref, dup_ref, wg_ref, wu_ref, dh2_ref, xh_ref, rstd_ref, g_ref, dh_ref, dhb_ref, dg_ref, db_ref):
        i = pl.program_id(0)
        dx1 = ALPHA * dh2_ref[...]
        for sh in range(N_CHIPS):
            sl = slice(sh * FF_PAD, (sh + 1) * FF_PAD)
            dx1 = dx1 + _dot(da_ref[:, sl], wg_ref[sh]) + _dot(dup_ref[:, sl], wu_ref[sh])
        xhat = xh_ref[...]

        @pl.when(i == 0)
        def _():
            dg_ref[...] = jnp.zeros_like(dg_ref)
            db_ref[...] = jnp.zeros_like(db_ref)

        dg_ref[...] += jnp.sum(dx1 * xhat, axis=0, keepdims=True)
        db_ref[...] += jnp.sum(dx1, axis=0, keepdims=True)
        dh = _layer_norm_bwd(dx1, xhat, rstd_ref[:, 0:1], g_ref[...])
        dh_ref[...] = dh
        dhb_ref[...] = dh.astype(bf16)

    row = pl.BlockSpec((tm, D_MODEL), lambda i: (i, 0))
    wide = pl.BlockSpec((tm, D_FF_PAD), lambda i: (i, 0))
    wfull = pl.BlockSpec((N_CHIPS, FF_PAD, D_MODEL), lambda i: (0, 0, 0))
    vec = pl.BlockSpec((1, D_MODEL), lambda i: (0, 0))
    return _call(
        body, name="ffn_in_bwd_ln1", grid=(s // tm,),
        in_specs=[wide, wide, wfull, wfull, row, row, pl.BlockSpec((tm, LANES), lambda i: (i, 0)), vec],
        out_specs=[row, row, vec, vec],
        out_shape=[jax.ShapeDtypeStruct((s, D_MODEL), f32), jax.ShapeDtypeStruct((s, D_MODEL), bf16),
                   jax.ShapeDtypeStruct((1, D_MODEL), f32), jax.ShapeDtypeStruct((1, D_MODEL), f32)],
        cargo=cargo)(da_bf, dup_bf, wg_g, wu_g, dh2, xhat1, rstd1, ln1_g)


def _merge_bwd(dh1_bf, wout, o_bf, pm_bf, z, wba, wbp):
    s = dh1_bf.shape[0]
    tm = 256

    def body(dh_ref, wout_ref, o_ref, pm_ref, ga_ref, gp_ref, wba_ref, wbp_ref,
             dzg_ref, dya_ref, dyp_ref, do_ref, dpm_ref, dbg_ref):
        i = pl.program_id(0)
        dm = _dot_nt(dh_ref[...], wout_ref[...])
        ya = _dot_shards(o_ref[...], wba_ref)
        yp = _dot_shards(pm_ref[...], wbp_ref)
        ga, gp = ga_ref[...], gp_ref[...]
        dza = dm * ya * ga * (1.0 - ga)
        dzp = dm * yp * gp * (1.0 - gp)

        @pl.when(i == 0)
        def _():
            dbg_ref[...] = jnp.zeros_like(dbg_ref)

        dbg_ref[:, 0:D_MODEL] += jnp.sum(dza, axis=0, keepdims=True)
        dbg_ref[:, D_MODEL:2 * D_MODEL] += jnp.sum(dzp, axis=0, keepdims=True)
        dzg_ref[:, 0:D_MODEL] = dza.astype(bf16)
        dzg_ref[:, D_MODEL:2 * D_MODEL] = dzp.astype(bf16)
        dya = (dm * ga).astype(bf16)
        dyp = (dm * gp).astype(bf16)
        dya_ref[...] = dya
        dyp_ref[...] = dyp
        do_ref[...] = _dot_nt_shards(dya, wba_ref).astype(bf16)
        dpm_ref[...] = _dot_nt_shards(dyp, wbp_ref)

    row = pl.BlockSpec((tm, D_MODEL), lambda i: (i, 0))
    half = pl.BlockSpec((tm, D_ATTN), lambda i: (i, 0))
    full = lambda r, c: pl.BlockSpec((r, c), lambda i: (0, 0))
    return _call(
        body, name="merge_bwd", grid=(s // tm,),
        in_specs=[row, full(D_MODEL, D_MODEL), half, half,
                  pl.BlockSpec((tm, D_MODEL), lambda i: (i, 2)), pl.BlockSpec((tm, D_MODEL), lambda i: (i, 3)),
                  _column_shards_spec(D_ATTN, D_MODEL), _column_shards_spec(D_POOL, D_MODEL)],
        out_specs=[pl.BlockSpec((tm, 2 * D_MODEL), lambda i: (i, 0)), row, row, half, half,
                   pl.BlockSpec((1, 2 * D_MODEL), lambda i: (0, 0))],
        out_shape=[jax.ShapeDtypeStruct((s, 2 * D_MODEL), bf16), jax.ShapeDtypeStruct((s, D_MODEL), bf16),
                   jax.ShapeDtypeStruct((s, D_MODEL), bf16), jax.ShapeDtypeStruct((s, D_ATTN), bf16),
                   jax.ShapeDtypeStruct((s, D_POOL), f32), jax.ShapeDtypeStruct((1, 2 * D_MODEL), f32)],
    )(dh1_bf, wout, o_bf, pm_bf, z, z, wba, wbp)


def _pool_bwd(dpm, mixed, pooled_bf, w_pool, pool_scale):
    s = dpm.shape[0]

    def body(dpm_ref, mixed_ref, pooled_ref, w_ref, sc_ref, du_ref, dw_ref, dsc_ref, pad):
        g = pl.program_id(0)
        dpm_v = dpm_ref[...]
        dsc_ref[...] = jnp.sum(dpm_v * mixed_ref[...], axis=0, keepdims=True)
        dmixed = (dpm_v * sc_ref[...]).astype(bf16)
        dw_ref[...] = _dot_tn(pooled_ref[...], dmixed)
        dpooled = _dot_nt(dmixed, w_ref[...].astype(bf16))
        t = lax.broadcasted_iota(jnp.int32, (s, POOL_GROUP), 0)
        count = jnp.minimum(t + 1, jnp.left_shift(2, g)).astype(f32)
        r = dpooled / count
        pad[0:s, :] = r
        pad[s:s + MAX_WINDOW, :] = jnp.zeros((MAX_WINDOW, POOL_GROUP), f32)
        acc = r
        snaps = []
        for d in range(1, MAX_WINDOW):
            acc = acc + pad[d:d + s, :]
            if d + 1 in (2, 4, 8, 16):
                snaps.append(acc)
        du_ref[...] = (_window_select(snaps, g) - dpooled).astype(bf16)

    blk = pl.BlockSpec((s, POOL_GROUP), lambda g: (0, g))
    return _call(
        body, name="pool_bwd", grid=(4,),
        in_specs=[blk, blk, blk, pl.BlockSpec((None, POOL_GROUP, POOL_GROUP), lambda g: (g, 0, 0)),
                  pl.BlockSpec((1, POOL_GROUP), lambda g: (0, g))],
        out_specs=[blk, pl.BlockSpec((None, POOL_GROUP, POOL_GROUP), lambda g: (g, 0, 0)),
                   pl.BlockSpec((1, POOL_GROUP), lambda g: (0, g))],
        out_shape=[jax.ShapeDtypeStruct((s, D_POOL), bf16), jax.ShapeDtypeStruct((4, POOL_GROUP, POOL_GROUP), f32),
                   jax.ShapeDtypeStruct((1, D_POOL), f32)],
        scratch=[pltpu.VMEM((s + MAX_WINDOW, POOL_GROUP), f32)])(dpm, mixed, pooled_bf, w_pool, pool_scale)


def _attn_bwd(z, o_bf, lse, do_bf, cos_t, sin_t, nb, cargo):
    s = z.shape[0]

    def body(q_ref, k_ref, v_ref, o_ref, lse_ref, do_ref, cq_ref, sq_ref, cf_ref, sf_ref,
             dqkv_ref, kt_s, dk_s, dv_s):
        i = pl.program_id(1)

        @pl.when(i == 0)
        def _():
            dk_s[...] = jnp.zeros_like(dk_s)
            dv_s[...] = jnp.zeros_like(dv_s)
            for j in range(nb):
                kt_s[j] = k_ref[j * MOBA_BLOCK:(j + 1) * MOBA_BLOCK, :].T.astype(bf16)

        k_heads = _head_lanes((MOBA_BLOCK, LANES))
        t_heads = _head_rows((LANES, MOBA_BLOCK))

        def query_block(c):
            q = q_ref[...]
            sels = _moba_select(q.astype(bf16), k_ref[...], c, nb)
            qs = (q * SCALE).astype(bf16)
            qraw = [jnp.where(hm, q, 0.0).astype(bf16) for hm in k_heads]
            do = do_ref[...].astype(f32)
            do_bf = do_ref[...]
            dob = [jnp.where(hm, do, 0.0).astype(bf16) for hm in k_heads]
            pick = _head_lanes((STAT_ROWS, LANES))
            r8 = lax.broadcasted_iota(jnp.int32, (STAT_ROWS, LANES), 0)
            head_pick = jnp.where(((r8 == 0) & pick[0]) | ((r8 == 1) & pick[1]), 1.0, 0.0)
            delta8 = lax.dot_general(head_pick, do * o_ref[...].astype(f32), (((1,), (1,)), ((), ())),
                                     precision=lax.Precision.HIGHEST, preferred_element_type=f32)
            lse8 = lse_ref[...]
            delta, lse_h = [delta8[0:1, :], delta8[1:2, :]], [lse8[0:1, :], lse8[1:2, :]]
            key = lax.broadcasted_iota(jnp.int32, (MOBA_BLOCK, MOBA_BLOCK), 0)
            qry = lax.broadcasted_iota(jnp.int32, (MOBA_BLOCK, MOBA_BLOCK), 1)

            def block(j, allow):
                rows = slice(j * MOBA_BLOCK, (j + 1) * MOBA_BLOCK)
                kj, vj, ktj = k_ref[rows, :], v_ref[rows, :], kt_s[j]
                dqt, dk, dv = None, None, None
                for h in range(2):
                    sc = _dot_nt(jnp.where(k_heads[h], kj, 0.0).astype(bf16), qs)
                    p = jnp.where(allow[h], jnp.exp(sc - lse_h[h]), 0.0)
                    dp = _dot_nt(jnp.where(k_heads[h], vj, 0.0).astype(bf16), do_bf)
                    ds = (p * (dp - delta[h]) * SCALE).astype(bf16)
                    tv = _dot(p.astype(bf16), dob[h])
                    tk = _dot(ds, qraw[h])
                    tq = _dot(jnp.where(t_heads[h], ktj, jnp.zeros_like(ktj)), ds)
                    dqt, dk, dv = (tq, tk, tv) if dqt is None else (dqt + tq, dk + tk, dv + tv)
                dk_s[rows, :] += dk
                dv_s[rows, :] += dv
                return dqt

            dqt = block(c, [key <= qry] * 2)
            for j in range(c):
                dqt = dqt + block(j, [sels[h][j:j + 1, :] > 0.0 for h in range(2)])
            dqkv_ref[0, c * MOBA_BLOCK:(c + 1) * MOBA_BLOCK, :] = _rope_bwd(dqt.T, cq_ref[...], sq_ref[...]).astype(bf16)

        for c in range(nb):
            pl.when(i == c)(functools.partial(query_block, c))

        @pl.when(i == nb - 1)
        def _():
            dqkv_ref[1] = _rope_bwd(dk_s[...], cf_ref[...], sf_ref[...]).astype(bf16)
            dqkv_ref[2] = dv_s[...].astype(bf16)

    qblk = pl.BlockSpec((MOBA_BLOCK, LANES), lambda hp, i: (i, hp))
    tq = pl.BlockSpec((MOBA_BLOCK, LANES), lambda hp, i: (i, 0))
    tf = pl.BlockSpec((s, LANES), lambda hp, i: (0, 0))
    return _cargo_call(
        body, cargo, name="attn_bwd", grid=(D_ATTN // LANES, nb),
        in_specs=[qblk, pl.BlockSpec((s, LANES), lambda hp, i: (0, 4 + hp)),
                  pl.BlockSpec((s, LANES), lambda hp, i: (0, 8 + hp)), qblk,
                  pl.BlockSpec((None, STAT_ROWS, MOBA_BLOCK), lambda hp, i: (hp, 0, i)), qblk, tq, tq, tf, tf],
        out_specs=[pl.BlockSpec((3, s, LANES), lambda hp, i: (0, 0, hp))],
        out_shape=[jax.ShapeDtypeStruct((3, s, D_ATTN), bf16)],
        scratch=[pltpu.VMEM((nb, LANES, MOBA_BLOCK), bf16), pltpu.VMEM((s, LANES), f32), pltpu.VMEM((s, LANES), f32)],
    )(z, z, z, o_bf, lse, do_bf, cos_t, sin_t, cos_t, sin_t)


def _inproj_dx(dqkv, du, dzg, win_g, dh1, more_items, place, cargo):
    s = du.shape[0]
    tm, half = 256, D_ATTN

    def compute(ins, outs, tile):
        dqkv_ref, du_ref, dzg_ref, w_ref, dh_ref = ins
        acc = ALPHA * dh_ref[...]
        for n, part in enumerate([dqkv_ref[0], dqkv_ref[1], dqkv_ref[2], du_ref[...]]):
            acc = acc + _dot_nt(part, w_ref[n // 2, :, (n % 2) * half:(n % 2 + 1) * half])
        for sh in range(2, N_CHIPS):
            acc = acc + _dot_nt(dzg_ref[:, (sh - 2) * D_MODEL:(sh - 1) * D_MODEL], w_ref[sh])
        outs[0][...] = acc

    row = lambda tile: pl.BlockSpec((tm, D_MODEL), lambda t, pc: (tile(t), 0))
    ins = lambda tile: [pl.BlockSpec((3, tm, half), lambda t, pc: (0, tile(t), 0)),
                        pl.BlockSpec((tm, half), lambda t, pc: (tile(t), 0)),
                        pl.BlockSpec((tm, 2 * D_MODEL), lambda t, pc: (tile(t), 0)),
                        pl.BlockSpec((N_CHIPS, D_MODEL, D_MODEL), lambda t, pc: (0, 0, 0)), row(tile)]
    main = (s // tm, ins, lambda tile: [row(tile)], [jax.ShapeDtypeStruct((s, D_MODEL), f32)],
            [dqkv, du, dzg, win_g, dh1], compute)
    outs, carried = _batched_call([main] + more_items, "inproj_dx", place, cargo)
    return (outs[0][0], [o[0] for o in outs[1:]]), carried


ANY = pl.BlockSpec(memory_space=pl.ANY)


def _chip_index():
    return 2 * lax.axis_index("x") + lax.axis_index("y")


def _peer(k):
    x, y, c = lax.axis_index("x"), lax.axis_index("y"), lax.axis_index("c")
    return (x ^ (k >> 1), y ^ (k & 1), c)


def _sibling():
    return (lax.axis_index("x"), lax.axis_index("y"), 1 - lax.axis_index("c"))


class _GatherCargo:
    aliased = ()

    def __init__(self, shards, pass_on):
        self.shards, self.pass_on, n = shards, pass_on, len(shards)
        self.inputs = [a.reshape(2, a.shape[0] // 2, a.shape[1]) for a in shards]
        self.out_shape = [jax.ShapeDtypeStruct((N_CHIPS,) + a.shape, a.dtype) for a in self.inputs]
        two, three, one = [pltpu.SemaphoreType.DMA(shp) for shp in ((2, n), (N_CHIPS - 1, n), (n,))]
        self.sems = [two, two, two, two, three, three, one, one]

    def _relay_rows(self, a):
        h = self.inputs[a].shape[1]
        cut = h // 2 if h % 32 == 0 else h
        return {1: (0, cut), 2: (cut, h)}

    def _copies(self, src, dst, sems, a):
        send_sems, recv_sems, rsend_sems, rrecv_sems, fsend_sems, frecv_sems, local_sems, own_sems = sems
        p, c = _chip_index(), lax.axis_index("c")
        copy = lambda s, d, **kw: functools.partial(pltpu.make_async_remote_copy, src_ref=s, dst_ref=d, **kw)
        own = copy(src[a], dst[a].at[p], send_sem=local_sems.at[a], recv_sem=own_sems.at[a], device_id=_sibling(),
                   device_id_type=MESH)
        out, arrive, relay, relayed, onward, landed = {}, {}, {}, {}, {}, {}
        for k in range(1, N_CHIPS):
            d2d = dict(send_sem=fsend_sems.at[k - 1, a], recv_sem=frecv_sems.at[k - 1, a], device_id=_sibling(),
                       device_id_type=MESH)
            got, theirs = dst[a].at[p ^ k, c], dst[a].at[p ^ k, 1 - c]
            onward[k] = copy(got, got, **d2d)
            landed[k] = copy(theirs, theirs, **d2d)
        for k, (lo, hi) in self._relay_rows(a).items():
            ici = dict(send_sem=send_sems.at[k - 1, a], recv_sem=recv_sems.at[k - 1, a], device_id=_peer(k),
                       device_id_type=MESH)
            out[k] = copy(src[a].at[c], dst[a].at[p, c], **ici)
            arrive[k] = copy(src[a].at[c], dst[a].at[p ^ k, c], **ici)
            if hi > lo:
                via = dict(send_sem=rsend_sems.at[k - 1, a], recv_sem=rrecv_sems.at[k - 1, a], device_id=_peer(3 - k),
                           device_id_type=MESH)
                rows = dst[a].at[p ^ k, c, lo:hi]
                relay[k] = copy(rows, rows, **via)
                mine = dst[a].at[p ^ 3, c, lo:hi]
                relayed[k] = copy(mine, mine, **via)
        return own, out, arrive, relay, relayed, onward, landed

    def stages(self, steps):
        n = len(self.shards)

        def start(src, dst, sems):
            for a in range(n):
                own, out, _, _, _, _, _ = self._copies(src, dst, sems, a)
                own().start()
                for cp in out.values():
                    cp().start()

        def pass_on(a):
            def act(src, dst, sems):
                _, _, arrive, relay, _, onward, _ = self._copies(src, dst, sems, a)
                for k in arrive:
                    arrive[k]().wait_recv()
                    if k in relay:
                        relay[k]().start()
                    onward[k]().start()
            return act

        def finish(src, dst, sems):
            for a in range(n):
                _, _, _, _, relayed, onward, _ = self._copies(src, dst, sems, a)
                for cp in relayed.values():
                    cp().wait_recv()
                onward[3]().start()
            for a in range(n):
                own, out, _, relay, _, onward, landed = self._copies(src, dst, sems, a)
                for cp in landed.values():
                    cp().wait_recv()
                for cp in list(out.values()) + list(relay.values()) + list(onward.values()):
                    cp().wait_send()
                own().wait()

        mids = [(min(steps - 1, int(self.pass_on[a] * steps)), pass_on(a)) for a in range(n)]
        return [(0, start)] + mids + [(steps - 1, finish)]

    def results(self, outs):
        return [o.reshape((N_CHIPS,) + a.shape) for o, a in zip(outs, self.shards)]


class _ExchangeCargo:
    aliased = ()

    def __init__(self, sums, whole):
        self.inputs, self.whole, n = sums, whole, len(sums)
        self.out_shape = [jax.ShapeDtypeStruct((N_CHIPS - 1,) + g.shape[1:], g.dtype) for g in sums]
        sem = pltpu.SemaphoreType.DMA((N_CHIPS - 1, n))
        self.sems = [sem, sem]

    def _copies(self, src, dst, sems):
        send_sems, recv_sems = sems
        p = _chip_index()
        return [pltpu.make_async_remote_copy(
            src_ref=src[a].at[0] if self.whole[a] else src[a].at[p ^ k], dst_ref=dst[a].at[k - 1],
            send_sem=send_sems.at[k - 1, a], recv_sem=recv_sems.at[k - 1, a], device_id=_peer(k), device_id_type=MESH)
            for k in range(1, N_CHIPS) for a in range(len(self.inputs))]

    def stages(self, steps):
        def start(src, dst, sems):
            for cp in self._copies(src, dst, sems):
                cp.start()

        def finish(src, dst, sems):
            copies = self._copies(src, dst, sems)
            for cp in copies:
                cp.wait_recv()
            for cp in copies:
                cp.wait_send()

        return [(0, start), (steps - 1, finish)]

    def results(self, outs):
        return list(outs)


class _SwapCargo:
    aliased = ()

    def __init__(self, split):
        self.inputs, n = split, len(split)
        self.out_shape = [jax.ShapeDtypeStruct((g.shape[0],) + g.shape[2:], g.dtype) for g in split]
        sem = pltpu.SemaphoreType.DMA((n,))
        self.sems = [sem, sem]

    def _copies(self, src, dst, sems):
        c = lax.axis_index("c")
        return [pltpu.make_async_remote_copy(src_ref=src[a].at[:, 1 - c], dst_ref=dst[a], send_sem=sems[0].at[a],
                                             recv_sem=sems[1].at[a], device_id=_sibling(), device_id_type=MESH)
                for a in range(len(self.inputs))]

    def stages(self, steps):
        def start(src, dst, sems):
            for cp in self._copies(src, dst, sems):
                cp.start()

        def finish(src, dst, sems):
            copies = self._copies(src, dst, sems)
            for cp in copies:
                cp.wait_recv()
            for cp in copies:
                cp.wait_send()

        return [(0, start), (steps - 1, finish)]

    def results(self, outs):
        return list(outs)


class _FillCargo:
    def __init__(self, bufs):
        self.inputs, n = bufs, len(bufs)
        self.out_shape = [jax.ShapeDtypeStruct(b.shape, b.dtype) for b in bufs]
        self.aliased = [(a, a) for a in range(n)]
        sem = pltpu.SemaphoreType.DMA((n,))
        self.sems = [sem, sem]

    def _copies(self, src, dst, sems, landing):
        c = lax.axis_index("c")
        half = 1 - c if landing else c
        return [pltpu.make_async_remote_copy(src_ref=src[a].at[c], dst_ref=dst[a].at[half], send_sem=sems[0].at[a],
                                             recv_sem=sems[1].at[a], device_id=_sibling(), device_id_type=MESH)
                for a in range(len(self.inputs))]

    def stages(self, steps):
        def start(src, dst, sems):
            for cp in self._copies(src, dst, sems, False):
                cp.start()

        def finish(src, dst, sems):
            for cp in self._copies(src, dst, sems, True):
                cp.wait_recv()
            for cp in self._copies(src, dst, sems, False):
                cp.wait_send()

        return [(0, start), (steps - 1, finish)]

    def results(self, outs):
        return [o.reshape(2 * b.shape[1], b.shape[2]) for o, b in zip(outs, self.inputs)]


class _MultiCargo:
    def __init__(self, cargos):
        self.cargos = cargos
        self.inputs = [a for cg in cargos for a in cg.inputs]
        self.out_shape = [o for cg in cargos for o in cg.out_shape]
        self.sems = [s for cg in cargos for s in cg.sems]
        self.aliased, i0, o0 = [], 0, 0
        for cg in cargos:
            self.aliased += [(i0 + i, o0 + o) for i, o in cg.aliased]
            i0, o0 = i0 + len(cg.inputs), o0 + len(cg.out_shape)

    def _parts(self, refs, count):
        out, off = [], 0
        for cg in self.cargos:
            out.append(refs[off:off + count(cg)])
            off += count(cg)
        return out

    def stages(self, steps):
        merged = []
        for n, cg in enumerate(self.cargos):
            for at, act in cg.stages(steps):
                def part(src, dst, sems, n=n, act=act):
                    act(self._parts(src, lambda g: len(g.inputs))[n], self._parts(dst, lambda g: len(g.out_shape))[n],
                        self._parts(sems, lambda g: len(g.sems))[n])
                merged.append((at, part))
        starts = [m for m in merged if m[0] == 0]
        return starts + sorted([m for m in merged if m[0] != 0], key=lambda m: m[0])

    def results(self, outs):
        return [cg.results(part) for cg, part in zip(self.cargos, self._parts(list(outs), lambda g: len(g.out_shape)))]


def _cargo_call(body, cargo, *, name, grid, in_specs, out_specs, out_shape, scratch=(), prefetch=None):
    n_in, n_out, n_scr = len(in_specs), len(out_specs), len(scratch)
    c_in, c_out = len(cargo.inputs), len(cargo.out_shape)
    steps = 1
    for g in grid:
        steps *= g
    stages = cargo.stages(steps)

    def wrapped(*refs):
        head, refs = (refs[:1], refs[1:]) if prefetch is not None else ((), refs)
        ins, refs = refs[:n_in], refs[n_in:]
        cin, refs = refs[:c_in], refs[c_in:]
        outs, refs = refs[:n_out], refs[n_out:]
        cout, refs = refs[:c_out], refs[c_out:]
        scr, sems = refs[:n_scr], refs[n_scr:]
        if not grid:
            for _, act in stages:
                act(cin, cout, sems)
            return
        step = 0
        for d in range(len(grid)):
            step = step * grid[d] + pl.program_id(d)
        for at, act in stages:
            if at == 0:
                pl.when(step == 0)(functools.partial(act, cin, cout, sems))
        body(*head, *ins, *outs, *scr)
        for at, act in stages:
            if at > 0:
                pl.when(step == at)(functools.partial(act, cin, cout, sems))

    params = dict(vmem_limit_bytes=VMEM_LIMIT)
    if grid:
        params["dimension_semantics"] = ("arbitrary",) * len(grid)
    specs = dict(grid=grid, in_specs=list(in_specs) + [ANY] * c_in, out_specs=list(out_specs) + [ANY] * c_out,
                 scratch_shapes=list(scratch) + cargo.sems)
    if prefetch is not None:
        specs = dict(grid_spec=pltpu.PrefetchScalarGridSpec(num_scalar_prefetch=1, **specs))
    head = () if prefetch is None else (prefetch,)
    res = pl.pallas_call(wrapped, name=name, out_shape=list(out_shape) + cargo.out_shape,
                         input_output_aliases={len(head) + n_in + i: n_out + o for i, o in cargo.aliased},
                         compiler_params=pltpu.CompilerParams(**params), **specs)
    return lambda *args: (lambda r: (r[:n_out], cargo.results(r[n_out:])))(res(*head, *args, *cargo.inputs))


def _split_halves(g):
    return g.reshape(g.shape[0], 2, g.shape[1] // 2, g.shape[2])


def _ew_tile(rows):
    return rows if rows <= 384 else 256


def _batched_call(items, name, place=None, cargo=None):
    offs, total = [], 0
    for it in items:
        offs.append(total)
        total += it[0]
    in_specs, out_specs, out_shape, args, spans = [], [], [], [], []
    for (n, ins, outs, shapes, operands, _), off in zip(items, offs):
        tile = lambda t, off=off, n=n: jnp.clip(t - off, 0, n - 1)
        i_s, o_s = ins(tile), outs(tile)
        spans.append((len(i_s), len(o_s)))
        in_specs += i_s
        out_specs += o_s
        out_shape += shapes
        args += operands
    n_in = len(in_specs)

    def body(*refs):
        if place is not None:
            refs = refs[1:]
        t = pl.program_id(0)
        i0, o0 = 0, n_in
        for (n, _, _, _, _, compute), off, (ni, no) in zip(items, offs, spans):
            pl.when((t >= off) & (t < off + n))(functools.partial(compute, refs[i0:i0 + ni], refs[o0:o0 + no], t - off))
            i0, o0 = i0 + ni, o0 + no

    params = pltpu.CompilerParams(dimension_semantics=("arbitrary",), vmem_limit_bytes=VMEM_LIMIT)
    carried = None
    if cargo is not None:
        res, carried = _cargo_call(body, cargo, name=name, grid=(total,), in_specs=in_specs, out_specs=out_specs,
                                   out_shape=out_shape, prefetch=place)(*args)
    elif place is None:
        res = pl.pallas_call(body, name=name, grid=(total,), in_specs=in_specs, out_specs=out_specs,
                             out_shape=out_shape, compiler_params=params)(*args)
    else:
        res = pl.pallas_call(
            body, name=name, out_shape=out_shape, compiler_params=params,
            grid_spec=pltpu.PrefetchScalarGridSpec(num_scalar_prefetch=1, grid=(total,), in_specs=in_specs,
                                                   out_specs=out_specs))(place, *args)
    out, o0 = [], 0
    for _, no in spans:
        out.append(res[o0:o0 + no])
        o0 += no
    return out if cargo is None else (out, carried)


def _add_pairs(pairs, place, name):
    def item(mine, theirs):
        lead, _, rows, cols = mine.shape
        tm = _ew_tile(rows)

        def compute(ins, outs, tile):
            outs[0][...] = (ins[0][...].astype(f32) + ins[1][...].astype(f32)).astype(outs[0].dtype)

        blk = lambda tile: pl.BlockSpec((lead, tm, cols), lambda t, pc: (0, tile(t), 0))
        mine_half = lambda tile: pl.BlockSpec((lead, None, tm, cols), lambda t, pc: (0, pc[1], tile(t), 0))
        return (rows // tm, lambda tile: [mine_half(tile), blk(tile)], lambda tile: [blk(tile)],
                [jax.ShapeDtypeStruct(theirs.shape, theirs.dtype)], [mine, theirs], compute)

    return [out[0] for out in _batched_call([item(a, b) for a, b in pairs], name, place)]


def _cargo_alone(cargo, name):
    return _cargo_call(None, cargo, name=name, grid=(), in_specs=[], out_specs=[], out_shape=[])()[1]


def _bf16_casts(x, mats, padded, cargo, name):
    def item(a, to_rows, tm):
        rows, cols = a.shape
        full = rows // tm

        def compute(ins, outs, tile):
            outs[0][...] = jnp.where(tile < full, ins[0][...], 0.0).astype(bf16)

        return (to_rows // tm, lambda tile: [pl.BlockSpec((tm, cols), lambda t: (jnp.minimum(tile(t), full - 1), 0))],
                lambda tile: [pl.BlockSpec((tm, cols), lambda t: (tile(t), 0))],
                [jax.ShapeDtypeStruct((to_rows, cols), bf16)], [a], compute)

    items = [item(x, x.shape[0], 128)] + [item(w, r, 64 if r > w.shape[0] else 128) for w, r in zip(mats, padded)]
    outs, carried = _batched_call(items, name, cargo=cargo)
    return [o[0] for o in outs], carried


def _sum_slots_items(jobs, max_rows=None):
    def item(own, recv, whole):
        _, rows, cols = own.shape
        tm = _ew_tile(rows)
        while max_rows is not None and tm > max_rows and tm % 16 == 0:
            tm //= 2

        def compute(ins, outs, tile):
            r0, r1, r2 = [ins[1][k].astype(f32) for k in range(N_CHIPS - 1)]
            outs[0][...] = (ins[0][...].astype(f32) + r0) + (r1 + r2)

        ins = lambda tile: [pl.BlockSpec((None, tm, cols), lambda t, pc: (0 if whole else pc[0], tile(t), 0)),
                            pl.BlockSpec((N_CHIPS - 1, tm, cols), lambda t, pc: (0, tile(t), 0))]
        outs = lambda tile: [pl.BlockSpec((None, tm, cols), lambda t, pc: (pc[1], tile(t), 0))]
        return rows // tm, ins, outs, [jax.ShapeDtypeStruct((2, rows, cols), f32)], [own, recv], compute

    return [item(*job) for job in jobs]


class _GradReducer:
    def __init__(self, place):
        self.place, self.split, self.sums, self.slots, self.bufs, self.full = place, {}, {}, {}, {}, {}

    def swap(self, named):
        self.split.update({nm: _split_halves(g) for nm, g in named})
        return _SwapCargo([self.split[nm] for nm, _ in named])

    def summed(self, names, got, tag):
        sums = _add_pairs([(self.split[nm], g) for nm, g in zip(names, got)], self.place, "presum_" + tag)
        self.sums.update(zip(names, sums))

    def exchange(self, names):
        return _ExchangeCargo([self.sums[nm] for nm in names], [nm == "small" for nm in names])

    def arrived(self, names, slots):
        self.slots.update(zip(names, slots))

    def tree_sum_items(self, names, max_rows=None):
        return _sum_slots_items([(self.sums[nm], self.slots[nm], nm == "small") for nm in names], max_rows)

    def tree_summed(self, names, bufs):
        self.bufs.update(zip(names, bufs))

    def fill(self, names):
        return _FillCargo([self.bufs[nm] for nm in names])

    def filled(self, names, grads):
        self.full.update(zip(names, grads))

    def reduce(self, names):
        rest = [nm for nm in names if nm not in self.bufs]
        outs = _batched_call(self.tree_sum_items(rest), "sum_slots", self.place)
        self.tree_summed(rest, [out[0] for out in outs])
        rest = [nm for nm in names if nm not in self.full]
        self.filled(rest, _cargo_alone(self.fill(rest), "sibling_fill"))
        return [self.full[nm] for nm in names]


def _adamw_update(w, m, v, g):
    m = ADAM_B1 * m + (1.0 - ADAM_B1) * g
    v = ADAM_B2 * v + (1.0 - ADAM_B2) * jnp.square(g)
    m_hat = m / (1.0 - ADAM_B1 ** ADAM_STEP)
    v_hat = v / (1.0 - ADAM_B2 ** ADAM_STEP)
    return -ADAM_LR * (m_hat / (jnp.sqrt(v_hat) + ADAM_EPS) + ADAM_WD * w), m, v


def _adamw(jobs, name):
    def item(w, m, v, grad, tm):
        rows, cols = w.shape
        gcols = grad.shape[1]

        def compute(ins, outs, tile):
            g = ins[3][0:tm, 0:cols]
            outs[0][...] = g
            outs[1][...], outs[2][...], outs[3][...] = _adamw_update(ins[0][...], ins[1][...], ins[2][...], g)

        blk = lambda tile: pl.BlockSpec((tm, cols), lambda t: (tile(t), 0))
        if tm == rows:
            gspec = lambda tile: pl.BlockSpec(grad.shape, lambda t: (0, 0))
        else:
            gspec = lambda tile: pl.BlockSpec((tm, gcols), lambda t: (tile(t), 0))
        return (rows // tm, lambda tile: [blk(tile)] * 3 + [gspec(tile)], lambda tile: [blk(tile)] * 4,
                [jax.ShapeDtypeStruct((rows, cols), f32)] * 4, [w, m, v, grad], compute)

    return _batched_call([item(*job) for job in jobs], name)


def _rope_tables(s):
    half = HEAD_DIM // 2
    inv_freq = 1.0 / (10000.0 ** (jnp.arange(half, dtype=f32) / half))
    ang = jnp.arange(s, dtype=f32)[:, None] * inv_freq[None, :]
    cos, sin = jnp.cos(ang), jnp.sin(ang)
    return jnp.tile(cos, (1, LANES // half)), jnp.tile(jnp.concatenate([-sin, sin], axis=1), (1, LANES // HEAD_DIM))


def _local_step(x, x_bf, target, win_g, gather, comm, b_gate, w_pool, pool_scale, ln1_g, ln1_b, convb_g, ln2_g, ln2_b):
    s = x.shape[0]
    nb = s // MOBA_BLOCK
    cos_t, sin_t = _rope_tables(s)

    z, (wba_g, wbp_g, wout_g) = _inproj(x_bf, win_g, cos_t, sin_t, b_gate, gather["inproj"])
    wout = wout_g.reshape(D_MODEL, D_MODEL)
    (o_bf, lse, pooled_bf, mixed, pm_bf), (wg_g, wu_g, convw_g) = _attn_fwd(z, nb, w_pool, pool_scale,
                                                                           gather["attn_fwd"])
    m_bf = _branch_merge(o_bf, pm_bf, z, wba_g, wbp_g)
    x1_bf, xhat1, rstd1 = _out_ln1(m_bf, wout, x, ln1_g, ln1_b)
    (a, up, ac, hh_bf), (wd_g,) = _ffn_up(x1_bf, wg_g, wu_g, convw_g, convb_g, gather["ffn_up"])
    wd = wd_g.reshape(D_FF_PAD, D_MODEL)
    loss, dh2, dh2_bf, d_ln2_g, d_ln2_b = _ffn_down_ln2_loss(hh_bf, wd, xhat1, ln1_g, ln1_b, ln2_g, ln2_b, target)

    da_bf, dup_bf, dconv = _ffn_act_bwd(dh2_bf, wd_g, ac, a, up, convw_g)
    ffn_shape = (N_CHIPS, FF_PAD, D_MODEL)
    d_wd = _matmul_tn(hh_bf, dh2_bf, 1, "dw_ffn_down").reshape(ffn_shape)
    d_wg, got = _matmul_tn(da_bf, x1_bf, 1, "dw_ffn_gate", cargo=comm.swap([("w_ffn_down", d_wd)]))
    comm.summed(["w_ffn_down"], got, "ffn_down")
    d_wg = d_wg.reshape(ffn_shape)
    d_wu = _matmul_tn(dup_bf, x1_bf, 1, "dw_ffn_up").reshape(ffn_shape)
    gate_up = ["w_ffn_gate", "w_ffn_up", "conv_w"]
    (dh1, dh1_bf, d_ln1_g, d_ln1_b), (slots, got) = _ffn_in_bwd_ln1(
        da_bf, dup_bf, wg_g, wu_g, dh2, xhat1, rstd1, ln1_g,
        _MultiCargo([comm.exchange(["w_ffn_down"]), comm.swap(list(zip(gate_up, [d_wg, d_wu, dconv])))]))
    comm.arrived(["w_ffn_down"], slots)
    comm.summed(gate_up, got, "ffn_gate_up")
    d_wout = _matmul_tn(m_bf, dh1_bf, 1, "dw_out").reshape(N_CHIPS, D_MODEL // N_CHIPS, D_MODEL)
    dzg_bf, dya_bf, dyp_bf, do_bf, dpm, d_bgate = _merge_bwd(dh1_bf, wout, o_bf, pm_bf, z, wba_g, wbp_g)
    d_wba = _matmul_tn(o_bf, dya_bf, N_CHIPS, "dw_branch_attn")
    d_wbp = _matmul_tn(pm_bf, dyp_bf, N_CHIPS, "dw_branch_pool")
    du_bf, d_wpool, d_pscale = _pool_bwd(dpm, mixed, pooled_bf, w_pool, pool_scale)
    d_convb = dconv[:, 3, :FF_SHARD].reshape(1, N_CHIPS * FF_SHARD)
    small = _pack_small([d_bgate, d_wpool.reshape(-1, POOL_GROUP), d_pscale, d_ln1_g, d_ln1_b, d_ln2_g, d_ln2_b,
                         d_convb], loss)[None]
    branch = ["w_out", "w_branch_attn", "w_branch_pool", "small"]
    (dqkv_bf,), (slots, got) = _attn_bwd(
        z, o_bf, lse, do_bf, cos_t, sin_t, nb,
        _MultiCargo([comm.exchange(gate_up), comm.swap(list(zip(branch, [d_wout, d_wba, d_wbp, small])))]))
    comm.arrived(gate_up, slots)
    comm.summed(branch, got, "branch")
    d_win, slots = _dw_in(x_bf, dqkv_bf, du_bf, dzg_bf, comm.exchange(branch))
    comm.arrived(branch, slots)
    ffn = ["w_ffn_down"] + gate_up
    bufs, got = _batched_call(comm.tree_sum_items(ffn), "swap_w_in", comm.place, comm.swap([("w_in", d_win)]))
    comm.tree_summed(ffn, [b[0] for b in bufs])
    comm.summed(["w_in"], got, "w_in")
    (grad_x, bufs), (slots, full) = _inproj_dx(
        dqkv_bf, du_bf, dzg_bf, win_g, dh1, comm.tree_sum_items(branch, 192), comm.place,
        _MultiCargo([comm.exchange(["w_in"]), comm.fill(ffn)]))
    comm.tree_summed(branch, bufs)
    comm.filled(ffn, full)
    comm.arrived(["w_in"], slots)
    return grad_x


SMALL_WEIGHTS = [("b_gate", 2 * D_MODEL), ("w_pool", 4 * POOL_GROUP * POOL_GROUP), ("pool_scale", D_POOL),
                 ("ln1_g", D_MODEL), ("ln1_b", D_MODEL), ("ln2_g", D_MODEL), ("ln2_b", D_MODEL),
                 ("conv_b", N_CHIPS * FF_SHARD)]
LOSS_ROW = 588
SMALL_ROWS = 592


def _small_pieces(ref, first_row, n):
    if ref.shape[0] > 1:
        return [(slice(first_row, first_row + n // LANES), (slice(None), slice(None)))]
    return [(slice(first_row + i, first_row + i + 1), (slice(None), slice(i * LANES, (i + 1) * LANES)))
            for i in range(n // LANES)]


def _small_first_rows():
    rows, r = [], 0
    for _, n in SMALL_WEIGHTS:
        rows.append(r)
        r += n // LANES
    return rows


def _pack_small(parts, loss):
    def body(*refs):
        out = refs[-1]
        out[...] = jnp.zeros_like(out)
        for src, r0, (_, n) in zip(refs, _small_first_rows(), SMALL_WEIGHTS):
            for rows, where in _small_pieces(src, r0, n):
                out[rows, :] = src[where]
        out[LOSS_ROW:LOSS_ROW + 1, :] = refs[len(parts)][0:1, :]

    return pl.pallas_call(body, name="pack_small", out_shape=jax.ShapeDtypeStruct((SMALL_ROWS, LANES), f32))(
        *parts, loss)


def _adamw_small(g_packed, g_conv, triples):
    n_in = 3 * len(triples)

    def body(g_ref, gc_ref, *refs):
        def update(k, g, where):
            w_ref, m_ref, v_ref = refs[3 * k:3 * k + 3]
            new = (g,) + _adamw_update(w_ref[where], m_ref[where], v_ref[where], g)
            for o_ref, val in zip(refs[n_in + 4 * k:n_in + 4 * k + 4], new):
                o_ref[where] = val

        for k, (r0, (_, n)) in enumerate(zip(_small_first_rows(), SMALL_WEIGHTS)):
            for rows, where in _small_pieces(refs[3 * k], r0, n):
                update(k, g_ref[rows, :], where)
        for tap in range(3):
            update(len(SMALL_WEIGHTS), gc_ref[tap:tap + 1, 0:FF_SHARD], (tap,))

    res = pl.pallas_call(
        body, name="adamw_small",
        out_shape=[jax.ShapeDtypeStruct(t[0].shape, f32) for t in triples for _ in range(4)],
    )(g_packed, g_conv, *[a for t in triples for a in t])
    return [res[4 * k:4 * k + 4] for k in range(len(triples))]


def _pad_conv_b(cb):
    return jnp.pad(cb.reshape(N_CHIPS, FF_SHARD), ((0, 0), (0, FF_PAD - FF_SHARD)))


def kernel(x, w_in, b_gate, w_branch_attn, w_pool, pool_scale, w_branch_pool, w_out, ln1_g, ln1_b, w_ffn_gate, w_ffn_up, conv_w, conv_b, w_ffn_down, ln2_g, ln2_b, loss_target, m_w_in, m_b_gate, m_w_branch_attn, m_w_pool, m_pool_scale, m_w_branch_pool, m_w_out, m_ln1_g, m_ln1_b, m_w_ffn_gate, m_w_ffn_up, m_conv_w, m_conv_b, m_w_ffn_down, m_ln2_g, m_ln2_b, v_w_in, v_b_gate, v_w_branch_attn, v_w_pool, v_pool_scale, v_w_branch_pool, v_w_out, v_ln1_g, v_ln1_b, v_w_ffn_gate, v_w_ffn_up, v_conv_w, v_conv_b, v_w_ffn_down, v_ln2_g, v_ln2_b):
    mats = [w_branch_attn[0], w_branch_pool[0], w_out[0], w_ffn_gate[0].T, w_ffn_up[0].T, w_ffn_down[0]]
    padded = [w.shape[0] for w in mats[:3]] + [FF_PAD] * 3
    (x_bf, *casted), (win_g,) = _bf16_casts(x[0], mats, padded, _GatherCargo([w_in[0].astype(bf16)], [0.6]),
                                            "gather_w_in")
    shards = [None] + casted + [jnp.pad(conv_w[0], ((0, CONV_ROWS - 3), (0, FF_PAD - FF_SHARD)))]
    gather = {"inproj": _GatherCargo(shards[1:4], [0.45, 0.5, 0.55]),
              "attn_fwd": _GatherCargo(shards[4:6] + shards[7:], [0.65, 0.65, 0.65]),
              "ffn_up": _GatherCargo(shards[6:7], [0.5])}

    place = jnp.stack([2 * lax.axis_index("x") + lax.axis_index("y"), lax.axis_index("c")]).astype(jnp.int32)
    comm = _GradReducer(place)
    convb_g = _pad_conv_b(conv_b).reshape(N_CHIPS, 1, FF_PAD)
    grad_x = _local_step(
        x[0], x_bf, loss_target[0], win_g, gather, comm, b_gate, w_pool[0], pool_scale, ln1_g, ln1_b, convb_g, ln2_g, ln2_b)
    names = ["w_in", "w_branch_attn", "w_branch_pool", "w_out", "w_ffn_gate", "w_ffn_up", "w_ffn_down", "conv_w", "small"]
    grads = comm.reduce(names)

    weights = [w_in[0], w_branch_attn[0], w_branch_pool[0], w_out[0], w_ffn_gate[0].T, w_ffn_up[0].T, w_ffn_down[0]]
    m_in = [m_w_in[0], m_w_branch_attn[0], m_w_branch_pool[0], m_w_out[0], m_w_ffn_gate[0].T, m_w_ffn_up[0].T,
            m_w_ffn_down[0]]
    v_in = [v_w_in[0], v_w_branch_attn[0], v_w_branch_pool[0], v_w_out[0], v_w_ffn_gate[0].T, v_w_ffn_up[0].T,
            v_w_ffn_down[0]]
    tiles = [256, 256, 256, 128, 176, 176, 176]
    jobs = list(zip(weights, m_in, v_in, grads, tiles))
    updated = _adamw(jobs[:4], "adamw_rest") + _adamw(jobs[4:], "adamw_ffn")
    res = {nm: [(r.T if nm in ("w_ffn_gate", "w_ffn_up") else r)[None] for r in outs4]
           for nm, outs4 in zip(names, updated)}

    small_w = [b_gate, w_pool, pool_scale, ln1_g, ln1_b, ln2_g, ln2_b, conv_b, conv_w]
    small_m = [m_b_gate, m_w_pool, m_pool_scale, m_ln1_g, m_ln1_b, m_ln2_g, m_ln2_b, m_conv_b, m_conv_w]
    small_v = [v_b_gate, v_w_pool, v_pool_scale, v_ln1_g, v_ln1_b, v_ln2_g, v_ln2_b, v_conv_b, v_conv_w]

    def as_kernel_sees(a):
        return a.reshape(-1, POOL_GROUP) if a.ndim == 4 else a.transpose(1, 0, 2) if a.ndim == 3 else a

    small_out = _adamw_small(grads[-1], grads[-2],
                             [tuple(as_kernel_sees(a) for a in t) for t in zip(small_w, small_m, small_v)])
    for nm, w, outs4 in zip([nm for nm, _ in SMALL_WEIGHTS] + ["conv_w"], small_w, small_out):
        res[nm] = [o.transpose(1, 0, 2) if w.ndim == 3 else o.reshape(w.shape) for o in outs4]
    loss = grads[-1][LOSS_ROW, 0]

    order = ["w_in", "b_gate", "w_branch_attn", "w_pool", "pool_scale", "w_branch_pool", "w_out", "ln1_g", "ln1_b",
             "w_ffn_gate", "w_ffn_up", "conv_w", "conv_b", "w_ffn_down", "ln2_g", "ln2_b"]
    outs = [loss, grad_x[None]]
    for kind in range(4):
        outs += [res[nm][kind] for nm in order]
    return tuple(outs)
```

```python
import functools

import jax
import jax.numpy as jnp
from jax import lax
from jax.experimental import pallas as pl
from jax.experimental.pallas import tpu as pltpu

D_MODEL = 1024
HEAD_DIM = 64
D_ATTN = 512
D_POOL = 512
MOBA_BLOCK = 256
MOBA_TOPK = 3
POOL_GROUP = 128
MAX_WINDOW = 16
FF_SHARD = 704
FF_PAD = 768
D_FF_PAD = 4 * FF_PAD
N_CHIPS = 4
LANES = 128
ALPHA = (2.0 * 1) ** 0.25
LN_EPS = 1e-5
NEG = -1e30
SCALE = HEAD_DIM ** -0.5
ADAM_LR, ADAM_B1, ADAM_B2, ADAM_EPS, ADAM_WD, ADAM_STEP = 0.001, 0.9, 0.999, 1e-08, 0.01, 10
VMEM_LIMIT = 56 * 1024 * 1024
MESH = pl.DeviceIdType.MESH

bf16 = jnp.bfloat16
f32 = jnp.float32


def _dot(a, b):
    return jnp.dot(a, b, preferred_element_type=f32)


def _dot_nt(a, b):
    return lax.dot_general(a, b, (((1,), (1,)), ((), ())), preferred_element_type=f32)


def _dot_tn(a, b):
    return lax.dot_general(a, b, (((0,), (0,)), ((), ())), preferred_element_type=f32)


def _dot_shards(a, w_ref):
    return jnp.concatenate([_dot(a, w_ref[p]) for p in range(w_ref.shape[0])], axis=1)


def _dot_nt_shards(d, w_ref):
    n = w_ref.shape[2]
    acc = _dot_nt(d[:, 0:n], w_ref[0])
    for p in range(1, w_ref.shape[0]):
        acc = acc + _dot_nt(d[:, p * n:(p + 1) * n], w_ref[p])
    return acc


def _column_shards_spec(k, n):
    return pl.BlockSpec((N_CHIPS, k, n // N_CHIPS), lambda i: (0, 0, 0))


def _call(body, *, name, grid, in_specs, out_specs, out_shape, scratch=(), cargo=None):
    if cargo is not None:
        return _cargo_call(body, cargo, name=name, grid=grid, in_specs=in_specs, out_specs=out_specs,
                           out_shape=out_shape, scratch=scratch)
    return pl.pallas_call(
        body, name=name, grid=grid, in_specs=in_specs, out_specs=out_specs, out_shape=out_shape,
        scratch_shapes=list(scratch),
        compiler_params=pltpu.CompilerParams(dimension_semantics=("arbitrary",) * len(grid),
                                             vmem_limit_bytes=VMEM_LIMIT))


def _rot_half(t):
    lane = lax.broadcasted_iota(jnp.int32, t.shape, 1)
    first = (lane % HEAD_DIM) < (HEAD_DIM // 2)
    return jnp.where(first, pltpu.roll(t, LANES - HEAD_DIM // 2, 1), pltpu.roll(t, HEAD_DIM // 2, 1))


def _rope(t, cos, sin_signed):
    return t * cos + _rot_half(t) * sin_signed


def _rope_bwd(d, cos, sin_signed):
    return d * cos + _rot_half(d * sin_signed)


def _gelu_parts(a):
    cdf = 0.5 * (1.0 + lax.erf(a * (2.0 ** -0.5)))
    pdf = jnp.exp(-0.5 * a * a) * ((2.0 * jnp.pi) ** -0.5)
    return a * cdf, cdf + a * pdf


def _layer_norm(h, g, b):
    mu = jnp.mean(h, axis=-1, keepdims=True)
    xc = h - mu
    var = jnp.mean(xc * xc, axis=-1, keepdims=True)
    rstd = lax.rsqrt(var + LN_EPS)
    xhat = xc * rstd
    return xhat * g + b, xhat, rstd


def _layer_norm_bwd(dy, xhat, rstd, g):
    dxh = dy * g
    m1 = jnp.mean(dxh, axis=-1, keepdims=True)
    m2 = jnp.mean(dxh * xhat, axis=-1, keepdims=True)
    return rstd * (dxh - m1 - xhat * m2)


def _inproj(x_bf, win_g, cos_t, sin_t, b_gate, cargo):
    s = x_bf.shape[0]
    tm, tn = 1024, 512

    def body(x_ref, w_ref, cos_ref, sin_ref, b_ref, o_ref):
        j = pl.program_id(1)
        acc = _dot(x_ref[...], w_ref[...])

        @pl.when(j < 2)
        def _():
            for c in range(tn // LANES):
                sl = slice(c * LANES, (c + 1) * LANES)
                o_ref[:, sl] = _rope(acc[:, sl], cos_ref[...], sin_ref[...])

        @pl.when((j >= 2) & (j < 4))
        def _():
            o_ref[...] = acc

        @pl.when(j >= 4)
        def _():
            o_ref[...] = jax.nn.sigmoid(acc + b_ref[...])

    (z,), carried = _cargo_call(
        body, cargo, name="inproj", grid=(s // tm, 4 * D_MODEL // tn),
        in_specs=[pl.BlockSpec((tm, D_MODEL), lambda i, j: (i, 0)),
                  pl.BlockSpec((None, D_MODEL, tn), lambda i, j: (j // 2, 0, j % 2)),
                  pl.BlockSpec((tm, LANES), lambda i, j: (i, 0)),
                  pl.BlockSpec((tm, LANES), lambda i, j: (i, 0)),
                  pl.BlockSpec((1, tn), lambda i, j: (0, jnp.maximum(j - 4, 0)))],
        out_specs=[pl.BlockSpec((tm, tn), lambda i, j: (i, j))],
        out_shape=[jax.ShapeDtypeStruct((s, 4 * D_MODEL), f32)])(x_bf, win_g, cos_t, sin_t, b_gate)
    return z, carried


STAT_ROWS = 8


def _pair_rows(v0, v1, fill):
    r = lax.broadcasted_iota(jnp.int32, (STAT_ROWS, v0.shape[1]), 0)
    return jnp.where(r == 0, v0, jnp.where(r == 1, v1, fill))


def _head_lanes(shape):
    lane = lax.broadcasted_iota(jnp.int32, shape, 1)
    return lane < HEAD_DIM, lane >= HEAD_DIM


def _head_rows(shape):
    row = lax.broadcasted_iota(jnp.int32, shape, 0)
    return row < HEAD_DIM, row >= HEAD_DIM


def _moba_select(q_bf, k_all, i, nb):
    k_mean = jnp.mean(k_all.reshape(nb, MOBA_BLOCK, LANES), axis=1)
    n_io = lax.broadcasted_iota(jnp.int32, (nb, MOBA_BLOCK), 0)
    past = n_io < i
    sels = []
    for head in _head_lanes((nb, LANES)):
        gate = _dot_nt(jnp.where(head, k_mean, 0.0).astype(bf16), q_bf)
        g = jnp.where(past, gate, NEG)
        rank = jnp.zeros((nb, MOBA_BLOCK), f32)
        for m in range(nb):
            gm = g[m:m + 1, :]
            rank = rank + jnp.where((gm > g) | ((gm == g) & (m < n_io)), 1.0, 0.0)
        sels.append(jnp.where(past & (rank < MOBA_TOPK), 1.0, 0.0))
    return sels


def _attn_fwd(z, nb, w_pool, pool_scale, cargo):
    s = z.shape[0]

    def body(q_ref, k_ref, v_ref, u_ref, wp_ref, sc_ref, o_ref, lse_ref, pooled_ref, mixed_ref, pm_ref, vt_s, pad):
        pair, i = pl.program_id(0), pl.program_id(1)

        @pl.when(i == 0)
        def _():
            for j in range(nb):
                vt_s[j] = v_ref[j * MOBA_BLOCK:(j + 1) * MOBA_BLOCK, :].T.astype(bf16)
            _pool_group(pair, u_ref, wp_ref, sc_ref, pooled_ref, mixed_ref, pm_ref, pad)

        k_heads = _head_lanes((MOBA_BLOCK, LANES))
        o_heads = _head_rows((LANES, MOBA_BLOCK))

        def block(j, qs, allow, m_old, l_old):
            kj, vtj = k_ref[j * MOBA_BLOCK:(j + 1) * MOBA_BLOCK, :], vt_s[j]
            m_new, alpha, l_new, pv = [], [], [], None
            for h in range(2):
                sc = jnp.where(allow[h], _dot_nt(jnp.where(k_heads[h], kj, 0.0).astype(bf16), qs), NEG)
                mx = jnp.max(sc, axis=0, keepdims=True)
                mn = mx if m_old is None else jnp.maximum(m_old[h], mx)
                p = jnp.exp(sc - mn)
                lsum = jnp.sum(p, axis=0, keepdims=True)
                if m_old is None:
                    a = None
                else:
                    a = jnp.exp(m_old[h] - mn)
                    lsum = a * l_old[h] + lsum
                t = _dot(jnp.where(o_heads[h], vtj, jnp.zeros_like(vtj)), p.astype(bf16))
                pv = t if pv is None else pv + t
                m_new.append(mn)
                alpha.append(a)
                l_new.append(lsum)
            return m_new, alpha, l_new, pv

        def query_block(c):
            q = q_ref[...]
            sels = _moba_select(q.astype(bf16), k_ref[...], c, nb)
            qs = (q * SCALE).astype(bf16)
            key = lax.broadcasted_iota(jnp.int32, (MOBA_BLOCK, MOBA_BLOCK), 0)
            qry = lax.broadcasted_iota(jnp.int32, (MOBA_BLOCK, MOBA_BLOCK), 1)
            m, _, l, acc = block(c, qs, [key <= qry] * 2, None, None)
            for j in range(c):
                m, alpha, l, pv = block(j, qs, [sels[h][j:j + 1, :] > 0.0 for h in range(2)], m, l)
                acc = acc * jnp.where(o_heads[0], alpha[0], alpha[1]) + pv
            o_ref[...] = (acc / jnp.where(o_heads[0], l[0], l[1])).T.astype(bf16)
            lse_ref[...] = _pair_rows(m[0] + jnp.log(l[0]), m[1] + jnp.log(l[1]), 0.0)

        for c in range(nb):
            pl.when(i == c)(functools.partial(query_block, c))

    group = pl.BlockSpec((s, POOL_GROUP), lambda hp, i: (0, hp))
    return _cargo_call(
        body, cargo, name="attn_fwd", grid=(D_ATTN // LANES, nb),
        in_specs=[pl.BlockSpec((MOBA_BLOCK, LANES), lambda hp, i: (i, hp)),
                  pl.BlockSpec((s, LANES), lambda hp, i: (0, 4 + hp)),
                  pl.BlockSpec((s, LANES), lambda hp, i: (0, 8 + hp)),
                  pl.BlockSpec((s, POOL_GROUP), lambda hp, i: (0, 12 + hp)),
                  pl.BlockSpec((None, POOL_GROUP, POOL_GROUP), lambda hp, i: (hp, 0, 0)),
                  pl.BlockSpec((1, POOL_GROUP), lambda hp, i: (0, hp))],
        out_specs=[pl.BlockSpec((MOBA_BLOCK, LANES), lambda hp, i: (i, hp)),
                   pl.BlockSpec((None, STAT_ROWS, MOBA_BLOCK), lambda hp, i: (hp, 0, i)), group, group, group],
        out_shape=[jax.ShapeDtypeStruct((s, D_ATTN), bf16),
                   jax.ShapeDtypeStruct((D_ATTN // LANES, STAT_ROWS, s), f32),
                   jax.ShapeDtypeStruct((s, D_POOL), bf16), jax.ShapeDtypeStruct((s, D_POOL), f32),
                   jax.ShapeDtypeStruct((s, D_POOL), bf16)],
        scratch=[pltpu.VMEM((nb, LANES, MOBA_BLOCK), bf16), pltpu.VMEM((s + MAX_WINDOW, POOL_GROUP), f32)],
    )(z, z, z, z, w_pool, pool_scale)


def _window_select(snaps, g):
    return jnp.where(g == 0, snaps[0], jnp.where(g == 1, snaps[1], jnp.where(g == 2, snaps[2], snaps[3])))


def _pool_group(g, u_ref, w_ref, sc_ref, pooled_ref, mixed_ref, pm_ref, pad):
    s = u_ref.shape[0]
    u = u_ref[...]
    pad[0:MAX_WINDOW, :] = jnp.zeros((MAX_WINDOW, POOL_GROUP), f32)
    pad[MAX_WINDOW:MAX_WINDOW + s, :] = u
    acc = u
    snaps = []
    for d in range(1, MAX_WINDOW):
        acc = acc + pad[MAX_WINDOW - d:MAX_WINDOW - d + s, :]
        if d + 1 in (2, 4, 8, 16):
            snaps.append(acc)
    win = _window_select(snaps, g)
    t = lax.broadcasted_iota(jnp.int32, (s, POOL_GROUP), 0)
    count = jnp.minimum(t + 1, jnp.left_shift(2, g)).astype(f32)
    pooled = (win / count - u).astype(bf16)
    mixed = _dot(pooled, w_ref[...].astype(bf16))
    pooled_ref[...] = pooled
    mixed_ref[...] = mixed
    pm_ref[...] = (mixed * sc_ref[...]).astype(bf16)


def _merge_out_ln1(o_bf, pm_bf, z, wba, wbp, wout, x, ln_g, ln_b):
    s = x.shape[0]
    tm = 512

    def body(o_ref, pm_ref, ga_ref, gp_ref, wba_ref, wbp_ref, w_ref, x_ref, g_ref, b_ref,
             m_ref, x1_ref, xhat_ref, rstd_ref):
        ya = _dot_shards(o_ref[...], wba_ref)
        yp = _dot_shards(pm_ref[...], wbp_ref)
        m = (ga_ref[...] * ya + gp_ref[...] * yp).astype(bf16)
        m_ref[...] = m
        h = ALPHA * x_ref[...] + _dot(m, w_ref[...])
        y, xhat, rstd = _layer_norm(h, g_ref[...], b_ref[...])
        x1_ref[...] = y.astype(bf16)
        xhat_ref[...] = xhat
        rstd_ref[...] = jnp.broadcast_to(rstd, (tm, LANES))

    row = pl.BlockSpec((tm, D_MODEL), lambda i: (i, 0))
    vec = pl.BlockSpec((1, D_MODEL), lambda i: (0, 0))
    return _call(
        body, name="merge_out_ln1", grid=(s // tm,),
        in_specs=[pl.BlockSpec((tm, D_ATTN), lambda i: (i, 0)), pl.BlockSpec((tm, D_POOL), lambda i: (i, 0)),
                  pl.BlockSpec((tm, D_MODEL), lambda i: (i, 2)), pl.BlockSpec((tm, D_MODEL), lambda i: (i, 3)),
                  _column_shards_spec(D_ATTN, D_MODEL), _column_shards_spec(D_POOL, D_MODEL),
                  pl.BlockSpec((D_MODEL, D_MODEL), lambda i: (0, 0)), row, vec, vec],
        out_specs=[row, row, row, pl.BlockSpec((tm, LANES), lambda i: (i, 0))],
        out_shape=[jax.ShapeDtypeStruct((s, D_MODEL), bf16), jax.ShapeDtypeStruct((s, D_MODEL), bf16),
                   jax.ShapeDtypeStruct((s, D_MODEL), f32), jax.ShapeDtypeStruct((s, LANES), f32)],
    )(o_bf, pm_bf, z, z, wba, wbp, wout, x, ln_g, ln_b)


FF_TILE = 256
FF_TILES_PER_SHARD = FF_PAD // FF_TILE
CONV_PAD = 8
CONV_ROWS = 16

def _ff_weight_spec(rows):
    return pl.BlockSpec((None, rows, FF_TILE), lambda j: (j // FF_TILES_PER_SHARD, 0, j % FF_TILES_PER_SHARD))


def _ff_rows_spec():
    return pl.BlockSpec((None, FF_TILE, D_MODEL), lambda j: (j // FF_TILES_PER_SHARD, j % FF_TILES_PER_SHARD, 0))


def _row_shifts(x, shifts):
    s = x.shape[0]
    padded = jnp.concatenate([x, jnp.zeros((CONV_PAD, x.shape[1]), x.dtype)], axis=0)
    return [pltpu.roll(padded, d % (s + CONV_PAD), 0)[0:s] for d in shifts]


def _ffn_up(x1_bf, wg_g, wu_g, convw_g, convb_g, cargo):
    s = x1_bf.shape[0]

    def body(x_ref, wg_ref, wu_ref, cw_ref, cb_ref, a_ref, up_ref, ac_ref, hh_ref):
        x1 = x_ref[...]
        a = _dot_nt(x1, wg_ref[...])
        up = _dot_nt(x1, wu_ref[...])
        a_ref[...] = a
        up_ref[...] = up
        a1, a2 = _row_shifts(a, (1, 2))
        ac = a * cw_ref[2:3, :] + a1 * cw_ref[1:2, :] + a2 * cw_ref[0:1, :] + cb_ref[...]
        ac_ref[...] = ac
        hg, _ = _gelu_parts(ac)
        hh_ref[...] = (hg * up).astype(bf16)

    col = pl.BlockSpec((s, FF_TILE), lambda j: (0, j))
    wide = jax.ShapeDtypeStruct((s, D_FF_PAD), f32)
    return _call(
        body, name="ffn_up", grid=(D_FF_PAD // FF_TILE,),
        in_specs=[pl.BlockSpec((s, D_MODEL), lambda j: (0, 0)), _ff_rows_spec(), _ff_rows_spec(),
                  _ff_weight_spec(8), _ff_weight_spec(1)],
        out_specs=[col, col, col, col],
        out_shape=[wide, wide, wide, jax.ShapeDtypeStruct((s, D_FF_PAD), bf16)],
        cargo=cargo)(x1_bf, wg_g, wu_g, convw_g, convb_g)


def _ffn_down_ln2_loss(hh_bf, wd, xhat1, ln1_g, ln1_b, ln2_g, ln2_b, target):
    s = hh_bf.shape[0]
    tm = 256

    def body(hh_ref, w_ref, xh1_ref, g1_ref, b1_ref, g2_ref, b2_ref, t_ref, loss_ref, dh_ref, dhb_ref, dg_ref, db_ref):
        i = pl.program_id(0)
        x1 = xh1_ref[...] * g1_ref[...] + b1_ref[...]
        h = ALPHA * x1 + _dot(hh_ref[...], w_ref[...])
        y, xhat, rstd = _layer_norm(h, g2_ref[...], b2_ref[...])
        err = y - t_ref[...]
        part = 0.5 * jnp.sum(jnp.mean(err * err, axis=-1, keepdims=True), axis=0, keepdims=True)
        dy = err * (1.0 / D_MODEL)

        @pl.when(i == 0)
        def _():
            loss_ref[...] = jnp.zeros_like(loss_ref)
            dg_ref[...] = jnp.zeros_like(dg_ref)
            db_ref[...] = jnp.zeros_like(db_ref)

        loss_ref[...] += jnp.broadcast_to(part, loss_ref.shape)
        dg_ref[...] += jnp.sum(dy * xhat, axis=0, keepdims=True)
        db_ref[...] += jnp.sum(dy, axis=0, keepdims=True)
        dh = _layer_norm_bwd(dy, xhat, rstd, g2_ref[...])
        dh_ref[...] = dh
        dhb_ref[...] = dh.astype(bf16)

    row = pl.BlockSpec((tm, D_MODEL), lambda i: (i, 0))
    vec = pl.BlockSpec((1, D_MODEL), lambda i: (0, 0))
    return _call(
        body, name="ffn_down_ln2_loss", grid=(s // tm,),
        in_specs=[pl.BlockSpec((tm, D_FF_PAD), lambda i: (i, 0)),
                  pl.BlockSpec((D_FF_PAD, D_MODEL), lambda i: (0, 0)),
                  row, vec, vec, vec, vec, row],
        out_specs=[pl.BlockSpec((8, LANES), lambda i: (0, 0)), row, row, vec, vec],
        out_shape=[jax.ShapeDtypeStruct((8, LANES), f32), jax.ShapeDtypeStruct((s, D_MODEL), f32),
                   jax.ShapeDtypeStruct((s, D_MODEL), bf16), jax.ShapeDtypeStruct((1, D_MODEL), f32),
                   jax.ShapeDtypeStruct((1, D_MODEL), f32)])(hh_bf, wd, xhat1, ln1_g, ln1_b, ln2_g, ln2_b, target)


def _ffn_act_bwd(dh2_bf, wd_g, ac, a, up, convw_g):
    s = dh2_bf.shape[0]

    def body(dh_ref, wd_ref, ac_ref, a_ref, up_ref, cw_ref, da_ref, dup_ref, dc_ref):
        dhh = _dot_nt(dh_ref[...], wd_ref[...])
        hg, dgelu = _gelu_parts(ac_ref[...])
        dup_ref[...] = (dhh * hg).astype(bf16)
        dac = dhh * up_ref[...] * dgelu
        dac1, dac2 = _row_shifts(dac, (-1, -2))
        a = a_ref[...]
        rows = [jnp.sum(d * a, axis=0, keepdims=True) for d in (dac2, dac1, dac)]
        rows.append(jnp.sum(dac, axis=0, keepdims=True))
        rows.append(jnp.zeros((CONV_ROWS - 4, FF_TILE), f32))
        dc_ref[...] = jnp.concatenate(rows, axis=0)
        da_ref[...] = (dac * cw_ref[2:3, :] + dac1 * cw_ref[1:2, :] + dac2 * cw_ref[0:1, :]).astype(bf16)

    col = pl.BlockSpec((s, FF_TILE), lambda j: (0, j))
    return _call(
        body, name="ffn_act_bwd", grid=(D_FF_PAD // FF_TILE,),
        in_specs=[pl.BlockSpec((s, D_MODEL), lambda j: (0, 0)), _ff_rows_spec(), col, col, col, _ff_weight_spec(8)],
        out_specs=[col, col, _ff_weight_spec(CONV_ROWS)],
        out_shape=[jax.ShapeDtypeStruct((s, D_FF_PAD), bf16), jax.ShapeDtypeStruct((s, D_FF_PAD), bf16),
                   jax.ShapeDtypeStruct((N_CHIPS, CONV_ROWS, FF_PAD), f32)],
    )(dh2_bf, wd_g, ac, a, up, convw_g)


def _matmul_tn(a, b, n_shards, name, tm=512, tn=1024, cargo=None):
    k, m = a.shape
    n = b.shape[1]
    tm, tn = min(tm, m), min(tn, n // n_shards)
    per = n // n_shards // tn

    def body(a_ref, b_ref, o_ref, at_s):
        @pl.when(pl.program_id(1) == 0)
        def _():
            at_s[...] = a_ref[...].T

        o_ref[...] = _dot(at_s[...], b_ref[...]).astype(bf16)

    res = _call(
        body, name=name, grid=(m // tm, n // tn),
        in_specs=[pl.BlockSpec((k, tm), lambda i, j: (0, i)), pl.BlockSpec((k, tn), lambda i, j: (0, j))],
        out_specs=[pl.BlockSpec((None, tm, tn), lambda i, j: (j // per, i, j % per))],
        out_shape=[jax.ShapeDtypeStruct((n_shards, m, n // n_shards), bf16)], scratch=[pltpu.VMEM((tm, k), bf16)],
        cargo=cargo)(a, b)
    return res[0] if cargo is None else (res[0][0], res[1])


def _dw_in(x_bf, dqkv, du, dzg, cargo):
    k, m = x_bf.shape
    tm, half = 512, D_ATTN

    def body(a_ref, dqkv_ref, du_ref, dzg_ref, o_ref, at_s):
        j = pl.program_id(1)

        @pl.when(j == 0)
        def _():
            at_s[...] = a_ref[...].T
            o_ref[:, 0:half] = _dot(at_s[...], dqkv_ref[0]).astype(bf16)
            o_ref[:, half:2 * half] = _dot(at_s[...], dqkv_ref[1]).astype(bf16)

        @pl.when(j == 1)
        def _():
            o_ref[:, 0:half] = _dot(at_s[...], dqkv_ref[2]).astype(bf16)
            o_ref[:, half:2 * half] = _dot(at_s[...], du_ref[...]).astype(bf16)

        @pl.when(j >= 2)
        def _():
            o_ref[...] = _dot(at_s[...], dzg_ref[...]).astype(bf16)

    (d_win,), carried = _cargo_call(
        body, cargo, name="dw_in", grid=(m // tm, N_CHIPS),
        in_specs=[pl.BlockSpec((k, tm), lambda i, j: (0, i)), pl.BlockSpec((3, k, half), lambda i, j: (0, 0, 0)),
                  pl.BlockSpec((k, half), lambda i, j: (0, 0)),
                  pl.BlockSpec((k, D_MODEL), lambda i, j: (0, jnp.maximum(j - 2, 0)))],
        out_specs=[pl.BlockSpec((None, tm, D_MODEL), lambda i, j: (j, i, 0))],
        out_shape=[jax.ShapeDtypeStruct((N_CHIPS, m, D_MODEL), bf16)], scratch=[pltpu.VMEM((tm, k), bf16)],
    )(x_bf, dqkv, du, dzg)
    return d_win, carried


def _ffn_in_bwd_ln1(da_bf, dup_bf, wg_g, wu_g, dh2, xhat1, rstd1, ln1_g, cargo):
    s = da_bf.shape[0]
    tm = 256

    def body(da_ref, dup_ref, wg_ref, wu_ref, dh2_ref, xh_ref, rstd_ref, g_ref, dh_ref, dhb_ref, dg_ref, db_ref):
        i = pl.program_id(0)
        dx1 = ALPHA * dh2_ref[...]
        for sh in range(N_CHIPS):
            sl = slice(sh * FF_PAD, (sh + 1) * FF_PAD)
            dx1 = dx1 + _dot(da_ref[:, sl], wg_ref[sh]) + _dot(dup_ref[:, sl], wu_ref[sh])
        xhat = xh_ref[...]

        @pl.when(i == 0)
        def _():
            dg_ref[...] = jnp.zeros_like(dg_ref)
            db_ref[...] = jnp.zeros_like(db_ref)

        dg_ref[...] += jnp.sum(dx1 * xhat, axis=0, keepdims=True)
        db_ref[...] += jnp.sum(dx1, axis=0, keepdims=True)
        dh = _layer_norm_bwd(dx1, xhat, rstd_ref[:, 0:1], g_ref[...])
        dh_ref[...] = dh
        dhb_ref[...] = dh.astype(bf16)

    row = pl.BlockSpec((tm, D_MODEL), lambda i: (i, 0))
    wide = pl.BlockSpec((tm, D_FF_PAD), lambda i: (i, 0))
    wfull = pl.BlockSpec((N_CHIPS, FF_PAD, D_MODEL), lambda i: (0, 0, 0))
    vec = pl.BlockSpec((1, D_MODEL), lambda i: (0, 0))
    return _call(
        body, name="ffn_in_bwd_ln1", grid=(s // tm,),
        in_specs=[wide, wide, wfull, wfull, row, row, pl.BlockSpec((tm, LANES), lambda i: (i, 0)), vec],
        out_specs=[row, row, vec, vec],
        out_shape=[jax.ShapeDtypeStruct((s, D_MODEL), f32), jax.ShapeDtypeStruct((s, D_MODEL), bf16),
                   jax.ShapeDtypeStruct((1, D_MODEL), f32), jax.ShapeDtypeStruct((1, D_MODEL), f32)],
        cargo=cargo)(da_bf, dup_bf, wg_g, wu_g, dh2, xhat1, rstd1, ln1_g)


def _merge_bwd(dh1_bf, wout, o_bf, pm_bf, z, wba, wbp):
    s = dh1_bf.shape[0]
    tm = 256

    def body(dh_ref, wout_ref, o_ref, pm_ref, ga_ref, gp_ref, wba_ref, wbp_ref,
             dzg_ref, dya_ref, dyp_ref, do_ref, dpm_ref, dbg_ref):
        i = pl.program_id(0)
        dm = _dot_nt(dh_ref[...], wout_ref[...])
        ya = _dot_shards(o_ref[...], wba_ref)
        yp = _dot_shards(pm_ref[...], wbp_ref)
        ga, gp = ga_ref[...], gp_ref[...]
        dza = dm * ya * ga * (1.0 - ga)
        dzp = dm * yp * gp * (1.0 - gp)

        @pl.when(i == 0)
        def _():
            dbg_ref[...] = jnp.zeros_like(dbg_ref)

        dbg_ref[:, 0:D_MODEL] += jnp.sum(dza, axis=0, keepdims=True)
        dbg_ref[:, D_MODEL:2 * D_MODEL] += jnp.sum(dzp, axis=0, keepdims=True)
        dzg_ref[:, 0:D_MODEL] = dza.astype(bf16)
        dzg_ref[:, D_MODEL:2 * D_MODEL] = dzp.astype(bf16)
        dya = (dm * ga).astype(bf16)
        dyp = (dm * gp).astype(bf16)
        dya_ref[...] = dya
        dyp_ref[...] = dyp
        do_ref[...] = _dot_nt_shards(dya, wba_ref).astype(bf16)
        dpm_ref[...] = _dot_nt_shards(dyp, wbp_ref)

    row = pl.BlockSpec((tm, D_MODEL), lambda i: (i, 0))
    half = pl.BlockSpec((tm, D_ATTN), lambda i: (i, 0))
    full = lambda r, c: pl.BlockSpec((r, c), lambda i: (0, 0))
    return _call(
        body, name="merge_bwd", grid=(s // tm,),
        in_specs=[row, full(D_MODEL, D_MODEL), half, half,
                  pl.BlockSpec((tm, D_MODEL), lambda i: (i, 2)), pl.BlockSpec((tm, D_MODEL), lambda i: (i, 3)),
                  _column_shards_spec(D_ATTN, D_MODEL), _column_shards_spec(D_POOL, D_MODEL)],
        out_specs=[pl.BlockSpec((tm, 2 * D_MODEL), lambda i: (i, 0)), row, row, half, half,
                   pl.BlockSpec((1, 2 * D_MODEL), lambda i: (0, 0))],
        out_shape=[jax.ShapeDtypeStruct((s, 2 * D_MODEL), bf16), jax.ShapeDtypeStruct((s, D_MODEL), bf16),
                   jax.ShapeDtypeStruct((s, D_MODEL), bf16), jax.ShapeDtypeStruct((s, D_ATTN), bf16),
                   jax.ShapeDtypeStruct((s, D_POOL), f32), jax.ShapeDtypeStruct((1, 2 * D_MODEL), f32)],
    )(dh1_bf, wout, o_bf, pm_bf, z, z, wba, wbp)


def _pool_bwd(dpm, mixed, pooled_bf, w_pool, pool_scale):
    s = dpm.shape[0]

    def body(dpm_ref, mixed_ref, pooled_ref, w_ref, sc_ref, du_ref, dw_ref, dsc_ref, pad):
        g = pl.program_id(0)
        dpm_v = dpm_ref[...]
        dsc_ref[...] = jnp.sum(dpm_v * mixed_ref[...], axis=0, keepdims=True)
        dmixed = (dpm_v * sc_ref[...]).astype(bf16)
        dw_ref[...] = _dot_tn(pooled_ref[...], dmixed)
        dpooled = _dot_nt(dmixed, w_ref[...].astype(bf16))
        t = lax.broadcasted_iota(jnp.int32, (s, POOL_GROUP), 0)
        count = jnp.minimum(t + 1, jnp.left_shift(2, g)).astype(f32)
        r = dpooled / count
        pad[0:s, :] = r
        pad[s:s + MAX_WINDOW, :] = jnp.zeros((MAX_WINDOW, POOL_GROUP), f32)
        acc = r
        snaps = []
        for d in range(1, MAX_WINDOW):
            acc = acc + pad[d:d + s, :]
            if d + 1 in (2, 4, 8, 16):
                snaps.append(acc)
        du_ref[...] = (_window_select(snaps, g) - dpooled).astype(bf16)

    blk = pl.BlockSpec((s, POOL_GROUP), lambda g: (0, g))
    return _call(
        body, name="pool_bwd", grid=(4,),
        in_specs=[blk, blk, blk, pl.BlockSpec((None, POOL_GROUP, POOL_GROUP), lambda g: (g, 0, 0)),
                  pl.BlockSpec((1, POOL_GROUP), lambda g: (0, g))],
        out_specs=[blk, pl.BlockSpec((None, POOL_GROUP, POOL_GROUP), lambda g: (g, 0, 0)),
                   pl.BlockSpec((1, POOL_GROUP), lambda g: (0, g))],
        out_shape=[jax.ShapeDtypeStruct((s, D_POOL), bf16), jax.ShapeDtypeStruct((4, POOL_GROUP, POOL_GROUP), f32),
                   jax.ShapeDtypeStruct((1, D_POOL), f32)],
        scratch=[pltpu.VMEM((s + MAX_WINDOW, POOL_GROUP), f32)])(dpm, mixed, pooled_bf, w_pool, pool_scale)


def _attn_bwd(z, o_bf, lse, do_bf, cos_t, sin_t, nb, cargo):
    s = z.shape[0]

    def body(q_ref, k_ref, v_ref, o_ref, lse_ref, do_ref, cq_ref, sq_ref, cf_ref, sf_ref,
             dqkv_ref, kt_s, dk_s, dv_s):
        i = pl.program_id(1)

        @pl.when(i == 0)
        def _():
            dk_s[...] = jnp.zeros_like(dk_s)
            dv_s[...] = jnp.zeros_like(dv_s)
            for j in range(nb):
                kt_s[j] = k_ref[j * MOBA_BLOCK:(j + 1) * MOBA_BLOCK, :].T.astype(bf16)

        k_heads = _head_lanes((MOBA_BLOCK, LANES))
        t_heads = _head_rows((LANES, MOBA_BLOCK))

        def query_block(c):
            q = q_ref[...]
            sels = _moba_select(q.astype(bf16), k_ref[...], c, nb)
            qs = (q * SCALE).astype(bf16)
            qraw = [jnp.where(hm, q, 0.0).astype(bf16) for hm in k_heads]
            do = do_ref[...].astype(f32)
            do_bf = do_ref[...]
            dob = [jnp.where(hm, do, 0.0).astype(bf16) for hm in k_heads]
            pick = _head_lanes((STAT_ROWS, LANES))
            r8 = lax.broadcasted_iota(jnp.int32, (STAT_ROWS, LANES), 0)
            head_pick = jnp.where(((r8 == 0) & pick[0]) | ((r8 == 1) & pick[1]), 1.0, 0.0)
            delta8 = lax.dot_general(head_pick, do * o_ref[...].astype(f32), (((1,), (1,)), ((), ())),
                                     precision=lax.Precision.HIGHEST, preferred_element_type=f32)
            lse8 = lse_ref[...]
            delta, lse_h = [delta8[0:1, :], delta8[1:2, :]], [lse8[0:1, :], lse8[1:2, :]]
            key = lax.broadcasted_iota(jnp.int32, (MOBA_BLOCK, MOBA_BLOCK), 0)
            qry = lax.broadcasted_iota(jnp.int32, (MOBA_BLOCK, MOBA_BLOCK), 1)

            def block(j, allow):
                rows = slice(j * MOBA_BLOCK, (j + 1) * MOBA_BLOCK)
                kj, vj, ktj = k_ref[rows, :], v_ref[rows, :], kt_s[j]
                dqt, dk, dv = None, None, None
                for h in range(2):
                    sc = _dot_nt(jnp.where(k_heads[h], kj, 0.0).astype(bf16), qs)
                    p = jnp.where(allow[h], jnp.exp(sc - lse_h[h]), 0.0)
                    dp = _dot_nt(jnp.where(k_heads[h], vj, 0.0).astype(bf16), do_bf)
                    ds = (p * (dp - delta[h]) * SCALE).astype(bf16)
                    tv = _dot(p.astype(bf16), dob[h])
                    tk = _dot(ds, qraw[h])
                    tq = _dot(jnp.where(t_heads[h], ktj, jnp.zeros_like(ktj)), ds)
                    dqt, dk, dv = (tq, tk, tv) if dqt is None else (dqt + tq, dk + tk, dv + tv)
                dk_s[rows, :] += dk
                dv_s[rows, :] += dv
                return dqt

            dqt = block(c, [key <= qry] * 2)
            for j in range(c):
                dqt = dqt + block(j, [sels[h][j:j + 1, :] > 0.0 for h in range(2)])
            dqkv_ref[0, c * MOBA_BLOCK:(c + 1) * MOBA_BLOCK, :] = _rope_bwd(dqt.T, cq_ref[...], sq_ref[...]).astype(bf16)

        for c in range(nb):
            pl.when(i == c)(functools.partial(query_block, c))

        @pl.when(i == nb - 1)
        def _():
            dqkv_ref[1] = _rope_bwd(dk_s[...], cf_ref[...], sf_ref[...]).astype(bf16)
            dqkv_ref[2] = dv_s[...].astype(bf16)

    qblk = pl.BlockSpec((MOBA_BLOCK, LANES), lambda hp, i: (i, hp))
    tq = pl.BlockSpec((MOBA_BLOCK, LANES), lambda hp, i: (i, 0))
    tf = pl.BlockSpec((s, LANES), lambda hp, i: (0, 0))
    return _cargo_call(
        body, cargo, name="attn_bwd", grid=(D_ATTN // LANES, nb),
        in_specs=[qblk, pl.BlockSpec((s, LANES), lambda hp, i: (0, 4 + hp)),
                  pl.BlockSpec((s, LANES), lambda hp, i: (0, 8 + hp)), qblk,
                  pl.BlockSpec((None, STAT_ROWS, MOBA_BLOCK), lambda hp, i: (hp, 0, i)), qblk, tq, tq, tf, tf],
        out_specs=[pl.BlockSpec((3, s, LANES), lambda hp, i: (0, 0, hp))],
        out_shape=[jax.ShapeDtypeStruct((3, s, D_ATTN), bf16)],
        scratch=[pltpu.VMEM((nb, LANES, MOBA_BLOCK), bf16), pltpu.VMEM((s, LANES), f32), pltpu.VMEM((s, LANES), f32)],
    )(z, z, z, o_bf, lse, do_bf, cos_t, sin_t, cos_t, sin_t)


def _inproj_dx(dqkv, du, dzg, win_g, dh1, more_items, place, cargo):
    s = du.shape[0]
    tm, half = 256, D_ATTN

    def compute(ins, outs, tile):
        dqkv_ref, du_ref, dzg_ref, w_ref, dh_ref = ins
        acc = ALPHA * dh_ref[...]
        for n, part in enumerate([dqkv_ref[0], dqkv_ref[1], dqkv_ref[2], du_ref[...]]):
            acc = acc + _dot_nt(part, w_ref[n // 2, :, (n % 2) * half:(n % 2 + 1) * half])
        for sh in range(2, N_CHIPS):
            acc = acc + _dot_nt(dzg_ref[:, (sh - 2) * D_MODEL:(sh - 1) * D_MODEL], w_ref[sh])
        outs[0][...] = acc

    row = lambda tile: pl.BlockSpec((tm, D_MODEL), lambda t, pc: (tile(t), 0))
    ins = lambda tile: [pl.BlockSpec((3, tm, half), lambda t, pc: (0, tile(t), 0)),
                        pl.BlockSpec((tm, half), lambda t, pc: (tile(t), 0)),
                        pl.BlockSpec((tm, 2 * D_MODEL), lambda t, pc: (tile(t), 0)),
                        pl.BlockSpec((N_CHIPS, D_MODEL, D_MODEL), lambda t, pc: (0, 0, 0)), row(tile)]
    main = (s // tm, ins, lambda tile: [row(tile)], [jax.ShapeDtypeStruct((s, D_MODEL), f32)],
            [dqkv, du, dzg, win_g, dh1], compute)
    outs, carried = _batched_call([main] + more_items, "inproj_dx", place, cargo)
    return (outs[0][0], [o[0] for o in outs[1:]]), carried


ANY = pl.BlockSpec(memory_space=pl.ANY)


def _chip_index():
    return 2 * lax.axis_index("x") + lax.axis_index("y")


def _peer(k):
    x, y, c = lax.axis_index("x"), lax.axis_index("y"), lax.axis_index("c")
    return (x ^ (k >> 1), y ^ (k & 1), c)


def _sibling():
    return (lax.axis_index("x"), lax.axis_index("y"), 1 - lax.axis_index("c"))


class _GatherCargo:
    aliased = ()

    def __init__(self, shards, pass_on):
        self.shards, self.pass_on, n = shards, pass_on, len(shards)
        self.inputs = [a.reshape(2, a.shape[0] // 2, a.shape[1]) for a in shards]
        self.out_shape = [jax.ShapeDtypeStruct((N_CHIPS,) + a.shape, a.dtype) for a in self.inputs]
        two, three, one = [pltpu.SemaphoreType.DMA(shp) for shp in ((2, n), (N_CHIPS - 1, n), (n,))]
        self.sems = [two, two, two, two, three, three, one, one]

    def _relay_rows(self, a):
        h = self.inputs[a].shape[1]
        cut = h // 2 if h % 32 == 0 else h
        return {1: (0, cut), 2: (cut, h)}

    def _copies(self, src, dst, sems, a):
        send_sems, recv_sems, rsend_sems, rrecv_sems, fsend_sems, frecv_sems, local_sems, own_sems = sems
        p, c = _chip_index(), lax.axis_index("c")
        copy = lambda s, d, **kw: functools.partial(pltpu.make_async_remote_copy, src_ref=s, dst_ref=d, **kw)
        own = copy(src[a], dst[a].at[p], send_sem=local_sems.at[a], recv_sem=own_sems.at[a], device_id=_sibling(),
                   device_id_type=MESH)
        out, arrive, relay, relayed, onward, landed = {}, {}, {}, {}, {}, {}
        for k in range(1, N_CHIPS):
            d2d = dict(send_sem=fsend_sems.at[k - 1, a], recv_sem=frecv_sems.at[k - 1, a], device_id=_sibling(),
                       device_id_type=MESH)
            got, theirs = dst[a].at[p ^ k, c], dst[a].at[p ^ k, 1 - c]
            onward[k] = copy(got, got, **d2d)
            landed[k] = copy(theirs, theirs, **d2d)
        for k, (lo, hi) in self._relay_rows(a).items():
            ici = dict(send_sem=send_sems.at[k - 1, a], recv_sem=recv_sems.at[k - 1, a], device_id=_peer(k),
                       device_id_type=MESH)
            out[k] = copy(src[a].at[c], dst[a].at[p, c], **ici)
            arrive[k] = copy(src[a].at[c], dst[a].at[p ^ k, c], **ici)
            if hi > lo:
                via = dict(send_sem=rsend_sems.at[k - 1, a], recv_sem=rrecv_sems.at[k - 1, a], device_id=_peer(3 - k),
                           device_id_type=MESH)
                rows = dst[a].at[p ^ k, c, lo:hi]
                relay[k] = copy(rows, rows, **via)
                mine = dst[a].at[p ^ 3, c, lo:hi]
                relayed[k] = copy(mine, mine, **via)
        return own, out, arrive, relay, relayed, onward, landed

    def stages(self, steps):
        n = len(self.shards)

        def start(src, dst, sems):
            for a in range(n):
                own, out, _, _, _, _, _ = self._copies(src, dst, sems, a)
                own().start()
                for cp in out.values():
                    cp().start()

        def pass_on(a):
            def act(src, dst, sems):
                _, _, arrive, relay, _, onward, _ = self._copies(src, dst, sems, a)
                for k in arrive:
                    arrive[k]().wait_recv()
                    if k in relay:
                        relay[k]().start()
                    onward[k]().start()
            return act

        def finish(src, dst, sems):
            for a in range(n):
                _, _, _, _, relayed, onward, _ = self._copies(src, dst, sems, a)
                for cp in relayed.values():
                    cp().wait_recv()
                onward[3]().start()
            for a in range(n):
                own, out, _, relay, _, onward, landed = self._copies(src, dst, sems, a)
                for cp in landed.values():
                    cp().wait_recv()
                for cp in list(out.values()) + list(relay.values()) + list(onward.values()):
                    cp().wait_send()
                own().wait()

        mids = [(min(steps - 1, int(self.pass_on[a] * steps)), pass_on(a)) for a in range(n)]
        return [(0, start)] + mids + [(steps - 1, finish)]

    def results(self, outs):
        return [o.reshape((N_CHIPS,) + a.shape) for o, a in zip(outs, self.shards)]


class _ExchangeCargo:
    aliased = ()

    def __init__(self, sums, whole):
        self.inputs, self.whole, n = sums, whole, len(sums)
        self.out_shape = [jax.ShapeDtypeStruct((N_CHIPS - 1,) + g.shape[1:], g.dtype) for g in sums]
        sem = pltpu.SemaphoreType.DMA((N_CHIPS - 1, n))
        self.sems = [sem, sem]

    def _copies(self, src, dst, sems):
        send_sems, recv_sems = sems
        p = _chip_index()
        return [pltpu.make_async_remote_copy(
            src_ref=src[a].at[0] if self.whole[a] else src[a].at[p ^ k], dst_ref=dst[a].at[k - 1],
            send_sem=send_sems.at[k - 1, a], recv_sem=recv_sems.at[k - 1, a], device_id=_peer(k), device_id_type=MESH)
            for k in range(1, N_CHIPS) for a in range(len(self.inputs))]

    def stages(self, steps):
        def start(src, dst, sems):
            for cp in self._copies(src, dst, sems):
                cp.start()

        def finish(src, dst, sems):
            copies = self._copies(src, dst, sems)
            for cp in copies:
                cp.wait_recv()
            for cp in copies:
                cp.wait_send()

        return [(0, start), (steps - 1, finish)]

    def results(self, outs):
        return list(outs)


class _SwapCargo:
    aliased = ()

    def __init__(self, split):
        self.inputs, n = split, len(split)
        self.out_shape = [jax.ShapeDtypeStruct((g.shape[0],) + g.shape[2:], g.dtype) for g in split]
        sem = pltpu.SemaphoreType.DMA((n,))
        self.sems = [sem, sem]

    def _copies(self, src, dst, sems):
        c = lax.axis_index("c")
        return [pltpu.make_async_remote_copy(src_ref=src[a].at[:, 1 - c], dst_ref=dst[a], send_sem=sems[0].at[a],
                                             recv_sem=sems[1].at[a], device_id=_sibling(), device_id_type=MESH)
                for a in range(len(self.inputs))]

    def stages(self, steps):
        def start(src, dst, sems):
            for cp in self._copies(src, dst, sems):
                cp.start()

        def finish(src, dst, sems):
            copies = self._copies(src, dst, sems)
            for cp in copies:
                cp.wait_recv()
            for cp in copies:
                cp.wait_send()

        return [(0, start), (steps - 1, finish)]

    def results(self, outs):
        return list(outs)


class _FillCargo:
    def __init__(self, bufs):
        self.inputs, n = bufs, len(bufs)
        self.out_shape = [jax.ShapeDtypeStruct(b.shape, b.dtype) for b in bufs]
        self.aliased = [(a, a) for a in range(n)]
        sem = pltpu.SemaphoreType.DMA((n,))
        self.sems = [sem, sem]

    def _copies(self, src, dst, sems, landing):
        c = lax.axis_index("c")
        half = 1 - c if landing else c
        return [pltpu.make_async_remote_copy(src_ref=src[a].at[c], dst_ref=dst[a].at[half], send_sem=sems[0].at[a],
                                             recv_sem=sems[1].at[a], device_id=_sibling(), device_id_type=MESH)
                for a in range(len(self.inputs))]

    def stages(self, steps):
        def start(src, dst, sems):
            for cp in self._copies(src, dst, sems, False):
                cp.start()

        def finish(src, dst, sems):
            for cp in self._copies(src, dst, sems, True):
                cp.wait_recv()
            for cp in self._copies(src, dst, sems, False):
                cp.wait_send()

        return [(0, start), (steps - 1, finish)]

    def results(self, outs):
        return [o.reshape(2 * b.shape[1], b.shape[2]) for o, b in zip(outs, self.inputs)]


class _MultiCargo:
    def __init__(self, cargos):
        self.cargos = cargos
        self.inputs = [a for cg in cargos for a in cg.inputs]
        self.out_shape = [o for cg in cargos for o in cg.out_shape]
        self.sems = [s for cg in cargos for s in cg.sems]
        self.aliased, i0, o0 = [], 0, 0
        for cg in cargos:
            self.aliased += [(i0 + i, o0 + o) for i, o in cg.aliased]
            i0, o0 = i0 + len(cg.inputs), o0 + len(cg.out_shape)

    def _parts(self, refs, count):
        out, off = [], 0
        for cg in self.cargos:
            out.append(refs[off:off + count(cg)])
            off += count(cg)
        return out

    def stages(self, steps):
        merged = []
        for n, cg in enumerate(self.cargos):
            for at, act in cg.stages(steps):
                def part(src, dst, sems, n=n, act=act):
                    act(self._parts(src, lambda g: len(g.inputs))[n], self._parts(dst, lambda g: len(g.out_shape))[n],
                        self._parts(sems, lambda g: len(g.sems))[n])
                merged.append((at, part))
        starts = [m for m in merged if m[0] == 0]
        return starts + sorted([m for m in merged if m[0] != 0], key=lambda m: m[0])

    def results(self, outs):
        return [cg.results(part) for cg, part in zip(self.cargos, self._parts(list(outs), lambda g: len(g.out_shape)))]


def _cargo_call(body, cargo, *, name, grid, in_specs, out_specs, out_shape, scratch=(), prefetch=None):
    n_in, n_out, n_scr = len(in_specs), len(out_specs), len(scratch)
    c_in, c_out = len(cargo.inputs), len(cargo.out_shape)
    steps = 1
    for g in grid:
        steps *= g
    stages = cargo.stages(steps)

    def wrapped(*refs):
        head, refs = (refs[:1], refs[1:]) if prefetch is not None else ((), refs)
        ins, refs = refs[:n_in], refs[n_in:]
        cin, refs = refs[:c_in], refs[c_in:]
        outs, refs = refs[:n_out], refs[n_out:]
        cout, refs = refs[:c_out], refs[c_out:]
        scr, sems = refs[:n_scr], refs[n_scr:]
        if not grid:
            for _, act in stages:
                act(cin, cout, sems)
            return
        step = 0
        for d in range(len(grid)):
            step = step * grid[d] + pl.program_id(d)
        for at, act in stages:
            if at == 0:
                pl.when(step == 0)(functools.partial(act, cin, cout, sems))
        body(*head, *ins, *outs, *scr)
        for at, act in stages:
            if at > 0:
                pl.when(step == at)(functools.partial(act, cin, cout, sems))

    params = dict(vmem_limit_bytes=VMEM_LIMIT)
    if grid:
        params["dimension_semantics"] = ("arbitrary",) * len(grid)
    specs = dict(grid=grid, in_specs=list(in_specs) + [ANY] * c_in, out_specs=list(out_specs) + [ANY] * c_out,
                 scratch_shapes=list(scratch) + cargo.sems)
    if prefetch is not None:
        specs = dict(grid_spec=pltpu.PrefetchScalarGridSpec(num_scalar_prefetch=1, **specs))
    head = () if prefetch is None else (prefetch,)
    res = pl.pallas_call(wrapped, name=name, out_shape=list(out_shape) + cargo.out_shape,
                         input_output_aliases={len(head) + n_in + i: n_out + o for i, o in cargo.aliased},
                         compiler_params=pltpu.CompilerParams(**params), **specs)
    return lambda *args: (lambda r: (r[:n_out], cargo.results(r[n_out:])))(res(*head, *args, *cargo.inputs))


def _split_halves(g):
    return g.reshape(g.shape[0], 2, g.shape[1] // 2, g.shape[2])


def _ew_tile(rows):
    return rows if rows <= 384 else 256


def _batched_call(items, name, place=None, cargo=None):
    offs, total = [], 0
    for it in items:
        offs.append(total)
        total += it[0]
    in_specs, out_specs, out_shape, args, spans = [], [], [], [], []
    for (n, ins, outs, shapes, operands, _), off in zip(items, offs):
        tile = lambda t, off=off, n=n: jnp.clip(t - off, 0, n - 1)
        i_s, o_s = ins(tile), outs(tile)
        spans.append((len(i_s), len(o_s)))
        in_specs += i_s
        out_specs += o_s
        out_shape += shapes
        args += operands
    n_in = len(in_specs)

    def body(*refs):
        if place is not None:
            refs = refs[1:]
        t = pl.program_id(0)
        i0, o0 = 0, n_in
        for (n, _, _, _, _, compute), off, (ni, no) in zip(items, offs, spans):
            pl.when((t >= off) & (t < off + n))(functools.partial(compute, refs[i0:i0 + ni], refs[o0:o0 + no], t - off))
            i0, o0 = i0 + ni, o0 + no

    params = pltpu.CompilerParams(dimension_semantics=("arbitrary",), vmem_limit_bytes=VMEM_LIMIT)
    carried = None
    if cargo is not None:
        res, carried = _cargo_call(body, cargo, name=name, grid=(total,), in_specs=in_specs, out_specs=out_specs,
                                   out_shape=out_shape, prefetch=place)(*args)
    elif place is None:
        res = pl.pallas_call(body, name=name, grid=(total,), in_specs=in_specs, out_specs=out_specs,
                             out_shape=out_shape, compiler_params=params)(*args)
    else:
        res = pl.pallas_call(
            body, name=name, out_shape=out_shape, compiler_params=params,
            grid_spec=pltpu.PrefetchScalarGridSpec(num_scalar_prefetch=1, grid=(total,), in_specs=in_specs,
                                                   out_specs=out_specs))(place, *args)
    out, o0 = [], 0
    for _, no in spans:
        out.append(res[o0:o0 + no])
        o0 += no
    return out if cargo is None else (out, carried)


def _add_pairs(pairs, place, name):
    def item(mine, theirs):
        lead, _, rows, cols = mine.shape
        tm = _ew_tile(rows)

        def compute(ins, outs, tile):
            outs[0][...] = (ins[0][...].astype(f32) + ins[1][...].astype(f32)).astype(outs[0].dtype)

        blk = lambda tile: pl.BlockSpec((lead, tm, cols), lambda t, pc: (0, tile(t), 0))
        mine_half = lambda tile: pl.BlockSpec((lead, None, tm, cols), lambda t, pc: (0, pc[1], tile(t), 0))
        return (rows // tm, lambda tile: [mine_half(tile), blk(tile)], lambda tile: [blk(tile)],
                [jax.ShapeDtypeStruct(theirs.shape, theirs.dtype)], [mine, theirs], compute)

    return [out[0] for out in _batched_call([item(a, b) for a, b in pairs], name, place)]


def _cargo_alone(cargo, name):
    return _cargo_call(None, cargo, name=name, grid=(), in_specs=[], out_specs=[], out_shape=[])()[1]


def _bf16_casts(x, mats, padded, cargo, name):
    def item(a, to_rows, tm):
        rows, cols = a.shape
        full = rows // tm

        def compute(ins, outs, tile):
            outs[0][...] = jnp.where(tile < full, ins[0][...], 0.0).astype(bf16)

        return (to_rows // tm, lambda tile: [pl.BlockSpec((tm, cols), lambda t: (jnp.minimum(tile(t), full - 1), 0))],
                lambda tile: [pl.BlockSpec((tm, cols), lambda t: (tile(t), 0))],
                [jax.ShapeDtypeStruct((to_rows, cols), bf16)], [a], compute)

    items = [item(x, x.shape[0], 128)] + [item(w, r, 64 if r > w.shape[0] else 128) for w, r in zip(mats, padded)]
    outs, carried = _batched_call(items, name, cargo=cargo)
    return [o[0] for o in outs], carried


def _sum_slots_items(jobs, max_rows=None):
    def item(own, recv, whole):
        _, rows, cols = own.shape
        tm = _ew_tile(rows)
        while max_rows is not None and tm > max_rows and tm % 16 == 0:
            tm //= 2

        def compute(ins, outs, tile):
            r0, r1, r2 = [ins[1][k].astype(f32) for k in range(N_CHIPS - 1)]
            outs[0][...] = (ins[0][...].astype(f32) + r0) + (r1 + r2)

        ins = lambda tile: [pl.BlockSpec((None, tm, cols), lambda t, pc: (0 if whole else pc[0], tile(t), 0)),
                            pl.BlockSpec((N_CHIPS - 1, tm, cols), lambda t, pc: (0, tile(t), 0))]
        outs = lambda tile: [pl.BlockSpec((None, tm, cols), lambda t, pc: (pc[1], tile(t), 0))]
        return rows // tm, ins, outs, [jax.ShapeDtypeStruct((2, rows, cols), f32)], [own, recv], compute

    return [item(*job) for job in jobs]


class _GradReducer:
    def __init__(self, place):
        self.place, self.split, self.sums, self.slots, self.bufs, self.full = place, {}, {}, {}, {}, {}

    def swap(self, named):
        self.split.update({nm: _split_halves(g) for nm, g in named})
        return _SwapCargo([self.split[nm] for nm, _ in named])

    def summed(self, names, got, tag):
        sums = _add_pairs([(self.split[nm], g) for nm, g in zip(names, got)], self.place, "presum_" + tag)
        self.sums.update(zip(names, sums))

    def exchange(self, names):
        return _ExchangeCargo([self.sums[nm] for nm in names], [nm == "small" for nm in names])

    def arrived(self, names, slots):
        self.slots.update(zip(names, slots))

    def tree_sum_items(self, names, max_rows=None):
        return _sum_slots_items([(self.sums[nm], self.slots[nm], nm == "small") for nm in names], max_rows)

    def tree_summed(self, names, bufs):
        self.bufs.update(zip(names, bufs))

    def fill(self, names):
        return _FillCargo([self.bufs[nm] for nm in names])

    def filled(self, names, grads):
        self.full.update(zip(names, grads))

    def reduce(self, names):
        rest = [nm for nm in names if nm not in self.bufs]
        outs = _batched_call(self.tree_sum_items(rest), "sum_slots", self.place)
        self.tree_summed(rest, [out[0] for out in outs])
        rest = [nm for nm in names if nm not in self.full]
        self.filled(rest, _cargo_alone(self.fill(rest), "sibling_fill"))
        return [self.full[nm] for nm in names]


def _adamw_update(w, m, v, g):
    m = ADAM_B1 * m + (1.0 - ADAM_B1) * g
    v = ADAM_B2 * v + (1.0 - ADAM_B2) * jnp.square(g)
    m_hat = m / (1.0 - ADAM_B1 ** ADAM_STEP)
    v_hat = v / (1.0 - ADAM_B2 ** ADAM_STEP)
    return -ADAM_LR * (m_hat / (jnp.sqrt(v_hat) + ADAM_EPS) + ADAM_WD * w), m, v


def _adamw(jobs, name):
    def item(w, m, v, grad, tm):
        rows, cols = w.shape
        gcols = grad.shape[1]

        def compute(ins, outs, tile):
            g = ins[3][0:tm, 0:cols]
            outs[0][...] = g
            outs[1][...], outs[2][...], outs[3][...] = _adamw_update(ins[0][...], ins[1][...], ins[2][...], g)

        blk = lambda tile: pl.BlockSpec((tm, cols), lambda t: (tile(t), 0))
        if tm == rows:
            gspec = lambda tile: pl.BlockSpec(grad.shape, lambda t: (0, 0))
        else:
            gspec = lambda tile: pl.BlockSpec((tm, gcols), lambda t: (tile(t), 0))
        return (rows // tm, lambda tile: [blk(tile)] * 3 + [gspec(tile)], lambda tile: [blk(tile)] * 4,
                [jax.ShapeDtypeStruct((rows, cols), f32)] * 4, [w, m, v, grad], compute)

    return _batched_call([item(*job) for job in jobs], name)


def _rope_tables(s):
    half = HEAD_DIM // 2
    inv_freq = 1.0 / (10000.0 ** (jnp.arange(half, dtype=f32) / half))
    ang = jnp.arange(s, dtype=f32)[:, None] * inv_freq[None, :]
    cos, sin = jnp.cos(ang), jnp.sin(ang)
    return jnp.tile(cos, (1, LANES // half)), jnp.tile(jnp.concatenate([-sin, sin], axis=1), (1, LANES // HEAD_DIM))


def _local_step(x, x_bf, target, win_g, gather, comm, b_gate, w_pool, pool_scale, ln1_g, ln1_b, convb_g, ln2_g, ln2_b):
    s = x.shape[0]
    nb = s // MOBA_BLOCK
    cos_t, sin_t = _rope_tables(s)

    z, (wba_g, wbp_g, wout_g) = _inproj(x_bf, win_g, cos_t, sin_t, b_gate, gather["inproj"])
    wout = wout_g.reshape(D_MODEL, D_MODEL)
    (o_bf, lse, pooled_bf, mixed, pm_bf), (wg_g, wu_g, convw_g) = _attn_fwd(z, nb, w_pool, pool_scale,
                                                                           gather["attn_fwd"])
    m_bf, x1_bf, xhat1, rstd1 = _merge_out_ln1(o_bf, pm_bf, z, wba_g, wbp_g, wout, x, ln1_g, ln1_b)
    (a, up, ac, hh_bf), (wd_g,) = _ffn_up(x1_bf, wg_g, wu_g, convw_g, convb_g, gather["ffn_up"])
    wd = wd_g.reshape(D_FF_PAD, D_MODEL)
    loss, dh2, dh2_bf, d_ln2_g, d_ln2_b = _ffn_down_ln2_loss(hh_bf, wd, xhat1, ln1_g, ln1_b, ln2_g, ln2_b, target)

    da_bf, dup_bf, dconv = _ffn_act_bwd(dh2_bf, wd_g, ac, a, up, convw_g)
    ffn_shape = (N_CHIPS, FF_PAD, D_MODEL)
    d_wd = _matmul_tn(hh_bf, dh2_bf, 1, "dw_ffn_down").reshape(ffn_shape)
    d_wg, got = _matmul_tn(da_bf, x1_bf, 1, "dw_ffn_gate", cargo=comm.swap([("w_ffn_down", d_wd)]))
    comm.summed(["w_ffn_down"], got, "ffn_down")
    d_wg = d_wg.reshape(ffn_shape)
    d_wu = _matmul_tn(dup_bf, x1_bf, 1, "dw_ffn_up").reshape(ffn_shape)
    gate_up = ["w_ffn_gate", "w_ffn_up", "conv_w"]
    (dh1, dh1_bf, d_ln1_g, d_ln1_b), (slots, got) = _ffn_in_bwd_ln1(
        da_bf, dup_bf, wg_g, wu_g, dh2, xhat1, rstd1, ln1_g,
        _MultiCargo([comm.exchange(["w_ffn_down"]), comm.swap(list(zip(gate_up, [d_wg, d_wu, dconv])))]))
    comm.arrived(["w_ffn_down"], slots)
    comm.summed(gate_up, got, "ffn_gate_up")
    d_wout = _matmul_tn(m_bf, dh1_bf, 1, "dw_out").reshape(N_CHIPS, D_MODEL // N_CHIPS, D_MODEL)
    dzg_bf, dya_bf, dyp_bf, do_bf, dpm, d_bgate = _merge_bwd(dh1_bf, wout, o_bf, pm_bf, z, wba_g, wbp_g)
    d_wba = _matmul_tn(o_bf, dya_bf, N_CHIPS, "dw_branch_attn")
    d_wbp = _matmul_tn(pm_bf, dyp_bf, N_CHIPS, "dw_branch_pool")
    du_bf, d_wpool, d_pscale = _pool_bwd(dpm, mixed, pooled_bf, w_pool, pool_scale)
    d_convb = dconv[:, 3, :FF_SHARD].reshape(1, N_CHIPS * FF_SHARD)
    small = _pack_small([d_bgate, d_wpool.reshape(-1, POOL_GROUP), d_pscale, d_ln1_g, d_ln1_b, d_ln2_g, d_ln2_b,
                         d_convb], loss)[None]
    branch = ["w_out", "w_branch_attn", "w_branch_pool", "small"]
    (dqkv_bf,), (slots, got) = _attn_bwd(
        z, o_bf, lse, do_bf, cos_t, sin_t, nb,
        _MultiCargo([comm.exchange(gate_up), comm.swap(list(zip(branch, [d_wout, d_wba, d_wbp, small])))]))
    comm.arrived(gate_up, slots)
    comm.summed(branch, got, "branch")
    d_win, slots = _dw_in(x_bf, dqkv_bf, du_bf, dzg_bf, comm.exchange(branch))
    comm.arrived(branch, slots)
    ffn = ["w_ffn_down"] + gate_up
    bufs, got = _batched_call(comm.tree_sum_items(ffn), "swap_w_in", comm.place, comm.swap([("w_in", d_win)]))
    comm.tree_summed(ffn, [b[0] for b in bufs])
    comm.summed(["w_in"], got, "w_in")
    (grad_x, bufs), (slots, full) = _inproj_dx(
        dqkv_bf, du_bf, dzg_bf, win_g, dh1, comm.tree_sum_items(branch, 192), comm.place,
        _MultiCargo([comm.exchange(["w_in"]), comm.fill(ffn)]))
    comm.tree_summed(branch, bufs)
    comm.filled(ffn, full)
    comm.arrived(["w_in"], slots)
    return grad_x


SMALL_WEIGHTS = [("b_gate", 2 * D_MODEL), ("w_pool", 4 * POOL_GROUP * POOL_GROUP), ("pool_scale", D_POOL),
                 ("ln1_g", D_MODEL), ("ln1_b", D_MODEL), ("ln2_g", D_MODEL), ("ln2_b", D_MODEL),
                 ("conv_b", N_CHIPS * FF_SHARD)]
LOSS_ROW = 588
SMALL_ROWS = 592


def _small_pieces(ref, first_row, n):
    if ref.shape[0] > 1:
        return [(slice(first_row, first_row + n // LANES), (slice(None), slice(None)))]
    return [(slice(first_row + i, first_row + i + 1), (slice(None), slice(i * LANES, (i + 1) * LANES)))
            for i in range(n // LANES)]


def _small_first_rows():
    rows, r = [], 0
    for _, n in SMALL_WEIGHTS:
        rows.append(r)
        r += n // LANES
    return rows


def _pack_small(parts, loss):
    def body(*refs):
        out = refs[-1]
        out[...] = jnp.zeros_like(out)
        for src, r0, (_, n) in zip(refs, _small_first_rows(), SMALL_WEIGHTS):
            for rows, where in _small_pieces(src, r0, n):
                out[rows, :] = src[where]
        out[LOSS_ROW:LOSS_ROW + 1, :] = refs[len(parts)][0:1, :]

    return pl.pallas_call(body, name="pack_small", out_shape=jax.ShapeDtypeStruct((SMALL_ROWS, LANES), f32))(
        *parts, loss)


def _adamw_small(g_packed, g_conv, triples):
    n_in = 3 * len(triples)

    def body(g_ref, gc_ref, *refs):
        def update(k, g, where):
            w_ref, m_ref, v_ref = refs[3 * k:3 * k + 3]
            new = (g,) + _adamw_update(w_ref[where], m_ref[where], v_ref[where], g)
            for o_ref, val in zip(refs[n_in + 4 * k:n_in + 4 * k + 4], new):
                o_ref[where] = val

        for k, (r0, (_, n)) in enumerate(zip(_small_first_rows(), SMALL_WEIGHTS)):
            for rows, where in _small_pieces(refs[3 * k], r0, n):
                update(k, g_ref[rows, :], where)
        for tap in range(3):
            update(len(SMALL_WEIGHTS), gc_ref[tap:tap + 1, 0:FF_SHARD], (tap,))

    res = pl.pallas_call(
        body, name="adamw_small",
        out_shape=[jax.ShapeDtypeStruct(t[0].shape, f32) for t in triples for _ in range(4)],
    )(g_packed, g_conv, *[a for t in triples for a in t])
    return [res[4 * k:4 * k + 4] for k in range(len(triples))]


def _pad_conv_b(cb):
    return jnp.pad(cb.reshape(N_CHIPS, FF_SHARD), ((0, 0), (0, FF_PAD - FF_SHARD)))


def kernel(x, w_in, b_gate, w_branch_attn, w_pool, pool_scale, w_branch_pool, w_out, ln1_g, ln1_b, w_ffn_gate, w_ffn_up, conv_w, conv_b, w_ffn_down, ln2_g, ln2_b, loss_target, m_w_in, m_b_gate, m_w_branch_attn, m_w_pool, m_pool_scale, m_w_branch_pool, m_w_out, m_ln1_g, m_ln1_b, m_w_ffn_gate, m_w_ffn_up, m_conv_w, m_conv_b, m_w_ffn_down, m_ln2_g, m_ln2_b, v_w_in, v_b_gate, v_w_branch_attn, v_w_pool, v_pool_scale, v_w_branch_pool, v_w_out, v_ln1_g, v_ln1_b, v_w_ffn_gate, v_w_ffn_up, v_conv_w, v_conv_b, v_w_ffn_down, v_ln2_g, v_ln2_b):
    mats = [w_branch_attn[0], w_branch_pool[0], w_out[0], w_ffn_gate[0].T, w_ffn_up[0].T, w_ffn_down[0]]
    padded = [w.shape[0] for w in mats[:3]] + [FF_PAD] * 3
    (x_bf, *casted), (win_g,) = _bf16_casts(x[0], mats, padded, _GatherCargo([w_in[0].astype(bf16)], [0.6]),
                                            "gather_w_in")
    shards = [None] + casted + [jnp.pad(conv_w[0], ((0, CONV_ROWS - 3), (0, FF_PAD - FF_SHARD)))]
    gather = {"inproj": _GatherCargo(shards[1:4], [0.45, 0.5, 0.55]),
              "attn_fwd": _GatherCargo(shards[4:6] + shards[7:], [0.65, 0.65, 0.65]),
              "ffn_up": _GatherCargo(shards[6:7], [0.5])}

    place = jnp.stack([2 * lax.axis_index("x") + lax.axis_index("y"), lax.axis_index("c")]).astype(jnp.int32)
    comm = _GradReducer(place)
    convb_g = _pad_conv_b(conv_b).reshape(N_CHIPS, 1, FF_PAD)
    grad_x = _local_step(
        x[0], x_bf, loss_target[0], win_g, gather, comm, b_gate, w_pool[0], pool_scale, ln1_g, ln1_b, convb_g, ln2_g, ln2_b)
    names = ["w_in", "w_branch_attn", "w_branch_pool", "w_out", "w_ffn_gate", "w_ffn_up", "w_ffn_down", "conv_w", "small"]
    grads = comm.reduce(names)

    weights = [w_in[0], w_branch_attn[0], w_branch_pool[0], w_out[0], w_ffn_gate[0].T, w_ffn_up[0].T, w_ffn_down[0]]
    m_in = [m_w_in[0], m_w_branch_attn[0], m_w_branch_pool[0], m_w_out[0], m_w_ffn_gate[0].T, m_w_ffn_up[0].T,
            m_w_ffn_down[0]]
    v_in = [v_w_in[0], v_w_branch_attn[0], v_w_branch_pool[0], v_w_out[0], v_w_ffn_gate[0].T, v_w_ffn_up[0].T,
            v_w_ffn_down[0]]
    tiles = [256, 256, 256, 128, 176, 176, 176]
    jobs = list(zip(weights, m_in, v_in, grads, tiles))
    updated = _adamw(jobs[:4], "adamw_rest") + _adamw(jobs[4:], "adamw_ffn")
    res = {nm: [(r.T if nm in ("w_ffn_gate", "w_ffn_up") else r)[None] for r in outs4]
           for nm, outs4 in zip(names, updated)}

    small_w = [b_gate, w_pool, pool_scale, ln1_g, ln1_b, ln2_g, ln2_b, conv_b, conv_w]
    small_m = [m_b_gate, m_w_pool, m_pool_scale, m_ln1_g, m_ln1_b, m_ln2_g, m_ln2_b, m_conv_b, m_conv_w]
    small_v = [v_b_gate, v_w_pool, v_pool_scale, v_ln1_g, v_ln1_b, v_ln2_g, v_ln2_b, v_conv_b, v_conv_w]

    def as_kernel_sees(a):
        return a.reshape(-1, POOL_GROUP) if a.ndim == 4 else a.transpose(1, 0, 2) if a.ndim == 3 else a

    small_out = _adamw_small(grads[-1], grads[-2],
                             [tuple(as_kernel_sees(a) for a in t) for t in zip(small_w, small_m, small_v)])
    for nm, w, outs4 in zip([nm for nm, _ in SMALL_WEIGHTS] + ["conv_w"], small_w, small_out):
        res[nm] = [o.transpose(1, 0, 2) if w.ndim == 3 else o.reshape(w.shape) for o in outs4]
    loss = grads[-1][LOSS_ROW, 0]

    order = ["w_in", "b_gate", "w_branch_attn", "w_pool", "pool_scale", "w_branch_pool", "w_out", "ln1_g", "ln1_b",
             "w_ffn_gate", "w_ffn_up", "conv_w", "conv_b", "w_ffn_down", "ln2_g", "ln2_b"]
    outs = [loss, grad_x[None]]
    for kind in range(4):
        outs += [res[nm][kind] for nm in order]
    return tuple(outs)
```

```python
import functools

import jax
import jax.numpy as jnp
from jax import lax
from jax.experimental import pallas as pl
from jax.experimental.pallas import tpu as pltpu

D_MODEL = 1024
HEAD_DIM = 64
D_ATTN = 512
D_POOL = 512
MOBA_BLOCK = 256
MOBA_TOPK = 3
POOL_GROUP = 128
MAX_WINDOW = 16
FF_SHARD = 704
FF_PAD = 768
D_FF_PAD = 4 * FF_PAD
N_CHIPS = 4
LANES = 128
ALPHA = (2.0 * 1) ** 0.25
LN_EPS = 1e-5
NEG = -1e30
SCALE = HEAD_DIM ** -0.5
ADAM_LR, ADAM_B1, ADAM_B2, ADAM_EPS, ADAM_WD, ADAM_STEP = 0.001, 0.9, 0.999, 1e-08, 0.01, 10
VMEM_LIMIT = 56 * 1024 * 1024
MESH = pl.DeviceIdType.MESH

bf16 = jnp.bfloat16
f32 = jnp.float32


def _dot(a, b):
    return jnp.dot(a, b, preferred_element_type=f32)


def _dot_nt(a, b):
    return lax.dot_general(a, b, (((1,), (1,)), ((), ())), preferred_element_type=f32)


def _dot_tn(a, b):
    return lax.dot_general(a, b, (((0,), (0,)), ((), ())), preferred_element_type=f32)


def _dot_shards(a, w_ref):
    return jnp.concatenate([_dot(a, w_ref[p]) for p in range(w_ref.shape[0])], axis=1)


def _dot_nt_shards(d, w_ref):
    n = w_ref.shape[2]
    acc = _dot_nt(d[:, 0:n], w_ref[0])
    for p in range(1, w_ref.shape[0]):
        acc = acc + _dot_nt(d[:, p * n:(p + 1) * n], w_ref[p])
    return acc


def _column_shards_spec(k, n):
    return pl.BlockSpec((N_CHIPS, k, n // N_CHIPS), lambda i: (0, 0, 0))


def _call(body, *, name, grid, in_specs, out_specs, out_shape, scratch=(), cargo=None):
    if cargo is not None:
        return _cargo_call(body, cargo, name=name, grid=grid, in_specs=in_specs, out_specs=out_specs,
                           out_shape=out_shape, scratch=scratch)
    return pl.pallas_call(
        body, name=name, grid=grid, in_specs=in_specs, out_specs=out_specs, out_shape=out_shape,
        scratch_shapes=list(scratch),
        compiler_params=pltpu.CompilerParams(dimension_semantics=("arbitrary",) * len(grid),
                                             vmem_limit_bytes=VMEM_LIMIT))


def _rot_half(t):
    lane = lax.broadcasted_iota(jnp.int32, t.shape, 1)
    first = (lane % HEAD_DIM) < (HEAD_DIM // 2)
    return jnp.where(first, pltpu.roll(t, LANES - HEAD_DIM // 2, 1), pltpu.roll(t, HEAD_DIM // 2, 1))


def _rope(t, cos, sin_signed):
    return t * cos + _rot_half(t) * sin_signed


def _rope_bwd(d, cos, sin_signed):
    return d * cos + _rot_half(d * sin_signed)


def _gelu_parts(a):
    cdf = 0.5 * (1.0 + lax.erf(a * (2.0 ** -0.5)))
    pdf = jnp.exp(-0.5 * a * a) * ((2.0 * jnp.pi) ** -0.5)
    return a * cdf, cdf + a * pdf


def _layer_norm(h, g, b):
    mu = jnp.mean(h, axis=-1, keepdims=True)
    xc = h - mu
    var = jnp.mean(xc * xc, axis=-1, keepdims=True)
    rstd = lax.rsqrt(var + LN_EPS)
    xhat = xc * rstd
    return xhat * g + b, xhat, rstd


def _layer_norm_bwd(dy, xhat, rstd, g):
    dxh = dy * g
    m1 = jnp.mean(dxh, axis=-1, keepdims=True)
    m2 = jnp.mean(dxh * xhat, axis=-1, keepdims=True)
    return rstd * (dxh - m1 - xhat * m2)


def _inproj(x_bf, win_g, cos_t, sin_t, b_gate, cargo):
    s = x_bf.shape[0]
    tm, tn = 1024, 512

    def body(x_ref, w_ref, cos_ref, sin_ref, b_ref, o_ref):
        j = pl.program_id(1)
        acc = _dot(x_ref[...], w_ref[...])

        @pl.when(j < 2)
        def _():
            for c in range(tn // LANES):
                sl = slice(c * LANES, (c + 1) * LANES)
                o_ref[:, sl] = _rope(acc[:, sl], cos_ref[...], sin_ref[...])

        @pl.when((j >= 2) & (j < 4))
        def _():
            o_ref[...] = acc

        @pl.when(j >= 4)
        def _():
            o_ref[...] = jax.nn.sigmoid(acc + b_ref[...])

    (z,), carried = _cargo_call(
        body, cargo, name="inproj", grid=(s // tm, 4 * D_MODEL // tn),
        in_specs=[pl.BlockSpec((tm, D_MODEL), lambda i, j: (i, 0)),
                  pl.BlockSpec((None, D_MODEL, tn), lambda i, j: (j // 2, 0, j % 2)),
                  pl.BlockSpec((tm, LANES), lambda i, j: (i, 0)),
                  pl.BlockSpec((tm, LANES), lambda i, j: (i, 0)),
                  pl.BlockSpec((1, tn), lambda i, j: (0, jnp.maximum(j - 4, 0)))],
        out_specs=[pl.BlockSpec((tm, tn), lambda i, j: (i, j))],
        out_shape=[jax.ShapeDtypeStruct((s, 4 * D_MODEL), f32)])(x_bf, win_g, cos_t, sin_t, b_gate)
    return z, carried


STAT_ROWS = 8


def _pair_rows(v0, v1, fill):
    r = lax.broadcasted_iota(jnp.int32, (STAT_ROWS, v0.shape[1]), 0)
    return jnp.where(r == 0, v0, jnp.where(r == 1, v1, fill))


def _head_lanes(shape):
    lane = lax.broadcasted_iota(jnp.int32, shape, 1)
    return lane < HEAD_DIM, lane >= HEAD_DIM


def _head_rows(shape):
    row = lax.broadcasted_iota(jnp.int32, shape, 0)
    return row < HEAD_DIM, row >= HEAD_DIM


def _moba_select(q_bf, k_all, i, nb):
    k_mean = jnp.mean(k_all.reshape(nb, MOBA_BLOCK, LANES), axis=1)
    n_io = lax.broadcasted_iota(jnp.int32, (nb, MOBA_BLOCK), 0)
    past = n_io < i
    sels = []
    for head in _head_lanes((nb, LANES)):
        gate = _dot_nt(jnp.where(head, k_mean, 0.0).astype(bf16), q_bf)
        g = jnp.where(past, gate, NEG)
        rank = jnp.zeros((nb, MOBA_BLOCK), f32)
        for m in range(nb):
            gm = g[m:m + 1, :]
            rank = rank + jnp.where((gm > g) | ((gm == g) & (m < n_io)), 1.0, 0.0)
        sels.append(jnp.where(past & (rank < MOBA_TOPK), 1.0, 0.0))
    return sels


def _attn_fwd(z, nb, w_pool, pool_scale, cargo):
    s = z.shape[0]

    def body(q_ref, k_ref, v_ref, u_ref, wp_ref, sc_ref, o_ref, lse_ref, pooled_ref, mixed_ref, pm_ref, vt_s, pad):
        pair, i = pl.program_id(0), pl.program_id(1)

        @pl.when(i == 0)
        def _():
            for j in range(nb):
                vt_s[j] = v_ref[j * MOBA_BLOCK:(j + 1) * MOBA_BLOCK, :].T.astype(bf16)
            _pool_group(pair, u_ref, wp_ref, sc_ref, pooled_ref, mixed_ref, pm_ref, pad)

        k_heads = _head_lanes((MOBA_BLOCK, LANES))
        o_heads = _head_rows((LANES, MOBA_BLOCK))

        def block(j, qs, allow, m_old, l_old):
            kj, vtj = k_ref[j * MOBA_BLOCK:(j + 1) * MOBA_BLOCK, :], vt_s[j]
            m_new, alpha, l_new, pv = [], [], [], None
            for h in range(2):
                sc = jnp.where(allow[h], _dot_nt(jnp.where(k_heads[h], kj, 0.0).astype(bf16), qs), NEG)
                mx = jnp.max(sc, axis=0, keepdims=True)
                mn = mx if m_old is None else jnp.maximum(m_old[h], mx)
                p = jnp.exp(sc - mn)
                lsum = jnp.sum(p, axis=0, keepdims=True)
                if m_old is None:
                    a = None
                else:
                    a = jnp.exp(m_old[h] - mn)
                    lsum = a * l_old[h] + lsum
                t = _dot(jnp.where(o_heads[h], vtj, jnp.zeros_like(vtj)), p.astype(bf16))
                pv = t if pv is None else pv + t
                m_new.append(mn)
                alpha.append(a)
                l_new.append(lsum)
            return m_new, alpha, l_new, pv

        def query_block(c):
            q = q_ref[...]
            sels = _moba_select(q.astype(bf16), k_ref[...], c, nb)
            qs = (q * SCALE).astype(bf16)
            key = lax.broadcasted_iota(jnp.int32, (MOBA_BLOCK, MOBA_BLOCK), 0)
            qry = lax.broadcasted_iota(jnp.int32, (MOBA_BLOCK, MOBA_BLOCK), 1)
            m, _, l, acc = block(c, qs, [key <= qry] * 2, None, None)
            for j in range(c):
                m, alpha, l, pv = block(j, qs, [sels[h][j:j + 1, :] > 0.0 for h in range(2)], m, l)
                acc = acc * jnp.where(o_heads[0], alpha[0], alpha[1]) + pv
            o_ref[...] = (acc / jnp.where(o_heads[0], l[0], l[1])).T.astype(bf16)
            lse_ref[...] = _pair_rows(m[0] + jnp.log(l[0]), m[1] + jnp.log(l[1]), 0.0)

        for c in range(nb):
            pl.when(i == c)(functools.partial(query_block, c))

    group = pl.BlockSpec((s, POOL_GROUP), lambda hp, i: (0, hp))
    return _cargo_call(
        body, cargo, name="attn_fwd", grid=(D_ATTN // LANES, nb),
        in_specs=[pl.BlockSpec((MOBA_BLOCK, LANES), lambda hp, i: (i, hp)),
                  pl.BlockSpec((s, LANES), lambda hp, i: (0, 4 + hp)),
                  pl.BlockSpec((s, LANES), lambda hp, i: (0, 8 + hp)),
                  pl.BlockSpec((s, POOL_GROUP), lambda hp, i: (0, 12 + hp)),
                  pl.BlockSpec((None, POOL_GROUP, POOL_GROUP), lambda hp, i: (hp, 0, 0)),
                  pl.BlockSpec((1, POOL_GROUP), lambda hp, i: (0, hp))],
        out_specs=[pl.BlockSpec((MOBA_BLOCK, LANES), lambda hp, i: (i, hp)),
                   pl.BlockSpec((None, STAT_ROWS, MOBA_BLOCK), lambda hp, i: (hp, 0, i)), group, group, group],
        out_shape=[jax.ShapeDtypeStruct((s, D_ATTN), bf16),
                   jax.ShapeDtypeStruct((D_ATTN // LANES, STAT_ROWS, s), f32),
                   jax.ShapeDtypeStruct((s, D_POOL), bf16), jax.ShapeDtypeStruct((s, D_POOL), f32),
                   jax.ShapeDtypeStruct((s, D_POOL), bf16)],
        scratch=[pltpu.VMEM((nb, LANES, MOBA_BLOCK), bf16), pltpu.VMEM((s + MAX_WINDOW, POOL_GROUP), f32)],
    )(z, z, z, z, w_pool, pool_scale)


def _window_select(snaps, g):
    return jnp.where(g == 0, snaps[0], jnp.where(g == 1, snaps[1], jnp.where(g == 2, snaps[2], snaps[3])))


def _pool_group(g, u_ref, w_ref, sc_ref, pooled_ref, mixed_ref, pm_ref, pad):
    s = u_ref.shape[0]
    u = u_ref[...]
    pad[0:MAX_WINDOW, :] = jnp.zeros((MAX_WINDOW, POOL_GROUP), f32)
    pad[MAX_WINDOW:MAX_WINDOW + s, :] = u
    acc = u
    snaps = []
    for d in range(1, MAX_WINDOW):
        acc = acc + pad[MAX_WINDOW - d:MAX_WINDOW - d + s, :]
        if d + 1 in (2, 4, 8, 16):
            snaps.append(acc)
    win = _window_select(snaps, g)
    t = lax.broadcasted_iota(jnp.int32, (s, POOL_GROUP), 0)
    count = jnp.minimum(t + 1, jnp.left_shift(2, g)).astype(f32)
    pooled = (win / count - u).astype(bf16)
    mixed = _dot(pooled, w_ref[...].astype(bf16))
    pooled_ref[...] = pooled
    mixed_ref[...] = mixed
    pm_ref[...] = (mixed * sc_ref[...]).astype(bf16)


def _merge_out_ln1(o_bf, pm_bf, z, wba, wbp, wout, x, ln_g, ln_b):
    s = x.shape[0]
    tm = 512

    def body(o_ref, pm_ref, ga_ref, gp_ref, wba_ref, wbp_ref, w_ref, x_ref, g_ref, b_ref,
             m_ref, x1_ref, xhat_ref, rstd_ref):
        ya = _dot_shards(o_ref[...], wba_ref)
        yp = _dot_shards(pm_ref[...], wbp_ref)
        m = (ga_ref[...] * ya + gp_ref[...] * yp).astype(bf16)
        m_ref[...] = m
        h = ALPHA * x_ref[...] + _dot(m, w_ref[...])
        y, xhat, rstd = _layer_norm(h, g_ref[...], b_ref[...])
        x1_ref[...] = y.astype(bf16)
        xhat_ref[...] = xhat
        rstd_ref[...] = jnp.broadcast_to(rstd, (tm, LANES))

    row = pl.BlockSpec((tm, D_MODEL), lambda i: (i, 0))
    vec = pl.BlockSpec((1, D_MODEL), lambda i: (0, 0))
    return _call(
        body, name="merge_out_ln1", grid=(s // tm,),
        in_specs=[pl.BlockSpec((tm, D_ATTN), lambda i: (i, 0)), pl.BlockSpec((tm, D_POOL), lambda i: (i, 0)),
                  pl.BlockSpec((tm, D_MODEL), lambda i: (i, 2)), pl.BlockSpec((tm, D_MODEL), lambda i: (i, 3)),
                  _column_shards_spec(D_ATTN, D_MODEL), _column_shards_spec(D_POOL, D_MODEL),
                  pl.BlockSpec((D_MODEL, D_MODEL), lambda i: (0, 0)), row, vec, vec],
        out_specs=[row, row, row, pl.BlockSpec((tm, LANES), lambda i: (i, 0))],
        out_shape=[jax.ShapeDtypeStruct((s, D_MODEL), bf16), jax.ShapeDtypeStruct((s, D_MODEL), bf16),
                   jax.ShapeDtypeStruct((s, D_MODEL), f32), jax.ShapeDtypeStruct((s, LANES), f32)],
    )(o_bf, pm_bf, z, z, wba, wbp, wout, x, ln_g, ln_b)


FF_TILE = 256
FF_TILES_PER_SHARD = FF_PAD // FF_TILE
CONV_PAD = 8
CONV_ROWS = 16

def _ff_weight_spec(rows):
    return pl.BlockSpec((None, rows, FF_TILE), lambda j: (j // FF_TILES_PER_SHARD, 0, j % FF_TILES_PER_SHARD))


def _ff_rows_spec():
    return pl.BlockSpec((None, FF_TILE, D_MODEL), lambda j: (j // FF_TILES_PER_SHARD, j % FF_TILES_PER_SHARD, 0))


def _row_shifts(x, shifts):
    s = x.shape[0]
    padded = jnp.concatenate([x, jnp.zeros((CONV_PAD, x.shape[1]), x.dtype)], axis=0)
    return [pltpu.roll(padded, d % (s + CONV_PAD), 0)[0:s] for d in shifts]


def _ffn_up(x1_bf, wg_g, wu_g, convw_g, convb_g, cargo):
    s = x1_bf.shape[0]

    def body(x_ref, wg_ref, wu_ref, cw_ref, cb_ref, a_ref, up_ref, ac_ref, hh_ref):
        x1 = x_ref[...]
        a = _dot_nt(x1, wg_ref[...])
        up = _dot_nt(x1, wu_ref[...])
        a_ref[...] = a
        up_ref[...] = up
        a1, a2 = _row_shifts(a, (1, 2))
        ac = a * cw_ref[2:3, :] + a1 * cw_ref[1:2, :] + a2 * cw_ref[0:1, :] + cb_ref[...]
        ac_ref[...] = ac
        hg, _ = _gelu_parts(ac)
        hh_ref[...] = (hg * up).astype(bf16)

    col = pl.BlockSpec((s, FF_TILE), lambda j: (0, j))
    wide = jax.ShapeDtypeStruct((s, D_FF_PAD), f32)
    return _call(
        body, name="ffn_up", grid=(D_FF_PAD // FF_TILE,),
        in_specs=[pl.BlockSpec((s, D_MODEL), lambda j: (0, 0)), _ff_rows_spec(), _ff_rows_spec(),
                  _ff_weight_spec(8), _ff_weight_spec(1)],
        out_specs=[col, col, col, col],
        out_shape=[wide, wide, wide, jax.ShapeDtypeStruct((s, D_FF_PAD), bf16)],
        cargo=cargo)(x1_bf, wg_g, wu_g, convw_g, convb_g)


def _ffn_down_ln2_loss(hh_bf, wd, xhat1, ln1_g, ln1_b, ln2_g, ln2_b, target):
    s = hh_bf.shape[0]
    tm = 256

    def body(hh_ref, w_ref, xh1_ref, g1_ref, b1_ref, g2_ref, b2_ref, t_ref, loss_ref, dh_ref, dhb_ref, dg_ref, db_ref):
        i = pl.program_id(0)
        x1 = xh1_ref[...] * g1_ref[...] + b1_ref[...]
        h = ALPHA * x1 + _dot(hh_ref[...], w_ref[...])
        y, xhat, rstd = _layer_norm(h, g2_ref[...], b2_ref[...])
        err = y - t_ref[...]
        part = 0.5 * jnp.sum(jnp.mean(err * err, axis=-1, keepdims=True), axis=0, keepdims=True)
        dy = err * (1.0 / D_MODEL)

        @pl.when(i == 0)
        def _():
            loss_ref[...] = jnp.zeros_like(loss_ref)
            dg_ref[...] = jnp.zeros_like(dg_ref)
            db_ref[...] = jnp.zeros_like(db_ref)

        loss_ref[...] += jnp.broadcast_to(part, loss_ref.shape)
        dg_ref[...] += jnp.sum(dy * xhat, axis=0, keepdims=True)
        db_ref[...] += jnp.sum(dy, axis=0, keepdims=True)
        dh = _layer_norm_bwd(dy, xhat, rstd, g2_ref[...])
        dh_ref[...] = dh
        dhb_ref[...] = dh.astype(bf16)

    row = pl.BlockSpec((tm, D_MODEL), lambda i: (i, 0))
    vec = pl.BlockSpec((1, D_MODEL), lambda i: (0, 0))
    return _call(
        body, name="ffn_down_ln2_loss", grid=(s // tm,),
        in_specs=[pl.BlockSpec((tm, D_FF_PAD), lambda i: (i, 0)),
                  pl.BlockSpec((D_FF_PAD, D_MODEL), lambda i: (0, 0)),
                  row, vec, vec, vec, vec, row],
        out_specs=[pl.BlockSpec((8, LANES), lambda i: (0, 0)), row, row, vec, vec],
        out_shape=[jax.ShapeDtypeStruct((8, LANES), f32), jax.ShapeDtypeStruct((s, D_MODEL), f32),
                   jax.ShapeDtypeStruct((s, D_MODEL), bf16), jax.ShapeDtypeStruct((1, D_MODEL), f32),
                   jax.ShapeDtypeStruct((1, D_MODEL), f32)])(hh_bf, wd, xhat1, ln1_g, ln1_b, ln2_g, ln2_b, target)


def _ffn_act_bwd(dh2_bf, wd_g, ac, a, up, convw_g):
    s = dh2_bf.shape[0]

    def body(dh_ref, wd_ref, ac_ref, a_ref, up_ref, cw_ref, da_ref, dup_ref, dc_ref):
        dhh = _dot_nt(dh_ref[...], wd_ref[...])
        hg, dgelu = _gelu_parts(ac_ref[...])
        dup_ref[...] = (dhh * hg).astype(bf16)
        dac = dhh * up_ref[...] * dgelu
        dac1, dac2 = _row_shifts(dac, (-1, -2))
        a = a_ref[...]
        rows = [jnp.sum(d * a, axis=0, keepdims=True) for d in (dac2, dac1, dac)]
        rows.append(jnp.sum(dac, axis=0, keepdims=True))
        rows.append(jnp.zeros((CONV_ROWS - 4, FF_TILE), f32))
        dc_ref[...] = jnp.concatenate(rows, axis=0)
        da_ref[...] = (dac * cw_ref[2:3, :] + dac1 * cw_ref[1:2, :] + dac2 * cw_ref[0:1, :]).astype(bf16)

    col = pl.BlockSpec((s, FF_TILE), lambda j: (0, j))
    return _call(
        body, name="ffn_act_bwd", grid=(D_FF_PAD // FF_TILE,),
        in_specs=[pl.BlockSpec((s, D_MODEL), lambda j: (0, 0)), _ff_rows_spec(), col, col, col, _ff_weight_spec(8)],
        out_specs=[col, col, _ff_weight_spec(CONV_ROWS)],
        out_shape=[jax.ShapeDtypeStruct((s, D_FF_PAD), bf16), jax.ShapeDtypeStruct((s, D_FF_PAD), bf16),
                   jax.ShapeDtypeStruct((N_CHIPS, CONV_ROWS, FF_PAD), f32)],
    )(dh2_bf, wd_g, ac, a, up, convw_g)


def _matmul_tn(a, b, n_shards, name, tm=512, tn=1024, cargo=None):
    k, m = a.shape
    n = b.shape[1]
    tm, tn = min(tm, m), min(tn, n // n_shards)
    per = n // n_shards // tn

    def body(a_ref, b_ref, o_ref, at_s):
        @pl.when(pl.program_id(1) == 0)
        def _():
            at_s[...] = a_ref[...].T

        o_ref[...] = _dot(at_s[...], b_ref[...]).astype(bf16)

    res = _call(
        body, name=name, grid=(m // tm, n // tn),
        in_specs=[pl.BlockSpec((k, tm), lambda i, j: (0, i)), pl.BlockSpec((k, tn), lambda i, j: (0, j))],
        out_specs=[pl.BlockSpec((None, tm, tn), lambda i, j: (j // per, i, j % per))],
        out_shape=[jax.ShapeDtypeStruct((n_shards, m, n // n_shards), bf16)], scratch=[pltpu.VMEM((tm, k), bf16)],
        cargo=cargo)(a, b)
    return res[0] if cargo is None else (res[0][0], res[1])


def _dw_branches(o_bf, dya_bf, pm_bf, dyp_bf):
    k, m = o_bf.shape
    tn = D_MODEL // N_CHIPS
    last = N_CHIPS - 1

    def body(a0_ref, b0_ref, a1_ref, b1_ref, o0_ref, o1_ref, at_s):
        g, j = pl.program_id(0), pl.program_id(1)

        @pl.when((g == 0) & (j == 0))
        def _():
            at_s[...] = a0_ref[...].T

        @pl.when((g == 1) & (j == 0))
        def _():
            at_s[...] = a1_ref[...].T

        @pl.when(g == 0)
        def _():
            o0_ref[...] = _dot(at_s[...], b0_ref[...]).astype(bf16)

        @pl.when(g == 1)
        def _():
            o1_ref[...] = _dot(at_s[...], b1_ref[...]).astype(bf16)

    first = lambda g, j: jnp.where(g == 0, j, last)
    second = lambda g, j: jnp.where(g == 1, j, 0)
    whole = pl.BlockSpec((k, m), lambda g, j: (0, 0))
    shape = jax.ShapeDtypeStruct((N_CHIPS, m, tn), bf16)
    return _call(
        body, name="dw_branches", grid=(2, N_CHIPS),
        in_specs=[whole, pl.BlockSpec((k, tn), lambda g, j: (0, first(g, j))),
                  whole, pl.BlockSpec((k, tn), lambda g, j: (0, second(g, j)))],
        out_specs=[pl.BlockSpec((None, m, tn), lambda g, j: (first(g, j), 0, 0)),
                   pl.BlockSpec((None, m, tn), lambda g, j: (second(g, j), 0, 0))],
        out_shape=[shape, shape], scratch=[pltpu.VMEM((m, k), bf16)])(o_bf, dya_bf, pm_bf, dyp_bf)


def _dw_in(x_bf, dqkv, du, dzg, cargo):
    k, m = x_bf.shape
    tm, half = 512, D_ATTN

    def body(a_ref, dqkv_ref, du_ref, dzg_ref, o_ref, at_s):
        j = pl.program_id(1)

        @pl.when(j == 0)
        def _():
            at_s[...] = a_ref[...].T
            o_ref[:, 0:half] = _dot(at_s[...], dqkv_ref[0]).astype(bf16)
            o_ref[:, half:2 * half] = _dot(at_s[...], dqkv_ref[1]).astype(bf16)

        @pl.when(j == 1)
        def _():
            o_ref[:, 0:half] = _dot(at_s[...], dqkv_ref[2]).astype(bf16)
            o_ref[:, half:2 * half] = _dot(at_s[...], du_ref[...]).astype(bf16)

        @pl.when(j >= 2)
        def _():
            o_ref[...] = _dot(at_s[...], dzg_ref[...]).astype(bf16)

    (d_win,), carried = _cargo_call(
        body, cargo, name="dw_in", grid=(m // tm, N_CHIPS),
        in_specs=[pl.BlockSpec((k, tm), lambda i, j: (0, i)), pl.BlockSpec((3, k, half), lambda i, j: (0, 0, 0)),
                  pl.BlockSpec((k, half), lambda i, j: (0, 0)),
                  pl.BlockSpec((k, D_MODEL), lambda i, j: (0, jnp.maximum(j - 2, 0)))],
        out_specs=[pl.BlockSpec((None, tm, D_MODEL), lambda i, j: (j, i, 0))],
        out_shape=[jax.ShapeDtypeStruct((N_CHIPS, m, D_MODEL), bf16)], scratch=[pltpu.VMEM((tm, k), bf16)],
    )(x_bf, dqkv, du, dzg)
    return d_win, carried


def _ffn_in_bwd_ln1(da_bf, dup_bf, wg_g, wu_g, dh2, xhat1, rstd1, ln1_g, cargo):
    s = da_bf.shape[0]
    tm = 256

    def body(da_ref, dup_ref, wg_ref, wu_ref, dh2_ref, xh_ref, rstd_ref, g_ref, dh_ref, dhb_ref, dg_ref, db_ref):
        i = pl.program_id(0)
        dx1 = ALPHA * dh2_ref[...]
        for sh in range(N_CHIPS):
            sl = slice(sh * FF_PAD, (sh + 1) * FF_PAD)
            dx1 = dx1 + _dot(da_ref[:, sl], wg_ref[sh]) + _dot(dup_ref[:, sl], wu_ref[sh])
        xhat = xh_ref[...]

        @pl.when(i == 0)
        def _():
            dg_ref[...] = jnp.zeros_like(dg_ref)
            db_ref[...] = jnp.zeros_like(db_ref)

        dg_ref[...] += jnp.sum(dx1 * xhat, axis=0, keepdims=True)
        db_ref[...] += jnp.sum(dx1, axis=0, keepdims=True)
        dh = _layer_norm_bwd(dx1, xhat, rstd_ref[:, 0:1], g_ref[...])
        dh_ref[...] = dh
        dhb_ref[...] = dh.astype(bf16)

    row = pl.BlockSpec((tm, D_MODEL), lambda i: (i, 0))
    wide = pl.BlockSpec((tm, D_FF_PAD), lambda i: (i, 0))
    wfull = pl.BlockSpec((N_CHIPS, FF_PAD, D_MODEL), lambda i: (0, 0, 0))
    vec = pl.BlockSpec((1, D_MODEL), lambda i: (0, 0))
    return _call(
        body, name="ffn_in_bwd_ln1", grid=(s // tm,),
        in_specs=[wide, wide, wfull, wfull, row, row, pl.BlockSpec((tm, LANES), lambda i: (i, 0)), vec],
        out_specs=[row, row, vec, vec],
        out_shape=[jax.ShapeDtypeStruct((s, D_MODEL), f32), jax.ShapeDtypeStruct((s, D_MODEL), bf16),
                   jax.ShapeDtypeStruct((1, D_MODEL), f32), jax.ShapeDtypeStruct((1, D_MODEL), f32)],
        cargo=cargo)(da_bf, dup_bf, wg_g, wu_g, dh2, xhat1, rstd1, ln1_g)


def _merge_bwd(dh1_bf, wout, o_bf, pm_bf, z, wba, wbp):
    s = dh1_bf.shape[0]
    tm = 256

    def body(dh_ref, wout_ref, o_ref, pm_ref, ga_ref, gp_ref, wba_ref, wbp_ref,
             dzg_ref, dya_ref, dyp_ref, do_ref, dpm_ref, dbg_ref):
        i = pl.program_id(0)
        dm = _dot_nt(dh_ref[...], wout_ref[...])
        ya = _dot_shards(o_ref[...], wba_ref)
        yp = _dot_shards(pm_ref[...], wbp_ref)
        ga, gp = ga_ref[...], gp_ref[...]
        dza = dm * ya * ga * (1.0 - ga)
        dzp = dm * yp * gp * (1.0 - gp)

        @pl.when(i == 0)
        def _():
            dbg_ref[...] = jnp.zeros_like(dbg_ref)

        dbg_ref[:, 0:D_MODEL] += jnp.sum(dza, axis=0, keepdims=True)
        dbg_ref[:, D_MODEL:2 * D_MODEL] += jnp.sum(dzp, axis=0, keepdims=True)
        dzg_ref[:, 0:D_MODEL] = dza.astype(bf16)
        dzg_ref[:, D_MODEL:2 * D_MODEL] = dzp.astype(bf16)
        dya = (dm * ga).astype(bf16)
        dyp = (dm * gp).astype(bf16)
        dya_ref[...] = dya
        dyp_ref[...] = dyp
        do_ref[...] = _dot_nt_shards(dya, wba_ref).astype(bf16)
        dpm_ref[...] = _dot_nt_shards(dyp, wbp_ref)

    row = pl.BlockSpec((tm, D_MODEL), lambda i: (i, 0))
    half = pl.BlockSpec((tm, D_ATTN), lambda i: (i, 0))
    full = lambda r, c: pl.BlockSpec((r, c), lambda i: (0, 0))
    return _call(
        body, name="merge_bwd", grid=(s // tm,),
        in_specs=[row, full(D_MODEL, D_MODEL), half, half,
                  pl.BlockSpec((tm, D_MODEL), lambda i: (i, 2)), pl.BlockSpec((tm, D_MODEL), lambda i: (i, 3)),
                  _column_shards_spec(D_ATTN, D_MODEL), _column_shards_spec(D_POOL, D_MODEL)],
        out_specs=[pl.BlockSpec((tm, 2 * D_MODEL), lambda i: (i, 0)), row, row, half, half,
                   pl.BlockSpec((1, 2 * D_MODEL), lambda i: (0, 0))],
        out_shape=[jax.ShapeDtypeStruct((s, 2 * D_MODEL), bf16), jax.ShapeDtypeStruct((s, D_MODEL), bf16),
                   jax.ShapeDtypeStruct((s, D_MODEL), bf16), jax.ShapeDtypeStruct((s, D_ATTN), bf16),
                   jax.ShapeDtypeStruct((s, D_POOL), f32), jax.ShapeDtypeStruct((1, 2 * D_MODEL), f32)],
    )(dh1_bf, wout, o_bf, pm_bf, z, z, wba, wbp)


def _pool_bwd(dpm, mixed, pooled_bf, w_pool, pool_scale):
    s = dpm.shape[0]

    def body(dpm_ref, mixed_ref, pooled_ref, w_ref, sc_ref, du_ref, dw_ref, dsc_ref, pad):
        g = pl.program_id(0)
        dpm_v = dpm_ref[...]
        dsc_ref[...] = jnp.sum(dpm_v * mixed_ref[...], axis=0, keepdims=True)
        dmixed = (dpm_v * sc_ref[...]).astype(bf16)
        dw_ref[...] = _dot_tn(pooled_ref[...], dmixed)
        dpooled = _dot_nt(dmixed, w_ref[...].astype(bf16))
        t = lax.broadcasted_iota(jnp.int32, (s, POOL_GROUP), 0)
        count = jnp.minimum(t + 1, jnp.left_shift(2, g)).astype(f32)
        r = dpooled / count
        pad[0:s, :] = r
        pad[s:s + MAX_WINDOW, :] = jnp.zeros((MAX_WINDOW, POOL_GROUP), f32)
        acc = r
        snaps = []
        for d in range(1, MAX_WINDOW):
            acc = acc + pad[d:d + s, :]
            if d + 1 in (2, 4, 8, 16):
                snaps.append(acc)
        du_ref[...] = (_window_select(snaps, g) - dpooled).astype(bf16)

    blk = pl.BlockSpec((s, POOL_GROUP), lambda g: (0, g))
    return _call(
        body, name="pool_bwd", grid=(4,),
        in_specs=[blk, blk, blk, pl.BlockSpec((None, POOL_GROUP, POOL_GROUP), lambda g: (g, 0, 0)),
                  pl.BlockSpec((1, POOL_GROUP), lambda g: (0, g))],
        out_specs=[blk, pl.BlockSpec((None, POOL_GROUP, POOL_GROUP), lambda g: (g, 0, 0)),
                   pl.BlockSpec((1, POOL_GROUP), lambda g: (0, g))],
        out_shape=[jax.ShapeDtypeStruct((s, D_POOL), bf16), jax.ShapeDtypeStruct((4, POOL_GROUP, POOL_GROUP), f32),
                   jax.ShapeDtypeStruct((1, D_POOL), f32)],
        scratch=[pltpu.VMEM((s + MAX_WINDOW, POOL_GROUP), f32)])(dpm, mixed, pooled_bf, w_pool, pool_scale)


def _attn_bwd(z, o_bf, lse, do_bf, cos_t, sin_t, nb, cargo):
    s = z.shape[0]

    def body(q_ref, k_ref, v_ref, o_ref, lse_ref, do_ref, cq_ref, sq_ref, cf_ref, sf_ref,
             dqkv_ref, kt_s, dk_s, dv_s):
        i = pl.program_id(1)

        @pl.when(i == 0)
        def _():
            dk_s[...] = jnp.zeros_like(dk_s)
            dv_s[...] = jnp.zeros_like(dv_s)
            for j in range(nb):
                kt_s[j] = k_ref[j * MOBA_BLOCK:(j + 1) * MOBA_BLOCK, :].T.astype(bf16)

        k_heads = _head_lanes((MOBA_BLOCK, LANES))
        t_heads = _head_rows((LANES, MOBA_BLOCK))

        def query_block(c):
            q = q_ref[...]
            sels = _moba_select(q.astype(bf16), k_ref[...], c, nb)
            qs = (q * SCALE).astype(bf16)
            qraw = [jnp.where(hm, q, 0.0).astype(bf16) for hm in k_heads]
            do = do_ref[...].astype(f32)
            do_bf = do_ref[...]
            dob = [jnp.where(hm, do, 0.0).astype(bf16) for hm in k_heads]
            pick = _head_lanes((STAT_ROWS, LANES))
            r8 = lax.broadcasted_iota(jnp.int32, (STAT_ROWS, LANES), 0)
            head_pick = jnp.where(((r8 == 0) & pick[0]) | ((r8 == 1) & pick[1]), 1.0, 0.0)
            delta8 = lax.dot_general(head_pick, do * o_ref[...].astype(f32), (((1,), (1,)), ((), ())),
                                     precision=lax.Precision.HIGHEST, preferred_element_type=f32)
            lse8 = lse_ref[...]
            delta, lse_h = [delta8[0:1, :], delta8[1:2, :]], [lse8[0:1, :], lse8[1:2, :]]
            key = lax.broadcasted_iota(jnp.int32, (MOBA_BLOCK, MOBA_BLOCK), 0)
            qry = lax.broadcasted_iota(jnp.int32, (MOBA_BLOCK, MOBA_BLOCK), 1)

            def block(j, allow):
                rows = slice(j * MOBA_BLOCK, (j + 1) * MOBA_BLOCK)
                kj, vj, ktj = k_ref[rows, :], v_ref[rows, :], kt_s[j]
                dqt, dk, dv = None, None, None
                for h in range(2):
                    sc = _dot_nt(jnp.where(k_heads[h], kj, 0.0).astype(bf16), qs)
                    p = jnp.where(allow[h], jnp.exp(sc - lse_h[h]), 0.0)
                    dp = _dot_nt(jnp.where(k_heads[h], vj, 0.0).astype(bf16), do_bf)
                    ds = (p * (dp - delta[h]) * SCALE).astype(bf16)
                    tv = _dot(p.astype(bf16), dob[h])
                    tk = _dot(ds, qraw[h])
                    tq = _dot(jnp.where(t_heads[h], ktj, jnp.zeros_like(ktj)), ds)
                    dqt, dk, dv = (tq, tk, tv) if dqt is None else (dqt + tq, dk + tk, dv + tv)
                dk_s[rows, :] += dk
                dv_s[rows, :] += dv
                return dqt

            dqt = block(c, [key <= qry] * 2)
            for j in range(c):
                dqt = dqt + block(j, [sels[h][j:j + 1, :] > 0.0 for h in range(2)])
            dqkv_ref[0, c * MOBA_BLOCK:(c + 1) * MOBA_BLOCK, :] = _rope_bwd(dqt.T, cq_ref[...], sq_ref[...]).astype(bf16)

        for c in range(nb):
            pl.when(i == c)(functools.partial(query_block, c))

        @pl.when(i == nb - 1)
        def _():
            dqkv_ref[1] = _rope_bwd(dk_s[...], cf_ref[...], sf_ref[...]).astype(bf16)
            dqkv_ref[2] = dv_s[...].astype(bf16)

    qblk = pl.BlockSpec((MOBA_BLOCK, LANES), lambda hp, i: (i, hp))
    tq = pl.BlockSpec((MOBA_BLOCK, LANES), lambda hp, i: (i, 0))
    tf = pl.BlockSpec((s, LANES), lambda hp, i: (0, 0))
    return _cargo_call(
        body, cargo, name="attn_bwd", grid=(D_ATTN // LANES, nb),
        in_specs=[qblk, pl.BlockSpec((s, LANES), lambda hp, i: (0, 4 + hp)),
                  pl.BlockSpec((s, LANES), lambda hp, i: (0, 8 + hp)), qblk,
                  pl.BlockSpec((None, STAT_ROWS, MOBA_BLOCK), lambda hp, i: (hp, 0, i)), qblk, tq, tq, tf, tf],
        out_specs=[pl.BlockSpec((3, s, LANES), lambda hp, i: (0, 0, hp))],
        out_shape=[jax.ShapeDtypeStruct((3, s, D_ATTN), bf16)],
        scratch=[pltpu.VMEM((nb, LANES, MOBA_BLOCK), bf16), pltpu.VMEM((s, LANES), f32), pltpu.VMEM((s, LANES), f32)],
    )(z, z, z, o_bf, lse, do_bf, cos_t, sin_t, cos_t, sin_t)


def _inproj_dx(dqkv, du, dzg, win_g, dh1, more_items, place, cargo):
    s = du.shape[0]
    tm, half = 256, D_ATTN

    def compute(ins, outs, tile):
        dqkv_ref, du_ref, dzg_ref, w_ref, dh_ref = ins
        acc = ALPHA * dh_ref[...]
        for n, part in enumerate([dqkv_ref[0], dqkv_ref[1], dqkv_ref[2], du_ref[...]]):
            acc = acc + _dot_nt(part, w_ref[n // 2, :, (n % 2) * half:(n % 2 + 1) * half])
        for sh in range(2, N_CHIPS):
            acc = acc + _dot_nt(dzg_ref[:, (sh - 2) * D_MODEL:(sh - 1) * D_MODEL], w_ref[sh])
        outs[0][...] = acc

    row = lambda tile: pl.BlockSpec((tm, D_MODEL), lambda t, pc: (tile(t), 0))
    ins = lambda tile: [pl.BlockSpec((3, tm, half), lambda t, pc: (0, tile(t), 0)),
                        pl.BlockSpec((tm, half), lambda t, pc: (tile(t), 0)),
                        pl.BlockSpec((tm, 2 * D_MODEL), lambda t, pc: (tile(t), 0)),
                        pl.BlockSpec((N_CHIPS, D_MODEL, D_MODEL), lambda t, pc: (0, 0, 0)), row(tile)]
    main = (s // tm, ins, lambda tile: [row(tile)], [jax.ShapeDtypeStruct((s, D_MODEL), f32)],
            [dqkv, du, dzg, win_g, dh1], compute)
    outs, carried = _batched_call([main] + more_items, "inproj_dx", place, cargo)
    return (outs[0][0], [o[0] for o in outs[1:]]), carried


ANY = pl.BlockSpec(memory_space=pl.ANY)


def _chip_index():
    return 2 * lax.axis_index("x") + lax.axis_index("y")


def _peer(k):
    x, y, c = lax.axis_index("x"), lax.axis_index("y"), lax.axis_index("c")
    return (x ^ (k >> 1), y ^ (k & 1), c)


def _sibling():
    return (lax.axis_index("x"), lax.axis_index("y"), 1 - lax.axis_index("c"))


class _GatherCargo:
    aliased = ()

    def __init__(self, shards, pass_on):
        self.shards, self.pass_on, n = shards, pass_on, len(shards)
        self.inputs = [a.reshape(2, a.shape[0] // 2, a.shape[1]) for a in shards]
        self.out_shape = [jax.ShapeDtypeStruct((N_CHIPS,) + a.shape, a.dtype) for a in self.inputs]
        two, three, one = [pltpu.SemaphoreType.DMA(shp) for shp in ((2, n), (N_CHIPS - 1, n), (n,))]
        self.sems = [two, two, two, two, three, three, one, one]

    def _relay_rows(self, a):
        h = self.inputs[a].shape[1]
        cut = h // 2 if h % 32 == 0 else h
        return {1: (0, cut), 2: (cut, h)}

    def _copies(self, src, dst, sems, a):
        send_sems, recv_sems, rsend_sems, rrecv_sems, fsend_sems, frecv_sems, local_sems, own_sems = sems
        p, c = _chip_index(), lax.axis_index("c")
        copy = lambda s, d, **kw: functools.partial(pltpu.make_async_remote_copy, src_ref=s, dst_ref=d, **kw)
        own = copy(src[a], dst[a].at[p], send_sem=local_sems.at[a], recv_sem=own_sems.at[a], device_id=_sibling(),
                   device_id_type=MESH)
        out, arrive, relay, relayed, onward, landed = {}, {}, {}, {}, {}, {}
        for k in range(1, N_CHIPS):
            d2d = dict(send_sem=fsend_sems.at[k - 1, a], recv_sem=frecv_sems.at[k - 1, a], device_id=_sibling(),
                       device_id_type=MESH)
            got, theirs = dst[a].at[p ^ k, c], dst[a].at[p ^ k, 1 - c]
            onward[k] = copy(got, got, **d2d)
            landed[k] = copy(theirs, theirs, **d2d)
        for k, (lo, hi) in self._relay_rows(a).items():
            ici = dict(send_sem=send_sems.at[k - 1, a], recv_sem=recv_sems.at[k - 1, a], device_id=_peer(k),
                       device_id_type=MESH)
            out[k] = copy(src[a].at[c], dst[a].at[p, c], **ici)
            arrive[k] = copy(src[a].at[c], dst[a].at[p ^ k, c], **ici)
            if hi > lo:
                via = dict(send_sem=rsend_sems.at[k - 1, a], recv_sem=rrecv_sems.at[k - 1, a], device_id=_peer(3 - k),
                           device_id_type=MESH)
                rows = dst[a].at[p ^ k, c, lo:hi]
                relay[k] = copy(rows, rows, **via)
                mine = dst[a].at[p ^ 3, c, lo:hi]
                relayed[k] = copy(mine, mine, **via)
        return own, out, arrive, relay, relayed, onward, landed

    def stages(self, steps):
        n = len(self.shards)

        def start(src, dst, sems):
            for a in range(n):
                own, out, _, _, _, _, _ = self._copies(src, dst, sems, a)
                own().start()
                for cp in out.values():
                    cp().start()

        def pass_on(a):
            def act(src, dst, sems):
                _, _, arrive, relay, _, onward, _ = self._copies(src, dst, sems, a)
                for k in arrive:
                    arrive[k]().wait_recv()
                    if k in relay:
                        relay[k]().start()
                    onward[k]().start()
            return act

        def finish(src, dst, sems):
            for a in range(n):
                _, _, _, _, relayed, onward, _ = self._copies(src, dst, sems, a)
                for cp in relayed.values():
                    cp().wait_recv()
                onward[3]().start()
            for a in range(n):
                own, out, _, relay, _, onward, landed = self._copies(src, dst, sems, a)
                for cp in landed.values():
                    cp().wait_recv()
                for cp in list(out.values()) + list(relay.values()) + list(onward.values()):
                    cp().wait_send()
                own().wait()

        mids = [(min(steps - 1, int(self.pass_on[a] * steps)), pass_on(a)) for a in range(n)]
        return [(0, start)] + mids + [(steps - 1, finish)]

    def results(self, outs):
        return [o.reshape((N_CHIPS,) + a.shape) for o, a in zip(outs, self.shards)]


class _ExchangeCargo:
    aliased = ()

    def __init__(self, sums, whole):
        self.inputs, self.whole, n = sums, whole, len(sums)
        self.out_shape = [jax.ShapeDtypeStruct((N_CHIPS - 1,) + g.shape[1:], g.dtype) for g in sums]
        sem = pltpu.SemaphoreType.DMA((N_CHIPS - 1, n))
        self.sems = [sem, sem]

    def _copies(self, src, dst, sems):
        send_sems, recv_sems = sems
        p = _chip_index()
        return [pltpu.make_async_remote_copy(
            src_ref=src[a].at[0] if self.whole[a] else src[a].at[p ^ k], dst_ref=dst[a].at[k - 1],
            send_sem=send_sems.at[k - 1, a], recv_sem=recv_sems.at[k - 1, a], device_id=_peer(k), device_id_type=MESH)
            for k in range(1, N_CHIPS) for a in range(len(self.inputs))]

    def stages(self, steps):
        def start(src, dst, sems):
            for cp in self._copies(src, dst, sems):
                cp.start()

        def finish(src, dst, sems):
            copies = self._copies(src, dst, sems)
            for cp in copies:
                cp.wait_recv()
            for cp in copies:
                cp.wait_send()

        return [(0, start), (steps - 1, finish)]

    def results(self, outs):
        return list(outs)


class _SwapCargo:
    aliased = ()

    def __init__(self, split):
        self.inputs, n = split, len(split)
        self.out_shape = [jax.ShapeDtypeStruct((g.shape[0],) + g.shape[2:], g.dtype) for g in split]
        sem = pltpu.SemaphoreType.DMA((n,))
        self.sems = [sem, sem]

    def _copies(self, src, dst, sems):
        c = lax.axis_index("c")
        return [pltpu.make_async_remote_copy(src_ref=src[a].at[:, 1 - c], dst_ref=dst[a], send_sem=sems[0].at[a],
                                             recv_sem=sems[1].at[a], device_id=_sibling(), device_id_type=MESH)
                for a in range(len(self.inputs))]

    def stages(self, steps):
        def start(src, dst, sems):
            for cp in self._copies(src, dst, sems):
                cp.start()

        def finish(src, dst, sems):
            copies = self._copies(src, dst, sems)
            for cp in copies:
                cp.wait_recv()
            for cp in copies:
                cp.wait_send()

        return [(0, start), (steps - 1, finish)]

    def results(self, outs):
        return list(outs)


class _FillCargo:
    def __init__(self, bufs):
        self.inputs, n = bufs, len(bufs)
        self.out_shape = [jax.ShapeDtypeStruct(b.shape, b.dtype) for b in bufs]
        self.aliased = [(a, a) for a in range(n)]
        sem = pltpu.SemaphoreType.DMA((n,))
        self.sems = [sem, sem]

    def _copies(self, src, dst, sems, landing):
        c = lax.axis_index("c")
        half = 1 - c if landing else c
        return [pltpu.make_async_remote_copy(src_ref=src[a].at[c], dst_ref=dst[a].at[half], send_sem=sems[0].at[a],
                                             recv_sem=sems[1].at[a], device_id=_sibling(), device_id_type=MESH)
                for a in range(len(self.inputs))]

    def stages(self, steps):
        def start(src, dst, sems):
            for cp in self._copies(src, dst, sems, False):
                cp.start()

        def finish(src, dst, sems):
            for cp in self._copies(src, dst, sems, True):
                cp.wait_recv()
            for cp in self._copies(src, dst, sems, False):
                cp.wait_send()

        return [(0, start), (steps - 1, finish)]

    def results(self, outs):
        return [o.reshape(2 * b.shape[1], b.shape[2]) for o, b in zip(outs, self.inputs)]


class _MultiCargo:
    def __init__(self, cargos):
        self.cargos = cargos
        self.inputs = [a for cg in cargos for a in cg.inputs]
        self.out_shape = [o for cg in cargos for o in cg.out_shape]
        self.sems = [s for cg in cargos for s in cg.sems]
        self.aliased, i0, o0 = [], 0, 0
        for cg in cargos:
            self.aliased += [(i0 + i, o0 + o) for i, o in cg.aliased]
            i0, o0 = i0 + len(cg.inputs), o0 + len(cg.out_shape)

    def _parts(self, refs, count):
        out, off = [], 0
        for cg in self.cargos:
            out.append(refs[off:off + count(cg)])
            off += count(cg)
        return out

    def stages(self, steps):
        merged = []
        for n, cg in enumerate(self.cargos):
            for at, act in cg.stages(steps):
                def part(src, dst, sems, n=n, act=act):
                    act(self._parts(src, lambda g: len(g.inputs))[n], self._parts(dst, lambda g: len(g.out_shape))[n],
                        self._parts(sems, lambda g: len(g.sems))[n])
                merged.append((at, part))
        starts = [m for m in merged if m[0] == 0]
        return starts + sorted([m for m in merged if m[0] != 0], key=lambda m: m[0])

    def results(self, outs):
        return [cg.results(part) for cg, part in zip(self.cargos, self._parts(list(outs), lambda g: len(g.out_shape)))]


def _cargo_call(body, cargo, *, name, grid, in_specs, out_specs, out_shape, scratch=(), prefetch=None):
    n_in, n_out, n_scr = len(in_specs), len(out_specs), len(scratch)
    c_in, c_out = len(cargo.inputs), len(cargo.out_shape)
    steps = 1
    for g in grid:
        steps *= g
    stages = cargo.stages(steps)

    def wrapped(*refs):
        head, refs = (refs[:1], refs[1:]) if prefetch is not None else ((), refs)
        ins, refs = refs[:n_in], refs[n_in:]
        cin, refs = refs[:c_in], refs[c_in:]
        outs, refs = refs[:n_out], refs[n_out:]
        cout, refs = refs[:c_out], refs[c_out:]
        scr, sems = refs[:n_scr], refs[n_scr:]
        if not grid:
            for _, act in stages:
                act(cin, cout, sems)
            return
        step = 0
        for d in range(len(grid)):
            step = step * grid[d] + pl.program_id(d)
        for at, act in stages:
            if at == 0:
                pl.when(step == 0)(functools.partial(act, cin, cout, sems))
        body(*head, *ins, *outs, *scr)
        for at, act in stages:
            if at > 0:
                pl.when(step == at)(functools.partial(act, cin, cout, sems))

    params = dict(vmem_limit_bytes=VMEM_LIMIT)
    if grid:
        params["dimension_semantics"] = ("arbitrary",) * len(grid)
    specs = dict(grid=grid, in_specs=list(in_specs) + [ANY] * c_in, out_specs=list(out_specs) + [ANY] * c_out,
                 scratch_shapes=list(scratch) + cargo.sems)
    if prefetch is not None:
        specs = dict(grid_spec=pltpu.PrefetchScalarGridSpec(num_scalar_prefetch=1, **specs))
    head = () if prefetch is None else (prefetch,)
    res = pl.pallas_call(wrapped, name=name, out_shape=list(out_shape) + cargo.out_shape,
                         input_output_aliases={len(head) + n_in + i: n_out + o for i, o in cargo.aliased},
                         compiler_params=pltpu.CompilerParams(**params), **specs)
    return lambda *args: (lambda r: (r[:n_out], cargo.results(r[n_out:])))(res(*head, *args, *cargo.inputs))


def _split_halves(g):
    return g.reshape(g.shape[0], 2, g.shape[1] // 2, g.shape[2])


def _ew_tile(rows):
    return rows if rows <= 384 else 256


def _batched_call(items, name, place=None, cargo=None):
    offs, total = [], 0
    for it in items:
        offs.append(total)
        total += it[0]
    in_specs, out_specs, out_shape, args, spans = [], [], [], [], []
    for (n, ins, outs, shapes, operands, _), off in zip(items, offs):
        tile = lambda t, off=off, n=n: jnp.clip(t - off, 0, n - 1)
        i_s, o_s = ins(tile), outs(tile)
        spans.append((len(i_s), len(o_s)))
        in_specs += i_s
        out_specs += o_s
        out_shape += shapes
        args += operands
    n_in = len(in_specs)

    def body(*refs):
        if place is not None:
            refs = refs[1:]
        t = pl.program_id(0)
        i0, o0 = 0, n_in
        for (n, _, _, _, _, compute), off, (ni, no) in zip(items, offs, spans):
            pl.when((t >= off) & (t < off + n))(functools.partial(compute, refs[i0:i0 + ni], refs[o0:o0 + no], t - off))
            i0, o0 = i0 + ni, o0 + no

    params = pltpu.CompilerParams(dimension_semantics=("arbitrary",), vmem_limit_bytes=VMEM_LIMIT)
    carried = None
    if cargo is not None:
        res, carried = _cargo_call(body, cargo, name=name, grid=(total,), in_specs=in_specs, out_specs=out_specs,
                                   out_shape=out_shape, prefetch=place)(*args)
    elif place is None:
        res = pl.pallas_call(body, name=name, grid=(total,), in_specs=in_specs, out_specs=out_specs,
                             out_shape=out_shape, compiler_params=params)(*args)
    else:
        res = pl.pallas_call(
            body, name=name, out_shape=out_shape, compiler_params=params,
            grid_spec=pltpu.PrefetchScalarGridSpec(num_scalar_prefetch=1, grid=(total,), in_specs=in_specs,
                                                   out_specs=out_specs))(place, *args)
    out, o0 = [], 0
    for _, no in spans:
        out.append(res[o0:o0 + no])
        o0 += no
    return out if cargo is None else (out, carried)


def _add_pairs(pairs, place, name):
    def item(mine, theirs):
        lead, _, rows, cols = mine.shape
        tm = _ew_tile(rows)

        def compute(ins, outs, tile):
            outs[0][...] = (ins[0][...].astype(f32) + ins[1][...].astype(f32)).astype(outs[0].dtype)

        blk = lambda tile: pl.BlockSpec((lead, tm, cols), lambda t, pc: (0, tile(t), 0))
        mine_half = lambda tile: pl.BlockSpec((lead, None, tm, cols), lambda t, pc: (0, pc[1], tile(t), 0))
        return (rows // tm, lambda tile: [mine_half(tile), blk(tile)], lambda tile: [blk(tile)],
                [jax.ShapeDtypeStruct(theirs.shape, theirs.dtype)], [mine, theirs], compute)

    return [out[0] for out in _batched_call([item(a, b) for a, b in pairs], name, place)]


def _cargo_alone(cargo, name):
    return _cargo_call(None, cargo, name=name, grid=(), in_specs=[], out_specs=[], out_shape=[])()[1]


def _bf16_casts(x, mats, padded, cargo, name):
    def item(a, to_rows, tm):
        rows, cols = a.shape
        full = rows // tm

        def compute(ins, outs, tile):
            outs[0][...] = jnp.where(tile < full, ins[0][...], 0.0).astype(bf16)

        return (to_rows // tm, lambda tile: [pl.BlockSpec((tm, cols), lambda t: (jnp.minimum(tile(t), full - 1), 0))],
                lambda tile: [pl.BlockSpec((tm, cols), lambda t: (tile(t), 0))],
                [jax.ShapeDtypeStruct((to_rows, cols), bf16)], [a], compute)

    items = [item(x, x.shape[0], 128)] + [item(w, r, 64 if r > w.shape[0] else 128) for w, r in zip(mats, padded)]
    outs, carried = _batched_call(items, name, cargo=cargo)
    return [o[0] for o in outs], carried


def _sum_slots_items(jobs, max_rows=None):
    def item(own, recv, whole):
        _, rows, cols = own.shape
        tm = _ew_tile(rows)
        while max_rows is not None and tm > max_rows and tm % 16 == 0:
            tm //= 2

        def compute(ins, outs, tile):
            r0, r1, r2 = [ins[1][k].astype(f32) for k in range(N_CHIPS - 1)]
            outs[0][...] = (ins[0][...].astype(f32) + r0) + (r1 + r2)

        ins = lambda tile: [pl.BlockSpec((None, tm, cols), lambda t, pc: (0 if whole else pc[0], tile(t), 0)),
                            pl.BlockSpec((N_CHIPS - 1, tm, cols), lambda t, pc: (0, tile(t), 0))]
        outs = lambda tile: [pl.BlockSpec((None, tm, cols), lambda t, pc: (pc[1], tile(t), 0))]
        return rows // tm, ins, outs, [jax.ShapeDtypeStruct((2, rows, cols), f32)], [own, recv], compute

    return [item(*job) for job in jobs]


class _GradReducer:
    def __init__(self, place):
        self.place, self.split, self.sums, self.slots, self.bufs, self.full = place, {}, {}, {}, {}, {}

    def swap(self, named):
        self.split.update({nm: _split_halves(g) for nm, g in named})
        return _SwapCargo([self.split[nm] for nm, _ in named])

    def summed(self, names, got, tag):
        sums = _add_pairs([(self.split[nm], g) for nm, g in zip(names, got)], self.place, "presum_" + tag)
        self.sums.update(zip(names, sums))

    def exchange(self, names):
        return _ExchangeCargo([self.sums[nm] for nm in names], [nm == "small" for nm in names])

    def arrived(self, names, slots):
        self.slots.update(zip(names, slots))

    def tree_sum_items(self, names, max_rows=None):
        return _sum_slots_items([(self.sums[nm], self.slots[nm], nm == "small") for nm in names], max_rows)

    def tree_summed(self, names, bufs):
        self.bufs.update(zip(names, bufs))

    def fill(self, names):
        return _FillCargo([self.bufs[nm] for nm in names])

    def filled(self, names, grads):
        self.full.update(zip(names, grads))

    def reduce(self, names):
        rest = [nm for nm in names if nm not in self.bufs]
        outs = _batched_call(self.tree_sum_items(rest), "sum_slots", self.place)
        self.tree_summed(rest, [out[0] for out in outs])
        rest = [nm for nm in names if nm not in self.full]
        self.filled(rest, _cargo_alone(self.fill(rest), "sibling_fill"))
        return [self.full[nm] for nm in names]


def _adamw_update(w, m, v, g):
    m = ADAM_B1 * m + (1.0 - ADAM_B1) * g
    v = ADAM_B2 * v + (1.0 - ADAM_B2) * jnp.square(g)
    m_hat = m / (1.0 - ADAM_B1 ** ADAM_STEP)
    v_hat = v / (1.0 - ADAM_B2 ** ADAM_STEP)
    return -ADAM_LR * (m_hat / (jnp.sqrt(v_hat) + ADAM_EPS) + ADAM_WD * w), m, v


def _adamw(jobs, name):
    def item(w, m, v, grad, tm):
        rows, cols = w.shape
        gcols = grad.shape[1]

        def compute(ins, outs, tile):
            g = ins[3][0:tm, 0:cols]
            outs[0][...] = g
            outs[1][...], outs[2][...], outs[3][...] = _adamw_update(ins[0][...], ins[1][...], ins[2][...], g)

        blk = lambda tile: pl.BlockSpec((tm, cols), lambda t: (tile(t), 0))
        if tm == rows:
            gspec = lambda tile: pl.BlockSpec(grad.shape, lambda t: (0, 0))
        else:
            gspec = lambda tile: pl.BlockSpec((tm, gcols), lambda t: (tile(t), 0))
        return (rows // tm, lambda tile: [blk(tile)] * 3 + [gspec(tile)], lambda tile: [blk(tile)] * 4,
                [jax.ShapeDtypeStruct((rows, cols), f32)] * 4, [w, m, v, grad], compute)

    return _batched_call([item(*job) for job in jobs], name)


def _rope_tables(s):
    half = HEAD_DIM // 2
    inv_freq = 1.0 / (10000.0 ** (jnp.arange(half, dtype=f32) / half))
    ang = jnp.arange(s, dtype=f32)[:, None] * inv_freq[None, :]
    cos, sin = jnp.cos(ang), jnp.sin(ang)
    return jnp.tile(cos, (1, LANES // half)), jnp.tile(jnp.concatenate([-sin, sin], axis=1), (1, LANES // HEAD_DIM))


def _local_step(x, x_bf, target, win_g, gather, comm, b_gate, w_pool, pool_scale, ln1_g, ln1_b, convb_g, ln2_g, ln2_b):
    s = x.shape[0]
    nb = s // MOBA_BLOCK
    cos_t, sin_t = _rope_tables(s)

    z, (wba_g, wbp_g, wout_g) = _inproj(x_bf, win_g, cos_t, sin_t, b_gate, gather["inproj"])
    wout = wout_g.reshape(D_MODEL, D_MODEL)
    (o_bf, lse, pooled_bf, mixed, pm_bf), (wg_g, wu_g, convw_g) = _attn_fwd(z, nb, w_pool, pool_scale,
                                                                           gather["attn_fwd"])
    m_bf, x1_bf, xhat1, rstd1 = _merge_out_ln1(o_bf, pm_bf, z, wba_g, wbp_g, wout, x, ln1_g, ln1_b)
    (a, up, ac, hh_bf), (wd_g,) = _ffn_up(x1_bf, wg_g, wu_g, convw_g, convb_g, gather["ffn_up"])
    wd = wd_g.reshape(D_FF_PAD, D_MODEL)
    loss, dh2, dh2_bf, d_ln2_g, d_ln2_b = _ffn_down_ln2_loss(hh_bf, wd, xhat1, ln1_g, ln1_b, ln2_g, ln2_b, target)

    da_bf, dup_bf, dconv = _ffn_act_bwd(dh2_bf, wd_g, ac, a, up, convw_g)
    ffn_shape = (N_CHIPS, FF_PAD, D_MODEL)
    d_wd = _matmul_tn(hh_bf, dh2_bf, 1, "dw_ffn_down").reshape(ffn_shape)
    d_wg, got = _matmul_tn(da_bf, x1_bf, 1, "dw_ffn_gate", cargo=comm.swap([("w_ffn_down", d_wd)]))
    comm.summed(["w_ffn_down"], got, "ffn_down")
    d_wg = d_wg.reshape(ffn_shape)
    d_wu = _matmul_tn(dup_bf, x1_bf, 1, "dw_ffn_up").reshape(ffn_shape)
    gate_up = ["w_ffn_gate", "w_ffn_up", "conv_w"]
    (dh1, dh1_bf, d_ln1_g, d_ln1_b), (slots, got) = _ffn_in_bwd_ln1(
        da_bf, dup_bf, wg_g, wu_g, dh2, xhat1, rstd1, ln1_g,
        _MultiCargo([comm.exchange(["w_ffn_down"]), comm.swap(list(zip(gate_up, [d_wg, d_wu, dconv])))]))
    comm.arrived(["w_ffn_down"], slots)
    comm.summed(gate_up, got, "ffn_gate_up")
    d_wout = _matmul_tn(m_bf, dh1_bf, 1, "dw_out").reshape(N_CHIPS, D_MODEL // N_CHIPS, D_MODEL)
    dzg_bf, dya_bf, dyp_bf, do_bf, dpm, d_bgate = _merge_bwd(dh1_bf, wout, o_bf, pm_bf, z, wba_g, wbp_g)
    d_wba, d_wbp = _dw_branches(o_bf, dya_bf, pm_bf, dyp_bf)
    du_bf, d_wpool, d_pscale = _pool_bwd(dpm, mixed, pooled_bf, w_pool, pool_scale)
    d_convb = dconv[:, 3, :FF_SHARD].reshape(1, N_CHIPS * FF_SHARD)
    small = _pack_small([d_bgate, d_wpool.reshape(-1, POOL_GROUP), d_pscale, d_ln1_g, d_ln1_b, d_ln2_g, d_ln2_b,
                         d_convb], loss)[None]
    branch = ["w_out", "w_branch_attn", "w_branch_pool", "small"]
    (dqkv_bf,), (slots, got) = _attn_bwd(
        z, o_bf, lse, do_bf, cos_t, sin_t, nb,
        _MultiCargo([comm.exchange(gate_up), comm.swap(list(zip(branch, [d_wout, d_wba, d_wbp, small])))]))
    comm.arrived(gate_up, slots)
    comm.summed(branch, got, "branch")
    d_win, slots = _dw_in(x_bf, dqkv_bf, du_bf, dzg_bf, comm.exchange(branch))
    comm.arrived(branch, slots)
    ffn = ["w_ffn_down"] + gate_up
    bufs, got = _batched_call(comm.tree_sum_items(ffn), "swap_w_in", comm.place, comm.swap([("w_in", d_win)]))
    comm.tree_summed(ffn, [b[0] for b in bufs])
    comm.summed(["w_in"], got, "w_in")
    (grad_x, bufs), (slots, full) = _inproj_dx(
        dqkv_bf, du_bf, dzg_bf, win_g, dh1, comm.tree_sum_items(branch, 192), comm.place,
        _MultiCargo([comm.exchange(["w_in"]), comm.fill(ffn)]))
    comm.tree_summed(branch, bufs)
    comm.filled(ffn, full)
    comm.arrived(["w_in"], slots)
    return grad_x


SMALL_WEIGHTS = [("b_gate", 2 * D_MODEL), ("w_pool", 4 * POOL_GROUP * POOL_GROUP), ("pool_scale", D_POOL),
                 ("ln1_g", D_MODEL), ("ln1_b", D_MODEL), ("ln2_g", D_MODEL), ("ln2_b", D_MODEL),
                 ("conv_b", N_CHIPS * FF_SHARD)]
LOSS_ROW = 588
SMALL_ROWS = 592


def _small_pieces(ref, first_row, n):
    if ref.shape[0] > 1:
        return [(slice(first_row, first_row + n // LANES), (slice(None), slice(None)))]
    return [(slice(first_row + i, first_row + i + 1), (slice(None), slice(i * LANES, (i + 1) * LANES)))
            for i in range(n // LANES)]


def _small_first_rows():
    rows, r = [], 0
    for _, n in SMALL_WEIGHTS:
        rows.append(r)
        r += n // LANES
    return rows


def _pack_small(parts, loss):
    def body(*refs):
        out = refs[-1]
        out[...] = jnp.zeros_like(out)
        for src, r0, (_, n) in zip(refs, _small_first_rows(), SMALL_WEIGHTS):
            for rows, where in _small_pieces(src, r0, n):
                out[rows, :] = src[where]
        out[LOSS_ROW:LOSS_ROW + 1, :] = refs[len(parts)][0:1, :]

    return pl.pallas_call(body, name="pack_small", out_shape=jax.ShapeDtypeStruct((SMALL_ROWS, LANES), f32))(
        *parts, loss)


def _adamw_small(g_packed, g_conv, triples):
    n_in = 3 * len(triples)

    def body(g_ref, gc_ref, *refs):
        def update(k, g, where):
            w_ref, m_ref, v_ref = refs[3 * k:3 * k + 3]
            new = (g,) + _adamw_update(w_ref[where], m_ref[where], v_ref[where], g)
            for o_ref, val in zip(refs[n_in + 4 * k:n_in + 4 * k + 4], new):
                o_ref[where] = val

        for k, (r0, (_, n)) in enumerate(zip(_small_first_rows(), SMALL_WEIGHTS)):
            for rows, where in _small_pieces(refs[3 * k], r0, n):
                update(k, g_ref[rows, :], where)
        for tap in range(3):
            update(len(SMALL_WEIGHTS), gc_ref[tap:tap + 1, 0:FF_SHARD], (tap,))

    res = pl.pallas_call(
        body, name="adamw_small",
        out_shape=[jax.ShapeDtypeStruct(t[0].shape, f32) for t in triples for _ in range(4)],
    )(g_packed, g_conv, *[a for t in triples for a in t])
    return [res[4 * k:4 * k + 4] for k in range(len(triples))]


def _pad_conv_b(cb):
    return jnp.pad(cb.reshape(N_CHIPS, FF_SHARD), ((0, 0), (0, FF_PAD - FF_SHARD)))


def kernel(x, w_in, b_gate, w_branch_attn, w_pool, pool_scale, w_branch_pool, w_out, ln1_g, ln1_b, w_ffn_gate, w_ffn_up, conv_w, conv_b, w_ffn_down, ln2_g, ln2_b, loss_target, m_w_in, m_b_gate, m_w_branch_attn, m_w_pool, m_pool_scale, m_w_branch_pool, m_w_out, m_ln1_g, m_ln1_b, m_w_ffn_gate, m_w_ffn_up, m_conv_w, m_conv_b, m_w_ffn_down, m_ln2_g, m_ln2_b, v_w_in, v_b_gate, v_w_branch_attn, v_w_pool, v_pool_scale, v_w_branch_pool, v_w_out, v_ln1_g, v_ln1_b, v_w_ffn_gate, v_w_ffn_up, v_conv_w, v_conv_b, v_w_ffn_down, v_ln2_g, v_ln2_b):
    mats = [w_branch_attn[0], w_branch_pool[0], w_out[0], w_ffn_gate[0].T, w_ffn_up[0].T, w_ffn_down[0]]
    padded = [w.shape[0] for w in mats[:3]] + [FF_PAD] * 3
    (x_bf, *casted), (win_g,) = _bf16_casts(x[0], mats, padded, _GatherCargo([w_in[0].astype(bf16)], [0.6]),
                                            "gather_w_in")
    shards = [None] + casted + [jnp.pad(conv_w[0], ((0, CONV_ROWS - 3), (0, FF_PAD - FF_SHARD)))]
    gather = {"inproj": _GatherCargo(shards[1:4], [0.45, 0.5, 0.55]),
              "attn_fwd": _GatherCargo(shards[4:6] + shards[7:], [0.65, 0.65, 0.65]),
              "ffn_up": _GatherCargo(shards[6:7], [0.5])}

    place = jnp.stack([2 * lax.axis_index("x") + lax.axis_index("y"), lax.axis_index("c")]).astype(jnp.int32)
    comm = _GradReducer(place)
    convb_g = _pad_conv_b(conv_b).reshape(N_CHIPS, 1, FF_PAD)
    grad_x = _local_step(
        x[0], x_bf, loss_target[0], win_g, gather, comm, b_gate, w_pool[0], pool_scale, ln1_g, ln1_b, convb_g, ln2_g, ln2_b)
    names = ["w_in", "w_branch_attn", "w_branch_pool", "w_out", "w_ffn_gate", "w_ffn_up", "w_ffn_down", "conv_w", "small"]
    grads = comm.reduce(names)

    weights = [w_in[0], w_branch_attn[0], w_branch_pool[0], w_out[0], w_ffn_gate[0].T, w_ffn_up[0].T, w_ffn_down[0]]
    m_in = [m_w_in[0], m_w_branch_attn[0], m_w_branch_pool[0], m_w_out[0], m_w_ffn_gate[0].T, m_w_ffn_up[0].T,
            m_w_ffn_down[0]]
    v_in = [v_w_in[0], v_w_branch_attn[0], v_w_branch_pool[0], v_w_out[0], v_w_ffn_gate[0].T, v_w_ffn_up[0].T,
            v_w_ffn_down[0]]
    tiles = [256, 256, 256, 128, 176, 176, 176]
    jobs = list(zip(weights, m_in, v_in, grads, tiles))
    updated = _adamw(jobs[:4], "adamw_rest") + _adamw(jobs[4:], "adamw_ffn")
    res = {nm: [(r.T if nm in ("w_ffn_gate", "w_ffn_up") else r)[None] for r in outs4]
           for nm, outs4 in zip(names, updated)}

    small_w = [b_gate, w_pool, pool_scale, ln1_g, ln1_b, ln2_g, ln2_b, conv_b, conv_w]
    small_m = [m_b_gate, m_w_pool, m_pool_scale, m_ln1_g, m_ln1_b, m_ln2_g, m_ln2_b, m_conv_b, m_conv_w]
    small_v = [v_b_gate, v_w_pool, v_pool_scale, v_ln1_g, v_ln1_b, v_ln2_g, v_ln2_b, v_conv_b, v_conv_w]

    def as_kernel_sees(a):
        return a.reshape(-1, POOL_GROUP) if a.ndim == 4 else a.transpose(1, 0, 2) if a.ndim == 3 else a

    small_out = _adamw_small(grads[-1], grads[-2],
                             [tuple(as_kernel_sees(a) for a in t) for t in zip(small_w, small_m, small_v)])
    for nm, w, outs4 in zip([nm for nm, _ in SMALL_WEIGHTS] + ["conv_w"], small_w, small_out):
        res[nm] = [o.transpose(1, 0, 2) if w.ndim == 3 else o.reshape(w.shape) for o in outs4]
    loss = grads[-1][LOSS_ROW, 0]

    order = ["w_in", "b_gate", "w_branch_attn", "w_pool", "pool_scale", "w_branch_pool", "w_out", "ln1_g", "ln1_b",
             "w_ffn_gate", "w_ffn_up", "conv_w", "conv_b", "w_ffn_down", "ln2_g", "ln2_b"]
    outs = [loss, grad_x[None]]
    for kind in range(4):
        outs += [res[nm][kind] for nm in order]
    return tuple(outs)
```

```python
import functools

import jax
import jax.numpy as jnp
from jax import lax
from jax.experimental import pallas as pl
from jax.experimental.pallas import tpu as pltpu

D_MODEL = 1024
HEAD_DIM = 64
D_ATTN = 512
D_POOL = 512
MOBA_BLOCK = 256
MOBA_TOPK = 3
POOL_GROUP = 128
MAX_WINDOW = 16
FF_SHARD = 704
FF_PAD = 768
D_FF_PAD = 4 * FF_PAD
N_CHIPS = 4
LANES = 128
ALPHA = (2.0 * 1) ** 0.25
LN_EPS = 1e-5
NEG = -1e30
SCALE = HEAD_DIM ** -0.5
ADAM_LR, ADAM_B1, ADAM_B2, ADAM_EPS, ADAM_WD, ADAM_STEP = 0.001, 0.9, 0.999, 1e-08, 0.01, 10
VMEM_LIMIT = 56 * 1024 * 1024
MESH = pl.DeviceIdType.MESH

bf16 = jnp.bfloat16
f32 = jnp.float32


def _dot(a, b):
    return jnp.dot(a, b, preferred_element_type=f32)


def _dot_nt(a, b):
    return lax.dot_general(a, b, (((1,), (1,)), ((), ())), preferred_element_type=f32)


def _dot_tn(a, b):
    return lax.dot_general(a, b, (((0,), (0,)), ((), ())), preferred_element_type=f32)


def _dot_shards(a, w_ref):
    return jnp.concatenate([_dot(a, w_ref[p]) for p in range(w_ref.shape[0])], axis=1)


def _dot_nt_shards(d, w_ref):
    n = w_ref.shape[2]
    acc = _dot_nt(d[:, 0:n], w_ref[0])
    for p in range(1, w_ref.shape[0]):
        acc = acc + _dot_nt(d[:, p * n:(p + 1) * n], w_ref[p])
    return acc


def _column_shards_spec(k, n):
    return pl.BlockSpec((N_CHIPS, k, n // N_CHIPS), lambda i: (0, 0, 0))


def _call(body, *, name, grid, in_specs, out_specs, out_shape, scratch=(), cargo=None):
    if cargo is not None:
        return _cargo_call(body, cargo, name=name, grid=grid, in_specs=in_specs, out_specs=out_specs,
                           out_shape=out_shape, scratch=scratch)
    return pl.pallas_call(
        body, name=name, grid=grid, in_specs=in_specs, out_specs=out_specs, out_shape=out_shape,
        scratch_shapes=list(scratch),
        compiler_params=pltpu.CompilerParams(dimension_semantics=("arbitrary",) * len(grid),
                                             vmem_limit_bytes=VMEM_LIMIT))


def _rot_half(t):
    lane = lax.broadcasted_iota(jnp.int32, t.shape, 1)
    first = (lane % HEAD_DIM) < (HEAD_DIM // 2)
    return jnp.where(first, pltpu.roll(t, LANES - HEAD_DIM // 2, 1), pltpu.roll(t, HEAD_DIM // 2, 1))


def _rope(t, cos, sin_signed):
    return t * cos + _rot_half(t) * sin_signed


def _rope_bwd(d, cos, sin_signed):
    return d * cos + _rot_half(d * sin_signed)


def _gelu_parts(a):
    cdf = 0.5 * (1.0 + lax.erf(a * (2.0 ** -0.5)))
    pdf = jnp.exp(-0.5 * a * a) * ((2.0 * jnp.pi) ** -0.5)
    return a * cdf, cdf + a * pdf


def _layer_norm(h, g, b):
    mu = jnp.mean(h, axis=-1, keepdims=True)
    xc = h - mu
    var = jnp.mean(xc * xc, axis=-1, keepdims=True)
    rstd = lax.rsqrt(var + LN_EPS)
    xhat = xc * rstd
    return xhat * g + b, xhat, rstd


def _layer_norm_bwd(dy, xhat, rstd, g):
    dxh = dy * g
    m1 = jnp.mean(dxh, axis=-1, keepdims=True)
    m2 = jnp.mean(dxh * xhat, axis=-1, keepdims=True)
    return rstd * (dxh - m1 - xhat * m2)


def _inproj(x_bf, win_g, cos_t, sin_t, b_gate, cargo):
    s = x_bf.shape[0]
    tm, tn = 1024, 512

    def body(x_ref, w_ref, cos_ref, sin_ref, b_ref, o_ref):
        j = pl.program_id(1)
        acc = _dot(x_ref[...], w_ref[...])

        @pl.when(j < 2)
        def _():
            for c in range(tn // LANES):
                sl = slice(c * LANES, (c + 1) * LANES)
                o_ref[:, sl] = _rope(acc[:, sl], cos_ref[...], sin_ref[...])

        @pl.when((j >= 2) & (j < 4))
        def _():
            o_ref[...] = acc

        @pl.when(j >= 4)
        def _():
            o_ref[...] = jax.nn.sigmoid(acc + b_ref[...])

    (z,), carried = _cargo_call(
        body, cargo, name="inproj", grid=(s // tm, 4 * D_MODEL // tn),
        in_specs=[pl.BlockSpec((tm, D_MODEL), lambda i, j: (i, 0)),
                  pl.BlockSpec((None, D_MODEL, tn), lambda i, j: (j // 2, 0, j % 2)),
                  pl.BlockSpec((tm, LANES), lambda i, j: (i, 0)),
                  pl.BlockSpec((tm, LANES), lambda i, j: (i, 0)),
                  pl.BlockSpec((1, tn), lambda i, j: (0, jnp.maximum(j - 4, 0)))],
        out_specs=[pl.BlockSpec((tm, tn), lambda i, j: (i, j))],
        out_shape=[jax.ShapeDtypeStruct((s, 4 * D_MODEL), f32)])(x_bf, win_g, cos_t, sin_t, b_gate)
    return z, carried


STAT_ROWS = 8


def _pair_rows(v0, v1, fill):
    r = lax.broadcasted_iota(jnp.int32, (STAT_ROWS, v0.shape[1]), 0)
    return jnp.where(r == 0, v0, jnp.where(r == 1, v1, fill))


def _head_lanes(shape):
    lane = lax.broadcasted_iota(jnp.int32, shape, 1)
    return lane < HEAD_DIM, lane >= HEAD_DIM


def _head_rows(shape):
    row = lax.broadcasted_iota(jnp.int32, shape, 0)
    return row < HEAD_DIM, row >= HEAD_DIM


def _moba_select(q_bf, k_all, i, nb):
    k_mean = jnp.mean(k_all.reshape(nb, MOBA_BLOCK, LANES), axis=1)
    n_io = lax.broadcasted_iota(jnp.int32, (nb, MOBA_BLOCK), 0)
    past = n_io < i
    sels = []
    for head in _head_lanes((nb, LANES)):
        gate = _dot_nt(jnp.where(head, k_mean, 0.0).astype(bf16), q_bf)
        g = jnp.where(past, gate, NEG)
        rank = jnp.zeros((nb, MOBA_BLOCK), f32)
        for m in range(nb):
            gm = g[m:m + 1, :]
            rank = rank + jnp.where((gm > g) | ((gm == g) & (m < n_io)), 1.0, 0.0)
        sels.append(jnp.where(past & (rank < MOBA_TOPK), 1.0, 0.0))
    return sels


def _attn_fwd(z, nb, w_pool, pool_scale, cargo):
    s = z.shape[0]

    def body(q_ref, k_ref, v_ref, u_ref, wp_ref, sc_ref, o_ref, lse_ref, pooled_ref, mixed_ref, pm_ref, vt_s, pad):
        pair, i = pl.program_id(0), pl.program_id(1)

        @pl.when(i == 0)
        def _():
            for j in range(nb):
                vt_s[j] = v_ref[j * MOBA_BLOCK:(j + 1) * MOBA_BLOCK, :].T.astype(bf16)
            _pool_group(pair, u_ref, wp_ref, sc_ref, pooled_ref, mixed_ref, pm_ref, pad)

        k_heads = _head_lanes((MOBA_BLOCK, LANES))
        o_heads = _head_rows((LANES, MOBA_BLOCK))

        def block(j, qs, allow, m_old, l_old):
            kj, vtj = k_ref[j * MOBA_BLOCK:(j + 1) * MOBA_BLOCK, :], vt_s[j]
            m_new, alpha, l_new, pv = [], [], [], None
            for h in range(2):
                sc = jnp.where(allow[h], _dot_nt(jnp.where(k_heads[h], kj, 0.0).astype(bf16), qs), NEG)
                mx = jnp.max(sc, axis=0, keepdims=True)
                mn = mx if m_old is None else jnp.maximum(m_old[h], mx)
                p = jnp.exp(sc - mn)
                lsum = jnp.sum(p, axis=0, keepdims=True)
                if m_old is None:
                    a = None
                else:
                    a = jnp.exp(m_old[h] - mn)
                    lsum = a * l_old[h] + lsum
                t = _dot(jnp.where(o_heads[h], vtj, jnp.zeros_like(vtj)), p.astype(bf16))
                pv = t if pv is None else pv + t
                m_new.append(mn)
                alpha.append(a)
                l_new.append(lsum)
            return m_new, alpha, l_new, pv

        def query_block(c):
            q = q_ref[...]
            sels = _moba_select(q.astype(bf16), k_ref[...], c, nb)
            qs = (q * SCALE).astype(bf16)
            key = lax.broadcasted_iota(jnp.int32, (MOBA_BLOCK, MOBA_BLOCK), 0)
            qry = lax.broadcasted_iota(jnp.int32, (MOBA_BLOCK, MOBA_BLOCK), 1)
            m, _, l, acc = block(c, qs, [key <= qry] * 2, None, None)
            for j in range(c):
                m, alpha, l, pv = block(j, qs, [sels[h][j:j + 1, :] > 0.0 for h in range(2)], m, l)
                acc = acc * jnp.where(o_heads[0], alpha[0], alpha[1]) + pv
            o_ref[...] = (acc / jnp.where(o_heads[0], l[0], l[1])).T.astype(bf16)
            lse_ref[...] = _pair_rows(m[0] + jnp.log(l[0]), m[1] + jnp.log(l[1]), 0.0)

        for c in range(nb):
            pl.when(i == c)(functools.partial(query_block, c))

    group = pl.BlockSpec((s, POOL_GROUP), lambda hp, i: (0, hp))
    return _cargo_call(
        body, cargo, name="attn_fwd", grid=(D_ATTN // LANES, nb),
        in_specs=[pl.BlockSpec((MOBA_BLOCK, LANES), lambda hp, i: (i, hp)),
                  pl.BlockSpec((s, LANES), lambda hp, i: (0, 4 + hp)),
                  pl.BlockSpec((s, LANES), lambda hp, i: (0, 8 + hp)),
                  pl.BlockSpec((s, POOL_GROUP), lambda hp, i: (0, 12 + hp)),
                  pl.BlockSpec((None, POOL_GROUP, POOL_GROUP), lambda hp, i: (hp, 0, 0)),
                  pl.BlockSpec((1, POOL_GROUP), lambda hp, i: (0, hp))],
        out_specs=[pl.BlockSpec((MOBA_BLOCK, LANES), lambda hp, i: (i, hp)),
                   pl.BlockSpec((None, STAT_ROWS, MOBA_BLOCK), lambda hp, i: (hp, 0, i)), group, group, group],
        out_shape=[jax.ShapeDtypeStruct((s, D_ATTN), bf16),
                   jax.ShapeDtypeStruct((D_ATTN // LANES, STAT_ROWS, s), f32),
                   jax.ShapeDtypeStruct((s, D_POOL), bf16), jax.ShapeDtypeStruct((s, D_POOL), f32),
                   jax.ShapeDtypeStruct((s, D_POOL), bf16)],
        scratch=[pltpu.VMEM((nb, LANES, MOBA_BLOCK), bf16), pltpu.VMEM((s + MAX_WINDOW, POOL_GROUP), f32)],
    )(z, z, z, z, w_pool, pool_scale)


def _window_select(snaps, g):
    return jnp.where(g == 0, snaps[0], jnp.where(g == 1, snaps[1], jnp.where(g == 2, snaps[2], snaps[3])))


def _pool_group(g, u_ref, w_ref, sc_ref, pooled_ref, mixed_ref, pm_ref, pad):
    s = u_ref.shape[0]
    u = u_ref[...]
    pad[0:MAX_WINDOW, :] = jnp.zeros((MAX_WINDOW, POOL_GROUP), f32)
    pad[MAX_WINDOW:MAX_WINDOW + s, :] = u
    acc = u
    snaps = []
    for d in range(1, MAX_WINDOW):
        acc = acc + pad[MAX_WINDOW - d:MAX_WINDOW - d + s, :]
        if d + 1 in (2, 4, 8, 16):
            snaps.append(acc)
    win = _window_select(snaps, g)
    t = lax.broadcasted_iota(jnp.int32, (s, POOL_GROUP), 0)
    count = jnp.minimum(t + 1, jnp.left_shift(2, g)).astype(f32)
    pooled = (win / count - u).astype(bf16)
    mixed = _dot(pooled, w_ref[...].astype(bf16))
    pooled_ref[...] = pooled
    mixed_ref[...] = mixed
    pm_ref[...] = (mixed * sc_ref[...]).astype(bf16)


def _merge_out_ln1(o_bf, pm_bf, z, wba, wbp, wout, x, ln_g, ln_b):
    s = x.shape[0]
    tm = 512

    def body(o_ref, pm_ref, ga_ref, gp_ref, wba_ref, wbp_ref, w_ref, x_ref, g_ref, b_ref,
             m_ref, x1_ref, xhat_ref, rstd_ref):
        ya = _dot_shards(o_ref[...], wba_ref)
        yp = _dot_shards(pm_ref[...], wbp_ref)
        m = (ga_ref[...] * ya + gp_ref[...] * yp).astype(bf16)
        m_ref[...] = m
        h = ALPHA * x_ref[...] + _dot(m, w_ref[...])
        y, xhat, rstd = _layer_norm(h, g_ref[...], b_ref[...])
        x1_ref[...] = y.astype(bf16)
        xhat_ref[...] = xhat
        rstd_ref[...] = jnp.broadcast_to(rstd, (tm, LANES))

    row = pl.BlockSpec((tm, D_MODEL), lambda i: (i, 0))
    vec = pl.BlockSpec((1, D_MODEL), lambda i: (0, 0))
    return _call(
        body, name="merge_out_ln1", grid=(s // tm,),
        in_specs=[pl.BlockSpec((tm, D_ATTN), lambda i: (i, 0)), pl.BlockSpec((tm, D_POOL), lambda i: (i, 0)),
                  pl.BlockSpec((tm, D_MODEL), lambda i: (i, 2)), pl.BlockSpec((tm, D_MODEL), lambda i: (i, 3)),
                  _column_shards_spec(D_ATTN, D_MODEL), _column_shards_spec(D_POOL, D_MODEL),
                  pl.BlockSpec((D_MODEL, D_MODEL), lambda i: (0, 0)), row, vec, vec],
        out_specs=[row, row, row, pl.BlockSpec((tm, LANES), lambda i: (i, 0))],
        out_shape=[jax.ShapeDtypeStruct((s, D_MODEL), bf16), jax.ShapeDtypeStruct((s, D_MODEL), bf16),
                   jax.ShapeDtypeStruct((s, D_MODEL), f32), jax.ShapeDtypeStruct((s, LANES), f32)],
    )(o_bf, pm_bf, z, z, wba, wbp, wout, x, ln_g, ln_b)


FF_TILE = 256
FF_TILES_PER_SHARD = FF_PAD // FF_TILE
CONV_PAD = 8
CONV_ROWS = 16

def _ff_weight_spec(rows):
    return pl.BlockSpec((None, rows, FF_TILE), lambda j: (j // FF_TILES_PER_SHARD, 0, j % FF_TILES_PER_SHARD))


def _ff_rows_spec():
    return pl.BlockSpec((None, FF_TILE, D_MODEL), lambda j: (j // FF_TILES_PER_SHARD, j % FF_TILES_PER_SHARD, 0))


def _row_shifts(x, shifts):
    s = x.shape[0]
    padded = jnp.concatenate([x, jnp.zeros((CONV_PAD, x.shape[1]), x.dtype)], axis=0)
    return [pltpu.roll(padded, d % (s + CONV_PAD), 0)[0:s] for d in shifts]


def _ffn_up(x1_bf, wg_g, wu_g, convw_g, convb_g, cargo):
    s = x1_bf.shape[0]

    def body(x_ref, wg_ref, wu_ref, cw_ref, cb_ref, a_ref, up_ref, ac_ref, hh_ref):
        x1 = x_ref[...]
        a = _dot_nt(x1, wg_ref[...])
        up = _dot_nt(x1, wu_ref[...])
        a_ref[...] = a
        up_ref[...] = up
        a1, a2 = _row_shifts(a, (1, 2))
        ac = a * cw_ref[2:3, :] + a1 * cw_ref[1:2, :] + a2 * cw_ref[0:1, :] + cb_ref[...]
        ac_ref[...] = ac
        hg, _ = _gelu_parts(ac)
        hh_ref[...] = (hg * up).astype(bf16)

    col = pl.BlockSpec((s, FF_TILE), lambda j: (0, j))
    wide = jax.ShapeDtypeStruct((s, D_FF_PAD), f32)
    return _call(
        body, name="ffn_up", grid=(D_FF_PAD // FF_TILE,),
        in_specs=[pl.BlockSpec((s, D_MODEL), lambda j: (0, 0)), _ff_rows_spec(), _ff_rows_spec(),
                  _ff_weight_spec(8), _ff_weight_spec(1)],
        out_specs=[col, col, col, col],
        out_shape=[wide, wide, wide, jax.ShapeDtypeStruct((s, D_FF_PAD), bf16)],
        cargo=cargo)(x1_bf, wg_g, wu_g, convw_g, convb_g)


def _ffn_down_ln2_loss(hh_bf, wd, xhat1, ln1_g, ln1_b, ln2_g, ln2_b, target):
    s = hh_bf.shape[0]
    tm = 256

    def body(hh_ref, w_ref, xh1_ref, g1_ref, b1_ref, g2_ref, b2_ref, t_ref, loss_ref, dh_ref, dhb_ref, dg_ref, db_ref):
        i = pl.program_id(0)
        x1 = xh1_ref[...] * g1_ref[...] + b1_ref[...]
        h = ALPHA * x1 + _dot(hh_ref[...], w_ref[...])
        y, xhat, rstd = _layer_norm(h, g2_ref[...], b2_ref[...])
        err = y - t_ref[...]
        part = 0.5 * jnp.sum(jnp.mean(err * err, axis=-1, keepdims=True), axis=0, keepdims=True)
        dy = err * (1.0 / D_MODEL)

        @pl.when(i == 0)
        def _():
            loss_ref[...] = jnp.zeros_like(loss_ref)
            dg_ref[...] = jnp.zeros_like(dg_ref)
            db_ref[...] = jnp.zeros_like(db_ref)

        loss_ref[...] += jnp.broadcast_to(part, loss_ref.shape)
        dg_ref[...] += jnp.sum(dy * xhat, axis=0, keepdims=True)
        db_ref[...] += jnp.sum(dy, axis=0, keepdims=True)
        dh = _layer_norm_bwd(dy, xhat, rstd, g2_ref[...])
        dh_ref[...] = dh
        dhb_ref[...] = dh.astype(bf16)

    row = pl.BlockSpec((tm, D_MODEL), lambda i: (i, 0))
    vec = pl.BlockSpec((1, D_MODEL), lambda i: (0, 0))
    return _call(
        body, name="ffn_down_ln2_loss", grid=(s // tm,),
        in_specs=[pl.BlockSpec((tm, D_FF_PAD), lambda i: (i, 0)),
                  pl.BlockSpec((D_FF_PAD, D_MODEL), lambda i: (0, 0)),
                  row, vec, vec, vec, vec, row],
        out_specs=[pl.BlockSpec((8, LANES), lambda i: (0, 0)), row, row, vec, vec],
        out_shape=[jax.ShapeDtypeStruct((8, LANES), f32), jax.ShapeDtypeStruct((s, D_MODEL), f32),
                   jax.ShapeDtypeStruct((s, D_MODEL), bf16), jax.ShapeDtypeStruct((1, D_MODEL), f32),
                   jax.ShapeDtypeStruct((1, D_MODEL), f32)])(hh_bf, wd, xhat1, ln1_g, ln1_b, ln2_g, ln2_b, target)


def _ffn_act_bwd(dh2_bf, wd_g, ac, a, up, convw_g):
    s = dh2_bf.shape[0]

    def body(dh_ref, wd_ref, ac_ref, a_ref, up_ref, cw_ref, da_ref, dup_ref, dc_ref):
        dhh = _dot_nt(dh_ref[...], wd_ref[...])
        hg, dgelu = _gelu_parts(ac_ref[...])
        dup_ref[...] = (dhh * hg).astype(bf16)
        dac = dhh * up_ref[...] * dgelu
        dac1, dac2 = _row_shifts(dac, (-1, -2))
        a = a_ref[...]
        rows = [jnp.sum(d * a, axis=0, keepdims=True) for d in (dac2, dac1, dac)]
        rows.append(jnp.sum(dac, axis=0, keepdims=True))
        rows.append(jnp.zeros((CONV_ROWS - 4, FF_TILE), f32))
        dc_ref[...] = jnp.concatenate(rows, axis=0)
        da_ref[...] = (dac * cw_ref[2:3, :] + dac1 * cw_ref[1:2, :] + dac2 * cw_ref[0:1, :]).astype(bf16)

    col = pl.BlockSpec((s, FF_TILE), lambda j: (0, j))
    return _call(
        body, name="ffn_act_bwd", grid=(D_FF_PAD // FF_TILE,),
        in_specs=[pl.BlockSpec((s, D_MODEL), lambda j: (0, 0)), _ff_rows_spec(), col, col, col, _ff_weight_spec(8)],
        out_specs=[col, col, _ff_weight_spec(CONV_ROWS)],
        out_shape=[jax.ShapeDtypeStruct((s, D_FF_PAD), bf16), jax.ShapeDtypeStruct((s, D_FF_PAD), bf16),
                   jax.ShapeDtypeStruct((N_CHIPS, CONV_ROWS, FF_PAD), f32)],
    )(dh2_bf, wd_g, ac, a, up, convw_g)


def _matmul_tn(a, b, n_shards, name, tm=512, tn=1024, cargo=None):
    k, m = a.shape
    n = b.shape[1]
    tm, tn = min(tm, m), min(tn, n // n_shards)
    per = n // n_shards // tn

    def body(a_ref, b_ref, o_ref, at_s):
        @pl.when(pl.program_id(1) == 0)
        def _():
            at_s[...] = a_ref[...].T

        o_ref[...] = _dot(at_s[...], b_ref[...]).astype(bf16)

    res = _call(
        body, name=name, grid=(m // tm, n // tn),
        in_specs=[pl.BlockSpec((k, tm), lambda i, j: (0, i)), pl.BlockSpec((k, tn), lambda i, j: (0, j))],
        out_specs=[pl.BlockSpec((None, tm, tn), lambda i, j: (j // per, i, j % per))],
        out_shape=[jax.ShapeDtypeStruct((n_shards, m, n // n_shards), bf16)], scratch=[pltpu.VMEM((tm, k), bf16)],
        cargo=cargo)(a, b)
    return res[0] if cargo is None else (res[0][0], res[1])


def _dw_ffn_gate_up(da_bf, dup_bf, x1_bf, cargo):
    k, m = da_bf.shape
    n = x1_bf.shape[1]
    tm = 512
    last = m // tm - 1

    def body(a0_ref, a1_ref, b_ref, o0_ref, o1_ref, at_s):
        g = pl.program_id(0)

        @pl.when(g == 0)
        def _():
            at_s[...] = a0_ref[...].T
            o0_ref[...] = _dot(at_s[...], b_ref[...]).astype(bf16)

        @pl.when(g == 1)
        def _():
            at_s[...] = a1_ref[...].T
            o1_ref[...] = _dot(at_s[...], b_ref[...]).astype(bf16)

    first = lambda g, i: jnp.where(g == 0, i, last)
    second = lambda g, i: jnp.where(g == 1, i, 0)
    shape = jax.ShapeDtypeStruct((m, n), bf16)
    return _call(
        body, name="dw_ffn_gate_up", grid=(2, m // tm),
        in_specs=[pl.BlockSpec((k, tm), lambda g, i: (0, first(g, i))),
                  pl.BlockSpec((k, tm), lambda g, i: (0, second(g, i))), pl.BlockSpec((k, n), lambda g, i: (0, 0))],
        out_specs=[pl.BlockSpec((tm, n), lambda g, i: (first(g, i), 0)),
                   pl.BlockSpec((tm, n), lambda g, i: (second(g, i), 0))],
        out_shape=[shape, shape], scratch=[pltpu.VMEM((tm, k), bf16)], cargo=cargo)(da_bf, dup_bf, x1_bf)


def _dw_branches(o_bf, dya_bf, pm_bf, dyp_bf):
    k, m = o_bf.shape
    tn = D_MODEL // N_CHIPS
    last = N_CHIPS - 1

    def body(a0_ref, b0_ref, a1_ref, b1_ref, o0_ref, o1_ref, at_s):
        g, j = pl.program_id(0), pl.program_id(1)

        @pl.when((g == 0) & (j == 0))
        def _():
            at_s[...] = a0_ref[...].T

        @pl.when((g == 1) & (j == 0))
        def _():
            at_s[...] = a1_ref[...].T

        @pl.when(g == 0)
        def _():
            o0_ref[...] = _dot(at_s[...], b0_ref[...]).astype(bf16)

        @pl.when(g == 1)
        def _():
            o1_ref[...] = _dot(at_s[...], b1_ref[...]).astype(bf16)

    first = lambda g, j: jnp.where(g == 0, j, last)
    second = lambda g, j: jnp.where(g == 1, j, 0)
    whole = pl.BlockSpec((k, m), lambda g, j: (0, 0))
    shape = jax.ShapeDtypeStruct((N_CHIPS, m, tn), bf16)
    return _call(
        body, name="dw_branches", grid=(2, N_CHIPS),
        in_specs=[whole, pl.BlockSpec((k, tn), lambda g, j: (0, first(g, j))),
                  whole, pl.BlockSpec((k, tn), lambda g, j: (0, second(g, j)))],
        out_specs=[pl.BlockSpec((None, m, tn), lambda g, j: (first(g, j), 0, 0)),
                   pl.BlockSpec((None, m, tn), lambda g, j: (second(g, j), 0, 0))],
        out_shape=[shape, shape], scratch=[pltpu.VMEM((m, k), bf16)])(o_bf, dya_bf, pm_bf, dyp_bf)


def _dw_in(x_bf, dqkv, du, dzg, cargo):
    k, m = x_bf.shape
    tm, half = 512, D_ATTN

    def body(a_ref, dqkv_ref, du_ref, dzg_ref, o_ref, at_s):
        j = pl.program_id(1)

        @pl.when(j == 0)
        def _():
            at_s[...] = a_ref[...].T
            o_ref[:, 0:half] = _dot(at_s[...], dqkv_ref[0]).astype(bf16)
            o_ref[:, half:2 * half] = _dot(at_s[...], dqkv_ref[1]).astype(bf16)

        @pl.when(j == 1)
        def _():
            o_ref[:, 0:half] = _dot(at_s[...], dqkv_ref[2]).astype(bf16)
            o_ref[:, half:2 * half] = _dot(at_s[...], du_ref[...]).astype(bf16)

        @pl.when(j >= 2)
        def _():
            o_ref[...] = _dot(at_s[...], dzg_ref[...]).astype(bf16)

    (d_win,), carried = _cargo_call(
        body, cargo, name="dw_in", grid=(m // tm, N_CHIPS),
        in_specs=[pl.BlockSpec((k, tm), lambda i, j: (0, i)), pl.BlockSpec((3, k, half), lambda i, j: (0, 0, 0)),
                  pl.BlockSpec((k, half), lambda i, j: (0, 0)),
                  pl.BlockSpec((k, D_MODEL), lambda i, j: (0, jnp.maximum(j - 2, 0)))],
        out_specs=[pl.BlockSpec((None, tm, D_MODEL), lambda i, j: (j, i, 0))],
        out_shape=[jax.ShapeDtypeStruct((N_CHIPS, m, D_MODEL), bf16)], scratch=[pltpu.VMEM((tm, k), bf16)],
    )(x_bf, dqkv, du, dzg)
    return d_win, carried


def _ffn_in_bwd_ln1(da_bf, dup_bf, wg_g, wu_g, dh2, xhat1, rstd1, ln1_g, cargo):
    s = da_bf.shape[0]
    tm = 256

    def body(da_ref, dup_ref, wg_ref, wu_ref, dh2_ref, xh_ref, rstd_ref, g_ref, dh_ref, dhb_ref, dg_ref, db_ref):
        i = pl.program_id(0)
        dx1 = ALPHA * dh2_ref[...]
        for sh in range(N_CHIPS):
            sl = slice(sh * FF_PAD, (sh + 1) * FF_PAD)
            dx1 = dx1 + _dot(da_ref[:, sl], wg_ref[sh]) + _dot(dup_ref[:, sl], wu_ref[sh])
        xhat = xh_ref[...]

        @pl.when(i == 0)
        def _():
            dg_ref[...] = jnp.zeros_like(dg_ref)
            db_ref[...] = jnp.zeros_like(db_ref)

        dg_ref[...] += jnp.sum(dx1 * xhat, axis=0, keepdims=True)
        db_ref[...] += jnp.sum(dx1, axis=0, keepdims=True)
        dh = _layer_norm_bwd(dx1, xhat, rstd_ref[:, 0:1], g_ref[...])
        dh_ref[...] = dh
        dhb_ref[...] = dh.astype(bf16)

    row = pl.BlockSpec((tm, D_MODEL), lambda i: (i, 0))
    wide = pl.BlockSpec((tm, D_FF_PAD), lambda i: (i, 0))
    wfull = pl.BlockSpec((N_CHIPS, FF_PAD, D_MODEL), lambda i: (0, 0, 0))
    vec = pl.BlockSpec((1, D_MODEL), lambda i: (0, 0))
    return _call(
        body, name="ffn_in_bwd_ln1", grid=(s // tm,),
        in_specs=[wide, wide, wfull, wfull, row, row, pl.BlockSpec((tm, LANES), lambda i: (i, 0)), vec],
        out_specs=[row, row, vec, vec],
        out_shape=[jax.ShapeDtypeStruct((s, D_MODEL), f32), jax.ShapeDtypeStruct((s, D_MODEL), bf16),
                   jax.ShapeDtypeStruct((1, D_MODEL), f32), jax.ShapeDtypeStruct((1, D_MODEL), f32)],
        cargo=cargo)(da_bf, dup_bf, wg_g, wu_g, dh2, xhat1, rstd1, ln1_g)


def _merge_bwd(dh1_bf, wout, o_bf, pm_bf, z, wba, wbp):
    s = dh1_bf.shape[0]
    tm = 256

    def body(dh_ref, wout_ref, o_ref, pm_ref, ga_ref, gp_ref, wba_ref, wbp_ref,
             dzg_ref, dya_ref, dyp_ref, do_ref, dpm_ref, dbg_ref):
        i = pl.program_id(0)
        dm = _dot_nt(dh_ref[...], wout_ref[...])
        ya = _dot_shards(o_ref[...], wba_ref)
        yp = _dot_shards(pm_ref[...], wbp_ref)
        ga, gp = ga_ref[...], gp_ref[...]
        dza = dm * ya * ga * (1.0 - ga)
        dzp = dm * yp * gp * (1.0 - gp)

        @pl.when(i == 0)
        def _():
            dbg_ref[...] = jnp.zeros_like(dbg_ref)

        dbg_ref[:, 0:D_MODEL] += jnp.sum(dza, axis=0, keepdims=True)
        dbg_ref[:, D_MODEL:2 * D_MODEL] += jnp.sum(dzp, axis=0, keepdims=True)
        dzg_ref[:, 0:D_MODEL] = dza.astype(bf16)
        dzg_ref[:, D_MODEL:2 * D_MODEL] = dzp.astype(bf16)
        dya = (dm * ga).astype(bf16)
        dyp = (dm * gp).astype(bf16)
        dya_ref[...] = dya
        dyp_ref[...] = dyp
        do_ref[...] = _dot_nt_shards(dya, wba_ref).astype(bf16)
        dpm_ref[...] = _dot_nt_shards(dyp, wbp_ref)

    row = pl.BlockSpec((tm, D_MODEL), lambda i: (i, 0))
    half = pl.BlockSpec((tm, D_ATTN), lambda i: (i, 0))
    full = lambda r, c: pl.BlockSpec((r, c), lambda i: (0, 0))
    return _call(
        body, name="merge_bwd", grid=(s // tm,),
        in_specs=[row, full(D_MODEL, D_MODEL), half, half,
                  pl.BlockSpec((tm, D_MODEL), lambda i: (i, 2)), pl.BlockSpec((tm, D_MODEL), lambda i: (i, 3)),
                  _column_shards_spec(D_ATTN, D_MODEL), _column_shards_spec(D_POOL, D_MODEL)],
        out_specs=[pl.BlockSpec((tm, 2 * D_MODEL), lambda i: (i, 0)), row, row, half, half,
                   pl.BlockSpec((1, 2 * D_MODEL), lambda i: (0, 0))],
        out_shape=[jax.ShapeDtypeStruct((s, 2 * D_MODEL), bf16), jax.ShapeDtypeStruct((s, D_MODEL), bf16),
                   jax.ShapeDtypeStruct((s, D_MODEL), bf16), jax.ShapeDtypeStruct((s, D_ATTN), bf16),
                   jax.ShapeDtypeStruct((s, D_POOL), f32), jax.ShapeDtypeStruct((1, 2 * D_MODEL), f32)],
    )(dh1_bf, wout, o_bf, pm_bf, z, z, wba, wbp)


def _pool_bwd(dpm, mixed, pooled_bf, w_pool, pool_scale):
    s = dpm.shape[0]

    def body(dpm_ref, mixed_ref, pooled_ref, w_ref, sc_ref, du_ref, dw_ref, dsc_ref, pad):
        g = pl.program_id(0)
        dpm_v = dpm_ref[...]
        dsc_ref[...] = jnp.sum(dpm_v * mixed_ref[...], axis=0, keepdims=True)
        dmixed = (dpm_v * sc_ref[...]).astype(bf16)
        dw_ref[...] = _dot_tn(pooled_ref[...], dmixed)
        dpooled = _dot_nt(dmixed, w_ref[...].astype(bf16))
        t = lax.broadcasted_iota(jnp.int32, (s, POOL_GROUP), 0)
        count = jnp.minimum(t + 1, jnp.left_shift(2, g)).astype(f32)
        r = dpooled / count
        pad[0:s, :] = r
        pad[s:s + MAX_WINDOW, :] = jnp.zeros((MAX_WINDOW, POOL_GROUP), f32)
        acc = r
        snaps = []
        for d in range(1, MAX_WINDOW):
            acc = acc + pad[d:d + s, :]
            if d + 1 in (2, 4, 8, 16):
                snaps.append(acc)
        du_ref[...] = (_window_select(snaps, g) - dpooled).astype(bf16)

    blk = pl.BlockSpec((s, POOL_GROUP), lambda g: (0, g))
    return _call(
        body, name="pool_bwd", grid=(4,),
        in_specs=[blk, blk, blk, pl.BlockSpec((None, POOL_GROUP, POOL_GROUP), lambda g: (g, 0, 0)),
                  pl.BlockSpec((1, POOL_GROUP), lambda g: (0, g))],
        out_specs=[blk, pl.BlockSpec((None, POOL_GROUP, POOL_GROUP), lambda g: (g, 0, 0)),
                   pl.BlockSpec((1, POOL_GROUP), lambda g: (0, g))],
        out_shape=[jax.ShapeDtypeStruct((s, D_POOL), bf16), jax.ShapeDtypeStruct((4, POOL_GROUP, POOL_GROUP), f32),
                   jax.ShapeDtypeStruct((1, D_POOL), f32)],
        scratch=[pltpu.VMEM((s + MAX_WINDOW, POOL_GROUP), f32)])(dpm, mixed, pooled_bf, w_pool, pool_scale)


def _attn_bwd(z, o_bf, lse, do_bf, cos_t, sin_t, nb, cargo):
    s = z.shape[0]

    def body(q_ref, k_ref, v_ref, o_ref, lse_ref, do_ref, cq_ref, sq_ref, cf_ref, sf_ref,
             dqkv_ref, kt_s, dk_s, dv_s):
        i = pl.program_id(1)

        @pl.when(i == 0)
        def _():
            dk_s[...] = jnp.zeros_like(dk_s)
            dv_s[...] = jnp.zeros_like(dv_s)
            for j in range(nb):
                kt_s[j] = k_ref[j * MOBA_BLOCK:(j + 1) * MOBA_BLOCK, :].T.astype(bf16)

        k_heads = _head_lanes((MOBA_BLOCK, LANES))
        t_heads = _head_rows((LANES, MOBA_BLOCK))

        def query_block(c):
            q = q_ref[...]
            sels = _moba_select(q.astype(bf16), k_ref[...], c, nb)
            qs = (q * SCALE).astype(bf16)
            qraw = [jnp.where(hm, q, 0.0).astype(bf16) for hm in k_heads]
            do = do_ref[...].astype(f32)
            do_bf = do_ref[...]
            dob = [jnp.where(hm, do, 0.0).astype(bf16) for hm in k_heads]
            pick = _head_lanes((STAT_ROWS, LANES))
            r8 = lax.broadcasted_iota(jnp.int32, (STAT_ROWS, LANES), 0)
            head_pick = jnp.where(((r8 == 0) & pick[0]) | ((r8 == 1) & pick[1]), 1.0, 0.0)
            delta8 = lax.dot_general(head_pick, do * o_ref[...].astype(f32), (((1,), (1,)), ((), ())),
                                     precision=lax.Precision.HIGHEST, preferred_element_type=f32)
            lse8 = lse_ref[...]
            delta, lse_h = [delta8[0:1, :], delta8[1:2, :]], [lse8[0:1, :], lse8[1:2, :]]
            key = lax.broadcasted_iota(jnp.int32, (MOBA_BLOCK, MOBA_BLOCK), 0)
            qry = lax.broadcasted_iota(jnp.int32, (MOBA_BLOCK, MOBA_BLOCK), 1)

            def block(j, allow):
                rows = slice(j * MOBA_BLOCK, (j + 1) * MOBA_BLOCK)
                kj, vj, ktj = k_ref[rows, :], v_ref[rows, :], kt_s[j]
                dqt, dk, dv = None, None, None
                for h in range(2):
                    sc = _dot_nt(jnp.where(k_heads[h], kj, 0.0).astype(bf16), qs)
                    p = jnp.where(allow[h], jnp.exp(sc - lse_h[h]), 0.0)
                    dp = _dot_nt(jnp.where(k_heads[h], vj, 0.0).astype(bf16), do_bf)
                    ds = (p * (dp - delta[h]) * SCALE).astype(bf16)
                    tv = _dot(p.astype(bf16), dob[h])
                    tk = _dot(ds, qraw[h])
                    tq = _dot(jnp.where(t_heads[h], ktj, jnp.zeros_like(ktj)), ds)
                    dqt, dk, dv = (tq, tk, tv) if dqt is None else (dqt + tq, dk + tk, dv + tv)
                dk_s[rows, :] += dk
                dv_s[rows, :] += dv
                return dqt

            dqt = block(c, [key <= qry] * 2)
            for j in range(c):
                dqt = dqt + block(j, [sels[h][j:j + 1, :] > 0.0 for h in range(2)])
            dqkv_ref[0, c * MOBA_BLOCK:(c + 1) * MOBA_BLOCK, :] = _rope_bwd(dqt.T, cq_ref[...], sq_ref[...]).astype(bf16)

        for c in range(nb):
            pl.when(i == c)(functools.partial(query_block, c))

        @pl.when(i == nb - 1)
        def _():
            dqkv_ref[1] = _rope_bwd(dk_s[...], cf_ref[...], sf_ref[...]).astype(bf16)
            dqkv_ref[2] = dv_s[...].astype(bf16)

    qblk = pl.BlockSpec((MOBA_BLOCK, LANES), lambda hp, i: (i, hp))
    tq = pl.BlockSpec((MOBA_BLOCK, LANES), lambda hp, i: (i, 0))
    tf = pl.BlockSpec((s, LANES), lambda hp, i: (0, 0))
    return _cargo_call(
        body, cargo, name="attn_bwd", grid=(D_ATTN // LANES, nb),
        in_specs=[qblk, pl.BlockSpec((s, LANES), lambda hp, i: (0, 4 + hp)),
                  pl.BlockSpec((s, LANES), lambda hp, i: (0, 8 + hp)), qblk,
                  pl.BlockSpec((None, STAT_ROWS, MOBA_BLOCK), lambda hp, i: (hp, 0, i)), qblk, tq, tq, tf, tf],
        out_specs=[pl.BlockSpec((3, s, LANES), lambda hp, i: (0, 0, hp))],
        out_shape=[jax.ShapeDtypeStruct((3, s, D_ATTN), bf16)],
        scratch=[pltpu.VMEM((nb, LANES, MOBA_BLOCK), bf16), pltpu.VMEM((s, LANES), f32), pltpu.VMEM((s, LANES), f32)],
    )(z, z, z, o_bf, lse, do_bf, cos_t, sin_t, cos_t, sin_t)


def _inproj_dx(dqkv, du, dzg, win_g, dh1, more_items, place, cargo):
    s = du.shape[0]
    tm, half = 256, D_ATTN

    def compute(ins, outs, tile):
        dqkv_ref, du_ref, dzg_ref, w_ref, dh_ref = ins
        acc = ALPHA * dh_ref[...]
        for n, part in enumerate([dqkv_ref[0], dqkv_ref[1], dqkv_ref[2], du_ref[...]]):
            acc = acc + _dot_nt(part, w_ref[n // 2, :, (n % 2) * half:(n % 2 + 1) * half])
        for sh in range(2, N_CHIPS):
            acc = acc + _dot_nt(dzg_ref[:, (sh - 2) * D_MODEL:(sh - 1) * D_MODEL], w_ref[sh])
        outs[0][...] = acc

    row = lambda tile: pl.BlockSpec((tm, D_MODEL), lambda t, pc: (tile(t), 0))
    ins = lambda tile: [pl.BlockSpec((3, tm, half), lambda t, pc: (0, tile(t), 0)),
                        pl.BlockSpec((tm, half), lambda t, pc: (tile(t), 0)),
                        pl.BlockSpec((tm, 2 * D_MODEL), lambda t, pc: (tile(t), 0)),
                        pl.BlockSpec((N_CHIPS, D_MODEL, D_MODEL), lambda t, pc: (0, 0, 0)), row(tile)]
    main = (s // tm, ins, lambda tile: [row(tile)], [jax.ShapeDtypeStruct((s, D_MODEL), f32)],
            [dqkv, du, dzg, win_g, dh1], compute)
    outs, carried = _batched_call([main] + more_items, "inproj_dx", place, cargo)
    return (outs[0][0], [o[0] for o in outs[1:]]), carried


ANY = pl.BlockSpec(memory_space=pl.ANY)


def _chip_index():
    return 2 * lax.axis_index("x") + lax.axis_index("y")


def _peer(k):
    x, y, c = lax.axis_index("x"), lax.axis_index("y"), lax.axis_index("c")
    return (x ^ (k >> 1), y ^ (k & 1), c)


def _sibling():
    return (lax.axis_index("x"), lax.axis_index("y"), 1 - lax.axis_index("c"))


class _GatherCargo:
    aliased = ()

    def __init__(self, shards, pass_on):
        self.shards, self.pass_on, n = shards, pass_on, len(shards)
        self.inputs = [a.reshape(2, a.shape[0] // 2, a.shape[1]) for a in shards]
        self.out_shape = [jax.ShapeDtypeStruct((N_CHIPS,) + a.shape, a.dtype) for a in self.inputs]
        two, three, one = [pltpu.SemaphoreType.DMA(shp) for shp in ((2, n), (N_CHIPS - 1, n), (n,))]
        self.sems = [two, two, two, two, three, three, one, one]

    def _relay_rows(self, a):
        h = self.inputs[a].shape[1]
        cut = h // 2 if h % 32 == 0 else h
        return {1: (0, cut), 2: (cut, h)}

    def _copies(self, src, dst, sems, a):
        send_sems, recv_sems, rsend_sems, rrecv_sems, fsend_sems, frecv_sems, local_sems, own_sems = sems
        p, c = _chip_index(), lax.axis_index("c")
        copy = lambda s, d, **kw: functools.partial(pltpu.make_async_remote_copy, src_ref=s, dst_ref=d, **kw)
        own = copy(src[a], dst[a].at[p], send_sem=local_sems.at[a], recv_sem=own_sems.at[a], device_id=_sibling(),
                   device_id_type=MESH)
        out, arrive, relay, relayed, onward, landed = {}, {}, {}, {}, {}, {}
        for k in range(1, N_CHIPS):
            d2d = dict(send_sem=fsend_sems.at[k - 1, a], recv_sem=frecv_sems.at[k - 1, a], device_id=_sibling(),
                       device_id_type=MESH)
            got, theirs = dst[a].at[p ^ k, c], dst[a].at[p ^ k, 1 - c]
            onward[k] = copy(got, got, **d2d)
            landed[k] = copy(theirs, theirs, **d2d)
        for k, (lo, hi) in self._relay_rows(a).items():
            ici = dict(send_sem=send_sems.at[k - 1, a], recv_sem=recv_sems.at[k - 1, a], device_id=_peer(k),
                       device_id_type=MESH)
            out[k] = copy(src[a].at[c], dst[a].at[p, c], **ici)
            arrive[k] = copy(src[a].at[c], dst[a].at[p ^ k, c], **ici)
            if hi > lo:
                via = dict(send_sem=rsend_sems.at[k - 1, a], recv_sem=rrecv_sems.at[k - 1, a], device_id=_peer(3 - k),
                           device_id_type=MESH)
                rows = dst[a].at[p ^ k, c, lo:hi]
                relay[k] = copy(rows, rows, **via)
                mine = dst[a].at[p ^ 3, c, lo:hi]
                relayed[k] = copy(mine, mine, **via)
        return own, out, arrive, relay, relayed, onward, landed

    def stages(self, steps):
        n = len(self.shards)

        def start(src, dst, sems):
            for a in range(n):
                own, out, _, _, _, _, _ = self._copies(src, dst, sems, a)
                own().start()
                for cp in out.values():
                    cp().start()

        def pass_on(a):
            def act(src, dst, sems):
                _, _, arrive, relay, _, onward, _ = self._copies(src, dst, sems, a)
                for k in arrive:
                    arrive[k]().wait_recv()
                    if k in relay:
                        relay[k]().start()
                    onward[k]().start()
            return act

        def finish(src, dst, sems):
            for a in range(n):
                _, _, _, _, relayed, onward, _ = self._copies(src, dst, sems, a)
                for cp in relayed.values():
                    cp().wait_recv()
                onward[3]().start()
            for a in range(n):
                own, out, _, relay, _, onward, landed = self._copies(src, dst, sems, a)
                for cp in landed.values():
                    cp().wait_recv()
                for cp in list(out.values()) + list(relay.values()) + list(onward.values()):
                    cp().wait_send()
                own().wait()

        mids = [(min(steps - 1, int(self.pass_on[a] * steps)), pass_on(a)) for a in range(n)]
        return [(0, start)] + mids + [(steps - 1, finish)]

    def results(self, outs):
        return [o.reshape((N_CHIPS,) + a.shape) for o, a in zip(outs, self.shards)]


class _ExchangeCargo:
    aliased = ()

    def __init__(self, sums, whole):
        self.inputs, self.whole, n = sums, whole, len(sums)
        self.out_shape = [jax.ShapeDtypeStruct((N_CHIPS - 1,) + g.shape[1:], g.dtype) for g in sums]
        sem = pltpu.SemaphoreType.DMA((N_CHIPS - 1, n))
        self.sems = [sem, sem]

    def _copies(self, src, dst, sems):
        send_sems, recv_sems = sems
        p = _chip_index()
        return [pltpu.make_async_remote_copy(
            src_ref=src[a].at[0] if self.whole[a] else src[a].at[p ^ k], dst_ref=dst[a].at[k - 1],
            send_sem=send_sems.at[k - 1, a], recv_sem=recv_sems.at[k - 1, a], device_id=_peer(k), device_id_type=MESH)
            for k in range(1, N_CHIPS) for a in range(len(self.inputs))]

    def stages(self, steps):
        def start(src, dst, sems):
            for cp in self._copies(src, dst, sems):
                cp.start()

        def finish(src, dst, sems):
            copies = self._copies(src, dst, sems)
            for cp in copies:
                cp.wait_recv()
            for cp in copies:
                cp.wait_send()

        return [(0, start), (steps - 1, finish)]

    def results(self, outs):
        return list(outs)


class _SwapCargo:
    aliased = ()

    def __init__(self, split):
        self.inputs, n = split, len(split)
        self.out_shape = [jax.ShapeDtypeStruct((g.shape[0],) + g.shape[2:], g.dtype) for g in split]
        sem = pltpu.SemaphoreType.DMA((n,))
        self.sems = [sem, sem]

    def _copies(self, src, dst, sems):
        c = lax.axis_index("c")
        return [pltpu.make_async_remote_copy(src_ref=src[a].at[:, 1 - c], dst_ref=dst[a], send_sem=sems[0].at[a],
                                             recv_sem=sems[1].at[a], device_id=_sibling(), device_id_type=MESH)
                for a in range(len(self.inputs))]

    def stages(self, steps):
        def start(src, dst, sems):
            for cp in self._copies(src, dst, sems):
                cp.start()

        def finish(src, dst, sems):
            copies = self._copies(src, dst, sems)
            for cp in copies:
                cp.wait_recv()
            for cp in copies:
                cp.wait_send()

        return [(0, start), (steps - 1, finish)]

    def results(self, outs):
        return list(outs)


class _FillCargo:
    def __init__(self, bufs):
        self.inputs, n = bufs, len(bufs)
        self.out_shape = [jax.ShapeDtypeStruct(b.shape, b.dtype) for b in bufs]
        self.aliased = [(a, a) for a in range(n)]
        sem = pltpu.SemaphoreType.DMA((n,))
        self.sems = [sem, sem]

    def _copies(self, src, dst, sems, landing):
        c = lax.axis_index("c")
        half = 1 - c if landing else c
        return [pltpu.make_async_remote_copy(src_ref=src[a].at[c], dst_ref=dst[a].at[half], send_sem=sems[0].at[a],
                                             recv_sem=sems[1].at[a], device_id=_sibling(), device_id_type=MESH)
                for a in range(len(self.inputs))]

    def stages(self, steps):
        def start(src, dst, sems):
            for cp in self._copies(src, dst, sems, False):
                cp.start()

        def finish(src, dst, sems):
            for cp in self._copies(src, dst, sems, True):
                cp.wait_recv()
            for cp in self._copies(src, dst, sems, False):
                cp.wait_send()

        return [(0, start), (steps - 1, finish)]

    def results(self, outs):
        return [o.reshape(2 * b.shape[1], b.shape[2]) for o, b in zip(outs, self.inputs)]


class _MultiCargo:
    def __init__(self, cargos):
        self.cargos = cargos
        self.inputs = [a for cg in cargos for a in cg.inputs]
        self.out_shape = [o for cg in cargos for o in cg.out_shape]
        self.sems = [s for cg in cargos for s in cg.sems]
        self.aliased, i0, o0 = [], 0, 0
        for cg in cargos:
            self.aliased += [(i0 + i, o0 + o) for i, o in cg.aliased]
            i0, o0 = i0 + len(cg.inputs), o0 + len(cg.out_shape)

    def _parts(self, refs, count):
        out, off = [], 0
        for cg in self.cargos:
            out.append(refs[off:off + count(cg)])
            off += count(cg)
        return out

    def stages(self, steps):
        merged = []
        for n, cg in enumerate(self.cargos):
            for at, act in cg.stages(steps):
                def part(src, dst, sems, n=n, act=act):
                    act(self._parts(src, lambda g: len(g.inputs))[n], self._parts(dst, lambda g: len(g.out_shape))[n],
                        self._parts(sems, lambda g: len(g.sems))[n])
                merged.append((at, part))
        starts = [m for m in merged if m[0] == 0]
        return starts + sorted([m for m in merged if m[0] != 0], key=lambda m: m[0])

    def results(self, outs):
        return [cg.results(part) for cg, part in zip(self.cargos, self._parts(list(outs), lambda g: len(g.out_shape)))]


def _cargo_call(body, cargo, *, name, grid, in_specs, out_specs, out_shape, scratch=(), prefetch=None):
    n_in, n_out, n_scr = len(in_specs), len(out_specs), len(scratch)
    c_in, c_out = len(cargo.inputs), len(cargo.out_shape)
    steps = 1
    for g in grid:
        steps *= g
    stages = cargo.stages(steps)

    def wrapped(*refs):
        head, refs = (refs[:1], refs[1:]) if prefetch is not None else ((), refs)
        ins, refs = refs[:n_in], refs[n_in:]
        cin, refs = refs[:c_in], refs[c_in:]
        outs, refs = refs[:n_out], refs[n_out:]
        cout, refs = refs[:c_out], refs[c_out:]
        scr, sems = refs[:n_scr], refs[n_scr:]
        if not grid:
            for _, act in stages:
                act(cin, cout, sems)
            return
        step = 0
        for d in range(len(grid)):
            step = step * grid[d] + pl.program_id(d)
        for at, act in stages:
            if at == 0:
                pl.when(step == 0)(functools.partial(act, cin, cout, sems))
        body(*head, *ins, *outs, *scr)
        for at, act in stages:
            if at > 0:
                pl.when(step == at)(functools.partial(act, cin, cout, sems))

    params = dict(vmem_limit_bytes=VMEM_LIMIT)
    if grid:
        params["dimension_semantics"] = ("arbitrary",) * len(grid)
    specs = dict(grid=grid, in_specs=list(in_specs) + [ANY] * c_in, out_specs=list(out_specs) + [ANY] * c_out,
                 scratch_shapes=list(scratch) + cargo.sems)
    if prefetch is not None:
        specs = dict(grid_spec=pltpu.PrefetchScalarGridSpec(num_scalar_prefetch=1, **specs))
    head = () if prefetch is None else (prefetch,)
    res = pl.pallas_call(wrapped, name=name, out_shape=list(out_shape) + cargo.out_shape,
                         input_output_aliases={len(head) + n_in + i: n_out + o for i, o in cargo.aliased},
                         compiler_params=pltpu.CompilerParams(**params), **specs)
    return lambda *args: (lambda r: (r[:n_out], cargo.results(r[n_out:])))(res(*head, *args, *cargo.inputs))


def _split_halves(g):
    return g.reshape(g.shape[0], 2, g.shape[1] // 2, g.shape[2])


def _ew_tile(rows):
    return rows if rows <= 384 else 256


def _batched_call(items, name, place=None, cargo=None):
    offs, total = [], 0
    for it in items:
        offs.append(total)
        total += it[0]
    in_specs, out_specs, out_shape, args, spans = [], [], [], [], []
    for (n, ins, outs, shapes, operands, _), off in zip(items, offs):
        tile = lambda t, off=off, n=n: jnp.clip(t - off, 0, n - 1)
        i_s, o_s = ins(tile), outs(tile)
        spans.append((len(i_s), len(o_s)))
        in_specs += i_s
        out_specs += o_s
        out_shape += shapes
        args += operands
    n_in = len(in_specs)

    def body(*refs):
        if place is not None:
            refs = refs[1:]
        t = pl.program_id(0)
        i0, o0 = 0, n_in
        for (n, _, _, _, _, compute), off, (ni, no) in zip(items, offs, spans):
            pl.when((t >= off) & (t < off + n))(functools.partial(compute, refs[i0:i0 + ni], refs[o0:o0 + no], t - off))
            i0, o0 = i0 + ni, o0 + no

    params = pltpu.CompilerParams(dimension_semantics=("arbitrary",), vmem_limit_bytes=VMEM_LIMIT)
    carried = None
    if cargo is not None:
        res, carried = _cargo_call(body, cargo, name=name, grid=(total,), in_specs=in_specs, out_specs=out_specs,
                                   out_shape=out_shape, prefetch=place)(*args)
    elif place is None:
        res = pl.pallas_call(body, name=name, grid=(total,), in_specs=in_specs, out_specs=out_specs,
                             out_shape=out_shape, compiler_params=params)(*args)
    else:
        res = pl.pallas_call(
            body, name=name, out_shape=out_shape, compiler_params=params,
            grid_spec=pltpu.PrefetchScalarGridSpec(num_scalar_prefetch=1, grid=(total,), in_specs=in_specs,
                                                   out_specs=out_specs))(place, *args)
    out, o0 = [], 0
    for _, no in spans:
        out.append(res[o0:o0 + no])
        o0 += no
    return out if cargo is None else (out, carried)


def _add_pairs(pairs, place, name):
    def item(mine, theirs):
        lead, _, rows, cols = mine.shape
        tm = _ew_tile(rows)

        def compute(ins, outs, tile):
            outs[0][...] = (ins[0][...].astype(f32) + ins[1][...].astype(f32)).astype(outs[0].dtype)

        blk = lambda tile: pl.BlockSpec((lead, tm, cols), lambda t, pc: (0, tile(t), 0))
        mine_half = lambda tile: pl.BlockSpec((lead, None, tm, cols), lambda t, pc: (0, pc[1], tile(t), 0))
        return (rows // tm, lambda tile: [mine_half(tile), blk(tile)], lambda tile: [blk(tile)],
                [jax.ShapeDtypeStruct(theirs.shape, theirs.dtype)], [mine, theirs], compute)

    return [out[0] for out in _batched_call([item(a, b) for a, b in pairs], name, place)]


def _cargo_alone(cargo, name):
    return _cargo_call(None, cargo, name=name, grid=(), in_specs=[], out_specs=[], out_shape=[])()[1]


def _bf16_casts(x, mats, padded, cargo, name):
    def item(a, to_rows, tm):
        rows, cols = a.shape
        full = rows // tm

        def compute(ins, outs, tile):
            outs[0][...] = jnp.where(tile < full, ins[0][...], 0.0).astype(bf16)

        return (to_rows // tm, lambda tile: [pl.BlockSpec((tm, cols), lambda t: (jnp.minimum(tile(t), full - 1), 0))],
                lambda tile: [pl.BlockSpec((tm, cols), lambda t: (tile(t), 0))],
                [jax.ShapeDtypeStruct((to_rows, cols), bf16)], [a], compute)

    items = [item(x, x.shape[0], 128)] + [item(w, r, 64 if r > w.shape[0] else 128) for w, r in zip(mats, padded)]
    outs, carried = _batched_call(items, name, cargo=cargo)
    return [o[0] for o in outs], carried


def _sum_slots_items(jobs, max_rows=None):
    def item(own, recv, whole):
        _, rows, cols = own.shape
        tm = _ew_tile(rows)
        while max_rows is not None and tm > max_rows and tm % 16 == 0:
            tm //= 2

        def compute(ins, outs, tile):
            r0, r1, r2 = [ins[1][k].astype(f32) for k in range(N_CHIPS - 1)]
            outs[0][...] = (ins[0][...].astype(f32) + r0) + (r1 + r2)

        ins = lambda tile: [pl.BlockSpec((None, tm, cols), lambda t, pc: (0 if whole else pc[0], tile(t), 0)),
                            pl.BlockSpec((N_CHIPS - 1, tm, cols), lambda t, pc: (0, tile(t), 0))]
        outs = lambda tile: [pl.BlockSpec((None, tm, cols), lambda t, pc: (pc[1], tile(t), 0))]
        return rows // tm, ins, outs, [jax.ShapeDtypeStruct((2, rows, cols), f32)], [own, recv], compute

    return [item(*job) for job in jobs]


class _GradReducer:
    def __init__(self, place):
        self.place, self.split, self.sums, self.slots, self.bufs, self.full = place, {}, {}, {}, {}, {}

    def swap(self, named):
        self.split.update({nm: _split_halves(g) for nm, g in named})
        return _SwapCargo([self.split[nm] for nm, _ in named])

    def summed(self, names, got, tag):
        sums = _add_pairs([(self.split[nm], g) for nm, g in zip(names, got)], self.place, "presum_" + tag)
        self.sums.update(zip(names, sums))

    def exchange(self, names):
        return _ExchangeCargo([self.sums[nm] for nm in names], [nm == "small" for nm in names])

    def arrived(self, names, slots):
        self.slots.update(zip(names, slots))

    def tree_sum_items(self, names, max_rows=None):
        return _sum_slots_items([(self.sums[nm], self.slots[nm], nm == "small") for nm in names], max_rows)

    def tree_summed(self, names, bufs):
        self.bufs.update(zip(names, bufs))

    def fill(self, names):
        return _FillCargo([self.bufs[nm] for nm in names])

    def filled(self, names, grads):
        self.full.update(zip(names, grads))

    def reduce(self, names):
        rest = [nm for nm in names if nm not in self.bufs]
        outs = _batched_call(self.tree_sum_items(rest), "sum_slots", self.place)
        self.tree_summed(rest, [out[0] for out in outs])
        rest = [nm for nm in names if nm not in self.full]
        self.filled(rest, _cargo_alone(self.fill(rest), "sibling_fill"))
        return [self.full[nm] for nm in names]


def _adamw_update(w, m, v, g):
    m = ADAM_B1 * m + (1.0 - ADAM_B1) * g
    v = ADAM_B2 * v + (1.0 - ADAM_B2) * jnp.square(g)
    m_hat = m / (1.0 - ADAM_B1 ** ADAM_STEP)
    v_hat = v / (1.0 - ADAM_B2 ** ADAM_STEP)
    return -ADAM_LR * (m_hat / (jnp.sqrt(v_hat) + ADAM_EPS) + ADAM_WD * w), m, v


def _adamw(jobs, name):
    def item(w, m, v, grad, tm):
        rows, cols = w.shape
        gcols = grad.shape[1]

        def compute(ins, outs, tile):
            g = ins[3][0:tm, 0:cols]
            outs[0][...] = g
            outs[1][...], outs[2][...], outs[3][...] = _adamw_update(ins[0][...], ins[1][...], ins[2][...], g)

        blk = lambda tile: pl.BlockSpec((tm, cols), lambda t: (tile(t), 0))
        if tm == rows:
            gspec = lambda tile: pl.BlockSpec(grad.shape, lambda t: (0, 0))
        else:
            gspec = lambda tile: pl.BlockSpec((tm, gcols), lambda t: (tile(t), 0))
        return (rows // tm, lambda tile: [blk(tile)] * 3 + [gspec(tile)], lambda tile: [blk(tile)] * 4,
                [jax.ShapeDtypeStruct((rows, cols), f32)] * 4, [w, m, v, grad], compute)

    return _batched_call([item(*job) for job in jobs], name)


def _rope_tables(s):
    half = HEAD_DIM // 2
    inv_freq = 1.0 / (10000.0 ** (jnp.arange(half, dtype=f32) / half))
    ang = jnp.arange(s, dtype=f32)[:, None] * inv_freq[None, :]
    cos, sin = jnp.cos(ang), jnp.sin(ang)
    return jnp.tile(cos, (1, LANES // half)), jnp.tile(jnp.concatenate([-sin, sin], axis=1), (1, LANES // HEAD_DIM))


def _local_step(x, x_bf, target, win_g, gather, comm, b_gate, w_pool, pool_scale, ln1_g, ln1_b, convb_g, ln2_g, ln2_b):
    s = x.shape[0]
    nb = s // MOBA_BLOCK
    cos_t, sin_t = _rope_tables(s)

    z, (wba_g, wbp_g, wout_g) = _inproj(x_bf, win_g, cos_t, sin_t, b_gate, gather["inproj"])
    wout = wout_g.reshape(D_MODEL, D_MODEL)
    (o_bf, lse, pooled_bf, mixed, pm_bf), (wg_g, wu_g, convw_g) = _attn_fwd(z, nb, w_pool, pool_scale,
                                                                           gather["attn_fwd"])
    m_bf, x1_bf, xhat1, rstd1 = _merge_out_ln1(o_bf, pm_bf, z, wba_g, wbp_g, wout, x, ln1_g, ln1_b)
    (a, up, ac, hh_bf), (wd_g,) = _ffn_up(x1_bf, wg_g, wu_g, convw_g, convb_g, gather["ffn_up"])
    wd = wd_g.reshape(D_FF_PAD, D_MODEL)
    loss, dh2, dh2_bf, d_ln2_g, d_ln2_b = _ffn_down_ln2_loss(hh_bf, wd, xhat1, ln1_g, ln1_b, ln2_g, ln2_b, target)

    da_bf, dup_bf, dconv = _ffn_act_bwd(dh2_bf, wd_g, ac, a, up, convw_g)
    ffn_shape = (N_CHIPS, FF_PAD, D_MODEL)
    d_wd = _matmul_tn(hh_bf, dh2_bf, 1, "dw_ffn_down").reshape(ffn_shape)
    (d_wg, d_wu), got = _dw_ffn_gate_up(da_bf, dup_bf, x1_bf, comm.swap([("w_ffn_down", d_wd)]))
    comm.summed(["w_ffn_down"], got, "ffn_down")
    d_wg, d_wu = d_wg.reshape(ffn_shape), d_wu.reshape(ffn_shape)
    gate_up = ["w_ffn_gate", "w_ffn_up", "conv_w"]
    (dh1, dh1_bf, d_ln1_g, d_ln1_b), (slots, got) = _ffn_in_bwd_ln1(
        da_bf, dup_bf, wg_g, wu_g, dh2, xhat1, rstd1, ln1_g,
        _MultiCargo([comm.exchange(["w_ffn_down"]), comm.swap(list(zip(gate_up, [d_wg, d_wu, dconv])))]))
    comm.arrived(["w_ffn_down"], slots)
    comm.summed(gate_up, got, "ffn_gate_up")
    d_wout = _matmul_tn(m_bf, dh1_bf, 1, "dw_out").reshape(N_CHIPS, D_MODEL // N_CHIPS, D_MODEL)
    dzg_bf, dya_bf, dyp_bf, do_bf, dpm, d_bgate = _merge_bwd(dh1_bf, wout, o_bf, pm_bf, z, wba_g, wbp_g)
    d_wba, d_wbp = _dw_branches(o_bf, dya_bf, pm_bf, dyp_bf)
    du_bf, d_wpool, d_pscale = _pool_bwd(dpm, mixed, pooled_bf, w_pool, pool_scale)
    d_convb = dconv[:, 3, :FF_SHARD].reshape(1, N_CHIPS * FF_SHARD)
    small = _pack_small([d_bgate, d_wpool.reshape(-1, POOL_GROUP), d_pscale, d_ln1_g, d_ln1_b, d_ln2_g, d_ln2_b,
                         d_convb], loss)[None]
    branch = ["w_out", "w_branch_attn", "w_branch_pool", "small"]
    (dqkv_bf,), (slots, got) = _attn_bwd(
        z, o_bf, lse, do_bf, cos_t, sin_t, nb,
        _MultiCargo([comm.exchange(gate_up), comm.swap(list(zip(branch, [d_wout, d_wba, d_wbp, small])))]))
    comm.arrived(gate_up, slots)
    comm.summed(branch, got, "branch")
    d_win, slots = _dw_in(x_bf, dqkv_bf, du_bf, dzg_bf, comm.exchange(branch))
    comm.arrived(branch, slots)
    ffn = ["w_ffn_down"] + gate_up
    bufs, got = _batched_call(comm.tree_sum_items(ffn), "swap_w_in", comm.place, comm.swap([("w_in", d_win)]))
    comm.tree_summed(ffn, [b[0] for b in bufs])
    comm.summed(["w_in"], got, "w_in")
    (grad_x, bufs), (slots, full) = _inproj_dx(
        dqkv_bf, du_bf, dzg_bf, win_g, dh1, comm.tree_sum_items(branch, 192), comm.place,
        _MultiCargo([comm.exchange(["w_in"]), comm.fill(ffn)]))
    comm.tree_summed(branch, bufs)
    comm.filled(ffn, full)
    comm.arrived(["w_in"], slots)
    return grad_x


SMALL_WEIGHTS = [("b_gate", 2 * D_MODEL), ("w_pool", 4 * POOL_GROUP * POOL_GROUP), ("pool_scale", D_POOL),
                 ("ln1_g", D_MODEL), ("ln1_b", D_MODEL), ("ln2_g", D_MODEL), ("ln2_b", D_MODEL),
                 ("conv_b", N_CHIPS * FF_SHARD)]
LOSS_ROW = 588
SMALL_ROWS = 592


def _small_pieces(ref, first_row, n):
    if ref.shape[0] > 1:
        return [(slice(first_row, first_row + n // LANES), (slice(None), slice(None)))]
    return [(slice(first_row + i, first_row + i + 1), (slice(None), slice(i * LANES, (i + 1) * LANES)))
            for i in range(n // LANES)]


def _small_first_rows():
    rows, r = [], 0
    for _, n in SMALL_WEIGHTS:
        rows.append(r)
        r += n // LANES
    return rows


def _pack_small(parts, loss):
    def body(*refs):
        out = refs[-1]
        out[...] = jnp.zeros_like(out)
        for src, r0, (_, n) in zip(refs, _small_first_rows(), SMALL_WEIGHTS):
            for rows, where in _small_pieces(src, r0, n):
                out[rows, :] = src[where]
        out[LOSS_ROW:LOSS_ROW + 1, :] = refs[len(parts)][0:1, :]

    return pl.pallas_call(body, name="pack_small", out_shape=jax.ShapeDtypeStruct((SMALL_ROWS, LANES), f32))(
        *parts, loss)


def _adamw_small(g_packed, g_conv, triples):
    n_in = 3 * len(triples)

    def body(g_ref, gc_ref, *refs):
        def update(k, g, where):
            w_ref, m_ref, v_ref = refs[3 * k:3 * k + 3]
            new = (g,) + _adamw_update(w_ref[where], m_ref[where], v_ref[where], g)
            for o_ref, val in zip(refs[n_in + 4 * k:n_in + 4 * k + 4], new):
                o_ref[where] = val

        for k, (r0, (_, n)) in enumerate(zip(_small_first_rows(), SMALL_WEIGHTS)):
            for rows, where in _small_pieces(refs[3 * k], r0, n):
                update(k, g_ref[rows, :], where)
        for tap in range(3):
            update(len(SMALL_WEIGHTS), gc_ref[tap:tap + 1, 0:FF_SHARD], (tap,))

    res = pl.pallas_call(
        body, name="adamw_small",
        out_shape=[jax.ShapeDtypeStruct(t[0].shape, f32) for t in triples for _ in range(4)],
    )(g_packed, g_conv, *[a for t in triples for a in t])
    return [res[4 * k:4 * k + 4] for k in range(len(triples))]


def _pad_conv_b(cb):
    return jnp.pad(cb.reshape(N_CHIPS, FF_SHARD), ((0, 0), (0, FF_PAD - FF_SHARD)))


def kernel(x, w_in, b_gate, w_branch_attn, w_pool, pool_scale, w_branch_pool, w_out, ln1_g, ln1_b, w_ffn_gate, w_ffn_up, conv_w, conv_b, w_ffn_down, ln2_g, ln2_b, loss_target, m_w_in, m_b_gate, m_w_branch_attn, m_w_pool, m_pool_scale, m_w_branch_pool, m_w_out, m_ln1_g, m_ln1_b, m_w_ffn_gate, m_w_ffn_up, m_conv_w, m_conv_b, m_w_ffn_down, m_ln2_g, m_ln2_b, v_w_in, v_b_gate, v_w_branch_attn, v_w_pool, v_pool_scale, v_w_branch_pool, v_w_out, v_ln1_g, v_ln1_b, v_w_ffn_gate, v_w_ffn_up, v_conv_w, v_conv_b, v_w_ffn_down, v_ln2_g, v_ln2_b):
    mats = [w_branch_attn[0], w_branch_pool[0], w_out[0], w_ffn_gate[0].T, w_ffn_up[0].T, w_ffn_down[0]]
    padded = [w.shape[0] for w in mats[:3]] + [FF_PAD] * 3
    (x_bf, *casted), (win_g,) = _bf16_casts(x[0], mats, padded, _GatherCargo([w_in[0].astype(bf16)], [0.6]),
                                            "gather_w_in")
    shards = [None] + casted + [jnp.pad(conv_w[0], ((0, CONV_ROWS - 3), (0, FF_PAD - FF_SHARD)))]
    gather = {"inproj": _GatherCargo(shards[1:4], [0.45, 0.5, 0.55]),
              "attn_fwd": _GatherCargo(shards[4:6] + shards[7:], [0.65, 0.65, 0.65]),
              "ffn_up": _GatherCargo(shards[6:7], [0.5])}

    place = jnp.stack([2 * lax.axis_index("x") + lax.axis_index("y"), lax.axis_index("c")]).astype(jnp.int32)
    comm = _GradReducer(place)
    convb_g = _pad_conv_b(conv_b).reshape(N_CHIPS, 1, FF_PAD)
    grad_x = _local_step(
        x[0], x_bf, loss_target[0], win_g, gather, comm, b_gate, w_pool[0], pool_scale, ln1_g, ln1_b, convb_g, ln2_g, ln2_b)
    names = ["w_in", "w_branch_attn", "w_branch_pool", "w_out", "w_ffn_gate", "w_ffn_up", "w_ffn_down", "conv_w", "small"]
    grads = comm.reduce(names)

    weights = [w_in[0], w_branch_attn[0], w_branch_pool[0], w_out[0], w_ffn_gate[0].T, w_ffn_up[0].T, w_ffn_down[0]]
    m_in = [m_w_in[0], m_w_branch_attn[0], m_w_branch_pool[0], m_w_out[0], m_w_ffn_gate[0].T, m_w_ffn_up[0].T,
            m_w_ffn_down[0]]
    v_in = [v_w_in[0], v_w_branch_attn[0], v_w_branch_pool[0], v_w_out[0], v_w_ffn_gate[0].T, v_w_ffn_up[0].T,
            v_w_ffn_down[0]]
    tiles = [256, 256, 256, 128, 176, 176, 176]
    jobs = list(zip(weights, m_in, v_in, grads, tiles))
    updated = _adamw(jobs[:4], "adamw_rest") + _adamw(jobs[4:], "adamw_ffn")
    res = {nm: [(r.T if nm in ("w_ffn_gate", "w_ffn_up") else r)[None] for r in outs4]
           for nm, outs4 in zip(names, updated)}

    small_w = [b_gate, w_pool, pool_scale, ln1_g, ln1_b, ln2_g, ln2_b, conv_b, conv_w]
    small_m = [m_b_gate, m_w_pool, m_pool_scale, m_ln1_g, m_ln1_b, m_ln2_g, m_ln2_b, m_conv_b, m_conv_w]
    small_v = [v_b_gate, v_w_pool, v_pool_scale, v_ln1_g, v_ln1_b, v_ln2_g, v_ln2_b, v_conv_b, v_conv_w]

    def as_kernel_sees(a):
        return a.reshape(-1, POOL_GROUP) if a.ndim == 4 else a.transpose(1, 0, 2) if a.ndim == 3 else a

    small_out = _adamw_small(grads[-1], grads[-2],
                             [tuple(as_kernel_sees(a) for a in t) for t in zip(small_w, small_m, small_v)])
    for nm, w, outs4 in zip([nm for nm, _ in SMALL_WEIGHTS] + ["conv_w"], small_w, small_out):
        res[nm] = [o.transpose(1, 0, 2) if w.ndim == 3 else o.reshape(w.shape) for o in outs4]
    loss = grads[-1][LOSS_ROW, 0]

    order = ["w_in", "b_gate", "w_branch_attn", "w_pool", "pool_scale", "w_branch_pool", "w_out", "ln1_g", "ln1_b",
             "w_ffn_gate", "w_ffn_up", "conv_w", "conv_b", "w_ffn_down", "ln2_g", "ln2_b"]
    outs = [loss, grad_x[None]]
    for kind in range(4):
        outs += [res[nm][kind] for nm in order]
    return tuple(outs)
```
